```python
import jax, jax.numpy as jnp
from jax import lax
import numpy as np

D_MODEL = 1024
BATCH = 8
SEQ = 2048
DEPTH = 4

GRID_W = 64
N_MEM = 256
HEAD_DIM = 64
N_BRANCH = 4
BRANCH_WIDTH = D_MODEL // N_BRANCH
IN_WIDTH = 9 * BRANCH_WIDTH
RET_HEADS = BRANCH_WIDTH // HEAD_DIM
RET_CHUNK = 128
ROPE_THETA = 10000.0
POOL_WINDOWS = (2, 4, 8, 16)
POOL_GROUPS = len(POOL_WINDOWS)
POOL_GROUP_DIM = BRANCH_WIDTH // POOL_GROUPS
NA_HEADS = BRANCH_WIDTH // HEAD_DIM
NA_WIN_ROWS = 8
NA_WIN_COLS = 16
NA_QBLOCK_COLS = 16
NA_KBLOCK_COLS = 2 * NA_QBLOCK_COLS
MEM_HEADS = BRANCH_WIDTH // HEAD_DIM
FF_HIDDEN = -(-8 * D_MODEL // (3 * 256)) * 256
NEG_INF = -1e30
EPS = 1e-6

kernel_name = "hybrid_retention_pool_natten_memory_encoder"


def rms_norm(x, g):
    xf = x.astype(jnp.float32)
    y = xf * lax.rsqrt(jnp.mean(xf * xf, axis=-1, keepdims=True) + EPS)
    return (y * g.astype(jnp.float32)).astype(x.dtype)


def split_heads(t, n_heads):
    b, s, _ = t.shape
    return t.reshape(b, s, n_heads, -1).transpose(0, 2, 1, 3)


def merge_heads(t):
    b, h, s, d = t.shape
    return t.transpose(0, 2, 1, 3).reshape(b, s, h * d)


def rotary(t, pos):
    half = t.shape[-1] // 2
    inv = ROPE_THETA ** (-jnp.arange(half, dtype=jnp.float32) / half)
    ang = pos[:, None] * inv[None, :]
    cos, sin = jnp.cos(ang), jnp.sin(ang)
    tf = t.astype(jnp.float32)
    t1, t2 = tf[..., :half], tf[..., half:]
    return jnp.concatenate([t1 * cos - t2 * sin, t1 * sin + t2 * cos], axis=-1).astype(t.dtype)


def retention_dir(q, k, v, log_gamma, include_diag):
    b, h, s, d = q.shape
    c = RET_CHUNK
    n = s // c
    dt = q.dtype
    qc, kc, vc = (t.reshape(b, h, n, c, d) for t in (q, k, v))
    idx = jnp.arange(c, dtype=jnp.float32)
    diff = idx[:, None] - idx[None, :]
    mask = (diff >= 0) if include_diag else (diff > 0)
    lg = log_gamma.astype(jnp.float32)[:, None]
    d_intra = jnp.where(mask[None], jnp.exp(jnp.where(mask, diff, 0.0)[None] * lg[:, :, None]), 0.0)
    scores = jnp.einsum('bhncd,bhnmd->bhncm', qc, kc) * d_intra[None, :, None].astype(dt)
    intra = jnp.einsum('bhncm,bhnme->bhnce', scores, vc)
    k_decay = jnp.exp((c - 1 - idx)[None, :] * lg).astype(dt)
    kv = jnp.einsum('bhncd,bhnce->nbhde', kc * k_decay[None, :, None, :, None], vc)
    chunk_decay = jnp.exp(c * lg[:, 0]).astype(dt)[None, :, None, None]

    def step(state, kv_n):
        return chunk_decay * state + kv_n, state

    _, states = lax.scan(step, jnp.zeros_like(kv[0]), kv)
    q_decay = jnp.exp((idx + 1)[None, :] * lg).astype(dt)
    cross = jnp.einsum('bhncd,nbhde->bhnce', qc * q_decay[None, :, None, :, None], states)
    return (intra + cross).reshape(b, h, s, d)


def bidirectional_retention(q, k, v, log_gamma_fwd, log_gamma_bwd):
    fwd = retention_dir(q, k, v, log_gamma_fwd, True)
    flip = lambda t: jnp.flip(t, axis=2)
    bwd = flip(retention_dir(flip(q), flip(k), flip(v), log_gamma_bwd, False))
    return fwd + bwd


def multiscale_pool(v, w_group, scale):
    b, s, cw = v.shape
    vg = v.reshape(b, s, POOL_GROUPS, POOL_GROUP_DIM)
    cs = jnp.cumsum(vg.astype(jnp.float32), axis=1)
    cs = jnp.concatenate([jnp.zeros_like(cs[:, :1]), cs], axis=1)
    t = np.arange(s)[:, None]
    half = np.array(POOL_WINDOWS)[None, :] // 2
    lo = np.clip(t - half, 0, s)
    hi = np.clip(t + half, 0, s)
    g_idx = np.arange(POOL_GROUPS)[None, :]
    win_sum = cs[:, hi, g_idx] - cs[:, lo, g_idx]
    count = jnp.asarray((hi - lo)[None, :, :, None], dtype=jnp.float32)
    pooled = (win_sum / count).astype(v.dtype) - vg
    mixed = jnp.einsum('bsgc,gce->bsge', pooled, w_group)
    return mixed.reshape(b, s, cw) * scale


def neighbourhood_attention(q, k, v, rpb):
    b, h, s, d = q.shape
    rows = s // GRID_W
    wr = min(NA_WIN_ROWS, rows)
    n_cb = GRID_W // NA_QBLOCK_COLS
    r = np.arange(rows)
    row_idx = np.clip(r - wr // 2, 0, rows - wr)[:, None] + np.arange(wr)[None, :]
    cb = np.arange(n_cb)
    kcol_idx = np.clip(cb * NA_QBLOCK_COLS - NA_WIN_COLS // 2, 0, GRID_W - NA_KBLOCK_COLS)[:, None] \
        + np.arange(NA_KBLOCK_COLS)[None, :]
    qcol = cb[:, None] * NA_QBLOCK_COLS + np.arange(NA_QBLOCK_COLS)[None, :]
    qwin = np.clip(qcol - NA_WIN_COLS // 2, 0, GRID_W - NA_WIN_COLS)
    col_mask = (kcol_idx[:, None, :] >= qwin[:, :, None]) & (kcol_idx[:, None, :] < qwin[:, :, None] + NA_WIN_COLS)
    row_off = row_idx - r[:, None]
    col_off = np.clip(kcol_idx[:, None, :] - qcol[:, :, None], -(NA_WIN_COLS - 1), NA_WIN_COLS - 1)
    bias = rpb[:, row_off[:, None, None, :, None] + NA_WIN_ROWS - 1,
               col_off[None, :, :, None, :] + NA_WIN_COLS - 1]
    bias = jnp.where(col_mask[None, None, :, :, None, :], bias.astype(jnp.float32), NEG_INF)

    qg = q.reshape(b, h, rows, n_cb, NA_QBLOCK_COLS, d)
    k_grid = k.reshape(b, h, rows, GRID_W, d)
    v_grid = v.reshape(b, h, rows, GRID_W, d)
    ri = row_idx[:, None, :, None]
    ci = kcol_idx[None, :, None, :]
    kg = k_grid[:, :, ri, ci]
    vg = v_grid[:, :, ri, ci]
    sc = jnp.einsum('bhrnqd,bhrnwkd->bhrnqwk', qg, kg).astype(jnp.float32) * (d ** -0.5) + bias[None]
    p = jax.nn.softmax(sc, axis=(-2, -1))
    o = jnp.einsum('bhrnqwk,bhrnwkd->bhrnqd', p.astype(v.dtype), vg)
    return o.reshape(b, h, s, d)


def memory_attention(q, mk, mv):
    sc = jnp.einsum('bhsd,bhmd->bhsm', q, mk).astype(jnp.float32) * (q.shape[-1] ** -0.5)
    p = jax.nn.softmax(sc, axis=-1)
    return jnp.einsum('bhsm,bhmd->bhsd', p.astype(mv.dtype), mv)


def hybrid_layer(x, mem, norm_mix_g, norm_mem_g, w_in, w_gate, ret_decay_fwd, ret_decay_bwd,
                 ret_norm_g, pool_w, pool_scale, na_q_norm_g, na_k_norm_g, na_rpb,
                 mem_q_norm_g, mem_k_norm_g, w_mem_kv, w_branch, w_out, norm_ffn_g,
                 w_ffn_in, w_ffn_out):
    b, s, dm = x.shape
    h = rms_norm(x, norm_mix_g)
    proj = h @ w_in
    rq, rk, rv, rg, pv, nq, nk, nv, mq = jnp.split(proj, 9, axis=-1)

    pos = jnp.arange(s, dtype=jnp.float32)
    rq_h = rotary(split_heads(rq, RET_HEADS), pos) * (HEAD_DIM ** -0.5)
    rk_h = rotary(split_heads(rk, RET_HEADS), pos)
    ret = bidirectional_retention(rq_h, rk_h, split_heads(rv, RET_HEADS),
                                  jax.nn.log_sigmoid(ret_decay_fwd.astype(jnp.float32)),
                                  jax.nn.log_sigmoid(ret_decay_bwd.astype(jnp.float32)))
    ret = merge_heads(rms_norm(ret, ret_norm_g.reshape(RET_HEADS, 1, HEAD_DIM))) * jax.nn.silu(rg)

    pool = multiscale_pool(pv, pool_w, pool_scale)

    na = merge_heads(neighbourhood_attention(rms_norm(split_heads(nq, NA_HEADS), na_q_norm_g),
                                             rms_norm(split_heads(nk, NA_HEADS), na_k_norm_g),
                                             split_heads(nv, NA_HEADS), na_rpb))

    mk, mv = jnp.split(rms_norm(mem, norm_mem_g) @ w_mem_kv, 2, axis=-1)
    mo = merge_heads(memory_attention(rms_norm(split_heads(mq, MEM_HEADS), mem_q_norm_g),
                                      rms_norm(split_heads(mk, MEM_HEADS), mem_k_norm_g),
                                      split_heads(mv, MEM_HEADS)))

    branches = jnp.stack([ret, pool, na, mo], axis=2)
    up = jnp.einsum('bsnc,ncd->bsnd', branches, w_branch)
    gates = jax.nn.sigmoid(h @ w_gate).reshape(b, s, N_BRANCH, dm)
    merged = jnp.einsum('bsnd,bsnd->bsd', gates, up)
    x = x + merged @ w_out

    a, g = jnp.split(rms_norm(x, norm_ffn_g) @ w_ffn_in, 2, axis=-1)
    return x + (jax.nn.silu(a) * g) @ w_ffn_out


def _fwd_setup_inputs(seed: int = 0) -> dict:
    key = jax.random.key(seed)
    ks = jax.random.split(key, 22)
    f32 = jnp.float32
    L, D, BW = DEPTH, D_MODEL, BRANCH_WIDTH

    def nrm(k, shape, scale):
        return jax.random.normal(k, shape, f32) * scale

    base_logit = jnp.log(2.0 ** (5.0 + jnp.arange(RET_HEADS, dtype=f32)) - 1.0)
    return {
        "x": nrm(ks[0], (BATCH, SEQ, D), 1.0),
        "mem": nrm(ks[1], (BATCH, N_MEM, D), 1.0),
        "norm_mix_g": 1.0 + nrm(ks[2], (L, D), 0.02),
        "norm_mem_g": 1.0 + nrm(ks[3], (L, D), 0.02),
        "w_in": nrm(ks[4], (L, D, IN_WIDTH), D ** -0.5),
        "w_gate": nrm(ks[5], (L, D, N_BRANCH * D), D ** -0.5),
        "ret_decay_fwd": base_logit[None, :] + nrm(ks[6], (L, RET_HEADS), 0.1),
        "ret_decay_bwd": base_logit[None, :] + nrm(ks[7], (L, RET_HEADS), 0.1),
        "ret_norm_g": 1.0 + nrm(ks[8], (L, BW), 0.02),
        "pool_w": nrm(ks[9], (L, POOL_GROUPS, POOL_GROUP_DIM, POOL_GROUP_DIM), POOL_GROUP_DIM ** -0.5),
        "pool_scale": 1.0 + nrm(ks[10], (L, BW), 0.02),
        "na_q_norm_g": 1.0 + nrm(ks[11], (L, HEAD_DIM), 0.02),
        "na_k_norm_g": 1.0 + nrm(ks[12], (L, HEAD_DIM), 0.02),
        "na_rpb": nrm(ks[13], (L, NA_HEADS, 2 * NA_WIN_ROWS - 1, 2 * NA_WIN_COLS - 1), 0.02),
        "mem_q_norm_g": 1.0 + nrm(ks[14], (L, HEAD_DIM), 0.02),
        "mem_k_norm_g": 1.0 + nrm(ks[15], (L, HEAD_DIM), 0.02),
        "w_mem_kv": nrm(ks[16], (L, D, 2 * BW), D ** -0.5),
        "w_branch": nrm(ks[17], (L, N_BRANCH, BW, D), BW ** -0.5),
        "w_out": nrm(ks[18], (L, D, D), D ** -0.5),
        "norm_ffn_g": 1.0 + nrm(ks[19], (L, D), 0.02),
        "w_ffn_in": nrm(ks[20], (L, D, 2 * FF_HIDDEN), D ** -0.5),
        "w_ffn_out": nrm(ks[21], (L, FF_HIDDEN, D), FF_HIDDEN ** -0.5),
    }


def _fwd_reference(x, mem, norm_mix_g, norm_mem_g, w_in, w_gate, ret_decay_fwd, ret_decay_bwd,
              ret_norm_g, pool_w, pool_scale, na_q_norm_g, na_k_norm_g, na_rpb,
              mem_q_norm_g, mem_k_norm_g, w_mem_kv, w_branch, w_out, norm_ffn_g,
              w_ffn_in, w_ffn_out):
    for l in range(DEPTH):
        x = hybrid_layer(x, mem, norm_mix_g[l], norm_mem_g[l], w_in[l], w_gate[l],
                         ret_decay_fwd[l], ret_decay_bwd[l], ret_norm_g[l], pool_w[l],
                         pool_scale[l], na_q_norm_g[l], na_k_norm_g[l], na_rpb[l],
                         mem_q_norm_g[l], mem_k_norm_g[l], w_mem_kv[l], w_branch[l],
                         w_out[l], norm_ffn_g[l], w_ffn_in[l], w_ffn_out[l])
    return x


import jax as _jax
import jax.numpy as _jnp

TWIN_FORMAT = 'train_step'
FWD_PARAMS = ['x', 'mem', 'norm_mix_g', 'norm_mem_g', 'w_in', 'w_gate', 'ret_decay_fwd', 'ret_decay_bwd', 'ret_norm_g', 'pool_w', 'pool_scale', 'na_q_norm_g', 'na_k_norm_g', 'na_rpb', 'mem_q_norm_g', 'mem_k_norm_g', 'w_mem_kv', 'w_branch', 'w_out', 'norm_ffn_g', 'w_ffn_in', 'w_ffn_out']
TWIN_WEIGHTS = ['norm_mix_g', 'norm_mem_g', 'w_in', 'w_gate', 'ret_decay_fwd', 'ret_decay_bwd', 'ret_norm_g', 'pool_w', 'pool_scale', 'na_q_norm_g', 'na_k_norm_g', 'na_rpb', 'mem_q_norm_g', 'mem_k_norm_g', 'w_mem_kv', 'w_branch', 'w_out', 'norm_ffn_g', 'w_ffn_in', 'w_ffn_out']
TWIN_DIFF_INPUT = 'x'
TWIN_INPUTS = ['x', 'mem', 'norm_mix_g', 'norm_mem_g', 'w_in', 'w_gate', 'ret_decay_fwd', 'ret_decay_bwd', 'ret_norm_g', 'pool_w', 'pool_scale', 'na_q_norm_g', 'na_k_norm_g', 'na_rpb', 'mem_q_norm_g', 'mem_k_norm_g', 'w_mem_kv', 'w_branch', 'w_out', 'norm_ffn_g', 'w_ffn_in', 'w_ffn_out', 'loss_target', 'm_norm_mix_g', 'm_norm_mem_g', 'm_w_in', 'm_w_gate', 'm_ret_decay_fwd', 'm_ret_decay_bwd', 'm_ret_norm_g', 'm_pool_w', 'm_pool_scale', 'm_na_q_norm_g', 'm_na_k_norm_g', 'm_na_rpb', 'm_mem_q_norm_g', 'm_mem_k_norm_g', 'm_w_mem_kv', 'm_w_branch', 'm_w_out', 'm_norm_ffn_g', 'm_w_ffn_in', 'm_w_ffn_out', 'v_norm_mix_g', 'v_norm_mem_g', 'v_w_in', 'v_w_gate', 'v_ret_decay_fwd', 'v_ret_decay_bwd', 'v_ret_norm_g', 'v_pool_w', 'v_pool_scale', 'v_na_q_norm_g', 'v_na_k_norm_g', 'v_na_rpb', 'v_mem_q_norm_g', 'v_mem_k_norm_g', 'v_w_mem_kv', 'v_w_branch', 'v_w_out', 'v_norm_ffn_g', 'v_w_ffn_in', 'v_w_ffn_out']
TWIN_OUTPUTS = ['loss', 'grad_x', 'grad_norm_mix_g', 'grad_norm_mem_g', 'grad_w_in', 'grad_w_gate', 'grad_ret_decay_fwd', 'grad_ret_decay_bwd', 'grad_ret_norm_g', 'grad_pool_w', 'grad_pool_scale', 'grad_na_q_norm_g', 'grad_na_k_norm_g', 'grad_na_rpb', 'grad_mem_q_norm_g', 'grad_mem_k_norm_g', 'grad_w_mem_kv', 'grad_w_branch', 'grad_w_out', 'grad_norm_ffn_g', 'grad_w_ffn_in', 'grad_w_ffn_out', 'delta_norm_mix_g', 'delta_norm_mem_g', 'delta_w_in', 'delta_w_gate', 'delta_ret_decay_fwd', 'delta_ret_decay_bwd', 'delta_ret_norm_g', 'delta_pool_w', 'delta_pool_scale', 'delta_na_q_norm_g', 'delta_na_k_norm_g', 'delta_na_rpb', 'delta_mem_q_norm_g', 'delta_mem_k_norm_g', 'delta_w_mem_kv', 'delta_w_branch', 'delta_w_out', 'delta_norm_ffn_g', 'delta_w_ffn_in', 'delta_w_ffn_out', 'new_m_norm_mix_g', 'new_m_norm_mem_g', 'new_m_w_in', 'new_m_w_gate', 'new_m_ret_decay_fwd', 'new_m_ret_decay_bwd', 'new_m_ret_norm_g', 'new_m_pool_w', 'new_m_pool_scale', 'new_m_na_q_norm_g', 'new_m_na_k_norm_g', 'new_m_na_rpb', 'new_m_mem_q_norm_g', 'new_m_mem_k_norm_g', 'new_m_w_mem_kv', 'new_m_w_branch', 'new_m_w_out', 'new_m_norm_ffn_g', 'new_m_w_ffn_in', 'new_m_w_ffn_out', 'new_v_norm_mix_g', 'new_v_norm_mem_g', 'new_v_w_in', 'new_v_w_gate', 'new_v_ret_decay_fwd', 'new_v_ret_decay_bwd', 'new_v_ret_norm_g', 'new_v_pool_w', 'new_v_pool_scale', 'new_v_na_q_norm_g', 'new_v_na_k_norm_g', 'new_v_na_rpb', 'new_v_mem_q_norm_g', 'new_v_mem_k_norm_g', 'new_v_w_mem_kv', 'new_v_w_branch', 'new_v_w_out', 'new_v_norm_ffn_g', 'new_v_w_ffn_in', 'new_v_w_ffn_out']
TWIN_LEAF_KINDS = {'loss': 'loss', 'grad_x': 'grad_x', 'grad_norm_mix_g': 'grad_w', 'grad_norm_mem_g': 'grad_w', 'grad_w_in': 'grad_w', 'grad_w_gate': 'grad_w', 'grad_ret_decay_fwd': 'grad_w', 'grad_ret_decay_bwd': 'grad_w', 'grad_ret_norm_g': 'grad_w', 'grad_pool_w': 'grad_w', 'grad_pool_scale': 'grad_w', 'grad_na_q_norm_g': 'grad_w', 'grad_na_k_norm_g': 'grad_w', 'grad_na_rpb': 'grad_w', 'grad_mem_q_norm_g': 'grad_w', 'grad_mem_k_norm_g': 'grad_w', 'grad_w_mem_kv': 'grad_w', 'grad_w_branch': 'grad_w', 'grad_w_out': 'grad_w', 'grad_norm_ffn_g': 'grad_w', 'grad_w_ffn_in': 'grad_w', 'grad_w_ffn_out': 'grad_w', 'delta_norm_mix_g': 'delta_w', 'delta_norm_mem_g': 'delta_w', 'delta_w_in': 'delta_w', 'delta_w_gate': 'delta_w', 'delta_ret_decay_fwd': 'delta_w', 'delta_ret_decay_bwd': 'delta_w', 'delta_ret_norm_g': 'delta_w', 'delta_pool_w': 'delta_w', 'delta_pool_scale': 'delta_w', 'delta_na_q_norm_g': 'delta_w', 'delta_na_k_norm_g': 'delta_w', 'delta_na_rpb': 'delta_w', 'delta_mem_q_norm_g': 'delta_w', 'delta_mem_k_norm_g': 'delta_w', 'delta_w_mem_kv': 'delta_w', 'delta_w_branch': 'delta_w', 'delta_w_out': 'delta_w', 'delta_norm_ffn_g': 'delta_w', 'delta_w_ffn_in': 'delta_w', 'delta_w_ffn_out': 'delta_w', 'new_m_norm_mix_g': 'new_m', 'new_m_norm_mem_g': 'new_m', 'new_m_w_in': 'new_m', 'new_m_w_gate': 'new_m', 'new_m_ret_decay_fwd': 'new_m', 'new_m_ret_decay_bwd': 'new_m', 'new_m_ret_norm_g': 'new_m', 'new_m_pool_w': 'new_m', 'new_m_pool_scale': 'new_m', 'new_m_na_q_norm_g': 'new_m', 'new_m_na_k_norm_g': 'new_m', 'new_m_na_rpb': 'new_m', 'new_m_mem_q_norm_g': 'new_m', 'new_m_mem_k_norm_g': 'new_m', 'new_m_w_mem_kv': 'new_m', 'new_m_w_branch': 'new_m', 'new_m_w_out': 'new_m', 'new_m_norm_ffn_g': 'new_m', 'new_m_w_ffn_in': 'new_m', 'new_m_w_ffn_out': 'new_m', 'new_v_norm_mix_g': 'new_v', 'new_v_norm_mem_g': 'new_v', 'new_v_w_in': 'new_v', 'new_v_w_gate': 'new_v', 'new_v_ret_decay_fwd': 'new_v', 'new_v_ret_decay_bwd': 'new_v', 'new_v_ret_norm_g': 'new_v', 'new_v_pool_w': 'new_v', 'new_v_pool_scale': 'new_v', 'new_v_na_q_norm_g': 'new_v', 'new_v_na_k_norm_g': 'new_v', 'new_v_na_rpb': 'new_v', 'new_v_mem_q_norm_g': 'new_v', 'new_v_mem_k_norm_g': 'new_v', 'new_v_w_mem_kv': 'new_v', 'new_v_w_branch': 'new_v', 'new_v_w_out': 'new_v', 'new_v_norm_ffn_g': 'new_v', 'new_v_w_ffn_in': 'new_v', 'new_v_w_ffn_out': 'new_v'}


def _forward(args):
    return _fwd_reference(*[args[k] for k in FWD_PARAMS])


def _output_shape():
    out = _jax.eval_shape(lambda: _forward(_fwd_setup_inputs(0)))
    return out.shape, out.dtype

N_MICROBATCH = 1
ADAM_LR = 0.001
ADAM_B1 = 0.9
ADAM_B2 = 0.999
ADAM_EPS = 1e-08
ADAM_WD = 0.01
ADAM_STEP = 10
PER_EXAMPLE_BATCH_AXIS = {'x': 0, 'mem': 0, 'loss_target': 0}
SHARED_INPUTS = []
_WEIGHT_DTYPES = {'norm_mix_g': _jnp.float32, 'norm_mem_g': _jnp.float32, 'w_in': _jnp.float32, 'w_gate': _jnp.float32, 'ret_decay_fwd': _jnp.float32, 'ret_decay_bwd': _jnp.float32, 'ret_norm_g': _jnp.float32, 'pool_w': _jnp.float32, 'pool_scale': _jnp.float32, 'na_q_norm_g': _jnp.float32, 'na_k_norm_g': _jnp.float32, 'na_rpb': _jnp.float32, 'mem_q_norm_g': _jnp.float32, 'mem_k_norm_g': _jnp.float32, 'w_mem_kv': _jnp.float32, 'w_branch': _jnp.float32, 'w_out': _jnp.float32, 'norm_ffn_g': _jnp.float32, 'w_ffn_in': _jnp.float32, 'w_ffn_out': _jnp.float32}
MOMENT_SCALE = {'norm_mix_g': 6.161818e+00, 'norm_mem_g': 8.580154e-02, 'w_in': 4.417592e-01, 'w_gate': 6.774429e-02, 'ret_decay_fwd': 1.530547e+00, 'ret_decay_bwd': 1.821360e+00, 'ret_norm_g': 6.562512e+00, 'pool_w': 2.036200e+00, 'pool_scale': 1.552856e+01, 'na_q_norm_g': 7.145341e-01, 'na_k_norm_g': 7.118145e-01, 'na_rpb': 4.493712e-02, 'mem_q_norm_g': 6.112246e-01, 'mem_k_norm_g': 6.113193e-01, 'w_mem_kv': 1.028207e-01, 'w_branch': 3.087362e-01, 'w_out': 5.681008e-01, 'norm_ffn_g': 1.231323e+01, 'w_ffn_in': 2.020605e-01, 'w_ffn_out': 3.391446e-01}


def _to_microbatches(a, axis):
    t = _jnp.moveaxis(a, axis, 0)
    t = t.reshape((N_MICROBATCH, t.shape[0] // N_MICROBATCH) + t.shape[1:])
    return _jnp.moveaxis(t, 1, axis + 1)


def setup_inputs(seed: int = 0) -> dict:
    inp = _fwd_setup_inputs(seed)
    key = _jax.random.fold_in(_jax.random.key(seed), 7919)
    shape, _ = _output_shape()
    out = dict(inp)
    out["loss_target"] = _jax.random.normal(_jax.random.fold_in(key, 0), shape, _jnp.float32)
    for i, name in enumerate(TWIN_WEIGHTS):
        w = inp[name].astype(_jnp.float32)
        if MOMENT_SCALE is None:
            s = _jnp.sqrt(_jnp.mean(_jnp.square(w)) + 1e-30)
        else:
            s = MOMENT_SCALE[name]
        km, kv = _jax.random.split(_jax.random.fold_in(key, i + 1))
        out[name] = w
        out["m_" + name] = s * _jax.random.normal(km, w.shape, _jnp.float32)
        out["v_" + name] = (s * s) * _jax.random.uniform(kv, w.shape, _jnp.float32, 0.5, 1.5)
    if N_MICROBATCH > 1:
        for name, axis in PER_EXAMPLE_BATCH_AXIS.items():
            out[name] = _to_microbatches(out[name], axis)
    return {'x': out['x'], 'mem': out['mem'], 'norm_mix_g': out['norm_mix_g'], 'norm_mem_g': out['norm_mem_g'], 'w_in': out['w_in'], 'w_gate': out['w_gate'], 'ret_decay_fwd': out['ret_decay_fwd'], 'ret_decay_bwd': out['ret_decay_bwd'], 'ret_norm_g': out['ret_norm_g'], 'pool_w': out['pool_w'], 'pool_scale': out['pool_scale'], 'na_q_norm_g': out['na_q_norm_g'], 'na_k_norm_g': out['na_k_norm_g'], 'na_rpb': out['na_rpb'], 'mem_q_norm_g': out['mem_q_norm_g'], 'mem_k_norm_g': out['mem_k_norm_g'], 'w_mem_kv': out['w_mem_kv'], 'w_branch': out['w_branch'], 'w_out': out['w_out'], 'norm_ffn_g': out['norm_ffn_g'], 'w_ffn_in': out['w_ffn_in'], 'w_ffn_out': out['w_ffn_out'], 'loss_target': out['loss_target'], 'm_norm_mix_g': out['m_norm_mix_g'], 'm_norm_mem_g': out['m_norm_mem_g'], 'm_w_in': out['m_w_in'], 'm_w_gate': out['m_w_gate'], 'm_ret_decay_fwd': out['m_ret_decay_fwd'], 'm_ret_decay_bwd': out['m_ret_decay_bwd'], 'm_ret_norm_g': out['m_ret_norm_g'], 'm_pool_w': out['m_pool_w'], 'm_pool_scale': out['m_pool_scale'], 'm_na_q_norm_g': out['m_na_q_norm_g'], 'm_na_k_norm_g': out['m_na_k_norm_g'], 'm_na_rpb': out['m_na_rpb'], 'm_mem_q_norm_g': out['m_mem_q_norm_g'], 'm_mem_k_norm_g': out['m_mem_k_norm_g'], 'm_w_mem_kv': out['m_w_mem_kv'], 'm_w_branch': out['m_w_branch'], 'm_w_out': out['m_w_out'], 'm_norm_ffn_g': out['m_norm_ffn_g'], 'm_w_ffn_in': out['m_w_ffn_in'], 'm_w_ffn_out': out['m_w_ffn_out'], 'v_norm_mix_g': out['v_norm_mix_g'], 'v_norm_mem_g': out['v_norm_mem_g'], 'v_w_in': out['v_w_in'], 'v_w_gate': out['v_w_gate'], 'v_ret_decay_fwd': out['v_ret_decay_fwd'], 'v_ret_decay_bwd': out['v_ret_decay_bwd'], 'v_ret_norm_g': out['v_ret_norm_g'], 'v_pool_w': out['v_pool_w'], 'v_pool_scale': out['v_pool_scale'], 'v_na_q_norm_g': out['v_na_q_norm_g'], 'v_na_k_norm_g': out['v_na_k_norm_g'], 'v_na_rpb': out['v_na_rpb'], 'v_mem_q_norm_g': out['v_mem_q_norm_g'], 'v_mem_k_norm_g': out['v_mem_k_norm_g'], 'v_w_mem_kv': out['v_w_mem_kv'], 'v_w_branch': out['v_w_branch'], 'v_w_out': out['v_w_out'], 'v_norm_ffn_g': out['v_norm_ffn_g'], 'v_w_ffn_in': out['v_w_ffn_in'], 'v_w_ffn_out': out['v_w_ffn_out']}


def _loss(weights, diff, rest, loss_target):
    with _jax.named_scope("forward"):
        args = {**rest, TWIN_DIFF_INPUT: diff, **{k: w.astype(_WEIGHT_DTYPES[k]) for k, w in weights.items()}}
        y = _forward(args)
    with _jax.named_scope("loss_head"):
        err = _jnp.square(y.astype(_jnp.float32) - loss_target)
        return 0.5 * _jnp.sum(_jnp.mean(err, axis=-1)) if err.ndim else 0.5 * err


def _adamw(w, g, m, v):
    m = ADAM_B1 * m + (1.0 - ADAM_B1) * g
    v = ADAM_B2 * v + (1.0 - ADAM_B2) * _jnp.square(g)
    m_hat = m / (1.0 - ADAM_B1 ** ADAM_STEP)
    v_hat = v / (1.0 - ADAM_B2 ** ADAM_STEP)
    delta = -ADAM_LR * (m_hat / (_jnp.sqrt(v_hat) + ADAM_EPS) + ADAM_WD * w)
    return delta, m, v


def reference(x, mem, norm_mix_g, norm_mem_g, w_in, w_gate, ret_decay_fwd, ret_decay_bwd, ret_norm_g, pool_w, pool_scale, na_q_norm_g, na_k_norm_g, na_rpb, mem_q_norm_g, mem_k_norm_g, w_mem_kv, w_branch, w_out, norm_ffn_g, w_ffn_in, w_ffn_out, loss_target, m_norm_mix_g, m_norm_mem_g, m_w_in, m_w_gate, m_ret_decay_fwd, m_ret_decay_bwd, m_ret_norm_g, m_pool_w, m_pool_scale, m_na_q_norm_g, m_na_k_norm_g, m_na_rpb, m_mem_q_norm_g, m_mem_k_norm_g, m_w_mem_kv, m_w_branch, m_w_out, m_norm_ffn_g, m_w_ffn_in, m_w_ffn_out, v_norm_mix_g, v_norm_mem_g, v_w_in, v_w_gate, v_ret_decay_fwd, v_ret_decay_bwd, v_ret_norm_g, v_pool_w, v_pool_scale, v_na_q_norm_g, v_na_k_norm_g, v_na_rpb, v_mem_q_norm_g, v_mem_k_norm_g, v_w_mem_kv, v_w_branch, v_w_out, v_norm_ffn_g, v_w_ffn_in, v_w_ffn_out):
    given = dict(x=x, mem=mem, norm_mix_g=norm_mix_g, norm_mem_g=norm_mem_g, w_in=w_in, w_gate=w_gate, ret_decay_fwd=ret_decay_fwd, ret_decay_bwd=ret_decay_bwd, ret_norm_g=ret_norm_g, pool_w=pool_w, pool_scale=pool_scale, na_q_norm_g=na_q_norm_g, na_k_norm_g=na_k_norm_g, na_rpb=na_rpb, mem_q_norm_g=mem_q_norm_g, mem_k_norm_g=mem_k_norm_g, w_mem_kv=w_mem_kv, w_branch=w_branch, w_out=w_out, norm_ffn_g=norm_ffn_g, w_ffn_in=w_ffn_in, w_ffn_out=w_ffn_out, loss_target=loss_target, m_norm_mix_g=m_norm_mix_g, m_norm_mem_g=m_norm_mem_g, m_w_in=m_w_in, m_w_gate=m_w_gate, m_ret_decay_fwd=m_ret_decay_fwd, m_ret_decay_bwd=m_ret_decay_bwd, m_ret_norm_g=m_ret_norm_g, m_pool_w=m_pool_w, m_pool_scale=m_pool_scale, m_na_q_norm_g=m_na_q_norm_g, m_na_k_norm_g=m_na_k_norm_g, m_na_rpb=m_na_rpb, m_mem_q_norm_g=m_mem_q_norm_g, m_mem_k_norm_g=m_mem_k_norm_g, m_w_mem_kv=m_w_mem_kv, m_w_branch=m_w_branch, m_w_out=m_w_out, m_norm_ffn_g=m_norm_ffn_g, m_w_ffn_in=m_w_ffn_in, m_w_ffn_out=m_w_ffn_out, v_norm_mix_g=v_norm_mix_g, v_norm_mem_g=v_norm_mem_g, v_w_in=v_w_in, v_w_gate=v_w_gate, v_ret_decay_fwd=v_ret_decay_fwd, v_ret_decay_bwd=v_ret_decay_bwd, v_ret_norm_g=v_ret_norm_g, v_pool_w=v_pool_w, v_pool_scale=v_pool_scale, v_na_q_norm_g=v_na_q_norm_g, v_na_k_norm_g=v_na_k_norm_g, v_na_rpb=v_na_rpb, v_mem_q_norm_g=v_mem_q_norm_g, v_mem_k_norm_g=v_mem_k_norm_g, v_w_mem_kv=v_w_mem_kv, v_w_branch=v_w_branch, v_w_out=v_w_out, v_norm_ffn_g=v_norm_ffn_g, v_w_ffn_in=v_w_ffn_in, v_w_ffn_out=v_w_ffn_out)
    weights = {n: given[n] for n in TWIN_WEIGHTS}
    shared = {n: given[n] for n in SHARED_INPUTS}
    per_example = {n: given[n] for n in ['x', 'mem']}
    grad_fn = _jax.value_and_grad(_loss, argnums=(0, 1))

    def one_microbatch(ex, loss_target):
        ex = dict(ex)
        diff = ex.pop(TWIN_DIFF_INPUT)
        return grad_fn(weights, diff, {**shared, **ex}, loss_target)

    if N_MICROBATCH == 1:
        loss, (grad_w, grad_x) = one_microbatch(per_example, given["loss_target"])
    else:
        def body(carry, xs):
            loss_sum, grad_sum = carry
            l_k, (gw_k, gx_k) = one_microbatch(xs[0], xs[1])
            with _jax.named_scope("update"):
                return (loss_sum + l_k, _jax.tree.map(_jnp.add, grad_sum, gw_k)), gx_k

        init = (_jnp.zeros((), _jnp.float32), _jax.tree.map(_jnp.zeros_like, weights))
        (loss, grad_w), grad_x = _jax.lax.scan(body, init, (per_example, given["loss_target"]))
    with _jax.named_scope("update"):
        delta_w, new_m, new_v = {}, {}, {}
        for n in TWIN_WEIGHTS:
            delta_w[n], new_m[n], new_v[n] = _adamw(weights[n], grad_w[n], given["m_" + n], given["v_" + n])
    return (loss, grad_x, *[grad_w[n] for n in TWIN_WEIGHTS], *[delta_w[n] for n in TWIN_WEIGHTS],
            *[new_m[n] for n in TWIN_WEIGHTS], *[new_v[n] for n in TWIN_WEIGHTS])
```

```python
import functools

import numpy as np
import jax
import jax.numpy as jnp
from jax import lax
from jax.experimental import pallas as pl
from jax.experimental.pallas import tpu as pltpu

F32 = jnp.float32
BF16 = jnp.bfloat16
MXU = jnp.bfloat16
HI = lax.Precision.HIGHEST

DEPTH = 4
D = 1024
BW = 256
HD = 64
NH = 4
GRID_W = 64
NA_ROWS_WIN = 8
NA_COLS_WIN = 16
N_MEM = 256
FF = 2816
EPS = 1e-6
NEG = -1e30
ROPE_THETA = 10000.0
POOL_HALF_MAX = 8

ADAM_LR, ADAM_B1, ADAM_B2, ADAM_EPS, ADAM_WD, ADAM_STEP = 0.001, 0.9, 0.999, 1e-08, 0.01, 10

N_DEV = 8
VMEM_LIMIT = 56 * 1024 * 1024

RQ, RK, RV, RG, PV, NQ, NK, NV, MQ = range(9)

MESH = pl.DeviceIdType.MESH
ANY = pl.BlockSpec(memory_space=pl.ANY)
SMEM = pl.BlockSpec(memory_space=pltpu.SMEM)


def _cp(**kw):
    return pltpu.CompilerParams(vmem_limit_bytes=VMEM_LIMIT, **kw)


def _tile(n, cap):
    if n <= cap:
        return n
    best = None
    for t in range(128, cap + 1, 128):
        if n % t == 0:
            best = t
    assert best is not None, (n, cap)
    return best


def _sds(shape, dtype):
    return jax.ShapeDtypeStruct(shape, dtype)


def _lane_head(shape):
    return lax.shift_right_logical(lax.broadcasted_iota(jnp.int32, shape, len(shape) - 1), 6)


def _group_mean(z):
    i = lax.shift_right_logical(lax.broadcasted_iota(jnp.int32, (BW, BW), 0), 6)
    j = lax.shift_right_logical(lax.broadcasted_iota(jnp.int32, (BW, BW), 1), 6)
    g = jnp.where(i == j, 1.0 / HD, 0.0).astype(F32)
    return jnp.dot(z, g, precision=HI, preferred_element_type=F32)


def _gnorm(t, g):
    r = lax.rsqrt(_group_mean(t * t) + EPS)
    return t * r * g


def _gnorm_bwd(dy, t, g):
    r = lax.rsqrt(_group_mean(t * t) + EPS)
    th = t * r
    dth = dy * g
    dt = r * (dth - th * _group_mean(dth * th))
    return dt, dy * th


def _swap_halves(t):
    lane = lax.broadcasted_iota(jnp.int32, t.shape, 1)
    return jnp.where((lane & 63) < 32, pltpu.roll(t, BW - 32, 1), pltpu.roll(t, 32, 1))


def _sigmoid(x):
    return 1.0 / (1.0 + jnp.exp(-x))


def _dot(a, b, ta=False, tb=False):
    return lax.dot_general(a.astype(MXU), b.astype(MXU), (((0 if ta else 1,), (1 if tb else 0,)), ((), ())),
                           preferred_element_type=F32)


def _stack_heads(t):
    head = _lane_head(t.shape)
    return jnp.concatenate([jnp.where(head == h, t, jnp.zeros_like(t)) for h in range(NH)], axis=0)


def _unstack_heads(t, rows):
    head = _lane_head((rows, BW))
    out = jnp.zeros((rows, BW), F32)
    for h in range(NH):
        out = out + jnp.where(head == h, t[h * rows:(h + 1) * rows], 0.0)
    return out


def _softmax_rows(s):
    m = jnp.max(s, axis=-1, keepdims=True)
    e = jnp.exp(s - m)
    return e / jnp.sum(e, axis=-1, keepdims=True)


def _acc(ref, val, first):
    @pl.when(first)
    def _():
        ref[...] = val

    @pl.when(jnp.logical_not(first))
    def _():
        ref[...] += val


def _mm(a, b, *, ta=False, tb=False, out_dtype=F32, add=None, name):
    m, k = (a.shape[1], a.shape[0]) if ta else a.shape
    n = b.shape[0] if tb else b.shape[1]
    tm, tn, tk = _tile(m, 1024), _tile(n, 512), _tile(k, 1024)
    nk = k // tk

    def body(*refs):
        if add is None:
            a_ref, b_ref, o_ref, acc_ref = refs
        else:
            a_ref, b_ref, c_ref, o_ref, acc_ref = refs
        kk = pl.program_id(2)
        _acc(acc_ref, _dot(a_ref[...], b_ref[...], ta, tb), kk == 0)

        @pl.when(kk == nk - 1)
        def _():
            r = acc_ref[...]
            if add is not None:
                r = r + c_ref[...]
            o_ref[...] = r.astype(out_dtype)

    a_spec = pl.BlockSpec((tk, tm), lambda i, j, kk: (kk, i)) if ta else pl.BlockSpec((tm, tk), lambda i, j, kk: (i, kk))
    b_spec = pl.BlockSpec((tn, tk), lambda i, j, kk: (j, kk)) if tb else pl.BlockSpec((tk, tn), lambda i, j, kk: (kk, j))
    o_spec = pl.BlockSpec((tm, tn), lambda i, j, kk: (i, j))
    ins, args = [a_spec, b_spec], [a, b]
    if add is not None:
        ins.append(o_spec)
        args.append(add)
    return pl.pallas_call(
        body, grid=(m // tm, n // tn, nk), in_specs=ins, out_specs=o_spec, out_shape=_sds((m, n), out_dtype),
        scratch_shapes=[pltpu.VMEM((tm, tn), F32)], name=name,
        compiler_params=_cp(dimension_semantics=("parallel", "parallel", "arbitrary")))(*args)


def _rmsnorm_fwd(x, g, name):
    t, d = x.shape
    tm = _tile(t, 256)

    def body(x_ref, g_ref, o_ref):
        xv = x_ref[...]
        r = lax.rsqrt(jnp.mean(xv * xv, axis=-1, keepdims=True) + EPS)
        o_ref[...] = (xv * r * g_ref[...]).astype(o_ref.dtype)

    return pl.pallas_call(
        body, grid=(t // tm,), in_specs=[pl.BlockSpec((tm, d), lambda i: (i, 0)), pl.BlockSpec((1, d), lambda i: (0, 0))],
        out_specs=pl.BlockSpec((tm, d), lambda i: (i, 0)), out_shape=_sds((t, d), BF16), name=name, compiler_params=_cp())(x, g)


def _rmsnorm_bwd(dh, x, g, res, name):
    t, d = x.shape
    tm = _tile(t, 256)

    def body(dh_ref, x_ref, g_ref, res_ref, dx_ref, dg_ref):
        xv = x_ref[...]
        dhv = dh_ref[...]
        r = lax.rsqrt(jnp.mean(xv * xv, axis=-1, keepdims=True) + EPS)
        xh = xv * r
        dxh = dhv * g_ref[...]
        dx_ref[...] = res_ref[...] + r * (dxh - xh * jnp.mean(dxh * xh, axis=-1, keepdims=True))
        _acc(dg_ref, jnp.sum(dhv * xh, axis=0, keepdims=True), pl.program_id(0) == 0)

    row = pl.BlockSpec((tm, d), lambda i: (i, 0))
    vec = pl.BlockSpec((1, d), lambda i: (0, 0))
    return pl.pallas_call(
        body, grid=(t // tm,), in_specs=[row, row, vec, row], out_specs=(row, vec),
        out_shape=(_sds((t, d), F32), _sds((1, d), F32)), name=name, compiler_params=_cp())(dh, x, g, res)


def _prep_fwd(proj, cos2, sin2, g_naq, g_nak, g_mq):
    t = proj.shape[0]
    tm = 256

    def body(p_ref, cos_ref, sin_ref, gq_ref, gk_ref, gm_ref, rq_ref, rk_ref, rv_ref, nq_ref, nk_ref, nv_ref, mq_ref):
        def col(c):
            return p_ref[:, c * BW:(c + 1) * BW]

        cosv, sinv = cos_ref[...], sin_ref[...]

        def rot(tv):
            return tv * cosv + _swap_halves(tv) * sinv

        rq_ref[...] = (rot(col(RQ)) * (HD ** -0.5)).astype(BF16)
        rk_ref[...] = rot(col(RK)).astype(BF16)
        rv_ref[...] = col(RV).astype(BF16)
        nq_ref[...] = _gnorm(col(NQ), gq_ref[...]).astype(BF16)
        nk_ref[...] = _gnorm(col(NK), gk_ref[...]).astype(BF16)
        nv_ref[...] = col(NV).astype(BF16)
        mq_ref[...] = _gnorm(col(MQ), gm_ref[...]).astype(BF16)

    blk = pl.BlockSpec((tm, BW), lambda i: (i, 0))
    vec = pl.BlockSpec((1, BW), lambda i: (0, 0))
    return pl.pallas_call(
        body, grid=(t // tm,), in_specs=[pl.BlockSpec((tm, 9 * BW), lambda i: (i, 0)), blk, blk, vec, vec, vec],
        out_specs=tuple(blk for _ in range(7)), out_shape=tuple(_sds((t, BW), BF16) for _ in range(7)),
        name="prep_fwd", compiler_params=_cp())(proj, cos2, sin2, g_naq, g_nak, g_mq)


def _prep_bwd(proj, cos2, sin2, g_naq, g_nak, g_mq, d_rq, d_rk, d_rv, d_rg, d_pv, d_nq, d_nk, d_nv, d_mq):
    t = proj.shape[0]
    tm = 256

    def body(p_ref, cos_ref, sin_ref, gq_ref, gk_ref, gm_ref, drq_ref, drk_ref, drv_ref, drg_ref, dpv_ref, dnq_ref, dnk_ref,
             dnv_ref, dmq_ref, o_ref, dgq_ref, dgk_ref, dgm_ref):
        first = pl.program_id(0) == 0

        def col(c):
            return p_ref[:, c * BW:(c + 1) * BW]

        def put(c, v):
            o_ref[:, c * BW:(c + 1) * BW] = v.astype(BF16)

        cosv, sinv = cos_ref[...], sin_ref[...]

        def rot_t(dv):
            return dv * cosv + _swap_halves(dv * sinv)

        put(RQ, rot_t(drq_ref[...] * (HD ** -0.5)))
        put(RK, rot_t(drk_ref[...]))
        put(RV, drv_ref[...])
        put(RG, drg_ref[...])
        put(PV, dpv_ref[...])
        dq, gq = _gnorm_bwd(dnq_ref[...], col(NQ), gq_ref[...])
        put(NQ, dq)
        _acc(dgq_ref, jnp.sum(gq, axis=0, keepdims=True), first)
        dk, gk = _gnorm_bwd(dnk_ref[...], col(NK), gk_ref[...])
        put(NK, dk)
        _acc(dgk_ref, jnp.sum(gk, axis=0, keepdims=True), first)
        put(NV, dnv_ref[...])
        dm, gm = _gnorm_bwd(dmq_ref[...], col(MQ), gm_ref[...])
        put(MQ, dm)
        _acc(dgm_ref, jnp.sum(gm, axis=0, keepdims=True), first)

    blk = pl.BlockSpec((tm, BW), lambda i: (i, 0))
    vec = pl.BlockSpec((1, BW), lambda i: (0, 0))
    wide = pl.BlockSpec((tm, 9 * BW), lambda i: (i, 0))
    return pl.pallas_call(
        body, grid=(t // tm,), in_specs=[wide, blk, blk, vec, vec, vec] + [blk] * 9, out_specs=(wide, vec, vec, vec),
        out_shape=(_sds((t, 9 * BW), BF16), _sds((1, BW), F32), _sds((1, BW), F32), _sds((1, BW), F32)),
        name="prep_bwd", compiler_params=_cp())(proj, cos2, sin2, g_naq, g_nak, g_mq, d_rq, d_rk, d_rv, d_rg, d_pv, d_nq, d_nk,
                                                d_nv, d_mq)


RET_TQ = 64


def _ret_decay(i, tq, t, lgf_ref, lgb_ref):
    rows = NH * tq
    n = i * tq + (lax.broadcasted_iota(jnp.int32, (rows, 1), 0) & (tq - 1))
    m = lax.broadcasted_iota(jnp.int32, (1, t), 1)
    diff = n - m
    causal = diff >= 0
    dist = jnp.abs(diff).astype(F32)
    lgf = jnp.concatenate([jnp.full((tq, 1), lgf_ref[h], F32) for h in range(NH)], axis=0)
    lgb = jnp.concatenate([jnp.full((tq, 1), lgb_ref[h], F32) for h in range(NH)], axis=0)
    return causal, dist, jnp.exp(dist * jnp.where(causal, lgf, lgb))


def _ret_fwd(q, k, v, proj, lgf, lgb, g_ret):
    t = q.shape[0]
    tq = RET_TQ

    def body(lgf_ref, lgb_ref, q_ref, k_ref, v_ref, rg_ref, g_ref, o_ref, ret_ref):
        i = pl.program_id(0)
        qs = _stack_heads(q_ref[...])
        s = _dot(qs, k_ref[...], tb=True)
        _, _, dm = _ret_decay(i, tq, t, lgf_ref, lgb_ref)
        o = _unstack_heads(_dot(s * dm, v_ref[...]), tq)
        o_ref[...] = o
        rg = rg_ref[...]
        ret_ref[...] = (_gnorm(o, g_ref[...]) * (rg * _sigmoid(rg))).astype(BF16)

    blk = pl.BlockSpec((tq, BW), lambda i: (i, 0))
    whole = pl.BlockSpec((t, BW), lambda i: (0, 0))
    return pl.pallas_call(
        body, grid=(t // tq,),
        in_specs=[SMEM, SMEM, blk, whole, whole, pl.BlockSpec((tq, BW), lambda i: (i, RG)), pl.BlockSpec((1, BW), lambda i: (0, 0))],
        out_specs=(blk, blk), out_shape=(_sds((t, BW), F32), _sds((t, BW), BF16)), name="ret_fwd",
        compiler_params=_cp())(lgf, lgb, q, k, v, proj, g_ret)


def _ret_bwd(dbr, o_ret, q, k, v, proj, lgf, lgb, g_ret):
    t = q.shape[0]
    tq = RET_TQ
    nblk = t // tq

    def body(lgf_ref, lgb_ref, d_ref, o_ref, q_ref, k_ref, v_ref, rg_ref, g_ref,
             dq_ref, dk_ref, dv_ref, drg_ref, dg_ref, dlg_ref, accf_ref, accb_ref):
        i = pl.program_id(0)
        first = i == 0
        dret, o, rg, g = d_ref[...], o_ref[...], rg_ref[...], g_ref[...]
        sg = _sigmoid(rg)
        dy = dret * (rg * sg)
        do, dgain = _gnorm_bwd(dy, o, g)
        drg_ref[...] = dret * _gnorm(o, g) * (sg * (1.0 + rg * (1.0 - sg)))
        _acc(dg_ref, jnp.sum(dgain, axis=0, keepdims=True), first)

        dos = _stack_heads(do).astype(MXU)
        qs = _stack_heads(q_ref[...])
        kv, vv = k_ref[...], v_ref[...]
        s = _dot(qs, kv, tb=True)
        causal, dist, dm = _ret_decay(i, tq, t, lgf_ref, lgb_ref)
        da = _dot(dos, vv, tb=True)
        _acc(dv_ref, _dot(s * dm, dos, ta=True), first)
        ds = da * dm
        w = ds * s * dist
        _acc(accf_ref, jnp.sum(jnp.where(causal, w, 0.0), axis=1, keepdims=True), first)
        _acc(accb_ref, jnp.sum(jnp.where(causal, 0.0, w), axis=1, keepdims=True), first)
        dsb = ds.astype(MXU)
        dq_ref[...] = _unstack_heads(_dot(dsb, kv), tq)
        _acc(dk_ref, _dot(dsb, qs, ta=True), first)

        @pl.when(i == nblk - 1)
        def _():
            for h in range(NH):
                dlg_ref[h:h + 1, :] = jnp.full((1, 128), jnp.sum(accf_ref[h * tq:(h + 1) * tq, :]), F32)
                dlg_ref[NH + h:NH + h + 1, :] = jnp.full((1, 128), jnp.sum(accb_ref[h * tq:(h + 1) * tq, :]), F32)

    blk = pl.BlockSpec((tq, BW), lambda i: (i, 0))
    whole = pl.BlockSpec((t, BW), lambda i: (0, 0))
    vec = pl.BlockSpec((1, BW), lambda i: (0, 0))
    return pl.pallas_call(
        body, grid=(nblk,),
        in_specs=[SMEM, SMEM, blk, blk, blk, whole, whole, pl.BlockSpec((tq, BW), lambda i: (i, RG)), vec],
        out_specs=(blk, whole, whole, blk, vec, pl.BlockSpec((2 * NH, 128), lambda i: (0, 0))),
        out_shape=(_sds((t, BW), F32), _sds((t, BW), F32), _sds((t, BW), F32), _sds((t, BW), F32), _sds((1, BW), F32),
                   _sds((2 * NH, 128), F32)),
        scratch_shapes=[pltpu.VMEM((NH * tq, 1), F32), pltpu.VMEM((NH * tq, 1), F32)], name="ret_bwd",
        compiler_params=_cp())(lgf, lgb, dbr, o_ret, q, k, v, proj, g_ret)


def _pool_windows(t):
    row = lax.broadcasted_iota(jnp.int32, (t, BW), 0)
    half = lax.shift_left(jnp.ones((t, BW), jnp.int32), _lane_head((t, BW)))
    cnt = (jnp.minimum(row + half, t) - jnp.maximum(row - half, 0)).astype(F32)
    return row, half, cnt


def _pool_window_sum(v, row, half, t, transpose):
    out = jnp.zeros_like(v)
    for j in range(-POOL_HALF_MAX, POOL_HALF_MAX):
        src = row - j if transpose else row + j
        ok = (src >= 0) & (src < t) & (j >= -half) & (j < half)
        out = out + jnp.where(ok, pltpu.roll(v, (j if transpose else -j) % t, 0), 0.0)
    return out


def _pool_fwd(proj, wbd, scale):
    t = proj.shape[0]

    def body(v_ref, w_ref, s_ref, o_ref):
        v = v_ref[...]
        row, half, cnt = _pool_windows(t)
        pooled = _pool_window_sum(v, row, half, t, False) / cnt - v
        o_ref[...] = (_dot(pooled, w_ref[...]) * s_ref[...]).astype(BF16)

    return pl.pallas_call(
        body, grid=(1,),
        in_specs=[pl.BlockSpec((t, BW), lambda i: (0, PV)), pl.BlockSpec((BW, BW), lambda i: (0, 0)), pl.BlockSpec((1, BW), lambda i: (0, 0))],
        out_specs=pl.BlockSpec((t, BW), lambda i: (0, 0)), out_shape=_sds((t, BW), BF16), name="pool_fwd",
        compiler_params=_cp())(proj, wbd, scale)


def _pool_bwd(dbr, proj, wbd, scale):
    t = proj.shape[0]

    def body(d_ref, v_ref, w_ref, s_ref, dv_ref, dw_ref, ds_ref):
        v, dout = v_ref[...], d_ref[...]
        row, half, cnt = _pool_windows(t)
        pooled = _pool_window_sum(v, row, half, t, False) / cnt - v
        mixed = _dot(pooled, w_ref[...])
        ds_ref[...] = jnp.sum(dout * mixed, axis=0, keepdims=True)
        dmixed = dout * s_ref[...]
        dw_ref[...] = _dot(pooled, dmixed, ta=True)
        dpooled = _dot(dmixed, w_ref[...], tb=True)
        dv_ref[...] = _pool_window_sum(dpooled / cnt, row, half, t, True) - dpooled

    return pl.pallas_call(
        body, grid=(1,),
        in_specs=[pl.BlockSpec((t, BW), lambda i: (0, 1)), pl.BlockSpec((t, BW), lambda i: (0, PV)),
                  pl.BlockSpec((BW, BW), lambda i: (0, 0)), pl.BlockSpec((1, BW), lambda i: (0, 0))],
        out_specs=(pl.BlockSpec((t, BW), lambda i: (0, 0)), pl.BlockSpec((BW, BW), lambda i: (0, 0)), pl.BlockSpec((1, BW), lambda i: (0, 0))),
        out_shape=(_sds((t, BW), F32), _sds((BW, BW), F32), _sds((1, BW), F32)), name="pool_bwd",
        compiler_params=_cp())(dbr, proj, wbd, scale)


NA_KEYS = NA_ROWS_WIN * GRID_W


def _na_window(r, n_rows):
    rs = jnp.clip(r - NA_ROWS_WIN // 2, 0, n_rows - NA_ROWS_WIN)
    return pl.multiple_of(rs * GRID_W, GRID_W), rs - r + (NA_ROWS_WIN - 1)


def _na_fwd(q, k, v, ball):
    t = q.shape[0]
    n_rows = t // GRID_W

    def body(q_ref, k_ref, v_ref, b_ref, o_ref):
        start, a0 = _na_window(pl.program_id(0), n_rows)
        qs = _stack_heads(q_ref[...])
        s = _dot(qs, k_ref[pl.ds(start, NA_KEYS), :], tb=True) * (HD ** -0.5) + b_ref[a0]
        p = _softmax_rows(s)
        o_ref[...] = _unstack_heads(_dot(p, v_ref[pl.ds(start, NA_KEYS), :]), GRID_W).astype(BF16)

    blk = pl.BlockSpec((GRID_W, BW), lambda r: (r, 0))
    whole = pl.BlockSpec((t, BW), lambda r: (0, 0))
    return pl.pallas_call(
        body, grid=(n_rows,), in_specs=[blk, whole, whole, pl.BlockSpec(ball.shape, lambda r: (0, 0, 0))],
        out_specs=blk, out_shape=_sds((t, BW), BF16), name="na_fwd", compiler_params=_cp())(q, k, v, ball)


def _na_bwd(dbr, q, k, v, ball):
    t = q.shape[0]
    n_rows = t // GRID_W

    def body(d_ref, q_ref, k_ref, v_ref, b_ref, dq_ref, dk_ref, dv_ref, db_ref):
        r = pl.program_id(0)
        start, a0 = _na_window(r, n_rows)
        keys = pl.ds(start, NA_KEYS)

        @pl.when(r == 0)
        def _():
            dk_ref[...] = jnp.zeros_like(dk_ref)
            dv_ref[...] = jnp.zeros_like(dv_ref)
            db_ref[...] = jnp.zeros_like(db_ref)

        qs = _stack_heads(q_ref[...])
        kb, vb = k_ref[keys, :], v_ref[keys, :]
        p = _softmax_rows(_dot(qs, kb, tb=True) * (HD ** -0.5) + b_ref[a0])
        dos = _stack_heads(d_ref[...]).astype(MXU)
        dp = _dot(dos, vb, tb=True)
        dv_ref[keys, :] += _dot(p, dos, ta=True)
        ds = p * (dp - jnp.sum(dp * p, axis=-1, keepdims=True))
        db_ref[a0] += ds
        dsb = (ds * (HD ** -0.5)).astype(MXU)
        dq_ref[...] = _unstack_heads(_dot(dsb, kb), GRID_W)
        dk_ref[keys, :] += _dot(dsb, qs, ta=True)

    blk = pl.BlockSpec((GRID_W, BW), lambda r: (r, 0))
    whole = pl.BlockSpec((t, BW), lambda r: (0, 0))
    tab = pl.BlockSpec(ball.shape, lambda r: (0, 0, 0))
    return pl.pallas_call(
        body, grid=(n_rows,), in_specs=[pl.BlockSpec((GRID_W, BW), lambda r: (r, 2)), blk, whole, whole, tab],
        out_specs=(blk, whole, whole, tab),
        out_shape=(_sds((t, BW), F32), _sds((t, BW), F32), _sds((t, BW), F32), _sds(ball.shape, F32)), name="na_bwd",
        compiler_params=_cp())(dbr, q, k, v, ball)


def _rpb_expand(rpb_pad, onehot):
    def body(r_ref, e_ref, o_ref):
        o_ref[...] = jnp.dot(r_ref[...], e_ref[...], precision=HI, preferred_element_type=F32)

    return pl.pallas_call(body, out_shape=_sds((64, GRID_W * GRID_W), F32), name="rpb_expand", compiler_params=_cp())(rpb_pad, onehot)


def _rpb_reduce(dtab, onehot):
    def body(d_ref, e_ref, o_ref):
        o_ref[...] = lax.dot_general(d_ref[...], e_ref[...], (((1,), (1,)), ((), ())), precision=HI, preferred_element_type=F32)

    return pl.pallas_call(body, out_shape=_sds((64, 128), F32), name="rpb_reduce", compiler_params=_cp())(dtab, onehot)


MEM_TQ = 256


def _mem_fwd(q, mk, mv):
    t = q.shape[0]
    tq = MEM_TQ

    def body(q_ref, k_ref, v_ref, o_ref):
        qv = q_ref[...]
        head = _lane_head(qv.shape)
        out = jnp.zeros((tq, BW), F32)
        for h in range(NH):
            p = _softmax_rows(_dot(jnp.where(head == h, qv, jnp.zeros_like(qv)), k_ref[...], tb=True) * (HD ** -0.5))
            out = out + jnp.where(head == h, _dot(p, v_ref[...]), 0.0)
        o_ref[...] = out.astype(BF16)

    blk = pl.BlockSpec((tq, BW), lambda i: (i, 0))
    kv = pl.BlockSpec((N_MEM, BW), lambda i: (0, 0))
    return pl.pallas_call(body, grid=(t // tq,), in_specs=[blk, kv, kv], out_specs=blk, out_shape=_sds((t, BW), BF16),
                          name="mem_fwd", compiler_params=_cp())(q, mk, mv)


def _mem_bwd(dbr, q, mk, mv):
    t = q.shape[0]
    tq = MEM_TQ

    def body(d_ref, q_ref, k_ref, v_ref, dq_ref, dk_ref, dv_ref):
        first = pl.program_id(0) == 0
        qv, dout = q_ref[...], d_ref[...]
        head = _lane_head(qv.shape)
        dq = jnp.zeros((tq, BW), F32)
        dk = jnp.zeros((N_MEM, BW), F32)
        dv = jnp.zeros((N_MEM, BW), F32)
        for h in range(NH):
            qh = jnp.where(head == h, qv, jnp.zeros_like(qv))
            doh = jnp.where(head == h, dout, 0.0).astype(MXU)
            p = _softmax_rows(_dot(qh, k_ref[...], tb=True) * (HD ** -0.5))
            dp = _dot(doh, v_ref[...], tb=True)
            dv = dv + _dot(p, doh, ta=True)
            dsb = (p * (dp - jnp.sum(dp * p, axis=-1, keepdims=True)) * (HD ** -0.5)).astype(MXU)
            dq = dq + jnp.where(head == h, _dot(dsb, k_ref[...]), 0.0)
            dk = dk + _dot(dsb, qh, ta=True)
        dq_ref[...] = dq
        _acc(dk_ref, dk, first)
        _acc(dv_ref, dv, first)

    blk = pl.BlockSpec((tq, BW), lambda i: (i, 0))
    kv = pl.BlockSpec((N_MEM, BW), lambda i: (0, 0))
    return pl.pallas_call(
        body, grid=(t // tq,), in_specs=[pl.BlockSpec((tq, BW), lambda i: (i, 3)), blk, kv, kv], out_specs=(blk, kv, kv),
        out_shape=(_sds((t, BW), F32), _sds((N_MEM, BW), F32), _sds((N_MEM, BW), F32)), name="mem_bwd",
        compiler_params=_cp())(dbr, q, mk, mv)


def _memkv_prep(kv, g_mk):
    def body(kv_ref, g_ref, k_ref, v_ref):
        k_ref[...] = _gnorm(kv_ref[:, 0:BW], g_ref[...]).astype(BF16)
        v_ref[...] = kv_ref[:, BW:2 * BW].astype(BF16)

    return pl.pallas_call(body, out_shape=(_sds((N_MEM, BW), BF16), _sds((N_MEM, BW), BF16)), name="memkv_prep",
                          compiler_params=_cp())(kv, g_mk)


def _memkv_bwd(kv, dk, dv, g_mk):
    def body(kv_ref, dk_ref, dv_ref, g_ref, o_ref, dg_ref):
        dkk, gain = _gnorm_bwd(dk_ref[...], kv_ref[:, 0:BW], g_ref[...])
        o_ref[:, 0:BW] = dkk.astype(BF16)
        o_ref[:, BW:2 * BW] = dv_ref[...].astype(BF16)
        dg_ref[...] = jnp.sum(gain, axis=0, keepdims=True)

    return pl.pallas_call(body, out_shape=(_sds((N_MEM, 2 * BW), BF16), _sds((1, BW), F32)), name="memkv_bwd",
                          compiler_params=_cp())(kv, dk, dv, g_mk)


MERGE_TM = 256


def _merge_fwd(brs, wbt, gp):
    t = gp.shape[0]
    tm = MERGE_TM

    def body(b0, b1, b2, b3, wb_ref, gp_ref, o_ref):
        out = jnp.zeros((tm, D), F32)
        for n, b_ref in enumerate((b0, b1, b2, b3)):
            up = _dot(b_ref[...], wb_ref[n], tb=True)
            out = out + _sigmoid(gp_ref[:, n * D:(n + 1) * D]) * up
        o_ref[...] = out.astype(BF16)

    blk = pl.BlockSpec((tm, BW), lambda i: (i, 0))
    return pl.pallas_call(
        body, grid=(t // tm,),
        in_specs=[blk, blk, blk, blk, pl.BlockSpec((NH, D, BW), lambda i: (0, 0, 0)), pl.BlockSpec((tm, NH * D), lambda i: (i, 0))],
        out_specs=pl.BlockSpec((tm, D), lambda i: (i, 0)), out_shape=_sds((t, D), BF16), name="merge_fwd",
        compiler_params=_cp())(*brs, wbt, gp)


def _merge_bwd(dmerged, brs, wbt, gp):
    t = gp.shape[0]
    tm = MERGE_TM

    def body(d_ref, b0, b1, b2, b3, wb_ref, gp_ref, dgp_ref, dup_ref):
        dm = d_ref[...]
        for n, b_ref in enumerate((b0, b1, b2, b3)):
            up = _dot(b_ref[...], wb_ref[n], tb=True)
            g = _sigmoid(gp_ref[:, n * D:(n + 1) * D])
            dgp_ref[:, n * D:(n + 1) * D] = (dm * up * (g * (1.0 - g))).astype(BF16)
            dup_ref[:, n * D:(n + 1) * D] = (dm * g).astype(BF16)

    row = pl.BlockSpec((tm, D), lambda i: (i, 0))
    blk = pl.BlockSpec((tm, BW), lambda i: (i, 0))
    wide = pl.BlockSpec((tm, NH * D), lambda i: (i, 0))
    return pl.pallas_call(
        body, grid=(t // tm,), in_specs=[row, blk, blk, blk, blk, pl.BlockSpec((NH, D, BW), lambda i: (0, 0, 0)), wide],
        out_specs=(wide, wide), out_shape=(_sds((t, NH * D), BF16), _sds((t, NH * D), BF16)), name="merge_bwd",
        compiler_params=_cp())(dmerged, *brs, wbt, gp)


def _dbranch(dup, wbt):
    t = dup.shape[0]
    tm = 512

    def body(d_ref, w_ref, o_ref):
        o_ref[...] = _dot(d_ref[...], w_ref[...])

    return pl.pallas_call(
        body, grid=(t // tm, NH), in_specs=[pl.BlockSpec((tm, D), lambda i, n: (i, n)), pl.BlockSpec((None, D, BW), lambda i, n: (n, 0, 0))],
        out_specs=pl.BlockSpec((tm, BW), lambda i, n: (i, n)), out_shape=_sds((t, NH * BW), F32), name="dbranch",
        compiler_params=_cp())(dup, wbt)


def _dwbranch(brs, dup):
    t = dup.shape[0]

    def body(b0, b1, b2, b3, d_ref, o_ref):
        for n, b_ref in enumerate((b0, b1, b2, b3)):
            o_ref[n] = _dot(d_ref[:, n * D:(n + 1) * D], b_ref[...], ta=True).astype(BF16)

    return pl.pallas_call(body, out_shape=_sds((NH, D, BW), BF16), name="dwbranch", compiler_params=_cp())(*brs, dup)


def _swiglu_fwd(ag):
    t = ag.shape[0]
    tm = 256

    def body(ag_ref, o_ref):
        a, g = ag_ref[:, 0:FF], ag_ref[:, FF:2 * FF]
        o_ref[...] = (a * _sigmoid(a) * g).astype(BF16)

    return pl.pallas_call(body, grid=(t // tm,), in_specs=[pl.BlockSpec((tm, 2 * FF), lambda i: (i, 0))],
                          out_specs=pl.BlockSpec((tm, FF), lambda i: (i, 0)), out_shape=_sds((t, FF), BF16), name="swiglu_fwd",
                          compiler_params=_cp())(ag)


def _swiglu_bwd(ag, dy):
    t = ag.shape[0]
    tm = 256

    def body(ag_ref, dy_ref, o_ref):
        a, g, d = ag_ref[:, 0:FF], ag_ref[:, FF:2 * FF], dy_ref[...]
        s = _sigmoid(a)
        o_ref[:, 0:FF] = (d * g * (s * (1.0 + a * (1.0 - s)))).astype(BF16)
        o_ref[:, FF:2 * FF] = (d * (a * s)).astype(BF16)

    return pl.pallas_call(
        body, grid=(t // tm,), in_specs=[pl.BlockSpec((tm, 2 * FF), lambda i: (i, 0)), pl.BlockSpec((tm, FF), lambda i: (i, 0))],
        out_specs=pl.BlockSpec((tm, 2 * FF), lambda i: (i, 0)), out_shape=_sds((t, 2 * FF), BF16), name="swiglu_bwd",
        compiler_params=_cp())(ag, dy)


def _loss_head(y, target):
    t, d = y.shape
    tm = 256

    def body(y_ref, t_ref, dy_ref, l_ref):
        e = y_ref[...] - t_ref[...]
        dy_ref[...] = e * (1.0 / d)
        _acc(l_ref, jnp.full((8, 128), 0.5 * jnp.sum(jnp.sum(e * e, axis=-1, keepdims=True) * (1.0 / d)), F32), pl.program_id(0) == 0)

    row = pl.BlockSpec((tm, d), lambda i: (i, 0))
    return pl.pallas_call(body, grid=(t // tm,), in_specs=[row, row], out_specs=(row, pl.BlockSpec((8, 128), lambda i: (0, 0))),
                          out_shape=(_sds((t, d), F32), _sds((8, 128), F32)), name="loss_head", compiler_params=_cp())(y, target)


def _sum_slots(x, name):
    k, r, c = x.shape
    tr = _tile(r, 512) if r % 128 == 0 else r

    def body(x_ref, o_ref):
        acc = x_ref[0].astype(F32)
        for s in range(1, k):
            acc = acc + x_ref[s].astype(F32)
        o_ref[...] = acc

    return pl.pallas_call(body, grid=(r // tr,), in_specs=[pl.BlockSpec((k, tr, c), lambda i: (0, i, 0))],
                          out_specs=pl.BlockSpec((tr, c), lambda i: (i, 0)), out_shape=_sds((r, c), F32), name=name,
                          compiler_params=_cp())(x)


def _pair_sum(buf, recv, cidx):
    k, _, r, c = buf.shape
    tr = r

    def body(c_ref, b_ref, r_ref, o_ref):
        o_ref[...] = (b_ref[...].astype(F32) + r_ref[...].astype(F32)).astype(BF16)

    return pl.pallas_call(
        body,
        grid_spec=pltpu.PrefetchScalarGridSpec(
            num_scalar_prefetch=1, grid=(k, r // tr),
            in_specs=[pl.BlockSpec((None, None, tr, c), lambda s, i, cref: (s, cref[0], i, 0)),
                      pl.BlockSpec((None, tr, c), lambda s, i, cref: (s, i, 0))],
            out_specs=pl.BlockSpec((None, tr, c), lambda s, i, cref: (s, i, 0))),
        out_shape=_sds((k, r, c), BF16), name="rs_pair_sum", compiler_params=_cp())(cidx, buf, recv)


def _adamw(w, g, m, v, name):
    r, c = w.shape
    tr = r
    if r > 1024:
        tr = next(cand for cand in (512, 256, 128, 64, 32, 16, 8) if r % cand == 0)

    def body(w_ref, g_ref, m_ref, v_ref, d_ref, nm_ref, nv_ref):
        gv = g_ref[...]
        mn = ADAM_B1 * m_ref[...] + (1.0 - ADAM_B1) * gv
        vn = ADAM_B2 * v_ref[...] + (1.0 - ADAM_B2) * (gv * gv)
        m_hat = mn / (1.0 - ADAM_B1 ** ADAM_STEP)
        v_hat = vn / (1.0 - ADAM_B2 ** ADAM_STEP)
        d_ref[...] = -ADAM_LR * (m_hat / (jnp.sqrt(v_hat) + ADAM_EPS) + ADAM_WD * w_ref[...])
        nm_ref[...] = mn
        nv_ref[...] = vn

    blk = pl.BlockSpec((tr, c), lambda i: (i, 0))
    return pl.pallas_call(body, grid=(r // tr,), in_specs=[blk] * 4, out_specs=(blk,) * 3,
                          out_shape=tuple(_sds((r, c), F32) for _ in range(3)), name=name, compiler_params=_cp())(w, g, m, v)


def _all_gather(shards, name):
    n = len(shards)

    def body(*refs):
        x_refs, out_refs = refs[:n], refs[n:2 * n]
        send_sems, recv_sems, local_sems = refs[2 * n:]
        x, y, cc = lax.axis_index("x"), lax.axis_index("y"), lax.axis_index("c")
        me, sibling = (x, y, cc), (x, y, 1 - cc)
        chips = [(1 - x, y), (x, 1 - y), (1 - x, 1 - y)]

        def copy(i, k, block, to, own=False):
            px, py, pc = block
            slot = out_refs[i].at[4 * px + 2 * py + pc]
            return pltpu.make_async_remote_copy(
                src_ref=x_refs[i] if own else slot, dst_ref=slot, send_sem=send_sems.at[7 * i + k],
                recv_sem=recv_sems.at[7 * i + k], device_id=to, device_id_type=MESH)

        mine = [pltpu.make_async_copy(x_refs[i], out_refs[i].at[4 * x + 2 * y + cc], local_sems.at[i]) for i in range(n)]
        for cp in mine:
            cp.start()
        first = []
        for j, chip in enumerate(chips):
            first += [copy(i, 1 + j, me, (*chip, cc), own=True) for i in range(n)]
        first += [copy(i, 0, me, sibling, own=True) for i in range(n)]
        for cp in first:
            cp.start()
        passed = []
        for j, chip in enumerate(chips):
            for i in range(n):
                copy(i, 1 + j, (*chip, cc), me).wait_recv()
                cp = copy(i, 4 + j, (*chip, cc), sibling)
                cp.start()
                passed.append(cp)
        for i in range(n):
            copy(i, 0, sibling, me).wait_recv()
        for j, chip in enumerate(chips):
            for i in range(n):
                copy(i, 4 + j, (*chip, 1 - cc), me).wait_recv()
        for cp in first + passed:
            cp.wait_send()
        for cp in mine:
            cp.wait()

    return pl.pallas_call(
        body, out_shape=tuple(_sds((N_DEV,) + s.shape, s.dtype) for s in shards), in_specs=[ANY] * n, out_specs=(ANY,) * n,
        scratch_shapes=[pltpu.SemaphoreType.DMA((7 * n,)), pltpu.SemaphoreType.DMA((7 * n,)), pltpu.SemaphoreType.DMA((n,))],
        name=name)(*shards)


def _rs_core_swap(bufs, name):
    n = len(bufs)

    def body(*refs):
        b_refs, recv_refs = refs[:n], refs[n:2 * n]
        send_sems, recv_sems = refs[2 * n:]
        x, y, cc = lax.axis_index("x"), lax.axis_index("y"), lax.axis_index("c")
        copies = [pltpu.make_async_remote_copy(
            src_ref=b_refs[i].at[s, 1 - cc], dst_ref=recv_refs[i].at[s], send_sem=send_sems.at[4 * i + s],
            recv_sem=recv_sems.at[4 * i + s], device_id=(x, y, 1 - cc), device_id_type=MESH) for i in range(n) for s in range(4)]
        for cp in copies:
            cp.start()
        for cp in copies:
            cp.wait()

    return pl.pallas_call(
        body, out_shape=tuple(_sds((4,) + b.shape[2:], b.dtype) for b in bufs), in_specs=[ANY] * n, out_specs=(ANY,) * n,
        scratch_shapes=[pltpu.SemaphoreType.DMA((4 * n,)), pltpu.SemaphoreType.DMA((4 * n,))], name=name)(*bufs)


def _rs_chip_swap(parts, name):
    n = len(parts)

    def body(*refs):
        p_refs, recv_refs = refs[:n], refs[n:2 * n]
        send_sems, recv_sems, local_sems = refs[2 * n:]
        x, y, cc = lax.axis_index("x"), lax.axis_index("y"), lax.axis_index("c")
        my_chip = 2 * x + y
        own = [pltpu.make_async_copy(p_refs[i].at[my_chip], recv_refs[i].at[my_chip], local_sems.at[i]) for i in range(n)]
        for cp in own:
            cp.start()
        copies = []
        for j, (px, py) in enumerate([(1 - x, y), (x, 1 - y), (1 - x, 1 - y)]):
            copies += [pltpu.make_async_remote_copy(
                src_ref=p_refs[i].at[2 * px + py], dst_ref=recv_refs[i].at[my_chip], send_sem=send_sems.at[3 * i + j],
                recv_sem=recv_sems.at[3 * i + j], device_id=(px, py, cc), device_id_type=MESH) for i in range(n)]
        for cp in copies:
            cp.start()
        for cp in copies:
            cp.wait()
        for cp in own:
            cp.wait()

    return pl.pallas_call(
        body, out_shape=tuple(_sds(p.shape, p.dtype) for p in parts), in_specs=[ANY] * n, out_specs=(ANY,) * n,
        scratch_shapes=[pltpu.SemaphoreType.DMA((3 * n,)), pltpu.SemaphoreType.DMA((3 * n,)), pltpu.SemaphoreType.DMA((n,))],
        name=name)(*parts)


BIG = (("w_in", True), ("w_gate", True), ("w_mem_kv", False), ("w_branch", True), ("w_out", False), ("w_ffn_in", True),
       ("w_ffn_out", False))

SMALL = ("norm_mix_g", "norm_mem_g", "ret_decay_fwd", "ret_decay_bwd", "ret_norm_g", "pool_w", "pool_scale", "na_q_norm_g",
         "na_k_norm_g", "na_rpb", "mem_q_norm_g", "mem_k_norm_g", "norm_ffn_g")
WEIGHTS = ("norm_mix_g", "norm_mem_g", "w_in", "w_gate", "ret_decay_fwd", "ret_decay_bwd", "ret_norm_g", "pool_w", "pool_scale",
           "na_q_norm_g", "na_k_norm_g", "na_rpb", "mem_q_norm_g", "mem_k_norm_g", "w_mem_kv", "w_branch", "w_out", "norm_ffn_g",
           "w_ffn_in", "w_ffn_out")


def _to_exchange(name, transposed, shard):
    if name == "w_branch":
        return jnp.swapaxes(shard, 1, 2).reshape(NH * (D // N_DEV), BW)
    return shard.T if transposed else shard


def _from_exchange(name, transposed, block):
    if name == "w_branch":
        return jnp.swapaxes(block.reshape(NH, D // N_DEV, BW), 1, 2)
    return block.T if transposed else block


def _whole_from_gathered(name, g):
    if name == "w_branch":
        return jnp.swapaxes(g.reshape(N_DEV, NH, D // N_DEV, BW), 0, 1).reshape(NH, D, BW)
    return g.reshape(N_DEV * g.shape[1], g.shape[2])


def _by_destination(name, g):
    if name == "w_branch":
        g = jnp.swapaxes(g.reshape(NH, N_DEV, D // N_DEV, BW), 0, 1).reshape(N_DEV * NH * (D // N_DEV), BW)
    return g.reshape(4, 2, g.shape[0] // N_DEV, g.shape[1])


SMALL_PAD = 1024


def _pack_small(vals, loss=None):
    parts = [vals[n] for n in SMALL] + [jnp.zeros((1,), F32) if loss is None else loss.reshape(1)]
    rows = []
    for p in parts:
        flat = p.reshape(-1)
        rows.append(jnp.pad(flat, (0, -flat.shape[0] % SMALL_PAD)).reshape(-1, 128))
    return jnp.concatenate(rows, axis=0)


def _unpack_small(packed, like):
    out, off = {}, 0
    for n in SMALL:
        sz = int(np.prod(like[n].shape))
        nrow = -(-sz // SMALL_PAD) * (SMALL_PAD // 128)
        out[n] = packed[off:off + nrow].reshape(-1)[:sz].reshape(like[n].shape)
        off += nrow
    return out, packed[off, 0]


def _na_constants():
    c = np.arange(GRID_W)
    win = np.clip(c - NA_COLS_WIN // 2, 0, GRID_W - NA_COLS_WIN)
    kc = np.arange(GRID_W)
    inside = (kc[None, :] >= win[:, None]) & (kc[None, :] < win[:, None] + NA_COLS_WIN)
    off = kc[None, :] - c[:, None] + NA_COLS_WIN - 1
    onehot = np.zeros((128, GRID_W, GRID_W), np.float32)
    for b in range(2 * NA_COLS_WIN - 1):
        onehot[b] = (off == b) & inside
    maskadd = np.where(inside, 0.0, NEG).astype(np.float32)
    return onehot.reshape(128, GRID_W * GRID_W), maskadd


def _na_bias_table(tab, maskadd):
    n_off = 2 * NA_ROWS_WIN - 1
    t4 = tab[:NH * n_off].reshape(NH, n_off, GRID_W, GRID_W) + maskadd[None, None]
    ball = jnp.stack([t4[:, a0:a0 + NA_ROWS_WIN] for a0 in range(NA_ROWS_WIN)], axis=1)
    return ball.transpose(1, 0, 3, 2, 4).reshape(NA_ROWS_WIN, NH * GRID_W, NA_KEYS)


def _rotary_tables(t):
    half = HD // 2
    inv = ROPE_THETA ** (-jnp.arange(half, dtype=F32) / half)
    ang = jnp.arange(t, dtype=F32)[:, None] * inv[None, :]
    cos, sin = jnp.cos(ang), jnp.sin(ang)
    return jnp.tile(jnp.concatenate([cos, cos], axis=-1), (1, NH)), jnp.tile(jnp.concatenate([-sin, sin], axis=-1), (1, NH))


def _block_diag(pw):
    out = jnp.zeros((BW, BW), pw.dtype)
    for g in range(NH):
        out = lax.dynamic_update_slice(out, pw[g], (g * HD, g * HD))
    return out


def _tile4(g):
    return jnp.tile(g.reshape(1, HD), (1, NH))


def _layer_fwd(x, mem, sw, lw, consts):
    cos2, sin2, onehot, maskadd = consts
    h = _rmsnorm_fwd(x, sw["norm_mix_g"].reshape(1, D), "norm_mix_fwd")
    proj = _mm(h, lw["w_in"], tb=True, name="mm_in")
    gp = _mm(h, lw["w_gate"], tb=True, name="mm_gate")
    g_naq, g_nak, g_mq = _tile4(sw["na_q_norm_g"]), _tile4(sw["na_k_norm_g"]), _tile4(sw["mem_q_norm_g"])
    rq, rk, rv, nq, nk, nv, mq = _prep_fwd(proj, cos2, sin2, g_naq, g_nak, g_mq)

    lgf, lgb = jax.nn.log_sigmoid(sw["ret_decay_fwd"]), jax.nn.log_sigmoid(sw["ret_decay_bwd"])
    g_ret = sw["ret_norm_g"].reshape(1, BW)
    o_ret, ret = _ret_fwd(rq, rk, rv, proj, lgf, lgb, g_ret)

    wbd = _block_diag(sw["pool_w"]).astype(BF16)
    p_scale = sw["pool_scale"].reshape(1, BW)
    pool = _pool_fwd(proj, wbd, p_scale)

    rpb_pad = jnp.pad(sw["na_rpb"].reshape(NH * 15, 31), ((0, 4), (0, 97)))
    ball = _na_bias_table(_rpb_expand(rpb_pad, onehot), maskadd)
    na = _na_fwd(nq, nk, nv, ball)

    memn = _rmsnorm_fwd(mem, sw["norm_mem_g"].reshape(1, D), "norm_mem_fwd")
    kv = _mm(memn, lw["w_mem_kv"], name="mm_memkv")
    g_mk = _tile4(sw["mem_k_norm_g"])
    mk, mv = _memkv_prep(kv, g_mk)
    mo = _mem_fwd(mq, mk, mv)

    br = (ret, pool, na, mo)
    merged = _merge_fwd(br, lw["w_branch"], gp)
    x1 = _mm(merged, lw["w_out"], add=x, name="mm_out")
    h2 = _rmsnorm_fwd(x1, sw["norm_ffn_g"].reshape(1, D), "norm_ffn_fwd")
    ag = _mm(h2, lw["w_ffn_in"], tb=True, name="mm_ffn_in")
    yff = _swiglu_fwd(ag)
    x2 = _mm(yff, lw["w_ffn_out"], add=x1, name="mm_ffn_out")
    saved = dict(x=x, h=h, proj=proj, gp=gp, rq=rq, rk=rk, rv=rv, nq=nq, nk=nk, nv=nv, mq=mq, o_ret=o_ret, ball=ball, memn=memn,
                 kv=kv, mk=mk, mv=mv, br=br, merged=merged, x1=x1, h2=h2, ag=ag, yff=yff, lgf=lgf, lgb=lgb, wbd=wbd)
    return x2, saved


def _layer_bwd(dx2, mem, sw, lw, sv, consts):
    cos2, sin2, onehot, maskadd = consts
    gb, gs = {}, {}
    dy = _mm(dx2, lw["w_ffn_out"], tb=True, name="mm_ffn_out_dx")
    gb["w_ffn_out"] = _mm(sv["yff"], dx2, ta=True, out_dtype=BF16, name="mm_ffn_out_dw")
    dag = _swiglu_bwd(sv["ag"], dy)
    dh2 = _mm(dag, lw["w_ffn_in"], name="mm_ffn_in_dx")
    gb["w_ffn_in"] = _mm(dag, sv["h2"], ta=True, out_dtype=BF16, name="mm_ffn_in_dw")
    dx1, dg = _rmsnorm_bwd(dh2, sv["x1"], sw["norm_ffn_g"].reshape(1, D), dx2, "norm_ffn_bwd")
    gs["norm_ffn_g"] = dg.reshape(D)

    dmerged = _mm(dx1, lw["w_out"], tb=True, name="mm_out_dx")
    gb["w_out"] = _mm(sv["merged"], dx1, ta=True, out_dtype=BF16, name="mm_out_dw")
    dgp, dup = _merge_bwd(dmerged, sv["br"], lw["w_branch"], sv["gp"])
    dbr = _dbranch(dup, lw["w_branch"])
    gb["w_branch"] = _dwbranch(sv["br"], dup)

    g_ret = sw["ret_norm_g"].reshape(1, BW)
    d_rq, d_rk, d_rv, d_rg, dg_ret, dlg = _ret_bwd(dbr, sv["o_ret"], sv["rq"], sv["rk"], sv["rv"], sv["proj"], sv["lgf"], sv["lgb"], g_ret)
    gs["ret_norm_g"] = dg_ret.reshape(BW)
    _, vjp_f = jax.vjp(jax.nn.log_sigmoid, sw["ret_decay_fwd"])
    _, vjp_b = jax.vjp(jax.nn.log_sigmoid, sw["ret_decay_bwd"])
    gs["ret_decay_fwd"] = vjp_f(dlg[0:NH, 0])[0]
    gs["ret_decay_bwd"] = vjp_b(dlg[NH:2 * NH, 0])[0]

    p_scale = sw["pool_scale"].reshape(1, BW)
    d_pv, dwbd, dscale = _pool_bwd(dbr, sv["proj"], sv["wbd"], p_scale)
    gs["pool_w"] = jnp.stack([dwbd[g * HD:(g + 1) * HD, g * HD:(g + 1) * HD] for g in range(NH)])
    gs["pool_scale"] = dscale.reshape(BW)

    d_nq, d_nk, d_nv, dball = _na_bwd(dbr, sv["nq"], sv["nk"], sv["nv"], sv["ball"])
    _, vjp_tab = jax.vjp(lambda tab: _na_bias_table(tab, maskadd), jnp.zeros((64, GRID_W * GRID_W), F32))
    drpb = _rpb_reduce(vjp_tab(dball)[0], onehot)
    gs["na_rpb"] = drpb[:NH * 15, :31].reshape(NH, 15, 31)

    d_mq, d_mk, d_mv = _mem_bwd(dbr, sv["mq"], sv["mk"], sv["mv"])
    g_mk = _tile4(sw["mem_k_norm_g"])
    dkv, dg_mk = _memkv_bwd(sv["kv"], d_mk, d_mv, g_mk)
    gs["mem_k_norm_g"] = dg_mk.reshape(NH, HD).sum(0)
    gb["w_mem_kv"] = _mm(sv["memn"], dkv, ta=True, out_dtype=BF16, name="mm_memkv_dw")
    dmemn = _mm(dkv, lw["w_mem_kv"], tb=True, name="mm_memkv_dx")
    _, dg_mem = _rmsnorm_bwd(dmemn, mem, sw["norm_mem_g"].reshape(1, D), jnp.zeros_like(mem), "norm_mem_bwd")
    gs["norm_mem_g"] = dg_mem.reshape(D)

    g_naq, g_nak, g_mq = _tile4(sw["na_q_norm_g"]), _tile4(sw["na_k_norm_g"]), _tile4(sw["mem_q_norm_g"])
    dproj, dg_naq, dg_nak, dg_mq = _prep_bwd(sv["proj"], cos2, sin2, g_naq, g_nak, g_mq, d_rq, d_rk, d_rv, d_rg, d_pv, d_nq, d_nk,
                                             d_nv, d_mq)
    gs["na_q_norm_g"] = dg_naq.reshape(NH, HD).sum(0)
    gs["na_k_norm_g"] = dg_nak.reshape(NH, HD).sum(0)
    gs["mem_q_norm_g"] = dg_mq.reshape(NH, HD).sum(0)

    dh = _mm(dproj, lw["w_in"], name="mm_in_dx")
    dh = _mm(dgp, lw["w_gate"], add=dh, name="mm_gate_dx")
    gb["w_in"] = _mm(dproj, sv["h"], ta=True, out_dtype=BF16, name="mm_in_dw")
    gb["w_gate"] = _mm(dgp, sv["h"], ta=True, out_dtype=BF16, name="mm_gate_dw")
    dx, dg = _rmsnorm_bwd(dh, sv["x"], sw["norm_mix_g"].reshape(1, D), dx1, "norm_mix_bwd")
    gs["norm_mix_g"] = dg.reshape(D)
    return dx, gb, gs


def _local_step(x, mem, target, small, layers):
    t = x.shape[0]
    cos2, sin2 = _rotary_tables(t)
    onehot, maskadd = _na_constants()
    consts = (cos2, sin2, jnp.asarray(onehot), jnp.asarray(maskadd))
    saved, cur = [], x
    for l in range(DEPTH):
        sw = {n: small[n][l] for n in SMALL}
        cur, sv = _layer_fwd(cur, mem, sw, layers[l], consts)
        saved.append(sv)
    dy, loss_tile = _loss_head(cur, target)
    big = [None] * DEPTH
    small_g = {n: [None] * DEPTH for n in SMALL}
    for l in reversed(range(DEPTH)):
        sw = {n: small[n][l] for n in SMALL}
        dy, gb, gs = _layer_bwd(dy, mem, sw, layers[l], saved[l], consts)
        big[l] = gb
        for n in SMALL:
            small_g[n][l] = gs[n]
    return loss_tile[0, 0], dy, big, {n: jnp.stack(v) for n, v in small_g.items()}


def _flat2d(a):
    return a.reshape(-1, a.shape[-1])


def kernel(x, mem, norm_mix_g, norm_mem_g, w_in, w_gate, ret_decay_fwd, ret_decay_bwd, ret_norm_g, pool_w, pool_scale, na_q_norm_g, na_k_norm_g, na_rpb, mem_q_norm_g, mem_k_norm_g, w_mem_kv, w_branch, w_out, norm_ffn_g, w_ffn_in, w_ffn_out, loss_target, m_norm_mix_g, m_norm_mem_g, m_w_in, m_w_gate, m_ret_decay_fwd, m_ret_decay_bwd, m_ret_norm_g, m_pool_w, m_pool_scale, m_na_q_norm_g, m_na_k_norm_g, m_na_rpb, m_mem_q_norm_g, m_mem_k_norm_g, m_w_mem_kv, m_w_branch, m_w_out, m_norm_ffn_g, m_w_ffn_in, m_w_ffn_out, v_norm_mix_g, v_norm_mem_g, v_w_in, v_w_gate, v_ret_decay_fwd, v_ret_decay_bwd, v_ret_norm_g, v_pool_w, v_pool_scale, v_na_q_norm_g, v_na_k_norm_g, v_na_rpb, v_mem_q_norm_g, v_mem_k_norm_g, v_w_mem_kv, v_w_branch, v_w_out, v_norm_ffn_g, v_w_ffn_in, v_w_ffn_out):
    w = dict(norm_mix_g=norm_mix_g, norm_mem_g=norm_mem_g, w_in=w_in, w_gate=w_gate, ret_decay_fwd=ret_decay_fwd,
             ret_decay_bwd=ret_decay_bwd, ret_norm_g=ret_norm_g, pool_w=pool_w, pool_scale=pool_scale, na_q_norm_g=na_q_norm_g,
             na_k_norm_g=na_k_norm_g, na_rpb=na_rpb, mem_q_norm_g=mem_q_norm_g, mem_k_norm_g=mem_k_norm_g, w_mem_kv=w_mem_kv,
             w_branch=w_branch, w_out=w_out, norm_ffn_g=norm_ffn_g, w_ffn_in=w_ffn_in, w_ffn_out=w_ffn_out)
    m = dict(norm_mix_g=m_norm_mix_g, norm_mem_g=m_norm_mem_g, w_in=m_w_in, w_gate=m_w_gate, ret_decay_fwd=m_ret_decay_fwd,
             ret_decay_bwd=m_ret_decay_bwd, ret_norm_g=m_ret_norm_g, pool_w=m_pool_w, pool_scale=m_pool_scale, na_q_norm_g=m_na_q_norm_g,
             na_k_norm_g=m_na_k_norm_g, na_rpb=m_na_rpb, mem_q_norm_g=m_mem_q_norm_g, mem_k_norm_g=m_mem_k_norm_g, w_mem_kv=m_w_mem_kv,
             w_branch=m_w_branch, w_out=m_w_out, norm_ffn_g=m_norm_ffn_g, w_ffn_in=m_w_ffn_in, w_ffn_out=m_w_ffn_out)
    v = dict(norm_mix_g=v_norm_mix_g, norm_mem_g=v_norm_mem_g, w_in=v_w_in, w_gate=v_w_gate, ret_decay_fwd=v_ret_decay_fwd,
             ret_decay_bwd=v_ret_decay_bwd, ret_norm_g=v_ret_norm_g, pool_w=v_pool_w, pool_scale=v_pool_scale, na_q_norm_g=v_na_q_norm_g,
             na_k_norm_g=v_na_k_norm_g, na_rpb=v_na_rpb, mem_q_norm_g=v_mem_q_norm_g, mem_k_norm_g=v_mem_k_norm_g, w_mem_kv=v_w_mem_kv,
             w_branch=v_w_branch, w_out=v_w_out, norm_ffn_g=v_norm_ffn_g, w_ffn_in=v_w_ffn_in, w_ffn_out=v_w_ffn_out)
    assert x.shape == (1, 2048, D) and mem.shape == (1, N_MEM, D) and w_in.shape == (DEPTH, D, 9 * BW // N_DEV)

    layers = []
    for l in range(DEPTH):
        blocks = [_to_exchange(name, tr, w[name][l]).astype(BF16) for name, tr in BIG]
        gathered = _all_gather(blocks, "gather_weights")
        layers.append({name: _whole_from_gathered(name, g) for (name, _), g in zip(BIG, gathered)})

    loss_local, dx, big_g, small_g = _local_step(x[0], mem[0], loss_target[0], {n: w[n] for n in SMALL}, layers)

    cidx = lax.axis_index("c").astype(jnp.int32).reshape(1)
    g_layers = []
    for l in range(DEPTH):
        send = [_by_destination(name, big_g[l][name]) for name, _ in BIG]
        from_core = _rs_core_swap(send, "rs_core_swap")
        chip_part = [_pair_sum(s, r, cidx) for s, r in zip(send, from_core)]
        by_chip = _rs_chip_swap(chip_part, "rs_chip_swap")
        g_layers.append({name: _from_exchange(name, tr, _sum_slots(b, "rs_chip_sum")) for (name, tr), b in zip(BIG, by_chip)})
    g_shard = {name: jnp.stack([g_layers[l][name] for l in range(DEPTH)]) for name, _ in BIG}

    small_all, = _all_gather([_pack_small(small_g, loss_local)], "gather_small")
    packed_g = _sum_slots(small_all, "small_sum")
    small_sum, loss = _unpack_small(packed_g, {n: w[n] for n in SMALL})

    grads, delta, new_m, new_v = {}, {}, {}, {}
    for name, _ in BIG:
        grads[name] = g_shard[name]
        d_, m_, v_ = _adamw(_flat2d(w[name]), _flat2d(grads[name]), _flat2d(m[name]), _flat2d(v[name]), "adamw_" + name)
        delta[name], new_m[name], new_v[name] = (a.reshape(w[name].shape) for a in (d_, m_, v_))
    d_, m_, v_ = _adamw(_pack_small({n: w[n] for n in SMALL}), packed_g, _pack_small({n: m[n] for n in SMALL}),
                        _pack_small({n: v[n] for n in SMALL}), "adamw_small")
    like = {n: w[n] for n in SMALL}
    ds, _ = _unpack_small(d_, like)
    ms, _ = _unpack_small(m_, like)
    vs, _ = _unpack_small(v_, like)
    for n in SMALL:
        grads[n], delta[n], new_m[n], new_v[n] = small_sum[n], ds[n], ms[n], vs[n]

    return (loss, dx[None], *[grads[n] for n in WEIGHTS], *[delta[n] for n in WEIGHTS], *[new_m[n] for n in WEIGHTS],
            *[new_v[n] for n in WEIGHTS])
```

```python
import functools

import numpy as np
import jax
import jax.numpy as jnp
from jax import lax
from jax.experimental import pallas as pl
from jax.experimental.pallas import tpu as pltpu

F32 = jnp.float32
BF16 = jnp.bfloat16
MXU = jnp.bfloat16
HI = lax.Precision.HIGHEST

DEPTH = 4
D = 1024
BW = 256
HD = 64
NH = 4
GRID_W = 64
NA_ROWS_WIN = 8
NA_COLS_WIN = 16
N_MEM = 256
FF = 2816
EPS = 1e-6
NEG = -1e30
ROPE_THETA = 10000.0
POOL_HALF_MAX = 8

ADAM_LR, ADAM_B1, ADAM_B2, ADAM_EPS, ADAM_WD, ADAM_STEP = 0.001, 0.9, 0.999, 1e-08, 0.01, 10

N_DEV = 8
VMEM_LIMIT = 56 * 1024 * 1024

RQ, RK, RV, RG, PV, NQ, NK, NV, MQ = range(9)

MESH = pl.DeviceIdType.MESH
ANY = pl.BlockSpec(memory_space=pl.ANY)
SMEM = pl.BlockSpec(memory_space=pltpu.SMEM)


def _cp(**kw):
    return pltpu.CompilerParams(vmem_limit_bytes=VMEM_LIMIT, **kw)


def _tile(n, cap):
    if n <= cap:
        return n
    best = None
    for t in range(128, cap + 1, 128):
        if n % t == 0:
            best = t
    assert best is not None, (n, cap)
    return best


def _sds(shape, dtype):
    return jax.ShapeDtypeStruct(shape, dtype)


def _lane_head(shape):
    return lax.shift_right_logical(lax.broadcasted_iota(jnp.int32, shape, len(shape) - 1), 6)


def _group_mean(z):
    i = lax.shift_right_logical(lax.broadcasted_iota(jnp.int32, (BW, BW), 0), 6)
    j = lax.shift_right_logical(lax.broadcasted_iota(jnp.int32, (BW, BW), 1), 6)
    g = jnp.where(i == j, 1.0 / HD, 0.0).astype(F32)
    return jnp.dot(z, g, precision=HI, preferred_element_type=F32)


def _gnorm(t, g):
    r = lax.rsqrt(_group_mean(t * t) + EPS)
    return t * r * g


def _gnorm_bwd(dy, t, g):
    r = lax.rsqrt(_group_mean(t * t) + EPS)
    th = t * r
    dth = dy * g
    dt = r * (dth - th * _group_mean(dth * th))
    return dt, dy * th


def _swap_halves(t):
    lane = lax.broadcasted_iota(jnp.int32, t.shape, 1)
    return jnp.where((lane & 63) < 32, pltpu.roll(t, BW - 32, 1), pltpu.roll(t, 32, 1))


def _sigmoid(x):
    return 1.0 / (1.0 + jnp.exp(-x))


def _dot(a, b, ta=False, tb=False):
    return lax.dot_general(a.astype(MXU), b.astype(MXU), (((0 if ta else 1,), (1 if tb else 0,)), ((), ())),
                           preferred_element_type=F32)


def _stack_heads(t):
    head = _lane_head(t.shape)
    return jnp.concatenate([jnp.where(head == h, t, jnp.zeros_like(t)) for h in range(NH)], axis=0)


def _unstack_heads(t, rows):
    head = _lane_head((rows, BW))
    out = jnp.zeros((rows, BW), F32)
    for h in range(NH):
        out = out + jnp.where(head == h, t[h * rows:(h + 1) * rows], 0.0)
    return out


def _softmax_rows(s):
    m = jnp.max(s, axis=-1, keepdims=True)
    e = jnp.exp(s - m)
    return e / jnp.sum(e, axis=-1, keepdims=True)


def _acc(ref, val, first):
    @pl.when(first)
    def _():
        ref[...] = val

    @pl.when(jnp.logical_not(first))
    def _():
        ref[...] += val


def _mm(a, b, *, ta=False, tb=False, out_dtype=F32, add=None, dep=None, name):
    m, k = (a.shape[1], a.shape[0]) if ta else a.shape
    n = b.shape[0] if tb else b.shape[1]
    tm, tn, tk = _tile(m, 1024), _tile(n, 512), _tile(k, 1024)
    nk = k // tk

    def body(*refs):
        if dep is not None:
            refs = refs[:-3] + refs[-2:]
        if add is None:
            a_ref, b_ref, o_ref, acc_ref = refs
        else:
            a_ref, b_ref, c_ref, o_ref, acc_ref = refs
        kk = pl.program_id(2)
        _acc(acc_ref, _dot(a_ref[...], b_ref[...], ta, tb), kk == 0)

        @pl.when(kk == nk - 1)
        def _():
            r = acc_ref[...]
            if add is not None:
                r = r + c_ref[...]
            o_ref[...] = r.astype(out_dtype)

    a_spec = pl.BlockSpec((tk, tm), lambda i, j, kk: (kk, i)) if ta else pl.BlockSpec((tm, tk), lambda i, j, kk: (i, kk))
    b_spec = pl.BlockSpec((tn, tk), lambda i, j, kk: (j, kk)) if tb else pl.BlockSpec((tk, tn), lambda i, j, kk: (kk, j))
    o_spec = pl.BlockSpec((tm, tn), lambda i, j, kk: (i, j))
    ins, args = [a_spec, b_spec], [a, b]
    if add is not None:
        ins.append(o_spec)
        args.append(add)
    if dep is not None:
        ins.append(pl.BlockSpec((8, 128), lambda i, j, kk: (0, 0)))
        args.append(dep)
    return pl.pallas_call(
        body, grid=(m // tm, n // tn, nk), in_specs=ins, out_specs=o_spec, out_shape=_sds((m, n), out_dtype),
        scratch_shapes=[pltpu.VMEM((tm, tn), F32)], name=name,
        compiler_params=_cp(dimension_semantics=("parallel", "parallel", "arbitrary")))(*args)


def _rmsnorm_fwd(x, g, name):
    t, d = x.shape
    tm = _tile(t, 256)

    def body(x_ref, g_ref, o_ref):
        xv = x_ref[...]
        r = lax.rsqrt(jnp.mean(xv * xv, axis=-1, keepdims=True) + EPS)
        o_ref[...] = (xv * r * g_ref[...]).astype(o_ref.dtype)

    return pl.pallas_call(
        body, grid=(t // tm,), in_specs=[pl.BlockSpec((tm, d), lambda i: (i, 0)), pl.BlockSpec((1, d), lambda i: (0, 0))],
        out_specs=pl.BlockSpec((tm, d), lambda i: (i, 0)), out_shape=_sds((t, d), BF16), name=name, compiler_params=_cp())(x, g)


def _rmsnorm_bwd(dh, x, g, res, name):
    t, d = x.shape
    tm = _tile(t, 256)

    def body(dh_ref, x_ref, g_ref, res_ref, dx_ref, dg_ref):
        xv = x_ref[...]
        dhv = dh_ref[...]
        r = lax.rsqrt(jnp.mean(xv * xv, axis=-1, keepdims=True) + EPS)
        xh = xv * r
        dxh = dhv * g_ref[...]
        dx_ref[...] = res_ref[...] + r * (dxh - xh * jnp.mean(dxh * xh, axis=-1, keepdims=True))
        _acc(dg_ref, jnp.sum(dhv * xh, axis=0, keepdims=True), pl.program_id(0) == 0)

    row = pl.BlockSpec((tm, d), lambda i: (i, 0))
    vec = pl.BlockSpec((1, d), lambda i: (0, 0))
    return pl.pallas_call(
        body, grid=(t // tm,), in_specs=[row, row, vec, row], out_specs=(row, vec),
        out_shape=(_sds((t, d), F32), _sds((1, d), F32)), name=name, compiler_params=_cp())(dh, x, g, res)


def _prep_fwd(proj, cos2, sin2, g_naq, g_nak, g_mq):
    t = proj.shape[0]
    tm = 256

    def body(p_ref, cos_ref, sin_ref, gq_ref, gk_ref, gm_ref, rq_ref, rk_ref, rv_ref, nq_ref, nk_ref, nv_ref, mq_ref):
        def col(c):
            return p_ref[:, c * BW:(c + 1) * BW]

        cosv, sinv = cos_ref[...], sin_ref[...]

        def rot(tv):
            return tv * cosv + _swap_halves(tv) * sinv

        rq_ref[...] = (rot(col(RQ)) * (HD ** -0.5)).astype(BF16)
        rk_ref[...] = rot(col(RK)).astype(BF16)
        rv_ref[...] = col(RV).astype(BF16)
        nq_ref[...] = _gnorm(col(NQ), gq_ref[...]).astype(BF16)
        nk_ref[...] = _gnorm(col(NK), gk_ref[...]).astype(BF16)
        nv_ref[...] = col(NV).astype(BF16)
        mq_ref[...] = _gnorm(col(MQ), gm_ref[...]).astype(BF16)

    blk = pl.BlockSpec((tm, BW), lambda i: (i, 0))
    vec = pl.BlockSpec((1, BW), lambda i: (0, 0))
    return pl.pallas_call(
        body, grid=(t // tm,), in_specs=[pl.BlockSpec((tm, 9 * BW), lambda i: (i, 0)), blk, blk, vec, vec, vec],
        out_specs=tuple(blk for _ in range(7)), out_shape=tuple(_sds((t, BW), BF16) for _ in range(7)),
        name="prep_fwd", compiler_params=_cp())(proj, cos2, sin2, g_naq, g_nak, g_mq)


def _prep_bwd(proj, cos2, sin2, g_naq, g_nak, g_mq, d_rq, d_rk, d_rv, d_rg, d_pv, d_nq, d_nk, d_nv, d_mq):
    t = proj.shape[0]
    tm = 256

    def body(p_ref, cos_ref, sin_ref, gq_ref, gk_ref, gm_ref, drq_ref, drk_ref, drv_ref, drg_ref, dpv_ref, dnq_ref, dnk_ref,
             dnv_ref, dmq_ref, o_ref, dgq_ref, dgk_ref, dgm_ref):
        first = pl.program_id(0) == 0

        def col(c):
            return p_ref[:, c * BW:(c + 1) * BW]

        def put(c, v):
            o_ref[:, c * BW:(c + 1) * BW] = v.astype(BF16)

        cosv, sinv = cos_ref[...], sin_ref[...]

        def rot_t(dv):
            return dv * cosv + _swap_halves(dv * sinv)

        put(RQ, rot_t(drq_ref[...] * (HD ** -0.5)))
        put(RK, rot_t(drk_ref[...]))
        put(RV, drv_ref[...])
        put(RG, drg_ref[...])
        put(PV, dpv_ref[...])
        dq, gq = _gnorm_bwd(dnq_ref[...], col(NQ), gq_ref[...])
        put(NQ, dq)
        _acc(dgq_ref, jnp.sum(gq, axis=0, keepdims=True), first)
        dk, gk = _gnorm_bwd(dnk_ref[...], col(NK), gk_ref[...])
        put(NK, dk)
        _acc(dgk_ref, jnp.sum(gk, axis=0, keepdims=True), first)
        put(NV, dnv_ref[...])
        dm, gm = _gnorm_bwd(dmq_ref[...], col(MQ), gm_ref[...])
        put(MQ, dm)
        _acc(dgm_ref, jnp.sum(gm, axis=0, keepdims=True), first)

    blk = pl.BlockSpec((tm, BW), lambda i: (i, 0))
    vec = pl.BlockSpec((1, BW), lambda i: (0, 0))
    wide = pl.BlockSpec((tm, 9 * BW), lambda i: (i, 0))
    return pl.pallas_call(
        body, grid=(t // tm,), in_specs=[wide, blk, blk, vec, vec, vec] + [blk] * 9, out_specs=(wide, vec, vec, vec),
        out_shape=(_sds((t, 9 * BW), BF16), _sds((1, BW), F32), _sds((1, BW), F32), _sds((1, BW), F32)),
        name="prep_bwd", compiler_params=_cp())(proj, cos2, sin2, g_naq, g_nak, g_mq, d_rq, d_rk, d_rv, d_rg, d_pv, d_nq, d_nk,
                                                d_nv, d_mq)


RET_TQ = 64


def _ret_decay(i, tq, t, lgf_ref, lgb_ref):
    rows = NH * tq
    n = i * tq + (lax.broadcasted_iota(jnp.int32, (rows, 1), 0) & (tq - 1))
    m = lax.broadcasted_iota(jnp.int32, (1, t), 1)
    diff = n - m
    causal = diff >= 0
    dist = jnp.abs(diff).astype(F32)
    lgf = jnp.concatenate([jnp.full((tq, 1), lgf_ref[h], F32) for h in range(NH)], axis=0)
    lgb = jnp.concatenate([jnp.full((tq, 1), lgb_ref[h], F32) for h in range(NH)], axis=0)
    return causal, dist, jnp.exp(dist * jnp.where(causal, lgf, lgb))


def _ret_fwd(q, k, v, proj, lgf, lgb, g_ret):
    t = q.shape[0]
    tq = RET_TQ

    def body(lgf_ref, lgb_ref, q_ref, k_ref, v_ref, rg_ref, g_ref, o_ref, ret_ref):
        i = pl.program_id(0)
        qs = _stack_heads(q_ref[...])
        s = _dot(qs, k_ref[...], tb=True)
        _, _, dm = _ret_decay(i, tq, t, lgf_ref, lgb_ref)
        o = _unstack_heads(_dot(s * dm, v_ref[...]), tq)
        o_ref[...] = o
        rg = rg_ref[...]
        ret_ref[...] = (_gnorm(o, g_ref[...]) * (rg * _sigmoid(rg))).astype(BF16)

    blk = pl.BlockSpec((tq, BW), lambda i: (i, 0))
    whole = pl.BlockSpec((t, BW), lambda i: (0, 0))
    return pl.pallas_call(
        body, grid=(t // tq,),
        in_specs=[SMEM, SMEM, blk, whole, whole, pl.BlockSpec((tq, BW), lambda i: (i, RG)), pl.BlockSpec((1, BW), lambda i: (0, 0))],
        out_specs=(blk, blk), out_shape=(_sds((t, BW), F32), _sds((t, BW), BF16)), name="ret_fwd",
        compiler_params=_cp())(lgf, lgb, q, k, v, proj, g_ret)


def _ret_bwd(dbr, o_ret, q, k, v, proj, lgf, lgb, g_ret):
    t = q.shape[0]
    tq = RET_TQ
    nblk = t // tq

    def body(lgf_ref, lgb_ref, d_ref, o_ref, q_ref, k_ref, v_ref, rg_ref, g_ref,
             dq_ref, dk_ref, dv_ref, drg_ref, dg_ref, dlg_ref, accf_ref, accb_ref):
        i = pl.program_id(0)
        first = i == 0
        dret, o, rg, g = d_ref[...], o_ref[...], rg_ref[...], g_ref[...]
        sg = _sigmoid(rg)
        dy = dret * (rg * sg)
        do, dgain = _gnorm_bwd(dy, o, g)
        drg_ref[...] = dret * _gnorm(o, g) * (sg * (1.0 + rg * (1.0 - sg)))
        _acc(dg_ref, jnp.sum(dgain, axis=0, keepdims=True), first)

        dos = _stack_heads(do).astype(MXU)
        qs = _stack_heads(q_ref[...])
        kv, vv = k_ref[...], v_ref[...]
        s = _dot(qs, kv, tb=True)
        causal, dist, dm = _ret_decay(i, tq, t, lgf_ref, lgb_ref)
        da = _dot(dos, vv, tb=True)
        _acc(dv_ref, _dot(s * dm, dos, ta=True), first)
        ds = da * dm
        w = ds * s * dist
        _acc(accf_ref, jnp.sum(jnp.where(causal, w, 0.0), axis=1, keepdims=True), first)
        _acc(accb_ref, jnp.sum(jnp.where(causal, 0.0, w), axis=1, keepdims=True), first)
        dsb = ds.astype(MXU)
        dq_ref[...] = _unstack_heads(_dot(dsb, kv), tq)
        _acc(dk_ref, _dot(dsb, qs, ta=True), first)

        @pl.when(i == nblk - 1)
        def _():
            for h in range(NH):
                dlg_ref[h:h + 1, :] = jnp.full((1, 128), jnp.sum(accf_ref[h * tq:(h + 1) * tq, :]), F32)
                dlg_ref[NH + h:NH + h + 1, :] = jnp.full((1, 128), jnp.sum(accb_ref[h * tq:(h + 1) * tq, :]), F32)

    blk = pl.BlockSpec((tq, BW), lambda i: (i, 0))
    whole = pl.BlockSpec((t, BW), lambda i: (0, 0))
    vec = pl.BlockSpec((1, BW), lambda i: (0, 0))
    return pl.pallas_call(
        body, grid=(nblk,),
        in_specs=[SMEM, SMEM, blk, blk, blk, whole, whole, pl.BlockSpec((tq, BW), lambda i: (i, RG)), vec],
        out_specs=(blk, whole, whole, blk, vec, pl.BlockSpec((2 * NH, 128), lambda i: (0, 0))),
        out_shape=(_sds((t, BW), F32), _sds((t, BW), F32), _sds((t, BW), F32), _sds((t, BW), F32), _sds((1, BW), F32),
                   _sds((2 * NH, 128), F32)),
        scratch_shapes=[pltpu.VMEM((NH * tq, 1), F32), pltpu.VMEM((NH * tq, 1), F32)], name="ret_bwd",
        compiler_params=_cp())(lgf, lgb, dbr, o_ret, q, k, v, proj, g_ret)


def _pool_windows(t):
    row = lax.broadcasted_iota(jnp.int32, (t, BW), 0)
    half = lax.shift_left(jnp.ones((t, BW), jnp.int32), _lane_head((t, BW)))
    cnt = (jnp.minimum(row + half, t) - jnp.maximum(row - half, 0)).astype(F32)
    return row, half, cnt


def _pool_window_sum(v, row, half, t, transpose):
    out = jnp.zeros_like(v)
    for j in range(-POOL_HALF_MAX, POOL_HALF_MAX):
        src = row - j if transpose else row + j
        ok = (src >= 0) & (src < t) & (j >= -half) & (j < half)
        out = out + jnp.where(ok, pltpu.roll(v, (j if transpose else -j) % t, 0), 0.0)
    return out


def _pool_fwd(proj, wbd, scale):
    t = proj.shape[0]

    def body(v_ref, w_ref, s_ref, o_ref):
        v = v_ref[...]
        row, half, cnt = _pool_windows(t)
        pooled = _pool_window_sum(v, row, half, t, False) / cnt - v
        o_ref[...] = (_dot(pooled, w_ref[...]) * s_ref[...]).astype(BF16)

    return pl.pallas_call(
        body, grid=(1,),
        in_specs=[pl.BlockSpec((t, BW), lambda i: (0, PV)), pl.BlockSpec((BW, BW), lambda i: (0, 0)), pl.BlockSpec((1, BW), lambda i: (0, 0))],
        out_specs=pl.BlockSpec((t, BW), lambda i: (0, 0)), out_shape=_sds((t, BW), BF16), name="pool_fwd",
        compiler_params=_cp())(proj, wbd, scale)


def _pool_bwd(dbr, proj, wbd, scale):
    t = proj.shape[0]

    def body(d_ref, v_ref, w_ref, s_ref, dv_ref, dw_ref, ds_ref):
        v, dout = v_ref[...], d_ref[...]
        row, half, cnt = _pool_windows(t)
        pooled = _pool_window_sum(v, row, half, t, False) / cnt - v
        mixed = _dot(pooled, w_ref[...])
        ds_ref[...] = jnp.sum(dout * mixed, axis=0, keepdims=True)
        dmixed = dout * s_ref[...]
        dw_ref[...] = _dot(pooled, dmixed, ta=True)
        dpooled = _dot(dmixed, w_ref[...], tb=True)
        dv_ref[...] = _pool_window_sum(dpooled / cnt, row, half, t, True) - dpooled

    return pl.pallas_call(
        body, grid=(1,),
        in_specs=[pl.BlockSpec((t, BW), lambda i: (0, 1)), pl.BlockSpec((t, BW), lambda i: (0, PV)),
                  pl.BlockSpec((BW, BW), lambda i: (0, 0)), pl.BlockSpec((1, BW), lambda i: (0, 0))],
        out_specs=(pl.BlockSpec((t, BW), lambda i: (0, 0)), pl.BlockSpec((BW, BW), lambda i: (0, 0)), pl.BlockSpec((1, BW), lambda i: (0, 0))),
        out_shape=(_sds((t, BW), F32), _sds((BW, BW), F32), _sds((1, BW), F32)), name="pool_bwd",
        compiler_params=_cp())(dbr, proj, wbd, scale)


NA_KEYS = NA_ROWS_WIN * GRID_W


def _na_window(r, n_rows):
    rs = jnp.clip(r - NA_ROWS_WIN // 2, 0, n_rows - NA_ROWS_WIN)
    return pl.multiple_of(rs * GRID_W, GRID_W), rs - r + (NA_ROWS_WIN - 1)


def _na_fwd(q, k, v, ball):
    t = q.shape[0]
    n_rows = t // GRID_W

    def body(q_ref, k_ref, v_ref, b_ref, o_ref):
        start, a0 = _na_window(pl.program_id(0), n_rows)
        qs = _stack_heads(q_ref[...])
        s = _dot(qs, k_ref[pl.ds(start, NA_KEYS), :], tb=True) * (HD ** -0.5) + b_ref[a0]
        p = _softmax_rows(s)
        o_ref[...] = _unstack_heads(_dot(p, v_ref[pl.ds(start, NA_KEYS), :]), GRID_W).astype(BF16)

    blk = pl.BlockSpec((GRID_W, BW), lambda r: (r, 0))
    whole = pl.BlockSpec((t, BW), lambda r: (0, 0))
    return pl.pallas_call(
        body, grid=(n_rows,), in_specs=[blk, whole, whole, pl.BlockSpec(ball.shape, lambda r: (0, 0, 0))],
        out_specs=blk, out_shape=_sds((t, BW), BF16), name="na_fwd", compiler_params=_cp())(q, k, v, ball)


def _na_bwd(dbr, q, k, v, ball):
    t = q.shape[0]
    n_rows = t // GRID_W

    def body(d_ref, q_ref, k_ref, v_ref, b_ref, dq_ref, dk_ref, dv_ref, db_ref):
        r = pl.program_id(0)
        start, a0 = _na_window(r, n_rows)
        keys = pl.ds(start, NA_KEYS)

        @pl.when(r == 0)
        def _():
            dk_ref[...] = jnp.zeros_like(dk_ref)
            dv_ref[...] = jnp.zeros_like(dv_ref)
            db_ref[...] = jnp.zeros_like(db_ref)

        qs = _stack_heads(q_ref[...])
        kb, vb = k_ref[keys, :], v_ref[keys, :]
        p = _softmax_rows(_dot(qs, kb, tb=True) * (HD ** -0.5) + b_ref[a0])
        dos = _stack_heads(d_ref[...]).astype(MXU)
        dp = _dot(dos, vb, tb=True)
        dv_ref[keys, :] += _dot(p, dos, ta=True)
        ds = p * (dp - jnp.sum(dp * p, axis=-1, keepdims=True))
        db_ref[a0] += ds
        dsb = (ds * (HD ** -0.5)).astype(MXU)
        dq_ref[...] = _unstack_heads(_dot(dsb, kb), GRID_W)
        dk_ref[keys, :] += _dot(dsb, qs, ta=True)

    blk = pl.BlockSpec((GRID_W, BW), lambda r: (r, 0))
    whole = pl.BlockSpec((t, BW), lambda r: (0, 0))
    tab = pl.BlockSpec(ball.shape, lambda r: (0, 0, 0))
    return pl.pallas_call(
        body, grid=(n_rows,), in_specs=[pl.BlockSpec((GRID_W, BW), lambda r: (r, 2)), blk, whole, whole, tab],
        out_specs=(blk, whole, whole, tab),
        out_shape=(_sds((t, BW), F32), _sds((t, BW), F32), _sds((t, BW), F32), _sds(ball.shape, F32)), name="na_bwd",
        compiler_params=_cp())(dbr, q, k, v, ball)


def _rpb_expand(rpb_pad, onehot):
    def body(r_ref, e_ref, o_ref):
        o_ref[...] = jnp.dot(r_ref[...], e_ref[...], precision=HI, preferred_element_type=F32)

    return pl.pallas_call(body, out_shape=_sds((64, GRID_W * GRID_W), F32), name="rpb_expand", compiler_params=_cp())(rpb_pad, onehot)


def _rpb_reduce(dtab, onehot):
    def body(d_ref, e_ref, o_ref):
        o_ref[...] = lax.dot_general(d_ref[...], e_ref[...], (((1,), (1,)), ((), ())), precision=HI, preferred_element_type=F32)

    return pl.pallas_call(body, out_shape=_sds((64, 128), F32), name="rpb_reduce", compiler_params=_cp())(dtab, onehot)


MEM_TQ = 256


def _mem_fwd(q, mk, mv):
    t = q.shape[0]
    tq = MEM_TQ

    def body(q_ref, k_ref, v_ref, o_ref):
        qv = q_ref[...]
        head = _lane_head(qv.shape)
        out = jnp.zeros((tq, BW), F32)
        for h in range(NH):
            p = _softmax_rows(_dot(jnp.where(head == h, qv, jnp.zeros_like(qv)), k_ref[...], tb=True) * (HD ** -0.5))
            out = out + jnp.where(head == h, _dot(p, v_ref[...]), 0.0)
        o_ref[...] = out.astype(BF16)

    blk = pl.BlockSpec((tq, BW), lambda i: (i, 0))
    kv = pl.BlockSpec((N_MEM, BW), lambda i: (0, 0))
    return pl.pallas_call(body, grid=(t // tq,), in_specs=[blk, kv, kv], out_specs=blk, out_shape=_sds((t, BW), BF16),
                          name="mem_fwd", compiler_params=_cp())(q, mk, mv)


def _mem_bwd(dbr, q, mk, mv):
    t = q.shape[0]
    tq = MEM_TQ

    def body(d_ref, q_ref, k_ref, v_ref, dq_ref, dk_ref, dv_ref):
        first = pl.program_id(0) == 0
        qv, dout = q_ref[...], d_ref[...]
        head = _lane_head(qv.shape)
        dq = jnp.zeros((tq, BW), F32)
        dk = jnp.zeros((N_MEM, BW), F32)
        dv = jnp.zeros((N_MEM, BW), F32)
        for h in range(NH):
            qh = jnp.where(head == h, qv, jnp.zeros_like(qv))
            doh = jnp.where(head == h, dout, 0.0).astype(MXU)
            p = _softmax_rows(_dot(qh, k_ref[...], tb=True) * (HD ** -0.5))
            dp = _dot(doh, v_ref[...], tb=True)
            dv = dv + _dot(p, doh, ta=True)
            dsb = (p * (dp - jnp.sum(dp * p, axis=-1, keepdims=True)) * (HD ** -0.5)).astype(MXU)
            dq = dq + jnp.where(head == h, _dot(dsb, k_ref[...]), 0.0)
            dk = dk + _dot(dsb, qh, ta=True)
        dq_ref[...] = dq
        _acc(dk_ref, dk, first)
        _acc(dv_ref, dv, first)

    blk = pl.BlockSpec((tq, BW), lambda i: (i, 0))
    kv = pl.BlockSpec((N_MEM, BW), lambda i: (0, 0))
    return pl.pallas_call(
        body, grid=(t // tq,), in_specs=[pl.BlockSpec((tq, BW), lambda i: (i, 3)), blk, kv, kv], out_specs=(blk, kv, kv),
        out_shape=(_sds((t, BW), F32), _sds((N_MEM, BW), F32), _sds((N_MEM, BW), F32)), name="mem_bwd",
        compiler_params=_cp())(dbr, q, mk, mv)


def _memkv_prep(kv, g_mk):
    def body(kv_ref, g_ref, k_ref, v_ref):
        k_ref[...] = _gnorm(kv_ref[:, 0:BW], g_ref[...]).astype(BF16)
        v_ref[...] = kv_ref[:, BW:2 * BW].astype(BF16)

    return pl.pallas_call(body, out_shape=(_sds((N_MEM, BW), BF16), _sds((N_MEM, BW), BF16)), name="memkv_prep",
                          compiler_params=_cp())(kv, g_mk)


def _memkv_bwd(kv, dk, dv, g_mk):
    def body(kv_ref, dk_ref, dv_ref, g_ref, o_ref, dg_ref):
        dkk, gain = _gnorm_bwd(dk_ref[...], kv_ref[:, 0:BW], g_ref[...])
        o_ref[:, 0:BW] = dkk.astype(BF16)
        o_ref[:, BW:2 * BW] = dv_ref[...].astype(BF16)
        dg_ref[...] = jnp.sum(gain, axis=0, keepdims=True)

    return pl.pallas_call(body, out_shape=(_sds((N_MEM, 2 * BW), BF16), _sds((1, BW), F32)), name="memkv_bwd",
                          compiler_params=_cp())(kv, dk, dv, g_mk)


MERGE_TM = 256


def _merge_fwd(brs, wbt, gp):
    t = gp.shape[0]
    tm = MERGE_TM

    def body(b0, b1, b2, b3, wb_ref, gp_ref, o_ref):
        out = jnp.zeros((tm, D), F32)
        for n, b_ref in enumerate((b0, b1, b2, b3)):
            up = _dot(b_ref[...], wb_ref[n], tb=True)
            out = out + _sigmoid(gp_ref[:, n * D:(n + 1) * D]) * up
        o_ref[...] = out.astype(BF16)

    blk = pl.BlockSpec((tm, BW), lambda i: (i, 0))
    return pl.pallas_call(
        body, grid=(t // tm,),
        in_specs=[blk, blk, blk, blk, pl.BlockSpec((NH, D, BW), lambda i: (0, 0, 0)), pl.BlockSpec((tm, NH * D), lambda i: (i, 0))],
        out_specs=pl.BlockSpec((tm, D), lambda i: (i, 0)), out_shape=_sds((t, D), BF16), name="merge_fwd",
        compiler_params=_cp())(*brs, wbt, gp)


def _merge_bwd(dmerged, brs, wbt, gp):
    t = gp.shape[0]
    tm = MERGE_TM

    def body(d_ref, b0, b1, b2, b3, wb_ref, gp_ref, dgp_ref, dup_ref):
        dm = d_ref[...]
        for n, b_ref in enumerate((b0, b1, b2, b3)):
            up = _dot(b_ref[...], wb_ref[n], tb=True)
            g = _sigmoid(gp_ref[:, n * D:(n + 1) * D])
            dgp_ref[:, n * D:(n + 1) * D] = (dm * up * (g * (1.0 - g))).astype(BF16)
            dup_ref[:, n * D:(n + 1) * D] = (dm * g).astype(BF16)

    row = pl.BlockSpec((tm, D), lambda i: (i, 0))
    blk = pl.BlockSpec((tm, BW), lambda i: (i, 0))
    wide = pl.BlockSpec((tm, NH * D), lambda i: (i, 0))
    return pl.pallas_call(
        body, grid=(t // tm,), in_specs=[row, blk, blk, blk, blk, pl.BlockSpec((NH, D, BW), lambda i: (0, 0, 0)), wide],
        out_specs=(wide, wide), out_shape=(_sds((t, NH * D), BF16), _sds((t, NH * D), BF16)), name="merge_bwd",
        compiler_params=_cp())(dmerged, *brs, wbt, gp)


def _dbranch(dup, wbt):
    t = dup.shape[0]
    tm = 512

    def body(d_ref, w_ref, o_ref):
        o_ref[...] = _dot(d_ref[...], w_ref[...])

    return pl.pallas_call(
        body, grid=(t // tm, NH), in_specs=[pl.BlockSpec((tm, D), lambda i, n: (i, n)), pl.BlockSpec((None, D, BW), lambda i, n: (n, 0, 0))],
        out_specs=pl.BlockSpec((tm, BW), lambda i, n: (i, n)), out_shape=_sds((t, NH * BW), F32), name="dbranch",
        compiler_params=_cp())(dup, wbt)


def _dwbranch(brs, dup):
    t = dup.shape[0]

    def body(b0, b1, b2, b3, d_ref, o_ref):
        for n, b_ref in enumerate((b0, b1, b2, b3)):
            o_ref[n] = _dot(d_ref[:, n * D:(n + 1) * D], b_ref[...], ta=True).astype(BF16)

    return pl.pallas_call(body, out_shape=_sds((NH, D, BW), BF16), name="dwbranch", compiler_params=_cp())(*brs, dup)


def _swiglu_fwd(ag):
    t = ag.shape[0]
    tm = 256

    def body(ag_ref, o_ref):
        a, g = ag_ref[:, 0:FF], ag_ref[:, FF:2 * FF]
        o_ref[...] = (a * _sigmoid(a) * g).astype(BF16)

    return pl.pallas_call(body, grid=(t // tm,), in_specs=[pl.BlockSpec((tm, 2 * FF), lambda i: (i, 0))],
                          out_specs=pl.BlockSpec((tm, FF), lambda i: (i, 0)), out_shape=_sds((t, FF), BF16), name="swiglu_fwd",
                          compiler_params=_cp())(ag)


def _swiglu_bwd(ag, dy):
    t = ag.shape[0]
    tm = 256

    def body(ag_ref, dy_ref, o_ref):
        a, g, d = ag_ref[:, 0:FF], ag_ref[:, FF:2 * FF], dy_ref[...]
        s = _sigmoid(a)
        o_ref[:, 0:FF] = (d * g * (s * (1.0 + a * (1.0 - s)))).astype(BF16)
        o_ref[:, FF:2 * FF] = (d * (a * s)).astype(BF16)

    return pl.pallas_call(
        body, grid=(t // tm,), in_specs=[pl.BlockSpec((tm, 2 * FF), lambda i: (i, 0)), pl.BlockSpec((tm, FF), lambda i: (i, 0))],
        out_specs=pl.BlockSpec((tm, 2 * FF), lambda i: (i, 0)), out_shape=_sds((t, 2 * FF), BF16), name="swiglu_bwd",
        compiler_params=_cp())(ag, dy)


def _loss_head(y, target):
    t, d = y.shape
    tm = 256

    def body(y_ref, t_ref, dy_ref, l_ref):
        e = y_ref[...] - t_ref[...]
        dy_ref[...] = e * (1.0 / d)
        _acc(l_ref, jnp.full((8, 128), 0.5 * jnp.sum(jnp.sum(e * e, axis=-1, keepdims=True) * (1.0 / d)), F32), pl.program_id(0) == 0)

    row = pl.BlockSpec((tm, d), lambda i: (i, 0))
    return pl.pallas_call(body, grid=(t // tm,), in_specs=[row, row], out_specs=(row, pl.BlockSpec((8, 128), lambda i: (0, 0))),
                          out_shape=(_sds((t, d), F32), _sds((8, 128), F32)), name="loss_head", compiler_params=_cp())(y, target)


def _sum_slots(x, name):
    k, r, c = x.shape
    tr = _tile(r, 512) if r % 128 == 0 else r

    def body(x_ref, o_ref):
        acc = x_ref[0].astype(F32)
        for s in range(1, k):
            acc = acc + x_ref[s].astype(F32)
        o_ref[...] = acc

    return pl.pallas_call(body, grid=(r // tr,), in_specs=[pl.BlockSpec((k, tr, c), lambda i: (0, i, 0))],
                          out_specs=pl.BlockSpec((tr, c), lambda i: (i, 0)), out_shape=_sds((r, c), F32), name=name,
                          compiler_params=_cp())(x)


def _pair_sum(buf, recv, cidx):
    k, _, r, c = buf.shape
    tr = r

    def body(c_ref, b_ref, r_ref, o_ref):
        o_ref[...] = (b_ref[...].astype(F32) + r_ref[...].astype(F32)).astype(BF16)

    return pl.pallas_call(
        body,
        grid_spec=pltpu.PrefetchScalarGridSpec(
            num_scalar_prefetch=1, grid=(k, r // tr),
            in_specs=[pl.BlockSpec((None, None, tr, c), lambda s, i, cref: (s, cref[0], i, 0)),
                      pl.BlockSpec((None, tr, c), lambda s, i, cref: (s, i, 0))],
            out_specs=pl.BlockSpec((None, tr, c), lambda s, i, cref: (s, i, 0))),
        out_shape=_sds((k, r, c), BF16), name="rs_pair_sum", compiler_params=_cp())(cidx, buf, recv)


def _adamw(w, g, m, v, name):
    r, c = w.shape
    tr = r
    if r > 1024:
        tr = next(cand for cand in (512, 256, 128, 64, 32, 16, 8) if r % cand == 0)

    def body(w_ref, g_ref, m_ref, v_ref, d_ref, nm_ref, nv_ref):
        gv = g_ref[...]
        mn = ADAM_B1 * m_ref[...] + (1.0 - ADAM_B1) * gv
        vn = ADAM_B2 * v_ref[...] + (1.0 - ADAM_B2) * (gv * gv)
        m_hat = mn / (1.0 - ADAM_B1 ** ADAM_STEP)
        v_hat = vn / (1.0 - ADAM_B2 ** ADAM_STEP)
        d_ref[...] = -ADAM_LR * (m_hat / (jnp.sqrt(v_hat) + ADAM_EPS) + ADAM_WD * w_ref[...])
        nm_ref[...] = mn
        nv_ref[...] = vn

    blk = pl.BlockSpec((tr, c), lambda i: (i, 0))
    return pl.pallas_call(body, grid=(r // tr,), in_specs=[blk] * 4, out_specs=(blk,) * 3,
                          out_shape=tuple(_sds((r, c), F32) for _ in range(3)), name=name, compiler_params=_cp())(w, g, m, v)


def _all_gather(shards, name):
    n = len(shards)

    def body(*refs):
        x_refs, out_refs = refs[:n], refs[n:2 * n]
        send_sems, recv_sems, local_sems = refs[2 * n:]
        x, y, cc = lax.axis_index("x"), lax.axis_index("y"), lax.axis_index("c")
        me, sibling = (x, y, cc), (x, y, 1 - cc)
        chips = [(1 - x, y), (x, 1 - y), (1 - x, 1 - y)]

        def copy(i, k, block, to, own=False):
            px, py, pc = block
            slot = out_refs[i].at[4 * px + 2 * py + pc]
            return pltpu.make_async_remote_copy(
                src_ref=x_refs[i] if own else slot, dst_ref=slot, send_sem=send_sems.at[7 * i + k],
                recv_sem=recv_sems.at[7 * i + k], device_id=to, device_id_type=MESH)

        mine = [pltpu.make_async_copy(x_refs[i], out_refs[i].at[4 * x + 2 * y + cc], local_sems.at[i]) for i in range(n)]
        for cp in mine:
            cp.start()
        first = []
        for j, chip in enumerate(chips):
            first += [copy(i, 1 + j, me, (*chip, cc), own=True) for i in range(n)]
        first += [copy(i, 0, me, sibling, own=True) for i in range(n)]
        for cp in first:
            cp.start()
        passed = []
        for j, chip in enumerate(chips):
            for i in range(n):
                copy(i, 1 + j, (*chip, cc), me).wait_recv()
                cp = copy(i, 4 + j, (*chip, cc), sibling)
                cp.start()
                passed.append(cp)
        for i in range(n):
            copy(i, 0, sibling, me).wait_recv()
        for j, chip in enumerate(chips):
            for i in range(n):
                copy(i, 4 + j, (*chip, 1 - cc), me).wait_recv()
        for cp in first + passed:
            cp.wait_send()
        for cp in mine:
            cp.wait()

    return pl.pallas_call(
        body, out_shape=tuple(_sds((N_DEV,) + s.shape, s.dtype) for s in shards), in_specs=[ANY] * n, out_specs=(ANY,) * n,
        scratch_shapes=[pltpu.SemaphoreType.DMA((7 * n,)), pltpu.SemaphoreType.DMA((7 * n,)), pltpu.SemaphoreType.DMA((n,))],
        name=name)(*shards)


def _rs_core_swap(bufs, name):
    n = len(bufs)

    def body(*refs):
        b_refs, recv_refs = refs[:n], refs[n:2 * n]
        send_sems, recv_sems = refs[2 * n:]
        x, y, cc = lax.axis_index("x"), lax.axis_index("y"), lax.axis_index("c")
        copies = [pltpu.make_async_remote_copy(
            src_ref=b_refs[i].at[s, 1 - cc], dst_ref=recv_refs[i].at[s], send_sem=send_sems.at[4 * i + s],
            recv_sem=recv_sems.at[4 * i + s], device_id=(x, y, 1 - cc), device_id_type=MESH) for i in range(n) for s in range(4)]
        for cp in copies:
            cp.start()
        for cp in copies:
            cp.wait()

    return pl.pallas_call(
        body, out_shape=tuple(_sds((4,) + b.shape[2:], b.dtype) for b in bufs), in_specs=[ANY] * n, out_specs=(ANY,) * n,
        scratch_shapes=[pltpu.SemaphoreType.DMA((4 * n,)), pltpu.SemaphoreType.DMA((4 * n,))], name=name)(*bufs)


HBM = pl.BlockSpec(memory_space=pltpu.HBM)
SEMS = pl.BlockSpec(memory_space=pltpu.SEMAPHORE)
EFFECT = pltpu.SideEffectType.DATAFLOW_SIDE_EFFECTING


def _hbm(a):
    return pltpu.HBM(a.shape, a.dtype)


def _other_chips(x, y):
    return [(1 - x, y), (x, 1 - y), (1 - x, 1 - y)]


def _ici_start(srcs, lands, by_chip, name):
    n = len(srcs)

    def body(*refs):
        s_refs, land_refs = refs[:n], refs[n:2 * n]
        send_sems, recv_sems = refs[2 * n], refs[2 * n + 1]
        token = refs[-1]
        x, y, cc = lax.axis_index("x"), lax.axis_index("y"), lax.axis_index("c")
        mine = 2 * x + y if by_chip else 4 * x + 2 * y + cc
        for px, py in _other_chips(x, y):
            for i in range(n):
                pltpu.make_async_remote_copy(
                    src_ref=s_refs[i].at[2 * px + py] if by_chip else s_refs[i], dst_ref=land_refs[i].at[mine],
                    send_sem=send_sems.at[i], recv_sem=recv_sems.at[i], device_id=(px, py, cc), device_id_type=MESH).start()
        token[...] = jnp.zeros_like(token)

    out = pl.pallas_call(
        body, name=name,
        out_shape=(pltpu.SemaphoreType.DMA((n,)), pltpu.SemaphoreType.DMA((n,)), *[_hbm(s) for s in srcs], *[_hbm(l) for l in lands],
                   _sds((8, 128), F32)),
        in_specs=[HBM] * (2 * n), out_specs=(SEMS, SEMS, *[HBM] * (2 * n), pl.BlockSpec(memory_space=pltpu.VMEM)),
        input_output_aliases={i: 2 + i for i in range(2 * n)}, compiler_params=pltpu.CompilerParams(has_side_effects=EFFECT),
    )(*[pltpu.with_memory_space_constraint(s, pltpu.HBM) for s in srcs],
      *[pltpu.with_memory_space_constraint(l, pltpu.HBM) for l in lands])
    return out[0], out[1], out[2:2 + n], out[2 + n:2 + 2 * n], out[-1]


def _ici_wait(started, after, name):
    send_sems, recv_sems, srcs, lands, _ = started
    n = len(srcs)

    def body(*refs):
        land_refs = refs[n:2 * n]
        send_sems, recv_sems = refs[2 * n], refs[2 * n + 1]
        x, y, cc = lax.axis_index("x"), lax.axis_index("y"), lax.axis_index("c")
        for i in range(n):
            three = land_refs[i].at[pl.ds(0, 3)]
            cp = pltpu.make_async_remote_copy(src_ref=three, dst_ref=three, send_sem=send_sems.at[i], recv_sem=recv_sems.at[i],
                                              device_id=(x, y, cc), device_id_type=MESH)
            cp.wait_send()
            cp.wait_recv()

    return pl.pallas_call(
        body, name=name, out_shape=tuple(_hbm(l) for l in lands), in_specs=[HBM] * (2 * n) + [SEMS, SEMS, ANY],
        out_specs=tuple([HBM] * n), input_output_aliases={n + i: i for i in range(n)},
        compiler_params=pltpu.CompilerParams(has_side_effects=EFFECT))(*srcs, *lands, send_sems, recv_sems, after)


def _gather_d2d(blocks, lands, name):
    n = len(blocks)

    def body(*refs):
        x_refs, land_refs = refs[:n], refs[2 * n:3 * n]
        send_sems, recv_sems, local_sems = refs[3 * n:]
        x, y, cc = lax.axis_index("x"), lax.axis_index("y"), lax.axis_index("c")
        sibling = (x, y, 1 - cc)
        mine = [pltpu.make_async_copy(x_refs[i], land_refs[i].at[4 * x + 2 * y + cc], local_sems.at[i]) for i in range(n)]
        for cp in mine:
            cp.start()
        copies = []
        for i in range(n):
            slot = land_refs[i].at[4 * x + 2 * y + cc]
            copies.append(pltpu.make_async_remote_copy(src_ref=x_refs[i], dst_ref=slot, send_sem=send_sems.at[4 * i],
                                                       recv_sem=recv_sems.at[4 * i], device_id=sibling, device_id_type=MESH))
            for j, (px, py) in enumerate(_other_chips(x, y)):
                slot = land_refs[i].at[4 * px + 2 * py + cc]
                copies.append(pltpu.make_async_remote_copy(src_ref=slot, dst_ref=slot, send_sem=send_sems.at[4 * i + 1 + j],
                                                           recv_sem=recv_sems.at[4 * i + 1 + j], device_id=sibling, device_id_type=MESH))
        for cp in copies:
            cp.start()
        for i in range(n):
            slot = land_refs[i].at[4 * x + 2 * y + (1 - cc)]
            pltpu.make_async_remote_copy(src_ref=slot, dst_ref=slot, send_sem=send_sems.at[4 * i], recv_sem=recv_sems.at[4 * i],
                                         device_id=sibling, device_id_type=MESH).wait_recv()
            for j, (px, py) in enumerate(_other_chips(x, y)):
                slot = land_refs[i].at[4 * px + 2 * py + (1 - cc)]
                pltpu.make_async_remote_copy(src_ref=slot, dst_ref=slot, send_sem=send_sems.at[4 * i + 1 + j],
                                             recv_sem=recv_sems.at[4 * i + 1 + j], device_id=sibling, device_id_type=MESH).wait_recv()
        for cp in copies:
            cp.wait_send()
        for cp in mine:
            cp.wait()

    return pl.pallas_call(
        body, out_shape=tuple(_sds(l.shape, l.dtype) for l in lands), in_specs=[ANY] * (2 * n), out_specs=(ANY,) * n,
        input_output_aliases={n + i: i for i in range(n)},
        scratch_shapes=[pltpu.SemaphoreType.DMA((4 * n,)), pltpu.SemaphoreType.DMA((4 * n,)), pltpu.SemaphoreType.DMA((n,))],
        name=name)(*blocks, *lands)


def _sum_own(part, recv, chip, name):
    k, r, c = part.shape

    def body(c_ref, p_ref, r_ref, o_ref):
        acc = jnp.zeros((r, c), F32)
        for s in range(k):
            acc = acc + jnp.where(c_ref[0] == s, p_ref[s], r_ref[s]).astype(F32)
        o_ref[...] = acc

    blk = pl.BlockSpec((k, r, c), lambda i, cref: (0, 0, 0))
    return pl.pallas_call(
        body, grid_spec=pltpu.PrefetchScalarGridSpec(num_scalar_prefetch=1, grid=(1,), in_specs=[blk, blk],
                                                     out_specs=pl.BlockSpec((r, c), lambda i, cref: (0, 0))),
        out_shape=_sds((r, c), F32), name=name, compiler_params=_cp())(chip, part, recv)


BIG = (("w_in", True), ("w_gate", True), ("w_mem_kv", False), ("w_branch", True), ("w_out", False), ("w_ffn_in", True),
       ("w_ffn_out", False))

SMALL = ("norm_mix_g", "norm_mem_g", "ret_decay_fwd", "ret_decay_bwd", "ret_norm_g", "pool_w", "pool_scale", "na_q_norm_g",
         "na_k_norm_g", "na_rpb", "mem_q_norm_g", "mem_k_norm_g", "norm_ffn_g")
WEIGHTS = ("norm_mix_g", "norm_mem_g", "w_in", "w_gate", "ret_decay_fwd", "ret_decay_bwd", "ret_norm_g", "pool_w", "pool_scale",
           "na_q_norm_g", "na_k_norm_g", "na_rpb", "mem_q_norm_g", "mem_k_norm_g", "w_mem_kv", "w_branch", "w_out", "norm_ffn_g",
           "w_ffn_in", "w_ffn_out")


def _to_exchange(name, transposed, shard):
    if name == "w_branch":
        return jnp.swapaxes(shard, 1, 2).reshape(NH * (D // N_DEV), BW)
    return shard.T if transposed else shard


def _from_exchange(name, transposed, block):
    if name == "w_branch":
        return jnp.swapaxes(block.reshape(NH, D // N_DEV, BW), 1, 2)
    return block.T if transposed else block


def _whole_from_gathered(name, g):
    if name == "w_branch":
        return jnp.swapaxes(g.reshape(N_DEV, NH, D // N_DEV, BW), 0, 1).reshape(NH, D, BW)
    return g.reshape(N_DEV * g.shape[1], g.shape[2])


def _by_destination(name, g):
    if name == "w_branch":
        g = jnp.swapaxes(g.reshape(NH, N_DEV, D // N_DEV, BW), 0, 1).reshape(N_DEV * NH * (D // N_DEV), BW)
    return g.reshape(4, 2, g.shape[0] // N_DEV, g.shape[1])


SMALL_PAD = 1024


def _pack_small(vals, loss=None):
    parts = [vals[n] for n in SMALL] + [jnp.zeros((1,), F32) if loss is None else loss.reshape(1)]
    rows = []
    for p in parts:
        flat = p.reshape(-1)
        rows.append(jnp.pad(flat, (0, -flat.shape[0] % SMALL_PAD)).reshape(-1, 128))
    return jnp.concatenate(rows, axis=0)


def _unpack_small(packed, like):
    out, off = {}, 0
    for n in SMALL:
        sz = int(np.prod(like[n].shape))
        nrow = -(-sz // SMALL_PAD) * (SMALL_PAD // 128)
        out[n] = packed[off:off + nrow].reshape(-1)[:sz].reshape(like[n].shape)
        off += nrow
    return out, packed[off, 0]


def _na_constants():
    c = np.arange(GRID_W)
    win = np.clip(c - NA_COLS_WIN // 2, 0, GRID_W - NA_COLS_WIN)
    kc = np.arange(GRID_W)
    inside = (kc[None, :] >= win[:, None]) & (kc[None, :] < win[:, None] + NA_COLS_WIN)
    off = kc[None, :] - c[:, None] + NA_COLS_WIN - 1
    onehot = np.zeros((128, GRID_W, GRID_W), np.float32)
    for b in range(2 * NA_COLS_WIN - 1):
        onehot[b] = (off == b) & inside
    maskadd = np.where(inside, 0.0, NEG).astype(np.float32)
    return onehot.reshape(128, GRID_W * GRID_W), maskadd


def _na_bias_table(tab, maskadd):
    n_off = 2 * NA_ROWS_WIN - 1
    t4 = tab[:NH * n_off].reshape(NH, n_off, GRID_W, GRID_W) + maskadd[None, None]
    ball = jnp.stack([t4[:, a0:a0 + NA_ROWS_WIN] for a0 in range(NA_ROWS_WIN)], axis=1)
    return ball.transpose(1, 0, 3, 2, 4).reshape(NA_ROWS_WIN, NH * GRID_W, NA_KEYS)


def _rotary_tables(t):
    half = HD // 2
    inv = ROPE_THETA ** (-jnp.arange(half, dtype=F32) / half)
    ang = jnp.arange(t, dtype=F32)[:, None] * inv[None, :]
    cos, sin = jnp.cos(ang), jnp.sin(ang)
    return jnp.tile(jnp.concatenate([cos, cos], axis=-1), (1, NH)), jnp.tile(jnp.concatenate([-sin, sin], axis=-1), (1, NH))


def _block_diag(pw):
    out = jnp.zeros((BW, BW), pw.dtype)
    for g in range(NH):
        out = lax.dynamic_update_slice(out, pw[g], (g * HD, g * HD))
    return out


def _tile4(g):
    return jnp.tile(g.reshape(1, HD), (1, NH))


def _layer_fwd(x, mem, sw, lw, consts):
    cos2, sin2, onehot, maskadd = consts
    h = _rmsnorm_fwd(x, sw["norm_mix_g"].reshape(1, D), "norm_mix_fwd")
    proj = _mm(h, lw["w_in"], tb=True, name="mm_in")
    gp = _mm(h, lw["w_gate"], tb=True, name="mm_gate")
    g_naq, g_nak, g_mq = _tile4(sw["na_q_norm_g"]), _tile4(sw["na_k_norm_g"]), _tile4(sw["mem_q_norm_g"])
    rq, rk, rv, nq, nk, nv, mq = _prep_fwd(proj, cos2, sin2, g_naq, g_nak, g_mq)

    lgf, lgb = jax.nn.log_sigmoid(sw["ret_decay_fwd"]), jax.nn.log_sigmoid(sw["ret_decay_bwd"])
    g_ret = sw["ret_norm_g"].reshape(1, BW)
    o_ret, ret = _ret_fwd(rq, rk, rv, proj, lgf, lgb, g_ret)

    wbd = _block_diag(sw["pool_w"]).astype(BF16)
    p_scale = sw["pool_scale"].reshape(1, BW)
    pool = _pool_fwd(proj, wbd, p_scale)

    rpb_pad = jnp.pad(sw["na_rpb"].reshape(NH * 15, 31), ((0, 4), (0, 97)))
    ball = _na_bias_table(_rpb_expand(rpb_pad, onehot), maskadd)
    na = _na_fwd(nq, nk, nv, ball)

    memn = _rmsnorm_fwd(mem, sw["norm_mem_g"].reshape(1, D), "norm_mem_fwd")
    kv = _mm(memn, lw["w_mem_kv"], name="mm_memkv")
    g_mk = _tile4(sw["mem_k_norm_g"])
    mk, mv = _memkv_prep(kv, g_mk)
    mo = _mem_fwd(mq, mk, mv)

    br = (ret, pool, na, mo)
    merged = _merge_fwd(br, lw["w_branch"], gp)
    x1 = _mm(merged, lw["w_out"], add=x, name="mm_out")
    h2 = _rmsnorm_fwd(x1, sw["norm_ffn_g"].reshape(1, D), "norm_ffn_fwd")
    ag = _mm(h2, lw["w_ffn_in"], tb=True, name="mm_ffn_in")
    yff = _swiglu_fwd(ag)
    x2 = _mm(yff, lw["w_ffn_out"], add=x1, name="mm_ffn_out")
    saved = dict(x=x, h=h, proj=proj, gp=gp, rq=rq, rk=rk, rv=rv, nq=nq, nk=nk, nv=nv, mq=mq, o_ret=o_ret, ball=ball, memn=memn,
                 kv=kv, mk=mk, mv=mv, br=br, merged=merged, x1=x1, h2=h2, ag=ag, yff=yff, lgf=lgf, lgb=lgb, wbd=wbd)
    return x2, saved


def _layer_bwd(dx2, mem, sw, lw, sv, consts, dep=None):
    cos2, sin2, onehot, maskadd = consts
    gb, gs = {}, {}
    dy = _mm(dx2, lw["w_ffn_out"], tb=True, dep=dep, name="mm_ffn_out_dx")
    gb["w_ffn_out"] = _mm(sv["yff"], dx2, ta=True, out_dtype=BF16, name="mm_ffn_out_dw")
    dag = _swiglu_bwd(sv["ag"], dy)
    dh2 = _mm(dag, lw["w_ffn_in"], name="mm_ffn_in_dx")
    gb["w_ffn_in"] = _mm(dag, sv["h2"], ta=True, out_dtype=BF16, name="mm_ffn_in_dw")
    dx1, dg = _rmsnorm_bwd(dh2, sv["x1"], sw["norm_ffn_g"].reshape(1, D), dx2, "norm_ffn_bwd")
    gs["norm_ffn_g"] = dg.reshape(D)

    dmerged = _mm(dx1, lw["w_out"], tb=True, name="mm_out_dx")
    gb["w_out"] = _mm(sv["merged"], dx1, ta=True, out_dtype=BF16, name="mm_out_dw")
    dgp, dup = _merge_bwd(dmerged, sv["br"], lw["w_branch"], sv["gp"])
    dbr = _dbranch(dup, lw["w_branch"])
    gb["w_branch"] = _dwbranch(sv["br"], dup)

    g_ret = sw["ret_norm_g"].reshape(1, BW)
    d_rq, d_rk, d_rv, d_rg, dg_ret, dlg = _ret_bwd(dbr, sv["o_ret"], sv["rq"], sv["rk"], sv["rv"], sv["proj"], sv["lgf"], sv["lgb"], g_ret)
    gs["ret_norm_g"] = dg_ret.reshape(BW)
    _, vjp_f = jax.vjp(jax.nn.log_sigmoid, sw["ret_decay_fwd"])
    _, vjp_b = jax.vjp(jax.nn.log_sigmoid, sw["ret_decay_bwd"])
    gs["ret_decay_fwd"] = vjp_f(dlg[0:NH, 0])[0]
    gs["ret_decay_bwd"] = vjp_b(dlg[NH:2 * NH, 0])[0]

    p_scale = sw["pool_scale"].reshape(1, BW)
    d_pv, dwbd, dscale = _pool_bwd(dbr, sv["proj"], sv["wbd"], p_scale)
    gs["pool_w"] = jnp.stack([dwbd[g * HD:(g + 1) * HD, g * HD:(g + 1) * HD] for g in range(NH)])
    gs["pool_scale"] = dscale.reshape(BW)

    d_nq, d_nk, d_nv, dball = _na_bwd(dbr, sv["nq"], sv["nk"], sv["nv"], sv["ball"])
    _, vjp_tab = jax.vjp(lambda tab: _na_bias_table(tab, maskadd), jnp.zeros((64, GRID_W * GRID_W), F32))
    drpb = _rpb_reduce(vjp_tab(dball)[0], onehot)
    gs["na_rpb"] = drpb[:NH * 15, :31].reshape(NH, 15, 31)

    d_mq, d_mk, d_mv = _mem_bwd(dbr, sv["mq"], sv["mk"], sv["mv"])
    g_mk = _tile4(sw["mem_k_norm_g"])
    dkv, dg_mk = _memkv_bwd(sv["kv"], d_mk, d_mv, g_mk)
    gs["mem_k_norm_g"] = dg_mk.reshape(NH, HD).sum(0)
    gb["w_mem_kv"] = _mm(sv["memn"], dkv, ta=True, out_dtype=BF16, name="mm_memkv_dw")
    dmemn = _mm(dkv, lw["w_mem_kv"], tb=True, name="mm_memkv_dx")
    _, dg_mem = _rmsnorm_bwd(dmemn, mem, sw["norm_mem_g"].reshape(1, D), jnp.zeros_like(mem), "norm_mem_bwd")
    gs["norm_mem_g"] = dg_mem.reshape(D)

    g_naq, g_nak, g_mq = _tile4(sw["na_q_norm_g"]), _tile4(sw["na_k_norm_g"]), _tile4(sw["mem_q_norm_g"])
    dproj, dg_naq, dg_nak, dg_mq = _prep_bwd(sv["proj"], cos2, sin2, g_naq, g_nak, g_mq, d_rq, d_rk, d_rv, d_rg, d_pv, d_nq, d_nk,
                                             d_nv, d_mq)
    gs["na_q_norm_g"] = dg_naq.reshape(NH, HD).sum(0)
    gs["na_k_norm_g"] = dg_nak.reshape(NH, HD).sum(0)
    gs["mem_q_norm_g"] = dg_mq.reshape(NH, HD).sum(0)

    dh = _mm(dproj, lw["w_in"], name="mm_in_dx")
    dh = _mm(dgp, lw["w_gate"], add=dh, name="mm_gate_dx")
    gb["w_in"] = _mm(dproj, sv["h"], ta=True, out_dtype=BF16, name="mm_in_dw")
    gb["w_gate"] = _mm(dgp, sv["h"], ta=True, out_dtype=BF16, name="mm_gate_dw")
    dx, dg = _rmsnorm_bwd(dh, sv["x"], sw["norm_mix_g"].reshape(1, D), dx1, "norm_mix_bwd")
    gs["norm_mix_g"] = dg.reshape(D)
    return dx, gb, gs


def _local_step(x, mem, target, small, get_layer, on_grads):
    t = x.shape[0]
    cos2, sin2 = _rotary_tables(t)
    onehot, maskadd = _na_constants()
    consts = (cos2, sin2, jnp.asarray(onehot), jnp.asarray(maskadd))
    saved, weights, cur = [], [], x
    for l in range(DEPTH):
        sw = {n: small[n][l] for n in SMALL}
        weights.append(get_layer(l, cur))
        cur, sv = _layer_fwd(cur, mem, sw, weights[l], consts)
        saved.append(sv)
    dy, loss_tile = _loss_head(cur, target)
    small_g = {n: [None] * DEPTH for n in SMALL}
    dep = None
    for l in reversed(range(DEPTH)):
        sw = {n: small[n][l] for n in SMALL}
        dy, gb, gs = _layer_bwd(dy, mem, sw, weights[l], saved[l], consts, dep)
        dep = on_grads(l, gb, dy)
        for n in SMALL:
            small_g[n][l] = gs[n]
    return loss_tile[0, 0], dy, {n: jnp.stack(v) for n, v in small_g.items()}


def _flat2d(a):
    return a.reshape(-1, a.shape[-1])


def kernel(x, mem, norm_mix_g, norm_mem_g, w_in, w_gate, ret_decay_fwd, ret_decay_bwd, ret_norm_g, pool_w, pool_scale, na_q_norm_g, na_k_norm_g, na_rpb, mem_q_norm_g, mem_k_norm_g, w_mem_kv, w_branch, w_out, norm_ffn_g, w_ffn_in, w_ffn_out, loss_target, m_norm_mix_g, m_norm_mem_g, m_w_in, m_w_gate, m_ret_decay_fwd, m_ret_decay_bwd, m_ret_norm_g, m_pool_w, m_pool_scale, m_na_q_norm_g, m_na_k_norm_g, m_na_rpb, m_mem_q_norm_g, m_mem_k_norm_g, m_w_mem_kv, m_w_branch, m_w_out, m_norm_ffn_g, m_w_ffn_in, m_w_ffn_out, v_norm_mix_g, v_norm_mem_g, v_w_in, v_w_gate, v_ret_decay_fwd, v_ret_decay_bwd, v_ret_norm_g, v_pool_w, v_pool_scale, v_na_q_norm_g, v_na_k_norm_g, v_na_rpb, v_mem_q_norm_g, v_mem_k_norm_g, v_w_mem_kv, v_w_branch, v_w_out, v_norm_ffn_g, v_w_ffn_in, v_w_ffn_out):
    w = dict(norm_mix_g=norm_mix_g, norm_mem_g=norm_mem_g, w_in=w_in, w_gate=w_gate, ret_decay_fwd=ret_decay_fwd,
             ret_decay_bwd=ret_decay_bwd, ret_norm_g=ret_norm_g, pool_w=pool_w, pool_scale=pool_scale, na_q_norm_g=na_q_norm_g,
             na_k_norm_g=na_k_norm_g, na_rpb=na_rpb, mem_q_norm_g=mem_q_norm_g, mem_k_norm_g=mem_k_norm_g, w_mem_kv=w_mem_kv,
             w_branch=w_branch, w_out=w_out, norm_ffn_g=norm_ffn_g, w_ffn_in=w_ffn_in, w_ffn_out=w_ffn_out)
    m = dict(norm_mix_g=m_norm_mix_g, norm_mem_g=m_norm_mem_g, w_in=m_w_in, w_gate=m_w_gate, ret_decay_fwd=m_ret_decay_fwd,
             ret_decay_bwd=m_ret_decay_bwd, ret_norm_g=m_ret_norm_g, pool_w=m_pool_w, pool_scale=m_pool_scale, na_q_norm_g=m_na_q_norm_g,
             na_k_norm_g=m_na_k_norm_g, na_rpb=m_na_rpb, mem_q_norm_g=m_mem_q_norm_g, mem_k_norm_g=m_mem_k_norm_g, w_mem_kv=m_w_mem_kv,
             w_branch=m_w_branch, w_out=m_w_out, norm_ffn_g=m_norm_ffn_g, w_ffn_in=m_w_ffn_in, w_ffn_out=m_w_ffn_out)
    v = dict(norm_mix_g=v_norm_mix_g, norm_mem_g=v_norm_mem_g, w_in=v_w_in, w_gate=v_w_gate, ret_decay_fwd=v_ret_decay_fwd,
             ret_decay_bwd=v_ret_decay_bwd, ret_norm_g=v_ret_norm_g, pool_w=v_pool_w, pool_scale=v_pool_scale, na_q_norm_g=v_na_q_norm_g,
             na_k_norm_g=v_na_k_norm_g, na_rpb=v_na_rpb, mem_q_norm_g=v_mem_q_norm_g, mem_k_norm_g=v_mem_k_norm_g, w_mem_kv=v_w_mem_kv,
             w_branch=v_w_branch, w_out=v_w_out, norm_ffn_g=v_norm_ffn_g, w_ffn_in=v_w_ffn_in, w_ffn_out=v_w_ffn_out)
    assert x.shape == (1, 2048, D) and mem.shape == (1, N_MEM, D) and w_in.shape == (DEPTH, D, 9 * BW // N_DEV)

    started = []
    for l in range(DEPTH):
        blocks = [_to_exchange(name, tr, w[name][l]).astype(BF16) for name, tr in BIG]
        lands = [lax.empty((N_DEV,) + b.shape, BF16) for b in blocks]
        started.append(_ici_start(blocks, lands, False, "gather_ici_start_%d" % l))
    all_started = started[0][4] + started[1][4] + started[2][4] + started[3][4]

    def get_layer(l, after):
        lands = _ici_wait(started[l], all_started if l == 0 else after, "gather_ici_wait_%d" % l)
        whole = _gather_d2d(started[l][2], lands, "gather_d2d")
        return {name: _whole_from_gathered(name, g) for (name, _), g in zip(BIG, whole)}

    cidx = lax.axis_index("c").astype(jnp.int32).reshape(1)
    chip = (2 * lax.axis_index("x") + lax.axis_index("y")).astype(jnp.int32).reshape(1)
    in_flight, g_layers = [], [None] * DEPTH

    def finish(l, st, after):
        recv = _ici_wait(st, after, "rs_ici_wait_%d" % l)
        g_layers[l] = {name: _from_exchange(name, tr, _sum_own(p, r, chip, "rs_chip_sum"))
                       for (name, tr), p, r in zip(BIG, st[2], recv)}

    def on_grads(l, gb, after):
        send = [_by_destination(name, gb[name]) for name, _ in BIG]
        from_core = _rs_core_swap(send, "rs_core_swap")
        chip_part = [_pair_sum(s, r, cidx) for s, r in zip(send, from_core)]
        st = _ici_start(chip_part, [lax.empty(p.shape, BF16) for p in chip_part], True, "rs_ici_start_%d" % l)
        if in_flight:
            finish(*in_flight.pop(), after)
        in_flight.append((l, st))
        if l == 0:
            finish(*in_flight.pop(), st[4])
        return st[4]

    loss_local, dx, small_g = _local_step(x[0], mem[0], loss_target[0], {n: w[n] for n in SMALL}, get_layer, on_grads)
    g_shard = {name: jnp.stack([g_layers[l][name] for l in range(DEPTH)]) for name, _ in BIG}

    small_all, = _all_gather([_pack_small(small_g, loss_local)], "gather_small")
    packed_g = _sum_slots(small_all, "small_sum")
    small_sum, loss = _unpack_small(packed_g, {n: w[n] for n in SMALL})

    grads, delta, new_m, new_v = {}, {}, {}, {}
    for name, _ in BIG:
        grads[name] = g_shard[name]
        d_, m_, v_ = _adamw(_flat2d(w[name]), _flat2d(grads[name]), _flat2d(m[name]), _flat2d(v[name]), "adamw_" + name)
        delta[name], new_m[name], new_v[name] = (a.reshape(w[name].shape) for a in (d_, m_, v_))
    d_, m_, v_ = _adamw(_pack_small({n: w[n] for n in SMALL}), packed_g, _pack_small({n: m[n] for n in SMALL}),
                        _pack_small({n: v[n] for n in SMALL}), "adamw_small")
    like = {n: w[n] for n in SMALL}
    ds, _ = _unpack_small(d_, like)
    ms, _ = _unpack_small(m_, like)
    vs, _ = _unpack_small(v_, like)
    for n in SMALL:
        grads[n], delta[n], new_m[n], new_v[n] = small_sum[n], ds[n], ms[n], vs[n]

    return (loss, dx[None], *[grads[n] for n in WEIGHTS], *[delta[n] for n in WEIGHTS], *[new_m[n] for n in WEIGHTS],
            *[new_v[n] for n in WEIGHTS])
```

```python
import functools

import numpy as np
import jax
import jax.numpy as jnp
from jax import lax
from jax.experimental import pallas as pl
from jax.experimental.pallas import tpu as pltpu

F32 = jnp.float32
BF16 = jnp.bfloat16
MXU = jnp.bfloat16
HI = lax.Precision.HIGHEST

DEPTH = 4
D = 1024
BW = 256
HD = 64
NH = 4
GRID_W = 64
NA_ROWS_WIN = 8
NA_COLS_WIN = 16
N_MEM = 256
FF = 2816
EPS = 1e-6
NEG = -1e30
ROPE_THETA = 10000.0
POOL_HALF_MAX = 8

ADAM_LR, ADAM_B1, ADAM_B2, ADAM_EPS, ADAM_WD, ADAM_STEP = 0.001, 0.9, 0.999, 1e-08, 0.01, 10

N_DEV = 8
VMEM_LIMIT = 56 * 1024 * 1024

RQ, RK, RV, RG, PV, NQ, NK, NV, MQ = range(9)

MESH = pl.DeviceIdType.MESH
ANY = pl.BlockSpec(memory_space=pl.ANY)
SMEM = pl.BlockSpec(memory_space=pltpu.SMEM)


def _cp(**kw):
    return pltpu.CompilerParams(vmem_limit_bytes=VMEM_LIMIT, **kw)


def _tile(n, cap):
    if n <= cap:
        return n
    best = None
    for t in range(128, cap + 1, 128):
        if n % t == 0:
            best = t
    assert best is not None, (n, cap)
    return best


def _sds(shape, dtype):
    return jax.ShapeDtypeStruct(shape, dtype)


def _lane_head(shape):
    return lax.shift_right_logical(lax.broadcasted_iota(jnp.int32, shape, len(shape) - 1), 6)


def _group_mean(z):
    i = lax.shift_right_logical(lax.broadcasted_iota(jnp.int32, (BW, BW), 0), 6)
    j = lax.shift_right_logical(lax.broadcasted_iota(jnp.int32, (BW, BW), 1), 6)
    g = jnp.where(i == j, 1.0 / HD, 0.0).astype(F32)
    return jnp.dot(z, g, precision=HI, preferred_element_type=F32)


def _gnorm(t, g):
    r = lax.rsqrt(_group_mean(t * t) + EPS)
    return t * r * g


def _gnorm_bwd(dy, t, g):
    r = lax.rsqrt(_group_mean(t * t) + EPS)
    th = t * r
    dth = dy * g
    dt = r * (dth - th * _group_mean(dth * th))
    return dt, dy * th


def _swap_halves(t):
    lane = lax.broadcasted_iota(jnp.int32, t.shape, 1)
    return jnp.where((lane & 63) < 32, pltpu.roll(t, BW - 32, 1), pltpu.roll(t, 32, 1))


def _sigmoid(x):
    return 1.0 / (1.0 + jnp.exp(-x))


def _dot(a, b, ta=False, tb=False):
    return lax.dot_general(a.astype(MXU), b.astype(MXU), (((0 if ta else 1,), (1 if tb else 0,)), ((), ())),
                           preferred_element_type=F32)


def _stack_heads(t):
    head = _lane_head(t.shape)
    return jnp.concatenate([jnp.where(head == h, t, jnp.zeros_like(t)) for h in range(NH)], axis=0)


def _unstack_heads(t, rows):
    head = _lane_head((rows, BW))
    out = jnp.zeros((rows, BW), F32)
    for h in range(NH):
        out = out + jnp.where(head == h, t[h * rows:(h + 1) * rows], 0.0)
    return out


def _softmax_rows(s):
    m = jnp.max(s, axis=-1, keepdims=True)
    e = jnp.exp(s - m)
    return e / jnp.sum(e, axis=-1, keepdims=True)


def _acc(ref, val, first):
    @pl.when(first)
    def _():
        ref[...] = val

    @pl.when(jnp.logical_not(first))
    def _():
        ref[...] += val


def _mm(a, b, *, ta=False, tb=False, out_dtype=F32, add=None, dep=None, name):
    m, k = (a.shape[1], a.shape[0]) if ta else a.shape
    n = b.shape[0] if tb else b.shape[1]
    tm, tn, tk = _tile(m, 1024), _tile(n, 512), _tile(k, 1024)
    nk = k // tk

    def body(*refs):
        if dep is not None:
            refs = refs[:-3] + refs[-2:]
        if add is None:
            a_ref, b_ref, o_ref, acc_ref = refs
        else:
            a_ref, b_ref, c_ref, o_ref, acc_ref = refs
        kk = pl.program_id(2)
        _acc(acc_ref, _dot(a_ref[...], b_ref[...], ta, tb), kk == 0)

        @pl.when(kk == nk - 1)
        def _():
            r = acc_ref[...]
            if add is not None:
                r = r + c_ref[...]
            o_ref[...] = r.astype(out_dtype)

    a_spec = pl.BlockSpec((tk, tm), lambda i, j, kk: (kk, i)) if ta else pl.BlockSpec((tm, tk), lambda i, j, kk: (i, kk))
    b_spec = pl.BlockSpec((tn, tk), lambda i, j, kk: (j, kk)) if tb else pl.BlockSpec((tk, tn), lambda i, j, kk: (kk, j))
    o_spec = pl.BlockSpec((tm, tn), lambda i, j, kk: (i, j))
    ins, args = [a_spec, b_spec], [a, b]
    if add is not None:
        ins.append(o_spec)
        args.append(add)
    if dep is not None:
        ins.append(pl.BlockSpec((8, 128), lambda i, j, kk: (0, 0)))
        args.append(dep)
    return pl.pallas_call(
        body, grid=(m // tm, n // tn, nk), in_specs=ins, out_specs=o_spec, out_shape=_sds((m, n), out_dtype),
        scratch_shapes=[pltpu.VMEM((tm, tn), F32)], name=name,
        compiler_params=_cp(dimension_semantics=("parallel", "parallel", "arbitrary")))(*args)


def _rmsnorm_fwd(x, g, name):
    t, d = x.shape
    tm = _tile(t, 256)

    def body(x_ref, g_ref, o_ref):
        xv = x_ref[...]
        r = lax.rsqrt(jnp.mean(xv * xv, axis=-1, keepdims=True) + EPS)
        o_ref[...] = (xv * r * g_ref[...]).astype(o_ref.dtype)

    return pl.pallas_call(
        body, grid=(t // tm,), in_specs=[pl.BlockSpec((tm, d), lambda i: (i, 0)), pl.BlockSpec((1, d), lambda i: (0, 0))],
        out_specs=pl.BlockSpec((tm, d), lambda i: (i, 0)), out_shape=_sds((t, d), BF16), name=name, compiler_params=_cp())(x, g)


def _rmsnorm_bwd(dh, x, g, res, name):
    t, d = x.shape
    tm = _tile(t, 256)

    def body(dh_ref, x_ref, g_ref, res_ref, dx_ref, dg_ref):
        xv = x_ref[...]
        dhv = dh_ref[...]
        r = lax.rsqrt(jnp.mean(xv * xv, axis=-1, keepdims=True) + EPS)
        xh = xv * r
        dxh = dhv * g_ref[...]
        dx_ref[...] = res_ref[...] + r * (dxh - xh * jnp.mean(dxh * xh, axis=-1, keepdims=True))
        _acc(dg_ref, jnp.sum(dhv * xh, axis=0, keepdims=True), pl.program_id(0) == 0)

    row = pl.BlockSpec((tm, d), lambda i: (i, 0))
    vec = pl.BlockSpec((1, d), lambda i: (0, 0))
    return pl.pallas_call(
        body, grid=(t // tm,), in_specs=[row, row, vec, row], out_specs=(row, vec),
        out_shape=(_sds((t, d), F32), _sds((1, d), F32)), name=name, compiler_params=_cp())(dh, x, g, res)


def _prep_fwd(proj, cos2, sin2, g_naq, g_nak, g_mq):
    t = proj.shape[0]
    tm = 256

    def body(p_ref, cos_ref, sin_ref, gq_ref, gk_ref, gm_ref, rq_ref, rk_ref, rv_ref, nq_ref, nk_ref, nv_ref, mq_ref):
        def col(c):
            return p_ref[:, c * BW:(c + 1) * BW]

        cosv, sinv = cos_ref[...], sin_ref[...]

        def rot(tv):
            return tv * cosv + _swap_halves(tv) * sinv

        rq_ref[...] = (rot(col(RQ)) * (HD ** -0.5)).astype(BF16)
        rk_ref[...] = rot(col(RK)).astype(BF16)
        rv_ref[...] = col(RV).astype(BF16)
        nq_ref[...] = _gnorm(col(NQ), gq_ref[...]).astype(BF16)
        nk_ref[...] = _gnorm(col(NK), gk_ref[...]).astype(BF16)
        nv_ref[...] = col(NV).astype(BF16)
        mq_ref[...] = _gnorm(col(MQ), gm_ref[...]).astype(BF16)

    blk = pl.BlockSpec((tm, BW), lambda i: (i, 0))
    vec = pl.BlockSpec((1, BW), lambda i: (0, 0))
    return pl.pallas_call(
        body, grid=(t // tm,), in_specs=[pl.BlockSpec((tm, 9 * BW), lambda i: (i, 0)), blk, blk, vec, vec, vec],
        out_specs=tuple(blk for _ in range(7)), out_shape=tuple(_sds((t, BW), BF16) for _ in range(7)),
        name="prep_fwd", compiler_params=_cp())(proj, cos2, sin2, g_naq, g_nak, g_mq)


def _prep_bwd(proj, cos2, sin2, g_naq, g_nak, g_mq, d_rq, d_rk, d_rv, d_rg, d_pv, d_nq, d_nk, d_nv, d_mq):
    t = proj.shape[0]
    tm = 256

    def body(p_ref, cos_ref, sin_ref, gq_ref, gk_ref, gm_ref, drq_ref, drk_ref, drv_ref, drg_ref, dpv_ref, dnq_ref, dnk_ref,
             dnv_ref, dmq_ref, o_ref, dgq_ref, dgk_ref, dgm_ref):
        first = pl.program_id(0) == 0

        def col(c):
            return p_ref[:, c * BW:(c + 1) * BW]

        def put(c, v):
            o_ref[:, c * BW:(c + 1) * BW] = v.astype(BF16)

        cosv, sinv = cos_ref[...], sin_ref[...]

        def rot_t(dv):
            return dv * cosv + _swap_halves(dv * sinv)

        put(RQ, rot_t(drq_ref[...] * (HD ** -0.5)))
        put(RK, rot_t(drk_ref[...]))
        put(RV, drv_ref[...])
        put(RG, drg_ref[...])
        put(PV, dpv_ref[...])
        dq, gq = _gnorm_bwd(dnq_ref[...], col(NQ), gq_ref[...])
        put(NQ, dq)
        _acc(dgq_ref, jnp.sum(gq, axis=0, keepdims=True), first)
        dk, gk = _gnorm_bwd(dnk_ref[...], col(NK), gk_ref[...])
        put(NK, dk)
        _acc(dgk_ref, jnp.sum(gk, axis=0, keepdims=True), first)
        put(NV, dnv_ref[...])
        dm, gm = _gnorm_bwd(dmq_ref[...], col(MQ), gm_ref[...])
        put(MQ, dm)
        _acc(dgm_ref, jnp.sum(gm, axis=0, keepdims=True), first)

    blk = pl.BlockSpec((tm, BW), lambda i: (i, 0))
    vec = pl.BlockSpec((1, BW), lambda i: (0, 0))
    wide = pl.BlockSpec((tm, 9 * BW), lambda i: (i, 0))
    return pl.pallas_call(
        body, grid=(t // tm,), in_specs=[wide, blk, blk, vec, vec, vec] + [blk] * 9, out_specs=(wide, vec, vec, vec),
        out_shape=(_sds((t, 9 * BW), BF16), _sds((1, BW), F32), _sds((1, BW), F32), _sds((1, BW), F32)),
        name="prep_bwd", compiler_params=_cp())(proj, cos2, sin2, g_naq, g_nak, g_mq, d_rq, d_rk, d_rv, d_rg, d_pv, d_nq, d_nk,
                                                d_nv, d_mq)


RET_TQ = 64


def _ret_decay(i, tq, t, lgf_ref, lgb_ref):
    rows = NH * tq
    n = i * tq + (lax.broadcasted_iota(jnp.int32, (rows, 1), 0) & (tq - 1))
    m = lax.broadcasted_iota(jnp.int32, (1, t), 1)
    diff = n - m
    causal = diff >= 0
    dist = jnp.abs(diff).astype(F32)
    lgf = jnp.concatenate([jnp.full((tq, 1), lgf_ref[h], F32) for h in range(NH)], axis=0)
    lgb = jnp.concatenate([jnp.full((tq, 1), lgb_ref[h], F32) for h in range(NH)], axis=0)
    return causal, dist, jnp.exp(dist * jnp.where(causal, lgf, lgb))


def _ret_fwd(q, k, v, proj, lgf, lgb, g_ret):
    t = q.shape[0]
    tq = RET_TQ

    def body(lgf_ref, lgb_ref, q_ref, k_ref, v_ref, rg_ref, g_ref, o_ref, ret_ref):
        i = pl.program_id(0)
        qs = _stack_heads(q_ref[...])
        s = _dot(qs, k_ref[...], tb=True)
        _, _, dm = _ret_decay(i, tq, t, lgf_ref, lgb_ref)
        o = _unstack_heads(_dot(s * dm, v_ref[...]), tq)
        o_ref[...] = o
        rg = rg_ref[...]
        ret_ref[...] = (_gnorm(o, g_ref[...]) * (rg * _sigmoid(rg))).astype(BF16)

    blk = pl.BlockSpec((tq, BW), lambda i: (i, 0))
    whole = pl.BlockSpec((t, BW), lambda i: (0, 0))
    return pl.pallas_call(
        body, grid=(t // tq,),
        in_specs=[SMEM, SMEM, blk, whole, whole, pl.BlockSpec((tq, BW), lambda i: (i, RG)), pl.BlockSpec((1, BW), lambda i: (0, 0))],
        out_specs=(blk, blk), out_shape=(_sds((t, BW), F32), _sds((t, BW), BF16)), name="ret_fwd",
        compiler_params=_cp())(lgf, lgb, q, k, v, proj, g_ret)


def _ret_bwd(dbr, o_ret, q, k, v, proj, lgf, lgb, g_ret):
    t = q.shape[0]
    tq = RET_TQ
    nblk = t // tq

    def body(lgf_ref, lgb_ref, d_ref, o_ref, q_ref, k_ref, v_ref, rg_ref, g_ref,
             dq_ref, dk_ref, dv_ref, drg_ref, dg_ref, dlg_ref, accf_ref, accb_ref):
        i = pl.program_id(0)
        first = i == 0
        dret, o, rg, g = d_ref[...], o_ref[...], rg_ref[...], g_ref[...]
        sg = _sigmoid(rg)
        dy = dret * (rg * sg)
        do, dgain = _gnorm_bwd(dy, o, g)
        drg_ref[...] = dret * _gnorm(o, g) * (sg * (1.0 + rg * (1.0 - sg)))
        _acc(dg_ref, jnp.sum(dgain, axis=0, keepdims=True), first)

        dos = _stack_heads(do).astype(MXU)
        qs = _stack_heads(q_ref[...])
        kv, vv = k_ref[...], v_ref[...]
        s = _dot(qs, kv, tb=True)
        causal, dist, dm = _ret_decay(i, tq, t, lgf_ref, lgb_ref)
        da = _dot(dos, vv, tb=True)
        _acc(dv_ref, _dot(s * dm, dos, ta=True), first)
        ds = da * dm
        w = ds * s * dist
        _acc(accf_ref, jnp.sum(jnp.where(causal, w, 0.0), axis=1, keepdims=True), first)
        _acc(accb_ref, jnp.sum(jnp.where(causal, 0.0, w), axis=1, keepdims=True), first)
        dsb = ds.astype(MXU)
        dq_ref[...] = _unstack_heads(_dot(dsb, kv), tq)
        _acc(dk_ref, _dot(dsb, qs, ta=True), first)

        @pl.when(i == nblk - 1)
        def _():
            for h in range(NH):
                dlg_ref[h:h + 1, :] = jnp.full((1, 128), jnp.sum(accf_ref[h * tq:(h + 1) * tq, :]), F32)
                dlg_ref[NH + h:NH + h + 1, :] = jnp.full((1, 128), jnp.sum(accb_ref[h * tq:(h + 1) * tq, :]), F32)

    blk = pl.BlockSpec((tq, BW), lambda i: (i, 0))
    whole = pl.BlockSpec((t, BW), lambda i: (0, 0))
    vec = pl.BlockSpec((1, BW), lambda i: (0, 0))
    return pl.pallas_call(
        body, grid=(nblk,),
        in_specs=[SMEM, SMEM, blk, blk, blk, whole, whole, pl.BlockSpec((tq, BW), lambda i: (i, RG)), vec],
        out_specs=(blk, whole, whole, blk, vec, pl.BlockSpec((2 * NH, 128), lambda i: (0, 0))),
        out_shape=(_sds((t, BW), F32), _sds((t, BW), F32), _sds((t, BW), F32), _sds((t, BW), F32), _sds((1, BW), F32),
                   _sds((2 * NH, 128), F32)),
        scratch_shapes=[pltpu.VMEM((NH * tq, 1), F32), pltpu.VMEM((NH * tq, 1), F32)], name="ret_bwd",
        compiler_params=_cp())(lgf, lgb, dbr, o_ret, q, k, v, proj, g_ret)


def _pool_windows(t):
    row = lax.broadcasted_iota(jnp.int32, (t, BW), 0)
    half = lax.shift_left(jnp.ones((t, BW), jnp.int32), _lane_head((t, BW)))
    cnt = (jnp.minimum(row + half, t) - jnp.maximum(row - half, 0)).astype(F32)
    return row, half, cnt


def _pool_window_sum(v, row, half, t, transpose):
    out = jnp.zeros_like(v)
    for j in range(-POOL_HALF_MAX, POOL_HALF_MAX):
        src = row - j if transpose else row + j
        ok = (src >= 0) & (src < t) & (j >= -half) & (j < half)
        out = out + jnp.where(ok, pltpu.roll(v, (j if transpose else -j) % t, 0), 0.0)
    return out


def _pool_fwd(proj, wbd, scale):
    t = proj.shape[0]

    def body(v_ref, w_ref, s_ref, o_ref):
        v = v_ref[...]
        row, half, cnt = _pool_windows(t)
        pooled = _pool_window_sum(v, row, half, t, False) / cnt - v
        o_ref[...] = (_dot(pooled, w_ref[...]) * s_ref[...]).astype(BF16)

    return pl.pallas_call(
        body, grid=(1,),
        in_specs=[pl.BlockSpec((t, BW), lambda i: (0, PV)), pl.BlockSpec((BW, BW), lambda i: (0, 0)), pl.BlockSpec((1, BW), lambda i: (0, 0))],
        out_specs=pl.BlockSpec((t, BW), lambda i: (0, 0)), out_shape=_sds((t, BW), BF16), name="pool_fwd",
        compiler_params=_cp())(proj, wbd, scale)


def _pool_bwd(dbr, proj, wbd, scale):
    t = proj.shape[0]

    def body(d_ref, v_ref, w_ref, s_ref, dv_ref, dw_ref, ds_ref):
        v, dout = v_ref[...], d_ref[...]
        row, half, cnt = _pool_windows(t)
        pooled = _pool_window_sum(v, row, half, t, False) / cnt - v
        mixed = _dot(pooled, w_ref[...])
        ds_ref[...] = jnp.sum(dout * mixed, axis=0, keepdims=True)
        dmixed = dout * s_ref[...]
        dw_ref[...] = _dot(pooled, dmixed, ta=True)
        dpooled = _dot(dmixed, w_ref[...], tb=True)
        dv_ref[...] = _pool_window_sum(dpooled / cnt, row, half, t, True) - dpooled

    return pl.pallas_call(
        body, grid=(1,),
        in_specs=[pl.BlockSpec((t, BW), lambda i: (0, 1)), pl.BlockSpec((t, BW), lambda i: (0, PV)),
                  pl.BlockSpec((BW, BW), lambda i: (0, 0)), pl.BlockSpec((1, BW), lambda i: (0, 0))],
        out_specs=(pl.BlockSpec((t, BW), lambda i: (0, 0)), pl.BlockSpec((BW, BW), lambda i: (0, 0)), pl.BlockSpec((1, BW), lambda i: (0, 0))),
        out_shape=(_sds((t, BW), F32), _sds((BW, BW), F32), _sds((1, BW), F32)), name="pool_bwd",
        compiler_params=_cp())(dbr, proj, wbd, scale)


NA_KEYS = NA_ROWS_WIN * GRID_W


def _na_window(r, n_rows):
    rs = jnp.clip(r - NA_ROWS_WIN // 2, 0, n_rows - NA_ROWS_WIN)
    return pl.multiple_of(rs * GRID_W, GRID_W), rs - r + (NA_ROWS_WIN - 1)


def _na_fwd(q, k, v, ball):
    t = q.shape[0]
    n_rows = t // GRID_W

    def body(q_ref, k_ref, v_ref, b_ref, o_ref):
        start, a0 = _na_window(pl.program_id(0), n_rows)
        qs = _stack_heads(q_ref[...])
        s = _dot(qs, k_ref[pl.ds(start, NA_KEYS), :], tb=True) * (HD ** -0.5) + b_ref[a0]
        p = _softmax_rows(s)
        o_ref[...] = _unstack_heads(_dot(p, v_ref[pl.ds(start, NA_KEYS), :]), GRID_W).astype(BF16)

    blk = pl.BlockSpec((GRID_W, BW), lambda r: (r, 0))
    whole = pl.BlockSpec((t, BW), lambda r: (0, 0))
    return pl.pallas_call(
        body, grid=(n_rows,), in_specs=[blk, whole, whole, pl.BlockSpec(ball.shape, lambda r: (0, 0, 0))],
        out_specs=blk, out_shape=_sds((t, BW), BF16), name="na_fwd", compiler_params=_cp())(q, k, v, ball)


def _na_bwd(dbr, q, k, v, ball):
    t = q.shape[0]
    n_rows = t // GRID_W

    def body(d_ref, q_ref, k_ref, v_ref, b_ref, dq_ref, dk_ref, dv_ref, db_ref):
        r = pl.program_id(0)
        start, a0 = _na_window(r, n_rows)
        keys = pl.ds(start, NA_KEYS)

        @pl.when(r == 0)
        def _():
            dk_ref[...] = jnp.zeros_like(dk_ref)
            dv_ref[...] = jnp.zeros_like(dv_ref)
            db_ref[...] = jnp.zeros_like(db_ref)

        qs = _stack_heads(q_ref[...])
        kb, vb = k_ref[keys, :], v_ref[keys, :]
        p = _softmax_rows(_dot(qs, kb, tb=True) * (HD ** -0.5) + b_ref[a0])
        dos = _stack_heads(d_ref[...]).astype(MXU)
        dp = _dot(dos, vb, tb=True)
        dv_ref[keys, :] += _dot(p, dos, ta=True)
        ds = p * (dp - jnp.sum(dp * p, axis=-1, keepdims=True))
        db_ref[a0] += ds
        dsb = (ds * (HD ** -0.5)).astype(MXU)
        dq_ref[...] = _unstack_heads(_dot(dsb, kb), GRID_W)
        dk_ref[keys, :] += _dot(dsb, qs, ta=True)

    blk = pl.BlockSpec((GRID_W, BW), lambda r: (r, 0))
    whole = pl.BlockSpec((t, BW), lambda r: (0, 0))
    tab = pl.BlockSpec(ball.shape, lambda r: (0, 0, 0))
    return pl.pallas_call(
        body, grid=(n_rows,), in_specs=[pl.BlockSpec((GRID_W, BW), lambda r: (r, 2)), blk, whole, whole, tab],
        out_specs=(blk, whole, whole, tab),
        out_shape=(_sds((t, BW), F32), _sds((t, BW), F32), _sds((t, BW), F32), _sds(ball.shape, F32)), name="na_bwd",
        compiler_params=_cp())(dbr, q, k, v, ball)


def _rpb_expand(rpb_pad, onehot):
    def body(r_ref, e_ref, o_ref):
        o_ref[...] = jnp.dot(r_ref[...], e_ref[...], precision=HI, preferred_element_type=F32)

    return pl.pallas_call(body, out_shape=_sds((64, GRID_W * GRID_W), F32), name="rpb_expand", compiler_params=_cp())(rpb_pad, onehot)


def _rpb_reduce(dtab, onehot):
    def body(d_ref, e_ref, o_ref):
        o_ref[...] = lax.dot_general(d_ref[...], e_ref[...], (((1,), (1,)), ((), ())), precision=HI, preferred_element_type=F32)

    return pl.pallas_call(body, out_shape=_sds((64, 128), F32), name="rpb_reduce", compiler_params=_cp())(dtab, onehot)


MEM_TQ = 256


def _mem_fwd(q, mk, mv):
    t = q.shape[0]
    tq = MEM_TQ

    def body(q_ref, k_ref, v_ref, o_ref):
        qv = q_ref[...]
        head = _lane_head(qv.shape)
        out = jnp.zeros((tq, BW), F32)
        for h in range(NH):
            p = _softmax_rows(_dot(jnp.where(head == h, qv, jnp.zeros_like(qv)), k_ref[...], tb=True) * (HD ** -0.5))
            out = out + jnp.where(head == h, _dot(p, v_ref[...]), 0.0)
        o_ref[...] = out.astype(BF16)

    blk = pl.BlockSpec((tq, BW), lambda i: (i, 0))
    kv = pl.BlockSpec((N_MEM, BW), lambda i: (0, 0))
    return pl.pallas_call(body, grid=(t // tq,), in_specs=[blk, kv, kv], out_specs=blk, out_shape=_sds((t, BW), BF16),
                          name="mem_fwd", compiler_params=_cp())(q, mk, mv)


def _mem_bwd(dbr, q, mk, mv):
    t = q.shape[0]
    tq = MEM_TQ

    def body(d_ref, q_ref, k_ref, v_ref, dq_ref, dk_ref, dv_ref):
        first = pl.program_id(0) == 0
        qv, dout = q_ref[...], d_ref[...]
        head = _lane_head(qv.shape)
        dq = jnp.zeros((tq, BW), F32)
        dk = jnp.zeros((N_MEM, BW), F32)
        dv = jnp.zeros((N_MEM, BW), F32)
        for h in range(NH):
            qh = jnp.where(head == h, qv, jnp.zeros_like(qv))
            doh = jnp.where(head == h, dout, 0.0).astype(MXU)
            p = _softmax_rows(_dot(qh, k_ref[...], tb=True) * (HD ** -0.5))
            dp = _dot(doh, v_ref[...], tb=True)
            dv = dv + _dot(p, doh, ta=True)
            dsb = (p * (dp - jnp.sum(dp * p, axis=-1, keepdims=True)) * (HD ** -0.5)).astype(MXU)
            dq = dq + jnp.where(head == h, _dot(dsb, k_ref[...]), 0.0)
            dk = dk + _dot(dsb, qh, ta=True)
        dq_ref[...] = dq
        _acc(dk_ref, dk, first)
        _acc(dv_ref, dv, first)

    blk = pl.BlockSpec((tq, BW), lambda i: (i, 0))
    kv = pl.BlockSpec((N_MEM, BW), lambda i: (0, 0))
    return pl.pallas_call(
        body, grid=(t // tq,), in_specs=[pl.BlockSpec((tq, BW), lambda i: (i, 3)), blk, kv, kv], out_specs=(blk, kv, kv),
        out_shape=(_sds((t, BW), F32), _sds((N_MEM, BW), F32), _sds((N_MEM, BW), F32)), name="mem_bwd",
        compiler_params=_cp())(dbr, q, mk, mv)


def _memkv_prep(kv, g_mk):
    def body(kv_ref, g_ref, k_ref, v_ref):
        k_ref[...] = _gnorm(kv_ref[:, 0:BW], g_ref[...]).astype(BF16)
        v_ref[...] = kv_ref[:, BW:2 * BW].astype(BF16)

    return pl.pallas_call(body, out_shape=(_sds((N_MEM, BW), BF16), _sds((N_MEM, BW), BF16)), name="memkv_prep",
                          compiler_params=_cp())(kv, g_mk)


def _memkv_bwd(kv, dk, dv, g_mk):
    def body(kv_ref, dk_ref, dv_ref, g_ref, o_ref, dg_ref):
        dkk, gain = _gnorm_bwd(dk_ref[...], kv_ref[:, 0:BW], g_ref[...])
        o_ref[:, 0:BW] = dkk.astype(BF16)
        o_ref[:, BW:2 * BW] = dv_ref[...].astype(BF16)
        dg_ref[...] = jnp.sum(gain, axis=0, keepdims=True)

    return pl.pallas_call(body, out_shape=(_sds((N_MEM, 2 * BW), BF16), _sds((1, BW), F32)), name="memkv_bwd",
                          compiler_params=_cp())(kv, dk, dv, g_mk)


MERGE_TM = 256


def _merge_fwd(brs, wbt, gp):
    t = gp.shape[0]
    tm = MERGE_TM

    def body(b0, b1, b2, b3, wb_ref, gp_ref, o_ref):
        out = jnp.zeros((tm, D), F32)
        for n, b_ref in enumerate((b0, b1, b2, b3)):
            up = _dot(b_ref[...], wb_ref[n], tb=True)
            out = out + _sigmoid(gp_ref[:, n * D:(n + 1) * D]) * up
        o_ref[...] = out.astype(BF16)

    blk = pl.BlockSpec((tm, BW), lambda i: (i, 0))
    return pl.pallas_call(
        body, grid=(t // tm,),
        in_specs=[blk, blk, blk, blk, pl.BlockSpec((NH, D, BW), lambda i: (0, 0, 0)), pl.BlockSpec((tm, NH * D), lambda i: (i, 0))],
        out_specs=pl.BlockSpec((tm, D), lambda i: (i, 0)), out_shape=_sds((t, D), BF16), name="merge_fwd",
        compiler_params=_cp())(*brs, wbt, gp)


def _merge_bwd(dmerged, brs, wbt, gp):
    t = gp.shape[0]
    tm = MERGE_TM

    def body(d_ref, b0, b1, b2, b3, wb_ref, gp_ref, dgp_ref, dup_ref):
        dm = d_ref[...]
        for n, b_ref in enumerate((b0, b1, b2, b3)):
            up = _dot(b_ref[...], wb_ref[n], tb=True)
            g = _sigmoid(gp_ref[:, n * D:(n + 1) * D])
            dgp_ref[:, n * D:(n + 1) * D] = (dm * up * (g * (1.0 - g))).astype(BF16)
            dup_ref[:, n * D:(n + 1) * D] = (dm * g).astype(BF16)

    row = pl.BlockSpec((tm, D), lambda i: (i, 0))
    blk = pl.BlockSpec((tm, BW), lambda i: (i, 0))
    wide = pl.BlockSpec((tm, NH * D), lambda i: (i, 0))
    return pl.pallas_call(
        body, grid=(t // tm,), in_specs=[row, blk, blk, blk, blk, pl.BlockSpec((NH, D, BW), lambda i: (0, 0, 0)), wide],
        out_specs=(wide, wide), out_shape=(_sds((t, NH * D), BF16), _sds((t, NH * D), BF16)), name="merge_bwd",
        compiler_params=_cp())(dmerged, *brs, wbt, gp)


def _dbranch(dup, wbt):
    t = dup.shape[0]
    tm = 512

    def body(d_ref, w_ref, o_ref):
        o_ref[...] = _dot(d_ref[...], w_ref[...])

    return pl.pallas_call(
        body, grid=(t // tm, NH), in_specs=[pl.BlockSpec((tm, D), lambda i, n: (i, n)), pl.BlockSpec((None, D, BW), lambda i, n: (n, 0, 0))],
        out_specs=pl.BlockSpec((tm, BW), lambda i, n: (i, n)), out_shape=_sds((t, NH * BW), F32), name="dbranch",
        compiler_params=_cp())(dup, wbt)


def _dwbranch(brs, dup):
    t = dup.shape[0]

    def body(b0, b1, b2, b3, d_ref, o_ref):
        for n, b_ref in enumerate((b0, b1, b2, b3)):
            o_ref[n] = _dot(d_ref[:, n * D:(n + 1) * D], b_ref[...], ta=True).astype(BF16)

    return pl.pallas_call(body, out_shape=_sds((NH, D, BW), BF16), name="dwbranch", compiler_params=_cp())(*brs, dup)


def _swiglu_fwd(ag):
    t = ag.shape[0]
    tm = 256

    def body(ag_ref, o_ref):
        a, g = ag_ref[:, 0:FF], ag_ref[:, FF:2 * FF]
        o_ref[...] = (a * _sigmoid(a) * g).astype(BF16)

    return pl.pallas_call(body, grid=(t // tm,), in_specs=[pl.BlockSpec((tm, 2 * FF), lambda i: (i, 0))],
                          out_specs=pl.BlockSpec((tm, FF), lambda i: (i, 0)), out_shape=_sds((t, FF), BF16), name="swiglu_fwd",
                          compiler_params=_cp())(ag)


def _swiglu_bwd(ag, dy):
    t = ag.shape[0]
    tm = 256

    def body(ag_ref, dy_ref, o_ref):
        a, g, d = ag_ref[:, 0:FF], ag_ref[:, FF:2 * FF], dy_ref[...]
        s = _sigmoid(a)
        o_ref[:, 0:FF] = (d * g * (s * (1.0 + a * (1.0 - s)))).astype(BF16)
        o_ref[:, FF:2 * FF] = (d * (a * s)).astype(BF16)

    return pl.pallas_call(
        body, grid=(t // tm,), in_specs=[pl.BlockSpec((tm, 2 * FF), lambda i: (i, 0)), pl.BlockSpec((tm, FF), lambda i: (i, 0))],
        out_specs=pl.BlockSpec((tm, 2 * FF), lambda i: (i, 0)), out_shape=_sds((t, 2 * FF), BF16), name="swiglu_bwd",
        compiler_params=_cp())(ag, dy)


def _loss_head(y, target):
    t, d = y.shape
    tm = 256

    def body(y_ref, t_ref, dy_ref, l_ref):
        e = y_ref[...] - t_ref[...]
        dy_ref[...] = e * (1.0 / d)
        _acc(l_ref, jnp.full((8, 128), 0.5 * jnp.sum(jnp.sum(e * e, axis=-1, keepdims=True) * (1.0 / d)), F32), pl.program_id(0) == 0)

    row = pl.BlockSpec((tm, d), lambda i: (i, 0))
    return pl.pallas_call(body, grid=(t // tm,), in_specs=[row, row], out_specs=(row, pl.BlockSpec((8, 128), lambda i: (0, 0))),
                          out_shape=(_sds((t, d), F32), _sds((8, 128), F32)), name="loss_head", compiler_params=_cp())(y, target)


def _sum_slots(x, name):
    k, r, c = x.shape
    tr = _tile(r, 512) if r % 128 == 0 else r

    def body(x_ref, o_ref):
        acc = x_ref[0].astype(F32)
        for s in range(1, k):
            acc = acc + x_ref[s].astype(F32)
        o_ref[...] = acc

    return pl.pallas_call(body, grid=(r // tr,), in_specs=[pl.BlockSpec((k, tr, c), lambda i: (0, i, 0))],
                          out_specs=pl.BlockSpec((tr, c), lambda i: (i, 0)), out_shape=_sds((r, c), F32), name=name,
                          compiler_params=_cp())(x)


def _pair_sum(bufs, recvs, cidx):
    n = len(bufs)

    def body(c_ref, *refs):
        for i in range(n):
            refs[2 * n + i][...] = (refs[i][...].astype(F32) + refs[n + i][...].astype(F32)).astype(BF16)

    return pl.pallas_call(
        body,
        grid_spec=pltpu.PrefetchScalarGridSpec(
            num_scalar_prefetch=1, grid=(4,),
            in_specs=[pl.BlockSpec((None, None) + b.shape[2:], lambda s, cref: (s, cref[0], 0, 0)) for b in bufs]
            + [pl.BlockSpec((None,) + r.shape[1:], lambda s, cref: (s, 0, 0)) for r in recvs],
            out_specs=tuple(pl.BlockSpec((None,) + r.shape[1:], lambda s, cref: (s, 0, 0)) for r in recvs)),
        out_shape=tuple(_sds(r.shape, BF16) for r in recvs), name="rs_pair_sum", compiler_params=_cp())(cidx, *bufs, *recvs)


def _adamw(w, g, m, v, name):
    r, c = w.shape
    tr = r
    if r > 1024:
        tr = next(cand for cand in (512, 256, 128, 64, 32, 16, 8) if r % cand == 0)

    def body(w_ref, g_ref, m_ref, v_ref, d_ref, nm_ref, nv_ref):
        gv = g_ref[...]
        mn = ADAM_B1 * m_ref[...] + (1.0 - ADAM_B1) * gv
        vn = ADAM_B2 * v_ref[...] + (1.0 - ADAM_B2) * (gv * gv)
        m_hat = mn / (1.0 - ADAM_B1 ** ADAM_STEP)
        v_hat = vn / (1.0 - ADAM_B2 ** ADAM_STEP)
        d_ref[...] = -ADAM_LR * (m_hat / (jnp.sqrt(v_hat) + ADAM_EPS) + ADAM_WD * w_ref[...])
        nm_ref[...] = mn
        nv_ref[...] = vn

    blk = pl.BlockSpec((tr, c), lambda i: (i, 0))
    return pl.pallas_call(body, grid=(r // tr,), in_specs=[blk] * 4, out_specs=(blk,) * 3,
                          out_shape=tuple(_sds((r, c), F32) for _ in range(3)), name=name, compiler_params=_cp())(w, g, m, v)


def _all_gather(shards, name):
    n = len(shards)

    def body(*refs):
        x_refs, out_refs = refs[:n], refs[n:2 * n]
        send_sems, recv_sems, local_sems = refs[2 * n:]
        x, y, cc = lax.axis_index("x"), lax.axis_index("y"), lax.axis_index("c")
        me, sibling = (x, y, cc), (x, y, 1 - cc)
        chips = [(1 - x, y), (x, 1 - y), (1 - x, 1 - y)]

        def copy(i, k, block, to, own=False):
            px, py, pc = block
            slot = out_refs[i].at[4 * px + 2 * py + pc]
            return pltpu.make_async_remote_copy(
                src_ref=x_refs[i] if own else slot, dst_ref=slot, send_sem=send_sems.at[7 * i + k],
                recv_sem=recv_sems.at[7 * i + k], device_id=to, device_id_type=MESH)

        mine = [pltpu.make_async_copy(x_refs[i], out_refs[i].at[4 * x + 2 * y + cc], local_sems.at[i]) for i in range(n)]
        for cp in mine:
            cp.start()
        first = []
        for j, chip in enumerate(chips):
            first += [copy(i, 1 + j, me, (*chip, cc), own=True) for i in range(n)]
        first += [copy(i, 0, me, sibling, own=True) for i in range(n)]
        for cp in first:
            cp.start()
        passed = []
        for j, chip in enumerate(chips):
            for i in range(n):
                copy(i, 1 + j, (*chip, cc), me).wait_recv()
                cp = copy(i, 4 + j, (*chip, cc), sibling)
                cp.start()
                passed.append(cp)
        for i in range(n):
            copy(i, 0, sibling, me).wait_recv()
        for j, chip in enumerate(chips):
            for i in range(n):
                copy(i, 4 + j, (*chip, 1 - cc), me).wait_recv()
        for cp in first + passed:
            cp.wait_send()
        for cp in mine:
            cp.wait()

    return pl.pallas_call(
        body, out_shape=tuple(_sds((N_DEV,) + s.shape, s.dtype) for s in shards), in_specs=[ANY] * n, out_specs=(ANY,) * n,
        scratch_shapes=[pltpu.SemaphoreType.DMA((7 * n,)), pltpu.SemaphoreType.DMA((7 * n,)), pltpu.SemaphoreType.DMA((n,))],
        name=name)(*shards)


def _rs_core_swap(bufs, name):
    n = len(bufs)

    def body(*refs):
        b_refs, recv_refs = refs[:n], refs[n:2 * n]
        send_sems, recv_sems = refs[2 * n:]
        x, y, cc = lax.axis_index("x"), lax.axis_index("y"), lax.axis_index("c")
        copies = [pltpu.make_async_remote_copy(
            src_ref=b_refs[i].at[s, 1 - cc], dst_ref=recv_refs[i].at[s], send_sem=send_sems.at[4 * i + s],
            recv_sem=recv_sems.at[4 * i + s], device_id=(x, y, 1 - cc), device_id_type=MESH) for i in range(n) for s in range(4)]
        for cp in copies:
            cp.start()
        for cp in copies:
            cp.wait()

    return pl.pallas_call(
        body, out_shape=tuple(_sds((4,) + b.shape[2:], b.dtype) for b in bufs), in_specs=[ANY] * n, out_specs=(ANY,) * n,
        scratch_shapes=[pltpu.SemaphoreType.DMA((4 * n,)), pltpu.SemaphoreType.DMA((4 * n,))], name=name)(*bufs)


HBM = pl.BlockSpec(memory_space=pltpu.HBM)
SEMS = pl.BlockSpec(memory_space=pltpu.SEMAPHORE)
EFFECT = pltpu.SideEffectType.DATAFLOW_SIDE_EFFECTING


def _hbm(a):
    return pltpu.HBM(a.shape, a.dtype)


def _other_chips(x, y):
    return [(1 - x, y), (x, 1 - y), (1 - x, 1 - y)]


def _ici_start(srcs, lands, by_chip, name):
    n = len(srcs)

    def body(*refs):
        s_refs, land_refs = refs[:n], refs[n:2 * n]
        send_sems, recv_sems = refs[2 * n], refs[2 * n + 1]
        token = refs[-1]
        x, y, cc = lax.axis_index("x"), lax.axis_index("y"), lax.axis_index("c")
        mine = 2 * x + y if by_chip else 4 * x + 2 * y + cc
        for px, py in _other_chips(x, y):
            for i in range(n):
                pltpu.make_async_remote_copy(
                    src_ref=s_refs[i].at[2 * px + py] if by_chip else s_refs[i], dst_ref=land_refs[i].at[mine],
                    send_sem=send_sems.at[i], recv_sem=recv_sems.at[i], device_id=(px, py, cc), device_id_type=MESH).start()
        token[...] = jnp.zeros_like(token)

    out = pl.pallas_call(
        body, name=name,
        out_shape=(pltpu.SemaphoreType.DMA((n,)), pltpu.SemaphoreType.DMA((n,)), *[_hbm(s) for s in srcs], *[_hbm(l) for l in lands],
                   _sds((8, 128), F32)),
        in_specs=[HBM] * (2 * n), out_specs=(SEMS, SEMS, *[HBM] * (2 * n), pl.BlockSpec(memory_space=pltpu.VMEM)),
        input_output_aliases={i: 2 + i for i in range(2 * n)}, compiler_params=pltpu.CompilerParams(has_side_effects=EFFECT),
    )(*[pltpu.with_memory_space_constraint(s, pltpu.HBM) for s in srcs],
      *[pltpu.with_memory_space_constraint(l, pltpu.HBM) for l in lands])
    return out[0], out[1], out[2:2 + n], out[2 + n:2 + 2 * n], out[-1]


def _ici_wait(started, after, name):
    send_sems, recv_sems, srcs, lands, _ = started
    n = len(srcs)

    def body(*refs):
        land_refs = refs[n:2 * n]
        send_sems, recv_sems = refs[2 * n], refs[2 * n + 1]
        x, y, cc = lax.axis_index("x"), lax.axis_index("y"), lax.axis_index("c")
        for i in range(n):
            three = land_refs[i].at[pl.ds(0, 3)]
            cp = pltpu.make_async_remote_copy(src_ref=three, dst_ref=three, send_sem=send_sems.at[i], recv_sem=recv_sems.at[i],
                                              device_id=(x, y, cc), device_id_type=MESH)
            cp.wait_send()
            cp.wait_recv()

    return pl.pallas_call(
        body, name=name, out_shape=tuple(_hbm(l) for l in lands), in_specs=[HBM] * (2 * n) + [SEMS, SEMS, ANY],
        out_specs=tuple([HBM] * n), input_output_aliases={n + i: i for i in range(n)},
        compiler_params=pltpu.CompilerParams(has_side_effects=EFFECT))(*srcs, *lands, send_sems, recv_sems, after)


def _gather_d2d(blocks, lands, name):
    n = len(blocks)

    def body(*refs):
        x_refs, land_refs = refs[:n], refs[2 * n:3 * n]
        send_sems, recv_sems, in_sems, out_sems = refs[3 * n:3 * n + 4]
        stage = refs[3 * n + 4:]
        x, y, cc = lax.axis_index("x"), lax.axis_index("y"), lax.axis_index("c")
        sibling = (x, y, 1 - cc)
        staged = [pltpu.make_async_copy(x_refs[i], stage[i], in_sems.at[i]) for i in range(n)]
        for cp in staged:
            cp.start()
        copies = []
        for i in range(n):
            slot = land_refs[i].at[4 * x + 2 * y + cc]
            copies.append(pltpu.make_async_remote_copy(src_ref=x_refs[i], dst_ref=slot, send_sem=send_sems.at[4 * i],
                                                       recv_sem=recv_sems.at[4 * i], device_id=sibling, device_id_type=MESH))
            for j, (px, py) in enumerate(_other_chips(x, y)):
                slot = land_refs[i].at[4 * px + 2 * py + cc]
                copies.append(pltpu.make_async_remote_copy(src_ref=slot, dst_ref=slot, send_sem=send_sems.at[4 * i + 1 + j],
                                                           recv_sem=recv_sems.at[4 * i + 1 + j], device_id=sibling, device_id_type=MESH))
        for cp in copies:
            cp.start()
        mine = []
        for i in range(n):
            staged[i].wait()
            mine.append(pltpu.make_async_copy(stage[i], land_refs[i].at[4 * x + 2 * y + cc], out_sems.at[i]))
            mine[i].start()
        for i in range(n):
            slot = land_refs[i].at[4 * x + 2 * y + (1 - cc)]
            pltpu.make_async_remote_copy(src_ref=slot, dst_ref=slot, send_sem=send_sems.at[4 * i], recv_sem=recv_sems.at[4 * i],
                                         device_id=sibling, device_id_type=MESH).wait_recv()
            for j, (px, py) in enumerate(_other_chips(x, y)):
                slot = land_refs[i].at[4 * px + 2 * py + (1 - cc)]
                pltpu.make_async_remote_copy(src_ref=slot, dst_ref=slot, send_sem=send_sems.at[4 * i + 1 + j],
                                             recv_sem=recv_sems.at[4 * i + 1 + j], device_id=sibling, device_id_type=MESH).wait_recv()
        for cp in copies:
            cp.wait_send()
        for cp in mine:
            cp.wait()

    return pl.pallas_call(
        body, out_shape=tuple(_sds(l.shape, l.dtype) for l in lands), in_specs=[ANY] * (2 * n), out_specs=(ANY,) * n,
        input_output_aliases={n + i: i for i in range(n)},
        scratch_shapes=[pltpu.SemaphoreType.DMA((4 * n,)), pltpu.SemaphoreType.DMA((4 * n,)), pltpu.SemaphoreType.DMA((n,)),
                        pltpu.SemaphoreType.DMA((n,))] + [pltpu.VMEM(b.shape, b.dtype) for b in blocks],
        name=name, compiler_params=_cp())(*blocks, *lands)


def _sum_own(parts, recvs, chip, name):
    n = len(parts)

    def body(c_ref, *refs):
        s = pl.program_id(0)
        for i in range(n):
            val = jnp.where(c_ref[0] == s, refs[i][...], refs[n + i][...]).astype(F32)
            _acc(refs[2 * n + i], val, s == 0)

    ins = [pl.BlockSpec((None,) + p.shape[1:], lambda s, cref: (s, 0, 0)) for p in parts]
    return pl.pallas_call(
        body, grid_spec=pltpu.PrefetchScalarGridSpec(
            num_scalar_prefetch=1, grid=(4,), in_specs=ins + ins,
            out_specs=tuple(pl.BlockSpec(p.shape[1:], lambda s, cref: (0, 0)) for p in parts)),
        out_shape=tuple(_sds(p.shape[1:], F32) for p in parts), name=name, compiler_params=_cp())(chip, *parts, *recvs)


BIG = (("w_in", True), ("w_gate", True), ("w_mem_kv", False), ("w_branch", True), ("w_out", False), ("w_ffn_in", True),
       ("w_ffn_out", False))

SMALL = ("norm_mix_g", "norm_mem_g", "ret_decay_fwd", "ret_decay_bwd", "ret_norm_g", "pool_w", "pool_scale", "na_q_norm_g",
         "na_k_norm_g", "na_rpb", "mem_q_norm_g", "mem_k_norm_g", "norm_ffn_g")
WEIGHTS = ("norm_mix_g", "norm_mem_g", "w_in", "w_gate", "ret_decay_fwd", "ret_decay_bwd", "ret_norm_g", "pool_w", "pool_scale",
           "na_q_norm_g", "na_k_norm_g", "na_rpb", "mem_q_norm_g", "mem_k_norm_g", "w_mem_kv", "w_branch", "w_out", "norm_ffn_g",
           "w_ffn_in", "w_ffn_out")


def _to_exchange(name, transposed, shard):
    if name == "w_branch":
        return jnp.swapaxes(shard, 1, 2).reshape(NH * (D // N_DEV), BW)
    return shard.T if transposed else shard


def _from_exchange(name, transposed, block):
    if name == "w_branch":
        return jnp.swapaxes(block.reshape(NH, D // N_DEV, BW), 1, 2)
    return block.T if transposed else block


def _whole_from_gathered(name, g):
    if name == "w_branch":
        return jnp.swapaxes(g.reshape(N_DEV, NH, D // N_DEV, BW), 0, 1).reshape(NH, D, BW)
    return g.reshape(N_DEV * g.shape[1], g.shape[2])


def _by_destination(name, g):
    if name == "w_branch":
        g = jnp.swapaxes(g.reshape(NH, N_DEV, D // N_DEV, BW), 0, 1).reshape(N_DEV * NH * (D // N_DEV), BW)
    return g.reshape(4, 2, g.shape[0] // N_DEV, g.shape[1])


SMALL_PAD = 1024


def _pack_small(vals, loss=None):
    parts = [vals[n] for n in SMALL] + [jnp.zeros((1,), F32) if loss is None else loss.reshape(1)]
    rows = []
    for p in parts:
        flat = p.reshape(-1)
        rows.append(jnp.pad(flat, (0, -flat.shape[0] % SMALL_PAD)).reshape(-1, 128))
    return jnp.concatenate(rows, axis=0)


def _unpack_small(packed, like):
    out, off = {}, 0
    for n in SMALL:
        sz = int(np.prod(like[n].shape))
        nrow = -(-sz // SMALL_PAD) * (SMALL_PAD // 128)
        out[n] = packed[off:off + nrow].reshape(-1)[:sz].reshape(like[n].shape)
        off += nrow
    return out, packed[off, 0]


def _na_constants():
    c = np.arange(GRID_W)
    win = np.clip(c - NA_COLS_WIN // 2, 0, GRID_W - NA_COLS_WIN)
    kc = np.arange(GRID_W)
    inside = (kc[None, :] >= win[:, None]) & (kc[None, :] < win[:, None] + NA_COLS_WIN)
    off = kc[None, :] - c[:, None] + NA_COLS_WIN - 1
    onehot = np.zeros((128, GRID_W, GRID_W), np.float32)
    for b in range(2 * NA_COLS_WIN - 1):
        onehot[b] = (off == b) & inside
    maskadd = np.where(inside, 0.0, NEG).astype(np.float32)
    return onehot.reshape(128, GRID_W * GRID_W), maskadd


def _na_bias_table(tab, maskadd):
    n_off = 2 * NA_ROWS_WIN - 1
    t4 = tab[:NH * n_off].reshape(NH, n_off, GRID_W, GRID_W) + maskadd[None, None]
    ball = jnp.stack([t4[:, a0:a0 + NA_ROWS_WIN] for a0 in range(NA_ROWS_WIN)], axis=1)
    return ball.transpose(1, 0, 3, 2, 4).reshape(NA_ROWS_WIN, NH * GRID_W, NA_KEYS)


def _rotary_tables(t):
    half = HD // 2
    inv = ROPE_THETA ** (-jnp.arange(half, dtype=F32) / half)
    ang = jnp.arange(t, dtype=F32)[:, None] * inv[None, :]
    cos, sin = jnp.cos(ang), jnp.sin(ang)
    return jnp.tile(jnp.concatenate([cos, cos], axis=-1), (1, NH)), jnp.tile(jnp.concatenate([-sin, sin], axis=-1), (1, NH))


def _block_diag(pw):
    out = jnp.zeros((BW, BW), pw.dtype)
    for g in range(NH):
        out = lax.dynamic_update_slice(out, pw[g], (g * HD, g * HD))
    return out


def _tile4(g):
    return jnp.tile(g.reshape(1, HD), (1, NH))


def _layer_fwd(x, mem, sw, lw, consts):
    cos2, sin2, onehot, maskadd = consts
    h = _rmsnorm_fwd(x, sw["norm_mix_g"].reshape(1, D), "norm_mix_fwd")
    proj = _mm(h, lw["w_in"], tb=True, name="mm_in")
    gp = _mm(h, lw["w_gate"], tb=True, name="mm_gate")
    g_naq, g_nak, g_mq = _tile4(sw["na_q_norm_g"]), _tile4(sw["na_k_norm_g"]), _tile4(sw["mem_q_norm_g"])
    rq, rk, rv, nq, nk, nv, mq = _prep_fwd(proj, cos2, sin2, g_naq, g_nak, g_mq)

    lgf, lgb = jax.nn.log_sigmoid(sw["ret_decay_fwd"]), jax.nn.log_sigmoid(sw["ret_decay_bwd"])
    g_ret = sw["ret_norm_g"].reshape(1, BW)
    o_ret, ret = _ret_fwd(rq, rk, rv, proj, lgf, lgb, g_ret)

    wbd = _block_diag(sw["pool_w"]).astype(BF16)
    p_scale = sw["pool_scale"].reshape(1, BW)
    pool = _pool_fwd(proj, wbd, p_scale)

    rpb_pad = jnp.pad(sw["na_rpb"].reshape(NH * 15, 31), ((0, 4), (0, 97)))
    ball = _na_bias_table(_rpb_expand(rpb_pad, onehot), maskadd)
    na = _na_fwd(nq, nk, nv, ball)

    memn = _rmsnorm_fwd(mem, sw["norm_mem_g"].reshape(1, D), "norm_mem_fwd")
    kv = _mm(memn, lw["w_mem_kv"], name="mm_memkv")
    g_mk = _tile4(sw["mem_k_norm_g"])
    mk, mv = _memkv_prep(kv, g_mk)
    mo = _mem_fwd(mq, mk, mv)

    br = (ret, pool, na, mo)
    merged = _merge_fwd(br, lw["w_branch"], gp)
    x1 = _mm(merged, lw["w_out"], add=x, name="mm_out")
    h2 = _rmsnorm_fwd(x1, sw["norm_ffn_g"].reshape(1, D), "norm_ffn_fwd")
    ag = _mm(h2, lw["w_ffn_in"], tb=True, name="mm_ffn_in")
    yff = _swiglu_fwd(ag)
    x2 = _mm(yff, lw["w_ffn_out"], add=x1, name="mm_ffn_out")
    saved = dict(x=x, h=h, proj=proj, gp=gp, rq=rq, rk=rk, rv=rv, nq=nq, nk=nk, nv=nv, mq=mq, o_ret=o_ret, ball=ball, memn=memn,
                 kv=kv, mk=mk, mv=mv, br=br, merged=merged, x1=x1, h2=h2, ag=ag, yff=yff, lgf=lgf, lgb=lgb, wbd=wbd)
    return x2, saved


def _layer_bwd(dx2, mem, sw, lw, sv, consts, dep=None):
    cos2, sin2, onehot, maskadd = consts
    gb, gs = {}, {}
    dy = _mm(dx2, lw["w_ffn_out"], tb=True, dep=dep, name="mm_ffn_out_dx")
    gb["w_ffn_out"] = _mm(sv["yff"], dx2, ta=True, out_dtype=BF16, name="mm_ffn_out_dw")
    dag = _swiglu_bwd(sv["ag"], dy)
    dh2 = _mm(dag, lw["w_ffn_in"], name="mm_ffn_in_dx")
    gb["w_ffn_in"] = _mm(dag, sv["h2"], ta=True, out_dtype=BF16, name="mm_ffn_in_dw")
    dx1, dg = _rmsnorm_bwd(dh2, sv["x1"], sw["norm_ffn_g"].reshape(1, D), dx2, "norm_ffn_bwd")
    gs["norm_ffn_g"] = dg.reshape(D)

    dmerged = _mm(dx1, lw["w_out"], tb=True, name="mm_out_dx")
    gb["w_out"] = _mm(sv["merged"], dx1, ta=True, out_dtype=BF16, name="mm_out_dw")
    dgp, dup = _merge_bwd(dmerged, sv["br"], lw["w_branch"], sv["gp"])
    dbr = _dbranch(dup, lw["w_branch"])
    gb["w_branch"] = _dwbranch(sv["br"], dup)

    g_ret = sw["ret_norm_g"].reshape(1, BW)
    d_rq, d_rk, d_rv, d_rg, dg_ret, dlg = _ret_bwd(dbr, sv["o_ret"], sv["rq"], sv["rk"], sv["rv"], sv["proj"], sv["lgf"], sv["lgb"], g_ret)
    gs["ret_norm_g"] = dg_ret.reshape(BW)
    _, vjp_f = jax.vjp(jax.nn.log_sigmoid, sw["ret_decay_fwd"])
    _, vjp_b = jax.vjp(jax.nn.log_sigmoid, sw["ret_decay_bwd"])
    gs["ret_decay_fwd"] = vjp_f(dlg[0:NH, 0])[0]
    gs["ret_decay_bwd"] = vjp_b(dlg[NH:2 * NH, 0])[0]

    p_scale = sw["pool_scale"].reshape(1, BW)
    d_pv, dwbd, dscale = _pool_bwd(dbr, sv["proj"], sv["wbd"], p_scale)
    gs["pool_w"] = jnp.stack([dwbd[g * HD:(g + 1) * HD, g * HD:(g + 1) * HD] for g in range(NH)])
    gs["pool_scale"] = dscale.reshape(BW)

    d_nq, d_nk, d_nv, dball = _na_bwd(dbr, sv["nq"], sv["nk"], sv["nv"], sv["ball"])
    _, vjp_tab = jax.vjp(lambda tab: _na_bias_table(tab, maskadd), jnp.zeros((64, GRID_W * GRID_W), F32))
    drpb = _rpb_reduce(vjp_tab(dball)[0], onehot)
    gs["na_rpb"] = drpb[:NH * 15, :31].reshape(NH, 15, 31)

    d_mq, d_mk, d_mv = _mem_bwd(dbr, sv["mq"], sv["mk"], sv["mv"])
    g_mk = _tile4(sw["mem_k_norm_g"])
    dkv, dg_mk = _memkv_bwd(sv["kv"], d_mk, d_mv, g_mk)
    gs["mem_k_norm_g"] = dg_mk.reshape(NH, HD).sum(0)
    gb["w_mem_kv"] = _mm(sv["memn"], dkv, ta=True, out_dtype=BF16, name="mm_memkv_dw")
    dmemn = _mm(dkv, lw["w_mem_kv"], tb=True, name="mm_memkv_dx")
    _, dg_mem = _rmsnorm_bwd(dmemn, mem, sw["norm_mem_g"].reshape(1, D), jnp.zeros_like(mem), "norm_mem_bwd")
    gs["norm_mem_g"] = dg_mem.reshape(D)

    g_naq, g_nak, g_mq = _tile4(sw["na_q_norm_g"]), _tile4(sw["na_k_norm_g"]), _tile4(sw["mem_q_norm_g"])
    dproj, dg_naq, dg_nak, dg_mq = _prep_bwd(sv["proj"], cos2, sin2, g_naq, g_nak, g_mq, d_rq, d_rk, d_rv, d_rg, d_pv, d_nq, d_nk,
                                             d_nv, d_mq)
    gs["na_q_norm_g"] = dg_naq.reshape(NH, HD).sum(0)
    gs["na_k_norm_g"] = dg_nak.reshape(NH, HD).sum(0)
    gs["mem_q_norm_g"] = dg_mq.reshape(NH, HD).sum(0)

    dh = _mm(dproj, lw["w_in"], name="mm_in_dx")
    dh = _mm(dgp, lw["w_gate"], add=dh, name="mm_gate_dx")
    gb["w_in"] = _mm(dproj, sv["h"], ta=True, out_dtype=BF16, name="mm_in_dw")
    gb["w_gate"] = _mm(dgp, sv["h"], ta=True, out_dtype=BF16, name="mm_gate_dw")
    dx, dg = _rmsnorm_bwd(dh, sv["x"], sw["norm_mix_g"].reshape(1, D), dx1, "norm_mix_bwd")
    gs["norm_mix_g"] = dg.reshape(D)
    return dx, gb, gs


def _local_step(x, mem, target, small, get_layer, on_grads):
    t = x.shape[0]
    cos2, sin2 = _rotary_tables(t)
    onehot, maskadd = _na_constants()
    consts = (cos2, sin2, jnp.asarray(onehot), jnp.asarray(maskadd))
    saved, weights, cur = [], [], x
    for l in range(DEPTH):
        sw = {n: small[n][l] for n in SMALL}
        weights.append(get_layer(l, cur))
        cur, sv = _layer_fwd(cur, mem, sw, weights[l], consts)
        saved.append(sv)
    dy, loss_tile = _loss_head(cur, target)
    small_g = {n: [None] * DEPTH for n in SMALL}
    dep = None
    for l in reversed(range(DEPTH)):
        sw = {n: small[n][l] for n in SMALL}
        dy, gb, gs = _layer_bwd(dy, mem, sw, weights[l], saved[l], consts, dep)
        dep = on_grads(l, gb, dy)
        for n in SMALL:
            small_g[n][l] = gs[n]
    return loss_tile[0, 0], dy, {n: jnp.stack(v) for n, v in small_g.items()}


def _flat2d(a):
    return a.reshape(-1, a.shape[-1])


def kernel(x, mem, norm_mix_g, norm_mem_g, w_in, w_gate, ret_decay_fwd, ret_decay_bwd, ret_norm_g, pool_w, pool_scale, na_q_norm_g, na_k_norm_g, na_rpb, mem_q_norm_g, mem_k_norm_g, w_mem_kv, w_branch, w_out, norm_ffn_g, w_ffn_in, w_ffn_out, loss_target, m_norm_mix_g, m_norm_mem_g, m_w_in, m_w_gate, m_ret_decay_fwd, m_ret_decay_bwd, m_ret_norm_g, m_pool_w, m_pool_scale, m_na_q_norm_g, m_na_k_norm_g, m_na_rpb, m_mem_q_norm_g, m_mem_k_norm_g, m_w_mem_kv, m_w_branch, m_w_out, m_norm_ffn_g, m_w_ffn_in, m_w_ffn_out, v_norm_mix_g, v_norm_mem_g, v_w_in, v_w_gate, v_ret_decay_fwd, v_ret_decay_bwd, v_ret_norm_g, v_pool_w, v_pool_scale, v_na_q_norm_g, v_na_k_norm_g, v_na_rpb, v_mem_q_norm_g, v_mem_k_norm_g, v_w_mem_kv, v_w_branch, v_w_out, v_norm_ffn_g, v_w_ffn_in, v_w_ffn_out):
    w = dict(norm_mix_g=norm_mix_g, norm_mem_g=norm_mem_g, w_in=w_in, w_gate=w_gate, ret_decay_fwd=ret_decay_fwd,
             ret_decay_bwd=ret_decay_bwd, ret_norm_g=ret_norm_g, pool_w=pool_w, pool_scale=pool_scale, na_q_norm_g=na_q_norm_g,
             na_k_norm_g=na_k_norm_g, na_rpb=na_rpb, mem_q_norm_g=mem_q_norm_g, mem_k_norm_g=mem_k_norm_g, w_mem_kv=w_mem_kv,
             w_branch=w_branch, w_out=w_out, norm_ffn_g=norm_ffn_g, w_ffn_in=w_ffn_in, w_ffn_out=w_ffn_out)
    m = dict(norm_mix_g=m_norm_mix_g, norm_mem_g=m_norm_mem_g, w_in=m_w_in, w_gate=m_w_gate, ret_decay_fwd=m_ret_decay_fwd,
             ret_decay_bwd=m_ret_decay_bwd, ret_norm_g=m_ret_norm_g, pool_w=m_pool_w, pool_scale=m_pool_scale, na_q_norm_g=m_na_q_norm_g,
             na_k_norm_g=m_na_k_norm_g, na_rpb=m_na_rpb, mem_q_norm_g=m_mem_q_norm_g, mem_k_norm_g=m_mem_k_norm_g, w_mem_kv=m_w_mem_kv,
             w_branch=m_w_branch, w_out=m_w_out, norm_ffn_g=m_norm_ffn_g, w_ffn_in=m_w_ffn_in, w_ffn_out=m_w_ffn_out)
    v = dict(norm_mix_g=v_norm_mix_g, norm_mem_g=v_norm_mem_g, w_in=v_w_in, w_gate=v_w_gate, ret_decay_fwd=v_ret_decay_fwd,
             ret_decay_bwd=v_ret_decay_bwd, ret_norm_g=v_ret_norm_g, pool_w=v_pool_w, pool_scale=v_pool_scale, na_q_norm_g=v_na_q_norm_g,
             na_k_norm_g=v_na_k_norm_g, na_rpb=v_na_rpb, mem_q_norm_g=v_mem_q_norm_g, mem_k_norm_g=v_mem_k_norm_g, w_mem_kv=v_w_mem_kv,
             w_branch=v_w_branch, w_out=v_w_out, norm_ffn_g=v_norm_ffn_g, w_ffn_in=v_w_ffn_in, w_ffn_out=v_w_ffn_out)
    assert x.shape == (1, 2048, D) and mem.shape == (1, N_MEM, D) and w_in.shape == (DEPTH, D, 9 * BW // N_DEV)

    started = []
    for l in range(DEPTH):
        blocks = [_to_exchange(name, tr, w[name][l]).astype(BF16) for name, tr in BIG]
        lands = [lax.empty((N_DEV,) + b.shape, BF16) for b in blocks]
        started.append(_ici_start(blocks, lands, False, "gather_ici_start_%d" % l))
    all_started = started[0][4] + started[1][4] + started[2][4] + started[3][4]

    def get_layer(l, after):
        lands = _ici_wait(started[l], all_started if l == 0 else after, "gather_ici_wait_%d" % l)
        whole = _gather_d2d(started[l][2], lands, "gather_d2d")
        return {name: _whole_from_gathered(name, g) for (name, _), g in zip(BIG, whole)}

    cidx = lax.axis_index("c").astype(jnp.int32).reshape(1)
    chip = (2 * lax.axis_index("x") + lax.axis_index("y")).astype(jnp.int32).reshape(1)
    in_flight, g_layers = [], [None] * DEPTH

    def finish(l, st, after):
        recv = _ici_wait(st, after, "rs_ici_wait_%d" % l)
        sums = _sum_own(st[2], recv, chip, "rs_chip_sum")
        g_layers[l] = {name: _from_exchange(name, tr, s) for (name, tr), s in zip(BIG, sums)}

    def on_grads(l, gb, after):
        send = [_by_destination(name, gb[name]) for name, _ in BIG]
        from_core = _rs_core_swap(send, "rs_core_swap")
        chip_part = _pair_sum(send, from_core, cidx)
        st = _ici_start(chip_part, [lax.empty(p.shape, BF16) for p in chip_part], True, "rs_ici_start_%d" % l)
        if in_flight:
            finish(*in_flight.pop(), after)
        in_flight.append((l, st))
        if l == 0:
            finish(*in_flight.pop(), st[4])
        return st[4]

    loss_local, dx, small_g = _local_step(x[0], mem[0], loss_target[0], {n: w[n] for n in SMALL}, get_layer, on_grads)
    g_shard = {name: jnp.stack([g_layers[l][name] for l in range(DEPTH)]) for name, _ in BIG}

    small_all, = _all_gather([_pack_small(small_g, loss_local)], "gather_small")
    packed_g = _sum_slots(small_all, "small_sum")
    small_sum, loss = _unpack_small(packed_g, {n: w[n] for n in SMALL})

    grads, delta, new_m, new_v = {}, {}, {}, {}
    for name, _ in BIG:
        grads[name] = g_shard[name]
        d_, m_, v_ = _adamw(_flat2d(w[name]), _flat2d(grads[name]), _flat2d(m[name]), _flat2d(v[name]), "adamw_" + name)
        delta[name], new_m[name], new_v[name] = (a.reshape(w[name].shape) for a in (d_, m_, v_))
    d_, m_, v_ = _adamw(_pack_small({n: w[n] for n in SMALL}), packed_g, _pack_small({n: m[n] for n in SMALL}),
                        _pack_small({n: v[n] for n in SMALL}), "adamw_small")
    like = {n: w[n] for n in SMALL}
    ds, _ = _unpack_small(d_, like)
    ms, _ = _unpack_small(m_, like)
    vs, _ = _unpack_small(v_, like)
    for n in SMALL:
        grads[n], delta[n], new_m[n], new_v[n] = small_sum[n], ds[n], ms[n], vs[n]

    return (loss, dx[None], *[grads[n] for n in WEIGHTS], *[delta[n] for n in WEIGHTS], *[new_m[n] for n in WEIGHTS],
            *[new_v[n] for n in WEIGHTS])
```

```python
import functools

import numpy as np
import jax
import jax.numpy as jnp
from jax import lax
from jax.experimental import pallas as pl
from jax.experimental.pallas import tpu as pltpu

F32 = jnp.float32
BF16 = jnp.bfloat16
MXU = jnp.bfloat16
HI = lax.Precision.HIGHEST

DEPTH = 4
D = 1024
BW = 256
HD = 64
NH = 4
GRID_W = 64
NA_ROWS_WIN = 8
NA_COLS_WIN = 16
N_MEM = 256
FF = 2816
EPS = 1e-6
NEG = -1e30
ROPE_THETA = 10000.0
POOL_HALF_MAX = 8

ADAM_LR, ADAM_B1, ADAM_B2, ADAM_EPS, ADAM_WD, ADAM_STEP = 0.001, 0.9, 0.999, 1e-08, 0.01, 10

N_DEV = 8
VMEM_LIMIT = 56 * 1024 * 1024

RQ, RK, RV, RG, PV, NQ, NK, NV, MQ = range(9)

MESH = pl.DeviceIdType.MESH
ANY = pl.BlockSpec(memory_space=pl.ANY)
SMEM = pl.BlockSpec(memory_space=pltpu.SMEM)


def _cp(**kw):
    return pltpu.CompilerParams(vmem_limit_bytes=VMEM_LIMIT, **kw)


def _tile(n, cap):
    if n <= cap:
        return n
    best = None
    for t in range(128, cap + 1, 128):
        if n % t == 0:
            best = t
    assert best is not None, (n, cap)
    return best


def _sds(shape, dtype):
    return jax.ShapeDtypeStruct(shape, dtype)


def _lane_head(shape):
    return lax.shift_right_logical(lax.broadcasted_iota(jnp.int32, shape, len(shape) - 1), 6)


def _group_mean(z):
    i = lax.shift_right_logical(lax.broadcasted_iota(jnp.int32, (BW, BW), 0), 6)
    j = lax.shift_right_logical(lax.broadcasted_iota(jnp.int32, (BW, BW), 1), 6)
    g = jnp.where(i == j, 1.0 / HD, 0.0).astype(F32)
    return jnp.dot(z, g, precision=HI, preferred_element_type=F32)


def _gnorm(t, g):
    r = lax.rsqrt(_group_mean(t * t) + EPS)
    return t * r * g


def _gnorm_bwd(dy, t, g):
    r = lax.rsqrt(_group_mean(t * t) + EPS)
    th = t * r
    dth = dy * g
    dt = r * (dth - th * _group_mean(dth * th))
    return dt, dy * th


def _swap_halves(t):
    lane = lax.broadcasted_iota(jnp.int32, t.shape, 1)
    return jnp.where((lane & 63) < 32, pltpu.roll(t, BW - 32, 1), pltpu.roll(t, 32, 1))


def _sigmoid(x):
    return 1.0 / (1.0 + jnp.exp(-x))


def _dot(a, b, ta=False, tb=False):
    return lax.dot_general(a.astype(MXU), b.astype(MXU), (((0 if ta else 1,), (1 if tb else 0,)), ((), ())),
                           preferred_element_type=F32)


def _stack_heads(t):
    head = _lane_head(t.shape)
    return jnp.concatenate([jnp.where(head == h, t, jnp.zeros_like(t)) for h in range(NH)], axis=0)


def _unstack_heads(t, rows):
    head = _lane_head((rows, BW))
    out = jnp.zeros((rows, BW), F32)
    for h in range(NH):
        out = out + jnp.where(head == h, t[h * rows:(h + 1) * rows], 0.0)
    return out


def _softmax_rows(s):
    m = jnp.max(s, axis=-1, keepdims=True)
    e = jnp.exp(s - m)
    return e / jnp.sum(e, axis=-1, keepdims=True)


def _acc(ref, val, first):
    @pl.when(first)
    def _():
        ref[...] = val

    @pl.when(jnp.logical_not(first))
    def _():
        ref[...] += val


def _mm(a, b, *, ta=False, tb=False, out_dtype=F32, add=None, dep=None, name):
    m, k = (a.shape[1], a.shape[0]) if ta else a.shape
    n = b.shape[0] if tb else b.shape[1]
    tm, tn = _tile(m, 1024), _tile(n, 512)

    def body(*refs):
        if add is None:
            a_ref, b_ref, o_ref = refs[:2] + refs[-1:]
            r = _dot(a_ref[...], b_ref[...], ta, tb)
        else:
            a_ref, b_ref, c_ref, o_ref = refs[:3] + refs[-1:]
            r = _dot(a_ref[...], b_ref[...], ta, tb) + c_ref[...]
        o_ref[...] = r.astype(out_dtype)

    a_spec = pl.BlockSpec((k, tm), lambda i, j: (0, i)) if ta else pl.BlockSpec((tm, k), lambda i, j: (i, 0))
    b_spec = pl.BlockSpec((tn, k), lambda i, j: (j, 0)) if tb else pl.BlockSpec((k, tn), lambda i, j: (0, j))
    o_spec = pl.BlockSpec((tm, tn), lambda i, j: (i, j))
    ins, args = [a_spec, b_spec], [a, b]
    if add is not None:
        ins.append(o_spec)
        args.append(add)
    if dep is not None:
        ins.append(pl.BlockSpec((8, 128), lambda i, j: (0, 0)))
        args.append(dep)
    return pl.pallas_call(
        body, grid=(m // tm, n // tn), in_specs=ins, out_specs=o_spec, out_shape=_sds((m, n), out_dtype), name=name,
        compiler_params=_cp(dimension_semantics=("parallel", "parallel")))(*args)


def _rmsnorm_fwd(x, g, name):
    t, d = x.shape
    tm = _tile(t, 256)

    def body(x_ref, g_ref, o_ref):
        xv = x_ref[...]
        r = lax.rsqrt(jnp.mean(xv * xv, axis=-1, keepdims=True) + EPS)
        o_ref[...] = (xv * r * g_ref[...]).astype(o_ref.dtype)

    return pl.pallas_call(
        body, grid=(t // tm,), in_specs=[pl.BlockSpec((tm, d), lambda i: (i, 0)), pl.BlockSpec((1, d), lambda i: (0, 0))],
        out_specs=pl.BlockSpec((tm, d), lambda i: (i, 0)), out_shape=_sds((t, d), BF16), name=name, compiler_params=_cp())(x, g)


def _rmsnorm_bwd(dh, x, g, res, name):
    t, d = x.shape
    tm = _tile(t, 256)

    def body(dh_ref, x_ref, g_ref, res_ref, dx_ref, dg_ref):
        xv = x_ref[...]
        dhv = dh_ref[...]
        r = lax.rsqrt(jnp.mean(xv * xv, axis=-1, keepdims=True) + EPS)
        xh = xv * r
        dxh = dhv * g_ref[...]
        dx_ref[...] = res_ref[...] + r * (dxh - xh * jnp.mean(dxh * xh, axis=-1, keepdims=True))
        _acc(dg_ref, jnp.sum(dhv * xh, axis=0, keepdims=True), pl.program_id(0) == 0)

    row = pl.BlockSpec((tm, d), lambda i: (i, 0))
    vec = pl.BlockSpec((1, d), lambda i: (0, 0))
    return pl.pallas_call(
        body, grid=(t // tm,), in_specs=[row, row, vec, row], out_specs=(row, vec),
        out_shape=(_sds((t, d), F32), _sds((1, d), F32)), name=name, compiler_params=_cp())(dh, x, g, res)


def _prep_fwd(proj, cos2, sin2, g_naq, g_nak, g_mq):
    t = proj.shape[0]
    tm = 256

    def body(p_ref, cos_ref, sin_ref, gq_ref, gk_ref, gm_ref, rq_ref, rk_ref, rv_ref, nq_ref, nk_ref, nv_ref, mq_ref):
        def col(c):
            return p_ref[:, c * BW:(c + 1) * BW]

        cosv, sinv = cos_ref[...], sin_ref[...]

        def rot(tv):
            return tv * cosv + _swap_halves(tv) * sinv

        rq_ref[...] = (rot(col(RQ)) * (HD ** -0.5)).astype(BF16)
        rk_ref[...] = rot(col(RK)).astype(BF16)
        rv_ref[...] = col(RV).astype(BF16)
        nq_ref[...] = _gnorm(col(NQ), gq_ref[...]).astype(BF16)
        nk_ref[...] = _gnorm(col(NK), gk_ref[...]).astype(BF16)
        nv_ref[...] = col(NV).astype(BF16)
        mq_ref[...] = _gnorm(col(MQ), gm_ref[...]).astype(BF16)

    blk = pl.BlockSpec((tm, BW), lambda i: (i, 0))
    vec = pl.BlockSpec((1, BW), lambda i: (0, 0))
    return pl.pallas_call(
        body, grid=(t // tm,), in_specs=[pl.BlockSpec((tm, 9 * BW), lambda i: (i, 0)), blk, blk, vec, vec, vec],
        out_specs=tuple(blk for _ in range(7)), out_shape=tuple(_sds((t, BW), BF16) for _ in range(7)),
        name="prep_fwd", compiler_params=_cp())(proj, cos2, sin2, g_naq, g_nak, g_mq)


def _prep_bwd(proj, cos2, sin2, g_naq, g_nak, g_mq, d_rq, d_rk, d_rv, d_rg, d_pv, d_nq, d_nk, d_nv, d_mq):
    t = proj.shape[0]
    tm = 256

    def body(p_ref, cos_ref, sin_ref, gq_ref, gk_ref, gm_ref, drq_ref, drk_ref, drv_ref, drg_ref, dpv_ref, dnq_ref, dnk_ref,
             dnv_ref, dmq_ref, o_ref, dgq_ref, dgk_ref, dgm_ref):
        first = pl.program_id(0) == 0

        def col(c):
            return p_ref[:, c * BW:(c + 1) * BW]

        def put(c, v):
            o_ref[:, c * BW:(c + 1) * BW] = v.astype(BF16)

        cosv, sinv = cos_ref[...], sin_ref[...]

        def rot_t(dv):
            return dv * cosv + _swap_halves(dv * sinv)

        put(RQ, rot_t(drq_ref[...] * (HD ** -0.5)))
        put(RK, rot_t(drk_ref[...]))
        put(RV, drv_ref[...])
        put(RG, drg_ref[...])
        put(PV, dpv_ref[...])
        dq, gq = _gnorm_bwd(dnq_ref[...], col(NQ), gq_ref[...])
        put(NQ, dq)
        _acc(dgq_ref, jnp.sum(gq, axis=0, keepdims=True), first)
        dk, gk = _gnorm_bwd(dnk_ref[...], col(NK), gk_ref[...])
        put(NK, dk)
        _acc(dgk_ref, jnp.sum(gk, axis=0, keepdims=True), first)
        put(NV, dnv_ref[...])
        dm, gm = _gnorm_bwd(dmq_ref[...], col(MQ), gm_ref[...])
        put(MQ, dm)
        _acc(dgm_ref, jnp.sum(gm, axis=0, keepdims=True), first)

    blk = pl.BlockSpec((tm, BW), lambda i: (i, 0))
    vec = pl.BlockSpec((1, BW), lambda i: (0, 0))
    wide = pl.BlockSpec((tm, 9 * BW), lambda i: (i, 0))
    return pl.pallas_call(
        body, grid=(t // tm,), in_specs=[wide, blk, blk, vec, vec, vec] + [blk] * 9, out_specs=(wide, vec, vec, vec),
        out_shape=(_sds((t, 9 * BW), BF16), _sds((1, BW), F32), _sds((1, BW), F32), _sds((1, BW), F32)),
        name="prep_bwd", compiler_params=_cp())(proj, cos2, sin2, g_naq, g_nak, g_mq, d_rq, d_rk, d_rv, d_rg, d_pv, d_nq, d_nk,
                                                d_nv, d_mq)


RET_TQ = 64


def _ret_decay(i, tq, t, lgf_ref, lgb_ref):
    rows = NH * tq
    n = i * tq + (lax.broadcasted_iota(jnp.int32, (rows, 1), 0) & (tq - 1))
    m = lax.broadcasted_iota(jnp.int32, (1, t), 1)
    diff = n - m
    causal = diff >= 0
    dist = jnp.abs(diff).astype(F32)
    lgf = jnp.concatenate([jnp.full((tq, 1), lgf_ref[h], F32) for h in range(NH)], axis=0)
    lgb = jnp.concatenate([jnp.full((tq, 1), lgb_ref[h], F32) for h in range(NH)], axis=0)
    return causal, dist, jnp.exp(dist * jnp.where(causal, lgf, lgb))


def _ret_fwd(q, k, v, proj, lgf, lgb, g_ret):
    t = q.shape[0]
    tq = RET_TQ

    def body(lgf_ref, lgb_ref, q_ref, k_ref, v_ref, rg_ref, g_ref, o_ref, ret_ref):
        i = pl.program_id(0)
        qs = _stack_heads(q_ref[...])
        s = _dot(qs, k_ref[...], tb=True)
        _, _, dm = _ret_decay(i, tq, t, lgf_ref, lgb_ref)
        o = _unstack_heads(_dot(s * dm, v_ref[...]), tq)
        o_ref[...] = o
        rg = rg_ref[...]
        ret_ref[...] = (_gnorm(o, g_ref[...]) * (rg * _sigmoid(rg))).astype(BF16)

    blk = pl.BlockSpec((tq, BW), lambda i: (i, 0))
    whole = pl.BlockSpec((t, BW), lambda i: (0, 0))
    return pl.pallas_call(
        body, grid=(t // tq,),
        in_specs=[SMEM, SMEM, blk, whole, whole, pl.BlockSpec((tq, BW), lambda i: (i, RG)), pl.BlockSpec((1, BW), lambda i: (0, 0))],
        out_specs=(blk, blk), out_shape=(_sds((t, BW), F32), _sds((t, BW), BF16)), name="ret_fwd",
        compiler_params=_cp())(lgf, lgb, q, k, v, proj, g_ret)


def _ret_bwd(dbr, o_ret, q, k, v, proj, lgf, lgb, g_ret):
    t = q.shape[0]
    tq = RET_TQ
    nblk = t // tq

    def body(lgf_ref, lgb_ref, d_ref, o_ref, q_ref, k_ref, v_ref, rg_ref, g_ref,
             dq_ref, dk_ref, dv_ref, drg_ref, dg_ref, dlg_ref, accf_ref, accb_ref):
        i = pl.program_id(0)
        first = i == 0
        dret, o, rg, g = d_ref[...], o_ref[...], rg_ref[...], g_ref[...]
        sg = _sigmoid(rg)
        dy = dret * (rg * sg)
        do, dgain = _gnorm_bwd(dy, o, g)
        drg_ref[...] = dret * _gnorm(o, g) * (sg * (1.0 + rg * (1.0 - sg)))
        _acc(dg_ref, jnp.sum(dgain, axis=0, keepdims=True), first)

        dos = _stack_heads(do).astype(MXU)
        qs = _stack_heads(q_ref[...])
        kv, vv = k_ref[...], v_ref[...]
        s = _dot(qs, kv, tb=True)
        causal, dist, dm = _ret_decay(i, tq, t, lgf_ref, lgb_ref)
        da = _dot(dos, vv, tb=True)
        _acc(dv_ref, _dot(s * dm, dos, ta=True), first)
        ds = da * dm
        w = ds * s * dist
        _acc(accf_ref, jnp.sum(jnp.where(causal, w, 0.0), axis=1, keepdims=True), first)
        _acc(accb_ref, jnp.sum(jnp.where(causal, 0.0, w), axis=1, keepdims=True), first)
        dsb = ds.astype(MXU)
        dq_ref[...] = _unstack_heads(_dot(dsb, kv), tq)
        _acc(dk_ref, _dot(dsb, qs, ta=True), first)

        @pl.when(i == nblk - 1)
        def _():
            for h in range(NH):
                dlg_ref[h:h + 1, :] = jnp.full((1, 128), jnp.sum(accf_ref[h * tq:(h + 1) * tq, :]), F32)
                dlg_ref[NH + h:NH + h + 1, :] = jnp.full((1, 128), jnp.sum(accb_ref[h * tq:(h + 1) * tq, :]), F32)

    blk = pl.BlockSpec((tq, BW), lambda i: (i, 0))
    whole = pl.BlockSpec((t, BW), lambda i: (0, 0))
    vec = pl.BlockSpec((1, BW), lambda i: (0, 0))
    return pl.pallas_call(
        body, grid=(nblk,),
        in_specs=[SMEM, SMEM, blk, blk, blk, whole, whole, pl.BlockSpec((tq, BW), lambda i: (i, RG)), vec],
        out_specs=(blk, whole, whole, blk, vec, pl.BlockSpec((2 * NH, 128), lambda i: (0, 0))),
        out_shape=(_sds((t, BW), F32), _sds((t, BW), F32), _sds((t, BW), F32), _sds((t, BW), F32), _sds((1, BW), F32),
                   _sds((2 * NH, 128), F32)),
        scratch_shapes=[pltpu.VMEM((NH * tq, 1), F32), pltpu.VMEM((NH * tq, 1), F32)], name="ret_bwd",
        compiler_params=_cp())(lgf, lgb, dbr, o_ret, q, k, v, proj, g_ret)


def _pool_windows(t):
    row = lax.broadcasted_iota(jnp.int32, (t, BW), 0)
    half = lax.shift_left(jnp.ones((t, BW), jnp.int32), _lane_head((t, BW)))
    cnt = (jnp.minimum(row + half, t) - jnp.maximum(row - half, 0)).astype(F32)
    return row, half, cnt


def _pool_window_sum(v, row, half, t, transpose):
    out = jnp.zeros_like(v)
    for j in range(-POOL_HALF_MAX, POOL_HALF_MAX):
        src = row - j if transpose else row + j
        ok = (src >= 0) & (src < t) & (j >= -half) & (j < half)
        out = out + jnp.where(ok, pltpu.roll(v, (j if transpose else -j) % t, 0), 0.0)
    return out


def _pool_fwd(proj, wbd, scale):
    t = proj.shape[0]

    def body(v_ref, w_ref, s_ref, o_ref):
        v = v_ref[...]
        row, half, cnt = _pool_windows(t)
        pooled = _pool_window_sum(v, row, half, t, False) / cnt - v
        o_ref[...] = (_dot(pooled, w_ref[...]) * s_ref[...]).astype(BF16)

    return pl.pallas_call(
        body, grid=(1,),
        in_specs=[pl.BlockSpec((t, BW), lambda i: (0, PV)), pl.BlockSpec((BW, BW), lambda i: (0, 0)), pl.BlockSpec((1, BW), lambda i: (0, 0))],
        out_specs=pl.BlockSpec((t, BW), lambda i: (0, 0)), out_shape=_sds((t, BW), BF16), name="pool_fwd",
        compiler_params=_cp())(proj, wbd, scale)


def _pool_bwd(dbr, proj, wbd, scale):
    t = proj.shape[0]

    def body(d_ref, v_ref, w_ref, s_ref, dv_ref, dw_ref, ds_ref):
        v, dout = v_ref[...], d_ref[...]
        row, half, cnt = _pool_windows(t)
        pooled = _pool_window_sum(v, row, half, t, False) / cnt - v
        mixed = _dot(pooled, w_ref[...])
        ds_ref[...] = jnp.sum(dout * mixed, axis=0, keepdims=True)
        dmixed = dout * s_ref[...]
        dw_ref[...] = _dot(pooled, dmixed, ta=True)
        dpooled = _dot(dmixed, w_ref[...], tb=True)
        dv_ref[...] = _pool_window_sum(dpooled / cnt, row, half, t, True) - dpooled

    return pl.pallas_call(
        body, grid=(1,),
        in_specs=[pl.BlockSpec((t, BW), lambda i: (0, 1)), pl.BlockSpec((t, BW), lambda i: (0, PV)),
                  pl.BlockSpec((BW, BW), lambda i: (0, 0)), pl.BlockSpec((1, BW), lambda i: (0, 0))],
        out_specs=(pl.BlockSpec((t, BW), lambda i: (0, 0)), pl.BlockSpec((BW, BW), lambda i: (0, 0)), pl.BlockSpec((1, BW), lambda i: (0, 0))),
        out_shape=(_sds((t, BW), F32), _sds((BW, BW), F32), _sds((1, BW), F32)), name="pool_bwd",
        compiler_params=_cp())(dbr, proj, wbd, scale)


NA_KEYS = NA_ROWS_WIN * GRID_W


def _na_window(r, n_rows):
    rs = jnp.clip(r - NA_ROWS_WIN // 2, 0, n_rows - NA_ROWS_WIN)
    return pl.multiple_of(rs * GRID_W, GRID_W), rs - r + (NA_ROWS_WIN - 1)


def _na_fwd(q, k, v, ball):
    t = q.shape[0]
    n_rows = t // GRID_W

    def body(q_ref, k_ref, v_ref, b_ref, o_ref):
        start, a0 = _na_window(pl.program_id(0), n_rows)
        qs = _stack_heads(q_ref[...])
        s = _dot(qs, k_ref[pl.ds(start, NA_KEYS), :], tb=True) * (HD ** -0.5) + b_ref[a0]
        p = _softmax_rows(s)
        o_ref[...] = _unstack_heads(_dot(p, v_ref[pl.ds(start, NA_KEYS), :]), GRID_W).astype(BF16)

    blk = pl.BlockSpec((GRID_W, BW), lambda r: (r, 0))
    whole = pl.BlockSpec((t, BW), lambda r: (0, 0))
    return pl.pallas_call(
        body, grid=(n_rows,), in_specs=[blk, whole, whole, pl.BlockSpec(ball.shape, lambda r: (0, 0, 0))],
        out_specs=blk, out_shape=_sds((t, BW), BF16), name="na_fwd", compiler_params=_cp())(q, k, v, ball)


def _na_bwd(dbr, q, k, v, ball):
    t = q.shape[0]
    n_rows = t // GRID_W

    def body(d_ref, q_ref, k_ref, v_ref, b_ref, dq_ref, dk_ref, dv_ref, db_ref):
        r = pl.program_id(0)
        start, a0 = _na_window(r, n_rows)
        keys = pl.ds(start, NA_KEYS)

        @pl.when(r == 0)
        def _():
            dk_ref[...] = jnp.zeros_like(dk_ref)
            dv_ref[...] = jnp.zeros_like(dv_ref)
            db_ref[...] = jnp.zeros_like(db_ref)

        qs = _stack_heads(q_ref[...])
        kb, vb = k_ref[keys, :], v_ref[keys, :]
        p = _softmax_rows(_dot(qs, kb, tb=True) * (HD ** -0.5) + b_ref[a0])
        dos = _stack_heads(d_ref[...]).astype(MXU)
        dp = _dot(dos, vb, tb=True)
        dv_ref[keys, :] += _dot(p, dos, ta=True)
        ds = p * (dp - jnp.sum(dp * p, axis=-1, keepdims=True))
        db_ref[a0] += ds
        dsb = (ds * (HD ** -0.5)).astype(MXU)
        dq_ref[...] = _unstack_heads(_dot(dsb, kb), GRID_W)
        dk_ref[keys, :] += _dot(dsb, qs, ta=True)

    blk = pl.BlockSpec((GRID_W, BW), lambda r: (r, 0))
    whole = pl.BlockSpec((t, BW), lambda r: (0, 0))
    tab = pl.BlockSpec(ball.shape, lambda r: (0, 0, 0))
    return pl.pallas_call(
        body, grid=(n_rows,), in_specs=[pl.BlockSpec((GRID_W, BW), lambda r: (r, 2)), blk, whole, whole, tab],
        out_specs=(blk, whole, whole, tab),
        out_shape=(_sds((t, BW), F32), _sds((t, BW), F32), _sds((t, BW), F32), _sds(ball.shape, F32)), name="na_bwd",
        compiler_params=_cp())(dbr, q, k, v, ball)


def _rpb_expand(rpb_pad, onehot):
    def body(r_ref, e_ref, o_ref):
        o_ref[...] = jnp.dot(r_ref[...], e_ref[...], precision=HI, preferred_element_type=F32)

    return pl.pallas_call(body, out_shape=_sds((64, GRID_W * GRID_W), F32), name="rpb_expand", compiler_params=_cp())(rpb_pad, onehot)


def _rpb_reduce(dtab, onehot):
    def body(d_ref, e_ref, o_ref):
        o_ref[...] = lax.dot_general(d_ref[...], e_ref[...], (((1,), (1,)), ((), ())), precision=HI, preferred_element_type=F32)

    return pl.pallas_call(body, out_shape=_sds((64, 128), F32), name="rpb_reduce", compiler_params=_cp())(dtab, onehot)


MEM_TQ = 256


def _mem_fwd(q, mk, mv):
    t = q.shape[0]
    tq = MEM_TQ

    def body(q_ref, k_ref, v_ref, o_ref):
        qv = q_ref[...]
        head = _lane_head(qv.shape)
        out = jnp.zeros((tq, BW), F32)
        for h in range(NH):
            p = _softmax_rows(_dot(jnp.where(head == h, qv, jnp.zeros_like(qv)), k_ref[...], tb=True) * (HD ** -0.5))
            out = out + jnp.where(head == h, _dot(p, v_ref[...]), 0.0)
        o_ref[...] = out.astype(BF16)

    blk = pl.BlockSpec((tq, BW), lambda i: (i, 0))
    kv = pl.BlockSpec((N_MEM, BW), lambda i: (0, 0))
    return pl.pallas_call(body, grid=(t // tq,), in_specs=[blk, kv, kv], out_specs=blk, out_shape=_sds((t, BW), BF16),
                          name="mem_fwd", compiler_params=_cp())(q, mk, mv)


def _mem_bwd(dbr, q, mk, mv):
    t = q.shape[0]
    tq = MEM_TQ

    def body(d_ref, q_ref, k_ref, v_ref, dq_ref, dk_ref, dv_ref):
        first = pl.program_id(0) == 0
        qv, dout = q_ref[...], d_ref[...]
        head = _lane_head(qv.shape)
        dq = jnp.zeros((tq, BW), F32)
        dk = jnp.zeros((N_MEM, BW), F32)
        dv = jnp.zeros((N_MEM, BW), F32)
        for h in range(NH):
            qh = jnp.where(head == h, qv, jnp.zeros_like(qv))
            doh = jnp.where(head == h, dout, 0.0).astype(MXU)
            p = _softmax_rows(_dot(qh, k_ref[...], tb=True) * (HD ** -0.5))
            dp = _dot(doh, v_ref[...], tb=True)
            dv = dv + _dot(p, doh, ta=True)
            dsb = (p * (dp - jnp.sum(dp * p, axis=-1, keepdims=True)) * (HD ** -0.5)).astype(MXU)
            dq = dq + jnp.where(head == h, _dot(dsb, k_ref[...]), 0.0)
            dk = dk + _dot(dsb, qh, ta=True)
        dq_ref[...] = dq
        _acc(dk_ref, dk, first)
        _acc(dv_ref, dv, first)

    blk = pl.BlockSpec((tq, BW), lambda i: (i, 0))
    kv = pl.BlockSpec((N_MEM, BW), lambda i: (0, 0))
    return pl.pallas_call(
        body, grid=(t // tq,), in_specs=[pl.BlockSpec((tq, BW), lambda i: (i, 3)), blk, kv, kv], out_specs=(blk, kv, kv),
        out_shape=(_sds((t, BW), F32), _sds((N_MEM, BW), F32), _sds((N_MEM, BW), F32)), name="mem_bwd",
        compiler_params=_cp())(dbr, q, mk, mv)


def _memkv_prep(kv, g_mk):
    def body(kv_ref, g_ref, k_ref, v_ref):
        k_ref[...] = _gnorm(kv_ref[:, 0:BW], g_ref[...]).astype(BF16)
        v_ref[...] = kv_ref[:, BW:2 * BW].astype(BF16)

    return pl.pallas_call(body, out_shape=(_sds((N_MEM, BW), BF16), _sds((N_MEM, BW), BF16)), name="memkv_prep",
                          compiler_params=_cp())(kv, g_mk)


def _memkv_bwd(kv, dk, dv, g_mk):
    def body(kv_ref, dk_ref, dv_ref, g_ref, o_ref, dg_ref):
        dkk, gain = _gnorm_bwd(dk_ref[...], kv_ref[:, 0:BW], g_ref[...])
        o_ref[:, 0:BW] = dkk.astype(BF16)
        o_ref[:, BW:2 * BW] = dv_ref[...].astype(BF16)
        dg_ref[...] = jnp.sum(gain, axis=0, keepdims=True)

    return pl.pallas_call(body, out_shape=(_sds((N_MEM, 2 * BW), BF16), _sds((1, BW), F32)), name="memkv_bwd",
                          compiler_params=_cp())(kv, dk, dv, g_mk)


MERGE_TM = 256


def _merge_fwd(brs, wbt, gp):
    t = gp.shape[0]
    tm = MERGE_TM

    def body(b0, b1, b2, b3, wb_ref, gp_ref, o_ref):
        out = jnp.zeros((tm, D), F32)
        for n, b_ref in enumerate((b0, b1, b2, b3)):
            up = _dot(b_ref[...], wb_ref[n], tb=True)
            out = out + _sigmoid(gp_ref[:, n * D:(n + 1) * D]) * up
        o_ref[...] = out.astype(BF16)

    blk = pl.BlockSpec((tm, BW), lambda i: (i, 0))
    return pl.pallas_call(
        body, grid=(t // tm,),
        in_specs=[blk, blk, blk, blk, pl.BlockSpec((NH, D, BW), lambda i: (0, 0, 0)), pl.BlockSpec((tm, NH * D), lambda i: (i, 0))],
        out_specs=pl.BlockSpec((tm, D), lambda i: (i, 0)), out_shape=_sds((t, D), BF16), name="merge_fwd",
        compiler_params=_cp())(*brs, wbt, gp)


def _merge_bwd(dmerged, brs, wbt, gp):
    t = gp.shape[0]
    tm = MERGE_TM

    def body(d_ref, b0, b1, b2, b3, wb_ref, gp_ref, dgp_ref, dup_ref):
        dm = d_ref[...]
        for n, b_ref in enumerate((b0, b1, b2, b3)):
            up = _dot(b_ref[...], wb_ref[n], tb=True)
            g = _sigmoid(gp_ref[:, n * D:(n + 1) * D])
            dgp_ref[:, n * D:(n + 1) * D] = (dm * up * (g * (1.0 - g))).astype(BF16)
            dup_ref[:, n * D:(n + 1) * D] = (dm * g).astype(BF16)

    row = pl.BlockSpec((tm, D), lambda i: (i, 0))
    blk = pl.BlockSpec((tm, BW), lambda i: (i, 0))
    wide = pl.BlockSpec((tm, NH * D), lambda i: (i, 0))
    return pl.pallas_call(
        body, grid=(t // tm,), in_specs=[row, blk, blk, blk, blk, pl.BlockSpec((NH, D, BW), lambda i: (0, 0, 0)), wide],
        out_specs=(wide, wide), out_shape=(_sds((t, NH * D), BF16), _sds((t, NH * D), BF16)), name="merge_bwd",
        compiler_params=_cp())(dmerged, *brs, wbt, gp)


def _dbranch(dup, wbt):
    t = dup.shape[0]
    tm = 512

    def body(d_ref, w_ref, o_ref):
        o_ref[...] = _dot(d_ref[...], w_ref[...])

    return pl.pallas_call(
        body, grid=(t // tm, NH), in_specs=[pl.BlockSpec((tm, D), lambda i, n: (i, n)), pl.BlockSpec((None, D, BW), lambda i, n: (n, 0, 0))],
        out_specs=pl.BlockSpec((tm, BW), lambda i, n: (i, n)), out_shape=_sds((t, NH * BW), F32), name="dbranch",
        compiler_params=_cp())(dup, wbt)


def _dwbranch(brs, dup):
    t = dup.shape[0]

    def body(b0, b1, b2, b3, d_ref, o_ref):
        for n, b_ref in enumerate((b0, b1, b2, b3)):
            o_ref[n] = _dot(d_ref[:, n * D:(n + 1) * D], b_ref[...], ta=True).astype(BF16)

    return pl.pallas_call(body, out_shape=_sds((NH, D, BW), BF16), name="dwbranch", compiler_params=_cp())(*brs, dup)


def _swiglu_fwd(ag):
    t = ag.shape[0]
    tm = 256

    def body(ag_ref, o_ref):
        a, g = ag_ref[:, 0:FF], ag_ref[:, FF:2 * FF]
        o_ref[...] = (a * _sigmoid(a) * g).astype(BF16)

    return pl.pallas_call(body, grid=(t // tm,), in_specs=[pl.BlockSpec((tm, 2 * FF), lambda i: (i, 0))],
                          out_specs=pl.BlockSpec((tm, FF), lambda i: (i, 0)), out_shape=_sds((t, FF), BF16), name="swiglu_fwd",
                          compiler_params=_cp())(ag)


def _swiglu_bwd(ag, dy):
    t = ag.shape[0]
    tm = 256

    def body(ag_ref, dy_ref, o_ref):
        a, g, d = ag_ref[:, 0:FF], ag_ref[:, FF:2 * FF], dy_ref[...]
        s = _sigmoid(a)
        o_ref[:, 0:FF] = (d * g * (s * (1.0 + a * (1.0 - s)))).astype(BF16)
        o_ref[:, FF:2 * FF] = (d * (a * s)).astype(BF16)

    return pl.pallas_call(
        body, grid=(t // tm,), in_specs=[pl.BlockSpec((tm, 2 * FF), lambda i: (i, 0)), pl.BlockSpec((tm, FF), lambda i: (i, 0))],
        out_specs=pl.BlockSpec((tm, 2 * FF), lambda i: (i, 0)), out_shape=_sds((t, 2 * FF), BF16), name="swiglu_bwd",
        compiler_params=_cp())(ag, dy)


def _loss_head(y, target):
    t, d = y.shape
    tm = 256

    def body(y_ref, t_ref, dy_ref, l_ref):
        e = y_ref[...] - t_ref[...]
        dy_ref[...] = e * (1.0 / d)
        _acc(l_ref, jnp.full((8, 128), 0.5 * jnp.sum(jnp.sum(e * e, axis=-1, keepdims=True) * (1.0 / d)), F32), pl.program_id(0) == 0)

    row = pl.BlockSpec((tm, d), lambda i: (i, 0))
    return pl.pallas_call(body, grid=(t // tm,), in_specs=[row, row], out_specs=(row, pl.BlockSpec((8, 128), lambda i: (0, 0))),
                          out_shape=(_sds((t, d), F32), _sds((8, 128), F32)), name="loss_head", compiler_params=_cp())(y, target)


def _sum_slots(x, name):
    k, r, c = x.shape
    tr = _tile(r, 512) if r % 128 == 0 else r

    def body(x_ref, o_ref):
        acc = x_ref[0].astype(F32)
        for s in range(1, k):
            acc = acc + x_ref[s].astype(F32)
        o_ref[...] = acc

    return pl.pallas_call(body, grid=(r // tr,), in_specs=[pl.BlockSpec((k, tr, c), lambda i: (0, i, 0))],
                          out_specs=pl.BlockSpec((tr, c), lambda i: (i, 0)), out_shape=_sds((r, c), F32), name=name,
                          compiler_params=_cp())(x)


def _pair_sum(bufs, recvs, cidx):
    n = len(bufs)

    def body(c_ref, *refs):
        for i in range(n):
            refs[2 * n + i][...] = (refs[i][...].astype(F32) + refs[n + i][...].astype(F32)).astype(BF16)

    return pl.pallas_call(
        body,
        grid_spec=pltpu.PrefetchScalarGridSpec(
            num_scalar_prefetch=1, grid=(4,),
            in_specs=[pl.BlockSpec((None, None) + b.shape[2:], lambda s, cref: (s, cref[0], 0, 0)) for b in bufs]
            + [pl.BlockSpec((None,) + r.shape[1:], lambda s, cref: (s, 0, 0)) for r in recvs],
            out_specs=tuple(pl.BlockSpec((None,) + r.shape[1:], lambda s, cref: (s, 0, 0)) for r in recvs)),
        out_shape=tuple(_sds(r.shape, BF16) for r in recvs), name="rs_pair_sum", compiler_params=_cp())(cidx, *bufs, *recvs)


def _adamw(w, g, m, v, name):
    r, c = w.shape
    tr = r
    if r > 1024:
        tr = next(cand for cand in (512, 256, 128, 64, 32, 16, 8) if r % cand == 0)

    def body(w_ref, g_ref, m_ref, v_ref, d_ref, nm_ref, nv_ref):
        gv = g_ref[...]
        mn = ADAM_B1 * m_ref[...] + (1.0 - ADAM_B1) * gv
        vn = ADAM_B2 * v_ref[...] + (1.0 - ADAM_B2) * (gv * gv)
        m_hat = mn / (1.0 - ADAM_B1 ** ADAM_STEP)
        v_hat = vn / (1.0 - ADAM_B2 ** ADAM_STEP)
        d_ref[...] = -ADAM_LR * (m_hat / (jnp.sqrt(v_hat) + ADAM_EPS) + ADAM_WD * w_ref[...])
        nm_ref[...] = mn
        nv_ref[...] = vn

    blk = pl.BlockSpec((tr, c), lambda i: (i, 0))
    return pl.pallas_call(body, grid=(r // tr,), in_specs=[blk] * 4, out_specs=(blk,) * 3,
                          out_shape=tuple(_sds((r, c), F32) for _ in range(3)), name=name, compiler_params=_cp())(w, g, m, v)


def _all_gather(shards, name):
    n = len(shards)

    def body(*refs):
        x_refs, out_refs = refs[:n], refs[n:2 * n]
        send_sems, recv_sems, local_sems = refs[2 * n:]
        x, y, cc = lax.axis_index("x"), lax.axis_index("y"), lax.axis_index("c")
        me, sibling = (x, y, cc), (x, y, 1 - cc)
        chips = [(1 - x, y), (x, 1 - y), (1 - x, 1 - y)]

        def copy(i, k, block, to, own=False):
            px, py, pc = block
            slot = out_refs[i].at[4 * px + 2 * py + pc]
            return pltpu.make_async_remote_copy(
                src_ref=x_refs[i] if own else slot, dst_ref=slot, send_sem=send_sems.at[7 * i + k],
                recv_sem=recv_sems.at[7 * i + k], device_id=to, device_id_type=MESH)

        mine = [pltpu.make_async_copy(x_refs[i], out_refs[i].at[4 * x + 2 * y + cc], local_sems.at[i]) for i in range(n)]
        for cp in mine:
            cp.start()
        first = []
        for j, chip in enumerate(chips):
            first += [copy(i, 1 + j, me, (*chip, cc), own=True) for i in range(n)]
        first += [copy(i, 0, me, sibling, own=True) for i in range(n)]
        for cp in first:
            cp.start()
        passed = []
        for j, chip in enumerate(chips):
            for i in range(n):
                copy(i, 1 + j, (*chip, cc), me).wait_recv()
                cp = copy(i, 4 + j, (*chip, cc), sibling)
                cp.start()
                passed.append(cp)
        for i in range(n):
            copy(i, 0, sibling, me).wait_recv()
        for j, chip in enumerate(chips):
            for i in range(n):
                copy(i, 4 + j, (*chip, 1 - cc), me).wait_recv()
        for cp in first + passed:
            cp.wait_send()
        for cp in mine:
            cp.wait()

    return pl.pallas_call(
        body, out_shape=tuple(_sds((N_DEV,) + s.shape, s.dtype) for s in shards), in_specs=[ANY] * n, out_specs=(ANY,) * n,
        scratch_shapes=[pltpu.SemaphoreType.DMA((7 * n,)), pltpu.SemaphoreType.DMA((7 * n,)), pltpu.SemaphoreType.DMA((n,))],
        name=name)(*shards)


def _rs_core_swap(bufs, name):
    n = len(bufs)

    def body(*refs):
        b_refs, recv_refs = refs[:n], refs[n:2 * n]
        send_sems, recv_sems = refs[2 * n:]
        x, y, cc = lax.axis_index("x"), lax.axis_index("y"), lax.axis_index("c")
        copies = [pltpu.make_async_remote_copy(
            src_ref=b_refs[i].at[s, 1 - cc], dst_ref=recv_refs[i].at[s], send_sem=send_sems.at[4 * i + s],
            recv_sem=recv_sems.at[4 * i + s], device_id=(x, y, 1 - cc), device_id_type=MESH) for i in range(n) for s in range(4)]
        for cp in copies:
            cp.start()
        for cp in copies:
            cp.wait()

    return pl.pallas_call(
        body, out_shape=tuple(_sds((4,) + b.shape[2:], b.dtype) for b in bufs), in_specs=[ANY] * n, out_specs=(ANY,) * n,
        scratch_shapes=[pltpu.SemaphoreType.DMA((4 * n,)), pltpu.SemaphoreType.DMA((4 * n,))], name=name)(*bufs)


HBM = pl.BlockSpec(memory_space=pltpu.HBM)
SEMS = pl.BlockSpec(memory_space=pltpu.SEMAPHORE)
EFFECT = pltpu.SideEffectType.DATAFLOW_SIDE_EFFECTING


def _hbm(a):
    return pltpu.HBM(a.shape, a.dtype)


def _other_chips(x, y):
    return [(1 - x, y), (x, 1 - y), (1 - x, 1 - y)]


def _ici_start(srcs, lands, by_chip, name):
    n = len(srcs)

    def body(*refs):
        s_refs, land_refs = refs[:n], refs[n:2 * n]
        send_sems, recv_sems = refs[2 * n], refs[2 * n + 1]
        token = refs[-1]
        x, y, cc = lax.axis_index("x"), lax.axis_index("y"), lax.axis_index("c")
        mine = 2 * x + y if by_chip else 4 * x + 2 * y + cc
        for px, py in _other_chips(x, y):
            for i in range(n):
                pltpu.make_async_remote_copy(
                    src_ref=s_refs[i].at[2 * px + py] if by_chip else s_refs[i], dst_ref=land_refs[i].at[mine],
                    send_sem=send_sems.at[i], recv_sem=recv_sems.at[i], device_id=(px, py, cc), device_id_type=MESH).start()
        token[...] = jnp.zeros_like(token)

    out = pl.pallas_call(
        body, name=name,
        out_shape=(pltpu.SemaphoreType.DMA((n,)), pltpu.SemaphoreType.DMA((n,)), *[_hbm(s) for s in srcs], *[_hbm(l) for l in lands],
                   _sds((8, 128), F32)),
        in_specs=[HBM] * (2 * n), out_specs=(SEMS, SEMS, *[HBM] * (2 * n), pl.BlockSpec(memory_space=pltpu.VMEM)),
        input_output_aliases={i: 2 + i for i in range(2 * n)}, compiler_params=pltpu.CompilerParams(has_side_effects=EFFECT),
    )(*[pltpu.with_memory_space_constraint(s, pltpu.HBM) for s in srcs],
      *[pltpu.with_memory_space_constraint(l, pltpu.HBM) for l in lands])
    return out[0], out[1], out[2:2 + n], out[2 + n:2 + 2 * n], out[-1]


def _ici_wait(started, after, name):
    send_sems, recv_sems, srcs, lands, _ = started
    n = len(srcs)

    def body(*refs):
        land_refs = refs[n:2 * n]
        send_sems, recv_sems = refs[2 * n], refs[2 * n + 1]
        x, y, cc = lax.axis_index("x"), lax.axis_index("y"), lax.axis_index("c")
        for i in range(n):
            three = land_refs[i].at[pl.ds(0, 3)]
            cp = pltpu.make_async_remote_copy(src_ref=three, dst_ref=three, send_sem=send_sems.at[i], recv_sem=recv_sems.at[i],
                                              device_id=(x, y, cc), device_id_type=MESH)
            cp.wait_send()
            cp.wait_recv()

    return pl.pallas_call(
        body, name=name, out_shape=tuple(_hbm(l) for l in lands), in_specs=[HBM] * (2 * n) + [SEMS, SEMS, ANY],
        out_specs=tuple([HBM] * n), input_output_aliases={n + i: i for i in range(n)},
        compiler_params=pltpu.CompilerParams(has_side_effects=EFFECT))(*srcs, *lands, send_sems, recv_sems, after)


def _gather_d2d(blocks, lands, name):
    n = len(blocks)

    def body(*refs):
        x_refs, land_refs = refs[:n], refs[2 * n:3 * n]
        send_sems, recv_sems, in_sems, out_sems = refs[3 * n:3 * n + 4]
        stage = refs[3 * n + 4:]
        x, y, cc = lax.axis_index("x"), lax.axis_index("y"), lax.axis_index("c")
        sibling = (x, y, 1 - cc)
        staged = [pltpu.make_async_copy(x_refs[i], stage[i], in_sems.at[i]) for i in range(n)]
        for cp in staged:
            cp.start()
        copies = []
        for i in range(n):
            slot = land_refs[i].at[4 * x + 2 * y + cc]
            copies.append(pltpu.make_async_remote_copy(src_ref=x_refs[i], dst_ref=slot, send_sem=send_sems.at[4 * i],
                                                       recv_sem=recv_sems.at[4 * i], device_id=sibling, device_id_type=MESH))
            for j, (px, py) in enumerate(_other_chips(x, y)):
                slot = land_refs[i].at[4 * px + 2 * py + cc]
                copies.append(pltpu.make_async_remote_copy(src_ref=slot, dst_ref=slot, send_sem=send_sems.at[4 * i + 1 + j],
                                                           recv_sem=recv_sems.at[4 * i + 1 + j], device_id=sibling, device_id_type=MESH))
        for cp in copies:
            cp.start()
        mine = []
        for i in range(n):
            staged[i].wait()
            mine.append(pltpu.make_async_copy(stage[i], land_refs[i].at[4 * x + 2 * y + cc], out_sems.at[i]))
            mine[i].start()
        for i in range(n):
            slot = land_refs[i].at[4 * x + 2 * y + (1 - cc)]
            pltpu.make_async_remote_copy(src_ref=slot, dst_ref=slot, send_sem=send_sems.at[4 * i], recv_sem=recv_sems.at[4 * i],
                                         device_id=sibling, device_id_type=MESH).wait_recv()
            for j, (px, py) in enumerate(_other_chips(x, y)):
                slot = land_refs[i].at[4 * px + 2 * py + (1 - cc)]
                pltpu.make_async_remote_copy(src_ref=slot, dst_ref=slot, send_sem=send_sems.at[4 * i + 1 + j],
                                             recv_sem=recv_sems.at[4 * i + 1 + j], device_id=sibling, device_id_type=MESH).wait_recv()
        for cp in copies:
            cp.wait_send()
        for cp in mine:
            cp.wait()

    return pl.pallas_call(
        body, out_shape=tuple(_sds(l.shape, l.dtype) for l in lands), in_specs=[ANY] * (2 * n), out_specs=(ANY,) * n,
        input_output_aliases={n + i: i for i in range(n)},
        scratch_shapes=[pltpu.SemaphoreType.DMA((4 * n,)), pltpu.SemaphoreType.DMA((4 * n,)), pltpu.SemaphoreType.DMA((n,)),
                        pltpu.SemaphoreType.DMA((n,))] + [pltpu.VMEM(b.shape, b.dtype) for b in blocks],
        name=name, compiler_params=_cp())(*blocks, *lands)


def _sum_own(parts, recvs, chip, name):
    n = len(parts)

    def body(c_ref, *refs):
        s = pl.program_id(0)
        for i in range(n):
            val = jnp.where(c_ref[0] == s, refs[i][...], refs[n + i][...]).astype(F32)
            _acc(refs[2 * n + i], val, s == 0)

    ins = [pl.BlockSpec((None,) + p.shape[1:], lambda s, cref: (s, 0, 0)) for p in parts]
    return pl.pallas_call(
        body, grid_spec=pltpu.PrefetchScalarGridSpec(
            num_scalar_prefetch=1, grid=(4,), in_specs=ins + ins,
            out_specs=tuple(pl.BlockSpec(p.shape[1:], lambda s, cref: (0, 0)) for p in parts)),
        out_shape=tuple(_sds(p.shape[1:], F32) for p in parts), name=name, compiler_params=_cp())(chip, *parts, *recvs)


BIG = (("w_in", True), ("w_gate", True), ("w_mem_kv", False), ("w_branch", True), ("w_out", False), ("w_ffn_in", True),
       ("w_ffn_out", False))

SMALL = ("norm_mix_g", "norm_mem_g", "ret_decay_fwd", "ret_decay_bwd", "ret_norm_g", "pool_w", "pool_scale", "na_q_norm_g",
         "na_k_norm_g", "na_rpb", "mem_q_norm_g", "mem_k_norm_g", "norm_ffn_g")
WEIGHTS = ("norm_mix_g", "norm_mem_g", "w_in", "w_gate", "ret_decay_fwd", "ret_decay_bwd", "ret_norm_g", "pool_w", "pool_scale",
           "na_q_norm_g", "na_k_norm_g", "na_rpb", "mem_q_norm_g", "mem_k_norm_g", "w_mem_kv", "w_branch", "w_out", "norm_ffn_g",
           "w_ffn_in", "w_ffn_out")


def _to_exchange(name, transposed, shard):
    if name == "w_branch":
        return jnp.swapaxes(shard, 1, 2).reshape(NH * (D // N_DEV), BW)
    return shard.T if transposed else shard


def _from_exchange(name, transposed, block):
    if name == "w_branch":
        return jnp.swapaxes(block.reshape(NH, D // N_DEV, BW), 1, 2)
    return block.T if transposed else block


def _whole_from_gathered(name, g):
    if name == "w_branch":
        return jnp.swapaxes(g.reshape(N_DEV, NH, D // N_DEV, BW), 0, 1).reshape(NH, D, BW)
    return g.reshape(N_DEV * g.shape[1], g.shape[2])


def _by_destination(name, g):
    if name == "w_branch":
        g = jnp.swapaxes(g.reshape(NH, N_DEV, D // N_DEV, BW), 0, 1).reshape(N_DEV * NH * (D // N_DEV), BW)
    return g.reshape(4, 2, g.shape[0] // N_DEV, g.shape[1])


SMALL_PAD = 1024


def _pack_small(vals, loss=None):
    parts = [vals[n] for n in SMALL] + [jnp.zeros((1,), F32) if loss is None else loss.reshape(1)]
    rows = []
    for p in parts:
        flat = p.reshape(-1)
        rows.append(jnp.pad(flat, (0, -flat.shape[0] % SMALL_PAD)).reshape(-1, 128))
    return jnp.concatenate(rows, axis=0)


def _unpack_small(packed, like):
    out, off = {}, 0
    for n in SMALL:
        sz = int(np.prod(like[n].shape))
        nrow = -(-sz // SMALL_PAD) * (SMALL_PAD // 128)
        out[n] = packed[off:off + nrow].reshape(-1)[:sz].reshape(like[n].shape)
        off += nrow
    return out, packed[off, 0]


def _na_constants():
    c = np.arange(GRID_W)
    win = np.clip(c - NA_COLS_WIN // 2, 0, GRID_W - NA_COLS_WIN)
    kc = np.arange(GRID_W)
    inside = (kc[None, :] >= win[:, None]) & (kc[None, :] < win[:, None] + NA_COLS_WIN)
    off = kc[None, :] - c[:, None] + NA_COLS_WIN - 1
    onehot = np.zeros((128, GRID_W, GRID_W), np.float32)
    for b in range(2 * NA_COLS_WIN - 1):
        onehot[b] = (off == b) & inside
    maskadd = np.where(inside, 0.0, NEG).astype(np.float32)
    return onehot.reshape(128, GRID_W * GRID_W), maskadd


def _na_bias_table(tab, maskadd):
    n_off = 2 * NA_ROWS_WIN - 1
    t4 = tab[:NH * n_off].reshape(NH, n_off, GRID_W, GRID_W) + maskadd[None, None]
    ball = jnp.stack([t4[:, a0:a0 + NA_ROWS_WIN] for a0 in range(NA_ROWS_WIN)], axis=1)
    return ball.transpose(1, 0, 3, 2, 4).reshape(NA_ROWS_WIN, NH * GRID_W, NA_KEYS)


def _rotary_tables(t):
    half = HD // 2
    inv = ROPE_THETA ** (-jnp.arange(half, dtype=F32) / half)
    ang = jnp.arange(t, dtype=F32)[:, None] * inv[None, :]
    cos, sin = jnp.cos(ang), jnp.sin(ang)
    return jnp.tile(jnp.concatenate([cos, cos], axis=-1), (1, NH)), jnp.tile(jnp.concatenate([-sin, sin], axis=-1), (1, NH))


def _block_diag(pw):
    out = jnp.zeros((BW, BW), pw.dtype)
    for g in range(NH):
        out = lax.dynamic_update_slice(out, pw[g], (g * HD, g * HD))
    return out


def _tile4(g):
    return jnp.tile(g.reshape(1, HD), (1, NH))


def _layer_fwd(x, mem, sw, lw, consts):
    cos2, sin2, onehot, maskadd = consts
    h = _rmsnorm_fwd(x, sw["norm_mix_g"].reshape(1, D), "norm_mix_fwd")
    proj = _mm(h, lw["w_in"], tb=True, name="mm_in")
    gp = _mm(h, lw["w_gate"], tb=True, name="mm_gate")
    g_naq, g_nak, g_mq = _tile4(sw["na_q_norm_g"]), _tile4(sw["na_k_norm_g"]), _tile4(sw["mem_q_norm_g"])
    rq, rk, rv, nq, nk, nv, mq = _prep_fwd(proj, cos2, sin2, g_naq, g_nak, g_mq)

    lgf, lgb = jax.nn.log_sigmoid(sw["ret_decay_fwd"]), jax.nn.log_sigmoid(sw["ret_decay_bwd"])
    g_ret = sw["ret_norm_g"].reshape(1, BW)
    o_ret, ret = _ret_fwd(rq, rk, rv, proj, lgf, lgb, g_ret)

    wbd = _block_diag(sw["pool_w"]).astype(BF16)
    p_scale = sw["pool_scale"].reshape(1, BW)
    pool = _pool_fwd(proj, wbd, p_scale)

    rpb_pad = jnp.pad(sw["na_rpb"].reshape(NH * 15, 31), ((0, 4), (0, 97)))
    ball = _na_bias_table(_rpb_expand(rpb_pad, onehot), maskadd)
    na = _na_fwd(nq, nk, nv, ball)

    memn = _rmsnorm_fwd(mem, sw["norm_mem_g"].reshape(1, D), "norm_mem_fwd")
    kv = _mm(memn, lw["w_mem_kv"], name="mm_memkv")
    g_mk = _tile4(sw["mem_k_norm_g"])
    mk, mv = _memkv_prep(kv, g_mk)
    mo = _mem_fwd(mq, mk, mv)

    br = (ret, pool, na, mo)
    merged = _merge_fwd(br, lw["w_branch"], gp)
    x1 = _mm(merged, lw["w_out"], add=x, name="mm_out")
    h2 = _rmsnorm_fwd(x1, sw["norm_ffn_g"].reshape(1, D), "norm_ffn_fwd")
    ag = _mm(h2, lw["w_ffn_in"], tb=True, name="mm_ffn_in")
    yff = _swiglu_fwd(ag)
    x2 = _mm(yff, lw["w_ffn_out"], add=x1, name="mm_ffn_out")
    saved = dict(x=x, h=h, proj=proj, gp=gp, rq=rq, rk=rk, rv=rv, nq=nq, nk=nk, nv=nv, mq=mq, o_ret=o_ret, ball=ball, memn=memn,
                 kv=kv, mk=mk, mv=mv, br=br, merged=merged, x1=x1, h2=h2, ag=ag, yff=yff, lgf=lgf, lgb=lgb, wbd=wbd)
    return x2, saved


def _layer_bwd(dx2, mem, sw, lw, sv, consts, dep=None):
    cos2, sin2, onehot, maskadd = consts
    gb, gs = {}, {}
    dy = _mm(dx2, lw["w_ffn_out"], tb=True, dep=dep, name="mm_ffn_out_dx")
    gb["w_ffn_out"] = _mm(sv["yff"], dx2, ta=True, out_dtype=BF16, name="mm_ffn_out_dw")
    dag = _swiglu_bwd(sv["ag"], dy)
    dh2 = _mm(dag, lw["w_ffn_in"], name="mm_ffn_in_dx")
    gb["w_ffn_in"] = _mm(dag, sv["h2"], ta=True, out_dtype=BF16, name="mm_ffn_in_dw")
    dx1, dg = _rmsnorm_bwd(dh2, sv["x1"], sw["norm_ffn_g"].reshape(1, D), dx2, "norm_ffn_bwd")
    gs["norm_ffn_g"] = dg.reshape(D)

    dmerged = _mm(dx1, lw["w_out"], tb=True, name="mm_out_dx")
    gb["w_out"] = _mm(sv["merged"], dx1, ta=True, out_dtype=BF16, name="mm_out_dw")
    dgp, dup = _merge_bwd(dmerged, sv["br"], lw["w_branch"], sv["gp"])
    dbr = _dbranch(dup, lw["w_branch"])
    gb["w_branch"] = _dwbranch(sv["br"], dup)

    g_ret = sw["ret_norm_g"].reshape(1, BW)
    d_rq, d_rk, d_rv, d_rg, dg_ret, dlg = _ret_bwd(dbr, sv["o_ret"], sv["rq"], sv["rk"], sv["rv"], sv["proj"], sv["lgf"], sv["lgb"], g_ret)
    gs["ret_norm_g"] = dg_ret.reshape(BW)
    _, vjp_f = jax.vjp(jax.nn.log_sigmoid, sw["ret_decay_fwd"])
    _, vjp_b = jax.vjp(jax.nn.log_sigmoid, sw["ret_decay_bwd"])
    gs["ret_decay_fwd"] = vjp_f(dlg[0:NH, 0])[0]
    gs["ret_decay_bwd"] = vjp_b(dlg[NH:2 * NH, 0])[0]

    p_scale = sw["pool_scale"].reshape(1, BW)
    d_pv, dwbd, dscale = _pool_bwd(dbr, sv["proj"], sv["wbd"], p_scale)
    gs["pool_w"] = jnp.stack([dwbd[g * HD:(g + 1) * HD, g * HD:(g + 1) * HD] for g in range(NH)])
    gs["pool_scale"] = dscale.reshape(BW)

    d_nq, d_nk, d_nv, dball = _na_bwd(dbr, sv["nq"], sv["nk"], sv["nv"], sv["ball"])
    _, vjp_tab = jax.vjp(lambda tab: _na_bias_table(tab, maskadd), jnp.zeros((64, GRID_W * GRID_W), F32))
    drpb = _rpb_reduce(vjp_tab(dball)[0], onehot)
    gs["na_rpb"] = drpb[:NH * 15, :31].reshape(NH, 15, 31)

    d_mq, d_mk, d_mv = _mem_bwd(dbr, sv["mq"], sv["mk"], sv["mv"])
    g_mk = _tile4(sw["mem_k_norm_g"])
    dkv, dg_mk = _memkv_bwd(sv["kv"], d_mk, d_mv, g_mk)
    gs["mem_k_norm_g"] = dg_mk.reshape(NH, HD).sum(0)
    gb["w_mem_kv"] = _mm(sv["memn"], dkv, ta=True, out_dtype=BF16, name="mm_memkv_dw")
    dmemn = _mm(dkv, lw["w_mem_kv"], tb=True, name="mm_memkv_dx")
    _, dg_mem = _rmsnorm_bwd(dmemn, mem, sw["norm_mem_g"].reshape(1, D), jnp.zeros_like(mem), "norm_mem_bwd")
    gs["norm_mem_g"] = dg_mem.reshape(D)

    g_naq, g_nak, g_mq = _tile4(sw["na_q_norm_g"]), _tile4(sw["na_k_norm_g"]), _tile4(sw["mem_q_norm_g"])
    dproj, dg_naq, dg_nak, dg_mq = _prep_bwd(sv["proj"], cos2, sin2, g_naq, g_nak, g_mq, d_rq, d_rk, d_rv, d_rg, d_pv, d_nq, d_nk,
                                             d_nv, d_mq)
    gs["na_q_norm_g"] = dg_naq.reshape(NH, HD).sum(0)
    gs["na_k_norm_g"] = dg_nak.reshape(NH, HD).sum(0)
    gs["mem_q_norm_g"] = dg_mq.reshape(NH, HD).sum(0)

    dh = _mm(dproj, lw["w_in"], name="mm_in_dx")
    dh = _mm(dgp, lw["w_gate"], add=dh, name="mm_gate_dx")
    gb["w_in"] = _mm(dproj, sv["h"], ta=True, out_dtype=BF16, name="mm_in_dw")
    gb["w_gate"] = _mm(dgp, sv["h"], ta=True, out_dtype=BF16, name="mm_gate_dw")
    dx, dg = _rmsnorm_bwd(dh, sv["x"], sw["norm_mix_g"].reshape(1, D), dx1, "norm_mix_bwd")
    gs["norm_mix_g"] = dg.reshape(D)
    return dx, gb, gs


def _local_step(x, mem, target, small, get_layer, on_grads):
    t = x.shape[0]
    cos2, sin2 = _rotary_tables(t)
    onehot, maskadd = _na_constants()
    consts = (cos2, sin2, jnp.asarray(onehot), jnp.asarray(maskadd))
    saved, weights, cur = [], [], x
    for l in range(DEPTH):
        sw = {n: small[n][l] for n in SMALL}
        weights.append(get_layer(l, cur))
        cur, sv = _layer_fwd(cur, mem, sw, weights[l], consts)
        saved.append(sv)
    dy, loss_tile = _loss_head(cur, target)
    small_g = {n: [None] * DEPTH for n in SMALL}
    dep = None
    for l in reversed(range(DEPTH)):
        sw = {n: small[n][l] for n in SMALL}
        dy, gb, gs = _layer_bwd(dy, mem, sw, weights[l], saved[l], consts, dep)
        dep = on_grads(l, gb, dy)
        for n in SMALL:
            small_g[n][l] = gs[n]
    return loss_tile[0, 0], dy, {n: jnp.stack(v) for n, v in small_g.items()}


def _flat2d(a):
    return a.reshape(-1, a.shape[-1])


def kernel(x, mem, norm_mix_g, norm_mem_g, w_in, w_gate, ret_decay_fwd, ret_decay_bwd, ret_norm_g, pool_w, pool_scale, na_q_norm_g, na_k_norm_g, na_rpb, mem_q_norm_g, mem_k_norm_g, w_mem_kv, w_branch, w_out, norm_ffn_g, w_ffn_in, w_ffn_out, loss_target, m_norm_mix_g, m_norm_mem_g, m_w_in, m_w_gate, m_ret_decay_fwd, m_ret_decay_bwd, m_ret_norm_g, m_pool_w, m_pool_scale, m_na_q_norm_g, m_na_k_norm_g, m_na_rpb, m_mem_q_norm_g, m_mem_k_norm_g, m_w_mem_kv, m_w_branch, m_w_out, m_norm_ffn_g, m_w_ffn_in, m_w_ffn_out, v_norm_mix_g, v_norm_mem_g, v_w_in, v_w_gate, v_ret_decay_fwd, v_ret_decay_bwd, v_ret_norm_g, v_pool_w, v_pool_scale, v_na_q_norm_g, v_na_k_norm_g, v_na_rpb, v_mem_q_norm_g, v_mem_k_norm_g, v_w_mem_kv, v_w_branch, v_w_out, v_norm_ffn_g, v_w_ffn_in, v_w_ffn_out):
    w = dict(norm_mix_g=norm_mix_g, norm_mem_g=norm_mem_g, w_in=w_in, w_gate=w_gate, ret_decay_fwd=ret_decay_fwd,
             ret_decay_bwd=ret_decay_bwd, ret_norm_g=ret_norm_g, pool_w=pool_w, pool_scale=pool_scale, na_q_norm_g=na_q_norm_g,
             na_k_norm_g=na_k_norm_g, na_rpb=na_rpb, mem_q_norm_g=mem_q_norm_g, mem_k_norm_g=mem_k_norm_g, w_mem_kv=w_mem_kv,
             w_branch=w_branch, w_out=w_out, norm_ffn_g=norm_ffn_g, w_ffn_in=w_ffn_in, w_ffn_out=w_ffn_out)
    m = dict(norm_mix_g=m_norm_mix_g, norm_mem_g=m_norm_mem_g, w_in=m_w_in, w_gate=m_w_gate, ret_decay_fwd=m_ret_decay_fwd,
             ret_decay_bwd=m_ret_decay_bwd, ret_norm_g=m_ret_norm_g, pool_w=m_pool_w, pool_scale=m_pool_scale, na_q_norm_g=m_na_q_norm_g,
             na_k_norm_g=m_na_k_norm_g, na_rpb=m_na_rpb, mem_q_norm_g=m_mem_q_norm_g, mem_k_norm_g=m_mem_k_norm_g, w_mem_kv=m_w_mem_kv,
             w_branch=m_w_branch, w_out=m_w_out, norm_ffn_g=m_norm_ffn_g, w_ffn_in=m_w_ffn_in, w_ffn_out=m_w_ffn_out)
    v = dict(norm_mix_g=v_norm_mix_g, norm_mem_g=v_norm_mem_g, w_in=v_w_in, w_gate=v_w_gate, ret_decay_fwd=v_ret_decay_fwd,
             ret_decay_bwd=v_ret_decay_bwd, ret_norm_g=v_ret_norm_g, pool_w=v_pool_w, pool_scale=v_pool_scale, na_q_norm_g=v_na_q_norm_g,
             na_k_norm_g=v_na_k_norm_g, na_rpb=v_na_rpb, mem_q_norm_g=v_mem_q_norm_g, mem_k_norm_g=v_mem_k_norm_g, w_mem_kv=v_w_mem_kv,
             w_branch=v_w_branch, w_out=v_w_out, norm_ffn_g=v_norm_ffn_g, w_ffn_in=v_w_ffn_in, w_ffn_out=v_w_ffn_out)
    assert x.shape == (1, 2048, D) and mem.shape == (1, N_MEM, D) and w_in.shape == (DEPTH, D, 9 * BW // N_DEV)

    started = []
    for l in range(DEPTH):
        blocks = [_to_exchange(name, tr, w[name][l]).astype(BF16) for name, tr in BIG]
        lands = [lax.empty((N_DEV,) + b.shape, BF16) for b in blocks]
        started.append(_ici_start(blocks, lands, False, "gather_ici_start_%d" % l))
    all_started = started[0][4] + started[1][4] + started[2][4] + started[3][4]

    def get_layer(l, after):
        lands = _ici_wait(started[l], all_started if l == 0 else after, "gather_ici_wait_%d" % l)
        whole = _gather_d2d(started[l][2], lands, "gather_d2d")
        return {name: _whole_from_gathered(name, g) for (name, _), g in zip(BIG, whole)}

    cidx = lax.axis_index("c").astype(jnp.int32).reshape(1)
    chip = (2 * lax.axis_index("x") + lax.axis_index("y")).astype(jnp.int32).reshape(1)
    in_flight, g_layers = [], [None] * DEPTH

    def finish(l, st, after):
        recv = _ici_wait(st, after, "rs_ici_wait_%d" % l)
        sums = _sum_own(st[2], recv, chip, "rs_chip_sum")
        g_layers[l] = {name: _from_exchange(name, tr, s) for (name, tr), s in zip(BIG, sums)}

    def on_grads(l, gb, after):
        send = [_by_destination(name, gb[name]) for name, _ in BIG]
        from_core = _rs_core_swap(send, "rs_core_swap")
        chip_part = _pair_sum(send, from_core, cidx)
        st = _ici_start(chip_part, [lax.empty(p.shape, BF16) for p in chip_part], True, "rs_ici_start_%d" % l)
        if in_flight:
            finish(*in_flight.pop(), after)
        in_flight.append((l, st))
        if l == 0:
            finish(*in_flight.pop(), st[4])
        return st[4]

    loss_local, dx, small_g = _local_step(x[0], mem[0], loss_target[0], {n: w[n] for n in SMALL}, get_layer, on_grads)
    g_shard = {name: jnp.stack([g_layers[l][name] for l in range(DEPTH)]) for name, _ in BIG}

    small_all, = _all_gather([_pack_small(small_g, loss_local)], "gather_small")
    packed_g = _sum_slots(small_all, "small_sum")
    small_sum, loss = _unpack_small(packed_g, {n: w[n] for n in SMALL})

    grads, delta, new_m, new_v = {}, {}, {}, {}
    for name, _ in BIG:
        grads[name] = g_shard[name]
        d_, m_, v_ = _adamw(_flat2d(w[name]), _flat2d(grads[name]), _flat2d(m[name]), _flat2d(v[name]), "adamw_" + name)
        delta[name], new_m[name], new_v[name] = (a.reshape(w[name].shape) for a in (d_, m_, v_))
    d_, m_, v_ = _adamw(_pack_small({n: w[n] for n in SMALL}), packed_g, _pack_small({n: m[n] for n in SMALL}),
                        _pack_small({n: v[n] for n in SMALL}), "adamw_small")
    like = {n: w[n] for n in SMALL}
    ds, _ = _unpack_small(d_, like)
    ms, _ = _unpack_small(m_, like)
    vs, _ = _unpack_small(v_, like)
    for n in SMALL:
        grads[n], delta[n], new_m[n], new_v[n] = small_sum[n], ds[n], ms[n], vs[n]

    return (loss, dx[None], *[grads[n] for n in WEIGHTS], *[delta[n] for n in WEIGHTS], *[new_m[n] for n in WEIGHTS],
            *[new_v[n] for n in WEIGHTS])
```

```python
import functools

import numpy as np
import jax
import jax.numpy as jnp
from jax import lax
from jax.experimental import pallas as pl
from jax.experimental.pallas import tpu as pltpu

F32 = jnp.float32
BF16 = jnp.bfloat16
MXU = jnp.bfloat16
HI = lax.Precision.HIGHEST

DEPTH = 4
D = 1024
BW = 256
HD = 64
NH = 4
GRID_W = 64
NA_ROWS_WIN = 8
NA_COLS_WIN = 16
N_MEM = 256
FF = 2816
EPS = 1e-6
NEG = -1e30
ROPE_THETA = 10000.0
POOL_HALF_MAX = 8

ADAM_LR, ADAM_B1, ADAM_B2, ADAM_EPS, ADAM_WD, ADAM_STEP = 0.001, 0.9, 0.999, 1e-08, 0.01, 10

N_DEV = 8
VMEM_LIMIT = 56 * 1024 * 1024

RQ, RK, RV, RG, PV, NQ, NK, NV, MQ = range(9)

MESH = pl.DeviceIdType.MESH
ANY = pl.BlockSpec(memory_space=pl.ANY)
SMEM = pl.BlockSpec(memory_space=pltpu.SMEM)


def _cp(**kw):
    return pltpu.CompilerParams(vmem_limit_bytes=VMEM_LIMIT, **kw)


def _tile(n, cap):
    if n <= cap:
        return n
    best = None
    for t in range(128, cap + 1, 128):
        if n % t == 0:
            best = t
    assert best is not None, (n, cap)
    return best


def _sds(shape, dtype):
    return jax.ShapeDtypeStruct(shape, dtype)


def _lane_head(shape):
    return lax.shift_right_logical(lax.broadcasted_iota(jnp.int32, shape, len(shape) - 1), 6)


def _group_mean(z):
    i = lax.shift_right_logical(lax.broadcasted_iota(jnp.int32, (BW, BW), 0), 6)
    j = lax.shift_right_logical(lax.broadcasted_iota(jnp.int32, (BW, BW), 1), 6)
    g = jnp.where(i == j, 1.0 / HD, 0.0).astype(F32)
    return jnp.dot(z, g, precision=HI, preferred_element_type=F32)


def _gnorm(t, g):
    r = lax.rsqrt(_group_mean(t * t) + EPS)
    return t * r * g


def _gnorm_bwd(dy, t, g):
    r = lax.rsqrt(_group_mean(t * t) + EPS)
    th = t * r
    dth = dy * g
    dt = r * (dth - th * _group_mean(dth * th))
    return dt, dy * th


def _swap_halves(t):
    lane = lax.broadcasted_iota(jnp.int32, t.shape, 1)
    return jnp.where((lane & 63) < 32, pltpu.roll(t, BW - 32, 1), pltpu.roll(t, 32, 1))


def _sigmoid(x):
    return 1.0 / (1.0 + jnp.exp(-x))


def _dot(a, b, ta=False, tb=False):
    return lax.dot_general(a.astype(MXU), b.astype(MXU), (((0 if ta else 1,), (1 if tb else 0,)), ((), ())),
                           preferred_element_type=F32)


def _stack_heads(t):
    head = _lane_head(t.shape)
    return jnp.concatenate([jnp.where(head == h, t, jnp.zeros_like(t)) for h in range(NH)], axis=0)


def _unstack_heads(t, rows):
    head = _lane_head((rows, BW))
    out = jnp.zeros((rows, BW), F32)
    for h in range(NH):
        out = out + jnp.where(head == h, t[h * rows:(h + 1) * rows], 0.0)
    return out


def _softmax_rows(s):
    m = jnp.max(s, axis=-1, keepdims=True)
    e = jnp.exp(s - m)
    return e / jnp.sum(e, axis=-1, keepdims=True)


def _acc(ref, val, first):
    @pl.when(first)
    def _():
        ref[...] = val

    @pl.when(jnp.logical_not(first))
    def _():
        ref[...] += val


def _mm(a, b, *, ta=False, tb=False, out_dtype=F32, add=None, dep=None, name):
    m, k = (a.shape[1], a.shape[0]) if ta else a.shape
    n = b.shape[0] if tb else b.shape[1]
    tm, tn = _tile(m, 1024), _tile(n, 512)

    def body(*refs):
        if add is None:
            a_ref, b_ref, o_ref = refs[:2] + refs[-1:]
            r = _dot(a_ref[...], b_ref[...], ta, tb)
        else:
            a_ref, b_ref, c_ref, o_ref = refs[:3] + refs[-1:]
            r = _dot(a_ref[...], b_ref[...], ta, tb) + c_ref[...]
        o_ref[...] = r.astype(out_dtype)

    a_spec = pl.BlockSpec((k, tm), lambda i, j: (0, i)) if ta else pl.BlockSpec((tm, k), lambda i, j: (i, 0))
    b_spec = pl.BlockSpec((tn, k), lambda i, j: (j, 0)) if tb else pl.BlockSpec((k, tn), lambda i, j: (0, j))
    o_spec = pl.BlockSpec((tm, tn), lambda i, j: (i, j))
    ins, args = [a_spec, b_spec], [a, b]
    if add is not None:
        ins.append(o_spec)
        args.append(add)
    if dep is not None:
        ins.append(pl.BlockSpec((8, 128), lambda i, j: (0, 0)))
        args.append(dep)
    return pl.pallas_call(
        body, grid=(m // tm, n // tn), in_specs=ins, out_specs=o_spec, out_shape=_sds((m, n), out_dtype), name=name,
        compiler_params=_cp(dimension_semantics=("parallel", "parallel")))(*args)


def _rmsnorm_fwd(x, g, name):
    t, d = x.shape
    tm = _tile(t, 256)

    def body(x_ref, g_ref, o_ref):
        xv = x_ref[...]
        r = lax.rsqrt(jnp.mean(xv * xv, axis=-1, keepdims=True) + EPS)
        o_ref[...] = (xv * r * g_ref[...]).astype(o_ref.dtype)

    return pl.pallas_call(
        body, grid=(t // tm,), in_specs=[pl.BlockSpec((tm, d), lambda i: (i, 0)), pl.BlockSpec((1, d), lambda i: (0, 0))],
        out_specs=pl.BlockSpec((tm, d), lambda i: (i, 0)), out_shape=_sds((t, d), BF16), name=name, compiler_params=_cp())(x, g)


def _rmsnorm_bwd(dh, x, g, res, name):
    t, d = x.shape
    tm = _tile(t, 256)

    def body(dh_ref, x_ref, g_ref, res_ref, dx_ref, dg_ref):
        xv = x_ref[...]
        dhv = dh_ref[...]
        r = lax.rsqrt(jnp.mean(xv * xv, axis=-1, keepdims=True) + EPS)
        xh = xv * r
        dxh = dhv * g_ref[...]
        dx_ref[...] = res_ref[...] + r * (dxh - xh * jnp.mean(dxh * xh, axis=-1, keepdims=True))
        _acc(dg_ref, jnp.sum(dhv * xh, axis=0, keepdims=True), pl.program_id(0) == 0)

    row = pl.BlockSpec((tm, d), lambda i: (i, 0))
    vec = pl.BlockSpec((1, d), lambda i: (0, 0))
    return pl.pallas_call(
        body, grid=(t // tm,), in_specs=[row, row, vec, row], out_specs=(row, vec),
        out_shape=(_sds((t, d), F32), _sds((1, d), F32)), name=name, compiler_params=_cp())(dh, x, g, res)


def _prep_fwd(proj, cos2, sin2, g_naq, g_nak, g_mq):
    t = proj.shape[0]
    tm = 256

    def body(p_ref, cos_ref, sin_ref, gq_ref, gk_ref, gm_ref, rq_ref, rk_ref, rv_ref, nq_ref, nk_ref, nv_ref, mq_ref):
        def col(c):
            return p_ref[:, c * BW:(c + 1) * BW]

        cosv, sinv = cos_ref[...], sin_ref[...]

        def rot(tv):
            return tv * cosv + _swap_halves(tv) * sinv

        rq_ref[...] = (rot(col(RQ)) * (HD ** -0.5)).astype(BF16)
        rk_ref[...] = rot(col(RK)).astype(BF16)
        rv_ref[...] = col(RV).astype(BF16)
        nq_ref[...] = _gnorm(col(NQ), gq_ref[...]).astype(BF16)
        nk_ref[...] = _gnorm(col(NK), gk_ref[...]).astype(BF16)
        nv_ref[...] = col(NV).astype(BF16)
        mq_ref[...] = _gnorm(col(MQ), gm_ref[...]).astype(BF16)

    blk = pl.BlockSpec((tm, BW), lambda i: (i, 0))
    vec = pl.BlockSpec((1, BW), lambda i: (0, 0))
    return pl.pallas_call(
        body, grid=(t // tm,), in_specs=[pl.BlockSpec((tm, 9 * BW), lambda i: (i, 0)), blk, blk, vec, vec, vec],
        out_specs=tuple(blk for _ in range(7)), out_shape=tuple(_sds((t, BW), BF16) for _ in range(7)),
        name="prep_fwd", compiler_params=_cp())(proj, cos2, sin2, g_naq, g_nak, g_mq)


def _prep_bwd(proj, cos2, sin2, g_naq, g_nak, g_mq, d_rq, d_rk, d_rv, d_rg, d_pv, d_nq, d_nk, d_nv, d_mq):
    t = proj.shape[0]
    tm = 256

    def body(p_ref, cos_ref, sin_ref, gq_ref, gk_ref, gm_ref, drq_ref, drk_ref, drv_ref, drg_ref, dpv_ref, dnq_ref, dnk_ref,
             dnv_ref, dmq_ref, o_ref, dgq_ref, dgk_ref, dgm_ref):
        first = pl.program_id(0) == 0

        def col(c):
            return p_ref[:, c * BW:(c + 1) * BW]

        def put(c, v):
            o_ref[:, c * BW:(c + 1) * BW] = v.astype(BF16)

        cosv, sinv = cos_ref[...], sin_ref[...]

        def rot_t(dv):
            return dv * cosv + _swap_halves(dv * sinv)

        put(RQ, rot_t(drq_ref[...] * (HD ** -0.5)))
        put(RK, rot_t(drk_ref[...]))
        put(RV, drv_ref[...])
        put(RG, drg_ref[...])
        put(PV, dpv_ref[...])
        dq, gq = _gnorm_bwd(dnq_ref[...], col(NQ), gq_ref[...])
        put(NQ, dq)
        _acc(dgq_ref, jnp.sum(gq, axis=0, keepdims=True), first)
        dk, gk = _gnorm_bwd(dnk_ref[...], col(NK), gk_ref[...])
        put(NK, dk)
        _acc(dgk_ref, jnp.sum(gk, axis=0, keepdims=True), first)
        put(NV, dnv_ref[...])
        dm, gm = _gnorm_bwd(dmq_ref[...], col(MQ), gm_ref[...])
        put(MQ, dm)
        _acc(dgm_ref, jnp.sum(gm, axis=0, keepdims=True), first)

    blk = pl.BlockSpec((tm, BW), lambda i: (i, 0))
    vec = pl.BlockSpec((1, BW), lambda i: (0, 0))
    wide = pl.BlockSpec((tm, 9 * BW), lambda i: (i, 0))
    return pl.pallas_call(
        body, grid=(t // tm,), in_specs=[wide, blk, blk, vec, vec, vec] + [blk] * 9, out_specs=(wide, vec, vec, vec),
        out_shape=(_sds((t, 9 * BW), BF16), _sds((1, BW), F32), _sds((1, BW), F32), _sds((1, BW), F32)),
        name="prep_bwd", compiler_params=_cp())(proj, cos2, sin2, g_naq, g_nak, g_mq, d_rq, d_rk, d_rv, d_rg, d_pv, d_nq, d_nk,
                                                d_nv, d_mq)


RET_TQ = 64


def _ret_decay(i, tq, t, lgf_ref, lgb_ref):
    rows = NH * tq
    n = i * tq + (lax.broadcasted_iota(jnp.int32, (rows, 1), 0) & (tq - 1))
    m = lax.broadcasted_iota(jnp.int32, (1, t), 1)
    diff = n - m
    causal = diff >= 0
    dist = jnp.abs(diff).astype(F32)
    lgf = jnp.concatenate([jnp.full((tq, 1), lgf_ref[h], F32) for h in range(NH)], axis=0)
    lgb = jnp.concatenate([jnp.full((tq, 1), lgb_ref[h], F32) for h in range(NH)], axis=0)
    return causal, dist, jnp.exp(dist * jnp.where(causal, lgf, lgb))


def _ret_fwd(q, k, v, proj, lgf, lgb, g_ret):
    t = q.shape[0]
    tq = RET_TQ

    def body(lgf_ref, lgb_ref, q_ref, k_ref, v_ref, rg_ref, g_ref, o_ref, ret_ref):
        i = pl.program_id(0)
        qs = _stack_heads(q_ref[...])
        s = _dot(qs, k_ref[...], tb=True)
        _, _, dm = _ret_decay(i, tq, t, lgf_ref, lgb_ref)
        o = _unstack_heads(_dot(s * dm, v_ref[...]), tq)
        o_ref[...] = o
        rg = rg_ref[...]
        ret_ref[...] = (_gnorm(o, g_ref[...]) * (rg * _sigmoid(rg))).astype(BF16)

    blk = pl.BlockSpec((tq, BW), lambda i: (i, 0))
    whole = pl.BlockSpec((t, BW), lambda i: (0, 0))
    return pl.pallas_call(
        body, grid=(t // tq,),
        in_specs=[SMEM, SMEM, blk, whole, whole, pl.BlockSpec((tq, BW), lambda i: (i, RG)), pl.BlockSpec((1, BW), lambda i: (0, 0))],
        out_specs=(blk, blk), out_shape=(_sds((t, BW), F32), _sds((t, BW), BF16)), name="ret_fwd",
        compiler_params=_cp())(lgf, lgb, q, k, v, proj, g_ret)


def _ret_bwd(dbr, o_ret, q, k, v, proj, lgf, lgb, g_ret):
    t = q.shape[0]
    tq = RET_TQ
    nblk = t // tq

    def body(lgf_ref, lgb_ref, d_ref, o_ref, q_ref, k_ref, v_ref, rg_ref, g_ref,
             dq_ref, dk_ref, dv_ref, drg_ref, dg_ref, dlg_ref, accf_ref, accb_ref):
        i = pl.program_id(0)
        first = i == 0
        dret, o, rg, g = d_ref[...], o_ref[...], rg_ref[...], g_ref[...]
        sg = _sigmoid(rg)
        dy = dret * (rg * sg)
        do, dgain = _gnorm_bwd(dy, o, g)
        drg_ref[...] = dret * _gnorm(o, g) * (sg * (1.0 + rg * (1.0 - sg)))
        _acc(dg_ref, jnp.sum(dgain, axis=0, keepdims=True), first)

        dos = _stack_heads(do).astype(MXU)
        qs = _stack_heads(q_ref[...])
        kv, vv = k_ref[...], v_ref[...]
        s = _dot(qs, kv, tb=True)
        causal, dist, dm = _ret_decay(i, tq, t, lgf_ref, lgb_ref)
        da = _dot(dos, vv, tb=True)
        _acc(dv_ref, _dot(s * dm, dos, ta=True), first)
        ds = da * dm
        w = ds * s * dist
        _acc(accf_ref, jnp.sum(jnp.where(causal, w, 0.0), axis=1, keepdims=True), first)
        _acc(accb_ref, jnp.sum(jnp.where(causal, 0.0, w), axis=1, keepdims=True), first)
        dsb = ds.astype(MXU)
        dq_ref[...] = _unstack_heads(_dot(dsb, kv), tq)
        _acc(dk_ref, _dot(dsb, qs, ta=True), first)

        @pl.when(i == nblk - 1)
        def _():
            for h in range(NH):
                dlg_ref[h:h + 1, :] = jnp.full((1, 128), jnp.sum(accf_ref[h * tq:(h + 1) * tq, :]), F32)
                dlg_ref[NH + h:NH + h + 1, :] = jnp.full((1, 128), jnp.sum(accb_ref[h * tq:(h + 1) * tq, :]), F32)

    blk = pl.BlockSpec((tq, BW), lambda i: (i, 0))
    whole = pl.BlockSpec((t, BW), lambda i: (0, 0))
    vec = pl.BlockSpec((1, BW), lambda i: (0, 0))
    return pl.pallas_call(
        body, grid=(nblk,),
        in_specs=[SMEM, SMEM, blk, blk, blk, whole, whole, pl.BlockSpec((tq, BW), lambda i: (i, RG)), vec],
        out_specs=(blk, whole, whole, blk, vec, pl.BlockSpec((2 * NH, 128), lambda i: (0, 0))),
        out_shape=(_sds((t, BW), F32), _sds((t, BW), F32), _sds((t, BW), F32), _sds((t, BW), F32), _sds((1, BW), F32),
                   _sds((2 * NH, 128), F32)),
        scratch_shapes=[pltpu.VMEM((NH * tq, 1), F32), pltpu.VMEM((NH * tq, 1), F32)], name="ret_bwd",
        compiler_params=_cp())(lgf, lgb, dbr, o_ret, q, k, v, proj, g_ret)


def _pool_windows(t):
    row = lax.broadcasted_iota(jnp.int32, (t, BW), 0)
    half = lax.shift_left(jnp.ones((t, BW), jnp.int32), _lane_head((t, BW)))
    cnt = (jnp.minimum(row + half, t) - jnp.maximum(row - half, 0)).astype(F32)
    return row, half, cnt


def _pool_window_sum(v, row, half, t, transpose):
    out = jnp.zeros_like(v)
    for j in range(-POOL_HALF_MAX, POOL_HALF_MAX):
        src = row - j if transpose else row + j
        ok = (src >= 0) & (src < t) & (j >= -half) & (j < half)
        out = out + jnp.where(ok, pltpu.roll(v, (j if transpose else -j) % t, 0), 0.0)
    return out


def _pool_fwd(proj, wbd, scale):
    t = proj.shape[0]

    def body(v_ref, w_ref, s_ref, o_ref):
        v = v_ref[...]
        row, half, cnt = _pool_windows(t)
        pooled = _pool_window_sum(v, row, half, t, False) / cnt - v
        o_ref[...] = (_dot(pooled, w_ref[...]) * s_ref[...]).astype(BF16)

    return pl.pallas_call(
        body, grid=(1,),
        in_specs=[pl.BlockSpec((t, BW), lambda i: (0, PV)), pl.BlockSpec((BW, BW), lambda i: (0, 0)), pl.BlockSpec((1, BW), lambda i: (0, 0))],
        out_specs=pl.BlockSpec((t, BW), lambda i: (0, 0)), out_shape=_sds((t, BW), BF16), name="pool_fwd",
        compiler_params=_cp())(proj, wbd, scale)


def _pool_bwd(dbr, proj, wbd, scale):
    t = proj.shape[0]

    def body(d_ref, v_ref, w_ref, s_ref, dv_ref, dw_ref, ds_ref):
        v, dout = v_ref[...], d_ref[...]
        row, half, cnt = _pool_windows(t)
        pooled = _pool_window_sum(v, row, half, t, False) / cnt - v
        mixed = _dot(pooled, w_ref[...])
        ds_ref[...] = jnp.sum(dout * mixed, axis=0, keepdims=True)
        dmixed = dout * s_ref[...]
        dw_ref[...] = _dot(pooled, dmixed, ta=True)
        dpooled = _dot(dmixed, w_ref[...], tb=True)
        dv_ref[...] = _pool_window_sum(dpooled / cnt, row, half, t, True) - dpooled

    return pl.pallas_call(
        body, grid=(1,),
        in_specs=[pl.BlockSpec((t, BW), lambda i: (0, 1)), pl.BlockSpec((t, BW), lambda i: (0, PV)),
                  pl.BlockSpec((BW, BW), lambda i: (0, 0)), pl.BlockSpec((1, BW), lambda i: (0, 0))],
        out_specs=(pl.BlockSpec((t, BW), lambda i: (0, 0)), pl.BlockSpec((BW, BW), lambda i: (0, 0)), pl.BlockSpec((1, BW), lambda i: (0, 0))),
        out_shape=(_sds((t, BW), F32), _sds((BW, BW), F32), _sds((1, BW), F32)), name="pool_bwd",
        compiler_params=_cp())(dbr, proj, wbd, scale)


NA_KEYS = NA_ROWS_WIN * GRID_W


def _na_window(r, n_rows):
    rs = jnp.clip(r - NA_ROWS_WIN // 2, 0, n_rows - NA_ROWS_WIN)
    return pl.multiple_of(rs * GRID_W, GRID_W), rs - r + (NA_ROWS_WIN - 1)


def _na_fwd(q, k, v, ball):
    t = q.shape[0]
    n_rows = t // GRID_W

    def body(q_ref, k_ref, v_ref, b_ref, o_ref):
        start, a0 = _na_window(pl.program_id(0), n_rows)
        qs = _stack_heads(q_ref[...])
        s = _dot(qs, k_ref[pl.ds(start, NA_KEYS), :], tb=True) * (HD ** -0.5) + b_ref[a0]
        p = _softmax_rows(s)
        o_ref[...] = _unstack_heads(_dot(p, v_ref[pl.ds(start, NA_KEYS), :]), GRID_W).astype(BF16)

    blk = pl.BlockSpec((GRID_W, BW), lambda r: (r, 0))
    whole = pl.BlockSpec((t, BW), lambda r: (0, 0))
    return pl.pallas_call(
        body, grid=(n_rows,), in_specs=[blk, whole, whole, pl.BlockSpec(ball.shape, lambda r: (0, 0, 0))],
        out_specs=blk, out_shape=_sds((t, BW), BF16), name="na_fwd", compiler_params=_cp())(q, k, v, ball)


def _na_bwd(dbr, q, k, v, ball):
    t = q.shape[0]
    n_rows = t // GRID_W

    def body(d_ref, q_ref, k_ref, v_ref, b_ref, dq_ref, dk_ref, dv_ref, db_ref):
        r = pl.program_id(0)
        start, a0 = _na_window(r, n_rows)
        keys = pl.ds(start, NA_KEYS)

        @pl.when(r == 0)
        def _():
            dk_ref[...] = jnp.zeros_like(dk_ref)
            dv_ref[...] = jnp.zeros_like(dv_ref)
            db_ref[...] = jnp.zeros_like(db_ref)

        qs = _stack_heads(q_ref[...])
        kb, vb = k_ref[keys, :], v_ref[keys, :]
        p = _softmax_rows(_dot(qs, kb, tb=True) * (HD ** -0.5) + b_ref[a0])
        dos = _stack_heads(d_ref[...]).astype(MXU)
        dp = _dot(dos, vb, tb=True)
        dv_ref[keys, :] += _dot(p, dos, ta=True)
        ds = p * (dp - jnp.sum(dp * p, axis=-1, keepdims=True))
        db_ref[a0] += ds
        dsb = (ds * (HD ** -0.5)).astype(MXU)
        dq_ref[...] = _unstack_heads(_dot(dsb, kb), GRID_W)
        dk_ref[keys, :] += _dot(dsb, qs, ta=True)

    blk = pl.BlockSpec((GRID_W, BW), lambda r: (r, 0))
    whole = pl.BlockSpec((t, BW), lambda r: (0, 0))
    tab = pl.BlockSpec(ball.shape, lambda r: (0, 0, 0))
    return pl.pallas_call(
        body, grid=(n_rows,), in_specs=[pl.BlockSpec((GRID_W, BW), lambda r: (r, 2)), blk, whole, whole, tab],
        out_specs=(blk, whole, whole, tab),
        out_shape=(_sds((t, BW), F32), _sds((t, BW), F32), _sds((t, BW), F32), _sds(ball.shape, F32)), name="na_bwd",
        compiler_params=_cp())(dbr, q, k, v, ball)


def _rpb_expand(rpb_pad, onehot):
    def body(r_ref, e_ref, o_ref):
        o_ref[...] = jnp.dot(r_ref[...], e_ref[...], precision=HI, preferred_element_type=F32)

    return pl.pallas_call(body, out_shape=_sds((64, GRID_W * GRID_W), F32), name="rpb_expand", compiler_params=_cp())(rpb_pad, onehot)


def _rpb_reduce(dtab, onehot):
    def body(d_ref, e_ref, o_ref):
        o_ref[...] = lax.dot_general(d_ref[...], e_ref[...], (((1,), (1,)), ((), ())), precision=HI, preferred_element_type=F32)

    return pl.pallas_call(body, out_shape=_sds((64, 128), F32), name="rpb_reduce", compiler_params=_cp())(dtab, onehot)


MEM_TQ = 256


def _mem_fwd(q, mk, mv):
    t = q.shape[0]
    tq = MEM_TQ

    def body(q_ref, k_ref, v_ref, o_ref):
        qv = q_ref[...]
        head = _lane_head(qv.shape)
        out = jnp.zeros((tq, BW), F32)
        for h in range(NH):
            p = _softmax_rows(_dot(jnp.where(head == h, qv, jnp.zeros_like(qv)), k_ref[...], tb=True) * (HD ** -0.5))
            out = out + jnp.where(head == h, _dot(p, v_ref[...]), 0.0)
        o_ref[...] = out.astype(BF16)

    blk = pl.BlockSpec((tq, BW), lambda i: (i, 0))
    kv = pl.BlockSpec((N_MEM, BW), lambda i: (0, 0))
    return pl.pallas_call(body, grid=(t // tq,), in_specs=[blk, kv, kv], out_specs=blk, out_shape=_sds((t, BW), BF16),
                          name="mem_fwd", compiler_params=_cp())(q, mk, mv)


def _mem_bwd(dbr, q, mk, mv):
    t = q.shape[0]
    tq = MEM_TQ

    def body(d_ref, q_ref, k_ref, v_ref, dq_ref, dk_ref, dv_ref):
        first = pl.program_id(0) == 0
        qv, dout = q_ref[...], d_ref[...]
        head = _lane_head(qv.shape)
        dq = jnp.zeros((tq, BW), F32)
        dk = jnp.zeros((N_MEM, BW), F32)
        dv = jnp.zeros((N_MEM, BW), F32)
        for h in range(NH):
            qh = jnp.where(head == h, qv, jnp.zeros_like(qv))
            doh = jnp.where(head == h, dout, 0.0).astype(MXU)
            p = _softmax_rows(_dot(qh, k_ref[...], tb=True) * (HD ** -0.5))
            dp = _dot(doh, v_ref[...], tb=True)
            dv = dv + _dot(p, doh, ta=True)
            dsb = (p * (dp - jnp.sum(dp * p, axis=-1, keepdims=True)) * (HD ** -0.5)).astype(MXU)
            dq = dq + jnp.where(head == h, _dot(dsb, k_ref[...]), 0.0)
            dk = dk + _dot(dsb, qh, ta=True)
        dq_ref[...] = dq
        _acc(dk_ref, dk, first)
        _acc(dv_ref, dv, first)

    blk = pl.BlockSpec((tq, BW), lambda i: (i, 0))
    kv = pl.BlockSpec((N_MEM, BW), lambda i: (0, 0))
    return pl.pallas_call(
        body, grid=(t // tq,), in_specs=[pl.BlockSpec((tq, BW), lambda i: (i, 3)), blk, kv, kv], out_specs=(blk, kv, kv),
        out_shape=(_sds((t, BW), F32), _sds((N_MEM, BW), F32), _sds((N_MEM, BW), F32)), name="mem_bwd",
        compiler_params=_cp())(dbr, q, mk, mv)


def _memkv_prep(kv, g_mk):
    def body(kv_ref, g_ref, k_ref, v_ref):
        k_ref[...] = _gnorm(kv_ref[:, 0:BW], g_ref[...]).astype(BF16)
        v_ref[...] = kv_ref[:, BW:2 * BW].astype(BF16)

    return pl.pallas_call(body, out_shape=(_sds((N_MEM, BW), BF16), _sds((N_MEM, BW), BF16)), name="memkv_prep",
                          compiler_params=_cp())(kv, g_mk)


def _memkv_bwd(kv, dk, dv, g_mk):
    def body(kv_ref, dk_ref, dv_ref, g_ref, o_ref, dg_ref):
        dkk, gain = _gnorm_bwd(dk_ref[...], kv_ref[:, 0:BW], g_ref[...])
        o_ref[:, 0:BW] = dkk.astype(BF16)
        o_ref[:, BW:2 * BW] = dv_ref[...].astype(BF16)
        dg_ref[...] = jnp.sum(gain, axis=0, keepdims=True)

    return pl.pallas_call(body, out_shape=(_sds((N_MEM, 2 * BW), BF16), _sds((1, BW), F32)), name="memkv_bwd",
                          compiler_params=_cp())(kv, dk, dv, g_mk)


MERGE_TM = 256


def _merge_fwd(brs, wbt, gp):
    t = gp.shape[0]
    tm = MERGE_TM

    def body(b0, b1, b2, b3, wb_ref, gp_ref, o_ref):
        out = jnp.zeros((tm, D), F32)
        for n, b_ref in enumerate((b0, b1, b2, b3)):
            up = _dot(b_ref[...], wb_ref[n], tb=True)
            out = out + _sigmoid(gp_ref[:, n * D:(n + 1) * D]) * up
        o_ref[...] = out.astype(BF16)

    blk = pl.BlockSpec((tm, BW), lambda i: (i, 0))
    return pl.pallas_call(
        body, grid=(t // tm,),
        in_specs=[blk, blk, blk, blk, pl.BlockSpec((NH, D, BW), lambda i: (0, 0, 0)), pl.BlockSpec((tm, NH * D), lambda i: (i, 0))],
        out_specs=pl.BlockSpec((tm, D), lambda i: (i, 0)), out_shape=_sds((t, D), BF16), name="merge_fwd",
        compiler_params=_cp())(*brs, wbt, gp)


def _merge_bwd(dmerged, brs, wbt, gp):
    t = gp.shape[0]
    tm = MERGE_TM

    def body(d_ref, b0, b1, b2, b3, wb_ref, gp_ref, dgp_ref, dup_ref):
        dm = d_ref[...]
        for n, b_ref in enumerate((b0, b1, b2, b3)):
            up = _dot(b_ref[...], wb_ref[n], tb=True)
            g = _sigmoid(gp_ref[:, n * D:(n + 1) * D])
            dgp_ref[:, n * D:(n + 1) * D] = (dm * up * (g * (1.0 - g))).astype(BF16)
            dup_ref[:, n * D:(n + 1) * D] = (dm * g).astype(BF16)

    row = pl.BlockSpec((tm, D), lambda i: (i, 0))
    blk = pl.BlockSpec((tm, BW), lambda i: (i, 0))
    wide = pl.BlockSpec((tm, NH * D), lambda i: (i, 0))
    return pl.pallas_call(
        body, grid=(t // tm,), in_specs=[row, blk, blk, blk, blk, pl.BlockSpec((NH, D, BW), lambda i: (0, 0, 0)), wide],
        out_specs=(wide, wide), out_shape=(_sds((t, NH * D), BF16), _sds((t, NH * D), BF16)), name="merge_bwd",
        compiler_params=_cp())(dmerged, *brs, wbt, gp)


def _dbranch(dup, wbt):
    t = dup.shape[0]
    tm = 512

    def body(d_ref, w_ref, o_ref):
        o_ref[...] = _dot(d_ref[...], w_ref[...])

    return pl.pallas_call(
        body, grid=(t // tm, NH), in_specs=[pl.BlockSpec((tm, D), lambda i, n: (i, n)), pl.BlockSpec((None, D, BW), lambda i, n: (n, 0, 0))],
        out_specs=pl.BlockSpec((tm, BW), lambda i, n: (i, n)), out_shape=_sds((t, NH * BW), F32), name="dbranch",
        compiler_params=_cp())(dup, wbt)


def _dwbranch(brs, dup):
    t = dup.shape[0]

    def body(b0, b1, b2, b3, d_ref, o_ref):
        for n, b_ref in enumerate((b0, b1, b2, b3)):
            o_ref[n] = _dot(d_ref[:, n * D:(n + 1) * D], b_ref[...], ta=True).astype(BF16)

    return pl.pallas_call(body, out_shape=_sds((NH, D, BW), BF16), name="dwbranch", compiler_params=_cp())(*brs, dup)


def _swiglu_fwd(ag):
    t = ag.shape[0]
    tm = 256

    def body(ag_ref, o_ref):
        a, g = ag_ref[:, 0:FF], ag_ref[:, FF:2 * FF]
        o_ref[...] = (a * _sigmoid(a) * g).astype(BF16)

    return pl.pallas_call(body, grid=(t // tm,), in_specs=[pl.BlockSpec((tm, 2 * FF), lambda i: (i, 0))],
                          out_specs=pl.BlockSpec((tm, FF), lambda i: (i, 0)), out_shape=_sds((t, FF), BF16), name="swiglu_fwd",
                          compiler_params=_cp())(ag)


def _swiglu_bwd(ag, dy):
    t = ag.shape[0]
    tm = 256

    def body(ag_ref, dy_ref, o_ref):
        a, g, d = ag_ref[:, 0:FF], ag_ref[:, FF:2 * FF], dy_ref[...]
        s = _sigmoid(a)
        o_ref[:, 0:FF] = (d * g * (s * (1.0 + a * (1.0 - s)))).astype(BF16)
        o_ref[:, FF:2 * FF] = (d * (a * s)).astype(BF16)

    return pl.pallas_call(
        body, grid=(t // tm,), in_specs=[pl.BlockSpec((tm, 2 * FF), lambda i: (i, 0)), pl.BlockSpec((tm, FF), lambda i: (i, 0))],
        out_specs=pl.BlockSpec((tm, 2 * FF), lambda i: (i, 0)), out_shape=_sds((t, 2 * FF), BF16), name="swiglu_bwd",
        compiler_params=_cp())(ag, dy)


def _loss_head(y, target):
    t, d = y.shape
    tm = 256

    def body(y_ref, t_ref, dy_ref, l_ref):
        e = y_ref[...] - t_ref[...]
        dy_ref[...] = e * (1.0 / d)
        _acc(l_ref, jnp.full((8, 128), 0.5 * jnp.sum(jnp.sum(e * e, axis=-1, keepdims=True) * (1.0 / d)), F32), pl.program_id(0) == 0)

    row = pl.BlockSpec((tm, d), lambda i: (i, 0))
    return pl.pallas_call(body, grid=(t // tm,), in_specs=[row, row], out_specs=(row, pl.BlockSpec((8, 128), lambda i: (0, 0))),
                          out_shape=(_sds((t, d), F32), _sds((8, 128), F32)), name="loss_head", compiler_params=_cp())(y, target)


def _sum_slots(x, name):
    k, r, c = x.shape
    tr = _tile(r, 512) if r % 128 == 0 else r

    def body(x_ref, o_ref):
        acc = x_ref[0].astype(F32)
        for s in range(1, k):
            acc = acc + x_ref[s].astype(F32)
        o_ref[...] = acc

    return pl.pallas_call(body, grid=(r // tr,), in_specs=[pl.BlockSpec((k, tr, c), lambda i: (0, i, 0))],
                          out_specs=pl.BlockSpec((tr, c), lambda i: (i, 0)), out_shape=_sds((r, c), F32), name=name,
                          compiler_params=_cp())(x)


def _pair_sum(bufs, recvs, cidx):
    n = len(bufs)

    def body(c_ref, *refs):
        for i in range(n):
            refs[2 * n + i][...] = (refs[i][...].astype(F32) + refs[n + i][...].astype(F32)).astype(BF16)

    return pl.pallas_call(
        body,
        grid_spec=pltpu.PrefetchScalarGridSpec(
            num_scalar_prefetch=1, grid=(4,),
            in_specs=[pl.BlockSpec((None, None) + b.shape[2:], lambda s, cref: (s, cref[0], 0, 0)) for b in bufs]
            + [pl.BlockSpec((None,) + r.shape[1:], lambda s, cref: (s, 0, 0)) for r in recvs],
            out_specs=tuple(pl.BlockSpec((None,) + r.shape[1:], lambda s, cref: (s, 0, 0)) for r in recvs)),
        out_shape=tuple(_sds(r.shape, BF16) for r in recvs), name="rs_pair_sum", compiler_params=_cp())(cidx, *bufs, *recvs)


def _adamw(w, g, m, v, name):
    r, c = w.shape
    tr = r
    if r > 1024:
        tr = next(cand for cand in (512, 256, 128, 64, 32, 16, 8) if r % cand == 0)

    def body(w_ref, g_ref, m_ref, v_ref, d_ref, nm_ref, nv_ref):
        gv = g_ref[...]
        mn = ADAM_B1 * m_ref[...] + (1.0 - ADAM_B1) * gv
        vn = ADAM_B2 * v_ref[...] + (1.0 - ADAM_B2) * (gv * gv)
        m_hat = mn / (1.0 - ADAM_B1 ** ADAM_STEP)
        v_hat = vn / (1.0 - ADAM_B2 ** ADAM_STEP)
        d_ref[...] = -ADAM_LR * (m_hat / (jnp.sqrt(v_hat) + ADAM_EPS) + ADAM_WD * w_ref[...])
        nm_ref[...] = mn
        nv_ref[...] = vn

    blk = pl.BlockSpec((tr, c), lambda i: (i, 0))
    return pl.pallas_call(body, grid=(r // tr,), in_specs=[blk] * 4, out_specs=(blk,) * 3,
                          out_shape=tuple(_sds((r, c), F32) for _ in range(3)), name=name, compiler_params=_cp())(w, g, m, v)


def _all_gather(shards, name):
    n = len(shards)

    def body(*refs):
        x_refs, out_refs = refs[:n], refs[n:2 * n]
        send_sems, recv_sems, local_sems = refs[2 * n:]
        x, y, cc = lax.axis_index("x"), lax.axis_index("y"), lax.axis_index("c")
        me, sibling = (x, y, cc), (x, y, 1 - cc)
        chips = [(1 - x, y), (x, 1 - y), (1 - x, 1 - y)]

        def copy(i, k, block, to, own=False):
            px, py, pc = block
            slot = out_refs[i].at[4 * px + 2 * py + pc]
            return pltpu.make_async_remote_copy(
                src_ref=x_refs[i] if own else slot, dst_ref=slot, send_sem=send_sems.at[7 * i + k],
                recv_sem=recv_sems.at[7 * i + k], device_id=to, device_id_type=MESH)

        mine = [pltpu.make_async_copy(x_refs[i], out_refs[i].at[4 * x + 2 * y + cc], local_sems.at[i]) for i in range(n)]
        for cp in mine:
            cp.start()
        first = []
        for j, chip in enumerate(chips):
            first += [copy(i, 1 + j, me, (*chip, cc), own=True) for i in range(n)]
        first += [copy(i, 0, me, sibling, own=True) for i in range(n)]
        for cp in first:
            cp.start()
        passed = []
        for j, chip in enumerate(chips):
            for i in range(n):
                copy(i, 1 + j, (*chip, cc), me).wait_recv()
                cp = copy(i, 4 + j, (*chip, cc), sibling)
                cp.start()
                passed.append(cp)
        for i in range(n):
            copy(i, 0, sibling, me).wait_recv()
        for j, chip in enumerate(chips):
            for i in range(n):
                copy(i, 4 + j, (*chip, 1 - cc), me).wait_recv()
        for cp in first + passed:
            cp.wait_send()
        for cp in mine:
            cp.wait()

    return pl.pallas_call(
        body, out_shape=tuple(_sds((N_DEV,) + s.shape, s.dtype) for s in shards), in_specs=[ANY] * n, out_specs=(ANY,) * n,
        scratch_shapes=[pltpu.SemaphoreType.DMA((7 * n,)), pltpu.SemaphoreType.DMA((7 * n,)), pltpu.SemaphoreType.DMA((n,))],
        name=name)(*shards)


def _rs_core_swap(bufs, name):
    n = len(bufs)

    def body(*refs):
        b_refs, recv_refs = refs[:n], refs[n:2 * n]
        send_sems, recv_sems = refs[2 * n:]
        x, y, cc = lax.axis_index("x"), lax.axis_index("y"), lax.axis_index("c")
        copies = [pltpu.make_async_remote_copy(
            src_ref=b_refs[i].at[s, 1 - cc], dst_ref=recv_refs[i].at[s], send_sem=send_sems.at[4 * i + s],
            recv_sem=recv_sems.at[4 * i + s], device_id=(x, y, 1 - cc), device_id_type=MESH) for i in range(n) for s in range(4)]
        for cp in copies:
            cp.start()
        for cp in copies:
            cp.wait()

    return pl.pallas_call(
        body, out_shape=tuple(_sds((4,) + b.shape[2:], b.dtype) for b in bufs), in_specs=[ANY] * n, out_specs=(ANY,) * n,
        scratch_shapes=[pltpu.SemaphoreType.DMA((4 * n,)), pltpu.SemaphoreType.DMA((4 * n,))], name=name)(*bufs)


HBM = pl.BlockSpec(memory_space=pltpu.HBM)
SEMS = pl.BlockSpec(memory_space=pltpu.SEMAPHORE)
EFFECT = pltpu.SideEffectType.DATAFLOW_SIDE_EFFECTING


def _hbm(a):
    return pltpu.HBM(a.shape, a.dtype)


def _other_chips(x, y):
    return [(1 - x, y), (x, 1 - y), (1 - x, 1 - y)]


def _ici_start(srcs, lands, by_chip, name):
    n = len(srcs)

    def body(*refs):
        s_refs, land_refs = refs[:n], refs[n:2 * n]
        send_sems, recv_sems = refs[2 * n], refs[2 * n + 1]
        token = refs[-1]
        x, y, cc = lax.axis_index("x"), lax.axis_index("y"), lax.axis_index("c")
        mine = 2 * x + y if by_chip else 4 * x + 2 * y + cc
        for px, py in _other_chips(x, y):
            for i in range(n):
                pltpu.make_async_remote_copy(
                    src_ref=s_refs[i].at[2 * px + py] if by_chip else s_refs[i], dst_ref=land_refs[i].at[mine],
                    send_sem=send_sems.at[i], recv_sem=recv_sems.at[i], device_id=(px, py, cc), device_id_type=MESH).start()
        token[...] = jnp.zeros_like(token)

    out = pl.pallas_call(
        body, name=name,
        out_shape=(pltpu.SemaphoreType.DMA((n,)), pltpu.SemaphoreType.DMA((n,)), *[_hbm(s) for s in srcs], *[_hbm(l) for l in lands],
                   _sds((8, 128), F32)),
        in_specs=[HBM] * (2 * n), out_specs=(SEMS, SEMS, *[HBM] * (2 * n), pl.BlockSpec(memory_space=pltpu.VMEM)),
        input_output_aliases={i: 2 + i for i in range(2 * n)}, compiler_params=pltpu.CompilerParams(has_side_effects=EFFECT),
    )(*[pltpu.with_memory_space_constraint(s, pltpu.HBM) for s in srcs],
      *[pltpu.with_memory_space_constraint(l, pltpu.HBM) for l in lands])
    return out[0], out[1], out[2:2 + n], out[2 + n:2 + 2 * n], out[-1]


def _ici_wait(started, after, name):
    send_sems, recv_sems, srcs, lands, _ = started
    n = len(srcs)

    def body(*refs):
        land_refs = refs[n:2 * n]
        send_sems, recv_sems = refs[2 * n], refs[2 * n + 1]
        x, y, cc = lax.axis_index("x"), lax.axis_index("y"), lax.axis_index("c")
        for i in range(n):
            three = land_refs[i].at[pl.ds(0, 3)]
            cp = pltpu.make_async_remote_copy(src_ref=three, dst_ref=three, send_sem=send_sems.at[i], recv_sem=recv_sems.at[i],
                                              device_id=(x, y, cc), device_id_type=MESH)
            cp.wait_send()
            cp.wait_recv()

    return pl.pallas_call(
        body, name=name, out_shape=tuple(_hbm(l) for l in lands), in_specs=[HBM] * (2 * n) + [SEMS, SEMS, ANY],
        out_specs=tuple([HBM] * n), input_output_aliases={n + i: i for i in range(n)},
        compiler_params=pltpu.CompilerParams(has_side_effects=EFFECT))(*srcs, *lands, send_sems, recv_sems, after)


def _gather_d2d(blocks, lands, name):
    n = len(blocks)

    def body(*refs):
        x_refs, land_refs = refs[:n], refs[2 * n:3 * n]
        send_sems, recv_sems, in_sems, out_sems = refs[3 * n:3 * n + 4]
        stage = refs[3 * n + 4:]
        x, y, cc = lax.axis_index("x"), lax.axis_index("y"), lax.axis_index("c")
        sibling = (x, y, 1 - cc)
        staged = [pltpu.make_async_copy(x_refs[i], stage[i], in_sems.at[i]) for i in range(n)]
        for cp in staged:
            cp.start()
        copies = []
        for i in range(n):
            slot = land_refs[i].at[4 * x + 2 * y + cc]
            copies.append(pltpu.make_async_remote_copy(src_ref=x_refs[i], dst_ref=slot, send_sem=send_sems.at[4 * i],
                                                       recv_sem=recv_sems.at[4 * i], device_id=sibling, device_id_type=MESH))
            for j, (px, py) in enumerate(_other_chips(x, y)):
                slot = land_refs[i].at[4 * px + 2 * py + cc]
                copies.append(pltpu.make_async_remote_copy(src_ref=slot, dst_ref=slot, send_sem=send_sems.at[4 * i + 1 + j],
                                                           recv_sem=recv_sems.at[4 * i + 1 + j], device_id=sibling, device_id_type=MESH))
        for cp in copies:
            cp.start()
        mine = []
        for i in range(n):
            staged[i].wait()
            mine.append(pltpu.make_async_copy(stage[i], land_refs[i].at[4 * x + 2 * y + cc], out_sems.at[i]))
            mine[i].start()
        for i in range(n):
            slot = land_refs[i].at[4 * x + 2 * y + (1 - cc)]
            pltpu.make_async_remote_copy(src_ref=slot, dst_ref=slot, send_sem=send_sems.at[4 * i], recv_sem=recv_sems.at[4 * i],
                                         device_id=sibling, device_id_type=MESH).wait_recv()
            for j, (px, py) in enumerate(_other_chips(x, y)):
                slot = land_refs[i].at[4 * px + 2 * py + (1 - cc)]
                pltpu.make_async_remote_copy(src_ref=slot, dst_ref=slot, send_sem=send_sems.at[4 * i + 1 + j],
                                             recv_sem=recv_sems.at[4 * i + 1 + j], device_id=sibling, device_id_type=MESH).wait_recv()
        for cp in copies:
            cp.wait_send()
        for cp in mine:
            cp.wait()

    return pl.pallas_call(
        body, out_shape=tuple(_sds(l.shape, l.dtype) for l in lands), in_specs=[ANY] * (2 * n), out_specs=(ANY,) * n,
        input_output_aliases={n + i: i for i in range(n)},
        scratch_shapes=[pltpu.SemaphoreType.DMA((4 * n,)), pltpu.SemaphoreType.DMA((4 * n,)), pltpu.SemaphoreType.DMA((n,)),
                        pltpu.SemaphoreType.DMA((n,))] + [pltpu.VMEM(b.shape, b.dtype) for b in blocks],
        name=name, compiler_params=_cp())(*blocks, *lands)


def _sum_own(parts, recvs, chip, name):
    n = len(parts)

    def body(c_ref, *refs):
        s = pl.program_id(0)
        for i in range(n):
            val = jnp.where(c_ref[0] == s, refs[i][...], refs[n + i][...]).astype(F32)
            _acc(refs[2 * n + i], val, s == 0)

    ins = [pl.BlockSpec((None,) + p.shape[1:], lambda s, cref: (s, 0, 0)) for p in parts]
    return pl.pallas_call(
        body, grid_spec=pltpu.PrefetchScalarGridSpec(
            num_scalar_prefetch=1, grid=(4,), in_specs=ins + ins,
            out_specs=tuple(pl.BlockSpec(p.shape[1:], lambda s, cref: (0, 0)) for p in parts)),
        out_shape=tuple(_sds(p.shape[1:], F32) for p in parts), name=name, compiler_params=_cp())(chip, *parts, *recvs)


BIG = (("w_in", True), ("w_gate", True), ("w_mem_kv", False), ("w_branch", True), ("w_out", False), ("w_ffn_in", True),
       ("w_ffn_out", False))

SMALL = ("norm_mix_g", "norm_mem_g", "ret_decay_fwd", "ret_decay_bwd", "ret_norm_g", "pool_w", "pool_scale", "na_q_norm_g",
         "na_k_norm_g", "na_rpb", "mem_q_norm_g", "mem_k_norm_g", "norm_ffn_g")
WEIGHTS = ("norm_mix_g", "norm_mem_g", "w_in", "w_gate", "ret_decay_fwd", "ret_decay_bwd", "ret_norm_g", "pool_w", "pool_scale",
           "na_q_norm_g", "na_k_norm_g", "na_rpb", "mem_q_norm_g", "mem_k_norm_g", "w_mem_kv", "w_branch", "w_out", "norm_ffn_g",
           "w_ffn_in", "w_ffn_out")


def _to_exchange(name, transposed, shard):
    if name == "w_branch":
        return jnp.swapaxes(shard, 1, 2).reshape(NH * (D // N_DEV), BW)
    return shard.T if transposed else shard


def _from_exchange(name, transposed, block):
    if name == "w_branch":
        return jnp.swapaxes(block.reshape(NH, D // N_DEV, BW), 1, 2)
    return block.T if transposed else block


def _whole_from_gathered(name, g):
    if name == "w_branch":
        return jnp.swapaxes(g.reshape(N_DEV, NH, D // N_DEV, BW), 0, 1).reshape(NH, D, BW)
    return g.reshape(N_DEV * g.shape[1], g.shape[2])


def _by_destination(name, g):
    if name == "w_branch":
        g = jnp.swapaxes(g.reshape(NH, N_DEV, D // N_DEV, BW), 0, 1).reshape(N_DEV * NH * (D // N_DEV), BW)
    return g.reshape(4, 2, g.shape[0] // N_DEV, g.shape[1])


SMALL_PAD = 1024


def _pack_small(vals, loss=None):
    parts = [vals[n] for n in SMALL] + [jnp.zeros((1,), F32) if loss is None else loss.reshape(1)]
    rows = []
    for p in parts:
        flat = p.reshape(-1)
        rows.append(jnp.pad(flat, (0, -flat.shape[0] % SMALL_PAD)).reshape(-1, 128))
    return jnp.concatenate(rows, axis=0)


def _unpack_small(packed, like):
    out, off = {}, 0
    for n in SMALL:
        sz = int(np.prod(like[n].shape))
        nrow = -(-sz // SMALL_PAD) * (SMALL_PAD // 128)
        out[n] = packed[off:off + nrow].reshape(-1)[:sz].reshape(like[n].shape)
        off += nrow
    return out, packed[off, 0]


def _na_constants():
    c = np.arange(GRID_W)
    win = np.clip(c - NA_COLS_WIN // 2, 0, GRID_W - NA_COLS_WIN)
    kc = np.arange(GRID_W)
    inside = (kc[None, :] >= win[:, None]) & (kc[None, :] < win[:, None] + NA_COLS_WIN)
    off = kc[None, :] - c[:, None] + NA_COLS_WIN - 1
    onehot = np.zeros((128, GRID_W, GRID_W), np.float32)
    for b in range(2 * NA_COLS_WIN - 1):
        onehot[b] = (off == b) & inside
    maskadd = np.where(inside, 0.0, NEG).astype(np.float32)
    return onehot.reshape(128, GRID_W * GRID_W), maskadd


def _na_bias_table(tab, maskadd):
    n_off = 2 * NA_ROWS_WIN - 1
    t4 = tab[:NH * n_off].reshape(NH, n_off, GRID_W, GRID_W) + maskadd[None, None]
    ball = jnp.stack([t4[:, a0:a0 + NA_ROWS_WIN] for a0 in range(NA_ROWS_WIN)], axis=1)
    return ball.transpose(1, 0, 3, 2, 4).reshape(NA_ROWS_WIN, NH * GRID_W, NA_KEYS)


def _rotary_tables(t):
    half = HD // 2
    inv = ROPE_THETA ** (-jnp.arange(half, dtype=F32) / half)
    ang = jnp.arange(t, dtype=F32)[:, None] * inv[None, :]
    cos, sin = jnp.cos(ang), jnp.sin(ang)
    return jnp.tile(jnp.concatenate([cos, cos], axis=-1), (1, NH)), jnp.tile(jnp.concatenate([-sin, sin], axis=-1), (1, NH))


def _block_diag(pw):
    out = jnp.zeros((BW, BW), pw.dtype)
    for g in range(NH):
        out = lax.dynamic_update_slice(out, pw[g], (g * HD, g * HD))
    return out


def _tile4(g):
    return jnp.tile(g.reshape(1, HD), (1, NH))


def _layer_fwd(x, mem, sw, lw, consts):
    cos2, sin2, onehot, maskadd = consts
    h = _rmsnorm_fwd(x, sw["norm_mix_g"].reshape(1, D), "norm_mix_fwd")
    proj = _mm(h, lw["w_in"], tb=True, name="mm_in")
    gp = _mm(h, lw["w_gate"], tb=True, name="mm_gate")
    g_naq, g_nak, g_mq = _tile4(sw["na_q_norm_g"]), _tile4(sw["na_k_norm_g"]), _tile4(sw["mem_q_norm_g"])
    rq, rk, rv, nq, nk, nv, mq = _prep_fwd(proj, cos2, sin2, g_naq, g_nak, g_mq)

    lgf, lgb = jax.nn.log_sigmoid(sw["ret_decay_fwd"]), jax.nn.log_sigmoid(sw["ret_decay_bwd"])
    g_ret = sw["ret_norm_g"].reshape(1, BW)
    o_ret, ret = _ret_fwd(rq, rk, rv, proj, lgf, lgb, g_ret)

    wbd = _block_diag(sw["pool_w"]).astype(BF16)
    p_scale = sw["pool_scale"].reshape(1, BW)
    pool = _pool_fwd(proj, wbd, p_scale)

    rpb_pad = jnp.pad(sw["na_rpb"].reshape(NH * 15, 31), ((0, 4), (0, 97)))
    ball = _na_bias_table(_rpb_expand(rpb_pad, onehot), maskadd)
    na = _na_fwd(nq, nk, nv, ball)

    memn = _rmsnorm_fwd(mem, sw["norm_mem_g"].reshape(1, D), "norm_mem_fwd")
    kv = _mm(memn, lw["w_mem_kv"], name="mm_memkv")
    g_mk = _tile4(sw["mem_k_norm_g"])
    mk, mv = _memkv_prep(kv, g_mk)
    mo = _mem_fwd(mq, mk, mv)

    br = (ret, pool, na, mo)
    merged = _merge_fwd(br, lw["w_branch"], gp)
    x1 = _mm(merged, lw["w_out"], add=x, name="mm_out")
    h2 = _rmsnorm_fwd(x1, sw["norm_ffn_g"].reshape(1, D), "norm_ffn_fwd")
    ag = _mm(h2, lw["w_ffn_in"], tb=True, name="mm_ffn_in")
    yff = _swiglu_fwd(ag)
    x2 = _mm(yff, lw["w_ffn_out"], add=x1, name="mm_ffn_out")
    saved = dict(x=x, h=h, proj=proj, gp=gp, rq=rq, rk=rk, rv=rv, nq=nq, nk=nk, nv=nv, mq=mq, o_ret=o_ret, ball=ball, memn=memn,
                 kv=kv, mk=mk, mv=mv, br=br, merged=merged, x1=x1, h2=h2, ag=ag, yff=yff, lgf=lgf, lgb=lgb, wbd=wbd)
    return x2, saved


def _layer_bwd(dx2, mem, sw, lw, sv, consts, dep=None):
    cos2, sin2, onehot, maskadd = consts
    gb, gs = {}, {}
    dy = _mm(dx2, lw["w_ffn_out"], tb=True, dep=dep, name="mm_ffn_out_dx")
    gb["w_ffn_out"] = _mm(sv["yff"], dx2, ta=True, out_dtype=BF16, name="mm_ffn_out_dw")
    dag = _swiglu_bwd(sv["ag"], dy)
    dh2 = _mm(dag, lw["w_ffn_in"], name="mm_ffn_in_dx")
    gb["w_ffn_in"] = _mm(dag, sv["h2"], ta=True, out_dtype=BF16, name="mm_ffn_in_dw")
    dx1, dg = _rmsnorm_bwd(dh2, sv["x1"], sw["norm_ffn_g"].reshape(1, D), dx2, "norm_ffn_bwd")
    gs["norm_ffn_g"] = dg.reshape(D)

    dmerged = _mm(dx1, lw["w_out"], tb=True, name="mm_out_dx")
    gb["w_out"] = _mm(sv["merged"], dx1, ta=True, out_dtype=BF16, name="mm_out_dw")
    dgp, dup = _merge_bwd(dmerged, sv["br"], lw["w_branch"], sv["gp"])
    dbr = _dbranch(dup, lw["w_branch"])
    gb["w_branch"] = _dwbranch(sv["br"], dup)

    g_ret = sw["ret_norm_g"].reshape(1, BW)
    d_rq, d_rk, d_rv, d_rg, dg_ret, dlg = _ret_bwd(dbr, sv["o_ret"], sv["rq"], sv["rk"], sv["rv"], sv["proj"], sv["lgf"], sv["lgb"], g_ret)
    gs["ret_norm_g"] = dg_ret.reshape(BW)
    _, vjp_f = jax.vjp(jax.nn.log_sigmoid, sw["ret_decay_fwd"])
    _, vjp_b = jax.vjp(jax.nn.log_sigmoid, sw["ret_decay_bwd"])
    gs["ret_decay_fwd"] = vjp_f(dlg[0:NH, 0])[0]
    gs["ret_decay_bwd"] = vjp_b(dlg[NH:2 * NH, 0])[0]

    p_scale = sw["pool_scale"].reshape(1, BW)
    d_pv, dwbd, dscale = _pool_bwd(dbr, sv["proj"], sv["wbd"], p_scale)
    gs["pool_w"] = jnp.stack([dwbd[g * HD:(g + 1) * HD, g * HD:(g + 1) * HD] for g in range(NH)])
    gs["pool_scale"] = dscale.reshape(BW)

    d_nq, d_nk, d_nv, dball = _na_bwd(dbr, sv["nq"], sv["nk"], sv["nv"], sv["ball"])
    _, vjp_tab = jax.vjp(lambda tab: _na_bias_table(tab, maskadd), jnp.zeros((64, GRID_W * GRID_W), F32))
    drpb = _rpb_reduce(vjp_tab(dball)[0], onehot)
    gs["na_rpb"] = drpb[:NH * 15, :31].reshape(NH, 15, 31)

    d_mq, d_mk, d_mv = _mem_bwd(dbr, sv["mq"], sv["mk"], sv["mv"])
    g_mk = _tile4(sw["mem_k_norm_g"])
    dkv, dg_mk = _memkv_bwd(sv["kv"], d_mk, d_mv, g_mk)
    gs["mem_k_norm_g"] = dg_mk.reshape(NH, HD).sum(0)
    gb["w_mem_kv"] = _mm(sv["memn"], dkv, ta=True, out_dtype=BF16, name="mm_memkv_dw")
    dmemn = _mm(dkv, lw["w_mem_kv"], tb=True, name="mm_memkv_dx")
    _, dg_mem = _rmsnorm_bwd(dmemn, mem, sw["norm_mem_g"].reshape(1, D), jnp.zeros_like(mem), "norm_mem_bwd")
    gs["norm_mem_g"] = dg_mem.reshape(D)

    g_naq, g_nak, g_mq = _tile4(sw["na_q_norm_g"]), _tile4(sw["na_k_norm_g"]), _tile4(sw["mem_q_norm_g"])
    dproj, dg_naq, dg_nak, dg_mq = _prep_bwd(sv["proj"], cos2, sin2, g_naq, g_nak, g_mq, d_rq, d_rk, d_rv, d_rg, d_pv, d_nq, d_nk,
                                             d_nv, d_mq)
    gs["na_q_norm_g"] = dg_naq.reshape(NH, HD).sum(0)
    gs["na_k_norm_g"] = dg_nak.reshape(NH, HD).sum(0)
    gs["mem_q_norm_g"] = dg_mq.reshape(NH, HD).sum(0)

    dh = _mm(dproj, lw["w_in"], name="mm_in_dx")
    dh = _mm(dgp, lw["w_gate"], add=dh, name="mm_gate_dx")
    gb["w_in"] = _mm(dproj, sv["h"], ta=True, out_dtype=BF16, name="mm_in_dw")
    gb["w_gate"] = _mm(dgp, sv["h"], ta=True, out_dtype=BF16, name="mm_gate_dw")
    dx, dg = _rmsnorm_bwd(dh, sv["x"], sw["norm_mix_g"].reshape(1, D), dx1, "norm_mix_bwd")
    gs["norm_mix_g"] = dg.reshape(D)
    return dx, gb, gs


def _local_step(x, mem, target, small, get_layer, on_grads):
    t = x.shape[0]
    cos2, sin2 = _rotary_tables(t)
    onehot, maskadd = _na_constants()
    consts = (cos2, sin2, jnp.asarray(onehot), jnp.asarray(maskadd))
    saved, weights, cur = [], [], x
    for l in range(DEPTH):
        sw = {n: small[n][l] for n in SMALL}
        weights.append(get_layer(l, cur))
        cur, sv = _layer_fwd(cur, mem, sw, weights[l], consts)
        saved.append(sv)
    dy, loss_tile = _loss_head(cur, target)
    small_g = {n: [None] * DEPTH for n in SMALL}
    dep = None
    for l in reversed(range(DEPTH)):
        sw = {n: small[n][l] for n in SMALL}
        dy, gb, gs = _layer_bwd(dy, mem, sw, weights[l], saved[l], consts, dep)
        dep = on_grads(l, gb, dy)
        for n in SMALL:
            small_g[n][l] = gs[n]
    return loss_tile[0, 0], dy, {n: jnp.stack(v) for n, v in small_g.items()}


def _flat2d(a):
    return a.reshape(-1, a.shape[-1])


def kernel(x, mem, norm_mix_g, norm_mem_g, w_in, w_gate, ret_decay_fwd, ret_decay_bwd, ret_norm_g, pool_w, pool_scale, na_q_norm_g, na_k_norm_g, na_rpb, mem_q_norm_g, mem_k_norm_g, w_mem_kv, w_branch, w_out, norm_ffn_g, w_ffn_in, w_ffn_out, loss_target, m_norm_mix_g, m_norm_mem_g, m_w_in, m_w_gate, m_ret_decay_fwd, m_ret_decay_bwd, m_ret_norm_g, m_pool_w, m_pool_scale, m_na_q_norm_g, m_na_k_norm_g, m_na_rpb, m_mem_q_norm_g, m_mem_k_norm_g, m_w_mem_kv, m_w_branch, m_w_out, m_norm_ffn_g, m_w_ffn_in, m_w_ffn_out, v_norm_mix_g, v_norm_mem_g, v_w_in, v_w_gate, v_ret_decay_fwd, v_ret_decay_bwd, v_ret_norm_g, v_pool_w, v_pool_scale, v_na_q_norm_g, v_na_k_norm_g, v_na_rpb, v_mem_q_norm_g, v_mem_k_norm_g, v_w_mem_kv, v_w_branch, v_w_out, v_norm_ffn_g, v_w_ffn_in, v_w_ffn_out):
    w = dict(norm_mix_g=norm_mix_g, norm_mem_g=norm_mem_g, w_in=w_in, w_gate=w_gate, ret_decay_fwd=ret_decay_fwd,
             ret_decay_bwd=ret_decay_bwd, ret_norm_g=ret_norm_g, pool_w=pool_w, pool_scale=pool_scale, na_q_norm_g=na_q_norm_g,
             na_k_norm_g=na_k_norm_g, na_rpb=na_rpb, mem_q_norm_g=mem_q_norm_g, mem_k_norm_g=mem_k_norm_g, w_mem_kv=w_mem_kv,
             w_branch=w_branch, w_out=w_out, norm_ffn_g=norm_ffn_g, w_ffn_in=w_ffn_in, w_ffn_out=w_ffn_out)
    m = dict(norm_mix_g=m_norm_mix_g, norm_mem_g=m_norm_mem_g, w_in=m_w_in, w_gate=m_w_gate, ret_decay_fwd=m_ret_decay_fwd,
             ret_decay_bwd=m_ret_decay_bwd, ret_norm_g=m_ret_norm_g, pool_w=m_pool_w, pool_scale=m_pool_scale, na_q_norm_g=m_na_q_norm_g,
             na_k_norm_g=m_na_k_norm_g, na_rpb=m_na_rpb, mem_q_norm_g=m_mem_q_norm_g, mem_k_norm_g=m_mem_k_norm_g, w_mem_kv=m_w_mem_kv,
             w_branch=m_w_branch, w_out=m_w_out, norm_ffn_g=m_norm_ffn_g, w_ffn_in=m_w_ffn_in, w_ffn_out=m_w_ffn_out)
    v = dict(norm_mix_g=v_norm_mix_g, norm_mem_g=v_norm_mem_g, w_in=v_w_in, w_gate=v_w_gate, ret_decay_fwd=v_ret_decay_fwd,
             ret_decay_bwd=v_ret_decay_bwd, ret_norm_g=v_ret_norm_g, pool_w=v_pool_w, pool_scale=v_pool_scale, na_q_norm_g=v_na_q_norm_g,
             na_k_norm_g=v_na_k_norm_g, na_rpb=v_na_rpb, mem_q_norm_g=v_mem_q_norm_g, mem_k_norm_g=v_mem_k_norm_g, w_mem_kv=v_w_mem_kv,
             w_branch=v_w_branch, w_out=v_w_out, norm_ffn_g=v_norm_ffn_g, w_ffn_in=v_w_ffn_in, w_ffn_out=v_w_ffn_out)
    assert x.shape == (1, 2048, D) and mem.shape == (1, N_MEM, D) and w_in.shape == (DEPTH, D, 9 * BW // N_DEV)

    started = []
    for l in range(DEPTH):
        blocks = [_to_exchange(name, tr, w[name][l]).astype(BF16) for name, tr in BIG]
        lands = [lax.empty((N_DEV,) + b.shape, BF16) for b in blocks]
        started.append(_ici_start(blocks, lands, False, "gather_ici_start_%d" % l))
    all_started = started[0][4] + started[1][4] + started[2][4] + started[3][4]

    def get_layer(l, after):
        lands = _ici_wait(started[l], all_started if l == 0 else after, "gather_ici_wait_%d" % l)
        whole = _gather_d2d(started[l][2], lands, "gather_d2d")
        return {name: _whole_from_gathered(name, g) for (name, _), g in zip(BIG, whole)}

    cidx = lax.axis_index("c").astype(jnp.int32).reshape(1)
    chip = (2 * lax.axis_index("x") + lax.axis_index("y")).astype(jnp.int32).reshape(1)
    in_flight, g_layers = [], [None] * DEPTH

    def finish(l, st, after):
        recv = _ici_wait(st, after, "rs_ici_wait_%d" % l)
        sums = _sum_own(st[2], recv, chip, "rs_chip_sum")
        g_layers[l] = {name: _from_exchange(name, tr, s) if name == "w_branch" else s for (name, tr), s in zip(BIG, sums)}

    def on_grads(l, gb, after):
        send = [_by_destination(name, gb[name]) for name, _ in BIG]
        from_core = _rs_core_swap(send, "rs_core_swap")
        chip_part = _pair_sum(send, from_core, cidx)
        st = _ici_start(chip_part, [lax.empty(p.shape, BF16) for p in chip_part], True, "rs_ici_start_%d" % l)
        if in_flight:
            finish(*in_flight.pop(), after)
        in_flight.append((l, st))
        if l == 0:
            finish(*in_flight.pop(), st[4])
        return st[4]

    loss_local, dx, small_g = _local_step(x[0], mem[0], loss_target[0], {n: w[n] for n in SMALL}, get_layer, on_grads)
    g_shard = {name: jnp.stack([g_layers[l][name] for l in range(DEPTH)]) for name, _ in BIG}

    small_all, = _all_gather([_pack_small(small_g, loss_local)], "gather_small")
    packed_g = _sum_slots(small_all, "small_sum")
    small_sum, loss = _unpack_small(packed_g, {n: w[n] for n in SMALL})

    grads, delta, new_m, new_v = {}, {}, {}, {}
    for name, tr in BIG:
        flip = (lambda a: jnp.swapaxes(a, 1, 2)) if (tr and name != "w_branch") else (lambda a: a)
        wx, gx, mx, vx = flip(w[name]), g_shard[name], flip(m[name]), flip(v[name])
        d_, m_, v_ = _adamw(_flat2d(wx), _flat2d(gx), _flat2d(mx), _flat2d(vx), "adamw_" + name)
        grads[name] = flip(gx)
        delta[name], new_m[name], new_v[name] = (flip(a.reshape(wx.shape)) for a in (d_, m_, v_))
    d_, m_, v_ = _adamw(_pack_small({n: w[n] for n in SMALL}), packed_g, _pack_small({n: m[n] for n in SMALL}),
                        _pack_small({n: v[n] for n in SMALL}), "adamw_small")
    like = {n: w[n] for n in SMALL}
    ds, _ = _unpack_small(d_, like)
    ms, _ = _unpack_small(m_, like)
    vs, _ = _unpack_small(v_, like)
    for n in SMALL:
        grads[n], delta[n], new_m[n], new_v[n] = small_sum[n], ds[n], ms[n], vs[n]

    return (loss, dx[None], *[grads[n] for n in WEIGHTS], *[delta[n] for n in WEIGHTS], *[new_m[n] for n in WEIGHTS],
            *[new_v[n] for n in WEIGHTS])
```

```python
import functools

import numpy as np
import jax
import jax.numpy as jnp
from jax import lax
from jax.experimental import pallas as pl
from jax.experimental.pallas import tpu as pltpu

F32 = jnp.float32
BF16 = jnp.bfloat16
MXU = jnp.bfloat16
HI = lax.Precision.HIGHEST

DEPTH = 4
D = 1024
BW = 256
HD = 64
NH = 4
GRID_W = 64
NA_ROWS_WIN = 8
NA_COLS_WIN = 16
N_MEM = 256
FF = 2816
EPS = 1e-6
NEG = -1e30
ROPE_THETA = 10000.0
POOL_HALF_MAX = 8

ADAM_LR, ADAM_B1, ADAM_B2, ADAM_EPS, ADAM_WD, ADAM_STEP = 0.001, 0.9, 0.999, 1e-08, 0.01, 10

N_DEV = 8
VMEM_LIMIT = 56 * 1024 * 1024

RQ, RK, RV, RG, PV, NQ, NK, NV, MQ = range(9)

MESH = pl.DeviceIdType.MESH
ANY = pl.BlockSpec(memory_space=pl.ANY)
SMEM = pl.BlockSpec(memory_space=pltpu.SMEM)


def _cp(**kw):
    return pltpu.CompilerParams(vmem_limit_bytes=VMEM_LIMIT, **kw)


def _tile(n, cap):
    if n <= cap:
        return n
    best = None
    for t in range(128, cap + 1, 128):
        if n % t == 0:
            best = t
    assert best is not None, (n, cap)
    return best


def _sds(shape, dtype):
    return jax.ShapeDtypeStruct(shape, dtype)


def _lane_head(shape):
    return lax.shift_right_logical(lax.broadcasted_iota(jnp.int32, shape, len(shape) - 1), 6)


def _group_mean(z):
    i = lax.shift_right_logical(lax.broadcasted_iota(jnp.int32, (BW, BW), 0), 6)
    j = lax.shift_right_logical(lax.broadcasted_iota(jnp.int32, (BW, BW), 1), 6)
    g = jnp.where(i == j, 1.0 / HD, 0.0).astype(F32)
    return jnp.dot(z, g, precision=HI, preferred_element_type=F32)


def _gnorm(t, g):
    r = lax.rsqrt(_group_mean(t * t) + EPS)
    return t * r * g


def _gnorm_bwd(dy, t, g):
    r = lax.rsqrt(_group_mean(t * t) + EPS)
    th = t * r
    dth = dy * g
    dt = r * (dth - th * _group_mean(dth * th))
    return dt, dy * th


def _swap_halves(t):
    lane = lax.broadcasted_iota(jnp.int32, t.shape, 1)
    return jnp.where((lane & 63) < 32, pltpu.roll(t, BW - 32, 1), pltpu.roll(t, 32, 1))


def _sigmoid(x):
    return 1.0 / (1.0 + jnp.exp(-x))


def _dot(a, b, ta=False, tb=False):
    return lax.dot_general(a.astype(MXU), b.astype(MXU), (((0 if ta else 1,), (1 if tb else 0,)), ((), ())),
                           preferred_element_type=F32)


def _stack_heads(t):
    head = _lane_head(t.shape)
    return jnp.concatenate([jnp.where(head == h, t, jnp.zeros_like(t)) for h in range(NH)], axis=0)


def _unstack_heads(t, rows):
    head = _lane_head((rows, BW))
    out = jnp.zeros((rows, BW), F32)
    for h in range(NH):
        out = out + jnp.where(head == h, t[h * rows:(h + 1) * rows], 0.0)
    return out


def _softmax_rows(s):
    m = jnp.max(s, axis=-1, keepdims=True)
    e = jnp.exp(s - m)
    return e / jnp.sum(e, axis=-1, keepdims=True)


def _acc(ref, val, first):
    @pl.when(first)
    def _():
        ref[...] = val

    @pl.when(jnp.logical_not(first))
    def _():
        ref[...] += val


def _mm(a, b, *, ta=False, tb=False, out_dtype=F32, add=None, dep=None, name):
    m, k = (a.shape[1], a.shape[0]) if ta else a.shape
    n = b.shape[0] if tb else b.shape[1]
    tm, tn = _tile(m, 1024), _tile(n, 512)

    def body(*refs):
        if add is None:
            a_ref, b_ref, o_ref = refs[:2] + refs[-1:]
            r = _dot(a_ref[...], b_ref[...], ta, tb)
        else:
            a_ref, b_ref, c_ref, o_ref = refs[:3] + refs[-1:]
            r = _dot(a_ref[...], b_ref[...], ta, tb) + c_ref[...]
        o_ref[...] = r.astype(out_dtype)

    a_spec = pl.BlockSpec((k, tm), lambda i, j: (0, i)) if ta else pl.BlockSpec((tm, k), lambda i, j: (i, 0))
    b_spec = pl.BlockSpec((tn, k), lambda i, j: (j, 0)) if tb else pl.BlockSpec((k, tn), lambda i, j: (0, j))
    o_spec = pl.BlockSpec((tm, tn), lambda i, j: (i, j))
    ins, args = [a_spec, b_spec], [a, b]
    if add is not None:
        ins.append(o_spec)
        args.append(add)
    if dep is not None:
        ins.append(pl.BlockSpec((8, 128), lambda i, j: (0, 0)))
        args.append(dep)
    return pl.pallas_call(
        body, grid=(m // tm, n // tn), in_specs=ins, out_specs=o_spec, out_shape=_sds((m, n), out_dtype), name=name,
        compiler_params=_cp(dimension_semantics=("parallel", "parallel")))(*args)


def _rmsnorm_fwd(x, g, name):
    t, d = x.shape
    tm = _tile(t, 256)

    def body(x_ref, g_ref, o_ref):
        xv = x_ref[...]
        r = lax.rsqrt(jnp.mean(xv * xv, axis=-1, keepdims=True) + EPS)
        o_ref[...] = (xv * r * g_ref[...]).astype(o_ref.dtype)

    return pl.pallas_call(
        body, grid=(t // tm,), in_specs=[pl.BlockSpec((tm, d), lambda i: (i, 0)), pl.BlockSpec((1, d), lambda i: (0, 0))],
        out_specs=pl.BlockSpec((tm, d), lambda i: (i, 0)), out_shape=_sds((t, d), BF16), name=name, compiler_params=_cp())(x, g)


def _rmsnorm_bwd(dh, x, g, res, name):
    t, d = x.shape
    tm = _tile(t, 256)

    def body(dh_ref, x_ref, g_ref, res_ref, dx_ref, dg_ref):
        xv = x_ref[...]
        dhv = dh_ref[...]
        r = lax.rsqrt(jnp.mean(xv * xv, axis=-1, keepdims=True) + EPS)
        xh = xv * r
        dxh = dhv * g_ref[...]
        dx_ref[...] = res_ref[...] + r * (dxh - xh * jnp.mean(dxh * xh, axis=-1, keepdims=True))
        _acc(dg_ref, jnp.sum(dhv * xh, axis=0, keepdims=True), pl.program_id(0) == 0)

    row = pl.BlockSpec((tm, d), lambda i: (i, 0))
    vec = pl.BlockSpec((1, d), lambda i: (0, 0))
    return pl.pallas_call(
        body, grid=(t // tm,), in_specs=[row, row, vec, row], out_specs=(row, vec),
        out_shape=(_sds((t, d), F32), _sds((1, d), F32)), name=name, compiler_params=_cp())(dh, x, g, res)


def _prep_fwd(proj, cos2, sin2, g_naq, g_nak, g_mq):
    t = proj.shape[0]
    tm = 256

    def body(p_ref, cos_ref, sin_ref, gq_ref, gk_ref, gm_ref, rq_ref, rk_ref, rv_ref, nq_ref, nk_ref, nv_ref, mq_ref):
        def col(c):
            return p_ref[:, c * BW:(c + 1) * BW]

        cosv, sinv = cos_ref[...], sin_ref[...]

        def rot(tv):
            return tv * cosv + _swap_halves(tv) * sinv

        rq_ref[...] = (rot(col(RQ)) * (HD ** -0.5)).astype(BF16)
        rk_ref[...] = rot(col(RK)).astype(BF16)
        rv_ref[...] = col(RV).astype(BF16)
        nq_ref[...] = _gnorm(col(NQ), gq_ref[...]).astype(BF16)
        nk_ref[...] = _gnorm(col(NK), gk_ref[...]).astype(BF16)
        nv_ref[...] = col(NV).astype(BF16)
        mq_ref[...] = _gnorm(col(MQ), gm_ref[...]).astype(BF16)

    blk = pl.BlockSpec((tm, BW), lambda i: (i, 0))
    vec = pl.BlockSpec((1, BW), lambda i: (0, 0))
    return pl.pallas_call(
        body, grid=(t // tm,), in_specs=[pl.BlockSpec((tm, 9 * BW), lambda i: (i, 0)), blk, blk, vec, vec, vec],
        out_specs=tuple(blk for _ in range(7)), out_shape=tuple(_sds((t, BW), BF16) for _ in range(7)),
        name="prep_fwd", compiler_params=_cp())(proj, cos2, sin2, g_naq, g_nak, g_mq)


def _prep_bwd(proj, cos2, sin2, g_naq, g_nak, g_mq, d_rq, d_rk, d_rv, d_rg, d_pv, d_nq, d_nk, d_nv, d_mq):
    t = proj.shape[0]
    tm = 256

    def body(p_ref, cos_ref, sin_ref, gq_ref, gk_ref, gm_ref, drq_ref, drk_ref, drv_ref, drg_ref, dpv_ref, dnq_ref, dnk_ref,
             dnv_ref, dmq_ref, o_ref, dgq_ref, dgk_ref, dgm_ref):
        first = pl.program_id(0) == 0

        def col(c):
            return p_ref[:, c * BW:(c + 1) * BW]

        def put(c, v):
            o_ref[:, c * BW:(c + 1) * BW] = v.astype(BF16)

        cosv, sinv = cos_ref[...], sin_ref[...]

        def rot_t(dv):
            return dv * cosv + _swap_halves(dv * sinv)

        put(RQ, rot_t(drq_ref[...] * (HD ** -0.5)))
        put(RK, rot_t(drk_ref[...]))
        put(RV, drv_ref[...])
        put(RG, drg_ref[...])
        put(PV, dpv_ref[...])
        dq, gq = _gnorm_bwd(dnq_ref[...], col(NQ), gq_ref[...])
        put(NQ, dq)
        _acc(dgq_ref, jnp.sum(gq, axis=0, keepdims=True), first)
        dk, gk = _gnorm_bwd(dnk_ref[...], col(NK), gk_ref[...])
        put(NK, dk)
        _acc(dgk_ref, jnp.sum(gk, axis=0, keepdims=True), first)
        put(NV, dnv_ref[...])
        dm, gm = _gnorm_bwd(dmq_ref[...], col(MQ), gm_ref[...])
        put(MQ, dm)
        _acc(dgm_ref, jnp.sum(gm, axis=0, keepdims=True), first)

    blk = pl.BlockSpec((tm, BW), lambda i: (i, 0))
    vec = pl.BlockSpec((1, BW), lambda i: (0, 0))
    wide = pl.BlockSpec((tm, 9 * BW), lambda i: (i, 0))
    return pl.pallas_call(
        body, grid=(t // tm,), in_specs=[wide, blk, blk, vec, vec, vec] + [blk] * 9, out_specs=(wide, vec, vec, vec),
        out_shape=(_sds((t, 9 * BW), BF16), _sds((1, BW), F32), _sds((1, BW), F32), _sds((1, BW), F32)),
        name="prep_bwd", compiler_params=_cp())(proj, cos2, sin2, g_naq, g_nak, g_mq, d_rq, d_rk, d_rv, d_rg, d_pv, d_nq, d_nk,
                                                d_nv, d_mq)


RET_B = 256


def _ret_consts(lgf_ref, lgb_ref):
    bsz = RET_B
    head = _lane_head((1, BW))
    lf, lb = jnp.zeros((1, BW), F32), jnp.zeros((1, BW), F32)
    for h in range(NH):
        lf = lf + jnp.where(head == h, lgf_ref[h], 0.0)
        lb = lb + jnp.where(head == h, lgb_ref[h], 0.0)
    pos = lax.broadcasted_iota(jnp.int32, (bsz, BW), 0).astype(F32)
    up, down = pos + 1.0, (bsz - 1.0) - pos
    c = dict(up=up, down=down, kf=jnp.exp(down * lf), kb=jnp.exp(up * lb), qf=jnp.exp(up * lf), qb=jnp.exp(down * lb),
             cf=jnp.exp(bsz * lf), cb=jnp.exp(bsz * lb))
    diff = (lax.broadcasted_iota(jnp.int32, (NH * bsz, 1), 0) & (bsz - 1)) - lax.broadcasted_iota(jnp.int32, (1, bsz), 1)
    c["causal"] = diff >= 0
    c["dist"] = jnp.abs(diff).astype(F32)
    lgf = jnp.concatenate([jnp.full((bsz, 1), lgf_ref[h], F32) for h in range(NH)], axis=0)
    lgb = jnp.concatenate([jnp.full((bsz, 1), lgb_ref[h], F32) for h in range(NH)], axis=0)
    c["dm"] = jnp.exp(c["dist"] * jnp.where(c["causal"], lgf, lgb))
    c["bd"] = _lane_head((BW, BW)) == lax.shift_right_logical(lax.broadcasted_iota(jnp.int32, (BW, BW), 0), 6)
    return c


def _ret_states(k_ref, v_ref, st_ref, c, nb):
    bsz = RET_B

    def summary(b, decay):
        kb = k_ref[b * bsz:(b + 1) * bsz, :].astype(F32)
        return jnp.where(c["bd"], _dot(kb * decay, v_ref[b * bsz:(b + 1) * bsz, :], ta=True), 0.0)

    f = jnp.zeros((BW, BW), F32)
    for b in range(nb):
        st_ref[b] = f
        if b < nb - 1:
            f = c["cf"] * f + summary(b, c["kf"])
    g = jnp.zeros((BW, BW), F32)
    for b in reversed(range(nb)):
        st_ref[nb + b] = g
        if b > 0:
            g = c["cb"] * g + summary(b, c["kb"])


def _ret_fwd(q, k, v, proj, lgf, lgb, g_ret):
    t = q.shape[0]
    bsz, nb = RET_B, t // RET_B

    def body(lgf_ref, lgb_ref, q_ref, k_ref, v_ref, rg_ref, g_ref, o_ref, ret_ref, st_ref):
        c = _ret_consts(lgf_ref, lgb_ref)
        _ret_states(k_ref, v_ref, st_ref, c, nb)
        for b in range(nb):
            blk = slice(b * bsz, (b + 1) * bsz)
            qb, kb, vb = q_ref[blk, :], k_ref[blk, :], v_ref[blk, :]
            s = _dot(_stack_heads(qb), kb, tb=True)
            o = _unstack_heads(_dot(s * c["dm"], vb), bsz)
            q32 = qb.astype(F32)
            o = o + _dot(q32 * c["qf"], st_ref[b]) + _dot(q32 * c["qb"], st_ref[nb + b])
            o_ref[blk, :] = o
            rg = rg_ref[blk, :]
            ret_ref[blk, :] = (_gnorm(o, g_ref[...]) * (rg * _sigmoid(rg))).astype(BF16)

    whole = pl.BlockSpec((t, BW), lambda i: (0, 0))
    return pl.pallas_call(
        body, grid=(1,),
        in_specs=[SMEM, SMEM, whole, whole, whole, pl.BlockSpec((t, BW), lambda i: (0, RG)), pl.BlockSpec((1, BW), lambda i: (0, 0))],
        out_specs=(whole, whole), out_shape=(_sds((t, BW), F32), _sds((t, BW), BF16)),
        scratch_shapes=[pltpu.VMEM((2 * nb, BW, BW), F32)], name="ret_fwd", compiler_params=_cp())(lgf, lgb, q, k, v, proj, g_ret)


def _ret_post_bwd(dbr, o_ret, proj, g_ret):
    t = o_ret.shape[0]
    tm = 256

    def body(d_ref, o_ref, rg_ref, g_ref, do_ref, drg_ref, dg_ref):
        dret, o, rg, g = d_ref[...], o_ref[...], rg_ref[...], g_ref[...]
        sg = _sigmoid(rg)
        do, dgain = _gnorm_bwd(dret * (rg * sg), o, g)
        do_ref[...] = do.astype(BF16)
        drg_ref[...] = dret * _gnorm(o, g) * (sg * (1.0 + rg * (1.0 - sg)))
        _acc(dg_ref, jnp.sum(dgain, axis=0, keepdims=True), pl.program_id(0) == 0)

    blk = pl.BlockSpec((tm, BW), lambda i: (i, 0))
    vec = pl.BlockSpec((1, BW), lambda i: (0, 0))
    return pl.pallas_call(
        body, grid=(t // tm,), in_specs=[blk, blk, pl.BlockSpec((tm, BW), lambda i: (i, RG)), vec], out_specs=(blk, blk, vec),
        out_shape=(_sds((t, BW), BF16), _sds((t, BW), F32), _sds((1, BW), F32)), name="ret_post_bwd",
        compiler_params=_cp())(dbr, o_ret, proj, g_ret)


def _ret_bwd(do, q, k, v, lgf, lgb):
    t = q.shape[0]
    bsz, nb = RET_B, t // RET_B

    def body(lgf_ref, lgb_ref, d_ref, q_ref, k_ref, v_ref, dq_ref, dk_ref, dv_ref, dlg_ref, st_ref, sd_ref):
        c = _ret_consts(lgf_ref, lgb_ref)
        _ret_states(k_ref, v_ref, st_ref, c, nb)
        lane_f, lane_b = jnp.zeros((1, BW), F32), jnp.zeros((1, BW), F32)
        row_f, row_b = jnp.zeros((NH * bsz, 1), F32), jnp.zeros((NH * bsz, 1), F32)

        def rows(x):
            return jnp.sum(x, axis=0, keepdims=True)

        for b in range(nb):
            blk = slice(b * bsz, (b + 1) * bsz)
            qb, kb, vb, dob = q_ref[blk, :], k_ref[blk, :], v_ref[blk, :], d_ref[blk, :]
            q32 = qb.astype(F32)
            qs, dos = _stack_heads(qb), _stack_heads(dob)
            s = _dot(qs, kb, tb=True)
            da = _dot(dos, vb, tb=True)
            dv_ref[blk, :] = _dot(s * c["dm"], dos, ta=True)
            ds = da * c["dm"]
            w = ds * s * c["dist"]
            row_f = row_f + jnp.sum(jnp.where(c["causal"], w, 0.0), axis=1, keepdims=True)
            row_b = row_b + jnp.sum(jnp.where(c["causal"], 0.0, w), axis=1, keepdims=True)
            dsb = ds.astype(MXU)
            dk_ref[blk, :] = _dot(dsb, qs, ta=True)
            dq_f = _dot(dob, st_ref[b], tb=True) * c["qf"]
            dq_b = _dot(dob, st_ref[nb + b], tb=True) * c["qb"]
            lane_f = lane_f + rows(c["up"] * dq_f * q32)
            lane_b = lane_b + rows(c["down"] * dq_b * q32)
            dq_ref[blk, :] = _unstack_heads(_dot(dsb, kb), bsz) + dq_f + dq_b
            sd_ref[b] = jnp.where(c["bd"], _dot(q32 * c["qf"], dob, ta=True), 0.0)
            sd_ref[nb + b] = jnp.where(c["bd"], _dot(q32 * c["qb"], dob, ta=True), 0.0)

        def through_state(b, grad, decay, weight, lane):
            blk = slice(b * bsz, (b + 1) * bsz)
            k32 = k_ref[blk, :].astype(F32)
            dk = _dot(v_ref[blk, :], grad, tb=True) * decay
            dk_ref[blk, :] += dk
            dv_ref[blk, :] += _dot(k32 * decay, grad)
            return lane + rows(weight * dk * k32)

        phi = jnp.zeros((BW, BW), F32)
        for b in reversed(range(nb)):
            if b < nb - 1:
                lane_f = through_state(b, phi, c["kf"], c["down"], lane_f)
                lane_f = lane_f + bsz * rows(c["cf"] * st_ref[b] * phi)
            phi = sd_ref[b] + c["cf"] * phi
        gam = jnp.zeros((BW, BW), F32)
        for b in range(nb):
            if b > 0:
                lane_b = through_state(b, gam, c["kb"], c["up"], lane_b)
                lane_b = lane_b + bsz * rows(c["cb"] * st_ref[nb + b] * gam)
            gam = sd_ref[nb + b] + c["cb"] * gam

        head = _lane_head((1, BW))
        for h in range(NH):
            tot_f = jnp.sum(row_f[h * bsz:(h + 1) * bsz, :]) + jnp.sum(jnp.where(head == h, lane_f, 0.0))
            tot_b = jnp.sum(row_b[h * bsz:(h + 1) * bsz, :]) + jnp.sum(jnp.where(head == h, lane_b, 0.0))
            dlg_ref[h:h + 1, :] = jnp.full((1, 128), tot_f, F32)
            dlg_ref[NH + h:NH + h + 1, :] = jnp.full((1, 128), tot_b, F32)

    whole = pl.BlockSpec((t, BW), lambda i: (0, 0))
    return pl.pallas_call(
        body, grid=(1,), in_specs=[SMEM, SMEM, whole, whole, whole, whole],
        out_specs=(whole, whole, whole, pl.BlockSpec((2 * NH, 128), lambda i: (0, 0))),
        out_shape=(_sds((t, BW), F32), _sds((t, BW), F32), _sds((t, BW), F32), _sds((2 * NH, 128), F32)),
        scratch_shapes=[pltpu.VMEM((2 * nb, BW, BW), F32), pltpu.VMEM((2 * nb, BW, BW), F32)], name="ret_bwd",
        compiler_params=_cp())(lgf, lgb, do, q, k, v)


def _pool_windows(t):
    row = lax.broadcasted_iota(jnp.int32, (t, BW), 0)
    half = lax.shift_left(jnp.ones((t, BW), jnp.int32), _lane_head((t, BW)))
    cnt = (jnp.minimum(row + half, t) - jnp.maximum(row - half, 0)).astype(F32)
    return row, half, cnt


def _pool_window_sum(v, row, half, t, transpose):
    out = jnp.zeros_like(v)
    for j in range(-POOL_HALF_MAX, POOL_HALF_MAX):
        src = row - j if transpose else row + j
        ok = (src >= 0) & (src < t) & (j >= -half) & (j < half)
        out = out + jnp.where(ok, pltpu.roll(v, (j if transpose else -j) % t, 0), 0.0)
    return out


def _pool_fwd(proj, wbd, scale):
    t = proj.shape[0]

    def body(v_ref, w_ref, s_ref, o_ref):
        v = v_ref[...]
        row, half, cnt = _pool_windows(t)
        pooled = _pool_window_sum(v, row, half, t, False) / cnt - v
        o_ref[...] = (_dot(pooled, w_ref[...]) * s_ref[...]).astype(BF16)

    return pl.pallas_call(
        body, grid=(1,),
        in_specs=[pl.BlockSpec((t, BW), lambda i: (0, PV)), pl.BlockSpec((BW, BW), lambda i: (0, 0)), pl.BlockSpec((1, BW), lambda i: (0, 0))],
        out_specs=pl.BlockSpec((t, BW), lambda i: (0, 0)), out_shape=_sds((t, BW), BF16), name="pool_fwd",
        compiler_params=_cp())(proj, wbd, scale)


def _pool_bwd(dbr, proj, wbd, scale):
    t = proj.shape[0]

    def body(d_ref, v_ref, w_ref, s_ref, dv_ref, dw_ref, ds_ref):
        v, dout = v_ref[...], d_ref[...]
        row, half, cnt = _pool_windows(t)
        pooled = _pool_window_sum(v, row, half, t, False) / cnt - v
        mixed = _dot(pooled, w_ref[...])
        ds_ref[...] = jnp.sum(dout * mixed, axis=0, keepdims=True)
        dmixed = dout * s_ref[...]
        dw_ref[...] = _dot(pooled, dmixed, ta=True)
        dpooled = _dot(dmixed, w_ref[...], tb=True)
        dv_ref[...] = _pool_window_sum(dpooled / cnt, row, half, t, True) - dpooled

    return pl.pallas_call(
        body, grid=(1,),
        in_specs=[pl.BlockSpec((t, BW), lambda i: (0, 1)), pl.BlockSpec((t, BW), lambda i: (0, PV)),
                  pl.BlockSpec((BW, BW), lambda i: (0, 0)), pl.BlockSpec((1, BW), lambda i: (0, 0))],
        out_specs=(pl.BlockSpec((t, BW), lambda i: (0, 0)), pl.BlockSpec((BW, BW), lambda i: (0, 0)), pl.BlockSpec((1, BW), lambda i: (0, 0))),
        out_shape=(_sds((t, BW), F32), _sds((BW, BW), F32), _sds((1, BW), F32)), name="pool_bwd",
        compiler_params=_cp())(dbr, proj, wbd, scale)


NA_KEYS = NA_ROWS_WIN * GRID_W


def _na_window(r, n_rows):
    rs = jnp.clip(r - NA_ROWS_WIN // 2, 0, n_rows - NA_ROWS_WIN)
    return pl.multiple_of(rs * GRID_W, GRID_W), rs - r + (NA_ROWS_WIN - 1)


def _na_fwd(q, k, v, ball):
    t = q.shape[0]
    n_rows = t // GRID_W

    def body(q_ref, k_ref, v_ref, b_ref, o_ref):
        start, a0 = _na_window(pl.program_id(0), n_rows)
        qs = _stack_heads(q_ref[...])
        s = _dot(qs, k_ref[pl.ds(start, NA_KEYS), :], tb=True) * (HD ** -0.5) + b_ref[a0]
        p = _softmax_rows(s)
        o_ref[...] = _unstack_heads(_dot(p, v_ref[pl.ds(start, NA_KEYS), :]), GRID_W).astype(BF16)

    blk = pl.BlockSpec((GRID_W, BW), lambda r: (r, 0))
    whole = pl.BlockSpec((t, BW), lambda r: (0, 0))
    return pl.pallas_call(
        body, grid=(n_rows,), in_specs=[blk, whole, whole, pl.BlockSpec(ball.shape, lambda r: (0, 0, 0))],
        out_specs=blk, out_shape=_sds((t, BW), BF16), name="na_fwd", compiler_params=_cp())(q, k, v, ball)


def _na_bwd(dbr, q, k, v, ball):
    t = q.shape[0]
    n_rows = t // GRID_W

    def body(d_ref, q_ref, k_ref, v_ref, b_ref, dq_ref, dk_ref, dv_ref, db_ref):
        r = pl.program_id(0)
        start, a0 = _na_window(r, n_rows)
        keys = pl.ds(start, NA_KEYS)

        @pl.when(r == 0)
        def _():
            dk_ref[...] = jnp.zeros_like(dk_ref)
            dv_ref[...] = jnp.zeros_like(dv_ref)
            db_ref[...] = jnp.zeros_like(db_ref)

        qs = _stack_heads(q_ref[...])
        kb, vb = k_ref[keys, :], v_ref[keys, :]
        p = _softmax_rows(_dot(qs, kb, tb=True) * (HD ** -0.5) + b_ref[a0])
        dos = _stack_heads(d_ref[...]).astype(MXU)
        dp = _dot(dos, vb, tb=True)
        dv_ref[keys, :] += _dot(p, dos, ta=True)
        ds = p * (dp - jnp.sum(dp * p, axis=-1, keepdims=True))
        db_ref[a0] += ds
        dsb = (ds * (HD ** -0.5)).astype(MXU)
        dq_ref[...] = _unstack_heads(_dot(dsb, kb), GRID_W)
        dk_ref[keys, :] += _dot(dsb, qs, ta=True)

    blk = pl.BlockSpec((GRID_W, BW), lambda r: (r, 0))
    whole = pl.BlockSpec((t, BW), lambda r: (0, 0))
    tab = pl.BlockSpec(ball.shape, lambda r: (0, 0, 0))
    return pl.pallas_call(
        body, grid=(n_rows,), in_specs=[pl.BlockSpec((GRID_W, BW), lambda r: (r, 2)), blk, whole, whole, tab],
        out_specs=(blk, whole, whole, tab),
        out_shape=(_sds((t, BW), F32), _sds((t, BW), F32), _sds((t, BW), F32), _sds(ball.shape, F32)), name="na_bwd",
        compiler_params=_cp())(dbr, q, k, v, ball)


def _rpb_expand(rpb_pad, onehot):
    def body(r_ref, e_ref, o_ref):
        o_ref[...] = jnp.dot(r_ref[...], e_ref[...], precision=HI, preferred_element_type=F32)

    return pl.pallas_call(body, out_shape=_sds((64, GRID_W * GRID_W), F32), name="rpb_expand", compiler_params=_cp())(rpb_pad, onehot)


def _rpb_reduce(dtab, onehot):
    def body(d_ref, e_ref, o_ref):
        o_ref[...] = lax.dot_general(d_ref[...], e_ref[...], (((1,), (1,)), ((), ())), precision=HI, preferred_element_type=F32)

    return pl.pallas_call(body, out_shape=_sds((64, 128), F32), name="rpb_reduce", compiler_params=_cp())(dtab, onehot)


MEM_TQ = 256


def _mem_fwd(q, mk, mv):
    t = q.shape[0]
    tq = MEM_TQ

    def body(q_ref, k_ref, v_ref, o_ref):
        qv = q_ref[...]
        head = _lane_head(qv.shape)
        out = jnp.zeros((tq, BW), F32)
        for h in range(NH):
            p = _softmax_rows(_dot(jnp.where(head == h, qv, jnp.zeros_like(qv)), k_ref[...], tb=True) * (HD ** -0.5))
            out = out + jnp.where(head == h, _dot(p, v_ref[...]), 0.0)
        o_ref[...] = out.astype(BF16)

    blk = pl.BlockSpec((tq, BW), lambda i: (i, 0))
    kv = pl.BlockSpec((N_MEM, BW), lambda i: (0, 0))
    return pl.pallas_call(body, grid=(t // tq,), in_specs=[blk, kv, kv], out_specs=blk, out_shape=_sds((t, BW), BF16),
                          name="mem_fwd", compiler_params=_cp())(q, mk, mv)


def _mem_bwd(dbr, q, mk, mv):
    t = q.shape[0]
    tq = MEM_TQ

    def body(d_ref, q_ref, k_ref, v_ref, dq_ref, dk_ref, dv_ref):
        first = pl.program_id(0) == 0
        qv, dout = q_ref[...], d_ref[...]
        head = _lane_head(qv.shape)
        dq = jnp.zeros((tq, BW), F32)
        dk = jnp.zeros((N_MEM, BW), F32)
        dv = jnp.zeros((N_MEM, BW), F32)
        for h in range(NH):
            qh = jnp.where(head == h, qv, jnp.zeros_like(qv))
            doh = jnp.where(head == h, dout, 0.0).astype(MXU)
            p = _softmax_rows(_dot(qh, k_ref[...], tb=True) * (HD ** -0.5))
            dp = _dot(doh, v_ref[...], tb=True)
            dv = dv + _dot(p, doh, ta=True)
            dsb = (p * (dp - jnp.sum(dp * p, axis=-1, keepdims=True)) * (HD ** -0.5)).astype(MXU)
            dq = dq + jnp.where(head == h, _dot(dsb, k_ref[...]), 0.0)
            dk = dk + _dot(dsb, qh, ta=True)
        dq_ref[...] = dq
        _acc(dk_ref, dk, first)
        _acc(dv_ref, dv, first)

    blk = pl.BlockSpec((tq, BW), lambda i: (i, 0))
    kv = pl.BlockSpec((N_MEM, BW), lambda i: (0, 0))
    return pl.pallas_call(
        body, grid=(t // tq,), in_specs=[pl.BlockSpec((tq, BW), lambda i: (i, 3)), blk, kv, kv], out_specs=(blk, kv, kv),
        out_shape=(_sds((t, BW), F32), _sds((N_MEM, BW), F32), _sds((N_MEM, BW), F32)), name="mem_bwd",
        compiler_params=_cp())(dbr, q, mk, mv)


def _memkv_prep(kv, g_mk):
    def body(kv_ref, g_ref, k_ref, v_ref):
        k_ref[...] = _gnorm(kv_ref[:, 0:BW], g_ref[...]).astype(BF16)
        v_ref[...] = kv_ref[:, BW:2 * BW].astype(BF16)

    return pl.pallas_call(body, out_shape=(_sds((N_MEM, BW), BF16), _sds((N_MEM, BW), BF16)), name="memkv_prep",
                          compiler_params=_cp())(kv, g_mk)


def _memkv_bwd(kv, dk, dv, g_mk):
    def body(kv_ref, dk_ref, dv_ref, g_ref, o_ref, dg_ref):
        dkk, gain = _gnorm_bwd(dk_ref[...], kv_ref[:, 0:BW], g_ref[...])
        o_ref[:, 0:BW] = dkk.astype(BF16)
        o_ref[:, BW:2 * BW] = dv_ref[...].astype(BF16)
        dg_ref[...] = jnp.sum(gain, axis=0, keepdims=True)

    return pl.pallas_call(body, out_shape=(_sds((N_MEM, 2 * BW), BF16), _sds((1, BW), F32)), name="memkv_bwd",
                          compiler_params=_cp())(kv, dk, dv, g_mk)


MERGE_TM = 256


def _merge_fwd(brs, wbt, gp):
    t = gp.shape[0]
    tm = MERGE_TM

    def body(b0, b1, b2, b3, wb_ref, gp_ref, o_ref):
        out = jnp.zeros((tm, D), F32)
        for n, b_ref in enumerate((b0, b1, b2, b3)):
            up = _dot(b_ref[...], wb_ref[n], tb=True)
            out = out + _sigmoid(gp_ref[:, n * D:(n + 1) * D]) * up
        o_ref[...] = out.astype(BF16)

    blk = pl.BlockSpec((tm, BW), lambda i: (i, 0))
    return pl.pallas_call(
        body, grid=(t // tm,),
        in_specs=[blk, blk, blk, blk, pl.BlockSpec((NH, D, BW), lambda i: (0, 0, 0)), pl.BlockSpec((tm, NH * D), lambda i: (i, 0))],
        out_specs=pl.BlockSpec((tm, D), lambda i: (i, 0)), out_shape=_sds((t, D), BF16), name="merge_fwd",
        compiler_params=_cp())(*brs, wbt, gp)


def _merge_bwd(dmerged, brs, wbt, gp):
    t = gp.shape[0]
    tm = MERGE_TM

    def body(d_ref, b0, b1, b2, b3, wb_ref, gp_ref, dgp_ref, dup_ref):
        dm = d_ref[...]
        for n, b_ref in enumerate((b0, b1, b2, b3)):
            up = _dot(b_ref[...], wb_ref[n], tb=True)
            g = _sigmoid(gp_ref[:, n * D:(n + 1) * D])
            dgp_ref[:, n * D:(n + 1) * D] = (dm * up * (g * (1.0 - g))).astype(BF16)
            dup_ref[:, n * D:(n + 1) * D] = (dm * g).astype(BF16)

    row = pl.BlockSpec((tm, D), lambda i: (i, 0))
    blk = pl.BlockSpec((tm, BW), lambda i: (i, 0))
    wide = pl.BlockSpec((tm, NH * D), lambda i: (i, 0))
    return pl.pallas_call(
        body, grid=(t // tm,), in_specs=[row, blk, blk, blk, blk, pl.BlockSpec((NH, D, BW), lambda i: (0, 0, 0)), wide],
        out_specs=(wide, wide), out_shape=(_sds((t, NH * D), BF16), _sds((t, NH * D), BF16)), name="merge_bwd",
        compiler_params=_cp())(dmerged, *brs, wbt, gp)


def _dbranch(dup, wbt):
    t = dup.shape[0]
    tm = 512

    def body(d_ref, w_ref, o_ref):
        o_ref[...] = _dot(d_ref[...], w_ref[...])

    return pl.pallas_call(
        body, grid=(t // tm, NH), in_specs=[pl.BlockSpec((tm, D), lambda i, n: (i, n)), pl.BlockSpec((None, D, BW), lambda i, n: (n, 0, 0))],
        out_specs=pl.BlockSpec((tm, BW), lambda i, n: (i, n)), out_shape=_sds((t, NH * BW), F32), name="dbranch",
        compiler_params=_cp())(dup, wbt)


def _dwbranch(brs, dup):
    t = dup.shape[0]

    def body(b0, b1, b2, b3, d_ref, o_ref):
        for n, b_ref in enumerate((b0, b1, b2, b3)):
            o_ref[n] = _dot(d_ref[:, n * D:(n + 1) * D], b_ref[...], ta=True).astype(BF16)

    return pl.pallas_call(body, out_shape=_sds((NH, D, BW), BF16), name="dwbranch", compiler_params=_cp())(*brs, dup)


def _swiglu_fwd(ag):
    t = ag.shape[0]
    tm = 256

    def body(ag_ref, o_ref):
        a, g = ag_ref[:, 0:FF], ag_ref[:, FF:2 * FF]
        o_ref[...] = (a * _sigmoid(a) * g).astype(BF16)

    return pl.pallas_call(body, grid=(t // tm,), in_specs=[pl.BlockSpec((tm, 2 * FF), lambda i: (i, 0))],
                          out_specs=pl.BlockSpec((tm, FF), lambda i: (i, 0)), out_shape=_sds((t, FF), BF16), name="swiglu_fwd",
                          compiler_params=_cp())(ag)


def _swiglu_bwd(ag, dy):
    t = ag.shape[0]
    tm = 256

    def body(ag_ref, dy_ref, o_ref):
        a, g, d = ag_ref[:, 0:FF], ag_ref[:, FF:2 * FF], dy_ref[...]
        s = _sigmoid(a)
        o_ref[:, 0:FF] = (d * g * (s * (1.0 + a * (1.0 - s)))).astype(BF16)
        o_ref[:, FF:2 * FF] = (d * (a * s)).astype(BF16)

    return pl.pallas_call(
        body, grid=(t // tm,), in_specs=[pl.BlockSpec((tm, 2 * FF), lambda i: (i, 0)), pl.BlockSpec((tm, FF), lambda i: (i, 0))],
        out_specs=pl.BlockSpec((tm, 2 * FF), lambda i: (i, 0)), out_shape=_sds((t, 2 * FF), BF16), name="swiglu_bwd",
        compiler_params=_cp())(ag, dy)


def _loss_head(y, target):
    t, d = y.shape
    tm = 256

    def body(y_ref, t_ref, dy_ref, l_ref):
        e = y_ref[...] - t_ref[...]
        dy_ref[...] = e * (1.0 / d)
        _acc(l_ref, jnp.full((8, 128), 0.5 * jnp.sum(jnp.sum(e * e, axis=-1, keepdims=True) * (1.0 / d)), F32), pl.program_id(0) == 0)

    row = pl.BlockSpec((tm, d), lambda i: (i, 0))
    return pl.pallas_call(body, grid=(t // tm,), in_specs=[row, row], out_specs=(row, pl.BlockSpec((8, 128), lambda i: (0, 0))),
                          out_shape=(_sds((t, d), F32), _sds((8, 128), F32)), name="loss_head", compiler_params=_cp())(y, target)


def _sum_slots(x, name):
    k, r, c = x.shape
    tr = _tile(r, 512) if r % 128 == 0 else r

    def body(x_ref, o_ref):
        acc = x_ref[0].astype(F32)
        for s in range(1, k):
            acc = acc + x_ref[s].astype(F32)
        o_ref[...] = acc

    return pl.pallas_call(body, grid=(r // tr,), in_specs=[pl.BlockSpec((k, tr, c), lambda i: (0, i, 0))],
                          out_specs=pl.BlockSpec((tr, c), lambda i: (i, 0)), out_shape=_sds((r, c), F32), name=name,
                          compiler_params=_cp())(x)


def _pair_sum(bufs, recvs, cidx):
    n = len(bufs)

    def body(c_ref, *refs):
        for i in range(n):
            refs[2 * n + i][...] = (refs[i][...].astype(F32) + refs[n + i][...].astype(F32)).astype(BF16)

    return pl.pallas_call(
        body,
        grid_spec=pltpu.PrefetchScalarGridSpec(
            num_scalar_prefetch=1, grid=(4,),
            in_specs=[pl.BlockSpec((None, None) + b.shape[2:], lambda s, cref: (s, cref[0], 0, 0)) for b in bufs]
            + [pl.BlockSpec((None,) + r.shape[1:], lambda s, cref: (s, 0, 0)) for r in recvs],
            out_specs=tuple(pl.BlockSpec((None,) + r.shape[1:], lambda s, cref: (s, 0, 0)) for r in recvs)),
        out_shape=tuple(_sds(r.shape, BF16) for r in recvs), name="rs_pair_sum", compiler_params=_cp())(cidx, *bufs, *recvs)


def _adamw(w, g, m, v, name):
    r, c = w.shape
    tr = r
    if r > 1024:
        tr = next(cand for cand in (512, 256, 128, 64, 32, 16, 8) if r % cand == 0)

    def body(w_ref, g_ref, m_ref, v_ref, d_ref, nm_ref, nv_ref):
        gv = g_ref[...]
        mn = ADAM_B1 * m_ref[...] + (1.0 - ADAM_B1) * gv
        vn = ADAM_B2 * v_ref[...] + (1.0 - ADAM_B2) * (gv * gv)
        m_hat = mn / (1.0 - ADAM_B1 ** ADAM_STEP)
        v_hat = vn / (1.0 - ADAM_B2 ** ADAM_STEP)
        d_ref[...] = -ADAM_LR * (m_hat / (jnp.sqrt(v_hat) + ADAM_EPS) + ADAM_WD * w_ref[...])
        nm_ref[...] = mn
        nv_ref[...] = vn

    blk = pl.BlockSpec((tr, c), lambda i: (i, 0))
    return pl.pallas_call(body, grid=(r // tr,), in_specs=[blk] * 4, out_specs=(blk,) * 3,
                          out_shape=tuple(_sds((r, c), F32) for _ in range(3)), name=name, compiler_params=_cp())(w, g, m, v)


def _all_gather(shards, name):
    n = len(shards)

    def body(*refs):
        x_refs, out_refs = refs[:n], refs[n:2 * n]
        send_sems, recv_sems, local_sems = refs[2 * n:]
        x, y, cc = lax.axis_index("x"), lax.axis_index("y"), lax.axis_index("c")
        me, sibling = (x, y, cc), (x, y, 1 - cc)
        chips = [(1 - x, y), (x, 1 - y), (1 - x, 1 - y)]

        def copy(i, k, block, to, own=False):
            px, py, pc = block
            slot = out_refs[i].at[4 * px + 2 * py + pc]
            return pltpu.make_async_remote_copy(
                src_ref=x_refs[i] if own else slot, dst_ref=slot, send_sem=send_sems.at[7 * i + k],
                recv_sem=recv_sems.at[7 * i + k], device_id=to, device_id_type=MESH)

        mine = [pltpu.make_async_copy(x_refs[i], out_refs[i].at[4 * x + 2 * y + cc], local_sems.at[i]) for i in range(n)]
        for cp in mine:
            cp.start()
        first = []
        for j, chip in enumerate(chips):
            first += [copy(i, 1 + j, me, (*chip, cc), own=True) for i in range(n)]
        first += [copy(i, 0, me, sibling, own=True) for i in range(n)]
        for cp in first:
            cp.start()
        passed = []
        for j, chip in enumerate(chips):
            for i in range(n):
                copy(i, 1 + j, (*chip, cc), me).wait_recv()
                cp = copy(i, 4 + j, (*chip, cc), sibling)
                cp.start()
                passed.append(cp)
        for i in range(n):
            copy(i, 0, sibling, me).wait_recv()
        for j, chip in enumerate(chips):
            for i in range(n):
                copy(i, 4 + j, (*chip, 1 - cc), me).wait_recv()
        for cp in first + passed:
            cp.wait_send()
        for cp in mine:
            cp.wait()

    return pl.pallas_call(
        body, out_shape=tuple(_sds((N_DEV,) + s.shape, s.dtype) for s in shards), in_specs=[ANY] * n, out_specs=(ANY,) * n,
        scratch_shapes=[pltpu.SemaphoreType.DMA((7 * n,)), pltpu.SemaphoreType.DMA((7 * n,)), pltpu.SemaphoreType.DMA((n,))],
        name=name)(*shards)


def _rs_core_swap(bufs, name):
    n = len(bufs)

    def body(*refs):
        b_refs, recv_refs = refs[:n], refs[n:2 * n]
        send_sems, recv_sems = refs[2 * n:]
        x, y, cc = lax.axis_index("x"), lax.axis_index("y"), lax.axis_index("c")
        copies = [pltpu.make_async_remote_copy(
            src_ref=b_refs[i].at[s, 1 - cc], dst_ref=recv_refs[i].at[s], send_sem=send_sems.at[4 * i + s],
            recv_sem=recv_sems.at[4 * i + s], device_id=(x, y, 1 - cc), device_id_type=MESH) for i in range(n) for s in range(4)]
        for cp in copies:
            cp.start()
        for cp in copies:
            cp.wait()

    return pl.pallas_call(
        body, out_shape=tuple(_sds((4,) + b.shape[2:], b.dtype) for b in bufs), in_specs=[ANY] * n, out_specs=(ANY,) * n,
        scratch_shapes=[pltpu.SemaphoreType.DMA((4 * n,)), pltpu.SemaphoreType.DMA((4 * n,))], name=name)(*bufs)


HBM = pl.BlockSpec(memory_space=pltpu.HBM)
SEMS = pl.BlockSpec(memory_space=pltpu.SEMAPHORE)
EFFECT = pltpu.SideEffectType.DATAFLOW_SIDE_EFFECTING


def _hbm(a):
    return pltpu.HBM(a.shape, a.dtype)


def _other_chips(x, y):
    return [(1 - x, y), (x, 1 - y), (1 - x, 1 - y)]


def _ici_start(srcs, lands, by_chip, name):
    n = len(srcs)

    def body(*refs):
        s_refs, land_refs = refs[:n], refs[n:2 * n]
        send_sems, recv_sems = refs[2 * n], refs[2 * n + 1]
        token = refs[-1]
        x, y, cc = lax.axis_index("x"), lax.axis_index("y"), lax.axis_index("c")
        mine = 2 * x + y if by_chip else 4 * x + 2 * y + cc
        for px, py in _other_chips(x, y):
            for i in range(n):
                pltpu.make_async_remote_copy(
                    src_ref=s_refs[i].at[2 * px + py] if by_chip else s_refs[i], dst_ref=land_refs[i].at[mine],
                    send_sem=send_sems.at[i], recv_sem=recv_sems.at[i], device_id=(px, py, cc), device_id_type=MESH).start()
        token[...] = jnp.zeros_like(token)

    out = pl.pallas_call(
        body, name=name,
        out_shape=(pltpu.SemaphoreType.DMA((n,)), pltpu.SemaphoreType.DMA((n,)), *[_hbm(s) for s in srcs], *[_hbm(l) for l in lands],
                   _sds((8, 128), F32)),
        in_specs=[HBM] * (2 * n), out_specs=(SEMS, SEMS, *[HBM] * (2 * n), pl.BlockSpec(memory_space=pltpu.VMEM)),
        input_output_aliases={i: 2 + i for i in range(2 * n)}, compiler_params=pltpu.CompilerParams(has_side_effects=EFFECT),
    )(*[pltpu.with_memory_space_constraint(s, pltpu.HBM) for s in srcs],
      *[pltpu.with_memory_space_constraint(l, pltpu.HBM) for l in lands])
    return out[0], out[1], out[2:2 + n], out[2 + n:2 + 2 * n], out[-1]


def _ici_wait(started, after, name):
    send_sems, recv_sems, srcs, lands, _ = started
    n = len(srcs)

    def body(*refs):
        land_refs = refs[n:2 * n]
        send_sems, recv_sems = refs[2 * n], refs[2 * n + 1]
        x, y, cc = lax.axis_index("x"), lax.axis_index("y"), lax.axis_index("c")
        for i in range(n):
            three = land_refs[i].at[pl.ds(0, 3)]
            cp = pltpu.make_async_remote_copy(src_ref=three, dst_ref=three, send_sem=send_sems.at[i], recv_sem=recv_sems.at[i],
                                              device_id=(x, y, cc), device_id_type=MESH)
            cp.wait_send()
            cp.wait_recv()

    return pl.pallas_call(
        body, name=name, out_shape=tuple(_hbm(l) for l in lands), in_specs=[HBM] * (2 * n) + [SEMS, SEMS, ANY],
        out_specs=tuple([HBM] * n), input_output_aliases={n + i: i for i in range(n)},
        compiler_params=pltpu.CompilerParams(has_side_effects=EFFECT))(*srcs, *lands, send_sems, recv_sems, after)


def _gather_d2d(blocks, lands, name):
    n = len(blocks)

    def body(*refs):
        x_refs, land_refs = refs[:n], refs[2 * n:3 * n]
        send_sems, recv_sems, in_sems, out_sems = refs[3 * n:3 * n + 4]
        stage = refs[3 * n + 4:]
        x, y, cc = lax.axis_index("x"), lax.axis_index("y"), lax.axis_index("c")
        sibling = (x, y, 1 - cc)
        staged = [pltpu.make_async_copy(x_refs[i], stage[i], in_sems.at[i]) for i in range(n)]
        for cp in staged:
            cp.start()
        copies = []
        for i in range(n):
            slot = land_refs[i].at[4 * x + 2 * y + cc]
            copies.append(pltpu.make_async_remote_copy(src_ref=x_refs[i], dst_ref=slot, send_sem=send_sems.at[4 * i],
                                                       recv_sem=recv_sems.at[4 * i], device_id=sibling, device_id_type=MESH))
            for j, (px, py) in enumerate(_other_chips(x, y)):
                slot = land_refs[i].at[4 * px + 2 * py + cc]
                copies.append(pltpu.make_async_remote_copy(src_ref=slot, dst_ref=slot, send_sem=send_sems.at[4 * i + 1 + j],
                                                           recv_sem=recv_sems.at[4 * i + 1 + j], device_id=sibling, device_id_type=MESH))
        for cp in copies:
            cp.start()
        mine = []
        for i in range(n):
            staged[i].wait()
            mine.append(pltpu.make_async_copy(stage[i], land_refs[i].at[4 * x + 2 * y + cc], out_sems.at[i]))
            mine[i].start()
        for i in range(n):
            slot = land_refs[i].at[4 * x + 2 * y + (1 - cc)]
            pltpu.make_async_remote_copy(src_ref=slot, dst_ref=slot, send_sem=send_sems.at[4 * i], recv_sem=recv_sems.at[4 * i],
                                         device_id=sibling, device_id_type=MESH).wait_recv()
            for j, (px, py) in enumerate(_other_chips(x, y)):
                slot = land_refs[i].at[4 * px + 2 * py + (1 - cc)]
                pltpu.make_async_remote_copy(src_ref=slot, dst_ref=slot, send_sem=send_sems.at[4 * i + 1 + j],
                                             recv_sem=recv_sems.at[4 * i + 1 + j], device_id=sibling, device_id_type=MESH).wait_recv()
        for cp in copies:
            cp.wait_send()
        for cp in mine:
            cp.wait()

    return pl.pallas_call(
        body, out_shape=tuple(_sds(l.shape, l.dtype) for l in lands), in_specs=[ANY] * (2 * n), out_specs=(ANY,) * n,
        input_output_aliases={n + i: i for i in range(n)},
        scratch_shapes=[pltpu.SemaphoreType.DMA((4 * n,)), pltpu.SemaphoreType.DMA((4 * n,)), pltpu.SemaphoreType.DMA((n,)),
                        pltpu.SemaphoreType.DMA((n,))] + [pltpu.VMEM(b.shape, b.dtype) for b in blocks],
        name=name, compiler_params=_cp())(*blocks, *lands)


def _sum_own(parts, recvs, chip, name):
    n = len(parts)

    def body(c_ref, *refs):
        s = pl.program_id(0)
        for i in range(n):
            val = jnp.where(c_ref[0] == s, refs[i][...], refs[n + i][...]).astype(F32)
            _acc(refs[2 * n + i], val, s == 0)

    ins = [pl.BlockSpec((None,) + p.shape[1:], lambda s, cref: (s, 0, 0)) for p in parts]
    return pl.pallas_call(
        body, grid_spec=pltpu.PrefetchScalarGridSpec(
            num_scalar_prefetch=1, grid=(4,), in_specs=ins + ins,
            out_specs=tuple(pl.BlockSpec(p.shape[1:], lambda s, cref: (0, 0)) for p in parts)),
        out_shape=tuple(_sds(p.shape[1:], F32) for p in parts), name=name, compiler_params=_cp())(chip, *parts, *recvs)


BIG = (("w_in", True), ("w_gate", True), ("w_mem_kv", False), ("w_branch", True), ("w_out", False), ("w_ffn_in", True),
       ("w_ffn_out", False))

SMALL = ("norm_mix_g", "norm_mem_g", "ret_decay_fwd", "ret_decay_bwd", "ret_norm_g", "pool_w", "pool_scale", "na_q_norm_g",
         "na_k_norm_g", "na_rpb", "mem_q_norm_g", "mem_k_norm_g", "norm_ffn_g")
WEIGHTS = ("norm_mix_g", "norm_mem_g", "w_in", "w_gate", "ret_decay_fwd", "ret_decay_bwd", "ret_norm_g", "pool_w", "pool_scale",
           "na_q_norm_g", "na_k_norm_g", "na_rpb", "mem_q_norm_g", "mem_k_norm_g", "w_mem_kv", "w_branch", "w_out", "norm_ffn_g",
           "w_ffn_in", "w_ffn_out")


def _to_exchange(name, transposed, shard):
    if name == "w_branch":
        return jnp.swapaxes(shard, 1, 2).reshape(NH * (D // N_DEV), BW)
    return shard.T if transposed else shard


def _from_exchange(name, transposed, block):
    if name == "w_branch":
        return jnp.swapaxes(block.reshape(NH, D // N_DEV, BW), 1, 2)
    return block.T if transposed else block


def _whole_from_gathered(name, g):
    if name == "w_branch":
        return jnp.swapaxes(g.reshape(N_DEV, NH, D // N_DEV, BW), 0, 1).reshape(NH, D, BW)
    return g.reshape(N_DEV * g.shape[1], g.shape[2])


def _by_destination(name, g):
    if name == "w_branch":
        g = jnp.swapaxes(g.reshape(NH, N_DEV, D // N_DEV, BW), 0, 1).reshape(N_DEV * NH * (D // N_DEV), BW)
    return g.reshape(4, 2, g.shape[0] // N_DEV, g.shape[1])


SMALL_PAD = 1024


def _pack_small(vals, loss=None):
    parts = [vals[n] for n in SMALL] + [jnp.zeros((1,), F32) if loss is None else loss.reshape(1)]
    rows = []
    for p in parts:
        flat = p.reshape(-1)
        rows.append(jnp.pad(flat, (0, -flat.shape[0] % SMALL_PAD)).reshape(-1, 128))
    return jnp.concatenate(rows, axis=0)


def _unpack_small(packed, like):
    out, off = {}, 0
    for n in SMALL:
        sz = int(np.prod(like[n].shape))
        nrow = -(-sz // SMALL_PAD) * (SMALL_PAD // 128)
        out[n] = packed[off:off + nrow].reshape(-1)[:sz].reshape(like[n].shape)
        off += nrow
    return out, packed[off, 0]


def _na_constants():
    c = np.arange(GRID_W)
    win = np.clip(c - NA_COLS_WIN // 2, 0, GRID_W - NA_COLS_WIN)
    kc = np.arange(GRID_W)
    inside = (kc[None, :] >= win[:, None]) & (kc[None, :] < win[:, None] + NA_COLS_WIN)
    off = kc[None, :] - c[:, None] + NA_COLS_WIN - 1
    onehot = np.zeros((128, GRID_W, GRID_W), np.float32)
    for b in range(2 * NA_COLS_WIN - 1):
        onehot[b] = (off == b) & inside
    maskadd = np.where(inside, 0.0, NEG).astype(np.float32)
    return onehot.reshape(128, GRID_W * GRID_W), maskadd


def _na_bias_table(tab, maskadd):
    n_off = 2 * NA_ROWS_WIN - 1
    t4 = tab[:NH * n_off].reshape(NH, n_off, GRID_W, GRID_W) + maskadd[None, None]
    ball = jnp.stack([t4[:, a0:a0 + NA_ROWS_WIN] for a0 in range(NA_ROWS_WIN)], axis=1)
    return ball.transpose(1, 0, 3, 2, 4).reshape(NA_ROWS_WIN, NH * GRID_W, NA_KEYS)


def _rotary_tables(t):
    half = HD // 2
    inv = ROPE_THETA ** (-jnp.arange(half, dtype=F32) / half)
    ang = jnp.arange(t, dtype=F32)[:, None] * inv[None, :]
    cos, sin = jnp.cos(ang), jnp.sin(ang)
    return jnp.tile(jnp.concatenate([cos, cos], axis=-1), (1, NH)), jnp.tile(jnp.concatenate([-sin, sin], axis=-1), (1, NH))


def _block_diag(pw):
    out = jnp.zeros((BW, BW), pw.dtype)
    for g in range(NH):
        out = lax.dynamic_update_slice(out, pw[g], (g * HD, g * HD))
    return out


def _tile4(g):
    return jnp.tile(g.reshape(1, HD), (1, NH))


def _layer_fwd(x, mem, sw, lw, consts):
    cos2, sin2, onehot, maskadd = consts
    h = _rmsnorm_fwd(x, sw["norm_mix_g"].reshape(1, D), "norm_mix_fwd")
    proj = _mm(h, lw["w_in"], tb=True, name="mm_in")
    gp = _mm(h, lw["w_gate"], tb=True, name="mm_gate")
    g_naq, g_nak, g_mq = _tile4(sw["na_q_norm_g"]), _tile4(sw["na_k_norm_g"]), _tile4(sw["mem_q_norm_g"])
    rq, rk, rv, nq, nk, nv, mq = _prep_fwd(proj, cos2, sin2, g_naq, g_nak, g_mq)

    lgf, lgb = jax.nn.log_sigmoid(sw["ret_decay_fwd"]), jax.nn.log_sigmoid(sw["ret_decay_bwd"])
    g_ret = sw["ret_norm_g"].reshape(1, BW)
    o_ret, ret = _ret_fwd(rq, rk, rv, proj, lgf, lgb, g_ret)

    wbd = _block_diag(sw["pool_w"]).astype(BF16)
    p_scale = sw["pool_scale"].reshape(1, BW)
    pool = _pool_fwd(proj, wbd, p_scale)

    rpb_pad = jnp.pad(sw["na_rpb"].reshape(NH * 15, 31), ((0, 4), (0, 97)))
    ball = _na_bias_table(_rpb_expand(rpb_pad, onehot), maskadd)
    na = _na_fwd(nq, nk, nv, ball)

    memn = _rmsnorm_fwd(mem, sw["norm_mem_g"].reshape(1, D), "norm_mem_fwd")
    kv = _mm(memn, lw["w_mem_kv"], name="mm_memkv")
    g_mk = _tile4(sw["mem_k_norm_g"])
    mk, mv = _memkv_prep(kv, g_mk)
    mo = _mem_fwd(mq, mk, mv)

    br = (ret, pool, na, mo)
    merged = _merge_fwd(br, lw["w_branch"], gp)
    x1 = _mm(merged, lw["w_out"], add=x, name="mm_out")
    h2 = _rmsnorm_fwd(x1, sw["norm_ffn_g"].reshape(1, D), "norm_ffn_fwd")
    ag = _mm(h2, lw["w_ffn_in"], tb=True, name="mm_ffn_in")
    yff = _swiglu_fwd(ag)
    x2 = _mm(yff, lw["w_ffn_out"], add=x1, name="mm_ffn_out")
    saved = dict(x=x, h=h, proj=proj, gp=gp, rq=rq, rk=rk, rv=rv, nq=nq, nk=nk, nv=nv, mq=mq, o_ret=o_ret, ball=ball, memn=memn,
                 kv=kv, mk=mk, mv=mv, br=br, merged=merged, x1=x1, h2=h2, ag=ag, yff=yff, lgf=lgf, lgb=lgb, wbd=wbd)
    return x2, saved


def _layer_bwd(dx2, mem, sw, lw, sv, consts, dep=None):
    cos2, sin2, onehot, maskadd = consts
    gb, gs = {}, {}
    dy = _mm(dx2, lw["w_ffn_out"], tb=True, dep=dep, name="mm_ffn_out_dx")
    gb["w_ffn_out"] = _mm(sv["yff"], dx2, ta=True, out_dtype=BF16, name="mm_ffn_out_dw")
    dag = _swiglu_bwd(sv["ag"], dy)
    dh2 = _mm(dag, lw["w_ffn_in"], name="mm_ffn_in_dx")
    gb["w_ffn_in"] = _mm(dag, sv["h2"], ta=True, out_dtype=BF16, name="mm_ffn_in_dw")
    dx1, dg = _rmsnorm_bwd(dh2, sv["x1"], sw["norm_ffn_g"].reshape(1, D), dx2, "norm_ffn_bwd")
    gs["norm_ffn_g"] = dg.reshape(D)

    dmerged = _mm(dx1, lw["w_out"], tb=True, name="mm_out_dx")
    gb["w_out"] = _mm(sv["merged"], dx1, ta=True, out_dtype=BF16, name="mm_out_dw")
    dgp, dup = _merge_bwd(dmerged, sv["br"], lw["w_branch"], sv["gp"])
    dbr = _dbranch(dup, lw["w_branch"])
    gb["w_branch"] = _dwbranch(sv["br"], dup)

    g_ret = sw["ret_norm_g"].reshape(1, BW)
    do_ret, d_rg, dg_ret = _ret_post_bwd(dbr, sv["o_ret"], sv["proj"], g_ret)
    d_rq, d_rk, d_rv, dlg = _ret_bwd(do_ret, sv["rq"], sv["rk"], sv["rv"], sv["lgf"], sv["lgb"])
    gs["ret_norm_g"] = dg_ret.reshape(BW)
    _, vjp_f = jax.vjp(jax.nn.log_sigmoid, sw["ret_decay_fwd"])
    _, vjp_b = jax.vjp(jax.nn.log_sigmoid, sw["ret_decay_bwd"])
    gs["ret_decay_fwd"] = vjp_f(dlg[0:NH, 0])[0]
    gs["ret_decay_bwd"] = vjp_b(dlg[NH:2 * NH, 0])[0]

    p_scale = sw["pool_scale"].reshape(1, BW)
    d_pv, dwbd, dscale = _pool_bwd(dbr, sv["proj"], sv["wbd"], p_scale)
    gs["pool_w"] = jnp.stack([dwbd[g * HD:(g + 1) * HD, g * HD:(g + 1) * HD] for g in range(NH)])
    gs["pool_scale"] = dscale.reshape(BW)

    d_nq, d_nk, d_nv, dball = _na_bwd(dbr, sv["nq"], sv["nk"], sv["nv"], sv["ball"])
    _, vjp_tab = jax.vjp(lambda tab: _na_bias_table(tab, maskadd), jnp.zeros((64, GRID_W * GRID_W), F32))
    drpb = _rpb_reduce(vjp_tab(dball)[0], onehot)
    gs["na_rpb"] = drpb[:NH * 15, :31].reshape(NH, 15, 31)

    d_mq, d_mk, d_mv = _mem_bwd(dbr, sv["mq"], sv["mk"], sv["mv"])
    g_mk = _tile4(sw["mem_k_norm_g"])
    dkv, dg_mk = _memkv_bwd(sv["kv"], d_mk, d_mv, g_mk)
    gs["mem_k_norm_g"] = dg_mk.reshape(NH, HD).sum(0)
    gb["w_mem_kv"] = _mm(sv["memn"], dkv, ta=True, out_dtype=BF16, name="mm_memkv_dw")
    dmemn = _mm(dkv, lw["w_mem_kv"], tb=True, name="mm_memkv_dx")
    _, dg_mem = _rmsnorm_bwd(dmemn, mem, sw["norm_mem_g"].reshape(1, D), jnp.zeros_like(mem), "norm_mem_bwd")
    gs["norm_mem_g"] = dg_mem.reshape(D)

    g_naq, g_nak, g_mq = _tile4(sw["na_q_norm_g"]), _tile4(sw["na_k_norm_g"]), _tile4(sw["mem_q_norm_g"])
    dproj, dg_naq, dg_nak, dg_mq = _prep_bwd(sv["proj"], cos2, sin2, g_naq, g_nak, g_mq, d_rq, d_rk, d_rv, d_rg, d_pv, d_nq, d_nk,
                                             d_nv, d_mq)
    gs["na_q_norm_g"] = dg_naq.reshape(NH, HD).sum(0)
    gs["na_k_norm_g"] = dg_nak.reshape(NH, HD).sum(0)
    gs["mem_q_norm_g"] = dg_mq.reshape(NH, HD).sum(0)

    dh = _mm(dproj, lw["w_in"], name="mm_in_dx")
    dh = _mm(dgp, lw["w_gate"], add=dh, name="mm_gate_dx")
    gb["w_in"] = _mm(dproj, sv["h"], ta=True, out_dtype=BF16, name="mm_in_dw")
    gb["w_gate"] = _mm(dgp, sv["h"], ta=True, out_dtype=BF16, name="mm_gate_dw")
    dx, dg = _rmsnorm_bwd(dh, sv["x"], sw["norm_mix_g"].reshape(1, D), dx1, "norm_mix_bwd")
    gs["norm_mix_g"] = dg.reshape(D)
    return dx, gb, gs


def _local_step(x, mem, target, small, get_layer, on_grads):
    t = x.shape[0]
    cos2, sin2 = _rotary_tables(t)
    onehot, maskadd = _na_constants()
    consts = (cos2, sin2, jnp.asarray(onehot), jnp.asarray(maskadd))
    saved, weights, cur = [], [], x
    for l in range(DEPTH):
        sw = {n: small[n][l] for n in SMALL}
        weights.append(get_layer(l, cur))
        cur, sv = _layer_fwd(cur, mem, sw, weights[l], consts)
        saved.append(sv)
    dy, loss_tile = _loss_head(cur, target)
    small_g = {n: [None] * DEPTH for n in SMALL}
    dep = None
    for l in reversed(range(DEPTH)):
        sw = {n: small[n][l] for n in SMALL}
        dy, gb, gs = _layer_bwd(dy, mem, sw, weights[l], saved[l], consts, dep)
        dep = on_grads(l, gb, dy)
        for n in SMALL:
            small_g[n][l] = gs[n]
    return loss_tile[0, 0], dy, {n: jnp.stack(v) for n, v in small_g.items()}


def _flat2d(a):
    return a.reshape(-1, a.shape[-1])


def kernel(x, mem, norm_mix_g, norm_mem_g, w_in, w_gate, ret_decay_fwd, ret_decay_bwd, ret_norm_g, pool_w, pool_scale, na_q_norm_g, na_k_norm_g, na_rpb, mem_q_norm_g, mem_k_norm_g, w_mem_kv, w_branch, w_out, norm_ffn_g, w_ffn_in, w_ffn_out, loss_target, m_norm_mix_g, m_norm_mem_g, m_w_in, m_w_gate, m_ret_decay_fwd, m_ret_decay_bwd, m_ret_norm_g, m_pool_w, m_pool_scale, m_na_q_norm_g, m_na_k_norm_g, m_na_rpb, m_mem_q_norm_g, m_mem_k_norm_g, m_w_mem_kv, m_w_branch, m_w_out, m_norm_ffn_g, m_w_ffn_in, m_w_ffn_out, v_norm_mix_g, v_norm_mem_g, v_w_in, v_w_gate, v_ret_decay_fwd, v_ret_decay_bwd, v_ret_norm_g, v_pool_w, v_pool_scale, v_na_q_norm_g, v_na_k_norm_g, v_na_rpb, v_mem_q_norm_g, v_mem_k_norm_g, v_w_mem_kv, v_w_branch, v_w_out, v_norm_ffn_g, v_w_ffn_in, v_w_ffn_out):
    w = dict(norm_mix_g=norm_mix_g, norm_mem_g=norm_mem_g, w_in=w_in, w_gate=w_gate, ret_decay_fwd=ret_decay_fwd,
             ret_decay_bwd=ret_decay_bwd, ret_norm_g=ret_norm_g, pool_w=pool_w, pool_scale=pool_scale, na_q_norm_g=na_q_norm_g,
             na_k_norm_g=na_k_norm_g, na_rpb=na_rpb, mem_q_norm_g=mem_q_norm_g, mem_k_norm_g=mem_k_norm_g, w_mem_kv=w_mem_kv,
             w_branch=w_branch, w_out=w_out, norm_ffn_g=norm_ffn_g, w_ffn_in=w_ffn_in, w_ffn_out=w_ffn_out)
    m = dict(norm_mix_g=m_norm_mix_g, norm_mem_g=m_norm_mem_g, w_in=m_w_in, w_gate=m_w_gate, ret_decay_fwd=m_ret_decay_fwd,
             ret_decay_bwd=m_ret_decay_bwd, ret_norm_g=m_ret_norm_g, pool_w=m_pool_w, pool_scale=m_pool_scale, na_q_norm_g=m_na_q_norm_g,
             na_k_norm_g=m_na_k_norm_g, na_rpb=m_na_rpb, mem_q_norm_g=m_mem_q_norm_g, mem_k_norm_g=m_mem_k_norm_g, w_mem_kv=m_w_mem_kv,
             w_branch=m_w_branch, w_out=m_w_out, norm_ffn_g=m_norm_ffn_g, w_ffn_in=m_w_ffn_in, w_ffn_out=m_w_ffn_out)
    v = dict(norm_mix_g=v_norm_mix_g, norm_mem_g=v_norm_mem_g, w_in=v_w_in, w_gate=v_w_gate, ret_decay_fwd=v_ret_decay_fwd,
             ret_decay_bwd=v_ret_decay_bwd, ret_norm_g=v_ret_norm_g, pool_w=v_pool_w, pool_scale=v_pool_scale, na_q_norm_g=v_na_q_norm_g,
             na_k_norm_g=v_na_k_norm_g, na_rpb=v_na_rpb, mem_q_norm_g=v_mem_q_norm_g, mem_k_norm_g=v_mem_k_norm_g, w_mem_kv=v_w_mem_kv,
             w_branch=v_w_branch, w_out=v_w_out, norm_ffn_g=v_norm_ffn_g, w_ffn_in=v_w_ffn_in, w_ffn_out=v_w_ffn_out)
    assert x.shape == (1, 2048, D) and mem.shape == (1, N_MEM, D) and w_in.shape == (DEPTH, D, 9 * BW // N_DEV)

    started = []
    for l in range(DEPTH):
        blocks = [_to_exchange(name, tr, w[name][l]).astype(BF16) for name, tr in BIG]
        lands = [lax.empty((N_DEV,) + b.shape, BF16) for b in blocks]
        started.append(_ici_start(blocks, lands, False, "gather_ici_start_%d" % l))
    all_started = started[0][4] + started[1][4] + started[2][4] + started[3][4]

    def get_layer(l, after):
        lands = _ici_wait(started[l], all_started if l == 0 else after, "gather_ici_wait_%d" % l)
        whole = _gather_d2d(started[l][2], lands, "gather_d2d")
        return {name: _whole_from_gathered(name, g) for (name, _), g in zip(BIG, whole)}

    cidx = lax.axis_index("c").astype(jnp.int32).reshape(1)
    chip = (2 * lax.axis_index("x") + lax.axis_index("y")).astype(jnp.int32).reshape(1)
    in_flight, g_layers = [], [None] * DEPTH

    def finish(l, st, after):
        recv = _ici_wait(st, after, "rs_ici_wait_%d" % l)
        sums = _sum_own(st[2], recv, chip, "rs_chip_sum")
        g_layers[l] = {name: _from_exchange(name, tr, s) if name == "w_branch" else s for (name, tr), s in zip(BIG, sums)}

    def on_grads(l, gb, after):
        send = [_by_destination(name, gb[name]) for name, _ in BIG]
        from_core = _rs_core_swap(send, "rs_core_swap")
        chip_part = _pair_sum(send, from_core, cidx)
        st = _ici_start(chip_part, [lax.empty(p.shape, BF16) for p in chip_part], True, "rs_ici_start_%d" % l)
        if in_flight:
            finish(*in_flight.pop(), after)
        in_flight.append((l, st))
        if l == 0:
            finish(*in_flight.pop(), st[4])
        return st[4]

    loss_local, dx, small_g = _local_step(x[0], mem[0], loss_target[0], {n: w[n] for n in SMALL}, get_layer, on_grads)
    g_shard = {name: jnp.stack([g_layers[l][name] for l in range(DEPTH)]) for name, _ in BIG}

    small_all, = _all_gather([_pack_small(small_g, loss_local)], "gather_small")
    packed_g = _sum_slots(small_all, "small_sum")
    small_sum, loss = _unpack_small(packed_g, {n: w[n] for n in SMALL})

    grads, delta, new_m, new_v = {}, {}, {}, {}
    for name, tr in BIG:
        flip = (lambda a: jnp.swapaxes(a, 1, 2)) if (tr and name != "w_branch") else (lambda a: a)
        wx, gx, mx, vx = flip(w[name]), g_shard[name], flip(m[name]), flip(v[name])
        d_, m_, v_ = _adamw(_flat2d(wx), _flat2d(gx), _flat2d(mx), _flat2d(vx), "adamw_" + name)
        grads[name] = flip(gx)
        delta[name], new_m[name], new_v[name] = (flip(a.reshape(wx.shape)) for a in (d_, m_, v_))
    d_, m_, v_ = _adamw(_pack_small({n: w[n] for n in SMALL}), packed_g, _pack_small({n: m[n] for n in SMALL}),
                        _pack_small({n: v[n] for n in SMALL}), "adamw_small")
    like = {n: w[n] for n in SMALL}
    ds, _ = _unpack_small(d_, like)
    ms, _ = _unpack_small(m_, like)
    vs, _ = _unpack_small(v_, like)
    for n in SMALL:
        grads[n], delta[n], new_m[n], new_v[n] = small_sum[n], ds[n], ms[n], vs[n]

    return (loss, dx[None], *[grads[n] for n in WEIGHTS], *[delta[n] for n in WEIGHTS], *[new_m[n] for n in WEIGHTS],
            *[new_v[n] for n in WEIGHTS])
```

```python
import functools

import numpy as np
import jax
import jax.numpy as jnp
from jax import lax
from jax.experimental import pallas as pl
from jax.experimental.pallas import tpu as pltpu

F32 = jnp.float32
BF16 = jnp.bfloat16
MXU = jnp.bfloat16
HI = lax.Precision.HIGHEST

DEPTH = 4
D = 1024
BW = 256
HD = 64
NH = 4
GRID_W = 64
NA_ROWS_WIN = 8
NA_COLS_WIN = 16
N_MEM = 256
FF = 2816
EPS = 1e-6
NEG = -1e30
ROPE_THETA = 10000.0
POOL_HALF_MAX = 8

ADAM_LR, ADAM_B1, ADAM_B2, ADAM_EPS, ADAM_WD, ADAM_STEP = 0.001, 0.9, 0.999, 1e-08, 0.01, 10

N_DEV = 8
VMEM_LIMIT = 56 * 1024 * 1024

RQ, RK, RV, RG, PV, NQ, NK, NV, MQ = range(9)

MESH = pl.DeviceIdType.MESH
ANY = pl.BlockSpec(memory_space=pl.ANY)
SMEM = pl.BlockSpec(memory_space=pltpu.SMEM)


def _cp(**kw):
    return pltpu.CompilerParams(vmem_limit_bytes=VMEM_LIMIT, **kw)


def _tile(n, cap):
    if n <= cap:
        return n
    best = None
    for t in range(128, cap + 1, 128):
        if n % t == 0:
            best = t
    assert best is not None, (n, cap)
    return best


def _sds(shape, dtype):
    return jax.ShapeDtypeStruct(shape, dtype)


def _lane_head(shape):
    return lax.shift_right_logical(lax.broadcasted_iota(jnp.int32, shape, len(shape) - 1), 6)


def _group_mean(z):
    i = lax.shift_right_logical(lax.broadcasted_iota(jnp.int32, (BW, BW), 0), 6)
    j = lax.shift_right_logical(lax.broadcasted_iota(jnp.int32, (BW, BW), 1), 6)
    g = jnp.where(i == j, 1.0 / HD, 0.0).astype(F32)
    return jnp.dot(z, g, precision=HI, preferred_element_type=F32)


def _gnorm(t, g):
    r = lax.rsqrt(_group_mean(t * t) + EPS)
    return t * r * g


def _gnorm_bwd(dy, t, g):
    r = lax.rsqrt(_group_mean(t * t) + EPS)
    th = t * r
    dth = dy * g
    dt = r * (dth - th * _group_mean(dth * th))
    return dt, dy * th


def _swap_halves(t):
    lane = lax.broadcasted_iota(jnp.int32, t.shape, 1)
    return jnp.where((lane & 63) < 32, pltpu.roll(t, BW - 32, 1), pltpu.roll(t, 32, 1))


def _sigmoid(x):
    return 1.0 / (1.0 + jnp.exp(-x))


def _dot(a, b, ta=False, tb=False):
    return lax.dot_general(a.astype(MXU), b.astype(MXU), (((0 if ta else 1,), (1 if tb else 0,)), ((), ())),
                           preferred_element_type=F32)


def _stack_heads(t):
    head = _lane_head(t.shape)
    return jnp.concatenate([jnp.where(head == h, t, jnp.zeros_like(t)) for h in range(NH)], axis=0)


def _unstack_heads(t, rows):
    head = _lane_head((rows, BW))
    out = jnp.zeros((rows, BW), F32)
    for h in range(NH):
        out = out + jnp.where(head == h, t[h * rows:(h + 1) * rows], 0.0)
    return out


def _softmax_rows(s):
    m = jnp.max(s, axis=-1, keepdims=True)
    e = jnp.exp(s - m)
    return e / jnp.sum(e, axis=-1, keepdims=True)


def _acc(ref, val, first):
    @pl.when(first)
    def _():
        ref[...] = val

    @pl.when(jnp.logical_not(first))
    def _():
        ref[...] += val


def _mm(a, b, *, ta=False, tb=False, out_dtype=F32, add=None, dep=None, name):
    m, k = (a.shape[1], a.shape[0]) if ta else a.shape
    n = b.shape[0] if tb else b.shape[1]
    tm, tn = _tile(m, 1024), _tile(n, 512)

    def body(*refs):
        if add is None:
            a_ref, b_ref, o_ref = refs[:2] + refs[-1:]
            r = _dot(a_ref[...], b_ref[...], ta, tb)
        else:
            a_ref, b_ref, c_ref, o_ref = refs[:3] + refs[-1:]
            r = _dot(a_ref[...], b_ref[...], ta, tb) + c_ref[...]
        o_ref[...] = r.astype(out_dtype)

    a_spec = pl.BlockSpec((k, tm), lambda i, j: (0, i)) if ta else pl.BlockSpec((tm, k), lambda i, j: (i, 0))
    b_spec = pl.BlockSpec((tn, k), lambda i, j: (j, 0)) if tb else pl.BlockSpec((k, tn), lambda i, j: (0, j))
    o_spec = pl.BlockSpec((tm, tn), lambda i, j: (i, j))
    ins, args = [a_spec, b_spec], [a, b]
    if add is not None:
        ins.append(o_spec)
        args.append(add)
    if dep is not None:
        ins.append(pl.BlockSpec((8, 128), lambda i, j: (0, 0)))
        args.append(dep)
    return pl.pallas_call(
        body, grid=(m // tm, n // tn), in_specs=ins, out_specs=o_spec, out_shape=_sds((m, n), out_dtype), name=name,
        compiler_params=_cp(dimension_semantics=("parallel", "parallel")))(*args)


def _rmsnorm_fwd(x, g, name):
    t, d = x.shape
    tm = _tile(t, 256)

    def body(x_ref, g_ref, o_ref):
        xv = x_ref[...]
        r = lax.rsqrt(jnp.mean(xv * xv, axis=-1, keepdims=True) + EPS)
        o_ref[...] = (xv * r * g_ref[...]).astype(o_ref.dtype)

    return pl.pallas_call(
        body, grid=(t // tm,), in_specs=[pl.BlockSpec((tm, d), lambda i: (i, 0)), pl.BlockSpec((1, d), lambda i: (0, 0))],
        out_specs=pl.BlockSpec((tm, d), lambda i: (i, 0)), out_shape=_sds((t, d), BF16), name=name, compiler_params=_cp())(x, g)


def _rmsnorm_bwd(dh, x, g, res, name):
    t, d = x.shape
    tm = _tile(t, 256)

    def body(dh_ref, x_ref, g_ref, res_ref, dx_ref, dxb_ref, dg_ref):
        xv = x_ref[...]
        dhv = dh_ref[...]
        r = lax.rsqrt(jnp.mean(xv * xv, axis=-1, keepdims=True) + EPS)
        xh = xv * r
        dxh = dhv * g_ref[...]
        dx = res_ref[...] + r * (dxh - xh * jnp.mean(dxh * xh, axis=-1, keepdims=True))
        dx_ref[...] = dx
        dxb_ref[...] = dx.astype(BF16)
        _acc(dg_ref, jnp.sum(dhv * xh, axis=0, keepdims=True), pl.program_id(0) == 0)

    row = pl.BlockSpec((tm, d), lambda i: (i, 0))
    vec = pl.BlockSpec((1, d), lambda i: (0, 0))
    return pl.pallas_call(
        body, grid=(t // tm,), in_specs=[row, row, vec, row], out_specs=(row, row, vec),
        out_shape=(_sds((t, d), F32), _sds((t, d), BF16), _sds((1, d), F32)), name=name, compiler_params=_cp())(dh, x, g, res)


def _prep_fwd(proj, cos2, sin2, g_naq, g_nak, g_mq):
    t = proj.shape[0]
    tm = 256

    def body(p_ref, cos_ref, sin_ref, gq_ref, gk_ref, gm_ref, rq_ref, rk_ref, rv_ref, nq_ref, nk_ref, nv_ref, mq_ref):
        def col(c):
            return p_ref[:, c * BW:(c + 1) * BW]

        cosv, sinv = cos_ref[...], sin_ref[...]

        def rot(tv):
            return tv * cosv + _swap_halves(tv) * sinv

        rq_ref[...] = (rot(col(RQ)) * (HD ** -0.5)).astype(BF16)
        rk_ref[...] = rot(col(RK)).astype(BF16)
        rv_ref[...] = col(RV).astype(BF16)
        nq_ref[...] = _gnorm(col(NQ), gq_ref[...]).astype(BF16)
        nk_ref[...] = _gnorm(col(NK), gk_ref[...]).astype(BF16)
        nv_ref[...] = col(NV).astype(BF16)
        mq_ref[...] = _gnorm(col(MQ), gm_ref[...]).astype(BF16)

    blk = pl.BlockSpec((tm, BW), lambda i: (i, 0))
    vec = pl.BlockSpec((1, BW), lambda i: (0, 0))
    return pl.pallas_call(
        body, grid=(t // tm,), in_specs=[pl.BlockSpec((tm, 9 * BW), lambda i: (i, 0)), blk, blk, vec, vec, vec],
        out_specs=tuple(blk for _ in range(7)), out_shape=tuple(_sds((t, BW), BF16) for _ in range(7)),
        name="prep_fwd", compiler_params=_cp())(proj, cos2, sin2, g_naq, g_nak, g_mq)


def _prep_bwd(proj, cos2, sin2, g_naq, g_nak, g_mq, d_rq, d_rk, d_rv, d_rg, d_pv, d_nq, d_nk, d_nv, d_mq):
    t = proj.shape[0]
    tm = 256

    def body(p_ref, cos_ref, sin_ref, gq_ref, gk_ref, gm_ref, drq_ref, drk_ref, drv_ref, drg_ref, dpv_ref, dnq_ref, dnk_ref,
             dnv_ref, dmq_ref, o_ref, dgq_ref, dgk_ref, dgm_ref):
        first = pl.program_id(0) == 0

        def col(c):
            return p_ref[:, c * BW:(c + 1) * BW]

        def put(c, v):
            o_ref[:, c * BW:(c + 1) * BW] = v.astype(BF16)

        cosv, sinv = cos_ref[...], sin_ref[...]

        def rot_t(dv):
            return dv * cosv + _swap_halves(dv * sinv)

        put(RQ, rot_t(drq_ref[...] * (HD ** -0.5)))
        put(RK, rot_t(drk_ref[...]))
        put(RV, drv_ref[...])
        put(RG, drg_ref[...])
        put(PV, dpv_ref[...])
        dq, gq = _gnorm_bwd(dnq_ref[...], col(NQ), gq_ref[...])
        put(NQ, dq)
        _acc(dgq_ref, jnp.sum(gq, axis=0, keepdims=True), first)
        dk, gk = _gnorm_bwd(dnk_ref[...], col(NK), gk_ref[...])
        put(NK, dk)
        _acc(dgk_ref, jnp.sum(gk, axis=0, keepdims=True), first)
        put(NV, dnv_ref[...])
        dm, gm = _gnorm_bwd(dmq_ref[...], col(MQ), gm_ref[...])
        put(MQ, dm)
        _acc(dgm_ref, jnp.sum(gm, axis=0, keepdims=True), first)

    blk = pl.BlockSpec((tm, BW), lambda i: (i, 0))
    vec = pl.BlockSpec((1, BW), lambda i: (0, 0))
    wide = pl.BlockSpec((tm, 9 * BW), lambda i: (i, 0))
    return pl.pallas_call(
        body, grid=(t // tm,), in_specs=[wide, blk, blk, vec, vec, vec] + [blk] * 9, out_specs=(wide, vec, vec, vec),
        out_shape=(_sds((t, 9 * BW), BF16), _sds((1, BW), F32), _sds((1, BW), F32), _sds((1, BW), F32)),
        name="prep_bwd", compiler_params=_cp())(proj, cos2, sin2, g_naq, g_nak, g_mq, d_rq, d_rk, d_rv, d_rg, d_pv, d_nq, d_nk,
                                                d_nv, d_mq)


RET_B = 256


def _ret_consts(lgf_ref, lgb_ref):
    bsz = RET_B
    head = _lane_head((1, BW))
    lf, lb = jnp.zeros((1, BW), F32), jnp.zeros((1, BW), F32)
    for h in range(NH):
        lf = lf + jnp.where(head == h, lgf_ref[h], 0.0)
        lb = lb + jnp.where(head == h, lgb_ref[h], 0.0)
    pos = lax.broadcasted_iota(jnp.int32, (bsz, BW), 0).astype(F32)
    up, down = pos + 1.0, (bsz - 1.0) - pos
    c = dict(up=up, down=down, kf=jnp.exp(down * lf), kb=jnp.exp(up * lb), qf=jnp.exp(up * lf), qb=jnp.exp(down * lb),
             cf=jnp.exp(bsz * lf), cb=jnp.exp(bsz * lb))
    diff = (lax.broadcasted_iota(jnp.int32, (NH * bsz, 1), 0) & (bsz - 1)) - lax.broadcasted_iota(jnp.int32, (1, bsz), 1)
    c["causal"] = diff >= 0
    c["dist"] = jnp.abs(diff).astype(F32)
    lgf = jnp.concatenate([jnp.full((bsz, 1), lgf_ref[h], F32) for h in range(NH)], axis=0)
    lgb = jnp.concatenate([jnp.full((bsz, 1), lgb_ref[h], F32) for h in range(NH)], axis=0)
    c["dm"] = jnp.exp(c["dist"] * jnp.where(c["causal"], lgf, lgb))
    c["bd"] = _lane_head((BW, BW)) == lax.shift_right_logical(lax.broadcasted_iota(jnp.int32, (BW, BW), 0), 6)
    return c


def _ret_states(k_ref, v_ref, st_ref, c, nb):
    bsz = RET_B

    def summary(b, decay):
        kb = k_ref[b * bsz:(b + 1) * bsz, :].astype(F32)
        return jnp.where(c["bd"], _dot(kb * decay, v_ref[b * bsz:(b + 1) * bsz, :], ta=True), 0.0)

    f = jnp.zeros((BW, BW), F32)
    for b in range(nb):
        st_ref[b] = f
        if b < nb - 1:
            f = c["cf"] * f + summary(b, c["kf"])
    g = jnp.zeros((BW, BW), F32)
    for b in reversed(range(nb)):
        st_ref[nb + b] = g
        if b > 0:
            g = c["cb"] * g + summary(b, c["kb"])


def _ret_fwd(q, k, v, proj, lgf, lgb, g_ret):
    t = q.shape[0]
    bsz, nb = RET_B, t // RET_B

    def body(lgf_ref, lgb_ref, q_ref, k_ref, v_ref, rg_ref, g_ref, o_ref, ret_ref, st_ref):
        c = _ret_consts(lgf_ref, lgb_ref)
        _ret_states(k_ref, v_ref, st_ref, c, nb)
        for b in range(nb):
            blk = slice(b * bsz, (b + 1) * bsz)
            qb, kb, vb = q_ref[blk, :], k_ref[blk, :], v_ref[blk, :]
            s = _dot(_stack_heads(qb), kb, tb=True)
            o = _unstack_heads(_dot(s * c["dm"], vb), bsz)
            q32 = qb.astype(F32)
            o = o + _dot(q32 * c["qf"], st_ref[b]) + _dot(q32 * c["qb"], st_ref[nb + b])
            o_ref[blk, :] = o
            rg = rg_ref[blk, :]
            ret_ref[blk, :] = (_gnorm(o, g_ref[...]) * (rg * _sigmoid(rg))).astype(BF16)

    whole = pl.BlockSpec((t, BW), lambda i: (0, 0))
    return pl.pallas_call(
        body, grid=(1,),
        in_specs=[SMEM, SMEM, whole, whole, whole, pl.BlockSpec((t, BW), lambda i: (0, RG)), pl.BlockSpec((1, BW), lambda i: (0, 0))],
        out_specs=(whole, whole), out_shape=(_sds((t, BW), F32), _sds((t, BW), BF16)),
        scratch_shapes=[pltpu.VMEM((2 * nb, BW, BW), F32)], name="ret_fwd", compiler_params=_cp())(lgf, lgb, q, k, v, proj, g_ret)


def _ret_post_bwd(dbr, o_ret, proj, g_ret):
    t = o_ret.shape[0]
    tm = 256

    def body(d_ref, o_ref, rg_ref, g_ref, do_ref, drg_ref, dg_ref):
        dret, o, rg, g = d_ref[...], o_ref[...], rg_ref[...], g_ref[...]
        sg = _sigmoid(rg)
        do, dgain = _gnorm_bwd(dret * (rg * sg), o, g)
        do_ref[...] = do.astype(BF16)
        drg_ref[...] = dret * _gnorm(o, g) * (sg * (1.0 + rg * (1.0 - sg)))
        _acc(dg_ref, jnp.sum(dgain, axis=0, keepdims=True), pl.program_id(0) == 0)

    blk = pl.BlockSpec((tm, BW), lambda i: (i, 0))
    vec = pl.BlockSpec((1, BW), lambda i: (0, 0))
    return pl.pallas_call(
        body, grid=(t // tm,), in_specs=[blk, blk, pl.BlockSpec((tm, BW), lambda i: (i, RG)), vec], out_specs=(blk, blk, vec),
        out_shape=(_sds((t, BW), BF16), _sds((t, BW), F32), _sds((1, BW), F32)), name="ret_post_bwd",
        compiler_params=_cp())(dbr, o_ret, proj, g_ret)


def _ret_bwd(do, q, k, v, lgf, lgb):
    t = q.shape[0]
    bsz, nb = RET_B, t // RET_B

    def body(lgf_ref, lgb_ref, d_ref, q_ref, k_ref, v_ref, dq_ref, dk_ref, dv_ref, dlg_ref, st_ref, sd_ref):
        c = _ret_consts(lgf_ref, lgb_ref)
        _ret_states(k_ref, v_ref, st_ref, c, nb)
        lane_f, lane_b = jnp.zeros((1, BW), F32), jnp.zeros((1, BW), F32)
        row_f, row_b = jnp.zeros((NH * bsz, 1), F32), jnp.zeros((NH * bsz, 1), F32)

        def rows(x):
            return jnp.sum(x, axis=0, keepdims=True)

        for b in range(nb):
            blk = slice(b * bsz, (b + 1) * bsz)
            qb, kb, vb, dob = q_ref[blk, :], k_ref[blk, :], v_ref[blk, :], d_ref[blk, :]
            q32 = qb.astype(F32)
            qs, dos = _stack_heads(qb), _stack_heads(dob)
            s = _dot(qs, kb, tb=True)
            da = _dot(dos, vb, tb=True)
            dv_ref[blk, :] = _dot(s * c["dm"], dos, ta=True)
            ds = da * c["dm"]
            w = ds * s * c["dist"]
            row_f = row_f + jnp.sum(jnp.where(c["causal"], w, 0.0), axis=1, keepdims=True)
            row_b = row_b + jnp.sum(jnp.where(c["causal"], 0.0, w), axis=1, keepdims=True)
            dsb = ds.astype(MXU)
            dk_ref[blk, :] = _dot(dsb, qs, ta=True)
            dq_f = _dot(dob, st_ref[b], tb=True) * c["qf"]
            dq_b = _dot(dob, st_ref[nb + b], tb=True) * c["qb"]
            lane_f = lane_f + rows(c["up"] * dq_f * q32)
            lane_b = lane_b + rows(c["down"] * dq_b * q32)
            dq_ref[blk, :] = _unstack_heads(_dot(dsb, kb), bsz) + dq_f + dq_b
            sd_ref[b] = jnp.where(c["bd"], _dot(q32 * c["qf"], dob, ta=True), 0.0)
            sd_ref[nb + b] = jnp.where(c["bd"], _dot(q32 * c["qb"], dob, ta=True), 0.0)

        def through_state(b, grad, decay, weight, lane):
            blk = slice(b * bsz, (b + 1) * bsz)
            k32 = k_ref[blk, :].astype(F32)
            dk = _dot(v_ref[blk, :], grad, tb=True) * decay
            dk_ref[blk, :] += dk
            dv_ref[blk, :] += _dot(k32 * decay, grad)
            return lane + rows(weight * dk * k32)

        phi = jnp.zeros((BW, BW), F32)
        for b in reversed(range(nb)):
            if b < nb - 1:
                lane_f = through_state(b, phi, c["kf"], c["down"], lane_f)
                lane_f = lane_f + bsz * rows(c["cf"] * st_ref[b] * phi)
            phi = sd_ref[b] + c["cf"] * phi
        gam = jnp.zeros((BW, BW), F32)
        for b in range(nb):
            if b > 0:
                lane_b = through_state(b, gam, c["kb"], c["up"], lane_b)
                lane_b = lane_b + bsz * rows(c["cb"] * st_ref[nb + b] * gam)
            gam = sd_ref[nb + b] + c["cb"] * gam

        head = _lane_head((1, BW))
        for h in range(NH):
            tot_f = jnp.sum(row_f[h * bsz:(h + 1) * bsz, :]) + jnp.sum(jnp.where(head == h, lane_f, 0.0))
            tot_b = jnp.sum(row_b[h * bsz:(h + 1) * bsz, :]) + jnp.sum(jnp.where(head == h, lane_b, 0.0))
            dlg_ref[h:h + 1, :] = jnp.full((1, 128), tot_f, F32)
            dlg_ref[NH + h:NH + h + 1, :] = jnp.full((1, 128), tot_b, F32)

    whole = pl.BlockSpec((t, BW), lambda i: (0, 0))
    return pl.pallas_call(
        body, grid=(1,), in_specs=[SMEM, SMEM, whole, whole, whole, whole],
        out_specs=(whole, whole, whole, pl.BlockSpec((2 * NH, 128), lambda i: (0, 0))),
        out_shape=(_sds((t, BW), F32), _sds((t, BW), F32), _sds((t, BW), F32), _sds((2 * NH, 128), F32)),
        scratch_shapes=[pltpu.VMEM((2 * nb, BW, BW), F32), pltpu.VMEM((2 * nb, BW, BW), F32)], name="ret_bwd",
        compiler_params=_cp())(lgf, lgb, do, q, k, v)


def _pool_windows(t):
    row = lax.broadcasted_iota(jnp.int32, (t, BW), 0)
    half = lax.shift_left(jnp.ones((t, BW), jnp.int32), _lane_head((t, BW)))
    cnt = (jnp.minimum(row + half, t) - jnp.maximum(row - half, 0)).astype(F32)
    return row, half, cnt


def _pool_window_sum(v, row, half, t, transpose):
    out = jnp.zeros_like(v)
    for j in range(-POOL_HALF_MAX, POOL_HALF_MAX):
        src = row - j if transpose else row + j
        ok = (src >= 0) & (src < t) & (j >= -half) & (j < half)
        out = out + jnp.where(ok, pltpu.roll(v, (j if transpose else -j) % t, 0), 0.0)
    return out


def _pool_fwd(proj, wbd, scale):
    t = proj.shape[0]

    def body(v_ref, w_ref, s_ref, o_ref):
        v = v_ref[...]
        row, half, cnt = _pool_windows(t)
        pooled = _pool_window_sum(v, row, half, t, False) / cnt - v
        o_ref[...] = (_dot(pooled, w_ref[...]) * s_ref[...]).astype(BF16)

    return pl.pallas_call(
        body, grid=(1,),
        in_specs=[pl.BlockSpec((t, BW), lambda i: (0, PV)), pl.BlockSpec((BW, BW), lambda i: (0, 0)), pl.BlockSpec((1, BW), lambda i: (0, 0))],
        out_specs=pl.BlockSpec((t, BW), lambda i: (0, 0)), out_shape=_sds((t, BW), BF16), name="pool_fwd",
        compiler_params=_cp())(proj, wbd, scale)


def _pool_bwd(dbr, proj, wbd, scale):
    t = proj.shape[0]

    def body(d_ref, v_ref, w_ref, s_ref, dv_ref, dw_ref, ds_ref):
        v, dout = v_ref[...], d_ref[...]
        row, half, cnt = _pool_windows(t)
        pooled = _pool_window_sum(v, row, half, t, False) / cnt - v
        mixed = _dot(pooled, w_ref[...])
        ds_ref[...] = jnp.sum(dout * mixed, axis=0, keepdims=True)
        dmixed = dout * s_ref[...]
        dw_ref[...] = _dot(pooled, dmixed, ta=True)
        dpooled = _dot(dmixed, w_ref[...], tb=True)
        dv_ref[...] = _pool_window_sum(dpooled / cnt, row, half, t, True) - dpooled

    return pl.pallas_call(
        body, grid=(1,),
        in_specs=[pl.BlockSpec((t, BW), lambda i: (0, 1)), pl.BlockSpec((t, BW), lambda i: (0, PV)),
                  pl.BlockSpec((BW, BW), lambda i: (0, 0)), pl.BlockSpec((1, BW), lambda i: (0, 0))],
        out_specs=(pl.BlockSpec((t, BW), lambda i: (0, 0)), pl.BlockSpec((BW, BW), lambda i: (0, 0)), pl.BlockSpec((1, BW), lambda i: (0, 0))),
        out_shape=(_sds((t, BW), F32), _sds((BW, BW), F32), _sds((1, BW), F32)), name="pool_bwd",
        compiler_params=_cp())(dbr, proj, wbd, scale)


NA_KEYS = NA_ROWS_WIN * GRID_W


def _na_window(r, n_rows):
    rs = jnp.clip(r - NA_ROWS_WIN // 2, 0, n_rows - NA_ROWS_WIN)
    return pl.multiple_of(rs * GRID_W, GRID_W), rs - r + (NA_ROWS_WIN - 1)


NA_STEP_ROWS = 4


def _na_fwd(q, k, v, ball):
    t = q.shape[0]
    n_rows = t // GRID_W
    rows = NA_STEP_ROWS

    def body(q_ref, k_ref, v_ref, b_ref, o_ref):
        for rr in range(rows):
            start, a0 = _na_window(pl.program_id(0) * rows + rr, n_rows)
            own = slice(rr * GRID_W, (rr + 1) * GRID_W)
            qs = _stack_heads(q_ref[own, :])
            s = _dot(qs, k_ref[pl.ds(start, NA_KEYS), :], tb=True) * (HD ** -0.5) + b_ref[a0]
            p = _softmax_rows(s)
            o_ref[own, :] = _unstack_heads(_dot(p, v_ref[pl.ds(start, NA_KEYS), :]), GRID_W).astype(BF16)

    blk = pl.BlockSpec((rows * GRID_W, BW), lambda r: (r, 0))
    whole = pl.BlockSpec((t, BW), lambda r: (0, 0))
    return pl.pallas_call(
        body, grid=(n_rows // rows,), in_specs=[blk, whole, whole, pl.BlockSpec(ball.shape, lambda r: (0, 0, 0))],
        out_specs=blk, out_shape=_sds((t, BW), BF16), name="na_fwd", compiler_params=_cp())(q, k, v, ball)


def _na_bwd(dbr, q, k, v, ball):
    t = q.shape[0]
    n_rows = t // GRID_W

    rows = NA_STEP_ROWS

    def body(d_ref, q_ref, k_ref, v_ref, b_ref, dq_ref, dk_ref, dv_ref, db_ref):
        @pl.when(pl.program_id(0) == 0)
        def _():
            dk_ref[...] = jnp.zeros_like(dk_ref)
            dv_ref[...] = jnp.zeros_like(dv_ref)
            db_ref[...] = jnp.zeros_like(db_ref)

        for rr in range(rows):
            start, a0 = _na_window(pl.program_id(0) * rows + rr, n_rows)
            keys = pl.ds(start, NA_KEYS)
            own = slice(rr * GRID_W, (rr + 1) * GRID_W)
            qs = _stack_heads(q_ref[own, :])
            kb, vb = k_ref[keys, :], v_ref[keys, :]
            p = _softmax_rows(_dot(qs, kb, tb=True) * (HD ** -0.5) + b_ref[a0])
            dos = _stack_heads(d_ref[own, :]).astype(MXU)
            dp = _dot(dos, vb, tb=True)
            dv_ref[keys, :] += _dot(p, dos, ta=True)
            ds = p * (dp - jnp.sum(dp * p, axis=-1, keepdims=True))
            db_ref[a0] += ds
            dsb = (ds * (HD ** -0.5)).astype(MXU)
            dq_ref[own, :] = _unstack_heads(_dot(dsb, kb), GRID_W)
            dk_ref[keys, :] += _dot(dsb, qs, ta=True)

    blk = pl.BlockSpec((rows * GRID_W, BW), lambda r: (r, 0))
    whole = pl.BlockSpec((t, BW), lambda r: (0, 0))
    tab = pl.BlockSpec(ball.shape, lambda r: (0, 0, 0))
    return pl.pallas_call(
        body, grid=(n_rows // rows,), in_specs=[pl.BlockSpec((rows * GRID_W, BW), lambda r: (r, 2)), blk, whole, whole, tab],
        out_specs=(blk, whole, whole, tab),
        out_shape=(_sds((t, BW), F32), _sds((t, BW), F32), _sds((t, BW), F32), _sds(ball.shape, F32)), name="na_bwd",
        compiler_params=_cp())(dbr, q, k, v, ball)


def _rpb_expand(rpb_pad, onehot):
    def body(r_ref, e_ref, o_ref):
        o_ref[...] = jnp.dot(r_ref[...], e_ref[...], precision=HI, preferred_element_type=F32)

    return pl.pallas_call(body, out_shape=_sds((64, GRID_W * GRID_W), F32), name="rpb_expand", compiler_params=_cp())(rpb_pad, onehot)


def _rpb_reduce(dtab, onehot):
    def body(d_ref, e_ref, o_ref):
        o_ref[...] = lax.dot_general(d_ref[...], e_ref[...], (((1,), (1,)), ((), ())), precision=HI, preferred_element_type=F32)

    return pl.pallas_call(body, out_shape=_sds((64, 128), F32), name="rpb_reduce", compiler_params=_cp())(dtab, onehot)


MEM_TQ = 256


def _mem_fwd(q, mk, mv):
    t = q.shape[0]
    tq = MEM_TQ

    def body(q_ref, k_ref, v_ref, o_ref):
        qv = q_ref[...]
        head = _lane_head(qv.shape)
        out = jnp.zeros((tq, BW), F32)
        for h in range(NH):
            p = _softmax_rows(_dot(jnp.where(head == h, qv, jnp.zeros_like(qv)), k_ref[...], tb=True) * (HD ** -0.5))
            out = out + jnp.where(head == h, _dot(p, v_ref[...]), 0.0)
        o_ref[...] = out.astype(BF16)

    blk = pl.BlockSpec((tq, BW), lambda i: (i, 0))
    kv = pl.BlockSpec((N_MEM, BW), lambda i: (0, 0))
    return pl.pallas_call(body, grid=(t // tq,), in_specs=[blk, kv, kv], out_specs=blk, out_shape=_sds((t, BW), BF16),
                          name="mem_fwd", compiler_params=_cp())(q, mk, mv)


def _mem_bwd(dbr, q, mk, mv):
    t = q.shape[0]
    tq = MEM_TQ

    def body(d_ref, q_ref, k_ref, v_ref, dq_ref, dk_ref, dv_ref):
        first = pl.program_id(0) == 0
        qv, dout = q_ref[...], d_ref[...]
        head = _lane_head(qv.shape)
        dq = jnp.zeros((tq, BW), F32)
        dk = jnp.zeros((N_MEM, BW), F32)
        dv = jnp.zeros((N_MEM, BW), F32)
        for h in range(NH):
            qh = jnp.where(head == h, qv, jnp.zeros_like(qv))
            doh = jnp.where(head == h, dout, 0.0).astype(MXU)
            p = _softmax_rows(_dot(qh, k_ref[...], tb=True) * (HD ** -0.5))
            dp = _dot(doh, v_ref[...], tb=True)
            dv = dv + _dot(p, doh, ta=True)
            dsb = (p * (dp - jnp.sum(dp * p, axis=-1, keepdims=True)) * (HD ** -0.5)).astype(MXU)
            dq = dq + jnp.where(head == h, _dot(dsb, k_ref[...]), 0.0)
            dk = dk + _dot(dsb, qh, ta=True)
        dq_ref[...] = dq
        _acc(dk_ref, dk, first)
        _acc(dv_ref, dv, first)

    blk = pl.BlockSpec((tq, BW), lambda i: (i, 0))
    kv = pl.BlockSpec((N_MEM, BW), lambda i: (0, 0))
    return pl.pallas_call(
        body, grid=(t // tq,), in_specs=[pl.BlockSpec((tq, BW), lambda i: (i, 3)), blk, kv, kv], out_specs=(blk, kv, kv),
        out_shape=(_sds((t, BW), F32), _sds((N_MEM, BW), F32), _sds((N_MEM, BW), F32)), name="mem_bwd",
        compiler_params=_cp())(dbr, q, mk, mv)


def _memkv_prep(kv, g_mk):
    def body(kv_ref, g_ref, k_ref, v_ref):
        k_ref[...] = _gnorm(kv_ref[:, 0:BW], g_ref[...]).astype(BF16)
        v_ref[...] = kv_ref[:, BW:2 * BW].astype(BF16)

    return pl.pallas_call(body, out_shape=(_sds((N_MEM, BW), BF16), _sds((N_MEM, BW), BF16)), name="memkv_prep",
                          compiler_params=_cp())(kv, g_mk)


def _memkv_bwd(kv, dk, dv, g_mk):
    def body(kv_ref, dk_ref, dv_ref, g_ref, o_ref, dg_ref):
        dkk, gain = _gnorm_bwd(dk_ref[...], kv_ref[:, 0:BW], g_ref[...])
        o_ref[:, 0:BW] = dkk.astype(BF16)
        o_ref[:, BW:2 * BW] = dv_ref[...].astype(BF16)
        dg_ref[...] = jnp.sum(gain, axis=0, keepdims=True)

    return pl.pallas_call(body, out_shape=(_sds((N_MEM, 2 * BW), BF16), _sds((1, BW), F32)), name="memkv_bwd",
                          compiler_params=_cp())(kv, dk, dv, g_mk)


MERGE_TM = 256


def _merge_fwd(brs, wbt, gp):
    t = gp.shape[0]
    tm = MERGE_TM

    def body(b0, b1, b2, b3, wb_ref, gp_ref, o_ref):
        out = jnp.zeros((tm, D), F32)
        for n, b_ref in enumerate((b0, b1, b2, b3)):
            up = _dot(b_ref[...], wb_ref[n], tb=True)
            out = out + _sigmoid(gp_ref[:, n * D:(n + 1) * D]) * up
        o_ref[...] = out.astype(BF16)

    blk = pl.BlockSpec((tm, BW), lambda i: (i, 0))
    return pl.pallas_call(
        body, grid=(t // tm,),
        in_specs=[blk, blk, blk, blk, pl.BlockSpec((NH, D, BW), lambda i: (0, 0, 0)), pl.BlockSpec((tm, NH * D), lambda i: (i, 0))],
        out_specs=pl.BlockSpec((tm, D), lambda i: (i, 0)), out_shape=_sds((t, D), BF16), name="merge_fwd",
        compiler_params=_cp())(*brs, wbt, gp)


def _merge_bwd(dmerged, brs, wbt, gp):
    t = gp.shape[0]
    tm = MERGE_TM

    def body(d_ref, b0, b1, b2, b3, wb_ref, gp_ref, dgp_ref, dup_ref):
        dm = d_ref[...]
        for n, b_ref in enumerate((b0, b1, b2, b3)):
            up = _dot(b_ref[...], wb_ref[n], tb=True)
            g = _sigmoid(gp_ref[:, n * D:(n + 1) * D])
            dgp_ref[:, n * D:(n + 1) * D] = (dm * up * (g * (1.0 - g))).astype(BF16)
            dup_ref[:, n * D:(n + 1) * D] = (dm * g).astype(BF16)

    row = pl.BlockSpec((tm, D), lambda i: (i, 0))
    blk = pl.BlockSpec((tm, BW), lambda i: (i, 0))
    wide = pl.BlockSpec((tm, NH * D), lambda i: (i, 0))
    return pl.pallas_call(
        body, grid=(t // tm,), in_specs=[row, blk, blk, blk, blk, pl.BlockSpec((NH, D, BW), lambda i: (0, 0, 0)), wide],
        out_specs=(wide, wide), out_shape=(_sds((t, NH * D), BF16), _sds((t, NH * D), BF16)), name="merge_bwd",
        compiler_params=_cp())(dmerged, *brs, wbt, gp)


def _dbranch(dup, wbt):
    t = dup.shape[0]
    tm = 512

    def body(d_ref, w_ref, o_ref):
        o_ref[...] = _dot(d_ref[...], w_ref[...])

    return pl.pallas_call(
        body, grid=(t // tm, NH), in_specs=[pl.BlockSpec((tm, D), lambda i, n: (i, n)), pl.BlockSpec((None, D, BW), lambda i, n: (n, 0, 0))],
        out_specs=pl.BlockSpec((tm, BW), lambda i, n: (i, n)), out_shape=_sds((t, NH * BW), F32), name="dbranch",
        compiler_params=_cp())(dup, wbt)


def _dwbranch(brs, dup):
    t = dup.shape[0]

    def body(b0, b1, b2, b3, d_ref, o_ref):
        for n, b_ref in enumerate((b0, b1, b2, b3)):
            o_ref[n] = _dot(d_ref[:, n * D:(n + 1) * D], b_ref[...], ta=True).astype(BF16)

    return pl.pallas_call(body, out_shape=_sds((NH, D, BW), BF16), name="dwbranch", compiler_params=_cp())(*brs, dup)


def _swiglu_fwd(ag):
    t = ag.shape[0]
    tm = 256

    def body(ag_ref, o_ref):
        a, g = ag_ref[:, 0:FF], ag_ref[:, FF:2 * FF]
        o_ref[...] = (a * _sigmoid(a) * g).astype(BF16)

    return pl.pallas_call(body, grid=(t // tm,), in_specs=[pl.BlockSpec((tm, 2 * FF), lambda i: (i, 0))],
                          out_specs=pl.BlockSpec((tm, FF), lambda i: (i, 0)), out_shape=_sds((t, FF), BF16), name="swiglu_fwd",
                          compiler_params=_cp())(ag)


def _swiglu_bwd(ag, dy):
    t = ag.shape[0]
    tm = 256

    def body(ag_ref, dy_ref, o_ref):
        a, g, d = ag_ref[:, 0:FF], ag_ref[:, FF:2 * FF], dy_ref[...]
        s = _sigmoid(a)
        o_ref[:, 0:FF] = (d * g * (s * (1.0 + a * (1.0 - s)))).astype(BF16)
        o_ref[:, FF:2 * FF] = (d * (a * s)).astype(BF16)

    return pl.pallas_call(
        body, grid=(t // tm,), in_specs=[pl.BlockSpec((tm, 2 * FF), lambda i: (i, 0)), pl.BlockSpec((tm, FF), lambda i: (i, 0))],
        out_specs=pl.BlockSpec((tm, 2 * FF), lambda i: (i, 0)), out_shape=_sds((t, 2 * FF), BF16), name="swiglu_bwd",
        compiler_params=_cp())(ag, dy)


def _loss_head(y, target):
    t, d = y.shape
    tm = 256

    def body(y_ref, t_ref, dy_ref, dyb_ref, l_ref):
        e = y_ref[...] - t_ref[...]
        dy_ref[...] = e * (1.0 / d)
        dyb_ref[...] = (e * (1.0 / d)).astype(BF16)
        _acc(l_ref, jnp.full((8, 128), 0.5 * jnp.sum(jnp.sum(e * e, axis=-1, keepdims=True) * (1.0 / d)), F32), pl.program_id(0) == 0)

    row = pl.BlockSpec((tm, d), lambda i: (i, 0))
    return pl.pallas_call(body, grid=(t // tm,), in_specs=[row, row], out_specs=(row, row, pl.BlockSpec((8, 128), lambda i: (0, 0))),
                          out_shape=(_sds((t, d), F32), _sds((t, d), BF16), _sds((8, 128), F32)), name="loss_head",
                          compiler_params=_cp())(y, target)


def _sum_slots(x, name):
    k, r, c = x.shape
    tr = _tile(r, 512) if r % 128 == 0 else r

    def body(x_ref, o_ref):
        acc = x_ref[0].astype(F32)
        for s in range(1, k):
            acc = acc + x_ref[s].astype(F32)
        o_ref[...] = acc

    return pl.pallas_call(body, grid=(r // tr,), in_specs=[pl.BlockSpec((k, tr, c), lambda i: (0, i, 0))],
                          out_specs=pl.BlockSpec((tr, c), lambda i: (i, 0)), out_shape=_sds((r, c), F32), name=name,
                          compiler_params=_cp())(x)


def _pair_sum(bufs, recvs, cidx):
    n = len(bufs)

    def body(c_ref, *refs):
        for i in range(n):
            refs[2 * n + i][...] = (refs[i][...].astype(F32) + refs[n + i][...].astype(F32)).astype(BF16)

    return pl.pallas_call(
        body,
        grid_spec=pltpu.PrefetchScalarGridSpec(
            num_scalar_prefetch=1, grid=(4,),
            in_specs=[pl.BlockSpec((None, None) + b.shape[2:], lambda s, cref: (s, cref[0], 0, 0)) for b in bufs]
            + [pl.BlockSpec((None,) + r.shape[1:], lambda s, cref: (s, 0, 0)) for r in recvs],
            out_specs=tuple(pl.BlockSpec((None,) + r.shape[1:], lambda s, cref: (s, 0, 0)) for r in recvs)),
        out_shape=tuple(_sds(r.shape, BF16) for r in recvs), name="rs_pair_sum", compiler_params=_cp())(cidx, *bufs, *recvs)


def _adamw_update(w, gv, m, v):
    mn = ADAM_B1 * m + (1.0 - ADAM_B1) * gv
    vn = ADAM_B2 * v + (1.0 - ADAM_B2) * (gv * gv)
    m_hat = mn / (1.0 - ADAM_B1 ** ADAM_STEP)
    v_hat = vn / (1.0 - ADAM_B2 ** ADAM_STEP)
    return -ADAM_LR * (m_hat / (jnp.sqrt(v_hat) + ADAM_EPS) + ADAM_WD * w), mn, vn


def _adamw(w, g, m, v, name):
    r, c = w.shape

    def body(w_ref, g_ref, m_ref, v_ref, d_ref, nm_ref, nv_ref):
        d_ref[...], nm_ref[...], nv_ref[...] = _adamw_update(w_ref[...], g_ref[...], m_ref[...], v_ref[...])

    blk = pl.BlockSpec((r, c), lambda i: (0, 0))
    return pl.pallas_call(body, grid=(1,), in_specs=[blk] * 4, out_specs=(blk,) * 3,
                          out_shape=tuple(_sds((r, c), F32) for _ in range(3)), name=name, compiler_params=_cp())(w, g, m, v)


def _adamw_layers(w, gs, m, v, name):
    _, r, c = w.shape
    tr = max(d for d in range(8, r + 1, 8) if r % d == 0 and d * c * 4 <= 2 ** 20)

    def body(w_ref, m_ref, v_ref, *refs):
        g_refs, (d_ref, nm_ref, nv_ref, g_ref) = refs[:DEPTH], refs[DEPTH:]
        layer = pl.program_id(0)
        gv = g_refs[0][...]
        for k in range(1, DEPTH):
            gv = jnp.where(layer == k, g_refs[k][...], gv)
        d_ref[...], nm_ref[...], nv_ref[...] = _adamw_update(w_ref[...], gv, m_ref[...], v_ref[...])
        g_ref[...] = gv

    blk = pl.BlockSpec((None, tr, c), lambda l, i: (l, i, 0))
    g_specs = [pl.BlockSpec((tr, c), functools.partial(lambda l, i, k: (jnp.where(l == k, i, 0), 0), k=k)) for k in range(DEPTH)]
    return pl.pallas_call(body, grid=(DEPTH, r // tr), in_specs=[blk] * 3 + g_specs, out_specs=(blk,) * 4,
                          out_shape=tuple(_sds(w.shape, F32) for _ in range(4)), name=name, compiler_params=_cp())(w, m, v, *gs)


def _all_gather(shards, name):
    n = len(shards)

    def body(*refs):
        x_refs, out_refs = refs[:n], refs[n:2 * n]
        send_sems, recv_sems, local_sems = refs[2 * n:]
        x, y, cc = lax.axis_index("x"), lax.axis_index("y"), lax.axis_index("c")
        me, sibling = (x, y, cc), (x, y, 1 - cc)
        chips = [(1 - x, y), (x, 1 - y), (1 - x, 1 - y)]

        def copy(i, k, block, to, own=False):
            px, py, pc = block
            slot = out_refs[i].at[4 * px + 2 * py + pc]
            return pltpu.make_async_remote_copy(
                src_ref=x_refs[i] if own else slot, dst_ref=slot, send_sem=send_sems.at[7 * i + k],
                recv_sem=recv_sems.at[7 * i + k], device_id=to, device_id_type=MESH)

        mine = [pltpu.make_async_copy(x_refs[i], out_refs[i].at[4 * x + 2 * y + cc], local_sems.at[i]) for i in range(n)]
        for cp in mine:
            cp.start()
        first = []
        for j, chip in enumerate(chips):
            first += [copy(i, 1 + j, me, (*chip, cc), own=True) for i in range(n)]
        first += [copy(i, 0, me, sibling, own=True) for i in range(n)]
        for cp in first:
            cp.start()
        passed = []
        for j, chip in enumerate(chips):
            for i in range(n):
                copy(i, 1 + j, (*chip, cc), me).wait_recv()
                cp = copy(i, 4 + j, (*chip, cc), sibling)
                cp.start()
                passed.append(cp)
        for i in range(n):
            copy(i, 0, sibling, me).wait_recv()
        for j, chip in enumerate(chips):
            for i in range(n):
                copy(i, 4 + j, (*chip, 1 - cc), me).wait_recv()
        for cp in first + passed:
            cp.wait_send()
        for cp in mine:
            cp.wait()

    return pl.pallas_call(
        body, out_shape=tuple(_sds((N_DEV,) + s.shape, s.dtype) for s in shards), in_specs=[ANY] * n, out_specs=(ANY,) * n,
        scratch_shapes=[pltpu.SemaphoreType.DMA((7 * n,)), pltpu.SemaphoreType.DMA((7 * n,)), pltpu.SemaphoreType.DMA((n,))],
        name=name)(*shards)


def _rs_core_swap(bufs, name):
    n = len(bufs)

    def body(*refs):
        b_refs, recv_refs = refs[:n], refs[n:2 * n]
        send_sems, recv_sems = refs[2 * n:]
        x, y, cc = lax.axis_index("x"), lax.axis_index("y"), lax.axis_index("c")
        copies = [pltpu.make_async_remote_copy(
            src_ref=b_refs[i].at[s, 1 - cc], dst_ref=recv_refs[i].at[s], send_sem=send_sems.at[4 * i + s],
            recv_sem=recv_sems.at[4 * i + s], device_id=(x, y, 1 - cc), device_id_type=MESH) for i in range(n) for s in range(4)]
        for cp in copies:
            cp.start()
        for cp in copies:
            cp.wait()

    return pl.pallas_call(
        body, out_shape=tuple(_sds((4,) + b.shape[2:], b.dtype) for b in bufs), in_specs=[ANY] * n, out_specs=(ANY,) * n,
        scratch_shapes=[pltpu.SemaphoreType.DMA((4 * n,)), pltpu.SemaphoreType.DMA((4 * n,))], name=name)(*bufs)


HBM = pl.BlockSpec(memory_space=pltpu.HBM)
SEMS = pl.BlockSpec(memory_space=pltpu.SEMAPHORE)
EFFECT = pltpu.SideEffectType.DATAFLOW_SIDE_EFFECTING


def _hbm(a):
    return pltpu.HBM(a.shape, a.dtype)


def _other_chips(x, y):
    return [(1 - x, y), (x, 1 - y), (1 - x, 1 - y)]


def _ici_start(srcs, lands, by_chip, name):
    n = len(srcs)

    def body(*refs):
        s_refs, land_refs = refs[:n], refs[n:2 * n]
        send_sems, recv_sems = refs[2 * n], refs[2 * n + 1]
        token = refs[-1]
        x, y, cc = lax.axis_index("x"), lax.axis_index("y"), lax.axis_index("c")
        mine = 2 * x + y if by_chip else 4 * x + 2 * y + cc
        for px, py in _other_chips(x, y):
            for i in range(n):
                pltpu.make_async_remote_copy(
                    src_ref=s_refs[i].at[2 * px + py] if by_chip else s_refs[i], dst_ref=land_refs[i].at[mine],
                    send_sem=send_sems.at[i], recv_sem=recv_sems.at[i], device_id=(px, py, cc), device_id_type=MESH).start()
        token[...] = jnp.zeros_like(token)

    out = pl.pallas_call(
        body, name=name,
        out_shape=(pltpu.SemaphoreType.DMA((n,)), pltpu.SemaphoreType.DMA((n,)), *[_hbm(s) for s in srcs], *[_hbm(l) for l in lands],
                   _sds((8, 128), F32)),
        in_specs=[HBM] * (2 * n), out_specs=(SEMS, SEMS, *[HBM] * (2 * n), pl.BlockSpec(memory_space=pltpu.VMEM)),
        input_output_aliases={i: 2 + i for i in range(2 * n)}, compiler_params=pltpu.CompilerParams(has_side_effects=EFFECT),
    )(*[pltpu.with_memory_space_constraint(s, pltpu.HBM) for s in srcs],
      *[pltpu.with_memory_space_constraint(l, pltpu.HBM) for l in lands])
    return out[0], out[1], out[2:2 + n], out[2 + n:2 + 2 * n], out[-1]


def _ici_wait(started, after, name):
    send_sems, recv_sems, srcs, lands, _ = started
    n = len(srcs)

    def body(*refs):
        land_refs = refs[n:2 * n]
        send_sems, recv_sems = refs[2 * n], refs[2 * n + 1]
        x, y, cc = lax.axis_index("x"), lax.axis_index("y"), lax.axis_index("c")
        for i in range(n):
            three = land_refs[i].at[pl.ds(0, 3)]
            cp = pltpu.make_async_remote_copy(src_ref=three, dst_ref=three, send_sem=send_sems.at[i], recv_sem=recv_sems.at[i],
                                              device_id=(x, y, cc), device_id_type=MESH)
            cp.wait_send()
            cp.wait_recv()

    return pl.pallas_call(
        body, name=name, out_shape=tuple(_hbm(l) for l in lands), in_specs=[HBM] * (2 * n) + [SEMS, SEMS, ANY],
        out_specs=tuple([HBM] * n), input_output_aliases={n + i: i for i in range(n)},
        compiler_params=pltpu.CompilerParams(has_side_effects=EFFECT))(*srcs, *lands, send_sems, recv_sems, after)


def _gather_d2d(blocks, lands, name):
    n = len(blocks)

    def body(*refs):
        x_refs, land_refs = refs[:n], refs[2 * n:3 * n]
        send_sems, recv_sems, in_sems, out_sems = refs[3 * n:3 * n + 4]
        stage = refs[3 * n + 4:]
        x, y, cc = lax.axis_index("x"), lax.axis_index("y"), lax.axis_index("c")
        sibling = (x, y, 1 - cc)
        staged = [pltpu.make_async_copy(x_refs[i], stage[i], in_sems.at[i]) for i in range(n)]
        for cp in staged:
            cp.start()
        copies = []
        for i in range(n):
            slot = land_refs[i].at[4 * x + 2 * y + cc]
            copies.append(pltpu.make_async_remote_copy(src_ref=x_refs[i], dst_ref=slot, send_sem=send_sems.at[4 * i],
                                                       recv_sem=recv_sems.at[4 * i], device_id=sibling, device_id_type=MESH))
            for j, (px, py) in enumerate(_other_chips(x, y)):
                slot = land_refs[i].at[4 * px + 2 * py + cc]
                copies.append(pltpu.make_async_remote_copy(src_ref=slot, dst_ref=slot, send_sem=send_sems.at[4 * i + 1 + j],
                                                           recv_sem=recv_sems.at[4 * i + 1 + j], device_id=sibling, device_id_type=MESH))
        for cp in copies:
            cp.start()
        mine = []
        for i in range(n):
            staged[i].wait()
            mine.append(pltpu.make_async_copy(stage[i], land_refs[i].at[4 * x + 2 * y + cc], out_sems.at[i]))
            mine[i].start()
        for i in range(n):
            slot = land_refs[i].at[4 * x + 2 * y + (1 - cc)]
            pltpu.make_async_remote_copy(src_ref=slot, dst_ref=slot, send_sem=send_sems.at[4 * i], recv_sem=recv_sems.at[4 * i],
                                         device_id=sibling, device_id_type=MESH).wait_recv()
            for j, (px, py) in enumerate(_other_chips(x, y)):
                slot = land_refs[i].at[4 * px + 2 * py + (1 - cc)]
                pltpu.make_async_remote_copy(src_ref=slot, dst_ref=slot, send_sem=send_sems.at[4 * i + 1 + j],
                                             recv_sem=recv_sems.at[4 * i + 1 + j], device_id=sibling, device_id_type=MESH).wait_recv()
        for cp in copies:
            cp.wait_send()
        for cp in mine:
            cp.wait()

    return pl.pallas_call(
        body, out_shape=tuple(_sds(l.shape, l.dtype) for l in lands), in_specs=[ANY] * (2 * n), out_specs=(ANY,) * n,
        input_output_aliases={n + i: i for i in range(n)},
        scratch_shapes=[pltpu.SemaphoreType.DMA((4 * n,)), pltpu.SemaphoreType.DMA((4 * n,)), pltpu.SemaphoreType.DMA((n,)),
                        pltpu.SemaphoreType.DMA((n,))] + [pltpu.VMEM(b.shape, b.dtype) for b in blocks],
        name=name, compiler_params=_cp())(*blocks, *lands)


def _sum_own(parts, recvs, chip, name):
    n = len(parts)

    def body(c_ref, *refs):
        s = pl.program_id(0)
        for i in range(n):
            val = jnp.where(c_ref[0] == s, refs[i][...], refs[n + i][...]).astype(F32)
            _acc(refs[2 * n + i], val, s == 0)

    ins = [pl.BlockSpec((None,) + p.shape[1:], lambda s, cref: (s, 0, 0)) for p in parts]
    return pl.pallas_call(
        body, grid_spec=pltpu.PrefetchScalarGridSpec(
            num_scalar_prefetch=1, grid=(4,), in_specs=ins + ins,
            out_specs=tuple(pl.BlockSpec(p.shape[1:], lambda s, cref: (0, 0)) for p in parts)),
        out_shape=tuple(_sds(p.shape[1:], F32) for p in parts), name=name, compiler_params=_cp())(chip, *parts, *recvs)


BIG = (("w_in", True), ("w_gate", True), ("w_mem_kv", False), ("w_branch", True), ("w_out", False), ("w_ffn_in", True),
       ("w_ffn_out", False))

SMALL = ("norm_mix_g", "norm_mem_g", "ret_decay_fwd", "ret_decay_bwd", "ret_norm_g", "pool_w", "pool_scale", "na_q_norm_g",
         "na_k_norm_g", "na_rpb", "mem_q_norm_g", "mem_k_norm_g", "norm_ffn_g")
WEIGHTS = ("norm_mix_g", "norm_mem_g", "w_in", "w_gate", "ret_decay_fwd", "ret_decay_bwd", "ret_norm_g", "pool_w", "pool_scale",
           "na_q_norm_g", "na_k_norm_g", "na_rpb", "mem_q_norm_g", "mem_k_norm_g", "w_mem_kv", "w_branch", "w_out", "norm_ffn_g",
           "w_ffn_in", "w_ffn_out")


def _to_exchange(name, transposed, shard):
    if name == "w_branch":
        return jnp.swapaxes(shard, 1, 2).reshape(NH * (D // N_DEV), BW)
    return shard.T if transposed else shard


def _from_exchange(name, transposed, block):
    if name == "w_branch":
        return jnp.swapaxes(block.reshape(NH, D // N_DEV, BW), 1, 2)
    return block.T if transposed else block


def _whole_from_gathered(name, g):
    if name == "w_branch":
        return jnp.swapaxes(g.reshape(N_DEV, NH, D // N_DEV, BW), 0, 1).reshape(NH, D, BW)
    return g.reshape(N_DEV * g.shape[1], g.shape[2])


def _by_destination(name, g):
    if name == "w_branch":
        g = jnp.swapaxes(g.reshape(NH, N_DEV, D // N_DEV, BW), 0, 1).reshape(N_DEV * NH * (D // N_DEV), BW)
    return g.reshape(4, 2, g.shape[0] // N_DEV, g.shape[1])


SMALL_PAD = 1024


def _pack_small(vals, loss=None):
    parts = [vals[n] for n in SMALL] + [jnp.zeros((1,), F32) if loss is None else loss.reshape(1)]
    rows = []
    for p in parts:
        flat = p.reshape(-1)
        rows.append(jnp.pad(flat, (0, -flat.shape[0] % SMALL_PAD)).reshape(-1, 128))
    return jnp.concatenate(rows, axis=0)


def _unpack_small(packed, like):
    out, off = {}, 0
    for n in SMALL:
        sz = int(np.prod(like[n].shape))
        nrow = -(-sz // SMALL_PAD) * (SMALL_PAD // 128)
        out[n] = packed[off:off + nrow].reshape(-1)[:sz].reshape(like[n].shape)
        off += nrow
    return out, packed[off, 0]


def _na_constants():
    c = np.arange(GRID_W)
    win = np.clip(c - NA_COLS_WIN // 2, 0, GRID_W - NA_COLS_WIN)
    kc = np.arange(GRID_W)
    inside = (kc[None, :] >= win[:, None]) & (kc[None, :] < win[:, None] + NA_COLS_WIN)
    off = kc[None, :] - c[:, None] + NA_COLS_WIN - 1
    onehot = np.zeros((128, GRID_W, GRID_W), np.float32)
    for b in range(2 * NA_COLS_WIN - 1):
        onehot[b] = (off == b) & inside
    maskadd = np.where(inside, 0.0, NEG).astype(np.float32)
    return onehot.reshape(128, GRID_W * GRID_W), maskadd


def _na_bias_table(tab, maskadd):
    n_off = 2 * NA_ROWS_WIN - 1
    t4 = tab[:NH * n_off].reshape(NH, n_off, GRID_W, GRID_W) + maskadd[None, None]
    ball = jnp.stack([t4[:, a0:a0 + NA_ROWS_WIN] for a0 in range(NA_ROWS_WIN)], axis=1)
    return ball.transpose(1, 0, 3, 2, 4).reshape(NA_ROWS_WIN, NH * GRID_W, NA_KEYS)


def _rotary_tables(t):
    half = HD // 2
    inv = ROPE_THETA ** (-jnp.arange(half, dtype=F32) / half)
    ang = jnp.arange(t, dtype=F32)[:, None] * inv[None, :]
    cos, sin = jnp.cos(ang), jnp.sin(ang)
    return jnp.tile(jnp.concatenate([cos, cos], axis=-1), (1, NH)), jnp.tile(jnp.concatenate([-sin, sin], axis=-1), (1, NH))


def _block_diag(pw):
    out = jnp.zeros((BW, BW), pw.dtype)
    for g in range(NH):
        out = lax.dynamic_update_slice(out, pw[g], (g * HD, g * HD))
    return out


def _tile4(g):
    return jnp.tile(g.reshape(1, HD), (1, NH))


def _layer_fwd(x, mem, sw, lw, consts):
    cos2, sin2, onehot, maskadd = consts
    h = _rmsnorm_fwd(x, sw["norm_mix_g"].reshape(1, D), "norm_mix_fwd")
    proj = _mm(h, lw["w_in"], tb=True, name="mm_in")
    gp = _mm(h, lw["w_gate"], tb=True, name="mm_gate")
    g_naq, g_nak, g_mq = _tile4(sw["na_q_norm_g"]), _tile4(sw["na_k_norm_g"]), _tile4(sw["mem_q_norm_g"])
    rq, rk, rv, nq, nk, nv, mq = _prep_fwd(proj, cos2, sin2, g_naq, g_nak, g_mq)

    lgf, lgb = jax.nn.log_sigmoid(sw["ret_decay_fwd"]), jax.nn.log_sigmoid(sw["ret_decay_bwd"])
    g_ret = sw["ret_norm_g"].reshape(1, BW)
    o_ret, ret = _ret_fwd(rq, rk, rv, proj, lgf, lgb, g_ret)

    wbd = _block_diag(sw["pool_w"]).astype(BF16)
    p_scale = sw["pool_scale"].reshape(1, BW)
    pool = _pool_fwd(proj, wbd, p_scale)

    rpb_pad = jnp.pad(sw["na_rpb"].reshape(NH * 15, 31), ((0, 4), (0, 97)))
    ball = _na_bias_table(_rpb_expand(rpb_pad, onehot), maskadd)
    na = _na_fwd(nq, nk, nv, ball)

    memn = _rmsnorm_fwd(mem, sw["norm_mem_g"].reshape(1, D), "norm_mem_fwd")
    kv = _mm(memn, lw["w_mem_kv"], name="mm_memkv")
    g_mk = _tile4(sw["mem_k_norm_g"])
    mk, mv = _memkv_prep(kv, g_mk)
    mo = _mem_fwd(mq, mk, mv)

    br = (ret, pool, na, mo)
    merged = _merge_fwd(br, lw["w_branch"], gp)
    x1 = _mm(merged, lw["w_out"], add=x, name="mm_out")
    h2 = _rmsnorm_fwd(x1, sw["norm_ffn_g"].reshape(1, D), "norm_ffn_fwd")
    ag = _mm(h2, lw["w_ffn_in"], tb=True, name="mm_ffn_in")
    yff = _swiglu_fwd(ag)
    x2 = _mm(yff, lw["w_ffn_out"], add=x1, name="mm_ffn_out")
    saved = dict(x=x, h=h, proj=proj, gp=gp, rq=rq, rk=rk, rv=rv, nq=nq, nk=nk, nv=nv, mq=mq, o_ret=o_ret, ball=ball, memn=memn,
                 kv=kv, mk=mk, mv=mv, br=br, merged=merged, x1=x1, h2=h2, ag=ag, yff=yff, lgf=lgf, lgb=lgb, wbd=wbd)
    return x2, saved


def _layer_bwd(dx2, dx2b, mem, sw, lw, sv, consts, dep=None):
    cos2, sin2, onehot, maskadd = consts
    gb, gs = {}, {}
    dy = _mm(dx2b, lw["w_ffn_out"], tb=True, dep=dep, name="mm_ffn_out_dx")
    gb["w_ffn_out"] = _mm(sv["yff"], dx2b, ta=True, out_dtype=BF16, name="mm_ffn_out_dw")
    dag = _swiglu_bwd(sv["ag"], dy)
    dh2 = _mm(dag, lw["w_ffn_in"], name="mm_ffn_in_dx")
    gb["w_ffn_in"] = _mm(dag, sv["h2"], ta=True, out_dtype=BF16, name="mm_ffn_in_dw")
    dx1, dx1b, dg = _rmsnorm_bwd(dh2, sv["x1"], sw["norm_ffn_g"].reshape(1, D), dx2, "norm_ffn_bwd")
    gs["norm_ffn_g"] = dg.reshape(D)

    dmerged = _mm(dx1b, lw["w_out"], tb=True, name="mm_out_dx")
    gb["w_out"] = _mm(sv["merged"], dx1b, ta=True, out_dtype=BF16, name="mm_out_dw")
    dgp, dup = _merge_bwd(dmerged, sv["br"], lw["w_branch"], sv["gp"])
    dbr = _dbranch(dup, lw["w_branch"])
    gb["w_branch"] = _dwbranch(sv["br"], dup)

    g_ret = sw["ret_norm_g"].reshape(1, BW)
    do_ret, d_rg, dg_ret = _ret_post_bwd(dbr, sv["o_ret"], sv["proj"], g_ret)
    d_rq, d_rk, d_rv, dlg = _ret_bwd(do_ret, sv["rq"], sv["rk"], sv["rv"], sv["lgf"], sv["lgb"])
    gs["ret_norm_g"] = dg_ret.reshape(BW)
    _, vjp_f = jax.vjp(jax.nn.log_sigmoid, sw["ret_decay_fwd"])
    _, vjp_b = jax.vjp(jax.nn.log_sigmoid, sw["ret_decay_bwd"])
    gs["ret_decay_fwd"] = vjp_f(dlg[0:NH, 0])[0]
    gs["ret_decay_bwd"] = vjp_b(dlg[NH:2 * NH, 0])[0]

    p_scale = sw["pool_scale"].reshape(1, BW)
    d_pv, dwbd, dscale = _pool_bwd(dbr, sv["proj"], sv["wbd"], p_scale)
    gs["pool_w"] = jnp.stack([dwbd[g * HD:(g + 1) * HD, g * HD:(g + 1) * HD] for g in range(NH)])
    gs["pool_scale"] = dscale.reshape(BW)

    d_nq, d_nk, d_nv, dball = _na_bwd(dbr, sv["nq"], sv["nk"], sv["nv"], sv["ball"])
    _, vjp_tab = jax.vjp(lambda tab: _na_bias_table(tab, maskadd), jnp.zeros((64, GRID_W * GRID_W), F32))
    drpb = _rpb_reduce(vjp_tab(dball)[0], onehot)
    gs["na_rpb"] = drpb[:NH * 15, :31].reshape(NH, 15, 31)

    d_mq, d_mk, d_mv = _mem_bwd(dbr, sv["mq"], sv["mk"], sv["mv"])
    g_mk = _tile4(sw["mem_k_norm_g"])
    dkv, dg_mk = _memkv_bwd(sv["kv"], d_mk, d_mv, g_mk)
    gs["mem_k_norm_g"] = dg_mk.reshape(NH, HD).sum(0)
    gb["w_mem_kv"] = _mm(sv["memn"], dkv, ta=True, out_dtype=BF16, name="mm_memkv_dw")
    dmemn = _mm(dkv, lw["w_mem_kv"], tb=True, name="mm_memkv_dx")
    _, _, dg_mem = _rmsnorm_bwd(dmemn, mem, sw["norm_mem_g"].reshape(1, D), jnp.zeros_like(mem), "norm_mem_bwd")
    gs["norm_mem_g"] = dg_mem.reshape(D)

    g_naq, g_nak, g_mq = _tile4(sw["na_q_norm_g"]), _tile4(sw["na_k_norm_g"]), _tile4(sw["mem_q_norm_g"])
    dproj, dg_naq, dg_nak, dg_mq = _prep_bwd(sv["proj"], cos2, sin2, g_naq, g_nak, g_mq, d_rq, d_rk, d_rv, d_rg, d_pv, d_nq, d_nk,
                                             d_nv, d_mq)
    gs["na_q_norm_g"] = dg_naq.reshape(NH, HD).sum(0)
    gs["na_k_norm_g"] = dg_nak.reshape(NH, HD).sum(0)
    gs["mem_q_norm_g"] = dg_mq.reshape(NH, HD).sum(0)

    dh = _mm(dproj, lw["w_in"], name="mm_in_dx")
    dh = _mm(dgp, lw["w_gate"], add=dh, name="mm_gate_dx")
    gb["w_in"] = _mm(dproj, sv["h"], ta=True, out_dtype=BF16, name="mm_in_dw")
    gb["w_gate"] = _mm(dgp, sv["h"], ta=True, out_dtype=BF16, name="mm_gate_dw")
    dx, dxb, dg = _rmsnorm_bwd(dh, sv["x"], sw["norm_mix_g"].reshape(1, D), dx1, "norm_mix_bwd")
    gs["norm_mix_g"] = dg.reshape(D)
    return dx, dxb, gb, gs


def _local_step(x, mem, target, small, get_layer, on_grads):
    t = x.shape[0]
    cos2, sin2 = _rotary_tables(t)
    onehot, maskadd = _na_constants()
    consts = (cos2, sin2, jnp.asarray(onehot), jnp.asarray(maskadd))
    saved, weights, cur = [], [], x
    for l in range(DEPTH):
        sw = {n: small[n][l] for n in SMALL}
        weights.append(get_layer(l, cur))
        cur, sv = _layer_fwd(cur, mem, sw, weights[l], consts)
        saved.append(sv)
    dy, dyb, loss_tile = _loss_head(cur, target)
    small_g = {n: [None] * DEPTH for n in SMALL}
    dep = None
    for l in reversed(range(DEPTH)):
        sw = {n: small[n][l] for n in SMALL}
        dy, dyb, gb, gs = _layer_bwd(dy, dyb, mem, sw, weights[l], saved[l], consts, dep)
        dep = on_grads(l, gb, dy)
        for n in SMALL:
            small_g[n][l] = gs[n]
    return loss_tile[0, 0], dy, {n: jnp.stack(v) for n, v in small_g.items()}


def _flat2d(a):
    return a.reshape(-1, a.shape[-1])


def kernel(x, mem, norm_mix_g, norm_mem_g, w_in, w_gate, ret_decay_fwd, ret_decay_bwd, ret_norm_g, pool_w, pool_scale, na_q_norm_g, na_k_norm_g, na_rpb, mem_q_norm_g, mem_k_norm_g, w_mem_kv, w_branch, w_out, norm_ffn_g, w_ffn_in, w_ffn_out, loss_target, m_norm_mix_g, m_norm_mem_g, m_w_in, m_w_gate, m_ret_decay_fwd, m_ret_decay_bwd, m_ret_norm_g, m_pool_w, m_pool_scale, m_na_q_norm_g, m_na_k_norm_g, m_na_rpb, m_mem_q_norm_g, m_mem_k_norm_g, m_w_mem_kv, m_w_branch, m_w_out, m_norm_ffn_g, m_w_ffn_in, m_w_ffn_out, v_norm_mix_g, v_norm_mem_g, v_w_in, v_w_gate, v_ret_decay_fwd, v_ret_decay_bwd, v_ret_norm_g, v_pool_w, v_pool_scale, v_na_q_norm_g, v_na_k_norm_g, v_na_rpb, v_mem_q_norm_g, v_mem_k_norm_g, v_w_mem_kv, v_w_branch, v_w_out, v_norm_ffn_g, v_w_ffn_in, v_w_ffn_out):
    w = dict(norm_mix_g=norm_mix_g, norm_mem_g=norm_mem_g, w_in=w_in, w_gate=w_gate, ret_decay_fwd=ret_decay_fwd,
             ret_decay_bwd=ret_decay_bwd, ret_norm_g=ret_norm_g, pool_w=pool_w, pool_scale=pool_scale, na_q_norm_g=na_q_norm_g,
             na_k_norm_g=na_k_norm_g, na_rpb=na_rpb, mem_q_norm_g=mem_q_norm_g, mem_k_norm_g=mem_k_norm_g, w_mem_kv=w_mem_kv,
             w_branch=w_branch, w_out=w_out, norm_ffn_g=norm_ffn_g, w_ffn_in=w_ffn_in, w_ffn_out=w_ffn_out)
    m = dict(norm_mix_g=m_norm_mix_g, norm_mem_g=m_norm_mem_g, w_in=m_w_in, w_gate=m_w_gate, ret_decay_fwd=m_ret_decay_fwd,
             ret_decay_bwd=m_ret_decay_bwd, ret_norm_g=m_ret_norm_g, pool_w=m_pool_w, pool_scale=m_pool_scale, na_q_norm_g=m_na_q_norm_g,
             na_k_norm_g=m_na_k_norm_g, na_rpb=m_na_rpb, mem_q_norm_g=m_mem_q_norm_g, mem_k_norm_g=m_mem_k_norm_g, w_mem_kv=m_w_mem_kv,
             w_branch=m_w_branch, w_out=m_w_out, norm_ffn_g=m_norm_ffn_g, w_ffn_in=m_w_ffn_in, w_ffn_out=m_w_ffn_out)
    v = dict(norm_mix_g=v_norm_mix_g, norm_mem_g=v_norm_mem_g, w_in=v_w_in, w_gate=v_w_gate, ret_decay_fwd=v_ret_decay_fwd,
             ret_decay_bwd=v_ret_decay_bwd, ret_norm_g=v_ret_norm_g, pool_w=v_pool_w, pool_scale=v_pool_scale, na_q_norm_g=v_na_q_norm_g,
             na_k_norm_g=v_na_k_norm_g, na_rpb=v_na_rpb, mem_q_norm_g=v_mem_q_norm_g, mem_k_norm_g=v_mem_k_norm_g, w_mem_kv=v_w_mem_kv,
             w_branch=v_w_branch, w_out=v_w_out, norm_ffn_g=v_norm_ffn_g, w_ffn_in=v_w_ffn_in, w_ffn_out=v_w_ffn_out)
    assert x.shape == (1, 2048, D) and mem.shape == (1, N_MEM, D) and w_in.shape == (DEPTH, D, 9 * BW // N_DEV)

    started = []
    for l in range(DEPTH):
        blocks = [_to_exchange(name, tr, w[name][l]).astype(BF16) for name, tr in BIG]
        lands = [lax.empty((N_DEV,) + b.shape, BF16) for b in blocks]
        started.append(_ici_start(blocks, lands, False, "gather_ici_start_%d" % l))
    all_started = started[0][4] + started[1][4] + started[2][4] + started[3][4]

    def get_layer(l, after):
        lands = _ici_wait(started[l], all_started if l == 0 else after, "gather_ici_wait_%d" % l)
        whole = _gather_d2d(started[l][2], lands, "gather_d2d")
        return {name: _whole_from_gathered(name, g) for (name, _), g in zip(BIG, whole)}

    cidx = lax.axis_index("c").astype(jnp.int32).reshape(1)
    chip = (2 * lax.axis_index("x") + lax.axis_index("y")).astype(jnp.int32).reshape(1)
    in_flight, g_layers = [], [None] * DEPTH

    def finish(l, st, after):
        recv = _ici_wait(st, after, "rs_ici_wait_%d" % l)
        sums = _sum_own(st[2], recv, chip, "rs_chip_sum")
        g_layers[l] = {name: _from_exchange(name, tr, s) if name == "w_branch" else s for (name, tr), s in zip(BIG, sums)}

    def on_grads(l, gb, after):
        send = [_by_destination(name, gb[name]) for name, _ in BIG]
        from_core = _rs_core_swap(send, "rs_core_swap")
        chip_part = _pair_sum(send, from_core, cidx)
        st = _ici_start(chip_part, [lax.empty(p.shape, BF16) for p in chip_part], True, "rs_ici_start_%d" % l)
        if in_flight:
            finish(*in_flight.pop(), after)
        in_flight.append((l, st))
        if l == 0:
            finish(*in_flight.pop(), st[4])
        return st[4]

    loss_local, dx, small_g = _local_step(x[0], mem[0], loss_target[0], {n: w[n] for n in SMALL}, get_layer, on_grads)

    small_all, = _all_gather([_pack_small(small_g, loss_local)], "gather_small")
    packed_g = _sum_slots(small_all, "small_sum")
    small_sum, loss = _unpack_small(packed_g, {n: w[n] for n in SMALL})

    grads, delta, new_m, new_v = {}, {}, {}, {}
    for name, tr in BIG:
        flip = (lambda a: jnp.swapaxes(a, 1, 2)) if (tr and name != "w_branch") else (lambda a: a)
        rows3 = lambda a: a.reshape(DEPTH, -1, a.shape[-1])
        gs = [g_layers[l][name].reshape(-1, g_layers[l][name].shape[-1]) for l in range(DEPTH)]
        wx = flip(w[name])
        outs = _adamw_layers(rows3(wx), gs, rows3(flip(m[name])), rows3(flip(v[name])), "adamw_" + name)
        delta[name], new_m[name], new_v[name], grads[name] = (flip(a.reshape(wx.shape)) for a in outs)
    d_, m_, v_ = _adamw(_pack_small({n: w[n] for n in SMALL}), packed_g, _pack_small({n: m[n] for n in SMALL}),
                        _pack_small({n: v[n] for n in SMALL}), "adamw_small")
    like = {n: w[n] for n in SMALL}
    ds, _ = _unpack_small(d_, like)
    ms, _ = _unpack_small(m_, like)
    vs, _ = _unpack_small(v_, like)
    for n in SMALL:
        grads[n], delta[n], new_m[n], new_v[n] = small_sum[n], ds[n], ms[n], vs[n]

    return (loss, dx[None], *[grads[n] for n in WEIGHTS], *[delta[n] for n in WEIGHTS], *[new_m[n] for n in WEIGHTS],
            *[new_v[n] for n in WEIGHTS])
```

```python
import functools

import numpy as np
import jax
import jax.numpy as jnp
from jax import lax
from jax.experimental import pallas as pl
from jax.experimental.pallas import tpu as pltpu

F32 = jnp.float32
BF16 = jnp.bfloat16
MXU = jnp.bfloat16
HI = lax.Precision.HIGHEST

DEPTH = 4
D = 1024
BW = 256
HD = 64
NH = 4
GRID_W = 64
NA_ROWS_WIN = 8
NA_COLS_WIN = 16
N_MEM = 256
FF = 2816
EPS = 1e-6
NEG = -1e30
ROPE_THETA = 10000.0
POOL_HALF_MAX = 8

ADAM_LR, ADAM_B1, ADAM_B2, ADAM_EPS, ADAM_WD, ADAM_STEP = 0.001, 0.9, 0.999, 1e-08, 0.01, 10

N_DEV = 8
VMEM_LIMIT = 56 * 1024 * 1024

RQ, RK, RV, RG, PV, NQ, NK, NV, MQ = range(9)

MESH = pl.DeviceIdType.MESH
ANY = pl.BlockSpec(memory_space=pl.ANY)
SMEM = pl.BlockSpec(memory_space=pltpu.SMEM)


def _cp(**kw):
    return pltpu.CompilerParams(vmem_limit_bytes=VMEM_LIMIT, **kw)


def _tile(n, cap):
    if n <= cap:
        return n
    best = None
    for t in range(128, cap + 1, 128):
        if n % t == 0:
            best = t
    assert best is not None, (n, cap)
    return best


def _sds(shape, dtype):
    return jax.ShapeDtypeStruct(shape, dtype)


def _lane_head(shape):
    return lax.shift_right_logical(lax.broadcasted_iota(jnp.int32, shape, len(shape) - 1), 6)


def _group_mean(z):
    i = lax.shift_right_logical(lax.broadcasted_iota(jnp.int32, (BW, BW), 0), 6)
    j = lax.shift_right_logical(lax.broadcasted_iota(jnp.int32, (BW, BW), 1), 6)
    g = jnp.where(i == j, 1.0 / HD, 0.0).astype(F32)
    return jnp.dot(z, g, precision=HI, preferred_element_type=F32)


def _gnorm(t, g):
    r = lax.rsqrt(_group_mean(t * t) + EPS)
    return t * r * g


def _gnorm_bwd(dy, t, g):
    r = lax.rsqrt(_group_mean(t * t) + EPS)
    th = t * r
    dth = dy * g
    dt = r * (dth - th * _group_mean(dth * th))
    return dt, dy * th


def _swap_halves(t):
    lane = lax.broadcasted_iota(jnp.int32, t.shape, 1)
    return jnp.where((lane & 63) < 32, pltpu.roll(t, BW - 32, 1), pltpu.roll(t, 32, 1))


def _sigmoid(x):
    return 1.0 / (1.0 + jnp.exp(-x))


def _dot(a, b, ta=False, tb=False):
    return lax.dot_general(a.astype(MXU), b.astype(MXU), (((0 if ta else 1,), (1 if tb else 0,)), ((), ())),
                           preferred_element_type=F32)


def _stack_heads(t):
    head = _lane_head(t.shape)
    return jnp.concatenate([jnp.where(head == h, t, jnp.zeros_like(t)) for h in range(NH)], axis=0)


def _unstack_heads(t, rows):
    head = _lane_head((rows, BW))
    out = jnp.zeros((rows, BW), F32)
    for h in range(NH):
        out = out + jnp.where(head == h, t[h * rows:(h + 1) * rows], 0.0)
    return out


def _softmax_rows(s):
    m = jnp.max(s, axis=-1, keepdims=True)
    e = jnp.exp(s - m)
    return e / jnp.sum(e, axis=-1, keepdims=True)


def _acc(ref, val, first):
    @pl.when(first)
    def _():
        ref[...] = val

    @pl.when(jnp.logical_not(first))
    def _():
        ref[...] += val


def _mm(a, b, *, ta=False, tb=False, out_dtype=F32, add=None, dep=None, name):
    m, k = (a.shape[1], a.shape[0]) if ta else a.shape
    n = b.shape[0] if tb else b.shape[1]
    tm, tn = _tile(m, 1408), _tile(n, 512)

    def body(*refs):
        if add is None:
            a_ref, b_ref, o_ref = refs[:2] + refs[-1:]
            r = _dot(a_ref[...], b_ref[...], ta, tb)
        else:
            a_ref, b_ref, c_ref, o_ref = refs[:3] + refs[-1:]
            r = _dot(a_ref[...], b_ref[...], ta, tb) + c_ref[...]
        o_ref[...] = r.astype(out_dtype)

    a_spec = pl.BlockSpec((k, tm), lambda i, j: (0, i)) if ta else pl.BlockSpec((tm, k), lambda i, j: (i, 0))
    b_spec = pl.BlockSpec((tn, k), lambda i, j: (j, 0)) if tb else pl.BlockSpec((k, tn), lambda i, j: (0, j))
    o_spec = pl.BlockSpec((tm, tn), lambda i, j: (i, j))
    ins, args = [a_spec, b_spec], [a, b]
    if add is not None:
        ins.append(o_spec)
        args.append(add)
    if dep is not None:
        ins.append(pl.BlockSpec((8, 128), lambda i, j: (0, 0)))
        args.append(dep)
    return pl.pallas_call(
        body, grid=(m // tm, n // tn), in_specs=ins, out_specs=o_spec, out_shape=_sds((m, n), out_dtype), name=name,
        compiler_params=_cp(dimension_semantics=("parallel", "parallel")))(*args)


def _rmsnorm_fwd(x, g, name):
    t, d = x.shape
    tm = _tile(t, 256)

    def body(x_ref, g_ref, o_ref):
        xv = x_ref[...]
        r = lax.rsqrt(jnp.mean(xv * xv, axis=-1, keepdims=True) + EPS)
        o_ref[...] = (xv * r * g_ref[...]).astype(o_ref.dtype)

    return pl.pallas_call(
        body, grid=(t // tm,), in_specs=[pl.BlockSpec((tm, d), lambda i: (i, 0)), pl.BlockSpec((1, d), lambda i: (0, 0))],
        out_specs=pl.BlockSpec((tm, d), lambda i: (i, 0)), out_shape=_sds((t, d), BF16), name=name, compiler_params=_cp())(x, g)


def _rmsnorm_bwd(dh, x, g, res, name):
    t, d = x.shape
    tm = _tile(t, 256)

    def body(dh_ref, x_ref, g_ref, res_ref, dx_ref, dxb_ref, dg_ref):
        xv = x_ref[...]
        dhv = dh_ref[...]
        r = lax.rsqrt(jnp.mean(xv * xv, axis=-1, keepdims=True) + EPS)
        xh = xv * r
        dxh = dhv * g_ref[...]
        dx = res_ref[...] + r * (dxh - xh * jnp.mean(dxh * xh, axis=-1, keepdims=True))
        dx_ref[...] = dx
        dxb_ref[...] = dx.astype(BF16)
        _acc(dg_ref, jnp.sum(dhv * xh, axis=0, keepdims=True), pl.program_id(0) == 0)

    row = pl.BlockSpec((tm, d), lambda i: (i, 0))
    vec = pl.BlockSpec((1, d), lambda i: (0, 0))
    return pl.pallas_call(
        body, grid=(t // tm,), in_specs=[row, row, vec, row], out_specs=(row, row, vec),
        out_shape=(_sds((t, d), F32), _sds((t, d), BF16), _sds((1, d), F32)), name=name, compiler_params=_cp())(dh, x, g, res)


def _prep_fwd(proj, cos2, sin2, g_naq, g_nak, g_mq):
    t = proj.shape[0]
    tm = 256

    def body(p_ref, cos_ref, sin_ref, gq_ref, gk_ref, gm_ref, rq_ref, rk_ref, rv_ref, nq_ref, nk_ref, nv_ref, mq_ref):
        def col(c):
            return p_ref[:, c * BW:(c + 1) * BW]

        cosv, sinv = cos_ref[...], sin_ref[...]

        def rot(tv):
            return tv * cosv + _swap_halves(tv) * sinv

        rq_ref[...] = (rot(col(RQ)) * (HD ** -0.5)).astype(BF16)
        rk_ref[...] = rot(col(RK)).astype(BF16)
        rv_ref[...] = col(RV).astype(BF16)
        nq_ref[...] = _gnorm(col(NQ), gq_ref[...]).astype(BF16)
        nk_ref[...] = _gnorm(col(NK), gk_ref[...]).astype(BF16)
        nv_ref[...] = col(NV).astype(BF16)
        mq_ref[...] = _gnorm(col(MQ), gm_ref[...]).astype(BF16)

    blk = pl.BlockSpec((tm, BW), lambda i: (i, 0))
    vec = pl.BlockSpec((1, BW), lambda i: (0, 0))
    return pl.pallas_call(
        body, grid=(t // tm,), in_specs=[pl.BlockSpec((tm, 9 * BW), lambda i: (i, 0)), blk, blk, vec, vec, vec],
        out_specs=tuple(blk for _ in range(7)), out_shape=tuple(_sds((t, BW), BF16) for _ in range(7)),
        name="prep_fwd", compiler_params=_cp())(proj, cos2, sin2, g_naq, g_nak, g_mq)


def _prep_bwd(proj, cos2, sin2, g_naq, g_nak, g_mq, d_rq, d_rk, d_rv, d_rg, d_pv, d_nq, d_nk, d_nv, d_mq):
    t = proj.shape[0]
    tm = 256

    def body(p_ref, cos_ref, sin_ref, gq_ref, gk_ref, gm_ref, drq_ref, drk_ref, drv_ref, drg_ref, dpv_ref, dnq_ref, dnk_ref,
             dnv_ref, dmq_ref, o_ref, dgq_ref, dgk_ref, dgm_ref):
        first = pl.program_id(0) == 0

        def col(c):
            return p_ref[:, c * BW:(c + 1) * BW]

        def put(c, v):
            o_ref[:, c * BW:(c + 1) * BW] = v.astype(BF16)

        cosv, sinv = cos_ref[...], sin_ref[...]

        def rot_t(dv):
            return dv * cosv + _swap_halves(dv * sinv)

        put(RQ, rot_t(drq_ref[...] * (HD ** -0.5)))
        put(RK, rot_t(drk_ref[...]))
        put(RV, drv_ref[...])
        put(RG, drg_ref[...])
        put(PV, dpv_ref[...])
        dq, gq = _gnorm_bwd(dnq_ref[...], col(NQ), gq_ref[...])
        put(NQ, dq)
        _acc(dgq_ref, jnp.sum(gq, axis=0, keepdims=True), first)
        dk, gk = _gnorm_bwd(dnk_ref[...], col(NK), gk_ref[...])
        put(NK, dk)
        _acc(dgk_ref, jnp.sum(gk, axis=0, keepdims=True), first)
        put(NV, dnv_ref[...])
        dm, gm = _gnorm_bwd(dmq_ref[...], col(MQ), gm_ref[...])
        put(MQ, dm)
        _acc(dgm_ref, jnp.sum(gm, axis=0, keepdims=True), first)

    blk = pl.BlockSpec((tm, BW), lambda i: (i, 0))
    vec = pl.BlockSpec((1, BW), lambda i: (0, 0))
    wide = pl.BlockSpec((tm, 9 * BW), lambda i: (i, 0))
    return pl.pallas_call(
        body, grid=(t // tm,), in_specs=[wide, blk, blk, vec, vec, vec] + [blk] * 9, out_specs=(wide, vec, vec, vec),
        out_shape=(_sds((t, 9 * BW), BF16), _sds((1, BW), F32), _sds((1, BW), F32), _sds((1, BW), F32)),
        name="prep_bwd", compiler_params=_cp())(proj, cos2, sin2, g_naq, g_nak, g_mq, d_rq, d_rk, d_rv, d_rg, d_pv, d_nq, d_nk,
                                                d_nv, d_mq)


RET_B = 256


def _ret_consts(lgf_ref, lgb_ref):
    bsz = RET_B
    head = _lane_head((1, BW))
    lf, lb = jnp.zeros((1, BW), F32), jnp.zeros((1, BW), F32)
    for h in range(NH):
        lf = lf + jnp.where(head == h, lgf_ref[h], 0.0)
        lb = lb + jnp.where(head == h, lgb_ref[h], 0.0)
    pos = lax.broadcasted_iota(jnp.int32, (bsz, BW), 0).astype(F32)
    up, down = pos + 1.0, (bsz - 1.0) - pos
    c = dict(up=up, down=down, kf=jnp.exp(down * lf), kb=jnp.exp(up * lb), qf=jnp.exp(up * lf), qb=jnp.exp(down * lb),
             cf=jnp.exp(bsz * lf), cb=jnp.exp(bsz * lb))
    diff = (lax.broadcasted_iota(jnp.int32, (NH * bsz, 1), 0) & (bsz - 1)) - lax.broadcasted_iota(jnp.int32, (1, bsz), 1)
    c["causal"] = diff >= 0
    c["dist"] = jnp.abs(diff).astype(F32)
    lgf = jnp.concatenate([jnp.full((bsz, 1), lgf_ref[h], F32) for h in range(NH)], axis=0)
    lgb = jnp.concatenate([jnp.full((bsz, 1), lgb_ref[h], F32) for h in range(NH)], axis=0)
    c["dm"] = jnp.exp(c["dist"] * jnp.where(c["causal"], lgf, lgb))
    c["bd"] = _lane_head((BW, BW)) == lax.shift_right_logical(lax.broadcasted_iota(jnp.int32, (BW, BW), 0), 6)
    return c


def _ret_states(k_ref, v_ref, st_ref, c, nb):
    bsz = RET_B

    def summary(b, decay):
        kb = k_ref[b * bsz:(b + 1) * bsz, :].astype(F32)
        return jnp.where(c["bd"], _dot(kb * decay, v_ref[b * bsz:(b + 1) * bsz, :], ta=True), 0.0)

    f = jnp.zeros((BW, BW), F32)
    for b in range(nb):
        st_ref[b] = f
        if b < nb - 1:
            f = c["cf"] * f + summary(b, c["kf"])
    g = jnp.zeros((BW, BW), F32)
    for b in reversed(range(nb)):
        st_ref[nb + b] = g
        if b > 0:
            g = c["cb"] * g + summary(b, c["kb"])


def _ret_fwd(q, k, v, proj, lgf, lgb, g_ret):
    t = q.shape[0]
    bsz, nb = RET_B, t // RET_B

    def body(lgf_ref, lgb_ref, q_ref, k_ref, v_ref, rg_ref, g_ref, o_ref, ret_ref, st_ref):
        c = _ret_consts(lgf_ref, lgb_ref)
        _ret_states(k_ref, v_ref, st_ref, c, nb)
        for b in range(nb):
            blk = slice(b * bsz, (b + 1) * bsz)
            qb, kb, vb = q_ref[blk, :], k_ref[blk, :], v_ref[blk, :]
            s = _dot(_stack_heads(qb), kb, tb=True)
            o = _unstack_heads(_dot(s * c["dm"], vb), bsz)
            q32 = qb.astype(F32)
            o = o + _dot(q32 * c["qf"], st_ref[b]) + _dot(q32 * c["qb"], st_ref[nb + b])
            o_ref[blk, :] = o
            rg = rg_ref[blk, :]
            ret_ref[blk, :] = (_gnorm(o, g_ref[...]) * (rg * _sigmoid(rg))).astype(BF16)

    whole = pl.BlockSpec((t, BW), lambda i: (0, 0))
    return pl.pallas_call(
        body, grid=(1,),
        in_specs=[SMEM, SMEM, whole, whole, whole, pl.BlockSpec((t, BW), lambda i: (0, RG)), pl.BlockSpec((1, BW), lambda i: (0, 0))],
        out_specs=(whole, whole), out_shape=(_sds((t, BW), F32), _sds((t, BW), BF16)),
        scratch_shapes=[pltpu.VMEM((2 * nb, BW, BW), F32)], name="ret_fwd", compiler_params=_cp())(lgf, lgb, q, k, v, proj, g_ret)


def _ret_post_bwd(dbr, o_ret, proj, g_ret):
    t = o_ret.shape[0]
    tm = 256

    def body(d_ref, o_ref, rg_ref, g_ref, do_ref, drg_ref, dg_ref):
        dret, o, rg, g = d_ref[...], o_ref[...], rg_ref[...], g_ref[...]
        sg = _sigmoid(rg)
        do, dgain = _gnorm_bwd(dret * (rg * sg), o, g)
        do_ref[...] = do.astype(BF16)
        drg_ref[...] = dret * _gnorm(o, g) * (sg * (1.0 + rg * (1.0 - sg)))
        _acc(dg_ref, jnp.sum(dgain, axis=0, keepdims=True), pl.program_id(0) == 0)

    blk = pl.BlockSpec((tm, BW), lambda i: (i, 0))
    vec = pl.BlockSpec((1, BW), lambda i: (0, 0))
    return pl.pallas_call(
        body, grid=(t // tm,), in_specs=[blk, blk, pl.BlockSpec((tm, BW), lambda i: (i, RG)), vec], out_specs=(blk, blk, vec),
        out_shape=(_sds((t, BW), BF16), _sds((t, BW), F32), _sds((1, BW), F32)), name="ret_post_bwd",
        compiler_params=_cp())(dbr, o_ret, proj, g_ret)


def _ret_bwd(do, q, k, v, lgf, lgb):
    t = q.shape[0]
    bsz, nb = RET_B, t // RET_B

    def body(lgf_ref, lgb_ref, d_ref, q_ref, k_ref, v_ref, dq_ref, dk_ref, dv_ref, dlg_ref, st_ref, sd_ref):
        c = _ret_consts(lgf_ref, lgb_ref)
        _ret_states(k_ref, v_ref, st_ref, c, nb)
        lane_f, lane_b = jnp.zeros((1, BW), F32), jnp.zeros((1, BW), F32)
        row_f, row_b = jnp.zeros((NH * bsz, 1), F32), jnp.zeros((NH * bsz, 1), F32)

        def rows(x):
            return jnp.sum(x, axis=0, keepdims=True)

        for b in range(nb):
            blk = slice(b * bsz, (b + 1) * bsz)
            qb, kb, vb, dob = q_ref[blk, :], k_ref[blk, :], v_ref[blk, :], d_ref[blk, :]
            q32 = qb.astype(F32)
            qs, dos = _stack_heads(qb), _stack_heads(dob)
            s = _dot(qs, kb, tb=True)
            da = _dot(dos, vb, tb=True)
            dv_ref[blk, :] = _dot(s * c["dm"], dos, ta=True)
            ds = da * c["dm"]
            w = ds * s * c["dist"]
            row_f = row_f + jnp.sum(jnp.where(c["causal"], w, 0.0), axis=1, keepdims=True)
            row_b = row_b + jnp.sum(jnp.where(c["causal"], 0.0, w), axis=1, keepdims=True)
            dsb = ds.astype(MXU)
            dk_ref[blk, :] = _dot(dsb, qs, ta=True)
            dq_f = _dot(dob, st_ref[b], tb=True) * c["qf"]
            dq_b = _dot(dob, st_ref[nb + b], tb=True) * c["qb"]
            lane_f = lane_f + rows(c["up"] * dq_f * q32)
            lane_b = lane_b + rows(c["down"] * dq_b * q32)
            dq_ref[blk, :] = _unstack_heads(_dot(dsb, kb), bsz) + dq_f + dq_b
            sd_ref[b] = jnp.where(c["bd"], _dot(q32 * c["qf"], dob, ta=True), 0.0)
            sd_ref[nb + b] = jnp.where(c["bd"], _dot(q32 * c["qb"], dob, ta=True), 0.0)

        def through_state(b, grad, decay, weight, lane):
            blk = slice(b * bsz, (b + 1) * bsz)
            k32 = k_ref[blk, :].astype(F32)
            dk = _dot(v_ref[blk, :], grad, tb=True) * decay
            dk_ref[blk, :] += dk
            dv_ref[blk, :] += _dot(k32 * decay, grad)
            return lane + rows(weight * dk * k32)

        phi = jnp.zeros((BW, BW), F32)
        for b in reversed(range(nb)):
            if b < nb - 1:
                lane_f = through_state(b, phi, c["kf"], c["down"], lane_f)
                lane_f = lane_f + bsz * rows(c["cf"] * st_ref[b] * phi)
            phi = sd_ref[b] + c["cf"] * phi
        gam = jnp.zeros((BW, BW), F32)
        for b in range(nb):
            if b > 0:
                lane_b = through_state(b, gam, c["kb"], c["up"], lane_b)
                lane_b = lane_b + bsz * rows(c["cb"] * st_ref[nb + b] * gam)
            gam = sd_ref[nb + b] + c["cb"] * gam

        head = _lane_head((1, BW))
        for h in range(NH):
            tot_f = jnp.sum(row_f[h * bsz:(h + 1) * bsz, :]) + jnp.sum(jnp.where(head == h, lane_f, 0.0))
            tot_b = jnp.sum(row_b[h * bsz:(h + 1) * bsz, :]) + jnp.sum(jnp.where(head == h, lane_b, 0.0))
            dlg_ref[h:h + 1, :] = jnp.full((1, 128), tot_f, F32)
            dlg_ref[NH + h:NH + h + 1, :] = jnp.full((1, 128), tot_b, F32)

    whole = pl.BlockSpec((t, BW), lambda i: (0, 0))
    return pl.pallas_call(
        body, grid=(1,), in_specs=[SMEM, SMEM, whole, whole, whole, whole],
        out_specs=(whole, whole, whole, pl.BlockSpec((2 * NH, 128), lambda i: (0, 0))),
        out_shape=(_sds((t, BW), F32), _sds((t, BW), F32), _sds((t, BW), F32), _sds((2 * NH, 128), F32)),
        scratch_shapes=[pltpu.VMEM((2 * nb, BW, BW), F32), pltpu.VMEM((2 * nb, BW, BW), F32)], name="ret_bwd",
        compiler_params=_cp())(lgf, lgb, do, q, k, v)


def _pool_windows(t):
    row = lax.broadcasted_iota(jnp.int32, (t, BW), 0)
    half = lax.shift_left(jnp.ones((t, BW), jnp.int32), _lane_head((t, BW)))
    cnt = (jnp.minimum(row + half, t) - jnp.maximum(row - half, 0)).astype(F32)
    return row, half, cnt


def _pool_window_sum(v, row, half, t, transpose):
    out = jnp.zeros_like(v)
    for j in range(-POOL_HALF_MAX, POOL_HALF_MAX):
        src = row - j if transpose else row + j
        ok = (src >= 0) & (src < t) & (j >= -half) & (j < half)
        out = out + jnp.where(ok, pltpu.roll(v, (j if transpose else -j) % t, 0), 0.0)
    return out


def _pool_fwd(proj, wbd, scale):
    t = proj.shape[0]

    def body(v_ref, w_ref, s_ref, o_ref):
        v = v_ref[...]
        row, half, cnt = _pool_windows(t)
        pooled = _pool_window_sum(v, row, half, t, False) / cnt - v
        o_ref[...] = (_dot(pooled, w_ref[...]) * s_ref[...]).astype(BF16)

    return pl.pallas_call(
        body, grid=(1,),
        in_specs=[pl.BlockSpec((t, BW), lambda i: (0, PV)), pl.BlockSpec((BW, BW), lambda i: (0, 0)), pl.BlockSpec((1, BW), lambda i: (0, 0))],
        out_specs=pl.BlockSpec((t, BW), lambda i: (0, 0)), out_shape=_sds((t, BW), BF16), name="pool_fwd",
        compiler_params=_cp())(proj, wbd, scale)


def _pool_bwd(dbr, proj, wbd, scale):
    t = proj.shape[0]

    def body(d_ref, v_ref, w_ref, s_ref, dv_ref, dw_ref, ds_ref):
        v, dout = v_ref[...], d_ref[...]
        row, half, cnt = _pool_windows(t)
        pooled = _pool_window_sum(v, row, half, t, False) / cnt - v
        mixed = _dot(pooled, w_ref[...])
        ds_ref[...] = jnp.sum(dout * mixed, axis=0, keepdims=True)
        dmixed = dout * s_ref[...]
        dw_ref[...] = _dot(pooled, dmixed, ta=True)
        dpooled = _dot(dmixed, w_ref[...], tb=True)
        dv_ref[...] = _pool_window_sum(dpooled / cnt, row, half, t, True) - dpooled

    return pl.pallas_call(
        body, grid=(1,),
        in_specs=[pl.BlockSpec((t, BW), lambda i: (0, 1)), pl.BlockSpec((t, BW), lambda i: (0, PV)),
                  pl.BlockSpec((BW, BW), lambda i: (0, 0)), pl.BlockSpec((1, BW), lambda i: (0, 0))],
        out_specs=(pl.BlockSpec((t, BW), lambda i: (0, 0)), pl.BlockSpec((BW, BW), lambda i: (0, 0)), pl.BlockSpec((1, BW), lambda i: (0, 0))),
        out_shape=(_sds((t, BW), F32), _sds((BW, BW), F32), _sds((1, BW), F32)), name="pool_bwd",
        compiler_params=_cp())(dbr, proj, wbd, scale)


NA_KEYS = NA_ROWS_WIN * GRID_W


def _na_window(r, n_rows):
    rs = jnp.clip(r - NA_ROWS_WIN // 2, 0, n_rows - NA_ROWS_WIN)
    return pl.multiple_of(rs * GRID_W, GRID_W), rs - r + (NA_ROWS_WIN - 1)


NA_STEP_ROWS = 4


def _na_fwd(q, k, v, ball):
    t = q.shape[0]
    n_rows = t // GRID_W
    rows = NA_STEP_ROWS

    def body(q_ref, k_ref, v_ref, b_ref, o_ref):
        for rr in range(rows):
            start, a0 = _na_window(pl.program_id(0) * rows + rr, n_rows)
            own = slice(rr * GRID_W, (rr + 1) * GRID_W)
            qs = _stack_heads(q_ref[own, :])
            s = _dot(qs, k_ref[pl.ds(start, NA_KEYS), :], tb=True) * (HD ** -0.5) + b_ref[a0]
            p = _softmax_rows(s)
            o_ref[own, :] = _unstack_heads(_dot(p, v_ref[pl.ds(start, NA_KEYS), :]), GRID_W).astype(BF16)

    blk = pl.BlockSpec((rows * GRID_W, BW), lambda r: (r, 0))
    whole = pl.BlockSpec((t, BW), lambda r: (0, 0))
    return pl.pallas_call(
        body, grid=(n_rows // rows,), in_specs=[blk, whole, whole, pl.BlockSpec(ball.shape, lambda r: (0, 0, 0))],
        out_specs=blk, out_shape=_sds((t, BW), BF16), name="na_fwd", compiler_params=_cp())(q, k, v, ball)


def _na_bwd(dbr, q, k, v, ball):
    t = q.shape[0]
    n_rows = t // GRID_W

    rows = NA_STEP_ROWS

    def body(d_ref, q_ref, k_ref, v_ref, b_ref, dq_ref, dk_ref, dv_ref, db_ref):
        @pl.when(pl.program_id(0) == 0)
        def _():
            dk_ref[...] = jnp.zeros_like(dk_ref)
            dv_ref[...] = jnp.zeros_like(dv_ref)
            db_ref[...] = jnp.zeros_like(db_ref)

        for rr in range(rows):
            start, a0 = _na_window(pl.program_id(0) * rows + rr, n_rows)
            keys = pl.ds(start, NA_KEYS)
            own = slice(rr * GRID_W, (rr + 1) * GRID_W)
            qs = _stack_heads(q_ref[own, :])
            kb, vb = k_ref[keys, :], v_ref[keys, :]
            p = _softmax_rows(_dot(qs, kb, tb=True) * (HD ** -0.5) + b_ref[a0])
            dos = _stack_heads(d_ref[own, :]).astype(MXU)
            dp = _dot(dos, vb, tb=True)
            dv_ref[keys, :] += _dot(p, dos, ta=True)
            ds = p * (dp - jnp.sum(dp * p, axis=-1, keepdims=True))
            db_ref[a0] += ds
            dsb = (ds * (HD ** -0.5)).astype(MXU)
            dq_ref[own, :] = _unstack_heads(_dot(dsb, kb), GRID_W)
            dk_ref[keys, :] += _dot(dsb, qs, ta=True)

    blk = pl.BlockSpec((rows * GRID_W, BW), lambda r: (r, 0))
    whole = pl.BlockSpec((t, BW), lambda r: (0, 0))
    tab = pl.BlockSpec(ball.shape, lambda r: (0, 0, 0))
    return pl.pallas_call(
        body, grid=(n_rows // rows,), in_specs=[pl.BlockSpec((rows * GRID_W, BW), lambda r: (r, 2)), blk, whole, whole, tab],
        out_specs=(blk, whole, whole, tab),
        out_shape=(_sds((t, BW), F32), _sds((t, BW), F32), _sds((t, BW), F32), _sds(ball.shape, F32)), name="na_bwd",
        compiler_params=_cp())(dbr, q, k, v, ball)


def _rpb_expand(rpb_pad, onehot):
    def body(r_ref, e_ref, o_ref):
        o_ref[...] = jnp.dot(r_ref[...], e_ref[...], precision=HI, preferred_element_type=F32)

    return pl.pallas_call(body, out_shape=_sds((64, GRID_W * GRID_W), F32), name="rpb_expand", compiler_params=_cp())(rpb_pad, onehot)


def _rpb_reduce(dtab, onehot):
    def body(d_ref, e_ref, o_ref):
        o_ref[...] = lax.dot_general(d_ref[...], e_ref[...], (((1,), (1,)), ((), ())), precision=HI, preferred_element_type=F32)

    return pl.pallas_call(body, out_shape=_sds((64, 128), F32), name="rpb_reduce", compiler_params=_cp())(dtab, onehot)


MEM_TQ = 256


def _mem_fwd(q, mk, mv):
    t = q.shape[0]
    tq = MEM_TQ

    def body(q_ref, k_ref, v_ref, o_ref):
        qv = q_ref[...]
        head = _lane_head(qv.shape)
        out = jnp.zeros((tq, BW), F32)
        for h in range(NH):
            p = _softmax_rows(_dot(jnp.where(head == h, qv, jnp.zeros_like(qv)), k_ref[...], tb=True) * (HD ** -0.5))
            out = out + jnp.where(head == h, _dot(p, v_ref[...]), 0.0)
        o_ref[...] = out.astype(BF16)

    blk = pl.BlockSpec((tq, BW), lambda i: (i, 0))
    kv = pl.BlockSpec((N_MEM, BW), lambda i: (0, 0))
    return pl.pallas_call(body, grid=(t // tq,), in_specs=[blk, kv, kv], out_specs=blk, out_shape=_sds((t, BW), BF16),
                          name="mem_fwd", compiler_params=_cp())(q, mk, mv)


def _mem_bwd(dbr, q, mk, mv):
    t = q.shape[0]
    tq = MEM_TQ

    def body(d_ref, q_ref, k_ref, v_ref, dq_ref, dk_ref, dv_ref):
        first = pl.program_id(0) == 0
        qv, dout = q_ref[...], d_ref[...]
        head = _lane_head(qv.shape)
        dq = jnp.zeros((tq, BW), F32)
        dk = jnp.zeros((N_MEM, BW), F32)
        dv = jnp.zeros((N_MEM, BW), F32)
        for h in range(NH):
            qh = jnp.where(head == h, qv, jnp.zeros_like(qv))
            doh = jnp.where(head == h, dout, 0.0).astype(MXU)
            p = _softmax_rows(_dot(qh, k_ref[...], tb=True) * (HD ** -0.5))
            dp = _dot(doh, v_ref[...], tb=True)
            dv = dv + _dot(p, doh, ta=True)
            dsb = (p * (dp - jnp.sum(dp * p, axis=-1, keepdims=True)) * (HD ** -0.5)).astype(MXU)
            dq = dq + jnp.where(head == h, _dot(dsb, k_ref[...]), 0.0)
            dk = dk + _dot(dsb, qh, ta=True)
        dq_ref[...] = dq
        _acc(dk_ref, dk, first)
        _acc(dv_ref, dv, first)

    blk = pl.BlockSpec((tq, BW), lambda i: (i, 0))
    kv = pl.BlockSpec((N_MEM, BW), lambda i: (0, 0))
    return pl.pallas_call(
        body, grid=(t // tq,), in_specs=[pl.BlockSpec((tq, BW), lambda i: (i, 3)), blk, kv, kv], out_specs=(blk, kv, kv),
        out_shape=(_sds((t, BW), F32), _sds((N_MEM, BW), F32), _sds((N_MEM, BW), F32)), name="mem_bwd",
        compiler_params=_cp())(dbr, q, mk, mv)


def _memkv_prep(kv, g_mk):
    def body(kv_ref, g_ref, k_ref, v_ref):
        k_ref[...] = _gnorm(kv_ref[:, 0:BW], g_ref[...]).astype(BF16)
        v_ref[...] = kv_ref[:, BW:2 * BW].astype(BF16)

    return pl.pallas_call(body, out_shape=(_sds((N_MEM, BW), BF16), _sds((N_MEM, BW), BF16)), name="memkv_prep",
                          compiler_params=_cp())(kv, g_mk)


def _memkv_bwd(kv, dk, dv, g_mk):
    def body(kv_ref, dk_ref, dv_ref, g_ref, o_ref, dg_ref):
        dkk, gain = _gnorm_bwd(dk_ref[...], kv_ref[:, 0:BW], g_ref[...])
        o_ref[:, 0:BW] = dkk.astype(BF16)
        o_ref[:, BW:2 * BW] = dv_ref[...].astype(BF16)
        dg_ref[...] = jnp.sum(gain, axis=0, keepdims=True)

    return pl.pallas_call(body, out_shape=(_sds((N_MEM, 2 * BW), BF16), _sds((1, BW), F32)), name="memkv_bwd",
                          compiler_params=_cp())(kv, dk, dv, g_mk)


MERGE_TM = 256


def _merge_fwd(brs, wbt, gp):
    t = gp.shape[0]
    tm = MERGE_TM

    def body(b0, b1, b2, b3, wb_ref, gp_ref, o_ref):
        out = jnp.zeros((tm, D), F32)
        for n, b_ref in enumerate((b0, b1, b2, b3)):
            up = _dot(b_ref[...], wb_ref[n], tb=True)
            out = out + _sigmoid(gp_ref[:, n * D:(n + 1) * D]) * up
        o_ref[...] = out.astype(BF16)

    blk = pl.BlockSpec((tm, BW), lambda i: (i, 0))
    return pl.pallas_call(
        body, grid=(t // tm,),
        in_specs=[blk, blk, blk, blk, pl.BlockSpec((NH, D, BW), lambda i: (0, 0, 0)), pl.BlockSpec((tm, NH * D), lambda i: (i, 0))],
        out_specs=pl.BlockSpec((tm, D), lambda i: (i, 0)), out_shape=_sds((t, D), BF16), name="merge_fwd",
        compiler_params=_cp())(*brs, wbt, gp)


def _merge_bwd(dmerged, brs, wbt, gp):
    t = gp.shape[0]
    tm = MERGE_TM

    def body(d_ref, b0, b1, b2, b3, wb_ref, gp_ref, dgp_ref, dup_ref):
        dm = d_ref[...]
        for n, b_ref in enumerate((b0, b1, b2, b3)):
            up = _dot(b_ref[...], wb_ref[n], tb=True)
            g = _sigmoid(gp_ref[:, n * D:(n + 1) * D])
            dgp_ref[:, n * D:(n + 1) * D] = (dm * up * (g * (1.0 - g))).astype(BF16)
            dup_ref[:, n * D:(n + 1) * D] = (dm * g).astype(BF16)

    row = pl.BlockSpec((tm, D), lambda i: (i, 0))
    blk = pl.BlockSpec((tm, BW), lambda i: (i, 0))
    wide = pl.BlockSpec((tm, NH * D), lambda i: (i, 0))
    return pl.pallas_call(
        body, grid=(t // tm,), in_specs=[row, blk, blk, blk, blk, pl.BlockSpec((NH, D, BW), lambda i: (0, 0, 0)), wide],
        out_specs=(wide, wide), out_shape=(_sds((t, NH * D), BF16), _sds((t, NH * D), BF16)), name="merge_bwd",
        compiler_params=_cp())(dmerged, *brs, wbt, gp)


def _dbranch(dup, wbt):
    t = dup.shape[0]
    tm = 512

    def body(d_ref, w_ref, o_ref):
        o_ref[...] = _dot(d_ref[...], w_ref[...])

    return pl.pallas_call(
        body, grid=(t // tm, NH), in_specs=[pl.BlockSpec((tm, D), lambda i, n: (i, n)), pl.BlockSpec((None, D, BW), lambda i, n: (n, 0, 0))],
        out_specs=pl.BlockSpec((tm, BW), lambda i, n: (i, n)), out_shape=_sds((t, NH * BW), F32), name="dbranch",
        compiler_params=_cp())(dup, wbt)


def _dwbranch(brs, dup):
    t = dup.shape[0]

    def body(b0, b1, b2, b3, d_ref, o_ref):
        for n, b_ref in enumerate((b0, b1, b2, b3)):
            o_ref[n] = _dot(d_ref[:, n * D:(n + 1) * D], b_ref[...], ta=True).astype(BF16)

    return pl.pallas_call(body, out_shape=_sds((NH, D, BW), BF16), name="dwbranch", compiler_params=_cp())(*brs, dup)


def _swiglu_fwd(ag):
    t = ag.shape[0]
    tm = 256

    def body(ag_ref, o_ref):
        a, g = ag_ref[:, 0:FF], ag_ref[:, FF:2 * FF]
        o_ref[...] = (a * _sigmoid(a) * g).astype(BF16)

    return pl.pallas_call(body, grid=(t // tm,), in_specs=[pl.BlockSpec((tm, 2 * FF), lambda i: (i, 0))],
                          out_specs=pl.BlockSpec((tm, FF), lambda i: (i, 0)), out_shape=_sds((t, FF), BF16), name="swiglu_fwd",
                          compiler_params=_cp())(ag)


def _swiglu_bwd(ag, dy):
    t = ag.shape[0]
    tm = 256

    def body(ag_ref, dy_ref, o_ref):
        a, g, d = ag_ref[:, 0:FF], ag_ref[:, FF:2 * FF], dy_ref[...]
        s = _sigmoid(a)
        o_ref[:, 0:FF] = (d * g * (s * (1.0 + a * (1.0 - s)))).astype(BF16)
        o_ref[:, FF:2 * FF] = (d * (a * s)).astype(BF16)

    return pl.pallas_call(
        body, grid=(t // tm,), in_specs=[pl.BlockSpec((tm, 2 * FF), lambda i: (i, 0)), pl.BlockSpec((tm, FF), lambda i: (i, 0))],
        out_specs=pl.BlockSpec((tm, 2 * FF), lambda i: (i, 0)), out_shape=_sds((t, 2 * FF), BF16), name="swiglu_bwd",
        compiler_params=_cp())(ag, dy)


def _loss_head(y, target):
    t, d = y.shape
    tm = 256

    def body(y_ref, t_ref, dy_ref, dyb_ref, l_ref):
        e = y_ref[...] - t_ref[...]
        dy_ref[...] = e * (1.0 / d)
        dyb_ref[...] = (e * (1.0 / d)).astype(BF16)
        _acc(l_ref, jnp.full((8, 128), 0.5 * jnp.sum(jnp.sum(e * e, axis=-1, keepdims=True) * (1.0 / d)), F32), pl.program_id(0) == 0)

    row = pl.BlockSpec((tm, d), lambda i: (i, 0))
    return pl.pallas_call(body, grid=(t // tm,), in_specs=[row, row], out_specs=(row, row, pl.BlockSpec((8, 128), lambda i: (0, 0))),
                          out_shape=(_sds((t, d), F32), _sds((t, d), BF16), _sds((8, 128), F32)), name="loss_head",
                          compiler_params=_cp())(y, target)


def _sum_slots(x, name):
    k, r, c = x.shape
    tr = _tile(r, 512) if r % 128 == 0 else r

    def body(x_ref, o_ref):
        acc = x_ref[0].astype(F32)
        for s in range(1, k):
            acc = acc + x_ref[s].astype(F32)
        o_ref[...] = acc

    return pl.pallas_call(body, grid=(r // tr,), in_specs=[pl.BlockSpec((k, tr, c), lambda i: (0, i, 0))],
                          out_specs=pl.BlockSpec((tr, c), lambda i: (i, 0)), out_shape=_sds((r, c), F32), name=name,
                          compiler_params=_cp())(x)


def _pair_sum(bufs, recvs, cidx):
    n = len(bufs)

    def body(c_ref, *refs):
        for i in range(n):
            refs[2 * n + i][...] = (refs[i][...].astype(F32) + refs[n + i][...].astype(F32)).astype(BF16)

    return pl.pallas_call(
        body,
        grid_spec=pltpu.PrefetchScalarGridSpec(
            num_scalar_prefetch=1, grid=(4,),
            in_specs=[pl.BlockSpec((None, None) + b.shape[2:], lambda s, cref: (s, cref[0], 0, 0)) for b in bufs]
            + [pl.BlockSpec((None,) + r.shape[1:], lambda s, cref: (s, 0, 0)) for r in recvs],
            out_specs=tuple(pl.BlockSpec((None,) + r.shape[1:], lambda s, cref: (s, 0, 0)) for r in recvs)),
        out_shape=tuple(_sds(r.shape, BF16) for r in recvs), name="rs_pair_sum", compiler_params=_cp())(cidx, *bufs, *recvs)


def _adamw_update(w, gv, m, v):
    mn = ADAM_B1 * m + (1.0 - ADAM_B1) * gv
    vn = ADAM_B2 * v + (1.0 - ADAM_B2) * (gv * gv)
    m_hat = mn / (1.0 - ADAM_B1 ** ADAM_STEP)
    v_hat = vn / (1.0 - ADAM_B2 ** ADAM_STEP)
    return -ADAM_LR * (m_hat / (jnp.sqrt(v_hat) + ADAM_EPS) + ADAM_WD * w), mn, vn


def _adamw(w, g, m, v, name):
    r, c = w.shape

    def body(w_ref, g_ref, m_ref, v_ref, d_ref, nm_ref, nv_ref):
        d_ref[...], nm_ref[...], nv_ref[...] = _adamw_update(w_ref[...], g_ref[...], m_ref[...], v_ref[...])

    blk = pl.BlockSpec((r, c), lambda i: (0, 0))
    return pl.pallas_call(body, grid=(1,), in_specs=[blk] * 4, out_specs=(blk,) * 3,
                          out_shape=tuple(_sds((r, c), F32) for _ in range(3)), name=name, compiler_params=_cp())(w, g, m, v)


def _adamw_layer(layer, w, g, m, v, outs, name):
    _, r, c = w.shape
    tr = max(d for d in range(8, r + 1, 8) if r % d == 0 and d * c * 4 <= 2 ** 20)

    def body(w_ref, m_ref, v_ref, g_ref, *refs):
        d_ref, nm_ref, nv_ref, go_ref = refs[4:]
        gv = g_ref[...]
        d_ref[...], nm_ref[...], nv_ref[...] = _adamw_update(w_ref[...], gv, m_ref[...], v_ref[...])
        go_ref[...] = gv

    blk = pl.BlockSpec((None, tr, c), lambda i: (layer, i, 0))
    return pl.pallas_call(
        body, grid=(r // tr,), in_specs=[blk] * 3 + [pl.BlockSpec((tr, c), lambda i: (i, 0))] + [ANY] * 4, out_specs=(blk,) * 4,
        out_shape=tuple(_sds(w.shape, F32) for _ in range(4)), input_output_aliases={4 + j: j for j in range(4)}, name=name,
        compiler_params=_cp())(w, m, v, g, *outs)


def _all_gather(shards, name):
    n = len(shards)

    def body(*refs):
        x_refs, out_refs = refs[:n], refs[n:2 * n]
        send_sems, recv_sems, local_sems = refs[2 * n:]
        x, y, cc = lax.axis_index("x"), lax.axis_index("y"), lax.axis_index("c")
        me, sibling = (x, y, cc), (x, y, 1 - cc)
        chips = [(1 - x, y), (x, 1 - y), (1 - x, 1 - y)]

        def copy(i, k, block, to, own=False):
            px, py, pc = block
            slot = out_refs[i].at[4 * px + 2 * py + pc]
            return pltpu.make_async_remote_copy(
                src_ref=x_refs[i] if own else slot, dst_ref=slot, send_sem=send_sems.at[7 * i + k],
                recv_sem=recv_sems.at[7 * i + k], device_id=to, device_id_type=MESH)

        mine = [pltpu.make_async_copy(x_refs[i], out_refs[i].at[4 * x + 2 * y + cc], local_sems.at[i]) for i in range(n)]
        for cp in mine:
            cp.start()
        first = []
        for j, chip in enumerate(chips):
            first += [copy(i, 1 + j, me, (*chip, cc), own=True) for i in range(n)]
        first += [copy(i, 0, me, sibling, own=True) for i in range(n)]
        for cp in first:
            cp.start()
        passed = []
        for j, chip in enumerate(chips):
            for i in range(n):
                copy(i, 1 + j, (*chip, cc), me).wait_recv()
                cp = copy(i, 4 + j, (*chip, cc), sibling)
                cp.start()
                passed.append(cp)
        for i in range(n):
            copy(i, 0, sibling, me).wait_recv()
        for j, chip in enumerate(chips):
            for i in range(n):
                copy(i, 4 + j, (*chip, 1 - cc), me).wait_recv()
        for cp in first + passed:
            cp.wait_send()
        for cp in mine:
            cp.wait()

    return pl.pallas_call(
        body, out_shape=tuple(_sds((N_DEV,) + s.shape, s.dtype) for s in shards), in_specs=[ANY] * n, out_specs=(ANY,) * n,
        scratch_shapes=[pltpu.SemaphoreType.DMA((7 * n,)), pltpu.SemaphoreType.DMA((7 * n,)), pltpu.SemaphoreType.DMA((n,))],
        name=name)(*shards)


def _rs_core_swap(bufs, name):
    n = len(bufs)

    def body(*refs):
        b_refs, recv_refs = refs[:n], refs[n:2 * n]
        send_sems, recv_sems = refs[2 * n:]
        x, y, cc = lax.axis_index("x"), lax.axis_index("y"), lax.axis_index("c")
        copies = [pltpu.make_async_remote_copy(
            src_ref=b_refs[i].at[s, 1 - cc], dst_ref=recv_refs[i].at[s], send_sem=send_sems.at[4 * i + s],
            recv_sem=recv_sems.at[4 * i + s], device_id=(x, y, 1 - cc), device_id_type=MESH) for i in range(n) for s in range(4)]
        for cp in copies:
            cp.start()
        for cp in copies:
            cp.wait()

    return pl.pallas_call(
        body, out_shape=tuple(_sds((4,) + b.shape[2:], b.dtype) for b in bufs), in_specs=[ANY] * n, out_specs=(ANY,) * n,
        scratch_shapes=[pltpu.SemaphoreType.DMA((4 * n,)), pltpu.SemaphoreType.DMA((4 * n,))], name=name)(*bufs)


HBM = pl.BlockSpec(memory_space=pltpu.HBM)
SEMS = pl.BlockSpec(memory_space=pltpu.SEMAPHORE)
EFFECT = pltpu.SideEffectType.DATAFLOW_SIDE_EFFECTING


def _hbm(a):
    return pltpu.HBM(a.shape, a.dtype)


def _other_chips(x, y):
    return [(1 - x, y), (x, 1 - y), (1 - x, 1 - y)]


def _ici_start(srcs, lands, by_chip, name):
    n = len(srcs)

    def body(*refs):
        s_refs, land_refs = refs[:n], refs[n:2 * n]
        send_sems, recv_sems = refs[2 * n], refs[2 * n + 1]
        token = refs[-1]
        x, y, cc = lax.axis_index("x"), lax.axis_index("y"), lax.axis_index("c")
        mine = 2 * x + y if by_chip else 4 * x + 2 * y + cc
        for px, py in _other_chips(x, y):
            for i in range(n):
                pltpu.make_async_remote_copy(
                    src_ref=s_refs[i].at[2 * px + py] if by_chip else s_refs[i], dst_ref=land_refs[i].at[mine],
                    send_sem=send_sems.at[i], recv_sem=recv_sems.at[i], device_id=(px, py, cc), device_id_type=MESH).start()
        token[...] = jnp.zeros_like(token)

    out = pl.pallas_call(
        body, name=name,
        out_shape=(pltpu.SemaphoreType.DMA((n,)), pltpu.SemaphoreType.DMA((n,)), *[_hbm(s) for s in srcs], *[_hbm(l) for l in lands],
                   _sds((8, 128), F32)),
        in_specs=[HBM] * (2 * n), out_specs=(SEMS, SEMS, *[HBM] * (2 * n), pl.BlockSpec(memory_space=pltpu.VMEM)),
        input_output_aliases={i: 2 + i for i in range(2 * n)}, compiler_params=pltpu.CompilerParams(has_side_effects=EFFECT),
    )(*[pltpu.with_memory_space_constraint(s, pltpu.HBM) for s in srcs],
      *[pltpu.with_memory_space_constraint(l, pltpu.HBM) for l in lands])
    return out[0], out[1], out[2:2 + n], out[2 + n:2 + 2 * n], out[-1]


def _ici_wait(started, after, name):
    send_sems, recv_sems, srcs, lands, _ = started
    n = len(srcs)

    def body(*refs):
        land_refs = refs[n:2 * n]
        send_sems, recv_sems = refs[2 * n], refs[2 * n + 1]
        x, y, cc = lax.axis_index("x"), lax.axis_index("y"), lax.axis_index("c")
        for i in range(n):
            three = land_refs[i].at[pl.ds(0, 3)]
            cp = pltpu.make_async_remote_copy(src_ref=three, dst_ref=three, send_sem=send_sems.at[i], recv_sem=recv_sems.at[i],
                                              device_id=(x, y, cc), device_id_type=MESH)
            cp.wait_send()
            cp.wait_recv()

    return pl.pallas_call(
        body, name=name, out_shape=tuple(_hbm(l) for l in lands), in_specs=[HBM] * (2 * n) + [SEMS, SEMS, ANY],
        out_specs=tuple([HBM] * n), input_output_aliases={n + i: i for i in range(n)},
        compiler_params=pltpu.CompilerParams(has_side_effects=EFFECT))(*srcs, *lands, send_sems, recv_sems, after)


def _gather_d2d(blocks, lands, name):
    n = len(blocks)

    def body(*refs):
        x_refs, land_refs = refs[:n], refs[2 * n:3 * n]
        send_sems, recv_sems, in_sems, out_sems = refs[3 * n:3 * n + 4]
        stage = refs[3 * n + 4:]
        x, y, cc = lax.axis_index("x"), lax.axis_index("y"), lax.axis_index("c")
        sibling = (x, y, 1 - cc)
        staged = [pltpu.make_async_copy(x_refs[i], stage[i], in_sems.at[i]) for i in range(n)]
        for cp in staged:
            cp.start()
        copies = []
        for i in range(n):
            slot = land_refs[i].at[4 * x + 2 * y + cc]
            copies.append(pltpu.make_async_remote_copy(src_ref=x_refs[i], dst_ref=slot, send_sem=send_sems.at[4 * i],
                                                       recv_sem=recv_sems.at[4 * i], device_id=sibling, device_id_type=MESH))
            for j, (px, py) in enumerate(_other_chips(x, y)):
                slot = land_refs[i].at[4 * px + 2 * py + cc]
                copies.append(pltpu.make_async_remote_copy(src_ref=slot, dst_ref=slot, send_sem=send_sems.at[4 * i + 1 + j],
                                                           recv_sem=recv_sems.at[4 * i + 1 + j], device_id=sibling, device_id_type=MESH))
        for cp in copies:
            cp.start()
        mine = []
        for i in range(n):
            staged[i].wait()
            mine.append(pltpu.make_async_copy(stage[i], land_refs[i].at[4 * x + 2 * y + cc], out_sems.at[i]))
            mine[i].start()
        for i in range(n):
            slot = land_refs[i].at[4 * x + 2 * y + (1 - cc)]
            pltpu.make_async_remote_copy(src_ref=slot, dst_ref=slot, send_sem=send_sems.at[4 * i], recv_sem=recv_sems.at[4 * i],
                                         device_id=sibling, device_id_type=MESH).wait_recv()
            for j, (px, py) in enumerate(_other_chips(x, y)):
                slot = land_refs[i].at[4 * px + 2 * py + (1 - cc)]
                pltpu.make_async_remote_copy(src_ref=slot, dst_ref=slot, send_sem=send_sems.at[4 * i + 1 + j],
                                             recv_sem=recv_sems.at[4 * i + 1 + j], device_id=sibling, device_id_type=MESH).wait_recv()
        for cp in copies:
            cp.wait_send()
        for cp in mine:
            cp.wait()

    return pl.pallas_call(
        body, out_shape=tuple(_sds(l.shape, l.dtype) for l in lands), in_specs=[ANY] * (2 * n), out_specs=(ANY,) * n,
        input_output_aliases={n + i: i for i in range(n)},
        scratch_shapes=[pltpu.SemaphoreType.DMA((4 * n,)), pltpu.SemaphoreType.DMA((4 * n,)), pltpu.SemaphoreType.DMA((n,)),
                        pltpu.SemaphoreType.DMA((n,))] + [pltpu.VMEM(b.shape, b.dtype) for b in blocks],
        name=name, compiler_params=_cp())(*blocks, *lands)


def _sum_own(parts, recvs, chip, name):
    n = len(parts)

    def body(c_ref, *refs):
        s = pl.program_id(0)
        for i in range(n):
            val = jnp.where(c_ref[0] == s, refs[i][...], refs[n + i][...]).astype(F32)
            _acc(refs[2 * n + i], val, s == 0)

    ins = [pl.BlockSpec((None,) + p.shape[1:], lambda s, cref: (s, 0, 0)) for p in parts]
    return pl.pallas_call(
        body, grid_spec=pltpu.PrefetchScalarGridSpec(
            num_scalar_prefetch=1, grid=(4,), in_specs=ins + ins,
            out_specs=tuple(pl.BlockSpec(p.shape[1:], lambda s, cref: (0, 0)) for p in parts)),
        out_shape=tuple(_sds(p.shape[1:], F32) for p in parts), name=name, compiler_params=_cp())(chip, *parts, *recvs)


BIG = (("w_in", True), ("w_gate", True), ("w_mem_kv", False), ("w_branch", True), ("w_out", False), ("w_ffn_in", True),
       ("w_ffn_out", False))

SMALL = ("norm_mix_g", "norm_mem_g", "ret_decay_fwd", "ret_decay_bwd", "ret_norm_g", "pool_w", "pool_scale", "na_q_norm_g",
         "na_k_norm_g", "na_rpb", "mem_q_norm_g", "mem_k_norm_g", "norm_ffn_g")
WEIGHTS = ("norm_mix_g", "norm_mem_g", "w_in", "w_gate", "ret_decay_fwd", "ret_decay_bwd", "ret_norm_g", "pool_w", "pool_scale",
           "na_q_norm_g", "na_k_norm_g", "na_rpb", "mem_q_norm_g", "mem_k_norm_g", "w_mem_kv", "w_branch", "w_out", "norm_ffn_g",
           "w_ffn_in", "w_ffn_out")


def _to_exchange(name, transposed, shard):
    if name == "w_branch":
        return jnp.swapaxes(shard, 1, 2).reshape(NH * (D // N_DEV), BW)
    return shard.T if transposed else shard


def _from_exchange(name, transposed, block):
    if name == "w_branch":
        return jnp.swapaxes(block.reshape(NH, D // N_DEV, BW), 1, 2)
    return block.T if transposed else block


def _whole_from_gathered(name, g):
    if name == "w_branch":
        return jnp.swapaxes(g.reshape(N_DEV, NH, D // N_DEV, BW), 0, 1).reshape(NH, D, BW)
    return g.reshape(N_DEV * g.shape[1], g.shape[2])


def _by_destination(name, g):
    if name == "w_branch":
        g = jnp.swapaxes(g.reshape(NH, N_DEV, D // N_DEV, BW), 0, 1).reshape(N_DEV * NH * (D // N_DEV), BW)
    return g.reshape(4, 2, g.shape[0] // N_DEV, g.shape[1])


SMALL_PAD = 1024


def _pack_small(vals, loss=None):
    parts = [vals[n] for n in SMALL] + [jnp.zeros((1,), F32) if loss is None else loss.reshape(1)]
    rows = []
    for p in parts:
        flat = p.reshape(-1)
        rows.append(jnp.pad(flat, (0, -flat.shape[0] % SMALL_PAD)).reshape(-1, 128))
    return jnp.concatenate(rows, axis=0)


def _unpack_small(packed, like):
    out, off = {}, 0
    for n in SMALL:
        sz = int(np.prod(like[n].shape))
        nrow = -(-sz // SMALL_PAD) * (SMALL_PAD // 128)
        out[n] = packed[off:off + nrow].reshape(-1)[:sz].reshape(like[n].shape)
        off += nrow
    return out, packed[off, 0]


def _na_constants():
    c = np.arange(GRID_W)
    win = np.clip(c - NA_COLS_WIN // 2, 0, GRID_W - NA_COLS_WIN)
    kc = np.arange(GRID_W)
    inside = (kc[None, :] >= win[:, None]) & (kc[None, :] < win[:, None] + NA_COLS_WIN)
    off = kc[None, :] - c[:, None] + NA_COLS_WIN - 1
    onehot = np.zeros((128, GRID_W, GRID_W), np.float32)
    for b in range(2 * NA_COLS_WIN - 1):
        onehot[b] = (off == b) & inside
    maskadd = np.where(inside, 0.0, NEG).astype(np.float32)
    return onehot.reshape(128, GRID_W * GRID_W), maskadd


def _na_bias_table(tab, maskadd):
    n_off = 2 * NA_ROWS_WIN - 1
    t4 = tab[:NH * n_off].reshape(NH, n_off, GRID_W, GRID_W) + maskadd[None, None]
    ball = jnp.stack([t4[:, a0:a0 + NA_ROWS_WIN] for a0 in range(NA_ROWS_WIN)], axis=1)
    return ball.transpose(1, 0, 3, 2, 4).reshape(NA_ROWS_WIN, NH * GRID_W, NA_KEYS)


def _rotary_tables(t):
    half = HD // 2
    inv = ROPE_THETA ** (-jnp.arange(half, dtype=F32) / half)
    ang = jnp.arange(t, dtype=F32)[:, None] * inv[None, :]
    cos, sin = jnp.cos(ang), jnp.sin(ang)
    return jnp.tile(jnp.concatenate([cos, cos], axis=-1), (1, NH)), jnp.tile(jnp.concatenate([-sin, sin], axis=-1), (1, NH))


def _block_diag(pw):
    out = jnp.zeros((BW, BW), pw.dtype)
    for g in range(NH):
        out = lax.dynamic_update_slice(out, pw[g], (g * HD, g * HD))
    return out


def _tile4(g):
    return jnp.tile(g.reshape(1, HD), (1, NH))


def _layer_fwd(x, mem, sw, lw, consts):
    cos2, sin2, onehot, maskadd = consts
    h = _rmsnorm_fwd(x, sw["norm_mix_g"].reshape(1, D), "norm_mix_fwd")
    proj = _mm(h, lw["w_in"], tb=True, name="mm_in")
    gp = _mm(h, lw["w_gate"], tb=True, name="mm_gate")
    g_naq, g_nak, g_mq = _tile4(sw["na_q_norm_g"]), _tile4(sw["na_k_norm_g"]), _tile4(sw["mem_q_norm_g"])
    rq, rk, rv, nq, nk, nv, mq = _prep_fwd(proj, cos2, sin2, g_naq, g_nak, g_mq)

    lgf, lgb = jax.nn.log_sigmoid(sw["ret_decay_fwd"]), jax.nn.log_sigmoid(sw["ret_decay_bwd"])
    g_ret = sw["ret_norm_g"].reshape(1, BW)
    o_ret, ret = _ret_fwd(rq, rk, rv, proj, lgf, lgb, g_ret)

    wbd = _block_diag(sw["pool_w"]).astype(BF16)
    p_scale = sw["pool_scale"].reshape(1, BW)
    pool = _pool_fwd(proj, wbd, p_scale)

    rpb_pad = jnp.pad(sw["na_rpb"].reshape(NH * 15, 31), ((0, 4), (0, 97)))
    ball = _na_bias_table(_rpb_expand(rpb_pad, onehot), maskadd)
    na = _na_fwd(nq, nk, nv, ball)

    memn = _rmsnorm_fwd(mem, sw["norm_mem_g"].reshape(1, D), "norm_mem_fwd")
    kv = _mm(memn, lw["w_mem_kv"], name="mm_memkv")
    g_mk = _tile4(sw["mem_k_norm_g"])
    mk, mv = _memkv_prep(kv, g_mk)
    mo = _mem_fwd(mq, mk, mv)

    br = (ret, pool, na, mo)
    merged = _merge_fwd(br, lw["w_branch"], gp)
    x1 = _mm(merged, lw["w_out"], add=x, name="mm_out")
    h2 = _rmsnorm_fwd(x1, sw["norm_ffn_g"].reshape(1, D), "norm_ffn_fwd")
    ag = _mm(h2, lw["w_ffn_in"], tb=True, name="mm_ffn_in")
    yff = _swiglu_fwd(ag)
    x2 = _mm(yff, lw["w_ffn_out"], add=x1, name="mm_ffn_out")
    saved = dict(x=x, h=h, proj=proj, gp=gp, rq=rq, rk=rk, rv=rv, nq=nq, nk=nk, nv=nv, mq=mq, o_ret=o_ret, ball=ball, memn=memn,
                 kv=kv, mk=mk, mv=mv, br=br, merged=merged, x1=x1, h2=h2, ag=ag, yff=yff, lgf=lgf, lgb=lgb, wbd=wbd)
    return x2, saved


def _layer_bwd(dx2, dx2b, mem, sw, lw, sv, consts, dep=None):
    cos2, sin2, onehot, maskadd = consts
    gb, gs = {}, {}
    dy = _mm(dx2b, lw["w_ffn_out"], tb=True, dep=dep, name="mm_ffn_out_dx")
    gb["w_ffn_out"] = _mm(sv["yff"], dx2b, ta=True, out_dtype=BF16, name="mm_ffn_out_dw")
    dag = _swiglu_bwd(sv["ag"], dy)
    dh2 = _mm(dag, lw["w_ffn_in"], name="mm_ffn_in_dx")
    gb["w_ffn_in"] = _mm(dag, sv["h2"], ta=True, out_dtype=BF16, name="mm_ffn_in_dw")
    dx1, dx1b, dg = _rmsnorm_bwd(dh2, sv["x1"], sw["norm_ffn_g"].reshape(1, D), dx2, "norm_ffn_bwd")
    gs["norm_ffn_g"] = dg.reshape(D)

    dmerged = _mm(dx1b, lw["w_out"], tb=True, name="mm_out_dx")
    gb["w_out"] = _mm(sv["merged"], dx1b, ta=True, out_dtype=BF16, name="mm_out_dw")
    dgp, dup = _merge_bwd(dmerged, sv["br"], lw["w_branch"], sv["gp"])
    dbr = _dbranch(dup, lw["w_branch"])
    gb["w_branch"] = _dwbranch(sv["br"], dup)

    g_ret = sw["ret_norm_g"].reshape(1, BW)
    do_ret, d_rg, dg_ret = _ret_post_bwd(dbr, sv["o_ret"], sv["proj"], g_ret)
    d_rq, d_rk, d_rv, dlg = _ret_bwd(do_ret, sv["rq"], sv["rk"], sv["rv"], sv["lgf"], sv["lgb"])
    gs["ret_norm_g"] = dg_ret.reshape(BW)
    _, vjp_f = jax.vjp(jax.nn.log_sigmoid, sw["ret_decay_fwd"])
    _, vjp_b = jax.vjp(jax.nn.log_sigmoid, sw["ret_decay_bwd"])
    gs["ret_decay_fwd"] = vjp_f(dlg[0:NH, 0])[0]
    gs["ret_decay_bwd"] = vjp_b(dlg[NH:2 * NH, 0])[0]

    p_scale = sw["pool_scale"].reshape(1, BW)
    d_pv, dwbd, dscale = _pool_bwd(dbr, sv["proj"], sv["wbd"], p_scale)
    gs["pool_w"] = jnp.stack([dwbd[g * HD:(g + 1) * HD, g * HD:(g + 1) * HD] for g in range(NH)])
    gs["pool_scale"] = dscale.reshape(BW)

    d_nq, d_nk, d_nv, dball = _na_bwd(dbr, sv["nq"], sv["nk"], sv["nv"], sv["ball"])
    _, vjp_tab = jax.vjp(lambda tab: _na_bias_table(tab, maskadd), jnp.zeros((64, GRID_W * GRID_W), F32))
    drpb = _rpb_reduce(vjp_tab(dball)[0], onehot)
    gs["na_rpb"] = drpb[:NH * 15, :31].reshape(NH, 15, 31)

    d_mq, d_mk, d_mv = _mem_bwd(dbr, sv["mq"], sv["mk"], sv["mv"])
    g_mk = _tile4(sw["mem_k_norm_g"])
    dkv, dg_mk = _memkv_bwd(sv["kv"], d_mk, d_mv, g_mk)
    gs["mem_k_norm_g"] = dg_mk.reshape(NH, HD).sum(0)
    gb["w_mem_kv"] = _mm(sv["memn"], dkv, ta=True, out_dtype=BF16, name="mm_memkv_dw")
    dmemn = _mm(dkv, lw["w_mem_kv"], tb=True, name="mm_memkv_dx")
    _, _, dg_mem = _rmsnorm_bwd(dmemn, mem, sw["norm_mem_g"].reshape(1, D), jnp.zeros_like(mem), "norm_mem_bwd")
    gs["norm_mem_g"] = dg_mem.reshape(D)

    g_naq, g_nak, g_mq = _tile4(sw["na_q_norm_g"]), _tile4(sw["na_k_norm_g"]), _tile4(sw["mem_q_norm_g"])
    dproj, dg_naq, dg_nak, dg_mq = _prep_bwd(sv["proj"], cos2, sin2, g_naq, g_nak, g_mq, d_rq, d_rk, d_rv, d_rg, d_pv, d_nq, d_nk,
                                             d_nv, d_mq)
    gs["na_q_norm_g"] = dg_naq.reshape(NH, HD).sum(0)
    gs["na_k_norm_g"] = dg_nak.reshape(NH, HD).sum(0)
    gs["mem_q_norm_g"] = dg_mq.reshape(NH, HD).sum(0)

    dh = _mm(dproj, lw["w_in"], name="mm_in_dx")
    dh = _mm(dgp, lw["w_gate"], add=dh, name="mm_gate_dx")
    gb["w_in"] = _mm(dproj, sv["h"], ta=True, out_dtype=BF16, name="mm_in_dw")
    gb["w_gate"] = _mm(dgp, sv["h"], ta=True, out_dtype=BF16, name="mm_gate_dw")
    dx, dxb, dg = _rmsnorm_bwd(dh, sv["x"], sw["norm_mix_g"].reshape(1, D), dx1, "norm_mix_bwd")
    gs["norm_mix_g"] = dg.reshape(D)
    return dx, dxb, gb, gs


def _local_step(x, mem, target, small, get_layer, on_grads):
    t = x.shape[0]
    cos2, sin2 = _rotary_tables(t)
    onehot, maskadd = _na_constants()
    consts = (cos2, sin2, jnp.asarray(onehot), jnp.asarray(maskadd))
    saved, weights, cur = [], [], x
    for l in range(DEPTH):
        sw = {n: small[n][l] for n in SMALL}
        weights.append(get_layer(l, cur))
        cur, sv = _layer_fwd(cur, mem, sw, weights[l], consts)
        saved.append(sv)
    dy, dyb, loss_tile = _loss_head(cur, target)
    small_g = {n: [None] * DEPTH for n in SMALL}
    dep = None
    for l in reversed(range(DEPTH)):
        sw = {n: small[n][l] for n in SMALL}
        dy, dyb, gb, gs = _layer_bwd(dy, dyb, mem, sw, weights[l], saved[l], consts, dep)
        dep = on_grads(l, gb, dy)
        for n in SMALL:
            small_g[n][l] = gs[n]
    return loss_tile[0, 0], dy, {n: jnp.stack(v) for n, v in small_g.items()}


def _flat2d(a):
    return a.reshape(-1, a.shape[-1])


def kernel(x, mem, norm_mix_g, norm_mem_g, w_in, w_gate, ret_decay_fwd, ret_decay_bwd, ret_norm_g, pool_w, pool_scale, na_q_norm_g, na_k_norm_g, na_rpb, mem_q_norm_g, mem_k_norm_g, w_mem_kv, w_branch, w_out, norm_ffn_g, w_ffn_in, w_ffn_out, loss_target, m_norm_mix_g, m_norm_mem_g, m_w_in, m_w_gate, m_ret_decay_fwd, m_ret_decay_bwd, m_ret_norm_g, m_pool_w, m_pool_scale, m_na_q_norm_g, m_na_k_norm_g, m_na_rpb, m_mem_q_norm_g, m_mem_k_norm_g, m_w_mem_kv, m_w_branch, m_w_out, m_norm_ffn_g, m_w_ffn_in, m_w_ffn_out, v_norm_mix_g, v_norm_mem_g, v_w_in, v_w_gate, v_ret_decay_fwd, v_ret_decay_bwd, v_ret_norm_g, v_pool_w, v_pool_scale, v_na_q_norm_g, v_na_k_norm_g, v_na_rpb, v_mem_q_norm_g, v_mem_k_norm_g, v_w_mem_kv, v_w_branch, v_w_out, v_norm_ffn_g, v_w_ffn_in, v_w_ffn_out):
    w = dict(norm_mix_g=norm_mix_g, norm_mem_g=norm_mem_g, w_in=w_in, w_gate=w_gate, ret_decay_fwd=ret_decay_fwd,
             ret_decay_bwd=ret_decay_bwd, ret_norm_g=ret_norm_g, pool_w=pool_w, pool_scale=pool_scale, na_q_norm_g=na_q_norm_g,
             na_k_norm_g=na_k_norm_g, na_rpb=na_rpb, mem_q_norm_g=mem_q_norm_g, mem_k_norm_g=mem_k_norm_g, w_mem_kv=w_mem_kv,
             w_branch=w_branch, w_out=w_out, norm_ffn_g=norm_ffn_g, w_ffn_in=w_ffn_in, w_ffn_out=w_ffn_out)
    m = dict(norm_mix_g=m_norm_mix_g, norm_mem_g=m_norm_mem_g, w_in=m_w_in, w_gate=m_w_gate, ret_decay_fwd=m_ret_decay_fwd,
             ret_decay_bwd=m_ret_decay_bwd, ret_norm_g=m_ret_norm_g, pool_w=m_pool_w, pool_scale=m_pool_scale, na_q_norm_g=m_na_q_norm_g,
             na_k_norm_g=m_na_k_norm_g, na_rpb=m_na_rpb, mem_q_norm_g=m_mem_q_norm_g, mem_k_norm_g=m_mem_k_norm_g, w_mem_kv=m_w_mem_kv,
             w_branch=m_w_branch, w_out=m_w_out, norm_ffn_g=m_norm_ffn_g, w_ffn_in=m_w_ffn_in, w_ffn_out=m_w_ffn_out)
    v = dict(norm_mix_g=v_norm_mix_g, norm_mem_g=v_norm_mem_g, w_in=v_w_in, w_gate=v_w_gate, ret_decay_fwd=v_ret_decay_fwd,
             ret_decay_bwd=v_ret_decay_bwd, ret_norm_g=v_ret_norm_g, pool_w=v_pool_w, pool_scale=v_pool_scale, na_q_norm_g=v_na_q_norm_g,
             na_k_norm_g=v_na_k_norm_g, na_rpb=v_na_rpb, mem_q_norm_g=v_mem_q_norm_g, mem_k_norm_g=v_mem_k_norm_g, w_mem_kv=v_w_mem_kv,
             w_branch=v_w_branch, w_out=v_w_out, norm_ffn_g=v_norm_ffn_g, w_ffn_in=v_w_ffn_in, w_ffn_out=v_w_ffn_out)
    assert x.shape == (1, 2048, D) and mem.shape == (1, N_MEM, D) and w_in.shape == (DEPTH, D, 9 * BW // N_DEV)

    started = []
    for l in range(DEPTH):
        blocks = [_to_exchange(name, tr, w[name][l]).astype(BF16) for name, tr in BIG]
        lands = [lax.empty((N_DEV,) + b.shape, BF16) for b in blocks]
        started.append(_ici_start(blocks, lands, False, "gather_ici_start_%d" % l))
    all_started = started[0][4] + started[1][4] + started[2][4] + started[3][4]

    def get_layer(l, after):
        lands = _ici_wait(started[l], all_started if l == 0 else after, "gather_ici_wait_%d" % l)
        whole = _gather_d2d(started[l][2], lands, "gather_d2d")
        return {name: _whole_from_gathered(name, g) for (name, _), g in zip(BIG, whole)}

    cidx = lax.axis_index("c").astype(jnp.int32).reshape(1)
    chip = (2 * lax.axis_index("x") + lax.axis_index("y")).astype(jnp.int32).reshape(1)
    in_flight = []

    def flip_of(name, tr):
        return (lambda a: jnp.swapaxes(a, 1, 2)) if (tr and name != "w_branch") else (lambda a: a)

    def rows3(a):
        return a.reshape(DEPTH, -1, a.shape[-1])

    opt_in = {name: tuple(rows3(flip_of(name, tr)(t[name])) for t in (w, m, v)) for name, tr in BIG}
    opt_out = {name: tuple(lax.empty(opt_in[name][0].shape, F32) for _ in range(4)) for name, _ in BIG}

    def finish(l, st, after):
        recv = _ici_wait(st, after, "rs_ici_wait_%d" % l)
        sums = _sum_own(st[2], recv, chip, "rs_chip_sum")
        for (name, tr), s in zip(BIG, sums):
            g = _from_exchange(name, tr, s) if name == "w_branch" else s
            wx, mx, vx = opt_in[name]
            opt_out[name] = _adamw_layer(l, wx, g.reshape(-1, g.shape[-1]), mx, vx, opt_out[name], "adamw_" + name)

    def on_grads(l, gb, after):
        send = [_by_destination(name, gb[name]) for name, _ in BIG]
        from_core = _rs_core_swap(send, "rs_core_swap")
        chip_part = _pair_sum(send, from_core, cidx)
        st = _ici_start(chip_part, [lax.empty(p.shape, BF16) for p in chip_part], True, "rs_ici_start_%d" % l)
        in_flight.append((l, st))
        return st[4]

    loss_local, dx, small_g = _local_step(x[0], mem[0], loss_target[0], {n: w[n] for n in SMALL}, get_layer, on_grads)

    last_started = in_flight[-1][1][4]
    for l, st in in_flight[:-1]:
        finish(l, st, last_started)

    small_all, = _all_gather([_pack_small(small_g, loss_local) + last_started[0:1]], "gather_small")
    packed_g = _sum_slots(small_all, "small_sum")
    small_sum, loss = _unpack_small(packed_g, {n: w[n] for n in SMALL})
    d_, m_, v_ = _adamw(_pack_small({n: w[n] for n in SMALL}), packed_g, _pack_small({n: m[n] for n in SMALL}),
                        _pack_small({n: v[n] for n in SMALL}), "adamw_small")
    updated = d_[0:8]
    for name, _ in BIG:
        updated = updated + opt_out[name][0][1, 0:8, 0:128]
    finish(*in_flight[-1], updated)

    grads, delta, new_m, new_v = {}, {}, {}, {}
    for name, tr in BIG:
        shape = flip_of(name, tr)(w[name]).shape
        delta[name], new_m[name], new_v[name], grads[name] = (flip_of(name, tr)(a.reshape(shape)) for a in opt_out[name])
    like = {n: w[n] for n in SMALL}
    ds, _ = _unpack_small(d_, like)
    ms, _ = _unpack_small(m_, like)
    vs, _ = _unpack_small(v_, like)
    for n in SMALL:
        grads[n], delta[n], new_m[n], new_v[n] = small_sum[n], ds[n], ms[n], vs[n]

    return (loss, dx[None], *[grads[n] for n in WEIGHTS], *[delta[n] for n in WEIGHTS], *[new_m[n] for n in WEIGHTS],
            *[new_v[n] for n in WEIGHTS])
```

```python
import functools

import numpy as np
import jax
import jax.numpy as jnp
from jax import lax
from jax.experimental import pallas as pl
from jax.experimental.pallas import tpu as pltpu

F32 = jnp.float32
BF16 = jnp.bfloat16
MXU = jnp.bfloat16
HI = lax.Precision.HIGHEST

DEPTH = 4
D = 1024
BW = 256
HD = 64
NH = 4
GRID_W = 64
NA_ROWS_WIN = 8
NA_COLS_WIN = 16
N_MEM = 256
FF = 2816
EPS = 1e-6
NEG = -1e30
ROPE_THETA = 10000.0
POOL_HALF_MAX = 8

ADAM_LR, ADAM_B1, ADAM_B2, ADAM_EPS, ADAM_WD, ADAM_STEP = 0.001, 0.9, 0.999, 1e-08, 0.01, 10

N_DEV = 8
VMEM_LIMIT = 56 * 1024 * 1024

RQ, RK, RV, RG, PV, NQ, NK, NV, MQ = range(9)

MESH = pl.DeviceIdType.MESH
ANY = pl.BlockSpec(memory_space=pl.ANY)
SMEM = pl.BlockSpec(memory_space=pltpu.SMEM)


def _cp(**kw):
    return pltpu.CompilerParams(vmem_limit_bytes=VMEM_LIMIT, **kw)


def _tile(n, cap):
    if n <= cap:
        return n
    best = None
    for t in range(128, cap + 1, 128):
        if n % t == 0:
            best = t
    assert best is not None, (n, cap)
    return best


def _sds(shape, dtype):
    return jax.ShapeDtypeStruct(shape, dtype)


def _lane_head(shape):
    return lax.shift_right_logical(lax.broadcasted_iota(jnp.int32, shape, len(shape) - 1), 6)


def _group_mean(z):
    i = lax.shift_right_logical(lax.broadcasted_iota(jnp.int32, (BW, BW), 0), 6)
    j = lax.shift_right_logical(lax.broadcasted_iota(jnp.int32, (BW, BW), 1), 6)
    g = jnp.where(i == j, 1.0 / HD, 0.0).astype(F32)
    return jnp.dot(z, g, precision=HI, preferred_element_type=F32)


def _gnorm(t, g):
    r = lax.rsqrt(_group_mean(t * t) + EPS)
    return t * r * g


def _gnorm_bwd(dy, t, g):
    r = lax.rsqrt(_group_mean(t * t) + EPS)
    th = t * r
    dth = dy * g
    dt = r * (dth - th * _group_mean(dth * th))
    return dt, dy * th


def _swap_halves(t):
    lane = lax.broadcasted_iota(jnp.int32, t.shape, 1)
    return jnp.where((lane & 63) < 32, pltpu.roll(t, BW - 32, 1), pltpu.roll(t, 32, 1))


def _sigmoid(x):
    return 1.0 / (1.0 + jnp.exp(-x))


def _dot(a, b, ta=False, tb=False):
    return lax.dot_general(a.astype(MXU), b.astype(MXU), (((0 if ta else 1,), (1 if tb else 0,)), ((), ())),
                           preferred_element_type=F32)


def _stack_heads(t):
    head = _lane_head(t.shape)
    return jnp.concatenate([jnp.where(head == h, t, jnp.zeros_like(t)) for h in range(NH)], axis=0)


def _unstack_heads(t, rows):
    head = _lane_head((rows, BW))
    out = jnp.zeros((rows, BW), F32)
    for h in range(NH):
        out = out + jnp.where(head == h, t[h * rows:(h + 1) * rows], 0.0)
    return out


def _softmax_rows(s):
    m = jnp.max(s, axis=-1, keepdims=True)
    e = jnp.exp(s - m)
    return e / jnp.sum(e, axis=-1, keepdims=True)


def _acc(ref, val, first):
    @pl.when(first)
    def _():
        ref[...] = val

    @pl.when(jnp.logical_not(first))
    def _():
        ref[...] += val


def _mm(a, b, *, ta=False, tb=False, out_dtype=F32, add=None, dep=None, name):
    m, k = (a.shape[1], a.shape[0]) if ta else a.shape
    n = b.shape[0] if tb else b.shape[1]
    tm, tn = _tile(m, 1408), _tile(n, 512)

    def body(*refs):
        if add is None:
            a_ref, b_ref, o_ref = refs[:2] + refs[-1:]
            r = _dot(a_ref[...], b_ref[...], ta, tb)
        else:
            a_ref, b_ref, c_ref, o_ref = refs[:3] + refs[-1:]
            r = _dot(a_ref[...], b_ref[...], ta, tb) + c_ref[...]
        o_ref[...] = r.astype(out_dtype)

    a_spec = pl.BlockSpec((k, tm), lambda i, j: (0, i)) if ta else pl.BlockSpec((tm, k), lambda i, j: (i, 0))
    b_spec = pl.BlockSpec((tn, k), lambda i, j: (j, 0)) if tb else pl.BlockSpec((k, tn), lambda i, j: (0, j))
    o_spec = pl.BlockSpec((tm, tn), lambda i, j: (i, j))
    ins, args = [a_spec, b_spec], [a, b]
    if add is not None:
        ins.append(o_spec)
        args.append(add)
    if dep is not None:
        ins.append(pl.BlockSpec((8, 128), lambda i, j: (0, 0)))
        args.append(dep)
    return pl.pallas_call(
        body, grid=(m // tm, n // tn), in_specs=ins, out_specs=o_spec, out_shape=_sds((m, n), out_dtype), name=name,
        compiler_params=_cp(dimension_semantics=("parallel", "parallel")))(*args)


def _rmsnorm_fwd(x, g, name):
    t, d = x.shape
    tm = _tile(t, 256)

    def body(x_ref, g_ref, o_ref):
        xv = x_ref[...]
        r = lax.rsqrt(jnp.mean(xv * xv, axis=-1, keepdims=True) + EPS)
        o_ref[...] = (xv * r * g_ref[...]).astype(o_ref.dtype)

    return pl.pallas_call(
        body, grid=(t // tm,), in_specs=[pl.BlockSpec((tm, d), lambda i: (i, 0)), pl.BlockSpec((1, d), lambda i: (0, 0))],
        out_specs=pl.BlockSpec((tm, d), lambda i: (i, 0)), out_shape=_sds((t, d), BF16), name=name, compiler_params=_cp())(x, g)


def _rmsnorm_bwd(dh, x, g, res, name):
    t, d = x.shape
    tm = _tile(t, 256)

    def body(dh_ref, x_ref, g_ref, res_ref, dx_ref, dxb_ref, dg_ref):
        xv = x_ref[...]
        dhv = dh_ref[...]
        r = lax.rsqrt(jnp.mean(xv * xv, axis=-1, keepdims=True) + EPS)
        xh = xv * r
        dxh = dhv * g_ref[...]
        dx = res_ref[...] + r * (dxh - xh * jnp.mean(dxh * xh, axis=-1, keepdims=True))
        dx_ref[...] = dx
        dxb_ref[...] = dx.astype(BF16)
        _acc(dg_ref, jnp.sum(dhv * xh, axis=0, keepdims=True), pl.program_id(0) == 0)

    row = pl.BlockSpec((tm, d), lambda i: (i, 0))
    vec = pl.BlockSpec((1, d), lambda i: (0, 0))
    return pl.pallas_call(
        body, grid=(t // tm,), in_specs=[row, row, vec, row], out_specs=(row, row, vec),
        out_shape=(_sds((t, d), F32), _sds((t, d), BF16), _sds((1, d), F32)), name=name, compiler_params=_cp())(dh, x, g, res)


def _prep_fwd(proj, cos2, sin2, g_naq, g_nak, g_mq):
    t = proj.shape[0]
    tm = 256

    def body(p_ref, cos_ref, sin_ref, gq_ref, gk_ref, gm_ref, rq_ref, rk_ref, rv_ref, nq_ref, nk_ref, nv_ref, mq_ref):
        def col(c):
            return p_ref[:, c * BW:(c + 1) * BW]

        cosv, sinv = cos_ref[...], sin_ref[...]

        def rot(tv):
            return tv * cosv + _swap_halves(tv) * sinv

        rq_ref[...] = (rot(col(RQ)) * (HD ** -0.5)).astype(BF16)
        rk_ref[...] = rot(col(RK)).astype(BF16)
        rv_ref[...] = col(RV).astype(BF16)
        nq_ref[...] = _gnorm(col(NQ), gq_ref[...]).astype(BF16)
        nk_ref[...] = _gnorm(col(NK), gk_ref[...]).astype(BF16)
        nv_ref[...] = col(NV).astype(BF16)
        mq_ref[...] = _gnorm(col(MQ), gm_ref[...]).astype(BF16)

    blk = pl.BlockSpec((tm, BW), lambda i: (i, 0))
    vec = pl.BlockSpec((1, BW), lambda i: (0, 0))
    return pl.pallas_call(
        body, grid=(t // tm,), in_specs=[pl.BlockSpec((tm, 9 * BW), lambda i: (i, 0)), blk, blk, vec, vec, vec],
        out_specs=tuple(blk for _ in range(7)), out_shape=tuple(_sds((t, BW), BF16) for _ in range(7)),
        name="prep_fwd", compiler_params=_cp())(proj, cos2, sin2, g_naq, g_nak, g_mq)


def _prep_bwd(proj, cos2, sin2, g_naq, g_nak, g_mq, d_rq, d_rk, d_rv, d_rg, d_pv, d_nq, d_nk, d_nv, d_mq):
    t = proj.shape[0]
    tm = 256

    def body(p_ref, cos_ref, sin_ref, gq_ref, gk_ref, gm_ref, drq_ref, drk_ref, drv_ref, drg_ref, dpv_ref, dnq_ref, dnk_ref,
             dnv_ref, dmq_ref, o_ref, dgq_ref, dgk_ref, dgm_ref):
        first = pl.program_id(0) == 0

        def col(c):
            return p_ref[:, c * BW:(c + 1) * BW]

        def put(c, v):
            o_ref[:, c * BW:(c + 1) * BW] = v.astype(BF16)

        cosv, sinv = cos_ref[...], sin_ref[...]

        def rot_t(dv):
            return dv * cosv + _swap_halves(dv * sinv)

        put(RQ, rot_t(drq_ref[...] * (HD ** -0.5)))
        put(RK, rot_t(drk_ref[...]))
        put(RV, drv_ref[...])
        put(RG, drg_ref[...])
        put(PV, dpv_ref[...])
        dq, gq = _gnorm_bwd(dnq_ref[...], col(NQ), gq_ref[...])
        put(NQ, dq)
        _acc(dgq_ref, jnp.sum(gq, axis=0, keepdims=True), first)
        dk, gk = _gnorm_bwd(dnk_ref[...], col(NK), gk_ref[...])
        put(NK, dk)
        _acc(dgk_ref, jnp.sum(gk, axis=0, keepdims=True), first)
        put(NV, dnv_ref[...])
        dm, gm = _gnorm_bwd(dmq_ref[...], col(MQ), gm_ref[...])
        put(MQ, dm)
        _acc(dgm_ref, jnp.sum(gm, axis=0, keepdims=True), first)

    blk = pl.BlockSpec((tm, BW), lambda i: (i, 0))
    vec = pl.BlockSpec((1, BW), lambda i: (0, 0))
    wide = pl.BlockSpec((tm, 9 * BW), lambda i: (i, 0))
    return pl.pallas_call(
        body, grid=(t // tm,), in_specs=[wide, blk, blk, vec, vec, vec] + [blk] * 9, out_specs=(wide, vec, vec, vec),
        out_shape=(_sds((t, 9 * BW), BF16), _sds((1, BW), F32), _sds((1, BW), F32), _sds((1, BW), F32)),
        name="prep_bwd", compiler_params=_cp())(proj, cos2, sin2, g_naq, g_nak, g_mq, d_rq, d_rk, d_rv, d_rg, d_pv, d_nq, d_nk,
                                                d_nv, d_mq)


RET_B = 256


def _ret_consts(lgf_ref, lgb_ref):
    bsz = RET_B
    head = _lane_head((1, BW))
    lf, lb = jnp.zeros((1, BW), F32), jnp.zeros((1, BW), F32)
    for h in range(NH):
        lf = lf + jnp.where(head == h, lgf_ref[h], 0.0)
        lb = lb + jnp.where(head == h, lgb_ref[h], 0.0)
    pos = lax.broadcasted_iota(jnp.int32, (bsz, BW), 0).astype(F32)
    up, down = pos + 1.0, (bsz - 1.0) - pos
    c = dict(up=up, down=down, kf=jnp.exp(down * lf), kb=jnp.exp(up * lb), qf=jnp.exp(up * lf), qb=jnp.exp(down * lb),
             cf=jnp.exp(bsz * lf), cb=jnp.exp(bsz * lb))
    diff = (lax.broadcasted_iota(jnp.int32, (NH * bsz, 1), 0) & (bsz - 1)) - lax.broadcasted_iota(jnp.int32, (1, bsz), 1)
    c["causal"] = diff >= 0
    c["dist"] = jnp.abs(diff).astype(F32)
    lgf = jnp.concatenate([jnp.full((bsz, 1), lgf_ref[h], F32) for h in range(NH)], axis=0)
    lgb = jnp.concatenate([jnp.full((bsz, 1), lgb_ref[h], F32) for h in range(NH)], axis=0)
    c["dm"] = jnp.exp(c["dist"] * jnp.where(c["causal"], lgf, lgb))
    c["bd"] = _lane_head((BW, BW)) == lax.shift_right_logical(lax.broadcasted_iota(jnp.int32, (BW, BW), 0), 6)
    return c


def _ret_states(k_ref, v_ref, st_ref, c, nb):
    bsz = RET_B

    def summary(b, decay):
        kb = k_ref[b * bsz:(b + 1) * bsz, :].astype(F32)
        return jnp.where(c["bd"], _dot(kb * decay, v_ref[b * bsz:(b + 1) * bsz, :], ta=True), 0.0)

    f = jnp.zeros((BW, BW), F32)
    for b in range(nb):
        st_ref[b] = f
        if b < nb - 1:
            f = c["cf"] * f + summary(b, c["kf"])
    g = jnp.zeros((BW, BW), F32)
    for b in reversed(range(nb)):
        st_ref[nb + b] = g
        if b > 0:
            g = c["cb"] * g + summary(b, c["kb"])


def _ret_fwd(q, k, v, proj, lgf, lgb, g_ret):
    t = q.shape[0]
    bsz, nb = RET_B, t // RET_B

    def body(lgf_ref, lgb_ref, q_ref, k_ref, v_ref, rg_ref, g_ref, o_ref, ret_ref, st_ref):
        c = _ret_consts(lgf_ref, lgb_ref)
        _ret_states(k_ref, v_ref, st_ref, c, nb)
        for b in range(nb):
            blk = slice(b * bsz, (b + 1) * bsz)
            qb, kb, vb = q_ref[blk, :], k_ref[blk, :], v_ref[blk, :]
            s = _dot(_stack_heads(qb), kb, tb=True)
            o = _unstack_heads(_dot(s * c["dm"], vb), bsz)
            q32 = qb.astype(F32)
            o = o + _dot(q32 * c["qf"], st_ref[b]) + _dot(q32 * c["qb"], st_ref[nb + b])
            o_ref[blk, :] = o
            rg = rg_ref[blk, :]
            ret_ref[blk, :] = (_gnorm(o, g_ref[...]) * (rg * _sigmoid(rg))).astype(BF16)

    whole = pl.BlockSpec((t, BW), lambda i: (0, 0))
    return pl.pallas_call(
        body, grid=(1,),
        in_specs=[SMEM, SMEM, whole, whole, whole, pl.BlockSpec((t, BW), lambda i: (0, RG)), pl.BlockSpec((1, BW), lambda i: (0, 0))],
        out_specs=(whole, whole), out_shape=(_sds((t, BW), F32), _sds((t, BW), BF16)),
        scratch_shapes=[pltpu.VMEM((2 * nb, BW, BW), F32)], name="ret_fwd", compiler_params=_cp())(lgf, lgb, q, k, v, proj, g_ret)


def _ret_post_bwd(dbr, o_ret, proj, g_ret):
    t = o_ret.shape[0]
    tm = 256

    def body(d_ref, o_ref, rg_ref, g_ref, do_ref, drg_ref, dg_ref):
        dret, o, rg, g = d_ref[...], o_ref[...], rg_ref[...], g_ref[...]
        sg = _sigmoid(rg)
        do, dgain = _gnorm_bwd(dret * (rg * sg), o, g)
        do_ref[...] = do.astype(BF16)
        drg_ref[...] = dret * _gnorm(o, g) * (sg * (1.0 + rg * (1.0 - sg)))
        _acc(dg_ref, jnp.sum(dgain, axis=0, keepdims=True), pl.program_id(0) == 0)

    blk = pl.BlockSpec((tm, BW), lambda i: (i, 0))
    vec = pl.BlockSpec((1, BW), lambda i: (0, 0))
    return pl.pallas_call(
        body, grid=(t // tm,), in_specs=[blk, blk, pl.BlockSpec((tm, BW), lambda i: (i, RG)), vec], out_specs=(blk, blk, vec),
        out_shape=(_sds((t, BW), BF16), _sds((t, BW), F32), _sds((1, BW), F32)), name="ret_post_bwd",
        compiler_params=_cp())(dbr, o_ret, proj, g_ret)


def _ret_bwd(do, q, k, v, lgf, lgb):
    t = q.shape[0]
    bsz, nb = RET_B, t // RET_B

    def body(lgf_ref, lgb_ref, d_ref, q_ref, k_ref, v_ref, dq_ref, dk_ref, dv_ref, dlg_ref, st_ref, sd_ref):
        c = _ret_consts(lgf_ref, lgb_ref)
        _ret_states(k_ref, v_ref, st_ref, c, nb)
        lane_f, lane_b = jnp.zeros((1, BW), F32), jnp.zeros((1, BW), F32)
        row_f, row_b = jnp.zeros((NH * bsz, 1), F32), jnp.zeros((NH * bsz, 1), F32)

        def rows(x):
            return jnp.sum(x, axis=0, keepdims=True)

        for b in range(nb):
            blk = slice(b * bsz, (b + 1) * bsz)
            qb, kb, vb, dob = q_ref[blk, :], k_ref[blk, :], v_ref[blk, :], d_ref[blk, :]
            q32 = qb.astype(F32)
            qs, dos = _stack_heads(qb), _stack_heads(dob)
            s = _dot(qs, kb, tb=True)
            da = _dot(dos, vb, tb=True)
            dv_ref[blk, :] = _dot(s * c["dm"], dos, ta=True)
            ds = da * c["dm"]
            w = ds * s * c["dist"]
            row_f = row_f + jnp.sum(jnp.where(c["causal"], w, 0.0), axis=1, keepdims=True)
            row_b = row_b + jnp.sum(jnp.where(c["causal"], 0.0, w), axis=1, keepdims=True)
            dsb = ds.astype(MXU)
            dk_ref[blk, :] = _dot(dsb, qs, ta=True)
            dq_f = _dot(dob, st_ref[b], tb=True) * c["qf"]
            dq_b = _dot(dob, st_ref[nb + b], tb=True) * c["qb"]
            lane_f = lane_f + rows(c["up"] * dq_f * q32)
            lane_b = lane_b + rows(c["down"] * dq_b * q32)
            dq_ref[blk, :] = _unstack_heads(_dot(dsb, kb), bsz) + dq_f + dq_b
            sd_ref[b] = jnp.where(c["bd"], _dot(q32 * c["qf"], dob, ta=True), 0.0)
            sd_ref[nb + b] = jnp.where(c["bd"], _dot(q32 * c["qb"], dob, ta=True), 0.0)

        def through_state(b, grad, decay, weight, lane):
            blk = slice(b * bsz, (b + 1) * bsz)
            k32 = k_ref[blk, :].astype(F32)
            dk = _dot(v_ref[blk, :], grad, tb=True) * decay
            dk_ref[blk, :] += dk
            dv_ref[blk, :] += _dot(k32 * decay, grad)
            return lane + rows(weight * dk * k32)

        phi = jnp.zeros((BW, BW), F32)
        for b in reversed(range(nb)):
            if b < nb - 1:
                lane_f = through_state(b, phi, c["kf"], c["down"], lane_f)
                lane_f = lane_f + bsz * rows(c["cf"] * st_ref[b] * phi)
            phi = sd_ref[b] + c["cf"] * phi
        gam = jnp.zeros((BW, BW), F32)
        for b in range(nb):
            if b > 0:
                lane_b = through_state(b, gam, c["kb"], c["up"], lane_b)
                lane_b = lane_b + bsz * rows(c["cb"] * st_ref[nb + b] * gam)
            gam = sd_ref[nb + b] + c["cb"] * gam

        head = _lane_head((1, BW))
        for h in range(NH):
            tot_f = jnp.sum(row_f[h * bsz:(h + 1) * bsz, :]) + jnp.sum(jnp.where(head == h, lane_f, 0.0))
            tot_b = jnp.sum(row_b[h * bsz:(h + 1) * bsz, :]) + jnp.sum(jnp.where(head == h, lane_b, 0.0))
            dlg_ref[h:h + 1, :] = jnp.full((1, 128), tot_f, F32)
            dlg_ref[NH + h:NH + h + 1, :] = jnp.full((1, 128), tot_b, F32)

    whole = pl.BlockSpec((t, BW), lambda i: (0, 0))
    return pl.pallas_call(
        body, grid=(1,), in_specs=[SMEM, SMEM, whole, whole, whole, whole],
        out_specs=(whole, whole, whole, pl.BlockSpec((2 * NH, 128), lambda i: (0, 0))),
        out_shape=(_sds((t, BW), F32), _sds((t, BW), F32), _sds((t, BW), F32), _sds((2 * NH, 128), F32)),
        scratch_shapes=[pltpu.VMEM((2 * nb, BW, BW), F32), pltpu.VMEM((2 * nb, BW, BW), F32)], name="ret_bwd",
        compiler_params=_cp())(lgf, lgb, do, q, k, v)


def _pool_windows(t):
    row = lax.broadcasted_iota(jnp.int32, (t, BW), 0)
    half = lax.shift_left(jnp.ones((t, BW), jnp.int32), _lane_head((t, BW)))
    cnt = (jnp.minimum(row + half, t) - jnp.maximum(row - half, 0)).astype(F32)
    return row, half, cnt


def _pool_window_sum(v, row, half, t, transpose):
    out = jnp.zeros_like(v)
    for j in range(-POOL_HALF_MAX, POOL_HALF_MAX):
        src = row - j if transpose else row + j
        ok = (src >= 0) & (src < t) & (j >= -half) & (j < half)
        out = out + jnp.where(ok, pltpu.roll(v, (j if transpose else -j) % t, 0), 0.0)
    return out


def _pool_fwd(proj, wbd, scale):
    t = proj.shape[0]

    def body(v_ref, w_ref, s_ref, o_ref):
        v = v_ref[...]
        row, half, cnt = _pool_windows(t)
        pooled = _pool_window_sum(v, row, half, t, False) / cnt - v
        o_ref[...] = (_dot(pooled, w_ref[...]) * s_ref[...]).astype(BF16)

    return pl.pallas_call(
        body, grid=(1,),
        in_specs=[pl.BlockSpec((t, BW), lambda i: (0, PV)), pl.BlockSpec((BW, BW), lambda i: (0, 0)), pl.BlockSpec((1, BW), lambda i: (0, 0))],
        out_specs=pl.BlockSpec((t, BW), lambda i: (0, 0)), out_shape=_sds((t, BW), BF16), name="pool_fwd",
        compiler_params=_cp())(proj, wbd, scale)


def _pool_bwd(dbr, proj, wbd, scale):
    t = proj.shape[0]

    def body(d_ref, v_ref, w_ref, s_ref, dv_ref, dw_ref, ds_ref):
        v, dout = v_ref[...], d_ref[...]
        row, half, cnt = _pool_windows(t)
        pooled = _pool_window_sum(v, row, half, t, False) / cnt - v
        mixed = _dot(pooled, w_ref[...])
        ds_ref[...] = jnp.sum(dout * mixed, axis=0, keepdims=True)
        dmixed = dout * s_ref[...]
        dw_ref[...] = _dot(pooled, dmixed, ta=True)
        dpooled = _dot(dmixed, w_ref[...], tb=True)
        dv_ref[...] = _pool_window_sum(dpooled / cnt, row, half, t, True) - dpooled

    return pl.pallas_call(
        body, grid=(1,),
        in_specs=[pl.BlockSpec((t, BW), lambda i: (0, 1)), pl.BlockSpec((t, BW), lambda i: (0, PV)),
                  pl.BlockSpec((BW, BW), lambda i: (0, 0)), pl.BlockSpec((1, BW), lambda i: (0, 0))],
        out_specs=(pl.BlockSpec((t, BW), lambda i: (0, 0)), pl.BlockSpec((BW, BW), lambda i: (0, 0)), pl.BlockSpec((1, BW), lambda i: (0, 0))),
        out_shape=(_sds((t, BW), F32), _sds((BW, BW), F32), _sds((1, BW), F32)), name="pool_bwd",
        compiler_params=_cp())(dbr, proj, wbd, scale)


NA_KEYS = NA_ROWS_WIN * GRID_W


def _na_window(r, n_rows):
    rs = jnp.clip(r - NA_ROWS_WIN // 2, 0, n_rows - NA_ROWS_WIN)
    return pl.multiple_of(rs * GRID_W, GRID_W), rs - r + (NA_ROWS_WIN - 1)


NA_STEP_ROWS = 4


def _na_fwd(q, k, v, ball):
    t = q.shape[0]
    n_rows = t // GRID_W
    rows = NA_STEP_ROWS

    def body(q_ref, k_ref, v_ref, b_ref, o_ref):
        for rr in range(rows):
            start, a0 = _na_window(pl.program_id(0) * rows + rr, n_rows)
            own = slice(rr * GRID_W, (rr + 1) * GRID_W)
            qs = _stack_heads(q_ref[own, :])
            s = _dot(qs, k_ref[pl.ds(start, NA_KEYS), :], tb=True) * (HD ** -0.5) + b_ref[a0]
            p = _softmax_rows(s)
            o_ref[own, :] = _unstack_heads(_dot(p, v_ref[pl.ds(start, NA_KEYS), :]), GRID_W).astype(BF16)

    blk = pl.BlockSpec((rows * GRID_W, BW), lambda r: (r, 0))
    whole = pl.BlockSpec((t, BW), lambda r: (0, 0))
    return pl.pallas_call(
        body, grid=(n_rows // rows,), in_specs=[blk, whole, whole, pl.BlockSpec(ball.shape, lambda r: (0, 0, 0))],
        out_specs=blk, out_shape=_sds((t, BW), BF16), name="na_fwd", compiler_params=_cp())(q, k, v, ball)


def _na_bwd(dbr, q, k, v, ball):
    t = q.shape[0]
    n_rows = t // GRID_W

    rows = NA_STEP_ROWS

    def body(d_ref, q_ref, k_ref, v_ref, b_ref, dq_ref, dk_ref, dv_ref, db_ref):
        @pl.when(pl.program_id(0) == 0)
        def _():
            dk_ref[...] = jnp.zeros_like(dk_ref)
            dv_ref[...] = jnp.zeros_like(dv_ref)
            db_ref[...] = jnp.zeros_like(db_ref)

        for rr in range(rows):
            start, a0 = _na_window(pl.program_id(0) * rows + rr, n_rows)
            keys = pl.ds(start, NA_KEYS)
            own = slice(rr * GRID_W, (rr + 1) * GRID_W)
            qs = _stack_heads(q_ref[own, :])
            kb, vb = k_ref[keys, :], v_ref[keys, :]
            p = _softmax_rows(_dot(qs, kb, tb=True) * (HD ** -0.5) + b_ref[a0])
            dos = _stack_heads(d_ref[own, :]).astype(MXU)
            dp = _dot(dos, vb, tb=True)
            dv_ref[keys, :] += _dot(p, dos, ta=True)
            ds = p * (dp - jnp.sum(dp * p, axis=-1, keepdims=True))
            db_ref[a0] += ds
            dsb = (ds * (HD ** -0.5)).astype(MXU)
            dq_ref[own, :] = _unstack_heads(_dot(dsb, kb), GRID_W)
            dk_ref[keys, :] += _dot(dsb, qs, ta=True)

    blk = pl.BlockSpec((rows * GRID_W, BW), lambda r: (r, 0))
    whole = pl.BlockSpec((t, BW), lambda r: (0, 0))
    tab = pl.BlockSpec(ball.shape, lambda r: (0, 0, 0))
    return pl.pallas_call(
        body, grid=(n_rows // rows,), in_specs=[pl.BlockSpec((rows * GRID_W, BW), lambda r: (r, 2)), blk, whole, whole, tab],
        out_specs=(blk, whole, whole, tab),
        out_shape=(_sds((t, BW), F32), _sds((t, BW), F32), _sds((t, BW), F32), _sds(ball.shape, F32)), name="na_bwd",
        compiler_params=_cp())(dbr, q, k, v, ball)


def _rpb_expand(rpb_pad, onehot):
    def body(r_ref, e_ref, o_ref):
        o_ref[...] = jnp.dot(r_ref[...], e_ref[...], precision=HI, preferred_element_type=F32)

    return pl.pallas_call(body, out_shape=_sds((64, GRID_W * GRID_W), F32), name="rpb_expand", compiler_params=_cp())(rpb_pad, onehot)


def _rpb_reduce(dtab, onehot):
    def body(d_ref, e_ref, o_ref):
        o_ref[...] = lax.dot_general(d_ref[...], e_ref[...], (((1,), (1,)), ((), ())), precision=HI, preferred_element_type=F32)

    return pl.pallas_call(body, out_shape=_sds((64, 128), F32), name="rpb_reduce", compiler_params=_cp())(dtab, onehot)


MEM_TQ = 256


def _mem_fwd(q, mk, mv):
    t = q.shape[0]
    tq = MEM_TQ

    def body(q_ref, k_ref, v_ref, o_ref):
        qv = q_ref[...]
        head = _lane_head(qv.shape)
        out = jnp.zeros((tq, BW), F32)
        for h in range(NH):
            p = _softmax_rows(_dot(jnp.where(head == h, qv, jnp.zeros_like(qv)), k_ref[...], tb=True) * (HD ** -0.5))
            out = out + jnp.where(head == h, _dot(p, v_ref[...]), 0.0)
        o_ref[...] = out.astype(BF16)

    blk = pl.BlockSpec((tq, BW), lambda i: (i, 0))
    kv = pl.BlockSpec((N_MEM, BW), lambda i: (0, 0))
    return pl.pallas_call(body, grid=(t // tq,), in_specs=[blk, kv, kv], out_specs=blk, out_shape=_sds((t, BW), BF16),
                          name="mem_fwd", compiler_params=_cp())(q, mk, mv)


def _mem_bwd(dbr, q, mk, mv):
    t = q.shape[0]
    tq = MEM_TQ

    def body(d_ref, q_ref, k_ref, v_ref, dq_ref, dk_ref, dv_ref):
        first = pl.program_id(0) == 0
        qv, dout = q_ref[...], d_ref[...]
        head = _lane_head(qv.shape)
        dq = jnp.zeros((tq, BW), F32)
        dk = jnp.zeros((N_MEM, BW), F32)
        dv = jnp.zeros((N_MEM, BW), F32)
        for h in range(NH):
            qh = jnp.where(head == h, qv, jnp.zeros_like(qv))
            doh = jnp.where(head == h, dout, 0.0).astype(MXU)
            p = _softmax_rows(_dot(qh, k_ref[...], tb=True) * (HD ** -0.5))
            dp = _dot(doh, v_ref[...], tb=True)
            dv = dv + _dot(p, doh, ta=True)
            dsb = (p * (dp - jnp.sum(dp * p, axis=-1, keepdims=True)) * (HD ** -0.5)).astype(MXU)
            dq = dq + jnp.where(head == h, _dot(dsb, k_ref[...]), 0.0)
            dk = dk + _dot(dsb, qh, ta=True)
        dq_ref[...] = dq
        _acc(dk_ref, dk, first)
        _acc(dv_ref, dv, first)

    blk = pl.BlockSpec((tq, BW), lambda i: (i, 0))
    kv = pl.BlockSpec((N_MEM, BW), lambda i: (0, 0))
    return pl.pallas_call(
        body, grid=(t // tq,), in_specs=[pl.BlockSpec((tq, BW), lambda i: (i, 3)), blk, kv, kv], out_specs=(blk, kv, kv),
        out_shape=(_sds((t, BW), F32), _sds((N_MEM, BW), F32), _sds((N_MEM, BW), F32)), name="mem_bwd",
        compiler_params=_cp())(dbr, q, mk, mv)


def _memkv_prep(kv, g_mk):
    def body(kv_ref, g_ref, k_ref, v_ref):
        k_ref[...] = _gnorm(kv_ref[:, 0:BW], g_ref[...]).astype(BF16)
        v_ref[...] = kv_ref[:, BW:2 * BW].astype(BF16)

    return pl.pallas_call(body, out_shape=(_sds((N_MEM, BW), BF16), _sds((N_MEM, BW), BF16)), name="memkv_prep",
                          compiler_params=_cp())(kv, g_mk)


def _memkv_bwd(kv, dk, dv, g_mk):
    def body(kv_ref, dk_ref, dv_ref, g_ref, o_ref, dg_ref):
        dkk, gain = _gnorm_bwd(dk_ref[...], kv_ref[:, 0:BW], g_ref[...])
        o_ref[:, 0:BW] = dkk.astype(BF16)
        o_ref[:, BW:2 * BW] = dv_ref[...].astype(BF16)
        dg_ref[...] = jnp.sum(gain, axis=0, keepdims=True)

    return pl.pallas_call(body, out_shape=(_sds((N_MEM, 2 * BW), BF16), _sds((1, BW), F32)), name="memkv_bwd",
                          compiler_params=_cp())(kv, dk, dv, g_mk)


MERGE_TM = 256


def _merge_fwd(brs, wbt, gp):
    t = gp.shape[0]
    tm = MERGE_TM

    def body(b0, b1, b2, b3, wb_ref, gp_ref, o_ref):
        out = jnp.zeros((tm, D), F32)
        for n, b_ref in enumerate((b0, b1, b2, b3)):
            up = _dot(b_ref[...], wb_ref[n], tb=True)
            out = out + _sigmoid(gp_ref[:, n * D:(n + 1) * D]) * up
        o_ref[...] = out.astype(BF16)

    blk = pl.BlockSpec((tm, BW), lambda i: (i, 0))
    return pl.pallas_call(
        body, grid=(t // tm,),
        in_specs=[blk, blk, blk, blk, pl.BlockSpec((NH, D, BW), lambda i: (0, 0, 0)), pl.BlockSpec((tm, NH * D), lambda i: (i, 0))],
        out_specs=pl.BlockSpec((tm, D), lambda i: (i, 0)), out_shape=_sds((t, D), BF16), name="merge_fwd",
        compiler_params=_cp())(*brs, wbt, gp)


def _merge_bwd(dmerged, brs, wbt, gp):
    t = gp.shape[0]
    tm = MERGE_TM

    def body(d_ref, b0, b1, b2, b3, wb_ref, gp_ref, dgp_ref, dup_ref):
        dm = d_ref[...]
        for n, b_ref in enumerate((b0, b1, b2, b3)):
            up = _dot(b_ref[...], wb_ref[n], tb=True)
            g = _sigmoid(gp_ref[:, n * D:(n + 1) * D])
            dgp_ref[:, n * D:(n + 1) * D] = (dm * up * (g * (1.0 - g))).astype(BF16)
            dup_ref[:, n * D:(n + 1) * D] = (dm * g).astype(BF16)

    row = pl.BlockSpec((tm, D), lambda i: (i, 0))
    blk = pl.BlockSpec((tm, BW), lambda i: (i, 0))
    wide = pl.BlockSpec((tm, NH * D), lambda i: (i, 0))
    return pl.pallas_call(
        body, grid=(t // tm,), in_specs=[row, blk, blk, blk, blk, pl.BlockSpec((NH, D, BW), lambda i: (0, 0, 0)), wide],
        out_specs=(wide, wide), out_shape=(_sds((t, NH * D), BF16), _sds((t, NH * D), BF16)), name="merge_bwd",
        compiler_params=_cp())(dmerged, *brs, wbt, gp)


def _dbranch(dup, wbt):
    t = dup.shape[0]
    tm = 512

    def body(d_ref, w_ref, o_ref):
        o_ref[...] = _dot(d_ref[...], w_ref[...])

    return pl.pallas_call(
        body, grid=(t // tm, NH), in_specs=[pl.BlockSpec((tm, D), lambda i, n: (i, n)), pl.BlockSpec((None, D, BW), lambda i, n: (n, 0, 0))],
        out_specs=pl.BlockSpec((tm, BW), lambda i, n: (i, n)), out_shape=_sds((t, NH * BW), F32), name="dbranch",
        compiler_params=_cp())(dup, wbt)


def _dwbranch(brs, dup):
    t = dup.shape[0]

    def body(b0, b1, b2, b3, d_ref, o_ref):
        for n, b_ref in enumerate((b0, b1, b2, b3)):
            o_ref[n] = _dot(d_ref[:, n * D:(n + 1) * D], b_ref[...], ta=True).astype(BF16)

    return pl.pallas_call(body, out_shape=_sds((NH, D, BW), BF16), name="dwbranch", compiler_params=_cp())(*brs, dup)


def _swiglu_fwd(ag):
    t = ag.shape[0]
    tm = 256

    def body(ag_ref, o_ref):
        a, g = ag_ref[:, 0:FF], ag_ref[:, FF:2 * FF]
        o_ref[...] = (a * _sigmoid(a) * g).astype(BF16)

    return pl.pallas_call(body, grid=(t // tm,), in_specs=[pl.BlockSpec((tm, 2 * FF), lambda i: (i, 0))],
                          out_specs=pl.BlockSpec((tm, FF), lambda i: (i, 0)), out_shape=_sds((t, FF), BF16), name="swiglu_fwd",
                          compiler_params=_cp())(ag)


def _swiglu_bwd(ag, dy):
    t = ag.shape[0]
    tm = 256

    def body(ag_ref, dy_ref, o_ref):
        a, g, d = ag_ref[:, 0:FF], ag_ref[:, FF:2 * FF], dy_ref[...]
        s = _sigmoid(a)
        o_ref[:, 0:FF] = (d * g * (s * (1.0 + a * (1.0 - s)))).astype(BF16)
        o_ref[:, FF:2 * FF] = (d * (a * s)).astype(BF16)

    return pl.pallas_call(
        body, grid=(t // tm,), in_specs=[pl.BlockSpec((tm, 2 * FF), lambda i: (i, 0)), pl.BlockSpec((tm, FF), lambda i: (i, 0))],
        out_specs=pl.BlockSpec((tm, 2 * FF), lambda i: (i, 0)), out_shape=_sds((t, 2 * FF), BF16), name="swiglu_bwd",
        compiler_params=_cp())(ag, dy)


def _loss_head(y, target):
    t, d = y.shape
    tm = 256

    def body(y_ref, t_ref, dy_ref, dyb_ref, l_ref):
        e = y_ref[...] - t_ref[...]
        dy_ref[...] = e * (1.0 / d)
        dyb_ref[...] = (e * (1.0 / d)).astype(BF16)
        _acc(l_ref, jnp.full((8, 128), 0.5 * jnp.sum(jnp.sum(e * e, axis=-1, keepdims=True) * (1.0 / d)), F32), pl.program_id(0) == 0)

    row = pl.BlockSpec((tm, d), lambda i: (i, 0))
    return pl.pallas_call(body, grid=(t // tm,), in_specs=[row, row], out_specs=(row, row, pl.BlockSpec((8, 128), lambda i: (0, 0))),
                          out_shape=(_sds((t, d), F32), _sds((t, d), BF16), _sds((8, 128), F32)), name="loss_head",
                          compiler_params=_cp())(y, target)


def _sum_slots(x, name):
    k, r, c = x.shape
    tr = _tile(r, 512) if r % 128 == 0 else r

    def body(x_ref, o_ref):
        acc = x_ref[0].astype(F32)
        for s in range(1, k):
            acc = acc + x_ref[s].astype(F32)
        o_ref[...] = acc

    return pl.pallas_call(body, grid=(r // tr,), in_specs=[pl.BlockSpec((k, tr, c), lambda i: (0, i, 0))],
                          out_specs=pl.BlockSpec((tr, c), lambda i: (i, 0)), out_shape=_sds((r, c), F32), name=name,
                          compiler_params=_cp())(x)


def _pair_sum(bufs, recvs, cidx):
    n = len(bufs)

    def body(c_ref, *refs):
        for i in range(n):
            refs[2 * n + i][...] = (refs[i][...].astype(F32) + refs[n + i][...].astype(F32)).astype(BF16)

    return pl.pallas_call(
        body,
        grid_spec=pltpu.PrefetchScalarGridSpec(
            num_scalar_prefetch=1, grid=(4,),
            in_specs=[pl.BlockSpec((None, None) + b.shape[2:], lambda s, cref: (s, cref[0], 0, 0)) for b in bufs]
            + [pl.BlockSpec((None,) + r.shape[1:], lambda s, cref: (s, 0, 0)) for r in recvs],
            out_specs=tuple(pl.BlockSpec((None,) + r.shape[1:], lambda s, cref: (s, 0, 0)) for r in recvs)),
        out_shape=tuple(_sds(r.shape, BF16) for r in recvs), name="rs_pair_sum", compiler_params=_cp())(cidx, *bufs, *recvs)


def _adamw_update(w, gv, m, v):
    mn = ADAM_B1 * m + (1.0 - ADAM_B1) * gv
    vn = ADAM_B2 * v + (1.0 - ADAM_B2) * (gv * gv)
    m_hat = mn / (1.0 - ADAM_B1 ** ADAM_STEP)
    v_hat = vn / (1.0 - ADAM_B2 ** ADAM_STEP)
    return -ADAM_LR * (m_hat / (jnp.sqrt(v_hat) + ADAM_EPS) + ADAM_WD * w), mn, vn


def _adamw(w, g, m, v, name):
    r, c = w.shape

    def body(w_ref, g_ref, m_ref, v_ref, d_ref, nm_ref, nv_ref):
        d_ref[...], nm_ref[...], nv_ref[...] = _adamw_update(w_ref[...], g_ref[...], m_ref[...], v_ref[...])

    blk = pl.BlockSpec((r, c), lambda i: (0, 0))
    return pl.pallas_call(body, grid=(1,), in_specs=[blk] * 4, out_specs=(blk,) * 3,
                          out_shape=tuple(_sds((r, c), F32) for _ in range(3)), name=name, compiler_params=_cp())(w, g, m, v)


def _adamw_layer(layer, w, g, m, v, outs, name):
    _, r, c = w.shape
    tr = max(d for d in range(8, r + 1, 8) if r % d == 0 and d * c * 4 <= 2 ** 20)

    def body(w_ref, m_ref, v_ref, g_ref, *refs):
        d_ref, nm_ref, nv_ref, go_ref = refs[4:]
        gv = g_ref[...]
        d_ref[...], nm_ref[...], nv_ref[...] = _adamw_update(w_ref[...], gv, m_ref[...], v_ref[...])
        go_ref[...] = gv

    blk = pl.BlockSpec((None, tr, c), lambda i: (layer, i, 0))
    return pl.pallas_call(
        body, grid=(r // tr,), in_specs=[blk] * 3 + [pl.BlockSpec((tr, c), lambda i: (i, 0))] + [ANY] * 4, out_specs=(blk,) * 4,
        out_shape=tuple(_sds(w.shape, F32) for _ in range(4)), input_output_aliases={4 + j: j for j in range(4)}, name=name,
        compiler_params=_cp())(w, m, v, g, *outs)


def _all_gather(shards, name):
    n = len(shards)

    def body(*refs):
        x_refs, out_refs = refs[:n], refs[n:2 * n]
        send_sems, recv_sems, local_sems = refs[2 * n:]
        x, y, cc = lax.axis_index("x"), lax.axis_index("y"), lax.axis_index("c")
        me, sibling = (x, y, cc), (x, y, 1 - cc)
        chips = [(1 - x, y), (x, 1 - y), (1 - x, 1 - y)]

        def copy(i, k, block, to, own=False):
            px, py, pc = block
            slot = out_refs[i].at[4 * px + 2 * py + pc]
            return pltpu.make_async_remote_copy(
                src_ref=x_refs[i] if own else slot, dst_ref=slot, send_sem=send_sems.at[7 * i + k],
                recv_sem=recv_sems.at[7 * i + k], device_id=to, device_id_type=MESH)

        mine = [pltpu.make_async_copy(x_refs[i], out_refs[i].at[4 * x + 2 * y + cc], local_sems.at[i]) for i in range(n)]
        for cp in mine:
            cp.start()
        first = []
        for j, chip in enumerate(chips):
            first += [copy(i, 1 + j, me, (*chip, cc), own=True) for i in range(n)]
        first += [copy(i, 0, me, sibling, own=True) for i in range(n)]
        for cp in first:
            cp.start()
        passed = []
        for j, chip in enumerate(chips):
            for i in range(n):
                copy(i, 1 + j, (*chip, cc), me).wait_recv()
                cp = copy(i, 4 + j, (*chip, cc), sibling)
                cp.start()
                passed.append(cp)
        for i in range(n):
            copy(i, 0, sibling, me).wait_recv()
        for j, chip in enumerate(chips):
            for i in range(n):
                copy(i, 4 + j, (*chip, 1 - cc), me).wait_recv()
        for cp in first + passed:
            cp.wait_send()
        for cp in mine:
            cp.wait()

    return pl.pallas_call(
        body, out_shape=tuple(_sds((N_DEV,) + s.shape, s.dtype) for s in shards), in_specs=[ANY] * n, out_specs=(ANY,) * n,
        scratch_shapes=[pltpu.SemaphoreType.DMA((7 * n,)), pltpu.SemaphoreType.DMA((7 * n,)), pltpu.SemaphoreType.DMA((n,))],
        name=name)(*shards)


def _rs_core_swap(bufs, name):
    n = len(bufs)

    def body(*refs):
        b_refs, recv_refs = refs[:n], refs[n:2 * n]
        send_sems, recv_sems = refs[2 * n:]
        x, y, cc = lax.axis_index("x"), lax.axis_index("y"), lax.axis_index("c")
        copies = [pltpu.make_async_remote_copy(
            src_ref=b_refs[i].at[s, 1 - cc], dst_ref=recv_refs[i].at[s], send_sem=send_sems.at[4 * i + s],
            recv_sem=recv_sems.at[4 * i + s], device_id=(x, y, 1 - cc), device_id_type=MESH) for i in range(n) for s in range(4)]
        for cp in copies:
            cp.start()
        for cp in copies:
            cp.wait()

    return pl.pallas_call(
        body, out_shape=tuple(_sds((4,) + b.shape[2:], b.dtype) for b in bufs), in_specs=[ANY] * n, out_specs=(ANY,) * n,
        scratch_shapes=[pltpu.SemaphoreType.DMA((4 * n,)), pltpu.SemaphoreType.DMA((4 * n,))], name=name)(*bufs)


HBM = pl.BlockSpec(memory_space=pltpu.HBM)
SEMS = pl.BlockSpec(memory_space=pltpu.SEMAPHORE)
EFFECT = pltpu.SideEffectType.DATAFLOW_SIDE_EFFECTING


def _hbm(a):
    return pltpu.HBM(a.shape, a.dtype)


def _other_chips(x, y):
    return [(1 - x, y), (x, 1 - y), (1 - x, 1 - y)]


def _ici_start(srcs, lands, mode, name):
    n = len(srcs)

    def body(*refs):
        s_refs, land_refs = refs[:n], refs[n:2 * n]
        send_sems, recv_sems = refs[2 * n], refs[2 * n + 1]
        token = refs[-1]
        x, y, cc = lax.axis_index("x"), lax.axis_index("y"), lax.axis_index("c")
        mine = 2 * x + y if mode == "by_chip" else 4 * x + 2 * y + cc
        peers = [(px, py, cc) for px, py in _other_chips(x, y)]
        if mode == "by_device":
            peers = [(x, y, 1 - cc)] + peers + [(px, py, 1 - cc) for px, py in _other_chips(x, y)]
        for px, py, pc in peers:
            for i in range(n):
                src = s_refs[i]
                if mode == "by_chip":
                    src = src.at[2 * px + py]
                elif mode == "by_device":
                    src = src.at[4 * px + 2 * py + pc]
                pltpu.make_async_remote_copy(
                    src_ref=src, dst_ref=land_refs[i].at[mine], send_sem=send_sems.at[i], recv_sem=recv_sems.at[i],
                    device_id=(px, py, pc), device_id_type=MESH).start()
        token[...] = jnp.zeros_like(token)

    out = pl.pallas_call(
        body, name=name,
        out_shape=(pltpu.SemaphoreType.DMA((n,)), pltpu.SemaphoreType.DMA((n,)), *[_hbm(s) for s in srcs], *[_hbm(l) for l in lands],
                   _sds((8, 128), F32)),
        in_specs=[HBM] * (2 * n), out_specs=(SEMS, SEMS, *[HBM] * (2 * n), pl.BlockSpec(memory_space=pltpu.VMEM)),
        input_output_aliases={i: 2 + i for i in range(2 * n)}, compiler_params=pltpu.CompilerParams(has_side_effects=EFFECT),
    )(*[pltpu.with_memory_space_constraint(s, pltpu.HBM) for s in srcs],
      *[pltpu.with_memory_space_constraint(l, pltpu.HBM) for l in lands])
    return out[0], out[1], out[2:2 + n], out[2 + n:2 + 2 * n], out[-1], 7 if mode == "by_device" else 3


def _ici_wait(started, after, name):
    send_sems, recv_sems, srcs, lands, _, copies = started
    n = len(srcs)

    def body(*refs):
        land_refs = refs[n:2 * n]
        send_sems, recv_sems = refs[2 * n], refs[2 * n + 1]
        x, y, cc = lax.axis_index("x"), lax.axis_index("y"), lax.axis_index("c")
        for i in range(n):
            three = land_refs[i].at[pl.ds(0, copies)]
            cp = pltpu.make_async_remote_copy(src_ref=three, dst_ref=three, send_sem=send_sems.at[i], recv_sem=recv_sems.at[i],
                                              device_id=(x, y, cc), device_id_type=MESH)
            cp.wait_send()
            cp.wait_recv()

    return pl.pallas_call(
        body, name=name, out_shape=tuple(_hbm(l) for l in lands), in_specs=[HBM] * (2 * n) + [SEMS, SEMS, ANY],
        out_specs=tuple([HBM] * n), input_output_aliases={n + i: i for i in range(n)},
        compiler_params=pltpu.CompilerParams(has_side_effects=EFFECT))(*srcs, *lands, send_sems, recv_sems, after)


def _gather_d2d(blocks, lands, name):
    n = len(blocks)

    def body(*refs):
        x_refs, land_refs = refs[:n], refs[2 * n:3 * n]
        send_sems, recv_sems, in_sems, out_sems = refs[3 * n:3 * n + 4]
        stage = refs[3 * n + 4:]
        x, y, cc = lax.axis_index("x"), lax.axis_index("y"), lax.axis_index("c")
        sibling = (x, y, 1 - cc)
        staged = [pltpu.make_async_copy(x_refs[i], stage[i], in_sems.at[i]) for i in range(n)]
        for cp in staged:
            cp.start()
        copies = []
        for i in range(n):
            slot = land_refs[i].at[4 * x + 2 * y + cc]
            copies.append(pltpu.make_async_remote_copy(src_ref=x_refs[i], dst_ref=slot, send_sem=send_sems.at[4 * i],
                                                       recv_sem=recv_sems.at[4 * i], device_id=sibling, device_id_type=MESH))
            for j, (px, py) in enumerate(_other_chips(x, y)):
                slot = land_refs[i].at[4 * px + 2 * py + cc]
                copies.append(pltpu.make_async_remote_copy(src_ref=slot, dst_ref=slot, send_sem=send_sems.at[4 * i + 1 + j],
                                                           recv_sem=recv_sems.at[4 * i + 1 + j], device_id=sibling, device_id_type=MESH))
        for cp in copies:
            cp.start()
        mine = []
        for i in range(n):
            staged[i].wait()
            mine.append(pltpu.make_async_copy(stage[i], land_refs[i].at[4 * x + 2 * y + cc], out_sems.at[i]))
            mine[i].start()
        for i in range(n):
            slot = land_refs[i].at[4 * x + 2 * y + (1 - cc)]
            pltpu.make_async_remote_copy(src_ref=slot, dst_ref=slot, send_sem=send_sems.at[4 * i], recv_sem=recv_sems.at[4 * i],
                                         device_id=sibling, device_id_type=MESH).wait_recv()
            for j, (px, py) in enumerate(_other_chips(x, y)):
                slot = land_refs[i].at[4 * px + 2 * py + (1 - cc)]
                pltpu.make_async_remote_copy(src_ref=slot, dst_ref=slot, send_sem=send_sems.at[4 * i + 1 + j],
                                             recv_sem=recv_sems.at[4 * i + 1 + j], device_id=sibling, device_id_type=MESH).wait_recv()
        for cp in copies:
            cp.wait_send()
        for cp in mine:
            cp.wait()

    return pl.pallas_call(
        body, out_shape=tuple(_sds(l.shape, l.dtype) for l in lands), in_specs=[ANY] * (2 * n), out_specs=(ANY,) * n,
        input_output_aliases={n + i: i for i in range(n)},
        scratch_shapes=[pltpu.SemaphoreType.DMA((4 * n,)), pltpu.SemaphoreType.DMA((4 * n,)), pltpu.SemaphoreType.DMA((n,)),
                        pltpu.SemaphoreType.DMA((n,))] + [pltpu.VMEM(b.shape, b.dtype) for b in blocks],
        name=name, compiler_params=_cp())(*blocks, *lands)


def _sum_own(parts, recvs, mine, name):
    n = len(parts)

    def body(c_ref, *refs):
        s = pl.program_id(0)
        for i in range(n):
            val = jnp.where(c_ref[0] == s, refs[i][...], refs[n + i][...]).astype(F32)
            _acc(refs[2 * n + i], val, s == 0)

    ins = [pl.BlockSpec((None,) + p.shape[1:], lambda s, cref: (s, 0, 0)) for p in parts]
    return pl.pallas_call(
        body, grid_spec=pltpu.PrefetchScalarGridSpec(
            num_scalar_prefetch=1, grid=(parts[0].shape[0],), in_specs=ins + ins,
            out_specs=tuple(pl.BlockSpec(p.shape[1:], lambda s, cref: (0, 0)) for p in parts)),
        out_shape=tuple(_sds(p.shape[1:], F32) for p in parts), name=name, compiler_params=_cp())(mine, *parts, *recvs)


BIG = (("w_in", True), ("w_gate", True), ("w_mem_kv", False), ("w_branch", True), ("w_out", False), ("w_ffn_in", True),
       ("w_ffn_out", False))

SMALL = ("norm_mix_g", "norm_mem_g", "ret_decay_fwd", "ret_decay_bwd", "ret_norm_g", "pool_w", "pool_scale", "na_q_norm_g",
         "na_k_norm_g", "na_rpb", "mem_q_norm_g", "mem_k_norm_g", "norm_ffn_g")
WEIGHTS = ("norm_mix_g", "norm_mem_g", "w_in", "w_gate", "ret_decay_fwd", "ret_decay_bwd", "ret_norm_g", "pool_w", "pool_scale",
           "na_q_norm_g", "na_k_norm_g", "na_rpb", "mem_q_norm_g", "mem_k_norm_g", "w_mem_kv", "w_branch", "w_out", "norm_ffn_g",
           "w_ffn_in", "w_ffn_out")


def _to_exchange(name, transposed, shard):
    if name == "w_branch":
        return jnp.swapaxes(shard, 1, 2).reshape(NH * (D // N_DEV), BW)
    return shard.T if transposed else shard


def _from_exchange(name, transposed, block):
    if name == "w_branch":
        return jnp.swapaxes(block.reshape(NH, D // N_DEV, BW), 1, 2)
    return block.T if transposed else block


def _whole_from_gathered(name, g):
    if name == "w_branch":
        return jnp.swapaxes(g.reshape(N_DEV, NH, D // N_DEV, BW), 0, 1).reshape(NH, D, BW)
    return g.reshape(N_DEV * g.shape[1], g.shape[2])


def _by_destination(name, g):
    if name == "w_branch":
        g = jnp.swapaxes(g.reshape(NH, N_DEV, D // N_DEV, BW), 0, 1).reshape(N_DEV * NH * (D // N_DEV), BW)
    return g.reshape(4, 2, g.shape[0] // N_DEV, g.shape[1])


SMALL_PAD = 1024


def _pack_small(vals, loss=None):
    parts = [vals[n] for n in SMALL] + [jnp.zeros((1,), F32) if loss is None else loss.reshape(1)]
    rows = []
    for p in parts:
        flat = p.reshape(-1)
        rows.append(jnp.pad(flat, (0, -flat.shape[0] % SMALL_PAD)).reshape(-1, 128))
    return jnp.concatenate(rows, axis=0)


def _unpack_small(packed, like):
    out, off = {}, 0
    for n in SMALL:
        sz = int(np.prod(like[n].shape))
        nrow = -(-sz // SMALL_PAD) * (SMALL_PAD // 128)
        out[n] = packed[off:off + nrow].reshape(-1)[:sz].reshape(like[n].shape)
        off += nrow
    return out, packed[off, 0]


def _na_constants():
    c = np.arange(GRID_W)
    win = np.clip(c - NA_COLS_WIN // 2, 0, GRID_W - NA_COLS_WIN)
    kc = np.arange(GRID_W)
    inside = (kc[None, :] >= win[:, None]) & (kc[None, :] < win[:, None] + NA_COLS_WIN)
    off = kc[None, :] - c[:, None] + NA_COLS_WIN - 1
    onehot = np.zeros((128, GRID_W, GRID_W), np.float32)
    for b in range(2 * NA_COLS_WIN - 1):
        onehot[b] = (off == b) & inside
    maskadd = np.where(inside, 0.0, NEG).astype(np.float32)
    return onehot.reshape(128, GRID_W * GRID_W), maskadd


def _na_bias_table(tab, maskadd):
    n_off = 2 * NA_ROWS_WIN - 1
    t4 = tab[:NH * n_off].reshape(NH, n_off, GRID_W, GRID_W) + maskadd[None, None]
    ball = jnp.stack([t4[:, a0:a0 + NA_ROWS_WIN] for a0 in range(NA_ROWS_WIN)], axis=1)
    return ball.transpose(1, 0, 3, 2, 4).reshape(NA_ROWS_WIN, NH * GRID_W, NA_KEYS)


def _rotary_tables(t):
    half = HD // 2
    inv = ROPE_THETA ** (-jnp.arange(half, dtype=F32) / half)
    ang = jnp.arange(t, dtype=F32)[:, None] * inv[None, :]
    cos, sin = jnp.cos(ang), jnp.sin(ang)
    return jnp.tile(jnp.concatenate([cos, cos], axis=-1), (1, NH)), jnp.tile(jnp.concatenate([-sin, sin], axis=-1), (1, NH))


def _block_diag(pw):
    out = jnp.zeros((BW, BW), pw.dtype)
    for g in range(NH):
        out = lax.dynamic_update_slice(out, pw[g], (g * HD, g * HD))
    return out


def _tile4(g):
    return jnp.tile(g.reshape(1, HD), (1, NH))


def _layer_fwd(x, mem, sw, lw, consts):
    cos2, sin2, onehot, maskadd = consts
    h = _rmsnorm_fwd(x, sw["norm_mix_g"].reshape(1, D), "norm_mix_fwd")
    proj = _mm(h, lw["w_in"], tb=True, name="mm_in")
    gp = _mm(h, lw["w_gate"], tb=True, name="mm_gate")
    g_naq, g_nak, g_mq = _tile4(sw["na_q_norm_g"]), _tile4(sw["na_k_norm_g"]), _tile4(sw["mem_q_norm_g"])
    rq, rk, rv, nq, nk, nv, mq = _prep_fwd(proj, cos2, sin2, g_naq, g_nak, g_mq)

    lgf, lgb = jax.nn.log_sigmoid(sw["ret_decay_fwd"]), jax.nn.log_sigmoid(sw["ret_decay_bwd"])
    g_ret = sw["ret_norm_g"].reshape(1, BW)
    o_ret, ret = _ret_fwd(rq, rk, rv, proj, lgf, lgb, g_ret)

    wbd = _block_diag(sw["pool_w"]).astype(BF16)
    p_scale = sw["pool_scale"].reshape(1, BW)
    pool = _pool_fwd(proj, wbd, p_scale)

    rpb_pad = jnp.pad(sw["na_rpb"].reshape(NH * 15, 31), ((0, 4), (0, 97)))
    ball = _na_bias_table(_rpb_expand(rpb_pad, onehot), maskadd)
    na = _na_fwd(nq, nk, nv, ball)

    memn = _rmsnorm_fwd(mem, sw["norm_mem_g"].reshape(1, D), "norm_mem_fwd")
    kv = _mm(memn, lw["w_mem_kv"], name="mm_memkv")
    g_mk = _tile4(sw["mem_k_norm_g"])
    mk, mv = _memkv_prep(kv, g_mk)
    mo = _mem_fwd(mq, mk, mv)

    br = (ret, pool, na, mo)
    merged = _merge_fwd(br, lw["w_branch"], gp)
    x1 = _mm(merged, lw["w_out"], add=x, name="mm_out")
    h2 = _rmsnorm_fwd(x1, sw["norm_ffn_g"].reshape(1, D), "norm_ffn_fwd")
    ag = _mm(h2, lw["w_ffn_in"], tb=True, name="mm_ffn_in")
    yff = _swiglu_fwd(ag)
    x2 = _mm(yff, lw["w_ffn_out"], add=x1, name="mm_ffn_out")
    saved = dict(x=x, h=h, proj=proj, gp=gp, rq=rq, rk=rk, rv=rv, nq=nq, nk=nk, nv=nv, mq=mq, o_ret=o_ret, ball=ball, memn=memn,
                 kv=kv, mk=mk, mv=mv, br=br, merged=merged, x1=x1, h2=h2, ag=ag, yff=yff, lgf=lgf, lgb=lgb, wbd=wbd)
    return x2, saved


def _layer_bwd(dx2, dx2b, mem, sw, lw, sv, consts, dep=None):
    cos2, sin2, onehot, maskadd = consts
    gb, gs = {}, {}
    dy = _mm(dx2b, lw["w_ffn_out"], tb=True, dep=dep, name="mm_ffn_out_dx")
    gb["w_ffn_out"] = _mm(sv["yff"], dx2b, ta=True, out_dtype=BF16, name="mm_ffn_out_dw")
    dag = _swiglu_bwd(sv["ag"], dy)
    dh2 = _mm(dag, lw["w_ffn_in"], name="mm_ffn_in_dx")
    gb["w_ffn_in"] = _mm(dag, sv["h2"], ta=True, out_dtype=BF16, name="mm_ffn_in_dw")
    dx1, dx1b, dg = _rmsnorm_bwd(dh2, sv["x1"], sw["norm_ffn_g"].reshape(1, D), dx2, "norm_ffn_bwd")
    gs["norm_ffn_g"] = dg.reshape(D)

    dmerged = _mm(dx1b, lw["w_out"], tb=True, name="mm_out_dx")
    gb["w_out"] = _mm(sv["merged"], dx1b, ta=True, out_dtype=BF16, name="mm_out_dw")
    dgp, dup = _merge_bwd(dmerged, sv["br"], lw["w_branch"], sv["gp"])
    dbr = _dbranch(dup, lw["w_branch"])
    gb["w_branch"] = _dwbranch(sv["br"], dup)

    g_ret = sw["ret_norm_g"].reshape(1, BW)
    do_ret, d_rg, dg_ret = _ret_post_bwd(dbr, sv["o_ret"], sv["proj"], g_ret)
    d_rq, d_rk, d_rv, dlg = _ret_bwd(do_ret, sv["rq"], sv["rk"], sv["rv"], sv["lgf"], sv["lgb"])
    gs["ret_norm_g"] = dg_ret.reshape(BW)
    _, vjp_f = jax.vjp(jax.nn.log_sigmoid, sw["ret_decay_fwd"])
    _, vjp_b = jax.vjp(jax.nn.log_sigmoid, sw["ret_decay_bwd"])
    gs["ret_decay_fwd"] = vjp_f(dlg[0:NH, 0])[0]
    gs["ret_decay_bwd"] = vjp_b(dlg[NH:2 * NH, 0])[0]

    p_scale = sw["pool_scale"].reshape(1, BW)
    d_pv, dwbd, dscale = _pool_bwd(dbr, sv["proj"], sv["wbd"], p_scale)
    gs["pool_w"] = jnp.stack([dwbd[g * HD:(g + 1) * HD, g * HD:(g + 1) * HD] for g in range(NH)])
    gs["pool_scale"] = dscale.reshape(BW)

    d_nq, d_nk, d_nv, dball = _na_bwd(dbr, sv["nq"], sv["nk"], sv["nv"], sv["ball"])
    _, vjp_tab = jax.vjp(lambda tab: _na_bias_table(tab, maskadd), jnp.zeros((64, GRID_W * GRID_W), F32))
    drpb = _rpb_reduce(vjp_tab(dball)[0], onehot)
    gs["na_rpb"] = drpb[:NH * 15, :31].reshape(NH, 15, 31)

    d_mq, d_mk, d_mv = _mem_bwd(dbr, sv["mq"], sv["mk"], sv["mv"])
    g_mk = _tile4(sw["mem_k_norm_g"])
    dkv, dg_mk = _memkv_bwd(sv["kv"], d_mk, d_mv, g_mk)
    gs["mem_k_norm_g"] = dg_mk.reshape(NH, HD).sum(0)
    gb["w_mem_kv"] = _mm(sv["memn"], dkv, ta=True, out_dtype=BF16, name="mm_memkv_dw")
    dmemn = _mm(dkv, lw["w_mem_kv"], tb=True, name="mm_memkv_dx")
    _, _, dg_mem = _rmsnorm_bwd(dmemn, mem, sw["norm_mem_g"].reshape(1, D), jnp.zeros_like(mem), "norm_mem_bwd")
    gs["norm_mem_g"] = dg_mem.reshape(D)

    g_naq, g_nak, g_mq = _tile4(sw["na_q_norm_g"]), _tile4(sw["na_k_norm_g"]), _tile4(sw["mem_q_norm_g"])
    dproj, dg_naq, dg_nak, dg_mq = _prep_bwd(sv["proj"], cos2, sin2, g_naq, g_nak, g_mq, d_rq, d_rk, d_rv, d_rg, d_pv, d_nq, d_nk,
                                             d_nv, d_mq)
    gs["na_q_norm_g"] = dg_naq.reshape(NH, HD).sum(0)
    gs["na_k_norm_g"] = dg_nak.reshape(NH, HD).sum(0)
    gs["mem_q_norm_g"] = dg_mq.reshape(NH, HD).sum(0)

    dh = _mm(dproj, lw["w_in"], name="mm_in_dx")
    dh = _mm(dgp, lw["w_gate"], add=dh, name="mm_gate_dx")
    gb["w_in"] = _mm(dproj, sv["h"], ta=True, out_dtype=BF16, name="mm_in_dw")
    gb["w_gate"] = _mm(dgp, sv["h"], ta=True, out_dtype=BF16, name="mm_gate_dw")
    dx, dxb, dg = _rmsnorm_bwd(dh, sv["x"], sw["norm_mix_g"].reshape(1, D), dx1, "norm_mix_bwd")
    gs["norm_mix_g"] = dg.reshape(D)
    return dx, dxb, gb, gs


def _local_step(x, mem, target, small, get_layer, on_grads):
    t = x.shape[0]
    cos2, sin2 = _rotary_tables(t)
    onehot, maskadd = _na_constants()
    consts = (cos2, sin2, jnp.asarray(onehot), jnp.asarray(maskadd))
    saved, weights, cur = [], [], x
    for l in range(DEPTH):
        sw = {n: small[n][l] for n in SMALL}
        weights.append(get_layer(l, cur))
        cur, sv = _layer_fwd(cur, mem, sw, weights[l], consts)
        saved.append(sv)
    dy, dyb, loss_tile = _loss_head(cur, target)
    small_g = {n: [None] * DEPTH for n in SMALL}
    dep = None
    for l in reversed(range(DEPTH)):
        sw = {n: small[n][l] for n in SMALL}
        dy, dyb, gb, gs = _layer_bwd(dy, dyb, mem, sw, weights[l], saved[l], consts, dep)
        dep = on_grads(l, gb, dy)
        for n in SMALL:
            small_g[n][l] = gs[n]
    return loss_tile[0, 0], dy, {n: jnp.stack(v) for n, v in small_g.items()}


def _flat2d(a):
    return a.reshape(-1, a.shape[-1])


def kernel(x, mem, norm_mix_g, norm_mem_g, w_in, w_gate, ret_decay_fwd, ret_decay_bwd, ret_norm_g, pool_w, pool_scale, na_q_norm_g, na_k_norm_g, na_rpb, mem_q_norm_g, mem_k_norm_g, w_mem_kv, w_branch, w_out, norm_ffn_g, w_ffn_in, w_ffn_out, loss_target, m_norm_mix_g, m_norm_mem_g, m_w_in, m_w_gate, m_ret_decay_fwd, m_ret_decay_bwd, m_ret_norm_g, m_pool_w, m_pool_scale, m_na_q_norm_g, m_na_k_norm_g, m_na_rpb, m_mem_q_norm_g, m_mem_k_norm_g, m_w_mem_kv, m_w_branch, m_w_out, m_norm_ffn_g, m_w_ffn_in, m_w_ffn_out, v_norm_mix_g, v_norm_mem_g, v_w_in, v_w_gate, v_ret_decay_fwd, v_ret_decay_bwd, v_ret_norm_g, v_pool_w, v_pool_scale, v_na_q_norm_g, v_na_k_norm_g, v_na_rpb, v_mem_q_norm_g, v_mem_k_norm_g, v_w_mem_kv, v_w_branch, v_w_out, v_norm_ffn_g, v_w_ffn_in, v_w_ffn_out):
    w = dict(norm_mix_g=norm_mix_g, norm_mem_g=norm_mem_g, w_in=w_in, w_gate=w_gate, ret_decay_fwd=ret_decay_fwd,
             ret_decay_bwd=ret_decay_bwd, ret_norm_g=ret_norm_g, pool_w=pool_w, pool_scale=pool_scale, na_q_norm_g=na_q_norm_g,
             na_k_norm_g=na_k_norm_g, na_rpb=na_rpb, mem_q_norm_g=mem_q_norm_g, mem_k_norm_g=mem_k_norm_g, w_mem_kv=w_mem_kv,
             w_branch=w_branch, w_out=w_out, norm_ffn_g=norm_ffn_g, w_ffn_in=w_ffn_in, w_ffn_out=w_ffn_out)
    m = dict(norm_mix_g=m_norm_mix_g, norm_mem_g=m_norm_mem_g, w_in=m_w_in, w_gate=m_w_gate, ret_decay_fwd=m_ret_decay_fwd,
             ret_decay_bwd=m_ret_decay_bwd, ret_norm_g=m_ret_norm_g, pool_w=m_pool_w, pool_scale=m_pool_scale, na_q_norm_g=m_na_q_norm_g,
             na_k_norm_g=m_na_k_norm_g, na_rpb=m_na_rpb, mem_q_norm_g=m_mem_q_norm_g, mem_k_norm_g=m_mem_k_norm_g, w_mem_kv=m_w_mem_kv,
             w_branch=m_w_branch, w_out=m_w_out, norm_ffn_g=m_norm_ffn_g, w_ffn_in=m_w_ffn_in, w_ffn_out=m_w_ffn_out)
    v = dict(norm_mix_g=v_norm_mix_g, norm_mem_g=v_norm_mem_g, w_in=v_w_in, w_gate=v_w_gate, ret_decay_fwd=v_ret_decay_fwd,
             ret_decay_bwd=v_ret_decay_bwd, ret_norm_g=v_ret_norm_g, pool_w=v_pool_w, pool_scale=v_pool_scale, na_q_norm_g=v_na_q_norm_g,
             na_k_norm_g=v_na_k_norm_g, na_rpb=v_na_rpb, mem_q_norm_g=v_mem_q_norm_g, mem_k_norm_g=v_mem_k_norm_g, w_mem_kv=v_w_mem_kv,
             w_branch=v_w_branch, w_out=v_w_out, norm_ffn_g=v_norm_ffn_g, w_ffn_in=v_w_ffn_in, w_ffn_out=v_w_ffn_out)
    assert x.shape == (1, 2048, D) and mem.shape == (1, N_MEM, D) and w_in.shape == (DEPTH, D, 9 * BW // N_DEV)

    started = []
    for l in range(DEPTH):
        blocks = [_to_exchange(name, tr, w[name][l]).astype(BF16) for name, tr in BIG]
        lands = [lax.empty((N_DEV,) + b.shape, BF16) for b in blocks]
        started.append(_ici_start(blocks, lands, "gather", "gather_ici_start_%d" % l))
    all_started = started[0][4] + started[1][4] + started[2][4] + started[3][4]

    def get_layer(l, after):
        lands = _ici_wait(started[l], all_started if l == 0 else after, "gather_ici_wait_%d" % l)
        whole = _gather_d2d(started[l][2], lands, "gather_d2d")
        return {name: _whole_from_gathered(name, g) for (name, _), g in zip(BIG, whole)}

    cidx = lax.axis_index("c").astype(jnp.int32).reshape(1)
    chip = (2 * lax.axis_index("x") + lax.axis_index("y")).astype(jnp.int32).reshape(1)
    in_flight = []

    def flip_of(name, tr):
        return (lambda a: jnp.swapaxes(a, 1, 2)) if (tr and name != "w_branch") else (lambda a: a)

    def rows3(a):
        return a.reshape(DEPTH, -1, a.shape[-1])

    opt_in = {name: tuple(rows3(flip_of(name, tr)(t[name])) for t in (w, m, v)) for name, tr in BIG}
    opt_out = {name: tuple(lax.empty(opt_in[name][0].shape, F32) for _ in range(4)) for name, _ in BIG}

    device = (2 * chip + cidx).astype(jnp.int32)

    def finish(l, st, after):
        recv = _ici_wait(st, after, "rs_ici_wait_%d" % l)
        sums = _sum_own(st[2], recv, chip if st[5] == 3 else device, "rs_sum")
        for (name, tr), s in zip(BIG, sums):
            g = _from_exchange(name, tr, s) if name == "w_branch" else s
            wx, mx, vx = opt_in[name]
            opt_out[name] = _adamw_layer(l, wx, g.reshape(-1, g.shape[-1]), mx, vx, opt_out[name], "adamw_" + name)

    def on_grads(l, gb, after):
        send = [_by_destination(name, gb[name]) for name, _ in BIG]
        if l > 0:
            send = [s.reshape((N_DEV,) + s.shape[2:]) for s in send]
            st = _ici_start(send, [lax.empty(s.shape, BF16) for s in send], "by_device", "rs_ici_start_%d" % l)
        else:
            from_core = _rs_core_swap(send, "rs_core_swap")
            chip_part = _pair_sum(send, from_core, cidx)
            st = _ici_start(chip_part, [lax.empty(p.shape, BF16) for p in chip_part], "by_chip", "rs_ici_start_%d" % l)
        in_flight.append((l, st))
        return st[4]

    loss_local, dx, small_g = _local_step(x[0], mem[0], loss_target[0], {n: w[n] for n in SMALL}, get_layer, on_grads)

    last_started = in_flight[-1][1][4]
    for l, st in in_flight[:-1]:
        finish(l, st, last_started)

    small_all, = _all_gather([_pack_small(small_g, loss_local) + last_started[0:1]], "gather_small")
    packed_g = _sum_slots(small_all, "small_sum")
    small_sum, loss = _unpack_small(packed_g, {n: w[n] for n in SMALL})
    d_, m_, v_ = _adamw(_pack_small({n: w[n] for n in SMALL}), packed_g, _pack_small({n: m[n] for n in SMALL}),
                        _pack_small({n: v[n] for n in SMALL}), "adamw_small")
    updated = d_[0:8]
    for name, _ in BIG:
        updated = updated + opt_out[name][0][1, 0:8, 0:128]
    finish(*in_flight[-1], updated)

    grads, delta, new_m, new_v = {}, {}, {}, {}
    for name, tr in BIG:
        shape = flip_of(name, tr)(w[name]).shape
        delta[name], new_m[name], new_v[name], grads[name] = (flip_of(name, tr)(a.reshape(shape)) for a in opt_out[name])
    like = {n: w[n] for n in SMALL}
    ds, _ = _unpack_small(d_, like)
    ms, _ = _unpack_small(m_, like)
    vs, _ = _unpack_small(v_, like)
    for n in SMALL:
        grads[n], delta[n], new_m[n], new_v[n] = small_sum[n], ds[n], ms[n], vs[n]

    return (loss, dx[None], *[grads[n] for n in WEIGHTS], *[delta[n] for n in WEIGHTS], *[new_m[n] for n in WEIGHTS],
            *[new_v[n] for n in WEIGHTS])
```

```python
import functools

import numpy as np
import jax
import jax.numpy as jnp
from jax import lax
from jax.experimental import pallas as pl
from jax.experimental.pallas import tpu as pltpu

F32 = jnp.float32
BF16 = jnp.bfloat16
MXU = jnp.bfloat16
HI = lax.Precision.HIGHEST

DEPTH = 4
D = 1024
BW = 256
HD = 64
NH = 4
GRID_W = 64
NA_ROWS_WIN = 8
NA_COLS_WIN = 16
N_MEM = 256
FF = 2816
EPS = 1e-6
NEG = -1e30
ROPE_THETA = 10000.0
POOL_HALF_MAX = 8

ADAM_LR, ADAM_B1, ADAM_B2, ADAM_EPS, ADAM_WD, ADAM_STEP = 0.001, 0.9, 0.999, 1e-08, 0.01, 10

N_DEV = 8
VMEM_LIMIT = 56 * 1024 * 1024

RQ, RK, RV, RG, PV, NQ, NK, NV, MQ = range(9)

MESH = pl.DeviceIdType.MESH
ANY = pl.BlockSpec(memory_space=pl.ANY)
SMEM = pl.BlockSpec(memory_space=pltpu.SMEM)


def _cp(**kw):
    return pltpu.CompilerParams(vmem_limit_bytes=VMEM_LIMIT, **kw)


def _tile(n, cap):
    if n <= cap:
        return n
    best = None
    for t in range(128, cap + 1, 128):
        if n % t == 0:
            best = t
    assert best is not None, (n, cap)
    return best


def _sds(shape, dtype):
    return jax.ShapeDtypeStruct(shape, dtype)


def _lane_head(shape):
    return lax.shift_right_logical(lax.broadcasted_iota(jnp.int32, shape, len(shape) - 1), 6)


def _group_mean(z):
    i = lax.shift_right_logical(lax.broadcasted_iota(jnp.int32, (BW, BW), 0), 6)
    j = lax.shift_right_logical(lax.broadcasted_iota(jnp.int32, (BW, BW), 1), 6)
    g = jnp.where(i == j, 1.0 / HD, 0.0).astype(F32)
    return jnp.dot(z, g, precision=HI, preferred_element_type=F32)


def _gnorm(t, g):
    r = lax.rsqrt(_group_mean(t * t) + EPS)
    return t * r * g


def _gnorm_bwd(dy, t, g):
    r = lax.rsqrt(_group_mean(t * t) + EPS)
    th = t * r
    dth = dy * g
    dt = r * (dth - th * _group_mean(dth * th))
    return dt, dy * th


def _swap_halves(t):
    lane = lax.broadcasted_iota(jnp.int32, t.shape, 1)
    return jnp.where((lane & 63) < 32, pltpu.roll(t, BW - 32, 1), pltpu.roll(t, 32, 1))


def _sigmoid(x):
    return 1.0 / (1.0 + jnp.exp(-x))


def _dot(a, b, ta=False, tb=False):
    return lax.dot_general(a.astype(MXU), b.astype(MXU), (((0 if ta else 1,), (1 if tb else 0,)), ((), ())),
                           preferred_element_type=F32)


def _stack_heads(t):
    head = _lane_head(t.shape)
    return jnp.concatenate([jnp.where(head == h, t, jnp.zeros_like(t)) for h in range(NH)], axis=0)


def _unstack_heads(t, rows):
    head = _lane_head((rows, BW))
    out = jnp.zeros((rows, BW), F32)
    for h in range(NH):
        out = out + jnp.where(head == h, t[h * rows:(h + 1) * rows], 0.0)
    return out


def _softmax_rows(s):
    m = jnp.max(s, axis=-1, keepdims=True)
    e = jnp.exp(s - m)
    return e / jnp.sum(e, axis=-1, keepdims=True)


def _acc(ref, val, first):
    @pl.when(first)
    def _():
        ref[...] = val

    @pl.when(jnp.logical_not(first))
    def _():
        ref[...] += val


def _mm(a, b, *, ta=False, tb=False, out_dtype=F32, add=None, dep=None, name):
    m, k = (a.shape[1], a.shape[0]) if ta else a.shape
    n = b.shape[0] if tb else b.shape[1]
    tm, tn = _tile(m, 1408), _tile(n, 512)

    def body(*refs):
        if add is None:
            a_ref, b_ref, o_ref = refs[:2] + refs[-1:]
            r = _dot(a_ref[...], b_ref[...], ta, tb)
        else:
            a_ref, b_ref, c_ref, o_ref = refs[:3] + refs[-1:]
            r = _dot(a_ref[...], b_ref[...], ta, tb) + c_ref[...]
        o_ref[...] = r.astype(out_dtype)

    a_spec = pl.BlockSpec((k, tm), lambda i, j: (0, i)) if ta else pl.BlockSpec((tm, k), lambda i, j: (i, 0))
    b_spec = pl.BlockSpec((tn, k), lambda i, j: (j, 0)) if tb else pl.BlockSpec((k, tn), lambda i, j: (0, j))
    o_spec = pl.BlockSpec((tm, tn), lambda i, j: (i, j))
    ins, args = [a_spec, b_spec], [a, b]
    if add is not None:
        ins.append(o_spec)
        args.append(add)
    if dep is not None:
        ins.append(pl.BlockSpec((8, 128), lambda i, j: (0, 0)))
        args.append(dep)
    return pl.pallas_call(
        body, grid=(m // tm, n // tn), in_specs=ins, out_specs=o_spec, out_shape=_sds((m, n), out_dtype), name=name,
        compiler_params=_cp(dimension_semantics=("parallel", "parallel")))(*args)


def _rmsnorm_fwd(x, g, name):
    t, d = x.shape
    tm = _tile(t, 256)

    def body(x_ref, g_ref, o_ref):
        xv = x_ref[...]
        r = lax.rsqrt(jnp.mean(xv * xv, axis=-1, keepdims=True) + EPS)
        o_ref[...] = (xv * r * g_ref[...]).astype(o_ref.dtype)

    return pl.pallas_call(
        body, grid=(t // tm,), in_specs=[pl.BlockSpec((tm, d), lambda i: (i, 0)), pl.BlockSpec((1, d), lambda i: (0, 0))],
        out_specs=pl.BlockSpec((tm, d), lambda i: (i, 0)), out_shape=_sds((t, d), BF16), name=name, compiler_params=_cp())(x, g)


def _rmsnorm_bwd(dh, x, g, res, name):
    t, d = x.shape
    tm = _tile(t, 256)

    def body(dh_ref, x_ref, g_ref, res_ref, dx_ref, dxb_ref, dg_ref):
        xv = x_ref[...]
        dhv = dh_ref[...]
        r = lax.rsqrt(jnp.mean(xv * xv, axis=-1, keepdims=True) + EPS)
        xh = xv * r
        dxh = dhv * g_ref[...]
        dx = res_ref[...] + r * (dxh - xh * jnp.mean(dxh * xh, axis=-1, keepdims=True))
        dx_ref[...] = dx
        dxb_ref[...] = dx.astype(BF16)
        _acc(dg_ref, jnp.sum(dhv * xh, axis=0, keepdims=True), pl.program_id(0) == 0)

    row = pl.BlockSpec((tm, d), lambda i: (i, 0))
    vec = pl.BlockSpec((1, d), lambda i: (0, 0))
    return pl.pallas_call(
        body, grid=(t // tm,), in_specs=[row, row, vec, row], out_specs=(row, row, vec),
        out_shape=(_sds((t, d), F32), _sds((t, d), BF16), _sds((1, d), F32)), name=name, compiler_params=_cp())(dh, x, g, res)


def _prep_fwd(proj, cos2, sin2, g_naq, g_nak, g_mq):
    t = proj.shape[0]
    tm = 256

    def body(p_ref, cos_ref, sin_ref, gq_ref, gk_ref, gm_ref, rq_ref, rk_ref, rv_ref, nq_ref, nk_ref, nv_ref, mq_ref):
        def col(c):
            return p_ref[:, c * BW:(c + 1) * BW]

        cosv, sinv = cos_ref[...], sin_ref[...]

        def rot(tv):
            return tv * cosv + _swap_halves(tv) * sinv

        rq_ref[...] = (rot(col(RQ)) * (HD ** -0.5)).astype(BF16)
        rk_ref[...] = rot(col(RK)).astype(BF16)
        rv_ref[...] = col(RV).astype(BF16)
        nq_ref[...] = _gnorm(col(NQ), gq_ref[...]).astype(BF16)
        nk_ref[...] = _gnorm(col(NK), gk_ref[...]).astype(BF16)
        nv_ref[...] = col(NV).astype(BF16)
        mq_ref[...] = _gnorm(col(MQ), gm_ref[...]).astype(BF16)

    blk = pl.BlockSpec((tm, BW), lambda i: (i, 0))
    vec = pl.BlockSpec((1, BW), lambda i: (0, 0))
    return pl.pallas_call(
        body, grid=(t // tm,), in_specs=[pl.BlockSpec((tm, 9 * BW), lambda i: (i, 0)), blk, blk, vec, vec, vec],
        out_specs=tuple(blk for _ in range(7)), out_shape=tuple(_sds((t, BW), BF16) for _ in range(7)),
        name="prep_fwd", compiler_params=_cp())(proj, cos2, sin2, g_naq, g_nak, g_mq)


def _prep_bwd(proj, cos2, sin2, g_naq, g_nak, g_mq, d_rq, d_rk, d_rv, d_rg, d_pv, d_nq, d_nk, d_nv, d_mq):
    t = proj.shape[0]
    tm = 256

    def body(p_ref, cos_ref, sin_ref, gq_ref, gk_ref, gm_ref, drq_ref, drk_ref, drv_ref, drg_ref, dpv_ref, dnq_ref, dnk_ref,
             dnv_ref, dmq_ref, o_ref, dgq_ref, dgk_ref, dgm_ref):
        first = pl.program_id(0) == 0

        def col(c):
            return p_ref[:, c * BW:(c + 1) * BW]

        def put(c, v):
            o_ref[:, c * BW:(c + 1) * BW] = v.astype(BF16)

        cosv, sinv = cos_ref[...], sin_ref[...]

        def rot_t(dv):
            return dv * cosv + _swap_halves(dv * sinv)

        put(RQ, rot_t(drq_ref[...] * (HD ** -0.5)))
        put(RK, rot_t(drk_ref[...]))
        put(RV, drv_ref[...])
        put(RG, drg_ref[...])
        put(PV, dpv_ref[...])
        dq, gq = _gnorm_bwd(dnq_ref[...], col(NQ), gq_ref[...])
        put(NQ, dq)
        _acc(dgq_ref, jnp.sum(gq, axis=0, keepdims=True), first)
        dk, gk = _gnorm_bwd(dnk_ref[...], col(NK), gk_ref[...])
        put(NK, dk)
        _acc(dgk_ref, jnp.sum(gk, axis=0, keepdims=True), first)
        put(NV, dnv_ref[...])
        dm, gm = _gnorm_bwd(dmq_ref[...], col(MQ), gm_ref[...])
        put(MQ, dm)
        _acc(dgm_ref, jnp.sum(gm, axis=0, keepdims=True), first)

    blk = pl.BlockSpec((tm, BW), lambda i: (i, 0))
    vec = pl.BlockSpec((1, BW), lambda i: (0, 0))
    wide = pl.BlockSpec((tm, 9 * BW), lambda i: (i, 0))
    return pl.pallas_call(
        body, grid=(t // tm,), in_specs=[wide, blk, blk, vec, vec, vec] + [blk] * 9, out_specs=(wide, vec, vec, vec),
        out_shape=(_sds((t, 9 * BW), BF16), _sds((1, BW), F32), _sds((1, BW), F32), _sds((1, BW), F32)),
        name="prep_bwd", compiler_params=_cp())(proj, cos2, sin2, g_naq, g_nak, g_mq, d_rq, d_rk, d_rv, d_rg, d_pv, d_nq, d_nk,
                                                d_nv, d_mq)


RET_B = 256


def _ret_consts(lgf_ref, lgb_ref):
    bsz = RET_B
    head = _lane_head((1, BW))
    lf, lb = jnp.zeros((1, BW), F32), jnp.zeros((1, BW), F32)
    for h in range(NH):
        lf = lf + jnp.where(head == h, lgf_ref[h], 0.0)
        lb = lb + jnp.where(head == h, lgb_ref[h], 0.0)
    pos = lax.broadcasted_iota(jnp.int32, (bsz, BW), 0).astype(F32)
    up, down = pos + 1.0, (bsz - 1.0) - pos
    c = dict(up=up, down=down, kf=jnp.exp(down * lf), kb=jnp.exp(up * lb), qf=jnp.exp(up * lf), qb=jnp.exp(down * lb),
             cf=jnp.exp(bsz * lf), cb=jnp.exp(bsz * lb))
    diff = (lax.broadcasted_iota(jnp.int32, (NH * bsz, 1), 0) & (bsz - 1)) - lax.broadcasted_iota(jnp.int32, (1, bsz), 1)
    c["causal"] = diff >= 0
    c["dist"] = jnp.abs(diff).astype(F32)
    lgf = jnp.concatenate([jnp.full((bsz, 1), lgf_ref[h], F32) for h in range(NH)], axis=0)
    lgb = jnp.concatenate([jnp.full((bsz, 1), lgb_ref[h], F32) for h in range(NH)], axis=0)
    c["dm"] = jnp.exp(c["dist"] * jnp.where(c["causal"], lgf, lgb))
    c["bd"] = _lane_head((BW, BW)) == lax.shift_right_logical(lax.broadcasted_iota(jnp.int32, (BW, BW), 0), 6)
    return c


def _ret_states(k_ref, v_ref, st_ref, c, nb):
    bsz = RET_B

    def summary(b, decay):
        kb = k_ref[b * bsz:(b + 1) * bsz, :].astype(F32)
        return jnp.where(c["bd"], _dot(kb * decay, v_ref[b * bsz:(b + 1) * bsz, :], ta=True), 0.0)

    f = jnp.zeros((BW, BW), F32)
    for b in range(nb):
        st_ref[b] = f
        if b < nb - 1:
            f = c["cf"] * f + summary(b, c["kf"])
    g = jnp.zeros((BW, BW), F32)
    for b in reversed(range(nb)):
        st_ref[nb + b] = g
        if b > 0:
            g = c["cb"] * g + summary(b, c["kb"])


def _ret_fwd(q, k, v, proj, lgf, lgb, g_ret):
    t = q.shape[0]
    bsz, nb = RET_B, t // RET_B

    def body(lgf_ref, lgb_ref, q_ref, k_ref, v_ref, rg_ref, g_ref, o_ref, ret_ref, st_ref):
        c = _ret_consts(lgf_ref, lgb_ref)
        _ret_states(k_ref, v_ref, st_ref, c, nb)
        for b in range(nb):
            blk = slice(b * bsz, (b + 1) * bsz)
            qb, kb, vb = q_ref[blk, :], k_ref[blk, :], v_ref[blk, :]
            s = _dot(_stack_heads(qb), kb, tb=True)
            o = _unstack_heads(_dot(s * c["dm"], vb), bsz)
            q32 = qb.astype(F32)
            o = o + _dot(q32 * c["qf"], st_ref[b]) + _dot(q32 * c["qb"], st_ref[nb + b])
            o_ref[blk, :] = o
            rg = rg_ref[blk, :]
            ret_ref[blk, :] = (_gnorm(o, g_ref[...]) * (rg * _sigmoid(rg))).astype(BF16)

    whole = pl.BlockSpec((t, BW), lambda i: (0, 0))
    return pl.pallas_call(
        body, grid=(1,),
        in_specs=[SMEM, SMEM, whole, whole, whole, pl.BlockSpec((t, BW), lambda i: (0, RG)), pl.BlockSpec((1, BW), lambda i: (0, 0))],
        out_specs=(whole, whole), out_shape=(_sds((t, BW), F32), _sds((t, BW), BF16)),
        scratch_shapes=[pltpu.VMEM((2 * nb, BW, BW), F32)], name="ret_fwd", compiler_params=_cp())(lgf, lgb, q, k, v, proj, g_ret)


def _ret_post_bwd(dbr, o_ret, proj, g_ret):
    t = o_ret.shape[0]
    tm = 256

    def body(d_ref, o_ref, rg_ref, g_ref, do_ref, drg_ref, dg_ref):
        dret, o, rg, g = d_ref[...], o_ref[...], rg_ref[...], g_ref[...]
        sg = _sigmoid(rg)
        do, dgain = _gnorm_bwd(dret * (rg * sg), o, g)
        do_ref[...] = do.astype(BF16)
        drg_ref[...] = dret * _gnorm(o, g) * (sg * (1.0 + rg * (1.0 - sg)))
        _acc(dg_ref, jnp.sum(dgain, axis=0, keepdims=True), pl.program_id(0) == 0)

    blk = pl.BlockSpec((tm, BW), lambda i: (i, 0))
    vec = pl.BlockSpec((1, BW), lambda i: (0, 0))
    return pl.pallas_call(
        body, grid=(t // tm,), in_specs=[blk, blk, pl.BlockSpec((tm, BW), lambda i: (i, RG)), vec], out_specs=(blk, blk, vec),
        out_shape=(_sds((t, BW), BF16), _sds((t, BW), F32), _sds((1, BW), F32)), name="ret_post_bwd",
        compiler_params=_cp())(dbr, o_ret, proj, g_ret)


def _ret_bwd(do, q, k, v, lgf, lgb):
    t = q.shape[0]
    bsz, nb = RET_B, t // RET_B

    def body(lgf_ref, lgb_ref, d_ref, q_ref, k_ref, v_ref, dq_ref, dk_ref, dv_ref, dlg_ref, st_ref, sd_ref):
        c = _ret_consts(lgf_ref, lgb_ref)
        _ret_states(k_ref, v_ref, st_ref, c, nb)
        lane_f, lane_b = jnp.zeros((1, BW), F32), jnp.zeros((1, BW), F32)
        row_f, row_b = jnp.zeros((NH * bsz, 1), F32), jnp.zeros((NH * bsz, 1), F32)

        def rows(x):
            return jnp.sum(x, axis=0, keepdims=True)

        for b in range(nb):
            blk = slice(b * bsz, (b + 1) * bsz)
            qb, kb, vb, dob = q_ref[blk, :], k_ref[blk, :], v_ref[blk, :], d_ref[blk, :]
            q32 = qb.astype(F32)
            qs, dos = _stack_heads(qb), _stack_heads(dob)
            s = _dot(qs, kb, tb=True)
            da = _dot(dos, vb, tb=True)
            dv_ref[blk, :] = _dot(s * c["dm"], dos, ta=True)
            ds = da * c["dm"]
            w = ds * s * c["dist"]
            row_f = row_f + jnp.sum(jnp.where(c["causal"], w, 0.0), axis=1, keepdims=True)
            row_b = row_b + jnp.sum(jnp.where(c["causal"], 0.0, w), axis=1, keepdims=True)
            dsb = ds.astype(MXU)
            dk_ref[blk, :] = _dot(dsb, qs, ta=True)
            dq_f = _dot(dob, st_ref[b], tb=True) * c["qf"]
            dq_b = _dot(dob, st_ref[nb + b], tb=True) * c["qb"]
            lane_f = lane_f + rows(c["up"] * dq_f * q32)
            lane_b = lane_b + rows(c["down"] * dq_b * q32)
            dq_ref[blk, :] = _unstack_heads(_dot(dsb, kb), bsz) + dq_f + dq_b
            sd_ref[b] = jnp.where(c["bd"], _dot(q32 * c["qf"], dob, ta=True), 0.0)
            sd_ref[nb + b] = jnp.where(c["bd"], _dot(q32 * c["qb"], dob, ta=True), 0.0)

        def through_state(b, grad, decay, weight, lane):
            blk = slice(b * bsz, (b + 1) * bsz)
            k32 = k_ref[blk, :].astype(F32)
            dk = _dot(v_ref[blk, :], grad, tb=True) * decay
            dk_ref[blk, :] += dk
            dv_ref[blk, :] += _dot(k32 * decay, grad)
            return lane + rows(weight * dk * k32)

        phi = jnp.zeros((BW, BW), F32)
        for b in reversed(range(nb)):
            if b < nb - 1:
                lane_f = through_state(b, phi, c["kf"], c["down"], lane_f)
                lane_f = lane_f + bsz * rows(c["cf"] * st_ref[b] * phi)
            phi = sd_ref[b] + c["cf"] * phi
        gam = jnp.zeros((BW, BW), F32)
        for b in range(nb):
            if b > 0:
                lane_b = through_state(b, gam, c["kb"], c["up"], lane_b)
                lane_b = lane_b + bsz * rows(c["cb"] * st_ref[nb + b] * gam)
            gam = sd_ref[nb + b] + c["cb"] * gam

        head = _lane_head((1, BW))
        for h in range(NH):
            tot_f = jnp.sum(row_f[h * bsz:(h + 1) * bsz, :]) + jnp.sum(jnp.where(head == h, lane_f, 0.0))
            tot_b = jnp.sum(row_b[h * bsz:(h + 1) * bsz, :]) + jnp.sum(jnp.where(head == h, lane_b, 0.0))
            dlg_ref[h:h + 1, :] = jnp.full((1, 128), tot_f, F32)
            dlg_ref[NH + h:NH + h + 1, :] = jnp.full((1, 128), tot_b, F32)

    whole = pl.BlockSpec((t, BW), lambda i: (0, 0))
    return pl.pallas_call(
        body, grid=(1,), in_specs=[SMEM, SMEM, whole, whole, whole, whole],
        out_specs=(whole, whole, whole, pl.BlockSpec((2 * NH, 128), lambda i: (0, 0))),
        out_shape=(_sds((t, BW), F32), _sds((t, BW), F32), _sds((t, BW), F32), _sds((2 * NH, 128), F32)),
        scratch_shapes=[pltpu.VMEM((2 * nb, BW, BW), F32), pltpu.VMEM((2 * nb, BW, BW), F32)], name="ret_bwd",
        compiler_params=_cp())(lgf, lgb, do, q, k, v)


def _pool_windows(t):
    row = lax.broadcasted_iota(jnp.int32, (t, BW), 0)
    half = lax.shift_left(jnp.ones((t, BW), jnp.int32), _lane_head((t, BW)))
    cnt = (jnp.minimum(row + half, t) - jnp.maximum(row - half, 0)).astype(F32)
    return row, half, cnt


def _pool_window_sum(v, row, half, t, transpose):
    out = jnp.zeros_like(v)
    for j in range(-POOL_HALF_MAX, POOL_HALF_MAX):
        src = row - j if transpose else row + j
        ok = (src >= 0) & (src < t) & (j >= -half) & (j < half)
        out = out + jnp.where(ok, pltpu.roll(v, (j if transpose else -j) % t, 0), 0.0)
    return out


def _pool_fwd(proj, wbd, scale):
    t = proj.shape[0]

    def body(v_ref, w_ref, s_ref, o_ref):
        v = v_ref[...]
        row, half, cnt = _pool_windows(t)
        pooled = _pool_window_sum(v, row, half, t, False) / cnt - v
        o_ref[...] = (_dot(pooled, w_ref[...]) * s_ref[...]).astype(BF16)

    return pl.pallas_call(
        body, grid=(1,),
        in_specs=[pl.BlockSpec((t, BW), lambda i: (0, PV)), pl.BlockSpec((BW, BW), lambda i: (0, 0)), pl.BlockSpec((1, BW), lambda i: (0, 0))],
        out_specs=pl.BlockSpec((t, BW), lambda i: (0, 0)), out_shape=_sds((t, BW), BF16), name="pool_fwd",
        compiler_params=_cp())(proj, wbd, scale)


def _pool_bwd(dbr, proj, wbd, scale):
    t = proj.shape[0]

    def body(d_ref, v_ref, w_ref, s_ref, dv_ref, dw_ref, ds_ref):
        v, dout = v_ref[...], d_ref[...]
        row, half, cnt = _pool_windows(t)
        pooled = _pool_window_sum(v, row, half, t, False) / cnt - v
        mixed = _dot(pooled, w_ref[...])
        ds_ref[...] = jnp.sum(dout * mixed, axis=0, keepdims=True)
        dmixed = dout * s_ref[...]
        dw_ref[...] = _dot(pooled, dmixed, ta=True)
        dpooled = _dot(dmixed, w_ref[...], tb=True)
        dv_ref[...] = _pool_window_sum(dpooled / cnt, row, half, t, True) - dpooled

    return pl.pallas_call(
        body, grid=(1,),
        in_specs=[pl.BlockSpec((t, BW), lambda i: (0, 1)), pl.BlockSpec((t, BW), lambda i: (0, PV)),
                  pl.BlockSpec((BW, BW), lambda i: (0, 0)), pl.BlockSpec((1, BW), lambda i: (0, 0))],
        out_specs=(pl.BlockSpec((t, BW), lambda i: (0, 0)), pl.BlockSpec((BW, BW), lambda i: (0, 0)), pl.BlockSpec((1, BW), lambda i: (0, 0))),
        out_shape=(_sds((t, BW), F32), _sds((BW, BW), F32), _sds((1, BW), F32)), name="pool_bwd",
        compiler_params=_cp())(dbr, proj, wbd, scale)


NA_KEYS = NA_ROWS_WIN * GRID_W


def _na_window(r, n_rows):
    rs = jnp.clip(r - NA_ROWS_WIN // 2, 0, n_rows - NA_ROWS_WIN)
    return pl.multiple_of(rs * GRID_W, GRID_W), rs - r + (NA_ROWS_WIN - 1)


NA_STEP_ROWS = 4


def _na_fwd(q, k, v, ball):
    t = q.shape[0]
    n_rows = t // GRID_W
    rows = NA_STEP_ROWS

    def body(q_ref, k_ref, v_ref, b_ref, o_ref):
        for rr in range(rows):
            start, a0 = _na_window(pl.program_id(0) * rows + rr, n_rows)
            own = slice(rr * GRID_W, (rr + 1) * GRID_W)
            qs = _stack_heads(q_ref[own, :])
            s = _dot(qs, k_ref[pl.ds(start, NA_KEYS), :], tb=True) * (HD ** -0.5) + b_ref[a0]
            p = _softmax_rows(s)
            o_ref[own, :] = _unstack_heads(_dot(p, v_ref[pl.ds(start, NA_KEYS), :]), GRID_W).astype(BF16)

    blk = pl.BlockSpec((rows * GRID_W, BW), lambda r: (r, 0))
    whole = pl.BlockSpec((t, BW), lambda r: (0, 0))
    return pl.pallas_call(
        body, grid=(n_rows // rows,), in_specs=[blk, whole, whole, pl.BlockSpec(ball.shape, lambda r: (0, 0, 0))],
        out_specs=blk, out_shape=_sds((t, BW), BF16), name="na_fwd", compiler_params=_cp())(q, k, v, ball)


def _na_bwd(dbr, q, k, v, ball):
    t = q.shape[0]
    n_rows = t // GRID_W

    rows = NA_STEP_ROWS

    def body(d_ref, q_ref, k_ref, v_ref, b_ref, dq_ref, dk_ref, dv_ref, db_ref):
        @pl.when(pl.program_id(0) == 0)
        def _():
            dk_ref[...] = jnp.zeros_like(dk_ref)
            dv_ref[...] = jnp.zeros_like(dv_ref)
            db_ref[...] = jnp.zeros_like(db_ref)

        for rr in range(rows):
            start, a0 = _na_window(pl.program_id(0) * rows + rr, n_rows)
            keys = pl.ds(start, NA_KEYS)
            own = slice(rr * GRID_W, (rr + 1) * GRID_W)
            qs = _stack_heads(q_ref[own, :])
            kb, vb = k_ref[keys, :], v_ref[keys, :]
            p = _softmax_rows(_dot(qs, kb, tb=True) * (HD ** -0.5) + b_ref[a0])
            dos = _stack_heads(d_ref[own, :]).astype(MXU)
            dp = _dot(dos, vb, tb=True)
            dv_ref[keys, :] += _dot(p, dos, ta=True)
            ds = p * (dp - jnp.sum(dp * p, axis=-1, keepdims=True))
            db_ref[a0] += ds
            dsb = (ds * (HD ** -0.5)).astype(MXU)
            dq_ref[own, :] = _unstack_heads(_dot(dsb, kb), GRID_W)
            dk_ref[keys, :] += _dot(dsb, qs, ta=True)

    blk = pl.BlockSpec((rows * GRID_W, BW), lambda r: (r, 0))
    whole = pl.BlockSpec((t, BW), lambda r: (0, 0))
    tab = pl.BlockSpec(ball.shape, lambda r: (0, 0, 0))
    return pl.pallas_call(
        body, grid=(n_rows // rows,), in_specs=[pl.BlockSpec((rows * GRID_W, BW), lambda r: (r, 2)), blk, whole, whole, tab],
        out_specs=(blk, whole, whole, tab),
        out_shape=(_sds((t, BW), F32), _sds((t, BW), F32), _sds((t, BW), F32), _sds(ball.shape, F32)), name="na_bwd",
        compiler_params=_cp())(dbr, q, k, v, ball)


def _rpb_expand(rpb_pad, onehot):
    def body(r_ref, e_ref, o_ref):
        o_ref[...] = jnp.dot(r_ref[...], e_ref[...], precision=HI, preferred_element_type=F32)

    return pl.pallas_call(body, out_shape=_sds((64, GRID_W * GRID_W), F32), name="rpb_expand", compiler_params=_cp())(rpb_pad, onehot)


def _rpb_reduce(dtab, onehot):
    def body(d_ref, e_ref, o_ref):
        o_ref[...] = lax.dot_general(d_ref[...], e_ref[...], (((1,), (1,)), ((), ())), precision=HI, preferred_element_type=F32)

    return pl.pallas_call(body, out_shape=_sds((64, 128), F32), name="rpb_reduce", compiler_params=_cp())(dtab, onehot)


MEM_TQ = 256


def _mem_fwd(q, mk, mv):
    t = q.shape[0]
    tq = MEM_TQ

    def body(q_ref, k_ref, v_ref, o_ref):
        qv = q_ref[...]
        head = _lane_head(qv.shape)
        out = jnp.zeros((tq, BW), F32)
        for h in range(NH):
            p = _softmax_rows(_dot(jnp.where(head == h, qv, jnp.zeros_like(qv)), k_ref[...], tb=True) * (HD ** -0.5))
            out = out + jnp.where(head == h, _dot(p, v_ref[...]), 0.0)
        o_ref[...] = out.astype(BF16)

    blk = pl.BlockSpec((tq, BW), lambda i: (i, 0))
    kv = pl.BlockSpec((N_MEM, BW), lambda i: (0, 0))
    return pl.pallas_call(body, grid=(t // tq,), in_specs=[blk, kv, kv], out_specs=blk, out_shape=_sds((t, BW), BF16),
                          name="mem_fwd", compiler_params=_cp())(q, mk, mv)


def _mem_bwd(dbr, q, mk, mv):
    t = q.shape[0]
    tq = MEM_TQ

    def body(d_ref, q_ref, k_ref, v_ref, dq_ref, dk_ref, dv_ref):
        first = pl.program_id(0) == 0
        qv, dout = q_ref[...], d_ref[...]
        head = _lane_head(qv.shape)
        dq = jnp.zeros((tq, BW), F32)
        dk = jnp.zeros((N_MEM, BW), F32)
        dv = jnp.zeros((N_MEM, BW), F32)
        for h in range(NH):
            qh = jnp.where(head == h, qv, jnp.zeros_like(qv))
            doh = jnp.where(head == h, dout, 0.0).astype(MXU)
            p = _softmax_rows(_dot(qh, k_ref[...], tb=True) * (HD ** -0.5))
            dp = _dot(doh, v_ref[...], tb=True)
            dv = dv + _dot(p, doh, ta=True)
            dsb = (p * (dp - jnp.sum(dp * p, axis=-1, keepdims=True)) * (HD ** -0.5)).astype(MXU)
            dq = dq + jnp.where(head == h, _dot(dsb, k_ref[...]), 0.0)
            dk = dk + _dot(dsb, qh, ta=True)
        dq_ref[...] = dq
        _acc(dk_ref, dk, first)
        _acc(dv_ref, dv, first)

    blk = pl.BlockSpec((tq, BW), lambda i: (i, 0))
    kv = pl.BlockSpec((N_MEM, BW), lambda i: (0, 0))
    return pl.pallas_call(
        body, grid=(t // tq,), in_specs=[pl.BlockSpec((tq, BW), lambda i: (i, 3)), blk, kv, kv], out_specs=(blk, kv, kv),
        out_shape=(_sds((t, BW), F32), _sds((N_MEM, BW), F32), _sds((N_MEM, BW), F32)), name="mem_bwd",
        compiler_params=_cp())(dbr, q, mk, mv)


def _memkv_prep(kv, g_mk):
    def body(kv_ref, g_ref, k_ref, v_ref):
        k_ref[...] = _gnorm(kv_ref[:, 0:BW], g_ref[...]).astype(BF16)
        v_ref[...] = kv_ref[:, BW:2 * BW].astype(BF16)

    return pl.pallas_call(body, out_shape=(_sds((N_MEM, BW), BF16), _sds((N_MEM, BW), BF16)), name="memkv_prep",
                          compiler_params=_cp())(kv, g_mk)


def _memkv_bwd(kv, dk, dv, g_mk):
    def body(kv_ref, dk_ref, dv_ref, g_ref, o_ref, dg_ref):
        dkk, gain = _gnorm_bwd(dk_ref[...], kv_ref[:, 0:BW], g_ref[...])
        o_ref[:, 0:BW] = dkk.astype(BF16)
        o_ref[:, BW:2 * BW] = dv_ref[...].astype(BF16)
        dg_ref[...] = jnp.sum(gain, axis=0, keepdims=True)

    return pl.pallas_call(body, out_shape=(_sds((N_MEM, 2 * BW), BF16), _sds((1, BW), F32)), name="memkv_bwd",
                          compiler_params=_cp())(kv, dk, dv, g_mk)


MERGE_TM = 256


def _merge_fwd(brs, wbt, gp):
    t = gp.shape[0]
    tm = MERGE_TM

    def body(b0, b1, b2, b3, wb_ref, gp_ref, o_ref):
        out = jnp.zeros((tm, D), F32)
        for n, b_ref in enumerate((b0, b1, b2, b3)):
            up = _dot(b_ref[...], wb_ref[n], tb=True)
            out = out + _sigmoid(gp_ref[:, n * D:(n + 1) * D].astype(F32)) * up
        o_ref[...] = out.astype(BF16)

    blk = pl.BlockSpec((tm, BW), lambda i: (i, 0))
    return pl.pallas_call(
        body, grid=(t // tm,),
        in_specs=[blk, blk, blk, blk, pl.BlockSpec((NH, D, BW), lambda i: (0, 0, 0)), pl.BlockSpec((tm, NH * D), lambda i: (i, 0))],
        out_specs=pl.BlockSpec((tm, D), lambda i: (i, 0)), out_shape=_sds((t, D), BF16), name="merge_fwd",
        compiler_params=_cp())(*brs, wbt, gp)


def _merge_bwd(dmerged, brs, wbt, gp):
    t = gp.shape[0]
    tm = MERGE_TM

    def body(d_ref, b0, b1, b2, b3, wb_ref, gp_ref, dgp_ref, dup_ref):
        dm = d_ref[...]
        for n, b_ref in enumerate((b0, b1, b2, b3)):
            up = _dot(b_ref[...], wb_ref[n], tb=True)
            g = _sigmoid(gp_ref[:, n * D:(n + 1) * D].astype(F32))
            dgp_ref[:, n * D:(n + 1) * D] = (dm * up * (g * (1.0 - g))).astype(BF16)
            dup_ref[:, n * D:(n + 1) * D] = (dm * g).astype(BF16)

    row = pl.BlockSpec((tm, D), lambda i: (i, 0))
    blk = pl.BlockSpec((tm, BW), lambda i: (i, 0))
    wide = pl.BlockSpec((tm, NH * D), lambda i: (i, 0))
    return pl.pallas_call(
        body, grid=(t // tm,), in_specs=[row, blk, blk, blk, blk, pl.BlockSpec((NH, D, BW), lambda i: (0, 0, 0)), wide],
        out_specs=(wide, wide), out_shape=(_sds((t, NH * D), BF16), _sds((t, NH * D), BF16)), name="merge_bwd",
        compiler_params=_cp())(dmerged, *brs, wbt, gp)


def _dbranch(dup, wbt):
    t = dup.shape[0]
    tm = 512

    def body(d_ref, w_ref, o_ref):
        o_ref[...] = _dot(d_ref[...], w_ref[...])

    return pl.pallas_call(
        body, grid=(t // tm, NH), in_specs=[pl.BlockSpec((tm, D), lambda i, n: (i, n)), pl.BlockSpec((None, D, BW), lambda i, n: (n, 0, 0))],
        out_specs=pl.BlockSpec((tm, BW), lambda i, n: (i, n)), out_shape=_sds((t, NH * BW), F32), name="dbranch",
        compiler_params=_cp())(dup, wbt)


def _dwbranch(brs, dup):
    t = dup.shape[0]

    def body(b0, b1, b2, b3, d_ref, o_ref):
        for n, b_ref in enumerate((b0, b1, b2, b3)):
            o_ref[n] = _dot(d_ref[:, n * D:(n + 1) * D], b_ref[...], ta=True).astype(BF16)

    return pl.pallas_call(body, out_shape=_sds((NH, D, BW), BF16), name="dwbranch", compiler_params=_cp())(*brs, dup)


def _swiglu_fwd(ag):
    t = ag.shape[0]
    tm = 256

    def body(ag_ref, o_ref):
        a, g = ag_ref[:, 0:FF].astype(F32), ag_ref[:, FF:2 * FF].astype(F32)
        o_ref[...] = (a * _sigmoid(a) * g).astype(BF16)

    return pl.pallas_call(body, grid=(t // tm,), in_specs=[pl.BlockSpec((tm, 2 * FF), lambda i: (i, 0))],
                          out_specs=pl.BlockSpec((tm, FF), lambda i: (i, 0)), out_shape=_sds((t, FF), BF16), name="swiglu_fwd",
                          compiler_params=_cp())(ag)


def _swiglu_bwd(ag, dy):
    t = ag.shape[0]
    tm = 256

    def body(ag_ref, dy_ref, o_ref):
        a, g, d = ag_ref[:, 0:FF].astype(F32), ag_ref[:, FF:2 * FF].astype(F32), dy_ref[...].astype(F32)
        s = _sigmoid(a)
        o_ref[:, 0:FF] = (d * g * (s * (1.0 + a * (1.0 - s)))).astype(BF16)
        o_ref[:, FF:2 * FF] = (d * (a * s)).astype(BF16)

    return pl.pallas_call(
        body, grid=(t // tm,), in_specs=[pl.BlockSpec((tm, 2 * FF), lambda i: (i, 0)), pl.BlockSpec((tm, FF), lambda i: (i, 0))],
        out_specs=pl.BlockSpec((tm, 2 * FF), lambda i: (i, 0)), out_shape=_sds((t, 2 * FF), BF16), name="swiglu_bwd",
        compiler_params=_cp())(ag, dy)


def _loss_head(y, target):
    t, d = y.shape
    tm = 256

    def body(y_ref, t_ref, dy_ref, dyb_ref, l_ref):
        e = y_ref[...] - t_ref[...]
        dy_ref[...] = e * (1.0 / d)
        dyb_ref[...] = (e * (1.0 / d)).astype(BF16)
        _acc(l_ref, jnp.full((8, 128), 0.5 * jnp.sum(jnp.sum(e * e, axis=-1, keepdims=True) * (1.0 / d)), F32), pl.program_id(0) == 0)

    row = pl.BlockSpec((tm, d), lambda i: (i, 0))
    return pl.pallas_call(body, grid=(t // tm,), in_specs=[row, row], out_specs=(row, row, pl.BlockSpec((8, 128), lambda i: (0, 0))),
                          out_shape=(_sds((t, d), F32), _sds((t, d), BF16), _sds((8, 128), F32)), name="loss_head",
                          compiler_params=_cp())(y, target)


def _sum_slots(x, name):
    k, r, c = x.shape
    tr = _tile(r, 512) if r % 128 == 0 else r

    def body(x_ref, o_ref):
        acc = x_ref[0].astype(F32)
        for s in range(1, k):
            acc = acc + x_ref[s].astype(F32)
        o_ref[...] = acc

    return pl.pallas_call(body, grid=(r // tr,), in_specs=[pl.BlockSpec((k, tr, c), lambda i: (0, i, 0))],
                          out_specs=pl.BlockSpec((tr, c), lambda i: (i, 0)), out_shape=_sds((r, c), F32), name=name,
                          compiler_params=_cp())(x)


def _pair_sum(bufs, recvs, cidx):
    n = len(bufs)

    def body(c_ref, *refs):
        for i in range(n):
            refs[2 * n + i][...] = (refs[i][...].astype(F32) + refs[n + i][...].astype(F32)).astype(BF16)

    return pl.pallas_call(
        body,
        grid_spec=pltpu.PrefetchScalarGridSpec(
            num_scalar_prefetch=1, grid=(4,),
            in_specs=[pl.BlockSpec((None, None) + b.shape[2:], lambda s, cref: (s, cref[0], 0, 0)) for b in bufs]
            + [pl.BlockSpec((None,) + r.shape[1:], lambda s, cref: (s, 0, 0)) for r in recvs],
            out_specs=tuple(pl.BlockSpec((None,) + r.shape[1:], lambda s, cref: (s, 0, 0)) for r in recvs)),
        out_shape=tuple(_sds(r.shape, BF16) for r in recvs), name="rs_pair_sum", compiler_params=_cp())(cidx, *bufs, *recvs)


def _adamw_update(w, gv, m, v):
    mn = ADAM_B1 * m + (1.0 - ADAM_B1) * gv
    vn = ADAM_B2 * v + (1.0 - ADAM_B2) * (gv * gv)
    m_hat = mn / (1.0 - ADAM_B1 ** ADAM_STEP)
    v_hat = vn / (1.0 - ADAM_B2 ** ADAM_STEP)
    return -ADAM_LR * (m_hat / (jnp.sqrt(v_hat) + ADAM_EPS) + ADAM_WD * w), mn, vn


def _adamw(w, g, m, v, name):
    r, c = w.shape

    def body(w_ref, g_ref, m_ref, v_ref, d_ref, nm_ref, nv_ref):
        d_ref[...], nm_ref[...], nv_ref[...] = _adamw_update(w_ref[...], g_ref[...], m_ref[...], v_ref[...])

    blk = pl.BlockSpec((r, c), lambda i: (0, 0))
    return pl.pallas_call(body, grid=(1,), in_specs=[blk] * 4, out_specs=(blk,) * 3,
                          out_shape=tuple(_sds((r, c), F32) for _ in range(3)), name=name, compiler_params=_cp())(w, g, m, v)


def _adamw_layer(layer, w, g, m, v, outs, name):
    _, r, c = w.shape
    tr = max(d for d in range(8, r + 1, 8) if r % d == 0 and d * c * 4 <= 2 ** 20)

    def body(w_ref, m_ref, v_ref, g_ref, *refs):
        d_ref, nm_ref, nv_ref, go_ref = refs[4:]
        gv = g_ref[...]
        d_ref[...], nm_ref[...], nv_ref[...] = _adamw_update(w_ref[...], gv, m_ref[...], v_ref[...])
        go_ref[...] = gv

    blk = pl.BlockSpec((None, tr, c), lambda i: (layer, i, 0))
    return pl.pallas_call(
        body, grid=(r // tr,), in_specs=[blk] * 3 + [pl.BlockSpec((tr, c), lambda i: (i, 0))] + [ANY] * 4, out_specs=(blk,) * 4,
        out_shape=tuple(_sds(w.shape, F32) for _ in range(4)), input_output_aliases={4 + j: j for j in range(4)}, name=name,
        compiler_params=_cp())(w, m, v, g, *outs)


def _all_gather(shards, name):
    n = len(shards)

    def body(*refs):
        x_refs, out_refs = refs[:n], refs[n:2 * n]
        send_sems, recv_sems, local_sems = refs[2 * n:]
        x, y, cc = lax.axis_index("x"), lax.axis_index("y"), lax.axis_index("c")
        me, sibling = (x, y, cc), (x, y, 1 - cc)
        chips = [(1 - x, y), (x, 1 - y), (1 - x, 1 - y)]

        def copy(i, k, block, to, own=False):
            px, py, pc = block
            slot = out_refs[i].at[4 * px + 2 * py + pc]
            return pltpu.make_async_remote_copy(
                src_ref=x_refs[i] if own else slot, dst_ref=slot, send_sem=send_sems.at[7 * i + k],
                recv_sem=recv_sems.at[7 * i + k], device_id=to, device_id_type=MESH)

        mine = [pltpu.make_async_copy(x_refs[i], out_refs[i].at[4 * x + 2 * y + cc], local_sems.at[i]) for i in range(n)]
        for cp in mine:
            cp.start()
        first = []
        for j, chip in enumerate(chips):
            first += [copy(i, 1 + j, me, (*chip, cc), own=True) for i in range(n)]
        first += [copy(i, 0, me, sibling, own=True) for i in range(n)]
        for cp in first:
            cp.start()
        passed = []
        for j, chip in enumerate(chips):
            for i in range(n):
                copy(i, 1 + j, (*chip, cc), me).wait_recv()
                cp = copy(i, 4 + j, (*chip, cc), sibling)
                cp.start()
                passed.append(cp)
        for i in range(n):
            copy(i, 0, sibling, me).wait_recv()
        for j, chip in enumerate(chips):
            for i in range(n):
                copy(i, 4 + j, (*chip, 1 - cc), me).wait_recv()
        for cp in first + passed:
            cp.wait_send()
        for cp in mine:
            cp.wait()

    return pl.pallas_call(
        body, out_shape=tuple(_sds((N_DEV,) + s.shape, s.dtype) for s in shards), in_specs=[ANY] * n, out_specs=(ANY,) * n,
        scratch_shapes=[pltpu.SemaphoreType.DMA((7 * n,)), pltpu.SemaphoreType.DMA((7 * n,)), pltpu.SemaphoreType.DMA((n,))],
        name=name)(*shards)


def _rs_core_swap(bufs, name):
    n = len(bufs)

    def body(*refs):
        b_refs, recv_refs = refs[:n], refs[n:2 * n]
        send_sems, recv_sems = refs[2 * n:]
        x, y, cc = lax.axis_index("x"), lax.axis_index("y"), lax.axis_index("c")
        copies = [pltpu.make_async_remote_copy(
            src_ref=b_refs[i].at[s, 1 - cc], dst_ref=recv_refs[i].at[s], send_sem=send_sems.at[4 * i + s],
            recv_sem=recv_sems.at[4 * i + s], device_id=(x, y, 1 - cc), device_id_type=MESH) for i in range(n) for s in range(4)]
        for cp in copies:
            cp.start()
        for cp in copies:
            cp.wait()

    return pl.pallas_call(
        body, out_shape=tuple(_sds((4,) + b.shape[2:], b.dtype) for b in bufs), in_specs=[ANY] * n, out_specs=(ANY,) * n,
        scratch_shapes=[pltpu.SemaphoreType.DMA((4 * n,)), pltpu.SemaphoreType.DMA((4 * n,))], name=name)(*bufs)


HBM = pl.BlockSpec(memory_space=pltpu.HBM)
SEMS = pl.BlockSpec(memory_space=pltpu.SEMAPHORE)
EFFECT = pltpu.SideEffectType.DATAFLOW_SIDE_EFFECTING


def _hbm(a):
    return pltpu.HBM(a.shape, a.dtype)


def _other_chips(x, y):
    return [(1 - x, y), (x, 1 - y), (1 - x, 1 - y)]


def _ici_start(srcs, lands, mode, name):
    n = len(srcs)

    def body(*refs):
        s_refs, land_refs = refs[:n], refs[n:2 * n]
        send_sems, recv_sems = refs[2 * n], refs[2 * n + 1]
        token = refs[-1]
        x, y, cc = lax.axis_index("x"), lax.axis_index("y"), lax.axis_index("c")
        mine = 2 * x + y if mode == "by_chip" else 4 * x + 2 * y + cc
        peers = [(px, py, cc) for px, py in _other_chips(x, y)]
        if mode == "by_device":
            peers = [(x, y, 1 - cc)] + peers + [(px, py, 1 - cc) for px, py in _other_chips(x, y)]
        for px, py, pc in peers:
            for i in range(n):
                src = s_refs[i]
                if mode == "by_chip":
                    src = src.at[2 * px + py]
                elif mode == "by_device":
                    src = src.at[4 * px + 2 * py + pc]
                pltpu.make_async_remote_copy(
                    src_ref=src, dst_ref=land_refs[i].at[mine], send_sem=send_sems.at[i], recv_sem=recv_sems.at[i],
                    device_id=(px, py, pc), device_id_type=MESH).start()
        token[...] = jnp.zeros_like(token)

    out = pl.pallas_call(
        body, name=name,
        out_shape=(pltpu.SemaphoreType.DMA((n,)), pltpu.SemaphoreType.DMA((n,)), *[_hbm(s) for s in srcs], *[_hbm(l) for l in lands],
                   _sds((8, 128), F32)),
        in_specs=[HBM] * (2 * n), out_specs=(SEMS, SEMS, *[HBM] * (2 * n), pl.BlockSpec(memory_space=pltpu.VMEM)),
        input_output_aliases={i: 2 + i for i in range(2 * n)}, compiler_params=pltpu.CompilerParams(has_side_effects=EFFECT),
    )(*[pltpu.with_memory_space_constraint(s, pltpu.HBM) for s in srcs],
      *[pltpu.with_memory_space_constraint(l, pltpu.HBM) for l in lands])
    return out[0], out[1], out[2:2 + n], out[2 + n:2 + 2 * n], out[-1], 7 if mode == "by_device" else 3


def _ici_wait(started, after, name, only=None):
    send_sems, recv_sems, srcs, lands, _, copies = started
    only = list(range(len(srcs))) if only is None else only
    srcs, lands = [srcs[i] for i in only], [lands[i] for i in only]
    n = len(srcs)

    def body(*refs):
        land_refs = refs[n:2 * n]
        send_sems, recv_sems = refs[2 * n], refs[2 * n + 1]
        x, y, cc = lax.axis_index("x"), lax.axis_index("y"), lax.axis_index("c")
        for i in range(n):
            three = land_refs[i].at[pl.ds(0, copies)]
            cp = pltpu.make_async_remote_copy(src_ref=three, dst_ref=three, send_sem=send_sems.at[only[i]],
                                              recv_sem=recv_sems.at[only[i]],
                                              device_id=(x, y, cc), device_id_type=MESH)
            cp.wait_send()
            cp.wait_recv()

    return pl.pallas_call(
        body, name=name, out_shape=tuple(_hbm(l) for l in lands), in_specs=[HBM] * (2 * n) + [SEMS, SEMS, ANY],
        out_specs=tuple([HBM] * n), input_output_aliases={n + i: i for i in range(n)},
        compiler_params=pltpu.CompilerParams(has_side_effects=EFFECT))(*srcs, *lands, send_sems, recv_sems, after)


def _gather_d2d(blocks, lands, name):
    n = len(blocks)

    def body(*refs):
        x_refs, land_refs = refs[:n], refs[2 * n:3 * n]
        send_sems, recv_sems, in_sems, out_sems = refs[3 * n:3 * n + 4]
        stage = refs[3 * n + 4:]
        x, y, cc = lax.axis_index("x"), lax.axis_index("y"), lax.axis_index("c")
        sibling = (x, y, 1 - cc)
        staged = [pltpu.make_async_copy(x_refs[i], stage[i], in_sems.at[i]) for i in range(n)]
        for cp in staged:
            cp.start()
        copies = []
        for i in range(n):
            slot = land_refs[i].at[4 * x + 2 * y + cc]
            copies.append(pltpu.make_async_remote_copy(src_ref=x_refs[i], dst_ref=slot, send_sem=send_sems.at[4 * i],
                                                       recv_sem=recv_sems.at[4 * i], device_id=sibling, device_id_type=MESH))
            for j, (px, py) in enumerate(_other_chips(x, y)):
                slot = land_refs[i].at[4 * px + 2 * py + cc]
                copies.append(pltpu.make_async_remote_copy(src_ref=slot, dst_ref=slot, send_sem=send_sems.at[4 * i + 1 + j],
                                                           recv_sem=recv_sems.at[4 * i + 1 + j], device_id=sibling, device_id_type=MESH))
        for cp in copies:
            cp.start()
        mine = []
        for i in range(n):
            staged[i].wait()
            mine.append(pltpu.make_async_copy(stage[i], land_refs[i].at[4 * x + 2 * y + cc], out_sems.at[i]))
            mine[i].start()
        for i in range(n):
            slot = land_refs[i].at[4 * x + 2 * y + (1 - cc)]
            pltpu.make_async_remote_copy(src_ref=slot, dst_ref=slot, send_sem=send_sems.at[4 * i], recv_sem=recv_sems.at[4 * i],
                                         device_id=sibling, device_id_type=MESH).wait_recv()
            for j, (px, py) in enumerate(_other_chips(x, y)):
                slot = land_refs[i].at[4 * px + 2 * py + (1 - cc)]
                pltpu.make_async_remote_copy(src_ref=slot, dst_ref=slot, send_sem=send_sems.at[4 * i + 1 + j],
                                             recv_sem=recv_sems.at[4 * i + 1 + j], device_id=sibling, device_id_type=MESH).wait_recv()
        for cp in copies:
            cp.wait_send()
        for cp in mine:
            cp.wait()

    return pl.pallas_call(
        body, out_shape=tuple(_sds(l.shape, l.dtype) for l in lands), in_specs=[ANY] * (2 * n), out_specs=(ANY,) * n,
        input_output_aliases={n + i: i for i in range(n)},
        scratch_shapes=[pltpu.SemaphoreType.DMA((4 * n,)), pltpu.SemaphoreType.DMA((4 * n,)), pltpu.SemaphoreType.DMA((n,)),
                        pltpu.SemaphoreType.DMA((n,))] + [pltpu.VMEM(b.shape, b.dtype) for b in blocks],
        name=name, compiler_params=_cp())(*blocks, *lands)


def _sum_own(parts, recvs, mine, name):
    n = len(parts)

    def body(c_ref, *refs):
        s = pl.program_id(0)
        for i in range(n):
            val = jnp.where(c_ref[0] == s, refs[i][...], refs[n + i][...]).astype(F32)
            _acc(refs[2 * n + i], val, s == 0)

    kept = [pl.BlockSpec((None,) + p.shape[1:], lambda s, cref: (cref[0], 0, 0)) for p in parts]
    ins = [pl.BlockSpec((None,) + p.shape[1:], lambda s, cref: (s, 0, 0)) for p in parts]
    return pl.pallas_call(
        body, grid_spec=pltpu.PrefetchScalarGridSpec(
            num_scalar_prefetch=1, grid=(parts[0].shape[0],), in_specs=kept + ins,
            out_specs=tuple(pl.BlockSpec(p.shape[1:], lambda s, cref: (0, 0)) for p in parts)),
        out_shape=tuple(_sds(p.shape[1:], F32) for p in parts), name=name, compiler_params=_cp())(mine, *parts, *recvs)


BIG = (("w_in", True), ("w_gate", True), ("w_mem_kv", False), ("w_branch", True), ("w_out", False), ("w_ffn_in", True),
       ("w_ffn_out", False))

SMALL = ("norm_mix_g", "norm_mem_g", "ret_decay_fwd", "ret_decay_bwd", "ret_norm_g", "pool_w", "pool_scale", "na_q_norm_g",
         "na_k_norm_g", "na_rpb", "mem_q_norm_g", "mem_k_norm_g", "norm_ffn_g")
WEIGHTS = ("norm_mix_g", "norm_mem_g", "w_in", "w_gate", "ret_decay_fwd", "ret_decay_bwd", "ret_norm_g", "pool_w", "pool_scale",
           "na_q_norm_g", "na_k_norm_g", "na_rpb", "mem_q_norm_g", "mem_k_norm_g", "w_mem_kv", "w_branch", "w_out", "norm_ffn_g",
           "w_ffn_in", "w_ffn_out")


def _to_exchange(name, transposed, shard):
    if name == "w_branch":
        return jnp.swapaxes(shard, 1, 2).reshape(NH * (D // N_DEV), BW)
    return shard.T if transposed else shard


def _from_exchange(name, transposed, block):
    if name == "w_branch":
        return jnp.swapaxes(block.reshape(NH, D // N_DEV, BW), 1, 2)
    return block.T if transposed else block


def _whole_from_gathered(name, g):
    if name == "w_branch":
        return jnp.swapaxes(g.reshape(N_DEV, NH, D // N_DEV, BW), 0, 1).reshape(NH, D, BW)
    return g.reshape(N_DEV * g.shape[1], g.shape[2])


def _by_destination(name, g):
    if name == "w_branch":
        g = jnp.swapaxes(g.reshape(NH, N_DEV, D // N_DEV, BW), 0, 1).reshape(N_DEV * NH * (D // N_DEV), BW)
    return g.reshape(4, 2, g.shape[0] // N_DEV, g.shape[1])


SMALL_PAD = 1024


def _pack_small(vals, loss=None):
    parts = [vals[n] for n in SMALL] + [jnp.zeros((1,), F32) if loss is None else loss.reshape(1)]
    rows = []
    for p in parts:
        flat = p.reshape(-1)
        rows.append(jnp.pad(flat, (0, -flat.shape[0] % SMALL_PAD)).reshape(-1, 128))
    return jnp.concatenate(rows, axis=0)


def _unpack_small(packed, like):
    out, off = {}, 0
    for n in SMALL:
        sz = int(np.prod(like[n].shape))
        nrow = -(-sz // SMALL_PAD) * (SMALL_PAD // 128)
        out[n] = packed[off:off + nrow].reshape(-1)[:sz].reshape(like[n].shape)
        off += nrow
    return out, packed[off, 0]


def _na_constants():
    c = np.arange(GRID_W)
    win = np.clip(c - NA_COLS_WIN // 2, 0, GRID_W - NA_COLS_WIN)
    kc = np.arange(GRID_W)
    inside = (kc[None, :] >= win[:, None]) & (kc[None, :] < win[:, None] + NA_COLS_WIN)
    off = kc[None, :] - c[:, None] + NA_COLS_WIN - 1
    onehot = np.zeros((128, GRID_W, GRID_W), np.float32)
    for b in range(2 * NA_COLS_WIN - 1):
        onehot[b] = (off == b) & inside
    maskadd = np.where(inside, 0.0, NEG).astype(np.float32)
    return onehot.reshape(128, GRID_W * GRID_W), maskadd


def _na_bias_table(tab, maskadd):
    n_off = 2 * NA_ROWS_WIN - 1
    t4 = tab[:NH * n_off].reshape(NH, n_off, GRID_W, GRID_W) + maskadd[None, None]
    ball = jnp.stack([t4[:, a0:a0 + NA_ROWS_WIN] for a0 in range(NA_ROWS_WIN)], axis=1)
    return ball.transpose(1, 0, 3, 2, 4).reshape(NA_ROWS_WIN, NH * GRID_W, NA_KEYS)


def _rotary_tables(t):
    half = HD // 2
    inv = ROPE_THETA ** (-jnp.arange(half, dtype=F32) / half)
    ang = jnp.arange(t, dtype=F32)[:, None] * inv[None, :]
    cos, sin = jnp.cos(ang), jnp.sin(ang)
    return jnp.tile(jnp.concatenate([cos, cos], axis=-1), (1, NH)), jnp.tile(jnp.concatenate([-sin, sin], axis=-1), (1, NH))


def _block_diag(pw):
    out = jnp.zeros((BW, BW), pw.dtype)
    for g in range(NH):
        out = lax.dynamic_update_slice(out, pw[g], (g * HD, g * HD))
    return out


def _tile4(g):
    return jnp.tile(g.reshape(1, HD), (1, NH))


def _layer_fwd(x, mem, sw, lw, consts, more_weights=None):
    cos2, sin2, onehot, maskadd = consts
    h = _rmsnorm_fwd(x, sw["norm_mix_g"].reshape(1, D), "norm_mix_fwd")
    proj = _mm(h, lw["w_in"], tb=True, name="mm_in")
    gp = _mm(h, lw["w_gate"], tb=True, out_dtype=BF16, name="mm_gate")
    g_naq, g_nak, g_mq = _tile4(sw["na_q_norm_g"]), _tile4(sw["na_k_norm_g"]), _tile4(sw["mem_q_norm_g"])
    rq, rk, rv, nq, nk, nv, mq = _prep_fwd(proj, cos2, sin2, g_naq, g_nak, g_mq)

    lgf, lgb = jax.nn.log_sigmoid(sw["ret_decay_fwd"]), jax.nn.log_sigmoid(sw["ret_decay_bwd"])
    g_ret = sw["ret_norm_g"].reshape(1, BW)
    o_ret, ret = _ret_fwd(rq, rk, rv, proj, lgf, lgb, g_ret)

    wbd = _block_diag(sw["pool_w"]).astype(BF16)
    p_scale = sw["pool_scale"].reshape(1, BW)
    pool = _pool_fwd(proj, wbd, p_scale)

    rpb_pad = jnp.pad(sw["na_rpb"].reshape(NH * 15, 31), ((0, 4), (0, 97)))
    ball = _na_bias_table(_rpb_expand(rpb_pad, onehot), maskadd)
    na = _na_fwd(nq, nk, nv, ball)

    memn = _rmsnorm_fwd(mem, sw["norm_mem_g"].reshape(1, D), "norm_mem_fwd")
    kv = _mm(memn, lw["w_mem_kv"], name="mm_memkv")
    g_mk = _tile4(sw["mem_k_norm_g"])
    mk, mv = _memkv_prep(kv, g_mk)
    mo = _mem_fwd(mq, mk, mv)

    br = (ret, pool, na, mo)
    merged = _merge_fwd(br, lw["w_branch"], gp)
    x1 = _mm(merged, lw["w_out"], add=x, name="mm_out")
    if more_weights is not None:
        lw.update(more_weights(x1))
    h2 = _rmsnorm_fwd(x1, sw["norm_ffn_g"].reshape(1, D), "norm_ffn_fwd")
    ag = _mm(h2, lw["w_ffn_in"], tb=True, out_dtype=BF16, name="mm_ffn_in")
    yff = _swiglu_fwd(ag)
    x2 = _mm(yff, lw["w_ffn_out"], add=x1, name="mm_ffn_out")
    saved = dict(x=x, h=h, proj=proj, gp=gp, rq=rq, rk=rk, rv=rv, nq=nq, nk=nk, nv=nv, mq=mq, o_ret=o_ret, ball=ball, memn=memn,
                 kv=kv, mk=mk, mv=mv, br=br, merged=merged, x1=x1, h2=h2, ag=ag, yff=yff, lgf=lgf, lgb=lgb, wbd=wbd)
    return x2, saved


def _layer_bwd(dx2, dx2b, mem, sw, lw, sv, consts, dep=None):
    cos2, sin2, onehot, maskadd = consts
    gb, gs = {}, {}
    dy = _mm(dx2b, lw["w_ffn_out"], tb=True, out_dtype=BF16, dep=dep, name="mm_ffn_out_dx")
    gb["w_ffn_out"] = _mm(sv["yff"], dx2b, ta=True, out_dtype=BF16, name="mm_ffn_out_dw")
    dag = _swiglu_bwd(sv["ag"], dy)
    dh2 = _mm(dag, lw["w_ffn_in"], name="mm_ffn_in_dx")
    gb["w_ffn_in"] = _mm(dag, sv["h2"], ta=True, out_dtype=BF16, name="mm_ffn_in_dw")
    dx1, dx1b, dg = _rmsnorm_bwd(dh2, sv["x1"], sw["norm_ffn_g"].reshape(1, D), dx2, "norm_ffn_bwd")
    gs["norm_ffn_g"] = dg.reshape(D)

    dmerged = _mm(dx1b, lw["w_out"], tb=True, name="mm_out_dx")
    gb["w_out"] = _mm(sv["merged"], dx1b, ta=True, out_dtype=BF16, name="mm_out_dw")
    dgp, dup = _merge_bwd(dmerged, sv["br"], lw["w_branch"], sv["gp"])
    dbr = _dbranch(dup, lw["w_branch"])
    gb["w_branch"] = _dwbranch(sv["br"], dup)

    g_ret = sw["ret_norm_g"].reshape(1, BW)
    do_ret, d_rg, dg_ret = _ret_post_bwd(dbr, sv["o_ret"], sv["proj"], g_ret)
    d_rq, d_rk, d_rv, dlg = _ret_bwd(do_ret, sv["rq"], sv["rk"], sv["rv"], sv["lgf"], sv["lgb"])
    gs["ret_norm_g"] = dg_ret.reshape(BW)
    _, vjp_f = jax.vjp(jax.nn.log_sigmoid, sw["ret_decay_fwd"])
    _, vjp_b = jax.vjp(jax.nn.log_sigmoid, sw["ret_decay_bwd"])
    gs["ret_decay_fwd"] = vjp_f(dlg[0:NH, 0])[0]
    gs["ret_decay_bwd"] = vjp_b(dlg[NH:2 * NH, 0])[0]

    p_scale = sw["pool_scale"].reshape(1, BW)
    d_pv, dwbd, dscale = _pool_bwd(dbr, sv["proj"], sv["wbd"], p_scale)
    gs["pool_w"] = jnp.stack([dwbd[g * HD:(g + 1) * HD, g * HD:(g + 1) * HD] for g in range(NH)])
    gs["pool_scale"] = dscale.reshape(BW)

    d_nq, d_nk, d_nv, dball = _na_bwd(dbr, sv["nq"], sv["nk"], sv["nv"], sv["ball"])
    _, vjp_tab = jax.vjp(lambda tab: _na_bias_table(tab, maskadd), jnp.zeros((64, GRID_W * GRID_W), F32))
    drpb = _rpb_reduce(vjp_tab(dball)[0], onehot)
    gs["na_rpb"] = drpb[:NH * 15, :31].reshape(NH, 15, 31)

    d_mq, d_mk, d_mv = _mem_bwd(dbr, sv["mq"], sv["mk"], sv["mv"])
    g_mk = _tile4(sw["mem_k_norm_g"])
    dkv, dg_mk = _memkv_bwd(sv["kv"], d_mk, d_mv, g_mk)
    gs["mem_k_norm_g"] = dg_mk.reshape(NH, HD).sum(0)
    gb["w_mem_kv"] = _mm(sv["memn"], dkv, ta=True, out_dtype=BF16, name="mm_memkv_dw")
    dmemn = _mm(dkv, lw["w_mem_kv"], tb=True, name="mm_memkv_dx")
    _, _, dg_mem = _rmsnorm_bwd(dmemn, mem, sw["norm_mem_g"].reshape(1, D), jnp.zeros_like(mem), "norm_mem_bwd")
    gs["norm_mem_g"] = dg_mem.reshape(D)

    g_naq, g_nak, g_mq = _tile4(sw["na_q_norm_g"]), _tile4(sw["na_k_norm_g"]), _tile4(sw["mem_q_norm_g"])
    dproj, dg_naq, dg_nak, dg_mq = _prep_bwd(sv["proj"], cos2, sin2, g_naq, g_nak, g_mq, d_rq, d_rk, d_rv, d_rg, d_pv, d_nq, d_nk,
                                             d_nv, d_mq)
    gs["na_q_norm_g"] = dg_naq.reshape(NH, HD).sum(0)
    gs["na_k_norm_g"] = dg_nak.reshape(NH, HD).sum(0)
    gs["mem_q_norm_g"] = dg_mq.reshape(NH, HD).sum(0)

    dh = _mm(dproj, lw["w_in"], name="mm_in_dx")
    dh = _mm(dgp, lw["w_gate"], add=dh, name="mm_gate_dx")
    gb["w_in"] = _mm(dproj, sv["h"], ta=True, out_dtype=BF16, name="mm_in_dw")
    gb["w_gate"] = _mm(dgp, sv["h"], ta=True, out_dtype=BF16, name="mm_gate_dw")
    dx, dxb, dg = _rmsnorm_bwd(dh, sv["x"], sw["norm_mix_g"].reshape(1, D), dx1, "norm_mix_bwd")
    gs["norm_mix_g"] = dg.reshape(D)
    return dx, dxb, gb, gs


def _local_step(x, mem, target, small, get_layer, on_grads):
    t = x.shape[0]
    cos2, sin2 = _rotary_tables(t)
    onehot, maskadd = _na_constants()
    consts = (cos2, sin2, jnp.asarray(onehot), jnp.asarray(maskadd))
    saved, weights, cur = [], [], x
    for l in range(DEPTH):
        sw = {n: small[n][l] for n in SMALL}
        lw, more = get_layer(l, cur)
        weights.append(lw)
        cur, sv = _layer_fwd(cur, mem, sw, lw, consts, more)
        saved.append(sv)
    dy, dyb, loss_tile = _loss_head(cur, target)
    small_g = {n: [None] * DEPTH for n in SMALL}
    dep = None
    for l in reversed(range(DEPTH)):
        sw = {n: small[n][l] for n in SMALL}
        dy, dyb, gb, gs = _layer_bwd(dy, dyb, mem, sw, weights[l], saved[l], consts, dep)
        dep = on_grads(l, gb, dy)
        for n in SMALL:
            small_g[n][l] = gs[n]
    return loss_tile[0, 0], dy, {n: jnp.stack(v) for n, v in small_g.items()}


def _flat2d(a):
    return a.reshape(-1, a.shape[-1])


def kernel(x, mem, norm_mix_g, norm_mem_g, w_in, w_gate, ret_decay_fwd, ret_decay_bwd, ret_norm_g, pool_w, pool_scale, na_q_norm_g, na_k_norm_g, na_rpb, mem_q_norm_g, mem_k_norm_g, w_mem_kv, w_branch, w_out, norm_ffn_g, w_ffn_in, w_ffn_out, loss_target, m_norm_mix_g, m_norm_mem_g, m_w_in, m_w_gate, m_ret_decay_fwd, m_ret_decay_bwd, m_ret_norm_g, m_pool_w, m_pool_scale, m_na_q_norm_g, m_na_k_norm_g, m_na_rpb, m_mem_q_norm_g, m_mem_k_norm_g, m_w_mem_kv, m_w_branch, m_w_out, m_norm_ffn_g, m_w_ffn_in, m_w_ffn_out, v_norm_mix_g, v_norm_mem_g, v_w_in, v_w_gate, v_ret_decay_fwd, v_ret_decay_bwd, v_ret_norm_g, v_pool_w, v_pool_scale, v_na_q_norm_g, v_na_k_norm_g, v_na_rpb, v_mem_q_norm_g, v_mem_k_norm_g, v_w_mem_kv, v_w_branch, v_w_out, v_norm_ffn_g, v_w_ffn_in, v_w_ffn_out):
    w = dict(norm_mix_g=norm_mix_g, norm_mem_g=norm_mem_g, w_in=w_in, w_gate=w_gate, ret_decay_fwd=ret_decay_fwd,
             ret_decay_bwd=ret_decay_bwd, ret_norm_g=ret_norm_g, pool_w=pool_w, pool_scale=pool_scale, na_q_norm_g=na_q_norm_g,
             na_k_norm_g=na_k_norm_g, na_rpb=na_rpb, mem_q_norm_g=mem_q_norm_g, mem_k_norm_g=mem_k_norm_g, w_mem_kv=w_mem_kv,
             w_branch=w_branch, w_out=w_out, norm_ffn_g=norm_ffn_g, w_ffn_in=w_ffn_in, w_ffn_out=w_ffn_out)
    m = dict(norm_mix_g=m_norm_mix_g, norm_mem_g=m_norm_mem_g, w_in=m_w_in, w_gate=m_w_gate, ret_decay_fwd=m_ret_decay_fwd,
             ret_decay_bwd=m_ret_decay_bwd, ret_norm_g=m_ret_norm_g, pool_w=m_pool_w, pool_scale=m_pool_scale, na_q_norm_g=m_na_q_norm_g,
             na_k_norm_g=m_na_k_norm_g, na_rpb=m_na_rpb, mem_q_norm_g=m_mem_q_norm_g, mem_k_norm_g=m_mem_k_norm_g, w_mem_kv=m_w_mem_kv,
             w_branch=m_w_branch, w_out=m_w_out, norm_ffn_g=m_norm_ffn_g, w_ffn_in=m_w_ffn_in, w_ffn_out=m_w_ffn_out)
    v = dict(norm_mix_g=v_norm_mix_g, norm_mem_g=v_norm_mem_g, w_in=v_w_in, w_gate=v_w_gate, ret_decay_fwd=v_ret_decay_fwd,
             ret_decay_bwd=v_ret_decay_bwd, ret_norm_g=v_ret_norm_g, pool_w=v_pool_w, pool_scale=v_pool_scale, na_q_norm_g=v_na_q_norm_g,
             na_k_norm_g=v_na_k_norm_g, na_rpb=v_na_rpb, mem_q_norm_g=v_mem_q_norm_g, mem_k_norm_g=v_mem_k_norm_g, w_mem_kv=v_w_mem_kv,
             w_branch=v_w_branch, w_out=v_w_out, norm_ffn_g=v_norm_ffn_g, w_ffn_in=v_w_ffn_in, w_ffn_out=v_w_ffn_out)
    assert x.shape == (1, 2048, D) and mem.shape == (1, N_MEM, D) and w_in.shape == (DEPTH, D, 9 * BW // N_DEV)

    started = []
    for l in range(DEPTH):
        blocks = [_to_exchange(name, tr, w[name][l]).astype(BF16) for name, tr in BIG]
        lands = [lax.empty((N_DEV,) + b.shape, BF16) for b in blocks]
        started.append(_ici_start(blocks, lands, "gather", "gather_ici_start_%d" % l))
    all_started = started[0][4] + started[1][4] + started[2][4] + started[3][4]

    def get_group(l, only, after, tag):
        lands = _ici_wait(started[l], after, "gather_ici_wait_%d%s" % (l, tag), only)
        whole = _gather_d2d([started[l][2][i] for i in only], lands, "gather_d2d")
        return {BIG[i][0]: _whole_from_gathered(BIG[i][0], g) for i, g in zip(only, whole)}

    def get_layer(l, after):
        if l > 0:
            return get_group(l, list(range(len(BIG))), after, ""), None
        mixer = [i for i, (name, _) in enumerate(BIG) if not name.startswith("w_ffn")]
        ffn = [i for i, (name, _) in enumerate(BIG) if name.startswith("w_ffn")]
        return get_group(l, mixer, all_started, "a"), lambda after2: get_group(l, ffn, after2, "b")

    cidx = lax.axis_index("c").astype(jnp.int32).reshape(1)
    chip = (2 * lax.axis_index("x") + lax.axis_index("y")).astype(jnp.int32).reshape(1)
    in_flight = []

    def flip_of(name, tr):
        return (lambda a: jnp.swapaxes(a, 1, 2)) if (tr and name != "w_branch") else (lambda a: a)

    def rows3(a):
        return a.reshape(DEPTH, -1, a.shape[-1])

    opt_in = {name: tuple(rows3(flip_of(name, tr)(t[name])) for t in (w, m, v)) for name, tr in BIG}
    opt_out = {name: tuple(lax.empty(opt_in[name][0].shape, F32) for _ in range(4)) for name, _ in BIG}

    device = (2 * chip + cidx).astype(jnp.int32)

    def finish(l, st, after):
        recv = _ici_wait(st, after, "rs_ici_wait_%d" % l)
        sums = _sum_own(st[2], recv, chip if st[5] == 3 else device, "rs_sum")
        for (name, tr), s in zip(BIG, sums):
            g = _from_exchange(name, tr, s) if name == "w_branch" else s
            wx, mx, vx = opt_in[name]
            opt_out[name] = _adamw_layer(l, wx, g.reshape(-1, g.shape[-1]), mx, vx, opt_out[name], "adamw_" + name)

    def on_grads(l, gb, after):
        send = [_by_destination(name, gb[name]) for name, _ in BIG]
        if l > 0:
            send = [s.reshape((N_DEV,) + s.shape[2:]) for s in send]
            st = _ici_start(send, [lax.empty(s.shape, BF16) for s in send], "by_device", "rs_ici_start_%d" % l)
        else:
            from_core = _rs_core_swap(send, "rs_core_swap")
            chip_part = _pair_sum(send, from_core, cidx)
            st = _ici_start(chip_part, [lax.empty(p.shape, BF16) for p in chip_part], "by_chip", "rs_ici_start_%d" % l)
        in_flight.append((l, st))
        return st[4]

    loss_local, dx, small_g = _local_step(x[0], mem[0], loss_target[0], {n: w[n] for n in SMALL}, get_layer, on_grads)

    last_started = in_flight[-1][1][4]
    for l, st in in_flight[:-1]:
        finish(l, st, last_started)

    small_all, = _all_gather([_pack_small(small_g, loss_local) + last_started[0:1]], "gather_small")
    packed_g = _sum_slots(small_all, "small_sum")
    small_sum, loss = _unpack_small(packed_g, {n: w[n] for n in SMALL})
    d_, m_, v_ = _adamw(_pack_small({n: w[n] for n in SMALL}), packed_g, _pack_small({n: m[n] for n in SMALL}),
                        _pack_small({n: v[n] for n in SMALL}), "adamw_small")
    updated = d_[0:8]
    for name, _ in BIG:
        updated = updated + opt_out[name][0][1, 0:8, 0:128]
    finish(*in_flight[-1], updated)

    grads, delta, new_m, new_v = {}, {}, {}, {}
    for name, tr in BIG:
        shape = flip_of(name, tr)(w[name]).shape
        delta[name], new_m[name], new_v[name], grads[name] = (flip_of(name, tr)(a.reshape(shape)) for a in opt_out[name])
    like = {n: w[n] for n in SMALL}
    ds, _ = _unpack_small(d_, like)
    ms, _ = _unpack_small(m_, like)
    vs, _ = _unpack_small(v_, like)
    for n in SMALL:
        grads[n], delta[n], new_m[n], new_v[n] = small_sum[n], ds[n], ms[n], vs[n]

    return (loss, dx[None], *[grads[n] for n in WEIGHTS], *[delta[n] for n in WEIGHTS], *[new_m[n] for n in WEIGHTS],
            *[new_v[n] for n in WEIGHTS])
```

```python
import functools

import numpy as np
import jax
import jax.numpy as jnp
from jax import lax
from jax.experimental import pallas as pl
from jax.experimental.pallas import tpu as pltpu

F32 = jnp.float32
BF16 = jnp.bfloat16
MXU = jnp.bfloat16
HI = lax.Precision.HIGHEST

DEPTH = 4
D = 1024
BW = 256
HD = 64
NH = 4
GRID_W = 64
NA_ROWS_WIN = 8
NA_COLS_WIN = 16
N_MEM = 256
FF = 2816
EPS = 1e-6
NEG = -1e30
ROPE_THETA = 10000.0
POOL_HALF_MAX = 8

ADAM_LR, ADAM_B1, ADAM_B2, ADAM_EPS, ADAM_WD, ADAM_STEP = 0.001, 0.9, 0.999, 1e-08, 0.01, 10

N_DEV = 8
VMEM_LIMIT = 56 * 1024 * 1024

RQ, RK, RV, RG, PV, NQ, NK, NV, MQ = range(9)

MESH = pl.DeviceIdType.MESH
ANY = pl.BlockSpec(memory_space=pl.ANY)
SMEM = pl.BlockSpec(memory_space=pltpu.SMEM)


def _cp(**kw):
    return pltpu.CompilerParams(vmem_limit_bytes=VMEM_LIMIT, **kw)


def _tile(n, cap):
    if n <= cap:
        return n
    best = None
    for t in range(128, cap + 1, 128):
        if n % t == 0:
            best = t
    assert best is not None, (n, cap)
    return best


def _sds(shape, dtype):
    return jax.ShapeDtypeStruct(shape, dtype)


def _lane_head(shape):
    return lax.shift_right_logical(lax.broadcasted_iota(jnp.int32, shape, len(shape) - 1), 6)


def _group_mean(z):
    i = lax.shift_right_logical(lax.broadcasted_iota(jnp.int32, (BW, BW), 0), 6)
    j = lax.shift_right_logical(lax.broadcasted_iota(jnp.int32, (BW, BW), 1), 6)
    g = jnp.where(i == j, 1.0 / HD, 0.0).astype(BF16)
    z_hi = z.astype(BF16)
    z_lo = (z - z_hi.astype(F32)).astype(BF16)
    return jnp.dot(z_hi, g, preferred_element_type=F32) + jnp.dot(z_lo, g, preferred_element_type=F32)


def _gnorm(t, g):
    r = lax.rsqrt(_group_mean(t * t) + EPS)
    return t * r * g


def _gnorm_bwd(dy, t, g):
    r = lax.rsqrt(_group_mean(t * t) + EPS)
    th = t * r
    dth = dy * g
    dt = r * (dth - th * _group_mean(dth * th))
    return dt, dy * th


def _swap_halves(t):
    lane = lax.broadcasted_iota(jnp.int32, t.shape, 1)
    return jnp.where((lane & 63) < 32, pltpu.roll(t, BW - 32, 1), pltpu.roll(t, 32, 1))


def _sigmoid(x):
    return 1.0 / (1.0 + jnp.exp(-x))


def _dot(a, b, ta=False, tb=False):
    return lax.dot_general(a.astype(MXU), b.astype(MXU), (((0 if ta else 1,), (1 if tb else 0,)), ((), ())),
                           preferred_element_type=F32)


def _stack_heads(t):
    head = _lane_head(t.shape)
    return jnp.concatenate([jnp.where(head == h, t, jnp.zeros_like(t)) for h in range(NH)], axis=0)


def _unstack_heads(t, rows):
    head = _lane_head((rows, BW))
    out = jnp.zeros((rows, BW), F32)
    for h in range(NH):
        out = out + jnp.where(head == h, t[h * rows:(h + 1) * rows], 0.0)
    return out


def _softmax_rows(s):
    m = jnp.max(s, axis=-1, keepdims=True)
    e = jnp.exp(s - m)
    return e / jnp.sum(e, axis=-1, keepdims=True)


def _acc(ref, val, first):
    @pl.when(first)
    def _():
        ref[...] = val

    @pl.when(jnp.logical_not(first))
    def _():
        ref[...] += val


def _mm(a, b, *, ta=False, tb=False, out_dtype=F32, add=None, dep=None, name):
    m, k = (a.shape[1], a.shape[0]) if ta else a.shape
    n = b.shape[0] if tb else b.shape[1]
    tm, tn = _tile(m, 1408), _tile(n, 768)

    def body(*refs):
        if add is None:
            a_ref, b_ref, o_ref = refs[:2] + refs[-1:]
            r = _dot(a_ref[...], b_ref[...], ta, tb)
        else:
            a_ref, b_ref, c_ref, o_ref = refs[:3] + refs[-1:]
            r = _dot(a_ref[...], b_ref[...], ta, tb) + c_ref[...]
        o_ref[...] = r.astype(out_dtype)

    a_spec = pl.BlockSpec((k, tm), lambda i, j: (0, i)) if ta else pl.BlockSpec((tm, k), lambda i, j: (i, 0))
    b_spec = pl.BlockSpec((tn, k), lambda i, j: (j, 0)) if tb else pl.BlockSpec((k, tn), lambda i, j: (0, j))
    o_spec = pl.BlockSpec((tm, tn), lambda i, j: (i, j))
    ins, args = [a_spec, b_spec], [a, b]
    if add is not None:
        ins.append(o_spec)
        args.append(add)
    if dep is not None:
        ins.append(pl.BlockSpec((8, 128), lambda i, j: (0, 0)))
        args.append(dep)
    return pl.pallas_call(
        body, grid=(m // tm, n // tn), in_specs=ins, out_specs=o_spec, out_shape=_sds((m, n), out_dtype), name=name,
        compiler_params=_cp(dimension_semantics=("parallel", "parallel")))(*args)


def _rmsnorm_fwd(x, g, name):
    t, d = x.shape
    tm = _tile(t, 256)

    def body(x_ref, g_ref, o_ref):
        xv = x_ref[...]
        r = lax.rsqrt(jnp.mean(xv * xv, axis=-1, keepdims=True) + EPS)
        o_ref[...] = (xv * r * g_ref[...]).astype(o_ref.dtype)

    return pl.pallas_call(
        body, grid=(t // tm,), in_specs=[pl.BlockSpec((tm, d), lambda i: (i, 0)), pl.BlockSpec((1, d), lambda i: (0, 0))],
        out_specs=pl.BlockSpec((tm, d), lambda i: (i, 0)), out_shape=_sds((t, d), BF16), name=name, compiler_params=_cp())(x, g)


def _rmsnorm_bwd(dh, x, g, res, name):
    t, d = x.shape
    tm = _tile(t, 256)

    def body(dh_ref, x_ref, g_ref, res_ref, dx_ref, dxb_ref, dg_ref):
        xv = x_ref[...]
        dhv = dh_ref[...]
        r = lax.rsqrt(jnp.mean(xv * xv, axis=-1, keepdims=True) + EPS)
        xh = xv * r
        dxh = dhv * g_ref[...]
        dx = res_ref[...] + r * (dxh - xh * jnp.mean(dxh * xh, axis=-1, keepdims=True))
        dx_ref[...] = dx
        dxb_ref[...] = dx.astype(BF16)
        _acc(dg_ref, jnp.sum(dhv * xh, axis=0, keepdims=True), pl.program_id(0) == 0)

    row = pl.BlockSpec((tm, d), lambda i: (i, 0))
    vec = pl.BlockSpec((1, d), lambda i: (0, 0))
    return pl.pallas_call(
        body, grid=(t // tm,), in_specs=[row, row, vec, row], out_specs=(row, row, vec),
        out_shape=(_sds((t, d), F32), _sds((t, d), BF16), _sds((1, d), F32)), name=name, compiler_params=_cp())(dh, x, g, res)


def _prep_fwd(proj, cos2, sin2, g_naq, g_nak, g_mq):
    t = proj.shape[0]
    tm = 256

    def body(p_ref, cos_ref, sin_ref, gq_ref, gk_ref, gm_ref, rq_ref, rk_ref, rv_ref, nq_ref, nk_ref, nv_ref, mq_ref):
        def col(c):
            return p_ref[:, c * BW:(c + 1) * BW]

        cosv, sinv = cos_ref[...], sin_ref[...]

        def rot(tv):
            return tv * cosv + _swap_halves(tv) * sinv

        rq_ref[...] = (rot(col(RQ)) * (HD ** -0.5)).astype(BF16)
        rk_ref[...] = rot(col(RK)).astype(BF16)
        rv_ref[...] = col(RV).astype(BF16)
        nq_ref[...] = _gnorm(col(NQ), gq_ref[...]).astype(BF16)
        nk_ref[...] = _gnorm(col(NK), gk_ref[...]).astype(BF16)
        nv_ref[...] = col(NV).astype(BF16)
        mq_ref[...] = _gnorm(col(MQ), gm_ref[...]).astype(BF16)

    blk = pl.BlockSpec((tm, BW), lambda i: (i, 0))
    vec = pl.BlockSpec((1, BW), lambda i: (0, 0))
    return pl.pallas_call(
        body, grid=(t // tm,), in_specs=[pl.BlockSpec((tm, 9 * BW), lambda i: (i, 0)), blk, blk, vec, vec, vec],
        out_specs=tuple(blk for _ in range(7)), out_shape=tuple(_sds((t, BW), BF16) for _ in range(7)),
        name="prep_fwd", compiler_params=_cp())(proj, cos2, sin2, g_naq, g_nak, g_mq)


def _prep_bwd(proj, cos2, sin2, g_naq, g_nak, g_mq, d_rq, d_rk, d_rv, d_rg, d_pv, d_nq, d_nk, d_nv, d_mq):
    t = proj.shape[0]
    tm = 256

    def body(p_ref, cos_ref, sin_ref, gq_ref, gk_ref, gm_ref, drq_ref, drk_ref, drv_ref, drg_ref, dpv_ref, dnq_ref, dnk_ref,
             dnv_ref, dmq_ref, o_ref, dgq_ref, dgk_ref, dgm_ref):
        first = pl.program_id(0) == 0

        def col(c):
            return p_ref[:, c * BW:(c + 1) * BW]

        def put(c, v):
            o_ref[:, c * BW:(c + 1) * BW] = v.astype(BF16)

        cosv, sinv = cos_ref[...], sin_ref[...]

        def rot_t(dv):
            return dv * cosv + _swap_halves(dv * sinv)

        put(RQ, rot_t(drq_ref[...] * (HD ** -0.5)))
        put(RK, rot_t(drk_ref[...]))
        put(RV, drv_ref[...])
        put(RG, drg_ref[...])
        put(PV, dpv_ref[...])
        dq, gq = _gnorm_bwd(dnq_ref[...], col(NQ), gq_ref[...])
        put(NQ, dq)
        _acc(dgq_ref, jnp.sum(gq, axis=0, keepdims=True), first)
        dk, gk = _gnorm_bwd(dnk_ref[...], col(NK), gk_ref[...])
        put(NK, dk)
        _acc(dgk_ref, jnp.sum(gk, axis=0, keepdims=True), first)
        put(NV, dnv_ref[...])
        dm, gm = _gnorm_bwd(dmq_ref[...], col(MQ), gm_ref[...])
        put(MQ, dm)
        _acc(dgm_ref, jnp.sum(gm, axis=0, keepdims=True), first)

    blk = pl.BlockSpec((tm, BW), lambda i: (i, 0))
    vec = pl.BlockSpec((1, BW), lambda i: (0, 0))
    wide = pl.BlockSpec((tm, 9 * BW), lambda i: (i, 0))
    return pl.pallas_call(
        body, grid=(t // tm,), in_specs=[wide, blk, blk, vec, vec, vec] + [blk] * 9, out_specs=(wide, vec, vec, vec),
        out_shape=(_sds((t, 9 * BW), BF16), _sds((1, BW), F32), _sds((1, BW), F32), _sds((1, BW), F32)),
        name="prep_bwd", compiler_params=_cp())(proj, cos2, sin2, g_naq, g_nak, g_mq, d_rq, d_rk, d_rv, d_rg, d_pv, d_nq, d_nk,
                                                d_nv, d_mq)


RET_B = 256


def _ret_consts(lgf_ref, lgb_ref):
    bsz = RET_B
    head = _lane_head((1, BW))
    lf, lb = jnp.zeros((1, BW), F32), jnp.zeros((1, BW), F32)
    for h in range(NH):
        lf = lf + jnp.where(head == h, lgf_ref[h], 0.0)
        lb = lb + jnp.where(head == h, lgb_ref[h], 0.0)
    pos = lax.broadcasted_iota(jnp.int32, (bsz, BW), 0).astype(F32)
    up, down = pos + 1.0, (bsz - 1.0) - pos
    c = dict(up=up, down=down, kf=jnp.exp(down * lf), kb=jnp.exp(up * lb), qf=jnp.exp(up * lf), qb=jnp.exp(down * lb),
             cf=jnp.exp(bsz * lf), cb=jnp.exp(bsz * lb))
    diff = (lax.broadcasted_iota(jnp.int32, (NH * bsz, 1), 0) & (bsz - 1)) - lax.broadcasted_iota(jnp.int32, (1, bsz), 1)
    c["causal"] = diff >= 0
    c["dist"] = jnp.abs(diff).astype(F32)
    lgf = jnp.concatenate([jnp.full((bsz, 1), lgf_ref[h], F32) for h in range(NH)], axis=0)
    lgb = jnp.concatenate([jnp.full((bsz, 1), lgb_ref[h], F32) for h in range(NH)], axis=0)
    c["dm"] = jnp.exp(c["dist"] * jnp.where(c["causal"], lgf, lgb))
    c["bd"] = _lane_head((BW, BW)) == lax.shift_right_logical(lax.broadcasted_iota(jnp.int32, (BW, BW), 0), 6)
    return c


def _ret_states(k_ref, v_ref, st_ref, c, nb):
    bsz = RET_B

    def summary(b, decay):
        kb = k_ref[b * bsz:(b + 1) * bsz, :].astype(F32)
        return jnp.where(c["bd"], _dot(kb * decay, v_ref[b * bsz:(b + 1) * bsz, :], ta=True), 0.0)

    f = jnp.zeros((BW, BW), F32)
    for b in range(nb):
        st_ref[b] = f
        if b < nb - 1:
            f = c["cf"] * f + summary(b, c["kf"])
    g = jnp.zeros((BW, BW), F32)
    for b in reversed(range(nb)):
        st_ref[nb + b] = g
        if b > 0:
            g = c["cb"] * g + summary(b, c["kb"])


def _ret_fwd(q, k, v, proj, lgf, lgb, g_ret):
    t = q.shape[0]
    bsz, nb = RET_B, t // RET_B

    def body(lgf_ref, lgb_ref, q_ref, k_ref, v_ref, rg_ref, g_ref, o_ref, ret_ref, st_ref):
        c = _ret_consts(lgf_ref, lgb_ref)
        _ret_states(k_ref, v_ref, st_ref, c, nb)
        for b in range(nb):
            blk = slice(b * bsz, (b + 1) * bsz)
            qb, kb, vb = q_ref[blk, :], k_ref[blk, :], v_ref[blk, :]
            s = _dot(_stack_heads(qb), kb, tb=True)
            o = _unstack_heads(_dot(s * c["dm"], vb), bsz)
            q32 = qb.astype(F32)
            o = o + _dot(q32 * c["qf"], st_ref[b]) + _dot(q32 * c["qb"], st_ref[nb + b])
            o_ref[blk, :] = o
            rg = rg_ref[blk, :]
            ret_ref[blk, :] = (_gnorm(o, g_ref[...]) * (rg * _sigmoid(rg))).astype(BF16)

    whole = pl.BlockSpec((t, BW), lambda i: (0, 0))
    return pl.pallas_call(
        body, grid=(1,),
        in_specs=[SMEM, SMEM, whole, whole, whole, pl.BlockSpec((t, BW), lambda i: (0, RG)), pl.BlockSpec((1, BW), lambda i: (0, 0))],
        out_specs=(whole, whole), out_shape=(_sds((t, BW), F32), _sds((t, BW), BF16)),
        scratch_shapes=[pltpu.VMEM((2 * nb, BW, BW), F32)], name="ret_fwd", compiler_params=_cp())(lgf, lgb, q, k, v, proj, g_ret)


def _ret_post_bwd(dbr, o_ret, proj, g_ret):
    t = o_ret.shape[0]
    tm = 256

    def body(d_ref, o_ref, rg_ref, g_ref, do_ref, drg_ref, dg_ref):
        dret, o, rg, g = d_ref[...], o_ref[...], rg_ref[...], g_ref[...]
        sg = _sigmoid(rg)
        do, dgain = _gnorm_bwd(dret * (rg * sg), o, g)
        do_ref[...] = do.astype(BF16)
        drg_ref[...] = dret * _gnorm(o, g) * (sg * (1.0 + rg * (1.0 - sg)))
        _acc(dg_ref, jnp.sum(dgain, axis=0, keepdims=True), pl.program_id(0) == 0)

    blk = pl.BlockSpec((tm, BW), lambda i: (i, 0))
    vec = pl.BlockSpec((1, BW), lambda i: (0, 0))
    return pl.pallas_call(
        body, grid=(t // tm,), in_specs=[blk, blk, pl.BlockSpec((tm, BW), lambda i: (i, RG)), vec], out_specs=(blk, blk, vec),
        out_shape=(_sds((t, BW), BF16), _sds((t, BW), F32), _sds((1, BW), F32)), name="ret_post_bwd",
        compiler_params=_cp())(dbr, o_ret, proj, g_ret)


def _ret_bwd(do, q, k, v, lgf, lgb):
    t = q.shape[0]
    bsz, nb = RET_B, t // RET_B

    def body(lgf_ref, lgb_ref, d_ref, q_ref, k_ref, v_ref, dq_ref, dk_ref, dv_ref, dlg_ref, st_ref, sd_ref):
        c = _ret_consts(lgf_ref, lgb_ref)
        _ret_states(k_ref, v_ref, st_ref, c, nb)
        lane_f, lane_b = jnp.zeros((1, BW), F32), jnp.zeros((1, BW), F32)
        row_f, row_b = jnp.zeros((NH * bsz, 1), F32), jnp.zeros((NH * bsz, 1), F32)

        def rows(x):
            return jnp.sum(x, axis=0, keepdims=True)

        for b in range(nb):
            blk = slice(b * bsz, (b + 1) * bsz)
            qb, kb, vb, dob = q_ref[blk, :], k_ref[blk, :], v_ref[blk, :], d_ref[blk, :]
            q32 = qb.astype(F32)
            qs, dos = _stack_heads(qb), _stack_heads(dob)
            s = _dot(qs, kb, tb=True)
            da = _dot(dos, vb, tb=True)
            dv_ref[blk, :] = _dot(s * c["dm"], dos, ta=True)
            ds = da * c["dm"]
            w = ds * s * c["dist"]
            row_f = row_f + jnp.sum(jnp.where(c["causal"], w, 0.0), axis=1, keepdims=True)
            row_b = row_b + jnp.sum(jnp.where(c["causal"], 0.0, w), axis=1, keepdims=True)
            dsb = ds.astype(MXU)
            dk_ref[blk, :] = _dot(dsb, qs, ta=True)
            dq_f = _dot(dob, st_ref[b], tb=True) * c["qf"]
            dq_b = _dot(dob, st_ref[nb + b], tb=True) * c["qb"]
            lane_f = lane_f + rows(c["up"] * dq_f * q32)
            lane_b = lane_b + rows(c["down"] * dq_b * q32)
            dq_ref[blk, :] = _unstack_heads(_dot(dsb, kb), bsz) + dq_f + dq_b
            sd_ref[b] = jnp.where(c["bd"], _dot(q32 * c["qf"], dob, ta=True), 0.0)
            sd_ref[nb + b] = jnp.where(c["bd"], _dot(q32 * c["qb"], dob, ta=True), 0.0)

        def through_state(b, grad, decay, weight, lane):
            blk = slice(b * bsz, (b + 1) * bsz)
            k32 = k_ref[blk, :].astype(F32)
            dk = _dot(v_ref[blk, :], grad, tb=True) * decay
            dk_ref[blk, :] += dk
            dv_ref[blk, :] += _dot(k32 * decay, grad)
            return lane + rows(weight * dk * k32)

        phi = jnp.zeros((BW, BW), F32)
        for b in reversed(range(nb)):
            if b < nb - 1:
                lane_f = through_state(b, phi, c["kf"], c["down"], lane_f)
                lane_f = lane_f + bsz * rows(c["cf"] * st_ref[b] * phi)
            phi = sd_ref[b] + c["cf"] * phi
        gam = jnp.zeros((BW, BW), F32)
        for b in range(nb):
            if b > 0:
                lane_b = through_state(b, gam, c["kb"], c["up"], lane_b)
                lane_b = lane_b + bsz * rows(c["cb"] * st_ref[nb + b] * gam)
            gam = sd_ref[nb + b] + c["cb"] * gam

        head = _lane_head((1, BW))
        for h in range(NH):
            tot_f = jnp.sum(row_f[h * bsz:(h + 1) * bsz, :]) + jnp.sum(jnp.where(head == h, lane_f, 0.0))
            tot_b = jnp.sum(row_b[h * bsz:(h + 1) * bsz, :]) + jnp.sum(jnp.where(head == h, lane_b, 0.0))
            dlg_ref[h:h + 1, :] = jnp.full((1, 128), tot_f, F32)
            dlg_ref[NH + h:NH + h + 1, :] = jnp.full((1, 128), tot_b, F32)

    whole = pl.BlockSpec((t, BW), lambda i: (0, 0))
    return pl.pallas_call(
        body, grid=(1,), in_specs=[SMEM, SMEM, whole, whole, whole, whole],
        out_specs=(whole, whole, whole, pl.BlockSpec((2 * NH, 128), lambda i: (0, 0))),
        out_shape=(_sds((t, BW), F32), _sds((t, BW), F32), _sds((t, BW), F32), _sds((2 * NH, 128), F32)),
        scratch_shapes=[pltpu.VMEM((2 * nb, BW, BW), F32), pltpu.VMEM((2 * nb, BW, BW), F32)], name="ret_bwd",
        compiler_params=_cp())(lgf, lgb, do, q, k, v)


def _pool_windows(t):
    row = lax.broadcasted_iota(jnp.int32, (t, BW), 0)
    half = lax.shift_left(jnp.ones((t, BW), jnp.int32), _lane_head((t, BW)))
    cnt = (jnp.minimum(row + half, t) - jnp.maximum(row - half, 0)).astype(F32)
    return row, half, cnt


def _pool_window_sum(v, row, half, t, transpose):
    out = jnp.zeros_like(v)
    for j in range(-POOL_HALF_MAX, POOL_HALF_MAX):
        src = row - j if transpose else row + j
        ok = (src >= 0) & (src < t) & (j >= -half) & (j < half)
        out = out + jnp.where(ok, pltpu.roll(v, (j if transpose else -j) % t, 0), 0.0)
    return out


def _pool_fwd(proj, wbd, scale):
    t = proj.shape[0]

    def body(v_ref, w_ref, s_ref, o_ref):
        v = v_ref[...]
        row, half, cnt = _pool_windows(t)
        pooled = _pool_window_sum(v, row, half, t, False) / cnt - v
        o_ref[...] = (_dot(pooled, w_ref[...]) * s_ref[...]).astype(BF16)

    return pl.pallas_call(
        body, grid=(1,),
        in_specs=[pl.BlockSpec((t, BW), lambda i: (0, PV)), pl.BlockSpec((BW, BW), lambda i: (0, 0)), pl.BlockSpec((1, BW), lambda i: (0, 0))],
        out_specs=pl.BlockSpec((t, BW), lambda i: (0, 0)), out_shape=_sds((t, BW), BF16), name="pool_fwd",
        compiler_params=_cp())(proj, wbd, scale)


def _pool_bwd(dbr, proj, wbd, scale):
    t = proj.shape[0]

    def body(d_ref, v_ref, w_ref, s_ref, dv_ref, dw_ref, ds_ref):
        v, dout = v_ref[...], d_ref[...]
        row, half, cnt = _pool_windows(t)
        pooled = _pool_window_sum(v, row, half, t, False) / cnt - v
        mixed = _dot(pooled, w_ref[...])
        ds_ref[...] = jnp.sum(dout * mixed, axis=0, keepdims=True)
        dmixed = dout * s_ref[...]
        dw_ref[...] = _dot(pooled, dmixed, ta=True)
        dpooled = _dot(dmixed, w_ref[...], tb=True)
        dv_ref[...] = _pool_window_sum(dpooled / cnt, row, half, t, True) - dpooled

    return pl.pallas_call(
        body, grid=(1,),
        in_specs=[pl.BlockSpec((t, BW), lambda i: (0, 1)), pl.BlockSpec((t, BW), lambda i: (0, PV)),
                  pl.BlockSpec((BW, BW), lambda i: (0, 0)), pl.BlockSpec((1, BW), lambda i: (0, 0))],
        out_specs=(pl.BlockSpec((t, BW), lambda i: (0, 0)), pl.BlockSpec((BW, BW), lambda i: (0, 0)), pl.BlockSpec((1, BW), lambda i: (0, 0))),
        out_shape=(_sds((t, BW), F32), _sds((BW, BW), F32), _sds((1, BW), F32)), name="pool_bwd",
        compiler_params=_cp())(dbr, proj, wbd, scale)


NA_KEYS = NA_ROWS_WIN * GRID_W


def _na_window(r, n_rows):
    rs = jnp.clip(r - NA_ROWS_WIN // 2, 0, n_rows - NA_ROWS_WIN)
    return pl.multiple_of(rs * GRID_W, GRID_W), rs - r + (NA_ROWS_WIN - 1)


NA_STEP_ROWS = 8


def _na_fwd(q, k, v, ball):
    t = q.shape[0]
    n_rows = t // GRID_W
    rows = NA_STEP_ROWS

    def body(q_ref, k_ref, v_ref, b_ref, o_ref):
        for rr in range(rows):
            start, a0 = _na_window(pl.program_id(0) * rows + rr, n_rows)
            own = slice(rr * GRID_W, (rr + 1) * GRID_W)
            qs = _stack_heads(q_ref[own, :])
            s = _dot(qs, k_ref[pl.ds(start, NA_KEYS), :], tb=True) * (HD ** -0.5) + b_ref[a0]
            p = _softmax_rows(s)
            o_ref[own, :] = _unstack_heads(_dot(p, v_ref[pl.ds(start, NA_KEYS), :]), GRID_W).astype(BF16)

    blk = pl.BlockSpec((rows * GRID_W, BW), lambda r: (r, 0))
    whole = pl.BlockSpec((t, BW), lambda r: (0, 0))
    return pl.pallas_call(
        body, grid=(n_rows // rows,), in_specs=[blk, whole, whole, pl.BlockSpec(ball.shape, lambda r: (0, 0, 0))],
        out_specs=blk, out_shape=_sds((t, BW), BF16), name="na_fwd", compiler_params=_cp())(q, k, v, ball)


def _na_bwd(dbr, q, k, v, ball):
    t = q.shape[0]
    n_rows = t // GRID_W

    rows = NA_STEP_ROWS

    def body(d_ref, q_ref, k_ref, v_ref, b_ref, dq_ref, dk_ref, dv_ref, db_ref):
        @pl.when(pl.program_id(0) == 0)
        def _():
            dk_ref[...] = jnp.zeros_like(dk_ref)
            dv_ref[...] = jnp.zeros_like(dv_ref)
            db_ref[...] = jnp.zeros_like(db_ref)

        for rr in range(rows):
            start, a0 = _na_window(pl.program_id(0) * rows + rr, n_rows)
            keys = pl.ds(start, NA_KEYS)
            own = slice(rr * GRID_W, (rr + 1) * GRID_W)
            qs = _stack_heads(q_ref[own, :])
            kb, vb = k_ref[keys, :], v_ref[keys, :]
            p = _softmax_rows(_dot(qs, kb, tb=True) * (HD ** -0.5) + b_ref[a0])
            dos = _stack_heads(d_ref[own, :]).astype(MXU)
            dp = _dot(dos, vb, tb=True)
            dv_ref[keys, :] += _dot(p, dos, ta=True)
            ds = p * (dp - jnp.sum(dp * p, axis=-1, keepdims=True))
            db_ref[a0] += ds
            dsb = (ds * (HD ** -0.5)).astype(MXU)
            dq_ref[own, :] = _unstack_heads(_dot(dsb, kb), GRID_W)
            dk_ref[keys, :] += _dot(dsb, qs, ta=True)

    blk = pl.BlockSpec((rows * GRID_W, BW), lambda r: (r, 0))
    whole = pl.BlockSpec((t, BW), lambda r: (0, 0))
    tab = pl.BlockSpec(ball.shape, lambda r: (0, 0, 0))
    return pl.pallas_call(
        body, grid=(n_rows // rows,), in_specs=[pl.BlockSpec((rows * GRID_W, BW), lambda r: (r, 2)), blk, whole, whole, tab],
        out_specs=(blk, whole, whole, tab),
        out_shape=(_sds((t, BW), F32), _sds((t, BW), F32), _sds((t, BW), F32), _sds(ball.shape, F32)), name="na_bwd",
        compiler_params=_cp())(dbr, q, k, v, ball)


def _rpb_expand(rpb_pad, onehot):
    def body(r_ref, e_ref, o_ref):
        o_ref[...] = jnp.dot(r_ref[...], e_ref[...], precision=HI, preferred_element_type=F32)

    return pl.pallas_call(body, out_shape=_sds((rpb_pad.shape[0], GRID_W * GRID_W), F32), name="rpb_expand",
                          compiler_params=_cp())(rpb_pad, onehot)


def _rpb_reduce(dtab, onehot):
    def body(d_ref, e_ref, o_ref):
        o_ref[...] = lax.dot_general(d_ref[...], e_ref[...], (((1,), (1,)), ((), ())), precision=HI, preferred_element_type=F32)

    return pl.pallas_call(body, out_shape=_sds((dtab.shape[0], 128), F32), name="rpb_reduce", compiler_params=_cp())(dtab, onehot)


MEM_TQ = 256


def _mem_fwd(q, mk, mv):
    t = q.shape[0]
    tq = MEM_TQ

    def body(q_ref, k_ref, v_ref, o_ref):
        p = _softmax_rows(_dot(_stack_heads(q_ref[...]), k_ref[...], tb=True) * (HD ** -0.5))
        o_ref[...] = _unstack_heads(_dot(p, v_ref[...]), tq).astype(BF16)

    blk = pl.BlockSpec((tq, BW), lambda i: (i, 0))
    kv = pl.BlockSpec((N_MEM, BW), lambda i: (0, 0))
    return pl.pallas_call(body, grid=(t // tq,), in_specs=[blk, kv, kv], out_specs=blk, out_shape=_sds((t, BW), BF16),
                          name="mem_fwd", compiler_params=_cp())(q, mk, mv)


def _mem_bwd(dbr, q, mk, mv):
    t = q.shape[0]
    tq = MEM_TQ

    def body(d_ref, q_ref, k_ref, v_ref, dq_ref, dk_ref, dv_ref):
        first = pl.program_id(0) == 0
        qs = _stack_heads(q_ref[...])
        dos = _stack_heads(d_ref[...]).astype(MXU)
        p = _softmax_rows(_dot(qs, k_ref[...], tb=True) * (HD ** -0.5))
        dp = _dot(dos, v_ref[...], tb=True)
        _acc(dv_ref, _dot(p, dos, ta=True), first)
        dsb = (p * (dp - jnp.sum(dp * p, axis=-1, keepdims=True)) * (HD ** -0.5)).astype(MXU)
        dq_ref[...] = _unstack_heads(_dot(dsb, k_ref[...]), tq)
        _acc(dk_ref, _dot(dsb, qs, ta=True), first)

    blk = pl.BlockSpec((tq, BW), lambda i: (i, 0))
    kv = pl.BlockSpec((N_MEM, BW), lambda i: (0, 0))
    return pl.pallas_call(
        body, grid=(t // tq,), in_specs=[pl.BlockSpec((tq, BW), lambda i: (i, 3)), blk, kv, kv], out_specs=(blk, kv, kv),
        out_shape=(_sds((t, BW), F32), _sds((N_MEM, BW), F32), _sds((N_MEM, BW), F32)), name="mem_bwd",
        compiler_params=_cp())(dbr, q, mk, mv)


def _memkv_prep(kv, g_mk):
    def body(kv_ref, g_ref, k_ref, v_ref):
        k_ref[...] = _gnorm(kv_ref[:, 0:BW], g_ref[...]).astype(BF16)
        v_ref[...] = kv_ref[:, BW:2 * BW].astype(BF16)

    return pl.pallas_call(body, out_shape=(_sds((N_MEM, BW), BF16), _sds((N_MEM, BW), BF16)), name="memkv_prep",
                          compiler_params=_cp())(kv, g_mk)


def _memkv_bwd(kv, dk, dv, g_mk):
    def body(kv_ref, dk_ref, dv_ref, g_ref, o_ref, dg_ref):
        dkk, gain = _gnorm_bwd(dk_ref[...], kv_ref[:, 0:BW], g_ref[...])
        o_ref[:, 0:BW] = dkk.astype(BF16)
        o_ref[:, BW:2 * BW] = dv_ref[...].astype(BF16)
        dg_ref[...] = jnp.sum(gain, axis=0, keepdims=True)

    return pl.pallas_call(body, out_shape=(_sds((N_MEM, 2 * BW), BF16), _sds((1, BW), F32)), name="memkv_bwd",
                          compiler_params=_cp())(kv, dk, dv, g_mk)


MERGE_TM = 256


def _merge_fwd(brs, wbt, gp):
    t = gp.shape[0]
    tm = MERGE_TM

    def body(b0, b1, b2, b3, wb_ref, gp_ref, o_ref):
        out = jnp.zeros((tm, D), F32)
        for n, b_ref in enumerate((b0, b1, b2, b3)):
            up = _dot(b_ref[...], wb_ref[n], tb=True)
            out = out + _sigmoid(gp_ref[:, n * D:(n + 1) * D].astype(F32)) * up
        o_ref[...] = out.astype(BF16)

    blk = pl.BlockSpec((tm, BW), lambda i: (i, 0))
    return pl.pallas_call(
        body, grid=(t // tm,),
        in_specs=[blk, blk, blk, blk, pl.BlockSpec((NH, D, BW), lambda i: (0, 0, 0)), pl.BlockSpec((tm, NH * D), lambda i: (i, 0))],
        out_specs=pl.BlockSpec((tm, D), lambda i: (i, 0)), out_shape=_sds((t, D), BF16), name="merge_fwd",
        compiler_params=_cp())(*brs, wbt, gp)


def _merge_bwd(dmerged, brs, wbt, gp):
    t = gp.shape[0]
    tm = MERGE_TM

    def body(d_ref, b0, b1, b2, b3, wb_ref, gp_ref, dgp_ref, dup_ref):
        dm = d_ref[...]
        for n, b_ref in enumerate((b0, b1, b2, b3)):
            up = _dot(b_ref[...], wb_ref[n], tb=True)
            g = _sigmoid(gp_ref[:, n * D:(n + 1) * D].astype(F32))
            dgp_ref[:, n * D:(n + 1) * D] = (dm * up * (g * (1.0 - g))).astype(BF16)
            dup_ref[:, n * D:(n + 1) * D] = (dm * g).astype(BF16)

    row = pl.BlockSpec((tm, D), lambda i: (i, 0))
    blk = pl.BlockSpec((tm, BW), lambda i: (i, 0))
    wide = pl.BlockSpec((tm, NH * D), lambda i: (i, 0))
    return pl.pallas_call(
        body, grid=(t // tm,), in_specs=[row, blk, blk, blk, blk, pl.BlockSpec((NH, D, BW), lambda i: (0, 0, 0)), wide],
        out_specs=(wide, wide), out_shape=(_sds((t, NH * D), BF16), _sds((t, NH * D), BF16)), name="merge_bwd",
        compiler_params=_cp())(dmerged, *brs, wbt, gp)


def _dbranch(dup, wbt):
    t = dup.shape[0]
    tm = 1024

    def body(d_ref, w_ref, o_ref):
        o_ref[...] = _dot(d_ref[...], w_ref[...])

    return pl.pallas_call(
        body, grid=(t // tm, NH), in_specs=[pl.BlockSpec((tm, D), lambda i, n: (i, n)), pl.BlockSpec((None, D, BW), lambda i, n: (n, 0, 0))],
        out_specs=pl.BlockSpec((tm, BW), lambda i, n: (i, n)), out_shape=_sds((t, NH * BW), F32), name="dbranch",
        compiler_params=_cp())(dup, wbt)


def _dwbranch(brs, dup):
    t = dup.shape[0]

    def body(b0, b1, b2, b3, d_ref, o_ref):
        for n, b_ref in enumerate((b0, b1, b2, b3)):
            o_ref[n] = _dot(d_ref[:, n * D:(n + 1) * D], b_ref[...], ta=True).astype(BF16)

    return pl.pallas_call(body, out_shape=_sds((NH, D, BW), BF16), name="dwbranch", compiler_params=_cp())(*brs, dup)


def _swiglu_fwd(ag):
    t = ag.shape[0]
    tm = 256

    def body(ag_ref, o_ref):
        a, g = ag_ref[:, 0:FF].astype(F32), ag_ref[:, FF:2 * FF].astype(F32)
        o_ref[...] = (a * _sigmoid(a) * g).astype(BF16)

    return pl.pallas_call(body, grid=(t // tm,), in_specs=[pl.BlockSpec((tm, 2 * FF), lambda i: (i, 0))],
                          out_specs=pl.BlockSpec((tm, FF), lambda i: (i, 0)), out_shape=_sds((t, FF), BF16), name="swiglu_fwd",
                          compiler_params=_cp())(ag)


def _swiglu_bwd(ag, dy):
    t = ag.shape[0]
    tm = 256

    def body(ag_ref, dy_ref, o_ref):
        a, g, d = ag_ref[:, 0:FF].astype(F32), ag_ref[:, FF:2 * FF].astype(F32), dy_ref[...].astype(F32)
        s = _sigmoid(a)
        o_ref[:, 0:FF] = (d * g * (s * (1.0 + a * (1.0 - s)))).astype(BF16)
        o_ref[:, FF:2 * FF] = (d * (a * s)).astype(BF16)

    return pl.pallas_call(
        body, grid=(t // tm,), in_specs=[pl.BlockSpec((tm, 2 * FF), lambda i: (i, 0)), pl.BlockSpec((tm, FF), lambda i: (i, 0))],
        out_specs=pl.BlockSpec((tm, 2 * FF), lambda i: (i, 0)), out_shape=_sds((t, 2 * FF), BF16), name="swiglu_bwd",
        compiler_params=_cp())(ag, dy)


def _loss_head(y, target):
    t, d = y.shape
    tm = 256

    def body(y_ref, t_ref, dy_ref, dyb_ref, l_ref):
        e = y_ref[...] - t_ref[...]
        dy_ref[...] = e * (1.0 / d)
        dyb_ref[...] = (e * (1.0 / d)).astype(BF16)
        _acc(l_ref, jnp.full((8, 128), 0.5 * jnp.sum(jnp.sum(e * e, axis=-1, keepdims=True) * (1.0 / d)), F32), pl.program_id(0) == 0)

    row = pl.BlockSpec((tm, d), lambda i: (i, 0))
    return pl.pallas_call(body, grid=(t // tm,), in_specs=[row, row], out_specs=(row, row, pl.BlockSpec((8, 128), lambda i: (0, 0))),
                          out_shape=(_sds((t, d), F32), _sds((t, d), BF16), _sds((8, 128), F32)), name="loss_head",
                          compiler_params=_cp())(y, target)


def _sum_slots(x, name):
    k, r, c = x.shape
    tr = _tile(r, 512) if r % 128 == 0 else r

    def body(x_ref, o_ref):
        acc = x_ref[0].astype(F32)
        for s in range(1, k):
            acc = acc + x_ref[s].astype(F32)
        o_ref[...] = acc

    return pl.pallas_call(body, grid=(r // tr,), in_specs=[pl.BlockSpec((k, tr, c), lambda i: (0, i, 0))],
                          out_specs=pl.BlockSpec((tr, c), lambda i: (i, 0)), out_shape=_sds((r, c), F32), name=name,
                          compiler_params=_cp())(x)


def _pair_sum(bufs, recvs, cidx):
    n = len(bufs)

    def body(c_ref, *refs):
        for i in range(n):
            refs[2 * n + i][...] = (refs[i][...].astype(F32) + refs[n + i][...].astype(F32)).astype(BF16)

    return pl.pallas_call(
        body,
        grid_spec=pltpu.PrefetchScalarGridSpec(
            num_scalar_prefetch=1, grid=(4,),
            in_specs=[pl.BlockSpec((None, None) + b.shape[2:], lambda s, cref: (s, cref[0], 0, 0)) for b in bufs]
            + [pl.BlockSpec((None,) + r.shape[1:], lambda s, cref: (s, 0, 0)) for r in recvs],
            out_specs=tuple(pl.BlockSpec((None,) + r.shape[1:], lambda s, cref: (s, 0, 0)) for r in recvs)),
        out_shape=tuple(_sds(r.shape, BF16) for r in recvs), name="rs_pair_sum", compiler_params=_cp())(cidx, *bufs, *recvs)


def _adamw_update(w, gv, m, v):
    mn = ADAM_B1 * m + (1.0 - ADAM_B1) * gv
    vn = ADAM_B2 * v + (1.0 - ADAM_B2) * (gv * gv)
    m_hat = mn / (1.0 - ADAM_B1 ** ADAM_STEP)
    v_hat = vn / (1.0 - ADAM_B2 ** ADAM_STEP)
    return -ADAM_LR * (m_hat / (jnp.sqrt(v_hat) + ADAM_EPS) + ADAM_WD * w), mn, vn


def _adamw(w, g, m, v, name):
    r, c = w.shape

    def body(w_ref, g_ref, m_ref, v_ref, d_ref, nm_ref, nv_ref):
        d_ref[...], nm_ref[...], nv_ref[...] = _adamw_update(w_ref[...], g_ref[...], m_ref[...], v_ref[...])

    blk = pl.BlockSpec((r, c), lambda i: (0, 0))
    return pl.pallas_call(body, grid=(1,), in_specs=[blk] * 4, out_specs=(blk,) * 3,
                          out_shape=tuple(_sds((r, c), F32) for _ in range(3)), name=name, compiler_params=_cp())(w, g, m, v)


def _adamw_layer(layer, w, g, m, v, outs, name):
    _, r, c = w.shape
    tr = max(d for d in range(8, r + 1, 8) if r % d == 0 and d * c * 4 <= 2 ** 20)

    def body(w_ref, m_ref, v_ref, g_ref, *refs):
        d_ref, nm_ref, nv_ref, go_ref = refs[4:]
        gv = g_ref[...]
        d_ref[...], nm_ref[...], nv_ref[...] = _adamw_update(w_ref[...], gv, m_ref[...], v_ref[...])
        go_ref[...] = gv

    blk = pl.BlockSpec((None, tr, c), lambda i: (layer, i, 0))
    return pl.pallas_call(
        body, grid=(r // tr,), in_specs=[blk] * 3 + [pl.BlockSpec((tr, c), lambda i: (i, 0))] + [ANY] * 4, out_specs=(blk,) * 4,
        out_shape=tuple(_sds(w.shape, F32) for _ in range(4)), input_output_aliases={4 + j: j for j in range(4)}, name=name,
        compiler_params=_cp())(w, m, v, g, *outs)


def _all_gather(shards, name):
    n = len(shards)

    def body(*refs):
        x_refs, out_refs = refs[:n], refs[n:2 * n]
        send_sems, recv_sems, local_sems = refs[2 * n:]
        x, y, cc = lax.axis_index("x"), lax.axis_index("y"), lax.axis_index("c")
        me, sibling = (x, y, cc), (x, y, 1 - cc)
        chips = [(1 - x, y), (x, 1 - y), (1 - x, 1 - y)]

        def copy(i, k, block, to, own=False):
            px, py, pc = block
            slot = out_refs[i].at[4 * px + 2 * py + pc]
            return pltpu.make_async_remote_copy(
                src_ref=x_refs[i] if own else slot, dst_ref=slot, send_sem=send_sems.at[7 * i + k],
                recv_sem=recv_sems.at[7 * i + k], device_id=to, device_id_type=MESH)

        mine = [pltpu.make_async_copy(x_refs[i], out_refs[i].at[4 * x + 2 * y + cc], local_sems.at[i]) for i in range(n)]
        for cp in mine:
            cp.start()
        first = []
        for j, chip in enumerate(chips):
            first += [copy(i, 1 + j, me, (*chip, cc), own=True) for i in range(n)]
        first += [copy(i, 0, me, sibling, own=True) for i in range(n)]
        for cp in first:
            cp.start()
        passed = []
        for j, chip in enumerate(chips):
            for i in range(n):
                copy(i, 1 + j, (*chip, cc), me).wait_recv()
                cp = copy(i, 4 + j, (*chip, cc), sibling)
                cp.start()
                passed.append(cp)
        for i in range(n):
            copy(i, 0, sibling, me).wait_recv()
        for j, chip in enumerate(chips):
            for i in range(n):
                copy(i, 4 + j, (*chip, 1 - cc), me).wait_recv()
        for cp in first + passed:
            cp.wait_send()
        for cp in mine:
            cp.wait()

    return pl.pallas_call(
        body, out_shape=tuple(_sds((N_DEV,) + s.shape, s.dtype) for s in shards), in_specs=[ANY] * n, out_specs=(ANY,) * n,
        scratch_shapes=[pltpu.SemaphoreType.DMA((7 * n,)), pltpu.SemaphoreType.DMA((7 * n,)), pltpu.SemaphoreType.DMA((n,))],
        name=name)(*shards)


def _rs_core_swap(bufs, name):
    n = len(bufs)

    def body(*refs):
        b_refs, recv_refs = refs[:n], refs[n:2 * n]
        send_sems, recv_sems = refs[2 * n:]
        x, y, cc = lax.axis_index("x"), lax.axis_index("y"), lax.axis_index("c")
        copies = [pltpu.make_async_remote_copy(
            src_ref=b_refs[i].at[s, 1 - cc], dst_ref=recv_refs[i].at[s], send_sem=send_sems.at[4 * i + s],
            recv_sem=recv_sems.at[4 * i + s], device_id=(x, y, 1 - cc), device_id_type=MESH) for i in range(n) for s in range(4)]
        for cp in copies:
            cp.start()
        for cp in copies:
            cp.wait()

    return pl.pallas_call(
        body, out_shape=tuple(_sds((4,) + b.shape[2:], b.dtype) for b in bufs), in_specs=[ANY] * n, out_specs=(ANY,) * n,
        scratch_shapes=[pltpu.SemaphoreType.DMA((4 * n,)), pltpu.SemaphoreType.DMA((4 * n,))], name=name)(*bufs)


HBM = pl.BlockSpec(memory_space=pltpu.HBM)
SEMS = pl.BlockSpec(memory_space=pltpu.SEMAPHORE)
EFFECT = pltpu.SideEffectType.DATAFLOW_SIDE_EFFECTING


def _hbm(a):
    return pltpu.HBM(a.shape, a.dtype)


def _other_chips(x, y):
    return [(1 - x, y), (x, 1 - y), (1 - x, 1 - y)]


def _ici_start(srcs, lands, mode, name):
    n = len(srcs)

    def body(*refs):
        s_refs, land_refs = refs[:n], refs[n:2 * n]
        send_sems, recv_sems = refs[2 * n], refs[2 * n + 1]
        token = refs[-1]
        x, y, cc = lax.axis_index("x"), lax.axis_index("y"), lax.axis_index("c")
        mine = 2 * x + y if mode == "by_chip" else 4 * x + 2 * y + cc
        peers = [(px, py, cc) for px, py in _other_chips(x, y)]
        if mode == "by_device":
            peers = [(x, y, 1 - cc)] + peers + [(px, py, 1 - cc) for px, py in _other_chips(x, y)]
        for px, py, pc in peers:
            for i in range(n):
                src = s_refs[i]
                if mode == "by_chip":
                    src = src.at[2 * px + py]
                elif mode == "by_device":
                    src = src.at[4 * px + 2 * py + pc]
                pltpu.make_async_remote_copy(
                    src_ref=src, dst_ref=land_refs[i].at[mine], send_sem=send_sems.at[i], recv_sem=recv_sems.at[i],
                    device_id=(px, py, pc), device_id_type=MESH).start()
        token[...] = jnp.zeros_like(token)

    out = pl.pallas_call(
        body, name=name,
        out_shape=(pltpu.SemaphoreType.DMA((n,)), pltpu.SemaphoreType.DMA((n,)), *[_hbm(s) for s in srcs], *[_hbm(l) for l in lands],
                   _sds((8, 128), F32)),
        in_specs=[HBM] * (2 * n), out_specs=(SEMS, SEMS, *[HBM] * (2 * n), pl.BlockSpec(memory_space=pltpu.VMEM)),
        input_output_aliases={i: 2 + i for i in range(2 * n)}, compiler_params=pltpu.CompilerParams(has_side_effects=EFFECT),
    )(*[pltpu.with_memory_space_constraint(s, pltpu.HBM) for s in srcs],
      *[pltpu.with_memory_space_constraint(l, pltpu.HBM) for l in lands])
    return out[0], out[1], out[2:2 + n], out[2 + n:2 + 2 * n], out[-1], 7 if mode == "by_device" else 3


def _ici_wait(started, after, name, only=None):
    send_sems, recv_sems, srcs, lands, _, copies = started
    only = list(range(len(srcs))) if only is None else only
    srcs, lands = [srcs[i] for i in only], [lands[i] for i in only]
    n = len(srcs)

    def body(*refs):
        land_refs = refs[n:2 * n]
        send_sems, recv_sems = refs[2 * n], refs[2 * n + 1]
        x, y, cc = lax.axis_index("x"), lax.axis_index("y"), lax.axis_index("c")
        for i in range(n):
            three = land_refs[i].at[pl.ds(0, copies)]
            cp = pltpu.make_async_remote_copy(src_ref=three, dst_ref=three, send_sem=send_sems.at[only[i]],
                                              recv_sem=recv_sems.at[only[i]],
                                              device_id=(x, y, cc), device_id_type=MESH)
            cp.wait_send()
            cp.wait_recv()

    return pl.pallas_call(
        body, name=name, out_shape=tuple(_hbm(l) for l in lands), in_specs=[HBM] * (2 * n) + [SEMS, SEMS, ANY],
        out_specs=tuple([HBM] * n), input_output_aliases={n + i: i for i in range(n)},
        compiler_params=pltpu.CompilerParams(has_side_effects=EFFECT))(*srcs, *lands, send_sems, recv_sems, after)


def _gather_d2d(blocks, lands, name):
    n = len(blocks)

    def body(*refs):
        x_refs, land_refs = refs[:n], refs[2 * n:3 * n]
        send_sems, recv_sems, in_sems, out_sems = refs[3 * n:3 * n + 4]
        stage = refs[3 * n + 4:]
        x, y, cc = lax.axis_index("x"), lax.axis_index("y"), lax.axis_index("c")
        sibling = (x, y, 1 - cc)
        staged = [pltpu.make_async_copy(x_refs[i], stage[i], in_sems.at[i]) for i in range(n)]
        for cp in staged:
            cp.start()
        copies = []
        for i in range(n):
            slot = land_refs[i].at[4 * x + 2 * y + cc]
            copies.append(pltpu.make_async_remote_copy(src_ref=x_refs[i], dst_ref=slot, send_sem=send_sems.at[4 * i],
                                                       recv_sem=recv_sems.at[4 * i], device_id=sibling, device_id_type=MESH))
            for j, (px, py) in enumerate(_other_chips(x, y)):
                slot = land_refs[i].at[4 * px + 2 * py + cc]
                copies.append(pltpu.make_async_remote_copy(src_ref=slot, dst_ref=slot, send_sem=send_sems.at[4 * i + 1 + j],
                                                           recv_sem=recv_sems.at[4 * i + 1 + j], device_id=sibling, device_id_type=MESH))
        for cp in copies:
            cp.start()
        mine = []
        for i in range(n):
            staged[i].wait()
            mine.append(pltpu.make_async_copy(stage[i], land_refs[i].at[4 * x + 2 * y + cc], out_sems.at[i]))
            mine[i].start()
        for i in range(n):
            slot = land_refs[i].at[4 * x + 2 * y + (1 - cc)]
            pltpu.make_async_remote_copy(src_ref=slot, dst_ref=slot, send_sem=send_sems.at[4 * i], recv_sem=recv_sems.at[4 * i],
                                         device_id=sibling, device_id_type=MESH).wait_recv()
            for j, (px, py) in enumerate(_other_chips(x, y)):
                slot = land_refs[i].at[4 * px + 2 * py + (1 - cc)]
                pltpu.make_async_remote_copy(src_ref=slot, dst_ref=slot, send_sem=send_sems.at[4 * i + 1 + j],
                                             recv_sem=recv_sems.at[4 * i + 1 + j], device_id=sibling, device_id_type=MESH).wait_recv()
        for cp in copies:
            cp.wait_send()
        for cp in mine:
            cp.wait()

    return pl.pallas_call(
        body, out_shape=tuple(_sds(l.shape, l.dtype) for l in lands), in_specs=[ANY] * (2 * n), out_specs=(ANY,) * n,
        input_output_aliases={n + i: i for i in range(n)},
        scratch_shapes=[pltpu.SemaphoreType.DMA((4 * n,)), pltpu.SemaphoreType.DMA((4 * n,)), pltpu.SemaphoreType.DMA((n,)),
                        pltpu.SemaphoreType.DMA((n,))] + [pltpu.VMEM(b.shape, b.dtype) for b in blocks],
        name=name, compiler_params=_cp())(*blocks, *lands)


def _sum_own(parts, recvs, mine, name):
    n = len(parts)

    def body(c_ref, *refs):
        s = pl.program_id(0)
        for i in range(n):
            val = jnp.where(c_ref[0] == s, refs[i][...], refs[n + i][...]).astype(F32)
            _acc(refs[2 * n + i], val, s == 0)

    kept = [pl.BlockSpec((None,) + p.shape[1:], lambda s, cref: (cref[0], 0, 0)) for p in parts]
    ins = [pl.BlockSpec((None,) + p.shape[1:], lambda s, cref: (s, 0, 0)) for p in parts]
    return pl.pallas_call(
        body, grid_spec=pltpu.PrefetchScalarGridSpec(
            num_scalar_prefetch=1, grid=(parts[0].shape[0],), in_specs=kept + ins,
            out_specs=tuple(pl.BlockSpec(p.shape[1:], lambda s, cref: (0, 0)) for p in parts)),
        out_shape=tuple(_sds(p.shape[1:], F32) for p in parts), name=name, compiler_params=_cp())(mine, *parts, *recvs)


BIG = (("w_in", True), ("w_gate", True), ("w_mem_kv", False), ("w_branch", True), ("w_out", False), ("w_ffn_in", True),
       ("w_ffn_out", False))

SMALL = ("norm_mix_g", "norm_mem_g", "ret_decay_fwd", "ret_decay_bwd", "ret_norm_g", "pool_w", "pool_scale", "na_q_norm_g",
         "na_k_norm_g", "na_rpb", "mem_q_norm_g", "mem_k_norm_g", "norm_ffn_g")
WEIGHTS = ("norm_mix_g", "norm_mem_g", "w_in", "w_gate", "ret_decay_fwd", "ret_decay_bwd", "ret_norm_g", "pool_w", "pool_scale",
           "na_q_norm_g", "na_k_norm_g", "na_rpb", "mem_q_norm_g", "mem_k_norm_g", "w_mem_kv", "w_branch", "w_out", "norm_ffn_g",
           "w_ffn_in", "w_ffn_out")


def _to_exchange(name, transposed, shard):
    if name == "w_branch":
        return jnp.swapaxes(shard, 1, 2).reshape(NH * (D // N_DEV), BW)
    return shard.T if transposed else shard


def _from_exchange(name, transposed, block):
    if name == "w_branch":
        return jnp.swapaxes(block.reshape(NH, D // N_DEV, BW), 1, 2)
    return block.T if transposed else block


def _whole_from_gathered(name, g):
    if name == "w_branch":
        return jnp.swapaxes(g.reshape(N_DEV, NH, D // N_DEV, BW), 0, 1).reshape(NH, D, BW)
    return g.reshape(N_DEV * g.shape[1], g.shape[2])


def _by_destination(name, g):
    if name == "w_branch":
        g = jnp.swapaxes(g.reshape(NH, N_DEV, D // N_DEV, BW), 0, 1).reshape(N_DEV * NH * (D // N_DEV), BW)
    return g.reshape(4, 2, g.shape[0] // N_DEV, g.shape[1])


SMALL_PAD = 1024


def _pack_small(vals, loss=None):
    parts = [vals[n] for n in SMALL] + [jnp.zeros((1,), F32) if loss is None else loss.reshape(1)]
    rows = []
    for p in parts:
        flat = p.reshape(-1)
        rows.append(jnp.pad(flat, (0, -flat.shape[0] % SMALL_PAD)).reshape(-1, 128))
    return jnp.concatenate(rows, axis=0)


def _unpack_small(packed, like):
    out, off = {}, 0
    for n in SMALL:
        sz = int(np.prod(like[n].shape))
        nrow = -(-sz // SMALL_PAD) * (SMALL_PAD // 128)
        out[n] = packed[off:off + nrow].reshape(-1)[:sz].reshape(like[n].shape)
        off += nrow
    return out, packed[off, 0]


def _na_constants():
    c = np.arange(GRID_W)
    win = np.clip(c - NA_COLS_WIN // 2, 0, GRID_W - NA_COLS_WIN)
    kc = np.arange(GRID_W)
    inside = (kc[None, :] >= win[:, None]) & (kc[None, :] < win[:, None] + NA_COLS_WIN)
    off = kc[None, :] - c[:, None] + NA_COLS_WIN - 1
    onehot = np.zeros((128, GRID_W, GRID_W), np.float32)
    for b in range(2 * NA_COLS_WIN - 1):
        onehot[b] = (off == b) & inside
    maskadd = np.where(inside, 0.0, NEG).astype(np.float32)
    return onehot.reshape(128, GRID_W * GRID_W), maskadd


def _na_bias_table(tab, maskadd):
    n_off = 2 * NA_ROWS_WIN - 1
    t4 = tab[:NH * n_off].reshape(NH, n_off, GRID_W, GRID_W) + maskadd[None, None]
    ball = jnp.stack([t4[:, a0:a0 + NA_ROWS_WIN] for a0 in range(NA_ROWS_WIN)], axis=1)
    return ball.transpose(1, 0, 3, 2, 4).reshape(NA_ROWS_WIN, NH * GRID_W, NA_KEYS)


def _rotary_tables(t):
    half = HD // 2
    inv = ROPE_THETA ** (-jnp.arange(half, dtype=F32) / half)
    ang = jnp.arange(t, dtype=F32)[:, None] * inv[None, :]
    cos, sin = jnp.cos(ang), jnp.sin(ang)
    return jnp.tile(jnp.concatenate([cos, cos], axis=-1), (1, NH)), jnp.tile(jnp.concatenate([-sin, sin], axis=-1), (1, NH))


def _block_diag(pw):
    out = jnp.zeros((BW, BW), pw.dtype)
    for g in range(NH):
        out = lax.dynamic_update_slice(out, pw[g], (g * HD, g * HD))
    return out


def _tile4(g):
    return jnp.tile(g.reshape(1, HD), (1, NH))


def _layer_fwd(x, mem, sw, lw, consts, more_weights=None):
    cos2, sin2, onehot, maskadd = consts
    h = _rmsnorm_fwd(x, sw["norm_mix_g"].reshape(1, D), "norm_mix_fwd")
    proj = _mm(h, lw["w_in"], tb=True, name="mm_in")
    gp = _mm(h, lw["w_gate"], tb=True, out_dtype=BF16, name="mm_gate")
    g_naq, g_nak, g_mq = _tile4(sw["na_q_norm_g"]), _tile4(sw["na_k_norm_g"]), _tile4(sw["mem_q_norm_g"])
    rq, rk, rv, nq, nk, nv, mq = _prep_fwd(proj, cos2, sin2, g_naq, g_nak, g_mq)

    lgf, lgb = jax.nn.log_sigmoid(sw["ret_decay_fwd"]), jax.nn.log_sigmoid(sw["ret_decay_bwd"])
    g_ret = sw["ret_norm_g"].reshape(1, BW)
    o_ret, ret = _ret_fwd(rq, rk, rv, proj, lgf, lgb, g_ret)

    wbd = _block_diag(sw["pool_w"]).astype(BF16)
    p_scale = sw["pool_scale"].reshape(1, BW)
    pool = _pool_fwd(proj, wbd, p_scale)

    ball = sw["na_bias_table"]
    na = _na_fwd(nq, nk, nv, ball)

    memn = _rmsnorm_fwd(mem, sw["norm_mem_g"].reshape(1, D), "norm_mem_fwd")
    kv = _mm(memn, lw["w_mem_kv"], name="mm_memkv")
    g_mk = _tile4(sw["mem_k_norm_g"])
    mk, mv = _memkv_prep(kv, g_mk)
    mo = _mem_fwd(mq, mk, mv)

    br = (ret, pool, na, mo)
    merged = _merge_fwd(br, lw["w_branch"], gp)
    x1 = _mm(merged, lw["w_out"], add=x, name="mm_out")
    if more_weights is not None:
        lw.update(more_weights(x1))
    h2 = _rmsnorm_fwd(x1, sw["norm_ffn_g"].reshape(1, D), "norm_ffn_fwd")
    ag = _mm(h2, lw["w_ffn_in"], tb=True, out_dtype=BF16, name="mm_ffn_in")
    yff = _swiglu_fwd(ag)
    x2 = _mm(yff, lw["w_ffn_out"], add=x1, name="mm_ffn_out")
    saved = dict(x=x, h=h, proj=proj, gp=gp, rq=rq, rk=rk, rv=rv, nq=nq, nk=nk, nv=nv, mq=mq, o_ret=o_ret, ball=ball, memn=memn,
                 kv=kv, mk=mk, mv=mv, br=br, merged=merged, x1=x1, h2=h2, ag=ag, yff=yff, lgf=lgf, lgb=lgb, wbd=wbd)
    return x2, saved


def _layer_bwd(dx2, dx2b, mem, sw, lw, sv, consts, dep=None):
    cos2, sin2, onehot, maskadd = consts
    gb, gs = {}, {}
    dy = _mm(dx2b, lw["w_ffn_out"], tb=True, out_dtype=BF16, dep=dep, name="mm_ffn_out_dx")
    gb["w_ffn_out"] = _mm(sv["yff"], dx2b, ta=True, out_dtype=BF16, name="mm_ffn_out_dw")
    dag = _swiglu_bwd(sv["ag"], dy)
    dh2 = _mm(dag, lw["w_ffn_in"], name="mm_ffn_in_dx")
    gb["w_ffn_in"] = _mm(dag, sv["h2"], ta=True, out_dtype=BF16, name="mm_ffn_in_dw")
    dx1, dx1b, dg = _rmsnorm_bwd(dh2, sv["x1"], sw["norm_ffn_g"].reshape(1, D), dx2, "norm_ffn_bwd")
    gs["norm_ffn_g"] = dg.reshape(D)

    dmerged = _mm(dx1b, lw["w_out"], tb=True, name="mm_out_dx")
    gb["w_out"] = _mm(sv["merged"], dx1b, ta=True, out_dtype=BF16, name="mm_out_dw")
    dgp, dup = _merge_bwd(dmerged, sv["br"], lw["w_branch"], sv["gp"])
    dbr = _dbranch(dup, lw["w_branch"])
    gb["w_branch"] = _dwbranch(sv["br"], dup)

    g_ret = sw["ret_norm_g"].reshape(1, BW)
    do_ret, d_rg, dg_ret = _ret_post_bwd(dbr, sv["o_ret"], sv["proj"], g_ret)
    d_rq, d_rk, d_rv, dlg = _ret_bwd(do_ret, sv["rq"], sv["rk"], sv["rv"], sv["lgf"], sv["lgb"])
    gs["ret_norm_g"] = dg_ret.reshape(BW)
    _, vjp_f = jax.vjp(jax.nn.log_sigmoid, sw["ret_decay_fwd"])
    _, vjp_b = jax.vjp(jax.nn.log_sigmoid, sw["ret_decay_bwd"])
    gs["ret_decay_fwd"] = vjp_f(dlg[0:NH, 0])[0]
    gs["ret_decay_bwd"] = vjp_b(dlg[NH:2 * NH, 0])[0]

    p_scale = sw["pool_scale"].reshape(1, BW)
    d_pv, dwbd, dscale = _pool_bwd(dbr, sv["proj"], sv["wbd"], p_scale)
    gs["pool_w"] = jnp.stack([dwbd[g * HD:(g + 1) * HD, g * HD:(g + 1) * HD] for g in range(NH)])
    gs["pool_scale"] = dscale.reshape(BW)

    d_nq, d_nk, d_nv, gs["na_bias_table"] = _na_bwd(dbr, sv["nq"], sv["nk"], sv["nv"], sv["ball"])

    d_mq, d_mk, d_mv = _mem_bwd(dbr, sv["mq"], sv["mk"], sv["mv"])
    g_mk = _tile4(sw["mem_k_norm_g"])
    dkv, dg_mk = _memkv_bwd(sv["kv"], d_mk, d_mv, g_mk)
    gs["mem_k_norm_g"] = dg_mk.reshape(NH, HD).sum(0)
    gb["w_mem_kv"] = _mm(sv["memn"], dkv, ta=True, out_dtype=BF16, name="mm_memkv_dw")
    dmemn = _mm(dkv, lw["w_mem_kv"], tb=True, name="mm_memkv_dx")
    _, _, dg_mem = _rmsnorm_bwd(dmemn, mem, sw["norm_mem_g"].reshape(1, D), jnp.zeros_like(mem), "norm_mem_bwd")
    gs["norm_mem_g"] = dg_mem.reshape(D)

    g_naq, g_nak, g_mq = _tile4(sw["na_q_norm_g"]), _tile4(sw["na_k_norm_g"]), _tile4(sw["mem_q_norm_g"])
    dproj, dg_naq, dg_nak, dg_mq = _prep_bwd(sv["proj"], cos2, sin2, g_naq, g_nak, g_mq, d_rq, d_rk, d_rv, d_rg, d_pv, d_nq, d_nk,
                                             d_nv, d_mq)
    gs["na_q_norm_g"] = dg_naq.reshape(NH, HD).sum(0)
    gs["na_k_norm_g"] = dg_nak.reshape(NH, HD).sum(0)
    gs["mem_q_norm_g"] = dg_mq.reshape(NH, HD).sum(0)

    dh = _mm(dproj, lw["w_in"], name="mm_in_dx")
    dh = _mm(dgp, lw["w_gate"], add=dh, name="mm_gate_dx")
    gb["w_in"] = _mm(dproj, sv["h"], ta=True, out_dtype=BF16, name="mm_in_dw")
    gb["w_gate"] = _mm(dgp, sv["h"], ta=True, out_dtype=BF16, name="mm_gate_dw")
    dx, dxb, dg = _rmsnorm_bwd(dh, sv["x"], sw["norm_mix_g"].reshape(1, D), dx1, "norm_mix_bwd")
    gs["norm_mix_g"] = dg.reshape(D)
    return dx, dxb, gb, gs


def _local_step(x, mem, target, small, get_layer, on_grads):
    t = x.shape[0]
    cos2, sin2 = _rotary_tables(t)
    onehot, maskadd = _na_constants()
    onehot, maskadd = jnp.asarray(onehot), jnp.asarray(maskadd)
    consts = (cos2, sin2, onehot, maskadd)

    def bias_tables(rpb):
        pad = jnp.pad(rpb.reshape(DEPTH, NH * 15, 31), ((0, 0), (0, 4), (0, 97))).reshape(DEPTH * 64, 128)
        tabs = _rpb_expand(pad, onehot).reshape(DEPTH, 64, GRID_W * GRID_W)
        return jax.vmap(lambda tab: _na_bias_table(tab, maskadd))(tabs)

    tables = bias_tables(small["na_rpb"])
    saved, weights, cur = [], [], x
    for l in range(DEPTH):
        sw = {n: small[n][l] for n in SMALL}
        sw["na_bias_table"] = tables[l]
        lw, more = get_layer(l, cur)
        weights.append(lw)
        cur, sv = _layer_fwd(cur, mem, sw, lw, consts, more)
        saved.append(sv)
    dy, dyb, loss_tile = _loss_head(cur, target)
    small_g = {n: [None] * DEPTH for n in SMALL if n != "na_rpb"}
    d_tables = [None] * DEPTH
    dep = None
    for l in reversed(range(DEPTH)):
        sw = {n: small[n][l] for n in SMALL}
        dy, dyb, gb, gs = _layer_bwd(dy, dyb, mem, sw, weights[l], saved[l], consts, dep)
        dep = on_grads(l, gb, dy)
        d_tables[l] = gs["na_bias_table"]
        for n in small_g:
            small_g[n][l] = gs[n]
    small_g = {n: jnp.stack(v) for n, v in small_g.items()}
    _, vjp_tabs = jax.vjp(jax.vmap(lambda tab: _na_bias_table(tab, maskadd)), jnp.zeros((DEPTH, 64, GRID_W * GRID_W), F32))
    d_rpb = _rpb_reduce(vjp_tabs(jnp.stack(d_tables))[0].reshape(DEPTH * 64, GRID_W * GRID_W), onehot)
    small_g["na_rpb"] = d_rpb.reshape(DEPTH, 64, 128)[:, :NH * 15, :31].reshape(DEPTH, NH, 15, 31)
    return loss_tile[0, 0], dy, small_g


def _flat2d(a):
    return a.reshape(-1, a.shape[-1])


def kernel(x, mem, norm_mix_g, norm_mem_g, w_in, w_gate, ret_decay_fwd, ret_decay_bwd, ret_norm_g, pool_w, pool_scale, na_q_norm_g, na_k_norm_g, na_rpb, mem_q_norm_g, mem_k_norm_g, w_mem_kv, w_branch, w_out, norm_ffn_g, w_ffn_in, w_ffn_out, loss_target, m_norm_mix_g, m_norm_mem_g, m_w_in, m_w_gate, m_ret_decay_fwd, m_ret_decay_bwd, m_ret_norm_g, m_pool_w, m_pool_scale, m_na_q_norm_g, m_na_k_norm_g, m_na_rpb, m_mem_q_norm_g, m_mem_k_norm_g, m_w_mem_kv, m_w_branch, m_w_out, m_norm_ffn_g, m_w_ffn_in, m_w_ffn_out, v_norm_mix_g, v_norm_mem_g, v_w_in, v_w_gate, v_ret_decay_fwd, v_ret_decay_bwd, v_ret_norm_g, v_pool_w, v_pool_scale, v_na_q_norm_g, v_na_k_norm_g, v_na_rpb, v_mem_q_norm_g, v_mem_k_norm_g, v_w_mem_kv, v_w_branch, v_w_out, v_norm_ffn_g, v_w_ffn_in, v_w_ffn_out):
    w = dict(norm_mix_g=norm_mix_g, norm_mem_g=norm_mem_g, w_in=w_in, w_gate=w_gate, ret_decay_fwd=ret_decay_fwd,
             ret_decay_bwd=ret_decay_bwd, ret_norm_g=ret_norm_g, pool_w=pool_w, pool_scale=pool_scale, na_q_norm_g=na_q_norm_g,
             na_k_norm_g=na_k_norm_g, na_rpb=na_rpb, mem_q_norm_g=mem_q_norm_g, mem_k_norm_g=mem_k_norm_g, w_mem_kv=w_mem_kv,
             w_branch=w_branch, w_out=w_out, norm_ffn_g=norm_ffn_g, w_ffn_in=w_ffn_in, w_ffn_out=w_ffn_out)
    m = dict(norm_mix_g=m_norm_mix_g, norm_mem_g=m_norm_mem_g, w_in=m_w_in, w_gate=m_w_gate, ret_decay_fwd=m_ret_decay_fwd,
             ret_decay_bwd=m_ret_decay_bwd, ret_norm_g=m_ret_norm_g, pool_w=m_pool_w, pool_scale=m_pool_scale, na_q_norm_g=m_na_q_norm_g,
             na_k_norm_g=m_na_k_norm_g, na_rpb=m_na_rpb, mem_q_norm_g=m_mem_q_norm_g, mem_k_norm_g=m_mem_k_norm_g, w_mem_kv=m_w_mem_kv,
             w_branch=m_w_branch, w_out=m_w_out, norm_ffn_g=m_norm_ffn_g, w_ffn_in=m_w_ffn_in, w_ffn_out=m_w_ffn_out)
    v = dict(norm_mix_g=v_norm_mix_g, norm_mem_g=v_norm_mem_g, w_in=v_w_in, w_gate=v_w_gate, ret_decay_fwd=v_ret_decay_fwd,
             ret_decay_bwd=v_ret_decay_bwd, ret_norm_g=v_ret_norm_g, pool_w=v_pool_w, pool_scale=v_pool_scale, na_q_norm_g=v_na_q_norm_g,
             na_k_norm_g=v_na_k_norm_g, na_rpb=v_na_rpb, mem_q_norm_g=v_mem_q_norm_g, mem_k_norm_g=v_mem_k_norm_g, w_mem_kv=v_w_mem_kv,
             w_branch=v_w_branch, w_out=v_w_out, norm_ffn_g=v_norm_ffn_g, w_ffn_in=v_w_ffn_in, w_ffn_out=v_w_ffn_out)
    assert x.shape == (1, 2048, D) and mem.shape == (1, N_MEM, D) and w_in.shape == (DEPTH, D, 9 * BW // N_DEV)

    started = []
    for l in range(DEPTH):
        blocks = [_to_exchange(name, tr, w[name][l]).astype(BF16) for name, tr in BIG]
        lands = [lax.empty((N_DEV,) + b.shape, BF16) for b in blocks]
        started.append(_ici_start(blocks, lands, "gather", "gather_ici_start_%d" % l))
    all_started = started[0][4] + started[1][4] + started[2][4] + started[3][4]

    def get_group(l, only, after, tag):
        lands = _ici_wait(started[l], after, "gather_ici_wait_%d%s" % (l, tag), only)
        whole = _gather_d2d([started[l][2][i] for i in only], lands, "gather_d2d")
        return {BIG[i][0]: _whole_from_gathered(BIG[i][0], g) for i, g in zip(only, whole)}

    def get_layer(l, after):
        if l > 0:
            return get_group(l, list(range(len(BIG))), after, ""), None
        mixer = [i for i, (name, _) in enumerate(BIG) if not name.startswith("w_ffn")]
        ffn = [i for i, (name, _) in enumerate(BIG) if name.startswith("w_ffn")]
        return get_group(l, mixer, all_started, "a"), lambda after2: get_group(l, ffn, after2, "b")

    cidx = lax.axis_index("c").astype(jnp.int32).reshape(1)
    chip = (2 * lax.axis_index("x") + lax.axis_index("y")).astype(jnp.int32).reshape(1)
    in_flight = []

    def flip_of(name, tr):
        return (lambda a: jnp.swapaxes(a, 1, 2)) if name in ("w_in", "w_ffn_in") else (lambda a: a)

    def rows3(a):
        return a.reshape(DEPTH, -1, a.shape[-1])

    opt_in = {name: tuple(rows3(flip_of(name, tr)(t[name])) for t in (w, m, v)) for name, tr in BIG}
    opt_out = {name: tuple(lax.empty(opt_in[name][0].shape, F32) for _ in range(4)) for name, _ in BIG}

    device = (2 * chip + cidx).astype(jnp.int32)

    def finish(l, st, after):
        recv = _ici_wait(st, after, "rs_ici_wait_%d" % l)
        sums = _sum_own(st[2], recv, chip if st[5] == 3 else device, "rs_sum")
        for (name, tr), s in zip(BIG, sums):
            g = s if name in ("w_in", "w_ffn_in") else _from_exchange(name, tr, s)
            wx, mx, vx = opt_in[name]
            opt_out[name] = _adamw_layer(l, wx, g.reshape(-1, g.shape[-1]), mx, vx, opt_out[name], "adamw_" + name)

    def on_grads(l, gb, after):
        send = [_by_destination(name, gb[name]) for name, _ in BIG]
        if l > 0:
            send = [s.reshape((N_DEV,) + s.shape[2:]) for s in send]
            st = _ici_start(send, [lax.empty(s.shape, BF16) for s in send], "by_device", "rs_ici_start_%d" % l)
        else:
            from_core = _rs_core_swap(send, "rs_core_swap")
            chip_part = _pair_sum(send, from_core, cidx)
            st = _ici_start(chip_part, [lax.empty(p.shape, BF16) for p in chip_part], "by_chip", "rs_ici_start_%d" % l)
        in_flight.append((l, st))
        return st[4]

    loss_local, dx, small_g = _local_step(x[0], mem[0], loss_target[0], {n: w[n] for n in SMALL}, get_layer, on_grads)

    last_started = in_flight[-1][1][4]
    for l, st in in_flight[:-1]:
        finish(l, st, last_started)

    small_all, = _all_gather([_pack_small(small_g, loss_local) + last_started[0:1]], "gather_small")
    packed_g = _sum_slots(small_all, "small_sum")
    small_sum, loss = _unpack_small(packed_g, {n: w[n] for n in SMALL})
    d_, m_, v_ = _adamw(_pack_small({n: w[n] for n in SMALL}), packed_g, _pack_small({n: m[n] for n in SMALL}),
                        _pack_small({n: v[n] for n in SMALL}), "adamw_small")
    updated = d_[0:8]
    for name, _ in BIG:
        updated = updated + opt_out[name][0][1, 0:8, 0:128]
    finish(*in_flight[-1], updated)

    grads, delta, new_m, new_v = {}, {}, {}, {}
    for name, tr in BIG:
        shape = flip_of(name, tr)(w[name]).shape
        delta[name], new_m[name], new_v[name], grads[name] = (flip_of(name, tr)(a.reshape(shape)) for a in opt_out[name])
    like = {n: w[n] for n in SMALL}
    ds, _ = _unpack_small(d_, like)
    ms, _ = _unpack_small(m_, like)
    vs, _ = _unpack_small(v_, like)
    for n in SMALL:
        grads[n], delta[n], new_m[n], new_v[n] = small_sum[n], ds[n], ms[n], vs[n]

    return (loss, dx[None], *[grads[n] for n in WEIGHTS], *[delta[n] for n in WEIGHTS], *[new_m[n] for n in WEIGHTS],
            *[new_v[n] for n in WEIGHTS])
```

```python
import functools

import numpy as np
import jax
import jax.numpy as jnp
from jax import lax
from jax.experimental import pallas as pl
from jax.experimental.pallas import tpu as pltpu

F32 = jnp.float32
BF16 = jnp.bfloat16
MXU = jnp.bfloat16
HI = lax.Precision.HIGHEST

DEPTH = 4
D = 1024
BW = 256
HD = 64
NH = 4
GRID_W = 64
NA_ROWS_WIN = 8
NA_COLS_WIN = 16
N_MEM = 256
FF = 2816
EPS = 1e-6
NEG = -1e30
ROPE_THETA = 10000.0
POOL_HALF_MAX = 8

ADAM_LR, ADAM_B1, ADAM_B2, ADAM_EPS, ADAM_WD, ADAM_STEP = 0.001, 0.9, 0.999, 1e-08, 0.01, 10

N_DEV = 8
VMEM_LIMIT = 56 * 1024 * 1024

RQ, RK, RV, RG, PV, NQ, NK, NV, MQ = range(9)

MESH = pl.DeviceIdType.MESH
ANY = pl.BlockSpec(memory_space=pl.ANY)
SMEM = pl.BlockSpec(memory_space=pltpu.SMEM)


def _cp(**kw):
    return pltpu.CompilerParams(vmem_limit_bytes=VMEM_LIMIT, **kw)


def _tile(n, cap):
    if n <= cap:
        return n
    best = None
    for t in range(128, cap + 1, 128):
        if n % t == 0:
            best = t
    assert best is not None, (n, cap)
    return best


def _sds(shape, dtype):
    return jax.ShapeDtypeStruct(shape, dtype)


def _lane_head(shape):
    return lax.shift_right_logical(lax.broadcasted_iota(jnp.int32, shape, len(shape) - 1), 6)


def _group_mean(z):
    i = lax.shift_right_logical(lax.broadcasted_iota(jnp.int32, (BW, BW), 0), 6)
    j = lax.shift_right_logical(lax.broadcasted_iota(jnp.int32, (BW, BW), 1), 6)
    g = jnp.where(i == j, 1.0 / HD, 0.0).astype(BF16)
    z_hi = z.astype(BF16)
    z_lo = (z - z_hi.astype(F32)).astype(BF16)
    return jnp.dot(z_hi, g, preferred_element_type=F32) + jnp.dot(z_lo, g, preferred_element_type=F32)


def _gnorm(t, g):
    r = lax.rsqrt(_group_mean(t * t) + EPS)
    return t * r * g


def _gnorm_bwd(dy, t, g):
    r = lax.rsqrt(_group_mean(t * t) + EPS)
    th = t * r
    dth = dy * g
    dt = r * (dth - th * _group_mean(dth * th))
    return dt, dy * th


def _swap_halves(t):
    lane = lax.broadcasted_iota(jnp.int32, t.shape, 1)
    return jnp.where((lane & 63) < 32, pltpu.roll(t, BW - 32, 1), pltpu.roll(t, 32, 1))


def _sigmoid(x):
    return 1.0 / (1.0 + jnp.exp(-x))


def _dot(a, b, ta=False, tb=False):
    return lax.dot_general(a.astype(MXU), b.astype(MXU), (((0 if ta else 1,), (1 if tb else 0,)), ((), ())),
                           preferred_element_type=F32)


def _stack_heads(t):
    head = _lane_head(t.shape)
    return jnp.concatenate([jnp.where(head == h, t, jnp.zeros_like(t)) for h in range(NH)], axis=0)


def _unstack_heads(t, rows):
    head = _lane_head((rows, BW))
    out = jnp.zeros((rows, BW), F32)
    for h in range(NH):
        out = out + jnp.where(head == h, t[h * rows:(h + 1) * rows], 0.0)
    return out


def _softmax_rows(s):
    m = jnp.max(s, axis=-1, keepdims=True)
    e = jnp.exp(s - m)
    return e / jnp.sum(e, axis=-1, keepdims=True)


def _acc(ref, val, first):
    @pl.when(first)
    def _():
        ref[...] = val

    @pl.when(jnp.logical_not(first))
    def _():
        ref[...] += val


def _mm(a, b, *, ta=False, tb=False, out_dtype=F32, add=None, dep=None, name):
    m, k = (a.shape[1], a.shape[0]) if ta else a.shape
    n = b.shape[0] if tb else b.shape[1]
    tm, tn = _tile(m, 1408), _tile(n, 768)

    def body(*refs):
        if add is None:
            a_ref, b_ref, o_ref = refs[:2] + refs[-1:]
            r = _dot(a_ref[...], b_ref[...], ta, tb)
        else:
            a_ref, b_ref, c_ref, o_ref = refs[:3] + refs[-1:]
            r = _dot(a_ref[...], b_ref[...], ta, tb) + c_ref[...]
        o_ref[...] = r.astype(out_dtype)

    a_spec = pl.BlockSpec((k, tm), lambda i, j: (0, i)) if ta else pl.BlockSpec((tm, k), lambda i, j: (i, 0))
    b_spec = pl.BlockSpec((tn, k), lambda i, j: (j, 0)) if tb else pl.BlockSpec((k, tn), lambda i, j: (0, j))
    o_spec = pl.BlockSpec((tm, tn), lambda i, j: (i, j))
    ins, args = [a_spec, b_spec], [a, b]
    if add is not None:
        ins.append(o_spec)
        args.append(add)
    if dep is not None:
        ins.append(pl.BlockSpec((8, 128), lambda i, j: (0, 0)))
        args.append(dep)
    return pl.pallas_call(
        body, grid=(m // tm, n // tn), in_specs=ins, out_specs=o_spec, out_shape=_sds((m, n), out_dtype), name=name,
        compiler_params=_cp(dimension_semantics=("parallel", "parallel")))(*args)


def _rmsnorm_fwd(x, g, name):
    t, d = x.shape
    tm = _tile(t, 256)

    def body(x_ref, g_ref, o_ref):
        xv = x_ref[...]
        r = lax.rsqrt(jnp.mean(xv * xv, axis=-1, keepdims=True) + EPS)
        o_ref[...] = (xv * r * g_ref[...]).astype(o_ref.dtype)

    return pl.pallas_call(
        body, grid=(t // tm,), in_specs=[pl.BlockSpec((tm, d), lambda i: (i, 0)), pl.BlockSpec((1, d), lambda i: (0, 0))],
        out_specs=pl.BlockSpec((tm, d), lambda i: (i, 0)), out_shape=_sds((t, d), BF16), name=name, compiler_params=_cp())(x, g)


def _rmsnorm_bwd(dh, x, g, res, name):
    t, d = x.shape
    tm = _tile(t, 256)

    def body(dh_ref, x_ref, g_ref, res_ref, dx_ref, dxb_ref, dg_ref):
        xv = x_ref[...]
        dhv = dh_ref[...]
        r = lax.rsqrt(jnp.mean(xv * xv, axis=-1, keepdims=True) + EPS)
        xh = xv * r
        dxh = dhv * g_ref[...]
        dx = res_ref[...] + r * (dxh - xh * jnp.mean(dxh * xh, axis=-1, keepdims=True))
        dx_ref[...] = dx
        dxb_ref[...] = dx.astype(BF16)
        _acc(dg_ref, jnp.sum(dhv * xh, axis=0, keepdims=True), pl.program_id(0) == 0)

    row = pl.BlockSpec((tm, d), lambda i: (i, 0))
    vec = pl.BlockSpec((1, d), lambda i: (0, 0))
    return pl.pallas_call(
        body, grid=(t // tm,), in_specs=[row, row, vec, row], out_specs=(row, row, vec),
        out_shape=(_sds((t, d), F32), _sds((t, d), BF16), _sds((1, d), F32)), name=name, compiler_params=_cp())(dh, x, g, res)


def _prep_fwd(proj, cos2, sin2, g_naq, g_nak, g_mq):
    t = proj.shape[0]
    tm = 256

    def body(p_ref, cos_ref, sin_ref, gq_ref, gk_ref, gm_ref, rq_ref, rk_ref, rv_ref, nq_ref, nk_ref, nv_ref, mq_ref):
        def col(c):
            return p_ref[:, c * BW:(c + 1) * BW]

        cosv, sinv = cos_ref[...], sin_ref[...]

        def rot(tv):
            return tv * cosv + _swap_halves(tv) * sinv

        rq_ref[...] = (rot(col(RQ)) * (HD ** -0.5)).astype(BF16)
        rk_ref[...] = rot(col(RK)).astype(BF16)
        rv_ref[...] = col(RV).astype(BF16)
        nq_ref[...] = _gnorm(col(NQ), gq_ref[...]).astype(BF16)
        nk_ref[...] = _gnorm(col(NK), gk_ref[...]).astype(BF16)
        nv_ref[...] = col(NV).astype(BF16)
        mq_ref[...] = _gnorm(col(MQ), gm_ref[...]).astype(BF16)

    blk = pl.BlockSpec((tm, BW), lambda i: (i, 0))
    vec = pl.BlockSpec((1, BW), lambda i: (0, 0))
    return pl.pallas_call(
        body, grid=(t // tm,), in_specs=[pl.BlockSpec((tm, 9 * BW), lambda i: (i, 0)), blk, blk, vec, vec, vec],
        out_specs=tuple(blk for _ in range(7)), out_shape=tuple(_sds((t, BW), BF16) for _ in range(7)),
        name="prep_fwd", compiler_params=_cp())(proj, cos2, sin2, g_naq, g_nak, g_mq)


def _prep_bwd(proj, cos2, sin2, g_naq, g_nak, g_mq, d_rq, d_rk, d_rv, d_rg, d_pv, d_nq, d_nk, d_nv, d_mq):
    t = proj.shape[0]
    tm = 256

    def body(p_ref, cos_ref, sin_ref, gq_ref, gk_ref, gm_ref, drq_ref, drk_ref, drv_ref, drg_ref, dpv_ref, dnq_ref, dnk_ref,
             dnv_ref, dmq_ref, o_ref, dgq_ref, dgk_ref, dgm_ref):
        first = pl.program_id(0) == 0

        def col(c):
            return p_ref[:, c * BW:(c + 1) * BW]

        def put(c, v):
            o_ref[:, c * BW:(c + 1) * BW] = v.astype(BF16)

        cosv, sinv = cos_ref[...], sin_ref[...]

        def rot_t(dv):
            return dv * cosv + _swap_halves(dv * sinv)

        put(RQ, rot_t(drq_ref[...] * (HD ** -0.5)))
        put(RK, rot_t(drk_ref[...]))
        put(RV, drv_ref[...])
        put(RG, drg_ref[...])
        put(PV, dpv_ref[...])
        dq, gq = _gnorm_bwd(dnq_ref[...], col(NQ), gq_ref[...])
        put(NQ, dq)
        _acc(dgq_ref, jnp.sum(gq, axis=0, keepdims=True), first)
        dk, gk = _gnorm_bwd(dnk_ref[...], col(NK), gk_ref[...])
        put(NK, dk)
        _acc(dgk_ref, jnp.sum(gk, axis=0, keepdims=True), first)
        put(NV, dnv_ref[...])
        dm, gm = _gnorm_bwd(dmq_ref[...], col(MQ), gm_ref[...])
        put(MQ, dm)
        _acc(dgm_ref, jnp.sum(gm, axis=0, keepdims=True), first)

    blk = pl.BlockSpec((tm, BW), lambda i: (i, 0))
    vec = pl.BlockSpec((1, BW), lambda i: (0, 0))
    wide = pl.BlockSpec((tm, 9 * BW), lambda i: (i, 0))
    return pl.pallas_call(
        body, grid=(t // tm,), in_specs=[wide, blk, blk, vec, vec, vec] + [blk] * 9, out_specs=(wide, vec, vec, vec),
        out_shape=(_sds((t, 9 * BW), BF16), _sds((1, BW), F32), _sds((1, BW), F32), _sds((1, BW), F32)),
        name="prep_bwd", compiler_params=_cp())(proj, cos2, sin2, g_naq, g_nak, g_mq, d_rq, d_rk, d_rv, d_rg, d_pv, d_nq, d_nk,
                                                d_nv, d_mq)


RET_B = 256


def _ret_consts(lgf_ref, lgb_ref):
    bsz = RET_B
    head = _lane_head((1, BW))
    lf, lb = jnp.zeros((1, BW), F32), jnp.zeros((1, BW), F32)
    for h in range(NH):
        lf = lf + jnp.where(head == h, lgf_ref[h], 0.0)
        lb = lb + jnp.where(head == h, lgb_ref[h], 0.0)
    pos = lax.broadcasted_iota(jnp.int32, (bsz, BW), 0).astype(F32)
    up, down = pos + 1.0, (bsz - 1.0) - pos
    c = dict(up=up, down=down, kf=jnp.exp(down * lf), kb=jnp.exp(up * lb), qf=jnp.exp(up * lf), qb=jnp.exp(down * lb),
             cf=jnp.exp(bsz * lf), cb=jnp.exp(bsz * lb))
    diff = (lax.broadcasted_iota(jnp.int32, (NH * bsz, 1), 0) & (bsz - 1)) - lax.broadcasted_iota(jnp.int32, (1, bsz), 1)
    c["causal"] = diff >= 0
    c["dist"] = jnp.abs(diff).astype(F32)
    lgf = jnp.concatenate([jnp.full((bsz, 1), lgf_ref[h], F32) for h in range(NH)], axis=0)
    lgb = jnp.concatenate([jnp.full((bsz, 1), lgb_ref[h], F32) for h in range(NH)], axis=0)
    c["dm"] = jnp.exp(c["dist"] * jnp.where(c["causal"], lgf, lgb))
    c["bd"] = _lane_head((BW, BW)) == lax.shift_right_logical(lax.broadcasted_iota(jnp.int32, (BW, BW), 0), 6)
    return c


def _ret_states(k_ref, v_ref, st_ref, c, nb):
    bsz = RET_B

    def summary(b, decay):
        kb = k_ref[b * bsz:(b + 1) * bsz, :].astype(F32)
        return jnp.where(c["bd"], _dot(kb * decay, v_ref[b * bsz:(b + 1) * bsz, :], ta=True), 0.0)

    f = jnp.zeros((BW, BW), F32)
    for b in range(nb):
        st_ref[b] = f
        if b < nb - 1:
            f = c["cf"] * f + summary(b, c["kf"])
    g = jnp.zeros((BW, BW), F32)
    for b in reversed(range(nb)):
        st_ref[nb + b] = g
        if b > 0:
            g = c["cb"] * g + summary(b, c["kb"])


def _ret_fwd(q, k, v, proj, lgf, lgb, g_ret):
    t = q.shape[0]
    bsz, nb = RET_B, t // RET_B

    def body(lgf_ref, lgb_ref, q_ref, k_ref, v_ref, rg_ref, g_ref, o_ref, ret_ref, st_ref):
        c = _ret_consts(lgf_ref, lgb_ref)
        _ret_states(k_ref, v_ref, st_ref, c, nb)
        for b in range(nb):
            blk = slice(b * bsz, (b + 1) * bsz)
            qb, kb, vb = q_ref[blk, :], k_ref[blk, :], v_ref[blk, :]
            s = _dot(_stack_heads(qb), kb, tb=True)
            o = _unstack_heads(_dot(s * c["dm"], vb), bsz)
            q32 = qb.astype(F32)
            o = o + _dot(q32 * c["qf"], st_ref[b]) + _dot(q32 * c["qb"], st_ref[nb + b])
            o_ref[blk, :] = o
            rg = rg_ref[blk, :]
            ret_ref[blk, :] = (_gnorm(o, g_ref[...]) * (rg * _sigmoid(rg))).astype(BF16)

    whole = pl.BlockSpec((t, BW), lambda i: (0, 0))
    return pl.pallas_call(
        body, grid=(1,),
        in_specs=[SMEM, SMEM, whole, whole, whole, pl.BlockSpec((t, BW), lambda i: (0, RG)), pl.BlockSpec((1, BW), lambda i: (0, 0))],
        out_specs=(whole, whole), out_shape=(_sds((t, BW), F32), _sds((t, BW), BF16)),
        scratch_shapes=[pltpu.VMEM((2 * nb, BW, BW), F32)], name="ret_fwd", compiler_params=_cp())(lgf, lgb, q, k, v, proj, g_ret)


def _ret_post_bwd(dbr, o_ret, proj, g_ret):
    t = o_ret.shape[0]
    tm = 256

    def body(d_ref, o_ref, rg_ref, g_ref, do_ref, drg_ref, dg_ref):
        dret, o, rg, g = d_ref[...], o_ref[...], rg_ref[...], g_ref[...]
        sg = _sigmoid(rg)
        do, dgain = _gnorm_bwd(dret * (rg * sg), o, g)
        do_ref[...] = do.astype(BF16)
        drg_ref[...] = dret * _gnorm(o, g) * (sg * (1.0 + rg * (1.0 - sg)))
        _acc(dg_ref, jnp.sum(dgain, axis=0, keepdims=True), pl.program_id(0) == 0)

    blk = pl.BlockSpec((tm, BW), lambda i: (i, 0))
    vec = pl.BlockSpec((1, BW), lambda i: (0, 0))
    return pl.pallas_call(
        body, grid=(t // tm,), in_specs=[blk, blk, pl.BlockSpec((tm, BW), lambda i: (i, RG)), vec], out_specs=(blk, blk, vec),
        out_shape=(_sds((t, BW), BF16), _sds((t, BW), F32), _sds((1, BW), F32)), name="ret_post_bwd",
        compiler_params=_cp())(dbr, o_ret, proj, g_ret)


def _ret_bwd(do, q, k, v, lgf, lgb):
    t = q.shape[0]
    bsz, nb = RET_B, t // RET_B

    def body(lgf_ref, lgb_ref, d_ref, q_ref, k_ref, v_ref, dq_ref, dk_ref, dv_ref, dlg_ref, st_ref, sd_ref):
        c = _ret_consts(lgf_ref, lgb_ref)
        _ret_states(k_ref, v_ref, st_ref, c, nb)
        lane_f, lane_b = jnp.zeros((1, BW), F32), jnp.zeros((1, BW), F32)
        row_f, row_b = jnp.zeros((NH * bsz, 1), F32), jnp.zeros((NH * bsz, 1), F32)

        def rows(x):
            return jnp.sum(x, axis=0, keepdims=True)

        for b in range(nb):
            blk = slice(b * bsz, (b + 1) * bsz)
            qb, kb, vb, dob = q_ref[blk, :], k_ref[blk, :], v_ref[blk, :], d_ref[blk, :]
            q32 = qb.astype(F32)
            qs, dos = _stack_heads(qb), _stack_heads(dob)
            s = _dot(qs, kb, tb=True)
            da = _dot(dos, vb, tb=True)
            dv_ref[blk, :] = _dot(s * c["dm"], dos, ta=True)
            ds = da * c["dm"]
            w = ds * s * c["dist"]
            row_f = row_f + jnp.sum(jnp.where(c["causal"], w, 0.0), axis=1, keepdims=True)
            row_b = row_b + jnp.sum(jnp.where(c["causal"], 0.0, w), axis=1, keepdims=True)
            dsb = ds.astype(MXU)
            dk_ref[blk, :] = _dot(dsb, qs, ta=True)
            dq_f = _dot(dob, st_ref[b], tb=True) * c["qf"]
            dq_b = _dot(dob, st_ref[nb + b], tb=True) * c["qb"]
            lane_f = lane_f + rows(c["up"] * dq_f * q32)
            lane_b = lane_b + rows(c["down"] * dq_b * q32)
            dq_ref[blk, :] = _unstack_heads(_dot(dsb, kb), bsz) + dq_f + dq_b
            sd_ref[b] = jnp.where(c["bd"], _dot(q32 * c["qf"], dob, ta=True), 0.0)
            sd_ref[nb + b] = jnp.where(c["bd"], _dot(q32 * c["qb"], dob, ta=True), 0.0)

        def through_state(b, grad, decay, weight, lane):
            blk = slice(b * bsz, (b + 1) * bsz)
            k32 = k_ref[blk, :].astype(F32)
            dk = _dot(v_ref[blk, :], grad, tb=True) * decay
            dk_ref[blk, :] += dk
            dv_ref[blk, :] += _dot(k32 * decay, grad)
            return lane + rows(weight * dk * k32)

        phi = jnp.zeros((BW, BW), F32)
        for b in reversed(range(nb)):
            if b < nb - 1:
                lane_f = through_state(b, phi, c["kf"], c["down"], lane_f)
                lane_f = lane_f + bsz * rows(c["cf"] * st_ref[b] * phi)
            phi = sd_ref[b] + c["cf"] * phi
        gam = jnp.zeros((BW, BW), F32)
        for b in range(nb):
            if b > 0:
                lane_b = through_state(b, gam, c["kb"], c["up"], lane_b)
                lane_b = lane_b + bsz * rows(c["cb"] * st_ref[nb + b] * gam)
            gam = sd_ref[nb + b] + c["cb"] * gam

        head = _lane_head((1, BW))
        for h in range(NH):
            tot_f = jnp.sum(row_f[h * bsz:(h + 1) * bsz, :]) + jnp.sum(jnp.where(head == h, lane_f, 0.0))
            tot_b = jnp.sum(row_b[h * bsz:(h + 1) * bsz, :]) + jnp.sum(jnp.where(head == h, lane_b, 0.0))
            dlg_ref[h:h + 1, :] = jnp.full((1, 128), tot_f, F32)
            dlg_ref[NH + h:NH + h + 1, :] = jnp.full((1, 128), tot_b, F32)

    whole = pl.BlockSpec((t, BW), lambda i: (0, 0))
    return pl.pallas_call(
        body, grid=(1,), in_specs=[SMEM, SMEM, whole, whole, whole, whole],
        out_specs=(whole, whole, whole, pl.BlockSpec((2 * NH, 128), lambda i: (0, 0))),
        out_shape=(_sds((t, BW), F32), _sds((t, BW), F32), _sds((t, BW), F32), _sds((2 * NH, 128), F32)),
        scratch_shapes=[pltpu.VMEM((2 * nb, BW, BW), F32), pltpu.VMEM((2 * nb, BW, BW), F32)], name="ret_bwd",
        compiler_params=_cp())(lgf, lgb, do, q, k, v)


def _pool_windows(t):
    row = lax.broadcasted_iota(jnp.int32, (t, BW), 0)
    half = lax.shift_left(jnp.ones((t, BW), jnp.int32), _lane_head((t, BW)))
    cnt = (jnp.minimum(row + half, t) - jnp.maximum(row - half, 0)).astype(F32)
    return row, half, cnt


def _pool_window_sum(v, row, half, t, transpose):
    out = jnp.zeros_like(v)
    for j in range(-POOL_HALF_MAX, POOL_HALF_MAX):
        src = row - j if transpose else row + j
        ok = (src >= 0) & (src < t) & (j >= -half) & (j < half)
        out = out + jnp.where(ok, pltpu.roll(v, (j if transpose else -j) % t, 0), 0.0)
    return out


def _pool_fwd(proj, wbd, scale):
    t = proj.shape[0]

    def body(v_ref, w_ref, s_ref, o_ref):
        v = v_ref[...]
        row, half, cnt = _pool_windows(t)
        pooled = _pool_window_sum(v, row, half, t, False) / cnt - v
        o_ref[...] = (_dot(pooled, w_ref[...]) * s_ref[...]).astype(BF16)

    return pl.pallas_call(
        body, grid=(1,),
        in_specs=[pl.BlockSpec((t, BW), lambda i: (0, PV)), pl.BlockSpec((BW, BW), lambda i: (0, 0)), pl.BlockSpec((1, BW), lambda i: (0, 0))],
        out_specs=pl.BlockSpec((t, BW), lambda i: (0, 0)), out_shape=_sds((t, BW), BF16), name="pool_fwd",
        compiler_params=_cp())(proj, wbd, scale)


def _pool_bwd(dbr, proj, wbd, scale):
    t = proj.shape[0]

    def body(d_ref, v_ref, w_ref, s_ref, dv_ref, dw_ref, ds_ref):
        v, dout = v_ref[...], d_ref[...]
        row, half, cnt = _pool_windows(t)
        pooled = _pool_window_sum(v, row, half, t, False) / cnt - v
        mixed = _dot(pooled, w_ref[...])
        ds_ref[...] = jnp.sum(dout * mixed, axis=0, keepdims=True)
        dmixed = dout * s_ref[...]
        dw_ref[...] = _dot(pooled, dmixed, ta=True)
        dpooled = _dot(dmixed, w_ref[...], tb=True)
        dv_ref[...] = _pool_window_sum(dpooled / cnt, row, half, t, True) - dpooled

    return pl.pallas_call(
        body, grid=(1,),
        in_specs=[pl.BlockSpec((t, BW), lambda i: (0, 1)), pl.BlockSpec((t, BW), lambda i: (0, PV)),
                  pl.BlockSpec((BW, BW), lambda i: (0, 0)), pl.BlockSpec((1, BW), lambda i: (0, 0))],
        out_specs=(pl.BlockSpec((t, BW), lambda i: (0, 0)), pl.BlockSpec((BW, BW), lambda i: (0, 0)), pl.BlockSpec((1, BW), lambda i: (0, 0))),
        out_shape=(_sds((t, BW), F32), _sds((BW, BW), F32), _sds((1, BW), F32)), name="pool_bwd",
        compiler_params=_cp())(dbr, proj, wbd, scale)


NA_KEYS = NA_ROWS_WIN * GRID_W
NA_PAIRS = 2 * NA_ROWS_WIN - 2


def _na_window(r, n_rows):
    rs = jnp.clip(r - NA_ROWS_WIN // 2, 0, n_rows - NA_ROWS_WIN)
    return pl.multiple_of(rs * GRID_W, GRID_W), rs - r + (NA_ROWS_WIN - 1)


def _na_bias(b_ref, a0):
    return jnp.concatenate([b_ref[a0 + 2 * j] for j in range(NA_ROWS_WIN // 2)], axis=1)


NA_STEP_ROWS = 8


def _na_fwd(q, k, v, ball):
    t = q.shape[0]
    n_rows = t // GRID_W
    rows = NA_STEP_ROWS

    def body(q_ref, k_ref, v_ref, b_ref, o_ref):
        for rr in range(rows):
            start, a0 = _na_window(pl.program_id(0) * rows + rr, n_rows)
            own = slice(rr * GRID_W, (rr + 1) * GRID_W)
            qs = _stack_heads(q_ref[own, :])
            s = _dot(qs, k_ref[pl.ds(start, NA_KEYS), :], tb=True) * (HD ** -0.5) + _na_bias(b_ref, a0)
            p = _softmax_rows(s)
            o_ref[own, :] = _unstack_heads(_dot(p, v_ref[pl.ds(start, NA_KEYS), :]), GRID_W).astype(BF16)

    blk = pl.BlockSpec((rows * GRID_W, BW), lambda r: (r, 0))
    whole = pl.BlockSpec((t, BW), lambda r: (0, 0))
    return pl.pallas_call(
        body, grid=(n_rows // rows,), in_specs=[blk, whole, whole, pl.BlockSpec(ball.shape, lambda r: (0, 0, 0))],
        out_specs=blk, out_shape=_sds((t, BW), BF16), name="na_fwd", compiler_params=_cp())(q, k, v, ball)


def _na_bwd(dbr, q, k, v, ball):
    t = q.shape[0]
    n_rows = t // GRID_W

    rows = NA_STEP_ROWS

    def body(d_ref, q_ref, k_ref, v_ref, b_ref, dq_ref, dk_ref, dv_ref, db_ref):
        @pl.when(pl.program_id(0) == 0)
        def _():
            dk_ref[...] = jnp.zeros_like(dk_ref)
            dv_ref[...] = jnp.zeros_like(dv_ref)
            db_ref[...] = jnp.zeros_like(db_ref)

        for rr in range(rows):
            start, a0 = _na_window(pl.program_id(0) * rows + rr, n_rows)
            keys = pl.ds(start, NA_KEYS)
            own = slice(rr * GRID_W, (rr + 1) * GRID_W)
            qs = _stack_heads(q_ref[own, :])
            kb, vb = k_ref[keys, :], v_ref[keys, :]
            p = _softmax_rows(_dot(qs, kb, tb=True) * (HD ** -0.5) + _na_bias(b_ref, a0))
            dos = _stack_heads(d_ref[own, :]).astype(MXU)
            dp = _dot(dos, vb, tb=True)
            dv_ref[keys, :] += _dot(p, dos, ta=True)
            ds = p * (dp - jnp.sum(dp * p, axis=-1, keepdims=True))
            for j in range(NA_ROWS_WIN // 2):
                db_ref[a0 + 2 * j] += ds[:, 2 * j * GRID_W:(2 * j + 2) * GRID_W]
            dsb = (ds * (HD ** -0.5)).astype(MXU)
            dq_ref[own, :] = _unstack_heads(_dot(dsb, kb), GRID_W)
            dk_ref[keys, :] += _dot(dsb, qs, ta=True)

    blk = pl.BlockSpec((rows * GRID_W, BW), lambda r: (r, 0))
    whole = pl.BlockSpec((t, BW), lambda r: (0, 0))
    tab = pl.BlockSpec(ball.shape, lambda r: (0, 0, 0))
    return pl.pallas_call(
        body, grid=(n_rows // rows,), in_specs=[pl.BlockSpec((rows * GRID_W, BW), lambda r: (r, 2)), blk, whole, whole, tab],
        out_specs=(blk, whole, whole, tab),
        out_shape=(_sds((t, BW), F32), _sds((t, BW), F32), _sds((t, BW), F32), _sds(ball.shape, F32)), name="na_bwd",
        compiler_params=_cp())(dbr, q, k, v, ball)


def _rpb_expand(rpb_pad, onehot):
    def body(r_ref, e_ref, o_ref):
        o_ref[...] = jnp.dot(r_ref[...], e_ref[...], precision=HI, preferred_element_type=F32)

    return pl.pallas_call(body, out_shape=_sds((rpb_pad.shape[0], GRID_W * GRID_W), F32), name="rpb_expand",
                          compiler_params=_cp())(rpb_pad, onehot)


def _rpb_reduce(dtab, onehot):
    def body(d_ref, e_ref, o_ref):
        o_ref[...] = lax.dot_general(d_ref[...], e_ref[...], (((1,), (1,)), ((), ())), precision=HI, preferred_element_type=F32)

    return pl.pallas_call(body, out_shape=_sds((dtab.shape[0], 128), F32), name="rpb_reduce", compiler_params=_cp())(dtab, onehot)


MEM_TQ = 256


def _mem_fwd(q, mk, mv):
    t = q.shape[0]
    tq = MEM_TQ

    def body(q_ref, k_ref, v_ref, o_ref):
        p = _softmax_rows(_dot(_stack_heads(q_ref[...]), k_ref[...], tb=True) * (HD ** -0.5))
        o_ref[...] = _unstack_heads(_dot(p, v_ref[...]), tq).astype(BF16)

    blk = pl.BlockSpec((tq, BW), lambda i: (i, 0))
    kv = pl.BlockSpec((N_MEM, BW), lambda i: (0, 0))
    return pl.pallas_call(body, grid=(t // tq,), in_specs=[blk, kv, kv], out_specs=blk, out_shape=_sds((t, BW), BF16),
                          name="mem_fwd", compiler_params=_cp())(q, mk, mv)


def _mem_bwd(dbr, q, mk, mv):
    t = q.shape[0]
    tq = MEM_TQ

    def body(d_ref, q_ref, k_ref, v_ref, dq_ref, dk_ref, dv_ref):
        first = pl.program_id(0) == 0
        qs = _stack_heads(q_ref[...])
        dos = _stack_heads(d_ref[...]).astype(MXU)
        p = _softmax_rows(_dot(qs, k_ref[...], tb=True) * (HD ** -0.5))
        dp = _dot(dos, v_ref[...], tb=True)
        _acc(dv_ref, _dot(p, dos, ta=True), first)
        dsb = (p * (dp - jnp.sum(dp * p, axis=-1, keepdims=True)) * (HD ** -0.5)).astype(MXU)
        dq_ref[...] = _unstack_heads(_dot(dsb, k_ref[...]), tq)
        _acc(dk_ref, _dot(dsb, qs, ta=True), first)

    blk = pl.BlockSpec((tq, BW), lambda i: (i, 0))
    kv = pl.BlockSpec((N_MEM, BW), lambda i: (0, 0))
    return pl.pallas_call(
        body, grid=(t // tq,), in_specs=[pl.BlockSpec((tq, BW), lambda i: (i, 3)), blk, kv, kv], out_specs=(blk, kv, kv),
        out_shape=(_sds((t, BW), F32), _sds((N_MEM, BW), F32), _sds((N_MEM, BW), F32)), name="mem_bwd",
        compiler_params=_cp())(dbr, q, mk, mv)


def _memkv_prep(kv, g_mk):
    def body(kv_ref, g_ref, k_ref, v_ref):
        k_ref[...] = _gnorm(kv_ref[:, 0:BW], g_ref[...]).astype(BF16)
        v_ref[...] = kv_ref[:, BW:2 * BW].astype(BF16)

    return pl.pallas_call(body, out_shape=(_sds((N_MEM, BW), BF16), _sds((N_MEM, BW), BF16)), name="memkv_prep",
                          compiler_params=_cp())(kv, g_mk)


def _memkv_bwd(kv, dk, dv, g_mk):
    def body(kv_ref, dk_ref, dv_ref, g_ref, o_ref, dg_ref):
        dkk, gain = _gnorm_bwd(dk_ref[...], kv_ref[:, 0:BW], g_ref[...])
        o_ref[:, 0:BW] = dkk.astype(BF16)
        o_ref[:, BW:2 * BW] = dv_ref[...].astype(BF16)
        dg_ref[...] = jnp.sum(gain, axis=0, keepdims=True)

    return pl.pallas_call(body, out_shape=(_sds((N_MEM, 2 * BW), BF16), _sds((1, BW), F32)), name="memkv_bwd",
                          compiler_params=_cp())(kv, dk, dv, g_mk)


MERGE_TM = 256


def _merge_fwd(brs, wbt, gp):
    t = gp.shape[0]
    tm = MERGE_TM

    def body(b0, b1, b2, b3, wb_ref, gp_ref, o_ref):
        out = jnp.zeros((tm, D), F32)
        for n, b_ref in enumerate((b0, b1, b2, b3)):
            up = _dot(b_ref[...], wb_ref[n], tb=True)
            out = out + _sigmoid(gp_ref[:, n * D:(n + 1) * D].astype(F32)) * up
        o_ref[...] = out.astype(BF16)

    blk = pl.BlockSpec((tm, BW), lambda i: (i, 0))
    return pl.pallas_call(
        body, grid=(t // tm,),
        in_specs=[blk, blk, blk, blk, pl.BlockSpec((NH, D, BW), lambda i: (0, 0, 0)), pl.BlockSpec((tm, NH * D), lambda i: (i, 0))],
        out_specs=pl.BlockSpec((tm, D), lambda i: (i, 0)), out_shape=_sds((t, D), BF16), name="merge_fwd",
        compiler_params=_cp())(*brs, wbt, gp)


def _merge_bwd(dmerged, brs, wbt, gp):
    t = gp.shape[0]
    tm = MERGE_TM

    def body(d_ref, b0, b1, b2, b3, wb_ref, gp_ref, dgp_ref, dup_ref):
        dm = d_ref[...]
        for n, b_ref in enumerate((b0, b1, b2, b3)):
            up = _dot(b_ref[...], wb_ref[n], tb=True)
            g = _sigmoid(gp_ref[:, n * D:(n + 1) * D].astype(F32))
            dgp_ref[:, n * D:(n + 1) * D] = (dm * up * (g * (1.0 - g))).astype(BF16)
            dup_ref[:, n * D:(n + 1) * D] = (dm * g).astype(BF16)

    row = pl.BlockSpec((tm, D), lambda i: (i, 0))
    blk = pl.BlockSpec((tm, BW), lambda i: (i, 0))
    wide = pl.BlockSpec((tm, NH * D), lambda i: (i, 0))
    return pl.pallas_call(
        body, grid=(t // tm,), in_specs=[row, blk, blk, blk, blk, pl.BlockSpec((NH, D, BW), lambda i: (0, 0, 0)), wide],
        out_specs=(wide, wide), out_shape=(_sds((t, NH * D), BF16), _sds((t, NH * D), BF16)), name="merge_bwd",
        compiler_params=_cp())(dmerged, *brs, wbt, gp)


def _dbranch(dup, wbt):
    t = dup.shape[0]
    tm = 1024

    def body(d_ref, w_ref, o_ref):
        o_ref[...] = _dot(d_ref[...], w_ref[...])

    return pl.pallas_call(
        body, grid=(t // tm, NH), in_specs=[pl.BlockSpec((tm, D), lambda i, n: (i, n)), pl.BlockSpec((None, D, BW), lambda i, n: (n, 0, 0))],
        out_specs=pl.BlockSpec((tm, BW), lambda i, n: (i, n)), out_shape=_sds((t, NH * BW), F32), name="dbranch",
        compiler_params=_cp())(dup, wbt)


def _dwbranch(brs, dup):
    t = dup.shape[0]

    def body(b0, b1, b2, b3, d_ref, o_ref):
        for n, b_ref in enumerate((b0, b1, b2, b3)):
            o_ref[n] = _dot(d_ref[:, n * D:(n + 1) * D], b_ref[...], ta=True).astype(BF16)

    return pl.pallas_call(body, out_shape=_sds((NH, D, BW), BF16), name="dwbranch", compiler_params=_cp())(*brs, dup)


def _swiglu_fwd(ag):
    t = ag.shape[0]
    tm = 256

    def body(ag_ref, o_ref):
        a, g = ag_ref[:, 0:FF].astype(F32), ag_ref[:, FF:2 * FF].astype(F32)
        o_ref[...] = (a * _sigmoid(a) * g).astype(BF16)

    return pl.pallas_call(body, grid=(t // tm,), in_specs=[pl.BlockSpec((tm, 2 * FF), lambda i: (i, 0))],
                          out_specs=pl.BlockSpec((tm, FF), lambda i: (i, 0)), out_shape=_sds((t, FF), BF16), name="swiglu_fwd",
                          compiler_params=_cp())(ag)


def _swiglu_bwd(ag, dy):
    t = ag.shape[0]
    tm = 256

    def body(ag_ref, dy_ref, o_ref):
        a, g, d = ag_ref[:, 0:FF].astype(F32), ag_ref[:, FF:2 * FF].astype(F32), dy_ref[...].astype(F32)
        s = _sigmoid(a)
        o_ref[:, 0:FF] = (d * g * (s * (1.0 + a * (1.0 - s)))).astype(BF16)
        o_ref[:, FF:2 * FF] = (d * (a * s)).astype(BF16)

    return pl.pallas_call(
        body, grid=(t // tm,), in_specs=[pl.BlockSpec((tm, 2 * FF), lambda i: (i, 0)), pl.BlockSpec((tm, FF), lambda i: (i, 0))],
        out_specs=pl.BlockSpec((tm, 2 * FF), lambda i: (i, 0)), out_shape=_sds((t, 2 * FF), BF16), name="swiglu_bwd",
        compiler_params=_cp())(ag, dy)


def _loss_head(y, target):
    t, d = y.shape
    tm = 256

    def body(y_ref, t_ref, dy_ref, dyb_ref, l_ref):
        e = y_ref[...] - t_ref[...]
        dy_ref[...] = e * (1.0 / d)
        dyb_ref[...] = (e * (1.0 / d)).astype(BF16)
        _acc(l_ref, jnp.full((8, 128), 0.5 * jnp.sum(jnp.sum(e * e, axis=-1, keepdims=True) * (1.0 / d)), F32), pl.program_id(0) == 0)

    row = pl.BlockSpec((tm, d), lambda i: (i, 0))
    return pl.pallas_call(body, grid=(t // tm,), in_specs=[row, row], out_specs=(row, row, pl.BlockSpec((8, 128), lambda i: (0, 0))),
                          out_shape=(_sds((t, d), F32), _sds((t, d), BF16), _sds((8, 128), F32)), name="loss_head",
                          compiler_params=_cp())(y, target)


def _sum_slots(x, name):
    k, r, c = x.shape
    tr = _tile(r, 512) if r % 128 == 0 else r

    def body(x_ref, o_ref):
        acc = x_ref[0].astype(F32)
        for s in range(1, k):
            acc = acc + x_ref[s].astype(F32)
        o_ref[...] = acc

    return pl.pallas_call(body, grid=(r // tr,), in_specs=[pl.BlockSpec((k, tr, c), lambda i: (0, i, 0))],
                          out_specs=pl.BlockSpec((tr, c), lambda i: (i, 0)), out_shape=_sds((r, c), F32), name=name,
                          compiler_params=_cp())(x)


def _pair_sum(bufs, recvs, cidx):
    n = len(bufs)

    def body(c_ref, *refs):
        for i in range(n):
            refs[2 * n + i][...] = (refs[i][...].astype(F32) + refs[n + i][...].astype(F32)).astype(BF16)

    return pl.pallas_call(
        body,
        grid_spec=pltpu.PrefetchScalarGridSpec(
            num_scalar_prefetch=1, grid=(4,),
            in_specs=[pl.BlockSpec((None, None) + b.shape[2:], lambda s, cref: (s, cref[0], 0, 0)) for b in bufs]
            + [pl.BlockSpec((None,) + r.shape[1:], lambda s, cref: (s, 0, 0)) for r in recvs],
            out_specs=tuple(pl.BlockSpec((None,) + r.shape[1:], lambda s, cref: (s, 0, 0)) for r in recvs)),
        out_shape=tuple(_sds(r.shape, BF16) for r in recvs), name="rs_pair_sum", compiler_params=_cp())(cidx, *bufs, *recvs)


def _adamw_update(w, gv, m, v):
    mn = ADAM_B1 * m + (1.0 - ADAM_B1) * gv
    vn = ADAM_B2 * v + (1.0 - ADAM_B2) * (gv * gv)
    m_hat = mn / (1.0 - ADAM_B1 ** ADAM_STEP)
    v_hat = vn / (1.0 - ADAM_B2 ** ADAM_STEP)
    return -ADAM_LR * (m_hat / (jnp.sqrt(v_hat) + ADAM_EPS) + ADAM_WD * w), mn, vn


def _adamw(w, g, m, v, name):
    r, c = w.shape

    def body(w_ref, g_ref, m_ref, v_ref, d_ref, nm_ref, nv_ref):
        d_ref[...], nm_ref[...], nv_ref[...] = _adamw_update(w_ref[...], g_ref[...], m_ref[...], v_ref[...])

    blk = pl.BlockSpec((r, c), lambda i: (0, 0))
    return pl.pallas_call(body, grid=(1,), in_specs=[blk] * 4, out_specs=(blk,) * 3,
                          out_shape=tuple(_sds((r, c), F32) for _ in range(3)), name=name, compiler_params=_cp())(w, g, m, v)


def _adamw_layer(layer, w, g, m, v, outs, name):
    _, r, c = w.shape
    tr = max(d for d in range(8, r + 1, 8) if r % d == 0 and d * c * 4 <= 2 ** 20)

    def body(w_ref, m_ref, v_ref, g_ref, *refs):
        d_ref, nm_ref, nv_ref, go_ref = refs[4:]
        gv = g_ref[...]
        d_ref[...], nm_ref[...], nv_ref[...] = _adamw_update(w_ref[...], gv, m_ref[...], v_ref[...])
        go_ref[...] = gv

    blk = pl.BlockSpec((None, tr, c), lambda i: (layer, i, 0))
    return pl.pallas_call(
        body, grid=(r // tr,), in_specs=[blk] * 3 + [pl.BlockSpec((tr, c), lambda i: (i, 0))] + [ANY] * 4, out_specs=(blk,) * 4,
        out_shape=tuple(_sds(w.shape, F32) for _ in range(4)), input_output_aliases={4 + j: j for j in range(4)}, name=name,
        compiler_params=_cp())(w, m, v, g, *outs)


def _all_gather(shards, name):
    n = len(shards)

    def body(*refs):
        x_refs, out_refs = refs[:n], refs[n:2 * n]
        send_sems, recv_sems, local_sems = refs[2 * n:]
        x, y, cc = lax.axis_index("x"), lax.axis_index("y"), lax.axis_index("c")
        me, sibling = (x, y, cc), (x, y, 1 - cc)
        chips = [(1 - x, y), (x, 1 - y), (1 - x, 1 - y)]

        def copy(i, k, block, to, own=False):
            px, py, pc = block
            slot = out_refs[i].at[4 * px + 2 * py + pc]
            return pltpu.make_async_remote_copy(
                src_ref=x_refs[i] if own else slot, dst_ref=slot, send_sem=send_sems.at[7 * i + k],
                recv_sem=recv_sems.at[7 * i + k], device_id=to, device_id_type=MESH)

        mine = [pltpu.make_async_copy(x_refs[i], out_refs[i].at[4 * x + 2 * y + cc], local_sems.at[i]) for i in range(n)]
        for cp in mine:
            cp.start()
        first = []
        for j, chip in enumerate(chips):
            first += [copy(i, 1 + j, me, (*chip, cc), own=True) for i in range(n)]
        first += [copy(i, 0, me, sibling, own=True) for i in range(n)]
        for cp in first:
            cp.start()
        passed = []
        for j, chip in enumerate(chips):
            for i in range(n):
                copy(i, 1 + j, (*chip, cc), me).wait_recv()
                cp = copy(i, 4 + j, (*chip, cc), sibling)
                cp.start()
                passed.append(cp)
        for i in range(n):
            copy(i, 0, sibling, me).wait_recv()
        for j, chip in enumerate(chips):
            for i in range(n):
                copy(i, 4 + j, (*chip, 1 - cc), me).wait_recv()
        for cp in first + passed:
            cp.wait_send()
        for cp in mine:
            cp.wait()

    return pl.pallas_call(
        body, out_shape=tuple(_sds((N_DEV,) + s.shape, s.dtype) for s in shards), in_specs=[ANY] * n, out_specs=(ANY,) * n,
        scratch_shapes=[pltpu.SemaphoreType.DMA((7 * n,)), pltpu.SemaphoreType.DMA((7 * n,)), pltpu.SemaphoreType.DMA((n,))],
        name=name)(*shards)


def _rs_core_swap(bufs, name):
    n = len(bufs)

    def body(*refs):
        b_refs, recv_refs = refs[:n], refs[n:2 * n]
        send_sems, recv_sems = refs[2 * n:]
        x, y, cc = lax.axis_index("x"), lax.axis_index("y"), lax.axis_index("c")
        copies = [pltpu.make_async_remote_copy(
            src_ref=b_refs[i].at[s, 1 - cc], dst_ref=recv_refs[i].at[s], send_sem=send_sems.at[4 * i + s],
            recv_sem=recv_sems.at[4 * i + s], device_id=(x, y, 1 - cc), device_id_type=MESH) for i in range(n) for s in range(4)]
        for cp in copies:
            cp.start()
        for cp in copies:
            cp.wait()

    return pl.pallas_call(
        body, out_shape=tuple(_sds((4,) + b.shape[2:], b.dtype) for b in bufs), in_specs=[ANY] * n, out_specs=(ANY,) * n,
        scratch_shapes=[pltpu.SemaphoreType.DMA((4 * n,)), pltpu.SemaphoreType.DMA((4 * n,))], name=name)(*bufs)


HBM = pl.BlockSpec(memory_space=pltpu.HBM)
SEMS = pl.BlockSpec(memory_space=pltpu.SEMAPHORE)
EFFECT = pltpu.SideEffectType.DATAFLOW_SIDE_EFFECTING


def _hbm(a):
    return pltpu.HBM(a.shape, a.dtype)


def _other_chips(x, y):
    return [(1 - x, y), (x, 1 - y), (1 - x, 1 - y)]


def _ici_start(srcs, lands, mode, name):
    n = len(srcs)

    def body(*refs):
        s_refs, land_refs = refs[:n], refs[n:2 * n]
        send_sems, recv_sems = refs[2 * n], refs[2 * n + 1]
        token = refs[-1]
        x, y, cc = lax.axis_index("x"), lax.axis_index("y"), lax.axis_index("c")
        mine = 2 * x + y if mode == "by_chip" else 4 * x + 2 * y + cc
        peers = [(px, py, cc) for px, py in _other_chips(x, y)]
        if mode == "by_device":
            peers = [(x, y, 1 - cc)] + peers + [(px, py, 1 - cc) for px, py in _other_chips(x, y)]
        for px, py, pc in peers:
            for i in range(n):
                src = s_refs[i]
                if mode == "by_chip":
                    src = src.at[2 * px + py]
                elif mode == "by_device":
                    src = src.at[4 * px + 2 * py + pc]
                pltpu.make_async_remote_copy(
                    src_ref=src, dst_ref=land_refs[i].at[mine], send_sem=send_sems.at[i], recv_sem=recv_sems.at[i],
                    device_id=(px, py, pc), device_id_type=MESH).start()
        token[...] = jnp.zeros_like(token)

    out = pl.pallas_call(
        body, name=name,
        out_shape=(pltpu.SemaphoreType.DMA((n,)), pltpu.SemaphoreType.DMA((n,)), *[_hbm(s) for s in srcs], *[_hbm(l) for l in lands],
                   _sds((8, 128), F32)),
        in_specs=[HBM] * (2 * n), out_specs=(SEMS, SEMS, *[HBM] * (2 * n), pl.BlockSpec(memory_space=pltpu.VMEM)),
        input_output_aliases={i: 2 + i for i in range(2 * n)}, compiler_params=pltpu.CompilerParams(has_side_effects=EFFECT),
    )(*[pltpu.with_memory_space_constraint(s, pltpu.HBM) for s in srcs],
      *[pltpu.with_memory_space_constraint(l, pltpu.HBM) for l in lands])
    return out[0], out[1], out[2:2 + n], out[2 + n:2 + 2 * n], out[-1], 7 if mode == "by_device" else 3


def _ici_wait(started, after, name, only=None):
    send_sems, recv_sems, srcs, lands, _, copies = started
    only = list(range(len(srcs))) if only is None else only
    srcs, lands = [srcs[i] for i in only], [lands[i] for i in only]
    n = len(srcs)

    def body(*refs):
        land_refs = refs[n:2 * n]
        send_sems, recv_sems = refs[2 * n], refs[2 * n + 1]
        x, y, cc = lax.axis_index("x"), lax.axis_index("y"), lax.axis_index("c")
        for i in range(n):
            three = land_refs[i].at[pl.ds(0, copies)]
            cp = pltpu.make_async_remote_copy(src_ref=three, dst_ref=three, send_sem=send_sems.at[only[i]],
                                              recv_sem=recv_sems.at[only[i]],
                                              device_id=(x, y, cc), device_id_type=MESH)
            cp.wait_send()
            cp.wait_recv()

    return pl.pallas_call(
        body, name=name, out_shape=tuple(_hbm(l) for l in lands), in_specs=[HBM] * (2 * n) + [SEMS, SEMS, ANY],
        out_specs=tuple([HBM] * n), input_output_aliases={n + i: i for i in range(n)},
        compiler_params=pltpu.CompilerParams(has_side_effects=EFFECT))(*srcs, *lands, send_sems, recv_sems, after)


def _gather_d2d(blocks, lands, name):
    n = len(blocks)

    def body(*refs):
        x_refs, land_refs = refs[:n], refs[2 * n:3 * n]
        send_sems, recv_sems, in_sems, out_sems = refs[3 * n:3 * n + 4]
        stage = refs[3 * n + 4:]
        x, y, cc = lax.axis_index("x"), lax.axis_index("y"), lax.axis_index("c")
        sibling = (x, y, 1 - cc)
        staged = [pltpu.make_async_copy(x_refs[i], stage[i], in_sems.at[i]) for i in range(n)]
        for cp in staged:
            cp.start()
        copies = []
        for i in range(n):
            slot = land_refs[i].at[4 * x + 2 * y + cc]
            copies.append(pltpu.make_async_remote_copy(src_ref=x_refs[i], dst_ref=slot, send_sem=send_sems.at[4 * i],
                                                       recv_sem=recv_sems.at[4 * i], device_id=sibling, device_id_type=MESH))
            for j, (px, py) in enumerate(_other_chips(x, y)):
                slot = land_refs[i].at[4 * px + 2 * py + cc]
                copies.append(pltpu.make_async_remote_copy(src_ref=slot, dst_ref=slot, send_sem=send_sems.at[4 * i + 1 + j],
                                                           recv_sem=recv_sems.at[4 * i + 1 + j], device_id=sibling, device_id_type=MESH))
        for cp in copies:
            cp.start()
        mine = []
        for i in range(n):
            staged[i].wait()
            mine.append(pltpu.make_async_copy(stage[i], land_refs[i].at[4 * x + 2 * y + cc], out_sems.at[i]))
            mine[i].start()
        for i in range(n):
            slot = land_refs[i].at[4 * x + 2 * y + (1 - cc)]
            pltpu.make_async_remote_copy(src_ref=slot, dst_ref=slot, send_sem=send_sems.at[4 * i], recv_sem=recv_sems.at[4 * i],
                                         device_id=sibling, device_id_type=MESH).wait_recv()
            for j, (px, py) in enumerate(_other_chips(x, y)):
                slot = land_refs[i].at[4 * px + 2 * py + (1 - cc)]
                pltpu.make_async_remote_copy(src_ref=slot, dst_ref=slot, send_sem=send_sems.at[4 * i + 1 + j],
                                             recv_sem=recv_sems.at[4 * i + 1 + j], device_id=sibling, device_id_type=MESH).wait_recv()
        for cp in copies:
            cp.wait_send()
        for cp in mine:
            cp.wait()

    return pl.pallas_call(
        body, out_shape=tuple(_sds(l.shape, l.dtype) for l in lands), in_specs=[ANY] * (2 * n), out_specs=(ANY,) * n,
        input_output_aliases={n + i: i for i in range(n)},
        scratch_shapes=[pltpu.SemaphoreType.DMA((4 * n,)), pltpu.SemaphoreType.DMA((4 * n,)), pltpu.SemaphoreType.DMA((n,)),
                        pltpu.SemaphoreType.DMA((n,))] + [pltpu.VMEM(b.shape, b.dtype) for b in blocks],
        name=name, compiler_params=_cp())(*blocks, *lands)


def _sum_own(parts, recvs, mine, name):
    n = len(parts)

    def body(c_ref, *refs):
        s = pl.program_id(0)
        for i in range(n):
            val = jnp.where(c_ref[0] == s, refs[i][...], refs[n + i][...]).astype(F32)
            _acc(refs[2 * n + i], val, s == 0)

    kept = [pl.BlockSpec((None,) + p.shape[1:], lambda s, cref: (cref[0], 0, 0)) for p in parts]
    ins = [pl.BlockSpec((None,) + p.shape[1:], lambda s, cref: (s, 0, 0)) for p in parts]
    return pl.pallas_call(
        body, grid_spec=pltpu.PrefetchScalarGridSpec(
            num_scalar_prefetch=1, grid=(parts[0].shape[0],), in_specs=kept + ins,
            out_specs=tuple(pl.BlockSpec(p.shape[1:], lambda s, cref: (0, 0)) for p in parts)),
        out_shape=tuple(_sds(p.shape[1:], F32) for p in parts), name=name, compiler_params=_cp())(mine, *parts, *recvs)


BIG = (("w_in", True), ("w_gate", True), ("w_mem_kv", False), ("w_branch", True), ("w_out", False), ("w_ffn_in", True),
       ("w_ffn_out", False))

SMALL = ("norm_mix_g", "norm_mem_g", "ret_decay_fwd", "ret_decay_bwd", "ret_norm_g", "pool_w", "pool_scale", "na_q_norm_g",
         "na_k_norm_g", "na_rpb", "mem_q_norm_g", "mem_k_norm_g", "norm_ffn_g")
WEIGHTS = ("norm_mix_g", "norm_mem_g", "w_in", "w_gate", "ret_decay_fwd", "ret_decay_bwd", "ret_norm_g", "pool_w", "pool_scale",
           "na_q_norm_g", "na_k_norm_g", "na_rpb", "mem_q_norm_g", "mem_k_norm_g", "w_mem_kv", "w_branch", "w_out", "norm_ffn_g",
           "w_ffn_in", "w_ffn_out")


def _to_exchange(name, transposed, shard):
    if name == "w_branch":
        return jnp.swapaxes(shard, 1, 2).reshape(NH * (D // N_DEV), BW)
    return shard.T if transposed else shard


def _from_exchange(name, transposed, block):
    if name == "w_branch":
        return jnp.swapaxes(block.reshape(NH, D // N_DEV, BW), 1, 2)
    return block.T if transposed else block


def _whole_from_gathered(name, g):
    if name == "w_branch":
        return jnp.swapaxes(g.reshape(N_DEV, NH, D // N_DEV, BW), 0, 1).reshape(NH, D, BW)
    return g.reshape(N_DEV * g.shape[1], g.shape[2])


def _by_destination(name, g):
    if name == "w_branch":
        g = jnp.swapaxes(g.reshape(NH, N_DEV, D // N_DEV, BW), 0, 1).reshape(N_DEV * NH * (D // N_DEV), BW)
    return g.reshape(4, 2, g.shape[0] // N_DEV, g.shape[1])


SMALL_PAD = 1024


def _pack_small(vals, loss=None):
    parts = [vals[n] for n in SMALL] + [jnp.zeros((1,), F32) if loss is None else loss.reshape(1)]
    rows = []
    for p in parts:
        flat = p.reshape(-1)
        rows.append(jnp.pad(flat, (0, -flat.shape[0] % SMALL_PAD)).reshape(-1, 128))
    return jnp.concatenate(rows, axis=0)


def _unpack_small(packed, like):
    out, off = {}, 0
    for n in SMALL:
        sz = int(np.prod(like[n].shape))
        nrow = -(-sz // SMALL_PAD) * (SMALL_PAD // 128)
        out[n] = packed[off:off + nrow].reshape(-1)[:sz].reshape(like[n].shape)
        off += nrow
    return out, packed[off, 0]


def _na_constants():
    c = np.arange(GRID_W)
    win = np.clip(c - NA_COLS_WIN // 2, 0, GRID_W - NA_COLS_WIN)
    kc = np.arange(GRID_W)
    inside = (kc[None, :] >= win[:, None]) & (kc[None, :] < win[:, None] + NA_COLS_WIN)
    off = kc[None, :] - c[:, None] + NA_COLS_WIN - 1
    onehot = np.zeros((128, GRID_W, GRID_W), np.float32)
    for b in range(2 * NA_COLS_WIN - 1):
        onehot[b] = (off == b) & inside
    maskadd = np.where(inside, 0.0, NEG).astype(np.float32)
    return onehot.reshape(128, GRID_W * GRID_W), maskadd


def _na_bias_table(tab, maskadd):
    n_off = 2 * NA_ROWS_WIN - 1
    t4 = tab[:NH * n_off].reshape(NH, n_off, GRID_W, GRID_W) + maskadd[None, None]
    by_off = t4.transpose(1, 0, 2, 3).reshape(n_off, NH * GRID_W, GRID_W)
    return jnp.concatenate([by_off[:-1], by_off[1:]], axis=-1)


def _rotary_tables(t):
    half = HD // 2
    inv = ROPE_THETA ** (-jnp.arange(half, dtype=F32) / half)
    ang = jnp.arange(t, dtype=F32)[:, None] * inv[None, :]
    cos, sin = jnp.cos(ang), jnp.sin(ang)
    return jnp.tile(jnp.concatenate([cos, cos], axis=-1), (1, NH)), jnp.tile(jnp.concatenate([-sin, sin], axis=-1), (1, NH))


def _block_diag(pw):
    out = jnp.zeros((BW, BW), pw.dtype)
    for g in range(NH):
        out = lax.dynamic_update_slice(out, pw[g], (g * HD, g * HD))
    return out


def _tile4(g):
    return jnp.tile(g.reshape(1, HD), (1, NH))


def _layer_fwd(x, mem, sw, lw, consts, more_weights=None):
    cos2, sin2, onehot, maskadd = consts
    h = _rmsnorm_fwd(x, sw["norm_mix_g"].reshape(1, D), "norm_mix_fwd")
    proj = _mm(h, lw["w_in"], tb=True, name="mm_in")
    gp = _mm(h, lw["w_gate"], tb=True, out_dtype=BF16, name="mm_gate")
    g_naq, g_nak, g_mq = _tile4(sw["na_q_norm_g"]), _tile4(sw["na_k_norm_g"]), _tile4(sw["mem_q_norm_g"])
    rq, rk, rv, nq, nk, nv, mq = _prep_fwd(proj, cos2, sin2, g_naq, g_nak, g_mq)

    lgf, lgb = jax.nn.log_sigmoid(sw["ret_decay_fwd"]), jax.nn.log_sigmoid(sw["ret_decay_bwd"])
    g_ret = sw["ret_norm_g"].reshape(1, BW)
    o_ret, ret = _ret_fwd(rq, rk, rv, proj, lgf, lgb, g_ret)

    wbd = _block_diag(sw["pool_w"]).astype(BF16)
    p_scale = sw["pool_scale"].reshape(1, BW)
    pool = _pool_fwd(proj, wbd, p_scale)

    rpb_pad = jnp.pad(sw["na_rpb"].reshape(NH * 15, 31), ((0, 4), (0, 97)))
    ball = _na_bias_table(_rpb_expand(rpb_pad, onehot), maskadd)
    na = _na_fwd(nq, nk, nv, ball)

    memn = _rmsnorm_fwd(mem, sw["norm_mem_g"].reshape(1, D), "norm_mem_fwd")
    kv = _mm(memn, lw["w_mem_kv"], name="mm_memkv")
    g_mk = _tile4(sw["mem_k_norm_g"])
    mk, mv = _memkv_prep(kv, g_mk)
    mo = _mem_fwd(mq, mk, mv)

    br = (ret, pool, na, mo)
    merged = _merge_fwd(br, lw["w_branch"], gp)
    x1 = _mm(merged, lw["w_out"], add=x, name="mm_out")
    if more_weights is not None:
        lw.update(more_weights(x1))
    h2 = _rmsnorm_fwd(x1, sw["norm_ffn_g"].reshape(1, D), "norm_ffn_fwd")
    ag = _mm(h2, lw["w_ffn_in"], tb=True, out_dtype=BF16, name="mm_ffn_in")
    yff = _swiglu_fwd(ag)
    x2 = _mm(yff, lw["w_ffn_out"], add=x1, name="mm_ffn_out")
    saved = dict(x=x, h=h, proj=proj, gp=gp, rq=rq, rk=rk, rv=rv, nq=nq, nk=nk, nv=nv, mq=mq, o_ret=o_ret, ball=ball, memn=memn,
                 kv=kv, mk=mk, mv=mv, br=br, merged=merged, x1=x1, h2=h2, ag=ag, yff=yff, lgf=lgf, lgb=lgb, wbd=wbd)
    return x2, saved


def _layer_bwd(dx2, dx2b, mem, sw, lw, sv, consts, dep=None):
    cos2, sin2, onehot, maskadd = consts
    gb, gs = {}, {}
    dy = _mm(dx2b, lw["w_ffn_out"], tb=True, out_dtype=BF16, dep=dep, name="mm_ffn_out_dx")
    gb["w_ffn_out"] = _mm(sv["yff"], dx2b, ta=True, out_dtype=BF16, name="mm_ffn_out_dw")
    dag = _swiglu_bwd(sv["ag"], dy)
    dh2 = _mm(dag, lw["w_ffn_in"], name="mm_ffn_in_dx")
    gb["w_ffn_in"] = _mm(dag, sv["h2"], ta=True, out_dtype=BF16, name="mm_ffn_in_dw")
    dx1, dx1b, dg = _rmsnorm_bwd(dh2, sv["x1"], sw["norm_ffn_g"].reshape(1, D), dx2, "norm_ffn_bwd")
    gs["norm_ffn_g"] = dg.reshape(D)

    dmerged = _mm(dx1b, lw["w_out"], tb=True, name="mm_out_dx")
    gb["w_out"] = _mm(sv["merged"], dx1b, ta=True, out_dtype=BF16, name="mm_out_dw")
    dgp, dup = _merge_bwd(dmerged, sv["br"], lw["w_branch"], sv["gp"])
    dbr = _dbranch(dup, lw["w_branch"])
    gb["w_branch"] = _dwbranch(sv["br"], dup)

    g_ret = sw["ret_norm_g"].reshape(1, BW)
    do_ret, d_rg, dg_ret = _ret_post_bwd(dbr, sv["o_ret"], sv["proj"], g_ret)
    d_rq, d_rk, d_rv, dlg = _ret_bwd(do_ret, sv["rq"], sv["rk"], sv["rv"], sv["lgf"], sv["lgb"])
    gs["ret_norm_g"] = dg_ret.reshape(BW)
    _, vjp_f = jax.vjp(jax.nn.log_sigmoid, sw["ret_decay_fwd"])
    _, vjp_b = jax.vjp(jax.nn.log_sigmoid, sw["ret_decay_bwd"])
    gs["ret_decay_fwd"] = vjp_f(dlg[0:NH, 0])[0]
    gs["ret_decay_bwd"] = vjp_b(dlg[NH:2 * NH, 0])[0]

    p_scale = sw["pool_scale"].reshape(1, BW)
    d_pv, dwbd, dscale = _pool_bwd(dbr, sv["proj"], sv["wbd"], p_scale)
    gs["pool_w"] = jnp.stack([dwbd[g * HD:(g + 1) * HD, g * HD:(g + 1) * HD] for g in range(NH)])
    gs["pool_scale"] = dscale.reshape(BW)

    d_nq, d_nk, d_nv, dball = _na_bwd(dbr, sv["nq"], sv["nk"], sv["nv"], sv["ball"])
    _, vjp_tab = jax.vjp(lambda tab: _na_bias_table(tab, maskadd), jnp.zeros((64, GRID_W * GRID_W), F32))
    drpb = _rpb_reduce(vjp_tab(dball)[0], onehot)
    gs["na_rpb"] = drpb[:NH * 15, :31].reshape(NH, 15, 31)

    d_mq, d_mk, d_mv = _mem_bwd(dbr, sv["mq"], sv["mk"], sv["mv"])
    g_mk = _tile4(sw["mem_k_norm_g"])
    dkv, dg_mk = _memkv_bwd(sv["kv"], d_mk, d_mv, g_mk)
    gs["mem_k_norm_g"] = dg_mk.reshape(NH, HD).sum(0)
    gb["w_mem_kv"] = _mm(sv["memn"], dkv, ta=True, out_dtype=BF16, name="mm_memkv_dw")
    dmemn = _mm(dkv, lw["w_mem_kv"], tb=True, name="mm_memkv_dx")
    _, _, dg_mem = _rmsnorm_bwd(dmemn, mem, sw["norm_mem_g"].reshape(1, D), jnp.zeros_like(mem), "norm_mem_bwd")
    gs["norm_mem_g"] = dg_mem.reshape(D)

    g_naq, g_nak, g_mq = _tile4(sw["na_q_norm_g"]), _tile4(sw["na_k_norm_g"]), _tile4(sw["mem_q_norm_g"])
    dproj, dg_naq, dg_nak, dg_mq = _prep_bwd(sv["proj"], cos2, sin2, g_naq, g_nak, g_mq, d_rq, d_rk, d_rv, d_rg, d_pv, d_nq, d_nk,
                                             d_nv, d_mq)
    gs["na_q_norm_g"] = dg_naq.reshape(NH, HD).sum(0)
    gs["na_k_norm_g"] = dg_nak.reshape(NH, HD).sum(0)
    gs["mem_q_norm_g"] = dg_mq.reshape(NH, HD).sum(0)

    dh = _mm(dproj, lw["w_in"], name="mm_in_dx")
    dh = _mm(dgp, lw["w_gate"], add=dh, name="mm_gate_dx")
    gb["w_in"] = _mm(dproj, sv["h"], ta=True, out_dtype=BF16, name="mm_in_dw")
    gb["w_gate"] = _mm(dgp, sv["h"], ta=True, out_dtype=BF16, name="mm_gate_dw")
    dx, dxb, dg = _rmsnorm_bwd(dh, sv["x"], sw["norm_mix_g"].reshape(1, D), dx1, "norm_mix_bwd")
    gs["norm_mix_g"] = dg.reshape(D)
    return dx, dxb, gb, gs


def _local_step(x, mem, target, small, get_layer, on_grads):
    t = x.shape[0]
    cos2, sin2 = _rotary_tables(t)
    onehot, maskadd = _na_constants()
    consts = (cos2, sin2, jnp.asarray(onehot), jnp.asarray(maskadd))
    saved, weights, cur = [], [], x
    for l in range(DEPTH):
        sw = {n: small[n][l] for n in SMALL}
        lw, more = get_layer(l, cur)
        weights.append(lw)
        cur, sv = _layer_fwd(cur, mem, sw, lw, consts, more)
        saved.append(sv)
    dy, dyb, loss_tile = _loss_head(cur, target)
    small_g = {n: [None] * DEPTH for n in SMALL}
    dep = None
    for l in reversed(range(DEPTH)):
        sw = {n: small[n][l] for n in SMALL}
        dy, dyb, gb, gs = _layer_bwd(dy, dyb, mem, sw, weights[l], saved[l], consts, dep)
        dep = on_grads(l, gb, dy)
        for n in SMALL:
            small_g[n][l] = gs[n]
    return loss_tile[0, 0], dy, {n: jnp.stack(v) for n, v in small_g.items()}


def _flat2d(a):
    return a.reshape(-1, a.shape[-1])


def kernel(x, mem, norm_mix_g, norm_mem_g, w_in, w_gate, ret_decay_fwd, ret_decay_bwd, ret_norm_g, pool_w, pool_scale, na_q_norm_g, na_k_norm_g, na_rpb, mem_q_norm_g, mem_k_norm_g, w_mem_kv, w_branch, w_out, norm_ffn_g, w_ffn_in, w_ffn_out, loss_target, m_norm_mix_g, m_norm_mem_g, m_w_in, m_w_gate, m_ret_decay_fwd, m_ret_decay_bwd, m_ret_norm_g, m_pool_w, m_pool_scale, m_na_q_norm_g, m_na_k_norm_g, m_na_rpb, m_mem_q_norm_g, m_mem_k_norm_g, m_w_mem_kv, m_w_branch, m_w_out, m_norm_ffn_g, m_w_ffn_in, m_w_ffn_out, v_norm_mix_g, v_norm_mem_g, v_w_in, v_w_gate, v_ret_decay_fwd, v_ret_decay_bwd, v_ret_norm_g, v_pool_w, v_pool_scale, v_na_q_norm_g, v_na_k_norm_g, v_na_rpb, v_mem_q_norm_g, v_mem_k_norm_g, v_w_mem_kv, v_w_branch, v_w_out, v_norm_ffn_g, v_w_ffn_in, v_w_ffn_out):
    w = dict(norm_mix_g=norm_mix_g, norm_mem_g=norm_mem_g, w_in=w_in, w_gate=w_gate, ret_decay_fwd=ret_decay_fwd,
             ret_decay_bwd=ret_decay_bwd, ret_norm_g=ret_norm_g, pool_w=pool_w, pool_scale=pool_scale, na_q_norm_g=na_q_norm_g,
             na_k_norm_g=na_k_norm_g, na_rpb=na_rpb, mem_q_norm_g=mem_q_norm_g, mem_k_norm_g=mem_k_norm_g, w_mem_kv=w_mem_kv,
             w_branch=w_branch, w_out=w_out, norm_ffn_g=norm_ffn_g, w_ffn_in=w_ffn_in, w_ffn_out=w_ffn_out)
    m = dict(norm_mix_g=m_norm_mix_g, norm_mem_g=m_norm_mem_g, w_in=m_w_in, w_gate=m_w_gate, ret_decay_fwd=m_ret_decay_fwd,
             ret_decay_bwd=m_ret_decay_bwd, ret_norm_g=m_ret_norm_g, pool_w=m_pool_w, pool_scale=m_pool_scale, na_q_norm_g=m_na_q_norm_g,
             na_k_norm_g=m_na_k_norm_g, na_rpb=m_na_rpb, mem_q_norm_g=m_mem_q_norm_g, mem_k_norm_g=m_mem_k_norm_g, w_mem_kv=m_w_mem_kv,
             w_branch=m_w_branch, w_out=m_w_out, norm_ffn_g=m_norm_ffn_g, w_ffn_in=m_w_ffn_in, w_ffn_out=m_w_ffn_out)
    v = dict(norm_mix_g=v_norm_mix_g, norm_mem_g=v_norm_mem_g, w_in=v_w_in, w_gate=v_w_gate, ret_decay_fwd=v_ret_decay_fwd,
             ret_decay_bwd=v_ret_decay_bwd, ret_norm_g=v_ret_norm_g, pool_w=v_pool_w, pool_scale=v_pool_scale, na_q_norm_g=v_na_q_norm_g,
             na_k_norm_g=v_na_k_norm_g, na_rpb=v_na_rpb, mem_q_norm_g=v_mem_q_norm_g, mem_k_norm_g=v_mem_k_norm_g, w_mem_kv=v_w_mem_kv,
             w_branch=v_w_branch, w_out=v_w_out, norm_ffn_g=v_norm_ffn_g, w_ffn_in=v_w_ffn_in, w_ffn_out=v_w_ffn_out)
    assert x.shape == (1, 2048, D) and mem.shape == (1, N_MEM, D) and w_in.shape == (DEPTH, D, 9 * BW // N_DEV)

    started = []
    for l in range(DEPTH):
        blocks = [_to_exchange(name, tr, w[name][l]).astype(BF16) for name, tr in BIG]
        lands = [lax.empty((N_DEV,) + b.shape, BF16) for b in blocks]
        started.append(_ici_start(blocks, lands, "gather", "gather_ici_start_%d" % l))
    all_started = started[0][4] + started[1][4] + started[2][4] + started[3][4]

    def get_group(l, only, after, tag):
        lands = _ici_wait(started[l], after, "gather_ici_wait_%d%s" % (l, tag), only)
        whole = _gather_d2d([started[l][2][i] for i in only], lands, "gather_d2d")
        return {BIG[i][0]: _whole_from_gathered(BIG[i][0], g) for i, g in zip(only, whole)}

    def get_layer(l, after):
        if l > 0:
            return get_group(l, list(range(len(BIG))), after, ""), None
        mixer = [i for i, (name, _) in enumerate(BIG) if not name.startswith("w_ffn")]
        ffn = [i for i, (name, _) in enumerate(BIG) if name.startswith("w_ffn")]
        return get_group(l, mixer, all_started, "a"), lambda after2: get_group(l, ffn, after2, "b")

    cidx = lax.axis_index("c").astype(jnp.int32).reshape(1)
    chip = (2 * lax.axis_index("x") + lax.axis_index("y")).astype(jnp.int32).reshape(1)
    in_flight = []

    def flip_of(name, tr):
        return (lambda a: jnp.swapaxes(a, 1, 2)) if name in ("w_in", "w_ffn_in") else (lambda a: a)

    def rows3(a):
        return a.reshape(DEPTH, -1, a.shape[-1])

    opt_in = {name: tuple(rows3(flip_of(name, tr)(t[name])) for t in (w, m, v)) for name, tr in BIG}
    opt_out = {name: tuple(lax.empty(opt_in[name][0].shape, F32) for _ in range(4)) for name, _ in BIG}

    device = (2 * chip + cidx).astype(jnp.int32)

    def finish(l, st, after):
        recv = _ici_wait(st, after, "rs_ici_wait_%d" % l)
        sums = _sum_own(st[2], recv, chip if st[5] == 3 else device, "rs_sum")
        for (name, tr), s in zip(BIG, sums):
            g = s if name in ("w_in", "w_ffn_in") else _from_exchange(name, tr, s)
            wx, mx, vx = opt_in[name]
            opt_out[name] = _adamw_layer(l, wx, g.reshape(-1, g.shape[-1]), mx, vx, opt_out[name], "adamw_" + name)

    def on_grads(l, gb, after):
        send = [_by_destination(name, gb[name]) for name, _ in BIG]
        if l > 0:
            send = [s.reshape((N_DEV,) + s.shape[2:]) for s in send]
            st = _ici_start(send, [lax.empty(s.shape, BF16) for s in send], "by_device", "rs_ici_start_%d" % l)
        else:
            from_core = _rs_core_swap(send, "rs_core_swap")
            chip_part = _pair_sum(send, from_core, cidx)
            st = _ici_start(chip_part, [lax.empty(p.shape, BF16) for p in chip_part], "by_chip", "rs_ici_start_%d" % l)
        in_flight.append((l, st))
        return st[4]

    loss_local, dx, small_g = _local_step(x[0], mem[0], loss_target[0], {n: w[n] for n in SMALL}, get_layer, on_grads)

    last_started = in_flight[-1][1][4]
    for l, st in in_flight[:-1]:
        finish(l, st, last_started)

    small_all, = _all_gather([_pack_small(small_g, loss_local) + last_started[0:1]], "gather_small")
    packed_g = _sum_slots(small_all, "small_sum")
    small_sum, loss = _unpack_small(packed_g, {n: w[n] for n in SMALL})
    d_, m_, v_ = _adamw(_pack_small({n: w[n] for n in SMALL}), packed_g, _pack_small({n: m[n] for n in SMALL}),
                        _pack_small({n: v[n] for n in SMALL}), "adamw_small")
    updated = d_[0:8]
    for name, _ in BIG:
        updated = updated + opt_out[name][0][1, 0:8, 0:128]
    finish(*in_flight[-1], updated)

    grads, delta, new_m, new_v = {}, {}, {}, {}
    for name, tr in BIG:
        shape = flip_of(name, tr)(w[name]).shape
        delta[name], new_m[name], new_v[name], grads[name] = (flip_of(name, tr)(a.reshape(shape)) for a in opt_out[name])
    like = {n: w[n] for n in SMALL}
    ds, _ = _unpack_small(d_, like)
    ms, _ = _unpack_small(m_, like)
    vs, _ = _unpack_small(v_, like)
    for n in SMALL:
        grads[n], delta[n], new_m[n], new_v[n] = small_sum[n], ds[n], ms[n], vs[n]

    return (loss, dx[None], *[grads[n] for n in WEIGHTS], *[delta[n] for n in WEIGHTS], *[new_m[n] for n in WEIGHTS],
            *[new_v[n] for n in WEIGHTS])
```

```python
import functools

import numpy as np
import jax
import jax.numpy as jnp
from jax import lax
from jax.experimental import pallas as pl
from jax.experimental.pallas import tpu as pltpu

F32 = jnp.float32
BF16 = jnp.bfloat16
MXU = jnp.bfloat16
HI = lax.Precision.HIGHEST

DEPTH = 4
D = 1024
BW = 256
HD = 64
NH = 4
GRID_W = 64
NA_ROWS_WIN = 8
NA_COLS_WIN = 16
N_MEM = 256
FF = 2816
EPS = 1e-6
NEG = -1e30
ROPE_THETA = 10000.0
POOL_HALF_MAX = 8

ADAM_LR, ADAM_B1, ADAM_B2, ADAM_EPS, ADAM_WD, ADAM_STEP = 0.001, 0.9, 0.999, 1e-08, 0.01, 10

N_DEV = 8
VMEM_LIMIT = 56 * 1024 * 1024

RQ, RK, RV, RG, PV, NQ, NK, NV, MQ = range(9)

MESH = pl.DeviceIdType.MESH
ANY = pl.BlockSpec(memory_space=pl.ANY)
SMEM = pl.BlockSpec(memory_space=pltpu.SMEM)


def _cp(**kw):
    return pltpu.CompilerParams(vmem_limit_bytes=VMEM_LIMIT, **kw)


def _tile(n, cap):
    if n <= cap:
        return n
    best = None
    for t in range(128, cap + 1, 128):
        if n % t == 0:
            best = t
    assert best is not None, (n, cap)
    return best


def _sds(shape, dtype):
    return jax.ShapeDtypeStruct(shape, dtype)


def _lane_head(shape):
    return lax.shift_right_logical(lax.broadcasted_iota(jnp.int32, shape, len(shape) - 1), 6)


def _group_mean(z):
    i = lax.shift_right_logical(lax.broadcasted_iota(jnp.int32, (BW, BW), 0), 6)
    j = lax.shift_right_logical(lax.broadcasted_iota(jnp.int32, (BW, BW), 1), 6)
    g = jnp.where(i == j, 1.0 / HD, 0.0).astype(BF16)
    z_hi = z.astype(BF16)
    z_lo = (z - z_hi.astype(F32)).astype(BF16)
    return jnp.dot(z_hi, g, preferred_element_type=F32) + jnp.dot(z_lo, g, preferred_element_type=F32)


def _gnorm(t, g):
    r = lax.rsqrt(_group_mean(t * t) + EPS)
    return t * r * g


def _gnorm_bwd(dy, t, g):
    r = lax.rsqrt(_group_mean(t * t) + EPS)
    th = t * r
    dth = dy * g
    dt = r * (dth - th * _group_mean(dth * th))
    return dt, dy * th


def _swap_halves(t):
    lane = lax.broadcasted_iota(jnp.int32, t.shape, 1)
    return jnp.where((lane & 63) < 32, pltpu.roll(t, BW - 32, 1), pltpu.roll(t, 32, 1))


def _sigmoid(x):
    return 1.0 / (1.0 + jnp.exp(-x))


def _dot(a, b, ta=False, tb=False):
    return lax.dot_general(a.astype(MXU), b.astype(MXU), (((0 if ta else 1,), (1 if tb else 0,)), ((), ())),
                           preferred_element_type=F32)


def _stack_heads(t):
    head = _lane_head(t.shape)
    return jnp.concatenate([jnp.where(head == h, t, jnp.zeros_like(t)) for h in range(NH)], axis=0)


def _unstack_heads(t, rows):
    head = _lane_head((rows, BW))
    out = jnp.zeros((rows, BW), F32)
    for h in range(NH):
        out = out + jnp.where(head == h, t[h * rows:(h + 1) * rows], 0.0)
    return out


def _softmax_rows(s):
    m = jnp.max(s, axis=-1, keepdims=True)
    e = jnp.exp(s - m)
    return e / jnp.sum(e, axis=-1, keepdims=True)


def _acc(ref, val, first):
    @pl.when(first)
    def _():
        ref[...] = val

    @pl.when(jnp.logical_not(first))
    def _():
        ref[...] += val


def _mm(a, b, *, ta=False, tb=False, out_dtype=F32, add=None, dep=None, b_half=None, out_half=None, name):
    m, k = (a.shape[1], a.shape[0]) if ta else a.shape
    n = b.shape[0] if tb else b.shape[1]
    assert b_half is None or (not tb and b.shape[0] == 2 * k)
    tm, tn = _tile(m, 1408), _tile(n, 768)
    n_in = 2 + (add is not None) + (dep is not None) + (out_half is not None)

    def body(*refs):
        a_ref, b_ref, o_ref = refs[0], refs[1], refs[n_in]
        r = _dot(a_ref[...], b_ref[...], ta, tb)
        if add is not None:
            r = r + refs[2][...]
        o_ref[...] = r.astype(out_dtype)

    kb = 0 if b_half is None else b_half
    a_spec = pl.BlockSpec((k, tm), lambda i, j: (0, i)) if ta else pl.BlockSpec((tm, k), lambda i, j: (i, 0))
    b_spec = pl.BlockSpec((tn, k), lambda i, j: (j, 0)) if tb else pl.BlockSpec((k, tn), lambda i, j: (kb, j))
    plain = pl.BlockSpec((tm, tn), lambda i, j: (i, j))
    ins, args = [a_spec, b_spec], [a, b]
    if add is not None:
        ins.append(plain)
        args.append(add)
    if dep is not None:
        ins.append(pl.BlockSpec((8, 128), lambda i, j: (0, 0)))
        args.append(dep)
    o_spec, o_shape, aliases = plain, _sds((m, n), out_dtype), {}
    if out_half is not None:
        half, prev = out_half
        o_spec = pl.BlockSpec((tm, tn), lambda i, j: (i + half * (m // tm), j))
        o_shape = _sds((2 * m, n), out_dtype)
        ins.append(ANY)
        args.append(lax.empty((2 * m, n), out_dtype) if prev is None else prev)
        aliases = {len(args) - 1: 0}
    return pl.pallas_call(
        body, grid=(m // tm, n // tn), in_specs=ins, out_specs=o_spec, out_shape=o_shape, input_output_aliases=aliases, name=name,
        compiler_params=_cp(dimension_semantics=("parallel", "parallel")))(*args)


def _rmsnorm_fwd(x, g, name):
    t, d = x.shape
    tm = _tile(t, 256)

    def body(x_ref, g_ref, o_ref):
        xv = x_ref[...]
        r = lax.rsqrt(jnp.mean(xv * xv, axis=-1, keepdims=True) + EPS)
        o_ref[...] = (xv * r * g_ref[...]).astype(o_ref.dtype)

    return pl.pallas_call(
        body, grid=(t // tm,), in_specs=[pl.BlockSpec((tm, d), lambda i: (i, 0)), pl.BlockSpec((1, d), lambda i: (0, 0))],
        out_specs=pl.BlockSpec((tm, d), lambda i: (i, 0)), out_shape=_sds((t, d), BF16), name=name, compiler_params=_cp())(x, g)


def _rmsnorm_bwd(dh, x, g, res, name):
    t, d = x.shape
    tm = _tile(t, 256)

    def body(dh_ref, x_ref, g_ref, res_ref, dx_ref, dxb_ref, dg_ref):
        xv = x_ref[...]
        dhv = dh_ref[...]
        r = lax.rsqrt(jnp.mean(xv * xv, axis=-1, keepdims=True) + EPS)
        xh = xv * r
        dxh = dhv * g_ref[...]
        dx = res_ref[...] + r * (dxh - xh * jnp.mean(dxh * xh, axis=-1, keepdims=True))
        dx_ref[...] = dx
        dxb_ref[...] = dx.astype(BF16)
        _acc(dg_ref, jnp.sum(dhv * xh, axis=0, keepdims=True), pl.program_id(0) == 0)

    row = pl.BlockSpec((tm, d), lambda i: (i, 0))
    vec = pl.BlockSpec((1, d), lambda i: (0, 0))
    return pl.pallas_call(
        body, grid=(t // tm,), in_specs=[row, row, vec, row], out_specs=(row, row, vec),
        out_shape=(_sds((t, d), F32), _sds((t, d), BF16), _sds((1, d), F32)), name=name, compiler_params=_cp())(dh, x, g, res)


def _prep_fwd(proj, cos2, sin2, g_naq, g_nak, g_mq):
    t = proj.shape[0]
    tm = 256

    def body(p_ref, cos_ref, sin_ref, gq_ref, gk_ref, gm_ref, rq_ref, rk_ref, rv_ref, nq_ref, nk_ref, nv_ref, mq_ref):
        def col(c):
            return p_ref[:, c * BW:(c + 1) * BW]

        cosv, sinv = cos_ref[...], sin_ref[...]

        def rot(tv):
            return tv * cosv + _swap_halves(tv) * sinv

        rq_ref[...] = (rot(col(RQ)) * (HD ** -0.5)).astype(BF16)
        rk_ref[...] = rot(col(RK)).astype(BF16)
        rv_ref[...] = col(RV).astype(BF16)
        nq_ref[...] = _gnorm(col(NQ), gq_ref[...]).astype(BF16)
        nk_ref[...] = _gnorm(col(NK), gk_ref[...]).astype(BF16)
        nv_ref[...] = col(NV).astype(BF16)
        mq_ref[...] = _gnorm(col(MQ), gm_ref[...]).astype(BF16)

    blk = pl.BlockSpec((tm, BW), lambda i: (i, 0))
    vec = pl.BlockSpec((1, BW), lambda i: (0, 0))
    return pl.pallas_call(
        body, grid=(t // tm,), in_specs=[pl.BlockSpec((tm, 9 * BW), lambda i: (i, 0)), blk, blk, vec, vec, vec],
        out_specs=tuple(blk for _ in range(7)), out_shape=tuple(_sds((t, BW), BF16) for _ in range(7)),
        name="prep_fwd", compiler_params=_cp())(proj, cos2, sin2, g_naq, g_nak, g_mq)


def _prep_bwd(proj, cos2, sin2, g_naq, g_nak, g_mq, d_rq, d_rk, d_rv, d_rg, d_pv, d_nq, d_nk, d_nv, d_mq):
    t = proj.shape[0]
    tm = 256

    def body(p_ref, cos_ref, sin_ref, gq_ref, gk_ref, gm_ref, drq_ref, drk_ref, drv_ref, drg_ref, dpv_ref, dnq_ref, dnk_ref,
             dnv_ref, dmq_ref, o_ref, dgq_ref, dgk_ref, dgm_ref):
        first = pl.program_id(0) == 0

        def col(c):
            return p_ref[:, c * BW:(c + 1) * BW]

        def put(c, v):
            o_ref[:, c * BW:(c + 1) * BW] = v.astype(BF16)

        cosv, sinv = cos_ref[...], sin_ref[...]

        def rot_t(dv):
            return dv * cosv + _swap_halves(dv * sinv)

        put(RQ, rot_t(drq_ref[...] * (HD ** -0.5)))
        put(RK, rot_t(drk_ref[...]))
        put(RV, drv_ref[...])
        put(RG, drg_ref[...])
        put(PV, dpv_ref[...])
        dq, gq = _gnorm_bwd(dnq_ref[...], col(NQ), gq_ref[...])
        put(NQ, dq)
        _acc(dgq_ref, jnp.sum(gq, axis=0, keepdims=True), first)
        dk, gk = _gnorm_bwd(dnk_ref[...], col(NK), gk_ref[...])
        put(NK, dk)
        _acc(dgk_ref, jnp.sum(gk, axis=0, keepdims=True), first)
        put(NV, dnv_ref[...])
        dm, gm = _gnorm_bwd(dmq_ref[...], col(MQ), gm_ref[...])
        put(MQ, dm)
        _acc(dgm_ref, jnp.sum(gm, axis=0, keepdims=True), first)

    blk = pl.BlockSpec((tm, BW), lambda i: (i, 0))
    vec = pl.BlockSpec((1, BW), lambda i: (0, 0))
    wide = pl.BlockSpec((tm, 9 * BW), lambda i: (i, 0))
    return pl.pallas_call(
        body, grid=(t // tm,), in_specs=[wide, blk, blk, vec, vec, vec] + [blk] * 9, out_specs=(wide, vec, vec, vec),
        out_shape=(_sds((t, 9 * BW), BF16), _sds((1, BW), F32), _sds((1, BW), F32), _sds((1, BW), F32)),
        name="prep_bwd", compiler_params=_cp())(proj, cos2, sin2, g_naq, g_nak, g_mq, d_rq, d_rk, d_rv, d_rg, d_pv, d_nq, d_nk,
                                                d_nv, d_mq)


RET_B = 256


def _ret_consts(lgf_ref, lgb_ref):
    bsz = RET_B
    head = _lane_head((1, BW))
    lf, lb = jnp.zeros((1, BW), F32), jnp.zeros((1, BW), F32)
    for h in range(NH):
        lf = lf + jnp.where(head == h, lgf_ref[h], 0.0)
        lb = lb + jnp.where(head == h, lgb_ref[h], 0.0)
    pos = lax.broadcasted_iota(jnp.int32, (bsz, BW), 0).astype(F32)
    up, down = pos + 1.0, (bsz - 1.0) - pos
    c = dict(up=up, down=down, kf=jnp.exp(down * lf), kb=jnp.exp(up * lb), qf=jnp.exp(up * lf), qb=jnp.exp(down * lb),
             cf=jnp.exp(bsz * lf), cb=jnp.exp(bsz * lb))
    diff = (lax.broadcasted_iota(jnp.int32, (NH * bsz, 1), 0) & (bsz - 1)) - lax.broadcasted_iota(jnp.int32, (1, bsz), 1)
    c["causal"] = diff >= 0
    c["dist"] = jnp.abs(diff).astype(F32)
    lgf = jnp.concatenate([jnp.full((bsz, 1), lgf_ref[h], F32) for h in range(NH)], axis=0)
    lgb = jnp.concatenate([jnp.full((bsz, 1), lgb_ref[h], F32) for h in range(NH)], axis=0)
    c["dm"] = jnp.exp(c["dist"] * jnp.where(c["causal"], lgf, lgb))
    c["bd"] = _lane_head((BW, BW)) == lax.shift_right_logical(lax.broadcasted_iota(jnp.int32, (BW, BW), 0), 6)
    return c


def _ret_states(k_ref, v_ref, st_ref, c, nb):
    bsz = RET_B

    def summary(b, decay):
        kb = k_ref[b * bsz:(b + 1) * bsz, :].astype(F32)
        return jnp.where(c["bd"], _dot(kb * decay, v_ref[b * bsz:(b + 1) * bsz, :], ta=True), 0.0)

    f = jnp.zeros((BW, BW), F32)
    for b in range(nb):
        st_ref[b] = f
        if b < nb - 1:
            f = c["cf"] * f + summary(b, c["kf"])
    g = jnp.zeros((BW, BW), F32)
    for b in reversed(range(nb)):
        st_ref[nb + b] = g
        if b > 0:
            g = c["cb"] * g + summary(b, c["kb"])


def _ret_fwd(q, k, v, proj, lgf, lgb, g_ret):
    t = q.shape[0]
    bsz, nb = RET_B, t // RET_B

    def body(lgf_ref, lgb_ref, q_ref, k_ref, v_ref, rg_ref, g_ref, o_ref, ret_ref, st_ref):
        c = _ret_consts(lgf_ref, lgb_ref)
        _ret_states(k_ref, v_ref, st_ref, c, nb)
        for b in range(nb):
            blk = slice(b * bsz, (b + 1) * bsz)
            qb, kb, vb = q_ref[blk, :], k_ref[blk, :], v_ref[blk, :]
            s = _dot(_stack_heads(qb), kb, tb=True)
            o = _unstack_heads(_dot(s * c["dm"], vb), bsz)
            q32 = qb.astype(F32)
            o = o + _dot(q32 * c["qf"], st_ref[b]) + _dot(q32 * c["qb"], st_ref[nb + b])
            o_ref[blk, :] = o
            rg = rg_ref[blk, :]
            ret_ref[blk, :] = (_gnorm(o, g_ref[...]) * (rg * _sigmoid(rg))).astype(BF16)

    whole = pl.BlockSpec((t, BW), lambda i: (0, 0))
    return pl.pallas_call(
        body, grid=(1,),
        in_specs=[SMEM, SMEM, whole, whole, whole, pl.BlockSpec((t, BW), lambda i: (0, RG)), pl.BlockSpec((1, BW), lambda i: (0, 0))],
        out_specs=(whole, whole), out_shape=(_sds((t, BW), F32), _sds((t, BW), BF16)),
        scratch_shapes=[pltpu.VMEM((2 * nb, BW, BW), F32)], name="ret_fwd", compiler_params=_cp())(lgf, lgb, q, k, v, proj, g_ret)


def _ret_post_bwd(dbr, o_ret, proj, g_ret):
    t = o_ret.shape[0]
    tm = 256

    def body(d_ref, o_ref, rg_ref, g_ref, do_ref, drg_ref, dg_ref):
        dret, o, rg, g = d_ref[...], o_ref[...], rg_ref[...], g_ref[...]
        sg = _sigmoid(rg)
        do, dgain = _gnorm_bwd(dret * (rg * sg), o, g)
        do_ref[...] = do.astype(BF16)
        drg_ref[...] = dret * _gnorm(o, g) * (sg * (1.0 + rg * (1.0 - sg)))
        _acc(dg_ref, jnp.sum(dgain, axis=0, keepdims=True), pl.program_id(0) == 0)

    blk = pl.BlockSpec((tm, BW), lambda i: (i, 0))
    vec = pl.BlockSpec((1, BW), lambda i: (0, 0))
    return pl.pallas_call(
        body, grid=(t // tm,), in_specs=[blk, blk, pl.BlockSpec((tm, BW), lambda i: (i, RG)), vec], out_specs=(blk, blk, vec),
        out_shape=(_sds((t, BW), BF16), _sds((t, BW), F32), _sds((1, BW), F32)), name="ret_post_bwd",
        compiler_params=_cp())(dbr, o_ret, proj, g_ret)


def _ret_bwd(do, q, k, v, lgf, lgb):
    t = q.shape[0]
    bsz, nb = RET_B, t // RET_B

    def body(lgf_ref, lgb_ref, d_ref, q_ref, k_ref, v_ref, dq_ref, dk_ref, dv_ref, dlg_ref, st_ref, sd_ref):
        c = _ret_consts(lgf_ref, lgb_ref)
        _ret_states(k_ref, v_ref, st_ref, c, nb)
        lane_f, lane_b = jnp.zeros((1, BW), F32), jnp.zeros((1, BW), F32)
        row_f, row_b = jnp.zeros((NH * bsz, 1), F32), jnp.zeros((NH * bsz, 1), F32)

        def rows(x):
            return jnp.sum(x, axis=0, keepdims=True)

        for b in range(nb):
            blk = slice(b * bsz, (b + 1) * bsz)
            qb, kb, vb, dob = q_ref[blk, :], k_ref[blk, :], v_ref[blk, :], d_ref[blk, :]
            q32 = qb.astype(F32)
            qs, dos = _stack_heads(qb), _stack_heads(dob)
            s = _dot(qs, kb, tb=True)
            da = _dot(dos, vb, tb=True)
            dv_ref[blk, :] = _dot(s * c["dm"], dos, ta=True)
            ds = da * c["dm"]
            w = ds * s * c["dist"]
            row_f = row_f + jnp.sum(jnp.where(c["causal"], w, 0.0), axis=1, keepdims=True)
            row_b = row_b + jnp.sum(jnp.where(c["causal"], 0.0, w), axis=1, keepdims=True)
            dsb = ds.astype(MXU)
            dk_ref[blk, :] = _dot(dsb, qs, ta=True)
            dq_f = _dot(dob, st_ref[b], tb=True) * c["qf"]
            dq_b = _dot(dob, st_ref[nb + b], tb=True) * c["qb"]
            lane_f = lane_f + rows(c["up"] * dq_f * q32)
            lane_b = lane_b + rows(c["down"] * dq_b * q32)
            dq_ref[blk, :] = _unstack_heads(_dot(dsb, kb), bsz) + dq_f + dq_b
            sd_ref[b] = jnp.where(c["bd"], _dot(q32 * c["qf"], dob, ta=True), 0.0)
            sd_ref[nb + b] = jnp.where(c["bd"], _dot(q32 * c["qb"], dob, ta=True), 0.0)

        def through_state(b, grad, decay, weight, lane):
            blk = slice(b * bsz, (b + 1) * bsz)
            k32 = k_ref[blk, :].astype(F32)
            dk = _dot(v_ref[blk, :], grad, tb=True) * decay
            dk_ref[blk, :] += dk
            dv_ref[blk, :] += _dot(k32 * decay, grad)
            return lane + rows(weight * dk * k32)

        phi = jnp.zeros((BW, BW), F32)
        for b in reversed(range(nb)):
            if b < nb - 1:
                lane_f = through_state(b, phi, c["kf"], c["down"], lane_f)
                lane_f = lane_f + bsz * rows(c["cf"] * st_ref[b] * phi)
            phi = sd_ref[b] + c["cf"] * phi
        gam = jnp.zeros((BW, BW), F32)
        for b in range(nb):
            if b > 0:
                lane_b = through_state(b, gam, c["kb"], c["up"], lane_b)
                lane_b = lane_b + bsz * rows(c["cb"] * st_ref[nb + b] * gam)
            gam = sd_ref[nb + b] + c["cb"] * gam

        head = _lane_head((1, BW))
        for h in range(NH):
            tot_f = jnp.sum(row_f[h * bsz:(h + 1) * bsz, :]) + jnp.sum(jnp.where(head == h, lane_f, 0.0))
            tot_b = jnp.sum(row_b[h * bsz:(h + 1) * bsz, :]) + jnp.sum(jnp.where(head == h, lane_b, 0.0))
            dlg_ref[h:h + 1, :] = jnp.full((1, 128), tot_f, F32)
            dlg_ref[NH + h:NH + h + 1, :] = jnp.full((1, 128), tot_b, F32)

    whole = pl.BlockSpec((t, BW), lambda i: (0, 0))
    return pl.pallas_call(
        body, grid=(1,), in_specs=[SMEM, SMEM, whole, whole, whole, whole],
        out_specs=(whole, whole, whole, pl.BlockSpec((2 * NH, 128), lambda i: (0, 0))),
        out_shape=(_sds((t, BW), F32), _sds((t, BW), F32), _sds((t, BW), F32), _sds((2 * NH, 128), F32)),
        scratch_shapes=[pltpu.VMEM((2 * nb, BW, BW), F32), pltpu.VMEM((2 * nb, BW, BW), F32)], name="ret_bwd",
        compiler_params=_cp())(lgf, lgb, do, q, k, v)


def _pool_windows(t):
    row = lax.broadcasted_iota(jnp.int32, (t, BW), 0)
    half = lax.shift_left(jnp.ones((t, BW), jnp.int32), _lane_head((t, BW)))
    cnt = (jnp.minimum(row + half, t) - jnp.maximum(row - half, 0)).astype(F32)
    return row, half, cnt


def _pool_window_sum(v, row, half, t, transpose):
    out = jnp.zeros_like(v)
    for j in range(-POOL_HALF_MAX, POOL_HALF_MAX):
        src = row - j if transpose else row + j
        ok = (src >= 0) & (src < t) & (j >= -half) & (j < half)
        out = out + jnp.where(ok, pltpu.roll(v, (j if transpose else -j) % t, 0), 0.0)
    return out


def _pool_fwd(proj, wbd, scale):
    t = proj.shape[0]

    def body(v_ref, w_ref, s_ref, o_ref):
        v = v_ref[...]
        row, half, cnt = _pool_windows(t)
        pooled = _pool_window_sum(v, row, half, t, False) / cnt - v
        o_ref[...] = (_dot(pooled, w_ref[...]) * s_ref[...]).astype(BF16)

    return pl.pallas_call(
        body, grid=(1,),
        in_specs=[pl.BlockSpec((t, BW), lambda i: (0, PV)), pl.BlockSpec((BW, BW), lambda i: (0, 0)), pl.BlockSpec((1, BW), lambda i: (0, 0))],
        out_specs=pl.BlockSpec((t, BW), lambda i: (0, 0)), out_shape=_sds((t, BW), BF16), name="pool_fwd",
        compiler_params=_cp())(proj, wbd, scale)


def _pool_bwd(dbr, proj, wbd, scale):
    t = proj.shape[0]

    def body(d_ref, v_ref, w_ref, s_ref, dv_ref, dw_ref, ds_ref):
        v, dout = v_ref[...], d_ref[...]
        row, half, cnt = _pool_windows(t)
        pooled = _pool_window_sum(v, row, half, t, False) / cnt - v
        mixed = _dot(pooled, w_ref[...])
        ds_ref[...] = jnp.sum(dout * mixed, axis=0, keepdims=True)
        dmixed = dout * s_ref[...]
        dw_ref[...] = _dot(pooled, dmixed, ta=True)
        dpooled = _dot(dmixed, w_ref[...], tb=True)
        dv_ref[...] = _pool_window_sum(dpooled / cnt, row, half, t, True) - dpooled

    return pl.pallas_call(
        body, grid=(1,),
        in_specs=[pl.BlockSpec((t, BW), lambda i: (0, 1)), pl.BlockSpec((t, BW), lambda i: (0, PV)),
                  pl.BlockSpec((BW, BW), lambda i: (0, 0)), pl.BlockSpec((1, BW), lambda i: (0, 0))],
        out_specs=(pl.BlockSpec((t, BW), lambda i: (0, 0)), pl.BlockSpec((BW, BW), lambda i: (0, 0)), pl.BlockSpec((1, BW), lambda i: (0, 0))),
        out_shape=(_sds((t, BW), F32), _sds((BW, BW), F32), _sds((1, BW), F32)), name="pool_bwd",
        compiler_params=_cp())(dbr, proj, wbd, scale)


NA_KEYS = NA_ROWS_WIN * GRID_W
NA_PAIRS = 2 * NA_ROWS_WIN - 2


def _na_window(r, n_rows):
    rs = jnp.clip(r - NA_ROWS_WIN // 2, 0, n_rows - NA_ROWS_WIN)
    return pl.multiple_of(rs * GRID_W, GRID_W), rs - r + (NA_ROWS_WIN - 1)


def _na_bias(b_ref, a0):
    return jnp.concatenate([b_ref[a0 + 2 * j] for j in range(NA_ROWS_WIN // 2)], axis=1)


NA_STEP_ROWS = 8


def _na_fwd(q, k, v, ball):
    t = q.shape[0]
    n_rows = t // GRID_W
    rows = NA_STEP_ROWS

    def body(q_ref, k_ref, v_ref, b_ref, o_ref):
        for rr in range(rows):
            start, a0 = _na_window(pl.program_id(0) * rows + rr, n_rows)
            own = slice(rr * GRID_W, (rr + 1) * GRID_W)
            qs = _stack_heads(q_ref[own, :])
            s = _dot(qs, k_ref[pl.ds(start, NA_KEYS), :], tb=True) * (HD ** -0.5) + _na_bias(b_ref, a0)
            p = _softmax_rows(s)
            o_ref[own, :] = _unstack_heads(_dot(p, v_ref[pl.ds(start, NA_KEYS), :]), GRID_W).astype(BF16)

    blk = pl.BlockSpec((rows * GRID_W, BW), lambda r: (r, 0))
    whole = pl.BlockSpec((t, BW), lambda r: (0, 0))
    return pl.pallas_call(
        body, grid=(n_rows // rows,), in_specs=[blk, whole, whole, pl.BlockSpec(ball.shape, lambda r: (0, 0, 0))],
        out_specs=blk, out_shape=_sds((t, BW), BF16), name="na_fwd", compiler_params=_cp())(q, k, v, ball)


def _na_bwd(dbr, q, k, v, ball):
    t = q.shape[0]
    n_rows = t // GRID_W

    rows = NA_STEP_ROWS

    def body(d_ref, q_ref, k_ref, v_ref, b_ref, dq_ref, dk_ref, dv_ref, db_ref):
        @pl.when(pl.program_id(0) == 0)
        def _():
            dk_ref[...] = jnp.zeros_like(dk_ref)
            dv_ref[...] = jnp.zeros_like(dv_ref)
            db_ref[...] = jnp.zeros_like(db_ref)

        for rr in range(rows):
            start, a0 = _na_window(pl.program_id(0) * rows + rr, n_rows)
            keys = pl.ds(start, NA_KEYS)
            own = slice(rr * GRID_W, (rr + 1) * GRID_W)
            qs = _stack_heads(q_ref[own, :])
            kb, vb = k_ref[keys, :], v_ref[keys, :]
            p = _softmax_rows(_dot(qs, kb, tb=True) * (HD ** -0.5) + _na_bias(b_ref, a0))
            dos = _stack_heads(d_ref[own, :]).astype(MXU)
            dp = _dot(dos, vb, tb=True)
            dv_ref[keys, :] += _dot(p, dos, ta=True)
            ds = p * (dp - jnp.sum(dp * p, axis=-1, keepdims=True))
            for j in range(NA_ROWS_WIN // 2):
                db_ref[a0 + 2 * j] += ds[:, 2 * j * GRID_W:(2 * j + 2) * GRID_W]
            dsb = (ds * (HD ** -0.5)).astype(MXU)
            dq_ref[own, :] = _unstack_heads(_dot(dsb, kb), GRID_W)
            dk_ref[keys, :] += _dot(dsb, qs, ta=True)

    blk = pl.BlockSpec((rows * GRID_W, BW), lambda r: (r, 0))
    whole = pl.BlockSpec((t, BW), lambda r: (0, 0))
    tab = pl.BlockSpec(ball.shape, lambda r: (0, 0, 0))
    return pl.pallas_call(
        body, grid=(n_rows // rows,), in_specs=[pl.BlockSpec((rows * GRID_W, BW), lambda r: (r, 2)), blk, whole, whole, tab],
        out_specs=(blk, whole, whole, tab),
        out_shape=(_sds((t, BW), F32), _sds((t, BW), F32), _sds((t, BW), F32), _sds(ball.shape, F32)), name="na_bwd",
        compiler_params=_cp())(dbr, q, k, v, ball)


def _rpb_expand(rpb_pad, onehot):
    def body(r_ref, e_ref, o_ref):
        o_ref[...] = jnp.dot(r_ref[...], e_ref[...], precision=HI, preferred_element_type=F32)

    return pl.pallas_call(body, out_shape=_sds((rpb_pad.shape[0], GRID_W * GRID_W), F32), name="rpb_expand",
                          compiler_params=_cp())(rpb_pad, onehot)


def _rpb_reduce(dtab, onehot):
    def body(d_ref, e_ref, o_ref):
        o_ref[...] = lax.dot_general(d_ref[...], e_ref[...], (((1,), (1,)), ((), ())), precision=HI, preferred_element_type=F32)

    return pl.pallas_call(body, out_shape=_sds((dtab.shape[0], 128), F32), name="rpb_reduce", compiler_params=_cp())(dtab, onehot)


MEM_TQ = 256


def _mem_fwd(q, mk, mv):
    t = q.shape[0]
    tq = MEM_TQ

    def body(q_ref, k_ref, v_ref, o_ref):
        p = _softmax_rows(_dot(_stack_heads(q_ref[...]), k_ref[...], tb=True) * (HD ** -0.5))
        o_ref[...] = _unstack_heads(_dot(p, v_ref[...]), tq).astype(BF16)

    blk = pl.BlockSpec((tq, BW), lambda i: (i, 0))
    kv = pl.BlockSpec((N_MEM, BW), lambda i: (0, 0))
    return pl.pallas_call(body, grid=(t // tq,), in_specs=[blk, kv, kv], out_specs=blk, out_shape=_sds((t, BW), BF16),
                          name="mem_fwd", compiler_params=_cp())(q, mk, mv)


def _mem_bwd(dbr, q, mk, mv):
    t = q.shape[0]
    tq = MEM_TQ

    def body(d_ref, q_ref, k_ref, v_ref, dq_ref, dk_ref, dv_ref):
        first = pl.program_id(0) == 0
        qs = _stack_heads(q_ref[...])
        dos = _stack_heads(d_ref[...]).astype(MXU)
        p = _softmax_rows(_dot(qs, k_ref[...], tb=True) * (HD ** -0.5))
        dp = _dot(dos, v_ref[...], tb=True)
        _acc(dv_ref, _dot(p, dos, ta=True), first)
        dsb = (p * (dp - jnp.sum(dp * p, axis=-1, keepdims=True)) * (HD ** -0.5)).astype(MXU)
        dq_ref[...] = _unstack_heads(_dot(dsb, k_ref[...]), tq)
        _acc(dk_ref, _dot(dsb, qs, ta=True), first)

    blk = pl.BlockSpec((tq, BW), lambda i: (i, 0))
    kv = pl.BlockSpec((N_MEM, BW), lambda i: (0, 0))
    return pl.pallas_call(
        body, grid=(t // tq,), in_specs=[pl.BlockSpec((tq, BW), lambda i: (i, 3)), blk, kv, kv], out_specs=(blk, kv, kv),
        out_shape=(_sds((t, BW), F32), _sds((N_MEM, BW), F32), _sds((N_MEM, BW), F32)), name="mem_bwd",
        compiler_params=_cp())(dbr, q, mk, mv)


def _memkv_prep(kv, g_mk):
    def body(kv_ref, g_ref, k_ref, v_ref):
        k_ref[...] = _gnorm(kv_ref[:, 0:BW], g_ref[...]).astype(BF16)
        v_ref[...] = kv_ref[:, BW:2 * BW].astype(BF16)

    return pl.pallas_call(body, out_shape=(_sds((N_MEM, BW), BF16), _sds((N_MEM, BW), BF16)), name="memkv_prep",
                          compiler_params=_cp())(kv, g_mk)


def _memkv_bwd(kv, dk, dv, g_mk):
    def body(kv_ref, dk_ref, dv_ref, g_ref, o_ref, dg_ref):
        dkk, gain = _gnorm_bwd(dk_ref[...], kv_ref[:, 0:BW], g_ref[...])
        o_ref[:, 0:BW] = dkk.astype(BF16)
        o_ref[:, BW:2 * BW] = dv_ref[...].astype(BF16)
        dg_ref[...] = jnp.sum(gain, axis=0, keepdims=True)

    return pl.pallas_call(body, out_shape=(_sds((N_MEM, 2 * BW), BF16), _sds((1, BW), F32)), name="memkv_bwd",
                          compiler_params=_cp())(kv, dk, dv, g_mk)


MERGE_TM = 256


def _merge_fwd(brs, wbt, gp):
    t = gp.shape[0]
    tm = MERGE_TM

    def body(b0, b1, b2, b3, wb_ref, gp_ref, o_ref):
        out = jnp.zeros((tm, D), F32)
        for n, b_ref in enumerate((b0, b1, b2, b3)):
            up = _dot(b_ref[...], wb_ref[n], tb=True)
            out = out + _sigmoid(gp_ref[:, n * D:(n + 1) * D].astype(F32)) * up
        o_ref[...] = out.astype(BF16)

    blk = pl.BlockSpec((tm, BW), lambda i: (i, 0))
    return pl.pallas_call(
        body, grid=(t // tm,),
        in_specs=[blk, blk, blk, blk, pl.BlockSpec((NH, D, BW), lambda i: (0, 0, 0)), pl.BlockSpec((tm, NH * D), lambda i: (i, 0))],
        out_specs=pl.BlockSpec((tm, D), lambda i: (i, 0)), out_shape=_sds((t, D), BF16), name="merge_fwd",
        compiler_params=_cp())(*brs, wbt, gp)


def _merge_bwd(dmerged, brs, wbt, gp):
    t = gp.shape[0]
    tm = MERGE_TM

    def body(d_ref, b0, b1, b2, b3, wb_ref, gp_ref, dgp_ref, dup_ref):
        dm = d_ref[...]
        for n, b_ref in enumerate((b0, b1, b2, b3)):
            up = _dot(b_ref[...], wb_ref[n], tb=True)
            g = _sigmoid(gp_ref[:, n * D:(n + 1) * D].astype(F32))
            dgp_ref[:, n * D:(n + 1) * D] = (dm * up * (g * (1.0 - g))).astype(BF16)
            dup_ref[:, n * D:(n + 1) * D] = (dm * g).astype(BF16)

    row = pl.BlockSpec((tm, D), lambda i: (i, 0))
    blk = pl.BlockSpec((tm, BW), lambda i: (i, 0))
    wide = pl.BlockSpec((tm, NH * D), lambda i: (i, 0))
    return pl.pallas_call(
        body, grid=(t // tm,), in_specs=[row, blk, blk, blk, blk, pl.BlockSpec((NH, D, BW), lambda i: (0, 0, 0)), wide],
        out_specs=(wide, wide), out_shape=(_sds((t, NH * D), BF16), _sds((t, NH * D), BF16)), name="merge_bwd",
        compiler_params=_cp())(dmerged, *brs, wbt, gp)


def _dbranch(dup, wbt):
    t = dup.shape[0]
    tm = 1024

    def body(d_ref, w_ref, o_ref):
        o_ref[...] = _dot(d_ref[...], w_ref[...])

    return pl.pallas_call(
        body, grid=(t // tm, NH), in_specs=[pl.BlockSpec((tm, D), lambda i, n: (i, n)), pl.BlockSpec((None, D, BW), lambda i, n: (n, 0, 0))],
        out_specs=pl.BlockSpec((tm, BW), lambda i, n: (i, n)), out_shape=_sds((t, NH * BW), F32), name="dbranch",
        compiler_params=_cp())(dup, wbt)


def _dwbranch(brs, dup):
    t = dup.shape[0]

    def body(b0, b1, b2, b3, d_ref, o_ref):
        for n, b_ref in enumerate((b0, b1, b2, b3)):
            o_ref[n] = _dot(d_ref[:, n * D:(n + 1) * D], b_ref[...], ta=True).astype(BF16)

    return pl.pallas_call(body, out_shape=_sds((NH, D, BW), BF16), name="dwbranch", compiler_params=_cp())(*brs, dup)


FFN_TN = 256


def _ffn_in_fwd(h2, w_t):
    t = h2.shape[0]
    tm, tn = _tile(t, 1024), FFN_TN
    nj = FF // tn

    def body(x_ref, wa_ref, wg_ref, a_ref, g_ref, y_ref):
        x = x_ref[...]
        a, g = _dot(x, wa_ref[...], tb=True), _dot(x, wg_ref[...], tb=True)
        a_ref[...] = a.astype(BF16)
        g_ref[...] = g.astype(BF16)
        y_ref[...] = (a * _sigmoid(a) * g).astype(BF16)

    out = pl.BlockSpec((tm, tn), lambda i, j: (i, j))
    return pl.pallas_call(
        body, grid=(t // tm, nj),
        in_specs=[pl.BlockSpec((tm, D), lambda i, j: (i, 0)), pl.BlockSpec((tn, D), lambda i, j: (j, 0)),
                  pl.BlockSpec((tn, D), lambda i, j: (j + nj, 0))],
        out_specs=(out, out, out), out_shape=tuple(_sds((t, FF), BF16) for _ in range(3)), name="ffn_in_fwd",
        compiler_params=_cp(dimension_semantics=("parallel", "parallel")))(h2, w_t, w_t)


def _ffn_out_bwd(dx2b, w_out, a, g, dep):
    t = dx2b.shape[0]
    tm, tn = _tile(t, 1024), FFN_TN

    def body(*refs):
        x_ref, w_ref, a_ref, g_ref = refs[:4]
        da_ref, dg_ref = refs[-2:]
        d = _dot(x_ref[...], w_ref[...], tb=True)
        av, gv = a_ref[...].astype(F32), g_ref[...].astype(F32)
        s = _sigmoid(av)
        da_ref[...] = (d * gv * (s * (1.0 + av * (1.0 - s)))).astype(BF16)
        dg_ref[...] = (d * (av * s)).astype(BF16)

    blk = pl.BlockSpec((tm, tn), lambda i, j: (i, j))
    ins = [pl.BlockSpec((tm, D), lambda i, j: (i, 0)), pl.BlockSpec((tn, D), lambda i, j: (j, 0)), blk, blk]
    args = [dx2b, w_out, a, g]
    if dep is not None:
        ins.append(pl.BlockSpec((8, 128), lambda i, j: (0, 0)))
        args.append(dep)
    return pl.pallas_call(
        body, grid=(t // tm, FF // tn), in_specs=ins, out_specs=(blk, blk),
        out_shape=(_sds((t, FF), BF16), _sds((t, FF), BF16)), name="ffn_out_bwd",
        compiler_params=_cp(dimension_semantics=("parallel", "parallel")))(*args)


def _loss_head(y, target):
    t, d = y.shape
    tm = 256

    def body(y_ref, t_ref, dy_ref, dyb_ref, l_ref):
        e = y_ref[...] - t_ref[...]
        dy_ref[...] = e * (1.0 / d)
        dyb_ref[...] = (e * (1.0 / d)).astype(BF16)
        _acc(l_ref, jnp.full((8, 128), 0.5 * jnp.sum(jnp.sum(e * e, axis=-1, keepdims=True) * (1.0 / d)), F32), pl.program_id(0) == 0)

    row = pl.BlockSpec((tm, d), lambda i: (i, 0))
    return pl.pallas_call(body, grid=(t // tm,), in_specs=[row, row], out_specs=(row, row, pl.BlockSpec((8, 128), lambda i: (0, 0))),
                          out_shape=(_sds((t, d), F32), _sds((t, d), BF16), _sds((8, 128), F32)), name="loss_head",
                          compiler_params=_cp())(y, target)


def _sum_slots(x, name):
    k, r, c = x.shape
    tr = _tile(r, 512) if r % 128 == 0 else r

    def body(x_ref, o_ref):
        acc = x_ref[0].astype(F32)
        for s in range(1, k):
            acc = acc + x_ref[s].astype(F32)
        o_ref[...] = acc

    return pl.pallas_call(body, grid=(r // tr,), in_specs=[pl.BlockSpec((k, tr, c), lambda i: (0, i, 0))],
                          out_specs=pl.BlockSpec((tr, c), lambda i: (i, 0)), out_shape=_sds((r, c), F32), name=name,
                          compiler_params=_cp())(x)


def _pair_sum(bufs, recvs, cidx):
    n = len(bufs)

    def body(c_ref, *refs):
        for i in range(n):
            refs[2 * n + i][...] = (refs[i][...].astype(F32) + refs[n + i][...].astype(F32)).astype(BF16)

    return pl.pallas_call(
        body,
        grid_spec=pltpu.PrefetchScalarGridSpec(
            num_scalar_prefetch=1, grid=(4,),
            in_specs=[pl.BlockSpec((None, None) + b.shape[2:], lambda s, cref: (s, cref[0], 0, 0)) for b in bufs]
            + [pl.BlockSpec((None,) + r.shape[1:], lambda s, cref: (s, 0, 0)) for r in recvs],
            out_specs=tuple(pl.BlockSpec((None,) + r.shape[1:], lambda s, cref: (s, 0, 0)) for r in recvs)),
        out_shape=tuple(_sds(r.shape, BF16) for r in recvs), name="rs_pair_sum", compiler_params=_cp())(cidx, *bufs, *recvs)


def _adamw_update(w, gv, m, v):
    mn = ADAM_B1 * m + (1.0 - ADAM_B1) * gv
    vn = ADAM_B2 * v + (1.0 - ADAM_B2) * (gv * gv)
    m_hat = mn / (1.0 - ADAM_B1 ** ADAM_STEP)
    v_hat = vn / (1.0 - ADAM_B2 ** ADAM_STEP)
    return -ADAM_LR * (m_hat / (jnp.sqrt(v_hat) + ADAM_EPS) + ADAM_WD * w), mn, vn


def _adamw(w, g, m, v, name):
    r, c = w.shape

    def body(w_ref, g_ref, m_ref, v_ref, d_ref, nm_ref, nv_ref):
        d_ref[...], nm_ref[...], nv_ref[...] = _adamw_update(w_ref[...], g_ref[...], m_ref[...], v_ref[...])

    blk = pl.BlockSpec((r, c), lambda i: (0, 0))
    return pl.pallas_call(body, grid=(1,), in_specs=[blk] * 4, out_specs=(blk,) * 3,
                          out_shape=tuple(_sds((r, c), F32) for _ in range(3)), name=name, compiler_params=_cp())(w, g, m, v)


def _adamw_layer(layer, w, g, m, v, outs, name):
    _, r, c = w.shape
    tr = max(d for d in range(8, r + 1, 8) if r % d == 0 and d * c * 4 <= 2 ** 20)

    def body(w_ref, m_ref, v_ref, g_ref, *refs):
        d_ref, nm_ref, nv_ref, go_ref = refs[4:]
        gv = g_ref[...]
        d_ref[...], nm_ref[...], nv_ref[...] = _adamw_update(w_ref[...], gv, m_ref[...], v_ref[...])
        go_ref[...] = gv

    blk = pl.BlockSpec((None, tr, c), lambda i: (layer, i, 0))
    return pl.pallas_call(
        body, grid=(r // tr,), in_specs=[blk] * 3 + [pl.BlockSpec((tr, c), lambda i: (i, 0))] + [ANY] * 4, out_specs=(blk,) * 4,
        out_shape=tuple(_sds(w.shape, F32) for _ in range(4)), input_output_aliases={4 + j: j for j in range(4)}, name=name,
        compiler_params=_cp())(w, m, v, g, *outs)


def _all_gather(shards, name):
    n = len(shards)

    def body(*refs):
        x_refs, out_refs = refs[:n], refs[n:2 * n]
        send_sems, recv_sems, local_sems = refs[2 * n:]
        x, y, cc = lax.axis_index("x"), lax.axis_index("y"), lax.axis_index("c")
        me, sibling = (x, y, cc), (x, y, 1 - cc)
        chips = [(1 - x, y), (x, 1 - y), (1 - x, 1 - y)]

        def copy(i, k, block, to, own=False):
            px, py, pc = block
            slot = out_refs[i].at[4 * px + 2 * py + pc]
            return pltpu.make_async_remote_copy(
                src_ref=x_refs[i] if own else slot, dst_ref=slot, send_sem=send_sems.at[7 * i + k],
                recv_sem=recv_sems.at[7 * i + k], device_id=to, device_id_type=MESH)

        mine = [pltpu.make_async_copy(x_refs[i], out_refs[i].at[4 * x + 2 * y + cc], local_sems.at[i]) for i in range(n)]
        for cp in mine:
            cp.start()
        first = []
        for j, chip in enumerate(chips):
            first += [copy(i, 1 + j, me, (*chip, cc), own=True) for i in range(n)]
        first += [copy(i, 0, me, sibling, own=True) for i in range(n)]
        for cp in first:
            cp.start()
        passed = []
        for j, chip in enumerate(chips):
            for i in range(n):
                copy(i, 1 + j, (*chip, cc), me).wait_recv()
                cp = copy(i, 4 + j, (*chip, cc), sibling)
                cp.start()
                passed.append(cp)
        for i in range(n):
            copy(i, 0, sibling, me).wait_recv()
        for j, chip in enumerate(chips):
            for i in range(n):
                copy(i, 4 + j, (*chip, 1 - cc), me).wait_recv()
        for cp in first + passed:
            cp.wait_send()
        for cp in mine:
            cp.wait()

    return pl.pallas_call(
        body, out_shape=tuple(_sds((N_DEV,) + s.shape, s.dtype) for s in shards), in_specs=[ANY] * n, out_specs=(ANY,) * n,
        scratch_shapes=[pltpu.SemaphoreType.DMA((7 * n,)), pltpu.SemaphoreType.DMA((7 * n,)), pltpu.SemaphoreType.DMA((n,))],
        name=name)(*shards)


def _rs_core_swap(bufs, name):
    n = len(bufs)

    def body(*refs):
        b_refs, recv_refs = refs[:n], refs[n:2 * n]
        send_sems, recv_sems = refs[2 * n:]
        x, y, cc = lax.axis_index("x"), lax.axis_index("y"), lax.axis_index("c")
        copies = [pltpu.make_async_remote_copy(
            src_ref=b_refs[i].at[s, 1 - cc], dst_ref=recv_refs[i].at[s], send_sem=send_sems.at[4 * i + s],
            recv_sem=recv_sems.at[4 * i + s], device_id=(x, y, 1 - cc), device_id_type=MESH) for i in range(n) for s in range(4)]
        for cp in copies:
            cp.start()
        for cp in copies:
            cp.wait()

    return pl.pallas_call(
        body, out_shape=tuple(_sds((4,) + b.shape[2:], b.dtype) for b in bufs), in_specs=[ANY] * n, out_specs=(ANY,) * n,
        scratch_shapes=[pltpu.SemaphoreType.DMA((4 * n,)), pltpu.SemaphoreType.DMA((4 * n,))], name=name)(*bufs)


HBM = pl.BlockSpec(memory_space=pltpu.HBM)
SEMS = pl.BlockSpec(memory_space=pltpu.SEMAPHORE)
EFFECT = pltpu.SideEffectType.DATAFLOW_SIDE_EFFECTING


def _hbm(a):
    return pltpu.HBM(a.shape, a.dtype)


def _other_chips(x, y):
    return [(1 - x, y), (x, 1 - y), (1 - x, 1 - y)]


def _ici_start(srcs, lands, mode, name):
    n = len(srcs)

    def body(*refs):
        s_refs, land_refs = refs[:n], refs[n:2 * n]
        send_sems, recv_sems = refs[2 * n], refs[2 * n + 1]
        token = refs[-1]
        x, y, cc = lax.axis_index("x"), lax.axis_index("y"), lax.axis_index("c")
        mine = 2 * x + y if mode == "by_chip" else 4 * x + 2 * y + cc
        peers = [(px, py, cc) for px, py in _other_chips(x, y)]
        if mode == "by_device":
            peers = [(x, y, 1 - cc)] + peers + [(px, py, 1 - cc) for px, py in _other_chips(x, y)]
        for px, py, pc in peers:
            for i in range(n):
                src = s_refs[i]
                if mode == "by_chip":
                    src = src.at[2 * px + py]
                elif mode == "by_device":
                    src = src.at[4 * px + 2 * py + pc]
                pltpu.make_async_remote_copy(
                    src_ref=src, dst_ref=land_refs[i].at[mine], send_sem=send_sems.at[i], recv_sem=recv_sems.at[i],
                    device_id=(px, py, pc), device_id_type=MESH).start()
        token[...] = jnp.zeros_like(token)

    out = pl.pallas_call(
        body, name=name,
        out_shape=(pltpu.SemaphoreType.DMA((n,)), pltpu.SemaphoreType.DMA((n,)), *[_hbm(s) for s in srcs], *[_hbm(l) for l in lands],
                   _sds((8, 128), F32)),
        in_specs=[HBM] * (2 * n), out_specs=(SEMS, SEMS, *[HBM] * (2 * n), pl.BlockSpec(memory_space=pltpu.VMEM)),
        input_output_aliases={i: 2 + i for i in range(2 * n)}, compiler_params=pltpu.CompilerParams(has_side_effects=EFFECT),
    )(*[pltpu.with_memory_space_constraint(s, pltpu.HBM) for s in srcs],
      *[pltpu.with_memory_space_constraint(l, pltpu.HBM) for l in lands])
    return out[0], out[1], out[2:2 + n], out[2 + n:2 + 2 * n], out[-1], 7 if mode == "by_device" else 3


def _ici_wait(started, after, name, only=None):
    send_sems, recv_sems, srcs, lands, _, copies = started
    only = list(range(len(srcs))) if only is None else only
    srcs, lands = [srcs[i] for i in only], [lands[i] for i in only]
    n = len(srcs)

    def body(*refs):
        land_refs = refs[n:2 * n]
        send_sems, recv_sems = refs[2 * n], refs[2 * n + 1]
        x, y, cc = lax.axis_index("x"), lax.axis_index("y"), lax.axis_index("c")
        for i in range(n):
            three = land_refs[i].at[pl.ds(0, copies)]
            cp = pltpu.make_async_remote_copy(src_ref=three, dst_ref=three, send_sem=send_sems.at[only[i]],
                                              recv_sem=recv_sems.at[only[i]],
                                              device_id=(x, y, cc), device_id_type=MESH)
            cp.wait_send()
            cp.wait_recv()

    return pl.pallas_call(
        body, name=name, out_shape=tuple(_hbm(l) for l in lands), in_specs=[HBM] * (2 * n) + [SEMS, SEMS, ANY],
        out_specs=tuple([HBM] * n), input_output_aliases={n + i: i for i in range(n)},
        compiler_params=pltpu.CompilerParams(has_side_effects=EFFECT))(*srcs, *lands, send_sems, recv_sems, after)


def _gather_d2d(blocks, lands, name):
    n = len(blocks)

    def body(*refs):
        x_refs, land_refs = refs[:n], refs[2 * n:3 * n]
        send_sems, recv_sems, in_sems, out_sems = refs[3 * n:3 * n + 4]
        stage = refs[3 * n + 4:]
        x, y, cc = lax.axis_index("x"), lax.axis_index("y"), lax.axis_index("c")
        sibling = (x, y, 1 - cc)
        staged = [pltpu.make_async_copy(x_refs[i], stage[i], in_sems.at[i]) for i in range(n)]
        for cp in staged:
            cp.start()
        copies = []
        for i in range(n):
            slot = land_refs[i].at[4 * x + 2 * y + cc]
            copies.append(pltpu.make_async_remote_copy(src_ref=x_refs[i], dst_ref=slot, send_sem=send_sems.at[4 * i],
                                                       recv_sem=recv_sems.at[4 * i], device_id=sibling, device_id_type=MESH))
            for j, (px, py) in enumerate(_other_chips(x, y)):
                slot = land_refs[i].at[4 * px + 2 * py + cc]
                copies.append(pltpu.make_async_remote_copy(src_ref=slot, dst_ref=slot, send_sem=send_sems.at[4 * i + 1 + j],
                                                           recv_sem=recv_sems.at[4 * i + 1 + j], device_id=sibling, device_id_type=MESH))
        for cp in copies:
            cp.start()
        mine = []
        for i in range(n):
            staged[i].wait()
            mine.append(pltpu.make_async_copy(stage[i], land_refs[i].at[4 * x + 2 * y + cc], out_sems.at[i]))
            mine[i].start()
        for i in range(n):
            slot = land_refs[i].at[4 * x + 2 * y + (1 - cc)]
            pltpu.make_async_remote_copy(src_ref=slot, dst_ref=slot, send_sem=send_sems.at[4 * i], recv_sem=recv_sems.at[4 * i],
                                         device_id=sibling, device_id_type=MESH).wait_recv()
            for j, (px, py) in enumerate(_other_chips(x, y)):
                slot = land_refs[i].at[4 * px + 2 * py + (1 - cc)]
                pltpu.make_async_remote_copy(src_ref=slot, dst_ref=slot, send_sem=send_sems.at[4 * i + 1 + j],
                                             recv_sem=recv_sems.at[4 * i + 1 + j], device_id=sibling, device_id_type=MESH).wait_recv()
        for cp in copies:
            cp.wait_send()
        for cp in mine:
            cp.wait()

    return pl.pallas_call(
        body, out_shape=tuple(_sds(l.shape, l.dtype) for l in lands), in_specs=[ANY] * (2 * n), out_specs=(ANY,) * n,
        input_output_aliases={n + i: i for i in range(n)},
        scratch_shapes=[pltpu.SemaphoreType.DMA((4 * n,)), pltpu.SemaphoreType.DMA((4 * n,)), pltpu.SemaphoreType.DMA((n,)),
                        pltpu.SemaphoreType.DMA((n,))] + [pltpu.VMEM(b.shape, b.dtype) for b in blocks],
        name=name, compiler_params=_cp())(*blocks, *lands)


def _sum_own(parts, recvs, mine, name):
    n = len(parts)

    def body(c_ref, *refs):
        s = pl.program_id(0)
        for i in range(n):
            val = jnp.where(c_ref[0] == s, refs[i][...], refs[n + i][...]).astype(F32)
            _acc(refs[2 * n + i], val, s == 0)

    kept = [pl.BlockSpec((None,) + p.shape[1:], lambda s, cref: (cref[0], 0, 0)) for p in parts]
    ins = [pl.BlockSpec((None,) + p.shape[1:], lambda s, cref: (s, 0, 0)) for p in parts]
    return pl.pallas_call(
        body, grid_spec=pltpu.PrefetchScalarGridSpec(
            num_scalar_prefetch=1, grid=(parts[0].shape[0],), in_specs=kept + ins,
            out_specs=tuple(pl.BlockSpec(p.shape[1:], lambda s, cref: (0, 0)) for p in parts)),
        out_shape=tuple(_sds(p.shape[1:], F32) for p in parts), name=name, compiler_params=_cp())(mine, *parts, *recvs)


BIG = (("w_in", True), ("w_gate", True), ("w_mem_kv", False), ("w_branch", True), ("w_out", False), ("w_ffn_in", True),
       ("w_ffn_out", False))

SMALL = ("norm_mix_g", "norm_mem_g", "ret_decay_fwd", "ret_decay_bwd", "ret_norm_g", "pool_w", "pool_scale", "na_q_norm_g",
         "na_k_norm_g", "na_rpb", "mem_q_norm_g", "mem_k_norm_g", "norm_ffn_g")
WEIGHTS = ("norm_mix_g", "norm_mem_g", "w_in", "w_gate", "ret_decay_fwd", "ret_decay_bwd", "ret_norm_g", "pool_w", "pool_scale",
           "na_q_norm_g", "na_k_norm_g", "na_rpb", "mem_q_norm_g", "mem_k_norm_g", "w_mem_kv", "w_branch", "w_out", "norm_ffn_g",
           "w_ffn_in", "w_ffn_out")


def _to_exchange(name, transposed, shard):
    if name == "w_branch":
        return jnp.swapaxes(shard, 1, 2).reshape(NH * (D // N_DEV), BW)
    return shard.T if transposed else shard


def _from_exchange(name, transposed, block):
    if name == "w_branch":
        return jnp.swapaxes(block.reshape(NH, D // N_DEV, BW), 1, 2)
    return block.T if transposed else block


def _whole_from_gathered(name, g):
    if name == "w_branch":
        return jnp.swapaxes(g.reshape(N_DEV, NH, D // N_DEV, BW), 0, 1).reshape(NH, D, BW)
    return g.reshape(N_DEV * g.shape[1], g.shape[2])


def _by_destination(name, g):
    if name == "w_branch":
        g = jnp.swapaxes(g.reshape(NH, N_DEV, D // N_DEV, BW), 0, 1).reshape(N_DEV * NH * (D // N_DEV), BW)
    return g.reshape(4, 2, g.shape[0] // N_DEV, g.shape[1])


SMALL_PAD = 1024


def _pack_small(vals, loss=None):
    parts = [vals[n] for n in SMALL] + [jnp.zeros((1,), F32) if loss is None else loss.reshape(1)]
    rows = []
    for p in parts:
        flat = p.reshape(-1)
        rows.append(jnp.pad(flat, (0, -flat.shape[0] % SMALL_PAD)).reshape(-1, 128))
    return jnp.concatenate(rows, axis=0)


def _unpack_small(packed, like):
    out, off = {}, 0
    for n in SMALL:
        sz = int(np.prod(like[n].shape))
        nrow = -(-sz // SMALL_PAD) * (SMALL_PAD // 128)
        out[n] = packed[off:off + nrow].reshape(-1)[:sz].reshape(like[n].shape)
        off += nrow
    return out, packed[off, 0]


def _na_constants():
    c = np.arange(GRID_W)
    win = np.clip(c - NA_COLS_WIN // 2, 0, GRID_W - NA_COLS_WIN)
    kc = np.arange(GRID_W)
    inside = (kc[None, :] >= win[:, None]) & (kc[None, :] < win[:, None] + NA_COLS_WIN)
    off = kc[None, :] - c[:, None] + NA_COLS_WIN - 1
    onehot = np.zeros((128, GRID_W, GRID_W), np.float32)
    for b in range(2 * NA_COLS_WIN - 1):
        onehot[b] = (off == b) & inside
    maskadd = np.where(inside, 0.0, NEG).astype(np.float32)
    return onehot.reshape(128, GRID_W * GRID_W), maskadd


def _na_bias_table(tab, maskadd):
    n_off = 2 * NA_ROWS_WIN - 1
    t4 = tab[:NH * n_off].reshape(NH, n_off, GRID_W, GRID_W) + maskadd[None, None]
    by_off = t4.transpose(1, 0, 2, 3).reshape(n_off, NH * GRID_W, GRID_W)
    return jnp.concatenate([by_off[:-1], by_off[1:]], axis=-1)


def _rotary_tables(t):
    half = HD // 2
    inv = ROPE_THETA ** (-jnp.arange(half, dtype=F32) / half)
    ang = jnp.arange(t, dtype=F32)[:, None] * inv[None, :]
    cos, sin = jnp.cos(ang), jnp.sin(ang)
    return jnp.tile(jnp.concatenate([cos, cos], axis=-1), (1, NH)), jnp.tile(jnp.concatenate([-sin, sin], axis=-1), (1, NH))


def _block_diag(pw):
    out = jnp.zeros((BW, BW), pw.dtype)
    for g in range(NH):
        out = lax.dynamic_update_slice(out, pw[g], (g * HD, g * HD))
    return out


def _tile4(g):
    return jnp.tile(g.reshape(1, HD), (1, NH))


def _layer_fwd(x, mem, sw, lw, consts, more_weights=None):
    cos2, sin2, onehot, maskadd = consts
    h = _rmsnorm_fwd(x, sw["norm_mix_g"].reshape(1, D), "norm_mix_fwd")
    proj = _mm(h, lw["w_in"], tb=True, name="mm_in")
    gp = _mm(h, lw["w_gate"], tb=True, out_dtype=BF16, name="mm_gate")
    g_naq, g_nak, g_mq = _tile4(sw["na_q_norm_g"]), _tile4(sw["na_k_norm_g"]), _tile4(sw["mem_q_norm_g"])
    rq, rk, rv, nq, nk, nv, mq = _prep_fwd(proj, cos2, sin2, g_naq, g_nak, g_mq)

    lgf, lgb = jax.nn.log_sigmoid(sw["ret_decay_fwd"]), jax.nn.log_sigmoid(sw["ret_decay_bwd"])
    g_ret = sw["ret_norm_g"].reshape(1, BW)
    o_ret, ret = _ret_fwd(rq, rk, rv, proj, lgf, lgb, g_ret)

    wbd = _block_diag(sw["pool_w"]).astype(BF16)
    p_scale = sw["pool_scale"].reshape(1, BW)
    pool = _pool_fwd(proj, wbd, p_scale)

    rpb_pad = jnp.pad(sw["na_rpb"].reshape(NH * 15, 31), ((0, 4), (0, 97)))
    ball = _na_bias_table(_rpb_expand(rpb_pad, onehot), maskadd)
    na = _na_fwd(nq, nk, nv, ball)

    memn = _rmsnorm_fwd(mem, sw["norm_mem_g"].reshape(1, D), "norm_mem_fwd")
    kv = _mm(memn, lw["w_mem_kv"], name="mm_memkv")
    g_mk = _tile4(sw["mem_k_norm_g"])
    mk, mv = _memkv_prep(kv, g_mk)
    mo = _mem_fwd(mq, mk, mv)

    br = (ret, pool, na, mo)
    merged = _merge_fwd(br, lw["w_branch"], gp)
    x1 = _mm(merged, lw["w_out"], add=x, name="mm_out")
    if more_weights is not None:
        lw.update(more_weights(x1))
    h2 = _rmsnorm_fwd(x1, sw["norm_ffn_g"].reshape(1, D), "norm_ffn_fwd")
    ffa, ffg, yff = _ffn_in_fwd(h2, lw["w_ffn_in"])
    x2 = _mm(yff, lw["w_ffn_out"], add=x1, name="mm_ffn_out")
    saved = dict(x=x, h=h, proj=proj, gp=gp, rq=rq, rk=rk, rv=rv, nq=nq, nk=nk, nv=nv, mq=mq, o_ret=o_ret, ball=ball, memn=memn,
                 kv=kv, mk=mk, mv=mv, br=br, merged=merged, x1=x1, h2=h2, ffa=ffa, ffg=ffg, yff=yff, lgf=lgf, lgb=lgb, wbd=wbd)
    return x2, saved


def _layer_bwd(dx2, dx2b, mem, sw, lw, sv, consts, dep=None):
    cos2, sin2, onehot, maskadd = consts
    gb, gs = {}, {}
    d_a, d_g = _ffn_out_bwd(dx2b, lw["w_ffn_out"], sv["ffa"], sv["ffg"], dep)
    gb["w_ffn_out"] = _mm(sv["yff"], dx2b, ta=True, out_dtype=BF16, name="mm_ffn_out_dw")
    dh2 = _mm(d_a, lw["w_ffn_in"], b_half=0, name="mm_ffn_in_dx_a")
    dh2 = _mm(d_g, lw["w_ffn_in"], b_half=1, add=dh2, name="mm_ffn_in_dx_g")
    dw_a = _mm(d_a, sv["h2"], ta=True, out_dtype=BF16, out_half=(0, None), name="mm_ffn_in_dw_a")
    gb["w_ffn_in"] = _mm(d_g, sv["h2"], ta=True, out_dtype=BF16, out_half=(1, dw_a), name="mm_ffn_in_dw_g")
    dx1, dx1b, dg = _rmsnorm_bwd(dh2, sv["x1"], sw["norm_ffn_g"].reshape(1, D), dx2, "norm_ffn_bwd")
    gs["norm_ffn_g"] = dg.reshape(D)

    dmerged = _mm(dx1b, lw["w_out"], tb=True, name="mm_out_dx")
    gb["w_out"] = _mm(sv["merged"], dx1b, ta=True, out_dtype=BF16, name="mm_out_dw")
    dgp, dup = _merge_bwd(dmerged, sv["br"], lw["w_branch"], sv["gp"])
    dbr = _dbranch(dup, lw["w_branch"])
    gb["w_branch"] = _dwbranch(sv["br"], dup)

    g_ret = sw["ret_norm_g"].reshape(1, BW)
    do_ret, d_rg, dg_ret = _ret_post_bwd(dbr, sv["o_ret"], sv["proj"], g_ret)
    d_rq, d_rk, d_rv, dlg = _ret_bwd(do_ret, sv["rq"], sv["rk"], sv["rv"], sv["lgf"], sv["lgb"])
    gs["ret_norm_g"] = dg_ret.reshape(BW)
    _, vjp_f = jax.vjp(jax.nn.log_sigmoid, sw["ret_decay_fwd"])
    _, vjp_b = jax.vjp(jax.nn.log_sigmoid, sw["ret_decay_bwd"])
    gs["ret_decay_fwd"] = vjp_f(dlg[0:NH, 0])[0]
    gs["ret_decay_bwd"] = vjp_b(dlg[NH:2 * NH, 0])[0]

    p_scale = sw["pool_scale"].reshape(1, BW)
    d_pv, dwbd, dscale = _pool_bwd(dbr, sv["proj"], sv["wbd"], p_scale)
    gs["pool_w"] = jnp.stack([dwbd[g * HD:(g + 1) * HD, g * HD:(g + 1) * HD] for g in range(NH)])
    gs["pool_scale"] = dscale.reshape(BW)

    d_nq, d_nk, d_nv, dball = _na_bwd(dbr, sv["nq"], sv["nk"], sv["nv"], sv["ball"])
    _, vjp_tab = jax.vjp(lambda tab: _na_bias_table(tab, maskadd), jnp.zeros((64, GRID_W * GRID_W), F32))
    drpb = _rpb_reduce(vjp_tab(dball)[0], onehot)
    gs["na_rpb"] = drpb[:NH * 15, :31].reshape(NH, 15, 31)

    d_mq, d_mk, d_mv = _mem_bwd(dbr, sv["mq"], sv["mk"], sv["mv"])
    g_mk = _tile4(sw["mem_k_norm_g"])
    dkv, dg_mk = _memkv_bwd(sv["kv"], d_mk, d_mv, g_mk)
    gs["mem_k_norm_g"] = dg_mk.reshape(NH, HD).sum(0)
    gb["w_mem_kv"] = _mm(sv["memn"], dkv, ta=True, out_dtype=BF16, name="mm_memkv_dw")
    dmemn = _mm(dkv, lw["w_mem_kv"], tb=True, name="mm_memkv_dx")
    _, _, dg_mem = _rmsnorm_bwd(dmemn, mem, sw["norm_mem_g"].reshape(1, D), jnp.zeros_like(mem), "norm_mem_bwd")
    gs["norm_mem_g"] = dg_mem.reshape(D)

    g_naq, g_nak, g_mq = _tile4(sw["na_q_norm_g"]), _tile4(sw["na_k_norm_g"]), _tile4(sw["mem_q_norm_g"])
    dproj, dg_naq, dg_nak, dg_mq = _prep_bwd(sv["proj"], cos2, sin2, g_naq, g_nak, g_mq, d_rq, d_rk, d_rv, d_rg, d_pv, d_nq, d_nk,
                                             d_nv, d_mq)
    gs["na_q_norm_g"] = dg_naq.reshape(NH, HD).sum(0)
    gs["na_k_norm_g"] = dg_nak.reshape(NH, HD).sum(0)
    gs["mem_q_norm_g"] = dg_mq.reshape(NH, HD).sum(0)

    dh = _mm(dproj, lw["w_in"], name="mm_in_dx")
    dh = _mm(dgp, lw["w_gate"], add=dh, name="mm_gate_dx")
    gb["w_in"] = _mm(dproj, sv["h"], ta=True, out_dtype=BF16, name="mm_in_dw")
    gb["w_gate"] = _mm(dgp, sv["h"], ta=True, out_dtype=BF16, name="mm_gate_dw")
    dx, dxb, dg = _rmsnorm_bwd(dh, sv["x"], sw["norm_mix_g"].reshape(1, D), dx1, "norm_mix_bwd")
    gs["norm_mix_g"] = dg.reshape(D)
    return dx, dxb, gb, gs


def _local_step(x, mem, target, small, get_layer, on_grads):
    t = x.shape[0]
    cos2, sin2 = _rotary_tables(t)
    onehot, maskadd = _na_constants()
    consts = (cos2, sin2, jnp.asarray(onehot), jnp.asarray(maskadd))
    saved, weights, cur = [], [], x
    for l in range(DEPTH):
        sw = {n: small[n][l] for n in SMALL}
        lw, more = get_layer(l, cur)
        weights.append(lw)
        cur, sv = _layer_fwd(cur, mem, sw, lw, consts, more)
        saved.append(sv)
    dy, dyb, loss_tile = _loss_head(cur, target)
    small_g = {n: [None] * DEPTH for n in SMALL}
    dep = None
    for l in reversed(range(DEPTH)):
        sw = {n: small[n][l] for n in SMALL}
        dy, dyb, gb, gs = _layer_bwd(dy, dyb, mem, sw, weights[l], saved[l], consts, dep)
        dep = on_grads(l, gb, dy)
        for n in SMALL:
            small_g[n][l] = gs[n]
    return loss_tile[0, 0], dy, {n: jnp.stack(v) for n, v in small_g.items()}


def _flat2d(a):
    return a.reshape(-1, a.shape[-1])


def kernel(x, mem, norm_mix_g, norm_mem_g, w_in, w_gate, ret_decay_fwd, ret_decay_bwd, ret_norm_g, pool_w, pool_scale, na_q_norm_g, na_k_norm_g, na_rpb, mem_q_norm_g, mem_k_norm_g, w_mem_kv, w_branch, w_out, norm_ffn_g, w_ffn_in, w_ffn_out, loss_target, m_norm_mix_g, m_norm_mem_g, m_w_in, m_w_gate, m_ret_decay_fwd, m_ret_decay_bwd, m_ret_norm_g, m_pool_w, m_pool_scale, m_na_q_norm_g, m_na_k_norm_g, m_na_rpb, m_mem_q_norm_g, m_mem_k_norm_g, m_w_mem_kv, m_w_branch, m_w_out, m_norm_ffn_g, m_w_ffn_in, m_w_ffn_out, v_norm_mix_g, v_norm_mem_g, v_w_in, v_w_gate, v_ret_decay_fwd, v_ret_decay_bwd, v_ret_norm_g, v_pool_w, v_pool_scale, v_na_q_norm_g, v_na_k_norm_g, v_na_rpb, v_mem_q_norm_g, v_mem_k_norm_g, v_w_mem_kv, v_w_branch, v_w_out, v_norm_ffn_g, v_w_ffn_in, v_w_ffn_out):
    w = dict(norm_mix_g=norm_mix_g, norm_mem_g=norm_mem_g, w_in=w_in, w_gate=w_gate, ret_decay_fwd=ret_decay_fwd,
             ret_decay_bwd=ret_decay_bwd, ret_norm_g=ret_norm_g, pool_w=pool_w, pool_scale=pool_scale, na_q_norm_g=na_q_norm_g,
             na_k_norm_g=na_k_norm_g, na_rpb=na_rpb, mem_q_norm_g=mem_q_norm_g, mem_k_norm_g=mem_k_norm_g, w_mem_kv=w_mem_kv,
             w_branch=w_branch, w_out=w_out, norm_ffn_g=norm_ffn_g, w_ffn_in=w_ffn_in, w_ffn_out=w_ffn_out)
    m = dict(norm_mix_g=m_norm_mix_g, norm_mem_g=m_norm_mem_g, w_in=m_w_in, w_gate=m_w_gate, ret_decay_fwd=m_ret_decay_fwd,
             ret_decay_bwd=m_ret_decay_bwd, ret_norm_g=m_ret_norm_g, pool_w=m_pool_w, pool_scale=m_pool_scale, na_q_norm_g=m_na_q_norm_g,
             na_k_norm_g=m_na_k_norm_g, na_rpb=m_na_rpb, mem_q_norm_g=m_mem_q_norm_g, mem_k_norm_g=m_mem_k_norm_g, w_mem_kv=m_w_mem_kv,
             w_branch=m_w_branch, w_out=m_w_out, norm_ffn_g=m_norm_ffn_g, w_ffn_in=m_w_ffn_in, w_ffn_out=m_w_ffn_out)
    v = dict(norm_mix_g=v_norm_mix_g, norm_mem_g=v_norm_mem_g, w_in=v_w_in, w_gate=v_w_gate, ret_decay_fwd=v_ret_decay_fwd,
             ret_decay_bwd=v_ret_decay_bwd, ret_norm_g=v_ret_norm_g, pool_w=v_pool_w, pool_scale=v_pool_scale, na_q_norm_g=v_na_q_norm_g,
             na_k_norm_g=v_na_k_norm_g, na_rpb=v_na_rpb, mem_q_norm_g=v_mem_q_norm_g, mem_k_norm_g=v_mem_k_norm_g, w_mem_kv=v_w_mem_kv,
             w_branch=v_w_branch, w_out=v_w_out, norm_ffn_g=v_norm_ffn_g, w_ffn_in=v_w_ffn_in, w_ffn_out=v_w_ffn_out)
    assert x.shape == (1, 2048, D) and mem.shape == (1, N_MEM, D) and w_in.shape == (DEPTH, D, 9 * BW // N_DEV)

    started = []
    for l in range(DEPTH):
        blocks = [_to_exchange(name, tr, w[name][l]).astype(BF16) for name, tr in BIG]
        lands = [lax.empty((N_DEV,) + b.shape, BF16) for b in blocks]
        started.append(_ici_start(blocks, lands, "gather", "gather_ici_start_%d" % l))
    all_started = started[0][4] + started[1][4] + started[2][4] + started[3][4]

    def get_group(l, only, after, tag):
        lands = _ici_wait(started[l], after, "gather_ici_wait_%d%s" % (l, tag), only)
        whole = _gather_d2d([started[l][2][i] for i in only], lands, "gather_d2d")
        return {BIG[i][0]: _whole_from_gathered(BIG[i][0], g) for i, g in zip(only, whole)}

    def get_layer(l, after):
        if l > 0:
            return get_group(l, list(range(len(BIG))), after, ""), None
        mixer = [i for i, (name, _) in enumerate(BIG) if not name.startswith("w_ffn")]
        ffn = [i for i, (name, _) in enumerate(BIG) if name.startswith("w_ffn")]
        return get_group(l, mixer, all_started, "a"), lambda after2: get_group(l, ffn, after2, "b")

    cidx = lax.axis_index("c").astype(jnp.int32).reshape(1)
    chip = (2 * lax.axis_index("x") + lax.axis_index("y")).astype(jnp.int32).reshape(1)
    in_flight = []

    def flip_of(name, tr):
        return (lambda a: jnp.swapaxes(a, 1, 2)) if name in ("w_in", "w_ffn_in") else (lambda a: a)

    def rows3(a):
        return a.reshape(DEPTH, -1, a.shape[-1])

    opt_in = {name: tuple(rows3(flip_of(name, tr)(t[name])) for t in (w, m, v)) for name, tr in BIG}
    opt_out = {name: tuple(lax.empty(opt_in[name][0].shape, F32) for _ in range(4)) for name, _ in BIG}

    device = (2 * chip + cidx).astype(jnp.int32)

    def finish(l, st, after):
        recv = _ici_wait(st, after, "rs_ici_wait_%d" % l)
        sums = _sum_own(st[2], recv, chip if st[5] == 3 else device, "rs_sum")
        for (name, tr), s in zip(BIG, sums):
            g = s if name in ("w_in", "w_ffn_in") else _from_exchange(name, tr, s)
            wx, mx, vx = opt_in[name]
            opt_out[name] = _adamw_layer(l, wx, g.reshape(-1, g.shape[-1]), mx, vx, opt_out[name], "adamw_" + name)

    def on_grads(l, gb, after):
        send = [_by_destination(name, gb[name]) for name, _ in BIG]
        if l > 0:
            send = [s.reshape((N_DEV,) + s.shape[2:]) for s in send]
            st = _ici_start(send, [lax.empty(s.shape, BF16) for s in send], "by_device", "rs_ici_start_%d" % l)
        else:
            from_core = _rs_core_swap(send, "rs_core_swap")
            chip_part = _pair_sum(send, from_core, cidx)
            st = _ici_start(chip_part, [lax.empty(p.shape, BF16) for p in chip_part], "by_chip", "rs_ici_start_%d" % l)
        in_flight.append((l, st))
        return st[4]

    loss_local, dx, small_g = _local_step(x[0], mem[0], loss_target[0], {n: w[n] for n in SMALL}, get_layer, on_grads)

    last_started = in_flight[-1][1][4]
    for l, st in in_flight[:-1]:
        finish(l, st, last_started)

    small_all, = _all_gather([_pack_small(small_g, loss_local) + last_started[0:1]], "gather_small")
    packed_g = _sum_slots(small_all, "small_sum")
    small_sum, loss = _unpack_small(packed_g, {n: w[n] for n in SMALL})
    d_, m_, v_ = _adamw(_pack_small({n: w[n] for n in SMALL}), packed_g, _pack_small({n: m[n] for n in SMALL}),
                        _pack_small({n: v[n] for n in SMALL}), "adamw_small")
    updated = d_[0:8]
    for name, _ in BIG:
        updated = updated + opt_out[name][0][1, 0:8, 0:128]
    finish(*in_flight[-1], updated)

    grads, delta, new_m, new_v = {}, {}, {}, {}
    for name, tr in BIG:
        shape = flip_of(name, tr)(w[name]).shape
        delta[name], new_m[name], new_v[name], grads[name] = (flip_of(name, tr)(a.reshape(shape)) for a in opt_out[name])
    like = {n: w[n] for n in SMALL}
    ds, _ = _unpack_small(d_, like)
    ms, _ = _unpack_small(m_, like)
    vs, _ = _unpack_small(v_, like)
    for n in SMALL:
        grads[n], delta[n], new_m[n], new_v[n] = small_sum[n], ds[n], ms[n], vs[n]

    return (loss, dx[None], *[grads[n] for n in WEIGHTS], *[delta[n] for n in WEIGHTS], *[new_m[n] for n in WEIGHTS],
            *[new_v[n] for n in WEIGHTS])
```

```python
import functools

import numpy as np
import jax
import jax.numpy as jnp
from jax import lax
from jax.experimental import pallas as pl
from jax.experimental.pallas import tpu as pltpu

F32 = jnp.float32
BF16 = jnp.bfloat16
MXU = jnp.bfloat16
HI = lax.Precision.HIGHEST

DEPTH = 4
D = 1024
BW = 256
HD = 64
NH = 4
GRID_W = 64
NA_ROWS_WIN = 8
NA_COLS_WIN = 16
N_MEM = 256
FF = 2816
EPS = 1e-6
NEG = -1e30
ROPE_THETA = 10000.0
POOL_HALF_MAX = 8

ADAM_LR, ADAM_B1, ADAM_B2, ADAM_EPS, ADAM_WD, ADAM_STEP = 0.001, 0.9, 0.999, 1e-08, 0.01, 10

N_DEV = 8
VMEM_LIMIT = 56 * 1024 * 1024

RQ, RK, RV, RG, PV, NQ, NK, NV, MQ = range(9)

MESH = pl.DeviceIdType.MESH
ANY = pl.BlockSpec(memory_space=pl.ANY)
SMEM = pl.BlockSpec(memory_space=pltpu.SMEM)


def _cp(**kw):
    return pltpu.CompilerParams(vmem_limit_bytes=VMEM_LIMIT, **kw)


def _tile(n, cap):
    if n <= cap:
        return n
    best = None
    for t in range(128, cap + 1, 128):
        if n % t == 0:
            best = t
    assert best is not None, (n, cap)
    return best


def _sds(shape, dtype):
    return jax.ShapeDtypeStruct(shape, dtype)


def _lane_head(shape):
    return lax.shift_right_logical(lax.broadcasted_iota(jnp.int32, shape, len(shape) - 1), 6)


def _group_mean(z):
    i = lax.shift_right_logical(lax.broadcasted_iota(jnp.int32, (BW, BW), 0), 6)
    j = lax.shift_right_logical(lax.broadcasted_iota(jnp.int32, (BW, BW), 1), 6)
    g = jnp.where(i == j, 1.0 / HD, 0.0).astype(BF16)
    z_hi = z.astype(BF16)
    z_lo = (z - z_hi.astype(F32)).astype(BF16)
    return jnp.dot(z_hi, g, preferred_element_type=F32) + jnp.dot(z_lo, g, preferred_element_type=F32)


def _gnorm(t, g):
    r = lax.rsqrt(_group_mean(t * t) + EPS)
    return t * r * g


def _gnorm_bwd(dy, t, g):
    r = lax.rsqrt(_group_mean(t * t) + EPS)
    th = t * r
    dth = dy * g
    dt = r * (dth - th * _group_mean(dth * th))
    return dt, dy * th


def _swap_halves(t):
    lane = lax.broadcasted_iota(jnp.int32, t.shape, 1)
    return jnp.where((lane & 63) < 32, pltpu.roll(t, BW - 32, 1), pltpu.roll(t, 32, 1))


def _sigmoid(x):
    return 1.0 / (1.0 + jnp.exp(-x))


def _dot(a, b, ta=False, tb=False):
    return lax.dot_general(a.astype(MXU), b.astype(MXU), (((0 if ta else 1,), (1 if tb else 0,)), ((), ())),
                           preferred_element_type=F32)


def _stack_heads(t):
    head = _lane_head(t.shape)
    return jnp.concatenate([jnp.where(head == h, t, jnp.zeros_like(t)) for h in range(NH)], axis=0)


def _unstack_heads(t, rows):
    head = _lane_head((rows, BW))
    out = jnp.zeros((rows, BW), F32)
    for h in range(NH):
        out = out + jnp.where(head == h, t[h * rows:(h + 1) * rows], 0.0)
    return out


def _softmax_rows(s):
    m = jnp.max(s, axis=-1, keepdims=True)
    e = jnp.exp(s - m)
    return e / jnp.sum(e, axis=-1, keepdims=True)


def _acc(ref, val, first):
    @pl.when(first)
    def _():
        ref[...] = val

    @pl.when(jnp.logical_not(first))
    def _():
        ref[...] += val


def _mm(a, b, *, ta=False, tb=False, out_dtype=F32, add=None, dep=None, b_half=None, out_half=None, norm_g=None, name):
    m, k = (a.shape[1], a.shape[0]) if ta else a.shape
    n = b.shape[0] if tb else b.shape[1]
    assert b_half is None or (not tb and b.shape[0] == 2 * k)
    tm, tn = _tile(m, 1408), (n if norm_g is not None else _tile(n, 768))
    n_in = 2 + (add is not None) + (dep is not None) + (out_half is not None) + (norm_g is not None)

    def body(*refs):
        a_ref, b_ref, o_ref = refs[0], refs[1], refs[n_in]
        r = _dot(a_ref[...], b_ref[...], ta, tb)
        if add is not None:
            r = r + refs[2][...]
        o_ref[...] = r.astype(out_dtype)
        if norm_g is not None:
            scale = lax.rsqrt(jnp.mean(r * r, axis=-1, keepdims=True) + EPS)
            refs[n_in + 1][...] = (r * scale * refs[n_in - 1][...]).astype(BF16)

    kb = 0 if b_half is None else b_half
    a_spec = pl.BlockSpec((k, tm), lambda i, j: (0, i)) if ta else pl.BlockSpec((tm, k), lambda i, j: (i, 0))
    b_spec = pl.BlockSpec((tn, k), lambda i, j: (j, 0)) if tb else pl.BlockSpec((k, tn), lambda i, j: (kb, j))
    plain = pl.BlockSpec((tm, tn), lambda i, j: (i, j))
    ins, args = [a_spec, b_spec], [a, b]
    if add is not None:
        ins.append(plain)
        args.append(add)
    if dep is not None:
        ins.append(pl.BlockSpec((8, 128), lambda i, j: (0, 0)))
        args.append(dep)
    o_spec, o_shape, aliases = plain, _sds((m, n), out_dtype), {}
    if out_half is not None:
        half, prev = out_half
        o_spec = pl.BlockSpec((tm, tn), lambda i, j: (i + half * (m // tm), j))
        o_shape = _sds((2 * m, n), out_dtype)
        ins.append(ANY)
        args.append(lax.empty((2 * m, n), out_dtype) if prev is None else prev)
        aliases = {len(args) - 1: 0}
    if norm_g is not None:
        ins.append(pl.BlockSpec((1, n), lambda i, j: (0, 0)))
        args.append(norm_g)
        o_spec, o_shape = (o_spec, plain), (o_shape, _sds((m, n), BF16))
    return pl.pallas_call(
        body, grid=(m // tm, n // tn), in_specs=ins, out_specs=o_spec, out_shape=o_shape, input_output_aliases=aliases, name=name,
        compiler_params=_cp(dimension_semantics=("parallel", "parallel")))(*args)


def _rmsnorm_fwd(x, g, name):
    t, d = x.shape
    tm = _tile(t, 256)

    def body(x_ref, g_ref, o_ref):
        xv = x_ref[...]
        r = lax.rsqrt(jnp.mean(xv * xv, axis=-1, keepdims=True) + EPS)
        o_ref[...] = (xv * r * g_ref[...]).astype(o_ref.dtype)

    return pl.pallas_call(
        body, grid=(t // tm,), in_specs=[pl.BlockSpec((tm, d), lambda i: (i, 0)), pl.BlockSpec((1, d), lambda i: (0, 0))],
        out_specs=pl.BlockSpec((tm, d), lambda i: (i, 0)), out_shape=_sds((t, d), BF16), name=name, compiler_params=_cp())(x, g)


def _rmsnorm_bwd(dh, x, g, res, name):
    t, d = x.shape
    tm = _tile(t, 256)

    def body(dh_ref, x_ref, g_ref, res_ref, dx_ref, dxb_ref, dg_ref):
        xv = x_ref[...]
        dhv = dh_ref[...]
        r = lax.rsqrt(jnp.mean(xv * xv, axis=-1, keepdims=True) + EPS)
        xh = xv * r
        dxh = dhv * g_ref[...]
        dx = res_ref[...] + r * (dxh - xh * jnp.mean(dxh * xh, axis=-1, keepdims=True))
        dx_ref[...] = dx
        dxb_ref[...] = dx.astype(BF16)
        _acc(dg_ref, jnp.sum(dhv * xh, axis=0, keepdims=True), pl.program_id(0) == 0)

    row = pl.BlockSpec((tm, d), lambda i: (i, 0))
    vec = pl.BlockSpec((1, d), lambda i: (0, 0))
    return pl.pallas_call(
        body, grid=(t // tm,), in_specs=[row, row, vec, row], out_specs=(row, row, vec),
        out_shape=(_sds((t, d), F32), _sds((t, d), BF16), _sds((1, d), F32)), name=name, compiler_params=_cp())(dh, x, g, res)


def _prep_fwd(proj, cos2, sin2, g_naq, g_nak, g_mq):
    t = proj.shape[0]
    tm = 256

    def body(p_ref, cos_ref, sin_ref, gq_ref, gk_ref, gm_ref, rq_ref, rk_ref, rv_ref, nq_ref, nk_ref, nv_ref, mq_ref):
        def col(c):
            return p_ref[:, c * BW:(c + 1) * BW]

        cosv, sinv = cos_ref[...], sin_ref[...]

        def rot(tv):
            return tv * cosv + _swap_halves(tv) * sinv

        rq_ref[...] = (rot(col(RQ)) * (HD ** -0.5)).astype(BF16)
        rk_ref[...] = rot(col(RK)).astype(BF16)
        rv_ref[...] = col(RV).astype(BF16)
        nq_ref[...] = _gnorm(col(NQ), gq_ref[...]).astype(BF16)
        nk_ref[...] = _gnorm(col(NK), gk_ref[...]).astype(BF16)
        nv_ref[...] = col(NV).astype(BF16)
        mq_ref[...] = _gnorm(col(MQ), gm_ref[...]).astype(BF16)

    blk = pl.BlockSpec((tm, BW), lambda i: (i, 0))
    vec = pl.BlockSpec((1, BW), lambda i: (0, 0))
    return pl.pallas_call(
        body, grid=(t // tm,), in_specs=[pl.BlockSpec((tm, 9 * BW), lambda i: (i, 0)), blk, blk, vec, vec, vec],
        out_specs=tuple(blk for _ in range(7)), out_shape=tuple(_sds((t, BW), BF16) for _ in range(7)),
        name="prep_fwd", compiler_params=_cp())(proj, cos2, sin2, g_naq, g_nak, g_mq)


def _prep_bwd(proj, cos2, sin2, g_naq, g_nak, g_mq, d_rq, d_rk, d_rv, d_rg, d_pv, d_nq, d_nk, d_nv, d_mq):
    t = proj.shape[0]
    tm = 256

    def body(p_ref, cos_ref, sin_ref, gq_ref, gk_ref, gm_ref, drq_ref, drk_ref, drv_ref, drg_ref, dpv_ref, dnq_ref, dnk_ref,
             dnv_ref, dmq_ref, o_ref, dgq_ref, dgk_ref, dgm_ref):
        first = pl.program_id(0) == 0

        def col(c):
            return p_ref[:, c * BW:(c + 1) * BW]

        def put(c, v):
            o_ref[:, c * BW:(c + 1) * BW] = v.astype(BF16)

        cosv, sinv = cos_ref[...], sin_ref[...]

        def rot_t(dv):
            return dv * cosv + _swap_halves(dv * sinv)

        put(RQ, rot_t(drq_ref[...] * (HD ** -0.5)))
        put(RK, rot_t(drk_ref[...]))
        put(RV, drv_ref[...])
        put(RG, drg_ref[...])
        put(PV, dpv_ref[...])
        dq, gq = _gnorm_bwd(dnq_ref[...], col(NQ), gq_ref[...])
        put(NQ, dq)
        _acc(dgq_ref, jnp.sum(gq, axis=0, keepdims=True), first)
        dk, gk = _gnorm_bwd(dnk_ref[...], col(NK), gk_ref[...])
        put(NK, dk)
        _acc(dgk_ref, jnp.sum(gk, axis=0, keepdims=True), first)
        put(NV, dnv_ref[...])
        dm, gm = _gnorm_bwd(dmq_ref[...], col(MQ), gm_ref[...])
        put(MQ, dm)
        _acc(dgm_ref, jnp.sum(gm, axis=0, keepdims=True), first)

    blk = pl.BlockSpec((tm, BW), lambda i: (i, 0))
    vec = pl.BlockSpec((1, BW), lambda i: (0, 0))
    wide = pl.BlockSpec((tm, 9 * BW), lambda i: (i, 0))
    return pl.pallas_call(
        body, grid=(t // tm,), in_specs=[wide, blk, blk, vec, vec, vec] + [blk] * 9, out_specs=(wide, vec, vec, vec),
        out_shape=(_sds((t, 9 * BW), BF16), _sds((1, BW), F32), _sds((1, BW), F32), _sds((1, BW), F32)),
        name="prep_bwd", compiler_params=_cp())(proj, cos2, sin2, g_naq, g_nak, g_mq, d_rq, d_rk, d_rv, d_rg, d_pv, d_nq, d_nk,
                                                d_nv, d_mq)


RET_B = 256


def _ret_consts(lgf_ref, lgb_ref):
    bsz = RET_B
    head = _lane_head((1, BW))
    lf, lb = jnp.zeros((1, BW), F32), jnp.zeros((1, BW), F32)
    for h in range(NH):
        lf = lf + jnp.where(head == h, lgf_ref[h], 0.0)
        lb = lb + jnp.where(head == h, lgb_ref[h], 0.0)
    pos = lax.broadcasted_iota(jnp.int32, (bsz, BW), 0).astype(F32)
    up, down = pos + 1.0, (bsz - 1.0) - pos
    c = dict(up=up, down=down, kf=jnp.exp(down * lf), kb=jnp.exp(up * lb), qf=jnp.exp(up * lf), qb=jnp.exp(down * lb),
             cf=jnp.exp(bsz * lf), cb=jnp.exp(bsz * lb))
    diff = (lax.broadcasted_iota(jnp.int32, (NH * bsz, 1), 0) & (bsz - 1)) - lax.broadcasted_iota(jnp.int32, (1, bsz), 1)
    c["causal"] = diff >= 0
    c["dist"] = jnp.abs(diff).astype(F32)
    lgf = jnp.concatenate([jnp.full((bsz, 1), lgf_ref[h], F32) for h in range(NH)], axis=0)
    lgb = jnp.concatenate([jnp.full((bsz, 1), lgb_ref[h], F32) for h in range(NH)], axis=0)
    c["dm"] = jnp.exp(c["dist"] * jnp.where(c["causal"], lgf, lgb))
    c["bd"] = _lane_head((BW, BW)) == lax.shift_right_logical(lax.broadcasted_iota(jnp.int32, (BW, BW), 0), 6)
    return c


def _ret_states(k_ref, v_ref, st_ref, c, nb):
    bsz = RET_B

    def summary(b, decay):
        kb = k_ref[b * bsz:(b + 1) * bsz, :].astype(F32)
        return jnp.where(c["bd"], _dot(kb * decay, v_ref[b * bsz:(b + 1) * bsz, :], ta=True), 0.0)

    f = jnp.zeros((BW, BW), F32)
    for b in range(nb):
        st_ref[b] = f
        if b < nb - 1:
            f = c["cf"] * f + summary(b, c["kf"])
    g = jnp.zeros((BW, BW), F32)
    for b in reversed(range(nb)):
        st_ref[nb + b] = g
        if b > 0:
            g = c["cb"] * g + summary(b, c["kb"])


def _ret_fwd(q, k, v, proj, lgf, lgb, g_ret):
    t = q.shape[0]
    bsz, nb = RET_B, t // RET_B

    def body(lgf_ref, lgb_ref, q_ref, k_ref, v_ref, rg_ref, g_ref, o_ref, ret_ref, st_ref):
        c = _ret_consts(lgf_ref, lgb_ref)
        _ret_states(k_ref, v_ref, st_ref, c, nb)
        for b in range(nb):
            blk = slice(b * bsz, (b + 1) * bsz)
            qb, kb, vb = q_ref[blk, :], k_ref[blk, :], v_ref[blk, :]
            s = _dot(_stack_heads(qb), kb, tb=True)
            o = _unstack_heads(_dot(s * c["dm"], vb), bsz)
            q32 = qb.astype(F32)
            o = o + _dot(q32 * c["qf"], st_ref[b]) + _dot(q32 * c["qb"], st_ref[nb + b])
            o_ref[blk, :] = o
            rg = rg_ref[blk, :]
            ret_ref[blk, :] = (_gnorm(o, g_ref[...]) * (rg * _sigmoid(rg))).astype(BF16)

    whole = pl.BlockSpec((t, BW), lambda i: (0, 0))
    return pl.pallas_call(
        body, grid=(1,),
        in_specs=[SMEM, SMEM, whole, whole, whole, pl.BlockSpec((t, BW), lambda i: (0, RG)), pl.BlockSpec((1, BW), lambda i: (0, 0))],
        out_specs=(whole, whole), out_shape=(_sds((t, BW), F32), _sds((t, BW), BF16)),
        scratch_shapes=[pltpu.VMEM((2 * nb, BW, BW), F32)], name="ret_fwd", compiler_params=_cp())(lgf, lgb, q, k, v, proj, g_ret)


def _ret_post_bwd(dbr, o_ret, proj, g_ret):
    t = o_ret.shape[0]
    tm = 256

    def body(d_ref, o_ref, rg_ref, g_ref, do_ref, drg_ref, dg_ref):
        dret, o, rg, g = d_ref[...], o_ref[...], rg_ref[...], g_ref[...]
        sg = _sigmoid(rg)
        do, dgain = _gnorm_bwd(dret * (rg * sg), o, g)
        do_ref[...] = do.astype(BF16)
        drg_ref[...] = dret * _gnorm(o, g) * (sg * (1.0 + rg * (1.0 - sg)))
        _acc(dg_ref, jnp.sum(dgain, axis=0, keepdims=True), pl.program_id(0) == 0)

    blk = pl.BlockSpec((tm, BW), lambda i: (i, 0))
    vec = pl.BlockSpec((1, BW), lambda i: (0, 0))
    return pl.pallas_call(
        body, grid=(t // tm,), in_specs=[blk, blk, pl.BlockSpec((tm, BW), lambda i: (i, RG)), vec], out_specs=(blk, blk, vec),
        out_shape=(_sds((t, BW), BF16), _sds((t, BW), F32), _sds((1, BW), F32)), name="ret_post_bwd",
        compiler_params=_cp())(dbr, o_ret, proj, g_ret)


def _ret_bwd(do, q, k, v, lgf, lgb):
    t = q.shape[0]
    bsz, nb = RET_B, t // RET_B

    def body(lgf_ref, lgb_ref, d_ref, q_ref, k_ref, v_ref, dq_ref, dk_ref, dv_ref, dlg_ref, st_ref, sd_ref):
        c = _ret_consts(lgf_ref, lgb_ref)
        _ret_states(k_ref, v_ref, st_ref, c, nb)
        lane_f, lane_b = jnp.zeros((1, BW), F32), jnp.zeros((1, BW), F32)
        row_f, row_b = jnp.zeros((NH * bsz, 1), F32), jnp.zeros((NH * bsz, 1), F32)

        def rows(x):
            return jnp.sum(x, axis=0, keepdims=True)

        for b in range(nb):
            blk = slice(b * bsz, (b + 1) * bsz)
            qb, kb, vb, dob = q_ref[blk, :], k_ref[blk, :], v_ref[blk, :], d_ref[blk, :]
            q32 = qb.astype(F32)
            qs, dos = _stack_heads(qb), _stack_heads(dob)
            s = _dot(qs, kb, tb=True)
            da = _dot(dos, vb, tb=True)
            dv_ref[blk, :] = _dot(s * c["dm"], dos, ta=True)
            ds = da * c["dm"]
            w = ds * s * c["dist"]
            row_f = row_f + jnp.sum(jnp.where(c["causal"], w, 0.0), axis=1, keepdims=True)
            row_b = row_b + jnp.sum(jnp.where(c["causal"], 0.0, w), axis=1, keepdims=True)
            dsb = ds.astype(MXU)
            dk_ref[blk, :] = _dot(dsb, qs, ta=True)
            dq_f = _dot(dob, st_ref[b], tb=True) * c["qf"]
            dq_b = _dot(dob, st_ref[nb + b], tb=True) * c["qb"]
            lane_f = lane_f + rows(c["up"] * dq_f * q32)
            lane_b = lane_b + rows(c["down"] * dq_b * q32)
            dq_ref[blk, :] = _unstack_heads(_dot(dsb, kb), bsz) + dq_f + dq_b
            sd_ref[b] = jnp.where(c["bd"], _dot(q32 * c["qf"], dob, ta=True), 0.0)
            sd_ref[nb + b] = jnp.where(c["bd"], _dot(q32 * c["qb"], dob, ta=True), 0.0)

        def through_state(b, grad, decay, weight, lane):
            blk = slice(b * bsz, (b + 1) * bsz)
            k32 = k_ref[blk, :].astype(F32)
            dk = _dot(v_ref[blk, :], grad, tb=True) * decay
            dk_ref[blk, :] += dk
            dv_ref[blk, :] += _dot(k32 * decay, grad)
            return lane + rows(weight * dk * k32)

        phi = jnp.zeros((BW, BW), F32)
        for b in reversed(range(nb)):
            if b < nb - 1:
                lane_f = through_state(b, phi, c["kf"], c["down"], lane_f)
                lane_f = lane_f + bsz * rows(c["cf"] * st_ref[b] * phi)
            phi = sd_ref[b] + c["cf"] * phi
        gam = jnp.zeros((BW, BW), F32)
        for b in range(nb):
            if b > 0:
                lane_b = through_state(b, gam, c["kb"], c["up"], lane_b)
                lane_b = lane_b + bsz * rows(c["cb"] * st_ref[nb + b] * gam)
            gam = sd_ref[nb + b] + c["cb"] * gam

        head = _lane_head((1, BW))
        for h in range(NH):
            tot_f = jnp.sum(row_f[h * bsz:(h + 1) * bsz, :]) + jnp.sum(jnp.where(head == h, lane_f, 0.0))
            tot_b = jnp.sum(row_b[h * bsz:(h + 1) * bsz, :]) + jnp.sum(jnp.where(head == h, lane_b, 0.0))
            dlg_ref[h:h + 1, :] = jnp.full((1, 128), tot_f, F32)
            dlg_ref[NH + h:NH + h + 1, :] = jnp.full((1, 128), tot_b, F32)

    whole = pl.BlockSpec((t, BW), lambda i: (0, 0))
    return pl.pallas_call(
        body, grid=(1,), in_specs=[SMEM, SMEM, whole, whole, whole, whole],
        out_specs=(whole, whole, whole, pl.BlockSpec((2 * NH, 128), lambda i: (0, 0))),
        out_shape=(_sds((t, BW), F32), _sds((t, BW), F32), _sds((t, BW), F32), _sds((2 * NH, 128), F32)),
        scratch_shapes=[pltpu.VMEM((2 * nb, BW, BW), F32), pltpu.VMEM((2 * nb, BW, BW), F32)], name="ret_bwd",
        compiler_params=_cp())(lgf, lgb, do, q, k, v)


def _pool_windows(t):
    row = lax.broadcasted_iota(jnp.int32, (t, BW), 0)
    half = lax.shift_left(jnp.ones((t, BW), jnp.int32), _lane_head((t, BW)))
    cnt = (jnp.minimum(row + half, t) - jnp.maximum(row - half, 0)).astype(F32)
    return row, half, cnt


def _pool_window_sum(v, row, half, t, transpose):
    out = jnp.zeros_like(v)
    for j in range(-POOL_HALF_MAX, POOL_HALF_MAX):
        src = row - j if transpose else row + j
        ok = (src >= 0) & (src < t) & (j >= -half) & (j < half)
        out = out + jnp.where(ok, pltpu.roll(v, (j if transpose else -j) % t, 0), 0.0)
    return out


def _pool_fwd(proj, wbd, scale):
    t = proj.shape[0]

    def body(v_ref, w_ref, s_ref, o_ref):
        v = v_ref[...]
        row, half, cnt = _pool_windows(t)
        pooled = _pool_window_sum(v, row, half, t, False) / cnt - v
        o_ref[...] = (_dot(pooled, w_ref[...]) * s_ref[...]).astype(BF16)

    return pl.pallas_call(
        body, grid=(1,),
        in_specs=[pl.BlockSpec((t, BW), lambda i: (0, PV)), pl.BlockSpec((BW, BW), lambda i: (0, 0)), pl.BlockSpec((1, BW), lambda i: (0, 0))],
        out_specs=pl.BlockSpec((t, BW), lambda i: (0, 0)), out_shape=_sds((t, BW), BF16), name="pool_fwd",
        compiler_params=_cp())(proj, wbd, scale)


def _pool_bwd(dbr, proj, wbd, scale):
    t = proj.shape[0]

    def body(d_ref, v_ref, w_ref, s_ref, dv_ref, dw_ref, ds_ref):
        v, dout = v_ref[...], d_ref[...]
        row, half, cnt = _pool_windows(t)
        pooled = _pool_window_sum(v, row, half, t, False) / cnt - v
        mixed = _dot(pooled, w_ref[...])
        ds_ref[...] = jnp.sum(dout * mixed, axis=0, keepdims=True)
        dmixed = dout * s_ref[...]
        dw_ref[...] = _dot(pooled, dmixed, ta=True)
        dpooled = _dot(dmixed, w_ref[...], tb=True)
        dv_ref[...] = _pool_window_sum(dpooled / cnt, row, half, t, True) - dpooled

    return pl.pallas_call(
        body, grid=(1,),
        in_specs=[pl.BlockSpec((t, BW), lambda i: (0, 1)), pl.BlockSpec((t, BW), lambda i: (0, PV)),
                  pl.BlockSpec((BW, BW), lambda i: (0, 0)), pl.BlockSpec((1, BW), lambda i: (0, 0))],
        out_specs=(pl.BlockSpec((t, BW), lambda i: (0, 0)), pl.BlockSpec((BW, BW), lambda i: (0, 0)), pl.BlockSpec((1, BW), lambda i: (0, 0))),
        out_shape=(_sds((t, BW), F32), _sds((BW, BW), F32), _sds((1, BW), F32)), name="pool_bwd",
        compiler_params=_cp())(dbr, proj, wbd, scale)


NA_KEYS = NA_ROWS_WIN * GRID_W
NA_PAIRS = 2 * NA_ROWS_WIN - 2


def _na_window(r, n_rows):
    rs = jnp.clip(r - NA_ROWS_WIN // 2, 0, n_rows - NA_ROWS_WIN)
    return pl.multiple_of(rs * GRID_W, GRID_W), rs - r + (NA_ROWS_WIN - 1)


def _na_bias(b_ref, a0):
    return jnp.concatenate([b_ref[a0 + 2 * j] for j in range(NA_ROWS_WIN // 2)], axis=1)


NA_STEP_ROWS = 8


def _na_fwd(q, k, v, ball):
    t = q.shape[0]
    n_rows = t // GRID_W
    rows = NA_STEP_ROWS

    def body(q_ref, k_ref, v_ref, b_ref, o_ref):
        for rr in range(rows):
            start, a0 = _na_window(pl.program_id(0) * rows + rr, n_rows)
            own = slice(rr * GRID_W, (rr + 1) * GRID_W)
            qs = _stack_heads(q_ref[own, :])
            s = _dot(qs, k_ref[pl.ds(start, NA_KEYS), :], tb=True) * (HD ** -0.5) + _na_bias(b_ref, a0)
            p = _softmax_rows(s)
            o_ref[own, :] = _unstack_heads(_dot(p, v_ref[pl.ds(start, NA_KEYS), :]), GRID_W).astype(BF16)

    blk = pl.BlockSpec((rows * GRID_W, BW), lambda r: (r, 0))
    whole = pl.BlockSpec((t, BW), lambda r: (0, 0))
    return pl.pallas_call(
        body, grid=(n_rows // rows,), in_specs=[blk, whole, whole, pl.BlockSpec(ball.shape, lambda r: (0, 0, 0))],
        out_specs=blk, out_shape=_sds((t, BW), BF16), name="na_fwd", compiler_params=_cp())(q, k, v, ball)


def _na_bwd(dbr, q, k, v, ball):
    t = q.shape[0]
    n_rows = t // GRID_W

    rows = NA_STEP_ROWS

    def body(d_ref, q_ref, k_ref, v_ref, b_ref, dq_ref, dk_ref, dv_ref, db_ref):
        @pl.when(pl.program_id(0) == 0)
        def _():
            dk_ref[...] = jnp.zeros_like(dk_ref)
            dv_ref[...] = jnp.zeros_like(dv_ref)
            db_ref[...] = jnp.zeros_like(db_ref)

        for rr in range(rows):
            start, a0 = _na_window(pl.program_id(0) * rows + rr, n_rows)
            keys = pl.ds(start, NA_KEYS)
            own = slice(rr * GRID_W, (rr + 1) * GRID_W)
            qs = _stack_heads(q_ref[own, :])
            kb, vb = k_ref[keys, :], v_ref[keys, :]
            p = _softmax_rows(_dot(qs, kb, tb=True) * (HD ** -0.5) + _na_bias(b_ref, a0))
            dos = _stack_heads(d_ref[own, :]).astype(MXU)
            dp = _dot(dos, vb, tb=True)
            dv_ref[keys, :] += _dot(p, dos, ta=True)
            ds = p * (dp - jnp.sum(dp * p, axis=-1, keepdims=True))
            for j in range(NA_ROWS_WIN // 2):
                db_ref[a0 + 2 * j] += ds[:, 2 * j * GRID_W:(2 * j + 2) * GRID_W]
            dsb = (ds * (HD ** -0.5)).astype(MXU)
            dq_ref[own, :] = _unstack_heads(_dot(dsb, kb), GRID_W)
            dk_ref[keys, :] += _dot(dsb, qs, ta=True)

    blk = pl.BlockSpec((rows * GRID_W, BW), lambda r: (r, 0))
    whole = pl.BlockSpec((t, BW), lambda r: (0, 0))
    tab = pl.BlockSpec(ball.shape, lambda r: (0, 0, 0))
    return pl.pallas_call(
        body, grid=(n_rows // rows,), in_specs=[pl.BlockSpec((rows * GRID_W, BW), lambda r: (r, 2)), blk, whole, whole, tab],
        out_specs=(blk, whole, whole, tab),
        out_shape=(_sds((t, BW), F32), _sds((t, BW), F32), _sds((t, BW), F32), _sds(ball.shape, F32)), name="na_bwd",
        compiler_params=_cp())(dbr, q, k, v, ball)


def _rpb_expand(rpb_pad, onehot):
    def body(r_ref, e_ref, o_ref):
        o_ref[...] = jnp.dot(r_ref[...], e_ref[...], precision=HI, preferred_element_type=F32)

    return pl.pallas_call(body, out_shape=_sds((rpb_pad.shape[0], GRID_W * GRID_W), F32), name="rpb_expand",
                          compiler_params=_cp())(rpb_pad, onehot)


def _rpb_reduce(dtab, onehot):
    def body(d_ref, e_ref, o_ref):
        o_ref[...] = lax.dot_general(d_ref[...], e_ref[...], (((1,), (1,)), ((), ())), precision=HI, preferred_element_type=F32)

    return pl.pallas_call(body, out_shape=_sds((dtab.shape[0], 128), F32), name="rpb_reduce", compiler_params=_cp())(dtab, onehot)


MEM_TQ = 256


def _mem_fwd(q, mk, mv):
    t = q.shape[0]
    tq = MEM_TQ

    def body(q_ref, k_ref, v_ref, o_ref):
        p = _softmax_rows(_dot(_stack_heads(q_ref[...]), k_ref[...], tb=True) * (HD ** -0.5))
        o_ref[...] = _unstack_heads(_dot(p, v_ref[...]), tq).astype(BF16)

    blk = pl.BlockSpec((tq, BW), lambda i: (i, 0))
    kv = pl.BlockSpec((N_MEM, BW), lambda i: (0, 0))
    return pl.pallas_call(body, grid=(t // tq,), in_specs=[blk, kv, kv], out_specs=blk, out_shape=_sds((t, BW), BF16),
                          name="mem_fwd", compiler_params=_cp())(q, mk, mv)


def _mem_bwd(dbr, q, mk, mv):
    t = q.shape[0]
    tq = MEM_TQ

    def body(d_ref, q_ref, k_ref, v_ref, dq_ref, dk_ref, dv_ref):
        first = pl.program_id(0) == 0
        qs = _stack_heads(q_ref[...])
        dos = _stack_heads(d_ref[...]).astype(MXU)
        p = _softmax_rows(_dot(qs, k_ref[...], tb=True) * (HD ** -0.5))
        dp = _dot(dos, v_ref[...], tb=True)
        _acc(dv_ref, _dot(p, dos, ta=True), first)
        dsb = (p * (dp - jnp.sum(dp * p, axis=-1, keepdims=True)) * (HD ** -0.5)).astype(MXU)
        dq_ref[...] = _unstack_heads(_dot(dsb, k_ref[...]), tq)
        _acc(dk_ref, _dot(dsb, qs, ta=True), first)

    blk = pl.BlockSpec((tq, BW), lambda i: (i, 0))
    kv = pl.BlockSpec((N_MEM, BW), lambda i: (0, 0))
    return pl.pallas_call(
        body, grid=(t // tq,), in_specs=[pl.BlockSpec((tq, BW), lambda i: (i, 3)), blk, kv, kv], out_specs=(blk, kv, kv),
        out_shape=(_sds((t, BW), F32), _sds((N_MEM, BW), F32), _sds((N_MEM, BW), F32)), name="mem_bwd",
        compiler_params=_cp())(dbr, q, mk, mv)


def _memkv_prep(kv, g_mk):
    def body(kv_ref, g_ref, k_ref, v_ref):
        k_ref[...] = _gnorm(kv_ref[:, 0:BW], g_ref[...]).astype(BF16)
        v_ref[...] = kv_ref[:, BW:2 * BW].astype(BF16)

    return pl.pallas_call(body, out_shape=(_sds((N_MEM, BW), BF16), _sds((N_MEM, BW), BF16)), name="memkv_prep",
                          compiler_params=_cp())(kv, g_mk)


def _memkv_bwd(kv, dk, dv, g_mk):
    def body(kv_ref, dk_ref, dv_ref, g_ref, o_ref, dg_ref):
        dkk, gain = _gnorm_bwd(dk_ref[...], kv_ref[:, 0:BW], g_ref[...])
        o_ref[:, 0:BW] = dkk.astype(BF16)
        o_ref[:, BW:2 * BW] = dv_ref[...].astype(BF16)
        dg_ref[...] = jnp.sum(gain, axis=0, keepdims=True)

    return pl.pallas_call(body, out_shape=(_sds((N_MEM, 2 * BW), BF16), _sds((1, BW), F32)), name="memkv_bwd",
                          compiler_params=_cp())(kv, dk, dv, g_mk)


MERGE_TM = 256


def _merge_fwd(brs, wbt, gp):
    t = gp.shape[0]
    tm = MERGE_TM

    def body(b0, b1, b2, b3, wb_ref, gp_ref, o_ref):
        out = jnp.zeros((tm, D), F32)
        for n, b_ref in enumerate((b0, b1, b2, b3)):
            up = _dot(b_ref[...], wb_ref[n], tb=True)
            out = out + _sigmoid(gp_ref[:, n * D:(n + 1) * D].astype(F32)) * up
        o_ref[...] = out.astype(BF16)

    blk = pl.BlockSpec((tm, BW), lambda i: (i, 0))
    return pl.pallas_call(
        body, grid=(t // tm,),
        in_specs=[blk, blk, blk, blk, pl.BlockSpec((NH, D, BW), lambda i: (0, 0, 0)), pl.BlockSpec((tm, NH * D), lambda i: (i, 0))],
        out_specs=pl.BlockSpec((tm, D), lambda i: (i, 0)), out_shape=_sds((t, D), BF16), name="merge_fwd",
        compiler_params=_cp())(*brs, wbt, gp)


def _merge_bwd(dmerged, brs, wbt, gp):
    t = gp.shape[0]
    tm = MERGE_TM

    def body(d_ref, b0, b1, b2, b3, wb_ref, gp_ref, dgp_ref, dup_ref):
        dm = d_ref[...]
        for n, b_ref in enumerate((b0, b1, b2, b3)):
            up = _dot(b_ref[...], wb_ref[n], tb=True)
            g = _sigmoid(gp_ref[:, n * D:(n + 1) * D].astype(F32))
            dgp_ref[:, n * D:(n + 1) * D] = (dm * up * (g * (1.0 - g))).astype(BF16)
            dup_ref[:, n * D:(n + 1) * D] = (dm * g).astype(BF16)

    row = pl.BlockSpec((tm, D), lambda i: (i, 0))
    blk = pl.BlockSpec((tm, BW), lambda i: (i, 0))
    wide = pl.BlockSpec((tm, NH * D), lambda i: (i, 0))
    return pl.pallas_call(
        body, grid=(t // tm,), in_specs=[row, blk, blk, blk, blk, pl.BlockSpec((NH, D, BW), lambda i: (0, 0, 0)), wide],
        out_specs=(wide, wide), out_shape=(_sds((t, NH * D), BF16), _sds((t, NH * D), BF16)), name="merge_bwd",
        compiler_params=_cp())(dmerged, *brs, wbt, gp)


def _dbranch(dup, wbt):
    t = dup.shape[0]
    tm = 1024

    def body(d_ref, w_ref, o_ref):
        o_ref[...] = _dot(d_ref[...], w_ref[...])

    return pl.pallas_call(
        body, grid=(t // tm, NH), in_specs=[pl.BlockSpec((tm, D), lambda i, n: (i, n)), pl.BlockSpec((None, D, BW), lambda i, n: (n, 0, 0))],
        out_specs=pl.BlockSpec((tm, BW), lambda i, n: (i, n)), out_shape=_sds((t, NH * BW), F32), name="dbranch",
        compiler_params=_cp())(dup, wbt)


def _dwbranch(brs, dup):
    t = dup.shape[0]

    def body(b0, b1, b2, b3, d_ref, o_ref):
        for n, b_ref in enumerate((b0, b1, b2, b3)):
            o_ref[n] = _dot(d_ref[:, n * D:(n + 1) * D], b_ref[...], ta=True).astype(BF16)

    return pl.pallas_call(body, out_shape=_sds((NH, D, BW), BF16), name="dwbranch", compiler_params=_cp())(*brs, dup)


FFN_TN = 256


def _ffn_in_fwd(h2, w_t):
    t = h2.shape[0]
    tm, tn = _tile(t, 1024), FFN_TN
    nj = FF // tn

    def body(x_ref, wa_ref, wg_ref, a_ref, g_ref, y_ref):
        x = x_ref[...]
        a, g = _dot(x, wa_ref[...], tb=True), _dot(x, wg_ref[...], tb=True)
        a_ref[...] = a.astype(BF16)
        g_ref[...] = g.astype(BF16)
        y_ref[...] = (a * _sigmoid(a) * g).astype(BF16)

    out = pl.BlockSpec((tm, tn), lambda i, j: (i, j))
    return pl.pallas_call(
        body, grid=(t // tm, nj),
        in_specs=[pl.BlockSpec((tm, D), lambda i, j: (i, 0)), pl.BlockSpec((tn, D), lambda i, j: (j, 0)),
                  pl.BlockSpec((tn, D), lambda i, j: (j + nj, 0))],
        out_specs=(out, out, out), out_shape=tuple(_sds((t, FF), BF16) for _ in range(3)), name="ffn_in_fwd",
        compiler_params=_cp(dimension_semantics=("parallel", "parallel")))(h2, w_t, w_t)


def _ffn_out_bwd(dx2b, w_out, a, g, dep):
    t = dx2b.shape[0]
    tm, tn = _tile(t, 1024), FFN_TN

    def body(*refs):
        x_ref, w_ref, a_ref, g_ref = refs[:4]
        da_ref, dg_ref = refs[-2:]
        d = _dot(x_ref[...], w_ref[...], tb=True)
        av, gv = a_ref[...].astype(F32), g_ref[...].astype(F32)
        s = _sigmoid(av)
        da_ref[...] = (d * gv * (s * (1.0 + av * (1.0 - s)))).astype(BF16)
        dg_ref[...] = (d * (av * s)).astype(BF16)

    blk = pl.BlockSpec((tm, tn), lambda i, j: (i, j))
    ins = [pl.BlockSpec((tm, D), lambda i, j: (i, 0)), pl.BlockSpec((tn, D), lambda i, j: (j, 0)), blk, blk]
    args = [dx2b, w_out, a, g]
    if dep is not None:
        ins.append(pl.BlockSpec((8, 128), lambda i, j: (0, 0)))
        args.append(dep)
    return pl.pallas_call(
        body, grid=(t // tm, FF // tn), in_specs=ins, out_specs=(blk, blk),
        out_shape=(_sds((t, FF), BF16), _sds((t, FF), BF16)), name="ffn_out_bwd",
        compiler_params=_cp(dimension_semantics=("parallel", "parallel")))(*args)


def _loss_head(y, target):
    t, d = y.shape
    tm = 256

    def body(y_ref, t_ref, dy_ref, dyb_ref, l_ref):
        e = y_ref[...] - t_ref[...]
        dy_ref[...] = e * (1.0 / d)
        dyb_ref[...] = (e * (1.0 / d)).astype(BF16)
        _acc(l_ref, jnp.full((8, 128), 0.5 * jnp.sum(jnp.sum(e * e, axis=-1, keepdims=True) * (1.0 / d)), F32), pl.program_id(0) == 0)

    row = pl.BlockSpec((tm, d), lambda i: (i, 0))
    return pl.pallas_call(body, grid=(t // tm,), in_specs=[row, row], out_specs=(row, row, pl.BlockSpec((8, 128), lambda i: (0, 0))),
                          out_shape=(_sds((t, d), F32), _sds((t, d), BF16), _sds((8, 128), F32)), name="loss_head",
                          compiler_params=_cp())(y, target)


def _sum_slots(x, name):
    k, r, c = x.shape
    tr = _tile(r, 512) if r % 128 == 0 else r

    def body(x_ref, o_ref):
        acc = x_ref[0].astype(F32)
        for s in range(1, k):
            acc = acc + x_ref[s].astype(F32)
        o_ref[...] = acc

    return pl.pallas_call(body, grid=(r // tr,), in_specs=[pl.BlockSpec((k, tr, c), lambda i: (0, i, 0))],
                          out_specs=pl.BlockSpec((tr, c), lambda i: (i, 0)), out_shape=_sds((r, c), F32), name=name,
                          compiler_params=_cp())(x)


def _pair_sum(bufs, recvs, cidx):
    n = len(bufs)

    def body(c_ref, *refs):
        for i in range(n):
            refs[2 * n + i][...] = (refs[i][...].astype(F32) + refs[n + i][...].astype(F32)).astype(BF16)

    return pl.pallas_call(
        body,
        grid_spec=pltpu.PrefetchScalarGridSpec(
            num_scalar_prefetch=1, grid=(4,),
            in_specs=[pl.BlockSpec((None, None) + b.shape[2:], lambda s, cref: (s, cref[0], 0, 0)) for b in bufs]
            + [pl.BlockSpec((None,) + r.shape[1:], lambda s, cref: (s, 0, 0)) for r in recvs],
            out_specs=tuple(pl.BlockSpec((None,) + r.shape[1:], lambda s, cref: (s, 0, 0)) for r in recvs)),
        out_shape=tuple(_sds(r.shape, BF16) for r in recvs), name="rs_pair_sum", compiler_params=_cp())(cidx, *bufs, *recvs)


def _adamw_update(w, gv, m, v):
    mn = ADAM_B1 * m + (1.0 - ADAM_B1) * gv
    vn = ADAM_B2 * v + (1.0 - ADAM_B2) * (gv * gv)
    m_hat = mn / (1.0 - ADAM_B1 ** ADAM_STEP)
    v_hat = vn / (1.0 - ADAM_B2 ** ADAM_STEP)
    return -ADAM_LR * (m_hat / (jnp.sqrt(v_hat) + ADAM_EPS) + ADAM_WD * w), mn, vn


def _adamw(w, g, m, v, name):
    r, c = w.shape

    def body(w_ref, g_ref, m_ref, v_ref, d_ref, nm_ref, nv_ref):
        d_ref[...], nm_ref[...], nv_ref[...] = _adamw_update(w_ref[...], g_ref[...], m_ref[...], v_ref[...])

    blk = pl.BlockSpec((r, c), lambda i: (0, 0))
    return pl.pallas_call(body, grid=(1,), in_specs=[blk] * 4, out_specs=(blk,) * 3,
                          out_shape=tuple(_sds((r, c), F32) for _ in range(3)), name=name, compiler_params=_cp())(w, g, m, v)


def _adamw_layer(layer, w, g, m, v, outs, name):
    _, r, c = w.shape
    tr = max(d for d in range(8, r + 1, 8) if r % d == 0 and d * c * 4 <= 2 ** 20)

    def body(w_ref, m_ref, v_ref, g_ref, *refs):
        d_ref, nm_ref, nv_ref, go_ref = refs[4:]
        gv = g_ref[...]
        d_ref[...], nm_ref[...], nv_ref[...] = _adamw_update(w_ref[...], gv, m_ref[...], v_ref[...])
        go_ref[...] = gv

    blk = pl.BlockSpec((None, tr, c), lambda i: (layer, i, 0))
    return pl.pallas_call(
        body, grid=(r // tr,), in_specs=[blk] * 3 + [pl.BlockSpec((tr, c), lambda i: (i, 0))] + [ANY] * 4, out_specs=(blk,) * 4,
        out_shape=tuple(_sds(w.shape, F32) for _ in range(4)), input_output_aliases={4 + j: j for j in range(4)}, name=name,
        compiler_params=_cp())(w, m, v, g, *outs)


def _all_gather(shards, name):
    n = len(shards)

    def body(*refs):
        x_refs, out_refs = refs[:n], refs[n:2 * n]
        send_sems, recv_sems, local_sems = refs[2 * n:]
        x, y, cc = lax.axis_index("x"), lax.axis_index("y"), lax.axis_index("c")
        me, sibling = (x, y, cc), (x, y, 1 - cc)
        chips = [(1 - x, y), (x, 1 - y), (1 - x, 1 - y)]

        def copy(i, k, block, to, own=False):
            px, py, pc = block
            slot = out_refs[i].at[4 * px + 2 * py + pc]
            return pltpu.make_async_remote_copy(
                src_ref=x_refs[i] if own else slot, dst_ref=slot, send_sem=send_sems.at[7 * i + k],
                recv_sem=recv_sems.at[7 * i + k], device_id=to, device_id_type=MESH)

        mine = [pltpu.make_async_copy(x_refs[i], out_refs[i].at[4 * x + 2 * y + cc], local_sems.at[i]) for i in range(n)]
        for cp in mine:
            cp.start()
        first = []
        for j, chip in enumerate(chips):
            first += [copy(i, 1 + j, me, (*chip, cc), own=True) for i in range(n)]
        first += [copy(i, 0, me, sibling, own=True) for i in range(n)]
        for cp in first:
            cp.start()
        passed = []
        for j, chip in enumerate(chips):
            for i in range(n):
                copy(i, 1 + j, (*chip, cc), me).wait_recv()
                cp = copy(i, 4 + j, (*chip, cc), sibling)
                cp.start()
                passed.append(cp)
        for i in range(n):
            copy(i, 0, sibling, me).wait_recv()
        for j, chip in enumerate(chips):
            for i in range(n):
                copy(i, 4 + j, (*chip, 1 - cc), me).wait_recv()
        for cp in first + passed:
            cp.wait_send()
        for cp in mine:
            cp.wait()

    return pl.pallas_call(
        body, out_shape=tuple(_sds((N_DEV,) + s.shape, s.dtype) for s in shards), in_specs=[ANY] * n, out_specs=(ANY,) * n,
        scratch_shapes=[pltpu.SemaphoreType.DMA((7 * n,)), pltpu.SemaphoreType.DMA((7 * n,)), pltpu.SemaphoreType.DMA((n,))],
        name=name)(*shards)


def _rs_core_swap(bufs, name):
    n = len(bufs)

    def body(*refs):
        b_refs, recv_refs = refs[:n], refs[n:2 * n]
        send_sems, recv_sems = refs[2 * n:]
        x, y, cc = lax.axis_index("x"), lax.axis_index("y"), lax.axis_index("c")
        copies = [pltpu.make_async_remote_copy(
            src_ref=b_refs[i].at[s, 1 - cc], dst_ref=recv_refs[i].at[s], send_sem=send_sems.at[4 * i + s],
            recv_sem=recv_sems.at[4 * i + s], device_id=(x, y, 1 - cc), device_id_type=MESH) for i in range(n) for s in range(4)]
        for cp in copies:
            cp.start()
        for cp in copies:
            cp.wait()

    return pl.pallas_call(
        body, out_shape=tuple(_sds((4,) + b.shape[2:], b.dtype) for b in bufs), in_specs=[ANY] * n, out_specs=(ANY,) * n,
        scratch_shapes=[pltpu.SemaphoreType.DMA((4 * n,)), pltpu.SemaphoreType.DMA((4 * n,))], name=name)(*bufs)


HBM = pl.BlockSpec(memory_space=pltpu.HBM)
SEMS = pl.BlockSpec(memory_space=pltpu.SEMAPHORE)
EFFECT = pltpu.SideEffectType.DATAFLOW_SIDE_EFFECTING


def _hbm(a):
    return pltpu.HBM(a.shape, a.dtype)


def _other_chips(x, y):
    return [(1 - x, y), (x, 1 - y), (1 - x, 1 - y)]


def _ici_start(srcs, lands, mode, name):
    n = len(srcs)

    def body(*refs):
        s_refs, land_refs = refs[:n], refs[n:2 * n]
        send_sems, recv_sems = refs[2 * n], refs[2 * n + 1]
        token = refs[-1]
        x, y, cc = lax.axis_index("x"), lax.axis_index("y"), lax.axis_index("c")
        mine = 2 * x + y if mode == "by_chip" else 4 * x + 2 * y + cc
        peers = [(px, py, cc) for px, py in _other_chips(x, y)]
        if mode == "by_device":
            peers = [(x, y, 1 - cc)] + peers + [(px, py, 1 - cc) for px, py in _other_chips(x, y)]
        for px, py, pc in peers:
            for i in range(n):
                src = s_refs[i]
                if mode == "by_chip":
                    src = src.at[2 * px + py]
                elif mode == "by_device":
                    src = src.at[4 * px + 2 * py + pc]
                pltpu.make_async_remote_copy(
                    src_ref=src, dst_ref=land_refs[i].at[mine], send_sem=send_sems.at[i], recv_sem=recv_sems.at[i],
                    device_id=(px, py, pc), device_id_type=MESH).start()
        token[...] = jnp.zeros_like(token)

    out = pl.pallas_call(
        body, name=name,
        out_shape=(pltpu.SemaphoreType.DMA((n,)), pltpu.SemaphoreType.DMA((n,)), *[_hbm(s) for s in srcs], *[_hbm(l) for l in lands],
                   _sds((8, 128), F32)),
        in_specs=[HBM] * (2 * n), out_specs=(SEMS, SEMS, *[HBM] * (2 * n), pl.BlockSpec(memory_space=pltpu.VMEM)),
        input_output_aliases={i: 2 + i for i in range(2 * n)}, compiler_params=pltpu.CompilerParams(has_side_effects=EFFECT),
    )(*[pltpu.with_memory_space_constraint(s, pltpu.HBM) for s in srcs],
      *[pltpu.with_memory_space_constraint(l, pltpu.HBM) for l in lands])
    return out[0], out[1], out[2:2 + n], out[2 + n:2 + 2 * n], out[-1], 7 if mode == "by_device" else 3


def _ici_wait(started, after, name, only=None):
    send_sems, recv_sems, srcs, lands, _, copies = started
    only = list(range(len(srcs))) if only is None else only
    srcs, lands = [srcs[i] for i in only], [lands[i] for i in only]
    n = len(srcs)

    def body(*refs):
        land_refs = refs[n:2 * n]
        send_sems, recv_sems = refs[2 * n], refs[2 * n + 1]
        x, y, cc = lax.axis_index("x"), lax.axis_index("y"), lax.axis_index("c")
        for i in range(n):
            three = land_refs[i].at[pl.ds(0, copies)]
            cp = pltpu.make_async_remote_copy(src_ref=three, dst_ref=three, send_sem=send_sems.at[only[i]],
                                              recv_sem=recv_sems.at[only[i]],
                                              device_id=(x, y, cc), device_id_type=MESH)
            cp.wait_send()
            cp.wait_recv()

    return pl.pallas_call(
        body, name=name, out_shape=tuple(_hbm(l) for l in lands), in_specs=[HBM] * (2 * n) + [SEMS, SEMS, ANY],
        out_specs=tuple([HBM] * n), input_output_aliases={n + i: i for i in range(n)},
        compiler_params=pltpu.CompilerParams(has_side_effects=EFFECT))(*srcs, *lands, send_sems, recv_sems, after)


def _gather_d2d(blocks, lands, name):
    n = len(blocks)

    def body(*refs):
        x_refs, land_refs = refs[:n], refs[2 * n:3 * n]
        send_sems, recv_sems, in_sems, out_sems = refs[3 * n:3 * n + 4]
        stage = refs[3 * n + 4:]
        x, y, cc = lax.axis_index("x"), lax.axis_index("y"), lax.axis_index("c")
        sibling = (x, y, 1 - cc)
        staged = [pltpu.make_async_copy(x_refs[i], stage[i], in_sems.at[i]) for i in range(n)]
        for cp in staged:
            cp.start()
        copies = []
        for i in range(n):
            slot = land_refs[i].at[4 * x + 2 * y + cc]
            copies.append(pltpu.make_async_remote_copy(src_ref=x_refs[i], dst_ref=slot, send_sem=send_sems.at[4 * i],
                                                       recv_sem=recv_sems.at[4 * i], device_id=sibling, device_id_type=MESH))
            for j, (px, py) in enumerate(_other_chips(x, y)):
                slot = land_refs[i].at[4 * px + 2 * py + cc]
                copies.append(pltpu.make_async_remote_copy(src_ref=slot, dst_ref=slot, send_sem=send_sems.at[4 * i + 1 + j],
                                                           recv_sem=recv_sems.at[4 * i + 1 + j], device_id=sibling, device_id_type=MESH))
        for cp in copies:
            cp.start()
        mine = []
        for i in range(n):
            staged[i].wait()
            mine.append(pltpu.make_async_copy(stage[i], land_refs[i].at[4 * x + 2 * y + cc], out_sems.at[i]))
            mine[i].start()
        for i in range(n):
            slot = land_refs[i].at[4 * x + 2 * y + (1 - cc)]
            pltpu.make_async_remote_copy(src_ref=slot, dst_ref=slot, send_sem=send_sems.at[4 * i], recv_sem=recv_sems.at[4 * i],
                                         device_id=sibling, device_id_type=MESH).wait_recv()
            for j, (px, py) in enumerate(_other_chips(x, y)):
                slot = land_refs[i].at[4 * px + 2 * py + (1 - cc)]
                pltpu.make_async_remote_copy(src_ref=slot, dst_ref=slot, send_sem=send_sems.at[4 * i + 1 + j],
                                             recv_sem=recv_sems.at[4 * i + 1 + j], device_id=sibling, device_id_type=MESH).wait_recv()
        for cp in copies:
            cp.wait_send()
        for cp in mine:
            cp.wait()

    return pl.pallas_call(
        body, out_shape=tuple(_sds(l.shape, l.dtype) for l in lands), in_specs=[ANY] * (2 * n), out_specs=(ANY,) * n,
        input_output_aliases={n + i: i for i in range(n)},
        scratch_shapes=[pltpu.SemaphoreType.DMA((4 * n,)), pltpu.SemaphoreType.DMA((4 * n,)), pltpu.SemaphoreType.DMA((n,)),
                        pltpu.SemaphoreType.DMA((n,))] + [pltpu.VMEM(b.shape, b.dtype) for b in blocks],
        name=name, compiler_params=_cp())(*blocks, *lands)


def _sum_own(parts, recvs, mine, name):
    n = len(parts)

    def body(c_ref, *refs):
        s = pl.program_id(0)
        for i in range(n):
            val = jnp.where(c_ref[0] == s, refs[i][...], refs[n + i][...]).astype(F32)
            _acc(refs[2 * n + i], val, s == 0)

    kept = [pl.BlockSpec((None,) + p.shape[1:], lambda s, cref: (cref[0], 0, 0)) for p in parts]
    ins = [pl.BlockSpec((None,) + p.shape[1:], lambda s, cref: (s, 0, 0)) for p in parts]
    return pl.pallas_call(
        body, grid_spec=pltpu.PrefetchScalarGridSpec(
            num_scalar_prefetch=1, grid=(parts[0].shape[0],), in_specs=kept + ins,
            out_specs=tuple(pl.BlockSpec(p.shape[1:], lambda s, cref: (0, 0)) for p in parts)),
        out_shape=tuple(_sds(p.shape[1:], F32) for p in parts), name=name, compiler_params=_cp())(mine, *parts, *recvs)


BIG = (("w_in", True), ("w_gate", True), ("w_mem_kv", False), ("w_branch", True), ("w_out", False), ("w_ffn_in", True),
       ("w_ffn_out", False))

SMALL = ("norm_mix_g", "norm_mem_g", "ret_decay_fwd", "ret_decay_bwd", "ret_norm_g", "pool_w", "pool_scale", "na_q_norm_g",
         "na_k_norm_g", "na_rpb", "mem_q_norm_g", "mem_k_norm_g", "norm_ffn_g")
WEIGHTS = ("norm_mix_g", "norm_mem_g", "w_in", "w_gate", "ret_decay_fwd", "ret_decay_bwd", "ret_norm_g", "pool_w", "pool_scale",
           "na_q_norm_g", "na_k_norm_g", "na_rpb", "mem_q_norm_g", "mem_k_norm_g", "w_mem_kv", "w_branch", "w_out", "norm_ffn_g",
           "w_ffn_in", "w_ffn_out")


def _to_exchange(name, transposed, shard):
    if name == "w_branch":
        return jnp.swapaxes(shard, 1, 2).reshape(NH * (D // N_DEV), BW)
    return shard.T if transposed else shard


def _from_exchange(name, transposed, block):
    if name == "w_branch":
        return jnp.swapaxes(block.reshape(NH, D // N_DEV, BW), 1, 2)
    return block.T if transposed else block


def _whole_from_gathered(name, g):
    if name == "w_branch":
        return jnp.swapaxes(g.reshape(N_DEV, NH, D // N_DEV, BW), 0, 1).reshape(NH, D, BW)
    return g.reshape(N_DEV * g.shape[1], g.shape[2])


def _by_destination(name, g):
    if name == "w_branch":
        g = jnp.swapaxes(g.reshape(NH, N_DEV, D // N_DEV, BW), 0, 1).reshape(N_DEV * NH * (D // N_DEV), BW)
    return g.reshape(4, 2, g.shape[0] // N_DEV, g.shape[1])


SMALL_PAD = 1024


def _pack_small(vals, loss=None):
    parts = [vals[n] for n in SMALL] + [jnp.zeros((1,), F32) if loss is None else loss.reshape(1)]
    rows = []
    for p in parts:
        flat = p.reshape(-1)
        rows.append(jnp.pad(flat, (0, -flat.shape[0] % SMALL_PAD)).reshape(-1, 128))
    return jnp.concatenate(rows, axis=0)


def _unpack_small(packed, like):
    out, off = {}, 0
    for n in SMALL:
        sz = int(np.prod(like[n].shape))
        nrow = -(-sz // SMALL_PAD) * (SMALL_PAD // 128)
        out[n] = packed[off:off + nrow].reshape(-1)[:sz].reshape(like[n].shape)
        off += nrow
    return out, packed[off, 0]


def _na_constants():
    c = np.arange(GRID_W)
    win = np.clip(c - NA_COLS_WIN // 2, 0, GRID_W - NA_COLS_WIN)
    kc = np.arange(GRID_W)
    inside = (kc[None, :] >= win[:, None]) & (kc[None, :] < win[:, None] + NA_COLS_WIN)
    off = kc[None, :] - c[:, None] + NA_COLS_WIN - 1
    onehot = np.zeros((128, GRID_W, GRID_W), np.float32)
    for b in range(2 * NA_COLS_WIN - 1):
        onehot[b] = (off == b) & inside
    maskadd = np.where(inside, 0.0, NEG).astype(np.float32)
    return onehot.reshape(128, GRID_W * GRID_W), maskadd


def _na_bias_table(tab, maskadd):
    n_off = 2 * NA_ROWS_WIN - 1
    t4 = tab[:NH * n_off].reshape(NH, n_off, GRID_W, GRID_W) + maskadd[None, None]
    by_off = t4.transpose(1, 0, 2, 3).reshape(n_off, NH * GRID_W, GRID_W)
    return jnp.concatenate([by_off[:-1], by_off[1:]], axis=-1)


def _rotary_tables(t):
    half = HD // 2
    inv = ROPE_THETA ** (-jnp.arange(half, dtype=F32) / half)
    ang = jnp.arange(t, dtype=F32)[:, None] * inv[None, :]
    cos, sin = jnp.cos(ang), jnp.sin(ang)
    return jnp.tile(jnp.concatenate([cos, cos], axis=-1), (1, NH)), jnp.tile(jnp.concatenate([-sin, sin], axis=-1), (1, NH))


def _block_diag(pw):
    out = jnp.zeros((BW, BW), pw.dtype)
    for g in range(NH):
        out = lax.dynamic_update_slice(out, pw[g], (g * HD, g * HD))
    return out


def _tile4(g):
    return jnp.tile(g.reshape(1, HD), (1, NH))


def _layer_fwd(x, mem, sw, lw, consts, fetch, h=None, next_norm_g=None):
    cos2, sin2, onehot, maskadd = consts
    if h is None:
        h = _rmsnorm_fwd(x, sw["norm_mix_g"].reshape(1, D), "norm_mix_fwd")
    proj = _mm(h, lw["w_in"], tb=True, name="mm_in")
    gp = _mm(h, lw["w_gate"], tb=True, out_dtype=BF16, name="mm_gate")
    g_naq, g_nak, g_mq = _tile4(sw["na_q_norm_g"]), _tile4(sw["na_k_norm_g"]), _tile4(sw["mem_q_norm_g"])
    rq, rk, rv, nq, nk, nv, mq = _prep_fwd(proj, cos2, sin2, g_naq, g_nak, g_mq)

    lgf, lgb = jax.nn.log_sigmoid(sw["ret_decay_fwd"]), jax.nn.log_sigmoid(sw["ret_decay_bwd"])
    g_ret = sw["ret_norm_g"].reshape(1, BW)
    o_ret, ret = _ret_fwd(rq, rk, rv, proj, lgf, lgb, g_ret)

    wbd = _block_diag(sw["pool_w"]).astype(BF16)
    p_scale = sw["pool_scale"].reshape(1, BW)
    pool = _pool_fwd(proj, wbd, p_scale)

    rpb_pad = jnp.pad(sw["na_rpb"].reshape(NH * 15, 31), ((0, 4), (0, 97)))
    ball = _na_bias_table(_rpb_expand(rpb_pad, onehot), maskadd)
    na = _na_fwd(nq, nk, nv, ball)

    lw.update(fetch(1, na))
    memn = _rmsnorm_fwd(mem, sw["norm_mem_g"].reshape(1, D), "norm_mem_fwd")
    kv = _mm(memn, lw["w_mem_kv"], name="mm_memkv")
    g_mk = _tile4(sw["mem_k_norm_g"])
    mk, mv = _memkv_prep(kv, g_mk)
    mo = _mem_fwd(mq, mk, mv)

    br = (ret, pool, na, mo)
    merged = _merge_fwd(br, lw["w_branch"], gp)
    x1, h2 = _mm(merged, lw["w_out"], add=x, norm_g=sw["norm_ffn_g"].reshape(1, D), name="mm_out")
    lw.update(fetch(2, x1))
    ffa, ffg, yff = _ffn_in_fwd(h2, lw["w_ffn_in"])
    if next_norm_g is None:
        x2, h_next = _mm(yff, lw["w_ffn_out"], add=x1, name="mm_ffn_out"), None
    else:
        x2, h_next = _mm(yff, lw["w_ffn_out"], add=x1, norm_g=next_norm_g.reshape(1, D), name="mm_ffn_out")
    saved = dict(x=x, h=h, proj=proj, gp=gp, rq=rq, rk=rk, rv=rv, nq=nq, nk=nk, nv=nv, mq=mq, o_ret=o_ret, ball=ball, memn=memn,
                 kv=kv, mk=mk, mv=mv, br=br, merged=merged, x1=x1, h2=h2, ffa=ffa, ffg=ffg, yff=yff, lgf=lgf, lgb=lgb, wbd=wbd)
    return x2, h_next, saved


def _layer_bwd(dx2, dx2b, mem, sw, lw, sv, consts, dep=None):
    cos2, sin2, onehot, maskadd = consts
    gb, gs = {}, {}
    d_a, d_g = _ffn_out_bwd(dx2b, lw["w_ffn_out"], sv["ffa"], sv["ffg"], dep)
    gb["w_ffn_out"] = _mm(sv["yff"], dx2b, ta=True, out_dtype=BF16, name="mm_ffn_out_dw")
    dh2 = _mm(d_a, lw["w_ffn_in"], b_half=0, name="mm_ffn_in_dx_a")
    dh2 = _mm(d_g, lw["w_ffn_in"], b_half=1, add=dh2, name="mm_ffn_in_dx_g")
    dw_a = _mm(d_a, sv["h2"], ta=True, out_dtype=BF16, out_half=(0, None), name="mm_ffn_in_dw_a")
    gb["w_ffn_in"] = _mm(d_g, sv["h2"], ta=True, out_dtype=BF16, out_half=(1, dw_a), name="mm_ffn_in_dw_g")
    dx1, dx1b, dg = _rmsnorm_bwd(dh2, sv["x1"], sw["norm_ffn_g"].reshape(1, D), dx2, "norm_ffn_bwd")
    gs["norm_ffn_g"] = dg.reshape(D)

    dmerged = _mm(dx1b, lw["w_out"], tb=True, name="mm_out_dx")
    gb["w_out"] = _mm(sv["merged"], dx1b, ta=True, out_dtype=BF16, name="mm_out_dw")
    dgp, dup = _merge_bwd(dmerged, sv["br"], lw["w_branch"], sv["gp"])
    dbr = _dbranch(dup, lw["w_branch"])
    gb["w_branch"] = _dwbranch(sv["br"], dup)

    g_ret = sw["ret_norm_g"].reshape(1, BW)
    do_ret, d_rg, dg_ret = _ret_post_bwd(dbr, sv["o_ret"], sv["proj"], g_ret)
    d_rq, d_rk, d_rv, dlg = _ret_bwd(do_ret, sv["rq"], sv["rk"], sv["rv"], sv["lgf"], sv["lgb"])
    gs["ret_norm_g"] = dg_ret.reshape(BW)
    _, vjp_f = jax.vjp(jax.nn.log_sigmoid, sw["ret_decay_fwd"])
    _, vjp_b = jax.vjp(jax.nn.log_sigmoid, sw["ret_decay_bwd"])
    gs["ret_decay_fwd"] = vjp_f(dlg[0:NH, 0])[0]
    gs["ret_decay_bwd"] = vjp_b(dlg[NH:2 * NH, 0])[0]

    p_scale = sw["pool_scale"].reshape(1, BW)
    d_pv, dwbd, dscale = _pool_bwd(dbr, sv["proj"], sv["wbd"], p_scale)
    gs["pool_w"] = jnp.stack([dwbd[g * HD:(g + 1) * HD, g * HD:(g + 1) * HD] for g in range(NH)])
    gs["pool_scale"] = dscale.reshape(BW)

    d_nq, d_nk, d_nv, dball = _na_bwd(dbr, sv["nq"], sv["nk"], sv["nv"], sv["ball"])
    _, vjp_tab = jax.vjp(lambda tab: _na_bias_table(tab, maskadd), jnp.zeros((64, GRID_W * GRID_W), F32))
    drpb = _rpb_reduce(vjp_tab(dball)[0], onehot)
    gs["na_rpb"] = drpb[:NH * 15, :31].reshape(NH, 15, 31)

    d_mq, d_mk, d_mv = _mem_bwd(dbr, sv["mq"], sv["mk"], sv["mv"])
    g_mk = _tile4(sw["mem_k_norm_g"])
    dkv, dg_mk = _memkv_bwd(sv["kv"], d_mk, d_mv, g_mk)
    gs["mem_k_norm_g"] = dg_mk.reshape(NH, HD).sum(0)
    gb["w_mem_kv"] = _mm(sv["memn"], dkv, ta=True, out_dtype=BF16, name="mm_memkv_dw")
    dmemn = _mm(dkv, lw["w_mem_kv"], tb=True, name="mm_memkv_dx")
    _, _, dg_mem = _rmsnorm_bwd(dmemn, mem, sw["norm_mem_g"].reshape(1, D), jnp.zeros_like(mem), "norm_mem_bwd")
    gs["norm_mem_g"] = dg_mem.reshape(D)

    g_naq, g_nak, g_mq = _tile4(sw["na_q_norm_g"]), _tile4(sw["na_k_norm_g"]), _tile4(sw["mem_q_norm_g"])
    dproj, dg_naq, dg_nak, dg_mq = _prep_bwd(sv["proj"], cos2, sin2, g_naq, g_nak, g_mq, d_rq, d_rk, d_rv, d_rg, d_pv, d_nq, d_nk,
                                             d_nv, d_mq)
    gs["na_q_norm_g"] = dg_naq.reshape(NH, HD).sum(0)
    gs["na_k_norm_g"] = dg_nak.reshape(NH, HD).sum(0)
    gs["mem_q_norm_g"] = dg_mq.reshape(NH, HD).sum(0)

    dh = _mm(dproj, lw["w_in"], name="mm_in_dx")
    dh = _mm(dgp, lw["w_gate"], add=dh, name="mm_gate_dx")
    gb["w_in"] = _mm(dproj, sv["h"], ta=True, out_dtype=BF16, name="mm_in_dw")
    gb["w_gate"] = _mm(dgp, sv["h"], ta=True, out_dtype=BF16, name="mm_gate_dw")
    dx, dxb, dg = _rmsnorm_bwd(dh, sv["x"], sw["norm_mix_g"].reshape(1, D), dx1, "norm_mix_bwd")
    gs["norm_mix_g"] = dg.reshape(D)
    return dx, dxb, gb, gs


def _local_step(x, mem, target, small, get_layer, on_grads):
    t = x.shape[0]
    cos2, sin2 = _rotary_tables(t)
    onehot, maskadd = _na_constants()
    consts = (cos2, sin2, jnp.asarray(onehot), jnp.asarray(maskadd))
    saved, weights, cur, h = [], [], x, None
    for l in range(DEPTH):
        sw = {n: small[n][l] for n in SMALL}
        lw, fetch = get_layer(l, cur)
        weights.append(lw)
        cur, h, sv = _layer_fwd(cur, mem, sw, lw, consts, fetch, h, small["norm_mix_g"][l + 1] if l + 1 < DEPTH else None)
        saved.append(sv)
    dy, dyb, loss_tile = _loss_head(cur, target)
    small_g = {n: [None] * DEPTH for n in SMALL}
    dep = None
    for l in reversed(range(DEPTH)):
        sw = {n: small[n][l] for n in SMALL}
        dy, dyb, gb, gs = _layer_bwd(dy, dyb, mem, sw, weights[l], saved[l], consts, dep)
        dep = on_grads(l, gb, dy)
        for n in SMALL:
            small_g[n][l] = gs[n]
    return loss_tile[0, 0], dy, {n: jnp.stack(v) for n, v in small_g.items()}


def _flat2d(a):
    return a.reshape(-1, a.shape[-1])


def kernel(x, mem, norm_mix_g, norm_mem_g, w_in, w_gate, ret_decay_fwd, ret_decay_bwd, ret_norm_g, pool_w, pool_scale, na_q_norm_g, na_k_norm_g, na_rpb, mem_q_norm_g, mem_k_norm_g, w_mem_kv, w_branch, w_out, norm_ffn_g, w_ffn_in, w_ffn_out, loss_target, m_norm_mix_g, m_norm_mem_g, m_w_in, m_w_gate, m_ret_decay_fwd, m_ret_decay_bwd, m_ret_norm_g, m_pool_w, m_pool_scale, m_na_q_norm_g, m_na_k_norm_g, m_na_rpb, m_mem_q_norm_g, m_mem_k_norm_g, m_w_mem_kv, m_w_branch, m_w_out, m_norm_ffn_g, m_w_ffn_in, m_w_ffn_out, v_norm_mix_g, v_norm_mem_g, v_w_in, v_w_gate, v_ret_decay_fwd, v_ret_decay_bwd, v_ret_norm_g, v_pool_w, v_pool_scale, v_na_q_norm_g, v_na_k_norm_g, v_na_rpb, v_mem_q_norm_g, v_mem_k_norm_g, v_w_mem_kv, v_w_branch, v_w_out, v_norm_ffn_g, v_w_ffn_in, v_w_ffn_out):
    w = dict(norm_mix_g=norm_mix_g, norm_mem_g=norm_mem_g, w_in=w_in, w_gate=w_gate, ret_decay_fwd=ret_decay_fwd,
             ret_decay_bwd=ret_decay_bwd, ret_norm_g=ret_norm_g, pool_w=pool_w, pool_scale=pool_scale, na_q_norm_g=na_q_norm_g,
             na_k_norm_g=na_k_norm_g, na_rpb=na_rpb, mem_q_norm_g=mem_q_norm_g, mem_k_norm_g=mem_k_norm_g, w_mem_kv=w_mem_kv,
             w_branch=w_branch, w_out=w_out, norm_ffn_g=norm_ffn_g, w_ffn_in=w_ffn_in, w_ffn_out=w_ffn_out)
    m = dict(norm_mix_g=m_norm_mix_g, norm_mem_g=m_norm_mem_g, w_in=m_w_in, w_gate=m_w_gate, ret_decay_fwd=m_ret_decay_fwd,
             ret_decay_bwd=m_ret_decay_bwd, ret_norm_g=m_ret_norm_g, pool_w=m_pool_w, pool_scale=m_pool_scale, na_q_norm_g=m_na_q_norm_g,
             na_k_norm_g=m_na_k_norm_g, na_rpb=m_na_rpb, mem_q_norm_g=m_mem_q_norm_g, mem_k_norm_g=m_mem_k_norm_g, w_mem_kv=m_w_mem_kv,
             w_branch=m_w_branch, w_out=m_w_out, norm_ffn_g=m_norm_ffn_g, w_ffn_in=m_w_ffn_in, w_ffn_out=m_w_ffn_out)
    v = dict(norm_mix_g=v_norm_mix_g, norm_mem_g=v_norm_mem_g, w_in=v_w_in, w_gate=v_w_gate, ret_decay_fwd=v_ret_decay_fwd,
             ret_decay_bwd=v_ret_decay_bwd, ret_norm_g=v_ret_norm_g, pool_w=v_pool_w, pool_scale=v_pool_scale, na_q_norm_g=v_na_q_norm_g,
             na_k_norm_g=v_na_k_norm_g, na_rpb=v_na_rpb, mem_q_norm_g=v_mem_q_norm_g, mem_k_norm_g=v_mem_k_norm_g, w_mem_kv=v_w_mem_kv,
             w_branch=v_w_branch, w_out=v_w_out, norm_ffn_g=v_norm_ffn_g, w_ffn_in=v_w_ffn_in, w_ffn_out=v_w_ffn_out)
    assert x.shape == (1, 2048, D) and mem.shape == (1, N_MEM, D) and w_in.shape == (DEPTH, D, 9 * BW // N_DEV)

    started = []
    for l in range(DEPTH):
        blocks = [_to_exchange(name, tr, w[name][l]).astype(BF16) for name, tr in BIG]
        lands = [lax.empty((N_DEV,) + b.shape, BF16) for b in blocks]
        started.append(_ici_start(blocks, lands, "gather", "gather_ici_start_%d" % l))
    all_started = started[0][4] + started[1][4] + started[2][4] + started[3][4]

    def get_group(l, only, after, tag):
        lands = _ici_wait(started[l], after, "gather_ici_wait_%d%s" % (l, tag), only)
        whole = _gather_d2d([started[l][2][i] for i in only], lands, "gather_d2d")
        return {BIG[i][0]: _whole_from_gathered(BIG[i][0], g) for i, g in zip(only, whole)}

    def get_layer(l, after):
        if l > 0:
            return get_group(l, list(range(len(BIG))), after, ""), lambda stage, after2: {}
        groups = [[0, 1], [2, 3, 4], [5, 6]]
        return get_group(l, groups[0], all_started, "a"), lambda stage, after2: get_group(l, groups[stage], after2, "abc"[stage])

    cidx = lax.axis_index("c").astype(jnp.int32).reshape(1)
    chip = (2 * lax.axis_index("x") + lax.axis_index("y")).astype(jnp.int32).reshape(1)
    in_flight = []

    def flip_of(name, tr):
        return (lambda a: jnp.swapaxes(a, 1, 2)) if name in ("w_in", "w_ffn_in") else (lambda a: a)

    def rows3(a):
        return a.reshape(DEPTH, -1, a.shape[-1])

    opt_in = {name: tuple(rows3(flip_of(name, tr)(t[name])) for t in (w, m, v)) for name, tr in BIG}
    opt_out = {name: tuple(lax.empty(opt_in[name][0].shape, F32) for _ in range(4)) for name, _ in BIG}

    device = (2 * chip + cidx).astype(jnp.int32)

    def finish(l, st, after):
        recv = _ici_wait(st, after, "rs_ici_wait_%d" % l)
        sums = _sum_own(st[2], recv, chip if st[5] == 3 else device, "rs_sum")
        for (name, tr), s in zip(BIG, sums):
            g = s if name in ("w_in", "w_ffn_in") else _from_exchange(name, tr, s)
            wx, mx, vx = opt_in[name]
            opt_out[name] = _adamw_layer(l, wx, g.reshape(-1, g.shape[-1]), mx, vx, opt_out[name], "adamw_" + name)

    def on_grads(l, gb, after):
        send = [_by_destination(name, gb[name]) for name, _ in BIG]
        if l > 0:
            send = [s.reshape((N_DEV,) + s.shape[2:]) for s in send]
            st = _ici_start(send, [lax.empty(s.shape, BF16) for s in send], "by_device", "rs_ici_start_%d" % l)
        else:
            from_core = _rs_core_swap(send, "rs_core_swap")
            chip_part = _pair_sum(send, from_core, cidx)
            st = _ici_start(chip_part, [lax.empty(p.shape, BF16) for p in chip_part], "by_chip", "rs_ici_start_%d" % l)
        in_flight.append((l, st))
        return st[4]

    loss_local, dx, small_g = _local_step(x[0], mem[0], loss_target[0], {n: w[n] for n in SMALL}, get_layer, on_grads)

    last_started = in_flight[-1][1][4]
    for l, st in in_flight[:-1]:
        finish(l, st, last_started)

    small_all, = _all_gather([_pack_small(small_g, loss_local) + last_started[0:1]], "gather_small")
    packed_g = _sum_slots(small_all, "small_sum")
    small_sum, loss = _unpack_small(packed_g, {n: w[n] for n in SMALL})
    d_, m_, v_ = _adamw(_pack_small({n: w[n] for n in SMALL}), packed_g, _pack_small({n: m[n] for n in SMALL}),
                        _pack_small({n: v[n] for n in SMALL}), "adamw_small")
    updated = d_[0:8]
    for name, _ in BIG:
        updated = updated + opt_out[name][0][1, 0:8, 0:128]
    finish(*in_flight[-1], updated)

    grads, delta, new_m, new_v = {}, {}, {}, {}
    for name, tr in BIG:
        shape = flip_of(name, tr)(w[name]).shape
        delta[name], new_m[name], new_v[name], grads[name] = (flip_of(name, tr)(a.reshape(shape)) for a in opt_out[name])
    like = {n: w[n] for n in SMALL}
    ds, _ = _unpack_small(d_, like)
    ms, _ = _unpack_small(m_, like)
    vs, _ = _unpack_small(v_, like)
    for n in SMALL:
        grads[n], delta[n], new_m[n], new_v[n] = small_sum[n], ds[n], ms[n], vs[n]

    return (loss, dx[None], *[grads[n] for n in WEIGHTS], *[delta[n] for n in WEIGHTS], *[new_m[n] for n in WEIGHTS],
            *[new_v[n] for n in WEIGHTS])
```

```python
import functools

import numpy as np
import jax
import jax.numpy as jnp
from jax import lax
from jax.experimental import pallas as pl
from jax.experimental.pallas import tpu as pltpu

F32 = jnp.float32
BF16 = jnp.bfloat16
MXU = jnp.bfloat16
HI = lax.Precision.HIGHEST

DEPTH = 4
D = 1024
BW = 256
HD = 64
NH = 4
GRID_W = 64
NA_ROWS_WIN = 8
NA_COLS_WIN = 16
N_MEM = 256
FF = 2816
EPS = 1e-6
NEG = -1e30
ROPE_THETA = 10000.0
POOL_HALF_MAX = 8

ADAM_LR, ADAM_B1, ADAM_B2, ADAM_EPS, ADAM_WD, ADAM_STEP = 0.001, 0.9, 0.999, 1e-08, 0.01, 10

N_DEV = 8
VMEM_LIMIT = 56 * 1024 * 1024

RQ, RK, RV, RG, PV, NQ, NK, NV, MQ = range(9)

MESH = pl.DeviceIdType.MESH
ANY = pl.BlockSpec(memory_space=pl.ANY)
SMEM = pl.BlockSpec(memory_space=pltpu.SMEM)


def _cp(**kw):
    return pltpu.CompilerParams(vmem_limit_bytes=VMEM_LIMIT, **kw)


def _tile(n, cap):
    if n <= cap:
        return n
    best = None
    for t in range(128, cap + 1, 128):
        if n % t == 0:
            best = t
    assert best is not None, (n, cap)
    return best


def _sds(shape, dtype):
    return jax.ShapeDtypeStruct(shape, dtype)


def _lane_head(shape):
    return lax.shift_right_logical(lax.broadcasted_iota(jnp.int32, shape, len(shape) - 1), 6)


def _group_mean(z):
    i = lax.shift_right_logical(lax.broadcasted_iota(jnp.int32, (BW, BW), 0), 6)
    j = lax.shift_right_logical(lax.broadcasted_iota(jnp.int32, (BW, BW), 1), 6)
    g = jnp.where(i == j, 1.0 / HD, 0.0).astype(BF16)
    z_hi = z.astype(BF16)
    z_lo = (z - z_hi.astype(F32)).astype(BF16)
    return jnp.dot(z_hi, g, preferred_element_type=F32) + jnp.dot(z_lo, g, preferred_element_type=F32)


def _gnorm(t, g):
    r = lax.rsqrt(_group_mean(t * t) + EPS)
    return t * r * g


def _gnorm_bwd(dy, t, g):
    r = lax.rsqrt(_group_mean(t * t) + EPS)
    th = t * r
    dth = dy * g
    dt = r * (dth - th * _group_mean(dth * th))
    return dt, dy * th


def _swap_halves(t):
    lane = lax.broadcasted_iota(jnp.int32, t.shape, 1)
    return jnp.where((lane & 63) < 32, pltpu.roll(t, BW - 32, 1), pltpu.roll(t, 32, 1))


def _sigmoid(x):
    return 1.0 / (1.0 + jnp.exp(-x))


def _dot(a, b, ta=False, tb=False):
    return lax.dot_general(a.astype(MXU), b.astype(MXU), (((0 if ta else 1,), (1 if tb else 0,)), ((), ())),
                           preferred_element_type=F32)


def _stack_heads(t):
    head = _lane_head(t.shape)
    return jnp.concatenate([jnp.where(head == h, t, jnp.zeros_like(t)) for h in range(NH)], axis=0)


def _unstack_heads(t, rows):
    head = _lane_head((rows, BW))
    out = jnp.zeros((rows, BW), F32)
    for h in range(NH):
        out = out + jnp.where(head == h, t[h * rows:(h + 1) * rows], 0.0)
    return out


def _softmax_rows(s):
    m = jnp.max(s, axis=-1, keepdims=True)
    e = jnp.exp(s - m)
    return e / jnp.sum(e, axis=-1, keepdims=True)


def _acc(ref, val, first):
    @pl.when(first)
    def _():
        ref[...] = val

    @pl.when(jnp.logical_not(first))
    def _():
        ref[...] += val


def _mm(a, b, *, ta=False, tb=False, out_dtype=F32, add=None, dep=None, b_half=None, out_half=None, norm_g=None, name):
    m, k = (a.shape[1], a.shape[0]) if ta else a.shape
    n = b.shape[0] if tb else b.shape[1]
    assert b_half is None or (not tb and b.shape[0] == 2 * k)
    tm, tn = _tile(m, 1408), (n if norm_g is not None else _tile(n, 768))
    n_in = 2 + (add is not None) + (dep is not None) + (out_half is not None) + (norm_g is not None)

    def body(*refs):
        a_ref, b_ref, o_ref = refs[0], refs[1], refs[n_in]
        r = _dot(a_ref[...], b_ref[...], ta, tb)
        if add is not None:
            r = r + refs[2][...]
        o_ref[...] = r.astype(out_dtype)
        if norm_g is not None:
            scale = lax.rsqrt(jnp.mean(r * r, axis=-1, keepdims=True) + EPS)
            refs[n_in + 1][...] = (r * scale * refs[n_in - 1][...]).astype(BF16)

    kb = 0 if b_half is None else b_half
    a_spec = pl.BlockSpec((k, tm), lambda i, j: (0, i)) if ta else pl.BlockSpec((tm, k), lambda i, j: (i, 0))
    b_spec = pl.BlockSpec((tn, k), lambda i, j: (j, 0)) if tb else pl.BlockSpec((k, tn), lambda i, j: (kb, j))
    plain = pl.BlockSpec((tm, tn), lambda i, j: (i, j))
    ins, args = [a_spec, b_spec], [a, b]
    if add is not None:
        ins.append(plain)
        args.append(add)
    if dep is not None:
        ins.append(pl.BlockSpec((8, 128), lambda i, j: (0, 0)))
        args.append(dep)
    o_spec, o_shape, aliases = plain, _sds((m, n), out_dtype), {}
    if out_half is not None:
        half, prev = out_half
        o_spec = pl.BlockSpec((tm, tn), lambda i, j: (i + half * (m // tm), j))
        o_shape = _sds((2 * m, n), out_dtype)
        ins.append(ANY)
        args.append(lax.empty((2 * m, n), out_dtype) if prev is None else prev)
        aliases = {len(args) - 1: 0}
    if norm_g is not None:
        ins.append(pl.BlockSpec((1, n), lambda i, j: (0, 0)))
        args.append(norm_g)
        o_spec, o_shape = (o_spec, plain), (o_shape, _sds((m, n), BF16))
    return pl.pallas_call(
        body, grid=(m // tm, n // tn), in_specs=ins, out_specs=o_spec, out_shape=o_shape, input_output_aliases=aliases, name=name,
        compiler_params=_cp(dimension_semantics=("parallel", "parallel")))(*args)


def _mm_norm_bwd(a, b, add, x, g, res, *, b_half=None, name):
    m, k = a.shape
    n = b.shape[1]
    tm = 512
    kb = 0 if b_half is None else b_half

    def body(a_ref, b_ref, c_ref, x_ref, g_ref, res_ref, dx_ref, dxb_ref, dg_ref):
        dhv = _dot(a_ref[...], b_ref[...]) + c_ref[...]
        xv = x_ref[...]
        r = lax.rsqrt(jnp.mean(xv * xv, axis=-1, keepdims=True) + EPS)
        xh = xv * r
        dxh = dhv * g_ref[...]
        dx = res_ref[...] + r * (dxh - xh * jnp.mean(dxh * xh, axis=-1, keepdims=True))
        dx_ref[...] = dx
        dxb_ref[...] = dx.astype(BF16)
        _acc(dg_ref, jnp.sum(dhv * xh, axis=0, keepdims=True), pl.program_id(0) == 0)

    row = pl.BlockSpec((tm, n), lambda i: (i, 0))
    vec = pl.BlockSpec((1, n), lambda i: (0, 0))
    return pl.pallas_call(
        body, grid=(m // tm,),
        in_specs=[pl.BlockSpec((tm, k), lambda i: (i, 0)), pl.BlockSpec((k, n), lambda i: (kb, 0)), row, row, vec, row],
        out_specs=(row, row, vec), out_shape=(_sds((m, n), F32), _sds((m, n), BF16), _sds((1, n), F32)), name=name,
        compiler_params=_cp())(a, b, add, x, g, res)


def _rmsnorm_fwd(x, g, name):
    t, d = x.shape
    tm = _tile(t, 256)

    def body(x_ref, g_ref, o_ref):
        xv = x_ref[...]
        r = lax.rsqrt(jnp.mean(xv * xv, axis=-1, keepdims=True) + EPS)
        o_ref[...] = (xv * r * g_ref[...]).astype(o_ref.dtype)

    return pl.pallas_call(
        body, grid=(t // tm,), in_specs=[pl.BlockSpec((tm, d), lambda i: (i, 0)), pl.BlockSpec((1, d), lambda i: (0, 0))],
        out_specs=pl.BlockSpec((tm, d), lambda i: (i, 0)), out_shape=_sds((t, d), BF16), name=name, compiler_params=_cp())(x, g)


def _rmsnorm_bwd(dh, x, g, res, name):
    t, d = x.shape
    tm = _tile(t, 256)

    def body(dh_ref, x_ref, g_ref, res_ref, dx_ref, dxb_ref, dg_ref):
        xv = x_ref[...]
        dhv = dh_ref[...]
        r = lax.rsqrt(jnp.mean(xv * xv, axis=-1, keepdims=True) + EPS)
        xh = xv * r
        dxh = dhv * g_ref[...]
        dx = res_ref[...] + r * (dxh - xh * jnp.mean(dxh * xh, axis=-1, keepdims=True))
        dx_ref[...] = dx
        dxb_ref[...] = dx.astype(BF16)
        _acc(dg_ref, jnp.sum(dhv * xh, axis=0, keepdims=True), pl.program_id(0) == 0)

    row = pl.BlockSpec((tm, d), lambda i: (i, 0))
    vec = pl.BlockSpec((1, d), lambda i: (0, 0))
    return pl.pallas_call(
        body, grid=(t // tm,), in_specs=[row, row, vec, row], out_specs=(row, row, vec),
        out_shape=(_sds((t, d), F32), _sds((t, d), BF16), _sds((1, d), F32)), name=name, compiler_params=_cp())(dh, x, g, res)


def _prep_fwd(proj, cos2, sin2, g_naq, g_nak, g_mq):
    t = proj.shape[0]
    tm = 256

    def body(p_ref, cos_ref, sin_ref, gq_ref, gk_ref, gm_ref, rq_ref, rk_ref, rv_ref, nq_ref, nk_ref, nv_ref, mq_ref):
        def col(c):
            return p_ref[:, c * BW:(c + 1) * BW]

        cosv, sinv = cos_ref[...], sin_ref[...]

        def rot(tv):
            return tv * cosv + _swap_halves(tv) * sinv

        rq_ref[...] = (rot(col(RQ)) * (HD ** -0.5)).astype(BF16)
        rk_ref[...] = rot(col(RK)).astype(BF16)
        rv_ref[...] = col(RV).astype(BF16)
        nq_ref[...] = _gnorm(col(NQ), gq_ref[...]).astype(BF16)
        nk_ref[...] = _gnorm(col(NK), gk_ref[...]).astype(BF16)
        nv_ref[...] = col(NV).astype(BF16)
        mq_ref[...] = _gnorm(col(MQ), gm_ref[...]).astype(BF16)

    blk = pl.BlockSpec((tm, BW), lambda i: (i, 0))
    vec = pl.BlockSpec((1, BW), lambda i: (0, 0))
    return pl.pallas_call(
        body, grid=(t // tm,), in_specs=[pl.BlockSpec((tm, 9 * BW), lambda i: (i, 0)), blk, blk, vec, vec, vec],
        out_specs=tuple(blk for _ in range(7)), out_shape=tuple(_sds((t, BW), BF16) for _ in range(7)),
        name="prep_fwd", compiler_params=_cp())(proj, cos2, sin2, g_naq, g_nak, g_mq)


def _prep_bwd(proj, cos2, sin2, g_naq, g_nak, g_mq, d_rq, d_rk, d_rv, d_rg, d_pv, d_nq, d_nk, d_nv, d_mq):
    t = proj.shape[0]
    tm = 256

    def body(p_ref, cos_ref, sin_ref, gq_ref, gk_ref, gm_ref, drq_ref, drk_ref, drv_ref, drg_ref, dpv_ref, dnq_ref, dnk_ref,
             dnv_ref, dmq_ref, o_ref, dgq_ref, dgk_ref, dgm_ref):
        first = pl.program_id(0) == 0

        def col(c):
            return p_ref[:, c * BW:(c + 1) * BW]

        def put(c, v):
            o_ref[:, c * BW:(c + 1) * BW] = v.astype(BF16)

        cosv, sinv = cos_ref[...], sin_ref[...]

        def rot_t(dv):
            return dv * cosv + _swap_halves(dv * sinv)

        put(RQ, rot_t(drq_ref[...] * (HD ** -0.5)))
        put(RK, rot_t(drk_ref[...]))
        put(RV, drv_ref[...])
        put(RG, drg_ref[...])
        put(PV, dpv_ref[...])
        dq, gq = _gnorm_bwd(dnq_ref[...], col(NQ), gq_ref[...])
        put(NQ, dq)
        _acc(dgq_ref, jnp.sum(gq, axis=0, keepdims=True), first)
        dk, gk = _gnorm_bwd(dnk_ref[...], col(NK), gk_ref[...])
        put(NK, dk)
        _acc(dgk_ref, jnp.sum(gk, axis=0, keepdims=True), first)
        put(NV, dnv_ref[...])
        dm, gm = _gnorm_bwd(dmq_ref[...], col(MQ), gm_ref[...])
        put(MQ, dm)
        _acc(dgm_ref, jnp.sum(gm, axis=0, keepdims=True), first)

    blk = pl.BlockSpec((tm, BW), lambda i: (i, 0))
    vec = pl.BlockSpec((1, BW), lambda i: (0, 0))
    wide = pl.BlockSpec((tm, 9 * BW), lambda i: (i, 0))
    return pl.pallas_call(
        body, grid=(t // tm,), in_specs=[wide, blk, blk, vec, vec, vec] + [blk] * 9, out_specs=(wide, vec, vec, vec),
        out_shape=(_sds((t, 9 * BW), BF16), _sds((1, BW), F32), _sds((1, BW), F32), _sds((1, BW), F32)),
        name="prep_bwd", compiler_params=_cp())(proj, cos2, sin2, g_naq, g_nak, g_mq, d_rq, d_rk, d_rv, d_rg, d_pv, d_nq, d_nk,
                                                d_nv, d_mq)


RET_B = 256


def _ret_consts(lgf_ref, lgb_ref):
    bsz = RET_B
    head = _lane_head((1, BW))
    lf, lb = jnp.zeros((1, BW), F32), jnp.zeros((1, BW), F32)
    for h in range(NH):
        lf = lf + jnp.where(head == h, lgf_ref[h], 0.0)
        lb = lb + jnp.where(head == h, lgb_ref[h], 0.0)
    pos = lax.broadcasted_iota(jnp.int32, (bsz, BW), 0).astype(F32)
    up, down = pos + 1.0, (bsz - 1.0) - pos
    c = dict(up=up, down=down, kf=jnp.exp(down * lf), kb=jnp.exp(up * lb), qf=jnp.exp(up * lf), qb=jnp.exp(down * lb),
             cf=jnp.exp(bsz * lf), cb=jnp.exp(bsz * lb))
    diff = (lax.broadcasted_iota(jnp.int32, (NH * bsz, 1), 0) & (bsz - 1)) - lax.broadcasted_iota(jnp.int32, (1, bsz), 1)
    c["causal"] = diff >= 0
    c["dist"] = jnp.abs(diff).astype(F32)
    lgf = jnp.concatenate([jnp.full((bsz, 1), lgf_ref[h], F32) for h in range(NH)], axis=0)
    lgb = jnp.concatenate([jnp.full((bsz, 1), lgb_ref[h], F32) for h in range(NH)], axis=0)
    c["dm"] = jnp.exp(c["dist"] * jnp.where(c["causal"], lgf, lgb))
    c["bd"] = _lane_head((BW, BW)) == lax.shift_right_logical(lax.broadcasted_iota(jnp.int32, (BW, BW), 0), 6)
    return c


def _ret_states(k_ref, v_ref, st_ref, c, nb):
    bsz = RET_B

    def summary(b, decay):
        kb = k_ref[b * bsz:(b + 1) * bsz, :].astype(F32)
        return jnp.where(c["bd"], _dot(kb * decay, v_ref[b * bsz:(b + 1) * bsz, :], ta=True), 0.0)

    f = jnp.zeros((BW, BW), F32)
    for b in range(nb):
        st_ref[b] = f
        if b < nb - 1:
            f = c["cf"] * f + summary(b, c["kf"])
    g = jnp.zeros((BW, BW), F32)
    for b in reversed(range(nb)):
        st_ref[nb + b] = g
        if b > 0:
            g = c["cb"] * g + summary(b, c["kb"])


def _ret_fwd(q, k, v, proj, lgf, lgb, g_ret):
    t = q.shape[0]
    bsz, nb = RET_B, t // RET_B

    def body(lgf_ref, lgb_ref, q_ref, k_ref, v_ref, rg_ref, g_ref, o_ref, ret_ref, st_ref):
        c = _ret_consts(lgf_ref, lgb_ref)
        _ret_states(k_ref, v_ref, st_ref, c, nb)
        for b in range(nb):
            blk = slice(b * bsz, (b + 1) * bsz)
            qb, kb, vb = q_ref[blk, :], k_ref[blk, :], v_ref[blk, :]
            s = _dot(_stack_heads(qb), kb, tb=True)
            o = _unstack_heads(_dot(s * c["dm"], vb), bsz)
            q32 = qb.astype(F32)
            o = o + _dot(q32 * c["qf"], st_ref[b]) + _dot(q32 * c["qb"], st_ref[nb + b])
            o_ref[blk, :] = o
            rg = rg_ref[blk, :]
            ret_ref[blk, :] = (_gnorm(o, g_ref[...]) * (rg * _sigmoid(rg))).astype(BF16)

    whole = pl.BlockSpec((t, BW), lambda i: (0, 0))
    return pl.pallas_call(
        body, grid=(1,),
        in_specs=[SMEM, SMEM, whole, whole, whole, pl.BlockSpec((t, BW), lambda i: (0, RG)), pl.BlockSpec((1, BW), lambda i: (0, 0))],
        out_specs=(whole, whole), out_shape=(_sds((t, BW), F32), _sds((t, BW), BF16)),
        scratch_shapes=[pltpu.VMEM((2 * nb, BW, BW), F32)], name="ret_fwd", compiler_params=_cp())(lgf, lgb, q, k, v, proj, g_ret)


def _ret_post_bwd(dbr, o_ret, proj, g_ret):
    t = o_ret.shape[0]
    tm = 256

    def body(d_ref, o_ref, rg_ref, g_ref, do_ref, drg_ref, dg_ref):
        dret, o, rg, g = d_ref[...], o_ref[...], rg_ref[...], g_ref[...]
        sg = _sigmoid(rg)
        do, dgain = _gnorm_bwd(dret * (rg * sg), o, g)
        do_ref[...] = do.astype(BF16)
        drg_ref[...] = dret * _gnorm(o, g) * (sg * (1.0 + rg * (1.0 - sg)))
        _acc(dg_ref, jnp.sum(dgain, axis=0, keepdims=True), pl.program_id(0) == 0)

    blk = pl.BlockSpec((tm, BW), lambda i: (i, 0))
    vec = pl.BlockSpec((1, BW), lambda i: (0, 0))
    return pl.pallas_call(
        body, grid=(t // tm,), in_specs=[blk, blk, pl.BlockSpec((tm, BW), lambda i: (i, RG)), vec], out_specs=(blk, blk, vec),
        out_shape=(_sds((t, BW), BF16), _sds((t, BW), F32), _sds((1, BW), F32)), name="ret_post_bwd",
        compiler_params=_cp())(dbr, o_ret, proj, g_ret)


def _ret_bwd(do, q, k, v, lgf, lgb):
    t = q.shape[0]
    bsz, nb = RET_B, t // RET_B

    def body(lgf_ref, lgb_ref, d_ref, q_ref, k_ref, v_ref, dq_ref, dk_ref, dv_ref, dlg_ref, st_ref, sd_ref):
        c = _ret_consts(lgf_ref, lgb_ref)
        _ret_states(k_ref, v_ref, st_ref, c, nb)
        lane_f, lane_b = jnp.zeros((1, BW), F32), jnp.zeros((1, BW), F32)
        row_f, row_b = jnp.zeros((NH * bsz, 1), F32), jnp.zeros((NH * bsz, 1), F32)

        def rows(x):
            return jnp.sum(x, axis=0, keepdims=True)

        for b in range(nb):
            blk = slice(b * bsz, (b + 1) * bsz)
            qb, kb, vb, dob = q_ref[blk, :], k_ref[blk, :], v_ref[blk, :], d_ref[blk, :]
            q32 = qb.astype(F32)
            qs, dos = _stack_heads(qb), _stack_heads(dob)
            s = _dot(qs, kb, tb=True)
            da = _dot(dos, vb, tb=True)
            dv_ref[blk, :] = _dot(s * c["dm"], dos, ta=True)
            ds = da * c["dm"]
            w = ds * s * c["dist"]
            row_f = row_f + jnp.sum(jnp.where(c["causal"], w, 0.0), axis=1, keepdims=True)
            row_b = row_b + jnp.sum(jnp.where(c["causal"], 0.0, w), axis=1, keepdims=True)
            dsb = ds.astype(MXU)
            dk_ref[blk, :] = _dot(dsb, qs, ta=True)
            dq_f = _dot(dob, st_ref[b], tb=True) * c["qf"]
            dq_b = _dot(dob, st_ref[nb + b], tb=True) * c["qb"]
            lane_f = lane_f + rows(c["up"] * dq_f * q32)
            lane_b = lane_b + rows(c["down"] * dq_b * q32)
            dq_ref[blk, :] = _unstack_heads(_dot(dsb, kb), bsz) + dq_f + dq_b
            sd_ref[b] = jnp.where(c["bd"], _dot(q32 * c["qf"], dob, ta=True), 0.0)
            sd_ref[nb + b] = jnp.where(c["bd"], _dot(q32 * c["qb"], dob, ta=True), 0.0)

        def through_state(b, grad, decay, weight, lane):
            blk = slice(b * bsz, (b + 1) * bsz)
            k32 = k_ref[blk, :].astype(F32)
            dk = _dot(v_ref[blk, :], grad, tb=True) * decay
            dk_ref[blk, :] += dk
            dv_ref[blk, :] += _dot(k32 * decay, grad)
            return lane + rows(weight * dk * k32)

        phi = jnp.zeros((BW, BW), F32)
        for b in reversed(range(nb)):
            if b < nb - 1:
                lane_f = through_state(b, phi, c["kf"], c["down"], lane_f)
                lane_f = lane_f + bsz * rows(c["cf"] * st_ref[b] * phi)
            phi = sd_ref[b] + c["cf"] * phi
        gam = jnp.zeros((BW, BW), F32)
        for b in range(nb):
            if b > 0:
                lane_b = through_state(b, gam, c["kb"], c["up"], lane_b)
                lane_b = lane_b + bsz * rows(c["cb"] * st_ref[nb + b] * gam)
            gam = sd_ref[nb + b] + c["cb"] * gam

        head = _lane_head((1, BW))
        for h in range(NH):
            tot_f = jnp.sum(row_f[h * bsz:(h + 1) * bsz, :]) + jnp.sum(jnp.where(head == h, lane_f, 0.0))
            tot_b = jnp.sum(row_b[h * bsz:(h + 1) * bsz, :]) + jnp.sum(jnp.where(head == h, lane_b, 0.0))
            dlg_ref[h:h + 1, :] = jnp.full((1, 128), tot_f, F32)
            dlg_ref[NH + h:NH + h + 1, :] = jnp.full((1, 128), tot_b, F32)

    whole = pl.BlockSpec((t, BW), lambda i: (0, 0))
    return pl.pallas_call(
        body, grid=(1,), in_specs=[SMEM, SMEM, whole, whole, whole, whole],
        out_specs=(whole, whole, whole, pl.BlockSpec((2 * NH, 128), lambda i: (0, 0))),
        out_shape=(_sds((t, BW), F32), _sds((t, BW), F32), _sds((t, BW), F32), _sds((2 * NH, 128), F32)),
        scratch_shapes=[pltpu.VMEM((2 * nb, BW, BW), F32), pltpu.VMEM((2 * nb, BW, BW), F32)], name="ret_bwd",
        compiler_params=_cp())(lgf, lgb, do, q, k, v)


def _pool_windows(t):
    row = lax.broadcasted_iota(jnp.int32, (t, BW), 0)
    half = lax.shift_left(jnp.ones((t, BW), jnp.int32), _lane_head((t, BW)))
    cnt = (jnp.minimum(row + half, t) - jnp.maximum(row - half, 0)).astype(F32)
    return row, half, cnt


def _pool_window_sum(v, row, half, t, transpose):
    out = jnp.zeros_like(v)
    for j in range(-POOL_HALF_MAX, POOL_HALF_MAX):
        src = row - j if transpose else row + j
        ok = (src >= 0) & (src < t) & (j >= -half) & (j < half)
        out = out + jnp.where(ok, pltpu.roll(v, (j if transpose else -j) % t, 0), 0.0)
    return out


def _pool_fwd(proj, wbd, scale):
    t = proj.shape[0]

    def body(v_ref, w_ref, s_ref, o_ref):
        v = v_ref[...]
        row, half, cnt = _pool_windows(t)
        pooled = _pool_window_sum(v, row, half, t, False) / cnt - v
        o_ref[...] = (_dot(pooled, w_ref[...]) * s_ref[...]).astype(BF16)

    return pl.pallas_call(
        body, grid=(1,),
        in_specs=[pl.BlockSpec((t, BW), lambda i: (0, PV)), pl.BlockSpec((BW, BW), lambda i: (0, 0)), pl.BlockSpec((1, BW), lambda i: (0, 0))],
        out_specs=pl.BlockSpec((t, BW), lambda i: (0, 0)), out_shape=_sds((t, BW), BF16), name="pool_fwd",
        compiler_params=_cp())(proj, wbd, scale)


def _pool_bwd(dbr, proj, wbd, scale):
    t = proj.shape[0]

    def body(d_ref, v_ref, w_ref, s_ref, dv_ref, dw_ref, ds_ref):
        v, dout = v_ref[...], d_ref[...]
        row, half, cnt = _pool_windows(t)
        pooled = _pool_window_sum(v, row, half, t, False) / cnt - v
        mixed = _dot(pooled, w_ref[...])
        ds_ref[...] = jnp.sum(dout * mixed, axis=0, keepdims=True)
        dmixed = dout * s_ref[...]
        dw_ref[...] = _dot(pooled, dmixed, ta=True)
        dpooled = _dot(dmixed, w_ref[...], tb=True)
        dv_ref[...] = _pool_window_sum(dpooled / cnt, row, half, t, True) - dpooled

    return pl.pallas_call(
        body, grid=(1,),
        in_specs=[pl.BlockSpec((t, BW), lambda i: (0, 1)), pl.BlockSpec((t, BW), lambda i: (0, PV)),
                  pl.BlockSpec((BW, BW), lambda i: (0, 0)), pl.BlockSpec((1, BW), lambda i: (0, 0))],
        out_specs=(pl.BlockSpec((t, BW), lambda i: (0, 0)), pl.BlockSpec((BW, BW), lambda i: (0, 0)), pl.BlockSpec((1, BW), lambda i: (0, 0))),
        out_shape=(_sds((t, BW), F32), _sds((BW, BW), F32), _sds((1, BW), F32)), name="pool_bwd",
        compiler_params=_cp())(dbr, proj, wbd, scale)


NA_KEYS = NA_ROWS_WIN * GRID_W
NA_PAIRS = 2 * NA_ROWS_WIN - 2


def _na_window(r, n_rows):
    rs = jnp.clip(r - NA_ROWS_WIN // 2, 0, n_rows - NA_ROWS_WIN)
    return pl.multiple_of(rs * GRID_W, GRID_W), rs - r + (NA_ROWS_WIN - 1)


def _na_bias(b_ref, a0):
    return jnp.concatenate([b_ref[a0 + 2 * j] for j in range(NA_ROWS_WIN // 2)], axis=1)


NA_STEP_ROWS = 8


def _na_fwd(q, k, v, ball):
    t = q.shape[0]
    n_rows = t // GRID_W
    rows = NA_STEP_ROWS

    def body(q_ref, k_ref, v_ref, b_ref, o_ref):
        for rr in range(rows):
            start, a0 = _na_window(pl.program_id(0) * rows + rr, n_rows)
            own = slice(rr * GRID_W, (rr + 1) * GRID_W)
            qs = _stack_heads(q_ref[own, :])
            s = _dot(qs, k_ref[pl.ds(start, NA_KEYS), :], tb=True) * (HD ** -0.5) + _na_bias(b_ref, a0)
            p = _softmax_rows(s)
            o_ref[own, :] = _unstack_heads(_dot(p, v_ref[pl.ds(start, NA_KEYS), :]), GRID_W).astype(BF16)

    blk = pl.BlockSpec((rows * GRID_W, BW), lambda r: (r, 0))
    whole = pl.BlockSpec((t, BW), lambda r: (0, 0))
    return pl.pallas_call(
        body, grid=(n_rows // rows,), in_specs=[blk, whole, whole, pl.BlockSpec(ball.shape, lambda r: (0, 0, 0))],
        out_specs=blk, out_shape=_sds((t, BW), BF16), name="na_fwd", compiler_params=_cp())(q, k, v, ball)


def _na_bwd(dbr, q, k, v, ball):
    t = q.shape[0]
    n_rows = t // GRID_W

    rows = NA_STEP_ROWS

    def body(d_ref, q_ref, k_ref, v_ref, b_ref, dq_ref, dk_ref, dv_ref, db_ref):
        @pl.when(pl.program_id(0) == 0)
        def _():
            dk_ref[...] = jnp.zeros_like(dk_ref)
            dv_ref[...] = jnp.zeros_like(dv_ref)
            db_ref[...] = jnp.zeros_like(db_ref)

        for rr in range(rows):
            start, a0 = _na_window(pl.program_id(0) * rows + rr, n_rows)
            keys = pl.ds(start, NA_KEYS)
            own = slice(rr * GRID_W, (rr + 1) * GRID_W)
            qs = _stack_heads(q_ref[own, :])
            kb, vb = k_ref[keys, :], v_ref[keys, :]
            p = _softmax_rows(_dot(qs, kb, tb=True) * (HD ** -0.5) + _na_bias(b_ref, a0))
            dos = _stack_heads(d_ref[own, :]).astype(MXU)
            dp = _dot(dos, vb, tb=True)
            dv_ref[keys, :] += _dot(p, dos, ta=True)
            ds = p * (dp - jnp.sum(dp * p, axis=-1, keepdims=True))
            for j in range(NA_ROWS_WIN // 2):
                db_ref[a0 + 2 * j] += ds[:, 2 * j * GRID_W:(2 * j + 2) * GRID_W]
            dsb = (ds * (HD ** -0.5)).astype(MXU)
            dq_ref[own, :] = _unstack_heads(_dot(dsb, kb), GRID_W)
            dk_ref[keys, :] += _dot(dsb, qs, ta=True)

    blk = pl.BlockSpec((rows * GRID_W, BW), lambda r: (r, 0))
    whole = pl.BlockSpec((t, BW), lambda r: (0, 0))
    tab = pl.BlockSpec(ball.shape, lambda r: (0, 0, 0))
    return pl.pallas_call(
        body, grid=(n_rows // rows,), in_specs=[pl.BlockSpec((rows * GRID_W, BW), lambda r: (r, 2)), blk, whole, whole, tab],
        out_specs=(blk, whole, whole, tab),
        out_shape=(_sds((t, BW), F32), _sds((t, BW), F32), _sds((t, BW), F32), _sds(ball.shape, F32)), name="na_bwd",
        compiler_params=_cp())(dbr, q, k, v, ball)


def _rpb_expand(rpb_pad, onehot):
    def body(r_ref, e_ref, o_ref):
        o_ref[...] = jnp.dot(r_ref[...], e_ref[...], precision=HI, preferred_element_type=F32)

    return pl.pallas_call(body, out_shape=_sds((rpb_pad.shape[0], GRID_W * GRID_W), F32), name="rpb_expand",
                          compiler_params=_cp())(rpb_pad, onehot)


def _rpb_reduce(dtab, onehot):
    def body(d_ref, e_ref, o_ref):
        o_ref[...] = lax.dot_general(d_ref[...], e_ref[...], (((1,), (1,)), ((), ())), precision=HI, preferred_element_type=F32)

    return pl.pallas_call(body, out_shape=_sds((dtab.shape[0], 128), F32), name="rpb_reduce", compiler_params=_cp())(dtab, onehot)


MEM_TQ = 256


def _mem_fwd(q, mk, mv):
    t = q.shape[0]
    tq = MEM_TQ

    def body(q_ref, k_ref, v_ref, o_ref):
        p = _softmax_rows(_dot(_stack_heads(q_ref[...]), k_ref[...], tb=True) * (HD ** -0.5))
        o_ref[...] = _unstack_heads(_dot(p, v_ref[...]), tq).astype(BF16)

    blk = pl.BlockSpec((tq, BW), lambda i: (i, 0))
    kv = pl.BlockSpec((N_MEM, BW), lambda i: (0, 0))
    return pl.pallas_call(body, grid=(t // tq,), in_specs=[blk, kv, kv], out_specs=blk, out_shape=_sds((t, BW), BF16),
                          name="mem_fwd", compiler_params=_cp())(q, mk, mv)


def _mem_bwd(dbr, q, mk, mv):
    t = q.shape[0]
    tq = MEM_TQ

    def body(d_ref, q_ref, k_ref, v_ref, dq_ref, dk_ref, dv_ref):
        first = pl.program_id(0) == 0
        qs = _stack_heads(q_ref[...])
        dos = _stack_heads(d_ref[...]).astype(MXU)
        p = _softmax_rows(_dot(qs, k_ref[...], tb=True) * (HD ** -0.5))
        dp = _dot(dos, v_ref[...], tb=True)
        _acc(dv_ref, _dot(p, dos, ta=True), first)
        dsb = (p * (dp - jnp.sum(dp * p, axis=-1, keepdims=True)) * (HD ** -0.5)).astype(MXU)
        dq_ref[...] = _unstack_heads(_dot(dsb, k_ref[...]), tq)
        _acc(dk_ref, _dot(dsb, qs, ta=True), first)

    blk = pl.BlockSpec((tq, BW), lambda i: (i, 0))
    kv = pl.BlockSpec((N_MEM, BW), lambda i: (0, 0))
    return pl.pallas_call(
        body, grid=(t // tq,), in_specs=[pl.BlockSpec((tq, BW), lambda i: (i, 3)), blk, kv, kv], out_specs=(blk, kv, kv),
        out_shape=(_sds((t, BW), F32), _sds((N_MEM, BW), F32), _sds((N_MEM, BW), F32)), name="mem_bwd",
        compiler_params=_cp())(dbr, q, mk, mv)


def _memkv_prep(kv, g_mk):
    def body(kv_ref, g_ref, k_ref, v_ref):
        k_ref[...] = _gnorm(kv_ref[:, 0:BW], g_ref[...]).astype(BF16)
        v_ref[...] = kv_ref[:, BW:2 * BW].astype(BF16)

    return pl.pallas_call(body, out_shape=(_sds((N_MEM, BW), BF16), _sds((N_MEM, BW), BF16)), name="memkv_prep",
                          compiler_params=_cp())(kv, g_mk)


def _memkv_bwd(kv, dk, dv, g_mk):
    def body(kv_ref, dk_ref, dv_ref, g_ref, o_ref, dg_ref):
        dkk, gain = _gnorm_bwd(dk_ref[...], kv_ref[:, 0:BW], g_ref[...])
        o_ref[:, 0:BW] = dkk.astype(BF16)
        o_ref[:, BW:2 * BW] = dv_ref[...].astype(BF16)
        dg_ref[...] = jnp.sum(gain, axis=0, keepdims=True)

    return pl.pallas_call(body, out_shape=(_sds((N_MEM, 2 * BW), BF16), _sds((1, BW), F32)), name="memkv_bwd",
                          compiler_params=_cp())(kv, dk, dv, g_mk)


MERGE_TM = 256


def _merge_fwd(brs, wbt, gp):
    t = gp.shape[0]
    tm = MERGE_TM

    def body(b0, b1, b2, b3, wb_ref, gp_ref, o_ref):
        out = jnp.zeros((tm, D), F32)
        for n, b_ref in enumerate((b0, b1, b2, b3)):
            up = _dot(b_ref[...], wb_ref[n], tb=True)
            out = out + _sigmoid(gp_ref[:, n * D:(n + 1) * D].astype(F32)) * up
        o_ref[...] = out.astype(BF16)

    blk = pl.BlockSpec((tm, BW), lambda i: (i, 0))
    return pl.pallas_call(
        body, grid=(t // tm,),
        in_specs=[blk, blk, blk, blk, pl.BlockSpec((NH, D, BW), lambda i: (0, 0, 0)), pl.BlockSpec((tm, NH * D), lambda i: (i, 0))],
        out_specs=pl.BlockSpec((tm, D), lambda i: (i, 0)), out_shape=_sds((t, D), BF16), name="merge_fwd",
        compiler_params=_cp())(*brs, wbt, gp)


def _merge_bwd(dmerged, brs, wbt, gp):
    t = gp.shape[0]
    tm = MERGE_TM

    def body(d_ref, b0, b1, b2, b3, wb_ref, gp_ref, dgp_ref, dup_ref, dbr_ref):
        dm = d_ref[...]
        for n, b_ref in enumerate((b0, b1, b2, b3)):
            up = _dot(b_ref[...], wb_ref[n], tb=True)
            g = _sigmoid(gp_ref[:, n * D:(n + 1) * D].astype(F32))
            dgp_ref[:, n * D:(n + 1) * D] = (dm * up * (g * (1.0 - g))).astype(BF16)
            dup = (dm * g).astype(BF16)
            dup_ref[:, n * D:(n + 1) * D] = dup
            dbr_ref[:, n * BW:(n + 1) * BW] = _dot(dup, wb_ref[n])

    row = pl.BlockSpec((tm, D), lambda i: (i, 0))
    blk = pl.BlockSpec((tm, BW), lambda i: (i, 0))
    wide = pl.BlockSpec((tm, NH * D), lambda i: (i, 0))
    return pl.pallas_call(
        body, grid=(t // tm,), in_specs=[row, blk, blk, blk, blk, pl.BlockSpec((NH, D, BW), lambda i: (0, 0, 0)), wide],
        out_specs=(wide, wide, row), out_shape=(_sds((t, NH * D), BF16), _sds((t, NH * D), BF16), _sds((t, NH * BW), F32)),
        name="merge_bwd", compiler_params=_cp())(dmerged, *brs, wbt, gp)


def _dwbranch(brs, dup):
    t = dup.shape[0]

    def body(b0, b1, b2, b3, d_ref, o_ref):
        for n, b_ref in enumerate((b0, b1, b2, b3)):
            o_ref[n] = _dot(d_ref[:, n * D:(n + 1) * D], b_ref[...], ta=True).astype(BF16)

    return pl.pallas_call(body, out_shape=_sds((NH, D, BW), BF16), name="dwbranch", compiler_params=_cp())(*brs, dup)


FFN_TN = 256


def _ffn_in_fwd(h2, w_t):
    t = h2.shape[0]
    tm, tn = _tile(t, 1024), FFN_TN
    nj = FF // tn

    def body(x_ref, wa_ref, wg_ref, a_ref, g_ref, y_ref):
        x = x_ref[...]
        a, g = _dot(x, wa_ref[...], tb=True), _dot(x, wg_ref[...], tb=True)
        a_ref[...] = a.astype(BF16)
        g_ref[...] = g.astype(BF16)
        y_ref[...] = (a * _sigmoid(a) * g).astype(BF16)

    out = pl.BlockSpec((tm, tn), lambda i, j: (i, j))
    return pl.pallas_call(
        body, grid=(t // tm, nj),
        in_specs=[pl.BlockSpec((tm, D), lambda i, j: (i, 0)), pl.BlockSpec((tn, D), lambda i, j: (j, 0)),
                  pl.BlockSpec((tn, D), lambda i, j: (j + nj, 0))],
        out_specs=(out, out, out), out_shape=tuple(_sds((t, FF), BF16) for _ in range(3)), name="ffn_in_fwd",
        compiler_params=_cp(dimension_semantics=("parallel", "parallel")))(h2, w_t, w_t)


def _ffn_out_bwd(dx2b, w_out, a, g, dep):
    t = dx2b.shape[0]
    tm, tn = _tile(t, 1024), FFN_TN

    def body(*refs):
        x_ref, w_ref, a_ref, g_ref = refs[:4]
        da_ref, dg_ref = refs[-2:]
        d = _dot(x_ref[...], w_ref[...], tb=True)
        av, gv = a_ref[...].astype(F32), g_ref[...].astype(F32)
        s = _sigmoid(av)
        da_ref[...] = (d * gv * (s * (1.0 + av * (1.0 - s)))).astype(BF16)
        dg_ref[...] = (d * (av * s)).astype(BF16)

    blk = pl.BlockSpec((tm, tn), lambda i, j: (i, j))
    ins = [pl.BlockSpec((tm, D), lambda i, j: (i, 0)), pl.BlockSpec((tn, D), lambda i, j: (j, 0)), blk, blk]
    args = [dx2b, w_out, a, g]
    if dep is not None:
        ins.append(pl.BlockSpec((8, 128), lambda i, j: (0, 0)))
        args.append(dep)
    return pl.pallas_call(
        body, grid=(t // tm, FF // tn), in_specs=ins, out_specs=(blk, blk),
        out_shape=(_sds((t, FF), BF16), _sds((t, FF), BF16)), name="ffn_out_bwd",
        compiler_params=_cp(dimension_semantics=("parallel", "parallel")))(*args)


def _loss_head(y, target):
    t, d = y.shape
    tm = 256

    def body(y_ref, t_ref, dy_ref, dyb_ref, l_ref):
        e = y_ref[...] - t_ref[...]
        dy_ref[...] = e * (1.0 / d)
        dyb_ref[...] = (e * (1.0 / d)).astype(BF16)
        _acc(l_ref, jnp.full((8, 128), 0.5 * jnp.sum(jnp.sum(e * e, axis=-1, keepdims=True) * (1.0 / d)), F32), pl.program_id(0) == 0)

    row = pl.BlockSpec((tm, d), lambda i: (i, 0))
    return pl.pallas_call(body, grid=(t // tm,), in_specs=[row, row], out_specs=(row, row, pl.BlockSpec((8, 128), lambda i: (0, 0))),
                          out_shape=(_sds((t, d), F32), _sds((t, d), BF16), _sds((8, 128), F32)), name="loss_head",
                          compiler_params=_cp())(y, target)


def _sum_slots(x, name):
    k, r, c = x.shape
    tr = _tile(r, 512) if r % 128 == 0 else r

    def body(x_ref, o_ref):
        acc = x_ref[0].astype(F32)
        for s in range(1, k):
            acc = acc + x_ref[s].astype(F32)
        o_ref[...] = acc

    return pl.pallas_call(body, grid=(r // tr,), in_specs=[pl.BlockSpec((k, tr, c), lambda i: (0, i, 0))],
                          out_specs=pl.BlockSpec((tr, c), lambda i: (i, 0)), out_shape=_sds((r, c), F32), name=name,
                          compiler_params=_cp())(x)


def _pair_sum(bufs, recvs, cidx):
    n = len(bufs)

    def body(c_ref, *refs):
        for i in range(n):
            refs[2 * n + i][...] = (refs[i][...].astype(F32) + refs[n + i][...].astype(F32)).astype(BF16)

    return pl.pallas_call(
        body,
        grid_spec=pltpu.PrefetchScalarGridSpec(
            num_scalar_prefetch=1, grid=(4,),
            in_specs=[pl.BlockSpec((None, None) + b.shape[2:], lambda s, cref: (s, cref[0], 0, 0)) for b in bufs]
            + [pl.BlockSpec((None,) + r.shape[1:], lambda s, cref: (s, 0, 0)) for r in recvs],
            out_specs=tuple(pl.BlockSpec((None,) + r.shape[1:], lambda s, cref: (s, 0, 0)) for r in recvs)),
        out_shape=tuple(_sds(r.shape, BF16) for r in recvs), name="rs_pair_sum", compiler_params=_cp())(cidx, *bufs, *recvs)


def _adamw_update(w, gv, m, v):
    mn = ADAM_B1 * m + (1.0 - ADAM_B1) * gv
    vn = ADAM_B2 * v + (1.0 - ADAM_B2) * (gv * gv)
    m_hat = mn / (1.0 - ADAM_B1 ** ADAM_STEP)
    v_hat = vn / (1.0 - ADAM_B2 ** ADAM_STEP)
    return -ADAM_LR * (m_hat / (jnp.sqrt(v_hat) + ADAM_EPS) + ADAM_WD * w), mn, vn


def _adamw(w, g, m, v, name):
    r, c = w.shape

    def body(w_ref, g_ref, m_ref, v_ref, d_ref, nm_ref, nv_ref):
        d_ref[...], nm_ref[...], nv_ref[...] = _adamw_update(w_ref[...], g_ref[...], m_ref[...], v_ref[...])

    blk = pl.BlockSpec((r, c), lambda i: (0, 0))
    return pl.pallas_call(body, grid=(1,), in_specs=[blk] * 4, out_specs=(blk,) * 3,
                          out_shape=tuple(_sds((r, c), F32) for _ in range(3)), name=name, compiler_params=_cp())(w, g, m, v)


def _adamw_layer(layer, w, g, m, v, outs, name):
    _, r, c = w.shape
    tr = max(d for d in range(8, r + 1, 8) if r % d == 0 and d * c * 4 <= 2 ** 20)

    def body(w_ref, m_ref, v_ref, g_ref, *refs):
        d_ref, nm_ref, nv_ref, go_ref = refs[4:]
        gv = g_ref[...]
        d_ref[...], nm_ref[...], nv_ref[...] = _adamw_update(w_ref[...], gv, m_ref[...], v_ref[...])
        go_ref[...] = gv

    blk = pl.BlockSpec((None, tr, c), lambda i: (layer, i, 0))
    return pl.pallas_call(
        body, grid=(r // tr,), in_specs=[blk] * 3 + [pl.BlockSpec((tr, c), lambda i: (i, 0))] + [ANY] * 4, out_specs=(blk,) * 4,
        out_shape=tuple(_sds(w.shape, F32) for _ in range(4)), input_output_aliases={4 + j: j for j in range(4)}, name=name,
        compiler_params=_cp())(w, m, v, g, *outs)


def _all_gather(shards, name):
    n = len(shards)

    def body(*refs):
        x_refs, out_refs = refs[:n], refs[n:2 * n]
        send_sems, recv_sems, local_sems = refs[2 * n:]
        x, y, cc = lax.axis_index("x"), lax.axis_index("y"), lax.axis_index("c")
        me, sibling = (x, y, cc), (x, y, 1 - cc)
        chips = [(1 - x, y), (x, 1 - y), (1 - x, 1 - y)]

        def copy(i, k, block, to, own=False):
            px, py, pc = block
            slot = out_refs[i].at[4 * px + 2 * py + pc]
            return pltpu.make_async_remote_copy(
                src_ref=x_refs[i] if own else slot, dst_ref=slot, send_sem=send_sems.at[7 * i + k],
                recv_sem=recv_sems.at[7 * i + k], device_id=to, device_id_type=MESH)

        mine = [pltpu.make_async_copy(x_refs[i], out_refs[i].at[4 * x + 2 * y + cc], local_sems.at[i]) for i in range(n)]
        for cp in mine:
            cp.start()
        first = []
        for j, chip in enumerate(chips):
            first += [copy(i, 1 + j, me, (*chip, cc), own=True) for i in range(n)]
        first += [copy(i, 0, me, sibling, own=True) for i in range(n)]
        for cp in first:
            cp.start()
        passed = []
        for j, chip in enumerate(chips):
            for i in range(n):
                copy(i, 1 + j, (*chip, cc), me).wait_recv()
                cp = copy(i, 4 + j, (*chip, cc), sibling)
                cp.start()
                passed.append(cp)
        for i in range(n):
            copy(i, 0, sibling, me).wait_recv()
        for j, chip in enumerate(chips):
            for i in range(n):
                copy(i, 4 + j, (*chip, 1 - cc), me).wait_recv()
        for cp in first + passed:
            cp.wait_send()
        for cp in mine:
            cp.wait()

    return pl.pallas_call(
        body, out_shape=tuple(_sds((N_DEV,) + s.shape, s.dtype) for s in shards), in_specs=[ANY] * n, out_specs=(ANY,) * n,
        scratch_shapes=[pltpu.SemaphoreType.DMA((7 * n,)), pltpu.SemaphoreType.DMA((7 * n,)), pltpu.SemaphoreType.DMA((n,))],
        name=name)(*shards)


def _rs_core_swap(bufs, name):
    n = len(bufs)

    def body(*refs):
        b_refs, recv_refs = refs[:n], refs[n:2 * n]
        send_sems, recv_sems = refs[2 * n:]
        x, y, cc = lax.axis_index("x"), lax.axis_index("y"), lax.axis_index("c")
        copies = [pltpu.make_async_remote_copy(
            src_ref=b_refs[i].at[s, 1 - cc], dst_ref=recv_refs[i].at[s], send_sem=send_sems.at[4 * i + s],
            recv_sem=recv_sems.at[4 * i + s], device_id=(x, y, 1 - cc), device_id_type=MESH) for i in range(n) for s in range(4)]
        for cp in copies:
            cp.start()
        for cp in copies:
            cp.wait()

    return pl.pallas_call(
        body, out_shape=tuple(_sds((4,) + b.shape[2:], b.dtype) for b in bufs), in_specs=[ANY] * n, out_specs=(ANY,) * n,
        scratch_shapes=[pltpu.SemaphoreType.DMA((4 * n,)), pltpu.SemaphoreType.DMA((4 * n,))], name=name)(*bufs)


HBM = pl.BlockSpec(memory_space=pltpu.HBM)
SEMS = pl.BlockSpec(memory_space=pltpu.SEMAPHORE)
EFFECT = pltpu.SideEffectType.DATAFLOW_SIDE_EFFECTING


def _hbm(a):
    return pltpu.HBM(a.shape, a.dtype)


def _other_chips(x, y):
    return [(1 - x, y), (x, 1 - y), (1 - x, 1 - y)]


def _ici_start(srcs, lands, mode, name):
    n = len(srcs)

    def body(*refs):
        s_refs, land_refs = refs[:n], refs[n:2 * n]
        send_sems, recv_sems = refs[2 * n], refs[2 * n + 1]
        token = refs[-1]
        x, y, cc = lax.axis_index("x"), lax.axis_index("y"), lax.axis_index("c")
        mine = 2 * x + y if mode == "by_chip" else 4 * x + 2 * y + cc
        peers = [(px, py, cc) for px, py in _other_chips(x, y)]
        if mode == "by_device":
            peers = [(x, y, 1 - cc)] + peers + [(px, py, 1 - cc) for px, py in _other_chips(x, y)]
        for px, py, pc in peers:
            for i in range(n):
                src = s_refs[i]
                if mode == "by_chip":
                    src = src.at[2 * px + py]
                elif mode == "by_device":
                    src = src.at[4 * px + 2 * py + pc]
                pltpu.make_async_remote_copy(
                    src_ref=src, dst_ref=land_refs[i].at[mine], send_sem=send_sems.at[i], recv_sem=recv_sems.at[i],
                    device_id=(px, py, pc), device_id_type=MESH).start()
        token[...] = jnp.zeros_like(token)

    out = pl.pallas_call(
        body, name=name,
        out_shape=(pltpu.SemaphoreType.DMA((n,)), pltpu.SemaphoreType.DMA((n,)), *[_hbm(s) for s in srcs], *[_hbm(l) for l in lands],
                   _sds((8, 128), F32)),
        in_specs=[HBM] * (2 * n), out_specs=(SEMS, SEMS, *[HBM] * (2 * n), pl.BlockSpec(memory_space=pltpu.VMEM)),
        input_output_aliases={i: 2 + i for i in range(2 * n)}, compiler_params=pltpu.CompilerParams(has_side_effects=EFFECT),
    )(*[pltpu.with_memory_space_constraint(s, pltpu.HBM) for s in srcs],
      *[pltpu.with_memory_space_constraint(l, pltpu.HBM) for l in lands])
    return out[0], out[1], out[2:2 + n], out[2 + n:2 + 2 * n], out[-1], 7 if mode == "by_device" else 3


def _ici_wait(started, after, name, only=None):
    send_sems, recv_sems, srcs, lands, _, copies = started
    only = list(range(len(srcs))) if only is None else only
    srcs, lands = [srcs[i] for i in only], [lands[i] for i in only]
    n = len(srcs)

    def body(*refs):
        land_refs = refs[n:2 * n]
        send_sems, recv_sems = refs[2 * n], refs[2 * n + 1]
        x, y, cc = lax.axis_index("x"), lax.axis_index("y"), lax.axis_index("c")
        for i in range(n):
            three = land_refs[i].at[pl.ds(0, copies)]
            cp = pltpu.make_async_remote_copy(src_ref=three, dst_ref=three, send_sem=send_sems.at[only[i]],
                                              recv_sem=recv_sems.at[only[i]],
                                              device_id=(x, y, cc), device_id_type=MESH)
            cp.wait_send()
            cp.wait_recv()

    return pl.pallas_call(
        body, name=name, out_shape=tuple(_hbm(l) for l in lands), in_specs=[HBM] * (2 * n) + [SEMS, SEMS, ANY],
        out_specs=tuple([HBM] * n), input_output_aliases={n + i: i for i in range(n)},
        compiler_params=pltpu.CompilerParams(has_side_effects=EFFECT))(*srcs, *lands, send_sems, recv_sems, after)


def _gather_d2d(blocks, lands, name):
    n = len(blocks)

    def body(*refs):
        x_refs, land_refs = refs[:n], refs[2 * n:3 * n]
        send_sems, recv_sems, in_sems, out_sems = refs[3 * n:3 * n + 4]
        stage = refs[3 * n + 4:]
        x, y, cc = lax.axis_index("x"), lax.axis_index("y"), lax.axis_index("c")
        sibling = (x, y, 1 - cc)
        staged = [pltpu.make_async_copy(x_refs[i], stage[i], in_sems.at[i]) for i in range(n)]
        for cp in staged:
            cp.start()
        copies = []
        for i in range(n):
            slot = land_refs[i].at[4 * x + 2 * y + cc]
            copies.append(pltpu.make_async_remote_copy(src_ref=x_refs[i], dst_ref=slot, send_sem=send_sems.at[4 * i],
                                                       recv_sem=recv_sems.at[4 * i], device_id=sibling, device_id_type=MESH))
            for j, (px, py) in enumerate(_other_chips(x, y)):
                slot = land_refs[i].at[4 * px + 2 * py + cc]
                copies.append(pltpu.make_async_remote_copy(src_ref=slot, dst_ref=slot, send_sem=send_sems.at[4 * i + 1 + j],
                                                           recv_sem=recv_sems.at[4 * i + 1 + j], device_id=sibling, device_id_type=MESH))
        for cp in copies:
            cp.start()
        mine = []
        for i in range(n):
            staged[i].wait()
            mine.append(pltpu.make_async_copy(stage[i], land_refs[i].at[4 * x + 2 * y + cc], out_sems.at[i]))
            mine[i].start()
        for i in range(n):
            slot = land_refs[i].at[4 * x + 2 * y + (1 - cc)]
            pltpu.make_async_remote_copy(src_ref=slot, dst_ref=slot, send_sem=send_sems.at[4 * i], recv_sem=recv_sems.at[4 * i],
                                         device_id=sibling, device_id_type=MESH).wait_recv()
            for j, (px, py) in enumerate(_other_chips(x, y)):
                slot = land_refs[i].at[4 * px + 2 * py + (1 - cc)]
                pltpu.make_async_remote_copy(src_ref=slot, dst_ref=slot, send_sem=send_sems.at[4 * i + 1 + j],
                                             recv_sem=recv_sems.at[4 * i + 1 + j], device_id=sibling, device_id_type=MESH).wait_recv()
        for cp in copies:
            cp.wait_send()
        for cp in mine:
            cp.wait()

    return pl.pallas_call(
        body, out_shape=tuple(_sds(l.shape, l.dtype) for l in lands), in_specs=[ANY] * (2 * n), out_specs=(ANY,) * n,
        input_output_aliases={n + i: i for i in range(n)},
        scratch_shapes=[pltpu.SemaphoreType.DMA((4 * n,)), pltpu.SemaphoreType.DMA((4 * n,)), pltpu.SemaphoreType.DMA((n,)),
                        pltpu.SemaphoreType.DMA((n,))] + [pltpu.VMEM(b.shape, b.dtype) for b in blocks],
        name=name, compiler_params=_cp())(*blocks, *lands)


def _sum_own(parts, recvs, mine, name):
    n = len(parts)

    def body(c_ref, *refs):
        s = pl.program_id(0)
        for i in range(n):
            val = jnp.where(c_ref[0] == s, refs[i][...], refs[n + i][...]).astype(F32)
            _acc(refs[2 * n + i], val, s == 0)

    kept = [pl.BlockSpec((None,) + p.shape[1:], lambda s, cref: (cref[0], 0, 0)) for p in parts]
    ins = [pl.BlockSpec((None,) + p.shape[1:], lambda s, cref: (s, 0, 0)) for p in parts]
    return pl.pallas_call(
        body, grid_spec=pltpu.PrefetchScalarGridSpec(
            num_scalar_prefetch=1, grid=(parts[0].shape[0],), in_specs=kept + ins,
            out_specs=tuple(pl.BlockSpec(p.shape[1:], lambda s, cref: (0, 0)) for p in parts)),
        out_shape=tuple(_sds(p.shape[1:], F32) for p in parts), name=name, compiler_params=_cp())(mine, *parts, *recvs)


BIG = (("w_in", True), ("w_gate", True), ("w_mem_kv", False), ("w_branch", True), ("w_out", False), ("w_ffn_in", True),
       ("w_ffn_out", False))

SMALL = ("norm_mix_g", "norm_mem_g", "ret_decay_fwd", "ret_decay_bwd", "ret_norm_g", "pool_w", "pool_scale", "na_q_norm_g",
         "na_k_norm_g", "na_rpb", "mem_q_norm_g", "mem_k_norm_g", "norm_ffn_g")
WEIGHTS = ("norm_mix_g", "norm_mem_g", "w_in", "w_gate", "ret_decay_fwd", "ret_decay_bwd", "ret_norm_g", "pool_w", "pool_scale",
           "na_q_norm_g", "na_k_norm_g", "na_rpb", "mem_q_norm_g", "mem_k_norm_g", "w_mem_kv", "w_branch", "w_out", "norm_ffn_g",
           "w_ffn_in", "w_ffn_out")


def _to_exchange(name, transposed, shard):
    if name == "w_branch":
        return jnp.swapaxes(shard, 1, 2).reshape(NH * (D // N_DEV), BW)
    return shard.T if transposed else shard


def _from_exchange(name, transposed, block):
    if name == "w_branch":
        return jnp.swapaxes(block.reshape(NH, D // N_DEV, BW), 1, 2)
    return block.T if transposed else block


def _whole_from_gathered(name, g):
    if name == "w_branch":
        return jnp.swapaxes(g.reshape(N_DEV, NH, D // N_DEV, BW), 0, 1).reshape(NH, D, BW)
    return g.reshape(N_DEV * g.shape[1], g.shape[2])


def _by_destination(name, g):
    if name == "w_branch":
        g = jnp.swapaxes(g.reshape(NH, N_DEV, D // N_DEV, BW), 0, 1).reshape(N_DEV * NH * (D // N_DEV), BW)
    return g.reshape(4, 2, g.shape[0] // N_DEV, g.shape[1])


SMALL_PAD = 1024


def _pack_small(vals, loss=None):
    parts = [vals[n] for n in SMALL] + [jnp.zeros((1,), F32) if loss is None else loss.reshape(1)]
    rows = []
    for p in parts:
        flat = p.reshape(-1)
        rows.append(jnp.pad(flat, (0, -flat.shape[0] % SMALL_PAD)).reshape(-1, 128))
    return jnp.concatenate(rows, axis=0)


def _unpack_small(packed, like):
    out, off = {}, 0
    for n in SMALL:
        sz = int(np.prod(like[n].shape))
        nrow = -(-sz // SMALL_PAD) * (SMALL_PAD // 128)
        out[n] = packed[off:off + nrow].reshape(-1)[:sz].reshape(like[n].shape)
        off += nrow
    return out, packed[off, 0]


def _na_constants():
    c = np.arange(GRID_W)
    win = np.clip(c - NA_COLS_WIN // 2, 0, GRID_W - NA_COLS_WIN)
    kc = np.arange(GRID_W)
    inside = (kc[None, :] >= win[:, None]) & (kc[None, :] < win[:, None] + NA_COLS_WIN)
    off = kc[None, :] - c[:, None] + NA_COLS_WIN - 1
    onehot = np.zeros((128, GRID_W, GRID_W), np.float32)
    for b in range(2 * NA_COLS_WIN - 1):
        onehot[b] = (off == b) & inside
    maskadd = np.where(inside, 0.0, NEG).astype(np.float32)
    return onehot.reshape(128, GRID_W * GRID_W), maskadd


def _na_bias_table(tab, maskadd):
    n_off = 2 * NA_ROWS_WIN - 1
    t4 = tab[:NH * n_off].reshape(NH, n_off, GRID_W, GRID_W) + maskadd[None, None]
    by_off = t4.transpose(1, 0, 2, 3).reshape(n_off, NH * GRID_W, GRID_W)
    return jnp.concatenate([by_off[:-1], by_off[1:]], axis=-1)


def _rotary_tables(t):
    half = HD // 2
    inv = ROPE_THETA ** (-jnp.arange(half, dtype=F32) / half)
    ang = jnp.arange(t, dtype=F32)[:, None] * inv[None, :]
    cos, sin = jnp.cos(ang), jnp.sin(ang)
    return jnp.tile(jnp.concatenate([cos, cos], axis=-1), (1, NH)), jnp.tile(jnp.concatenate([-sin, sin], axis=-1), (1, NH))


def _block_diag(pw):
    out = jnp.zeros((BW, BW), pw.dtype)
    for g in range(NH):
        out = lax.dynamic_update_slice(out, pw[g], (g * HD, g * HD))
    return out


def _tile4(g):
    return jnp.tile(g.reshape(1, HD), (1, NH))


def _layer_fwd(x, mem, sw, lw, consts, fetch, h=None, next_norm_g=None):
    cos2, sin2, onehot, maskadd = consts
    if h is None:
        h = _rmsnorm_fwd(x, sw["norm_mix_g"].reshape(1, D), "norm_mix_fwd")
    proj = _mm(h, lw["w_in"], tb=True, name="mm_in")
    gp = _mm(h, lw["w_gate"], tb=True, out_dtype=BF16, name="mm_gate")
    g_naq, g_nak, g_mq = _tile4(sw["na_q_norm_g"]), _tile4(sw["na_k_norm_g"]), _tile4(sw["mem_q_norm_g"])
    rq, rk, rv, nq, nk, nv, mq = _prep_fwd(proj, cos2, sin2, g_naq, g_nak, g_mq)

    lgf, lgb = jax.nn.log_sigmoid(sw["ret_decay_fwd"]), jax.nn.log_sigmoid(sw["ret_decay_bwd"])
    g_ret = sw["ret_norm_g"].reshape(1, BW)
    o_ret, ret = _ret_fwd(rq, rk, rv, proj, lgf, lgb, g_ret)

    wbd = _block_diag(sw["pool_w"]).astype(BF16)
    p_scale = sw["pool_scale"].reshape(1, BW)
    pool = _pool_fwd(proj, wbd, p_scale)

    rpb_pad = jnp.pad(sw["na_rpb"].reshape(NH * 15, 31), ((0, 4), (0, 97)))
    ball = _na_bias_table(_rpb_expand(rpb_pad, onehot), maskadd)
    na = _na_fwd(nq, nk, nv, ball)

    lw.update(fetch(1, na))
    memn = _rmsnorm_fwd(mem, sw["norm_mem_g"].reshape(1, D), "norm_mem_fwd")
    kv = _mm(memn, lw["w_mem_kv"], name="mm_memkv")
    g_mk = _tile4(sw["mem_k_norm_g"])
    mk, mv = _memkv_prep(kv, g_mk)
    mo = _mem_fwd(mq, mk, mv)

    br = (ret, pool, na, mo)
    merged = _merge_fwd(br, lw["w_branch"], gp)
    x1, h2 = _mm(merged, lw["w_out"], add=x, norm_g=sw["norm_ffn_g"].reshape(1, D), name="mm_out")
    lw.update(fetch(2, x1))
    ffa, ffg, yff = _ffn_in_fwd(h2, lw["w_ffn_in"])
    if next_norm_g is None:
        x2, h_next = _mm(yff, lw["w_ffn_out"], add=x1, name="mm_ffn_out"), None
    else:
        x2, h_next = _mm(yff, lw["w_ffn_out"], add=x1, norm_g=next_norm_g.reshape(1, D), name="mm_ffn_out")
    saved = dict(x=x, h=h, proj=proj, gp=gp, rq=rq, rk=rk, rv=rv, nq=nq, nk=nk, nv=nv, mq=mq, o_ret=o_ret, ball=ball, memn=memn,
                 kv=kv, mk=mk, mv=mv, br=br, merged=merged, x1=x1, h2=h2, ffa=ffa, ffg=ffg, yff=yff, lgf=lgf, lgb=lgb, wbd=wbd)
    return x2, h_next, saved


def _layer_bwd(dx2, dx2b, mem, sw, lw, sv, consts, dep=None):
    cos2, sin2, onehot, maskadd = consts
    gb, gs = {}, {}
    d_a, d_g = _ffn_out_bwd(dx2b, lw["w_ffn_out"], sv["ffa"], sv["ffg"], dep)
    gb["w_ffn_out"] = _mm(sv["yff"], dx2b, ta=True, out_dtype=BF16, name="mm_ffn_out_dw")
    dh2 = _mm(d_a, lw["w_ffn_in"], b_half=0, name="mm_ffn_in_dx_a")
    dx1, dx1b, dg = _mm_norm_bwd(d_g, lw["w_ffn_in"], dh2, sv["x1"], sw["norm_ffn_g"].reshape(1, D), dx2, b_half=1,
                                 name="mm_ffn_in_dx_g")
    gs["norm_ffn_g"] = dg.reshape(D)
    dw_a = _mm(d_a, sv["h2"], ta=True, out_dtype=BF16, out_half=(0, None), name="mm_ffn_in_dw_a")
    gb["w_ffn_in"] = _mm(d_g, sv["h2"], ta=True, out_dtype=BF16, out_half=(1, dw_a), name="mm_ffn_in_dw_g")

    dmerged = _mm(dx1b, lw["w_out"], tb=True, name="mm_out_dx")
    gb["w_out"] = _mm(sv["merged"], dx1b, ta=True, out_dtype=BF16, name="mm_out_dw")
    dgp, dup, dbr = _merge_bwd(dmerged, sv["br"], lw["w_branch"], sv["gp"])
    gb["w_branch"] = _dwbranch(sv["br"], dup)

    g_ret = sw["ret_norm_g"].reshape(1, BW)
    do_ret, d_rg, dg_ret = _ret_post_bwd(dbr, sv["o_ret"], sv["proj"], g_ret)
    d_rq, d_rk, d_rv, dlg = _ret_bwd(do_ret, sv["rq"], sv["rk"], sv["rv"], sv["lgf"], sv["lgb"])
    gs["ret_norm_g"] = dg_ret.reshape(BW)
    _, vjp_f = jax.vjp(jax.nn.log_sigmoid, sw["ret_decay_fwd"])
    _, vjp_b = jax.vjp(jax.nn.log_sigmoid, sw["ret_decay_bwd"])
    gs["ret_decay_fwd"] = vjp_f(dlg[0:NH, 0])[0]
    gs["ret_decay_bwd"] = vjp_b(dlg[NH:2 * NH, 0])[0]

    p_scale = sw["pool_scale"].reshape(1, BW)
    d_pv, dwbd, dscale = _pool_bwd(dbr, sv["proj"], sv["wbd"], p_scale)
    gs["pool_w"] = jnp.stack([dwbd[g * HD:(g + 1) * HD, g * HD:(g + 1) * HD] for g in range(NH)])
    gs["pool_scale"] = dscale.reshape(BW)

    d_nq, d_nk, d_nv, dball = _na_bwd(dbr, sv["nq"], sv["nk"], sv["nv"], sv["ball"])
    _, vjp_tab = jax.vjp(lambda tab: _na_bias_table(tab, maskadd), jnp.zeros((64, GRID_W * GRID_W), F32))
    drpb = _rpb_reduce(vjp_tab(dball)[0], onehot)
    gs["na_rpb"] = drpb[:NH * 15, :31].reshape(NH, 15, 31)

    d_mq, d_mk, d_mv = _mem_bwd(dbr, sv["mq"], sv["mk"], sv["mv"])
    g_mk = _tile4(sw["mem_k_norm_g"])
    dkv, dg_mk = _memkv_bwd(sv["kv"], d_mk, d_mv, g_mk)
    gs["mem_k_norm_g"] = dg_mk.reshape(NH, HD).sum(0)
    gb["w_mem_kv"] = _mm(sv["memn"], dkv, ta=True, out_dtype=BF16, name="mm_memkv_dw")
    dmemn = _mm(dkv, lw["w_mem_kv"], tb=True, name="mm_memkv_dx")
    _, _, dg_mem = _rmsnorm_bwd(dmemn, mem, sw["norm_mem_g"].reshape(1, D), jnp.zeros_like(mem), "norm_mem_bwd")
    gs["norm_mem_g"] = dg_mem.reshape(D)

    g_naq, g_nak, g_mq = _tile4(sw["na_q_norm_g"]), _tile4(sw["na_k_norm_g"]), _tile4(sw["mem_q_norm_g"])
    dproj, dg_naq, dg_nak, dg_mq = _prep_bwd(sv["proj"], cos2, sin2, g_naq, g_nak, g_mq, d_rq, d_rk, d_rv, d_rg, d_pv, d_nq, d_nk,
                                             d_nv, d_mq)
    gs["na_q_norm_g"] = dg_naq.reshape(NH, HD).sum(0)
    gs["na_k_norm_g"] = dg_nak.reshape(NH, HD).sum(0)
    gs["mem_q_norm_g"] = dg_mq.reshape(NH, HD).sum(0)

    gb["w_in"] = _mm(dproj, sv["h"], ta=True, out_dtype=BF16, name="mm_in_dw")
    gb["w_gate"] = _mm(dgp, sv["h"], ta=True, out_dtype=BF16, name="mm_gate_dw")
    dh = _mm(dproj, lw["w_in"], name="mm_in_dx")
    dx, dxb, dg = _mm_norm_bwd(dgp, lw["w_gate"], dh, sv["x"], sw["norm_mix_g"].reshape(1, D), dx1, name="mm_gate_dx")
    gs["norm_mix_g"] = dg.reshape(D)
    return dx, dxb, gb, gs


def _local_step(x, mem, target, small, get_layer, on_grads):
    t = x.shape[0]
    cos2, sin2 = _rotary_tables(t)
    onehot, maskadd = _na_constants()
    consts = (cos2, sin2, jnp.asarray(onehot), jnp.asarray(maskadd))
    saved, weights, cur, h = [], [], x, None
    for l in range(DEPTH):
        sw = {n: small[n][l] for n in SMALL}
        lw, fetch = get_layer(l, cur)
        weights.append(lw)
        cur, h, sv = _layer_fwd(cur, mem, sw, lw, consts, fetch, h, small["norm_mix_g"][l + 1] if l + 1 < DEPTH else None)
        saved.append(sv)
    dy, dyb, loss_tile = _loss_head(cur, target)
    small_g = {n: [None] * DEPTH for n in SMALL}
    dep = None
    for l in reversed(range(DEPTH)):
        sw = {n: small[n][l] for n in SMALL}
        dy, dyb, gb, gs = _layer_bwd(dy, dyb, mem, sw, weights[l], saved[l], consts, dep)
        dep = on_grads(l, gb, dy)
        for n in SMALL:
            small_g[n][l] = gs[n]
    return loss_tile[0, 0], dy, {n: jnp.stack(v) for n, v in small_g.items()}


def _flat2d(a):
    return a.reshape(-1, a.shape[-1])


def kernel(x, mem, norm_mix_g, norm_mem_g, w_in, w_gate, ret_decay_fwd, ret_decay_bwd, ret_norm_g, pool_w, pool_scale, na_q_norm_g, na_k_norm_g, na_rpb, mem_q_norm_g, mem_k_norm_g, w_mem_kv, w_branch, w_out, norm_ffn_g, w_ffn_in, w_ffn_out, loss_target, m_norm_mix_g, m_norm_mem_g, m_w_in, m_w_gate, m_ret_decay_fwd, m_ret_decay_bwd, m_ret_norm_g, m_pool_w, m_pool_scale, m_na_q_norm_g, m_na_k_norm_g, m_na_rpb, m_mem_q_norm_g, m_mem_k_norm_g, m_w_mem_kv, m_w_branch, m_w_out, m_norm_ffn_g, m_w_ffn_in, m_w_ffn_out, v_norm_mix_g, v_norm_mem_g, v_w_in, v_w_gate, v_ret_decay_fwd, v_ret_decay_bwd, v_ret_norm_g, v_pool_w, v_pool_scale, v_na_q_norm_g, v_na_k_norm_g, v_na_rpb, v_mem_q_norm_g, v_mem_k_norm_g, v_w_mem_kv, v_w_branch, v_w_out, v_norm_ffn_g, v_w_ffn_in, v_w_ffn_out):
    w = dict(norm_mix_g=norm_mix_g, norm_mem_g=norm_mem_g, w_in=w_in, w_gate=w_gate, ret_decay_fwd=ret_decay_fwd,
             ret_decay_bwd=ret_decay_bwd, ret_norm_g=ret_norm_g, pool_w=pool_w, pool_scale=pool_scale, na_q_norm_g=na_q_norm_g,
             na_k_norm_g=na_k_norm_g, na_rpb=na_rpb, mem_q_norm_g=mem_q_norm_g, mem_k_norm_g=mem_k_norm_g, w_mem_kv=w_mem_kv,
             w_branch=w_branch, w_out=w_out, norm_ffn_g=norm_ffn_g, w_ffn_in=w_ffn_in, w_ffn_out=w_ffn_out)
    m = dict(norm_mix_g=m_norm_mix_g, norm_mem_g=m_norm_mem_g, w_in=m_w_in, w_gate=m_w_gate, ret_decay_fwd=m_ret_decay_fwd,
             ret_decay_bwd=m_ret_decay_bwd, ret_norm_g=m_ret_norm_g, pool_w=m_pool_w, pool_scale=m_pool_scale, na_q_norm_g=m_na_q_norm_g,
             na_k_norm_g=m_na_k_norm_g, na_rpb=m_na_rpb, mem_q_norm_g=m_mem_q_norm_g, mem_k_norm_g=m_mem_k_norm_g, w_mem_kv=m_w_mem_kv,
             w_branch=m_w_branch, w_out=m_w_out, norm_ffn_g=m_norm_ffn_g, w_ffn_in=m_w_ffn_in, w_ffn_out=m_w_ffn_out)
    v = dict(norm_mix_g=v_norm_mix_g, norm_mem_g=v_norm_mem_g, w_in=v_w_in, w_gate=v_w_gate, ret_decay_fwd=v_ret_decay_fwd,
             ret_decay_bwd=v_ret_decay_bwd, ret_norm_g=v_ret_norm_g, pool_w=v_pool_w, pool_scale=v_pool_scale, na_q_norm_g=v_na_q_norm_g,
             na_k_norm_g=v_na_k_norm_g, na_rpb=v_na_rpb, mem_q_norm_g=v_mem_q_norm_g, mem_k_norm_g=v_mem_k_norm_g, w_mem_kv=v_w_mem_kv,
             w_branch=v_w_branch, w_out=v_w_out, norm_ffn_g=v_norm_ffn_g, w_ffn_in=v_w_ffn_in, w_ffn_out=v_w_ffn_out)
    assert x.shape == (1, 2048, D) and mem.shape == (1, N_MEM, D) and w_in.shape == (DEPTH, D, 9 * BW // N_DEV)

    started = []
    for l in range(DEPTH):
        blocks = [_to_exchange(name, tr, w[name][l]).astype(BF16) for name, tr in BIG]
        lands = [lax.empty((N_DEV,) + b.shape, BF16) for b in blocks]
        started.append(_ici_start(blocks, lands, "gather", "gather_ici_start_%d" % l))
    all_started = started[0][4] + started[1][4] + started[2][4] + started[3][4]

    def get_group(l, only, after, tag):
        lands = _ici_wait(started[l], after, "gather_ici_wait_%d%s" % (l, tag), only)
        whole = _gather_d2d([started[l][2][i] for i in only], lands, "gather_d2d")
        return {BIG[i][0]: _whole_from_gathered(BIG[i][0], g) for i, g in zip(only, whole)}

    def get_layer(l, after):
        if l > 0:
            return get_group(l, list(range(len(BIG))), after, ""), lambda stage, after2: {}
        groups = [[0, 1], [2, 3, 4], [5, 6]]
        return get_group(l, groups[0], all_started, "a"), lambda stage, after2: get_group(l, groups[stage], after2, "abc"[stage])

    cidx = lax.axis_index("c").astype(jnp.int32).reshape(1)
    chip = (2 * lax.axis_index("x") + lax.axis_index("y")).astype(jnp.int32).reshape(1)
    in_flight = []

    def flip_of(name, tr):
        return (lambda a: jnp.swapaxes(a, 1, 2)) if name in ("w_in", "w_ffn_in") else (lambda a: a)

    def rows3(a):
        return a.reshape(DEPTH, -1, a.shape[-1])

    opt_in = {name: tuple(rows3(flip_of(name, tr)(t[name])) for t in (w, m, v)) for name, tr in BIG}
    opt_out = {name: tuple(lax.empty(opt_in[name][0].shape, F32) for _ in range(4)) for name, _ in BIG}

    device = (2 * chip + cidx).astype(jnp.int32)

    def finish(l, st, after):
        recv = _ici_wait(st, after, "rs_ici_wait_%d" % l)
        sums = _sum_own(st[2], recv, chip if st[5] == 3 else device, "rs_sum")
        for (name, tr), s in zip(BIG, sums):
            g = s if name in ("w_in", "w_ffn_in") else _from_exchange(name, tr, s)
            wx, mx, vx = opt_in[name]
            opt_out[name] = _adamw_layer(l, wx, g.reshape(-1, g.shape[-1]), mx, vx, opt_out[name], "adamw_" + name)

    def on_grads(l, gb, after):
        send = [_by_destination(name, gb[name]) for name, _ in BIG]
        if l > 0:
            send = [s.reshape((N_DEV,) + s.shape[2:]) for s in send]
            st = _ici_start(send, [lax.empty(s.shape, BF16) for s in send], "by_device", "rs_ici_start_%d" % l)
        else:
            from_core = _rs_core_swap(send, "rs_core_swap")
            chip_part = _pair_sum(send, from_core, cidx)
            st = _ici_start(chip_part, [lax.empty(p.shape, BF16) for p in chip_part], "by_chip", "rs_ici_start_%d" % l)
        in_flight.append((l, st))
        return st[4]

    loss_local, dx, small_g = _local_step(x[0], mem[0], loss_target[0], {n: w[n] for n in SMALL}, get_layer, on_grads)

    last_started = in_flight[-1][1][4]
    for l, st in in_flight[:-1]:
        finish(l, st, last_started)

    small_all, = _all_gather([_pack_small(small_g, loss_local) + last_started[0:1]], "gather_small")
    packed_g = _sum_slots(small_all, "small_sum")
    small_sum, loss = _unpack_small(packed_g, {n: w[n] for n in SMALL})
    d_, m_, v_ = _adamw(_pack_small({n: w[n] for n in SMALL}), packed_g, _pack_small({n: m[n] for n in SMALL}),
                        _pack_small({n: v[n] for n in SMALL}), "adamw_small")
    updated = d_[0:8]
    for name, _ in BIG:
        updated = updated + opt_out[name][0][1, 0:8, 0:128]
    finish(*in_flight[-1], updated)

    grads, delta, new_m, new_v = {}, {}, {}, {}
    for name, tr in BIG:
        shape = flip_of(name, tr)(w[name]).shape
        delta[name], new_m[name], new_v[name], grads[name] = (flip_of(name, tr)(a.reshape(shape)) for a in opt_out[name])
    like = {n: w[n] for n in SMALL}
    ds, _ = _unpack_small(d_, like)
    ms, _ = _unpack_small(m_, like)
    vs, _ = _unpack_small(v_, like)
    for n in SMALL:
        grads[n], delta[n], new_m[n], new_v[n] = small_sum[n], ds[n], ms[n], vs[n]

    return (loss, dx[None], *[grads[n] for n in WEIGHTS], *[delta[n] for n in WEIGHTS], *[new_m[n] for n in WEIGHTS],
            *[new_v[n] for n in WEIGHTS])
```

```python
import functools

import numpy as np
import jax
import jax.numpy as jnp
from jax import lax
from jax.experimental import pallas as pl
from jax.experimental.pallas import tpu as pltpu

F32 = jnp.float32
BF16 = jnp.bfloat16
MXU = jnp.bfloat16
HI = lax.Precision.HIGHEST

DEPTH = 4
D = 1024
BW = 256
HD = 64
NH = 4
GRID_W = 64
NA_ROWS_WIN = 8
NA_COLS_WIN = 16
N_MEM = 256
FF = 2816
EPS = 1e-6
NEG = -1e30
ROPE_THETA = 10000.0
POOL_HALF_MAX = 8

ADAM_LR, ADAM_B1, ADAM_B2, ADAM_EPS, ADAM_WD, ADAM_STEP = 0.001, 0.9, 0.999, 1e-08, 0.01, 10

N_DEV = 8
VMEM_LIMIT = 56 * 1024 * 1024

RQ, RK, RV, RG, PV, NQ, NK, NV, MQ = range(9)

MESH = pl.DeviceIdType.MESH
ANY = pl.BlockSpec(memory_space=pl.ANY)
SMEM = pl.BlockSpec(memory_space=pltpu.SMEM)


def _cp(**kw):
    return pltpu.CompilerParams(vmem_limit_bytes=VMEM_LIMIT, **kw)


def _tile(n, cap):
    if n <= cap:
        return n
    best = None
    for t in range(128, cap + 1, 128):
        if n % t == 0:
            best = t
    assert best is not None, (n, cap)
    return best


def _sds(shape, dtype):
    return jax.ShapeDtypeStruct(shape, dtype)


def _lane_head(shape):
    return lax.shift_right_logical(lax.broadcasted_iota(jnp.int32, shape, len(shape) - 1), 6)


def _group_mean(z):
    i = lax.shift_right_logical(lax.broadcasted_iota(jnp.int32, (BW, BW), 0), 6)
    j = lax.shift_right_logical(lax.broadcasted_iota(jnp.int32, (BW, BW), 1), 6)
    g = jnp.where(i == j, 1.0 / HD, 0.0).astype(BF16)
    z_hi = z.astype(BF16)
    z_lo = (z - z_hi.astype(F32)).astype(BF16)
    return jnp.dot(z_hi, g, preferred_element_type=F32) + jnp.dot(z_lo, g, preferred_element_type=F32)


def _gnorm(t, g):
    r = lax.rsqrt(_group_mean(t * t) + EPS)
    return t * r * g


def _gnorm_bwd(dy, t, g):
    r = lax.rsqrt(_group_mean(t * t) + EPS)
    th = t * r
    dth = dy * g
    dt = r * (dth - th * _group_mean(dth * th))
    return dt, dy * th


def _swap_halves(t):
    lane = lax.broadcasted_iota(jnp.int32, t.shape, 1)
    return jnp.where((lane & 63) < 32, pltpu.roll(t, BW - 32, 1), pltpu.roll(t, 32, 1))


def _sigmoid(x):
    return 1.0 / (1.0 + jnp.exp(-x))


def _dot(a, b, ta=False, tb=False):
    return lax.dot_general(a.astype(MXU), b.astype(MXU), (((0 if ta else 1,), (1 if tb else 0,)), ((), ())),
                           preferred_element_type=F32)


def _stack_heads(t):
    head = _lane_head(t.shape)
    return jnp.concatenate([jnp.where(head == h, t, jnp.zeros_like(t)) for h in range(NH)], axis=0)


def _unstack_heads(t, rows):
    head = _lane_head((rows, BW))
    out = jnp.zeros((rows, BW), F32)
    for h in range(NH):
        out = out + jnp.where(head == h, t[h * rows:(h + 1) * rows], 0.0)
    return out


def _softmax_rows(s):
    m = jnp.max(s, axis=-1, keepdims=True)
    e = jnp.exp(s - m)
    return e / jnp.sum(e, axis=-1, keepdims=True)


def _acc(ref, val, first):
    @pl.when(first)
    def _():
        ref[...] = val

    @pl.when(jnp.logical_not(first))
    def _():
        ref[...] += val


def _mm(a, b, *, ta=False, tb=False, out_dtype=F32, add=None, dep=None, b_half=None, out_half=None, norm_g=None, name):
    m, k = (a.shape[1], a.shape[0]) if ta else a.shape
    n = b.shape[0] if tb else b.shape[1]
    assert b_half is None or (not tb and b.shape[0] == 2 * k)
    tm, tn = _tile(m, 1408), (n if norm_g is not None else _tile(n, 768))
    n_in = 2 + (add is not None) + (dep is not None) + (out_half is not None) + (norm_g is not None)

    def body(*refs):
        a_ref, b_ref, o_ref = refs[0], refs[1], refs[n_in]
        r = _dot(a_ref[...], b_ref[...], ta, tb)
        if add is not None:
            r = r + refs[2][...]
        o_ref[...] = r.astype(out_dtype)
        if norm_g is not None:
            scale = lax.rsqrt(jnp.mean(r * r, axis=-1, keepdims=True) + EPS)
            refs[n_in + 1][...] = (r * scale * refs[n_in - 1][...]).astype(BF16)

    kb = 0 if b_half is None else b_half
    a_spec = pl.BlockSpec((k, tm), lambda i, j: (0, i)) if ta else pl.BlockSpec((tm, k), lambda i, j: (i, 0))
    b_spec = pl.BlockSpec((tn, k), lambda i, j: (j, 0)) if tb else pl.BlockSpec((k, tn), lambda i, j: (kb, j))
    plain = pl.BlockSpec((tm, tn), lambda i, j: (i, j))
    ins, args = [a_spec, b_spec], [a, b]
    if add is not None:
        ins.append(plain)
        args.append(add)
    if dep is not None:
        ins.append(pl.BlockSpec((8, 128), lambda i, j: (0, 0)))
        args.append(dep)
    o_spec, o_shape, aliases = plain, _sds((m, n), out_dtype), {}
    if out_half is not None:
        half, prev = out_half
        o_spec = pl.BlockSpec((tm, tn), lambda i, j: (i + half * (m // tm), j))
        o_shape = _sds((2 * m, n), out_dtype)
        ins.append(ANY)
        args.append(lax.empty((2 * m, n), out_dtype) if prev is None else prev)
        aliases = {len(args) - 1: 0}
    if norm_g is not None:
        ins.append(pl.BlockSpec((1, n), lambda i, j: (0, 0)))
        args.append(norm_g)
        o_spec, o_shape = (o_spec, plain), (o_shape, _sds((m, n), BF16))
    return pl.pallas_call(
        body, grid=(m // tm, n // tn), in_specs=ins, out_specs=o_spec, out_shape=o_shape, input_output_aliases=aliases, name=name,
        compiler_params=_cp(dimension_semantics=("parallel", "parallel")))(*args)


def _mm_norm_bwd(a, b, add, x, g, res, *, b_half=None, name):
    m, k = a.shape
    n = b.shape[1]
    tm = 512
    kb = 0 if b_half is None else b_half

    def body(a_ref, b_ref, c_ref, x_ref, g_ref, res_ref, dx_ref, dxb_ref, dg_ref):
        dhv = _dot(a_ref[...], b_ref[...]) + c_ref[...]
        xv = x_ref[...]
        r = lax.rsqrt(jnp.mean(xv * xv, axis=-1, keepdims=True) + EPS)
        xh = xv * r
        dxh = dhv * g_ref[...]
        dx = res_ref[...] + r * (dxh - xh * jnp.mean(dxh * xh, axis=-1, keepdims=True))
        dx_ref[...] = dx
        dxb_ref[...] = dx.astype(BF16)
        _acc(dg_ref, jnp.sum(dhv * xh, axis=0, keepdims=True), pl.program_id(0) == 0)

    row = pl.BlockSpec((tm, n), lambda i: (i, 0))
    vec = pl.BlockSpec((1, n), lambda i: (0, 0))
    return pl.pallas_call(
        body, grid=(m // tm,),
        in_specs=[pl.BlockSpec((tm, k), lambda i: (i, 0)), pl.BlockSpec((k, n), lambda i: (kb, 0)), row, row, vec, row],
        out_specs=(row, row, vec), out_shape=(_sds((m, n), F32), _sds((m, n), BF16), _sds((1, n), F32)), name=name,
        compiler_params=_cp())(a, b, add, x, g, res)


def _rmsnorm_fwd(x, g, name):
    t, d = x.shape
    tm = _tile(t, 256)

    def body(x_ref, g_ref, o_ref):
        xv = x_ref[...]
        r = lax.rsqrt(jnp.mean(xv * xv, axis=-1, keepdims=True) + EPS)
        o_ref[...] = (xv * r * g_ref[...]).astype(o_ref.dtype)

    return pl.pallas_call(
        body, grid=(t // tm,), in_specs=[pl.BlockSpec((tm, d), lambda i: (i, 0)), pl.BlockSpec((1, d), lambda i: (0, 0))],
        out_specs=pl.BlockSpec((tm, d), lambda i: (i, 0)), out_shape=_sds((t, d), BF16), name=name, compiler_params=_cp())(x, g)


def _rmsnorm_bwd(dh, x, g, res, name):
    t, d = x.shape
    tm = _tile(t, 256)

    def body(dh_ref, x_ref, g_ref, res_ref, dx_ref, dxb_ref, dg_ref):
        xv = x_ref[...]
        dhv = dh_ref[...]
        r = lax.rsqrt(jnp.mean(xv * xv, axis=-1, keepdims=True) + EPS)
        xh = xv * r
        dxh = dhv * g_ref[...]
        dx = res_ref[...] + r * (dxh - xh * jnp.mean(dxh * xh, axis=-1, keepdims=True))
        dx_ref[...] = dx
        dxb_ref[...] = dx.astype(BF16)
        _acc(dg_ref, jnp.sum(dhv * xh, axis=0, keepdims=True), pl.program_id(0) == 0)

    row = pl.BlockSpec((tm, d), lambda i: (i, 0))
    vec = pl.BlockSpec((1, d), lambda i: (0, 0))
    return pl.pallas_call(
        body, grid=(t // tm,), in_specs=[row, row, vec, row], out_specs=(row, row, vec),
        out_shape=(_sds((t, d), F32), _sds((t, d), BF16), _sds((1, d), F32)), name=name, compiler_params=_cp())(dh, x, g, res)


def _prep_fwd(proj, cos2, sin2, g_naq, g_nak, g_mq):
    t = proj.shape[0]
    tm = 256

    def body(p_ref, cos_ref, sin_ref, gq_ref, gk_ref, gm_ref, rq_ref, rk_ref, rv_ref, nq_ref, nk_ref, nv_ref, mq_ref):
        def col(c):
            return p_ref[:, c * BW:(c + 1) * BW]

        cosv, sinv = cos_ref[...], sin_ref[...]

        def rot(tv):
            return tv * cosv + _swap_halves(tv) * sinv

        rq_ref[...] = (rot(col(RQ)) * (HD ** -0.5)).astype(BF16)
        rk_ref[...] = rot(col(RK)).astype(BF16)
        rv_ref[...] = col(RV).astype(BF16)
        nq_ref[...] = _gnorm(col(NQ), gq_ref[...]).astype(BF16)
        nk_ref[...] = _gnorm(col(NK), gk_ref[...]).astype(BF16)
        nv_ref[...] = col(NV).astype(BF16)
        mq_ref[...] = _gnorm(col(MQ), gm_ref[...]).astype(BF16)

    blk = pl.BlockSpec((tm, BW), lambda i: (i, 0))
    vec = pl.BlockSpec((1, BW), lambda i: (0, 0))
    return pl.pallas_call(
        body, grid=(t // tm,), in_specs=[pl.BlockSpec((tm, 9 * BW), lambda i: (i, 0)), blk, blk, vec, vec, vec],
        out_specs=tuple(blk for _ in range(7)), out_shape=tuple(_sds((t, BW), BF16) for _ in range(7)),
        name="prep_fwd", compiler_params=_cp())(proj, cos2, sin2, g_naq, g_nak, g_mq)


def _prep_bwd(proj, cos2, sin2, g_naq, g_nak, g_mq, d_rq, d_rk, d_rv, d_rg, d_pv, d_nq, d_nk, d_nv, d_mq):
    t = proj.shape[0]
    tm = 256

    def body(p_ref, cos_ref, sin_ref, gq_ref, gk_ref, gm_ref, drq_ref, drk_ref, drv_ref, drg_ref, dpv_ref, dnq_ref, dnk_ref,
             dnv_ref, dmq_ref, o_ref, dgq_ref, dgk_ref, dgm_ref):
        first = pl.program_id(0) == 0

        def col(c):
            return p_ref[:, c * BW:(c + 1) * BW]

        def put(c, v):
            o_ref[:, c * BW:(c + 1) * BW] = v.astype(BF16)

        cosv, sinv = cos_ref[...], sin_ref[...]

        def rot_t(dv):
            return dv * cosv + _swap_halves(dv * sinv)

        put(RQ, rot_t(drq_ref[...] * (HD ** -0.5)))
        put(RK, rot_t(drk_ref[...]))
        put(RV, drv_ref[...])
        put(RG, drg_ref[...])
        put(PV, dpv_ref[...])
        dq, gq = _gnorm_bwd(dnq_ref[...], col(NQ), gq_ref[...])
        put(NQ, dq)
        _acc(dgq_ref, jnp.sum(gq, axis=0, keepdims=True), first)
        dk, gk = _gnorm_bwd(dnk_ref[...], col(NK), gk_ref[...])
        put(NK, dk)
        _acc(dgk_ref, jnp.sum(gk, axis=0, keepdims=True), first)
        put(NV, dnv_ref[...])
        dm, gm = _gnorm_bwd(dmq_ref[...], col(MQ), gm_ref[...])
        put(MQ, dm)
        _acc(dgm_ref, jnp.sum(gm, axis=0, keepdims=True), first)

    blk = pl.BlockSpec((tm, BW), lambda i: (i, 0))
    vec = pl.BlockSpec((1, BW), lambda i: (0, 0))
    wide = pl.BlockSpec((tm, 9 * BW), lambda i: (i, 0))
    return pl.pallas_call(
        body, grid=(t // tm,), in_specs=[wide, blk, blk, vec, vec, vec] + [blk] * 9, out_specs=(wide, vec, vec, vec),
        out_shape=(_sds((t, 9 * BW), BF16), _sds((1, BW), F32), _sds((1, BW), F32), _sds((1, BW), F32)),
        name="prep_bwd", compiler_params=_cp())(proj, cos2, sin2, g_naq, g_nak, g_mq, d_rq, d_rk, d_rv, d_rg, d_pv, d_nq, d_nk,
                                                d_nv, d_mq)


RET_B = 256


def _ret_consts(lgf_ref, lgb_ref):
    bsz = RET_B
    head = _lane_head((1, BW))
    lf, lb = jnp.zeros((1, BW), F32), jnp.zeros((1, BW), F32)
    for h in range(NH):
        lf = lf + jnp.where(head == h, lgf_ref[h], 0.0)
        lb = lb + jnp.where(head == h, lgb_ref[h], 0.0)
    pos = lax.broadcasted_iota(jnp.int32, (bsz, BW), 0).astype(F32)
    up, down = pos + 1.0, (bsz - 1.0) - pos
    c = dict(up=up, down=down, kf=jnp.exp(down * lf), kb=jnp.exp(up * lb), qf=jnp.exp(up * lf), qb=jnp.exp(down * lb),
             cf=jnp.exp(bsz * lf), cb=jnp.exp(bsz * lb))
    diff = (lax.broadcasted_iota(jnp.int32, (NH * bsz, 1), 0) & (bsz - 1)) - lax.broadcasted_iota(jnp.int32, (1, bsz), 1)
    c["causal"] = diff >= 0
    c["dist"] = jnp.abs(diff).astype(F32)
    lgf = jnp.concatenate([jnp.full((bsz, 1), lgf_ref[h], F32) for h in range(NH)], axis=0)
    lgb = jnp.concatenate([jnp.full((bsz, 1), lgb_ref[h], F32) for h in range(NH)], axis=0)
    c["dm"] = jnp.exp(c["dist"] * jnp.where(c["causal"], lgf, lgb))
    c["bd"] = _lane_head((BW, BW)) == lax.shift_right_logical(lax.broadcasted_iota(jnp.int32, (BW, BW), 0), 6)
    return c


def _ret_states(k_ref, v_ref, st_ref, c, nb):
    bsz = RET_B

    def summary(b, decay):
        kb = k_ref[b * bsz:(b + 1) * bsz, :].astype(F32)
        return jnp.where(c["bd"], _dot(kb * decay, v_ref[b * bsz:(b + 1) * bsz, :], ta=True), 0.0)

    f = jnp.zeros((BW, BW), F32)
    for b in range(nb):
        st_ref[b] = f
        if b < nb - 1:
            f = c["cf"] * f + summary(b, c["kf"])
    g = jnp.zeros((BW, BW), F32)
    for b in reversed(range(nb)):
        st_ref[nb + b] = g
        if b > 0:
            g = c["cb"] * g + summary(b, c["kb"])


def _ret_fwd(q, k, v, proj, lgf, lgb, g_ret):
    t = q.shape[0]
    bsz, nb = RET_B, t // RET_B

    def body(lgf_ref, lgb_ref, q_ref, k_ref, v_ref, rg_ref, g_ref, o_ref, ret_ref, st_ref):
        c = _ret_consts(lgf_ref, lgb_ref)
        _ret_states(k_ref, v_ref, st_ref, c, nb)
        for b in range(nb):
            blk = slice(b * bsz, (b + 1) * bsz)
            qb, kb, vb = q_ref[blk, :], k_ref[blk, :], v_ref[blk, :]
            s = _dot(_stack_heads(qb), kb, tb=True)
            o = _unstack_heads(_dot(s * c["dm"], vb), bsz)
            q32 = qb.astype(F32)
            o = o + _dot(q32 * c["qf"], st_ref[b]) + _dot(q32 * c["qb"], st_ref[nb + b])
            o_ref[blk, :] = o
            rg = rg_ref[blk, :]
            ret_ref[blk, :] = (_gnorm(o, g_ref[...]) * (rg * _sigmoid(rg))).astype(BF16)

    whole = pl.BlockSpec((t, BW), lambda i: (0, 0))
    return pl.pallas_call(
        body, grid=(1,),
        in_specs=[SMEM, SMEM, whole, whole, whole, pl.BlockSpec((t, BW), lambda i: (0, RG)), pl.BlockSpec((1, BW), lambda i: (0, 0))],
        out_specs=(whole, whole), out_shape=(_sds((t, BW), F32), _sds((t, BW), BF16)),
        scratch_shapes=[pltpu.VMEM((2 * nb, BW, BW), F32)], name="ret_fwd", compiler_params=_cp())(lgf, lgb, q, k, v, proj, g_ret)


def _ret_post_bwd(dbr, o_ret, proj, g_ret):
    t = o_ret.shape[0]
    tm = 256

    def body(d_ref, o_ref, rg_ref, g_ref, do_ref, drg_ref, dg_ref):
        dret, o, rg, g = d_ref[...], o_ref[...], rg_ref[...], g_ref[...]
        sg = _sigmoid(rg)
        do, dgain = _gnorm_bwd(dret * (rg * sg), o, g)
        do_ref[...] = do.astype(BF16)
        drg_ref[...] = dret * _gnorm(o, g) * (sg * (1.0 + rg * (1.0 - sg)))
        _acc(dg_ref, jnp.sum(dgain, axis=0, keepdims=True), pl.program_id(0) == 0)

    blk = pl.BlockSpec((tm, BW), lambda i: (i, 0))
    vec = pl.BlockSpec((1, BW), lambda i: (0, 0))
    return pl.pallas_call(
        body, grid=(t // tm,), in_specs=[blk, blk, pl.BlockSpec((tm, BW), lambda i: (i, RG)), vec], out_specs=(blk, blk, vec),
        out_shape=(_sds((t, BW), BF16), _sds((t, BW), F32), _sds((1, BW), F32)), name="ret_post_bwd",
        compiler_params=_cp())(dbr, o_ret, proj, g_ret)


def _ret_bwd(do, q, k, v, lgf, lgb):
    t = q.shape[0]
    bsz, nb = RET_B, t // RET_B

    def body(lgf_ref, lgb_ref, d_ref, q_ref, k_ref, v_ref, dq_ref, dk_ref, dv_ref, dlg_ref, st_ref, sd_ref):
        c = _ret_consts(lgf_ref, lgb_ref)
        _ret_states(k_ref, v_ref, st_ref, c, nb)
        lane_f, lane_b = jnp.zeros((1, BW), F32), jnp.zeros((1, BW), F32)
        row_f, row_b = jnp.zeros((NH * bsz, 1), F32), jnp.zeros((NH * bsz, 1), F32)

        def rows(x):
            return jnp.sum(x, axis=0, keepdims=True)

        for b in range(nb):
            blk = slice(b * bsz, (b + 1) * bsz)
            qb, kb, vb, dob = q_ref[blk, :], k_ref[blk, :], v_ref[blk, :], d_ref[blk, :]
            q32 = qb.astype(F32)
            qs, dos = _stack_heads(qb), _stack_heads(dob)
            s = _dot(qs, kb, tb=True)
            da = _dot(dos, vb, tb=True)
            dv_ref[blk, :] = _dot(s * c["dm"], dos, ta=True)
            ds = da * c["dm"]
            w = ds * s * c["dist"]
            row_f = row_f + jnp.sum(jnp.where(c["causal"], w, 0.0), axis=1, keepdims=True)
            row_b = row_b + jnp.sum(jnp.where(c["causal"], 0.0, w), axis=1, keepdims=True)
            dsb = ds.astype(MXU)
            dk_ref[blk, :] = _dot(dsb, qs, ta=True)
            dq_f = _dot(dob, st_ref[b], tb=True) * c["qf"]
            dq_b = _dot(dob, st_ref[nb + b], tb=True) * c["qb"]
            lane_f = lane_f + rows(c["up"] * dq_f * q32)
            lane_b = lane_b + rows(c["down"] * dq_b * q32)
            dq_ref[blk, :] = _unstack_heads(_dot(dsb, kb), bsz) + dq_f + dq_b
            sd_ref[b] = jnp.where(c["bd"], _dot(q32 * c["qf"], dob, ta=True), 0.0)
            sd_ref[nb + b] = jnp.where(c["bd"], _dot(q32 * c["qb"], dob, ta=True), 0.0)

        def through_state(b, grad, decay, weight, lane):
            blk = slice(b * bsz, (b + 1) * bsz)
            k32 = k_ref[blk, :].astype(F32)
            dk = _dot(v_ref[blk, :], grad, tb=True) * decay
            dk_ref[blk, :] += dk
            dv_ref[blk, :] += _dot(k32 * decay, grad)
            return lane + rows(weight * dk * k32)

        phi = jnp.zeros((BW, BW), F32)
        for b in reversed(range(nb)):
            if b < nb - 1:
                lane_f = through_state(b, phi, c["kf"], c["down"], lane_f)
                lane_f = lane_f + bsz * rows(c["cf"] * st_ref[b] * phi)
            phi = sd_ref[b] + c["cf"] * phi
        gam = jnp.zeros((BW, BW), F32)
        for b in range(nb):
            if b > 0:
                lane_b = through_state(b, gam, c["kb"], c["up"], lane_b)
                lane_b = lane_b + bsz * rows(c["cb"] * st_ref[nb + b] * gam)
            gam = sd_ref[nb + b] + c["cb"] * gam

        head = _lane_head((1, BW))
        for h in range(NH):
            tot_f = jnp.sum(row_f[h * bsz:(h + 1) * bsz, :]) + jnp.sum(jnp.where(head == h, lane_f, 0.0))
            tot_b = jnp.sum(row_b[h * bsz:(h + 1) * bsz, :]) + jnp.sum(jnp.where(head == h, lane_b, 0.0))
            dlg_ref[h:h + 1, :] = jnp.full((1, 128), tot_f, F32)
            dlg_ref[NH + h:NH + h + 1, :] = jnp.full((1, 128), tot_b, F32)

    whole = pl.BlockSpec((t, BW), lambda i: (0, 0))
    return pl.pallas_call(
        body, grid=(1,), in_specs=[SMEM, SMEM, whole, whole, whole, whole],
        out_specs=(whole, whole, whole, pl.BlockSpec((2 * NH, 128), lambda i: (0, 0))),
        out_shape=(_sds((t, BW), F32), _sds((t, BW), F32), _sds((t, BW), F32), _sds((2 * NH, 128), F32)),
        scratch_shapes=[pltpu.VMEM((2 * nb, BW, BW), F32), pltpu.VMEM((2 * nb, BW, BW), F32)], name="ret_bwd",
        compiler_params=_cp())(lgf, lgb, do, q, k, v)


def _pool_windows(t):
    row = lax.broadcasted_iota(jnp.int32, (t, BW), 0)
    half = lax.shift_left(jnp.ones((t, BW), jnp.int32), _lane_head((t, BW)))
    cnt = (jnp.minimum(row + half, t) - jnp.maximum(row - half, 0)).astype(F32)
    return row, half, cnt


def _pool_window_sum(v, row, half, t, transpose):
    out = jnp.zeros_like(v)
    for j in range(-POOL_HALF_MAX, POOL_HALF_MAX):
        src = row - j if transpose else row + j
        ok = (src >= 0) & (src < t) & (j >= -half) & (j < half)
        out = out + jnp.where(ok, pltpu.roll(v, (j if transpose else -j) % t, 0), 0.0)
    return out


def _pool_fwd(proj, wbd, scale):
    t = proj.shape[0]

    def body(v_ref, w_ref, s_ref, o_ref):
        v = v_ref[...]
        row, half, cnt = _pool_windows(t)
        pooled = _pool_window_sum(v, row, half, t, False) / cnt - v
        o_ref[...] = (_dot(pooled, w_ref[...]) * s_ref[...]).astype(BF16)

    return pl.pallas_call(
        body, grid=(1,),
        in_specs=[pl.BlockSpec((t, BW), lambda i: (0, PV)), pl.BlockSpec((BW, BW), lambda i: (0, 0)), pl.BlockSpec((1, BW), lambda i: (0, 0))],
        out_specs=pl.BlockSpec((t, BW), lambda i: (0, 0)), out_shape=_sds((t, BW), BF16), name="pool_fwd",
        compiler_params=_cp())(proj, wbd, scale)


def _pool_bwd(dbr, proj, wbd, scale):
    t = proj.shape[0]

    def body(d_ref, v_ref, w_ref, s_ref, dv_ref, dw_ref, ds_ref):
        v, dout = v_ref[...], d_ref[...]
        row, half, cnt = _pool_windows(t)
        pooled = _pool_window_sum(v, row, half, t, False) / cnt - v
        mixed = _dot(pooled, w_ref[...])
        ds_ref[...] = jnp.sum(dout * mixed, axis=0, keepdims=True)
        dmixed = dout * s_ref[...]
        dw_ref[...] = _dot(pooled, dmixed, ta=True)
        dpooled = _dot(dmixed, w_ref[...], tb=True)
        dv_ref[...] = _pool_window_sum(dpooled / cnt, row, half, t, True) - dpooled

    return pl.pallas_call(
        body, grid=(1,),
        in_specs=[pl.BlockSpec((t, BW), lambda i: (0, 1)), pl.BlockSpec((t, BW), lambda i: (0, PV)),
                  pl.BlockSpec((BW, BW), lambda i: (0, 0)), pl.BlockSpec((1, BW), lambda i: (0, 0))],
        out_specs=(pl.BlockSpec((t, BW), lambda i: (0, 0)), pl.BlockSpec((BW, BW), lambda i: (0, 0)), pl.BlockSpec((1, BW), lambda i: (0, 0))),
        out_shape=(_sds((t, BW), F32), _sds((BW, BW), F32), _sds((1, BW), F32)), name="pool_bwd",
        compiler_params=_cp())(dbr, proj, wbd, scale)


NA_KEYS = NA_ROWS_WIN * GRID_W
NA_PAIRS = 2 * NA_ROWS_WIN - 2


def _na_window(r, n_rows):
    rs = jnp.clip(r - NA_ROWS_WIN // 2, 0, n_rows - NA_ROWS_WIN)
    return pl.multiple_of(rs * GRID_W, GRID_W), rs - r + (NA_ROWS_WIN - 1)


def _na_bias(b_ref, a0):
    return jnp.concatenate([b_ref[a0 + 2 * j] for j in range(NA_ROWS_WIN // 2)], axis=1)


NA_STEP_ROWS = 8


def _na_fwd(q, k, v, ball):
    t = q.shape[0]
    n_rows = t // GRID_W
    rows = NA_STEP_ROWS

    def body(q_ref, k_ref, v_ref, b_ref, o_ref):
        for rr in range(rows):
            start, a0 = _na_window(pl.program_id(0) * rows + rr, n_rows)
            own = slice(rr * GRID_W, (rr + 1) * GRID_W)
            qs = _stack_heads(q_ref[own, :])
            s = _dot(qs, k_ref[pl.ds(start, NA_KEYS), :], tb=True) * (HD ** -0.5) + _na_bias(b_ref, a0)
            p = _softmax_rows(s)
            o_ref[own, :] = _unstack_heads(_dot(p, v_ref[pl.ds(start, NA_KEYS), :]), GRID_W).astype(BF16)

    blk = pl.BlockSpec((rows * GRID_W, BW), lambda r: (r, 0))
    whole = pl.BlockSpec((t, BW), lambda r: (0, 0))
    return pl.pallas_call(
        body, grid=(n_rows // rows,), in_specs=[blk, whole, whole, pl.BlockSpec(ball.shape, lambda r: (0, 0, 0))],
        out_specs=blk, out_shape=_sds((t, BW), BF16), name="na_fwd", compiler_params=_cp())(q, k, v, ball)


def _na_bwd(dbr, q, k, v, ball):
    t = q.shape[0]
    n_rows = t // GRID_W

    rows = NA_STEP_ROWS

    def body(d_ref, q_ref, k_ref, v_ref, b_ref, dq_ref, dk_ref, dv_ref, db_ref):
        @pl.when(pl.program_id(0) == 0)
        def _():
            dk_ref[...] = jnp.zeros_like(dk_ref)
            dv_ref[...] = jnp.zeros_like(dv_ref)
            db_ref[...] = jnp.zeros_like(db_ref)

        for rr in range(rows):
            start, a0 = _na_window(pl.program_id(0) * rows + rr, n_rows)
            keys = pl.ds(start, NA_KEYS)
            own = slice(rr * GRID_W, (rr + 1) * GRID_W)
            qs = _stack_heads(q_ref[own, :])
            kb, vb = k_ref[keys, :], v_ref[keys, :]
            p = _softmax_rows(_dot(qs, kb, tb=True) * (HD ** -0.5) + _na_bias(b_ref, a0))
            dos = _stack_heads(d_ref[own, :]).astype(MXU)
            dp = _dot(dos, vb, tb=True)
            dv_ref[keys, :] += _dot(p, dos, ta=True)
            ds = p * (dp - jnp.sum(dp * p, axis=-1, keepdims=True))
            for j in range(NA_ROWS_WIN // 2):
                db_ref[a0 + 2 * j] += ds[:, 2 * j * GRID_W:(2 * j + 2) * GRID_W]
            dsb = (ds * (HD ** -0.5)).astype(MXU)
            dq_ref[own, :] = _unstack_heads(_dot(dsb, kb), GRID_W)
            dk_ref[keys, :] += _dot(dsb, qs, ta=True)

    blk = pl.BlockSpec((rows * GRID_W, BW), lambda r: (r, 0))
    whole = pl.BlockSpec((t, BW), lambda r: (0, 0))
    tab = pl.BlockSpec(ball.shape, lambda r: (0, 0, 0))
    return pl.pallas_call(
        body, grid=(n_rows // rows,), in_specs=[pl.BlockSpec((rows * GRID_W, BW), lambda r: (r, 2)), blk, whole, whole, tab],
        out_specs=(blk, whole, whole, tab),
        out_shape=(_sds((t, BW), F32), _sds((t, BW), F32), _sds((t, BW), F32), _sds(ball.shape, F32)), name="na_bwd",
        compiler_params=_cp())(dbr, q, k, v, ball)


def _rpb_expand(rpb_pad, onehot):
    def body(r_ref, e_ref, o_ref):
        o_ref[...] = jnp.dot(r_ref[...], e_ref[...], precision=HI, preferred_element_type=F32)

    return pl.pallas_call(body, out_shape=_sds((rpb_pad.shape[0], GRID_W * GRID_W), F32), name="rpb_expand",
                          compiler_params=_cp())(rpb_pad, onehot)


def _rpb_reduce(dtab, onehot):
    def body(d_ref, e_ref, o_ref):
        o_ref[...] = lax.dot_general(d_ref[...], e_ref[...], (((1,), (1,)), ((), ())), precision=HI, preferred_element_type=F32)

    return pl.pallas_call(body, out_shape=_sds((dtab.shape[0], 128), F32), name="rpb_reduce", compiler_params=_cp())(dtab, onehot)


MEM_TQ = 256


def _mem_fwd(q, mk, mv):
    t = q.shape[0]
    tq = MEM_TQ

    def body(q_ref, k_ref, v_ref, o_ref):
        p = _softmax_rows(_dot(_stack_heads(q_ref[...]), k_ref[...], tb=True) * (HD ** -0.5))
        o_ref[...] = _unstack_heads(_dot(p, v_ref[...]), tq).astype(BF16)

    blk = pl.BlockSpec((tq, BW), lambda i: (i, 0))
    kv = pl.BlockSpec((N_MEM, BW), lambda i: (0, 0))
    return pl.pallas_call(body, grid=(t // tq,), in_specs=[blk, kv, kv], out_specs=blk, out_shape=_sds((t, BW), BF16),
                          name="mem_fwd", compiler_params=_cp())(q, mk, mv)


def _mem_bwd(dbr, q, mk, mv):
    t = q.shape[0]
    tq = MEM_TQ

    def body(d_ref, q_ref, k_ref, v_ref, dq_ref, dk_ref, dv_ref):
        first = pl.program_id(0) == 0
        qs = _stack_heads(q_ref[...])
        dos = _stack_heads(d_ref[...]).astype(MXU)
        p = _softmax_rows(_dot(qs, k_ref[...], tb=True) * (HD ** -0.5))
        dp = _dot(dos, v_ref[...], tb=True)
        _acc(dv_ref, _dot(p, dos, ta=True), first)
        dsb = (p * (dp - jnp.sum(dp * p, axis=-1, keepdims=True)) * (HD ** -0.5)).astype(MXU)
        dq_ref[...] = _unstack_heads(_dot(dsb, k_ref[...]), tq)
        _acc(dk_ref, _dot(dsb, qs, ta=True), first)

    blk = pl.BlockSpec((tq, BW), lambda i: (i, 0))
    kv = pl.BlockSpec((N_MEM, BW), lambda i: (0, 0))
    return pl.pallas_call(
        body, grid=(t // tq,), in_specs=[pl.BlockSpec((tq, BW), lambda i: (i, 3)), blk, kv, kv], out_specs=(blk, kv, kv),
        out_shape=(_sds((t, BW), F32), _sds((N_MEM, BW), F32), _sds((N_MEM, BW), F32)), name="mem_bwd",
        compiler_params=_cp())(dbr, q, mk, mv)


def _memkv_prep(kv, g_mk):
    def body(kv_ref, g_ref, k_ref, v_ref):
        k_ref[...] = _gnorm(kv_ref[:, 0:BW], g_ref[...]).astype(BF16)
        v_ref[...] = kv_ref[:, BW:2 * BW].astype(BF16)

    return pl.pallas_call(body, out_shape=(_sds((N_MEM, BW), BF16), _sds((N_MEM, BW), BF16)), name="memkv_prep",
                          compiler_params=_cp())(kv, g_mk)


def _memkv_bwd(kv, dk, dv, g_mk):
    def body(kv_ref, dk_ref, dv_ref, g_ref, o_ref, dg_ref):
        dkk, gain = _gnorm_bwd(dk_ref[...], kv_ref[:, 0:BW], g_ref[...])
        o_ref[:, 0:BW] = dkk.astype(BF16)
        o_ref[:, BW:2 * BW] = dv_ref[...].astype(BF16)
        dg_ref[...] = jnp.sum(gain, axis=0, keepdims=True)

    return pl.pallas_call(body, out_shape=(_sds((N_MEM, 2 * BW), BF16), _sds((1, BW), F32)), name="memkv_bwd",
                          compiler_params=_cp())(kv, dk, dv, g_mk)


MERGE_TM = 256


def _merge_fwd(brs, wbt, gp):
    t = gp.shape[0]
    tm = MERGE_TM

    def body(b0, b1, b2, b3, wb_ref, gp_ref, o_ref):
        out = jnp.zeros((tm, D), F32)
        for n, b_ref in enumerate((b0, b1, b2, b3)):
            up = _dot(b_ref[...], wb_ref[n], tb=True)
            out = out + _sigmoid(gp_ref[:, n * D:(n + 1) * D].astype(F32)) * up
        o_ref[...] = out.astype(BF16)

    blk = pl.BlockSpec((tm, BW), lambda i: (i, 0))
    return pl.pallas_call(
        body, grid=(t // tm,),
        in_specs=[blk, blk, blk, blk, pl.BlockSpec((NH, D, BW), lambda i: (0, 0, 0)), pl.BlockSpec((tm, NH * D), lambda i: (i, 0))],
        out_specs=pl.BlockSpec((tm, D), lambda i: (i, 0)), out_shape=_sds((t, D), BF16), name="merge_fwd",
        compiler_params=_cp())(*brs, wbt, gp)


def _merge_bwd(dmerged, brs, wbt, gp):
    t = gp.shape[0]
    tm = MERGE_TM
    steps = t // tm

    def body(d_ref, b0, b1, b2, b3, wb_ref, gp_ref, dgp_ref, dbr_ref, dwb_ref, acc_ref):
        i = pl.program_id(0)
        dm = d_ref[...]
        for n, b_ref in enumerate((b0, b1, b2, b3)):
            br = b_ref[...]
            up = _dot(br, wb_ref[n], tb=True)
            g = _sigmoid(gp_ref[:, n * D:(n + 1) * D].astype(F32))
            dgp_ref[:, n * D:(n + 1) * D] = (dm * up * (g * (1.0 - g))).astype(BF16)
            dup = (dm * g).astype(BF16)
            dbr_ref[:, n * BW:(n + 1) * BW] = _dot(dup, wb_ref[n])
            part = _dot(dup, br, ta=True)

            @pl.when(i == 0)
            def _():
                acc_ref[n] = part

            @pl.when(i > 0)
            def _():
                acc_ref[n] += part

        @pl.when(i == steps - 1)
        def _():
            dwb_ref[...] = acc_ref[...].astype(BF16)

    row = pl.BlockSpec((tm, D), lambda i: (i, 0))
    blk = pl.BlockSpec((tm, BW), lambda i: (i, 0))
    wide = pl.BlockSpec((tm, NH * D), lambda i: (i, 0))
    whole = pl.BlockSpec((NH, D, BW), lambda i: (0, 0, 0))
    return pl.pallas_call(
        body, grid=(steps,), in_specs=[row, blk, blk, blk, blk, whole, wide], out_specs=(wide, row, whole),
        out_shape=(_sds((t, NH * D), BF16), _sds((t, NH * BW), F32), _sds((NH, D, BW), BF16)),
        scratch_shapes=[pltpu.VMEM((NH, D, BW), F32)], name="merge_bwd", compiler_params=_cp())(dmerged, *brs, wbt, gp)


FFN_TN = 256


def _ffn_in_fwd(h2, w_t):
    t = h2.shape[0]
    tm, tn = _tile(t, 1024), FFN_TN
    nj = FF // tn

    def body(x_ref, wa_ref, wg_ref, a_ref, g_ref, y_ref):
        x = x_ref[...]
        a, g = _dot(x, wa_ref[...], tb=True), _dot(x, wg_ref[...], tb=True)
        a_ref[...] = a.astype(BF16)
        g_ref[...] = g.astype(BF16)
        y_ref[...] = (a * _sigmoid(a) * g).astype(BF16)

    out = pl.BlockSpec((tm, tn), lambda i, j: (i, j))
    return pl.pallas_call(
        body, grid=(t // tm, nj),
        in_specs=[pl.BlockSpec((tm, D), lambda i, j: (i, 0)), pl.BlockSpec((tn, D), lambda i, j: (j, 0)),
                  pl.BlockSpec((tn, D), lambda i, j: (j + nj, 0))],
        out_specs=(out, out, out), out_shape=tuple(_sds((t, FF), BF16) for _ in range(3)), name="ffn_in_fwd",
        compiler_params=_cp(dimension_semantics=("parallel", "parallel")))(h2, w_t, w_t)


def _ffn_out_bwd(dx2b, w_out, a, g, dep):
    t = dx2b.shape[0]
    tm, tn = _tile(t, 1024), FFN_TN

    def body(*refs):
        x_ref, w_ref, a_ref, g_ref = refs[:4]
        da_ref, dg_ref = refs[-2:]
        d = _dot(x_ref[...], w_ref[...], tb=True)
        av, gv = a_ref[...].astype(F32), g_ref[...].astype(F32)
        s = _sigmoid(av)
        da_ref[...] = (d * gv * (s * (1.0 + av * (1.0 - s)))).astype(BF16)
        dg_ref[...] = (d * (av * s)).astype(BF16)

    blk = pl.BlockSpec((tm, tn), lambda i, j: (i, j))
    ins = [pl.BlockSpec((tm, D), lambda i, j: (i, 0)), pl.BlockSpec((tn, D), lambda i, j: (j, 0)), blk, blk]
    args = [dx2b, w_out, a, g]
    if dep is not None:
        ins.append(pl.BlockSpec((8, 128), lambda i, j: (0, 0)))
        args.append(dep)
    return pl.pallas_call(
        body, grid=(t // tm, FF // tn), in_specs=ins, out_specs=(blk, blk),
        out_shape=(_sds((t, FF), BF16), _sds((t, FF), BF16)), name="ffn_out_bwd",
        compiler_params=_cp(dimension_semantics=("parallel", "parallel")))(*args)


def _loss_head(y, target):
    t, d = y.shape
    tm = 256

    def body(y_ref, t_ref, dy_ref, dyb_ref, l_ref):
        e = y_ref[...] - t_ref[...]
        dy_ref[...] = e * (1.0 / d)
        dyb_ref[...] = (e * (1.0 / d)).astype(BF16)
        _acc(l_ref, jnp.full((8, 128), 0.5 * jnp.sum(jnp.sum(e * e, axis=-1, keepdims=True) * (1.0 / d)), F32), pl.program_id(0) == 0)

    row = pl.BlockSpec((tm, d), lambda i: (i, 0))
    return pl.pallas_call(body, grid=(t // tm,), in_specs=[row, row], out_specs=(row, row, pl.BlockSpec((8, 128), lambda i: (0, 0))),
                          out_shape=(_sds((t, d), F32), _sds((t, d), BF16), _sds((8, 128), F32)), name="loss_head",
                          compiler_params=_cp())(y, target)


def _sum_slots(x, name):
    k, r, c = x.shape
    tr = _tile(r, 512) if r % 128 == 0 else r

    def body(x_ref, o_ref):
        acc = x_ref[0].astype(F32)
        for s in range(1, k):
            acc = acc + x_ref[s].astype(F32)
        o_ref[...] = acc

    return pl.pallas_call(body, grid=(r // tr,), in_specs=[pl.BlockSpec((k, tr, c), lambda i: (0, i, 0))],
                          out_specs=pl.BlockSpec((tr, c), lambda i: (i, 0)), out_shape=_sds((r, c), F32), name=name,
                          compiler_params=_cp())(x)


def _pair_sum(bufs, recvs, cidx):
    n = len(bufs)

    def body(c_ref, *refs):
        for i in range(n):
            refs[2 * n + i][...] = (refs[i][...].astype(F32) + refs[n + i][...].astype(F32)).astype(BF16)

    return pl.pallas_call(
        body,
        grid_spec=pltpu.PrefetchScalarGridSpec(
            num_scalar_prefetch=1, grid=(4,),
            in_specs=[pl.BlockSpec((None, None) + b.shape[2:], lambda s, cref: (s, cref[0], 0, 0)) for b in bufs]
            + [pl.BlockSpec((None,) + r.shape[1:], lambda s, cref: (s, 0, 0)) for r in recvs],
            out_specs=tuple(pl.BlockSpec((None,) + r.shape[1:], lambda s, cref: (s, 0, 0)) for r in recvs)),
        out_shape=tuple(_sds(r.shape, BF16) for r in recvs), name="rs_pair_sum", compiler_params=_cp())(cidx, *bufs, *recvs)


def _adamw_update(w, gv, m, v):
    mn = ADAM_B1 * m + (1.0 - ADAM_B1) * gv
    vn = ADAM_B2 * v + (1.0 - ADAM_B2) * (gv * gv)
    m_hat = mn / (1.0 - ADAM_B1 ** ADAM_STEP)
    v_hat = vn / (1.0 - ADAM_B2 ** ADAM_STEP)
    return -ADAM_LR * (m_hat / (jnp.sqrt(v_hat) + ADAM_EPS) + ADAM_WD * w), mn, vn


def _adamw(w, g, m, v, name):
    r, c = w.shape

    def body(w_ref, g_ref, m_ref, v_ref, d_ref, nm_ref, nv_ref):
        d_ref[...], nm_ref[...], nv_ref[...] = _adamw_update(w_ref[...], g_ref[...], m_ref[...], v_ref[...])

    blk = pl.BlockSpec((r, c), lambda i: (0, 0))
    return pl.pallas_call(body, grid=(1,), in_specs=[blk] * 4, out_specs=(blk,) * 3,
                          out_shape=tuple(_sds((r, c), F32) for _ in range(3)), name=name, compiler_params=_cp())(w, g, m, v)


def _adamw_layer(layer, w, g, m, v, outs, name):
    _, r, c = w.shape
    tr = max(d for d in range(8, r + 1, 8) if r % d == 0 and d * c * 4 <= 2 ** 20)

    def body(w_ref, m_ref, v_ref, g_ref, *refs):
        d_ref, nm_ref, nv_ref, go_ref = refs[4:]
        gv = g_ref[...]
        d_ref[...], nm_ref[...], nv_ref[...] = _adamw_update(w_ref[...], gv, m_ref[...], v_ref[...])
        go_ref[...] = gv

    blk = pl.BlockSpec((None, tr, c), lambda i: (layer, i, 0))
    return pl.pallas_call(
        body, grid=(r // tr,), in_specs=[blk] * 3 + [pl.BlockSpec((tr, c), lambda i: (i, 0))] + [ANY] * 4, out_specs=(blk,) * 4,
        out_shape=tuple(_sds(w.shape, F32) for _ in range(4)), input_output_aliases={4 + j: j for j in range(4)}, name=name,
        compiler_params=_cp())(w, m, v, g, *outs)


def _all_gather(shards, name):
    n = len(shards)

    def body(*refs):
        x_refs, out_refs = refs[:n], refs[n:2 * n]
        send_sems, recv_sems, local_sems = refs[2 * n:]
        x, y, cc = lax.axis_index("x"), lax.axis_index("y"), lax.axis_index("c")
        me, sibling = (x, y, cc), (x, y, 1 - cc)
        chips = [(1 - x, y), (x, 1 - y), (1 - x, 1 - y)]

        def copy(i, k, block, to, own=False):
            px, py, pc = block
            slot = out_refs[i].at[4 * px + 2 * py + pc]
            return pltpu.make_async_remote_copy(
                src_ref=x_refs[i] if own else slot, dst_ref=slot, send_sem=send_sems.at[7 * i + k],
                recv_sem=recv_sems.at[7 * i + k], device_id=to, device_id_type=MESH)

        mine = [pltpu.make_async_copy(x_refs[i], out_refs[i].at[4 * x + 2 * y + cc], local_sems.at[i]) for i in range(n)]
        for cp in mine:
            cp.start()
        first = []
        for j, chip in enumerate(chips):
            first += [copy(i, 1 + j, me, (*chip, cc), own=True) for i in range(n)]
        first += [copy(i, 0, me, sibling, own=True) for i in range(n)]
        for cp in first:
            cp.start()
        passed = []
        for j, chip in enumerate(chips):
            for i in range(n):
                copy(i, 1 + j, (*chip, cc), me).wait_recv()
                cp = copy(i, 4 + j, (*chip, cc), sibling)
                cp.start()
                passed.append(cp)
        for i in range(n):
            copy(i, 0, sibling, me).wait_recv()
        for j, chip in enumerate(chips):
            for i in range(n):
                copy(i, 4 + j, (*chip, 1 - cc), me).wait_recv()
        for cp in first + passed:
            cp.wait_send()
        for cp in mine:
            cp.wait()

    return pl.pallas_call(
        body, out_shape=tuple(_sds((N_DEV,) + s.shape, s.dtype) for s in shards), in_specs=[ANY] * n, out_specs=(ANY,) * n,
        scratch_shapes=[pltpu.SemaphoreType.DMA((7 * n,)), pltpu.SemaphoreType.DMA((7 * n,)), pltpu.SemaphoreType.DMA((n,))],
        name=name)(*shards)


def _rs_core_swap(bufs, name):
    n = len(bufs)

    def body(*refs):
        b_refs, recv_refs = refs[:n], refs[n:2 * n]
        send_sems, recv_sems = refs[2 * n:]
        x, y, cc = lax.axis_index("x"), lax.axis_index("y"), lax.axis_index("c")
        copies = [pltpu.make_async_remote_copy(
            src_ref=b_refs[i].at[s, 1 - cc], dst_ref=recv_refs[i].at[s], send_sem=send_sems.at[4 * i + s],
            recv_sem=recv_sems.at[4 * i + s], device_id=(x, y, 1 - cc), device_id_type=MESH) for i in range(n) for s in range(4)]
        for cp in copies:
            cp.start()
        for cp in copies:
            cp.wait()

    return pl.pallas_call(
        body, out_shape=tuple(_sds((4,) + b.shape[2:], b.dtype) for b in bufs), in_specs=[ANY] * n, out_specs=(ANY,) * n,
        scratch_shapes=[pltpu.SemaphoreType.DMA((4 * n,)), pltpu.SemaphoreType.DMA((4 * n,))], name=name)(*bufs)


HBM = pl.BlockSpec(memory_space=pltpu.HBM)
SEMS = pl.BlockSpec(memory_space=pltpu.SEMAPHORE)
EFFECT = pltpu.SideEffectType.DATAFLOW_SIDE_EFFECTING


def _hbm(a):
    return pltpu.HBM(a.shape, a.dtype)


def _other_chips(x, y):
    return [(1 - x, y), (x, 1 - y), (1 - x, 1 - y)]


def _ici_start(srcs, lands, mode, name, group=None):
    n = len(srcs)

    def body(*refs):
        s_refs, land_refs = refs[:n], refs[n:2 * n]
        send_sems, recv_sems = refs[2 * n], refs[2 * n + 1]
        token = refs[-1]
        x, y, cc = lax.axis_index("x"), lax.axis_index("y"), lax.axis_index("c")
        mine = 2 * x + y if mode == "by_chip" else 4 * x + 2 * y + cc
        peers = [(px, py, cc) for px, py in _other_chips(x, y)]
        if mode == "by_device":
            peers = [(x, y, 1 - cc)] + peers + [(px, py, 1 - cc) for px, py in _other_chips(x, y)]
        size = n if group is None else group
        for first in range(0, n, size):
            for px, py, pc in peers:
                for i in range(first, first + size):
                    src = s_refs[i]
                    if mode == "by_chip":
                        src = src.at[2 * px + py]
                    elif mode == "by_device":
                        src = src.at[4 * px + 2 * py + pc]
                    pltpu.make_async_remote_copy(
                        src_ref=src, dst_ref=land_refs[i].at[mine], send_sem=send_sems.at[i], recv_sem=recv_sems.at[i],
                        device_id=(px, py, pc), device_id_type=MESH).start()
        token[...] = jnp.zeros_like(token)

    out = pl.pallas_call(
        body, name=name,
        out_shape=(pltpu.SemaphoreType.DMA((n,)), pltpu.SemaphoreType.DMA((n,)), *[_hbm(s) for s in srcs], *[_hbm(l) for l in lands],
                   _sds((8, 128), F32)),
        in_specs=[HBM] * (2 * n), out_specs=(SEMS, SEMS, *[HBM] * (2 * n), pl.BlockSpec(memory_space=pltpu.VMEM)),
        input_output_aliases={i: 2 + i for i in range(2 * n)}, compiler_params=pltpu.CompilerParams(has_side_effects=EFFECT),
    )(*[pltpu.with_memory_space_constraint(s, pltpu.HBM) for s in srcs],
      *[pltpu.with_memory_space_constraint(l, pltpu.HBM) for l in lands])
    return out[0], out[1], out[2:2 + n], out[2 + n:2 + 2 * n], out[-1], 7 if mode == "by_device" else 3


def _ici_wait(started, after, name, only=None):
    send_sems, recv_sems, srcs, lands, _, copies = started
    only = list(range(len(srcs))) if only is None else only
    srcs, lands = [srcs[i] for i in only], [lands[i] for i in only]
    n = len(srcs)

    def body(*refs):
        land_refs = refs[n:2 * n]
        send_sems, recv_sems = refs[2 * n], refs[2 * n + 1]
        x, y, cc = lax.axis_index("x"), lax.axis_index("y"), lax.axis_index("c")
        for i in range(n):
            three = land_refs[i].at[pl.ds(0, copies)]
            cp = pltpu.make_async_remote_copy(src_ref=three, dst_ref=three, send_sem=send_sems.at[only[i]],
                                              recv_sem=recv_sems.at[only[i]],
                                              device_id=(x, y, cc), device_id_type=MESH)
            cp.wait_send()
            cp.wait_recv()

    return pl.pallas_call(
        body, name=name, out_shape=tuple(_hbm(l) for l in lands), in_specs=[HBM] * (2 * n) + [SEMS, SEMS, ANY],
        out_specs=tuple([HBM] * n), input_output_aliases={n + i: i for i in range(n)},
        compiler_params=pltpu.CompilerParams(has_side_effects=EFFECT))(*srcs, *lands, send_sems, recv_sems, after)


def _gather_d2d(blocks, lands, name):
    n = len(blocks)

    def body(*refs):
        x_refs, land_refs = refs[:n], refs[2 * n:3 * n]
        send_sems, recv_sems, in_sems, out_sems = refs[3 * n:3 * n + 4]
        stage = refs[3 * n + 4:]
        x, y, cc = lax.axis_index("x"), lax.axis_index("y"), lax.axis_index("c")
        sibling = (x, y, 1 - cc)
        staged = [pltpu.make_async_copy(x_refs[i], stage[i], in_sems.at[i]) for i in range(n)]
        for cp in staged:
            cp.start()
        copies = []
        for i in range(n):
            slot = land_refs[i].at[4 * x + 2 * y + cc]
            copies.append(pltpu.make_async_remote_copy(src_ref=x_refs[i], dst_ref=slot, send_sem=send_sems.at[4 * i],
                                                       recv_sem=recv_sems.at[4 * i], device_id=sibling, device_id_type=MESH))
            for j, (px, py) in enumerate(_other_chips(x, y)):
                slot = land_refs[i].at[4 * px + 2 * py + cc]
                copies.append(pltpu.make_async_remote_copy(src_ref=slot, dst_ref=slot, send_sem=send_sems.at[4 * i + 1 + j],
                                                           recv_sem=recv_sems.at[4 * i + 1 + j], device_id=sibling, device_id_type=MESH))
        for cp in copies:
            cp.start()
        mine = []
        for i in range(n):
            staged[i].wait()
            mine.append(pltpu.make_async_copy(stage[i], land_refs[i].at[4 * x + 2 * y + cc], out_sems.at[i]))
            mine[i].start()
        for i in range(n):
            slot = land_refs[i].at[4 * x + 2 * y + (1 - cc)]
            pltpu.make_async_remote_copy(src_ref=slot, dst_ref=slot, send_sem=send_sems.at[4 * i], recv_sem=recv_sems.at[4 * i],
                                         device_id=sibling, device_id_type=MESH).wait_recv()
            for j, (px, py) in enumerate(_other_chips(x, y)):
                slot = land_refs[i].at[4 * px + 2 * py + (1 - cc)]
                pltpu.make_async_remote_copy(src_ref=slot, dst_ref=slot, send_sem=send_sems.at[4 * i + 1 + j],
                                             recv_sem=recv_sems.at[4 * i + 1 + j], device_id=sibling, device_id_type=MESH).wait_recv()
        for cp in copies:
            cp.wait_send()
        for cp in mine:
            cp.wait()

    return pl.pallas_call(
        body, out_shape=tuple(_sds(l.shape, l.dtype) for l in lands), in_specs=[ANY] * (2 * n), out_specs=(ANY,) * n,
        input_output_aliases={n + i: i for i in range(n)},
        scratch_shapes=[pltpu.SemaphoreType.DMA((4 * n,)), pltpu.SemaphoreType.DMA((4 * n,)), pltpu.SemaphoreType.DMA((n,)),
                        pltpu.SemaphoreType.DMA((n,))] + [pltpu.VMEM(b.shape, b.dtype) for b in blocks],
        name=name, compiler_params=_cp())(*blocks, *lands)


def _sum_own(parts, recvs, mine, name):
    n = len(parts)

    def body(c_ref, *refs):
        s = pl.program_id(0)
        for i in range(n):
            val = jnp.where(c_ref[0] == s, refs[i][...], refs[n + i][...]).astype(F32)
            _acc(refs[2 * n + i], val, s == 0)

    kept = [pl.BlockSpec((None,) + p.shape[1:], lambda s, cref: (cref[0], 0, 0)) for p in parts]
    ins = [pl.BlockSpec((None,) + p.shape[1:], lambda s, cref: (s, 0, 0)) for p in parts]
    return pl.pallas_call(
        body, grid_spec=pltpu.PrefetchScalarGridSpec(
            num_scalar_prefetch=1, grid=(parts[0].shape[0],), in_specs=kept + ins,
            out_specs=tuple(pl.BlockSpec(p.shape[1:], lambda s, cref: (0, 0)) for p in parts)),
        out_shape=tuple(_sds(p.shape[1:], F32) for p in parts), name=name, compiler_params=_cp())(mine, *parts, *recvs)


BIG = (("w_in", True), ("w_gate", True), ("w_mem_kv", False), ("w_branch", True), ("w_out", False), ("w_ffn_in", True),
       ("w_ffn_out", False))

SMALL = ("norm_mix_g", "norm_mem_g", "ret_decay_fwd", "ret_decay_bwd", "ret_norm_g", "pool_w", "pool_scale", "na_q_norm_g",
         "na_k_norm_g", "na_rpb", "mem_q_norm_g", "mem_k_norm_g", "norm_ffn_g")
WEIGHTS = ("norm_mix_g", "norm_mem_g", "w_in", "w_gate", "ret_decay_fwd", "ret_decay_bwd", "ret_norm_g", "pool_w", "pool_scale",
           "na_q_norm_g", "na_k_norm_g", "na_rpb", "mem_q_norm_g", "mem_k_norm_g", "w_mem_kv", "w_branch", "w_out", "norm_ffn_g",
           "w_ffn_in", "w_ffn_out")


def _to_exchange(name, transposed, shard):
    if name == "w_branch":
        return jnp.swapaxes(shard, 1, 2).reshape(NH * (D // N_DEV), BW)
    return shard.T if transposed else shard


def _from_exchange(name, transposed, block):
    if name == "w_branch":
        return jnp.swapaxes(block.reshape(NH, D // N_DEV, BW), 1, 2)
    return block.T if transposed else block


def _whole_from_gathered(name, g):
    if name == "w_branch":
        return jnp.swapaxes(g.reshape(N_DEV, NH, D // N_DEV, BW), 0, 1).reshape(NH, D, BW)
    return g.reshape(N_DEV * g.shape[1], g.shape[2])


def _by_destination(name, g):
    if name == "w_branch":
        g = jnp.swapaxes(g.reshape(NH, N_DEV, D // N_DEV, BW), 0, 1).reshape(N_DEV * NH * (D // N_DEV), BW)
    return g.reshape(4, 2, g.shape[0] // N_DEV, g.shape[1])


SMALL_PAD = 1024


def _pack_small(vals, loss=None):
    parts = [vals[n] for n in SMALL] + [jnp.zeros((1,), F32) if loss is None else loss.reshape(1)]
    rows = []
    for p in parts:
        flat = p.reshape(-1)
        rows.append(jnp.pad(flat, (0, -flat.shape[0] % SMALL_PAD)).reshape(-1, 128))
    return jnp.concatenate(rows, axis=0)


def _unpack_small(packed, like):
    out, off = {}, 0
    for n in SMALL:
        sz = int(np.prod(like[n].shape))
        nrow = -(-sz // SMALL_PAD) * (SMALL_PAD // 128)
        out[n] = packed[off:off + nrow].reshape(-1)[:sz].reshape(like[n].shape)
        off += nrow
    return out, packed[off, 0]


def _na_constants():
    c = np.arange(GRID_W)
    win = np.clip(c - NA_COLS_WIN // 2, 0, GRID_W - NA_COLS_WIN)
    kc = np.arange(GRID_W)
    inside = (kc[None, :] >= win[:, None]) & (kc[None, :] < win[:, None] + NA_COLS_WIN)
    off = kc[None, :] - c[:, None] + NA_COLS_WIN - 1
    onehot = np.zeros((128, GRID_W, GRID_W), np.float32)
    for b in range(2 * NA_COLS_WIN - 1):
        onehot[b] = (off == b) & inside
    maskadd = np.where(inside, 0.0, NEG).astype(np.float32)
    return onehot.reshape(128, GRID_W * GRID_W), maskadd


def _na_bias_table(tab, maskadd):
    n_off = 2 * NA_ROWS_WIN - 1
    t4 = tab[:NH * n_off].reshape(NH, n_off, GRID_W, GRID_W) + maskadd[None, None]
    by_off = t4.transpose(1, 0, 2, 3).reshape(n_off, NH * GRID_W, GRID_W)
    return jnp.concatenate([by_off[:-1], by_off[1:]], axis=-1)


def _rotary_tables(t):
    half = HD // 2
    inv = ROPE_THETA ** (-jnp.arange(half, dtype=F32) / half)
    ang = jnp.arange(t, dtype=F32)[:, None] * inv[None, :]
    cos, sin = jnp.cos(ang), jnp.sin(ang)
    return jnp.tile(jnp.concatenate([cos, cos], axis=-1), (1, NH)), jnp.tile(jnp.concatenate([-sin, sin], axis=-1), (1, NH))


def _block_diag(pw):
    out = jnp.zeros((BW, BW), pw.dtype)
    for g in range(NH):
        out = lax.dynamic_update_slice(out, pw[g], (g * HD, g * HD))
    return out


def _tile4(g):
    return jnp.tile(g.reshape(1, HD), (1, NH))


def _layer_fwd(x, mem, sw, lw, consts, fetch, h=None, next_norm_g=None):
    cos2, sin2, onehot, maskadd = consts
    if h is None:
        h = _rmsnorm_fwd(x, sw["norm_mix_g"].reshape(1, D), "norm_mix_fwd")
    proj = _mm(h, lw["w_in"], tb=True, name="mm_in")
    gp = _mm(h, lw["w_gate"], tb=True, out_dtype=BF16, name="mm_gate")
    g_naq, g_nak, g_mq = _tile4(sw["na_q_norm_g"]), _tile4(sw["na_k_norm_g"]), _tile4(sw["mem_q_norm_g"])
    rq, rk, rv, nq, nk, nv, mq = _prep_fwd(proj, cos2, sin2, g_naq, g_nak, g_mq)

    lgf, lgb = jax.nn.log_sigmoid(sw["ret_decay_fwd"]), jax.nn.log_sigmoid(sw["ret_decay_bwd"])
    g_ret = sw["ret_norm_g"].reshape(1, BW)
    o_ret, ret = _ret_fwd(rq, rk, rv, proj, lgf, lgb, g_ret)

    wbd = _block_diag(sw["pool_w"]).astype(BF16)
    p_scale = sw["pool_scale"].reshape(1, BW)
    pool = _pool_fwd(proj, wbd, p_scale)

    rpb_pad = jnp.pad(sw["na_rpb"].reshape(NH * 15, 31), ((0, 4), (0, 97)))
    ball = _na_bias_table(_rpb_expand(rpb_pad, onehot), maskadd)
    na = _na_fwd(nq, nk, nv, ball)

    lw.update(fetch(1, na))
    memn = _rmsnorm_fwd(mem, sw["norm_mem_g"].reshape(1, D), "norm_mem_fwd")
    kv = _mm(memn, lw["w_mem_kv"], name="mm_memkv")
    g_mk = _tile4(sw["mem_k_norm_g"])
    mk, mv = _memkv_prep(kv, g_mk)
    mo = _mem_fwd(mq, mk, mv)

    br = (ret, pool, na, mo)
    merged = _merge_fwd(br, lw["w_branch"], gp)
    x1, h2 = _mm(merged, lw["w_out"], add=x, norm_g=sw["norm_ffn_g"].reshape(1, D), name="mm_out")
    lw.update(fetch(2, x1))
    ffa, ffg, yff = _ffn_in_fwd(h2, lw["w_ffn_in"])
    if next_norm_g is None:
        x2, h_next = _mm(yff, lw["w_ffn_out"], add=x1, name="mm_ffn_out"), None
    else:
        x2, h_next = _mm(yff, lw["w_ffn_out"], add=x1, norm_g=next_norm_g.reshape(1, D), name="mm_ffn_out")
    saved = dict(x=x, h=h, proj=proj, gp=gp, rq=rq, rk=rk, rv=rv, nq=nq, nk=nk, nv=nv, mq=mq, o_ret=o_ret, ball=ball, memn=memn,
                 kv=kv, mk=mk, mv=mv, br=br, merged=merged, x1=x1, h2=h2, ffa=ffa, ffg=ffg, yff=yff, lgf=lgf, lgb=lgb, wbd=wbd)
    return x2, h_next, saved


def _layer_bwd(dx2, dx2b, mem, sw, lw, sv, consts, dep=None):
    cos2, sin2, onehot, maskadd = consts
    gb, gs = {}, {}
    d_a, d_g = _ffn_out_bwd(dx2b, lw["w_ffn_out"], sv["ffa"], sv["ffg"], dep)
    gb["w_ffn_out"] = _mm(sv["yff"], dx2b, ta=True, out_dtype=BF16, name="mm_ffn_out_dw")
    dh2 = _mm(d_a, lw["w_ffn_in"], b_half=0, name="mm_ffn_in_dx_a")
    dx1, dx1b, dg = _mm_norm_bwd(d_g, lw["w_ffn_in"], dh2, sv["x1"], sw["norm_ffn_g"].reshape(1, D), dx2, b_half=1,
                                 name="mm_ffn_in_dx_g")
    gs["norm_ffn_g"] = dg.reshape(D)
    dw_a = _mm(d_a, sv["h2"], ta=True, out_dtype=BF16, out_half=(0, None), name="mm_ffn_in_dw_a")
    gb["w_ffn_in"] = _mm(d_g, sv["h2"], ta=True, out_dtype=BF16, out_half=(1, dw_a), name="mm_ffn_in_dw_g")

    dmerged = _mm(dx1b, lw["w_out"], tb=True, name="mm_out_dx")
    gb["w_out"] = _mm(sv["merged"], dx1b, ta=True, out_dtype=BF16, name="mm_out_dw")
    dgp, dbr, gb["w_branch"] = _merge_bwd(dmerged, sv["br"], lw["w_branch"], sv["gp"])

    g_ret = sw["ret_norm_g"].reshape(1, BW)
    do_ret, d_rg, dg_ret = _ret_post_bwd(dbr, sv["o_ret"], sv["proj"], g_ret)
    d_rq, d_rk, d_rv, dlg = _ret_bwd(do_ret, sv["rq"], sv["rk"], sv["rv"], sv["lgf"], sv["lgb"])
    gs["ret_norm_g"] = dg_ret.reshape(BW)
    _, vjp_f = jax.vjp(jax.nn.log_sigmoid, sw["ret_decay_fwd"])
    _, vjp_b = jax.vjp(jax.nn.log_sigmoid, sw["ret_decay_bwd"])
    gs["ret_decay_fwd"] = vjp_f(dlg[0:NH, 0])[0]
    gs["ret_decay_bwd"] = vjp_b(dlg[NH:2 * NH, 0])[0]

    p_scale = sw["pool_scale"].reshape(1, BW)
    d_pv, dwbd, dscale = _pool_bwd(dbr, sv["proj"], sv["wbd"], p_scale)
    gs["pool_w"] = jnp.stack([dwbd[g * HD:(g + 1) * HD, g * HD:(g + 1) * HD] for g in range(NH)])
    gs["pool_scale"] = dscale.reshape(BW)

    d_nq, d_nk, d_nv, dball = _na_bwd(dbr, sv["nq"], sv["nk"], sv["nv"], sv["ball"])
    _, vjp_tab = jax.vjp(lambda tab: _na_bias_table(tab, maskadd), jnp.zeros((64, GRID_W * GRID_W), F32))
    drpb = _rpb_reduce(vjp_tab(dball)[0], onehot)
    gs["na_rpb"] = drpb[:NH * 15, :31].reshape(NH, 15, 31)

    d_mq, d_mk, d_mv = _mem_bwd(dbr, sv["mq"], sv["mk"], sv["mv"])
    g_mk = _tile4(sw["mem_k_norm_g"])
    dkv, dg_mk = _memkv_bwd(sv["kv"], d_mk, d_mv, g_mk)
    gs["mem_k_norm_g"] = dg_mk.reshape(NH, HD).sum(0)
    gb["w_mem_kv"] = _mm(sv["memn"], dkv, ta=True, out_dtype=BF16, name="mm_memkv_dw")
    dmemn = _mm(dkv, lw["w_mem_kv"], tb=True, name="mm_memkv_dx")
    _, _, dg_mem = _rmsnorm_bwd(dmemn, mem, sw["norm_mem_g"].reshape(1, D), jnp.zeros_like(mem), "norm_mem_bwd")
    gs["norm_mem_g"] = dg_mem.reshape(D)

    g_naq, g_nak, g_mq = _tile4(sw["na_q_norm_g"]), _tile4(sw["na_k_norm_g"]), _tile4(sw["mem_q_norm_g"])
    dproj, dg_naq, dg_nak, dg_mq = _prep_bwd(sv["proj"], cos2, sin2, g_naq, g_nak, g_mq, d_rq, d_rk, d_rv, d_rg, d_pv, d_nq, d_nk,
                                             d_nv, d_mq)
    gs["na_q_norm_g"] = dg_naq.reshape(NH, HD).sum(0)
    gs["na_k_norm_g"] = dg_nak.reshape(NH, HD).sum(0)
    gs["mem_q_norm_g"] = dg_mq.reshape(NH, HD).sum(0)

    gb["w_in"] = _mm(dproj, sv["h"], ta=True, out_dtype=BF16, name="mm_in_dw")
    gb["w_gate"] = _mm(dgp, sv["h"], ta=True, out_dtype=BF16, name="mm_gate_dw")
    dh = _mm(dproj, lw["w_in"], name="mm_in_dx")
    dx, dxb, dg = _mm_norm_bwd(dgp, lw["w_gate"], dh, sv["x"], sw["norm_mix_g"].reshape(1, D), dx1, name="mm_gate_dx")
    gs["norm_mix_g"] = dg.reshape(D)
    return dx, dxb, gb, gs


def _local_step(x, mem, target, small, get_layer, on_grads):
    t = x.shape[0]
    cos2, sin2 = _rotary_tables(t)
    onehot, maskadd = _na_constants()
    consts = (cos2, sin2, jnp.asarray(onehot), jnp.asarray(maskadd))
    saved, weights, cur, h = [], [], x, None
    for l in range(DEPTH):
        sw = {n: small[n][l] for n in SMALL}
        lw, fetch = get_layer(l, cur)
        weights.append(lw)
        cur, h, sv = _layer_fwd(cur, mem, sw, lw, consts, fetch, h, small["norm_mix_g"][l + 1] if l + 1 < DEPTH else None)
        saved.append(sv)
    dy, dyb, loss_tile = _loss_head(cur, target)
    small_g = {n: [None] * DEPTH for n in SMALL}
    dep = None
    for l in reversed(range(DEPTH)):
        sw = {n: small[n][l] for n in SMALL}
        dy, dyb, gb, gs = _layer_bwd(dy, dyb, mem, sw, weights[l], saved[l], consts, dep)
        dep = on_grads(l, gb, dy)
        for n in SMALL:
            small_g[n][l] = gs[n]
    return loss_tile[0, 0], dy, {n: jnp.stack(v) for n, v in small_g.items()}


def _flat2d(a):
    return a.reshape(-1, a.shape[-1])


def kernel(x, mem, norm_mix_g, norm_mem_g, w_in, w_gate, ret_decay_fwd, ret_decay_bwd, ret_norm_g, pool_w, pool_scale, na_q_norm_g, na_k_norm_g, na_rpb, mem_q_norm_g, mem_k_norm_g, w_mem_kv, w_branch, w_out, norm_ffn_g, w_ffn_in, w_ffn_out, loss_target, m_norm_mix_g, m_norm_mem_g, m_w_in, m_w_gate, m_ret_decay_fwd, m_ret_decay_bwd, m_ret_norm_g, m_pool_w, m_pool_scale, m_na_q_norm_g, m_na_k_norm_g, m_na_rpb, m_mem_q_norm_g, m_mem_k_norm_g, m_w_mem_kv, m_w_branch, m_w_out, m_norm_ffn_g, m_w_ffn_in, m_w_ffn_out, v_norm_mix_g, v_norm_mem_g, v_w_in, v_w_gate, v_ret_decay_fwd, v_ret_decay_bwd, v_ret_norm_g, v_pool_w, v_pool_scale, v_na_q_norm_g, v_na_k_norm_g, v_na_rpb, v_mem_q_norm_g, v_mem_k_norm_g, v_w_mem_kv, v_w_branch, v_w_out, v_norm_ffn_g, v_w_ffn_in, v_w_ffn_out):
    w = dict(norm_mix_g=norm_mix_g, norm_mem_g=norm_mem_g, w_in=w_in, w_gate=w_gate, ret_decay_fwd=ret_decay_fwd,
             ret_decay_bwd=ret_decay_bwd, ret_norm_g=ret_norm_g, pool_w=pool_w, pool_scale=pool_scale, na_q_norm_g=na_q_norm_g,
             na_k_norm_g=na_k_norm_g, na_rpb=na_rpb, mem_q_norm_g=mem_q_norm_g, mem_k_norm_g=mem_k_norm_g, w_mem_kv=w_mem_kv,
             w_branch=w_branch, w_out=w_out, norm_ffn_g=norm_ffn_g, w_ffn_in=w_ffn_in, w_ffn_out=w_ffn_out)
    m = dict(norm_mix_g=m_norm_mix_g, norm_mem_g=m_norm_mem_g, w_in=m_w_in, w_gate=m_w_gate, ret_decay_fwd=m_ret_decay_fwd,
             ret_decay_bwd=m_ret_decay_bwd, ret_norm_g=m_ret_norm_g, pool_w=m_pool_w, pool_scale=m_pool_scale, na_q_norm_g=m_na_q_norm_g,
             na_k_norm_g=m_na_k_norm_g, na_rpb=m_na_rpb, mem_q_norm_g=m_mem_q_norm_g, mem_k_norm_g=m_mem_k_norm_g, w_mem_kv=m_w_mem_kv,
             w_branch=m_w_branch, w_out=m_w_out, norm_ffn_g=m_norm_ffn_g, w_ffn_in=m_w_ffn_in, w_ffn_out=m_w_ffn_out)
    v = dict(norm_mix_g=v_norm_mix_g, norm_mem_g=v_norm_mem_g, w_in=v_w_in, w_gate=v_w_gate, ret_decay_fwd=v_ret_decay_fwd,
             ret_decay_bwd=v_ret_decay_bwd, ret_norm_g=v_ret_norm_g, pool_w=v_pool_w, pool_scale=v_pool_scale, na_q_norm_g=v_na_q_norm_g,
             na_k_norm_g=v_na_k_norm_g, na_rpb=v_na_rpb, mem_q_norm_g=v_mem_q_norm_g, mem_k_norm_g=v_mem_k_norm_g, w_mem_kv=v_w_mem_kv,
             w_branch=v_w_branch, w_out=v_w_out, norm_ffn_g=v_norm_ffn_g, w_ffn_in=v_w_ffn_in, w_ffn_out=v_w_ffn_out)
    assert x.shape == (1, 2048, D) and mem.shape == (1, N_MEM, D) and w_in.shape == (DEPTH, D, 9 * BW // N_DEV)

    blocks = [_to_exchange(name, tr, w[name][l]).astype(BF16) for l in range(DEPTH) for name, tr in BIG]
    started = _ici_start(blocks, [lax.empty((N_DEV,) + b.shape, BF16) for b in blocks], "gather", "gather_ici_start", len(BIG))

    def get_group(l, only, after, tag):
        at = [l * len(BIG) + i for i in only]
        lands = _ici_wait(started, after, "gather_ici_wait_%d%s" % (l, tag), at)
        whole = _gather_d2d([started[2][i] for i in at], lands, "gather_d2d")
        return {BIG[i][0]: _whole_from_gathered(BIG[i][0], g) for i, g in zip(only, whole)}

    def get_layer(l, after):
        if l > 0:
            return get_group(l, list(range(len(BIG))), after, ""), lambda stage, after2: {}
        groups = [[0, 1], [2, 3, 4], [5, 6]]
        return get_group(l, groups[0], started[4], "a"), lambda stage, after2: get_group(l, groups[stage], after2, "abc"[stage])

    cidx = lax.axis_index("c").astype(jnp.int32).reshape(1)
    chip = (2 * lax.axis_index("x") + lax.axis_index("y")).astype(jnp.int32).reshape(1)
    in_flight = []

    def flip_of(name, tr):
        return (lambda a: jnp.swapaxes(a, 1, 2)) if name in ("w_in", "w_ffn_in") else (lambda a: a)

    def rows3(a):
        return a.reshape(DEPTH, -1, a.shape[-1])

    opt_in = {name: tuple(rows3(flip_of(name, tr)(t[name])) for t in (w, m, v)) for name, tr in BIG}
    opt_out = {name: tuple(lax.empty(opt_in[name][0].shape, F32) for _ in range(4)) for name, _ in BIG}

    device = (2 * chip + cidx).astype(jnp.int32)

    def finish(l, st, after):
        recv = _ici_wait(st, after, "rs_ici_wait_%d" % l)
        sums = _sum_own(st[2], recv, chip if st[5] == 3 else device, "rs_sum")
        for (name, tr), s in zip(BIG, sums):
            g = s if name in ("w_in", "w_ffn_in") else _from_exchange(name, tr, s)
            wx, mx, vx = opt_in[name]
            opt_out[name] = _adamw_layer(l, wx, g.reshape(-1, g.shape[-1]), mx, vx, opt_out[name], "adamw_" + name)

    def on_grads(l, gb, after):
        send = [_by_destination(name, gb[name]) for name, _ in BIG]
        if l > 0:
            send = [s.reshape((N_DEV,) + s.shape[2:]) for s in send]
            st = _ici_start(send, [lax.empty(s.shape, BF16) for s in send], "by_device", "rs_ici_start_%d" % l)
        else:
            from_core = _rs_core_swap(send, "rs_core_swap")
            chip_part = _pair_sum(send, from_core, cidx)
            st = _ici_start(chip_part, [lax.empty(p.shape, BF16) for p in chip_part], "by_chip", "rs_ici_start_%d" % l)
        in_flight.append((l, st))
        return st[4]

    loss_local, dx, small_g = _local_step(x[0], mem[0], loss_target[0], {n: w[n] for n in SMALL}, get_layer, on_grads)

    last_started = in_flight[-1][1][4]
    for l, st in in_flight[:-1]:
        finish(l, st, last_started)

    small_all, = _all_gather([_pack_small(small_g, loss_local) + last_started[0:1]], "gather_small")
    packed_g = _sum_slots(small_all, "small_sum")
    small_sum, loss = _unpack_small(packed_g, {n: w[n] for n in SMALL})
    d_, m_, v_ = _adamw(_pack_small({n: w[n] for n in SMALL}), packed_g, _pack_small({n: m[n] for n in SMALL}),
                        _pack_small({n: v[n] for n in SMALL}), "adamw_small")
    updated = d_[0:8]
    for name, _ in BIG:
        updated = updated + opt_out[name][0][1, 0:8, 0:128]
    finish(*in_flight[-1], updated)

    grads, delta, new_m, new_v = {}, {}, {}, {}
    for name, tr in BIG:
        shape = flip_of(name, tr)(w[name]).shape
        delta[name], new_m[name], new_v[name], grads[name] = (flip_of(name, tr)(a.reshape(shape)) for a in opt_out[name])
    like = {n: w[n] for n in SMALL}
    ds, _ = _unpack_small(d_, like)
    ms, _ = _unpack_small(m_, like)
    vs, _ = _unpack_small(v_, like)
    for n in SMALL:
        grads[n], delta[n], new_m[n], new_v[n] = small_sum[n], ds[n], ms[n], vs[n]

    return (loss, dx[None], *[grads[n] for n in WEIGHTS], *[delta[n] for n in WEIGHTS], *[new_m[n] for n in WEIGHTS],
            *[new_v[n] for n in WEIGHTS])
```

```python
import functools

import numpy as np
import jax
import jax.numpy as jnp
from jax import lax
from jax.experimental import pallas as pl
from jax.experimental.pallas import tpu as pltpu

F32 = jnp.float32
BF16 = jnp.bfloat16
MXU = jnp.bfloat16
HI = lax.Precision.HIGHEST

DEPTH = 4
D = 1024
BW = 256
HD = 64
NH = 4
GRID_W = 64
NA_ROWS_WIN = 8
NA_COLS_WIN = 16
N_MEM = 256
FF = 2816
EPS = 1e-6
NEG = -1e30
ROPE_THETA = 10000.0
POOL_HALF_MAX = 8

ADAM_LR, ADAM_B1, ADAM_B2, ADAM_EPS, ADAM_WD, ADAM_STEP = 0.001, 0.9, 0.999, 1e-08, 0.01, 10

N_DEV = 8
VMEM_LIMIT = 56 * 1024 * 1024

RQ, RK, RV, RG, PV, NQ, NK, NV, MQ = range(9)

MESH = pl.DeviceIdType.MESH
ANY = pl.BlockSpec(memory_space=pl.ANY)
SMEM = pl.BlockSpec(memory_space=pltpu.SMEM)


def _cp(**kw):
    return pltpu.CompilerParams(vmem_limit_bytes=VMEM_LIMIT, **kw)


def _tile(n, cap):
    if n <= cap:
        return n
    best = None
    for t in range(128, cap + 1, 128):
        if n % t == 0:
            best = t
    assert best is not None, (n, cap)
    return best


def _sds(shape, dtype):
    return jax.ShapeDtypeStruct(shape, dtype)


def _lane_head(shape):
    return lax.shift_right_logical(lax.broadcasted_iota(jnp.int32, shape, len(shape) - 1), 6)


def _group_mean(z):
    i = lax.shift_right_logical(lax.broadcasted_iota(jnp.int32, (BW, BW), 0), 6)
    j = lax.shift_right_logical(lax.broadcasted_iota(jnp.int32, (BW, BW), 1), 6)
    g = jnp.where(i == j, 1.0 / HD, 0.0).astype(BF16)
    z_hi = z.astype(BF16)
    z_lo = (z - z_hi.astype(F32)).astype(BF16)
    return jnp.dot(z_hi, g, preferred_element_type=F32) + jnp.dot(z_lo, g, preferred_element_type=F32)


def _gnorm(t, g):
    r = lax.rsqrt(_group_mean(t * t) + EPS)
    return t * r * g


def _gnorm_bwd(dy, t, g):
    r = lax.rsqrt(_group_mean(t * t) + EPS)
    th = t * r
    dth = dy * g
    dt = r * (dth - th * _group_mean(dth * th))
    return dt, dy * th


def _swap_halves(t):
    lane = lax.broadcasted_iota(jnp.int32, t.shape, 1)
    return jnp.where((lane & 63) < 32, pltpu.roll(t, BW - 32, 1), pltpu.roll(t, 32, 1))


def _sigmoid(x):
    return 1.0 / (1.0 + jnp.exp(-x))


def _dot(a, b, ta=False, tb=False):
    return lax.dot_general(a.astype(MXU), b.astype(MXU), (((0 if ta else 1,), (1 if tb else 0,)), ((), ())),
                           preferred_element_type=F32)


def _stack_heads(t):
    head = _lane_head(t.shape)
    return jnp.concatenate([jnp.where(head == h, t, jnp.zeros_like(t)) for h in range(NH)], axis=0)


def _unstack_heads(t, rows):
    head = _lane_head((rows, BW))
    out = jnp.zeros((rows, BW), F32)
    for h in range(NH):
        out = out + jnp.where(head == h, t[h * rows:(h + 1) * rows], 0.0)
    return out


def _softmax_rows(s):
    m = jnp.max(s, axis=-1, keepdims=True)
    e = jnp.exp(s - m)
    return e / jnp.sum(e, axis=-1, keepdims=True)


def _acc(ref, val, first):
    @pl.when(first)
    def _():
        ref[...] = val

    @pl.when(jnp.logical_not(first))
    def _():
        ref[...] += val


def _mm(a, b, *, ta=False, tb=False, out_dtype=F32, add=None, dep=None, b_half=None, out_half=None, norm_g=None, name):
    m, k = (a.shape[1], a.shape[0]) if ta else a.shape
    n = b.shape[0] if tb else b.shape[1]
    assert b_half is None or (not tb and b.shape[0] == 2 * k)
    tm, tn = _tile(m, 1408), (n if norm_g is not None else _tile(n, 768))
    n_in = 2 + (add is not None) + (dep is not None) + (out_half is not None) + (norm_g is not None)

    def body(*refs):
        a_ref, b_ref, o_ref = refs[0], refs[1], refs[n_in]
        r = _dot(a_ref[...], b_ref[...], ta, tb)
        if add is not None:
            r = r + refs[2][...]
        o_ref[...] = r.astype(out_dtype)
        if norm_g is not None:
            scale = lax.rsqrt(jnp.mean(r * r, axis=-1, keepdims=True) + EPS)
            refs[n_in + 1][...] = (r * scale * refs[n_in - 1][...]).astype(BF16)

    kb = 0 if b_half is None else b_half
    a_spec = pl.BlockSpec((k, tm), lambda i, j: (0, i)) if ta else pl.BlockSpec((tm, k), lambda i, j: (i, 0))
    b_spec = pl.BlockSpec((tn, k), lambda i, j: (j, 0)) if tb else pl.BlockSpec((k, tn), lambda i, j: (kb, j))
    plain = pl.BlockSpec((tm, tn), lambda i, j: (i, j))
    ins, args = [a_spec, b_spec], [a, b]
    if add is not None:
        ins.append(plain)
        args.append(add)
    if dep is not None:
        ins.append(pl.BlockSpec((8, 128), lambda i, j: (0, 0)))
        args.append(dep)
    o_spec, o_shape, aliases = plain, _sds((m, n), out_dtype), {}
    if out_half is not None:
        half, prev = out_half
        o_spec = pl.BlockSpec((tm, tn), lambda i, j: (i + half * (m // tm), j))
        o_shape = _sds((2 * m, n), out_dtype)
        ins.append(ANY)
        args.append(lax.empty((2 * m, n), out_dtype) if prev is None else prev)
        aliases = {len(args) - 1: 0}
    if norm_g is not None:
        ins.append(pl.BlockSpec((1, n), lambda i, j: (0, 0)))
        args.append(norm_g)
        o_spec, o_shape = (o_spec, plain), (o_shape, _sds((m, n), BF16))
    return pl.pallas_call(
        body, grid=(m // tm, n // tn), in_specs=ins, out_specs=o_spec, out_shape=o_shape, input_output_aliases=aliases, name=name,
        compiler_params=_cp(dimension_semantics=("parallel", "parallel")))(*args)


def _mm_norm_bwd(a, b, add, x, g, res, *, b_half=None, name):
    m, k = a.shape
    n = b.shape[1]
    tm = 512
    kb = 0 if b_half is None else b_half

    def body(a_ref, b_ref, c_ref, x_ref, g_ref, res_ref, dx_ref, dxb_ref, dg_ref):
        dhv = _dot(a_ref[...], b_ref[...]) + c_ref[...]
        xv = x_ref[...]
        r = lax.rsqrt(jnp.mean(xv * xv, axis=-1, keepdims=True) + EPS)
        xh = xv * r
        dxh = dhv * g_ref[...]
        dx = res_ref[...] + r * (dxh - xh * jnp.mean(dxh * xh, axis=-1, keepdims=True))
        dx_ref[...] = dx
        dxb_ref[...] = dx.astype(BF16)
        _acc(dg_ref, jnp.sum(dhv * xh, axis=0, keepdims=True), pl.program_id(0) == 0)

    row = pl.BlockSpec((tm, n), lambda i: (i, 0))
    vec = pl.BlockSpec((1, n), lambda i: (0, 0))
    return pl.pallas_call(
        body, grid=(m // tm,),
        in_specs=[pl.BlockSpec((tm, k), lambda i: (i, 0)), pl.BlockSpec((k, n), lambda i: (kb, 0)), row, row, vec, row],
        out_specs=(row, row, vec), out_shape=(_sds((m, n), F32), _sds((m, n), BF16), _sds((1, n), F32)), name=name,
        compiler_params=_cp())(a, b, add, x, g, res)


def _rmsnorm_fwd(x, g, name):
    t, d = x.shape
    tm = _tile(t, 256)

    def body(x_ref, g_ref, o_ref):
        xv = x_ref[...]
        r = lax.rsqrt(jnp.mean(xv * xv, axis=-1, keepdims=True) + EPS)
        o_ref[...] = (xv * r * g_ref[...]).astype(o_ref.dtype)

    return pl.pallas_call(
        body, grid=(t // tm,), in_specs=[pl.BlockSpec((tm, d), lambda i: (i, 0)), pl.BlockSpec((1, d), lambda i: (0, 0))],
        out_specs=pl.BlockSpec((tm, d), lambda i: (i, 0)), out_shape=_sds((t, d), BF16), name=name, compiler_params=_cp())(x, g)


def _rmsnorm_bwd(dh, x, g, res, name):
    t, d = x.shape
    tm = _tile(t, 256)

    def body(dh_ref, x_ref, g_ref, res_ref, dx_ref, dxb_ref, dg_ref):
        xv = x_ref[...]
        dhv = dh_ref[...]
        r = lax.rsqrt(jnp.mean(xv * xv, axis=-1, keepdims=True) + EPS)
        xh = xv * r
        dxh = dhv * g_ref[...]
        dx = res_ref[...] + r * (dxh - xh * jnp.mean(dxh * xh, axis=-1, keepdims=True))
        dx_ref[...] = dx
        dxb_ref[...] = dx.astype(BF16)
        _acc(dg_ref, jnp.sum(dhv * xh, axis=0, keepdims=True), pl.program_id(0) == 0)

    row = pl.BlockSpec((tm, d), lambda i: (i, 0))
    vec = pl.BlockSpec((1, d), lambda i: (0, 0))
    return pl.pallas_call(
        body, grid=(t // tm,), in_specs=[row, row, vec, row], out_specs=(row, row, vec),
        out_shape=(_sds((t, d), F32), _sds((t, d), BF16), _sds((1, d), F32)), name=name, compiler_params=_cp())(dh, x, g, res)


def _prep_fwd(proj, cos2, sin2, g_naq, g_nak, g_mq):
    t = proj.shape[0]
    tm = 256

    def body(p_ref, cos_ref, sin_ref, gq_ref, gk_ref, gm_ref, rq_ref, rk_ref, rv_ref, nq_ref, nk_ref, nv_ref, mq_ref):
        def col(c):
            return p_ref[:, c * BW:(c + 1) * BW]

        cosv, sinv = cos_ref[...], sin_ref[...]

        def rot(tv):
            return tv * cosv + _swap_halves(tv) * sinv

        rq_ref[...] = (rot(col(RQ)) * (HD ** -0.5)).astype(BF16)
        rk_ref[...] = rot(col(RK)).astype(BF16)
        rv_ref[...] = col(RV).astype(BF16)
        nq_ref[...] = _gnorm(col(NQ), gq_ref[...]).astype(BF16)
        nk_ref[...] = _gnorm(col(NK), gk_ref[...]).astype(BF16)
        nv_ref[...] = col(NV).astype(BF16)
        mq_ref[...] = _gnorm(col(MQ), gm_ref[...]).astype(BF16)

    blk = pl.BlockSpec((tm, BW), lambda i: (i, 0))
    vec = pl.BlockSpec((1, BW), lambda i: (0, 0))
    return pl.pallas_call(
        body, grid=(t // tm,), in_specs=[pl.BlockSpec((tm, 9 * BW), lambda i: (i, 0)), blk, blk, vec, vec, vec],
        out_specs=tuple(blk for _ in range(7)), out_shape=tuple(_sds((t, BW), BF16) for _ in range(7)),
        name="prep_fwd", compiler_params=_cp())(proj, cos2, sin2, g_naq, g_nak, g_mq)


def _prep_bwd(proj, cos2, sin2, g_naq, g_nak, g_mq, d_rq, d_rk, d_rv, d_rg, d_pv, d_nq, d_nk, d_nv, d_mq):
    t = proj.shape[0]
    tm = 256

    def body(p_ref, cos_ref, sin_ref, gq_ref, gk_ref, gm_ref, drq_ref, drk_ref, drv_ref, drg_ref, dpv_ref, dnq_ref, dnk_ref,
             dnv_ref, dmq_ref, o_ref, dgq_ref, dgk_ref, dgm_ref):
        first = pl.program_id(0) == 0

        def col(c):
            return p_ref[:, c * BW:(c + 1) * BW]

        def put(c, v):
            o_ref[:, c * BW:(c + 1) * BW] = v.astype(BF16)

        cosv, sinv = cos_ref[...], sin_ref[...]

        def rot_t(dv):
            return dv * cosv + _swap_halves(dv * sinv)

        put(RQ, rot_t(drq_ref[...] * (HD ** -0.5)))
        put(RK, rot_t(drk_ref[...]))
        put(RV, drv_ref[...])
        put(RG, drg_ref[...])
        put(PV, dpv_ref[...])
        dq, gq = _gnorm_bwd(dnq_ref[...], col(NQ), gq_ref[...])
        put(NQ, dq)
        _acc(dgq_ref, jnp.sum(gq, axis=0, keepdims=True), first)
        dk, gk = _gnorm_bwd(dnk_ref[...], col(NK), gk_ref[...])
        put(NK, dk)
        _acc(dgk_ref, jnp.sum(gk, axis=0, keepdims=True), first)
        put(NV, dnv_ref[...])
        dm, gm = _gnorm_bwd(dmq_ref[...], col(MQ), gm_ref[...])
        put(MQ, dm)
        _acc(dgm_ref, jnp.sum(gm, axis=0, keepdims=True), first)

    blk = pl.BlockSpec((tm, BW), lambda i: (i, 0))
    vec = pl.BlockSpec((1, BW), lambda i: (0, 0))
    wide = pl.BlockSpec((tm, 9 * BW), lambda i: (i, 0))
    return pl.pallas_call(
        body, grid=(t // tm,), in_specs=[wide, blk, blk, vec, vec, vec] + [blk] * 9, out_specs=(wide, vec, vec, vec),
        out_shape=(_sds((t, 9 * BW), BF16), _sds((1, BW), F32), _sds((1, BW), F32), _sds((1, BW), F32)),
        name="prep_bwd", compiler_params=_cp())(proj, cos2, sin2, g_naq, g_nak, g_mq, d_rq, d_rk, d_rv, d_rg, d_pv, d_nq, d_nk,
                                                d_nv, d_mq)


RET_B = 256


def _ret_consts(lgf_ref, lgb_ref):
    bsz = RET_B
    head = _lane_head((1, BW))
    lf, lb = jnp.zeros((1, BW), F32), jnp.zeros((1, BW), F32)
    for h in range(NH):
        lf = lf + jnp.where(head == h, lgf_ref[h], 0.0)
        lb = lb + jnp.where(head == h, lgb_ref[h], 0.0)
    pos = lax.broadcasted_iota(jnp.int32, (bsz, BW), 0).astype(F32)
    up, down = pos + 1.0, (bsz - 1.0) - pos
    c = dict(up=up, down=down, kf=jnp.exp(down * lf), kb=jnp.exp(up * lb), qf=jnp.exp(up * lf), qb=jnp.exp(down * lb),
             cf=jnp.exp(bsz * lf), cb=jnp.exp(bsz * lb))
    diff = (lax.broadcasted_iota(jnp.int32, (NH * bsz, 1), 0) & (bsz - 1)) - lax.broadcasted_iota(jnp.int32, (1, bsz), 1)
    c["causal"] = diff >= 0
    c["dist"] = jnp.abs(diff).astype(F32)
    lgf = jnp.concatenate([jnp.full((bsz, 1), lgf_ref[h], F32) for h in range(NH)], axis=0)
    lgb = jnp.concatenate([jnp.full((bsz, 1), lgb_ref[h], F32) for h in range(NH)], axis=0)
    c["dm"] = jnp.exp(c["dist"] * jnp.where(c["causal"], lgf, lgb))
    c["bd"] = _lane_head((BW, BW)) == lax.shift_right_logical(lax.broadcasted_iota(jnp.int32, (BW, BW), 0), 6)
    return c


def _ret_states(k_ref, v_ref, st_ref, c, nb):
    bsz = RET_B

    def summary(b, decay):
        kb = k_ref[b * bsz:(b + 1) * bsz, :].astype(F32)
        return jnp.where(c["bd"], _dot(kb * decay, v_ref[b * bsz:(b + 1) * bsz, :], ta=True), 0.0)

    f = jnp.zeros((BW, BW), F32)
    for b in range(nb):
        st_ref[b] = f
        if b < nb - 1:
            f = c["cf"] * f + summary(b, c["kf"])
    g = jnp.zeros((BW, BW), F32)
    for b in reversed(range(nb)):
        st_ref[nb + b] = g
        if b > 0:
            g = c["cb"] * g + summary(b, c["kb"])


def _ret_fwd(q, k, v, proj, lgf, lgb, g_ret):
    t = q.shape[0]
    bsz, nb = RET_B, t // RET_B

    def body(lgf_ref, lgb_ref, q_ref, k_ref, v_ref, rg_ref, g_ref, o_ref, ret_ref, st_ref):
        c = _ret_consts(lgf_ref, lgb_ref)
        _ret_states(k_ref, v_ref, st_ref, c, nb)
        for b in range(nb):
            blk = slice(b * bsz, (b + 1) * bsz)
            qb, kb, vb = q_ref[blk, :], k_ref[blk, :], v_ref[blk, :]
            s = _dot(_stack_heads(qb), kb, tb=True)
            o = _unstack_heads(_dot(s * c["dm"], vb), bsz)
            q32 = qb.astype(F32)
            o = o + _dot(q32 * c["qf"], st_ref[b]) + _dot(q32 * c["qb"], st_ref[nb + b])
            o_ref[blk, :] = o
            rg = rg_ref[blk, :]
            ret_ref[blk, :] = (_gnorm(o, g_ref[...]) * (rg * _sigmoid(rg))).astype(BF16)

    whole = pl.BlockSpec((t, BW), lambda i: (0, 0))
    return pl.pallas_call(
        body, grid=(1,),
        in_specs=[SMEM, SMEM, whole, whole, whole, pl.BlockSpec((t, BW), lambda i: (0, RG)), pl.BlockSpec((1, BW), lambda i: (0, 0))],
        out_specs=(whole, whole), out_shape=(_sds((t, BW), F32), _sds((t, BW), BF16)),
        scratch_shapes=[pltpu.VMEM((2 * nb, BW, BW), F32)], name="ret_fwd", compiler_params=_cp())(lgf, lgb, q, k, v, proj, g_ret)


def _ret_post_bwd(dbr, o_ret, proj, g_ret):
    t = o_ret.shape[0]
    tm = 256

    def body(d_ref, o_ref, rg_ref, g_ref, do_ref, drg_ref, dg_ref):
        dret, o, rg, g = d_ref[...], o_ref[...], rg_ref[...], g_ref[...]
        sg = _sigmoid(rg)
        do, dgain = _gnorm_bwd(dret * (rg * sg), o, g)
        do_ref[...] = do.astype(BF16)
        drg_ref[...] = dret * _gnorm(o, g) * (sg * (1.0 + rg * (1.0 - sg)))
        _acc(dg_ref, jnp.sum(dgain, axis=0, keepdims=True), pl.program_id(0) == 0)

    blk = pl.BlockSpec((tm, BW), lambda i: (i, 0))
    vec = pl.BlockSpec((1, BW), lambda i: (0, 0))
    return pl.pallas_call(
        body, grid=(t // tm,), in_specs=[blk, blk, pl.BlockSpec((tm, BW), lambda i: (i, RG)), vec], out_specs=(blk, blk, vec),
        out_shape=(_sds((t, BW), BF16), _sds((t, BW), F32), _sds((1, BW), F32)), name="ret_post_bwd",
        compiler_params=_cp())(dbr, o_ret, proj, g_ret)


def _ret_bwd(do, q, k, v, lgf, lgb):
    t = q.shape[0]
    bsz, nb = RET_B, t // RET_B

    def body(lgf_ref, lgb_ref, d_ref, q_ref, k_ref, v_ref, dq_ref, dk_ref, dv_ref, dlg_ref, st_ref, sd_ref):
        c = _ret_consts(lgf_ref, lgb_ref)
        _ret_states(k_ref, v_ref, st_ref, c, nb)
        lane_f, lane_b = jnp.zeros((1, BW), F32), jnp.zeros((1, BW), F32)
        row_f, row_b = jnp.zeros((NH * bsz, 1), F32), jnp.zeros((NH * bsz, 1), F32)

        def rows(x):
            return jnp.sum(x, axis=0, keepdims=True)

        for b in range(nb):
            blk = slice(b * bsz, (b + 1) * bsz)
            qb, kb, vb, dob = q_ref[blk, :], k_ref[blk, :], v_ref[blk, :], d_ref[blk, :]
            q32 = qb.astype(F32)
            qs, dos = _stack_heads(qb), _stack_heads(dob)
            s = _dot(qs, kb, tb=True)
            da = _dot(dos, vb, tb=True)
            dv_ref[blk, :] = _dot(s * c["dm"], dos, ta=True)
            ds = da * c["dm"]
            w = ds * s * c["dist"]
            row_f = row_f + jnp.sum(jnp.where(c["causal"], w, 0.0), axis=1, keepdims=True)
            row_b = row_b + jnp.sum(jnp.where(c["causal"], 0.0, w), axis=1, keepdims=True)
            dsb = ds.astype(MXU)
            dk_ref[blk, :] = _dot(dsb, qs, ta=True)
            dq_f = _dot(dob, st_ref[b], tb=True) * c["qf"]
            dq_b = _dot(dob, st_ref[nb + b], tb=True) * c["qb"]
            lane_f = lane_f + rows(c["up"] * dq_f * q32)
            lane_b = lane_b + rows(c["down"] * dq_b * q32)
            dq_ref[blk, :] = _unstack_heads(_dot(dsb, kb), bsz) + dq_f + dq_b
            sd_ref[b] = jnp.where(c["bd"], _dot(q32 * c["qf"], dob, ta=True), 0.0)
            sd_ref[nb + b] = jnp.where(c["bd"], _dot(q32 * c["qb"], dob, ta=True), 0.0)

        def through_state(b, grad, decay, weight, lane):
            blk = slice(b * bsz, (b + 1) * bsz)
            k32 = k_ref[blk, :].astype(F32)
            dk = _dot(v_ref[blk, :], grad, tb=True) * decay
            dk_ref[blk, :] += dk
            dv_ref[blk, :] += _dot(k32 * decay, grad)
            return lane + rows(weight * dk * k32)

        phi = jnp.zeros((BW, BW), F32)
        for b in reversed(range(nb)):
            if b < nb - 1:
                lane_f = through_state(b, phi, c["kf"], c["down"], lane_f)
                lane_f = lane_f + bsz * rows(c["cf"] * st_ref[b] * phi)
            phi = sd_ref[b] + c["cf"] * phi
        gam = jnp.zeros((BW, BW), F32)
        for b in range(nb):
            if b > 0:
                lane_b = through_state(b, gam, c["kb"], c["up"], lane_b)
                lane_b = lane_b + bsz * rows(c["cb"] * st_ref[nb + b] * gam)
            gam = sd_ref[nb + b] + c["cb"] * gam

        head = _lane_head((1, BW))
        for h in range(NH):
            tot_f = jnp.sum(row_f[h * bsz:(h + 1) * bsz, :]) + jnp.sum(jnp.where(head == h, lane_f, 0.0))
            tot_b = jnp.sum(row_b[h * bsz:(h + 1) * bsz, :]) + jnp.sum(jnp.where(head == h, lane_b, 0.0))
            dlg_ref[h:h + 1, :] = jnp.full((1, 128), tot_f, F32)
            dlg_ref[NH + h:NH + h + 1, :] = jnp.full((1, 128), tot_b, F32)

    whole = pl.BlockSpec((t, BW), lambda i: (0, 0))
    return pl.pallas_call(
        body, grid=(1,), in_specs=[SMEM, SMEM, whole, whole, whole, whole],
        out_specs=(whole, whole, whole, pl.BlockSpec((2 * NH, 128), lambda i: (0, 0))),
        out_shape=(_sds((t, BW), F32), _sds((t, BW), F32), _sds((t, BW), F32), _sds((2 * NH, 128), F32)),
        scratch_shapes=[pltpu.VMEM((2 * nb, BW, BW), F32), pltpu.VMEM((2 * nb, BW, BW), F32)], name="ret_bwd",
        compiler_params=_cp())(lgf, lgb, do, q, k, v)


def _pool_windows(t):
    row = lax.broadcasted_iota(jnp.int32, (t, BW), 0)
    half = lax.shift_left(jnp.ones((t, BW), jnp.int32), _lane_head((t, BW)))
    cnt = (jnp.minimum(row + half, t) - jnp.maximum(row - half, 0)).astype(F32)
    return row, half, cnt


def _pool_window_sum(v, row, half, t, transpose):
    out = jnp.zeros_like(v)
    for j in range(-POOL_HALF_MAX, POOL_HALF_MAX):
        src = row - j if transpose else row + j
        ok = (src >= 0) & (src < t) & (j >= -half) & (j < half)
        out = out + jnp.where(ok, pltpu.roll(v, (j if transpose else -j) % t, 0), 0.0)
    return out


def _pool_fwd(proj, wbd, scale):
    t = proj.shape[0]

    def body(v_ref, w_ref, s_ref, o_ref):
        v = v_ref[...]
        row, half, cnt = _pool_windows(t)
        pooled = _pool_window_sum(v, row, half, t, False) / cnt - v
        o_ref[...] = (_dot(pooled, w_ref[...]) * s_ref[...]).astype(BF16)

    return pl.pallas_call(
        body, grid=(1,),
        in_specs=[pl.BlockSpec((t, BW), lambda i: (0, PV)), pl.BlockSpec((BW, BW), lambda i: (0, 0)), pl.BlockSpec((1, BW), lambda i: (0, 0))],
        out_specs=pl.BlockSpec((t, BW), lambda i: (0, 0)), out_shape=_sds((t, BW), BF16), name="pool_fwd",
        compiler_params=_cp())(proj, wbd, scale)


def _pool_bwd(dbr, proj, wbd, scale):
    t = proj.shape[0]

    def body(d_ref, v_ref, w_ref, s_ref, dv_ref, dw_ref, ds_ref):
        v, dout = v_ref[...], d_ref[...]
        row, half, cnt = _pool_windows(t)
        pooled = _pool_window_sum(v, row, half, t, False) / cnt - v
        mixed = _dot(pooled, w_ref[...])
        ds_ref[...] = jnp.sum(dout * mixed, axis=0, keepdims=True)
        dmixed = dout * s_ref[...]
        dw_ref[...] = _dot(pooled, dmixed, ta=True)
        dpooled = _dot(dmixed, w_ref[...], tb=True)
        dv_ref[...] = _pool_window_sum(dpooled / cnt, row, half, t, True) - dpooled

    return pl.pallas_call(
        body, grid=(1,),
        in_specs=[pl.BlockSpec((t, BW), lambda i: (0, 1)), pl.BlockSpec((t, BW), lambda i: (0, PV)),
                  pl.BlockSpec((BW, BW), lambda i: (0, 0)), pl.BlockSpec((1, BW), lambda i: (0, 0))],
        out_specs=(pl.BlockSpec((t, BW), lambda i: (0, 0)), pl.BlockSpec((BW, BW), lambda i: (0, 0)), pl.BlockSpec((1, BW), lambda i: (0, 0))),
        out_shape=(_sds((t, BW), F32), _sds((BW, BW), F32), _sds((1, BW), F32)), name="pool_bwd",
        compiler_params=_cp())(dbr, proj, wbd, scale)


NA_KEYS = NA_ROWS_WIN * GRID_W
NA_PAIRS = 2 * NA_ROWS_WIN - 2


def _na_window(r, n_rows):
    rs = jnp.clip(r - NA_ROWS_WIN // 2, 0, n_rows - NA_ROWS_WIN)
    return pl.multiple_of(rs * GRID_W, GRID_W), rs - r + (NA_ROWS_WIN - 1)


def _na_bias(b_ref, a0):
    return jnp.concatenate([b_ref[a0 + 2 * j] for j in range(NA_ROWS_WIN // 2)], axis=1)


NA_STEP_ROWS = 8


def _na_fwd(q, k, v, ball):
    t = q.shape[0]
    n_rows = t // GRID_W
    rows = NA_STEP_ROWS

    def body(q_ref, k_ref, v_ref, b_ref, o_ref):
        for rr in range(rows):
            start, a0 = _na_window(pl.program_id(0) * rows + rr, n_rows)
            own = slice(rr * GRID_W, (rr + 1) * GRID_W)
            qs = _stack_heads(q_ref[own, :])
            s = _dot(qs, k_ref[pl.ds(start, NA_KEYS), :], tb=True) * (HD ** -0.5) + _na_bias(b_ref, a0)
            p = _softmax_rows(s)
            o_ref[own, :] = _unstack_heads(_dot(p, v_ref[pl.ds(start, NA_KEYS), :]), GRID_W).astype(BF16)

    blk = pl.BlockSpec((rows * GRID_W, BW), lambda r: (r, 0))
    whole = pl.BlockSpec((t, BW), lambda r: (0, 0))
    return pl.pallas_call(
        body, grid=(n_rows // rows,), in_specs=[blk, whole, whole, pl.BlockSpec(ball.shape, lambda r: (0, 0, 0))],
        out_specs=blk, out_shape=_sds((t, BW), BF16), name="na_fwd", compiler_params=_cp())(q, k, v, ball)


def _na_bwd(dbr, q, k, v, ball):
    t = q.shape[0]
    n_rows = t // GRID_W

    rows = NA_STEP_ROWS

    def body(d_ref, q_ref, k_ref, v_ref, b_ref, dq_ref, dk_ref, dv_ref, db_ref):
        @pl.when(pl.program_id(0) == 0)
        def _():
            dk_ref[...] = jnp.zeros_like(dk_ref)
            dv_ref[...] = jnp.zeros_like(dv_ref)
            db_ref[...] = jnp.zeros_like(db_ref)

        for rr in range(rows):
            start, a0 = _na_window(pl.program_id(0) * rows + rr, n_rows)
            keys = pl.ds(start, NA_KEYS)
            own = slice(rr * GRID_W, (rr + 1) * GRID_W)
            qs = _stack_heads(q_ref[own, :])
            kb, vb = k_ref[keys, :], v_ref[keys, :]
            p = _softmax_rows(_dot(qs, kb, tb=True) * (HD ** -0.5) + _na_bias(b_ref, a0))
            dos = _stack_heads(d_ref[own, :]).astype(MXU)
            dp = _dot(dos, vb, tb=True)
            dv_ref[keys, :] += _dot(p, dos, ta=True)
            ds = p * (dp - jnp.sum(dp * p, axis=-1, keepdims=True))
            for j in range(NA_ROWS_WIN // 2):
                db_ref[a0 + 2 * j] += ds[:, 2 * j * GRID_W:(2 * j + 2) * GRID_W]
            dsb = (ds * (HD ** -0.5)).astype(MXU)
            dq_ref[own, :] = _unstack_heads(_dot(dsb, kb), GRID_W)
            dk_ref[keys, :] += _dot(dsb, qs, ta=True)

    blk = pl.BlockSpec((rows * GRID_W, BW), lambda r: (r, 0))
    whole = pl.BlockSpec((t, BW), lambda r: (0, 0))
    tab = pl.BlockSpec(ball.shape, lambda r: (0, 0, 0))
    return pl.pallas_call(
        body, grid=(n_rows // rows,), in_specs=[pl.BlockSpec((rows * GRID_W, BW), lambda r: (r, 2)), blk, whole, whole, tab],
        out_specs=(blk, whole, whole, tab),
        out_shape=(_sds((t, BW), F32), _sds((t, BW), F32), _sds((t, BW), F32), _sds(ball.shape, F32)), name="na_bwd",
        compiler_params=_cp())(dbr, q, k, v, ball)


def _rpb_expand(rpb_pad, onehot):
    def body(r_ref, e_ref, o_ref):
        o_ref[...] = jnp.dot(r_ref[...], e_ref[...], precision=HI, preferred_element_type=F32)

    return pl.pallas_call(body, out_shape=_sds((rpb_pad.shape[0], GRID_W * GRID_W), F32), name="rpb_expand",
                          compiler_params=_cp())(rpb_pad, onehot)


def _rpb_reduce(dtab, onehot):
    def body(d_ref, e_ref, o_ref):
        o_ref[...] = lax.dot_general(d_ref[...], e_ref[...], (((1,), (1,)), ((), ())), precision=HI, preferred_element_type=F32)

    return pl.pallas_call(body, out_shape=_sds((dtab.shape[0], 128), F32), name="rpb_reduce", compiler_params=_cp())(dtab, onehot)


MEM_TQ = 256


def _mem_fwd(q, mk, mv):
    t = q.shape[0]
    tq = MEM_TQ

    def body(q_ref, k_ref, v_ref, o_ref):
        p = _softmax_rows(_dot(_stack_heads(q_ref[...]), k_ref[...], tb=True) * (HD ** -0.5))
        o_ref[...] = _unstack_heads(_dot(p, v_ref[...]), tq).astype(BF16)

    blk = pl.BlockSpec((tq, BW), lambda i: (i, 0))
    kv = pl.BlockSpec((N_MEM, BW), lambda i: (0, 0))
    return pl.pallas_call(body, grid=(t // tq,), in_specs=[blk, kv, kv], out_specs=blk, out_shape=_sds((t, BW), BF16),
                          name="mem_fwd", compiler_params=_cp())(q, mk, mv)


def _mem_bwd(dbr, q, mk, mv):
    t = q.shape[0]
    tq = MEM_TQ

    def body(d_ref, q_ref, k_ref, v_ref, dq_ref, dk_ref, dv_ref):
        first = pl.program_id(0) == 0
        qs = _stack_heads(q_ref[...])
        dos = _stack_heads(d_ref[...]).astype(MXU)
        p = _softmax_rows(_dot(qs, k_ref[...], tb=True) * (HD ** -0.5))
        dp = _dot(dos, v_ref[...], tb=True)
        _acc(dv_ref, _dot(p, dos, ta=True), first)
        dsb = (p * (dp - jnp.sum(dp * p, axis=-1, keepdims=True)) * (HD ** -0.5)).astype(MXU)
        dq_ref[...] = _unstack_heads(_dot(dsb, k_ref[...]), tq)
        _acc(dk_ref, _dot(dsb, qs, ta=True), first)

    blk = pl.BlockSpec((tq, BW), lambda i: (i, 0))
    kv = pl.BlockSpec((N_MEM, BW), lambda i: (0, 0))
    return pl.pallas_call(
        body, grid=(t // tq,), in_specs=[pl.BlockSpec((tq, BW), lambda i: (i, 3)), blk, kv, kv], out_specs=(blk, kv, kv),
        out_shape=(_sds((t, BW), F32), _sds((N_MEM, BW), F32), _sds((N_MEM, BW), F32)), name="mem_bwd",
        compiler_params=_cp())(dbr, q, mk, mv)


def _memkv_prep(kv, g_mk):
    def body(kv_ref, g_ref, k_ref, v_ref):
        k_ref[...] = _gnorm(kv_ref[:, 0:BW], g_ref[...]).astype(BF16)
        v_ref[...] = kv_ref[:, BW:2 * BW].astype(BF16)

    return pl.pallas_call(body, out_shape=(_sds((N_MEM, BW), BF16), _sds((N_MEM, BW), BF16)), name="memkv_prep",
                          compiler_params=_cp())(kv, g_mk)


def _memkv_bwd(kv, dk, dv, g_mk):
    def body(kv_ref, dk_ref, dv_ref, g_ref, o_ref, dg_ref):
        dkk, gain = _gnorm_bwd(dk_ref[...], kv_ref[:, 0:BW], g_ref[...])
        o_ref[:, 0:BW] = dkk.astype(BF16)
        o_ref[:, BW:2 * BW] = dv_ref[...].astype(BF16)
        dg_ref[...] = jnp.sum(gain, axis=0, keepdims=True)

    return pl.pallas_call(body, out_shape=(_sds((N_MEM, 2 * BW), BF16), _sds((1, BW), F32)), name="memkv_bwd",
                          compiler_params=_cp())(kv, dk, dv, g_mk)


MERGE_TM = 512


def _merge_fwd(brs, wbt, gp):
    t = gp.shape[0]
    tm = MERGE_TM

    def body(b0, b1, b2, b3, wb_ref, gp_ref, o_ref):
        out = jnp.zeros((tm, D), F32)
        for n, b_ref in enumerate((b0, b1, b2, b3)):
            up = _dot(b_ref[...], wb_ref[n], tb=True)
            out = out + _sigmoid(gp_ref[:, n * D:(n + 1) * D].astype(F32)) * up
        o_ref[...] = out.astype(BF16)

    blk = pl.BlockSpec((tm, BW), lambda i: (i, 0))
    return pl.pallas_call(
        body, grid=(t // tm,),
        in_specs=[blk, blk, blk, blk, pl.BlockSpec((NH, D, BW), lambda i: (0, 0, 0)), pl.BlockSpec((tm, NH * D), lambda i: (i, 0))],
        out_specs=pl.BlockSpec((tm, D), lambda i: (i, 0)), out_shape=_sds((t, D), BF16), name="merge_fwd",
        compiler_params=_cp())(*brs, wbt, gp)


def _merge_bwd(dmerged, brs, wbt, gp):
    t = gp.shape[0]
    tm = MERGE_TM
    steps = t // tm

    def body(d_ref, b0, b1, b2, b3, wb_ref, gp_ref, dgp_ref, dbr_ref, dwb_ref, acc_ref):
        i = pl.program_id(0)
        dm = d_ref[...]
        for n, b_ref in enumerate((b0, b1, b2, b3)):
            br = b_ref[...]
            up = _dot(br, wb_ref[n], tb=True)
            g = _sigmoid(gp_ref[:, n * D:(n + 1) * D].astype(F32))
            dgp_ref[:, n * D:(n + 1) * D] = (dm * up * (g * (1.0 - g))).astype(BF16)
            dup = (dm * g).astype(BF16)
            dbr_ref[:, n * BW:(n + 1) * BW] = _dot(dup, wb_ref[n])
            part = _dot(dup, br, ta=True)

            @pl.when(i == 0)
            def _():
                acc_ref[n] = part

            @pl.when(i > 0)
            def _():
                acc_ref[n] += part

        @pl.when(i == steps - 1)
        def _():
            dwb_ref[...] = acc_ref[...].astype(BF16)

    row = pl.BlockSpec((tm, D), lambda i: (i, 0))
    blk = pl.BlockSpec((tm, BW), lambda i: (i, 0))
    wide = pl.BlockSpec((tm, NH * D), lambda i: (i, 0))
    whole = pl.BlockSpec((NH, D, BW), lambda i: (0, 0, 0))
    return pl.pallas_call(
        body, grid=(steps,), in_specs=[row, blk, blk, blk, blk, whole, wide], out_specs=(wide, row, whole),
        out_shape=(_sds((t, NH * D), BF16), _sds((t, NH * BW), F32), _sds((NH, D, BW), BF16)),
        scratch_shapes=[pltpu.VMEM((NH, D, BW), F32)], name="merge_bwd", compiler_params=_cp())(dmerged, *brs, wbt, gp)


FFN_TN = 256


def _ffn_in_fwd(h2, w_t):
    t = h2.shape[0]
    tm, tn = _tile(t, 2048), FFN_TN
    nj = FF // tn

    def body(x_ref, wa_ref, wg_ref, a_ref, g_ref, y_ref):
        x = x_ref[...]
        a, g = _dot(x, wa_ref[...], tb=True), _dot(x, wg_ref[...], tb=True)
        a_ref[...] = a.astype(BF16)
        g_ref[...] = g.astype(BF16)
        y_ref[...] = (a * _sigmoid(a) * g).astype(BF16)

    out = pl.BlockSpec((tm, tn), lambda i, j: (i, j))
    return pl.pallas_call(
        body, grid=(t // tm, nj),
        in_specs=[pl.BlockSpec((tm, D), lambda i, j: (i, 0)), pl.BlockSpec((tn, D), lambda i, j: (j, 0)),
                  pl.BlockSpec((tn, D), lambda i, j: (j + nj, 0))],
        out_specs=(out, out, out), out_shape=tuple(_sds((t, FF), BF16) for _ in range(3)), name="ffn_in_fwd",
        compiler_params=_cp(dimension_semantics=("parallel", "parallel")))(h2, w_t, w_t)


def _ffn_out_bwd(dx2b, w_out, a, g, dep):
    t = dx2b.shape[0]
    tm, tn = _tile(t, 2048), FFN_TN

    def body(*refs):
        x_ref, w_ref, a_ref, g_ref = refs[:4]
        da_ref, dg_ref = refs[-2:]
        d = _dot(x_ref[...], w_ref[...], tb=True)
        av, gv = a_ref[...].astype(F32), g_ref[...].astype(F32)
        s = _sigmoid(av)
        da_ref[...] = (d * gv * (s * (1.0 + av * (1.0 - s)))).astype(BF16)
        dg_ref[...] = (d * (av * s)).astype(BF16)

    blk = pl.BlockSpec((tm, tn), lambda i, j: (i, j))
    ins = [pl.BlockSpec((tm, D), lambda i, j: (i, 0)), pl.BlockSpec((tn, D), lambda i, j: (j, 0)), blk, blk]
    args = [dx2b, w_out, a, g]
    if dep is not None:
        ins.append(pl.BlockSpec((8, 128), lambda i, j: (0, 0)))
        args.append(dep)
    return pl.pallas_call(
        body, grid=(t // tm, FF // tn), in_specs=ins, out_specs=(blk, blk),
        out_shape=(_sds((t, FF), BF16), _sds((t, FF), BF16)), name="ffn_out_bwd",
        compiler_params=_cp(dimension_semantics=("parallel", "parallel")))(*args)


def _loss_head(y, target):
    t, d = y.shape
    tm = 256

    def body(y_ref, t_ref, dy_ref, dyb_ref, l_ref):
        e = y_ref[...] - t_ref[...]
        dy_ref[...] = e * (1.0 / d)
        dyb_ref[...] = (e * (1.0 / d)).astype(BF16)
        _acc(l_ref, jnp.full((8, 128), 0.5 * jnp.sum(jnp.sum(e * e, axis=-1, keepdims=True) * (1.0 / d)), F32), pl.program_id(0) == 0)

    row = pl.BlockSpec((tm, d), lambda i: (i, 0))
    return pl.pallas_call(body, grid=(t // tm,), in_specs=[row, row], out_specs=(row, row, pl.BlockSpec((8, 128), lambda i: (0, 0))),
                          out_shape=(_sds((t, d), F32), _sds((t, d), BF16), _sds((8, 128), F32)), name="loss_head",
                          compiler_params=_cp())(y, target)


def _sum_slots(x, name):
    k, r, c = x.shape
    tr = _tile(r, 512) if r % 128 == 0 else r

    def body(x_ref, o_ref):
        acc = x_ref[0].astype(F32)
        for s in range(1, k):
            acc = acc + x_ref[s].astype(F32)
        o_ref[...] = acc

    return pl.pallas_call(body, grid=(r // tr,), in_specs=[pl.BlockSpec((k, tr, c), lambda i: (0, i, 0))],
                          out_specs=pl.BlockSpec((tr, c), lambda i: (i, 0)), out_shape=_sds((r, c), F32), name=name,
                          compiler_params=_cp())(x)


def _pair_sum(bufs, recvs, cidx):
    n = len(bufs)

    def body(c_ref, *refs):
        for i in range(n):
            refs[2 * n + i][...] = (refs[i][...].astype(F32) + refs[n + i][...].astype(F32)).astype(BF16)

    return pl.pallas_call(
        body,
        grid_spec=pltpu.PrefetchScalarGridSpec(
            num_scalar_prefetch=1, grid=(4,),
            in_specs=[pl.BlockSpec((None, None) + b.shape[2:], lambda s, cref: (s, cref[0], 0, 0)) for b in bufs]
            + [pl.BlockSpec((None,) + r.shape[1:], lambda s, cref: (s, 0, 0)) for r in recvs],
            out_specs=tuple(pl.BlockSpec((None,) + r.shape[1:], lambda s, cref: (s, 0, 0)) for r in recvs)),
        out_shape=tuple(_sds(r.shape, BF16) for r in recvs), name="rs_pair_sum", compiler_params=_cp())(cidx, *bufs, *recvs)


def _adamw_update(w, gv, m, v):
    mn = ADAM_B1 * m + (1.0 - ADAM_B1) * gv
    vn = ADAM_B2 * v + (1.0 - ADAM_B2) * (gv * gv)
    m_hat = mn / (1.0 - ADAM_B1 ** ADAM_STEP)
    v_hat = vn / (1.0 - ADAM_B2 ** ADAM_STEP)
    return -ADAM_LR * (m_hat / (jnp.sqrt(v_hat) + ADAM_EPS) + ADAM_WD * w), mn, vn


def _adamw(w, g, m, v, name):
    r, c = w.shape

    def body(w_ref, g_ref, m_ref, v_ref, d_ref, nm_ref, nv_ref):
        d_ref[...], nm_ref[...], nv_ref[...] = _adamw_update(w_ref[...], g_ref[...], m_ref[...], v_ref[...])

    blk = pl.BlockSpec((r, c), lambda i: (0, 0))
    return pl.pallas_call(body, grid=(1,), in_specs=[blk] * 4, out_specs=(blk,) * 3,
                          out_shape=tuple(_sds((r, c), F32) for _ in range(3)), name=name, compiler_params=_cp())(w, g, m, v)


def _adamw_layer(layer, w, g, m, v, outs, name):
    _, r, c = w.shape
    tr = max(d for d in range(8, r + 1, 8) if r % d == 0 and d * c * 4 <= 2 ** 20)

    def body(w_ref, m_ref, v_ref, g_ref, *refs):
        d_ref, nm_ref, nv_ref, go_ref = refs[4:]
        gv = g_ref[...]
        d_ref[...], nm_ref[...], nv_ref[...] = _adamw_update(w_ref[...], gv, m_ref[...], v_ref[...])
        go_ref[...] = gv

    blk = pl.BlockSpec((None, tr, c), lambda i: (layer, i, 0))
    return pl.pallas_call(
        body, grid=(r // tr,), in_specs=[blk] * 3 + [pl.BlockSpec((tr, c), lambda i: (i, 0))] + [ANY] * 4, out_specs=(blk,) * 4,
        out_shape=tuple(_sds(w.shape, F32) for _ in range(4)), input_output_aliases={4 + j: j for j in range(4)}, name=name,
        compiler_params=_cp())(w, m, v, g, *outs)


def _all_gather(shards, name):
    n = len(shards)

    def body(*refs):
        x_refs, out_refs = refs[:n], refs[n:2 * n]
        send_sems, recv_sems, local_sems = refs[2 * n:]
        x, y, cc = lax.axis_index("x"), lax.axis_index("y"), lax.axis_index("c")
        me, sibling = (x, y, cc), (x, y, 1 - cc)
        chips = [(1 - x, y), (x, 1 - y), (1 - x, 1 - y)]

        def copy(i, k, block, to, own=False):
            px, py, pc = block
            slot = out_refs[i].at[4 * px + 2 * py + pc]
            return pltpu.make_async_remote_copy(
                src_ref=x_refs[i] if own else slot, dst_ref=slot, send_sem=send_sems.at[7 * i + k],
                recv_sem=recv_sems.at[7 * i + k], device_id=to, device_id_type=MESH)

        mine = [pltpu.make_async_copy(x_refs[i], out_refs[i].at[4 * x + 2 * y + cc], local_sems.at[i]) for i in range(n)]
        for cp in mine:
            cp.start()
        first = []
        for j, chip in enumerate(chips):
            first += [copy(i, 1 + j, me, (*chip, cc), own=True) for i in range(n)]
        first += [copy(i, 0, me, sibling, own=True) for i in range(n)]
        for cp in first:
            cp.start()
        passed = []
        for j, chip in enumerate(chips):
            for i in range(n):
                copy(i, 1 + j, (*chip, cc), me).wait_recv()
                cp = copy(i, 4 + j, (*chip, cc), sibling)
                cp.start()
                passed.append(cp)
        for i in range(n):
            copy(i, 0, sibling, me).wait_recv()
        for j, chip in enumerate(chips):
            for i in range(n):
                copy(i, 4 + j, (*chip, 1 - cc), me).wait_recv()
        for cp in first + passed:
            cp.wait_send()
        for cp in mine:
            cp.wait()

    return pl.pallas_call(
        body, out_shape=tuple(_sds((N_DEV,) + s.shape, s.dtype) for s in shards), in_specs=[ANY] * n, out_specs=(ANY,) * n,
        scratch_shapes=[pltpu.SemaphoreType.DMA((7 * n,)), pltpu.SemaphoreType.DMA((7 * n,)), pltpu.SemaphoreType.DMA((n,))],
        name=name)(*shards)


def _rs_core_swap(bufs, name):
    n = len(bufs)

    def body(*refs):
        b_refs, recv_refs = refs[:n], refs[n:2 * n]
        send_sems, recv_sems = refs[2 * n:]
        x, y, cc = lax.axis_index("x"), lax.axis_index("y"), lax.axis_index("c")
        copies = [pltpu.make_async_remote_copy(
            src_ref=b_refs[i].at[s, 1 - cc], dst_ref=recv_refs[i].at[s], send_sem=send_sems.at[4 * i + s],
            recv_sem=recv_sems.at[4 * i + s], device_id=(x, y, 1 - cc), device_id_type=MESH) for i in range(n) for s in range(4)]
        for cp in copies:
            cp.start()
        for cp in copies:
            cp.wait()

    return pl.pallas_call(
        body, out_shape=tuple(_sds((4,) + b.shape[2:], b.dtype) for b in bufs), in_specs=[ANY] * n, out_specs=(ANY,) * n,
        scratch_shapes=[pltpu.SemaphoreType.DMA((4 * n,)), pltpu.SemaphoreType.DMA((4 * n,))], name=name)(*bufs)


HBM = pl.BlockSpec(memory_space=pltpu.HBM)
SEMS = pl.BlockSpec(memory_space=pltpu.SEMAPHORE)
EFFECT = pltpu.SideEffectType.DATAFLOW_SIDE_EFFECTING


def _hbm(a):
    return pltpu.HBM(a.shape, a.dtype)


def _other_chips(x, y):
    return [(1 - x, y), (x, 1 - y), (1 - x, 1 - y)]


def _ici_start(srcs, lands, mode, name, group=None):
    n = len(srcs)

    def body(*refs):
        s_refs, land_refs = refs[:n], refs[n:2 * n]
        send_sems, recv_sems = refs[2 * n], refs[2 * n + 1]
        token = refs[-1]
        x, y, cc = lax.axis_index("x"), lax.axis_index("y"), lax.axis_index("c")
        mine = 2 * x + y if mode == "by_chip" else 4 * x + 2 * y + cc
        peers = [(px, py, cc) for px, py in _other_chips(x, y)]
        if mode == "by_device":
            peers = [(x, y, 1 - cc)] + peers + [(px, py, 1 - cc) for px, py in _other_chips(x, y)]
        size = n if group is None else group
        for first in range(0, n, size):
            for px, py, pc in peers:
                for i in range(first, first + size):
                    src = s_refs[i]
                    if mode == "by_chip":
                        src = src.at[2 * px + py]
                    elif mode == "by_device":
                        src = src.at[4 * px + 2 * py + pc]
                    pltpu.make_async_remote_copy(
                        src_ref=src, dst_ref=land_refs[i].at[mine], send_sem=send_sems.at[i], recv_sem=recv_sems.at[i],
                        device_id=(px, py, pc), device_id_type=MESH).start()
        token[...] = jnp.zeros_like(token)

    out = pl.pallas_call(
        body, name=name,
        out_shape=(pltpu.SemaphoreType.DMA((n,)), pltpu.SemaphoreType.DMA((n,)), *[_hbm(s) for s in srcs], *[_hbm(l) for l in lands],
                   _sds((8, 128), F32)),
        in_specs=[HBM] * (2 * n), out_specs=(SEMS, SEMS, *[HBM] * (2 * n), pl.BlockSpec(memory_space=pltpu.VMEM)),
        input_output_aliases={i: 2 + i for i in range(2 * n)}, compiler_params=pltpu.CompilerParams(has_side_effects=EFFECT),
    )(*[pltpu.with_memory_space_constraint(s, pltpu.HBM) for s in srcs],
      *[pltpu.with_memory_space_constraint(l, pltpu.HBM) for l in lands])
    return out[0], out[1], out[2:2 + n], out[2 + n:2 + 2 * n], out[-1], 7 if mode == "by_device" else 3


def _ici_wait(started, after, name, only=None):
    send_sems, recv_sems, srcs, lands, _, copies = started
    only = list(range(len(srcs))) if only is None else only
    srcs, lands = [srcs[i] for i in only], [lands[i] for i in only]
    n = len(srcs)

    def body(*refs):
        land_refs = refs[n:2 * n]
        send_sems, recv_sems = refs[2 * n], refs[2 * n + 1]
        x, y, cc = lax.axis_index("x"), lax.axis_index("y"), lax.axis_index("c")
        for i in range(n):
            three = land_refs[i].at[pl.ds(0, copies)]
            cp = pltpu.make_async_remote_copy(src_ref=three, dst_ref=three, send_sem=send_sems.at[only[i]],
                                              recv_sem=recv_sems.at[only[i]],
                                              device_id=(x, y, cc), device_id_type=MESH)
            cp.wait_send()
            cp.wait_recv()

    return pl.pallas_call(
        body, name=name, out_shape=tuple(_hbm(l) for l in lands), in_specs=[HBM] * (2 * n) + [SEMS, SEMS, ANY],
        out_specs=tuple([HBM] * n), input_output_aliases={n + i: i for i in range(n)},
        compiler_params=pltpu.CompilerParams(has_side_effects=EFFECT))(*srcs, *lands, send_sems, recv_sems, after)


def _gather_d2d(blocks, lands, name):
    n = len(blocks)

    def body(*refs):
        x_refs, land_refs = refs[:n], refs[2 * n:3 * n]
        send_sems, recv_sems, in_sems, out_sems = refs[3 * n:3 * n + 4]
        stage = refs[3 * n + 4:]
        x, y, cc = lax.axis_index("x"), lax.axis_index("y"), lax.axis_index("c")
        sibling = (x, y, 1 - cc)
        staged = [pltpu.make_async_copy(x_refs[i], stage[i], in_sems.at[i]) for i in range(n)]
        for cp in staged:
            cp.start()
        copies = []
        for i in range(n):
            slot = land_refs[i].at[4 * x + 2 * y + cc]
            copies.append(pltpu.make_async_remote_copy(src_ref=x_refs[i], dst_ref=slot, send_sem=send_sems.at[4 * i],
                                                       recv_sem=recv_sems.at[4 * i], device_id=sibling, device_id_type=MESH))
            for j, (px, py) in enumerate(_other_chips(x, y)):
                slot = land_refs[i].at[4 * px + 2 * py + cc]
                copies.append(pltpu.make_async_remote_copy(src_ref=slot, dst_ref=slot, send_sem=send_sems.at[4 * i + 1 + j],
                                                           recv_sem=recv_sems.at[4 * i + 1 + j], device_id=sibling, device_id_type=MESH))
        for cp in copies:
            cp.start()
        mine = []
        for i in range(n):
            staged[i].wait()
            mine.append(pltpu.make_async_copy(stage[i], land_refs[i].at[4 * x + 2 * y + cc], out_sems.at[i]))
            mine[i].start()
        for i in range(n):
            slot = land_refs[i].at[4 * x + 2 * y + (1 - cc)]
            pltpu.make_async_remote_copy(src_ref=slot, dst_ref=slot, send_sem=send_sems.at[4 * i], recv_sem=recv_sems.at[4 * i],
                                         device_id=sibling, device_id_type=MESH).wait_recv()
            for j, (px, py) in enumerate(_other_chips(x, y)):
                slot = land_refs[i].at[4 * px + 2 * py + (1 - cc)]
                pltpu.make_async_remote_copy(src_ref=slot, dst_ref=slot, send_sem=send_sems.at[4 * i + 1 + j],
                                             recv_sem=recv_sems.at[4 * i + 1 + j], device_id=sibling, device_id_type=MESH).wait_recv()
        for cp in copies:
            cp.wait_send()
        for cp in mine:
            cp.wait()

    return pl.pallas_call(
        body, out_shape=tuple(_sds(l.shape, l.dtype) for l in lands), in_specs=[ANY] * (2 * n), out_specs=(ANY,) * n,
        input_output_aliases={n + i: i for i in range(n)},
        scratch_shapes=[pltpu.SemaphoreType.DMA((4 * n,)), pltpu.SemaphoreType.DMA((4 * n,)), pltpu.SemaphoreType.DMA((n,)),
                        pltpu.SemaphoreType.DMA((n,))] + [pltpu.VMEM(b.shape, b.dtype) for b in blocks],
        name=name, compiler_params=_cp())(*blocks, *lands)


def _sum_own(parts, recvs, mine, name):
    n = len(parts)

    def body(c_ref, *refs):
        s = pl.program_id(0)
        for i in range(n):
            val = jnp.where(c_ref[0] == s, refs[i][...], refs[n + i][...]).astype(F32)
            _acc(refs[2 * n + i], val, s == 0)

    kept = [pl.BlockSpec((None,) + p.shape[1:], lambda s, cref: (cref[0], 0, 0)) for p in parts]
    ins = [pl.BlockSpec((None,) + p.shape[1:], lambda s, cref: (s, 0, 0)) for p in parts]
    return pl.pallas_call(
        body, grid_spec=pltpu.PrefetchScalarGridSpec(
            num_scalar_prefetch=1, grid=(parts[0].shape[0],), in_specs=kept + ins,
            out_specs=tuple(pl.BlockSpec(p.shape[1:], lambda s, cref: (0, 0)) for p in parts)),
        out_shape=tuple(_sds(p.shape[1:], F32) for p in parts), name=name, compiler_params=_cp())(mine, *parts, *recvs)


BIG = (("w_in", True), ("w_gate", True), ("w_mem_kv", False), ("w_branch", True), ("w_out", False), ("w_ffn_in", True),
       ("w_ffn_out", False))

SMALL = ("norm_mix_g", "norm_mem_g", "ret_decay_fwd", "ret_decay_bwd", "ret_norm_g", "pool_w", "pool_scale", "na_q_norm_g",
         "na_k_norm_g", "na_rpb", "mem_q_norm_g", "mem_k_norm_g", "norm_ffn_g")
WEIGHTS = ("norm_mix_g", "norm_mem_g", "w_in", "w_gate", "ret_decay_fwd", "ret_decay_bwd", "ret_norm_g", "pool_w", "pool_scale",
           "na_q_norm_g", "na_k_norm_g", "na_rpb", "mem_q_norm_g", "mem_k_norm_g", "w_mem_kv", "w_branch", "w_out", "norm_ffn_g",
           "w_ffn_in", "w_ffn_out")


def _to_exchange(name, transposed, shard):
    if name == "w_branch":
        return jnp.swapaxes(shard, 1, 2).reshape(NH * (D // N_DEV), BW)
    return shard.T if transposed else shard


def _from_exchange(name, transposed, block):
    if name == "w_branch":
        return jnp.swapaxes(block.reshape(NH, D // N_DEV, BW), 1, 2)
    return block.T if transposed else block


def _whole_from_gathered(name, g):
    if name == "w_branch":
        return jnp.swapaxes(g.reshape(N_DEV, NH, D // N_DEV, BW), 0, 1).reshape(NH, D, BW)
    return g.reshape(N_DEV * g.shape[1], g.shape[2])


def _by_destination(name, g):
    if name == "w_branch":
        g = jnp.swapaxes(g.reshape(NH, N_DEV, D // N_DEV, BW), 0, 1).reshape(N_DEV * NH * (D // N_DEV), BW)
    return g.reshape(4, 2, g.shape[0] // N_DEV, g.shape[1])


SMALL_PAD = 1024


def _pack_small(vals, loss=None):
    parts = [vals[n] for n in SMALL] + [jnp.zeros((1,), F32) if loss is None else loss.reshape(1)]
    rows = []
    for p in parts:
        flat = p.reshape(-1)
        rows.append(jnp.pad(flat, (0, -flat.shape[0] % SMALL_PAD)).reshape(-1, 128))
    return jnp.concatenate(rows, axis=0)


def _unpack_small(packed, like):
    out, off = {}, 0
    for n in SMALL:
        sz = int(np.prod(like[n].shape))
        nrow = -(-sz // SMALL_PAD) * (SMALL_PAD // 128)
        out[n] = packed[off:off + nrow].reshape(-1)[:sz].reshape(like[n].shape)
        off += nrow
    return out, packed[off, 0]


def _na_constants():
    c = np.arange(GRID_W)
    win = np.clip(c - NA_COLS_WIN // 2, 0, GRID_W - NA_COLS_WIN)
    kc = np.arange(GRID_W)
    inside = (kc[None, :] >= win[:, None]) & (kc[None, :] < win[:, None] + NA_COLS_WIN)
    off = kc[None, :] - c[:, None] + NA_COLS_WIN - 1
    onehot = np.zeros((128, GRID_W, GRID_W), np.float32)
    for b in range(2 * NA_COLS_WIN - 1):
        onehot[b] = (off == b) & inside
    maskadd = np.where(inside, 0.0, NEG).astype(np.float32)
    return onehot.reshape(128, GRID_W * GRID_W), maskadd


def _na_bias_table(tab, maskadd):
    n_off = 2 * NA_ROWS_WIN - 1
    t4 = tab[:NH * n_off].reshape(NH, n_off, GRID_W, GRID_W) + maskadd[None, None]
    by_off = t4.transpose(1, 0, 2, 3).reshape(n_off, NH * GRID_W, GRID_W)
    return jnp.concatenate([by_off[:-1], by_off[1:]], axis=-1)


def _rotary_tables(t):
    half = HD // 2
    inv = ROPE_THETA ** (-jnp.arange(half, dtype=F32) / half)
    ang = jnp.arange(t, dtype=F32)[:, None] * inv[None, :]
    cos, sin = jnp.cos(ang), jnp.sin(ang)
    return jnp.tile(jnp.concatenate([cos, cos], axis=-1), (1, NH)), jnp.tile(jnp.concatenate([-sin, sin], axis=-1), (1, NH))


def _block_diag(pw):
    out = jnp.zeros((BW, BW), pw.dtype)
    for g in range(NH):
        out = lax.dynamic_update_slice(out, pw[g], (g * HD, g * HD))
    return out


def _tile4(g):
    return jnp.tile(g.reshape(1, HD), (1, NH))


def _layer_fwd(x, mem, sw, lw, consts, fetch, h=None, next_norm_g=None):
    cos2, sin2, onehot, maskadd = consts
    if h is None:
        h = _rmsnorm_fwd(x, sw["norm_mix_g"].reshape(1, D), "norm_mix_fwd")
    proj = _mm(h, lw["w_in"], tb=True, name="mm_in")
    gp = _mm(h, lw["w_gate"], tb=True, out_dtype=BF16, name="mm_gate")
    g_naq, g_nak, g_mq = _tile4(sw["na_q_norm_g"]), _tile4(sw["na_k_norm_g"]), _tile4(sw["mem_q_norm_g"])
    rq, rk, rv, nq, nk, nv, mq = _prep_fwd(proj, cos2, sin2, g_naq, g_nak, g_mq)

    lgf, lgb = jax.nn.log_sigmoid(sw["ret_decay_fwd"]), jax.nn.log_sigmoid(sw["ret_decay_bwd"])
    g_ret = sw["ret_norm_g"].reshape(1, BW)
    o_ret, ret = _ret_fwd(rq, rk, rv, proj, lgf, lgb, g_ret)

    wbd = _block_diag(sw["pool_w"]).astype(BF16)
    p_scale = sw["pool_scale"].reshape(1, BW)
    pool = _pool_fwd(proj, wbd, p_scale)

    rpb_pad = jnp.pad(sw["na_rpb"].reshape(NH * 15, 31), ((0, 4), (0, 97)))
    ball = _na_bias_table(_rpb_expand(rpb_pad, onehot), maskadd)
    na = _na_fwd(nq, nk, nv, ball)

    lw.update(fetch(1, na))
    memn = _rmsnorm_fwd(mem, sw["norm_mem_g"].reshape(1, D), "norm_mem_fwd")
    kv = _mm(memn, lw["w_mem_kv"], name="mm_memkv")
    g_mk = _tile4(sw["mem_k_norm_g"])
    mk, mv = _memkv_prep(kv, g_mk)
    mo = _mem_fwd(mq, mk, mv)

    br = (ret, pool, na, mo)
    merged = _merge_fwd(br, lw["w_branch"], gp)
    x1, h2 = _mm(merged, lw["w_out"], add=x, norm_g=sw["norm_ffn_g"].reshape(1, D), name="mm_out")
    lw.update(fetch(2, x1))
    ffa, ffg, yff = _ffn_in_fwd(h2, lw["w_ffn_in"])
    if next_norm_g is None:
        x2, h_next = _mm(yff, lw["w_ffn_out"], add=x1, name="mm_ffn_out"), None
    else:
        x2, h_next = _mm(yff, lw["w_ffn_out"], add=x1, norm_g=next_norm_g.reshape(1, D), name="mm_ffn_out")
    saved = dict(x=x, h=h, proj=proj, gp=gp, rq=rq, rk=rk, rv=rv, nq=nq, nk=nk, nv=nv, mq=mq, o_ret=o_ret, ball=ball, memn=memn,
                 kv=kv, mk=mk, mv=mv, br=br, merged=merged, x1=x1, h2=h2, ffa=ffa, ffg=ffg, yff=yff, lgf=lgf, lgb=lgb, wbd=wbd)
    return x2, h_next, saved


def _layer_bwd(dx2, dx2b, mem, sw, lw, sv, consts, dep=None):
    cos2, sin2, onehot, maskadd = consts
    gb, gs = {}, {}
    d_a, d_g = _ffn_out_bwd(dx2b, lw["w_ffn_out"], sv["ffa"], sv["ffg"], dep)
    gb["w_ffn_out"] = _mm(sv["yff"], dx2b, ta=True, out_dtype=BF16, name="mm_ffn_out_dw")
    dh2 = _mm(d_a, lw["w_ffn_in"], b_half=0, name="mm_ffn_in_dx_a")
    dx1, dx1b, dg = _mm_norm_bwd(d_g, lw["w_ffn_in"], dh2, sv["x1"], sw["norm_ffn_g"].reshape(1, D), dx2, b_half=1,
                                 name="mm_ffn_in_dx_g")
    gs["norm_ffn_g"] = dg.reshape(D)
    dw_a = _mm(d_a, sv["h2"], ta=True, out_dtype=BF16, out_half=(0, None), name="mm_ffn_in_dw_a")
    gb["w_ffn_in"] = _mm(d_g, sv["h2"], ta=True, out_dtype=BF16, out_half=(1, dw_a), name="mm_ffn_in_dw_g")

    dmerged = _mm(dx1b, lw["w_out"], tb=True, name="mm_out_dx")
    gb["w_out"] = _mm(sv["merged"], dx1b, ta=True, out_dtype=BF16, name="mm_out_dw")
    dgp, dbr, gb["w_branch"] = _merge_bwd(dmerged, sv["br"], lw["w_branch"], sv["gp"])

    g_ret = sw["ret_norm_g"].reshape(1, BW)
    do_ret, d_rg, dg_ret = _ret_post_bwd(dbr, sv["o_ret"], sv["proj"], g_ret)
    d_rq, d_rk, d_rv, dlg = _ret_bwd(do_ret, sv["rq"], sv["rk"], sv["rv"], sv["lgf"], sv["lgb"])
    gs["ret_norm_g"] = dg_ret.reshape(BW)
    _, vjp_f = jax.vjp(jax.nn.log_sigmoid, sw["ret_decay_fwd"])
    _, vjp_b = jax.vjp(jax.nn.log_sigmoid, sw["ret_decay_bwd"])
    gs["ret_decay_fwd"] = vjp_f(dlg[0:NH, 0])[0]
    gs["ret_decay_bwd"] = vjp_b(dlg[NH:2 * NH, 0])[0]

    p_scale = sw["pool_scale"].reshape(1, BW)
    d_pv, dwbd, dscale = _pool_bwd(dbr, sv["proj"], sv["wbd"], p_scale)
    gs["pool_w"] = jnp.stack([dwbd[g * HD:(g + 1) * HD, g * HD:(g + 1) * HD] for g in range(NH)])
    gs["pool_scale"] = dscale.reshape(BW)

    d_nq, d_nk, d_nv, dball = _na_bwd(dbr, sv["nq"], sv["nk"], sv["nv"], sv["ball"])
    _, vjp_tab = jax.vjp(lambda tab: _na_bias_table(tab, maskadd), jnp.zeros((64, GRID_W * GRID_W), F32))
    drpb = _rpb_reduce(vjp_tab(dball)[0], onehot)
    gs["na_rpb"] = drpb[:NH * 15, :31].reshape(NH, 15, 31)

    d_mq, d_mk, d_mv = _mem_bwd(dbr, sv["mq"], sv["mk"], sv["mv"])
    g_mk = _tile4(sw["mem_k_norm_g"])
    dkv, dg_mk = _memkv_bwd(sv["kv"], d_mk, d_mv, g_mk)
    gs["mem_k_norm_g"] = dg_mk.reshape(NH, HD).sum(0)
    gb["w_mem_kv"] = _mm(sv["memn"], dkv, ta=True, out_dtype=BF16, name="mm_memkv_dw")
    dmemn = _mm(dkv, lw["w_mem_kv"], tb=True, name="mm_memkv_dx")
    _, _, dg_mem = _rmsnorm_bwd(dmemn, mem, sw["norm_mem_g"].reshape(1, D), jnp.zeros_like(mem), "norm_mem_bwd")
    gs["norm_mem_g"] = dg_mem.reshape(D)

    g_naq, g_nak, g_mq = _tile4(sw["na_q_norm_g"]), _tile4(sw["na_k_norm_g"]), _tile4(sw["mem_q_norm_g"])
    dproj, dg_naq, dg_nak, dg_mq = _prep_bwd(sv["proj"], cos2, sin2, g_naq, g_nak, g_mq, d_rq, d_rk, d_rv, d_rg, d_pv, d_nq, d_nk,
                                             d_nv, d_mq)
    gs["na_q_norm_g"] = dg_naq.reshape(NH, HD).sum(0)
    gs["na_k_norm_g"] = dg_nak.reshape(NH, HD).sum(0)
    gs["mem_q_norm_g"] = dg_mq.reshape(NH, HD).sum(0)

    gb["w_in"] = _mm(dproj, sv["h"], ta=True, out_dtype=BF16, name="mm_in_dw")
    gb["w_gate"] = _mm(dgp, sv["h"], ta=True, out_dtype=BF16, name="mm_gate_dw")
    dh = _mm(dproj, lw["w_in"], name="mm_in_dx")
    dx, dxb, dg = _mm_norm_bwd(dgp, lw["w_gate"], dh, sv["x"], sw["norm_mix_g"].reshape(1, D), dx1, name="mm_gate_dx")
    gs["norm_mix_g"] = dg.reshape(D)
    return dx, dxb, gb, gs


def _local_step(x, mem, target, small, get_layer, on_grads):
    t = x.shape[0]
    cos2, sin2 = _rotary_tables(t)
    onehot, maskadd = _na_constants()
    consts = (cos2, sin2, jnp.asarray(onehot), jnp.asarray(maskadd))
    saved, weights, cur, h = [], [], x, None
    for l in range(DEPTH):
        sw = {n: small[n][l] for n in SMALL}
        lw, fetch = get_layer(l, cur)
        weights.append(lw)
        cur, h, sv = _layer_fwd(cur, mem, sw, lw, consts, fetch, h, small["norm_mix_g"][l + 1] if l + 1 < DEPTH else None)
        saved.append(sv)
    dy, dyb, loss_tile = _loss_head(cur, target)
    small_g = {n: [None] * DEPTH for n in SMALL}
    dep = None
    for l in reversed(range(DEPTH)):
        sw = {n: small[n][l] for n in SMALL}
        dy, dyb, gb, gs = _layer_bwd(dy, dyb, mem, sw, weights[l], saved[l], consts, dep)
        dep = on_grads(l, gb, dy)
        for n in SMALL:
            small_g[n][l] = gs[n]
    return loss_tile[0, 0], dy, {n: jnp.stack(v) for n, v in small_g.items()}


def _flat2d(a):
    return a.reshape(-1, a.shape[-1])


def kernel(x, mem, norm_mix_g, norm_mem_g, w_in, w_gate, ret_decay_fwd, ret_decay_bwd, ret_norm_g, pool_w, pool_scale, na_q_norm_g, na_k_norm_g, na_rpb, mem_q_norm_g, mem_k_norm_g, w_mem_kv, w_branch, w_out, norm_ffn_g, w_ffn_in, w_ffn_out, loss_target, m_norm_mix_g, m_norm_mem_g, m_w_in, m_w_gate, m_ret_decay_fwd, m_ret_decay_bwd, m_ret_norm_g, m_pool_w, m_pool_scale, m_na_q_norm_g, m_na_k_norm_g, m_na_rpb, m_mem_q_norm_g, m_mem_k_norm_g, m_w_mem_kv, m_w_branch, m_w_out, m_norm_ffn_g, m_w_ffn_in, m_w_ffn_out, v_norm_mix_g, v_norm_mem_g, v_w_in, v_w_gate, v_ret_decay_fwd, v_ret_decay_bwd, v_ret_norm_g, v_pool_w, v_pool_scale, v_na_q_norm_g, v_na_k_norm_g, v_na_rpb, v_mem_q_norm_g, v_mem_k_norm_g, v_w_mem_kv, v_w_branch, v_w_out, v_norm_ffn_g, v_w_ffn_in, v_w_ffn_out):
    w = dict(norm_mix_g=norm_mix_g, norm_mem_g=norm_mem_g, w_in=w_in, w_gate=w_gate, ret_decay_fwd=ret_decay_fwd,
             ret_decay_bwd=ret_decay_bwd, ret_norm_g=ret_norm_g, pool_w=pool_w, pool_scale=pool_scale, na_q_norm_g=na_q_norm_g,
             na_k_norm_g=na_k_norm_g, na_rpb=na_rpb, mem_q_norm_g=mem_q_norm_g, mem_k_norm_g=mem_k_norm_g, w_mem_kv=w_mem_kv,
             w_branch=w_branch, w_out=w_out, norm_ffn_g=norm_ffn_g, w_ffn_in=w_ffn_in, w_ffn_out=w_ffn_out)
    m = dict(norm_mix_g=m_norm_mix_g, norm_mem_g=m_norm_mem_g, w_in=m_w_in, w_gate=m_w_gate, ret_decay_fwd=m_ret_decay_fwd,
             ret_decay_bwd=m_ret_decay_bwd, ret_norm_g=m_ret_norm_g, pool_w=m_pool_w, pool_scale=m_pool_scale, na_q_norm_g=m_na_q_norm_g,
             na_k_norm_g=m_na_k_norm_g, na_rpb=m_na_rpb, mem_q_norm_g=m_mem_q_norm_g, mem_k_norm_g=m_mem_k_norm_g, w_mem_kv=m_w_mem_kv,
             w_branch=m_w_branch, w_out=m_w_out, norm_ffn_g=m_norm_ffn_g, w_ffn_in=m_w_ffn_in, w_ffn_out=m_w_ffn_out)
    v = dict(norm_mix_g=v_norm_mix_g, norm_mem_g=v_norm_mem_g, w_in=v_w_in, w_gate=v_w_gate, ret_decay_fwd=v_ret_decay_fwd,
             ret_decay_bwd=v_ret_decay_bwd, ret_norm_g=v_ret_norm_g, pool_w=v_pool_w, pool_scale=v_pool_scale, na_q_norm_g=v_na_q_norm_g,
             na_k_norm_g=v_na_k_norm_g, na_rpb=v_na_rpb, mem_q_norm_g=v_mem_q_norm_g, mem_k_norm_g=v_mem_k_norm_g, w_mem_kv=v_w_mem_kv,
             w_branch=v_w_branch, w_out=v_w_out, norm_ffn_g=v_norm_ffn_g, w_ffn_in=v_w_ffn_in, w_ffn_out=v_w_ffn_out)
    assert x.shape == (1, 2048, D) and mem.shape == (1, N_MEM, D) and w_in.shape == (DEPTH, D, 9 * BW // N_DEV)

    blocks = [_to_exchange(name, tr, w[name][l]).astype(BF16) for l in range(DEPTH) for name, tr in BIG]
    started = _ici_start(blocks, [lax.empty((N_DEV,) + b.shape, BF16) for b in blocks], "gather", "gather_ici_start", len(BIG))

    def get_group(l, only, after, tag):
        at = [l * len(BIG) + i for i in only]
        lands = _ici_wait(started, after, "gather_ici_wait_%d%s" % (l, tag), at)
        whole = _gather_d2d([started[2][i] for i in at], lands, "gather_d2d")
        return {BIG[i][0]: _whole_from_gathered(BIG[i][0], g) for i, g in zip(only, whole)}

    def get_layer(l, after):
        if l > 0:
            return get_group(l, list(range(len(BIG))), after, ""), lambda stage, after2: {}
        groups = [[0, 1], [2, 3, 4], [5, 6]]
        return get_group(l, groups[0], started[4], "a"), lambda stage, after2: get_group(l, groups[stage], after2, "abc"[stage])

    cidx = lax.axis_index("c").astype(jnp.int32).reshape(1)
    chip = (2 * lax.axis_index("x") + lax.axis_index("y")).astype(jnp.int32).reshape(1)
    in_flight = []

    def flip_of(name, tr):
        return (lambda a: jnp.swapaxes(a, 1, 2)) if name in ("w_in", "w_ffn_in") else (lambda a: a)

    def rows3(a):
        return a.reshape(DEPTH, -1, a.shape[-1])

    opt_in = {name: tuple(rows3(flip_of(name, tr)(t[name])) for t in (w, m, v)) for name, tr in BIG}
    opt_out = {name: tuple(lax.empty(opt_in[name][0].shape, F32) for _ in range(4)) for name, _ in BIG}

    device = (2 * chip + cidx).astype(jnp.int32)

    def finish(l, st, after):
        recv = _ici_wait(st, after, "rs_ici_wait_%d" % l)
        sums = _sum_own(st[2], recv, chip if st[5] == 3 else device, "rs_sum")
        for (name, tr), s in zip(BIG, sums):
            g = s if name in ("w_in", "w_ffn_in") else _from_exchange(name, tr, s)
            wx, mx, vx = opt_in[name]
            opt_out[name] = _adamw_layer(l, wx, g.reshape(-1, g.shape[-1]), mx, vx, opt_out[name], "adamw_" + name)

    def on_grads(l, gb, after):
        send = [_by_destination(name, gb[name]) for name, _ in BIG]
        if l > 0:
            send = [s.reshape((N_DEV,) + s.shape[2:]) for s in send]
            st = _ici_start(send, [lax.empty(s.shape, BF16) for s in send], "by_device", "rs_ici_start_%d" % l)
        else:
            from_core = _rs_core_swap(send, "rs_core_swap")
            chip_part = _pair_sum(send, from_core, cidx)
            st = _ici_start(chip_part, [lax.empty(p.shape, BF16) for p in chip_part], "by_chip", "rs_ici_start_%d" % l)
        in_flight.append((l, st))
        return st[4]

    loss_local, dx, small_g = _local_step(x[0], mem[0], loss_target[0], {n: w[n] for n in SMALL}, get_layer, on_grads)

    last_started = in_flight[-1][1][4]
    for l, st in in_flight[:-1]:
        finish(l, st, last_started)

    small_all, = _all_gather([_pack_small(small_g, loss_local) + last_started[0:1]], "gather_small")
    packed_g = _sum_slots(small_all, "small_sum")
    small_sum, loss = _unpack_small(packed_g, {n: w[n] for n in SMALL})
    d_, m_, v_ = _adamw(_pack_small({n: w[n] for n in SMALL}), packed_g, _pack_small({n: m[n] for n in SMALL}),
                        _pack_small({n: v[n] for n in SMALL}), "adamw_small")
    updated = d_[0:8]
    for name, _ in BIG:
        updated = updated + opt_out[name][0][1, 0:8, 0:128]
    finish(*in_flight[-1], updated)

    grads, delta, new_m, new_v = {}, {}, {}, {}
    for name, tr in BIG:
        shape = flip_of(name, tr)(w[name]).shape
        delta[name], new_m[name], new_v[name], grads[name] = (flip_of(name, tr)(a.reshape(shape)) for a in opt_out[name])
    like = {n: w[n] for n in SMALL}
    ds, _ = _unpack_small(d_, like)
    ms, _ = _unpack_small(m_, like)
    vs, _ = _unpack_small(v_, like)
    for n in SMALL:
        grads[n], delta[n], new_m[n], new_v[n] = small_sum[n], ds[n], ms[n], vs[n]

    return (loss, dx[None], *[grads[n] for n in WEIGHTS], *[delta[n] for n in WEIGHTS], *[new_m[n] for n in WEIGHTS],
            *[new_v[n] for n in WEIGHTS])
```

```python
import functools

import numpy as np
import jax
import jax.numpy as jnp
from jax import lax
from jax.experimental import pallas as pl
from jax.experimental.pallas import tpu as pltpu

F32 = jnp.float32
BF16 = jnp.bfloat16
MXU = jnp.bfloat16
HI = lax.Precision.HIGHEST

DEPTH = 4
D = 1024
BW = 256
HD = 64
NH = 4
GRID_W = 64
NA_ROWS_WIN = 8
NA_COLS_WIN = 16
N_MEM = 256
FF = 2816
EPS = 1e-6
NEG = -1e30
ROPE_THETA = 10000.0
POOL_HALF_MAX = 8

ADAM_LR, ADAM_B1, ADAM_B2, ADAM_EPS, ADAM_WD, ADAM_STEP = 0.001, 0.9, 0.999, 1e-08, 0.01, 10

N_DEV = 8
VMEM_LIMIT = 56 * 1024 * 1024
MM_VMEM_BUDGET = 40 * 1024 * 1024

RQ, RK, RV, RG, PV, NQ, NK, NV, MQ = range(9)

MESH = pl.DeviceIdType.MESH
ANY = pl.BlockSpec(memory_space=pl.ANY)
SMEM = pl.BlockSpec(memory_space=pltpu.SMEM)


def _cp(**kw):
    return pltpu.CompilerParams(vmem_limit_bytes=VMEM_LIMIT, **kw)


def _tile(n, cap):
    if n <= cap:
        return n
    best = None
    for t in range(128, cap + 1, 128):
        if n % t == 0:
            best = t
    assert best is not None, (n, cap)
    return best


def _sds(shape, dtype):
    return jax.ShapeDtypeStruct(shape, dtype)


def _lane_head(shape):
    return lax.shift_right_logical(lax.broadcasted_iota(jnp.int32, shape, len(shape) - 1), 6)


def _group_mean(z):
    i = lax.shift_right_logical(lax.broadcasted_iota(jnp.int32, (BW, BW), 0), 6)
    j = lax.shift_right_logical(lax.broadcasted_iota(jnp.int32, (BW, BW), 1), 6)
    g = jnp.where(i == j, 1.0 / HD, 0.0).astype(BF16)
    z_hi = z.astype(BF16)
    z_lo = (z - z_hi.astype(F32)).astype(BF16)
    return jnp.dot(z_hi, g, preferred_element_type=F32) + jnp.dot(z_lo, g, preferred_element_type=F32)


def _gnorm(t, g):
    r = lax.rsqrt(_group_mean(t * t) + EPS)
    return t * r * g


def _gnorm_bwd(dy, t, g):
    r = lax.rsqrt(_group_mean(t * t) + EPS)
    th = t * r
    dth = dy * g
    dt = r * (dth - th * _group_mean(dth * th))
    return dt, dy * th


def _swap_halves(t):
    lane = lax.broadcasted_iota(jnp.int32, t.shape, 1)
    return jnp.where((lane & 63) < 32, pltpu.roll(t, BW - 32, 1), pltpu.roll(t, 32, 1))


def _sigmoid(x):
    return 1.0 / (1.0 + jnp.exp(-x))


def _dot(a, b, ta=False, tb=False):
    return lax.dot_general(a.astype(MXU), b.astype(MXU), (((0 if ta else 1,), (1 if tb else 0,)), ((), ())),
                           preferred_element_type=F32)


def _stack_heads(t):
    head = _lane_head(t.shape)
    return jnp.concatenate([jnp.where(head == h, t, jnp.zeros_like(t)) for h in range(NH)], axis=0)


def _unstack_heads(t, rows):
    head = _lane_head((rows, BW))
    out = jnp.zeros((rows, BW), F32)
    for h in range(NH):
        out = out + jnp.where(head == h, t[h * rows:(h + 1) * rows], 0.0)
    return out


def _softmax_rows(s):
    m = jnp.max(s, axis=-1, keepdims=True)
    e = jnp.exp(s - m)
    return e / jnp.sum(e, axis=-1, keepdims=True)


def _acc(ref, val, first):
    @pl.when(first)
    def _():
        ref[...] = val

    @pl.when(jnp.logical_not(first))
    def _():
        ref[...] += val


def _mm(a, b, *, ta=False, tb=False, out_dtype=F32, add=None, dep=None, b_half=None, out_half=None, norm_g=None, name):
    m, k = (a.shape[1], a.shape[0]) if ta else a.shape
    n = b.shape[0] if tb else b.shape[1]
    assert b_half is None or (not tb and b.shape[0] == 2 * k)
    tm, tn = _tile(m, 1408), (n if norm_g is not None else _tile(n, 768))
    if not ta and m <= 2048:
        blocks = (m * k * a.dtype.itemsize + k * tn * b.dtype.itemsize + m * tn * jnp.dtype(out_dtype).itemsize
                  + (m * tn * 4 if add is not None else 0) + (m * tn * 2 if norm_g is not None else 0))
        if 2 * blocks <= MM_VMEM_BUDGET:
            tm = m
    n_in = 2 + (add is not None) + (dep is not None) + (out_half is not None) + (norm_g is not None)

    def body(*refs):
        a_ref, b_ref, o_ref = refs[0], refs[1], refs[n_in]
        r = _dot(a_ref[...], b_ref[...], ta, tb)
        if add is not None:
            r = r + refs[2][...]
        o_ref[...] = r.astype(out_dtype)
        if norm_g is not None:
            scale = lax.rsqrt(jnp.mean(r * r, axis=-1, keepdims=True) + EPS)
            refs[n_in + 1][...] = (r * scale * refs[n_in - 1][...]).astype(BF16)

    kb = 0 if b_half is None else b_half
    a_spec = pl.BlockSpec((k, tm), lambda i, j: (0, i)) if ta else pl.BlockSpec((tm, k), lambda i, j: (i, 0))
    b_spec = pl.BlockSpec((tn, k), lambda i, j: (j, 0)) if tb else pl.BlockSpec((k, tn), lambda i, j: (kb, j))
    plain = pl.BlockSpec((tm, tn), lambda i, j: (i, j))
    ins, args = [a_spec, b_spec], [a, b]
    if add is not None:
        ins.append(plain)
        args.append(add)
    if dep is not None:
        ins.append(pl.BlockSpec((8, 128), lambda i, j: (0, 0)))
        args.append(dep)
    o_spec, o_shape, aliases = plain, _sds((m, n), out_dtype), {}
    if out_half is not None:
        half, prev = out_half
        o_spec = pl.BlockSpec((tm, tn), lambda i, j: (i + half * (m // tm), j))
        o_shape = _sds((2 * m, n), out_dtype)
        ins.append(ANY)
        args.append(lax.empty((2 * m, n), out_dtype) if prev is None else prev)
        aliases = {len(args) - 1: 0}
    if norm_g is not None:
        ins.append(pl.BlockSpec((1, n), lambda i, j: (0, 0)))
        args.append(norm_g)
        o_spec, o_shape = (o_spec, plain), (o_shape, _sds((m, n), BF16))
    return pl.pallas_call(
        body, grid=(m // tm, n // tn), in_specs=ins, out_specs=o_spec, out_shape=o_shape, input_output_aliases=aliases, name=name,
        compiler_params=_cp(dimension_semantics=("parallel", "parallel")))(*args)


def _mm_norm_bwd(a, b, add, x, g, res, *, b_half=None, name):
    m, k = a.shape
    n = b.shape[1]
    tm = 512
    kb = 0 if b_half is None else b_half

    def body(a_ref, b_ref, c_ref, x_ref, g_ref, res_ref, dx_ref, dxb_ref, dg_ref):
        dhv = _dot(a_ref[...], b_ref[...]) + c_ref[...]
        xv = x_ref[...]
        r = lax.rsqrt(jnp.mean(xv * xv, axis=-1, keepdims=True) + EPS)
        xh = xv * r
        dxh = dhv * g_ref[...]
        dx = res_ref[...] + r * (dxh - xh * jnp.mean(dxh * xh, axis=-1, keepdims=True))
        dx_ref[...] = dx
        dxb_ref[...] = dx.astype(BF16)
        _acc(dg_ref, jnp.sum(dhv * xh, axis=0, keepdims=True), pl.program_id(0) == 0)

    row = pl.BlockSpec((tm, n), lambda i: (i, 0))
    vec = pl.BlockSpec((1, n), lambda i: (0, 0))
    return pl.pallas_call(
        body, grid=(m // tm,),
        in_specs=[pl.BlockSpec((tm, k), lambda i: (i, 0)), pl.BlockSpec((k, n), lambda i: (kb, 0)), row, row, vec, row],
        out_specs=(row, row, vec), out_shape=(_sds((m, n), F32), _sds((m, n), BF16), _sds((1, n), F32)), name=name,
        compiler_params=_cp())(a, b, add, x, g, res)


def _rmsnorm_fwd(x, g, name):
    t, d = x.shape
    tm = _tile(t, 256)

    def body(x_ref, g_ref, o_ref):
        xv = x_ref[...]
        r = lax.rsqrt(jnp.mean(xv * xv, axis=-1, keepdims=True) + EPS)
        o_ref[...] = (xv * r * g_ref[...]).astype(o_ref.dtype)

    return pl.pallas_call(
        body, grid=(t // tm,), in_specs=[pl.BlockSpec((tm, d), lambda i: (i, 0)), pl.BlockSpec((1, d), lambda i: (0, 0))],
        out_specs=pl.BlockSpec((tm, d), lambda i: (i, 0)), out_shape=_sds((t, d), BF16), name=name, compiler_params=_cp())(x, g)


def _rmsnorm_bwd(dh, x, g, res, name):
    t, d = x.shape
    tm = _tile(t, 256)

    def body(dh_ref, x_ref, g_ref, res_ref, dx_ref, dxb_ref, dg_ref):
        xv = x_ref[...]
        dhv = dh_ref[...]
        r = lax.rsqrt(jnp.mean(xv * xv, axis=-1, keepdims=True) + EPS)
        xh = xv * r
        dxh = dhv * g_ref[...]
        dx = res_ref[...] + r * (dxh - xh * jnp.mean(dxh * xh, axis=-1, keepdims=True))
        dx_ref[...] = dx
        dxb_ref[...] = dx.astype(BF16)
        _acc(dg_ref, jnp.sum(dhv * xh, axis=0, keepdims=True), pl.program_id(0) == 0)

    row = pl.BlockSpec((tm, d), lambda i: (i, 0))
    vec = pl.BlockSpec((1, d), lambda i: (0, 0))
    return pl.pallas_call(
        body, grid=(t // tm,), in_specs=[row, row, vec, row], out_specs=(row, row, vec),
        out_shape=(_sds((t, d), F32), _sds((t, d), BF16), _sds((1, d), F32)), name=name, compiler_params=_cp())(dh, x, g, res)


def _prep_fwd(proj, cos2, sin2, g_naq, g_nak, g_mq):
    t = proj.shape[0]
    tm = 256

    def body(p_ref, cos_ref, sin_ref, gq_ref, gk_ref, gm_ref, rq_ref, rk_ref, rv_ref, nq_ref, nk_ref, nv_ref, mq_ref):
        def col(c):
            return p_ref[:, c * BW:(c + 1) * BW]

        cosv, sinv = cos_ref[...], sin_ref[...]

        def rot(tv):
            return tv * cosv + _swap_halves(tv) * sinv

        rq_ref[...] = (rot(col(RQ)) * (HD ** -0.5)).astype(BF16)
        rk_ref[...] = rot(col(RK)).astype(BF16)
        rv_ref[...] = col(RV).astype(BF16)
        nq_ref[...] = _gnorm(col(NQ), gq_ref[...]).astype(BF16)
        nk_ref[...] = _gnorm(col(NK), gk_ref[...]).astype(BF16)
        nv_ref[...] = col(NV).astype(BF16)
        mq_ref[...] = _gnorm(col(MQ), gm_ref[...]).astype(BF16)

    blk = pl.BlockSpec((tm, BW), lambda i: (i, 0))
    vec = pl.BlockSpec((1, BW), lambda i: (0, 0))
    return pl.pallas_call(
        body, grid=(t // tm,), in_specs=[pl.BlockSpec((tm, 9 * BW), lambda i: (i, 0)), blk, blk, vec, vec, vec],
        out_specs=tuple(blk for _ in range(7)), out_shape=tuple(_sds((t, BW), BF16) for _ in range(7)),
        name="prep_fwd", compiler_params=_cp())(proj, cos2, sin2, g_naq, g_nak, g_mq)


def _prep_bwd(proj, cos2, sin2, g_naq, g_nak, g_mq, d_rq, d_rk, d_rv, d_rg, d_pv, d_nq, d_nk, d_nv, d_mq):
    t = proj.shape[0]
    tm = 256

    def body(p_ref, cos_ref, sin_ref, gq_ref, gk_ref, gm_ref, drq_ref, drk_ref, drv_ref, drg_ref, dpv_ref, dnq_ref, dnk_ref,
             dnv_ref, dmq_ref, o_ref, dgq_ref, dgk_ref, dgm_ref):
        first = pl.program_id(0) == 0

        def col(c):
            return p_ref[:, c * BW:(c + 1) * BW]

        def put(c, v):
            o_ref[:, c * BW:(c + 1) * BW] = v.astype(BF16)

        cosv, sinv = cos_ref[...], sin_ref[...]

        def rot_t(dv):
            return dv * cosv + _swap_halves(dv * sinv)

        put(RQ, rot_t(drq_ref[...] * (HD ** -0.5)))
        put(RK, rot_t(drk_ref[...]))
        put(RV, drv_ref[...])
        put(RG, drg_ref[...])
        put(PV, dpv_ref[...])
        dq, gq = _gnorm_bwd(dnq_ref[...], col(NQ), gq_ref[...])
        put(NQ, dq)
        _acc(dgq_ref, jnp.sum(gq, axis=0, keepdims=True), first)
        dk, gk = _gnorm_bwd(dnk_ref[...], col(NK), gk_ref[...])
        put(NK, dk)
        _acc(dgk_ref, jnp.sum(gk, axis=0, keepdims=True), first)
        put(NV, dnv_ref[...])
        dm, gm = _gnorm_bwd(dmq_ref[...], col(MQ), gm_ref[...])
        put(MQ, dm)
        _acc(dgm_ref, jnp.sum(gm, axis=0, keepdims=True), first)

    blk = pl.BlockSpec((tm, BW), lambda i: (i, 0))
    vec = pl.BlockSpec((1, BW), lambda i: (0, 0))
    wide = pl.BlockSpec((tm, 9 * BW), lambda i: (i, 0))
    return pl.pallas_call(
        body, grid=(t // tm,), in_specs=[wide, blk, blk, vec, vec, vec] + [blk] * 9, out_specs=(wide, vec, vec, vec),
        out_shape=(_sds((t, 9 * BW), BF16), _sds((1, BW), F32), _sds((1, BW), F32), _sds((1, BW), F32)),
        name="prep_bwd", compiler_params=_cp())(proj, cos2, sin2, g_naq, g_nak, g_mq, d_rq, d_rk, d_rv, d_rg, d_pv, d_nq, d_nk,
                                                d_nv, d_mq)


RET_B = 256


def _ret_consts(lgf_ref, lgb_ref):
    bsz = RET_B
    head = _lane_head((1, BW))
    lf, lb = jnp.zeros((1, BW), F32), jnp.zeros((1, BW), F32)
    for h in range(NH):
        lf = lf + jnp.where(head == h, lgf_ref[h], 0.0)
        lb = lb + jnp.where(head == h, lgb_ref[h], 0.0)
    pos = lax.broadcasted_iota(jnp.int32, (bsz, BW), 0).astype(F32)
    up, down = pos + 1.0, (bsz - 1.0) - pos
    c = dict(up=up, down=down, kf=jnp.exp(down * lf), kb=jnp.exp(up * lb), qf=jnp.exp(up * lf), qb=jnp.exp(down * lb),
             cf=jnp.exp(bsz * lf), cb=jnp.exp(bsz * lb))
    diff = (lax.broadcasted_iota(jnp.int32, (NH * bsz, 1), 0) & (bsz - 1)) - lax.broadcasted_iota(jnp.int32, (1, bsz), 1)
    c["causal"] = diff >= 0
    c["dist"] = jnp.abs(diff).astype(F32)
    lgf = jnp.concatenate([jnp.full((bsz, 1), lgf_ref[h], F32) for h in range(NH)], axis=0)
    lgb = jnp.concatenate([jnp.full((bsz, 1), lgb_ref[h], F32) for h in range(NH)], axis=0)
    c["dm"] = jnp.exp(c["dist"] * jnp.where(c["causal"], lgf, lgb))
    c["bd"] = _lane_head((BW, BW)) == lax.shift_right_logical(lax.broadcasted_iota(jnp.int32, (BW, BW), 0), 6)
    return c


def _ret_states(k_ref, v_ref, st_ref, c, nb):
    bsz = RET_B

    def summary(b, decay):
        kb = k_ref[b * bsz:(b + 1) * bsz, :].astype(F32)
        return jnp.where(c["bd"], _dot(kb * decay, v_ref[b * bsz:(b + 1) * bsz, :], ta=True), 0.0)

    f = jnp.zeros((BW, BW), F32)
    for b in range(nb):
        st_ref[b] = f
        if b < nb - 1:
            f = c["cf"] * f + summary(b, c["kf"])
    g = jnp.zeros((BW, BW), F32)
    for b in reversed(range(nb)):
        st_ref[nb + b] = g
        if b > 0:
            g = c["cb"] * g + summary(b, c["kb"])


def _ret_fwd(q, k, v, proj, lgf, lgb, g_ret):
    t = q.shape[0]
    bsz, nb = RET_B, t // RET_B

    def body(lgf_ref, lgb_ref, q_ref, k_ref, v_ref, rg_ref, g_ref, o_ref, ret_ref, st_ref):
        c = _ret_consts(lgf_ref, lgb_ref)
        _ret_states(k_ref, v_ref, st_ref, c, nb)
        for b in range(nb):
            blk = slice(b * bsz, (b + 1) * bsz)
            qb, kb, vb = q_ref[blk, :], k_ref[blk, :], v_ref[blk, :]
            s = _dot(_stack_heads(qb), kb, tb=True)
            o = _unstack_heads(_dot(s * c["dm"], vb), bsz)
            q32 = qb.astype(F32)
            o = o + _dot(q32 * c["qf"], st_ref[b]) + _dot(q32 * c["qb"], st_ref[nb + b])
            o_ref[blk, :] = o
            rg = rg_ref[blk, :]
            ret_ref[blk, :] = (_gnorm(o, g_ref[...]) * (rg * _sigmoid(rg))).astype(BF16)

    whole = pl.BlockSpec((t, BW), lambda i: (0, 0))
    return pl.pallas_call(
        body, grid=(1,),
        in_specs=[SMEM, SMEM, whole, whole, whole, pl.BlockSpec((t, BW), lambda i: (0, RG)), pl.BlockSpec((1, BW), lambda i: (0, 0))],
        out_specs=(whole, whole), out_shape=(_sds((t, BW), F32), _sds((t, BW), BF16)),
        scratch_shapes=[pltpu.VMEM((2 * nb, BW, BW), F32)], name="ret_fwd", compiler_params=_cp())(lgf, lgb, q, k, v, proj, g_ret)


def _ret_post_bwd(dbr, o_ret, proj, g_ret):
    t = o_ret.shape[0]
    tm = 256

    def body(d_ref, o_ref, rg_ref, g_ref, do_ref, drg_ref, dg_ref):
        dret, o, rg, g = d_ref[...], o_ref[...], rg_ref[...], g_ref[...]
        sg = _sigmoid(rg)
        do, dgain = _gnorm_bwd(dret * (rg * sg), o, g)
        do_ref[...] = do.astype(BF16)
        drg_ref[...] = dret * _gnorm(o, g) * (sg * (1.0 + rg * (1.0 - sg)))
        _acc(dg_ref, jnp.sum(dgain, axis=0, keepdims=True), pl.program_id(0) == 0)

    blk = pl.BlockSpec((tm, BW), lambda i: (i, 0))
    vec = pl.BlockSpec((1, BW), lambda i: (0, 0))
    return pl.pallas_call(
        body, grid=(t // tm,), in_specs=[blk, blk, pl.BlockSpec((tm, BW), lambda i: (i, RG)), vec], out_specs=(blk, blk, vec),
        out_shape=(_sds((t, BW), BF16), _sds((t, BW), F32), _sds((1, BW), F32)), name="ret_post_bwd",
        compiler_params=_cp())(dbr, o_ret, proj, g_ret)


def _ret_bwd(do, q, k, v, lgf, lgb):
    t = q.shape[0]
    bsz, nb = RET_B, t // RET_B

    def body(lgf_ref, lgb_ref, d_ref, q_ref, k_ref, v_ref, dq_ref, dk_ref, dv_ref, dlg_ref, st_ref, sd_ref):
        c = _ret_consts(lgf_ref, lgb_ref)
        _ret_states(k_ref, v_ref, st_ref, c, nb)
        lane_f, lane_b = jnp.zeros((1, BW), F32), jnp.zeros((1, BW), F32)
        row_f, row_b = jnp.zeros((NH * bsz, 1), F32), jnp.zeros((NH * bsz, 1), F32)

        def rows(x):
            return jnp.sum(x, axis=0, keepdims=True)

        for b in range(nb):
            blk = slice(b * bsz, (b + 1) * bsz)
            qb, kb, vb, dob = q_ref[blk, :], k_ref[blk, :], v_ref[blk, :], d_ref[blk, :]
            q32 = qb.astype(F32)
            qs, dos = _stack_heads(qb), _stack_heads(dob)
            s = _dot(qs, kb, tb=True)
            da = _dot(dos, vb, tb=True)
            dv_ref[blk, :] = _dot(s * c["dm"], dos, ta=True)
            ds = da * c["dm"]
            w = ds * s * c["dist"]
            row_f = row_f + jnp.sum(jnp.where(c["causal"], w, 0.0), axis=1, keepdims=True)
            row_b = row_b + jnp.sum(jnp.where(c["causal"], 0.0, w), axis=1, keepdims=True)
            dsb = ds.astype(MXU)
            dk_ref[blk, :] = _dot(dsb, qs, ta=True)
            dq_f = _dot(dob, st_ref[b], tb=True) * c["qf"]
            dq_b = _dot(dob, st_ref[nb + b], tb=True) * c["qb"]
            lane_f = lane_f + rows(c["up"] * dq_f * q32)
            lane_b = lane_b + rows(c["down"] * dq_b * q32)
            dq_ref[blk, :] = _unstack_heads(_dot(dsb, kb), bsz) + dq_f + dq_b
            sd_ref[b] = jnp.where(c["bd"], _dot(q32 * c["qf"], dob, ta=True), 0.0)
            sd_ref[nb + b] = jnp.where(c["bd"], _dot(q32 * c["qb"], dob, ta=True), 0.0)

        def through_state(b, grad, decay, weight, lane):
            blk = slice(b * bsz, (b + 1) * bsz)
            k32 = k_ref[blk, :].astype(F32)
            dk = _dot(v_ref[blk, :], grad, tb=True) * decay
            dk_ref[blk, :] += dk
            dv_ref[blk, :] += _dot(k32 * decay, grad)
            return lane + rows(weight * dk * k32)

        phi = jnp.zeros((BW, BW), F32)
        for b in reversed(range(nb)):
            if b < nb - 1:
                lane_f = through_state(b, phi, c["kf"], c["down"], lane_f)
                lane_f = lane_f + bsz * rows(c["cf"] * st_ref[b] * phi)
            phi = sd_ref[b] + c["cf"] * phi
        gam = jnp.zeros((BW, BW), F32)
        for b in range(nb):
            if b > 0:
                lane_b = through_state(b, gam, c["kb"], c["up"], lane_b)
                lane_b = lane_b + bsz * rows(c["cb"] * st_ref[nb + b] * gam)
            gam = sd_ref[nb + b] + c["cb"] * gam

        head = _lane_head((1, BW))
        for h in range(NH):
            tot_f = jnp.sum(row_f[h * bsz:(h + 1) * bsz, :]) + jnp.sum(jnp.where(head == h, lane_f, 0.0))
            tot_b = jnp.sum(row_b[h * bsz:(h + 1) * bsz, :]) + jnp.sum(jnp.where(head == h, lane_b, 0.0))
            dlg_ref[h:h + 1, :] = jnp.full((1, 128), tot_f, F32)
            dlg_ref[NH + h:NH + h + 1, :] = jnp.full((1, 128), tot_b, F32)

    whole = pl.BlockSpec((t, BW), lambda i: (0, 0))
    return pl.pallas_call(
        body, grid=(1,), in_specs=[SMEM, SMEM, whole, whole, whole, whole],
        out_specs=(whole, whole, whole, pl.BlockSpec((2 * NH, 128), lambda i: (0, 0))),
        out_shape=(_sds((t, BW), F32), _sds((t, BW), F32), _sds((t, BW), F32), _sds((2 * NH, 128), F32)),
        scratch_shapes=[pltpu.VMEM((2 * nb, BW, BW), F32), pltpu.VMEM((2 * nb, BW, BW), F32)], name="ret_bwd",
        compiler_params=_cp())(lgf, lgb, do, q, k, v)


def _pool_windows(t):
    row = lax.broadcasted_iota(jnp.int32, (t, BW), 0)
    half = lax.shift_left(jnp.ones((t, BW), jnp.int32), _lane_head((t, BW)))
    cnt = (jnp.minimum(row + half, t) - jnp.maximum(row - half, 0)).astype(F32)
    return row, half, cnt


def _pool_window_sum(v, row, half, t, transpose):
    out = jnp.zeros_like(v)
    for j in range(-POOL_HALF_MAX, POOL_HALF_MAX):
        src = row - j if transpose else row + j
        ok = (src >= 0) & (src < t) & (j >= -half) & (j < half)
        out = out + jnp.where(ok, pltpu.roll(v, (j if transpose else -j) % t, 0), 0.0)
    return out


def _pool_fwd(proj, wbd, scale):
    t = proj.shape[0]

    def body(v_ref, w_ref, s_ref, o_ref):
        v = v_ref[...]
        row, half, cnt = _pool_windows(t)
        pooled = _pool_window_sum(v, row, half, t, False) / cnt - v
        o_ref[...] = (_dot(pooled, w_ref[...]) * s_ref[...]).astype(BF16)

    return pl.pallas_call(
        body, grid=(1,),
        in_specs=[pl.BlockSpec((t, BW), lambda i: (0, PV)), pl.BlockSpec((BW, BW), lambda i: (0, 0)), pl.BlockSpec((1, BW), lambda i: (0, 0))],
        out_specs=pl.BlockSpec((t, BW), lambda i: (0, 0)), out_shape=_sds((t, BW), BF16), name="pool_fwd",
        compiler_params=_cp())(proj, wbd, scale)


def _pool_bwd(dbr, proj, wbd, scale):
    t = proj.shape[0]

    def body(d_ref, v_ref, w_ref, s_ref, dv_ref, dw_ref, ds_ref):
        v, dout = v_ref[...], d_ref[...]
        row, half, cnt = _pool_windows(t)
        pooled = _pool_window_sum(v, row, half, t, False) / cnt - v
        mixed = _dot(pooled, w_ref[...])
        ds_ref[...] = jnp.sum(dout * mixed, axis=0, keepdims=True)
        dmixed = dout * s_ref[...]
        dw_ref[...] = _dot(pooled, dmixed, ta=True)
        dpooled = _dot(dmixed, w_ref[...], tb=True)
        dv_ref[...] = _pool_window_sum(dpooled / cnt, row, half, t, True) - dpooled

    return pl.pallas_call(
        body, grid=(1,),
        in_specs=[pl.BlockSpec((t, BW), lambda i: (0, 1)), pl.BlockSpec((t, BW), lambda i: (0, PV)),
                  pl.BlockSpec((BW, BW), lambda i: (0, 0)), pl.BlockSpec((1, BW), lambda i: (0, 0))],
        out_specs=(pl.BlockSpec((t, BW), lambda i: (0, 0)), pl.BlockSpec((BW, BW), lambda i: (0, 0)), pl.BlockSpec((1, BW), lambda i: (0, 0))),
        out_shape=(_sds((t, BW), F32), _sds((BW, BW), F32), _sds((1, BW), F32)), name="pool_bwd",
        compiler_params=_cp())(dbr, proj, wbd, scale)


NA_KEYS = NA_ROWS_WIN * GRID_W
NA_PAIRS = 2 * NA_ROWS_WIN - 2


def _na_window(r, n_rows):
    rs = jnp.clip(r - NA_ROWS_WIN // 2, 0, n_rows - NA_ROWS_WIN)
    return pl.multiple_of(rs * GRID_W, GRID_W), rs - r + (NA_ROWS_WIN - 1)


def _na_bias(b_ref, a0):
    return jnp.concatenate([b_ref[a0 + 2 * j] for j in range(NA_ROWS_WIN // 2)], axis=1)


NA_STEP_ROWS = 8


def _na_fwd(q, k, v, ball):
    t = q.shape[0]
    n_rows = t // GRID_W
    rows = NA_STEP_ROWS

    def body(q_ref, k_ref, v_ref, b_ref, o_ref):
        for rr in range(rows):
            start, a0 = _na_window(pl.program_id(0) * rows + rr, n_rows)
            own = slice(rr * GRID_W, (rr + 1) * GRID_W)
            qs = _stack_heads(q_ref[own, :])
            s = _dot(qs, k_ref[pl.ds(start, NA_KEYS), :], tb=True) * (HD ** -0.5) + _na_bias(b_ref, a0)
            p = _softmax_rows(s)
            o_ref[own, :] = _unstack_heads(_dot(p, v_ref[pl.ds(start, NA_KEYS), :]), GRID_W).astype(BF16)

    blk = pl.BlockSpec((rows * GRID_W, BW), lambda r: (r, 0))
    whole = pl.BlockSpec((t, BW), lambda r: (0, 0))
    return pl.pallas_call(
        body, grid=(n_rows // rows,), in_specs=[blk, whole, whole, pl.BlockSpec(ball.shape, lambda r: (0, 0, 0))],
        out_specs=blk, out_shape=_sds((t, BW), BF16), name="na_fwd", compiler_params=_cp())(q, k, v, ball)


def _na_bwd(dbr, q, k, v, ball):
    t = q.shape[0]
    n_rows = t // GRID_W

    rows = NA_STEP_ROWS

    def body(d_ref, q_ref, k_ref, v_ref, b_ref, dq_ref, dk_ref, dv_ref, db_ref):
        @pl.when(pl.program_id(0) == 0)
        def _():
            dk_ref[...] = jnp.zeros_like(dk_ref)
            dv_ref[...] = jnp.zeros_like(dv_ref)
            db_ref[...] = jnp.zeros_like(db_ref)

        for rr in range(rows):
            start, a0 = _na_window(pl.program_id(0) * rows + rr, n_rows)
            keys = pl.ds(start, NA_KEYS)
            own = slice(rr * GRID_W, (rr + 1) * GRID_W)
            qs = _stack_heads(q_ref[own, :])
            kb, vb = k_ref[keys, :], v_ref[keys, :]
            p = _softmax_rows(_dot(qs, kb, tb=True) * (HD ** -0.5) + _na_bias(b_ref, a0))
            dos = _stack_heads(d_ref[own, :]).astype(MXU)
            dp = _dot(dos, vb, tb=True)
            dv_ref[keys, :] += _dot(p, dos, ta=True)
            ds = p * (dp - jnp.sum(dp * p, axis=-1, keepdims=True))
            for j in range(NA_ROWS_WIN // 2):
                db_ref[a0 + 2 * j] += ds[:, 2 * j * GRID_W:(2 * j + 2) * GRID_W]
            dsb = (ds * (HD ** -0.5)).astype(MXU)
            dq_ref[own, :] = _unstack_heads(_dot(dsb, kb), GRID_W)
            dk_ref[keys, :] += _dot(dsb, qs, ta=True)

    blk = pl.BlockSpec((rows * GRID_W, BW), lambda r: (r, 0))
    whole = pl.BlockSpec((t, BW), lambda r: (0, 0))
    tab = pl.BlockSpec(ball.shape, lambda r: (0, 0, 0))
    return pl.pallas_call(
        body, grid=(n_rows // rows,), in_specs=[pl.BlockSpec((rows * GRID_W, BW), lambda r: (r, 2)), blk, whole, whole, tab],
        out_specs=(blk, whole, whole, tab),
        out_shape=(_sds((t, BW), F32), _sds((t, BW), F32), _sds((t, BW), F32), _sds(ball.shape, F32)), name="na_bwd",
        compiler_params=_cp())(dbr, q, k, v, ball)


def _rpb_expand(rpb_pad, onehot):
    def body(r_ref, e_ref, o_ref):
        o_ref[...] = jnp.dot(r_ref[...], e_ref[...], precision=HI, preferred_element_type=F32)

    return pl.pallas_call(body, out_shape=_sds((rpb_pad.shape[0], GRID_W * GRID_W), F32), name="rpb_expand",
                          compiler_params=_cp())(rpb_pad, onehot)


def _rpb_reduce(dtab, onehot):
    def body(d_ref, e_ref, o_ref):
        o_ref[...] = lax.dot_general(d_ref[...], e_ref[...], (((1,), (1,)), ((), ())), precision=HI, preferred_element_type=F32)

    return pl.pallas_call(body, out_shape=_sds((dtab.shape[0], 128), F32), name="rpb_reduce", compiler_params=_cp())(dtab, onehot)


MEM_TQ = 256


def _mem_fwd(q, mk, mv):
    t = q.shape[0]
    tq = MEM_TQ

    def body(q_ref, k_ref, v_ref, o_ref):
        p = _softmax_rows(_dot(_stack_heads(q_ref[...]), k_ref[...], tb=True) * (HD ** -0.5))
        o_ref[...] = _unstack_heads(_dot(p, v_ref[...]), tq).astype(BF16)

    blk = pl.BlockSpec((tq, BW), lambda i: (i, 0))
    kv = pl.BlockSpec((N_MEM, BW), lambda i: (0, 0))
    return pl.pallas_call(body, grid=(t // tq,), in_specs=[blk, kv, kv], out_specs=blk, out_shape=_sds((t, BW), BF16),
                          name="mem_fwd", compiler_params=_cp())(q, mk, mv)


def _mem_bwd(dbr, q, mk, mv):
    t = q.shape[0]
    tq = MEM_TQ

    def body(d_ref, q_ref, k_ref, v_ref, dq_ref, dk_ref, dv_ref):
        first = pl.program_id(0) == 0
        qs = _stack_heads(q_ref[...])
        dos = _stack_heads(d_ref[...]).astype(MXU)
        p = _softmax_rows(_dot(qs, k_ref[...], tb=True) * (HD ** -0.5))
        dp = _dot(dos, v_ref[...], tb=True)
        _acc(dv_ref, _dot(p, dos, ta=True), first)
        dsb = (p * (dp - jnp.sum(dp * p, axis=-1, keepdims=True)) * (HD ** -0.5)).astype(MXU)
        dq_ref[...] = _unstack_heads(_dot(dsb, k_ref[...]), tq)
        _acc(dk_ref, _dot(dsb, qs, ta=True), first)

    blk = pl.BlockSpec((tq, BW), lambda i: (i, 0))
    kv = pl.BlockSpec((N_MEM, BW), lambda i: (0, 0))
    return pl.pallas_call(
        body, grid=(t // tq,), in_specs=[pl.BlockSpec((tq, BW), lambda i: (i, 3)), blk, kv, kv], out_specs=(blk, kv, kv),
        out_shape=(_sds((t, BW), F32), _sds((N_MEM, BW), F32), _sds((N_MEM, BW), F32)), name="mem_bwd",
        compiler_params=_cp())(dbr, q, mk, mv)


def _memkv_prep(kv, g_mk):
    def body(kv_ref, g_ref, k_ref, v_ref):
        k_ref[...] = _gnorm(kv_ref[:, 0:BW], g_ref[...]).astype(BF16)
        v_ref[...] = kv_ref[:, BW:2 * BW].astype(BF16)

    return pl.pallas_call(body, out_shape=(_sds((N_MEM, BW), BF16), _sds((N_MEM, BW), BF16)), name="memkv_prep",
                          compiler_params=_cp())(kv, g_mk)


def _memkv_bwd(kv, dk, dv, g_mk):
    def body(kv_ref, dk_ref, dv_ref, g_ref, o_ref, dg_ref):
        dkk, gain = _gnorm_bwd(dk_ref[...], kv_ref[:, 0:BW], g_ref[...])
        o_ref[:, 0:BW] = dkk.astype(BF16)
        o_ref[:, BW:2 * BW] = dv_ref[...].astype(BF16)
        dg_ref[...] = jnp.sum(gain, axis=0, keepdims=True)

    return pl.pallas_call(body, out_shape=(_sds((N_MEM, 2 * BW), BF16), _sds((1, BW), F32)), name="memkv_bwd",
                          compiler_params=_cp())(kv, dk, dv, g_mk)


MERGE_TM = 512


def _merge_fwd(brs, wbt, gp):
    t = gp.shape[0]
    tm = MERGE_TM

    def body(b0, b1, b2, b3, wb_ref, gp_ref, o_ref):
        out = jnp.zeros((tm, D), F32)
        for n, b_ref in enumerate((b0, b1, b2, b3)):
            up = _dot(b_ref[...], wb_ref[n], tb=True)
            out = out + _sigmoid(gp_ref[:, n * D:(n + 1) * D].astype(F32)) * up
        o_ref[...] = out.astype(BF16)

    blk = pl.BlockSpec((tm, BW), lambda i: (i, 0))
    return pl.pallas_call(
        body, grid=(t // tm,),
        in_specs=[blk, blk, blk, blk, pl.BlockSpec((NH, D, BW), lambda i: (0, 0, 0)), pl.BlockSpec((tm, NH * D), lambda i: (i, 0))],
        out_specs=pl.BlockSpec((tm, D), lambda i: (i, 0)), out_shape=_sds((t, D), BF16), name="merge_fwd",
        compiler_params=_cp())(*brs, wbt, gp)


def _merge_bwd(dmerged, brs, wbt, gp):
    t = gp.shape[0]
    tm = MERGE_TM
    steps = t // tm

    def body(d_ref, b0, b1, b2, b3, wb_ref, gp_ref, dgp_ref, dbr_ref, dwb_ref, acc_ref):
        i = pl.program_id(0)
        dm = d_ref[...]
        for n, b_ref in enumerate((b0, b1, b2, b3)):
            br = b_ref[...]
            up = _dot(br, wb_ref[n], tb=True)
            g = _sigmoid(gp_ref[:, n * D:(n + 1) * D].astype(F32))
            dgp_ref[:, n * D:(n + 1) * D] = (dm * up * (g * (1.0 - g))).astype(BF16)
            dup = (dm * g).astype(BF16)
            dbr_ref[:, n * BW:(n + 1) * BW] = _dot(dup, wb_ref[n])
            part = _dot(dup, br, ta=True)

            @pl.when(i == 0)
            def _():
                acc_ref[n] = part

            @pl.when(i > 0)
            def _():
                acc_ref[n] += part

        @pl.when(i == steps - 1)
        def _():
            dwb_ref[...] = acc_ref[...].astype(BF16)

    row = pl.BlockSpec((tm, D), lambda i: (i, 0))
    blk = pl.BlockSpec((tm, BW), lambda i: (i, 0))
    wide = pl.BlockSpec((tm, NH * D), lambda i: (i, 0))
    whole = pl.BlockSpec((NH, D, BW), lambda i: (0, 0, 0))
    return pl.pallas_call(
        body, grid=(steps,), in_specs=[row, blk, blk, blk, blk, whole, wide], out_specs=(wide, row, whole),
        out_shape=(_sds((t, NH * D), BF16), _sds((t, NH * BW), F32), _sds((NH, D, BW), BF16)),
        scratch_shapes=[pltpu.VMEM((NH, D, BW), F32)], name="merge_bwd", compiler_params=_cp())(dmerged, *brs, wbt, gp)


FFN_TN = 256


def _ffn_in_fwd(h2, w_t):
    t = h2.shape[0]
    tm, tn = _tile(t, 2048), FFN_TN
    nj = FF // tn

    def body(x_ref, wa_ref, wg_ref, a_ref, g_ref, y_ref):
        x = x_ref[...]
        a, g = _dot(x, wa_ref[...], tb=True), _dot(x, wg_ref[...], tb=True)
        a_ref[...] = a.astype(BF16)
        g_ref[...] = g.astype(BF16)
        y_ref[...] = (a * _sigmoid(a) * g).astype(BF16)

    out = pl.BlockSpec((tm, tn), lambda i, j: (i, j))
    return pl.pallas_call(
        body, grid=(t // tm, nj),
        in_specs=[pl.BlockSpec((tm, D), lambda i, j: (i, 0)), pl.BlockSpec((tn, D), lambda i, j: (j, 0)),
                  pl.BlockSpec((tn, D), lambda i, j: (j + nj, 0))],
        out_specs=(out, out, out), out_shape=tuple(_sds((t, FF), BF16) for _ in range(3)), name="ffn_in_fwd",
        compiler_params=_cp(dimension_semantics=("parallel", "parallel")))(h2, w_t, w_t)


def _ffn_out_bwd(dx2b, w_out, a, g, dep):
    t = dx2b.shape[0]
    tm, tn = _tile(t, 2048), FFN_TN

    def body(*refs):
        x_ref, w_ref, a_ref, g_ref = refs[:4]
        da_ref, dg_ref = refs[-2:]
        d = _dot(x_ref[...], w_ref[...], tb=True)
        av, gv = a_ref[...].astype(F32), g_ref[...].astype(F32)
        s = _sigmoid(av)
        da_ref[...] = (d * gv * (s * (1.0 + av * (1.0 - s)))).astype(BF16)
        dg_ref[...] = (d * (av * s)).astype(BF16)

    blk = pl.BlockSpec((tm, tn), lambda i, j: (i, j))
    ins = [pl.BlockSpec((tm, D), lambda i, j: (i, 0)), pl.BlockSpec((tn, D), lambda i, j: (j, 0)), blk, blk]
    args = [dx2b, w_out, a, g]
    if dep is not None:
        ins.append(pl.BlockSpec((8, 128), lambda i, j: (0, 0)))
        args.append(dep)
    return pl.pallas_call(
        body, grid=(t // tm, FF // tn), in_specs=ins, out_specs=(blk, blk),
        out_shape=(_sds((t, FF), BF16), _sds((t, FF), BF16)), name="ffn_out_bwd",
        compiler_params=_cp(dimension_semantics=("parallel", "parallel")))(*args)


def _loss_head(y, target):
    t, d = y.shape
    tm = 256

    def body(y_ref, t_ref, dy_ref, dyb_ref, l_ref):
        e = y_ref[...] - t_ref[...]
        dy_ref[...] = e * (1.0 / d)
        dyb_ref[...] = (e * (1.0 / d)).astype(BF16)
        _acc(l_ref, jnp.full((8, 128), 0.5 * jnp.sum(jnp.sum(e * e, axis=-1, keepdims=True) * (1.0 / d)), F32), pl.program_id(0) == 0)

    row = pl.BlockSpec((tm, d), lambda i: (i, 0))
    return pl.pallas_call(body, grid=(t // tm,), in_specs=[row, row], out_specs=(row, row, pl.BlockSpec((8, 128), lambda i: (0, 0))),
                          out_shape=(_sds((t, d), F32), _sds((t, d), BF16), _sds((8, 128), F32)), name="loss_head",
                          compiler_params=_cp())(y, target)


def _sum_slots(x, name):
    k, r, c = x.shape
    tr = _tile(r, 512) if r % 128 == 0 else r

    def body(x_ref, o_ref):
        acc = x_ref[0].astype(F32)
        for s in range(1, k):
            acc = acc + x_ref[s].astype(F32)
        o_ref[...] = acc

    return pl.pallas_call(body, grid=(r // tr,), in_specs=[pl.BlockSpec((k, tr, c), lambda i: (0, i, 0))],
                          out_specs=pl.BlockSpec((tr, c), lambda i: (i, 0)), out_shape=_sds((r, c), F32), name=name,
                          compiler_params=_cp())(x)


def _pair_sum(bufs, recvs, cidx):
    n = len(bufs)

    def body(c_ref, *refs):
        for i in range(n):
            refs[2 * n + i][...] = (refs[i][...].astype(F32) + refs[n + i][...].astype(F32)).astype(BF16)

    return pl.pallas_call(
        body,
        grid_spec=pltpu.PrefetchScalarGridSpec(
            num_scalar_prefetch=1, grid=(4,),
            in_specs=[pl.BlockSpec((None, None) + b.shape[2:], lambda s, cref: (s, cref[0], 0, 0)) for b in bufs]
            + [pl.BlockSpec((None,) + r.shape[1:], lambda s, cref: (s, 0, 0)) for r in recvs],
            out_specs=tuple(pl.BlockSpec((None,) + r.shape[1:], lambda s, cref: (s, 0, 0)) for r in recvs)),
        out_shape=tuple(_sds(r.shape, BF16) for r in recvs), name="rs_pair_sum", compiler_params=_cp())(cidx, *bufs, *recvs)


def _adamw_update(w, gv, m, v):
    mn = ADAM_B1 * m + (1.0 - ADAM_B1) * gv
    vn = ADAM_B2 * v + (1.0 - ADAM_B2) * (gv * gv)
    m_hat = mn / (1.0 - ADAM_B1 ** ADAM_STEP)
    v_hat = vn / (1.0 - ADAM_B2 ** ADAM_STEP)
    return -ADAM_LR * (m_hat / (jnp.sqrt(v_hat) + ADAM_EPS) + ADAM_WD * w), mn, vn


def _adamw(w, g, m, v, name):
    r, c = w.shape

    def body(w_ref, g_ref, m_ref, v_ref, d_ref, nm_ref, nv_ref):
        d_ref[...], nm_ref[...], nv_ref[...] = _adamw_update(w_ref[...], g_ref[...], m_ref[...], v_ref[...])

    blk = pl.BlockSpec((r, c), lambda i: (0, 0))
    return pl.pallas_call(body, grid=(1,), in_specs=[blk] * 4, out_specs=(blk,) * 3,
                          out_shape=tuple(_sds((r, c), F32) for _ in range(3)), name=name, compiler_params=_cp())(w, g, m, v)


def _adamw_layer(layer, w, g, m, v, outs, name):
    _, r, c = w.shape
    tr = max(d for d in range(8, r + 1, 8) if r % d == 0 and d * c * 4 <= 2 ** 20)

    def body(w_ref, m_ref, v_ref, g_ref, *refs):
        d_ref, nm_ref, nv_ref, go_ref = refs[4:]
        gv = g_ref[...]
        d_ref[...], nm_ref[...], nv_ref[...] = _adamw_update(w_ref[...], gv, m_ref[...], v_ref[...])
        go_ref[...] = gv

    blk = pl.BlockSpec((None, tr, c), lambda i: (layer, i, 0))
    return pl.pallas_call(
        body, grid=(r // tr,), in_specs=[blk] * 3 + [pl.BlockSpec((tr, c), lambda i: (i, 0))] + [ANY] * 4, out_specs=(blk,) * 4,
        out_shape=tuple(_sds(w.shape, F32) for _ in range(4)), input_output_aliases={4 + j: j for j in range(4)}, name=name,
        compiler_params=_cp())(w, m, v, g, *outs)


def _all_gather(shards, name):
    n = len(shards)

    def body(*refs):
        x_refs, out_refs = refs[:n], refs[n:2 * n]
        send_sems, recv_sems, local_sems = refs[2 * n:]
        x, y, cc = lax.axis_index("x"), lax.axis_index("y"), lax.axis_index("c")
        me, sibling = (x, y, cc), (x, y, 1 - cc)
        chips = [(1 - x, y), (x, 1 - y), (1 - x, 1 - y)]

        def copy(i, k, block, to, own=False):
            px, py, pc = block
            slot = out_refs[i].at[4 * px + 2 * py + pc]
            return pltpu.make_async_remote_copy(
                src_ref=x_refs[i] if own else slot, dst_ref=slot, send_sem=send_sems.at[7 * i + k],
                recv_sem=recv_sems.at[7 * i + k], device_id=to, device_id_type=MESH)

        mine = [pltpu.make_async_copy(x_refs[i], out_refs[i].at[4 * x + 2 * y + cc], local_sems.at[i]) for i in range(n)]
        for cp in mine:
            cp.start()
        first = []
        for j, chip in enumerate(chips):
            first += [copy(i, 1 + j, me, (*chip, cc), own=True) for i in range(n)]
        first += [copy(i, 0, me, sibling, own=True) for i in range(n)]
        for cp in first:
            cp.start()
        passed = []
        for j, chip in enumerate(chips):
            for i in range(n):
                copy(i, 1 + j, (*chip, cc), me).wait_recv()
                cp = copy(i, 4 + j, (*chip, cc), sibling)
                cp.start()
                passed.append(cp)
        for i in range(n):
            copy(i, 0, sibling, me).wait_recv()
        for j, chip in enumerate(chips):
            for i in range(n):
                copy(i, 4 + j, (*chip, 1 - cc), me).wait_recv()
        for cp in first + passed:
            cp.wait_send()
        for cp in mine:
            cp.wait()

    return pl.pallas_call(
        body, out_shape=tuple(_sds((N_DEV,) + s.shape, s.dtype) for s in shards), in_specs=[ANY] * n, out_specs=(ANY,) * n,
        scratch_shapes=[pltpu.SemaphoreType.DMA((7 * n,)), pltpu.SemaphoreType.DMA((7 * n,)), pltpu.SemaphoreType.DMA((n,))],
        name=name)(*shards)


def _rs_core_swap(bufs, name):
    n = len(bufs)

    def body(*refs):
        b_refs, recv_refs = refs[:n], refs[n:2 * n]
        send_sems, recv_sems = refs[2 * n:]
        x, y, cc = lax.axis_index("x"), lax.axis_index("y"), lax.axis_index("c")
        copies = [pltpu.make_async_remote_copy(
            src_ref=b_refs[i].at[s, 1 - cc], dst_ref=recv_refs[i].at[s], send_sem=send_sems.at[4 * i + s],
            recv_sem=recv_sems.at[4 * i + s], device_id=(x, y, 1 - cc), device_id_type=MESH) for i in range(n) for s in range(4)]
        for cp in copies:
            cp.start()
        for cp in copies:
            cp.wait()

    return pl.pallas_call(
        body, out_shape=tuple(_sds((4,) + b.shape[2:], b.dtype) for b in bufs), in_specs=[ANY] * n, out_specs=(ANY,) * n,
        scratch_shapes=[pltpu.SemaphoreType.DMA((4 * n,)), pltpu.SemaphoreType.DMA((4 * n,))], name=name)(*bufs)


HBM = pl.BlockSpec(memory_space=pltpu.HBM)
SEMS = pl.BlockSpec(memory_space=pltpu.SEMAPHORE)
EFFECT = pltpu.SideEffectType.DATAFLOW_SIDE_EFFECTING


def _hbm(a):
    return pltpu.HBM(a.shape, a.dtype)


def _other_chips(x, y):
    return [(1 - x, y), (x, 1 - y), (1 - x, 1 - y)]


def _ici_start(srcs, lands, mode, name, group=None):
    n = len(srcs)

    def body(*refs):
        s_refs, land_refs = refs[:n], refs[n:2 * n]
        send_sems, recv_sems = refs[2 * n], refs[2 * n + 1]
        token = refs[-1]
        x, y, cc = lax.axis_index("x"), lax.axis_index("y"), lax.axis_index("c")
        mine = 2 * x + y if mode == "by_chip" else 4 * x + 2 * y + cc
        peers = [(px, py, cc) for px, py in _other_chips(x, y)]
        if mode == "by_device":
            peers = [(x, y, 1 - cc)] + peers + [(px, py, 1 - cc) for px, py in _other_chips(x, y)]
        size = n if group is None else group
        for first in range(0, n, size):
            for px, py, pc in peers:
                for i in range(first, first + size):
                    src = s_refs[i]
                    if mode == "by_chip":
                        src = src.at[2 * px + py]
                    elif mode == "by_device":
                        src = src.at[4 * px + 2 * py + pc]
                    pltpu.make_async_remote_copy(
                        src_ref=src, dst_ref=land_refs[i].at[mine], send_sem=send_sems.at[i], recv_sem=recv_sems.at[i],
                        device_id=(px, py, pc), device_id_type=MESH).start()
        token[...] = jnp.zeros_like(token)

    out = pl.pallas_call(
        body, name=name,
        out_shape=(pltpu.SemaphoreType.DMA((n,)), pltpu.SemaphoreType.DMA((n,)), *[_hbm(s) for s in srcs], *[_hbm(l) for l in lands],
                   _sds((8, 128), F32)),
        in_specs=[HBM] * (2 * n), out_specs=(SEMS, SEMS, *[HBM] * (2 * n), pl.BlockSpec(memory_space=pltpu.VMEM)),
        input_output_aliases={i: 2 + i for i in range(2 * n)}, compiler_params=pltpu.CompilerParams(has_side_effects=EFFECT),
    )(*[pltpu.with_memory_space_constraint(s, pltpu.HBM) for s in srcs],
      *[pltpu.with_memory_space_constraint(l, pltpu.HBM) for l in lands])
    return out[0], out[1], out[2:2 + n], out[2 + n:2 + 2 * n], out[-1], 7 if mode == "by_device" else 3


def _ici_wait(started, after, name, only=None):
    send_sems, recv_sems, srcs, lands, _, copies = started
    only = list(range(len(srcs))) if only is None else only
    srcs, lands = [srcs[i] for i in only], [lands[i] for i in only]
    n = len(srcs)

    def body(*refs):
        land_refs = refs[n:2 * n]
        send_sems, recv_sems = refs[2 * n], refs[2 * n + 1]
        x, y, cc = lax.axis_index("x"), lax.axis_index("y"), lax.axis_index("c")
        for i in range(n):
            three = land_refs[i].at[pl.ds(0, copies)]
            cp = pltpu.make_async_remote_copy(src_ref=three, dst_ref=three, send_sem=send_sems.at[only[i]],
                                              recv_sem=recv_sems.at[only[i]],
                                              device_id=(x, y, cc), device_id_type=MESH)
            cp.wait_send()
            cp.wait_recv()

    return pl.pallas_call(
        body, name=name, out_shape=tuple(_hbm(l) for l in lands), in_specs=[HBM] * (2 * n) + [SEMS, SEMS, ANY],
        out_specs=tuple([HBM] * n), input_output_aliases={n + i: i for i in range(n)},
        compiler_params=pltpu.CompilerParams(has_side_effects=EFFECT))(*srcs, *lands, send_sems, recv_sems, after)


def _gather_d2d(blocks, lands, name):
    n = len(blocks)

    def body(*refs):
        x_refs, land_refs = refs[:n], refs[2 * n:3 * n]
        send_sems, recv_sems, in_sems, out_sems = refs[3 * n:3 * n + 4]
        stage = refs[3 * n + 4:]
        x, y, cc = lax.axis_index("x"), lax.axis_index("y"), lax.axis_index("c")
        sibling = (x, y, 1 - cc)
        staged = [pltpu.make_async_copy(x_refs[i], stage[i], in_sems.at[i]) for i in range(n)]
        for cp in staged:
            cp.start()
        copies = []
        for i in range(n):
            slot = land_refs[i].at[4 * x + 2 * y + cc]
            copies.append(pltpu.make_async_remote_copy(src_ref=x_refs[i], dst_ref=slot, send_sem=send_sems.at[4 * i],
                                                       recv_sem=recv_sems.at[4 * i], device_id=sibling, device_id_type=MESH))
            for j, (px, py) in enumerate(_other_chips(x, y)):
                slot = land_refs[i].at[4 * px + 2 * py + cc]
                copies.append(pltpu.make_async_remote_copy(src_ref=slot, dst_ref=slot, send_sem=send_sems.at[4 * i + 1 + j],
                                                           recv_sem=recv_sems.at[4 * i + 1 + j], device_id=sibling, device_id_type=MESH))
        for cp in copies:
            cp.start()
        mine = []
        for i in range(n):
            staged[i].wait()
            mine.append(pltpu.make_async_copy(stage[i], land_refs[i].at[4 * x + 2 * y + cc], out_sems.at[i]))
            mine[i].start()
        for i in range(n):
            slot = land_refs[i].at[4 * x + 2 * y + (1 - cc)]
            pltpu.make_async_remote_copy(src_ref=slot, dst_ref=slot, send_sem=send_sems.at[4 * i], recv_sem=recv_sems.at[4 * i],
                                         device_id=sibling, device_id_type=MESH).wait_recv()
            for j, (px, py) in enumerate(_other_chips(x, y)):
                slot = land_refs[i].at[4 * px + 2 * py + (1 - cc)]
                pltpu.make_async_remote_copy(src_ref=slot, dst_ref=slot, send_sem=send_sems.at[4 * i + 1 + j],
                                             recv_sem=recv_sems.at[4 * i + 1 + j], device_id=sibling, device_id_type=MESH).wait_recv()
        for cp in copies:
            cp.wait_send()
        for cp in mine:
            cp.wait()

    return pl.pallas_call(
        body, out_shape=tuple(_sds(l.shape, l.dtype) for l in lands), in_specs=[ANY] * (2 * n), out_specs=(ANY,) * n,
        input_output_aliases={n + i: i for i in range(n)},
        scratch_shapes=[pltpu.SemaphoreType.DMA((4 * n,)), pltpu.SemaphoreType.DMA((4 * n,)), pltpu.SemaphoreType.DMA((n,)),
                        pltpu.SemaphoreType.DMA((n,))] + [pltpu.VMEM(b.shape, b.dtype) for b in blocks],
        name=name, compiler_params=_cp())(*blocks, *lands)


def _sum_own(parts, recvs, mine, name):
    n = len(parts)

    def body(c_ref, *refs):
        s = pl.program_id(0)
        for i in range(n):
            val = jnp.where(c_ref[0] == s, refs[i][...], refs[n + i][...]).astype(F32)
            _acc(refs[2 * n + i], val, s == 0)

    kept = [pl.BlockSpec((None,) + p.shape[1:], lambda s, cref: (cref[0], 0, 0)) for p in parts]
    ins = [pl.BlockSpec((None,) + p.shape[1:], lambda s, cref: (s, 0, 0)) for p in parts]
    return pl.pallas_call(
        body, grid_spec=pltpu.PrefetchScalarGridSpec(
            num_scalar_prefetch=1, grid=(parts[0].shape[0],), in_specs=kept + ins,
            out_specs=tuple(pl.BlockSpec(p.shape[1:], lambda s, cref: (0, 0)) for p in parts)),
        out_shape=tuple(_sds(p.shape[1:], F32) for p in parts), name=name, compiler_params=_cp())(mine, *parts, *recvs)


BIG = (("w_in", True), ("w_gate", True), ("w_mem_kv", False), ("w_branch", True), ("w_out", False), ("w_ffn_in", True),
       ("w_ffn_out", False))

SMALL = ("norm_mix_g", "norm_mem_g", "ret_decay_fwd", "ret_decay_bwd", "ret_norm_g", "pool_w", "pool_scale", "na_q_norm_g",
         "na_k_norm_g", "na_rpb", "mem_q_norm_g", "mem_k_norm_g", "norm_ffn_g")
WEIGHTS = ("norm_mix_g", "norm_mem_g", "w_in", "w_gate", "ret_decay_fwd", "ret_decay_bwd", "ret_norm_g", "pool_w", "pool_scale",
           "na_q_norm_g", "na_k_norm_g", "na_rpb", "mem_q_norm_g", "mem_k_norm_g", "w_mem_kv", "w_branch", "w_out", "norm_ffn_g",
           "w_ffn_in", "w_ffn_out")


def _to_exchange(name, transposed, shard):
    if name == "w_branch":
        return jnp.swapaxes(shard, 1, 2).reshape(NH * (D // N_DEV), BW)
    return shard.T if transposed else shard


def _from_exchange(name, transposed, block):
    if name == "w_branch":
        return jnp.swapaxes(block.reshape(NH, D // N_DEV, BW), 1, 2)
    return block.T if transposed else block


def _whole_from_gathered(name, g):
    if name == "w_branch":
        return jnp.swapaxes(g.reshape(N_DEV, NH, D // N_DEV, BW), 0, 1).reshape(NH, D, BW)
    return g.reshape(N_DEV * g.shape[1], g.shape[2])


def _by_destination(name, g):
    if name == "w_branch":
        g = jnp.swapaxes(g.reshape(NH, N_DEV, D // N_DEV, BW), 0, 1).reshape(N_DEV * NH * (D // N_DEV), BW)
    return g.reshape(4, 2, g.shape[0] // N_DEV, g.shape[1])


SMALL_PAD = 1024


def _pack_small(vals, loss=None):
    parts = [vals[n] for n in SMALL] + [jnp.zeros((1,), F32) if loss is None else loss.reshape(1)]
    rows = []
    for p in parts:
        flat = p.reshape(-1)
        rows.append(jnp.pad(flat, (0, -flat.shape[0] % SMALL_PAD)).reshape(-1, 128))
    return jnp.concatenate(rows, axis=0)


def _unpack_small(packed, like):
    out, off = {}, 0
    for n in SMALL:
        sz = int(np.prod(like[n].shape))
        nrow = -(-sz // SMALL_PAD) * (SMALL_PAD // 128)
        out[n] = packed[off:off + nrow].reshape(-1)[:sz].reshape(like[n].shape)
        off += nrow
    return out, packed[off, 0]


def _na_constants():
    c = np.arange(GRID_W)
    win = np.clip(c - NA_COLS_WIN // 2, 0, GRID_W - NA_COLS_WIN)
    kc = np.arange(GRID_W)
    inside = (kc[None, :] >= win[:, None]) & (kc[None, :] < win[:, None] + NA_COLS_WIN)
    off = kc[None, :] - c[:, None] + NA_COLS_WIN - 1
    onehot = np.zeros((128, GRID_W, GRID_W), np.float32)
    for b in range(2 * NA_COLS_WIN - 1):
        onehot[b] = (off == b) & inside
    maskadd = np.where(inside, 0.0, NEG).astype(np.float32)
    return onehot.reshape(128, GRID_W * GRID_W), maskadd


def _na_bias_table(tab, maskadd):
    n_off = 2 * NA_ROWS_WIN - 1
    t4 = tab[:NH * n_off].reshape(NH, n_off, GRID_W, GRID_W) + maskadd[None, None]
    by_off = t4.transpose(1, 0, 2, 3).reshape(n_off, NH * GRID_W, GRID_W)
    return jnp.concatenate([by_off[:-1], by_off[1:]], axis=-1)


def _rotary_tables(t):
    half = HD // 2
    inv = ROPE_THETA ** (-jnp.arange(half, dtype=F32) / half)
    ang = jnp.arange(t, dtype=F32)[:, None] * inv[None, :]
    cos, sin = jnp.cos(ang), jnp.sin(ang)
    return jnp.tile(jnp.concatenate([cos, cos], axis=-1), (1, NH)), jnp.tile(jnp.concatenate([-sin, sin], axis=-1), (1, NH))


def _block_diag(pw):
    out = jnp.zeros((BW, BW), pw.dtype)
    for g in range(NH):
        out = lax.dynamic_update_slice(out, pw[g], (g * HD, g * HD))
    return out


def _tile4(g):
    return jnp.tile(g.reshape(1, HD), (1, NH))


def _layer_fwd(x, mem, sw, lw, consts, fetch, h=None, next_norm_g=None):
    cos2, sin2, onehot, maskadd = consts
    if h is None:
        h = _rmsnorm_fwd(x, sw["norm_mix_g"].reshape(1, D), "norm_mix_fwd")
    proj = _mm(h, lw["w_in"], tb=True, name="mm_in")
    gp = _mm(h, lw["w_gate"], tb=True, out_dtype=BF16, name="mm_gate")
    g_naq, g_nak, g_mq = _tile4(sw["na_q_norm_g"]), _tile4(sw["na_k_norm_g"]), _tile4(sw["mem_q_norm_g"])
    rq, rk, rv, nq, nk, nv, mq = _prep_fwd(proj, cos2, sin2, g_naq, g_nak, g_mq)

    lgf, lgb = jax.nn.log_sigmoid(sw["ret_decay_fwd"]), jax.nn.log_sigmoid(sw["ret_decay_bwd"])
    g_ret = sw["ret_norm_g"].reshape(1, BW)
    o_ret, ret = _ret_fwd(rq, rk, rv, proj, lgf, lgb, g_ret)

    wbd = _block_diag(sw["pool_w"]).astype(BF16)
    p_scale = sw["pool_scale"].reshape(1, BW)
    pool = _pool_fwd(proj, wbd, p_scale)

    rpb_pad = jnp.pad(sw["na_rpb"].reshape(NH * 15, 31), ((0, 4), (0, 97)))
    ball = _na_bias_table(_rpb_expand(rpb_pad, onehot), maskadd)
    na = _na_fwd(nq, nk, nv, ball)

    lw.update(fetch(1, na))
    memn = _rmsnorm_fwd(mem, sw["norm_mem_g"].reshape(1, D), "norm_mem_fwd")
    kv = _mm(memn, lw["w_mem_kv"], name="mm_memkv")
    g_mk = _tile4(sw["mem_k_norm_g"])
    mk, mv = _memkv_prep(kv, g_mk)
    mo = _mem_fwd(mq, mk, mv)

    br = (ret, pool, na, mo)
    merged = _merge_fwd(br, lw["w_branch"], gp)
    x1, h2 = _mm(merged, lw["w_out"], add=x, norm_g=sw["norm_ffn_g"].reshape(1, D), name="mm_out")
    lw.update(fetch(2, x1))
    ffa, ffg, yff = _ffn_in_fwd(h2, lw["w_ffn_in"])
    if next_norm_g is None:
        x2, h_next = _mm(yff, lw["w_ffn_out"], add=x1, name="mm_ffn_out"), None
    else:
        x2, h_next = _mm(yff, lw["w_ffn_out"], add=x1, norm_g=next_norm_g.reshape(1, D), name="mm_ffn_out")
    saved = dict(x=x, h=h, proj=proj, gp=gp, rq=rq, rk=rk, rv=rv, nq=nq, nk=nk, nv=nv, mq=mq, o_ret=o_ret, ball=ball, memn=memn,
                 kv=kv, mk=mk, mv=mv, br=br, merged=merged, x1=x1, h2=h2, ffa=ffa, ffg=ffg, yff=yff, lgf=lgf, lgb=lgb, wbd=wbd)
    return x2, h_next, saved


def _layer_bwd(dx2, dx2b, mem, sw, lw, sv, consts, dep=None):
    cos2, sin2, onehot, maskadd = consts
    gb, gs = {}, {}
    d_a, d_g = _ffn_out_bwd(dx2b, lw["w_ffn_out"], sv["ffa"], sv["ffg"], dep)
    gb["w_ffn_out"] = _mm(sv["yff"], dx2b, ta=True, out_dtype=BF16, name="mm_ffn_out_dw")
    dh2 = _mm(d_a, lw["w_ffn_in"], b_half=0, name="mm_ffn_in_dx_a")
    dx1, dx1b, dg = _mm_norm_bwd(d_g, lw["w_ffn_in"], dh2, sv["x1"], sw["norm_ffn_g"].reshape(1, D), dx2, b_half=1,
                                 name="mm_ffn_in_dx_g")
    gs["norm_ffn_g"] = dg.reshape(D)
    dw_a = _mm(d_a, sv["h2"], ta=True, out_dtype=BF16, out_half=(0, None), name="mm_ffn_in_dw_a")
    gb["w_ffn_in"] = _mm(d_g, sv["h2"], ta=True, out_dtype=BF16, out_half=(1, dw_a), name="mm_ffn_in_dw_g")

    dmerged = _mm(dx1b, lw["w_out"], tb=True, name="mm_out_dx")
    gb["w_out"] = _mm(sv["merged"], dx1b, ta=True, out_dtype=BF16, name="mm_out_dw")
    dgp, dbr, gb["w_branch"] = _merge_bwd(dmerged, sv["br"], lw["w_branch"], sv["gp"])

    g_ret = sw["ret_norm_g"].reshape(1, BW)
    do_ret, d_rg, dg_ret = _ret_post_bwd(dbr, sv["o_ret"], sv["proj"], g_ret)
    d_rq, d_rk, d_rv, dlg = _ret_bwd(do_ret, sv["rq"], sv["rk"], sv["rv"], sv["lgf"], sv["lgb"])
    gs["ret_norm_g"] = dg_ret.reshape(BW)
    _, vjp_f = jax.vjp(jax.nn.log_sigmoid, sw["ret_decay_fwd"])
    _, vjp_b = jax.vjp(jax.nn.log_sigmoid, sw["ret_decay_bwd"])
    gs["ret_decay_fwd"] = vjp_f(dlg[0:NH, 0])[0]
    gs["ret_decay_bwd"] = vjp_b(dlg[NH:2 * NH, 0])[0]

    p_scale = sw["pool_scale"].reshape(1, BW)
    d_pv, dwbd, dscale = _pool_bwd(dbr, sv["proj"], sv["wbd"], p_scale)
    gs["pool_w"] = jnp.stack([dwbd[g * HD:(g + 1) * HD, g * HD:(g + 1) * HD] for g in range(NH)])
    gs["pool_scale"] = dscale.reshape(BW)

    d_nq, d_nk, d_nv, dball = _na_bwd(dbr, sv["nq"], sv["nk"], sv["nv"], sv["ball"])
    _, vjp_tab = jax.vjp(lambda tab: _na_bias_table(tab, maskadd), jnp.zeros((64, GRID_W * GRID_W), F32))
    drpb = _rpb_reduce(vjp_tab(dball)[0], onehot)
    gs["na_rpb"] = drpb[:NH * 15, :31].reshape(NH, 15, 31)

    d_mq, d_mk, d_mv = _mem_bwd(dbr, sv["mq"], sv["mk"], sv["mv"])
    g_mk = _tile4(sw["mem_k_norm_g"])
    dkv, dg_mk = _memkv_bwd(sv["kv"], d_mk, d_mv, g_mk)
    gs["mem_k_norm_g"] = dg_mk.reshape(NH, HD).sum(0)
    gb["w_mem_kv"] = _mm(sv["memn"], dkv, ta=True, out_dtype=BF16, name="mm_memkv_dw")
    dmemn = _mm(dkv, lw["w_mem_kv"], tb=True, name="mm_memkv_dx")
    _, _, dg_mem = _rmsnorm_bwd(dmemn, mem, sw["norm_mem_g"].reshape(1, D), jnp.zeros_like(mem), "norm_mem_bwd")
    gs["norm_mem_g"] = dg_mem.reshape(D)

    g_naq, g_nak, g_mq = _tile4(sw["na_q_norm_g"]), _tile4(sw["na_k_norm_g"]), _tile4(sw["mem_q_norm_g"])
    dproj, dg_naq, dg_nak, dg_mq = _prep_bwd(sv["proj"], cos2, sin2, g_naq, g_nak, g_mq, d_rq, d_rk, d_rv, d_rg, d_pv, d_nq, d_nk,
                                             d_nv, d_mq)
    gs["na_q_norm_g"] = dg_naq.reshape(NH, HD).sum(0)
    gs["na_k_norm_g"] = dg_nak.reshape(NH, HD).sum(0)
    gs["mem_q_norm_g"] = dg_mq.reshape(NH, HD).sum(0)

    gb["w_in"] = _mm(dproj, sv["h"], ta=True, out_dtype=BF16, name="mm_in_dw")
    gb["w_gate"] = _mm(dgp, sv["h"], ta=True, out_dtype=BF16, name="mm_gate_dw")
    dh = _mm(dproj, lw["w_in"], name="mm_in_dx")
    dx, dxb, dg = _mm_norm_bwd(dgp, lw["w_gate"], dh, sv["x"], sw["norm_mix_g"].reshape(1, D), dx1, name="mm_gate_dx")
    gs["norm_mix_g"] = dg.reshape(D)
    return dx, dxb, gb, gs


def _local_step(x, mem, target, small, get_layer, on_grads):
    t = x.shape[0]
    cos2, sin2 = _rotary_tables(t)
    onehot, maskadd = _na_constants()
    consts = (cos2, sin2, jnp.asarray(onehot), jnp.asarray(maskadd))
    saved, weights, cur, h = [], [], x, None
    for l in range(DEPTH):
        sw = {n: small[n][l] for n in SMALL}
        lw, fetch = get_layer(l, cur)
        weights.append(lw)
        cur, h, sv = _layer_fwd(cur, mem, sw, lw, consts, fetch, h, small["norm_mix_g"][l + 1] if l + 1 < DEPTH else None)
        saved.append(sv)
    dy, dyb, loss_tile = _loss_head(cur, target)
    small_g = {n: [None] * DEPTH for n in SMALL}
    dep = None
    for l in reversed(range(DEPTH)):
        sw = {n: small[n][l] for n in SMALL}
        dy, dyb, gb, gs = _layer_bwd(dy, dyb, mem, sw, weights[l], saved[l], consts, dep)
        dep = on_grads(l, gb, dy)
        for n in SMALL:
            small_g[n][l] = gs[n]
    return loss_tile[0, 0], dy, {n: jnp.stack(v) for n, v in small_g.items()}


def _flat2d(a):
    return a.reshape(-1, a.shape[-1])


def kernel(x, mem, norm_mix_g, norm_mem_g, w_in, w_gate, ret_decay_fwd, ret_decay_bwd, ret_norm_g, pool_w, pool_scale, na_q_norm_g, na_k_norm_g, na_rpb, mem_q_norm_g, mem_k_norm_g, w_mem_kv, w_branch, w_out, norm_ffn_g, w_ffn_in, w_ffn_out, loss_target, m_norm_mix_g, m_norm_mem_g, m_w_in, m_w_gate, m_ret_decay_fwd, m_ret_decay_bwd, m_ret_norm_g, m_pool_w, m_pool_scale, m_na_q_norm_g, m_na_k_norm_g, m_na_rpb, m_mem_q_norm_g, m_mem_k_norm_g, m_w_mem_kv, m_w_branch, m_w_out, m_norm_ffn_g, m_w_ffn_in, m_w_ffn_out, v_norm_mix_g, v_norm_mem_g, v_w_in, v_w_gate, v_ret_decay_fwd, v_ret_decay_bwd, v_ret_norm_g, v_pool_w, v_pool_scale, v_na_q_norm_g, v_na_k_norm_g, v_na_rpb, v_mem_q_norm_g, v_mem_k_norm_g, v_w_mem_kv, v_w_branch, v_w_out, v_norm_ffn_g, v_w_ffn_in, v_w_ffn_out):
    w = dict(norm_mix_g=norm_mix_g, norm_mem_g=norm_mem_g, w_in=w_in, w_gate=w_gate, ret_decay_fwd=ret_decay_fwd,
             ret_decay_bwd=ret_decay_bwd, ret_norm_g=ret_norm_g, pool_w=pool_w, pool_scale=pool_scale, na_q_norm_g=na_q_norm_g,
             na_k_norm_g=na_k_norm_g, na_rpb=na_rpb, mem_q_norm_g=mem_q_norm_g, mem_k_norm_g=mem_k_norm_g, w_mem_kv=w_mem_kv,
             w_branch=w_branch, w_out=w_out, norm_ffn_g=norm_ffn_g, w_ffn_in=w_ffn_in, w_ffn_out=w_ffn_out)
    m = dict(norm_mix_g=m_norm_mix_g, norm_mem_g=m_norm_mem_g, w_in=m_w_in, w_gate=m_w_gate, ret_decay_fwd=m_ret_decay_fwd,
             ret_decay_bwd=m_ret_decay_bwd, ret_norm_g=m_ret_norm_g, pool_w=m_pool_w, pool_scale=m_pool_scale, na_q_norm_g=m_na_q_norm_g,
             na_k_norm_g=m_na_k_norm_g, na_rpb=m_na_rpb, mem_q_norm_g=m_mem_q_norm_g, mem_k_norm_g=m_mem_k_norm_g, w_mem_kv=m_w_mem_kv,
             w_branch=m_w_branch, w_out=m_w_out, norm_ffn_g=m_norm_ffn_g, w_ffn_in=m_w_ffn_in, w_ffn_out=m_w_ffn_out)
    v = dict(norm_mix_g=v_norm_mix_g, norm_mem_g=v_norm_mem_g, w_in=v_w_in, w_gate=v_w_gate, ret_decay_fwd=v_ret_decay_fwd,
             ret_decay_bwd=v_ret_decay_bwd, ret_norm_g=v_ret_norm_g, pool_w=v_pool_w, pool_scale=v_pool_scale, na_q_norm_g=v_na_q_norm_g,
             na_k_norm_g=v_na_k_norm_g, na_rpb=v_na_rpb, mem_q_norm_g=v_mem_q_norm_g, mem_k_norm_g=v_mem_k_norm_g, w_mem_kv=v_w_mem_kv,
             w_branch=v_w_branch, w_out=v_w_out, norm_ffn_g=v_norm_ffn_g, w_ffn_in=v_w_ffn_in, w_ffn_out=v_w_ffn_out)
    assert x.shape == (1, 2048, D) and mem.shape == (1, N_MEM, D) and w_in.shape == (DEPTH, D, 9 * BW // N_DEV)

    blocks = [_to_exchange(name, tr, w[name][l]).astype(BF16) for l in range(DEPTH) for name, tr in BIG]
    started = _ici_start(blocks, [lax.empty((N_DEV,) + b.shape, BF16) for b in blocks], "gather", "gather_ici_start", len(BIG))

    def get_group(l, only, after, tag):
        at = [l * len(BIG) + i for i in only]
        lands = _ici_wait(started, after, "gather_ici_wait_%d%s" % (l, tag), at)
        whole = _gather_d2d([started[2][i] for i in at], lands, "gather_d2d")
        return {BIG[i][0]: _whole_from_gathered(BIG[i][0], g) for i, g in zip(only, whole)}

    def get_layer(l, after):
        if l > 0:
            return get_group(l, list(range(len(BIG))), after, ""), lambda stage, after2: {}
        groups = [[0, 1], [2, 3, 4], [5, 6]]
        return get_group(l, groups[0], started[4], "a"), lambda stage, after2: get_group(l, groups[stage], after2, "abc"[stage])

    cidx = lax.axis_index("c").astype(jnp.int32).reshape(1)
    chip = (2 * lax.axis_index("x") + lax.axis_index("y")).astype(jnp.int32).reshape(1)
    in_flight = []

    def flip_of(name, tr):
        return (lambda a: jnp.swapaxes(a, 1, 2)) if name in ("w_in", "w_ffn_in") else (lambda a: a)

    def rows3(a):
        return a.reshape(DEPTH, -1, a.shape[-1])

    opt_in = {name: tuple(rows3(flip_of(name, tr)(t[name])) for t in (w, m, v)) for name, tr in BIG}
    opt_out = {name: tuple(lax.empty(opt_in[name][0].shape, F32) for _ in range(4)) for name, _ in BIG}

    device = (2 * chip + cidx).astype(jnp.int32)

    def finish(l, st, after):
        recv = _ici_wait(st, after, "rs_ici_wait_%d" % l)
        sums = _sum_own(st[2], recv, chip if st[5] == 3 else device, "rs_sum")
        for (name, tr), s in zip(BIG, sums):
            g = s if name in ("w_in", "w_ffn_in") else _from_exchange(name, tr, s)
            wx, mx, vx = opt_in[name]
            opt_out[name] = _adamw_layer(l, wx, g.reshape(-1, g.shape[-1]), mx, vx, opt_out[name], "adamw_" + name)

    def on_grads(l, gb, after):
        send = [_by_destination(name, gb[name]) for name, _ in BIG]
        if l > 0:
            send = [s.reshape((N_DEV,) + s.shape[2:]) for s in send]
            st = _ici_start(send, [lax.empty(s.shape, BF16) for s in send], "by_device", "rs_ici_start_%d" % l)
        else:
            from_core = _rs_core_swap(send, "rs_core_swap")
            chip_part = _pair_sum(send, from_core, cidx)
            st = _ici_start(chip_part, [lax.empty(p.shape, BF16) for p in chip_part], "by_chip", "rs_ici_start_%d" % l)
        in_flight.append((l, st))
        return st[4]

    loss_local, dx, small_g = _local_step(x[0], mem[0], loss_target[0], {n: w[n] for n in SMALL}, get_layer, on_grads)

    last_started = in_flight[-1][1][4]
    for l, st in in_flight[:-1]:
        finish(l, st, last_started)

    small_all, = _all_gather([_pack_small(small_g, loss_local) + last_started[0:1]], "gather_small")
    packed_g = _sum_slots(small_all, "small_sum")
    small_sum, loss = _unpack_small(packed_g, {n: w[n] for n in SMALL})
    d_, m_, v_ = _adamw(_pack_small({n: w[n] for n in SMALL}), packed_g, _pack_small({n: m[n] for n in SMALL}),
                        _pack_small({n: v[n] for n in SMALL}), "adamw_small")
    updated = d_[0:8]
    for name, _ in BIG:
        updated = updated + opt_out[name][0][1, 0:8, 0:128]
    finish(*in_flight[-1], updated)

    grads, delta, new_m, new_v = {}, {}, {}, {}
    for name, tr in BIG:
        shape = flip_of(name, tr)(w[name]).shape
        delta[name], new_m[name], new_v[name], grads[name] = (flip_of(name, tr)(a.reshape(shape)) for a in opt_out[name])
    like = {n: w[n] for n in SMALL}
    ds, _ = _unpack_small(d_, like)
    ms, _ = _unpack_small(m_, like)
    vs, _ = _unpack_small(v_, like)
    for n in SMALL:
        grads[n], delta[n], new_m[n], new_v[n] = small_sum[n], ds[n], ms[n], vs[n]

    return (loss, dx[None], *[grads[n] for n in WEIGHTS], *[delta[n] for n in WEIGHTS], *[new_m[n] for n in WEIGHTS],
            *[new_v[n] for n in WEIGHTS])
```

```python
import functools

import numpy as np
import jax
import jax.numpy as jnp
from jax import lax
from jax.experimental import pallas as pl
from jax.experimental.pallas import tpu as pltpu

F32 = jnp.float32
BF16 = jnp.bfloat16
MXU = jnp.bfloat16
HI = lax.Precision.HIGHEST

DEPTH = 4
D = 1024
BW = 256
HD = 64
NH = 4
GRID_W = 64
NA_ROWS_WIN = 8
NA_COLS_WIN = 16
N_MEM = 256
FF = 2816
EPS = 1e-6
NEG = -1e30
ROPE_THETA = 10000.0
POOL_HALF_MAX = 8

ADAM_LR, ADAM_B1, ADAM_B2, ADAM_EPS, ADAM_WD, ADAM_STEP = 0.001, 0.9, 0.999, 1e-08, 0.01, 10

N_DEV = 8
VMEM_LIMIT = 56 * 1024 * 1024
MM_VMEM_BUDGET = 40 * 1024 * 1024

RQ, RK, RV, RG, PV, NQ, NK, NV, MQ = range(9)

MESH = pl.DeviceIdType.MESH
ANY = pl.BlockSpec(memory_space=pl.ANY)
SMEM = pl.BlockSpec(memory_space=pltpu.SMEM)


def _cp(**kw):
    return pltpu.CompilerParams(vmem_limit_bytes=VMEM_LIMIT, **kw)


def _tile(n, cap):
    if n <= cap:
        return n
    best = None
    for t in range(128, cap + 1, 128):
        if n % t == 0:
            best = t
    assert best is not None, (n, cap)
    return best


def _sds(shape, dtype):
    return jax.ShapeDtypeStruct(shape, dtype)


def _lane_head(shape):
    return lax.shift_right_logical(lax.broadcasted_iota(jnp.int32, shape, len(shape) - 1), 6)


def _group_mean(z):
    i = lax.shift_right_logical(lax.broadcasted_iota(jnp.int32, (BW, BW), 0), 6)
    j = lax.shift_right_logical(lax.broadcasted_iota(jnp.int32, (BW, BW), 1), 6)
    g = jnp.where(i == j, 1.0 / HD, 0.0).astype(BF16)
    z_hi = z.astype(BF16)
    z_lo = (z - z_hi.astype(F32)).astype(BF16)
    return jnp.dot(z_hi, g, preferred_element_type=F32) + jnp.dot(z_lo, g, preferred_element_type=F32)


def _gnorm(t, g):
    r = lax.rsqrt(_group_mean(t * t) + EPS)
    return t * r * g


def _gnorm_bwd(dy, t, g):
    r = lax.rsqrt(_group_mean(t * t) + EPS)
    th = t * r
    dth = dy * g
    dt = r * (dth - th * _group_mean(dth * th))
    return dt, dy * th


def _swap_halves(t):
    lane = lax.broadcasted_iota(jnp.int32, t.shape, 1)
    return jnp.where((lane & 63) < 32, pltpu.roll(t, BW - 32, 1), pltpu.roll(t, 32, 1))


def _sigmoid(x):
    return 1.0 / (1.0 + jnp.exp(-x))


def _dot(a, b, ta=False, tb=False):
    return lax.dot_general(a.astype(MXU), b.astype(MXU), (((0 if ta else 1,), (1 if tb else 0,)), ((), ())),
                           preferred_element_type=F32)


def _stack_heads(t):
    head = _lane_head(t.shape)
    return jnp.concatenate([jnp.where(head == h, t, jnp.zeros_like(t)) for h in range(NH)], axis=0)


def _unstack_heads(t, rows):
    head = _lane_head((rows, BW))
    out = jnp.zeros((rows, BW), F32)
    for h in range(NH):
        out = out + jnp.where(head == h, t[h * rows:(h + 1) * rows], 0.0)
    return out


def _softmax_rows(s):
    m = jnp.max(s, axis=-1, keepdims=True)
    e = jnp.exp(s - m)
    return e / jnp.sum(e, axis=-1, keepdims=True)


def _acc(ref, val, first):
    @pl.when(first)
    def _():
        ref[...] = val

    @pl.when(jnp.logical_not(first))
    def _():
        ref[...] += val


def _mm(a, b, *, ta=False, tb=False, out_dtype=F32, add=None, dep=None, b_half=None, out_half=None, norm_g=None, name):
    m, k = (a.shape[1], a.shape[0]) if ta else a.shape
    n = b.shape[0] if tb else b.shape[1]
    assert b_half is None or (not tb and b.shape[0] == 2 * k)
    tm, tn = _tile(m, 1408), (n if norm_g is not None else _tile(n, 768))
    if not ta and m <= 2048:
        blocks = (m * k * a.dtype.itemsize + k * tn * b.dtype.itemsize + m * tn * jnp.dtype(out_dtype).itemsize
                  + (m * tn * 4 if add is not None else 0) + (m * tn * 2 if norm_g is not None else 0))
        if 2 * blocks <= MM_VMEM_BUDGET:
            tm = m
    n_in = 2 + (add is not None) + (dep is not None) + (out_half is not None) + (norm_g is not None)

    def body(*refs):
        a_ref, b_ref, o_ref = refs[0], refs[1], refs[n_in]
        r = _dot(a_ref[...], b_ref[...], ta, tb)
        if add is not None:
            r = r + refs[2][...]
        o_ref[...] = r.astype(out_dtype)
        if norm_g is not None:
            scale = lax.rsqrt(jnp.mean(r * r, axis=-1, keepdims=True) + EPS)
            refs[n_in + 1][...] = (r * scale * refs[n_in - 1][...]).astype(BF16)

    kb = 0 if b_half is None else b_half
    a_spec = pl.BlockSpec((k, tm), lambda i, j: (0, i)) if ta else pl.BlockSpec((tm, k), lambda i, j: (i, 0))
    b_spec = pl.BlockSpec((tn, k), lambda i, j: (j, 0)) if tb else pl.BlockSpec((k, tn), lambda i, j: (kb, j))
    plain = pl.BlockSpec((tm, tn), lambda i, j: (i, j))
    ins, args = [a_spec, b_spec], [a, b]
    if add is not None:
        ins.append(plain)
        args.append(add)
    if dep is not None:
        ins.append(pl.BlockSpec((8, 128), lambda i, j: (0, 0)))
        args.append(dep)
    o_spec, o_shape, aliases = plain, _sds((m, n), out_dtype), {}
    if out_half is not None:
        half, prev = out_half
        o_spec = pl.BlockSpec((tm, tn), lambda i, j: (i + half * (m // tm), j))
        o_shape = _sds((2 * m, n), out_dtype)
        ins.append(ANY)
        args.append(lax.empty((2 * m, n), out_dtype) if prev is None else prev)
        aliases = {len(args) - 1: 0}
    if norm_g is not None:
        ins.append(pl.BlockSpec((1, n), lambda i, j: (0, 0)))
        args.append(norm_g)
        o_spec, o_shape = (o_spec, plain), (o_shape, _sds((m, n), BF16))
    return pl.pallas_call(
        body, grid=(m // tm, n // tn), in_specs=ins, out_specs=o_spec, out_shape=o_shape, input_output_aliases=aliases, name=name,
        compiler_params=_cp(dimension_semantics=("parallel", "parallel")))(*args)


def _mm_norm_bwd(a, b, add, x, g, res, *, b_half=None, name):
    m, k = a.shape
    n = b.shape[1]
    tm = 512
    kb = 0 if b_half is None else b_half

    def body(a_ref, b_ref, c_ref, x_ref, g_ref, res_ref, dx_ref, dxb_ref, dg_ref):
        dhv = _dot(a_ref[...], b_ref[...]) + c_ref[...]
        xv = x_ref[...]
        r = lax.rsqrt(jnp.mean(xv * xv, axis=-1, keepdims=True) + EPS)
        xh = xv * r
        dxh = dhv * g_ref[...]
        dx = res_ref[...] + r * (dxh - xh * jnp.mean(dxh * xh, axis=-1, keepdims=True))
        dx_ref[...] = dx
        dxb_ref[...] = dx.astype(BF16)
        _acc(dg_ref, jnp.sum(dhv * xh, axis=0, keepdims=True), pl.program_id(0) == 0)

    row = pl.BlockSpec((tm, n), lambda i: (i, 0))
    vec = pl.BlockSpec((1, n), lambda i: (0, 0))
    return pl.pallas_call(
        body, grid=(m // tm,),
        in_specs=[pl.BlockSpec((tm, k), lambda i: (i, 0)), pl.BlockSpec((k, n), lambda i: (kb, 0)), row, row, vec, row],
        out_specs=(row, row, vec), out_shape=(_sds((m, n), F32), _sds((m, n), BF16), _sds((1, n), F32)), name=name,
        compiler_params=_cp())(a, b, add, x, g, res)


def _rmsnorm_fwd(x, g, name):
    t, d = x.shape
    tm = _tile(t, 256)

    def body(x_ref, g_ref, o_ref):
        xv = x_ref[...]
        r = lax.rsqrt(jnp.mean(xv * xv, axis=-1, keepdims=True) + EPS)
        o_ref[...] = (xv * r * g_ref[...]).astype(o_ref.dtype)

    return pl.pallas_call(
        body, grid=(t // tm,), in_specs=[pl.BlockSpec((tm, d), lambda i: (i, 0)), pl.BlockSpec((1, d), lambda i: (0, 0))],
        out_specs=pl.BlockSpec((tm, d), lambda i: (i, 0)), out_shape=_sds((t, d), BF16), name=name, compiler_params=_cp())(x, g)


def _rmsnorm_bwd(dh, x, g, res, name):
    t, d = x.shape
    tm = _tile(t, 256)

    def body(dh_ref, x_ref, g_ref, res_ref, dx_ref, dxb_ref, dg_ref):
        xv = x_ref[...]
        dhv = dh_ref[...]
        r = lax.rsqrt(jnp.mean(xv * xv, axis=-1, keepdims=True) + EPS)
        xh = xv * r
        dxh = dhv * g_ref[...]
        dx = res_ref[...] + r * (dxh - xh * jnp.mean(dxh * xh, axis=-1, keepdims=True))
        dx_ref[...] = dx
        dxb_ref[...] = dx.astype(BF16)
        _acc(dg_ref, jnp.sum(dhv * xh, axis=0, keepdims=True), pl.program_id(0) == 0)

    row = pl.BlockSpec((tm, d), lambda i: (i, 0))
    vec = pl.BlockSpec((1, d), lambda i: (0, 0))
    return pl.pallas_call(
        body, grid=(t // tm,), in_specs=[row, row, vec, row], out_specs=(row, row, vec),
        out_shape=(_sds((t, d), F32), _sds((t, d), BF16), _sds((1, d), F32)), name=name, compiler_params=_cp())(dh, x, g, res)


def _prep_fwd(proj, cos2, sin2, g_naq, g_nak, g_mq):
    t = proj.shape[0]
    tm = 256

    def body(p_ref, cos_ref, sin_ref, gq_ref, gk_ref, gm_ref, rq_ref, rk_ref, rv_ref, nq_ref, nk_ref, nv_ref, mq_ref):
        def col(c):
            return p_ref[:, c * BW:(c + 1) * BW]

        cosv, sinv = cos_ref[...], sin_ref[...]

        def rot(tv):
            return tv * cosv + _swap_halves(tv) * sinv

        rq_ref[...] = (rot(col(RQ)) * (HD ** -0.5)).astype(BF16)
        rk_ref[...] = rot(col(RK)).astype(BF16)
        rv_ref[...] = col(RV).astype(BF16)
        nq_ref[...] = _gnorm(col(NQ), gq_ref[...]).astype(BF16)
        nk_ref[...] = _gnorm(col(NK), gk_ref[...]).astype(BF16)
        nv_ref[...] = col(NV).astype(BF16)
        mq_ref[...] = _gnorm(col(MQ), gm_ref[...]).astype(BF16)

    blk = pl.BlockSpec((tm, BW), lambda i: (i, 0))
    vec = pl.BlockSpec((1, BW), lambda i: (0, 0))
    return pl.pallas_call(
        body, grid=(t // tm,), in_specs=[pl.BlockSpec((tm, 9 * BW), lambda i: (i, 0)), blk, blk, vec, vec, vec],
        out_specs=tuple(blk for _ in range(7)), out_shape=tuple(_sds((t, BW), BF16) for _ in range(7)),
        name="prep_fwd", compiler_params=_cp())(proj, cos2, sin2, g_naq, g_nak, g_mq)


def _prep_bwd(proj, cos2, sin2, g_naq, g_nak, g_mq, d_rq, d_rk, d_rv, d_rg, d_pv, d_nq, d_nk, d_nv, d_mq):
    t = proj.shape[0]
    tm = 256

    def body(p_ref, cos_ref, sin_ref, gq_ref, gk_ref, gm_ref, drq_ref, drk_ref, drv_ref, drg_ref, dpv_ref, dnq_ref, dnk_ref,
             dnv_ref, dmq_ref, o_ref, dgq_ref, dgk_ref, dgm_ref):
        first = pl.program_id(0) == 0

        def col(c):
            return p_ref[:, c * BW:(c + 1) * BW]

        def put(c, v):
            o_ref[:, c * BW:(c + 1) * BW] = v.astype(BF16)

        cosv, sinv = cos_ref[...], sin_ref[...]

        def rot_t(dv):
            return dv * cosv + _swap_halves(dv * sinv)

        put(RQ, rot_t(drq_ref[...] * (HD ** -0.5)))
        put(RK, rot_t(drk_ref[...]))
        put(RV, drv_ref[...])
        put(RG, drg_ref[...])
        put(PV, dpv_ref[...])
        dq, gq = _gnorm_bwd(dnq_ref[...], col(NQ), gq_ref[...])
        put(NQ, dq)
        _acc(dgq_ref, jnp.sum(gq, axis=0, keepdims=True), first)
        dk, gk = _gnorm_bwd(dnk_ref[...], col(NK), gk_ref[...])
        put(NK, dk)
        _acc(dgk_ref, jnp.sum(gk, axis=0, keepdims=True), first)
        put(NV, dnv_ref[...])
        dm, gm = _gnorm_bwd(dmq_ref[...], col(MQ), gm_ref[...])
        put(MQ, dm)
        _acc(dgm_ref, jnp.sum(gm, axis=0, keepdims=True), first)

    blk = pl.BlockSpec((tm, BW), lambda i: (i, 0))
    vec = pl.BlockSpec((1, BW), lambda i: (0, 0))
    wide = pl.BlockSpec((tm, 9 * BW), lambda i: (i, 0))
    return pl.pallas_call(
        body, grid=(t // tm,), in_specs=[wide, blk, blk, vec, vec, vec] + [blk] * 9, out_specs=(wide, vec, vec, vec),
        out_shape=(_sds((t, 9 * BW), BF16), _sds((1, BW), F32), _sds((1, BW), F32), _sds((1, BW), F32)),
        name="prep_bwd", compiler_params=_cp())(proj, cos2, sin2, g_naq, g_nak, g_mq, d_rq, d_rk, d_rv, d_rg, d_pv, d_nq, d_nk,
                                                d_nv, d_mq)


RET_B = 256


def _ret_consts(lgf_ref, lgb_ref):
    bsz = RET_B
    head = _lane_head((1, BW))
    lf, lb = jnp.zeros((1, BW), F32), jnp.zeros((1, BW), F32)
    for h in range(NH):
        lf = lf + jnp.where(head == h, lgf_ref[h], 0.0)
        lb = lb + jnp.where(head == h, lgb_ref[h], 0.0)
    pos = lax.broadcasted_iota(jnp.int32, (bsz, BW), 0).astype(F32)
    up, down = pos + 1.0, (bsz - 1.0) - pos
    c = dict(up=up, down=down, kf=jnp.exp(down * lf), kb=jnp.exp(up * lb), qf=jnp.exp(up * lf), qb=jnp.exp(down * lb),
             cf=jnp.exp(bsz * lf), cb=jnp.exp(bsz * lb))
    diff = (lax.broadcasted_iota(jnp.int32, (NH * bsz, 1), 0) & (bsz - 1)) - lax.broadcasted_iota(jnp.int32, (1, bsz), 1)
    c["causal"] = diff >= 0
    c["dist"] = jnp.abs(diff).astype(F32)
    lgf = jnp.concatenate([jnp.full((bsz, 1), lgf_ref[h], F32) for h in range(NH)], axis=0)
    lgb = jnp.concatenate([jnp.full((bsz, 1), lgb_ref[h], F32) for h in range(NH)], axis=0)
    c["dm"] = jnp.exp(c["dist"] * jnp.where(c["causal"], lgf, lgb))
    c["bd"] = _lane_head((BW, BW)) == lax.shift_right_logical(lax.broadcasted_iota(jnp.int32, (BW, BW), 0), 6)
    return c


def _ret_states(k_ref, v_ref, st_ref, c, nb):
    bsz = RET_B

    def summary(b, decay):
        kb = k_ref[b * bsz:(b + 1) * bsz, :].astype(F32)
        return jnp.where(c["bd"], _dot(kb * decay, v_ref[b * bsz:(b + 1) * bsz, :], ta=True), 0.0)

    f = jnp.zeros((BW, BW), F32)
    for b in range(nb):
        st_ref[b] = f
        if b < nb - 1:
            f = c["cf"] * f + summary(b, c["kf"])
    g = jnp.zeros((BW, BW), F32)
    for b in reversed(range(nb)):
        st_ref[nb + b] = g
        if b > 0:
            g = c["cb"] * g + summary(b, c["kb"])


def _ret_fwd(q, k, v, proj, lgf, lgb, g_ret):
    t = q.shape[0]
    bsz, nb = RET_B, t // RET_B

    def body(lgf_ref, lgb_ref, q_ref, k_ref, v_ref, rg_ref, g_ref, o_ref, ret_ref, st_ref):
        c = _ret_consts(lgf_ref, lgb_ref)
        _ret_states(k_ref, v_ref, st_ref, c, nb)
        for b in range(nb):
            blk = slice(b * bsz, (b + 1) * bsz)
            qb, kb, vb = q_ref[blk, :], k_ref[blk, :], v_ref[blk, :]
            s = _dot(_stack_heads(qb), kb, tb=True)
            o = _unstack_heads(_dot(s * c["dm"], vb), bsz)
            q32 = qb.astype(F32)
            o = o + _dot(q32 * c["qf"], st_ref[b]) + _dot(q32 * c["qb"], st_ref[nb + b])
            o_ref[blk, :] = o
            rg = rg_ref[blk, :]
            ret_ref[blk, :] = (_gnorm(o, g_ref[...]) * (rg * _sigmoid(rg))).astype(BF16)

    whole = pl.BlockSpec((t, BW), lambda i: (0, 0))
    return pl.pallas_call(
        body, grid=(1,),
        in_specs=[SMEM, SMEM, whole, whole, whole, pl.BlockSpec((t, BW), lambda i: (0, RG)), pl.BlockSpec((1, BW), lambda i: (0, 0))],
        out_specs=(whole, whole), out_shape=(_sds((t, BW), F32), _sds((t, BW), BF16)),
        scratch_shapes=[pltpu.VMEM((2 * nb, BW, BW), F32)], name="ret_fwd", compiler_params=_cp())(lgf, lgb, q, k, v, proj, g_ret)


def _ret_post_bwd(dbr, o_ret, proj, g_ret):
    t = o_ret.shape[0]
    tm = 256

    def body(d_ref, o_ref, rg_ref, g_ref, do_ref, drg_ref, dg_ref):
        dret, o, rg, g = d_ref[...], o_ref[...], rg_ref[...], g_ref[...]
        sg = _sigmoid(rg)
        do, dgain = _gnorm_bwd(dret * (rg * sg), o, g)
        do_ref[...] = do.astype(BF16)
        drg_ref[...] = dret * _gnorm(o, g) * (sg * (1.0 + rg * (1.0 - sg)))
        _acc(dg_ref, jnp.sum(dgain, axis=0, keepdims=True), pl.program_id(0) == 0)

    blk = pl.BlockSpec((tm, BW), lambda i: (i, 0))
    vec = pl.BlockSpec((1, BW), lambda i: (0, 0))
    return pl.pallas_call(
        body, grid=(t // tm,), in_specs=[blk, blk, pl.BlockSpec((tm, BW), lambda i: (i, RG)), vec], out_specs=(blk, blk, vec),
        out_shape=(_sds((t, BW), BF16), _sds((t, BW), F32), _sds((1, BW), F32)), name="ret_post_bwd",
        compiler_params=_cp())(dbr, o_ret, proj, g_ret)


def _ret_bwd(do, q, k, v, lgf, lgb):
    t = q.shape[0]
    bsz, nb = RET_B, t // RET_B

    def body(lgf_ref, lgb_ref, d_ref, q_ref, k_ref, v_ref, dq_ref, dk_ref, dv_ref, dlg_ref, st_ref, sd_ref):
        c = _ret_consts(lgf_ref, lgb_ref)
        _ret_states(k_ref, v_ref, st_ref, c, nb)
        lane_f, lane_b = jnp.zeros((1, BW), F32), jnp.zeros((1, BW), F32)
        row_f, row_b = jnp.zeros((NH * bsz, 1), F32), jnp.zeros((NH * bsz, 1), F32)

        def rows(x):
            return jnp.sum(x, axis=0, keepdims=True)

        for b in range(nb):
            blk = slice(b * bsz, (b + 1) * bsz)
            qb, kb, vb, dob = q_ref[blk, :], k_ref[blk, :], v_ref[blk, :], d_ref[blk, :]
            q32 = qb.astype(F32)
            qs, dos = _stack_heads(qb), _stack_heads(dob)
            s = _dot(qs, kb, tb=True)
            da = _dot(dos, vb, tb=True)
            dv_ref[blk, :] = _dot(s * c["dm"], dos, ta=True)
            ds = da * c["dm"]
            w = ds * s * c["dist"]
            row_f = row_f + jnp.sum(jnp.where(c["causal"], w, 0.0), axis=1, keepdims=True)
            row_b = row_b + jnp.sum(jnp.where(c["causal"], 0.0, w), axis=1, keepdims=True)
            dsb = ds.astype(MXU)
            dk_ref[blk, :] = _dot(dsb, qs, ta=True)
            dq_f = _dot(dob, st_ref[b], tb=True) * c["qf"]
            dq_b = _dot(dob, st_ref[nb + b], tb=True) * c["qb"]
            lane_f = lane_f + rows(c["up"] * dq_f * q32)
            lane_b = lane_b + rows(c["down"] * dq_b * q32)
            dq_ref[blk, :] = _unstack_heads(_dot(dsb, kb), bsz) + dq_f + dq_b
            sd_ref[b] = jnp.where(c["bd"], _dot(q32 * c["qf"], dob, ta=True), 0.0)
            sd_ref[nb + b] = jnp.where(c["bd"], _dot(q32 * c["qb"], dob, ta=True), 0.0)

        def through_state(b, grad, decay, weight, lane):
            blk = slice(b * bsz, (b + 1) * bsz)
            k32 = k_ref[blk, :].astype(F32)
            dk = _dot(v_ref[blk, :], grad, tb=True) * decay
            dk_ref[blk, :] += dk
            dv_ref[blk, :] += _dot(k32 * decay, grad)
            return lane + rows(weight * dk * k32)

        phi = jnp.zeros((BW, BW), F32)
        for b in reversed(range(nb)):
            if b < nb - 1:
                lane_f = through_state(b, phi, c["kf"], c["down"], lane_f)
                lane_f = lane_f + bsz * rows(c["cf"] * st_ref[b] * phi)
            phi = sd_ref[b] + c["cf"] * phi
        gam = jnp.zeros((BW, BW), F32)
        for b in range(nb):
            if b > 0:
                lane_b = through_state(b, gam, c["kb"], c["up"], lane_b)
                lane_b = lane_b + bsz * rows(c["cb"] * st_ref[nb + b] * gam)
            gam = sd_ref[nb + b] + c["cb"] * gam

        head = _lane_head((1, BW))
        for h in range(NH):
            tot_f = jnp.sum(row_f[h * bsz:(h + 1) * bsz, :]) + jnp.sum(jnp.where(head == h, lane_f, 0.0))
            tot_b = jnp.sum(row_b[h * bsz:(h + 1) * bsz, :]) + jnp.sum(jnp.where(head == h, lane_b, 0.0))
            dlg_ref[h:h + 1, :] = jnp.full((1, 128), tot_f, F32)
            dlg_ref[NH + h:NH + h + 1, :] = jnp.full((1, 128), tot_b, F32)

    whole = pl.BlockSpec((t, BW), lambda i: (0, 0))
    return pl.pallas_call(
        body, grid=(1,), in_specs=[SMEM, SMEM, whole, whole, whole, whole],
        out_specs=(whole, whole, whole, pl.BlockSpec((2 * NH, 128), lambda i: (0, 0))),
        out_shape=(_sds((t, BW), F32), _sds((t, BW), F32), _sds((t, BW), F32), _sds((2 * NH, 128), F32)),
        scratch_shapes=[pltpu.VMEM((2 * nb, BW, BW), F32), pltpu.VMEM((2 * nb, BW, BW), F32)], name="ret_bwd",
        compiler_params=_cp())(lgf, lgb, do, q, k, v)


def _pool_windows(t):
    row = lax.broadcasted_iota(jnp.int32, (t, BW), 0)
    half = lax.shift_left(jnp.ones((t, BW), jnp.int32), _lane_head((t, BW)))
    cnt = (jnp.minimum(row + half, t) - jnp.maximum(row - half, 0)).astype(F32)
    return row, half, cnt


def _pool_window_sum(v, row, half, t, transpose):
    out = jnp.zeros_like(v)
    for j in range(-POOL_HALF_MAX, POOL_HALF_MAX):
        src = row - j if transpose else row + j
        ok = (src >= 0) & (src < t) & (j >= -half) & (j < half)
        out = out + jnp.where(ok, pltpu.roll(v, (j if transpose else -j) % t, 0), 0.0)
    return out


def _pool_fwd(proj, wbd, scale):
    t = proj.shape[0]

    def body(v_ref, w_ref, s_ref, o_ref):
        v = v_ref[...]
        row, half, cnt = _pool_windows(t)
        pooled = _pool_window_sum(v, row, half, t, False) / cnt - v
        o_ref[...] = (_dot(pooled, w_ref[...]) * s_ref[...]).astype(BF16)

    return pl.pallas_call(
        body, grid=(1,),
        in_specs=[pl.BlockSpec((t, BW), lambda i: (0, PV)), pl.BlockSpec((BW, BW), lambda i: (0, 0)), pl.BlockSpec((1, BW), lambda i: (0, 0))],
        out_specs=pl.BlockSpec((t, BW), lambda i: (0, 0)), out_shape=_sds((t, BW), BF16), name="pool_fwd",
        compiler_params=_cp())(proj, wbd, scale)


def _pool_bwd(dbr, proj, wbd, scale):
    t = proj.shape[0]

    def body(d_ref, v_ref, w_ref, s_ref, dv_ref, dw_ref, ds_ref):
        v, dout = v_ref[...], d_ref[...]
        row, half, cnt = _pool_windows(t)
        pooled = _pool_window_sum(v, row, half, t, False) / cnt - v
        mixed = _dot(pooled, w_ref[...])
        ds_ref[...] = jnp.sum(dout * mixed, axis=0, keepdims=True)
        dmixed = dout * s_ref[...]
        dw_ref[...] = _dot(pooled, dmixed, ta=True)
        dpooled = _dot(dmixed, w_ref[...], tb=True)
        dv_ref[...] = _pool_window_sum(dpooled / cnt, row, half, t, True) - dpooled

    return pl.pallas_call(
        body, grid=(1,),
        in_specs=[pl.BlockSpec((t, BW), lambda i: (0, 1)), pl.BlockSpec((t, BW), lambda i: (0, PV)),
                  pl.BlockSpec((BW, BW), lambda i: (0, 0)), pl.BlockSpec((1, BW), lambda i: (0, 0))],
        out_specs=(pl.BlockSpec((t, BW), lambda i: (0, 0)), pl.BlockSpec((BW, BW), lambda i: (0, 0)), pl.BlockSpec((1, BW), lambda i: (0, 0))),
        out_shape=(_sds((t, BW), F32), _sds((BW, BW), F32), _sds((1, BW), F32)), name="pool_bwd",
        compiler_params=_cp())(dbr, proj, wbd, scale)


NA_KEYS = NA_ROWS_WIN * GRID_W
NA_PAIRS = 2 * NA_ROWS_WIN - 2


def _na_window(r, n_rows):
    rs = jnp.clip(r - NA_ROWS_WIN // 2, 0, n_rows - NA_ROWS_WIN)
    return pl.multiple_of(rs * GRID_W, GRID_W), rs - r + (NA_ROWS_WIN - 1)


def _na_bias(b_ref, a0):
    return jnp.concatenate([b_ref[a0 + 2 * j] for j in range(NA_ROWS_WIN // 2)], axis=1)


NA_STEP_ROWS = 8


def _na_fwd(q, k, v, ball):
    t = q.shape[0]
    n_rows = t // GRID_W
    rows = NA_STEP_ROWS

    def body(q_ref, k_ref, v_ref, b_ref, o_ref):
        for rr in range(rows):
            start, a0 = _na_window(pl.program_id(0) * rows + rr, n_rows)
            own = slice(rr * GRID_W, (rr + 1) * GRID_W)
            qs = _stack_heads(q_ref[own, :])
            s = _dot(qs, k_ref[pl.ds(start, NA_KEYS), :], tb=True) * (HD ** -0.5) + _na_bias(b_ref, a0)
            p = _softmax_rows(s)
            o_ref[own, :] = _unstack_heads(_dot(p, v_ref[pl.ds(start, NA_KEYS), :]), GRID_W).astype(BF16)

    blk = pl.BlockSpec((rows * GRID_W, BW), lambda r: (r, 0))
    whole = pl.BlockSpec((t, BW), lambda r: (0, 0))
    return pl.pallas_call(
        body, grid=(n_rows // rows,), in_specs=[blk, whole, whole, pl.BlockSpec(ball.shape, lambda r: (0, 0, 0))],
        out_specs=blk, out_shape=_sds((t, BW), BF16), name="na_fwd", compiler_params=_cp())(q, k, v, ball)


def _na_bwd(dbr, q, k, v, ball):
    t = q.shape[0]
    n_rows = t // GRID_W

    rows = NA_STEP_ROWS

    def body(d_ref, q_ref, k_ref, v_ref, b_ref, dq_ref, dk_ref, dv_ref, db_ref):
        @pl.when(pl.program_id(0) == 0)
        def _():
            dk_ref[...] = jnp.zeros_like(dk_ref)
            dv_ref[...] = jnp.zeros_like(dv_ref)
            db_ref[...] = jnp.zeros_like(db_ref)

        for rr in range(rows):
            start, a0 = _na_window(pl.program_id(0) * rows + rr, n_rows)
            keys = pl.ds(start, NA_KEYS)
            own = slice(rr * GRID_W, (rr + 1) * GRID_W)
            qs = _stack_heads(q_ref[own, :])
            kb, vb = k_ref[keys, :], v_ref[keys, :]
            p = _softmax_rows(_dot(qs, kb, tb=True) * (HD ** -0.5) + _na_bias(b_ref, a0))
            dos = _stack_heads(d_ref[own, :]).astype(MXU)
            dp = _dot(dos, vb, tb=True)
            dv_ref[keys, :] += _dot(p, dos, ta=True)
            ds = p * (dp - jnp.sum(dp * p, axis=-1, keepdims=True))
            for j in range(NA_ROWS_WIN // 2):
                db_ref[a0 + 2 * j] += ds[:, 2 * j * GRID_W:(2 * j + 2) * GRID_W]
            dsb = (ds * (HD ** -0.5)).astype(MXU)
            dq_ref[own, :] = _unstack_heads(_dot(dsb, kb), GRID_W)
            dk_ref[keys, :] += _dot(dsb, qs, ta=True)

    blk = pl.BlockSpec((rows * GRID_W, BW), lambda r: (r, 0))
    whole = pl.BlockSpec((t, BW), lambda r: (0, 0))
    tab = pl.BlockSpec(ball.shape, lambda r: (0, 0, 0))
    return pl.pallas_call(
        body, grid=(n_rows // rows,), in_specs=[pl.BlockSpec((rows * GRID_W, BW), lambda r: (r, 2)), blk, whole, whole, tab],
        out_specs=(blk, whole, whole, tab),
        out_shape=(_sds((t, BW), F32), _sds((t, BW), F32), _sds((t, BW), F32), _sds(ball.shape, F32)), name="na_bwd",
        compiler_params=_cp())(dbr, q, k, v, ball)


def _rpb_expand(rpb_pad, onehot):
    def body(r_ref, e_ref, o_ref):
        o_ref[...] = jnp.dot(r_ref[...], e_ref[...], precision=HI, preferred_element_type=F32)

    return pl.pallas_call(body, out_shape=_sds((rpb_pad.shape[0], GRID_W * GRID_W), F32), name="rpb_expand",
                          compiler_params=_cp())(rpb_pad, onehot)


def _rpb_reduce(dtab, onehot):
    def body(d_ref, e_ref, o_ref):
        o_ref[...] = lax.dot_general(d_ref[...], e_ref[...], (((1,), (1,)), ((), ())), precision=HI, preferred_element_type=F32)

    return pl.pallas_call(body, out_shape=_sds((dtab.shape[0], 128), F32), name="rpb_reduce", compiler_params=_cp())(dtab, onehot)


MEM_TQ = 256


def _mem_fwd(q, mk, mv):
    t = q.shape[0]
    tq = MEM_TQ

    def body(q_ref, k_ref, v_ref, o_ref):
        p = _softmax_rows(_dot(_stack_heads(q_ref[...]), k_ref[...], tb=True) * (HD ** -0.5))
        o_ref[...] = _unstack_heads(_dot(p, v_ref[...]), tq).astype(BF16)

    blk = pl.BlockSpec((tq, BW), lambda i: (i, 0))
    kv = pl.BlockSpec((N_MEM, BW), lambda i: (0, 0))
    return pl.pallas_call(body, grid=(t // tq,), in_specs=[blk, kv, kv], out_specs=blk, out_shape=_sds((t, BW), BF16),
                          name="mem_fwd", compiler_params=_cp())(q, mk, mv)


def _mem_bwd(dbr, q, mk, mv):
    t = q.shape[0]
    tq = MEM_TQ

    def body(d_ref, q_ref, k_ref, v_ref, dq_ref, dk_ref, dv_ref):
        first = pl.program_id(0) == 0
        qs = _stack_heads(q_ref[...])
        dos = _stack_heads(d_ref[...]).astype(MXU)
        p = _softmax_rows(_dot(qs, k_ref[...], tb=True) * (HD ** -0.5))
        dp = _dot(dos, v_ref[...], tb=True)
        _acc(dv_ref, _dot(p, dos, ta=True), first)
        dsb = (p * (dp - jnp.sum(dp * p, axis=-1, keepdims=True)) * (HD ** -0.5)).astype(MXU)
        dq_ref[...] = _unstack_heads(_dot(dsb, k_ref[...]), tq)
        _acc(dk_ref, _dot(dsb, qs, ta=True), first)

    blk = pl.BlockSpec((tq, BW), lambda i: (i, 0))
    kv = pl.BlockSpec((N_MEM, BW), lambda i: (0, 0))
    return pl.pallas_call(
        body, grid=(t // tq,), in_specs=[pl.BlockSpec((tq, BW), lambda i: (i, 3)), blk, kv, kv], out_specs=(blk, kv, kv),
        out_shape=(_sds((t, BW), F32), _sds((N_MEM, BW), F32), _sds((N_MEM, BW), F32)), name="mem_bwd",
        compiler_params=_cp())(dbr, q, mk, mv)


def _memkv_prep(kv, g_mk):
    def body(kv_ref, g_ref, k_ref, v_ref):
        k_ref[...] = _gnorm(kv_ref[:, 0:BW], g_ref[...]).astype(BF16)
        v_ref[...] = kv_ref[:, BW:2 * BW].astype(BF16)

    return pl.pallas_call(body, out_shape=(_sds((N_MEM, BW), BF16), _sds((N_MEM, BW), BF16)), name="memkv_prep",
                          compiler_params=_cp())(kv, g_mk)


def _memkv_bwd(kv, dk, dv, g_mk):
    def body(kv_ref, dk_ref, dv_ref, g_ref, o_ref, dg_ref):
        dkk, gain = _gnorm_bwd(dk_ref[...], kv_ref[:, 0:BW], g_ref[...])
        o_ref[:, 0:BW] = dkk.astype(BF16)
        o_ref[:, BW:2 * BW] = dv_ref[...].astype(BF16)
        dg_ref[...] = jnp.sum(gain, axis=0, keepdims=True)

    return pl.pallas_call(body, out_shape=(_sds((N_MEM, 2 * BW), BF16), _sds((1, BW), F32)), name="memkv_bwd",
                          compiler_params=_cp())(kv, dk, dv, g_mk)


MERGE_TM = 512


def _merge_fwd(brs, wbt, gp):
    t = gp.shape[0]
    tm = MERGE_TM

    def body(b0, b1, b2, b3, wb_ref, gp_ref, o_ref):
        out = jnp.zeros((tm, D), F32)
        for n, b_ref in enumerate((b0, b1, b2, b3)):
            up = _dot(b_ref[...], wb_ref[n], tb=True)
            out = out + _sigmoid(gp_ref[:, n * D:(n + 1) * D].astype(F32)) * up
        o_ref[...] = out.astype(BF16)

    blk = pl.BlockSpec((tm, BW), lambda i: (i, 0))
    return pl.pallas_call(
        body, grid=(t // tm,),
        in_specs=[blk, blk, blk, blk, pl.BlockSpec((NH, D, BW), lambda i: (0, 0, 0)), pl.BlockSpec((tm, NH * D), lambda i: (i, 0))],
        out_specs=pl.BlockSpec((tm, D), lambda i: (i, 0)), out_shape=_sds((t, D), BF16), name="merge_fwd",
        compiler_params=_cp())(*brs, wbt, gp)


def _merge_bwd(dmerged, brs, wbt, gp):
    t = gp.shape[0]
    tm = MERGE_TM
    steps = t // tm

    def body(d_ref, b0, b1, b2, b3, wb_ref, gp_ref, dgp_ref, dbr_ref, dwb_ref, acc_ref):
        i = pl.program_id(0)
        dm = d_ref[...]
        for n, b_ref in enumerate((b0, b1, b2, b3)):
            br = b_ref[...]
            up = _dot(br, wb_ref[n], tb=True)
            g = _sigmoid(gp_ref[:, n * D:(n + 1) * D].astype(F32))
            dgp_ref[:, n * D:(n + 1) * D] = (dm * up * (g * (1.0 - g))).astype(BF16)
            dup = (dm * g).astype(BF16)
            dbr_ref[:, n * BW:(n + 1) * BW] = _dot(dup, wb_ref[n])
            part = _dot(dup, br, ta=True)

            @pl.when(i == 0)
            def _():
                acc_ref[n] = part

            @pl.when(i > 0)
            def _():
                acc_ref[n] += part

        @pl.when(i == steps - 1)
        def _():
            dwb_ref[...] = acc_ref[...].astype(BF16)

    row = pl.BlockSpec((tm, D), lambda i: (i, 0))
    blk = pl.BlockSpec((tm, BW), lambda i: (i, 0))
    wide = pl.BlockSpec((tm, NH * D), lambda i: (i, 0))
    whole = pl.BlockSpec((NH, D, BW), lambda i: (0, 0, 0))
    return pl.pallas_call(
        body, grid=(steps,), in_specs=[row, blk, blk, blk, blk, whole, wide], out_specs=(wide, row, whole),
        out_shape=(_sds((t, NH * D), BF16), _sds((t, NH * BW), F32), _sds((NH, D, BW), BF16)),
        scratch_shapes=[pltpu.VMEM((NH, D, BW), F32)], name="merge_bwd", compiler_params=_cp())(dmerged, *brs, wbt, gp)


FFN_TN = 256


def _ffn_in_fwd(h2, w_t):
    t = h2.shape[0]
    tm, tn = _tile(t, 2048), FFN_TN
    nj = FF // tn

    def body(x_ref, wa_ref, wg_ref, a_ref, g_ref, y_ref):
        x = x_ref[...]
        a, g = _dot(x, wa_ref[...], tb=True), _dot(x, wg_ref[...], tb=True)
        a_ref[...] = a.astype(BF16)
        g_ref[...] = g.astype(BF16)
        y_ref[...] = (a * _sigmoid(a) * g).astype(BF16)

    out = pl.BlockSpec((tm, tn), lambda i, j: (i, j))
    return pl.pallas_call(
        body, grid=(t // tm, nj),
        in_specs=[pl.BlockSpec((tm, D), lambda i, j: (i, 0)), pl.BlockSpec((tn, D), lambda i, j: (j, 0)),
                  pl.BlockSpec((tn, D), lambda i, j: (j + nj, 0))],
        out_specs=(out, out, out), out_shape=tuple(_sds((t, FF), BF16) for _ in range(3)), name="ffn_in_fwd",
        compiler_params=_cp(dimension_semantics=("parallel", "parallel")))(h2, w_t, w_t)


def _ffn_out_bwd(dx2b, w_out, a, g, dep):
    t = dx2b.shape[0]
    tm, tn = _tile(t, 2048), FFN_TN

    def body(*refs):
        x_ref, w_ref, a_ref, g_ref = refs[:4]
        da_ref, dg_ref = refs[-2:]
        d = _dot(x_ref[...], w_ref[...], tb=True)
        av, gv = a_ref[...].astype(F32), g_ref[...].astype(F32)
        s = _sigmoid(av)
        da_ref[...] = (d * gv * (s * (1.0 + av * (1.0 - s)))).astype(BF16)
        dg_ref[...] = (d * (av * s)).astype(BF16)

    blk = pl.BlockSpec((tm, tn), lambda i, j: (i, j))
    ins = [pl.BlockSpec((tm, D), lambda i, j: (i, 0)), pl.BlockSpec((tn, D), lambda i, j: (j, 0)), blk, blk]
    args = [dx2b, w_out, a, g]
    if dep is not None:
        ins.append(pl.BlockSpec((8, 128), lambda i, j: (0, 0)))
        args.append(dep)
    return pl.pallas_call(
        body, grid=(t // tm, FF // tn), in_specs=ins, out_specs=(blk, blk),
        out_shape=(_sds((t, FF), BF16), _sds((t, FF), BF16)), name="ffn_out_bwd",
        compiler_params=_cp(dimension_semantics=("parallel", "parallel")))(*args)


def _loss_head(y, target):
    t, d = y.shape
    tm = 256

    def body(y_ref, t_ref, dy_ref, dyb_ref, l_ref):
        e = y_ref[...] - t_ref[...]
        dy_ref[...] = e * (1.0 / d)
        dyb_ref[...] = (e * (1.0 / d)).astype(BF16)
        _acc(l_ref, jnp.full((8, 128), 0.5 * jnp.sum(jnp.sum(e * e, axis=-1, keepdims=True) * (1.0 / d)), F32), pl.program_id(0) == 0)

    row = pl.BlockSpec((tm, d), lambda i: (i, 0))
    return pl.pallas_call(body, grid=(t // tm,), in_specs=[row, row], out_specs=(row, row, pl.BlockSpec((8, 128), lambda i: (0, 0))),
                          out_shape=(_sds((t, d), F32), _sds((t, d), BF16), _sds((8, 128), F32)), name="loss_head",
                          compiler_params=_cp())(y, target)


def _sum_slots(x, name):
    k, r, c = x.shape
    tr = _tile(r, 512) if r % 128 == 0 else r

    def body(x_ref, o_ref):
        acc = x_ref[0].astype(F32)
        for s in range(1, k):
            acc = acc + x_ref[s].astype(F32)
        o_ref[...] = acc

    return pl.pallas_call(body, grid=(r // tr,), in_specs=[pl.BlockSpec((k, tr, c), lambda i: (0, i, 0))],
                          out_specs=pl.BlockSpec((tr, c), lambda i: (i, 0)), out_shape=_sds((r, c), F32), name=name,
                          compiler_params=_cp())(x)


def _pair_sum(bufs, recvs, cidx):
    n = len(bufs)

    def body(c_ref, *refs):
        for i in range(n):
            refs[2 * n + i][...] = (refs[i][...].astype(F32) + refs[n + i][...].astype(F32)).astype(BF16)

    return pl.pallas_call(
        body,
        grid_spec=pltpu.PrefetchScalarGridSpec(
            num_scalar_prefetch=1, grid=(4,),
            in_specs=[pl.BlockSpec((None, None) + b.shape[2:], lambda s, cref: (s, cref[0], 0, 0)) for b in bufs]
            + [pl.BlockSpec((None,) + r.shape[1:], lambda s, cref: (s, 0, 0)) for r in recvs],
            out_specs=tuple(pl.BlockSpec((None,) + r.shape[1:], lambda s, cref: (s, 0, 0)) for r in recvs)),
        out_shape=tuple(_sds(r.shape, BF16) for r in recvs), name="rs_pair_sum", compiler_params=_cp())(cidx, *bufs, *recvs)


def _adamw_update(w, gv, m, v):
    mn = ADAM_B1 * m + (1.0 - ADAM_B1) * gv
    vn = ADAM_B2 * v + (1.0 - ADAM_B2) * (gv * gv)
    m_hat = mn / (1.0 - ADAM_B1 ** ADAM_STEP)
    v_hat = vn / (1.0 - ADAM_B2 ** ADAM_STEP)
    return -ADAM_LR * (m_hat / (jnp.sqrt(v_hat) + ADAM_EPS) + ADAM_WD * w), mn, vn


def _adamw(w, g, m, v, name):
    r, c = w.shape

    def body(w_ref, g_ref, m_ref, v_ref, d_ref, nm_ref, nv_ref):
        d_ref[...], nm_ref[...], nv_ref[...] = _adamw_update(w_ref[...], g_ref[...], m_ref[...], v_ref[...])

    blk = pl.BlockSpec((r, c), lambda i: (0, 0))
    return pl.pallas_call(body, grid=(1,), in_specs=[blk] * 4, out_specs=(blk,) * 3,
                          out_shape=tuple(_sds((r, c), F32) for _ in range(3)), name=name, compiler_params=_cp())(w, g, m, v)


def _adamw_layer(layer, w, g, m, v, outs, name):
    _, r, c = w.shape
    tr = max(d for d in range(8, r + 1, 8) if r % d == 0 and d * c * 4 <= 2 ** 20)

    def body(w_ref, m_ref, v_ref, g_ref, *refs):
        d_ref, nm_ref, nv_ref, go_ref = refs[4:]
        gv = g_ref[...]
        d_ref[...], nm_ref[...], nv_ref[...] = _adamw_update(w_ref[...], gv, m_ref[...], v_ref[...])
        go_ref[...] = gv

    blk = pl.BlockSpec((None, tr, c), lambda i: (layer, i, 0))
    return pl.pallas_call(
        body, grid=(r // tr,), in_specs=[blk] * 3 + [pl.BlockSpec((tr, c), lambda i: (i, 0))] + [ANY] * 4, out_specs=(blk,) * 4,
        out_shape=tuple(_sds(w.shape, F32) for _ in range(4)), input_output_aliases={4 + j: j for j in range(4)}, name=name,
        compiler_params=_cp())(w, m, v, g, *outs)


def _all_gather(shards, name):
    n = len(shards)

    def body(*refs):
        x_refs, out_refs = refs[:n], refs[n:2 * n]
        send_sems, recv_sems, local_sems = refs[2 * n:]
        x, y, cc = lax.axis_index("x"), lax.axis_index("y"), lax.axis_index("c")
        me, sibling = (x, y, cc), (x, y, 1 - cc)
        chips = [(1 - x, y), (x, 1 - y), (1 - x, 1 - y)]

        def copy(i, k, block, to, own=False):
            px, py, pc = block
            slot = out_refs[i].at[4 * px + 2 * py + pc]
            return pltpu.make_async_remote_copy(
                src_ref=x_refs[i] if own else slot, dst_ref=slot, send_sem=send_sems.at[7 * i + k],
                recv_sem=recv_sems.at[7 * i + k], device_id=to, device_id_type=MESH)

        mine = [pltpu.make_async_copy(x_refs[i], out_refs[i].at[4 * x + 2 * y + cc], local_sems.at[i]) for i in range(n)]
        for cp in mine:
            cp.start()
        first = []
        for j, chip in enumerate(chips):
            first += [copy(i, 1 + j, me, (*chip, cc), own=True) for i in range(n)]
        first += [copy(i, 0, me, sibling, own=True) for i in range(n)]
        for cp in first:
            cp.start()
        passed = []
        for j, chip in enumerate(chips):
            for i in range(n):
                copy(i, 1 + j, (*chip, cc), me).wait_recv()
                cp = copy(i, 4 + j, (*chip, cc), sibling)
                cp.start()
                passed.append(cp)
        for i in range(n):
            copy(i, 0, sibling, me).wait_recv()
        for j, chip in enumerate(chips):
            for i in range(n):
                copy(i, 4 + j, (*chip, 1 - cc), me).wait_recv()
        for cp in first + passed:
            cp.wait_send()
        for cp in mine:
            cp.wait()

    return pl.pallas_call(
        body, out_shape=tuple(_sds((N_DEV,) + s.shape, s.dtype) for s in shards), in_specs=[ANY] * n, out_specs=(ANY,) * n,
        scratch_shapes=[pltpu.SemaphoreType.DMA((7 * n,)), pltpu.SemaphoreType.DMA((7 * n,)), pltpu.SemaphoreType.DMA((n,))],
        name=name)(*shards)


def _rs_core_swap(bufs, name):
    n = len(bufs)

    def body(*refs):
        b_refs, recv_refs = refs[:n], refs[n:2 * n]
        send_sems, recv_sems = refs[2 * n:]
        x, y, cc = lax.axis_index("x"), lax.axis_index("y"), lax.axis_index("c")
        copies = [pltpu.make_async_remote_copy(
            src_ref=b_refs[i].at[s, 1 - cc], dst_ref=recv_refs[i].at[s], send_sem=send_sems.at[4 * i + s],
            recv_sem=recv_sems.at[4 * i + s], device_id=(x, y, 1 - cc), device_id_type=MESH) for i in range(n) for s in range(4)]
        for cp in copies:
            cp.start()
        for cp in copies:
            cp.wait()

    return pl.pallas_call(
        body, out_shape=tuple(_sds((4,) + b.shape[2:], b.dtype) for b in bufs), in_specs=[ANY] * n, out_specs=(ANY,) * n,
        scratch_shapes=[pltpu.SemaphoreType.DMA((4 * n,)), pltpu.SemaphoreType.DMA((4 * n,))], name=name)(*bufs)


HBM = pl.BlockSpec(memory_space=pltpu.HBM)
SEMS = pl.BlockSpec(memory_space=pltpu.SEMAPHORE)
EFFECT = pltpu.SideEffectType.DATAFLOW_SIDE_EFFECTING


def _hbm(a):
    return pltpu.HBM(a.shape, a.dtype)


def _other_chips(x, y):
    return [(1 - x, y), (x, 1 - y), (1 - x, 1 - y)]


def _ici_start(srcs, lands, mode, name, group=None):
    n = len(srcs)

    def body(*refs):
        s_refs, land_refs = refs[:n], refs[n:2 * n]
        send_sems, recv_sems = refs[2 * n], refs[2 * n + 1]
        token = refs[-1]
        x, y, cc = lax.axis_index("x"), lax.axis_index("y"), lax.axis_index("c")
        mine = 2 * x + y if mode == "by_chip" else 4 * x + 2 * y + cc
        peers = [(px, py, cc) for px, py in _other_chips(x, y)]
        if mode == "by_device":
            peers = [(x, y, 1 - cc)] + peers + [(px, py, 1 - cc) for px, py in _other_chips(x, y)]
        first = 0
        for size in ([n] if group is None else group):
            first += size
            for px, py, pc in peers:
                for i in range(first - size, first):
                    src = s_refs[i]
                    if mode == "by_chip":
                        src = src.at[2 * px + py]
                    elif mode == "by_device":
                        src = src.at[4 * px + 2 * py + pc]
                    pltpu.make_async_remote_copy(
                        src_ref=src, dst_ref=land_refs[i].at[mine], send_sem=send_sems.at[i], recv_sem=recv_sems.at[i],
                        device_id=(px, py, pc), device_id_type=MESH).start()
        token[...] = jnp.zeros_like(token)

    out = pl.pallas_call(
        body, name=name,
        out_shape=(pltpu.SemaphoreType.DMA((n,)), pltpu.SemaphoreType.DMA((n,)), *[_hbm(s) for s in srcs], *[_hbm(l) for l in lands],
                   _sds((8, 128), F32)),
        in_specs=[HBM] * (2 * n), out_specs=(SEMS, SEMS, *[HBM] * (2 * n), pl.BlockSpec(memory_space=pltpu.VMEM)),
        input_output_aliases={i: 2 + i for i in range(2 * n)}, compiler_params=pltpu.CompilerParams(has_side_effects=EFFECT),
    )(*[pltpu.with_memory_space_constraint(s, pltpu.HBM) for s in srcs],
      *[pltpu.with_memory_space_constraint(l, pltpu.HBM) for l in lands])
    return out[0], out[1], out[2:2 + n], out[2 + n:2 + 2 * n], out[-1], 7 if mode == "by_device" else 3


def _ici_wait(started, after, name, only=None):
    send_sems, recv_sems, srcs, lands, _, copies = started
    only = list(range(len(srcs))) if only is None else only
    srcs, lands = [srcs[i] for i in only], [lands[i] for i in only]
    n = len(srcs)

    def body(*refs):
        land_refs = refs[n:2 * n]
        send_sems, recv_sems = refs[2 * n], refs[2 * n + 1]
        x, y, cc = lax.axis_index("x"), lax.axis_index("y"), lax.axis_index("c")
        for i in range(n):
            three = land_refs[i].at[pl.ds(0, copies)]
            cp = pltpu.make_async_remote_copy(src_ref=three, dst_ref=three, send_sem=send_sems.at[only[i]],
                                              recv_sem=recv_sems.at[only[i]],
                                              device_id=(x, y, cc), device_id_type=MESH)
            cp.wait_send()
            cp.wait_recv()

    return pl.pallas_call(
        body, name=name, out_shape=tuple(_hbm(l) for l in lands), in_specs=[HBM] * (2 * n) + [SEMS, SEMS, ANY],
        out_specs=tuple([HBM] * n), input_output_aliases={n + i: i for i in range(n)},
        compiler_params=pltpu.CompilerParams(has_side_effects=EFFECT))(*srcs, *lands, send_sems, recv_sems, after)


def _gather_d2d(blocks, lands, name):
    n = len(blocks)

    def body(*refs):
        x_refs, land_refs = refs[:n], refs[2 * n:3 * n]
        send_sems, recv_sems, in_sems, out_sems = refs[3 * n:3 * n + 4]
        stage = refs[3 * n + 4:]
        x, y, cc = lax.axis_index("x"), lax.axis_index("y"), lax.axis_index("c")
        sibling = (x, y, 1 - cc)
        staged = [pltpu.make_async_copy(x_refs[i], stage[i], in_sems.at[i]) for i in range(n)]
        for cp in staged:
            cp.start()
        copies = []
        for i in range(n):
            slot = land_refs[i].at[4 * x + 2 * y + cc]
            copies.append(pltpu.make_async_remote_copy(src_ref=x_refs[i], dst_ref=slot, send_sem=send_sems.at[4 * i],
                                                       recv_sem=recv_sems.at[4 * i], device_id=sibling, device_id_type=MESH))
            for j, (px, py) in enumerate(_other_chips(x, y)):
                slot = land_refs[i].at[4 * px + 2 * py + cc]
                copies.append(pltpu.make_async_remote_copy(src_ref=slot, dst_ref=slot, send_sem=send_sems.at[4 * i + 1 + j],
                                                           recv_sem=recv_sems.at[4 * i + 1 + j], device_id=sibling, device_id_type=MESH))
        for cp in copies:
            cp.start()
        mine = []
        for i in range(n):
            staged[i].wait()
            mine.append(pltpu.make_async_copy(stage[i], land_refs[i].at[4 * x + 2 * y + cc], out_sems.at[i]))
            mine[i].start()
        for i in range(n):
            slot = land_refs[i].at[4 * x + 2 * y + (1 - cc)]
            pltpu.make_async_remote_copy(src_ref=slot, dst_ref=slot, send_sem=send_sems.at[4 * i], recv_sem=recv_sems.at[4 * i],
                                         device_id=sibling, device_id_type=MESH).wait_recv()
            for j, (px, py) in enumerate(_other_chips(x, y)):
                slot = land_refs[i].at[4 * px + 2 * py + (1 - cc)]
                pltpu.make_async_remote_copy(src_ref=slot, dst_ref=slot, send_sem=send_sems.at[4 * i + 1 + j],
                                             recv_sem=recv_sems.at[4 * i + 1 + j], device_id=sibling, device_id_type=MESH).wait_recv()
        for cp in copies:
            cp.wait_send()
        for cp in mine:
            cp.wait()

    return pl.pallas_call(
        body, out_shape=tuple(_sds(l.shape, l.dtype) for l in lands), in_specs=[ANY] * (2 * n), out_specs=(ANY,) * n,
        input_output_aliases={n + i: i for i in range(n)},
        scratch_shapes=[pltpu.SemaphoreType.DMA((4 * n,)), pltpu.SemaphoreType.DMA((4 * n,)), pltpu.SemaphoreType.DMA((n,)),
                        pltpu.SemaphoreType.DMA((n,))] + [pltpu.VMEM(b.shape, b.dtype) for b in blocks],
        name=name, compiler_params=_cp())(*blocks, *lands)


def _sum_own(parts, recvs, mine, name):
    n = len(parts)

    def body(c_ref, *refs):
        s = pl.program_id(0)
        for i in range(n):
            val = jnp.where(c_ref[0] == s, refs[i][...], refs[n + i][...]).astype(F32)
            _acc(refs[2 * n + i], val, s == 0)

    kept = [pl.BlockSpec((None,) + p.shape[1:], lambda s, cref: (cref[0], 0, 0)) for p in parts]
    ins = [pl.BlockSpec((None,) + p.shape[1:], lambda s, cref: (s, 0, 0)) for p in parts]
    return pl.pallas_call(
        body, grid_spec=pltpu.PrefetchScalarGridSpec(
            num_scalar_prefetch=1, grid=(parts[0].shape[0],), in_specs=kept + ins,
            out_specs=tuple(pl.BlockSpec(p.shape[1:], lambda s, cref: (0, 0)) for p in parts)),
        out_shape=tuple(_sds(p.shape[1:], F32) for p in parts), name=name, compiler_params=_cp())(mine, *parts, *recvs)


BIG = (("w_in", True), ("w_gate", True), ("w_mem_kv", False), ("w_branch", True), ("w_out", False), ("w_ffn_in", True),
       ("w_ffn_out", False))

SMALL = ("norm_mix_g", "norm_mem_g", "ret_decay_fwd", "ret_decay_bwd", "ret_norm_g", "pool_w", "pool_scale", "na_q_norm_g",
         "na_k_norm_g", "na_rpb", "mem_q_norm_g", "mem_k_norm_g", "norm_ffn_g")
WEIGHTS = ("norm_mix_g", "norm_mem_g", "w_in", "w_gate", "ret_decay_fwd", "ret_decay_bwd", "ret_norm_g", "pool_w", "pool_scale",
           "na_q_norm_g", "na_k_norm_g", "na_rpb", "mem_q_norm_g", "mem_k_norm_g", "w_mem_kv", "w_branch", "w_out", "norm_ffn_g",
           "w_ffn_in", "w_ffn_out")


def _to_exchange(name, transposed, shard):
    if name == "w_branch":
        return jnp.swapaxes(shard, 1, 2).reshape(NH * (D // N_DEV), BW)
    return shard.T if transposed else shard


def _from_exchange(name, transposed, block):
    if name == "w_branch":
        return jnp.swapaxes(block.reshape(NH, D // N_DEV, BW), 1, 2)
    return block.T if transposed else block


def _whole_from_gathered(name, g):
    if name == "w_branch":
        return jnp.swapaxes(g.reshape(N_DEV, NH, D // N_DEV, BW), 0, 1).reshape(NH, D, BW)
    return g.reshape(N_DEV * g.shape[1], g.shape[2])


def _by_destination(name, g):
    if name == "w_branch":
        g = jnp.swapaxes(g.reshape(NH, N_DEV, D // N_DEV, BW), 0, 1).reshape(N_DEV * NH * (D // N_DEV), BW)
    return g.reshape(4, 2, g.shape[0] // N_DEV, g.shape[1])


SMALL_PAD = 1024


def _pack_small(vals, loss=None):
    parts = [vals[n] for n in SMALL] + [jnp.zeros((1,), F32) if loss is None else loss.reshape(1)]
    rows = []
    for p in parts:
        flat = p.reshape(-1)
        rows.append(jnp.pad(flat, (0, -flat.shape[0] % SMALL_PAD)).reshape(-1, 128))
    return jnp.concatenate(rows, axis=0)


def _unpack_small(packed, like):
    out, off = {}, 0
    for n in SMALL:
        sz = int(np.prod(like[n].shape))
        nrow = -(-sz // SMALL_PAD) * (SMALL_PAD // 128)
        out[n] = packed[off:off + nrow].reshape(-1)[:sz].reshape(like[n].shape)
        off += nrow
    return out, packed[off, 0]


def _na_constants():
    c = np.arange(GRID_W)
    win = np.clip(c - NA_COLS_WIN // 2, 0, GRID_W - NA_COLS_WIN)
    kc = np.arange(GRID_W)
    inside = (kc[None, :] >= win[:, None]) & (kc[None, :] < win[:, None] + NA_COLS_WIN)
    off = kc[None, :] - c[:, None] + NA_COLS_WIN - 1
    onehot = np.zeros((128, GRID_W, GRID_W), np.float32)
    for b in range(2 * NA_COLS_WIN - 1):
        onehot[b] = (off == b) & inside
    maskadd = np.where(inside, 0.0, NEG).astype(np.float32)
    return onehot.reshape(128, GRID_W * GRID_W), maskadd


def _na_bias_table(tab, maskadd):
    n_off = 2 * NA_ROWS_WIN - 1
    t4 = tab[:NH * n_off].reshape(NH, n_off, GRID_W, GRID_W) + maskadd[None, None]
    by_off = t4.transpose(1, 0, 2, 3).reshape(n_off, NH * GRID_W, GRID_W)
    return jnp.concatenate([by_off[:-1], by_off[1:]], axis=-1)


def _rotary_tables(t):
    half = HD // 2
    inv = ROPE_THETA ** (-jnp.arange(half, dtype=F32) / half)
    ang = jnp.arange(t, dtype=F32)[:, None] * inv[None, :]
    cos, sin = jnp.cos(ang), jnp.sin(ang)
    return jnp.tile(jnp.concatenate([cos, cos], axis=-1), (1, NH)), jnp.tile(jnp.concatenate([-sin, sin], axis=-1), (1, NH))


def _block_diag(pw):
    out = jnp.zeros((BW, BW), pw.dtype)
    for g in range(NH):
        out = lax.dynamic_update_slice(out, pw[g], (g * HD, g * HD))
    return out


def _tile4(g):
    return jnp.tile(g.reshape(1, HD), (1, NH))


def _layer_fwd(x, mem, sw, lw, consts, fetch, h=None, next_norm_g=None):
    cos2, sin2, onehot, maskadd = consts
    if h is None:
        h = _rmsnorm_fwd(x, sw["norm_mix_g"].reshape(1, D), "norm_mix_fwd")
    proj = _mm(h, lw["w_in"], tb=True, name="mm_in")
    gp = _mm(h, lw["w_gate"], tb=True, out_dtype=BF16, name="mm_gate")
    g_naq, g_nak, g_mq = _tile4(sw["na_q_norm_g"]), _tile4(sw["na_k_norm_g"]), _tile4(sw["mem_q_norm_g"])
    rq, rk, rv, nq, nk, nv, mq = _prep_fwd(proj, cos2, sin2, g_naq, g_nak, g_mq)

    lgf, lgb = jax.nn.log_sigmoid(sw["ret_decay_fwd"]), jax.nn.log_sigmoid(sw["ret_decay_bwd"])
    g_ret = sw["ret_norm_g"].reshape(1, BW)
    o_ret, ret = _ret_fwd(rq, rk, rv, proj, lgf, lgb, g_ret)

    wbd = _block_diag(sw["pool_w"]).astype(BF16)
    p_scale = sw["pool_scale"].reshape(1, BW)
    pool = _pool_fwd(proj, wbd, p_scale)

    rpb_pad = jnp.pad(sw["na_rpb"].reshape(NH * 15, 31), ((0, 4), (0, 97)))
    ball = _na_bias_table(_rpb_expand(rpb_pad, onehot), maskadd)
    na = _na_fwd(nq, nk, nv, ball)

    lw.update(fetch(1, na))
    memn = _rmsnorm_fwd(mem, sw["norm_mem_g"].reshape(1, D), "norm_mem_fwd")
    kv = _mm(memn, lw["w_mem_kv"], name="mm_memkv")
    g_mk = _tile4(sw["mem_k_norm_g"])
    mk, mv = _memkv_prep(kv, g_mk)
    mo = _mem_fwd(mq, mk, mv)

    br = (ret, pool, na, mo)
    merged = _merge_fwd(br, lw["w_branch"], gp)
    x1, h2 = _mm(merged, lw["w_out"], add=x, norm_g=sw["norm_ffn_g"].reshape(1, D), name="mm_out")
    lw.update(fetch(2, x1))
    ffa, ffg, yff = _ffn_in_fwd(h2, lw["w_ffn_in"])
    if next_norm_g is None:
        x2, h_next = _mm(yff, lw["w_ffn_out"], add=x1, name="mm_ffn_out"), None
    else:
        x2, h_next = _mm(yff, lw["w_ffn_out"], add=x1, norm_g=next_norm_g.reshape(1, D), name="mm_ffn_out")
    saved = dict(x=x, h=h, proj=proj, gp=gp, rq=rq, rk=rk, rv=rv, nq=nq, nk=nk, nv=nv, mq=mq, o_ret=o_ret, ball=ball, memn=memn,
                 kv=kv, mk=mk, mv=mv, br=br, merged=merged, x1=x1, h2=h2, ffa=ffa, ffg=ffg, yff=yff, lgf=lgf, lgb=lgb, wbd=wbd)
    return x2, h_next, saved


def _layer_bwd(dx2, dx2b, mem, sw, lw, sv, consts, dep=None):
    cos2, sin2, onehot, maskadd = consts
    gb, gs = {}, {}
    d_a, d_g = _ffn_out_bwd(dx2b, lw["w_ffn_out"], sv["ffa"], sv["ffg"], dep)
    gb["w_ffn_out"] = _mm(sv["yff"], dx2b, ta=True, out_dtype=BF16, name="mm_ffn_out_dw")
    dh2 = _mm(d_a, lw["w_ffn_in"], b_half=0, name="mm_ffn_in_dx_a")
    dx1, dx1b, dg = _mm_norm_bwd(d_g, lw["w_ffn_in"], dh2, sv["x1"], sw["norm_ffn_g"].reshape(1, D), dx2, b_half=1,
                                 name="mm_ffn_in_dx_g")
    gs["norm_ffn_g"] = dg.reshape(D)
    dw_a = _mm(d_a, sv["h2"], ta=True, out_dtype=BF16, out_half=(0, None), name="mm_ffn_in_dw_a")
    gb["w_ffn_in"] = _mm(d_g, sv["h2"], ta=True, out_dtype=BF16, out_half=(1, dw_a), name="mm_ffn_in_dw_g")

    dmerged = _mm(dx1b, lw["w_out"], tb=True, name="mm_out_dx")
    gb["w_out"] = _mm(sv["merged"], dx1b, ta=True, out_dtype=BF16, name="mm_out_dw")
    dgp, dbr, gb["w_branch"] = _merge_bwd(dmerged, sv["br"], lw["w_branch"], sv["gp"])

    g_ret = sw["ret_norm_g"].reshape(1, BW)
    do_ret, d_rg, dg_ret = _ret_post_bwd(dbr, sv["o_ret"], sv["proj"], g_ret)
    d_rq, d_rk, d_rv, dlg = _ret_bwd(do_ret, sv["rq"], sv["rk"], sv["rv"], sv["lgf"], sv["lgb"])
    gs["ret_norm_g"] = dg_ret.reshape(BW)
    _, vjp_f = jax.vjp(jax.nn.log_sigmoid, sw["ret_decay_fwd"])
    _, vjp_b = jax.vjp(jax.nn.log_sigmoid, sw["ret_decay_bwd"])
    gs["ret_decay_fwd"] = vjp_f(dlg[0:NH, 0])[0]
    gs["ret_decay_bwd"] = vjp_b(dlg[NH:2 * NH, 0])[0]

    p_scale = sw["pool_scale"].reshape(1, BW)
    d_pv, dwbd, dscale = _pool_bwd(dbr, sv["proj"], sv["wbd"], p_scale)
    gs["pool_w"] = jnp.stack([dwbd[g * HD:(g + 1) * HD, g * HD:(g + 1) * HD] for g in range(NH)])
    gs["pool_scale"] = dscale.reshape(BW)

    d_nq, d_nk, d_nv, dball = _na_bwd(dbr, sv["nq"], sv["nk"], sv["nv"], sv["ball"])
    _, vjp_tab = jax.vjp(lambda tab: _na_bias_table(tab, maskadd), jnp.zeros((64, GRID_W * GRID_W), F32))
    drpb = _rpb_reduce(vjp_tab(dball)[0], onehot)
    gs["na_rpb"] = drpb[:NH * 15, :31].reshape(NH, 15, 31)

    d_mq, d_mk, d_mv = _mem_bwd(dbr, sv["mq"], sv["mk"], sv["mv"])
    g_mk = _tile4(sw["mem_k_norm_g"])
    dkv, dg_mk = _memkv_bwd(sv["kv"], d_mk, d_mv, g_mk)
    gs["mem_k_norm_g"] = dg_mk.reshape(NH, HD).sum(0)
    gb["w_mem_kv"] = _mm(sv["memn"], dkv, ta=True, out_dtype=BF16, name="mm_memkv_dw")
    dmemn = _mm(dkv, lw["w_mem_kv"], tb=True, name="mm_memkv_dx")
    _, _, dg_mem = _rmsnorm_bwd(dmemn, mem, sw["norm_mem_g"].reshape(1, D), jnp.zeros_like(mem), "norm_mem_bwd")
    gs["norm_mem_g"] = dg_mem.reshape(D)

    g_naq, g_nak, g_mq = _tile4(sw["na_q_norm_g"]), _tile4(sw["na_k_norm_g"]), _tile4(sw["mem_q_norm_g"])
    dproj, dg_naq, dg_nak, dg_mq = _prep_bwd(sv["proj"], cos2, sin2, g_naq, g_nak, g_mq, d_rq, d_rk, d_rv, d_rg, d_pv, d_nq, d_nk,
                                             d_nv, d_mq)
    gs["na_q_norm_g"] = dg_naq.reshape(NH, HD).sum(0)
    gs["na_k_norm_g"] = dg_nak.reshape(NH, HD).sum(0)
    gs["mem_q_norm_g"] = dg_mq.reshape(NH, HD).sum(0)

    gb["w_in"] = _mm(dproj, sv["h"], ta=True, out_dtype=BF16, name="mm_in_dw")
    gb["w_gate"] = _mm(dgp, sv["h"], ta=True, out_dtype=BF16, name="mm_gate_dw")
    dh = _mm(dproj, lw["w_in"], name="mm_in_dx")
    dx, dxb, dg = _mm_norm_bwd(dgp, lw["w_gate"], dh, sv["x"], sw["norm_mix_g"].reshape(1, D), dx1, name="mm_gate_dx")
    gs["norm_mix_g"] = dg.reshape(D)
    return dx, dxb, gb, gs


def _local_step(x, mem, target, small, get_layer, on_grads):
    t = x.shape[0]
    cos2, sin2 = _rotary_tables(t)
    onehot, maskadd = _na_constants()
    consts = (cos2, sin2, jnp.asarray(onehot), jnp.asarray(maskadd))
    saved, weights, cur, h = [], [], x, None
    for l in range(DEPTH):
        sw = {n: small[n][l] for n in SMALL}
        lw, fetch = get_layer(l, cur)
        weights.append(lw)
        cur, h, sv = _layer_fwd(cur, mem, sw, lw, consts, fetch, h, small["norm_mix_g"][l + 1] if l + 1 < DEPTH else None)
        saved.append(sv)
    dy, dyb, loss_tile = _loss_head(cur, target)
    small_g = {n: [None] * DEPTH for n in SMALL}
    dep = None
    for l in reversed(range(DEPTH)):
        sw = {n: small[n][l] for n in SMALL}
        dy, dyb, gb, gs = _layer_bwd(dy, dyb, mem, sw, weights[l], saved[l], consts, dep)
        dep = on_grads(l, gb, dy)
        for n in SMALL:
            small_g[n][l] = gs[n]
    return loss_tile[0, 0], dy, {n: jnp.stack(v) for n, v in small_g.items()}


def _flat2d(a):
    return a.reshape(-1, a.shape[-1])


def kernel(x, mem, norm_mix_g, norm_mem_g, w_in, w_gate, ret_decay_fwd, ret_decay_bwd, ret_norm_g, pool_w, pool_scale, na_q_norm_g, na_k_norm_g, na_rpb, mem_q_norm_g, mem_k_norm_g, w_mem_kv, w_branch, w_out, norm_ffn_g, w_ffn_in, w_ffn_out, loss_target, m_norm_mix_g, m_norm_mem_g, m_w_in, m_w_gate, m_ret_decay_fwd, m_ret_decay_bwd, m_ret_norm_g, m_pool_w, m_pool_scale, m_na_q_norm_g, m_na_k_norm_g, m_na_rpb, m_mem_q_norm_g, m_mem_k_norm_g, m_w_mem_kv, m_w_branch, m_w_out, m_norm_ffn_g, m_w_ffn_in, m_w_ffn_out, v_norm_mix_g, v_norm_mem_g, v_w_in, v_w_gate, v_ret_decay_fwd, v_ret_decay_bwd, v_ret_norm_g, v_pool_w, v_pool_scale, v_na_q_norm_g, v_na_k_norm_g, v_na_rpb, v_mem_q_norm_g, v_mem_k_norm_g, v_w_mem_kv, v_w_branch, v_w_out, v_norm_ffn_g, v_w_ffn_in, v_w_ffn_out):
    w = dict(norm_mix_g=norm_mix_g, norm_mem_g=norm_mem_g, w_in=w_in, w_gate=w_gate, ret_decay_fwd=ret_decay_fwd,
             ret_decay_bwd=ret_decay_bwd, ret_norm_g=ret_norm_g, pool_w=pool_w, pool_scale=pool_scale, na_q_norm_g=na_q_norm_g,
             na_k_norm_g=na_k_norm_g, na_rpb=na_rpb, mem_q_norm_g=mem_q_norm_g, mem_k_norm_g=mem_k_norm_g, w_mem_kv=w_mem_kv,
             w_branch=w_branch, w_out=w_out, norm_ffn_g=norm_ffn_g, w_ffn_in=w_ffn_in, w_ffn_out=w_ffn_out)
    m = dict(norm_mix_g=m_norm_mix_g, norm_mem_g=m_norm_mem_g, w_in=m_w_in, w_gate=m_w_gate, ret_decay_fwd=m_ret_decay_fwd,
             ret_decay_bwd=m_ret_decay_bwd, ret_norm_g=m_ret_norm_g, pool_w=m_pool_w, pool_scale=m_pool_scale, na_q_norm_g=m_na_q_norm_g,
             na_k_norm_g=m_na_k_norm_g, na_rpb=m_na_rpb, mem_q_norm_g=m_mem_q_norm_g, mem_k_norm_g=m_mem_k_norm_g, w_mem_kv=m_w_mem_kv,
             w_branch=m_w_branch, w_out=m_w_out, norm_ffn_g=m_norm_ffn_g, w_ffn_in=m_w_ffn_in, w_ffn_out=m_w_ffn_out)
    v = dict(norm_mix_g=v_norm_mix_g, norm_mem_g=v_norm_mem_g, w_in=v_w_in, w_gate=v_w_gate, ret_decay_fwd=v_ret_decay_fwd,
             ret_decay_bwd=v_ret_decay_bwd, ret_norm_g=v_ret_norm_g, pool_w=v_pool_w, pool_scale=v_pool_scale, na_q_norm_g=v_na_q_norm_g,
             na_k_norm_g=v_na_k_norm_g, na_rpb=v_na_rpb, mem_q_norm_g=v_mem_q_norm_g, mem_k_norm_g=v_mem_k_norm_g, w_mem_kv=v_w_mem_kv,
             w_branch=v_w_branch, w_out=v_w_out, norm_ffn_g=v_norm_ffn_g, w_ffn_in=v_w_ffn_in, w_ffn_out=v_w_ffn_out)
    assert x.shape == (1, 2048, D) and mem.shape == (1, N_MEM, D) and w_in.shape == (DEPTH, D, 9 * BW // N_DEV)

    first_groups = [[0, 1], [2, 3, 4], [5, 6]]
    blocks = [_to_exchange(name, tr, w[name][l]).astype(BF16) for l in range(DEPTH) for name, tr in BIG]
    started = _ici_start(blocks, [lax.empty((N_DEV,) + b.shape, BF16) for b in blocks], "gather", "gather_ici_start",
                         [len(g) for g in first_groups] + [len(BIG)] * (DEPTH - 1))

    def get_group(l, only, after, tag):
        at = [l * len(BIG) + i for i in only]
        lands = _ici_wait(started, after, "gather_ici_wait_%d%s" % (l, tag), at)
        whole = _gather_d2d([started[2][i] for i in at], lands, "gather_d2d")
        return {BIG[i][0]: _whole_from_gathered(BIG[i][0], g) for i, g in zip(only, whole)}

    def get_layer(l, after):
        if l > 0:
            return get_group(l, list(range(len(BIG))), after, ""), lambda stage, after2: {}
        return (get_group(l, first_groups[0], started[4], "a"),
                lambda stage, after2: get_group(l, first_groups[stage], after2, "abc"[stage]))

    cidx = lax.axis_index("c").astype(jnp.int32).reshape(1)
    chip = (2 * lax.axis_index("x") + lax.axis_index("y")).astype(jnp.int32).reshape(1)
    in_flight = []

    def flip_of(name, tr):
        return (lambda a: jnp.swapaxes(a, 1, 2)) if name in ("w_in", "w_ffn_in") else (lambda a: a)

    def rows3(a):
        return a.reshape(DEPTH, -1, a.shape[-1])

    opt_in = {name: tuple(rows3(flip_of(name, tr)(t[name])) for t in (w, m, v)) for name, tr in BIG}
    opt_out = {name: tuple(lax.empty(opt_in[name][0].shape, F32) for _ in range(4)) for name, _ in BIG}

    device = (2 * chip + cidx).astype(jnp.int32)

    def finish(l, st, after):
        recv = _ici_wait(st, after, "rs_ici_wait_%d" % l)
        sums = _sum_own(st[2], recv, chip if st[5] == 3 else device, "rs_sum")
        for (name, tr), s in zip(BIG, sums):
            g = s if name in ("w_in", "w_ffn_in") else _from_exchange(name, tr, s)
            wx, mx, vx = opt_in[name]
            opt_out[name] = _adamw_layer(l, wx, g.reshape(-1, g.shape[-1]), mx, vx, opt_out[name], "adamw_" + name)

    def on_grads(l, gb, after):
        send = [_by_destination(name, gb[name]) for name, _ in BIG]
        if l > 0:
            send = [s.reshape((N_DEV,) + s.shape[2:]) for s in send]
            st = _ici_start(send, [lax.empty(s.shape, BF16) for s in send], "by_device", "rs_ici_start_%d" % l)
        else:
            from_core = _rs_core_swap(send, "rs_core_swap")
            chip_part = _pair_sum(send, from_core, cidx)
            st = _ici_start(chip_part, [lax.empty(p.shape, BF16) for p in chip_part], "by_chip", "rs_ici_start_%d" % l)
        in_flight.append((l, st))
        return st[4]

    loss_local, dx, small_g = _local_step(x[0], mem[0], loss_target[0], {n: w[n] for n in SMALL}, get_layer, on_grads)

    last_started = in_flight[-1][1][4]
    for l, st in in_flight[:-1]:
        finish(l, st, last_started)

    small_all, = _all_gather([_pack_small(small_g, loss_local) + last_started[0:1]], "gather_small")
    packed_g = _sum_slots(small_all, "small_sum")
    small_sum, loss = _unpack_small(packed_g, {n: w[n] for n in SMALL})
    d_, m_, v_ = _adamw(_pack_small({n: w[n] for n in SMALL}), packed_g, _pack_small({n: m[n] for n in SMALL}),
                        _pack_small({n: v[n] for n in SMALL}), "adamw_small")
    updated = d_[0:8]
    for name, _ in BIG:
        updated = updated + opt_out[name][0][1, 0:8, 0:128]
    finish(*in_flight[-1], updated)

    grads, delta, new_m, new_v = {}, {}, {}, {}
    for name, tr in BIG:
        shape = flip_of(name, tr)(w[name]).shape
        delta[name], new_m[name], new_v[name], grads[name] = (flip_of(name, tr)(a.reshape(shape)) for a in opt_out[name])
    like = {n: w[n] for n in SMALL}
    ds, _ = _unpack_small(d_, like)
    ms, _ = _unpack_small(m_, like)
    vs, _ = _unpack_small(v_, like)
    for n in SMALL:
        grads[n], delta[n], new_m[n], new_v[n] = small_sum[n], ds[n], ms[n], vs[n]

    return (loss, dx[None], *[grads[n] for n in WEIGHTS], *[delta[n] for n in WEIGHTS], *[new_m[n] for n in WEIGHTS],
            *[new_v[n] for n in WEIGHTS])
```

```python
import functools

import numpy as np
import jax
import jax.numpy as jnp
from jax import lax
from jax.experimental import pallas as pl
from jax.experimental.pallas import tpu as pltpu

F32 = jnp.float32
BF16 = jnp.bfloat16
MXU = jnp.bfloat16
HI = lax.Precision.HIGHEST

DEPTH = 4
D = 1024
BW = 256
HD = 64
NH = 4
GRID_W = 64
NA_ROWS_WIN = 8
NA_COLS_WIN = 16
N_MEM = 256
FF = 2816
EPS = 1e-6
NEG = -1e30
ROPE_THETA = 10000.0
POOL_HALF_MAX = 8

ADAM_LR, ADAM_B1, ADAM_B2, ADAM_EPS, ADAM_WD, ADAM_STEP = 0.001, 0.9, 0.999, 1e-08, 0.01, 10

N_DEV = 8
VMEM_LIMIT = 56 * 1024 * 1024
MM_VMEM_BUDGET = 40 * 1024 * 1024

RQ, RK, RV, RG, PV, NQ, NK, NV, MQ = range(9)

MESH = pl.DeviceIdType.MESH
ANY = pl.BlockSpec(memory_space=pl.ANY)
SMEM = pl.BlockSpec(memory_space=pltpu.SMEM)


def _cp(**kw):
    return pltpu.CompilerParams(vmem_limit_bytes=VMEM_LIMIT, **kw)


def _tile(n, cap):
    if n <= cap:
        return n
    best = None
    for t in range(128, cap + 1, 128):
        if n % t == 0:
            best = t
    assert best is not None, (n, cap)
    return best


def _sds(shape, dtype):
    return jax.ShapeDtypeStruct(shape, dtype)


def _lane_head(shape):
    return lax.shift_right_logical(lax.broadcasted_iota(jnp.int32, shape, len(shape) - 1), 6)


def _group_mean(z):
    i = lax.shift_right_logical(lax.broadcasted_iota(jnp.int32, (BW, BW), 0), 6)
    j = lax.shift_right_logical(lax.broadcasted_iota(jnp.int32, (BW, BW), 1), 6)
    g = jnp.where(i == j, 1.0 / HD, 0.0).astype(BF16)
    z_hi = z.astype(BF16)
    z_lo = (z - z_hi.astype(F32)).astype(BF16)
    return jnp.dot(z_hi, g, preferred_element_type=F32) + jnp.dot(z_lo, g, preferred_element_type=F32)


def _gnorm(t, g):
    r = lax.rsqrt(_group_mean(t * t) + EPS)
    return t * r * g


def _gnorm_bwd(dy, t, g):
    r = lax.rsqrt(_group_mean(t * t) + EPS)
    th = t * r
    dth = dy * g
    dt = r * (dth - th * _group_mean(dth * th))
    return dt, dy * th


def _swap_halves(t):
    lane = lax.broadcasted_iota(jnp.int32, t.shape, 1)
    return jnp.where((lane & 63) < 32, pltpu.roll(t, BW - 32, 1), pltpu.roll(t, 32, 1))


def _sigmoid(x):
    return 1.0 / (1.0 + jnp.exp(-x))


def _dot(a, b, ta=False, tb=False):
    return lax.dot_general(a.astype(MXU), b.astype(MXU), (((0 if ta else 1,), (1 if tb else 0,)), ((), ())),
                           preferred_element_type=F32)


def _stack_heads(t):
    head = _lane_head(t.shape)
    return jnp.concatenate([jnp.where(head == h, t, jnp.zeros_like(t)) for h in range(NH)], axis=0)


def _unstack_heads(t, rows):
    head = _lane_head((rows, BW))
    out = jnp.zeros((rows, BW), F32)
    for h in range(NH):
        out = out + jnp.where(head == h, t[h * rows:(h + 1) * rows], 0.0)
    return out


def _softmax_rows(s):
    m = jnp.max(s, axis=-1, keepdims=True)
    e = jnp.exp(s - m)
    return e / jnp.sum(e, axis=-1, keepdims=True)


def _acc(ref, val, first):
    @pl.when(first)
    def _():
        ref[...] = val

    @pl.when(jnp.logical_not(first))
    def _():
        ref[...] += val


def _mm(a, b, *, ta=False, tb=False, out_dtype=F32, add=None, dep=None, b_half=None, out_half=None, norm_g=None, name):
    m, k = (a.shape[1], a.shape[0]) if ta else a.shape
    n = b.shape[0] if tb else b.shape[1]
    assert b_half is None or (not tb and b.shape[0] == 2 * k)
    tm, tn = _tile(m, 1408), (n if norm_g is not None else _tile(n, 768))
    if not ta and m <= 2048:
        blocks = (m * k * a.dtype.itemsize + k * tn * b.dtype.itemsize + m * tn * jnp.dtype(out_dtype).itemsize
                  + (m * tn * 4 if add is not None else 0) + (m * tn * 2 if norm_g is not None else 0))
        if 2 * blocks <= MM_VMEM_BUDGET:
            tm = m
    n_in = 2 + (add is not None) + (dep is not None) + (out_half is not None) + (norm_g is not None)

    def body(*refs):
        a_ref, b_ref, o_ref = refs[0], refs[1], refs[n_in]
        r = _dot(a_ref[...], b_ref[...], ta, tb)
        if add is not None:
            r = r + refs[2][...]
        o_ref[...] = r.astype(out_dtype)
        if norm_g is not None:
            scale = lax.rsqrt(jnp.mean(r * r, axis=-1, keepdims=True) + EPS)
            refs[n_in + 1][...] = (r * scale * refs[n_in - 1][...]).astype(BF16)

    kb = 0 if b_half is None else b_half
    a_spec = pl.BlockSpec((k, tm), lambda i, j: (0, i)) if ta else pl.BlockSpec((tm, k), lambda i, j: (i, 0))
    b_spec = pl.BlockSpec((tn, k), lambda i, j: (j, 0)) if tb else pl.BlockSpec((k, tn), lambda i, j: (kb, j))
    plain = pl.BlockSpec((tm, tn), lambda i, j: (i, j))
    ins, args = [a_spec, b_spec], [a, b]
    if add is not None:
        ins.append(plain)
        args.append(add)
    if dep is not None:
        ins.append(pl.BlockSpec((8, 128), lambda i, j: (0, 0)))
        args.append(dep)
    o_spec, o_shape, aliases = plain, _sds((m, n), out_dtype), {}
    if out_half is not None:
        half, prev = out_half
        o_spec = pl.BlockSpec((tm, tn), lambda i, j: (i + half * (m // tm), j))
        o_shape = _sds((2 * m, n), out_dtype)
        ins.append(ANY)
        args.append(lax.empty((2 * m, n), out_dtype) if prev is None else prev)
        aliases = {len(args) - 1: 0}
    if norm_g is not None:
        ins.append(pl.BlockSpec((1, n), lambda i, j: (0, 0)))
        args.append(norm_g)
        o_spec, o_shape = (o_spec, plain), (o_shape, _sds((m, n), BF16))
    return pl.pallas_call(
        body, grid=(m // tm, n // tn), in_specs=ins, out_specs=o_spec, out_shape=o_shape, input_output_aliases=aliases, name=name,
        compiler_params=_cp(dimension_semantics=("parallel", "parallel")))(*args)


def _mm_norm_bwd(a, b, add, x, g, res, *, b_half=None, name):
    m, k = a.shape
    n = b.shape[1]
    tm = 512
    kb = 0 if b_half is None else b_half

    def body(a_ref, b_ref, c_ref, x_ref, g_ref, res_ref, dx_ref, dxb_ref, dg_ref):
        dhv = _dot(a_ref[...], b_ref[...]) + c_ref[...]
        xv = x_ref[...]
        r = lax.rsqrt(jnp.mean(xv * xv, axis=-1, keepdims=True) + EPS)
        xh = xv * r
        dxh = dhv * g_ref[...]
        dx = res_ref[...] + r * (dxh - xh * jnp.mean(dxh * xh, axis=-1, keepdims=True))
        dx_ref[...] = dx
        dxb_ref[...] = dx.astype(BF16)
        _acc(dg_ref, jnp.sum(dhv * xh, axis=0, keepdims=True), pl.program_id(0) == 0)

    row = pl.BlockSpec((tm, n), lambda i: (i, 0))
    vec = pl.BlockSpec((1, n), lambda i: (0, 0))
    return pl.pallas_call(
        body, grid=(m // tm,),
        in_specs=[pl.BlockSpec((tm, k), lambda i: (i, 0)), pl.BlockSpec((k, n), lambda i: (kb, 0)), row, row, vec, row],
        out_specs=(row, row, vec), out_shape=(_sds((m, n), F32), _sds((m, n), BF16), _sds((1, n), F32)), name=name,
        compiler_params=_cp())(a, b, add, x, g, res)


def _rmsnorm_fwd(x, g, name):
    t, d = x.shape
    tm = _tile(t, 256)

    def body(x_ref, g_ref, o_ref):
        xv = x_ref[...]
        r = lax.rsqrt(jnp.mean(xv * xv, axis=-1, keepdims=True) + EPS)
        o_ref[...] = (xv * r * g_ref[...]).astype(o_ref.dtype)

    return pl.pallas_call(
        body, grid=(t // tm,), in_specs=[pl.BlockSpec((tm, d), lambda i: (i, 0)), pl.BlockSpec((1, d), lambda i: (0, 0))],
        out_specs=pl.BlockSpec((tm, d), lambda i: (i, 0)), out_shape=_sds((t, d), BF16), name=name, compiler_params=_cp())(x, g)


def _rmsnorm_bwd(dh, x, g, res, name):
    t, d = x.shape
    tm = _tile(t, 256)

    def body(dh_ref, x_ref, g_ref, res_ref, dx_ref, dxb_ref, dg_ref):
        xv = x_ref[...]
        dhv = dh_ref[...]
        r = lax.rsqrt(jnp.mean(xv * xv, axis=-1, keepdims=True) + EPS)
        xh = xv * r
        dxh = dhv * g_ref[...]
        dx = res_ref[...] + r * (dxh - xh * jnp.mean(dxh * xh, axis=-1, keepdims=True))
        dx_ref[...] = dx
        dxb_ref[...] = dx.astype(BF16)
        _acc(dg_ref, jnp.sum(dhv * xh, axis=0, keepdims=True), pl.program_id(0) == 0)

    row = pl.BlockSpec((tm, d), lambda i: (i, 0))
    vec = pl.BlockSpec((1, d), lambda i: (0, 0))
    return pl.pallas_call(
        body, grid=(t // tm,), in_specs=[row, row, vec, row], out_specs=(row, row, vec),
        out_shape=(_sds((t, d), F32), _sds((t, d), BF16), _sds((1, d), F32)), name=name, compiler_params=_cp())(dh, x, g, res)


def _prep_fwd(proj, cos2, sin2, g_naq, g_nak, g_mq):
    t = proj.shape[0]
    tm = 256

    def body(p_ref, cos_ref, sin_ref, gq_ref, gk_ref, gm_ref, rq_ref, rk_ref, rv_ref, nq_ref, nk_ref, nv_ref, mq_ref):
        def col(c):
            return p_ref[:, c * BW:(c + 1) * BW]

        cosv, sinv = cos_ref[...], sin_ref[...]

        def rot(tv):
            return tv * cosv + _swap_halves(tv) * sinv

        rq_ref[...] = (rot(col(RQ)) * (HD ** -0.5)).astype(BF16)
        rk_ref[...] = rot(col(RK)).astype(BF16)
        rv_ref[...] = col(RV).astype(BF16)
        nq_ref[...] = _gnorm(col(NQ), gq_ref[...]).astype(BF16)
        nk_ref[...] = _gnorm(col(NK), gk_ref[...]).astype(BF16)
        nv_ref[...] = col(NV).astype(BF16)
        mq_ref[...] = _gnorm(col(MQ), gm_ref[...]).astype(BF16)

    blk = pl.BlockSpec((tm, BW), lambda i: (i, 0))
    vec = pl.BlockSpec((1, BW), lambda i: (0, 0))
    return pl.pallas_call(
        body, grid=(t // tm,), in_specs=[pl.BlockSpec((tm, 9 * BW), lambda i: (i, 0)), blk, blk, vec, vec, vec],
        out_specs=tuple(blk for _ in range(7)), out_shape=tuple(_sds((t, BW), BF16) for _ in range(7)),
        name="prep_fwd", compiler_params=_cp())(proj, cos2, sin2, g_naq, g_nak, g_mq)


def _prep_bwd(proj, cos2, sin2, g_naq, g_nak, g_mq, d_rq, d_rk, d_rv, d_rg, d_pv, d_nq, d_nk, d_nv, d_mq):
    t = proj.shape[0]
    tm = 256

    def body(p_ref, cos_ref, sin_ref, gq_ref, gk_ref, gm_ref, drq_ref, drk_ref, drv_ref, drg_ref, dpv_ref, dnq_ref, dnk_ref,
             dnv_ref, dmq_ref, o_ref, dgq_ref, dgk_ref, dgm_ref):
        first = pl.program_id(0) == 0

        def col(c):
            return p_ref[:, c * BW:(c + 1) * BW]

        def put(c, v):
            o_ref[:, c * BW:(c + 1) * BW] = v.astype(BF16)

        cosv, sinv = cos_ref[...], sin_ref[...]

        def rot_t(dv):
            return dv * cosv + _swap_halves(dv * sinv)

        put(RQ, rot_t(drq_ref[...] * (HD ** -0.5)))
        put(RK, rot_t(drk_ref[...]))
        put(RV, drv_ref[...])
        put(RG, drg_ref[...])
        put(PV, dpv_ref[...])
        dq, gq = _gnorm_bwd(dnq_ref[...], col(NQ), gq_ref[...])
        put(NQ, dq)
        _acc(dgq_ref, jnp.sum(gq, axis=0, keepdims=True), first)
        dk, gk = _gnorm_bwd(dnk_ref[...], col(NK), gk_ref[...])
        put(NK, dk)
        _acc(dgk_ref, jnp.sum(gk, axis=0, keepdims=True), first)
        put(NV, dnv_ref[...])
        dm, gm = _gnorm_bwd(dmq_ref[...], col(MQ), gm_ref[...])
        put(MQ, dm)
        _acc(dgm_ref, jnp.sum(gm, axis=0, keepdims=True), first)

    blk = pl.BlockSpec((tm, BW), lambda i: (i, 0))
    vec = pl.BlockSpec((1, BW), lambda i: (0, 0))
    wide = pl.BlockSpec((tm, 9 * BW), lambda i: (i, 0))
    return pl.pallas_call(
        body, grid=(t // tm,), in_specs=[wide, blk, blk, vec, vec, vec] + [blk] * 9, out_specs=(wide, vec, vec, vec),
        out_shape=(_sds((t, 9 * BW), BF16), _sds((1, BW), F32), _sds((1, BW), F32), _sds((1, BW), F32)),
        name="prep_bwd", compiler_params=_cp())(proj, cos2, sin2, g_naq, g_nak, g_mq, d_rq, d_rk, d_rv, d_rg, d_pv, d_nq, d_nk,
                                                d_nv, d_mq)


RET_B = 256


def _ret_consts(lgf_ref, lgb_ref):
    bsz = RET_B
    head = _lane_head((1, BW))
    lf, lb = jnp.zeros((1, BW), F32), jnp.zeros((1, BW), F32)
    for h in range(NH):
        lf = lf + jnp.where(head == h, lgf_ref[h], 0.0)
        lb = lb + jnp.where(head == h, lgb_ref[h], 0.0)
    pos = lax.broadcasted_iota(jnp.int32, (bsz, BW), 0).astype(F32)
    up, down = pos + 1.0, (bsz - 1.0) - pos
    c = dict(up=up, down=down, kf=jnp.exp(down * lf), kb=jnp.exp(up * lb), qf=jnp.exp(up * lf), qb=jnp.exp(down * lb),
             cf=jnp.exp(bsz * lf), cb=jnp.exp(bsz * lb))
    diff = (lax.broadcasted_iota(jnp.int32, (NH * bsz, 1), 0) & (bsz - 1)) - lax.broadcasted_iota(jnp.int32, (1, bsz), 1)
    c["causal"] = diff >= 0
    c["dist"] = jnp.abs(diff).astype(F32)
    lgf = jnp.concatenate([jnp.full((bsz, 1), lgf_ref[h], F32) for h in range(NH)], axis=0)
    lgb = jnp.concatenate([jnp.full((bsz, 1), lgb_ref[h], F32) for h in range(NH)], axis=0)
    c["dm"] = jnp.exp(c["dist"] * jnp.where(c["causal"], lgf, lgb))
    c["bd"] = _lane_head((BW, BW)) == lax.shift_right_logical(lax.broadcasted_iota(jnp.int32, (BW, BW), 0), 6)
    return c


def _ret_states(k_ref, v_ref, st_ref, c, nb):
    bsz = RET_B

    def summary(b, decay):
        kb = k_ref[b * bsz:(b + 1) * bsz, :].astype(F32)
        return jnp.where(c["bd"], _dot(kb * decay, v_ref[b * bsz:(b + 1) * bsz, :], ta=True), 0.0)

    f = jnp.zeros((BW, BW), F32)
    for b in range(nb):
        st_ref[b] = f
        if b < nb - 1:
            f = c["cf"] * f + summary(b, c["kf"])
    g = jnp.zeros((BW, BW), F32)
    for b in reversed(range(nb)):
        st_ref[nb + b] = g
        if b > 0:
            g = c["cb"] * g + summary(b, c["kb"])


def _ret_fwd(q, k, v, proj, lgf, lgb, g_ret):
    t = q.shape[0]
    bsz, nb = RET_B, t // RET_B

    def body(lgf_ref, lgb_ref, q_ref, k_ref, v_ref, rg_ref, g_ref, o_ref, ret_ref, st_ref):
        c = _ret_consts(lgf_ref, lgb_ref)
        _ret_states(k_ref, v_ref, st_ref, c, nb)
        for b in range(nb):
            blk = slice(b * bsz, (b + 1) * bsz)
            qb, kb, vb = q_ref[blk, :], k_ref[blk, :], v_ref[blk, :]
            s = _dot(_stack_heads(qb), kb, tb=True)
            o = _unstack_heads(_dot(s * c["dm"], vb), bsz)
            q32 = qb.astype(F32)
            o = o + _dot(q32 * c["qf"], st_ref[b]) + _dot(q32 * c["qb"], st_ref[nb + b])
            o_ref[blk, :] = o
            rg = rg_ref[blk, :]
            ret_ref[blk, :] = (_gnorm(o, g_ref[...]) * (rg * _sigmoid(rg))).astype(BF16)

    whole = pl.BlockSpec((t, BW), lambda i: (0, 0))
    return pl.pallas_call(
        body, grid=(1,),
        in_specs=[SMEM, SMEM, whole, whole, whole, pl.BlockSpec((t, BW), lambda i: (0, RG)), pl.BlockSpec((1, BW), lambda i: (0, 0))],
        out_specs=(whole, whole), out_shape=(_sds((t, BW), F32), _sds((t, BW), BF16)),
        scratch_shapes=[pltpu.VMEM((2 * nb, BW, BW), F32)], name="ret_fwd", compiler_params=_cp())(lgf, lgb, q, k, v, proj, g_ret)


def _ret_post_bwd(dbr, o_ret, proj, g_ret):
    t = o_ret.shape[0]
    tm = 256

    def body(d_ref, o_ref, rg_ref, g_ref, do_ref, drg_ref, dg_ref):
        dret, o, rg, g = d_ref[...], o_ref[...], rg_ref[...], g_ref[...]
        sg = _sigmoid(rg)
        do, dgain = _gnorm_bwd(dret * (rg * sg), o, g)
        do_ref[...] = do.astype(BF16)
        drg_ref[...] = dret * _gnorm(o, g) * (sg * (1.0 + rg * (1.0 - sg)))
        _acc(dg_ref, jnp.sum(dgain, axis=0, keepdims=True), pl.program_id(0) == 0)

    blk = pl.BlockSpec((tm, BW), lambda i: (i, 0))
    vec = pl.BlockSpec((1, BW), lambda i: (0, 0))
    return pl.pallas_call(
        body, grid=(t // tm,), in_specs=[blk, blk, pl.BlockSpec((tm, BW), lambda i: (i, RG)), vec], out_specs=(blk, blk, vec),
        out_shape=(_sds((t, BW), BF16), _sds((t, BW), F32), _sds((1, BW), F32)), name="ret_post_bwd",
        compiler_params=_cp())(dbr, o_ret, proj, g_ret)


def _ret_bwd(do, q, k, v, lgf, lgb):
    t = q.shape[0]
    bsz, nb = RET_B, t // RET_B

    def body(lgf_ref, lgb_ref, d_ref, q_ref, k_ref, v_ref, dq_ref, dk_ref, dv_ref, dlg_ref, st_ref, sd_ref):
        c = _ret_consts(lgf_ref, lgb_ref)
        _ret_states(k_ref, v_ref, st_ref, c, nb)
        lane_f, lane_b = jnp.zeros((1, BW), F32), jnp.zeros((1, BW), F32)
        row_f, row_b = jnp.zeros((NH * bsz, 1), F32), jnp.zeros((NH * bsz, 1), F32)

        def rows(x):
            return jnp.sum(x, axis=0, keepdims=True)

        for b in range(nb):
            blk = slice(b * bsz, (b + 1) * bsz)
            qb, kb, vb, dob = q_ref[blk, :], k_ref[blk, :], v_ref[blk, :], d_ref[blk, :]
            q32 = qb.astype(F32)
            qs, dos = _stack_heads(qb), _stack_heads(dob)
            s = _dot(qs, kb, tb=True)
            da = _dot(dos, vb, tb=True)
            dv_ref[blk, :] = _dot(s * c["dm"], dos, ta=True)
            ds = da * c["dm"]
            w = ds * s * c["dist"]
            row_f = row_f + jnp.sum(jnp.where(c["causal"], w, 0.0), axis=1, keepdims=True)
            row_b = row_b + jnp.sum(jnp.where(c["causal"], 0.0, w), axis=1, keepdims=True)
            dsb = ds.astype(MXU)
            dk_ref[blk, :] = _dot(dsb, qs, ta=True)
            dq_f = _dot(dob, st_ref[b], tb=True) * c["qf"]
            dq_b = _dot(dob, st_ref[nb + b], tb=True) * c["qb"]
            lane_f = lane_f + rows(c["up"] * dq_f * q32)
            lane_b = lane_b + rows(c["down"] * dq_b * q32)
            dq_ref[blk, :] = _unstack_heads(_dot(dsb, kb), bsz) + dq_f + dq_b
            sd_ref[b] = jnp.where(c["bd"], _dot(q32 * c["qf"], dob, ta=True), 0.0)
            sd_ref[nb + b] = jnp.where(c["bd"], _dot(q32 * c["qb"], dob, ta=True), 0.0)

        def through_state(b, grad, decay, weight, lane):
            blk = slice(b * bsz, (b + 1) * bsz)
            k32 = k_ref[blk, :].astype(F32)
            dk = _dot(v_ref[blk, :], grad, tb=True) * decay
            dk_ref[blk, :] += dk
            dv_ref[blk, :] += _dot(k32 * decay, grad)
            return lane + rows(weight * dk * k32)

        phi = jnp.zeros((BW, BW), F32)
        for b in reversed(range(nb)):
            if b < nb - 1:
                lane_f = through_state(b, phi, c["kf"], c["down"], lane_f)
                lane_f = lane_f + bsz * rows(c["cf"] * st_ref[b] * phi)
            phi = sd_ref[b] + c["cf"] * phi
        gam = jnp.zeros((BW, BW), F32)
        for b in range(nb):
            if b > 0:
                lane_b = through_state(b, gam, c["kb"], c["up"], lane_b)
                lane_b = lane_b + bsz * rows(c["cb"] * st_ref[nb + b] * gam)
            gam = sd_ref[nb + b] + c["cb"] * gam

        head = _lane_head((1, BW))
        for h in range(NH):
            tot_f = jnp.sum(row_f[h * bsz:(h + 1) * bsz, :]) + jnp.sum(jnp.where(head == h, lane_f, 0.0))
            tot_b = jnp.sum(row_b[h * bsz:(h + 1) * bsz, :]) + jnp.sum(jnp.where(head == h, lane_b, 0.0))
            dlg_ref[h:h + 1, :] = jnp.full((1, 128), tot_f, F32)
            dlg_ref[NH + h:NH + h + 1, :] = jnp.full((1, 128), tot_b, F32)

    whole = pl.BlockSpec((t, BW), lambda i: (0, 0))
    return pl.pallas_call(
        body, grid=(1,), in_specs=[SMEM, SMEM, whole, whole, whole, whole],
        out_specs=(whole, whole, whole, pl.BlockSpec((2 * NH, 128), lambda i: (0, 0))),
        out_shape=(_sds((t, BW), F32), _sds((t, BW), F32), _sds((t, BW), F32), _sds((2 * NH, 128), F32)),
        scratch_shapes=[pltpu.VMEM((2 * nb, BW, BW), F32), pltpu.VMEM((2 * nb, BW, BW), F32)], name="ret_bwd",
        compiler_params=_cp())(lgf, lgb, do, q, k, v)


def _pool_windows(t):
    row = lax.broadcasted_iota(jnp.int32, (t, BW), 0)
    half = lax.shift_left(jnp.ones((t, BW), jnp.int32), _lane_head((t, BW)))
    cnt = (jnp.minimum(row + half, t) - jnp.maximum(row - half, 0)).astype(F32)
    return row, half, cnt


def _pool_window_sum(v, row, half, t, transpose):
    out = jnp.zeros_like(v)
    for j in range(-POOL_HALF_MAX, POOL_HALF_MAX):
        src = row - j if transpose else row + j
        ok = (src >= 0) & (src < t) & (j >= -half) & (j < half)
        out = out + jnp.where(ok, pltpu.roll(v, (j if transpose else -j) % t, 0), 0.0)
    return out


def _pool_fwd(proj, wbd, scale):
    t = proj.shape[0]

    def body(v_ref, w_ref, s_ref, o_ref):
        v = v_ref[...]
        row, half, cnt = _pool_windows(t)
        pooled = _pool_window_sum(v, row, half, t, False) / cnt - v
        o_ref[...] = (_dot(pooled, w_ref[...]) * s_ref[...]).astype(BF16)

    return pl.pallas_call(
        body, grid=(1,),
        in_specs=[pl.BlockSpec((t, BW), lambda i: (0, PV)), pl.BlockSpec((BW, BW), lambda i: (0, 0)), pl.BlockSpec((1, BW), lambda i: (0, 0))],
        out_specs=pl.BlockSpec((t, BW), lambda i: (0, 0)), out_shape=_sds((t, BW), BF16), name="pool_fwd",
        compiler_params=_cp())(proj, wbd, scale)


def _pool_bwd(dbr, proj, wbd, scale):
    t = proj.shape[0]

    def body(d_ref, v_ref, w_ref, s_ref, dv_ref, dw_ref, ds_ref):
        v, dout = v_ref[...], d_ref[...]
        row, half, cnt = _pool_windows(t)
        pooled = _pool_window_sum(v, row, half, t, False) / cnt - v
        mixed = _dot(pooled, w_ref[...])
        ds_ref[...] = jnp.sum(dout * mixed, axis=0, keepdims=True)
        dmixed = dout * s_ref[...]
        dw_ref[...] = _dot(pooled, dmixed, ta=True)
        dpooled = _dot(dmixed, w_ref[...], tb=True)
        dv_ref[...] = _pool_window_sum(dpooled / cnt, row, half, t, True) - dpooled

    return pl.pallas_call(
        body, grid=(1,),
        in_specs=[pl.BlockSpec((t, BW), lambda i: (0, 1)), pl.BlockSpec((t, BW), lambda i: (0, PV)),
                  pl.BlockSpec((BW, BW), lambda i: (0, 0)), pl.BlockSpec((1, BW), lambda i: (0, 0))],
        out_specs=(pl.BlockSpec((t, BW), lambda i: (0, 0)), pl.BlockSpec((BW, BW), lambda i: (0, 0)), pl.BlockSpec((1, BW), lambda i: (0, 0))),
        out_shape=(_sds((t, BW), F32), _sds((BW, BW), F32), _sds((1, BW), F32)), name="pool_bwd",
        compiler_params=_cp())(dbr, proj, wbd, scale)


NA_KEYS = NA_ROWS_WIN * GRID_W
NA_PAIRS = 2 * NA_ROWS_WIN - 2


def _na_window(r, n_rows):
    rs = jnp.clip(r - NA_ROWS_WIN // 2, 0, n_rows - NA_ROWS_WIN)
    return pl.multiple_of(rs * GRID_W, GRID_W), rs - r + (NA_ROWS_WIN - 1)


def _na_bias(b_ref, a0):
    return jnp.concatenate([b_ref[a0 + 2 * j] for j in range(NA_ROWS_WIN // 2)], axis=1)


NA_STEP_ROWS = 16


def _na_fwd(q, k, v, ball):
    t = q.shape[0]
    n_rows = t // GRID_W
    rows = NA_STEP_ROWS

    def body(q_ref, k_ref, v_ref, b_ref, o_ref):
        for rr in range(rows):
            start, a0 = _na_window(pl.program_id(0) * rows + rr, n_rows)
            own = slice(rr * GRID_W, (rr + 1) * GRID_W)
            qs = _stack_heads(q_ref[own, :])
            s = _dot(qs, k_ref[pl.ds(start, NA_KEYS), :], tb=True) * (HD ** -0.5) + _na_bias(b_ref, a0)
            p = _softmax_rows(s)
            o_ref[own, :] = _unstack_heads(_dot(p, v_ref[pl.ds(start, NA_KEYS), :]), GRID_W).astype(BF16)

    blk = pl.BlockSpec((rows * GRID_W, BW), lambda r: (r, 0))
    whole = pl.BlockSpec((t, BW), lambda r: (0, 0))
    return pl.pallas_call(
        body, grid=(n_rows // rows,), in_specs=[blk, whole, whole, pl.BlockSpec(ball.shape, lambda r: (0, 0, 0))],
        out_specs=blk, out_shape=_sds((t, BW), BF16), name="na_fwd", compiler_params=_cp())(q, k, v, ball)


def _na_bwd(dbr, q, k, v, ball):
    t = q.shape[0]
    n_rows = t // GRID_W

    rows = NA_STEP_ROWS

    def body(d_ref, q_ref, k_ref, v_ref, b_ref, dq_ref, dk_ref, dv_ref, db_ref):
        @pl.when(pl.program_id(0) == 0)
        def _():
            dk_ref[...] = jnp.zeros_like(dk_ref)
            dv_ref[...] = jnp.zeros_like(dv_ref)
            db_ref[...] = jnp.zeros_like(db_ref)

        for rr in range(rows):
            start, a0 = _na_window(pl.program_id(0) * rows + rr, n_rows)
            keys = pl.ds(start, NA_KEYS)
            own = slice(rr * GRID_W, (rr + 1) * GRID_W)
            qs = _stack_heads(q_ref[own, :])
            kb, vb = k_ref[keys, :], v_ref[keys, :]
            p = _softmax_rows(_dot(qs, kb, tb=True) * (HD ** -0.5) + _na_bias(b_ref, a0))
            dos = _stack_heads(d_ref[own, :]).astype(MXU)
            dp = _dot(dos, vb, tb=True)
            dv_ref[keys, :] += _dot(p, dos, ta=True)
            ds = p * (dp - jnp.sum(dp * p, axis=-1, keepdims=True))
            for j in range(NA_ROWS_WIN // 2):
                db_ref[a0 + 2 * j] += ds[:, 2 * j * GRID_W:(2 * j + 2) * GRID_W]
            dsb = (ds * (HD ** -0.5)).astype(MXU)
            dq_ref[own, :] = _unstack_heads(_dot(dsb, kb), GRID_W)
            dk_ref[keys, :] += _dot(dsb, qs, ta=True)

    blk = pl.BlockSpec((rows * GRID_W, BW), lambda r: (r, 0))
    whole = pl.BlockSpec((t, BW), lambda r: (0, 0))
    tab = pl.BlockSpec(ball.shape, lambda r: (0, 0, 0))
    return pl.pallas_call(
        body, grid=(n_rows // rows,), in_specs=[pl.BlockSpec((rows * GRID_W, BW), lambda r: (r, 2)), blk, whole, whole, tab],
        out_specs=(blk, whole, whole, tab),
        out_shape=(_sds((t, BW), F32), _sds((t, BW), F32), _sds((t, BW), F32), _sds(ball.shape, F32)), name="na_bwd",
        compiler_params=_cp())(dbr, q, k, v, ball)


def _rpb_expand(rpb_pad, onehot):
    def body(r_ref, e_ref, o_ref):
        o_ref[...] = jnp.dot(r_ref[...], e_ref[...], precision=HI, preferred_element_type=F32)

    return pl.pallas_call(body, out_shape=_sds((rpb_pad.shape[0], GRID_W * GRID_W), F32), name="rpb_expand",
                          compiler_params=_cp())(rpb_pad, onehot)


def _rpb_reduce(dtab, onehot):
    def body(d_ref, e_ref, o_ref):
        o_ref[...] = lax.dot_general(d_ref[...], e_ref[...], (((1,), (1,)), ((), ())), precision=HI, preferred_element_type=F32)

    return pl.pallas_call(body, out_shape=_sds((dtab.shape[0], 128), F32), name="rpb_reduce", compiler_params=_cp())(dtab, onehot)


MEM_TQ = 512


def _mem_fwd(q, mk, mv):
    t = q.shape[0]
    tq = MEM_TQ

    def body(q_ref, k_ref, v_ref, o_ref):
        p = _softmax_rows(_dot(_stack_heads(q_ref[...]), k_ref[...], tb=True) * (HD ** -0.5))
        o_ref[...] = _unstack_heads(_dot(p, v_ref[...]), tq).astype(BF16)

    blk = pl.BlockSpec((tq, BW), lambda i: (i, 0))
    kv = pl.BlockSpec((N_MEM, BW), lambda i: (0, 0))
    return pl.pallas_call(body, grid=(t // tq,), in_specs=[blk, kv, kv], out_specs=blk, out_shape=_sds((t, BW), BF16),
                          name="mem_fwd", compiler_params=_cp())(q, mk, mv)


def _mem_bwd(dbr, q, mk, mv):
    t = q.shape[0]
    tq = MEM_TQ

    def body(d_ref, q_ref, k_ref, v_ref, dq_ref, dk_ref, dv_ref):
        first = pl.program_id(0) == 0
        qs = _stack_heads(q_ref[...])
        dos = _stack_heads(d_ref[...]).astype(MXU)
        p = _softmax_rows(_dot(qs, k_ref[...], tb=True) * (HD ** -0.5))
        dp = _dot(dos, v_ref[...], tb=True)
        _acc(dv_ref, _dot(p, dos, ta=True), first)
        dsb = (p * (dp - jnp.sum(dp * p, axis=-1, keepdims=True)) * (HD ** -0.5)).astype(MXU)
        dq_ref[...] = _unstack_heads(_dot(dsb, k_ref[...]), tq)
        _acc(dk_ref, _dot(dsb, qs, ta=True), first)

    blk = pl.BlockSpec((tq, BW), lambda i: (i, 0))
    kv = pl.BlockSpec((N_MEM, BW), lambda i: (0, 0))
    return pl.pallas_call(
        body, grid=(t // tq,), in_specs=[pl.BlockSpec((tq, BW), lambda i: (i, 3)), blk, kv, kv], out_specs=(blk, kv, kv),
        out_shape=(_sds((t, BW), F32), _sds((N_MEM, BW), F32), _sds((N_MEM, BW), F32)), name="mem_bwd",
        compiler_params=_cp())(dbr, q, mk, mv)


def _memkv_prep(kv, g_mk):
    def body(kv_ref, g_ref, k_ref, v_ref):
        k_ref[...] = _gnorm(kv_ref[:, 0:BW], g_ref[...]).astype(BF16)
        v_ref[...] = kv_ref[:, BW:2 * BW].astype(BF16)

    return pl.pallas_call(body, out_shape=(_sds((N_MEM, BW), BF16), _sds((N_MEM, BW), BF16)), name="memkv_prep",
                          compiler_params=_cp())(kv, g_mk)


def _memkv_bwd(kv, dk, dv, g_mk):
    def body(kv_ref, dk_ref, dv_ref, g_ref, o_ref, dg_ref):
        dkk, gain = _gnorm_bwd(dk_ref[...], kv_ref[:, 0:BW], g_ref[...])
        o_ref[:, 0:BW] = dkk.astype(BF16)
        o_ref[:, BW:2 * BW] = dv_ref[...].astype(BF16)
        dg_ref[...] = jnp.sum(gain, axis=0, keepdims=True)

    return pl.pallas_call(body, out_shape=(_sds((N_MEM, 2 * BW), BF16), _sds((1, BW), F32)), name="memkv_bwd",
                          compiler_params=_cp())(kv, dk, dv, g_mk)


MERGE_TM = 512


def _merge_fwd(brs, wbt, gp):
    t = gp.shape[0]
    tm = MERGE_TM

    def body(b0, b1, b2, b3, wb_ref, gp_ref, o_ref):
        out = jnp.zeros((tm, D), F32)
        for n, b_ref in enumerate((b0, b1, b2, b3)):
            up = _dot(b_ref[...], wb_ref[n], tb=True)
            out = out + _sigmoid(gp_ref[:, n * D:(n + 1) * D].astype(F32)) * up
        o_ref[...] = out.astype(BF16)

    blk = pl.BlockSpec((tm, BW), lambda i: (i, 0))
    return pl.pallas_call(
        body, grid=(t // tm,),
        in_specs=[blk, blk, blk, blk, pl.BlockSpec((NH, D, BW), lambda i: (0, 0, 0)), pl.BlockSpec((tm, NH * D), lambda i: (i, 0))],
        out_specs=pl.BlockSpec((tm, D), lambda i: (i, 0)), out_shape=_sds((t, D), BF16), name="merge_fwd",
        compiler_params=_cp())(*brs, wbt, gp)


def _merge_bwd(dmerged, brs, wbt, gp):
    t = gp.shape[0]
    tm = MERGE_TM
    steps = t // tm

    def body(d_ref, b0, b1, b2, b3, wb_ref, gp_ref, dgp_ref, dbr_ref, dwb_ref, acc_ref):
        i = pl.program_id(0)
        dm = d_ref[...]
        for n, b_ref in enumerate((b0, b1, b2, b3)):
            br = b_ref[...]
            up = _dot(br, wb_ref[n], tb=True)
            g = _sigmoid(gp_ref[:, n * D:(n + 1) * D].astype(F32))
            dgp_ref[:, n * D:(n + 1) * D] = (dm * up * (g * (1.0 - g))).astype(BF16)
            dup = (dm * g).astype(BF16)
            dbr_ref[:, n * BW:(n + 1) * BW] = _dot(dup, wb_ref[n])
            part = _dot(dup, br, ta=True)

            @pl.when(i == 0)
            def _():
                acc_ref[n] = part

            @pl.when(i > 0)
            def _():
                acc_ref[n] += part

        @pl.when(i == steps - 1)
        def _():
            dwb_ref[...] = acc_ref[...].astype(BF16)

    row = pl.BlockSpec((tm, D), lambda i: (i, 0))
    blk = pl.BlockSpec((tm, BW), lambda i: (i, 0))
    wide = pl.BlockSpec((tm, NH * D), lambda i: (i, 0))
    whole = pl.BlockSpec((NH, D, BW), lambda i: (0, 0, 0))
    return pl.pallas_call(
        body, grid=(steps,), in_specs=[row, blk, blk, blk, blk, whole, wide], out_specs=(wide, row, whole),
        out_shape=(_sds((t, NH * D), BF16), _sds((t, NH * BW), F32), _sds((NH, D, BW), BF16)),
        scratch_shapes=[pltpu.VMEM((NH, D, BW), F32)], name="merge_bwd", compiler_params=_cp())(dmerged, *brs, wbt, gp)


FFN_TN = 256


def _ffn_in_fwd(h2, w_t):
    t = h2.shape[0]
    tm, tn = _tile(t, 2048), FFN_TN
    nj = FF // tn

    def body(x_ref, wa_ref, wg_ref, a_ref, g_ref, y_ref):
        x = x_ref[...]
        a, g = _dot(x, wa_ref[...], tb=True), _dot(x, wg_ref[...], tb=True)
        a_ref[...] = a.astype(BF16)
        g_ref[...] = g.astype(BF16)
        y_ref[...] = (a * _sigmoid(a) * g).astype(BF16)

    out = pl.BlockSpec((tm, tn), lambda i, j: (i, j))
    return pl.pallas_call(
        body, grid=(t // tm, nj),
        in_specs=[pl.BlockSpec((tm, D), lambda i, j: (i, 0)), pl.BlockSpec((tn, D), lambda i, j: (j, 0)),
                  pl.BlockSpec((tn, D), lambda i, j: (j + nj, 0))],
        out_specs=(out, out, out), out_shape=tuple(_sds((t, FF), BF16) for _ in range(3)), name="ffn_in_fwd",
        compiler_params=_cp(dimension_semantics=("parallel", "parallel")))(h2, w_t, w_t)


def _ffn_out_bwd(dx2b, w_out, a, g, dep):
    t = dx2b.shape[0]
    tm, tn = _tile(t, 2048), FFN_TN

    def body(*refs):
        x_ref, w_ref, a_ref, g_ref = refs[:4]
        da_ref, dg_ref = refs[-2:]
        d = _dot(x_ref[...], w_ref[...], tb=True)
        av, gv = a_ref[...].astype(F32), g_ref[...].astype(F32)
        s = _sigmoid(av)
        da_ref[...] = (d * gv * (s * (1.0 + av * (1.0 - s)))).astype(BF16)
        dg_ref[...] = (d * (av * s)).astype(BF16)

    blk = pl.BlockSpec((tm, tn), lambda i, j: (i, j))
    ins = [pl.BlockSpec((tm, D), lambda i, j: (i, 0)), pl.BlockSpec((tn, D), lambda i, j: (j, 0)), blk, blk]
    args = [dx2b, w_out, a, g]
    if dep is not None:
        ins.append(pl.BlockSpec((8, 128), lambda i, j: (0, 0)))
        args.append(dep)
    return pl.pallas_call(
        body, grid=(t // tm, FF // tn), in_specs=ins, out_specs=(blk, blk),
        out_shape=(_sds((t, FF), BF16), _sds((t, FF), BF16)), name="ffn_out_bwd",
        compiler_params=_cp(dimension_semantics=("parallel", "parallel")))(*args)


def _loss_head(y, target):
    t, d = y.shape
    tm = 256

    def body(y_ref, t_ref, dy_ref, dyb_ref, l_ref):
        e = y_ref[...] - t_ref[...]
        dy_ref[...] = e * (1.0 / d)
        dyb_ref[...] = (e * (1.0 / d)).astype(BF16)
        _acc(l_ref, jnp.full((8, 128), 0.5 * jnp.sum(jnp.sum(e * e, axis=-1, keepdims=True) * (1.0 / d)), F32), pl.program_id(0) == 0)

    row = pl.BlockSpec((tm, d), lambda i: (i, 0))
    return pl.pallas_call(body, grid=(t // tm,), in_specs=[row, row], out_specs=(row, row, pl.BlockSpec((8, 128), lambda i: (0, 0))),
                          out_shape=(_sds((t, d), F32), _sds((t, d), BF16), _sds((8, 128), F32)), name="loss_head",
                          compiler_params=_cp())(y, target)


def _sum_slots(x, name):
    k, r, c = x.shape
    tr = _tile(r, 512) if r % 128 == 0 else r

    def body(x_ref, o_ref):
        acc = x_ref[0].astype(F32)
        for s in range(1, k):
            acc = acc + x_ref[s].astype(F32)
        o_ref[...] = acc

    return pl.pallas_call(body, grid=(r // tr,), in_specs=[pl.BlockSpec((k, tr, c), lambda i: (0, i, 0))],
                          out_specs=pl.BlockSpec((tr, c), lambda i: (i, 0)), out_shape=_sds((r, c), F32), name=name,
                          compiler_params=_cp())(x)


def _pair_sum(bufs, recvs, cidx):
    n = len(bufs)

    def body(c_ref, *refs):
        for i in range(n):
            refs[2 * n + i][...] = (refs[i][...].astype(F32) + refs[n + i][...].astype(F32)).astype(BF16)

    return pl.pallas_call(
        body,
        grid_spec=pltpu.PrefetchScalarGridSpec(
            num_scalar_prefetch=1, grid=(4,),
            in_specs=[pl.BlockSpec((None, None) + b.shape[2:], lambda s, cref: (s, cref[0], 0, 0)) for b in bufs]
            + [pl.BlockSpec((None,) + r.shape[1:], lambda s, cref: (s, 0, 0)) for r in recvs],
            out_specs=tuple(pl.BlockSpec((None,) + r.shape[1:], lambda s, cref: (s, 0, 0)) for r in recvs)),
        out_shape=tuple(_sds(r.shape, BF16) for r in recvs), name="rs_pair_sum", compiler_params=_cp())(cidx, *bufs, *recvs)


def _adamw_update(w, gv, m, v):
    mn = ADAM_B1 * m + (1.0 - ADAM_B1) * gv
    vn = ADAM_B2 * v + (1.0 - ADAM_B2) * (gv * gv)
    m_hat = mn / (1.0 - ADAM_B1 ** ADAM_STEP)
    v_hat = vn / (1.0 - ADAM_B2 ** ADAM_STEP)
    return -ADAM_LR * (m_hat / (jnp.sqrt(v_hat) + ADAM_EPS) + ADAM_WD * w), mn, vn


def _adamw(w, g, m, v, name):
    r, c = w.shape

    def body(w_ref, g_ref, m_ref, v_ref, d_ref, nm_ref, nv_ref):
        d_ref[...], nm_ref[...], nv_ref[...] = _adamw_update(w_ref[...], g_ref[...], m_ref[...], v_ref[...])

    blk = pl.BlockSpec((r, c), lambda i: (0, 0))
    return pl.pallas_call(body, grid=(1,), in_specs=[blk] * 4, out_specs=(blk,) * 3,
                          out_shape=tuple(_sds((r, c), F32) for _ in range(3)), name=name, compiler_params=_cp())(w, g, m, v)


def _adamw_layer(layer, w, g, m, v, outs, name):
    _, r, c = w.shape
    tr = max(d for d in range(8, r + 1, 8) if r % d == 0 and d * c * 4 <= 2 ** 20)

    def body(w_ref, m_ref, v_ref, g_ref, *refs):
        d_ref, nm_ref, nv_ref, go_ref = refs[4:]
        gv = g_ref[...]
        d_ref[...], nm_ref[...], nv_ref[...] = _adamw_update(w_ref[...], gv, m_ref[...], v_ref[...])
        go_ref[...] = gv

    blk = pl.BlockSpec((None, tr, c), lambda i: (layer, i, 0))
    return pl.pallas_call(
        body, grid=(r // tr,), in_specs=[blk] * 3 + [pl.BlockSpec((tr, c), lambda i: (i, 0))] + [ANY] * 4, out_specs=(blk,) * 4,
        out_shape=tuple(_sds(w.shape, F32) for _ in range(4)), input_output_aliases={4 + j: j for j in range(4)}, name=name,
        compiler_params=_cp())(w, m, v, g, *outs)


def _all_gather(shards, name):
    n = len(shards)

    def body(*refs):
        x_refs, out_refs = refs[:n], refs[n:2 * n]
        send_sems, recv_sems, local_sems = refs[2 * n:]
        x, y, cc = lax.axis_index("x"), lax.axis_index("y"), lax.axis_index("c")
        me, sibling = (x, y, cc), (x, y, 1 - cc)
        chips = [(1 - x, y), (x, 1 - y), (1 - x, 1 - y)]

        def copy(i, k, block, to, own=False):
            px, py, pc = block
            slot = out_refs[i].at[4 * px + 2 * py + pc]
            return pltpu.make_async_remote_copy(
                src_ref=x_refs[i] if own else slot, dst_ref=slot, send_sem=send_sems.at[7 * i + k],
                recv_sem=recv_sems.at[7 * i + k], device_id=to, device_id_type=MESH)

        mine = [pltpu.make_async_copy(x_refs[i], out_refs[i].at[4 * x + 2 * y + cc], local_sems.at[i]) for i in range(n)]
        for cp in mine:
            cp.start()
        first = []
        for j, chip in enumerate(chips):
            first += [copy(i, 1 + j, me, (*chip, cc), own=True) for i in range(n)]
        first += [copy(i, 0, me, sibling, own=True) for i in range(n)]
        for cp in first:
            cp.start()
        passed = []
        for j, chip in enumerate(chips):
            for i in range(n):
                copy(i, 1 + j, (*chip, cc), me).wait_recv()
                cp = copy(i, 4 + j, (*chip, cc), sibling)
                cp.start()
                passed.append(cp)
        for i in range(n):
            copy(i, 0, sibling, me).wait_recv()
        for j, chip in enumerate(chips):
            for i in range(n):
                copy(i, 4 + j, (*chip, 1 - cc), me).wait_recv()
        for cp in first + passed:
            cp.wait_send()
        for cp in mine:
            cp.wait()

    return pl.pallas_call(
        body, out_shape=tuple(_sds((N_DEV,) + s.shape, s.dtype) for s in shards), in_specs=[ANY] * n, out_specs=(ANY,) * n,
        scratch_shapes=[pltpu.SemaphoreType.DMA((7 * n,)), pltpu.SemaphoreType.DMA((7 * n,)), pltpu.SemaphoreType.DMA((n,))],
        name=name)(*shards)


def _rs_core_swap(bufs, name):
    n = len(bufs)

    def body(*refs):
        b_refs, recv_refs = refs[:n], refs[n:2 * n]
        send_sems, recv_sems = refs[2 * n:]
        x, y, cc = lax.axis_index("x"), lax.axis_index("y"), lax.axis_index("c")
        copies = [pltpu.make_async_remote_copy(
            src_ref=b_refs[i].at[s, 1 - cc], dst_ref=recv_refs[i].at[s], send_sem=send_sems.at[4 * i + s],
            recv_sem=recv_sems.at[4 * i + s], device_id=(x, y, 1 - cc), device_id_type=MESH) for i in range(n) for s in range(4)]
        for cp in copies:
            cp.start()
        for cp in copies:
            cp.wait()

    return pl.pallas_call(
        body, out_shape=tuple(_sds((4,) + b.shape[2:], b.dtype) for b in bufs), in_specs=[ANY] * n, out_specs=(ANY,) * n,
        scratch_shapes=[pltpu.SemaphoreType.DMA((4 * n,)), pltpu.SemaphoreType.DMA((4 * n,))], name=name)(*bufs)


HBM = pl.BlockSpec(memory_space=pltpu.HBM)
SEMS = pl.BlockSpec(memory_space=pltpu.SEMAPHORE)
EFFECT = pltpu.SideEffectType.DATAFLOW_SIDE_EFFECTING


def _hbm(a):
    return pltpu.HBM(a.shape, a.dtype)


def _other_chips(x, y):
    return [(1 - x, y), (x, 1 - y), (1 - x, 1 - y)]


def _ici_start(srcs, lands, mode, name, group=None):
    n = len(srcs)

    def body(*refs):
        s_refs, land_refs = refs[:n], refs[n:2 * n]
        send_sems, recv_sems = refs[2 * n], refs[2 * n + 1]
        token = refs[-1]
        x, y, cc = lax.axis_index("x"), lax.axis_index("y"), lax.axis_index("c")
        mine = 2 * x + y if mode == "by_chip" else 4 * x + 2 * y + cc
        peers = [(px, py, cc) for px, py in _other_chips(x, y)]
        if mode == "by_device":
            peers = [(x, y, 1 - cc)] + peers + [(px, py, 1 - cc) for px, py in _other_chips(x, y)]
        first = 0
        for size in ([n] if group is None else group):
            first += size
            for px, py, pc in peers:
                for i in range(first - size, first):
                    src = s_refs[i]
                    if mode == "by_chip":
                        src = src.at[2 * px + py]
                    elif mode == "by_device":
                        src = src.at[4 * px + 2 * py + pc]
                    pltpu.make_async_remote_copy(
                        src_ref=src, dst_ref=land_refs[i].at[mine], send_sem=send_sems.at[i], recv_sem=recv_sems.at[i],
                        device_id=(px, py, pc), device_id_type=MESH).start()
        token[...] = jnp.zeros_like(token)

    out = pl.pallas_call(
        body, name=name,
        out_shape=(pltpu.SemaphoreType.DMA((n,)), pltpu.SemaphoreType.DMA((n,)), *[_hbm(s) for s in srcs], *[_hbm(l) for l in lands],
                   _sds((8, 128), F32)),
        in_specs=[HBM] * (2 * n), out_specs=(SEMS, SEMS, *[HBM] * (2 * n), pl.BlockSpec(memory_space=pltpu.VMEM)),
        input_output_aliases={i: 2 + i for i in range(2 * n)}, compiler_params=pltpu.CompilerParams(has_side_effects=EFFECT),
    )(*[pltpu.with_memory_space_constraint(s, pltpu.HBM) for s in srcs],
      *[pltpu.with_memory_space_constraint(l, pltpu.HBM) for l in lands])
    return out[0], out[1], out[2:2 + n], out[2 + n:2 + 2 * n], out[-1], 7 if mode == "by_device" else 3


def _ici_wait(started, after, name, only=None):
    send_sems, recv_sems, srcs, lands, _, copies = started
    only = list(range(len(srcs))) if only is None else only
    srcs, lands = [srcs[i] for i in only], [lands[i] for i in only]
    n = len(srcs)

    def body(*refs):
        land_refs = refs[n:2 * n]
        send_sems, recv_sems = refs[2 * n], refs[2 * n + 1]
        x, y, cc = lax.axis_index("x"), lax.axis_index("y"), lax.axis_index("c")
        for i in range(n):
            three = land_refs[i].at[pl.ds(0, copies)]
            cp = pltpu.make_async_remote_copy(src_ref=three, dst_ref=three, send_sem=send_sems.at[only[i]],
                                              recv_sem=recv_sems.at[only[i]],
                                              device_id=(x, y, cc), device_id_type=MESH)
            cp.wait_send()
            cp.wait_recv()

    return pl.pallas_call(
        body, name=name, out_shape=tuple(_hbm(l) for l in lands), in_specs=[HBM] * (2 * n) + [SEMS, SEMS, ANY],
        out_specs=tuple([HBM] * n), input_output_aliases={n + i: i for i in range(n)},
        compiler_params=pltpu.CompilerParams(has_side_effects=EFFECT))(*srcs, *lands, send_sems, recv_sems, after)


def _gather_d2d(blocks, lands, name):
    n = len(blocks)

    def body(*refs):
        x_refs, land_refs = refs[:n], refs[2 * n:3 * n]
        send_sems, recv_sems, in_sems, out_sems = refs[3 * n:3 * n + 4]
        stage = refs[3 * n + 4:]
        x, y, cc = lax.axis_index("x"), lax.axis_index("y"), lax.axis_index("c")
        sibling = (x, y, 1 - cc)
        staged = [pltpu.make_async_copy(x_refs[i], stage[i], in_sems.at[i]) for i in range(n)]
        for cp in staged:
            cp.start()
        copies = []
        for i in range(n):
            slot = land_refs[i].at[4 * x + 2 * y + cc]
            copies.append(pltpu.make_async_remote_copy(src_ref=x_refs[i], dst_ref=slot, send_sem=send_sems.at[4 * i],
                                                       recv_sem=recv_sems.at[4 * i], device_id=sibling, device_id_type=MESH))
            for j, (px, py) in enumerate(_other_chips(x, y)):
                slot = land_refs[i].at[4 * px + 2 * py + cc]
                copies.append(pltpu.make_async_remote_copy(src_ref=slot, dst_ref=slot, send_sem=send_sems.at[4 * i + 1 + j],
                                                           recv_sem=recv_sems.at[4 * i + 1 + j], device_id=sibling, device_id_type=MESH))
        for cp in copies:
            cp.start()
        mine = []
        for i in range(n):
            staged[i].wait()
            mine.append(pltpu.make_async_copy(stage[i], land_refs[i].at[4 * x + 2 * y + cc], out_sems.at[i]))
            mine[i].start()
        for i in range(n):
            slot = land_refs[i].at[4 * x + 2 * y + (1 - cc)]
            pltpu.make_async_remote_copy(src_ref=slot, dst_ref=slot, send_sem=send_sems.at[4 * i], recv_sem=recv_sems.at[4 * i],
                                         device_id=sibling, device_id_type=MESH).wait_recv()
            for j, (px, py) in enumerate(_other_chips(x, y)):
                slot = land_refs[i].at[4 * px + 2 * py + (1 - cc)]
                pltpu.make_async_remote_copy(src_ref=slot, dst_ref=slot, send_sem=send_sems.at[4 * i + 1 + j],
                                             recv_sem=recv_sems.at[4 * i + 1 + j], device_id=sibling, device_id_type=MESH).wait_recv()
        for cp in copies:
            cp.wait_send()
        for cp in mine:
            cp.wait()

    return pl.pallas_call(
        body, out_shape=tuple(_sds(l.shape, l.dtype) for l in lands), in_specs=[ANY] * (2 * n), out_specs=(ANY,) * n,
        input_output_aliases={n + i: i for i in range(n)},
        scratch_shapes=[pltpu.SemaphoreType.DMA((4 * n,)), pltpu.SemaphoreType.DMA((4 * n,)), pltpu.SemaphoreType.DMA((n,)),
                        pltpu.SemaphoreType.DMA((n,))] + [pltpu.VMEM(b.shape, b.dtype) for b in blocks],
        name=name, compiler_params=_cp())(*blocks, *lands)


def _sum_own(parts, recvs, mine, name):
    n = len(parts)

    def body(c_ref, *refs):
        s = pl.program_id(0)
        for i in range(n):
            val = jnp.where(c_ref[0] == s, refs[i][...], refs[n + i][...]).astype(F32)
            _acc(refs[2 * n + i], val, s == 0)

    kept = [pl.BlockSpec((None,) + p.shape[1:], lambda s, cref: (cref[0], 0, 0)) for p in parts]
    ins = [pl.BlockSpec((None,) + p.shape[1:], lambda s, cref: (s, 0, 0)) for p in parts]
    return pl.pallas_call(
        body, grid_spec=pltpu.PrefetchScalarGridSpec(
            num_scalar_prefetch=1, grid=(parts[0].shape[0],), in_specs=kept + ins,
            out_specs=tuple(pl.BlockSpec(p.shape[1:], lambda s, cref: (0, 0)) for p in parts)),
        out_shape=tuple(_sds(p.shape[1:], F32) for p in parts), name=name, compiler_params=_cp())(mine, *parts, *recvs)


BIG = (("w_in", True), ("w_gate", True), ("w_mem_kv", False), ("w_branch", True), ("w_out", False), ("w_ffn_in", True),
       ("w_ffn_out", False))

SMALL = ("norm_mix_g", "norm_mem_g", "ret_decay_fwd", "ret_decay_bwd", "ret_norm_g", "pool_w", "pool_scale", "na_q_norm_g",
         "na_k_norm_g", "na_rpb", "mem_q_norm_g", "mem_k_norm_g", "norm_ffn_g")
WEIGHTS = ("norm_mix_g", "norm_mem_g", "w_in", "w_gate", "ret_decay_fwd", "ret_decay_bwd", "ret_norm_g", "pool_w", "pool_scale",
           "na_q_norm_g", "na_k_norm_g", "na_rpb", "mem_q_norm_g", "mem_k_norm_g", "w_mem_kv", "w_branch", "w_out", "norm_ffn_g",
           "w_ffn_in", "w_ffn_out")


def _to_exchange(name, transposed, shard):
    if name == "w_branch":
        return jnp.swapaxes(shard, 1, 2).reshape(NH * (D // N_DEV), BW)
    return shard.T if transposed else shard


def _from_exchange(name, transposed, block):
    if name == "w_branch":
        return jnp.swapaxes(block.reshape(NH, D // N_DEV, BW), 1, 2)
    return block.T if transposed else block


def _whole_from_gathered(name, g):
    if name == "w_branch":
        return jnp.swapaxes(g.reshape(N_DEV, NH, D // N_DEV, BW), 0, 1).reshape(NH, D, BW)
    return g.reshape(N_DEV * g.shape[1], g.shape[2])


def _by_destination(name, g):
    if name == "w_branch":
        g = jnp.swapaxes(g.reshape(NH, N_DEV, D // N_DEV, BW), 0, 1).reshape(N_DEV * NH * (D // N_DEV), BW)
    return g.reshape(4, 2, g.shape[0] // N_DEV, g.shape[1])


SMALL_PAD = 1024


def _pack_small(vals, loss=None):
    parts = [vals[n] for n in SMALL] + [jnp.zeros((1,), F32) if loss is None else loss.reshape(1)]
    rows = []
    for p in parts:
        flat = p.reshape(-1)
        rows.append(jnp.pad(flat, (0, -flat.shape[0] % SMALL_PAD)).reshape(-1, 128))
    return jnp.concatenate(rows, axis=0)


def _unpack_small(packed, like):
    out, off = {}, 0
    for n in SMALL:
        sz = int(np.prod(like[n].shape))
        nrow = -(-sz // SMALL_PAD) * (SMALL_PAD // 128)
        out[n] = packed[off:off + nrow].reshape(-1)[:sz].reshape(like[n].shape)
        off += nrow
    return out, packed[off, 0]


def _na_constants():
    c = np.arange(GRID_W)
    win = np.clip(c - NA_COLS_WIN // 2, 0, GRID_W - NA_COLS_WIN)
    kc = np.arange(GRID_W)
    inside = (kc[None, :] >= win[:, None]) & (kc[None, :] < win[:, None] + NA_COLS_WIN)
    off = kc[None, :] - c[:, None] + NA_COLS_WIN - 1
    onehot = np.zeros((128, GRID_W, GRID_W), np.float32)
    for b in range(2 * NA_COLS_WIN - 1):
        onehot[b] = (off == b) & inside
    maskadd = np.where(inside, 0.0, NEG).astype(np.float32)
    return onehot.reshape(128, GRID_W * GRID_W), maskadd


def _na_bias_table(tab, maskadd):
    n_off = 2 * NA_ROWS_WIN - 1
    t4 = tab[:NH * n_off].reshape(NH, n_off, GRID_W, GRID_W) + maskadd[None, None]
    by_off = t4.transpose(1, 0, 2, 3).reshape(n_off, NH * GRID_W, GRID_W)
    return jnp.concatenate([by_off[:-1], by_off[1:]], axis=-1)


def _rotary_tables(t):
    half = HD // 2
    inv = ROPE_THETA ** (-jnp.arange(half, dtype=F32) / half)
    ang = jnp.arange(t, dtype=F32)[:, None] * inv[None, :]
    cos, sin = jnp.cos(ang), jnp.sin(ang)
    return jnp.tile(jnp.concatenate([cos, cos], axis=-1), (1, NH)), jnp.tile(jnp.concatenate([-sin, sin], axis=-1), (1, NH))


def _block_diag(pw):
    out = jnp.zeros((BW, BW), pw.dtype)
    for g in range(NH):
        out = lax.dynamic_update_slice(out, pw[g], (g * HD, g * HD))
    return out


def _tile4(g):
    return jnp.tile(g.reshape(1, HD), (1, NH))


def _layer_fwd(x, mem, sw, lw, consts, fetch, h=None, next_norm_g=None):
    cos2, sin2, onehot, maskadd = consts
    if h is None:
        h = _rmsnorm_fwd(x, sw["norm_mix_g"].reshape(1, D), "norm_mix_fwd")
    proj = _mm(h, lw["w_in"], tb=True, name="mm_in")
    gp = _mm(h, lw["w_gate"], tb=True, out_dtype=BF16, name="mm_gate")
    g_naq, g_nak, g_mq = _tile4(sw["na_q_norm_g"]), _tile4(sw["na_k_norm_g"]), _tile4(sw["mem_q_norm_g"])
    rq, rk, rv, nq, nk, nv, mq = _prep_fwd(proj, cos2, sin2, g_naq, g_nak, g_mq)

    lgf, lgb = jax.nn.log_sigmoid(sw["ret_decay_fwd"]), jax.nn.log_sigmoid(sw["ret_decay_bwd"])
    g_ret = sw["ret_norm_g"].reshape(1, BW)
    o_ret, ret = _ret_fwd(rq, rk, rv, proj, lgf, lgb, g_ret)

    wbd = _block_diag(sw["pool_w"]).astype(BF16)
    p_scale = sw["pool_scale"].reshape(1, BW)
    pool = _pool_fwd(proj, wbd, p_scale)

    rpb_pad = jnp.pad(sw["na_rpb"].reshape(NH * 15, 31), ((0, 4), (0, 97)))
    ball = _na_bias_table(_rpb_expand(rpb_pad, onehot), maskadd)
    na = _na_fwd(nq, nk, nv, ball)

    lw.update(fetch(1, na))
    memn = _rmsnorm_fwd(mem, sw["norm_mem_g"].reshape(1, D), "norm_mem_fwd")
    kv = _mm(memn, lw["w_mem_kv"], name="mm_memkv")
    g_mk = _tile4(sw["mem_k_norm_g"])
    mk, mv = _memkv_prep(kv, g_mk)
    mo = _mem_fwd(mq, mk, mv)

    br = (ret, pool, na, mo)
    merged = _merge_fwd(br, lw["w_branch"], gp)
    x1, h2 = _mm(merged, lw["w_out"], add=x, norm_g=sw["norm_ffn_g"].reshape(1, D), name="mm_out")
    lw.update(fetch(2, x1))
    ffa, ffg, yff = _ffn_in_fwd(h2, lw["w_ffn_in"])
    if next_norm_g is None:
        x2, h_next = _mm(yff, lw["w_ffn_out"], add=x1, name="mm_ffn_out"), None
    else:
        x2, h_next = _mm(yff, lw["w_ffn_out"], add=x1, norm_g=next_norm_g.reshape(1, D), name="mm_ffn_out")
    saved = dict(x=x, h=h, proj=proj, gp=gp, rq=rq, rk=rk, rv=rv, nq=nq, nk=nk, nv=nv, mq=mq, o_ret=o_ret, ball=ball, memn=memn,
                 kv=kv, mk=mk, mv=mv, br=br, merged=merged, x1=x1, h2=h2, ffa=ffa, ffg=ffg, yff=yff, lgf=lgf, lgb=lgb, wbd=wbd)
    return x2, h_next, saved


def _layer_bwd(dx2, dx2b, mem, sw, lw, sv, consts, dep=None):
    cos2, sin2, onehot, maskadd = consts
    gb, gs = {}, {}
    d_a, d_g = _ffn_out_bwd(dx2b, lw["w_ffn_out"], sv["ffa"], sv["ffg"], dep)
    gb["w_ffn_out"] = _mm(sv["yff"], dx2b, ta=True, out_dtype=BF16, name="mm_ffn_out_dw")
    dh2 = _mm(d_a, lw["w_ffn_in"], b_half=0, name="mm_ffn_in_dx_a")
    dx1, dx1b, dg = _mm_norm_bwd(d_g, lw["w_ffn_in"], dh2, sv["x1"], sw["norm_ffn_g"].reshape(1, D), dx2, b_half=1,
                                 name="mm_ffn_in_dx_g")
    gs["norm_ffn_g"] = dg.reshape(D)
    dw_a = _mm(d_a, sv["h2"], ta=True, out_dtype=BF16, out_half=(0, None), name="mm_ffn_in_dw_a")
    gb["w_ffn_in"] = _mm(d_g, sv["h2"], ta=True, out_dtype=BF16, out_half=(1, dw_a), name="mm_ffn_in_dw_g")

    dmerged = _mm(dx1b, lw["w_out"], tb=True, name="mm_out_dx")
    gb["w_out"] = _mm(sv["merged"], dx1b, ta=True, out_dtype=BF16, name="mm_out_dw")
    dgp, dbr, gb["w_branch"] = _merge_bwd(dmerged, sv["br"], lw["w_branch"], sv["gp"])

    g_ret = sw["ret_norm_g"].reshape(1, BW)
    do_ret, d_rg, dg_ret = _ret_post_bwd(dbr, sv["o_ret"], sv["proj"], g_ret)
    d_rq, d_rk, d_rv, dlg = _ret_bwd(do_ret, sv["rq"], sv["rk"], sv["rv"], sv["lgf"], sv["lgb"])
    gs["ret_norm_g"] = dg_ret.reshape(BW)
    _, vjp_f = jax.vjp(jax.nn.log_sigmoid, sw["ret_decay_fwd"])
    _, vjp_b = jax.vjp(jax.nn.log_sigmoid, sw["ret_decay_bwd"])
    gs["ret_decay_fwd"] = vjp_f(dlg[0:NH, 0])[0]
    gs["ret_decay_bwd"] = vjp_b(dlg[NH:2 * NH, 0])[0]

    p_scale = sw["pool_scale"].reshape(1, BW)
    d_pv, dwbd, dscale = _pool_bwd(dbr, sv["proj"], sv["wbd"], p_scale)
    gs["pool_w"] = jnp.stack([dwbd[g * HD:(g + 1) * HD, g * HD:(g + 1) * HD] for g in range(NH)])
    gs["pool_scale"] = dscale.reshape(BW)

    d_nq, d_nk, d_nv, dball = _na_bwd(dbr, sv["nq"], sv["nk"], sv["nv"], sv["ball"])
    _, vjp_tab = jax.vjp(lambda tab: _na_bias_table(tab, maskadd), jnp.zeros((64, GRID_W * GRID_W), F32))
    drpb = _rpb_reduce(vjp_tab(dball)[0], onehot)
    gs["na_rpb"] = drpb[:NH * 15, :31].reshape(NH, 15, 31)

    d_mq, d_mk, d_mv = _mem_bwd(dbr, sv["mq"], sv["mk"], sv["mv"])
    g_mk = _tile4(sw["mem_k_norm_g"])
    dkv, dg_mk = _memkv_bwd(sv["kv"], d_mk, d_mv, g_mk)
    gs["mem_k_norm_g"] = dg_mk.reshape(NH, HD).sum(0)
    gb["w_mem_kv"] = _mm(sv["memn"], dkv, ta=True, out_dtype=BF16, name="mm_memkv_dw")
    dmemn = _mm(dkv, lw["w_mem_kv"], tb=True, name="mm_memkv_dx")
    _, _, dg_mem = _rmsnorm_bwd(dmemn, mem, sw["norm_mem_g"].reshape(1, D), jnp.zeros_like(mem), "norm_mem_bwd")
    gs["norm_mem_g"] = dg_mem.reshape(D)

    g_naq, g_nak, g_mq = _tile4(sw["na_q_norm_g"]), _tile4(sw["na_k_norm_g"]), _tile4(sw["mem_q_norm_g"])
    dproj, dg_naq, dg_nak, dg_mq = _prep_bwd(sv["proj"], cos2, sin2, g_naq, g_nak, g_mq, d_rq, d_rk, d_rv, d_rg, d_pv, d_nq, d_nk,
                                             d_nv, d_mq)
    gs["na_q_norm_g"] = dg_naq.reshape(NH, HD).sum(0)
    gs["na_k_norm_g"] = dg_nak.reshape(NH, HD).sum(0)
    gs["mem_q_norm_g"] = dg_mq.reshape(NH, HD).sum(0)

    gb["w_in"] = _mm(dproj, sv["h"], ta=True, out_dtype=BF16, name="mm_in_dw")
    gb["w_gate"] = _mm(dgp, sv["h"], ta=True, out_dtype=BF16, name="mm_gate_dw")
    dh = _mm(dproj, lw["w_in"], name="mm_in_dx")
    dx, dxb, dg = _mm_norm_bwd(dgp, lw["w_gate"], dh, sv["x"], sw["norm_mix_g"].reshape(1, D), dx1, name="mm_gate_dx")
    gs["norm_mix_g"] = dg.reshape(D)
    return dx, dxb, gb, gs


def _local_step(x, mem, target, small, get_layer, on_grads):
    t = x.shape[0]
    cos2, sin2 = _rotary_tables(t)
    onehot, maskadd = _na_constants()
    consts = (cos2, sin2, jnp.asarray(onehot), jnp.asarray(maskadd))
    saved, weights, cur, h = [], [], x, None
    for l in range(DEPTH):
        sw = {n: small[n][l] for n in SMALL}
        lw, fetch = get_layer(l, cur)
        weights.append(lw)
        cur, h, sv = _layer_fwd(cur, mem, sw, lw, consts, fetch, h, small["norm_mix_g"][l + 1] if l + 1 < DEPTH else None)
        saved.append(sv)
    dy, dyb, loss_tile = _loss_head(cur, target)
    small_g = {n: [None] * DEPTH for n in SMALL}
    dep = None
    for l in reversed(range(DEPTH)):
        sw = {n: small[n][l] for n in SMALL}
        dy, dyb, gb, gs = _layer_bwd(dy, dyb, mem, sw, weights[l], saved[l], consts, dep)
        dep = on_grads(l, gb, dy)
        for n in SMALL:
            small_g[n][l] = gs[n]
    return loss_tile[0, 0], dy, {n: jnp.stack(v) for n, v in small_g.items()}


def _flat2d(a):
    return a.reshape(-1, a.shape[-1])


def kernel(x, mem, norm_mix_g, norm_mem_g, w_in, w_gate, ret_decay_fwd, ret_decay_bwd, ret_norm_g, pool_w, pool_scale, na_q_norm_g, na_k_norm_g, na_rpb, mem_q_norm_g, mem_k_norm_g, w_mem_kv, w_branch, w_out, norm_ffn_g, w_ffn_in, w_ffn_out, loss_target, m_norm_mix_g, m_norm_mem_g, m_w_in, m_w_gate, m_ret_decay_fwd, m_ret_decay_bwd, m_ret_norm_g, m_pool_w, m_pool_scale, m_na_q_norm_g, m_na_k_norm_g, m_na_rpb, m_mem_q_norm_g, m_mem_k_norm_g, m_w_mem_kv, m_w_branch, m_w_out, m_norm_ffn_g, m_w_ffn_in, m_w_ffn_out, v_norm_mix_g, v_norm_mem_g, v_w_in, v_w_gate, v_ret_decay_fwd, v_ret_decay_bwd, v_ret_norm_g, v_pool_w, v_pool_scale, v_na_q_norm_g, v_na_k_norm_g, v_na_rpb, v_mem_q_norm_g, v_mem_k_norm_g, v_w_mem_kv, v_w_branch, v_w_out, v_norm_ffn_g, v_w_ffn_in, v_w_ffn_out):
    w = dict(norm_mix_g=norm_mix_g, norm_mem_g=norm_mem_g, w_in=w_in, w_gate=w_gate, ret_decay_fwd=ret_decay_fwd,
             ret_decay_bwd=ret_decay_bwd, ret_norm_g=ret_norm_g, pool_w=pool_w, pool_scale=pool_scale, na_q_norm_g=na_q_norm_g,
             na_k_norm_g=na_k_norm_g, na_rpb=na_rpb, mem_q_norm_g=mem_q_norm_g, mem_k_norm_g=mem_k_norm_g, w_mem_kv=w_mem_kv,
             w_branch=w_branch, w_out=w_out, norm_ffn_g=norm_ffn_g, w_ffn_in=w_ffn_in, w_ffn_out=w_ffn_out)
    m = dict(norm_mix_g=m_norm_mix_g, norm_mem_g=m_norm_mem_g, w_in=m_w_in, w_gate=m_w_gate, ret_decay_fwd=m_ret_decay_fwd,
             ret_decay_bwd=m_ret_decay_bwd, ret_norm_g=m_ret_norm_g, pool_w=m_pool_w, pool_scale=m_pool_scale, na_q_norm_g=m_na_q_norm_g,
             na_k_norm_g=m_na_k_norm_g, na_rpb=m_na_rpb, mem_q_norm_g=m_mem_q_norm_g, mem_k_norm_g=m_mem_k_norm_g, w_mem_kv=m_w_mem_kv,
             w_branch=m_w_branch, w_out=m_w_out, norm_ffn_g=m_norm_ffn_g, w_ffn_in=m_w_ffn_in, w_ffn_out=m_w_ffn_out)
    v = dict(norm_mix_g=v_norm_mix_g, norm_mem_g=v_norm_mem_g, w_in=v_w_in, w_gate=v_w_gate, ret_decay_fwd=v_ret_decay_fwd,
             ret_decay_bwd=v_ret_decay_bwd, ret_norm_g=v_ret_norm_g, pool_w=v_pool_w, pool_scale=v_pool_scale, na_q_norm_g=v_na_q_norm_g,
             na_k_norm_g=v_na_k_norm_g, na_rpb=v_na_rpb, mem_q_norm_g=v_mem_q_norm_g, mem_k_norm_g=v_mem_k_norm_g, w_mem_kv=v_w_mem_kv,
             w_branch=v_w_branch, w_out=v_w_out, norm_ffn_g=v_norm_ffn_g, w_ffn_in=v_w_ffn_in, w_ffn_out=v_w_ffn_out)
    assert x.shape == (1, 2048, D) and mem.shape == (1, N_MEM, D) and w_in.shape == (DEPTH, D, 9 * BW // N_DEV)

    first_groups = [[0, 1], [2, 3, 4], [5, 6]]
    blocks = [_to_exchange(name, tr, w[name][l]).astype(BF16) for l in range(DEPTH) for name, tr in BIG]
    started = _ici_start(blocks, [lax.empty((N_DEV,) + b.shape, BF16) for b in blocks], "gather", "gather_ici_start",
                         [len(g) for g in first_groups] + [len(BIG)] * (DEPTH - 1))

    def get_group(l, only, after, tag):
        at = [l * len(BIG) + i for i in only]
        lands = _ici_wait(started, after, "gather_ici_wait_%d%s" % (l, tag), at)
        whole = _gather_d2d([started[2][i] for i in at], lands, "gather_d2d")
        return {BIG[i][0]: _whole_from_gathered(BIG[i][0], g) for i, g in zip(only, whole)}

    def get_layer(l, after):
        if l > 0:
            return get_group(l, list(range(len(BIG))), after, ""), lambda stage, after2: {}
        return (get_group(l, first_groups[0], started[4], "a"),
                lambda stage, after2: get_group(l, first_groups[stage], after2, "abc"[stage]))

    cidx = lax.axis_index("c").astype(jnp.int32).reshape(1)
    chip = (2 * lax.axis_index("x") + lax.axis_index("y")).astype(jnp.int32).reshape(1)
    in_flight = []

    def flip_of(name, tr):
        return (lambda a: jnp.swapaxes(a, 1, 2)) if name in ("w_in", "w_ffn_in") else (lambda a: a)

    def rows3(a):
        return a.reshape(DEPTH, -1, a.shape[-1])

    opt_in = {name: tuple(rows3(flip_of(name, tr)(t[name])) for t in (w, m, v)) for name, tr in BIG}
    opt_out = {name: tuple(lax.empty(opt_in[name][0].shape, F32) for _ in range(4)) for name, _ in BIG}

    device = (2 * chip + cidx).astype(jnp.int32)

    def finish(l, st, after):
        recv = _ici_wait(st, after, "rs_ici_wait_%d" % l)
        sums = _sum_own(st[2], recv, chip if st[5] == 3 else device, "rs_sum")
        for (name, tr), s in zip(BIG, sums):
            g = s if name in ("w_in", "w_ffn_in") else _from_exchange(name, tr, s)
            wx, mx, vx = opt_in[name]
            opt_out[name] = _adamw_layer(l, wx, g.reshape(-1, g.shape[-1]), mx, vx, opt_out[name], "adamw_" + name)

    def on_grads(l, gb, after):
        send = [_by_destination(name, gb[name]) for name, _ in BIG]
        if l > 0:
            send = [s.reshape((N_DEV,) + s.shape[2:]) for s in send]
            st = _ici_start(send, [lax.empty(s.shape, BF16) for s in send], "by_device", "rs_ici_start_%d" % l)
        else:
            from_core = _rs_core_swap(send, "rs_core_swap")
            chip_part = _pair_sum(send, from_core, cidx)
            st = _ici_start(chip_part, [lax.empty(p.shape, BF16) for p in chip_part], "by_chip", "rs_ici_start_%d" % l)
        in_flight.append((l, st))
        return st[4]

    loss_local, dx, small_g = _local_step(x[0], mem[0], loss_target[0], {n: w[n] for n in SMALL}, get_layer, on_grads)

    last_started = in_flight[-1][1][4]
    for l, st in in_flight[:-1]:
        finish(l, st, last_started)

    small_all, = _all_gather([_pack_small(small_g, loss_local) + last_started[0:1]], "gather_small")
    packed_g = _sum_slots(small_all, "small_sum")
    small_sum, loss = _unpack_small(packed_g, {n: w[n] for n in SMALL})
    d_, m_, v_ = _adamw(_pack_small({n: w[n] for n in SMALL}), packed_g, _pack_small({n: m[n] for n in SMALL}),
                        _pack_small({n: v[n] for n in SMALL}), "adamw_small")
    updated = d_[0:8]
    for name, _ in BIG:
        updated = updated + opt_out[name][0][1, 0:8, 0:128]
    finish(*in_flight[-1], updated)

    grads, delta, new_m, new_v = {}, {}, {}, {}
    for name, tr in BIG:
        shape = flip_of(name, tr)(w[name]).shape
        delta[name], new_m[name], new_v[name], grads[name] = (flip_of(name, tr)(a.reshape(shape)) for a in opt_out[name])
    like = {n: w[n] for n in SMALL}
    ds, _ = _unpack_small(d_, like)
    ms, _ = _unpack_small(m_, like)
    vs, _ = _unpack_small(v_, like)
    for n in SMALL:
        grads[n], delta[n], new_m[n], new_v[n] = small_sum[n], ds[n], ms[n], vs[n]

    return (loss, dx[None], *[grads[n] for n in WEIGHTS], *[delta[n] for n in WEIGHTS], *[new_m[n] for n in WEIGHTS],
            *[new_v[n] for n in WEIGHTS])
```

```python
import functools

import numpy as np
import jax
import jax.numpy as jnp
from jax import lax
from jax.experimental import pallas as pl
from jax.experimental.pallas import tpu as pltpu

F32 = jnp.float32
BF16 = jnp.bfloat16
MXU = jnp.bfloat16
HI = lax.Precision.HIGHEST

DEPTH = 4
D = 1024
BW = 256
HD = 64
NH = 4
GRID_W = 64
NA_ROWS_WIN = 8
NA_COLS_WIN = 16
N_MEM = 256
FF = 2816
EPS = 1e-6
NEG = -1e30
ROPE_THETA = 10000.0
POOL_HALF_MAX = 8

ADAM_LR, ADAM_B1, ADAM_B2, ADAM_EPS, ADAM_WD, ADAM_STEP = 0.001, 0.9, 0.999, 1e-08, 0.01, 10

N_DEV = 8
VMEM_LIMIT = 56 * 1024 * 1024
MM_VMEM_BUDGET = 40 * 1024 * 1024

RQ, RK, RV, RG, PV, NQ, NK, NV, MQ = range(9)

MESH = pl.DeviceIdType.MESH
ANY = pl.BlockSpec(memory_space=pl.ANY)
SMEM = pl.BlockSpec(memory_space=pltpu.SMEM)


def _cp(**kw):
    return pltpu.CompilerParams(vmem_limit_bytes=VMEM_LIMIT, **kw)


def _tile(n, cap):
    if n <= cap:
        return n
    best = None
    for t in range(128, cap + 1, 128):
        if n % t == 0:
            best = t
    assert best is not None, (n, cap)
    return best


def _sds(shape, dtype):
    return jax.ShapeDtypeStruct(shape, dtype)


def _lane_head(shape):
    return lax.shift_right_logical(lax.broadcasted_iota(jnp.int32, shape, len(shape) - 1), 6)


def _group_mean(z):
    i = lax.shift_right_logical(lax.broadcasted_iota(jnp.int32, (BW, BW), 0), 6)
    j = lax.shift_right_logical(lax.broadcasted_iota(jnp.int32, (BW, BW), 1), 6)
    g = jnp.where(i == j, 1.0 / HD, 0.0).astype(BF16)
    z_hi = z.astype(BF16)
    z_lo = (z - z_hi.astype(F32)).astype(BF16)
    return jnp.dot(z_hi, g, preferred_element_type=F32) + jnp.dot(z_lo, g, preferred_element_type=F32)


def _gnorm(t, g):
    r = lax.rsqrt(_group_mean(t * t) + EPS)
    return t * r * g


def _gnorm_bwd(dy, t, g):
    r = lax.rsqrt(_group_mean(t * t) + EPS)
    th = t * r
    dth = dy * g
    dt = r * (dth - th * _group_mean(dth * th))
    return dt, dy * th


def _swap_halves(t):
    lane = lax.broadcasted_iota(jnp.int32, t.shape, 1)
    return jnp.where((lane & 63) < 32, pltpu.roll(t, BW - 32, 1), pltpu.roll(t, 32, 1))


def _sigmoid(x):
    return 1.0 / (1.0 + jnp.exp(-x))


def _dot(a, b, ta=False, tb=False):
    return lax.dot_general(a.astype(MXU), b.astype(MXU), (((0 if ta else 1,), (1 if tb else 0,)), ((), ())),
                           preferred_element_type=F32)


def _stack_heads(t):
    head = _lane_head(t.shape)
    return jnp.concatenate([jnp.where(head == h, t, jnp.zeros_like(t)) for h in range(NH)], axis=0)


def _unstack_heads(t, rows):
    head = _lane_head((rows, BW))
    out = jnp.zeros((rows, BW), F32)
    for h in range(NH):
        out = out + jnp.where(head == h, t[h * rows:(h + 1) * rows], 0.0)
    return out


def _softmax_rows(s):
    m = jnp.max(s, axis=-1, keepdims=True)
    e = jnp.exp(s - m)
    return e / jnp.sum(e, axis=-1, keepdims=True)


def _acc(ref, val, first):
    @pl.when(first)
    def _():
        ref[...] = val

    @pl.when(jnp.logical_not(first))
    def _():
        ref[...] += val


def _mm(a, b, *, ta=False, tb=False, out_dtype=F32, add=None, dep=None, b_half=None, out_half=None, norm_g=None, name):
    m, k = (a.shape[1], a.shape[0]) if ta else a.shape
    n = b.shape[0] if tb else b.shape[1]
    assert b_half is None or (not tb and b.shape[0] == 2 * k)
    tm, tn = _tile(m, 1408), (n if norm_g is not None else _tile(n, 768))
    if not ta and m <= 2048:
        blocks = (m * k * a.dtype.itemsize + k * tn * b.dtype.itemsize + m * tn * jnp.dtype(out_dtype).itemsize
                  + (m * tn * 4 if add is not None else 0) + (m * tn * 2 if norm_g is not None else 0))
        if 2 * blocks <= MM_VMEM_BUDGET:
            tm = m
    n_in = 2 + (add is not None) + (dep is not None) + (out_half is not None) + (norm_g is not None)

    def body(*refs):
        a_ref, b_ref, o_ref = refs[0], refs[1], refs[n_in]
        r = _dot(a_ref[...], b_ref[...], ta, tb)
        if add is not None:
            r = r + refs[2][...]
        o_ref[...] = r.astype(out_dtype)
        if norm_g is not None:
            scale = lax.rsqrt(jnp.mean(r * r, axis=-1, keepdims=True) + EPS)
            refs[n_in + 1][...] = (r * scale * refs[n_in - 1][...]).astype(BF16)

    kb = 0 if b_half is None else b_half
    a_spec = pl.BlockSpec((k, tm), lambda i, j: (0, i)) if ta else pl.BlockSpec((tm, k), lambda i, j: (i, 0))
    b_spec = pl.BlockSpec((tn, k), lambda i, j: (j, 0)) if tb else pl.BlockSpec((k, tn), lambda i, j: (kb, j))
    plain = pl.BlockSpec((tm, tn), lambda i, j: (i, j))
    ins, args = [a_spec, b_spec], [a, b]
    if add is not None:
        ins.append(plain)
        args.append(add)
    if dep is not None:
        ins.append(pl.BlockSpec((8, 128), lambda i, j: (0, 0)))
        args.append(dep)
    o_spec, o_shape, aliases = plain, _sds((m, n), out_dtype), {}
    if out_half is not None:
        half, prev = out_half
        o_spec = pl.BlockSpec((tm, tn), lambda i, j: (i + half * (m // tm), j))
        o_shape = _sds((2 * m, n), out_dtype)
        ins.append(ANY)
        args.append(lax.empty((2 * m, n), out_dtype) if prev is None else prev)
        aliases = {len(args) - 1: 0}
    if norm_g is not None:
        ins.append(pl.BlockSpec((1, n), lambda i, j: (0, 0)))
        args.append(norm_g)
        o_spec, o_shape = (o_spec, plain), (o_shape, _sds((m, n), BF16))
    return pl.pallas_call(
        body, grid=(m // tm, n // tn), in_specs=ins, out_specs=o_spec, out_shape=o_shape, input_output_aliases=aliases, name=name,
        compiler_params=_cp(dimension_semantics=("parallel", "parallel")))(*args)


def _mm_norm_bwd(a, b, add, x, g, res, *, b_half=None, name):
    m, k = a.shape
    n = b.shape[1]
    tm = 512
    kb = 0 if b_half is None else b_half

    def body(a_ref, b_ref, c_ref, x_ref, g_ref, res_ref, dx_ref, dxb_ref, dg_ref):
        dhv = _dot(a_ref[...], b_ref[...]) + c_ref[...]
        xv = x_ref[...]
        r = lax.rsqrt(jnp.mean(xv * xv, axis=-1, keepdims=True) + EPS)
        xh = xv * r
        dxh = dhv * g_ref[...]
        dx = res_ref[...] + r * (dxh - xh * jnp.mean(dxh * xh, axis=-1, keepdims=True))
        dx_ref[...] = dx
        dxb_ref[...] = dx.astype(BF16)
        _acc(dg_ref, jnp.sum(dhv * xh, axis=0, keepdims=True), pl.program_id(0) == 0)

    row = pl.BlockSpec((tm, n), lambda i: (i, 0))
    vec = pl.BlockSpec((1, n), lambda i: (0, 0))
    return pl.pallas_call(
        body, grid=(m // tm,),
        in_specs=[pl.BlockSpec((tm, k), lambda i: (i, 0)), pl.BlockSpec((k, n), lambda i: (kb, 0)), row, row, vec, row],
        out_specs=(row, row, vec), out_shape=(_sds((m, n), F32), _sds((m, n), BF16), _sds((1, n), F32)), name=name,
        compiler_params=_cp())(a, b, add, x, g, res)


def _rmsnorm_fwd(x, g, name):
    t, d = x.shape
    tm = _tile(t, 512)

    def body(x_ref, g_ref, o_ref):
        xv = x_ref[...]
        r = lax.rsqrt(jnp.mean(xv * xv, axis=-1, keepdims=True) + EPS)
        o_ref[...] = (xv * r * g_ref[...]).astype(o_ref.dtype)

    return pl.pallas_call(
        body, grid=(t // tm,), in_specs=[pl.BlockSpec((tm, d), lambda i: (i, 0)), pl.BlockSpec((1, d), lambda i: (0, 0))],
        out_specs=pl.BlockSpec((tm, d), lambda i: (i, 0)), out_shape=_sds((t, d), BF16), name=name, compiler_params=_cp())(x, g)


def _rmsnorm_bwd(dh, x, g, res, name):
    t, d = x.shape
    tm = _tile(t, 512)

    def body(dh_ref, x_ref, g_ref, res_ref, dx_ref, dxb_ref, dg_ref):
        xv = x_ref[...]
        dhv = dh_ref[...]
        r = lax.rsqrt(jnp.mean(xv * xv, axis=-1, keepdims=True) + EPS)
        xh = xv * r
        dxh = dhv * g_ref[...]
        dx = res_ref[...] + r * (dxh - xh * jnp.mean(dxh * xh, axis=-1, keepdims=True))
        dx_ref[...] = dx
        dxb_ref[...] = dx.astype(BF16)
        _acc(dg_ref, jnp.sum(dhv * xh, axis=0, keepdims=True), pl.program_id(0) == 0)

    row = pl.BlockSpec((tm, d), lambda i: (i, 0))
    vec = pl.BlockSpec((1, d), lambda i: (0, 0))
    return pl.pallas_call(
        body, grid=(t // tm,), in_specs=[row, row, vec, row], out_specs=(row, row, vec),
        out_shape=(_sds((t, d), F32), _sds((t, d), BF16), _sds((1, d), F32)), name=name, compiler_params=_cp())(dh, x, g, res)


def _prep_fwd(proj, cos2, sin2, g_naq, g_nak, g_mq):
    t = proj.shape[0]
    tm = 512

    def body(p_ref, cos_ref, sin_ref, gq_ref, gk_ref, gm_ref, rq_ref, rk_ref, rv_ref, nq_ref, nk_ref, nv_ref, mq_ref):
        def col(c):
            return p_ref[:, c * BW:(c + 1) * BW]

        cosv, sinv = cos_ref[...], sin_ref[...]

        def rot(tv):
            return tv * cosv + _swap_halves(tv) * sinv

        rq_ref[...] = (rot(col(RQ)) * (HD ** -0.5)).astype(BF16)
        rk_ref[...] = rot(col(RK)).astype(BF16)
        rv_ref[...] = col(RV).astype(BF16)
        nq_ref[...] = _gnorm(col(NQ), gq_ref[...]).astype(BF16)
        nk_ref[...] = _gnorm(col(NK), gk_ref[...]).astype(BF16)
        nv_ref[...] = col(NV).astype(BF16)
        mq_ref[...] = _gnorm(col(MQ), gm_ref[...]).astype(BF16)

    blk = pl.BlockSpec((tm, BW), lambda i: (i, 0))
    vec = pl.BlockSpec((1, BW), lambda i: (0, 0))
    return pl.pallas_call(
        body, grid=(t // tm,), in_specs=[pl.BlockSpec((tm, 9 * BW), lambda i: (i, 0)), blk, blk, vec, vec, vec],
        out_specs=tuple(blk for _ in range(7)), out_shape=tuple(_sds((t, BW), BF16) for _ in range(7)),
        name="prep_fwd", compiler_params=_cp())(proj, cos2, sin2, g_naq, g_nak, g_mq)


def _prep_bwd(proj, cos2, sin2, g_naq, g_nak, g_mq, d_rq, d_rk, d_rv, d_rg, d_pv, d_nq, d_nk, d_nv, d_mq):
    t = proj.shape[0]
    tm = 512

    def body(p_ref, cos_ref, sin_ref, gq_ref, gk_ref, gm_ref, drq_ref, drk_ref, drv_ref, drg_ref, dpv_ref, dnq_ref, dnk_ref,
             dnv_ref, dmq_ref, o_ref, dgq_ref, dgk_ref, dgm_ref):
        first = pl.program_id(0) == 0

        def col(c):
            return p_ref[:, c * BW:(c + 1) * BW]

        def put(c, v):
            o_ref[:, c * BW:(c + 1) * BW] = v.astype(BF16)

        cosv, sinv = cos_ref[...], sin_ref[...]

        def rot_t(dv):
            return dv * cosv + _swap_halves(dv * sinv)

        put(RQ, rot_t(drq_ref[...] * (HD ** -0.5)))
        put(RK, rot_t(drk_ref[...]))
        put(RV, drv_ref[...])
        put(RG, drg_ref[...])
        put(PV, dpv_ref[...])
        dq, gq = _gnorm_bwd(dnq_ref[...], col(NQ), gq_ref[...])
        put(NQ, dq)
        _acc(dgq_ref, jnp.sum(gq, axis=0, keepdims=True), first)
        dk, gk = _gnorm_bwd(dnk_ref[...], col(NK), gk_ref[...])
        put(NK, dk)
        _acc(dgk_ref, jnp.sum(gk, axis=0, keepdims=True), first)
        put(NV, dnv_ref[...])
        dm, gm = _gnorm_bwd(dmq_ref[...], col(MQ), gm_ref[...])
        put(MQ, dm)
        _acc(dgm_ref, jnp.sum(gm, axis=0, keepdims=True), first)

    blk = pl.BlockSpec((tm, BW), lambda i: (i, 0))
    vec = pl.BlockSpec((1, BW), lambda i: (0, 0))
    wide = pl.BlockSpec((tm, 9 * BW), lambda i: (i, 0))
    return pl.pallas_call(
        body, grid=(t // tm,), in_specs=[wide, blk, blk, vec, vec, vec] + [blk] * 9, out_specs=(wide, vec, vec, vec),
        out_shape=(_sds((t, 9 * BW), BF16), _sds((1, BW), F32), _sds((1, BW), F32), _sds((1, BW), F32)),
        name="prep_bwd", compiler_params=_cp())(proj, cos2, sin2, g_naq, g_nak, g_mq, d_rq, d_rk, d_rv, d_rg, d_pv, d_nq, d_nk,
                                                d_nv, d_mq)


RET_B = 256


def _ret_consts(lgf_ref, lgb_ref):
    bsz = RET_B
    head = _lane_head((1, BW))
    lf, lb = jnp.zeros((1, BW), F32), jnp.zeros((1, BW), F32)
    for h in range(NH):
        lf = lf + jnp.where(head == h, lgf_ref[h], 0.0)
        lb = lb + jnp.where(head == h, lgb_ref[h], 0.0)
    pos = lax.broadcasted_iota(jnp.int32, (bsz, BW), 0).astype(F32)
    up, down = pos + 1.0, (bsz - 1.0) - pos
    c = dict(up=up, down=down, kf=jnp.exp(down * lf), kb=jnp.exp(up * lb), qf=jnp.exp(up * lf), qb=jnp.exp(down * lb),
             cf=jnp.exp(bsz * lf), cb=jnp.exp(bsz * lb))
    diff = (lax.broadcasted_iota(jnp.int32, (NH * bsz, 1), 0) & (bsz - 1)) - lax.broadcasted_iota(jnp.int32, (1, bsz), 1)
    c["causal"] = diff >= 0
    c["dist"] = jnp.abs(diff).astype(F32)
    lgf = jnp.concatenate([jnp.full((bsz, 1), lgf_ref[h], F32) for h in range(NH)], axis=0)
    lgb = jnp.concatenate([jnp.full((bsz, 1), lgb_ref[h], F32) for h in range(NH)], axis=0)
    c["dm"] = jnp.exp(c["dist"] * jnp.where(c["causal"], lgf, lgb))
    c["bd"] = _lane_head((BW, BW)) == lax.shift_right_logical(lax.broadcasted_iota(jnp.int32, (BW, BW), 0), 6)
    return c


def _ret_states(k_ref, v_ref, st_ref, c, nb):
    bsz = RET_B

    def summary(b, decay):
        kb = k_ref[b * bsz:(b + 1) * bsz, :].astype(F32)
        return jnp.where(c["bd"], _dot(kb * decay, v_ref[b * bsz:(b + 1) * bsz, :], ta=True), 0.0)

    f = jnp.zeros((BW, BW), F32)
    for b in range(nb):
        st_ref[b] = f
        if b < nb - 1:
            f = c["cf"] * f + summary(b, c["kf"])
    g = jnp.zeros((BW, BW), F32)
    for b in reversed(range(nb)):
        st_ref[nb + b] = g
        if b > 0:
            g = c["cb"] * g + summary(b, c["kb"])


def _ret_fwd(q, k, v, proj, lgf, lgb, g_ret):
    t = q.shape[0]
    bsz, nb = RET_B, t // RET_B

    def body(lgf_ref, lgb_ref, q_ref, k_ref, v_ref, rg_ref, g_ref, o_ref, ret_ref, st_ref):
        c = _ret_consts(lgf_ref, lgb_ref)
        _ret_states(k_ref, v_ref, st_ref, c, nb)
        for b in range(nb):
            blk = slice(b * bsz, (b + 1) * bsz)
            qb, kb, vb = q_ref[blk, :], k_ref[blk, :], v_ref[blk, :]
            s = _dot(_stack_heads(qb), kb, tb=True)
            o = _unstack_heads(_dot(s * c["dm"], vb), bsz)
            q32 = qb.astype(F32)
            o = o + _dot(q32 * c["qf"], st_ref[b]) + _dot(q32 * c["qb"], st_ref[nb + b])
            o_ref[blk, :] = o
            rg = rg_ref[blk, :]
            ret_ref[blk, :] = (_gnorm(o, g_ref[...]) * (rg * _sigmoid(rg))).astype(BF16)

    whole = pl.BlockSpec((t, BW), lambda i: (0, 0))
    return pl.pallas_call(
        body, grid=(1,),
        in_specs=[SMEM, SMEM, whole, whole, whole, pl.BlockSpec((t, BW), lambda i: (0, RG)), pl.BlockSpec((1, BW), lambda i: (0, 0))],
        out_specs=(whole, whole), out_shape=(_sds((t, BW), F32), _sds((t, BW), BF16)),
        scratch_shapes=[pltpu.VMEM((2 * nb, BW, BW), F32)], name="ret_fwd", compiler_params=_cp())(lgf, lgb, q, k, v, proj, g_ret)


def _ret_post_bwd(dbr, o_ret, proj, g_ret):
    t = o_ret.shape[0]
    tm = 512

    def body(d_ref, o_ref, rg_ref, g_ref, do_ref, drg_ref, dg_ref):
        dret, o, rg, g = d_ref[...], o_ref[...], rg_ref[...], g_ref[...]
        sg = _sigmoid(rg)
        do, dgain = _gnorm_bwd(dret * (rg * sg), o, g)
        do_ref[...] = do.astype(BF16)
        drg_ref[...] = dret * _gnorm(o, g) * (sg * (1.0 + rg * (1.0 - sg)))
        _acc(dg_ref, jnp.sum(dgain, axis=0, keepdims=True), pl.program_id(0) == 0)

    blk = pl.BlockSpec((tm, BW), lambda i: (i, 0))
    vec = pl.BlockSpec((1, BW), lambda i: (0, 0))
    return pl.pallas_call(
        body, grid=(t // tm,), in_specs=[blk, blk, pl.BlockSpec((tm, BW), lambda i: (i, RG)), vec], out_specs=(blk, blk, vec),
        out_shape=(_sds((t, BW), BF16), _sds((t, BW), F32), _sds((1, BW), F32)), name="ret_post_bwd",
        compiler_params=_cp())(dbr, o_ret, proj, g_ret)


def _ret_bwd(do, q, k, v, lgf, lgb):
    t = q.shape[0]
    bsz, nb = RET_B, t // RET_B

    def body(lgf_ref, lgb_ref, d_ref, q_ref, k_ref, v_ref, dq_ref, dk_ref, dv_ref, dlg_ref, st_ref, sd_ref):
        c = _ret_consts(lgf_ref, lgb_ref)
        _ret_states(k_ref, v_ref, st_ref, c, nb)
        lane_f, lane_b = jnp.zeros((1, BW), F32), jnp.zeros((1, BW), F32)
        row_f, row_b = jnp.zeros((NH * bsz, 1), F32), jnp.zeros((NH * bsz, 1), F32)

        def rows(x):
            return jnp.sum(x, axis=0, keepdims=True)

        for b in range(nb):
            blk = slice(b * bsz, (b + 1) * bsz)
            qb, kb, vb, dob = q_ref[blk, :], k_ref[blk, :], v_ref[blk, :], d_ref[blk, :]
            q32 = qb.astype(F32)
            qs, dos = _stack_heads(qb), _stack_heads(dob)
            s = _dot(qs, kb, tb=True)
            da = _dot(dos, vb, tb=True)
            dv_ref[blk, :] = _dot(s * c["dm"], dos, ta=True)
            ds = da * c["dm"]
            w = ds * s * c["dist"]
            row_f = row_f + jnp.sum(jnp.where(c["causal"], w, 0.0), axis=1, keepdims=True)
            row_b = row_b + jnp.sum(jnp.where(c["causal"], 0.0, w), axis=1, keepdims=True)
            dsb = ds.astype(MXU)
            dk_ref[blk, :] = _dot(dsb, qs, ta=True)
            dq_f = _dot(dob, st_ref[b], tb=True) * c["qf"]
            dq_b = _dot(dob, st_ref[nb + b], tb=True) * c["qb"]
            lane_f = lane_f + rows(c["up"] * dq_f * q32)
            lane_b = lane_b + rows(c["down"] * dq_b * q32)
            dq_ref[blk, :] = _unstack_heads(_dot(dsb, kb), bsz) + dq_f + dq_b
            sd_ref[b] = jnp.where(c["bd"], _dot(q32 * c["qf"], dob, ta=True), 0.0)
            sd_ref[nb + b] = jnp.where(c["bd"], _dot(q32 * c["qb"], dob, ta=True), 0.0)

        def through_state(b, grad, decay, weight, lane):
            blk = slice(b * bsz, (b + 1) * bsz)
            k32 = k_ref[blk, :].astype(F32)
            dk = _dot(v_ref[blk, :], grad, tb=True) * decay
            dk_ref[blk, :] += dk
            dv_ref[blk, :] += _dot(k32 * decay, grad)
            return lane + rows(weight * dk * k32)

        phi = jnp.zeros((BW, BW), F32)
        for b in reversed(range(nb)):
            if b < nb - 1:
                lane_f = through_state(b, phi, c["kf"], c["down"], lane_f)
                lane_f = lane_f + bsz * rows(c["cf"] * st_ref[b] * phi)
            phi = sd_ref[b] + c["cf"] * phi
        gam = jnp.zeros((BW, BW), F32)
        for b in range(nb):
            if b > 0:
                lane_b = through_state(b, gam, c["kb"], c["up"], lane_b)
                lane_b = lane_b + bsz * rows(c["cb"] * st_ref[nb + b] * gam)
            gam = sd_ref[nb + b] + c["cb"] * gam

        head = _lane_head((1, BW))
        for h in range(NH):
            tot_f = jnp.sum(row_f[h * bsz:(h + 1) * bsz, :]) + jnp.sum(jnp.where(head == h, lane_f, 0.0))
            tot_b = jnp.sum(row_b[h * bsz:(h + 1) * bsz, :]) + jnp.sum(jnp.where(head == h, lane_b, 0.0))
            dlg_ref[h:h + 1, :] = jnp.full((1, 128), tot_f, F32)
            dlg_ref[NH + h:NH + h + 1, :] = jnp.full((1, 128), tot_b, F32)

    whole = pl.BlockSpec((t, BW), lambda i: (0, 0))
    return pl.pallas_call(
        body, grid=(1,), in_specs=[SMEM, SMEM, whole, whole, whole, whole],
        out_specs=(whole, whole, whole, pl.BlockSpec((2 * NH, 128), lambda i: (0, 0))),
        out_shape=(_sds((t, BW), F32), _sds((t, BW), F32), _sds((t, BW), F32), _sds((2 * NH, 128), F32)),
        scratch_shapes=[pltpu.VMEM((2 * nb, BW, BW), F32), pltpu.VMEM((2 * nb, BW, BW), F32)], name="ret_bwd",
        compiler_params=_cp())(lgf, lgb, do, q, k, v)


def _pool_windows(t):
    row = lax.broadcasted_iota(jnp.int32, (t, BW), 0)
    half = lax.shift_left(jnp.ones((t, BW), jnp.int32), _lane_head((t, BW)))
    cnt = (jnp.minimum(row + half, t) - jnp.maximum(row - half, 0)).astype(F32)
    return row, half, cnt


def _pool_window_sum(v, row, half, t, transpose):
    out = jnp.zeros_like(v)
    for j in range(-POOL_HALF_MAX, POOL_HALF_MAX):
        src = row - j if transpose else row + j
        ok = (src >= 0) & (src < t) & (j >= -half) & (j < half)
        out = out + jnp.where(ok, pltpu.roll(v, (j if transpose else -j) % t, 0), 0.0)
    return out


def _pool_fwd(proj, wbd, scale):
    t = proj.shape[0]

    def body(v_ref, w_ref, s_ref, o_ref):
        v = v_ref[...]
        row, half, cnt = _pool_windows(t)
        pooled = _pool_window_sum(v, row, half, t, False) / cnt - v
        o_ref[...] = (_dot(pooled, w_ref[...]) * s_ref[...]).astype(BF16)

    return pl.pallas_call(
        body, grid=(1,),
        in_specs=[pl.BlockSpec((t, BW), lambda i: (0, PV)), pl.BlockSpec((BW, BW), lambda i: (0, 0)), pl.BlockSpec((1, BW), lambda i: (0, 0))],
        out_specs=pl.BlockSpec((t, BW), lambda i: (0, 0)), out_shape=_sds((t, BW), BF16), name="pool_fwd",
        compiler_params=_cp())(proj, wbd, scale)


def _pool_bwd(dbr, proj, wbd, scale):
    t = proj.shape[0]

    def body(d_ref, v_ref, w_ref, s_ref, dv_ref, dw_ref, ds_ref):
        v, dout = v_ref[...], d_ref[...]
        row, half, cnt = _pool_windows(t)
        pooled = _pool_window_sum(v, row, half, t, False) / cnt - v
        mixed = _dot(pooled, w_ref[...])
        ds_ref[...] = jnp.sum(dout * mixed, axis=0, keepdims=True)
        dmixed = dout * s_ref[...]
        dw_ref[...] = _dot(pooled, dmixed, ta=True)
        dpooled = _dot(dmixed, w_ref[...], tb=True)
        dv_ref[...] = _pool_window_sum(dpooled / cnt, row, half, t, True) - dpooled

    return pl.pallas_call(
        body, grid=(1,),
        in_specs=[pl.BlockSpec((t, BW), lambda i: (0, 1)), pl.BlockSpec((t, BW), lambda i: (0, PV)),
                  pl.BlockSpec((BW, BW), lambda i: (0, 0)), pl.BlockSpec((1, BW), lambda i: (0, 0))],
        out_specs=(pl.BlockSpec((t, BW), lambda i: (0, 0)), pl.BlockSpec((BW, BW), lambda i: (0, 0)), pl.BlockSpec((1, BW), lambda i: (0, 0))),
        out_shape=(_sds((t, BW), F32), _sds((BW, BW), F32), _sds((1, BW), F32)), name="pool_bwd",
        compiler_params=_cp())(dbr, proj, wbd, scale)


NA_KEYS = NA_ROWS_WIN * GRID_W
NA_PAIRS = 2 * NA_ROWS_WIN - 2


def _na_window(r, n_rows):
    rs = jnp.clip(r - NA_ROWS_WIN // 2, 0, n_rows - NA_ROWS_WIN)
    return pl.multiple_of(rs * GRID_W, GRID_W), rs - r + (NA_ROWS_WIN - 1)


def _na_bias(b_ref, a0):
    return jnp.concatenate([b_ref[a0 + 2 * j] for j in range(NA_ROWS_WIN // 2)], axis=1)


NA_STEP_ROWS = 16


def _na_fwd(q, k, v, ball):
    t = q.shape[0]
    n_rows = t // GRID_W
    rows = NA_STEP_ROWS

    def body(q_ref, k_ref, v_ref, b_ref, o_ref):
        for rr in range(rows):
            start, a0 = _na_window(pl.program_id(0) * rows + rr, n_rows)
            own = slice(rr * GRID_W, (rr + 1) * GRID_W)
            qs = _stack_heads(q_ref[own, :])
            s = _dot(qs, k_ref[pl.ds(start, NA_KEYS), :], tb=True) * (HD ** -0.5) + _na_bias(b_ref, a0)
            p = _softmax_rows(s)
            o_ref[own, :] = _unstack_heads(_dot(p, v_ref[pl.ds(start, NA_KEYS), :]), GRID_W).astype(BF16)

    blk = pl.BlockSpec((rows * GRID_W, BW), lambda r: (r, 0))
    whole = pl.BlockSpec((t, BW), lambda r: (0, 0))
    return pl.pallas_call(
        body, grid=(n_rows // rows,), in_specs=[blk, whole, whole, pl.BlockSpec(ball.shape, lambda r: (0, 0, 0))],
        out_specs=blk, out_shape=_sds((t, BW), BF16), name="na_fwd", compiler_params=_cp())(q, k, v, ball)


def _na_bwd(dbr, q, k, v, ball):
    t = q.shape[0]
    n_rows = t // GRID_W

    rows = NA_STEP_ROWS

    def body(d_ref, q_ref, k_ref, v_ref, b_ref, dq_ref, dk_ref, dv_ref, db_ref):
        @pl.when(pl.program_id(0) == 0)
        def _():
            dk_ref[...] = jnp.zeros_like(dk_ref)
            dv_ref[...] = jnp.zeros_like(dv_ref)
            db_ref[...] = jnp.zeros_like(db_ref)

        for rr in range(rows):
            start, a0 = _na_window(pl.program_id(0) * rows + rr, n_rows)
            keys = pl.ds(start, NA_KEYS)
            own = slice(rr * GRID_W, (rr + 1) * GRID_W)
            qs = _stack_heads(q_ref[own, :])
            kb, vb = k_ref[keys, :], v_ref[keys, :]
            p = _softmax_rows(_dot(qs, kb, tb=True) * (HD ** -0.5) + _na_bias(b_ref, a0))
            dos = _stack_heads(d_ref[own, :]).astype(MXU)
            dp = _dot(dos, vb, tb=True)
            dv_ref[keys, :] += _dot(p, dos, ta=True)
            ds = p * (dp - jnp.sum(dp * p, axis=-1, keepdims=True))
            for j in range(NA_ROWS_WIN // 2):
                db_ref[a0 + 2 * j] += ds[:, 2 * j * GRID_W:(2 * j + 2) * GRID_W]
            dsb = (ds * (HD ** -0.5)).astype(MXU)
            dq_ref[own, :] = _unstack_heads(_dot(dsb, kb), GRID_W)
            dk_ref[keys, :] += _dot(dsb, qs, ta=True)

    blk = pl.BlockSpec((rows * GRID_W, BW), lambda r: (r, 0))
    whole = pl.BlockSpec((t, BW), lambda r: (0, 0))
    tab = pl.BlockSpec(ball.shape, lambda r: (0, 0, 0))
    return pl.pallas_call(
        body, grid=(n_rows // rows,), in_specs=[pl.BlockSpec((rows * GRID_W, BW), lambda r: (r, 2)), blk, whole, whole, tab],
        out_specs=(blk, whole, whole, tab),
        out_shape=(_sds((t, BW), F32), _sds((t, BW), F32), _sds((t, BW), F32), _sds(ball.shape, F32)), name="na_bwd",
        compiler_params=_cp())(dbr, q, k, v, ball)


def _rpb_expand(rpb_pad, onehot):
    def body(r_ref, e_ref, o_ref):
        o_ref[...] = jnp.dot(r_ref[...], e_ref[...], precision=HI, preferred_element_type=F32)

    return pl.pallas_call(body, out_shape=_sds((rpb_pad.shape[0], GRID_W * GRID_W), F32), name="rpb_expand",
                          compiler_params=_cp())(rpb_pad, onehot)


def _rpb_reduce(dtab, onehot):
    def body(d_ref, e_ref, o_ref):
        o_ref[...] = lax.dot_general(d_ref[...], e_ref[...], (((1,), (1,)), ((), ())), precision=HI, preferred_element_type=F32)

    return pl.pallas_call(body, out_shape=_sds((dtab.shape[0], 128), F32), name="rpb_reduce", compiler_params=_cp())(dtab, onehot)


MEM_TQ = 512


def _mem_fwd(q, mk, mv):
    t = q.shape[0]
    tq = MEM_TQ

    def body(q_ref, k_ref, v_ref, o_ref):
        p = _softmax_rows(_dot(_stack_heads(q_ref[...]), k_ref[...], tb=True) * (HD ** -0.5))
        o_ref[...] = _unstack_heads(_dot(p, v_ref[...]), tq).astype(BF16)

    blk = pl.BlockSpec((tq, BW), lambda i: (i, 0))
    kv = pl.BlockSpec((N_MEM, BW), lambda i: (0, 0))
    return pl.pallas_call(body, grid=(t // tq,), in_specs=[blk, kv, kv], out_specs=blk, out_shape=_sds((t, BW), BF16),
                          name="mem_fwd", compiler_params=_cp())(q, mk, mv)


def _mem_bwd(dbr, q, mk, mv):
    t = q.shape[0]
    tq = MEM_TQ

    def body(d_ref, q_ref, k_ref, v_ref, dq_ref, dk_ref, dv_ref):
        first = pl.program_id(0) == 0
        qs = _stack_heads(q_ref[...])
        dos = _stack_heads(d_ref[...]).astype(MXU)
        p = _softmax_rows(_dot(qs, k_ref[...], tb=True) * (HD ** -0.5))
        dp = _dot(dos, v_ref[...], tb=True)
        _acc(dv_ref, _dot(p, dos, ta=True), first)
        dsb = (p * (dp - jnp.sum(dp * p, axis=-1, keepdims=True)) * (HD ** -0.5)).astype(MXU)
        dq_ref[...] = _unstack_heads(_dot(dsb, k_ref[...]), tq)
        _acc(dk_ref, _dot(dsb, qs, ta=True), first)

    blk = pl.BlockSpec((tq, BW), lambda i: (i, 0))
    kv = pl.BlockSpec((N_MEM, BW), lambda i: (0, 0))
    return pl.pallas_call(
        body, grid=(t // tq,), in_specs=[pl.BlockSpec((tq, BW), lambda i: (i, 3)), blk, kv, kv], out_specs=(blk, kv, kv),
        out_shape=(_sds((t, BW), F32), _sds((N_MEM, BW), F32), _sds((N_MEM, BW), F32)), name="mem_bwd",
        compiler_params=_cp())(dbr, q, mk, mv)


def _memkv_prep(kv, g_mk):
    def body(kv_ref, g_ref, k_ref, v_ref):
        k_ref[...] = _gnorm(kv_ref[:, 0:BW], g_ref[...]).astype(BF16)
        v_ref[...] = kv_ref[:, BW:2 * BW].astype(BF16)

    return pl.pallas_call(body, out_shape=(_sds((N_MEM, BW), BF16), _sds((N_MEM, BW), BF16)), name="memkv_prep",
                          compiler_params=_cp())(kv, g_mk)


def _memkv_bwd(kv, dk, dv, g_mk):
    def body(kv_ref, dk_ref, dv_ref, g_ref, o_ref, dg_ref):
        dkk, gain = _gnorm_bwd(dk_ref[...], kv_ref[:, 0:BW], g_ref[...])
        o_ref[:, 0:BW] = dkk.astype(BF16)
        o_ref[:, BW:2 * BW] = dv_ref[...].astype(BF16)
        dg_ref[...] = jnp.sum(gain, axis=0, keepdims=True)

    return pl.pallas_call(body, out_shape=(_sds((N_MEM, 2 * BW), BF16), _sds((1, BW), F32)), name="memkv_bwd",
                          compiler_params=_cp())(kv, dk, dv, g_mk)


MERGE_TM = 512


def _merge_fwd(brs, wbt, gp):
    t = gp.shape[0]
    tm = MERGE_TM

    def body(b0, b1, b2, b3, wb_ref, gp_ref, o_ref):
        out = jnp.zeros((tm, D), F32)
        for n, b_ref in enumerate((b0, b1, b2, b3)):
            up = _dot(b_ref[...], wb_ref[n], tb=True)
            out = out + _sigmoid(gp_ref[:, n * D:(n + 1) * D].astype(F32)) * up
        o_ref[...] = out.astype(BF16)

    blk = pl.BlockSpec((tm, BW), lambda i: (i, 0))
    return pl.pallas_call(
        body, grid=(t // tm,),
        in_specs=[blk, blk, blk, blk, pl.BlockSpec((NH, D, BW), lambda i: (0, 0, 0)), pl.BlockSpec((tm, NH * D), lambda i: (i, 0))],
        out_specs=pl.BlockSpec((tm, D), lambda i: (i, 0)), out_shape=_sds((t, D), BF16), name="merge_fwd",
        compiler_params=_cp())(*brs, wbt, gp)


def _merge_bwd(dmerged, brs, wbt, gp):
    t = gp.shape[0]
    tm = MERGE_TM
    steps = t // tm

    def body(d_ref, b0, b1, b2, b3, wb_ref, gp_ref, dgp_ref, dbr_ref, dwb_ref, acc_ref):
        i = pl.program_id(0)
        dm = d_ref[...]
        for n, b_ref in enumerate((b0, b1, b2, b3)):
            br = b_ref[...]
            up = _dot(br, wb_ref[n], tb=True)
            g = _sigmoid(gp_ref[:, n * D:(n + 1) * D].astype(F32))
            dgp_ref[:, n * D:(n + 1) * D] = (dm * up * (g * (1.0 - g))).astype(BF16)
            dup = (dm * g).astype(BF16)
            dbr_ref[:, n * BW:(n + 1) * BW] = _dot(dup, wb_ref[n])
            part = _dot(dup, br, ta=True)

            @pl.when(i == 0)
            def _():
                acc_ref[n] = part

            @pl.when(i > 0)
            def _():
                acc_ref[n] += part

        @pl.when(i == steps - 1)
        def _():
            dwb_ref[...] = acc_ref[...].astype(BF16)

    row = pl.BlockSpec((tm, D), lambda i: (i, 0))
    blk = pl.BlockSpec((tm, BW), lambda i: (i, 0))
    wide = pl.BlockSpec((tm, NH * D), lambda i: (i, 0))
    whole = pl.BlockSpec((NH, D, BW), lambda i: (0, 0, 0))
    return pl.pallas_call(
        body, grid=(steps,), in_specs=[row, blk, blk, blk, blk, whole, wide], out_specs=(wide, row, whole),
        out_shape=(_sds((t, NH * D), BF16), _sds((t, NH * BW), F32), _sds((NH, D, BW), BF16)),
        scratch_shapes=[pltpu.VMEM((NH, D, BW), F32)], name="merge_bwd", compiler_params=_cp())(dmerged, *brs, wbt, gp)


FFN_TN = 256


def _ffn_in_fwd(h2, w_t):
    t = h2.shape[0]
    tm, tn = _tile(t, 2048), FFN_TN
    nj = FF // tn

    def body(x_ref, wa_ref, wg_ref, a_ref, g_ref, y_ref):
        x = x_ref[...]
        a, g = _dot(x, wa_ref[...], tb=True), _dot(x, wg_ref[...], tb=True)
        a_ref[...] = a.astype(BF16)
        g_ref[...] = g.astype(BF16)
        y_ref[...] = (a * _sigmoid(a) * g).astype(BF16)

    out = pl.BlockSpec((tm, tn), lambda i, j: (i, j))
    return pl.pallas_call(
        body, grid=(t // tm, nj),
        in_specs=[pl.BlockSpec((tm, D), lambda i, j: (i, 0)), pl.BlockSpec((tn, D), lambda i, j: (j, 0)),
                  pl.BlockSpec((tn, D), lambda i, j: (j + nj, 0))],
        out_specs=(out, out, out), out_shape=tuple(_sds((t, FF), BF16) for _ in range(3)), name="ffn_in_fwd",
        compiler_params=_cp(dimension_semantics=("parallel", "parallel")))(h2, w_t, w_t)


def _ffn_out_bwd(dx2b, w_out, a, g, dep):
    t = dx2b.shape[0]
    tm, tn = _tile(t, 2048), FFN_TN

    def body(*refs):
        x_ref, w_ref, a_ref, g_ref = refs[:4]
        da_ref, dg_ref = refs[-2:]
        d = _dot(x_ref[...], w_ref[...], tb=True)
        av, gv = a_ref[...].astype(F32), g_ref[...].astype(F32)
        s = _sigmoid(av)
        da_ref[...] = (d * gv * (s * (1.0 + av * (1.0 - s)))).astype(BF16)
        dg_ref[...] = (d * (av * s)).astype(BF16)

    blk = pl.BlockSpec((tm, tn), lambda i, j: (i, j))
    ins = [pl.BlockSpec((tm, D), lambda i, j: (i, 0)), pl.BlockSpec((tn, D), lambda i, j: (j, 0)), blk, blk]
    args = [dx2b, w_out, a, g]
    if dep is not None:
        ins.append(pl.BlockSpec((8, 128), lambda i, j: (0, 0)))
        args.append(dep)
    return pl.pallas_call(
        body, grid=(t // tm, FF // tn), in_specs=ins, out_specs=(blk, blk),
        out_shape=(_sds((t, FF), BF16), _sds((t, FF), BF16)), name="ffn_out_bwd",
        compiler_params=_cp(dimension_semantics=("parallel", "parallel")))(*args)


def _loss_head(y, target):
    t, d = y.shape
    tm = 512

    def body(y_ref, t_ref, dy_ref, dyb_ref, l_ref):
        e = y_ref[...] - t_ref[...]
        dy_ref[...] = e * (1.0 / d)
        dyb_ref[...] = (e * (1.0 / d)).astype(BF16)
        _acc(l_ref, jnp.full((8, 128), 0.5 * jnp.sum(jnp.sum(e * e, axis=-1, keepdims=True) * (1.0 / d)), F32), pl.program_id(0) == 0)

    row = pl.BlockSpec((tm, d), lambda i: (i, 0))
    return pl.pallas_call(body, grid=(t // tm,), in_specs=[row, row], out_specs=(row, row, pl.BlockSpec((8, 128), lambda i: (0, 0))),
                          out_shape=(_sds((t, d), F32), _sds((t, d), BF16), _sds((8, 128), F32)), name="loss_head",
                          compiler_params=_cp())(y, target)


def _sum_slots(x, name):
    k, r, c = x.shape
    tr = _tile(r, 512) if r % 128 == 0 else r

    def body(x_ref, o_ref):
        acc = x_ref[0].astype(F32)
        for s in range(1, k):
            acc = acc + x_ref[s].astype(F32)
        o_ref[...] = acc

    return pl.pallas_call(body, grid=(r // tr,), in_specs=[pl.BlockSpec((k, tr, c), lambda i: (0, i, 0))],
                          out_specs=pl.BlockSpec((tr, c), lambda i: (i, 0)), out_shape=_sds((r, c), F32), name=name,
                          compiler_params=_cp())(x)


def _pair_sum(bufs, recvs, cidx):
    n = len(bufs)

    def body(c_ref, *refs):
        for i in range(n):
            refs[2 * n + i][...] = (refs[i][...].astype(F32) + refs[n + i][...].astype(F32)).astype(BF16)

    return pl.pallas_call(
        body,
        grid_spec=pltpu.PrefetchScalarGridSpec(
            num_scalar_prefetch=1, grid=(4,),
            in_specs=[pl.BlockSpec((None, None) + b.shape[2:], lambda s, cref: (s, cref[0], 0, 0)) for b in bufs]
            + [pl.BlockSpec((None,) + r.shape[1:], lambda s, cref: (s, 0, 0)) for r in recvs],
            out_specs=tuple(pl.BlockSpec((None,) + r.shape[1:], lambda s, cref: (s, 0, 0)) for r in recvs)),
        out_shape=tuple(_sds(r.shape, BF16) for r in recvs), name="rs_pair_sum", compiler_params=_cp())(cidx, *bufs, *recvs)


def _adamw_update(w, gv, m, v):
    mn = ADAM_B1 * m + (1.0 - ADAM_B1) * gv
    vn = ADAM_B2 * v + (1.0 - ADAM_B2) * (gv * gv)
    m_hat = mn / (1.0 - ADAM_B1 ** ADAM_STEP)
    v_hat = vn / (1.0 - ADAM_B2 ** ADAM_STEP)
    return -ADAM_LR * (m_hat / (jnp.sqrt(v_hat) + ADAM_EPS) + ADAM_WD * w), mn, vn


def _adamw(w, g, m, v, name):
    r, c = w.shape

    def body(w_ref, g_ref, m_ref, v_ref, d_ref, nm_ref, nv_ref):
        d_ref[...], nm_ref[...], nv_ref[...] = _adamw_update(w_ref[...], g_ref[...], m_ref[...], v_ref[...])

    blk = pl.BlockSpec((r, c), lambda i: (0, 0))
    return pl.pallas_call(body, grid=(1,), in_specs=[blk] * 4, out_specs=(blk,) * 3,
                          out_shape=tuple(_sds((r, c), F32) for _ in range(3)), name=name, compiler_params=_cp())(w, g, m, v)


def _adamw_layer(layer, w, g, m, v, outs, name):
    _, r, c = w.shape
    tr = max(d for d in range(8, r + 1, 8) if r % d == 0 and d * c * 4 <= 2 ** 20)

    def body(w_ref, m_ref, v_ref, g_ref, *refs):
        d_ref, nm_ref, nv_ref, go_ref = refs[4:]
        gv = g_ref[...]
        d_ref[...], nm_ref[...], nv_ref[...] = _adamw_update(w_ref[...], gv, m_ref[...], v_ref[...])
        go_ref[...] = gv

    blk = pl.BlockSpec((None, tr, c), lambda i: (layer, i, 0))
    return pl.pallas_call(
        body, grid=(r // tr,), in_specs=[blk] * 3 + [pl.BlockSpec((tr, c), lambda i: (i, 0))] + [ANY] * 4, out_specs=(blk,) * 4,
        out_shape=tuple(_sds(w.shape, F32) for _ in range(4)), input_output_aliases={4 + j: j for j in range(4)}, name=name,
        compiler_params=_cp())(w, m, v, g, *outs)


def _all_gather(shards, name):
    n = len(shards)

    def body(*refs):
        x_refs, out_refs = refs[:n], refs[n:2 * n]
        send_sems, recv_sems, local_sems = refs[2 * n:]
        x, y, cc = lax.axis_index("x"), lax.axis_index("y"), lax.axis_index("c")
        me, sibling = (x, y, cc), (x, y, 1 - cc)
        chips = [(1 - x, y), (x, 1 - y), (1 - x, 1 - y)]

        def copy(i, k, block, to, own=False):
            px, py, pc = block
            slot = out_refs[i].at[4 * px + 2 * py + pc]
            return pltpu.make_async_remote_copy(
                src_ref=x_refs[i] if own else slot, dst_ref=slot, send_sem=send_sems.at[7 * i + k],
                recv_sem=recv_sems.at[7 * i + k], device_id=to, device_id_type=MESH)

        mine = [pltpu.make_async_copy(x_refs[i], out_refs[i].at[4 * x + 2 * y + cc], local_sems.at[i]) for i in range(n)]
        for cp in mine:
            cp.start()
        first = []
        for j, chip in enumerate(chips):
            first += [copy(i, 1 + j, me, (*chip, cc), own=True) for i in range(n)]
        first += [copy(i, 0, me, sibling, own=True) for i in range(n)]
        for cp in first:
            cp.start()
        passed = []
        for j, chip in enumerate(chips):
            for i in range(n):
                copy(i, 1 + j, (*chip, cc), me).wait_recv()
                cp = copy(i, 4 + j, (*chip, cc), sibling)
                cp.start()
                passed.append(cp)
        for i in range(n):
            copy(i, 0, sibling, me).wait_recv()
        for j, chip in enumerate(chips):
            for i in range(n):
                copy(i, 4 + j, (*chip, 1 - cc), me).wait_recv()
        for cp in first + passed:
            cp.wait_send()
        for cp in mine:
            cp.wait()

    return pl.pallas_call(
        body, out_shape=tuple(_sds((N_DEV,) + s.shape, s.dtype) for s in shards), in_specs=[ANY] * n, out_specs=(ANY,) * n,
        scratch_shapes=[pltpu.SemaphoreType.DMA((7 * n,)), pltpu.SemaphoreType.DMA((7 * n,)), pltpu.SemaphoreType.DMA((n,))],
        name=name)(*shards)


def _rs_core_swap(bufs, name):
    n = len(bufs)

    def body(*refs):
        b_refs, recv_refs = refs[:n], refs[n:2 * n]
        send_sems, recv_sems = refs[2 * n:]
        x, y, cc = lax.axis_index("x"), lax.axis_index("y"), lax.axis_index("c")
        copies = [pltpu.make_async_remote_copy(
            src_ref=b_refs[i].at[s, 1 - cc], dst_ref=recv_refs[i].at[s], send_sem=send_sems.at[4 * i + s],
            recv_sem=recv_sems.at[4 * i + s], device_id=(x, y, 1 - cc), device_id_type=MESH) for i in range(n) for s in range(4)]
        for cp in copies:
            cp.start()
        for cp in copies:
            cp.wait()

    return pl.pallas_call(
        body, out_shape=tuple(_sds((4,) + b.shape[2:], b.dtype) for b in bufs), in_specs=[ANY] * n, out_specs=(ANY,) * n,
        scratch_shapes=[pltpu.SemaphoreType.DMA((4 * n,)), pltpu.SemaphoreType.DMA((4 * n,))], name=name)(*bufs)


HBM = pl.BlockSpec(memory_space=pltpu.HBM)
SEMS = pl.BlockSpec(memory_space=pltpu.SEMAPHORE)
EFFECT = pltpu.SideEffectType.DATAFLOW_SIDE_EFFECTING


def _hbm(a):
    return pltpu.HBM(a.shape, a.dtype)


def _other_chips(x, y):
    return [(1 - x, y), (x, 1 - y), (1 - x, 1 - y)]


def _ici_start(srcs, lands, mode, name, group=None):
    n = len(srcs)

    def body(*refs):
        s_refs, land_refs = refs[:n], refs[n:2 * n]
        send_sems, recv_sems = refs[2 * n], refs[2 * n + 1]
        token = refs[-1]
        x, y, cc = lax.axis_index("x"), lax.axis_index("y"), lax.axis_index("c")
        mine = 2 * x + y if mode == "by_chip" else 4 * x + 2 * y + cc
        peers = [(px, py, cc) for px, py in _other_chips(x, y)]
        if mode == "by_device":
            peers = [(x, y, 1 - cc)] + peers + [(px, py, 1 - cc) for px, py in _other_chips(x, y)]
        first = 0
        for size in ([n] if group is None else group):
            first += size
            for px, py, pc in peers:
                for i in range(first - size, first):
                    src = s_refs[i]
                    if mode == "by_chip":
                        src = src.at[2 * px + py]
                    elif mode == "by_device":
                        src = src.at[4 * px + 2 * py + pc]
                    pltpu.make_async_remote_copy(
                        src_ref=src, dst_ref=land_refs[i].at[mine], send_sem=send_sems.at[i], recv_sem=recv_sems.at[i],
                        device_id=(px, py, pc), device_id_type=MESH).start()
        token[...] = jnp.zeros_like(token)

    out = pl.pallas_call(
        body, name=name,
        out_shape=(pltpu.SemaphoreType.DMA((n,)), pltpu.SemaphoreType.DMA((n,)), *[_hbm(s) for s in srcs], *[_hbm(l) for l in lands],
                   _sds((8, 128), F32)),
        in_specs=[HBM] * (2 * n), out_specs=(SEMS, SEMS, *[HBM] * (2 * n), pl.BlockSpec(memory_space=pltpu.VMEM)),
        input_output_aliases={i: 2 + i for i in range(2 * n)}, compiler_params=pltpu.CompilerParams(has_side_effects=EFFECT),
    )(*[pltpu.with_memory_space_constraint(s, pltpu.HBM) for s in srcs],
      *[pltpu.with_memory_space_constraint(l, pltpu.HBM) for l in lands])
    return out[0], out[1], out[2:2 + n], out[2 + n:2 + 2 * n], out[-1], 7 if mode == "by_device" else 3


def _ici_wait(started, after, name, only=None):
    send_sems, recv_sems, srcs, lands, _, copies = started
    only = list(range(len(srcs))) if only is None else only
    srcs, lands = [srcs[i] for i in only], [lands[i] for i in only]
    n = len(srcs)

    def body(*refs):
        land_refs = refs[n:2 * n]
        send_sems, recv_sems = refs[2 * n], refs[2 * n + 1]
        x, y, cc = lax.axis_index("x"), lax.axis_index("y"), lax.axis_index("c")
        for i in range(n):
            three = land_refs[i].at[pl.ds(0, copies)]
            cp = pltpu.make_async_remote_copy(src_ref=three, dst_ref=three, send_sem=send_sems.at[only[i]],
                                              recv_sem=recv_sems.at[only[i]],
                                              device_id=(x, y, cc), device_id_type=MESH)
            cp.wait_send()
            cp.wait_recv()

    return pl.pallas_call(
        body, name=name, out_shape=tuple(_hbm(l) for l in lands), in_specs=[HBM] * (2 * n) + [SEMS, SEMS, ANY],
        out_specs=tuple([HBM] * n), input_output_aliases={n + i: i for i in range(n)},
        compiler_params=pltpu.CompilerParams(has_side_effects=EFFECT))(*srcs, *lands, send_sems, recv_sems, after)


def _gather_d2d(blocks, lands, name):
    n = len(blocks)

    def body(*refs):
        x_refs, land_refs = refs[:n], refs[2 * n:3 * n]
        send_sems, recv_sems, in_sems, out_sems = refs[3 * n:3 * n + 4]
        stage = refs[3 * n + 4:]
        x, y, cc = lax.axis_index("x"), lax.axis_index("y"), lax.axis_index("c")
        sibling = (x, y, 1 - cc)
        staged = [pltpu.make_async_copy(x_refs[i], stage[i], in_sems.at[i]) for i in range(n)]
        for cp in staged:
            cp.start()
        copies = []
        for i in range(n):
            slot = land_refs[i].at[4 * x + 2 * y + cc]
            copies.append(pltpu.make_async_remote_copy(src_ref=x_refs[i], dst_ref=slot, send_sem=send_sems.at[4 * i],
                                                       recv_sem=recv_sems.at[4 * i], device_id=sibling, device_id_type=MESH))
            for j, (px, py) in enumerate(_other_chips(x, y)):
                slot = land_refs[i].at[4 * px + 2 * py + cc]
                copies.append(pltpu.make_async_remote_copy(src_ref=slot, dst_ref=slot, send_sem=send_sems.at[4 * i + 1 + j],
                                                           recv_sem=recv_sems.at[4 * i + 1 + j], device_id=sibling, device_id_type=MESH))
        for cp in copies:
            cp.start()
        mine = []
        for i in range(n):
            staged[i].wait()
            mine.append(pltpu.make_async_copy(stage[i], land_refs[i].at[4 * x + 2 * y + cc], out_sems.at[i]))
            mine[i].start()
        for i in range(n):
            slot = land_refs[i].at[4 * x + 2 * y + (1 - cc)]
            pltpu.make_async_remote_copy(src_ref=slot, dst_ref=slot, send_sem=send_sems.at[4 * i], recv_sem=recv_sems.at[4 * i],
                                         device_id=sibling, device_id_type=MESH).wait_recv()
            for j, (px, py) in enumerate(_other_chips(x, y)):
                slot = land_refs[i].at[4 * px + 2 * py + (1 - cc)]
                pltpu.make_async_remote_copy(src_ref=slot, dst_ref=slot, send_sem=send_sems.at[4 * i + 1 + j],
                                             recv_sem=recv_sems.at[4 * i + 1 + j], device_id=sibling, device_id_type=MESH).wait_recv()
        for cp in copies:
            cp.wait_send()
        for cp in mine:
            cp.wait()

    return pl.pallas_call(
        body, out_shape=tuple(_sds(l.shape, l.dtype) for l in lands), in_specs=[ANY] * (2 * n), out_specs=(ANY,) * n,
        input_output_aliases={n + i: i for i in range(n)},
        scratch_shapes=[pltpu.SemaphoreType.DMA((4 * n,)), pltpu.SemaphoreType.DMA((4 * n,)), pltpu.SemaphoreType.DMA((n,)),
                        pltpu.SemaphoreType.DMA((n,))] + [pltpu.VMEM(b.shape, b.dtype) for b in blocks],
        name=name, compiler_params=_cp())(*blocks, *lands)


def _sum_own(parts, recvs, mine, name):
    n = len(parts)

    def body(c_ref, *refs):
        s = pl.program_id(0)
        for i in range(n):
            val = jnp.where(c_ref[0] == s, refs[i][...], refs[n + i][...]).astype(F32)
            _acc(refs[2 * n + i], val, s == 0)

    kept = [pl.BlockSpec((None,) + p.shape[1:], lambda s, cref: (cref[0], 0, 0)) for p in parts]
    ins = [pl.BlockSpec((None,) + p.shape[1:], lambda s, cref: (s, 0, 0)) for p in parts]
    return pl.pallas_call(
        body, grid_spec=pltpu.PrefetchScalarGridSpec(
            num_scalar_prefetch=1, grid=(parts[0].shape[0],), in_specs=kept + ins,
            out_specs=tuple(pl.BlockSpec(p.shape[1:], lambda s, cref: (0, 0)) for p in parts)),
        out_shape=tuple(_sds(p.shape[1:], F32) for p in parts), name=name, compiler_params=_cp())(mine, *parts, *recvs)


BIG = (("w_in", True), ("w_gate", True), ("w_mem_kv", False), ("w_branch", True), ("w_out", False), ("w_ffn_in", True),
       ("w_ffn_out", False))

SMALL = ("norm_mix_g", "norm_mem_g", "ret_decay_fwd", "ret_decay_bwd", "ret_norm_g", "pool_w", "pool_scale", "na_q_norm_g",
         "na_k_norm_g", "na_rpb", "mem_q_norm_g", "mem_k_norm_g", "norm_ffn_g")
WEIGHTS = ("norm_mix_g", "norm_mem_g", "w_in", "w_gate", "ret_decay_fwd", "ret_decay_bwd", "ret_norm_g", "pool_w", "pool_scale",
           "na_q_norm_g", "na_k_norm_g", "na_rpb", "mem_q_norm_g", "mem_k_norm_g", "w_mem_kv", "w_branch", "w_out", "norm_ffn_g",
           "w_ffn_in", "w_ffn_out")


def _to_exchange(name, transposed, shard):
    if name == "w_branch":
        return jnp.swapaxes(shard, 1, 2).reshape(NH * (D // N_DEV), BW)
    return shard.T if transposed else shard


def _from_exchange(name, transposed, block):
    if name == "w_branch":
        return jnp.swapaxes(block.reshape(NH, D // N_DEV, BW), 1, 2)
    return block.T if transposed else block


def _whole_from_gathered(name, g):
    if name == "w_branch":
        return jnp.swapaxes(g.reshape(N_DEV, NH, D // N_DEV, BW), 0, 1).reshape(NH, D, BW)
    return g.reshape(N_DEV * g.shape[1], g.shape[2])


def _by_destination(name, g):
    if name == "w_branch":
        g = jnp.swapaxes(g.reshape(NH, N_DEV, D // N_DEV, BW), 0, 1).reshape(N_DEV * NH * (D // N_DEV), BW)
    return g.reshape(4, 2, g.shape[0] // N_DEV, g.shape[1])


SMALL_PAD = 1024


def _pack_small(vals, loss=None):
    parts = [vals[n] for n in SMALL] + [jnp.zeros((1,), F32) if loss is None else loss.reshape(1)]
    rows = []
    for p in parts:
        flat = p.reshape(-1)
        rows.append(jnp.pad(flat, (0, -flat.shape[0] % SMALL_PAD)).reshape(-1, 128))
    return jnp.concatenate(rows, axis=0)


def _unpack_small(packed, like):
    out, off = {}, 0
    for n in SMALL:
        sz = int(np.prod(like[n].shape))
        nrow = -(-sz // SMALL_PAD) * (SMALL_PAD // 128)
        out[n] = packed[off:off + nrow].reshape(-1)[:sz].reshape(like[n].shape)
        off += nrow
    return out, packed[off, 0]


def _na_constants():
    c = np.arange(GRID_W)
    win = np.clip(c - NA_COLS_WIN // 2, 0, GRID_W - NA_COLS_WIN)
    kc = np.arange(GRID_W)
    inside = (kc[None, :] >= win[:, None]) & (kc[None, :] < win[:, None] + NA_COLS_WIN)
    off = kc[None, :] - c[:, None] + NA_COLS_WIN - 1
    onehot = np.zeros((128, GRID_W, GRID_W), np.float32)
    for b in range(2 * NA_COLS_WIN - 1):
        onehot[b] = (off == b) & inside
    maskadd = np.where(inside, 0.0, NEG).astype(np.float32)
    return onehot.reshape(128, GRID_W * GRID_W), maskadd


def _na_bias_table(tab, maskadd):
    n_off = 2 * NA_ROWS_WIN - 1
    t4 = tab[:NH * n_off].reshape(NH, n_off, GRID_W, GRID_W) + maskadd[None, None]
    by_off = t4.transpose(1, 0, 2, 3).reshape(n_off, NH * GRID_W, GRID_W)
    return jnp.concatenate([by_off[:-1], by_off[1:]], axis=-1)


def _rotary_tables(t):
    half = HD // 2
    inv = ROPE_THETA ** (-jnp.arange(half, dtype=F32) / half)
    ang = jnp.arange(t, dtype=F32)[:, None] * inv[None, :]
    cos, sin = jnp.cos(ang), jnp.sin(ang)
    return jnp.tile(jnp.concatenate([cos, cos], axis=-1), (1, NH)), jnp.tile(jnp.concatenate([-sin, sin], axis=-1), (1, NH))


def _block_diag(pw):
    out = jnp.zeros((BW, BW), pw.dtype)
    for g in range(NH):
        out = lax.dynamic_update_slice(out, pw[g], (g * HD, g * HD))
    return out


def _tile4(g):
    return jnp.tile(g.reshape(1, HD), (1, NH))


def _layer_fwd(x, mem, sw, lw, consts, fetch, h=None, next_norm_g=None):
    cos2, sin2, onehot, maskadd = consts
    if h is None:
        h = _rmsnorm_fwd(x, sw["norm_mix_g"].reshape(1, D), "norm_mix_fwd")
    proj = _mm(h, lw["w_in"], tb=True, name="mm_in")
    gp = _mm(h, lw["w_gate"], tb=True, out_dtype=BF16, name="mm_gate")
    g_naq, g_nak, g_mq = _tile4(sw["na_q_norm_g"]), _tile4(sw["na_k_norm_g"]), _tile4(sw["mem_q_norm_g"])
    rq, rk, rv, nq, nk, nv, mq = _prep_fwd(proj, cos2, sin2, g_naq, g_nak, g_mq)

    lgf, lgb = jax.nn.log_sigmoid(sw["ret_decay_fwd"]), jax.nn.log_sigmoid(sw["ret_decay_bwd"])
    g_ret = sw["ret_norm_g"].reshape(1, BW)
    o_ret, ret = _ret_fwd(rq, rk, rv, proj, lgf, lgb, g_ret)

    wbd = _block_diag(sw["pool_w"]).astype(BF16)
    p_scale = sw["pool_scale"].reshape(1, BW)
    pool = _pool_fwd(proj, wbd, p_scale)

    rpb_pad = jnp.pad(sw["na_rpb"].reshape(NH * 15, 31), ((0, 4), (0, 97)))
    ball = _na_bias_table(_rpb_expand(rpb_pad, onehot), maskadd)
    na = _na_fwd(nq, nk, nv, ball)

    lw.update(fetch(1, na))
    memn = _rmsnorm_fwd(mem, sw["norm_mem_g"].reshape(1, D), "norm_mem_fwd")
    kv = _mm(memn, lw["w_mem_kv"], name="mm_memkv")
    g_mk = _tile4(sw["mem_k_norm_g"])
    mk, mv = _memkv_prep(kv, g_mk)
    mo = _mem_fwd(mq, mk, mv)

    br = (ret, pool, na, mo)
    merged = _merge_fwd(br, lw["w_branch"], gp)
    x1, h2 = _mm(merged, lw["w_out"], add=x, norm_g=sw["norm_ffn_g"].reshape(1, D), name="mm_out")
    lw.update(fetch(2, x1))
    ffa, ffg, yff = _ffn_in_fwd(h2, lw["w_ffn_in"])
    if next_norm_g is None:
        x2, h_next = _mm(yff, lw["w_ffn_out"], add=x1, name="mm_ffn_out"), None
    else:
        x2, h_next = _mm(yff, lw["w_ffn_out"], add=x1, norm_g=next_norm_g.reshape(1, D), name="mm_ffn_out")
    saved = dict(x=x, h=h, proj=proj, gp=gp, rq=rq, rk=rk, rv=rv, nq=nq, nk=nk, nv=nv, mq=mq, o_ret=o_ret, ball=ball, memn=memn,
                 kv=kv, mk=mk, mv=mv, br=br, merged=merged, x1=x1, h2=h2, ffa=ffa, ffg=ffg, yff=yff, lgf=lgf, lgb=lgb, wbd=wbd)
    return x2, h_next, saved


def _layer_bwd(dx2, dx2b, mem, sw, lw, sv, consts, dep=None):
    cos2, sin2, onehot, maskadd = consts
    gb, gs = {}, {}
    d_a, d_g = _ffn_out_bwd(dx2b, lw["w_ffn_out"], sv["ffa"], sv["ffg"], dep)
    gb["w_ffn_out"] = _mm(sv["yff"], dx2b, ta=True, out_dtype=BF16, name="mm_ffn_out_dw")
    dh2 = _mm(d_a, lw["w_ffn_in"], b_half=0, name="mm_ffn_in_dx_a")
    dx1, dx1b, dg = _mm_norm_bwd(d_g, lw["w_ffn_in"], dh2, sv["x1"], sw["norm_ffn_g"].reshape(1, D), dx2, b_half=1,
                                 name="mm_ffn_in_dx_g")
    gs["norm_ffn_g"] = dg.reshape(D)
    dw_a = _mm(d_a, sv["h2"], ta=True, out_dtype=BF16, out_half=(0, None), name="mm_ffn_in_dw_a")
    gb["w_ffn_in"] = _mm(d_g, sv["h2"], ta=True, out_dtype=BF16, out_half=(1, dw_a), name="mm_ffn_in_dw_g")

    dmerged = _mm(dx1b, lw["w_out"], tb=True, name="mm_out_dx")
    gb["w_out"] = _mm(sv["merged"], dx1b, ta=True, out_dtype=BF16, name="mm_out_dw")
    dgp, dbr, gb["w_branch"] = _merge_bwd(dmerged, sv["br"], lw["w_branch"], sv["gp"])

    g_ret = sw["ret_norm_g"].reshape(1, BW)
    do_ret, d_rg, dg_ret = _ret_post_bwd(dbr, sv["o_ret"], sv["proj"], g_ret)
    d_rq, d_rk, d_rv, dlg = _ret_bwd(do_ret, sv["rq"], sv["rk"], sv["rv"], sv["lgf"], sv["lgb"])
    gs["ret_norm_g"] = dg_ret.reshape(BW)
    _, vjp_f = jax.vjp(jax.nn.log_sigmoid, sw["ret_decay_fwd"])
    _, vjp_b = jax.vjp(jax.nn.log_sigmoid, sw["ret_decay_bwd"])
    gs["ret_decay_fwd"] = vjp_f(dlg[0:NH, 0])[0]
    gs["ret_decay_bwd"] = vjp_b(dlg[NH:2 * NH, 0])[0]

    p_scale = sw["pool_scale"].reshape(1, BW)
    d_pv, dwbd, dscale = _pool_bwd(dbr, sv["proj"], sv["wbd"], p_scale)
    gs["pool_w"] = jnp.stack([dwbd[g * HD:(g + 1) * HD, g * HD:(g + 1) * HD] for g in range(NH)])
    gs["pool_scale"] = dscale.reshape(BW)

    d_nq, d_nk, d_nv, dball = _na_bwd(dbr, sv["nq"], sv["nk"], sv["nv"], sv["ball"])
    _, vjp_tab = jax.vjp(lambda tab: _na_bias_table(tab, maskadd), jnp.zeros((64, GRID_W * GRID_W), F32))
    drpb = _rpb_reduce(vjp_tab(dball)[0], onehot)
    gs["na_rpb"] = drpb[:NH * 15, :31].reshape(NH, 15, 31)

    d_mq, d_mk, d_mv = _mem_bwd(dbr, sv["mq"], sv["mk"], sv["mv"])
    g_mk = _tile4(sw["mem_k_norm_g"])
    dkv, dg_mk = _memkv_bwd(sv["kv"], d_mk, d_mv, g_mk)
    gs["mem_k_norm_g"] = dg_mk.reshape(NH, HD).sum(0)
    gb["w_mem_kv"] = _mm(sv["memn"], dkv, ta=True, out_dtype=BF16, name="mm_memkv_dw")
    dmemn = _mm(dkv, lw["w_mem_kv"], tb=True, name="mm_memkv_dx")
    _, _, dg_mem = _rmsnorm_bwd(dmemn, mem, sw["norm_mem_g"].reshape(1, D), jnp.zeros_like(mem), "norm_mem_bwd")
    gs["norm_mem_g"] = dg_mem.reshape(D)

    g_naq, g_nak, g_mq = _tile4(sw["na_q_norm_g"]), _tile4(sw["na_k_norm_g"]), _tile4(sw["mem_q_norm_g"])
    dproj, dg_naq, dg_nak, dg_mq = _prep_bwd(sv["proj"], cos2, sin2, g_naq, g_nak, g_mq, d_rq, d_rk, d_rv, d_rg, d_pv, d_nq, d_nk,
                                             d_nv, d_mq)
    gs["na_q_norm_g"] = dg_naq.reshape(NH, HD).sum(0)
    gs["na_k_norm_g"] = dg_nak.reshape(NH, HD).sum(0)
    gs["mem_q_norm_g"] = dg_mq.reshape(NH, HD).sum(0)

    gb["w_in"] = _mm(dproj, sv["h"], ta=True, out_dtype=BF16, name="mm_in_dw")
    gb["w_gate"] = _mm(dgp, sv["h"], ta=True, out_dtype=BF16, name="mm_gate_dw")
    dh = _mm(dproj, lw["w_in"], name="mm_in_dx")
    dx, dxb, dg = _mm_norm_bwd(dgp, lw["w_gate"], dh, sv["x"], sw["norm_mix_g"].reshape(1, D), dx1, name="mm_gate_dx")
    gs["norm_mix_g"] = dg.reshape(D)
    return dx, dxb, gb, gs


def _local_step(x, mem, target, small, get_layer, on_grads):
    t = x.shape[0]
    cos2, sin2 = _rotary_tables(t)
    onehot, maskadd = _na_constants()
    consts = (cos2, sin2, jnp.asarray(onehot), jnp.asarray(maskadd))
    saved, weights, cur, h = [], [], x, None
    for l in range(DEPTH):
        sw = {n: small[n][l] for n in SMALL}
        lw, fetch = get_layer(l, cur)
        weights.append(lw)
        cur, h, sv = _layer_fwd(cur, mem, sw, lw, consts, fetch, h, small["norm_mix_g"][l + 1] if l + 1 < DEPTH else None)
        saved.append(sv)
    dy, dyb, loss_tile = _loss_head(cur, target)
    small_g = {n: [None] * DEPTH for n in SMALL}
    dep = None
    for l in reversed(range(DEPTH)):
        sw = {n: small[n][l] for n in SMALL}
        dy, dyb, gb, gs = _layer_bwd(dy, dyb, mem, sw, weights[l], saved[l], consts, dep)
        dep = on_grads(l, gb, dy)
        for n in SMALL:
            small_g[n][l] = gs[n]
    return loss_tile[0, 0], dy, {n: jnp.stack(v) for n, v in small_g.items()}


def _flat2d(a):
    return a.reshape(-1, a.shape[-1])


def kernel(x, mem, norm_mix_g, norm_mem_g, w_in, w_gate, ret_decay_fwd, ret_decay_bwd, ret_norm_g, pool_w, pool_scale, na_q_norm_g, na_k_norm_g, na_rpb, mem_q_norm_g, mem_k_norm_g, w_mem_kv, w_branch, w_out, norm_ffn_g, w_ffn_in, w_ffn_out, loss_target, m_norm_mix_g, m_norm_mem_g, m_w_in, m_w_gate, m_ret_decay_fwd, m_ret_decay_bwd, m_ret_norm_g, m_pool_w, m_pool_scale, m_na_q_norm_g, m_na_k_norm_g, m_na_rpb, m_mem_q_norm_g, m_mem_k_norm_g, m_w_mem_kv, m_w_branch, m_w_out, m_norm_ffn_g, m_w_ffn_in, m_w_ffn_out, v_norm_mix_g, v_norm_mem_g, v_w_in, v_w_gate, v_ret_decay_fwd, v_ret_decay_bwd, v_ret_norm_g, v_pool_w, v_pool_scale, v_na_q_norm_g, v_na_k_norm_g, v_na_rpb, v_mem_q_norm_g, v_mem_k_norm_g, v_w_mem_kv, v_w_branch, v_w_out, v_norm_ffn_g, v_w_ffn_in, v_w_ffn_out):
    w = dict(norm_mix_g=norm_mix_g, norm_mem_g=norm_mem_g, w_in=w_in, w_gate=w_gate, ret_decay_fwd=ret_decay_fwd,
             ret_decay_bwd=ret_decay_bwd, ret_norm_g=ret_norm_g, pool_w=pool_w, pool_scale=pool_scale, na_q_norm_g=na_q_norm_g,
             na_k_norm_g=na_k_norm_g, na_rpb=na_rpb, mem_q_norm_g=mem_q_norm_g, mem_k_norm_g=mem_k_norm_g, w_mem_kv=w_mem_kv,
             w_branch=w_branch, w_out=w_out, norm_ffn_g=norm_ffn_g, w_ffn_in=w_ffn_in, w_ffn_out=w_ffn_out)
    m = dict(norm_mix_g=m_norm_mix_g, norm_mem_g=m_norm_mem_g, w_in=m_w_in, w_gate=m_w_gate, ret_decay_fwd=m_ret_decay_fwd,
             ret_decay_bwd=m_ret_decay_bwd, ret_norm_g=m_ret_norm_g, pool_w=m_pool_w, pool_scale=m_pool_scale, na_q_norm_g=m_na_q_norm_g,
             na_k_norm_g=m_na_k_norm_g, na_rpb=m_na_rpb, mem_q_norm_g=m_mem_q_norm_g, mem_k_norm_g=m_mem_k_norm_g, w_mem_kv=m_w_mem_kv,
             w_branch=m_w_branch, w_out=m_w_out, norm_ffn_g=m_norm_ffn_g, w_ffn_in=m_w_ffn_in, w_ffn_out=m_w_ffn_out)
    v = dict(norm_mix_g=v_norm_mix_g, norm_mem_g=v_norm_mem_g, w_in=v_w_in, w_gate=v_w_gate, ret_decay_fwd=v_ret_decay_fwd,
             ret_decay_bwd=v_ret_decay_bwd, ret_norm_g=v_ret_norm_g, pool_w=v_pool_w, pool_scale=v_pool_scale, na_q_norm_g=v_na_q_norm_g,
             na_k_norm_g=v_na_k_norm_g, na_rpb=v_na_rpb, mem_q_norm_g=v_mem_q_norm_g, mem_k_norm_g=v_mem_k_norm_g, w_mem_kv=v_w_mem_kv,
             w_branch=v_w_branch, w_out=v_w_out, norm_ffn_g=v_norm_ffn_g, w_ffn_in=v_w_ffn_in, w_ffn_out=v_w_ffn_out)
    assert x.shape == (1, 2048, D) and mem.shape == (1, N_MEM, D) and w_in.shape == (DEPTH, D, 9 * BW // N_DEV)

    first_groups = [[0, 1], [2, 3, 4], [5, 6]]
    blocks = [_to_exchange(name, tr, w[name][l]).astype(BF16) for l in range(DEPTH) for name, tr in BIG]
    started = _ici_start(blocks, [lax.empty((N_DEV,) + b.shape, BF16) for b in blocks], "gather", "gather_ici_start",
                         [len(g) for g in first_groups] + [len(BIG)] * (DEPTH - 1))

    def get_group(l, only, after, tag):
        at = [l * len(BIG) + i for i in only]
        lands = _ici_wait(started, after, "gather_ici_wait_%d%s" % (l, tag), at)
        whole = _gather_d2d([started[2][i] for i in at], lands, "gather_d2d")
        return {BIG[i][0]: _whole_from_gathered(BIG[i][0], g) for i, g in zip(only, whole)}

    def get_layer(l, after):
        if l > 0:
            return get_group(l, list(range(len(BIG))), after, ""), lambda stage, after2: {}
        return (get_group(l, first_groups[0], started[4], "a"),
                lambda stage, after2: get_group(l, first_groups[stage], after2, "abc"[stage]))

    cidx = lax.axis_index("c").astype(jnp.int32).reshape(1)
    chip = (2 * lax.axis_index("x") + lax.axis_index("y")).astype(jnp.int32).reshape(1)
    in_flight = []

    def flip_of(name, tr):
        return (lambda a: jnp.swapaxes(a, 1, 2)) if name in ("w_in", "w_ffn_in") else (lambda a: a)

    def rows3(a):
        return a.reshape(DEPTH, -1, a.shape[-1])

    opt_in = {name: tuple(rows3(flip_of(name, tr)(t[name])) for t in (w, m, v)) for name, tr in BIG}
    opt_out = {name: tuple(lax.empty(opt_in[name][0].shape, F32) for _ in range(4)) for name, _ in BIG}

    device = (2 * chip + cidx).astype(jnp.int32)

    def finish(l, st, after):
        recv = _ici_wait(st, after, "rs_ici_wait_%d" % l)
        sums = _sum_own(st[2], recv, chip if st[5] == 3 else device, "rs_sum")
        for (name, tr), s in zip(BIG, sums):
            g = s if name in ("w_in", "w_ffn_in") else _from_exchange(name, tr, s)
            wx, mx, vx = opt_in[name]
            opt_out[name] = _adamw_layer(l, wx, g.reshape(-1, g.shape[-1]), mx, vx, opt_out[name], "adamw_" + name)

    def on_grads(l, gb, after):
        send = [_by_destination(name, gb[name]) for name, _ in BIG]
        if l > 0:
            send = [s.reshape((N_DEV,) + s.shape[2:]) for s in send]
            st = _ici_start(send, [lax.empty(s.shape, BF16) for s in send], "by_device", "rs_ici_start_%d" % l)
        else:
            from_core = _rs_core_swap(send, "rs_core_swap")
            chip_part = _pair_sum(send, from_core, cidx)
            st = _ici_start(chip_part, [lax.empty(p.shape, BF16) for p in chip_part], "by_chip", "rs_ici_start_%d" % l)
        in_flight.append((l, st))
        return st[4]

    loss_local, dx, small_g = _local_step(x[0], mem[0], loss_target[0], {n: w[n] for n in SMALL}, get_layer, on_grads)

    last_started = in_flight[-1][1][4]
    for l, st in in_flight[:-1]:
        finish(l, st, last_started)

    small_all, = _all_gather([_pack_small(small_g, loss_local) + last_started[0:1]], "gather_small")
    packed_g = _sum_slots(small_all, "small_sum")
    small_sum, loss = _unpack_small(packed_g, {n: w[n] for n in SMALL})
    d_, m_, v_ = _adamw(_pack_small({n: w[n] for n in SMALL}), packed_g, _pack_small({n: m[n] for n in SMALL}),
                        _pack_small({n: v[n] for n in SMALL}), "adamw_small")
    updated = d_[0:8]
    for name, _ in BIG:
        updated = updated + opt_out[name][0][1, 0:8, 0:128]
    finish(*in_flight[-1], updated)

    grads, delta, new_m, new_v = {}, {}, {}, {}
    for name, tr in BIG:
        shape = flip_of(name, tr)(w[name]).shape
        delta[name], new_m[name], new_v[name], grads[name] = (flip_of(name, tr)(a.reshape(shape)) for a in opt_out[name])
    like = {n: w[n] for n in SMALL}
    ds, _ = _unpack_small(d_, like)
    ms, _ = _unpack_small(m_, like)
    vs, _ = _unpack_small(v_, like)
    for n in SMALL:
        grads[n], delta[n], new_m[n], new_v[n] = small_sum[n], ds[n], ms[n], vs[n]

    return (loss, dx[None], *[grads[n] for n in WEIGHTS], *[delta[n] for n in WEIGHTS], *[new_m[n] for n in WEIGHTS],
            *[new_v[n] for n in WEIGHTS])
```

```python
import functools

import numpy as np
import jax
import jax.numpy as jnp
from jax import lax
from jax.experimental import pallas as pl
from jax.experimental.pallas import tpu as pltpu

F32 = jnp.float32
BF16 = jnp.bfloat16
MXU = jnp.bfloat16
HI = lax.Precision.HIGHEST

DEPTH = 4
D = 1024
BW = 256
HD = 64
NH = 4
GRID_W = 64
NA_ROWS_WIN = 8
NA_COLS_WIN = 16
N_MEM = 256
FF = 2816
EPS = 1e-6
NEG = -1e30
ROPE_THETA = 10000.0
POOL_HALF_MAX = 8

ADAM_LR, ADAM_B1, ADAM_B2, ADAM_EPS, ADAM_WD, ADAM_STEP = 0.001, 0.9, 0.999, 1e-08, 0.01, 10

N_DEV = 8
VMEM_LIMIT = 56 * 1024 * 1024
MM_VMEM_BUDGET = 40 * 1024 * 1024

RQ, RK, RV, RG, PV, NQ, NK, NV, MQ = range(9)

MESH = pl.DeviceIdType.MESH
ANY = pl.BlockSpec(memory_space=pl.ANY)
SMEM = pl.BlockSpec(memory_space=pltpu.SMEM)


def _cp(**kw):
    return pltpu.CompilerParams(vmem_limit_bytes=VMEM_LIMIT, **kw)


def _tile(n, cap):
    if n <= cap:
        return n
    best = None
    for t in range(128, cap + 1, 128):
        if n % t == 0:
            best = t
    assert best is not None, (n, cap)
    return best


def _sds(shape, dtype):
    return jax.ShapeDtypeStruct(shape, dtype)


def _lane_head(shape):
    return lax.shift_right_logical(lax.broadcasted_iota(jnp.int32, shape, len(shape) - 1), 6)


def _group_mean(z):
    i = lax.shift_right_logical(lax.broadcasted_iota(jnp.int32, (BW, BW), 0), 6)
    j = lax.shift_right_logical(lax.broadcasted_iota(jnp.int32, (BW, BW), 1), 6)
    g = jnp.where(i == j, 1.0 / HD, 0.0).astype(BF16)
    z_hi = z.astype(BF16)
    z_lo = (z - z_hi.astype(F32)).astype(BF16)
    return jnp.dot(z_hi, g, preferred_element_type=F32) + jnp.dot(z_lo, g, preferred_element_type=F32)


def _gnorm(t, g):
    r = lax.rsqrt(_group_mean(t * t) + EPS)
    return t * r * g


def _gnorm_bwd(dy, t, g):
    r = lax.rsqrt(_group_mean(t * t) + EPS)
    th = t * r
    dth = dy * g
    dt = r * (dth - th * _group_mean(dth * th))
    return dt, dy * th


def _swap_halves(t):
    lane = lax.broadcasted_iota(jnp.int32, t.shape, 1)
    return jnp.where((lane & 63) < 32, pltpu.roll(t, BW - 32, 1), pltpu.roll(t, 32, 1))


def _sigmoid(x):
    return 1.0 / (1.0 + jnp.exp(-x))


def _dot(a, b, ta=False, tb=False):
    return lax.dot_general(a.astype(MXU), b.astype(MXU), (((0 if ta else 1,), (1 if tb else 0,)), ((), ())),
                           preferred_element_type=F32)


def _stack_heads(t):
    head = _lane_head(t.shape)
    return jnp.concatenate([jnp.where(head == h, t, jnp.zeros_like(t)) for h in range(NH)], axis=0)


def _unstack_heads(t, rows):
    head = _lane_head((rows, BW))
    out = jnp.zeros((rows, BW), F32)
    for h in range(NH):
        out = out + jnp.where(head == h, t[h * rows:(h + 1) * rows], 0.0)
    return out


def _softmax_rows(s):
    m = jnp.max(s, axis=-1, keepdims=True)
    e = jnp.exp(s - m)
    return e / jnp.sum(e, axis=-1, keepdims=True)


def _acc(ref, val, first):
    @pl.when(first)
    def _():
        ref[...] = val

    @pl.when(jnp.logical_not(first))
    def _():
        ref[...] += val


def _mm(a, b, *, ta=False, tb=False, out_dtype=F32, add=None, dep=None, b_half=None, out_half=None, norm_g=None, name):
    m, k = (a.shape[1], a.shape[0]) if ta else a.shape
    n = b.shape[0] if tb else b.shape[1]
    assert b_half is None or (not tb and b.shape[0] == 2 * k)
    tm, tn = _tile(m, 1408), (n if norm_g is not None else _tile(n, 768))
    if not ta and m <= 2048:
        blocks = (m * k * a.dtype.itemsize + k * tn * b.dtype.itemsize + m * tn * jnp.dtype(out_dtype).itemsize
                  + (m * tn * 4 if add is not None else 0) + (m * tn * 2 if norm_g is not None else 0))
        if 2 * blocks <= MM_VMEM_BUDGET:
            tm = m
    n_in = 2 + (add is not None) + (dep is not None) + (out_half is not None) + (norm_g is not None)

    def body(*refs):
        a_ref, b_ref, o_ref = refs[0], refs[1], refs[n_in]
        r = _dot(a_ref[...], b_ref[...], ta, tb)
        if add is not None:
            r = r + refs[2][...]
        o_ref[...] = r.astype(out_dtype)
        if norm_g is not None:
            scale = lax.rsqrt(jnp.mean(r * r, axis=-1, keepdims=True) + EPS)
            refs[n_in + 1][...] = (r * scale * refs[n_in - 1][...]).astype(BF16)

    kb = 0 if b_half is None else b_half
    a_spec = pl.BlockSpec((k, tm), lambda i, j: (0, i)) if ta else pl.BlockSpec((tm, k), lambda i, j: (i, 0))
    b_spec = pl.BlockSpec((tn, k), lambda i, j: (j, 0)) if tb else pl.BlockSpec((k, tn), lambda i, j: (kb, j))
    plain = pl.BlockSpec((tm, tn), lambda i, j: (i, j))
    ins, args = [a_spec, b_spec], [a, b]
    if add is not None:
        ins.append(plain)
        args.append(add)
    if dep is not None:
        ins.append(pl.BlockSpec((8, 128), lambda i, j: (0, 0)))
        args.append(dep)
    o_spec, o_shape, aliases = plain, _sds((m, n), out_dtype), {}
    if out_half is not None:
        half, prev = out_half
        o_spec = pl.BlockSpec((tm, tn), lambda i, j: (i + half * (m // tm), j))
        o_shape = _sds((2 * m, n), out_dtype)
        ins.append(ANY)
        args.append(lax.empty((2 * m, n), out_dtype) if prev is None else prev)
        aliases = {len(args) - 1: 0}
    if norm_g is not None:
        ins.append(pl.BlockSpec((1, n), lambda i, j: (0, 0)))
        args.append(norm_g)
        o_spec, o_shape = (o_spec, plain), (o_shape, _sds((m, n), BF16))
    return pl.pallas_call(
        body, grid=(m // tm, n // tn), in_specs=ins, out_specs=o_spec, out_shape=o_shape, input_output_aliases=aliases, name=name,
        compiler_params=_cp(dimension_semantics=("parallel", "parallel")))(*args)


def _mm_norm_bwd(a, b, add, x, g, res, *, b_half=None, name):
    m, k = a.shape
    n = b.shape[1]
    tm = 512
    kb = 0 if b_half is None else b_half

    def body(a_ref, b_ref, c_ref, x_ref, g_ref, res_ref, dx_ref, dxb_ref, dg_ref):
        dhv = _dot(a_ref[...], b_ref[...]) + c_ref[...]
        xv = x_ref[...]
        r = lax.rsqrt(jnp.mean(xv * xv, axis=-1, keepdims=True) + EPS)
        xh = xv * r
        dxh = dhv * g_ref[...]
        dx = res_ref[...] + r * (dxh - xh * jnp.mean(dxh * xh, axis=-1, keepdims=True))
        dx_ref[...] = dx
        dxb_ref[...] = dx.astype(BF16)
        _acc(dg_ref, jnp.sum(dhv * xh, axis=0, keepdims=True), pl.program_id(0) == 0)

    row = pl.BlockSpec((tm, n), lambda i: (i, 0))
    vec = pl.BlockSpec((1, n), lambda i: (0, 0))
    return pl.pallas_call(
        body, grid=(m // tm,),
        in_specs=[pl.BlockSpec((tm, k), lambda i: (i, 0)), pl.BlockSpec((k, n), lambda i: (kb, 0)), row, row, vec, row],
        out_specs=(row, row, vec), out_shape=(_sds((m, n), F32), _sds((m, n), BF16), _sds((1, n), F32)), name=name,
        compiler_params=_cp())(a, b, add, x, g, res)


def _rmsnorm_fwd(x, g, name):
    t, d = x.shape
    tm = _tile(t, 512)

    def body(x_ref, g_ref, o_ref):
        xv = x_ref[...]
        r = lax.rsqrt(jnp.mean(xv * xv, axis=-1, keepdims=True) + EPS)
        o_ref[...] = (xv * r * g_ref[...]).astype(o_ref.dtype)

    return pl.pallas_call(
        body, grid=(t // tm,), in_specs=[pl.BlockSpec((tm, d), lambda i: (i, 0)), pl.BlockSpec((1, d), lambda i: (0, 0))],
        out_specs=pl.BlockSpec((tm, d), lambda i: (i, 0)), out_shape=_sds((t, d), BF16), name=name, compiler_params=_cp())(x, g)


def _rmsnorm_bwd(dh, x, g, res, name):
    t, d = x.shape
    tm = _tile(t, 512)

    def body(dh_ref, x_ref, g_ref, res_ref, dx_ref, dxb_ref, dg_ref):
        xv = x_ref[...]
        dhv = dh_ref[...]
        r = lax.rsqrt(jnp.mean(xv * xv, axis=-1, keepdims=True) + EPS)
        xh = xv * r
        dxh = dhv * g_ref[...]
        dx = res_ref[...] + r * (dxh - xh * jnp.mean(dxh * xh, axis=-1, keepdims=True))
        dx_ref[...] = dx
        dxb_ref[...] = dx.astype(BF16)
        _acc(dg_ref, jnp.sum(dhv * xh, axis=0, keepdims=True), pl.program_id(0) == 0)

    row = pl.BlockSpec((tm, d), lambda i: (i, 0))
    vec = pl.BlockSpec((1, d), lambda i: (0, 0))
    return pl.pallas_call(
        body, grid=(t // tm,), in_specs=[row, row, vec, row], out_specs=(row, row, vec),
        out_shape=(_sds((t, d), F32), _sds((t, d), BF16), _sds((1, d), F32)), name=name, compiler_params=_cp())(dh, x, g, res)


def _prep_fwd(proj, cos2, sin2, g_naq, g_nak, g_mq):
    t = proj.shape[0]
    tm = 512

    def body(p_ref, cos_ref, sin_ref, gq_ref, gk_ref, gm_ref, rq_ref, rk_ref, rv_ref, nq_ref, nk_ref, nv_ref, mq_ref):
        def col(c):
            return p_ref[:, c * BW:(c + 1) * BW]

        cosv, sinv = cos_ref[...], sin_ref[...]

        def rot(tv):
            return tv * cosv + _swap_halves(tv) * sinv

        rq_ref[...] = (rot(col(RQ)) * (HD ** -0.5)).astype(BF16)
        rk_ref[...] = rot(col(RK)).astype(BF16)
        rv_ref[...] = col(RV).astype(BF16)
        nq_ref[...] = _gnorm(col(NQ), gq_ref[...]).astype(BF16)
        nk_ref[...] = _gnorm(col(NK), gk_ref[...]).astype(BF16)
        nv_ref[...] = col(NV).astype(BF16)
        mq_ref[...] = _gnorm(col(MQ), gm_ref[...]).astype(BF16)

    blk = pl.BlockSpec((tm, BW), lambda i: (i, 0))
    vec = pl.BlockSpec((1, BW), lambda i: (0, 0))
    return pl.pallas_call(
        body, grid=(t // tm,), in_specs=[pl.BlockSpec((tm, 9 * BW), lambda i: (i, 0)), blk, blk, vec, vec, vec],
        out_specs=tuple(blk for _ in range(7)), out_shape=tuple(_sds((t, BW), BF16) for _ in range(7)),
        name="prep_fwd", compiler_params=_cp())(proj, cos2, sin2, g_naq, g_nak, g_mq)


def _prep_bwd(proj, cos2, sin2, g_naq, g_nak, g_mq, d_rq, d_rk, d_rv, d_rg, d_pv, d_nq, d_nk, d_nv, d_mq):
    t = proj.shape[0]
    tm = 512

    def body(p_ref, cos_ref, sin_ref, gq_ref, gk_ref, gm_ref, drq_ref, drk_ref, drv_ref, drg_ref, dpv_ref, dnq_ref, dnk_ref,
             dnv_ref, dmq_ref, o_ref, dgq_ref, dgk_ref, dgm_ref):
        first = pl.program_id(0) == 0

        def col(c):
            return p_ref[:, c * BW:(c + 1) * BW]

        def put(c, v):
            o_ref[:, c * BW:(c + 1) * BW] = v.astype(BF16)

        cosv, sinv = cos_ref[...], sin_ref[...]

        def rot_t(dv):
            return dv * cosv + _swap_halves(dv * sinv)

        put(RQ, rot_t(drq_ref[...] * (HD ** -0.5)))
        put(RK, rot_t(drk_ref[...]))
        put(RV, drv_ref[...])
        put(RG, drg_ref[...])
        put(PV, dpv_ref[...])
        dq, gq = _gnorm_bwd(dnq_ref[...], col(NQ), gq_ref[...])
        put(NQ, dq)
        _acc(dgq_ref, jnp.sum(gq, axis=0, keepdims=True), first)
        dk, gk = _gnorm_bwd(dnk_ref[...], col(NK), gk_ref[...])
        put(NK, dk)
        _acc(dgk_ref, jnp.sum(gk, axis=0, keepdims=True), first)
        put(NV, dnv_ref[...])
        dm, gm = _gnorm_bwd(dmq_ref[...], col(MQ), gm_ref[...])
        put(MQ, dm)
        _acc(dgm_ref, jnp.sum(gm, axis=0, keepdims=True), first)

    blk = pl.BlockSpec((tm, BW), lambda i: (i, 0))
    vec = pl.BlockSpec((1, BW), lambda i: (0, 0))
    wide = pl.BlockSpec((tm, 9 * BW), lambda i: (i, 0))
    return pl.pallas_call(
        body, grid=(t // tm,), in_specs=[wide, blk, blk, vec, vec, vec] + [blk] * 9, out_specs=(wide, vec, vec, vec),
        out_shape=(_sds((t, 9 * BW), BF16), _sds((1, BW), F32), _sds((1, BW), F32), _sds((1, BW), F32)),
        name="prep_bwd", compiler_params=_cp())(proj, cos2, sin2, g_naq, g_nak, g_mq, d_rq, d_rk, d_rv, d_rg, d_pv, d_nq, d_nk,
                                                d_nv, d_mq)


RET_B = 256


def _ret_consts(lgf_ref, lgb_ref):
    bsz = RET_B
    head = _lane_head((1, BW))
    lf, lb = jnp.zeros((1, BW), F32), jnp.zeros((1, BW), F32)
    for h in range(NH):
        lf = lf + jnp.where(head == h, lgf_ref[h], 0.0)
        lb = lb + jnp.where(head == h, lgb_ref[h], 0.0)
    pos = lax.broadcasted_iota(jnp.int32, (bsz, BW), 0).astype(F32)
    up, down = pos + 1.0, (bsz - 1.0) - pos
    c = dict(up=up, down=down, kf=jnp.exp(down * lf), kb=jnp.exp(up * lb), qf=jnp.exp(up * lf), qb=jnp.exp(down * lb),
             cf=jnp.exp(bsz * lf), cb=jnp.exp(bsz * lb))
    diff = (lax.broadcasted_iota(jnp.int32, (NH * bsz, 1), 0) & (bsz - 1)) - lax.broadcasted_iota(jnp.int32, (1, bsz), 1)
    c["causal"] = diff >= 0
    c["dist"] = jnp.abs(diff).astype(F32)
    lgf = jnp.concatenate([jnp.full((bsz, 1), lgf_ref[h], F32) for h in range(NH)], axis=0)
    lgb = jnp.concatenate([jnp.full((bsz, 1), lgb_ref[h], F32) for h in range(NH)], axis=0)
    c["dm"] = jnp.exp(c["dist"] * jnp.where(c["causal"], lgf, lgb))
    c["bd"] = _lane_head((BW, BW)) == lax.shift_right_logical(lax.broadcasted_iota(jnp.int32, (BW, BW), 0), 6)
    return c


def _ret_states(k_ref, v_ref, st_ref, c, nb):
    bsz = RET_B

    def summary(b, decay):
        kb = k_ref[b * bsz:(b + 1) * bsz, :].astype(F32)
        return jnp.where(c["bd"], _dot(kb * decay, v_ref[b * bsz:(b + 1) * bsz, :], ta=True), 0.0)

    f = jnp.zeros((BW, BW), F32)
    for b in range(nb):
        st_ref[b] = f
        if b < nb - 1:
            f = c["cf"] * f + summary(b, c["kf"])
    g = jnp.zeros((BW, BW), F32)
    for b in reversed(range(nb)):
        st_ref[nb + b] = g
        if b > 0:
            g = c["cb"] * g + summary(b, c["kb"])


def _ret_fwd(q, k, v, proj, lgf, lgb, g_ret):
    t = q.shape[0]
    bsz, nb = RET_B, t // RET_B

    def body(lgf_ref, lgb_ref, q_ref, k_ref, v_ref, rg_ref, g_ref, o_ref, ret_ref, st_ref):
        c = _ret_consts(lgf_ref, lgb_ref)
        _ret_states(k_ref, v_ref, st_ref, c, nb)
        for b in range(nb):
            blk = slice(b * bsz, (b + 1) * bsz)
            qb, kb, vb = q_ref[blk, :], k_ref[blk, :], v_ref[blk, :]
            s = _dot(_stack_heads(qb), kb, tb=True)
            o = _unstack_heads(_dot(s * c["dm"], vb), bsz)
            q32 = qb.astype(F32)
            o = o + _dot(q32 * c["qf"], st_ref[b]) + _dot(q32 * c["qb"], st_ref[nb + b])
            o_ref[blk, :] = o
            rg = rg_ref[blk, :]
            ret_ref[blk, :] = (_gnorm(o, g_ref[...]) * (rg * _sigmoid(rg))).astype(BF16)

    whole = pl.BlockSpec((t, BW), lambda i: (0, 0))
    return pl.pallas_call(
        body, grid=(1,),
        in_specs=[SMEM, SMEM, whole, whole, whole, pl.BlockSpec((t, BW), lambda i: (0, RG)), pl.BlockSpec((1, BW), lambda i: (0, 0))],
        out_specs=(whole, whole), out_shape=(_sds((t, BW), F32), _sds((t, BW), BF16)),
        scratch_shapes=[pltpu.VMEM((2 * nb, BW, BW), F32)], name="ret_fwd", compiler_params=_cp())(lgf, lgb, q, k, v, proj, g_ret)


def _ret_post_bwd(dbr, o_ret, proj, g_ret):
    t = o_ret.shape[0]
    tm = 512

    def body(d_ref, o_ref, rg_ref, g_ref, do_ref, drg_ref, dg_ref):
        dret, o, rg, g = d_ref[...], o_ref[...], rg_ref[...], g_ref[...]
        sg = _sigmoid(rg)
        do, dgain = _gnorm_bwd(dret * (rg * sg), o, g)
        do_ref[...] = do.astype(BF16)
        drg_ref[...] = dret * _gnorm(o, g) * (sg * (1.0 + rg * (1.0 - sg)))
        _acc(dg_ref, jnp.sum(dgain, axis=0, keepdims=True), pl.program_id(0) == 0)

    blk = pl.BlockSpec((tm, BW), lambda i: (i, 0))
    vec = pl.BlockSpec((1, BW), lambda i: (0, 0))
    return pl.pallas_call(
        body, grid=(t // tm,), in_specs=[blk, blk, pl.BlockSpec((tm, BW), lambda i: (i, RG)), vec], out_specs=(blk, blk, vec),
        out_shape=(_sds((t, BW), BF16), _sds((t, BW), F32), _sds((1, BW), F32)), name="ret_post_bwd",
        compiler_params=_cp())(dbr, o_ret, proj, g_ret)


def _ret_bwd(do, q, k, v, lgf, lgb):
    t = q.shape[0]
    bsz, nb = RET_B, t // RET_B

    def body(lgf_ref, lgb_ref, d_ref, q_ref, k_ref, v_ref, dq_ref, dk_ref, dv_ref, dlg_ref, st_ref, sd_ref):
        c = _ret_consts(lgf_ref, lgb_ref)
        _ret_states(k_ref, v_ref, st_ref, c, nb)
        lane_f, lane_b = jnp.zeros((1, BW), F32), jnp.zeros((1, BW), F32)
        row_f, row_b = jnp.zeros((NH * bsz, 1), F32), jnp.zeros((NH * bsz, 1), F32)

        def rows(x):
            return jnp.sum(x, axis=0, keepdims=True)

        for b in range(nb):
            blk = slice(b * bsz, (b + 1) * bsz)
            qb, kb, vb, dob = q_ref[blk, :], k_ref[blk, :], v_ref[blk, :], d_ref[blk, :]
            q32 = qb.astype(F32)
            qs, dos = _stack_heads(qb), _stack_heads(dob)
            s = _dot(qs, kb, tb=True)
            da = _dot(dos, vb, tb=True)
            dv_ref[blk, :] = _dot(s * c["dm"], dos, ta=True)
            ds = da * c["dm"]
            w = ds * s * c["dist"]
            row_f = row_f + jnp.sum(jnp.where(c["causal"], w, 0.0), axis=1, keepdims=True)
            row_b = row_b + jnp.sum(jnp.where(c["causal"], 0.0, w), axis=1, keepdims=True)
            dsb = ds.astype(MXU)
            dk_ref[blk, :] = _dot(dsb, qs, ta=True)
            dq_f = _dot(dob, st_ref[b], tb=True) * c["qf"]
            dq_b = _dot(dob, st_ref[nb + b], tb=True) * c["qb"]
            lane_f = lane_f + rows(c["up"] * dq_f * q32)
            lane_b = lane_b + rows(c["down"] * dq_b * q32)
            dq_ref[blk, :] = _unstack_heads(_dot(dsb, kb), bsz) + dq_f + dq_b
            sd_ref[b] = jnp.where(c["bd"], _dot(q32 * c["qf"], dob, ta=True), 0.0)
            sd_ref[nb + b] = jnp.where(c["bd"], _dot(q32 * c["qb"], dob, ta=True), 0.0)

        def through_state(b, grad, decay, weight, lane):
            blk = slice(b * bsz, (b + 1) * bsz)
            k32 = k_ref[blk, :].astype(F32)
            dk = _dot(v_ref[blk, :], grad, tb=True) * decay
            dk_ref[blk, :] += dk
            dv_ref[blk, :] += _dot(k32 * decay, grad)
            return lane + rows(weight * dk * k32)

        phi = jnp.zeros((BW, BW), F32)
        for b in reversed(range(nb)):
            if b < nb - 1:
                lane_f = through_state(b, phi, c["kf"], c["down"], lane_f)
                lane_f = lane_f + bsz * rows(c["cf"] * st_ref[b] * phi)
            phi = sd_ref[b] + c["cf"] * phi
        gam = jnp.zeros((BW, BW), F32)
        for b in range(nb):
            if b > 0:
                lane_b = through_state(b, gam, c["kb"], c["up"], lane_b)
                lane_b = lane_b + bsz * rows(c["cb"] * st_ref[nb + b] * gam)
            gam = sd_ref[nb + b] + c["cb"] * gam

        head = _lane_head((1, BW))
        for h in range(NH):
            tot_f = jnp.sum(row_f[h * bsz:(h + 1) * bsz, :]) + jnp.sum(jnp.where(head == h, lane_f, 0.0))
            tot_b = jnp.sum(row_b[h * bsz:(h + 1) * bsz, :]) + jnp.sum(jnp.where(head == h, lane_b, 0.0))
            dlg_ref[h:h + 1, :] = jnp.full((1, 128), tot_f, F32)
            dlg_ref[NH + h:NH + h + 1, :] = jnp.full((1, 128), tot_b, F32)

    whole = pl.BlockSpec((t, BW), lambda i: (0, 0))
    return pl.pallas_call(
        body, grid=(1,), in_specs=[SMEM, SMEM, whole, whole, whole, whole],
        out_specs=(whole, whole, whole, pl.BlockSpec((2 * NH, 128), lambda i: (0, 0))),
        out_shape=(_sds((t, BW), F32), _sds((t, BW), F32), _sds((t, BW), F32), _sds((2 * NH, 128), F32)),
        scratch_shapes=[pltpu.VMEM((2 * nb, BW, BW), F32), pltpu.VMEM((2 * nb, BW, BW), F32)], name="ret_bwd",
        compiler_params=_cp())(lgf, lgb, do, q, k, v)


def _pool_windows(t):
    row = lax.broadcasted_iota(jnp.int32, (t, BW), 0)
    half = lax.shift_left(jnp.ones((t, BW), jnp.int32), _lane_head((t, BW)))
    cnt = (jnp.minimum(row + half, t) - jnp.maximum(row - half, 0)).astype(F32)
    return row, half, cnt


def _pool_window_sum(v, row, half, t, transpose):
    out = jnp.zeros_like(v)
    for j in range(-POOL_HALF_MAX, POOL_HALF_MAX):
        src = row - j if transpose else row + j
        ok = (src >= 0) & (src < t) & (j >= -half) & (j < half)
        out = out + jnp.where(ok, pltpu.roll(v, (j if transpose else -j) % t, 0), 0.0)
    return out


def _pool_fwd(proj, wbd, scale):
    t = proj.shape[0]

    def body(v_ref, w_ref, s_ref, o_ref):
        v = v_ref[...]
        row, half, cnt = _pool_windows(t)
        pooled = _pool_window_sum(v, row, half, t, False) / cnt - v
        o_ref[...] = (_dot(pooled, w_ref[...]) * s_ref[...]).astype(BF16)

    return pl.pallas_call(
        body, grid=(1,),
        in_specs=[pl.BlockSpec((t, BW), lambda i: (0, PV)), pl.BlockSpec((BW, BW), lambda i: (0, 0)), pl.BlockSpec((1, BW), lambda i: (0, 0))],
        out_specs=pl.BlockSpec((t, BW), lambda i: (0, 0)), out_shape=_sds((t, BW), BF16), name="pool_fwd",
        compiler_params=_cp())(proj, wbd, scale)


def _pool_bwd(dbr, proj, wbd, scale):
    t = proj.shape[0]

    def body(d_ref, v_ref, w_ref, s_ref, dv_ref, dw_ref, ds_ref):
        v, dout = v_ref[...], d_ref[...]
        row, half, cnt = _pool_windows(t)
        pooled = _pool_window_sum(v, row, half, t, False) / cnt - v
        mixed = _dot(pooled, w_ref[...])
        ds_ref[...] = jnp.sum(dout * mixed, axis=0, keepdims=True)
        dmixed = dout * s_ref[...]
        dw_ref[...] = _dot(pooled, dmixed, ta=True)
        dpooled = _dot(dmixed, w_ref[...], tb=True)
        dv_ref[...] = _pool_window_sum(dpooled / cnt, row, half, t, True) - dpooled

    return pl.pallas_call(
        body, grid=(1,),
        in_specs=[pl.BlockSpec((t, BW), lambda i: (0, 1)), pl.BlockSpec((t, BW), lambda i: (0, PV)),
                  pl.BlockSpec((BW, BW), lambda i: (0, 0)), pl.BlockSpec((1, BW), lambda i: (0, 0))],
        out_specs=(pl.BlockSpec((t, BW), lambda i: (0, 0)), pl.BlockSpec((BW, BW), lambda i: (0, 0)), pl.BlockSpec((1, BW), lambda i: (0, 0))),
        out_shape=(_sds((t, BW), F32), _sds((BW, BW), F32), _sds((1, BW), F32)), name="pool_bwd",
        compiler_params=_cp())(dbr, proj, wbd, scale)


NA_KEYS = NA_ROWS_WIN * GRID_W
NA_PAIRS = 2 * NA_ROWS_WIN - 2


def _na_window(r, n_rows):
    rs = jnp.clip(r - NA_ROWS_WIN // 2, 0, n_rows - NA_ROWS_WIN)
    return pl.multiple_of(rs * GRID_W, GRID_W), rs - r + (NA_ROWS_WIN - 1)


def _na_bias(b_ref, a0):
    return jnp.concatenate([b_ref[a0 + 2 * j] for j in range(NA_ROWS_WIN // 2)], axis=1)


NA_STEP_ROWS = 16


def _na_fwd(q, k, v, ball):
    t = q.shape[0]
    n_rows = t // GRID_W
    rows = NA_STEP_ROWS

    def body(q_ref, k_ref, v_ref, b_ref, o_ref):
        for rr in range(rows):
            start, a0 = _na_window(pl.program_id(0) * rows + rr, n_rows)
            own = slice(rr * GRID_W, (rr + 1) * GRID_W)
            qs = _stack_heads(q_ref[own, :])
            s = _dot(qs, k_ref[pl.ds(start, NA_KEYS), :], tb=True) * (HD ** -0.5) + _na_bias(b_ref, a0)
            p = _softmax_rows(s)
            o_ref[own, :] = _unstack_heads(_dot(p, v_ref[pl.ds(start, NA_KEYS), :]), GRID_W).astype(BF16)

    blk = pl.BlockSpec((rows * GRID_W, BW), lambda r: (r, 0))
    whole = pl.BlockSpec((t, BW), lambda r: (0, 0))
    return pl.pallas_call(
        body, grid=(n_rows // rows,), in_specs=[blk, whole, whole, pl.BlockSpec(ball.shape, lambda r: (0, 0, 0))],
        out_specs=blk, out_shape=_sds((t, BW), BF16), name="na_fwd", compiler_params=_cp())(q, k, v, ball)


def _na_bwd(dbr, q, k, v, ball):
    t = q.shape[0]
    n_rows = t // GRID_W

    rows = NA_STEP_ROWS

    def body(d_ref, q_ref, k_ref, v_ref, b_ref, dq_ref, dk_ref, dv_ref, db_ref):
        @pl.when(pl.program_id(0) == 0)
        def _():
            dk_ref[...] = jnp.zeros_like(dk_ref)
            dv_ref[...] = jnp.zeros_like(dv_ref)
            db_ref[...] = jnp.zeros_like(db_ref)

        for rr in range(rows):
            start, a0 = _na_window(pl.program_id(0) * rows + rr, n_rows)
            keys = pl.ds(start, NA_KEYS)
            own = slice(rr * GRID_W, (rr + 1) * GRID_W)
            qs = _stack_heads(q_ref[own, :])
            kb, vb = k_ref[keys, :], v_ref[keys, :]
            p = _softmax_rows(_dot(qs, kb, tb=True) * (HD ** -0.5) + _na_bias(b_ref, a0))
            dos = _stack_heads(d_ref[own, :]).astype(MXU)
            dp = _dot(dos, vb, tb=True)
            dv_ref[keys, :] += _dot(p, dos, ta=True)
            ds = p * (dp - jnp.sum(dp * p, axis=-1, keepdims=True))
            for j in range(NA_ROWS_WIN // 2):
                db_ref[a0 + 2 * j] += ds[:, 2 * j * GRID_W:(2 * j + 2) * GRID_W]
            dsb = (ds * (HD ** -0.5)).astype(MXU)
            dq_ref[own, :] = _unstack_heads(_dot(dsb, kb), GRID_W)
            dk_ref[keys, :] += _dot(dsb, qs, ta=True)

    blk = pl.BlockSpec((rows * GRID_W, BW), lambda r: (r, 0))
    whole = pl.BlockSpec((t, BW), lambda r: (0, 0))
    tab = pl.BlockSpec(ball.shape, lambda r: (0, 0, 0))
    return pl.pallas_call(
        body, grid=(n_rows // rows,), in_specs=[pl.BlockSpec((rows * GRID_W, BW), lambda r: (r, 2)), blk, whole, whole, tab],
        out_specs=(blk, whole, whole, tab),
        out_shape=(_sds((t, BW), F32), _sds((t, BW), F32), _sds((t, BW), F32), _sds(ball.shape, F32)), name="na_bwd",
        compiler_params=_cp())(dbr, q, k, v, ball)


def _rpb_expand(rpb_pad, onehot):
    def body(r_ref, e_ref, o_ref):
        o_ref[...] = jnp.dot(r_ref[...], e_ref[...], precision=HI, preferred_element_type=F32)

    return pl.pallas_call(body, out_shape=_sds((rpb_pad.shape[0], GRID_W * GRID_W), F32), name="rpb_expand",
                          compiler_params=_cp())(rpb_pad, onehot)


def _rpb_reduce(dtab, onehot):
    def body(d_ref, e_ref, o_ref):
        o_ref[...] = lax.dot_general(d_ref[...], e_ref[...], (((1,), (1,)), ((), ())), precision=HI, preferred_element_type=F32)

    return pl.pallas_call(body, out_shape=_sds((dtab.shape[0], 128), F32), name="rpb_reduce", compiler_params=_cp())(dtab, onehot)


MEM_TQ = 512


def _mem_fwd(q, mk, mv):
    t = q.shape[0]
    tq = MEM_TQ

    def body(q_ref, k_ref, v_ref, o_ref):
        p = _softmax_rows(_dot(_stack_heads(q_ref[...]), k_ref[...], tb=True) * (HD ** -0.5))
        o_ref[...] = _unstack_heads(_dot(p, v_ref[...]), tq).astype(BF16)

    blk = pl.BlockSpec((tq, BW), lambda i: (i, 0))
    kv = pl.BlockSpec((N_MEM, BW), lambda i: (0, 0))
    return pl.pallas_call(body, grid=(t // tq,), in_specs=[blk, kv, kv], out_specs=blk, out_shape=_sds((t, BW), BF16),
                          name="mem_fwd", compiler_params=_cp())(q, mk, mv)


def _mem_bwd(dbr, q, mk, mv):
    t = q.shape[0]
    tq = MEM_TQ

    def body(d_ref, q_ref, k_ref, v_ref, dq_ref, dk_ref, dv_ref):
        first = pl.program_id(0) == 0
        qs = _stack_heads(q_ref[...])
        dos = _stack_heads(d_ref[...]).astype(MXU)
        p = _softmax_rows(_dot(qs, k_ref[...], tb=True) * (HD ** -0.5))
        dp = _dot(dos, v_ref[...], tb=True)
        _acc(dv_ref, _dot(p, dos, ta=True), first)
        dsb = (p * (dp - jnp.sum(dp * p, axis=-1, keepdims=True)) * (HD ** -0.5)).astype(MXU)
        dq_ref[...] = _unstack_heads(_dot(dsb, k_ref[...]), tq)
        _acc(dk_ref, _dot(dsb, qs, ta=True), first)

    blk = pl.BlockSpec((tq, BW), lambda i: (i, 0))
    kv = pl.BlockSpec((N_MEM, BW), lambda i: (0, 0))
    return pl.pallas_call(
        body, grid=(t // tq,), in_specs=[pl.BlockSpec((tq, BW), lambda i: (i, 3)), blk, kv, kv], out_specs=(blk, kv, kv),
        out_shape=(_sds((t, BW), F32), _sds((N_MEM, BW), F32), _sds((N_MEM, BW), F32)), name="mem_bwd",
        compiler_params=_cp())(dbr, q, mk, mv)


def _memkv_prep(kv, g_mk):
    def body(kv_ref, g_ref, k_ref, v_ref):
        k_ref[...] = _gnorm(kv_ref[:, 0:BW], g_ref[...]).astype(BF16)
        v_ref[...] = kv_ref[:, BW:2 * BW].astype(BF16)

    return pl.pallas_call(body, out_shape=(_sds((N_MEM, BW), BF16), _sds((N_MEM, BW), BF16)), name="memkv_prep",
                          compiler_params=_cp())(kv, g_mk)


def _memkv_bwd(kv, dk, dv, g_mk):
    def body(kv_ref, dk_ref, dv_ref, g_ref, o_ref, dg_ref):
        dkk, gain = _gnorm_bwd(dk_ref[...], kv_ref[:, 0:BW], g_ref[...])
        o_ref[:, 0:BW] = dkk.astype(BF16)
        o_ref[:, BW:2 * BW] = dv_ref[...].astype(BF16)
        dg_ref[...] = jnp.sum(gain, axis=0, keepdims=True)

    return pl.pallas_call(body, out_shape=(_sds((N_MEM, 2 * BW), BF16), _sds((1, BW), F32)), name="memkv_bwd",
                          compiler_params=_cp())(kv, dk, dv, g_mk)


MERGE_TM = 512


def _merge_fwd(brs, wbt, gp):
    t = gp.shape[0]
    tm = MERGE_TM

    def body(b0, b1, b2, b3, wb_ref, gp_ref, o_ref):
        out = jnp.zeros((tm, D), F32)
        for n, b_ref in enumerate((b0, b1, b2, b3)):
            up = _dot(b_ref[...], wb_ref[n], tb=True)
            out = out + _sigmoid(gp_ref[:, n * D:(n + 1) * D].astype(F32)) * up
        o_ref[...] = out.astype(BF16)

    blk = pl.BlockSpec((tm, BW), lambda i: (i, 0))
    return pl.pallas_call(
        body, grid=(t // tm,),
        in_specs=[blk, blk, blk, blk, pl.BlockSpec((NH, D, BW), lambda i: (0, 0, 0)), pl.BlockSpec((tm, NH * D), lambda i: (i, 0))],
        out_specs=pl.BlockSpec((tm, D), lambda i: (i, 0)), out_shape=_sds((t, D), BF16), name="merge_fwd",
        compiler_params=_cp())(*brs, wbt, gp)


def _merge_bwd(dmerged, brs, wbt, gp):
    t = gp.shape[0]
    tm = MERGE_TM
    steps = t // tm

    def body(d_ref, b0, b1, b2, b3, wb_ref, gp_ref, dgp_ref, dbr_ref, dwb_ref, acc_ref):
        i = pl.program_id(0)
        dm = d_ref[...]
        for n, b_ref in enumerate((b0, b1, b2, b3)):
            br = b_ref[...]
            up = _dot(br, wb_ref[n], tb=True)
            g = _sigmoid(gp_ref[:, n * D:(n + 1) * D].astype(F32))
            dgp_ref[:, n * D:(n + 1) * D] = (dm * up * (g * (1.0 - g))).astype(BF16)
            dup = (dm * g).astype(BF16)
            dbr_ref[:, n * BW:(n + 1) * BW] = _dot(dup, wb_ref[n])
            part = _dot(dup, br, ta=True)

            @pl.when(i == 0)
            def _():
                acc_ref[n] = part

            @pl.when(i > 0)
            def _():
                acc_ref[n] += part

        @pl.when(i == steps - 1)
        def _():
            dwb_ref[...] = acc_ref[...].astype(BF16)

    row = pl.BlockSpec((tm, D), lambda i: (i, 0))
    blk = pl.BlockSpec((tm, BW), lambda i: (i, 0))
    wide = pl.BlockSpec((tm, NH * D), lambda i: (i, 0))
    whole = pl.BlockSpec((NH, D, BW), lambda i: (0, 0, 0))
    return pl.pallas_call(
        body, grid=(steps,), in_specs=[row, blk, blk, blk, blk, whole, wide], out_specs=(wide, row, whole),
        out_shape=(_sds((t, NH * D), BF16), _sds((t, NH * BW), F32), _sds((NH, D, BW), BF16)),
        scratch_shapes=[pltpu.VMEM((NH, D, BW), F32)], name="merge_bwd", compiler_params=_cp())(dmerged, *brs, wbt, gp)


FFN_TN = 256


def _ffn_in_fwd(h2, w_t):
    t = h2.shape[0]
    tm, tn = _tile(t, 2048), FFN_TN
    nj = FF // tn

    def body(x_ref, wa_ref, wg_ref, a_ref, g_ref, y_ref):
        x = x_ref[...]
        a, g = _dot(x, wa_ref[...], tb=True), _dot(x, wg_ref[...], tb=True)
        a_ref[...] = a.astype(BF16)
        g_ref[...] = g.astype(BF16)
        y_ref[...] = (a * _sigmoid(a) * g).astype(BF16)

    out = pl.BlockSpec((tm, tn), lambda i, j: (i, j))
    return pl.pallas_call(
        body, grid=(t // tm, nj),
        in_specs=[pl.BlockSpec((tm, D), lambda i, j: (i, 0)), pl.BlockSpec((tn, D), lambda i, j: (j, 0)),
                  pl.BlockSpec((tn, D), lambda i, j: (j + nj, 0))],
        out_specs=(out, out, out), out_shape=tuple(_sds((t, FF), BF16) for _ in range(3)), name="ffn_in_fwd",
        compiler_params=_cp(dimension_semantics=("parallel", "parallel")))(h2, w_t, w_t)


def _ffn_out_bwd(dx2b, w_out, a, g, dep):
    t = dx2b.shape[0]
    tm, tn = _tile(t, 2048), FFN_TN

    def body(*refs):
        x_ref, w_ref, a_ref, g_ref = refs[:4]
        da_ref, dg_ref = refs[-2:]
        d = _dot(x_ref[...], w_ref[...], tb=True)
        av, gv = a_ref[...].astype(F32), g_ref[...].astype(F32)
        s = _sigmoid(av)
        da_ref[...] = (d * gv * (s * (1.0 + av * (1.0 - s)))).astype(BF16)
        dg_ref[...] = (d * (av * s)).astype(BF16)

    blk = pl.BlockSpec((tm, tn), lambda i, j: (i, j))
    ins = [pl.BlockSpec((tm, D), lambda i, j: (i, 0)), pl.BlockSpec((tn, D), lambda i, j: (j, 0)), blk, blk]
    args = [dx2b, w_out, a, g]
    if dep is not None:
        ins.append(pl.BlockSpec((8, 128), lambda i, j: (0, 0)))
        args.append(dep)
    return pl.pallas_call(
        body, grid=(t // tm, FF // tn), in_specs=ins, out_specs=(blk, blk),
        out_shape=(_sds((t, FF), BF16), _sds((t, FF), BF16)), name="ffn_out_bwd",
        compiler_params=_cp(dimension_semantics=("parallel", "parallel")))(*args)


def _loss_head(y, target):
    t, d = y.shape
    tm = 512

    def body(y_ref, t_ref, dy_ref, dyb_ref, l_ref):
        e = y_ref[...] - t_ref[...]
        dy_ref[...] = e * (1.0 / d)
        dyb_ref[...] = (e * (1.0 / d)).astype(BF16)
        _acc(l_ref, jnp.full((8, 128), 0.5 * jnp.sum(jnp.sum(e * e, axis=-1, keepdims=True) * (1.0 / d)), F32), pl.program_id(0) == 0)

    row = pl.BlockSpec((tm, d), lambda i: (i, 0))
    return pl.pallas_call(body, grid=(t // tm,), in_specs=[row, row], out_specs=(row, row, pl.BlockSpec((8, 128), lambda i: (0, 0))),
                          out_shape=(_sds((t, d), F32), _sds((t, d), BF16), _sds((8, 128), F32)), name="loss_head",
                          compiler_params=_cp())(y, target)


def _sum_slots(x, name):
    k, r, c = x.shape
    tr = _tile(r, 512) if r % 128 == 0 else r

    def body(x_ref, o_ref):
        acc = x_ref[0].astype(F32)
        for s in range(1, k):
            acc = acc + x_ref[s].astype(F32)
        o_ref[...] = acc

    return pl.pallas_call(body, grid=(r // tr,), in_specs=[pl.BlockSpec((k, tr, c), lambda i: (0, i, 0))],
                          out_specs=pl.BlockSpec((tr, c), lambda i: (i, 0)), out_shape=_sds((r, c), F32), name=name,
                          compiler_params=_cp())(x)


def _pair_sum(bufs, recvs, cidx):
    n = len(bufs)

    def body(c_ref, *refs):
        for i in range(n):
            refs[2 * n + i][...] = (refs[i][...].astype(F32) + refs[n + i][...].astype(F32)).astype(BF16)

    return pl.pallas_call(
        body,
        grid_spec=pltpu.PrefetchScalarGridSpec(
            num_scalar_prefetch=1, grid=(4,),
            in_specs=[pl.BlockSpec((None, None) + b.shape[2:], lambda s, cref: (s, cref[0], 0, 0)) for b in bufs]
            + [pl.BlockSpec((None,) + r.shape[1:], lambda s, cref: (s, 0, 0)) for r in recvs],
            out_specs=tuple(pl.BlockSpec((None,) + r.shape[1:], lambda s, cref: (s, 0, 0)) for r in recvs)),
        out_shape=tuple(_sds(r.shape, BF16) for r in recvs), name="rs_pair_sum", compiler_params=_cp())(cidx, *bufs, *recvs)


def _adamw_update(w, gv, m, v):
    mn = ADAM_B1 * m + (1.0 - ADAM_B1) * gv
    vn = ADAM_B2 * v + (1.0 - ADAM_B2) * (gv * gv)
    m_hat = mn / (1.0 - ADAM_B1 ** ADAM_STEP)
    v_hat = vn / (1.0 - ADAM_B2 ** ADAM_STEP)
    return -ADAM_LR * (m_hat / (jnp.sqrt(v_hat) + ADAM_EPS) + ADAM_WD * w), mn, vn


def _adamw(w, g, m, v, name):
    r, c = w.shape

    def body(w_ref, g_ref, m_ref, v_ref, d_ref, nm_ref, nv_ref):
        d_ref[...], nm_ref[...], nv_ref[...] = _adamw_update(w_ref[...], g_ref[...], m_ref[...], v_ref[...])

    blk = pl.BlockSpec((r, c), lambda i: (0, 0))
    return pl.pallas_call(body, grid=(1,), in_specs=[blk] * 4, out_specs=(blk,) * 3,
                          out_shape=tuple(_sds((r, c), F32) for _ in range(3)), name=name, compiler_params=_cp())(w, g, m, v)


def _adamw_layer(layer, w, g, m, v, outs, name):
    _, r, c = w.shape
    tr = max(d for d in range(8, r + 1, 8) if r % d == 0 and d * c * 4 <= 2 ** 21)

    def body(w_ref, m_ref, v_ref, g_ref, *refs):
        d_ref, nm_ref, nv_ref, go_ref = refs[4:]
        gv = g_ref[...]
        d_ref[...], nm_ref[...], nv_ref[...] = _adamw_update(w_ref[...], gv, m_ref[...], v_ref[...])
        go_ref[...] = gv

    blk = pl.BlockSpec((None, tr, c), lambda i: (layer, i, 0))
    return pl.pallas_call(
        body, grid=(r // tr,), in_specs=[blk] * 3 + [pl.BlockSpec((tr, c), lambda i: (i, 0))] + [ANY] * 4, out_specs=(blk,) * 4,
        out_shape=tuple(_sds(w.shape, F32) for _ in range(4)), input_output_aliases={4 + j: j for j in range(4)}, name=name,
        compiler_params=_cp())(w, m, v, g, *outs)


def _all_gather(shards, name):
    n = len(shards)

    def body(*refs):
        x_refs, out_refs = refs[:n], refs[n:2 * n]
        send_sems, recv_sems, local_sems = refs[2 * n:]
        x, y, cc = lax.axis_index("x"), lax.axis_index("y"), lax.axis_index("c")
        me, sibling = (x, y, cc), (x, y, 1 - cc)
        chips = [(1 - x, y), (x, 1 - y), (1 - x, 1 - y)]

        def copy(i, k, block, to, own=False):
            px, py, pc = block
            slot = out_refs[i].at[4 * px + 2 * py + pc]
            return pltpu.make_async_remote_copy(
                src_ref=x_refs[i] if own else slot, dst_ref=slot, send_sem=send_sems.at[7 * i + k],
                recv_sem=recv_sems.at[7 * i + k], device_id=to, device_id_type=MESH)

        mine = [pltpu.make_async_copy(x_refs[i], out_refs[i].at[4 * x + 2 * y + cc], local_sems.at[i]) for i in range(n)]
        for cp in mine:
            cp.start()
        first = []
        for j, chip in enumerate(chips):
            first += [copy(i, 1 + j, me, (*chip, cc), own=True) for i in range(n)]
        first += [copy(i, 0, me, sibling, own=True) for i in range(n)]
        for cp in first:
            cp.start()
        passed = []
        for j, chip in enumerate(chips):
            for i in range(n):
                copy(i, 1 + j, (*chip, cc), me).wait_recv()
                cp = copy(i, 4 + j, (*chip, cc), sibling)
                cp.start()
                passed.append(cp)
        for i in range(n):
            copy(i, 0, sibling, me).wait_recv()
        for j, chip in enumerate(chips):
            for i in range(n):
                copy(i, 4 + j, (*chip, 1 - cc), me).wait_recv()
        for cp in first + passed:
            cp.wait_send()
        for cp in mine:
            cp.wait()

    return pl.pallas_call(
        body, out_shape=tuple(_sds((N_DEV,) + s.shape, s.dtype) for s in shards), in_specs=[ANY] * n, out_specs=(ANY,) * n,
        scratch_shapes=[pltpu.SemaphoreType.DMA((7 * n,)), pltpu.SemaphoreType.DMA((7 * n,)), pltpu.SemaphoreType.DMA((n,))],
        name=name)(*shards)


def _rs_core_swap(bufs, name):
    n = len(bufs)

    def body(*refs):
        b_refs, recv_refs = refs[:n], refs[n:2 * n]
        send_sems, recv_sems = refs[2 * n:]
        x, y, cc = lax.axis_index("x"), lax.axis_index("y"), lax.axis_index("c")
        copies = [pltpu.make_async_remote_copy(
            src_ref=b_refs[i].at[s, 1 - cc], dst_ref=recv_refs[i].at[s], send_sem=send_sems.at[4 * i + s],
            recv_sem=recv_sems.at[4 * i + s], device_id=(x, y, 1 - cc), device_id_type=MESH) for i in range(n) for s in range(4)]
        for cp in copies:
            cp.start()
        for cp in copies:
            cp.wait()

    return pl.pallas_call(
        body, out_shape=tuple(_sds((4,) + b.shape[2:], b.dtype) for b in bufs), in_specs=[ANY] * n, out_specs=(ANY,) * n,
        scratch_shapes=[pltpu.SemaphoreType.DMA((4 * n,)), pltpu.SemaphoreType.DMA((4 * n,))], name=name)(*bufs)


HBM = pl.BlockSpec(memory_space=pltpu.HBM)
SEMS = pl.BlockSpec(memory_space=pltpu.SEMAPHORE)
EFFECT = pltpu.SideEffectType.DATAFLOW_SIDE_EFFECTING


def _hbm(a):
    return pltpu.HBM(a.shape, a.dtype)


def _other_chips(x, y):
    return [(1 - x, y), (x, 1 - y), (1 - x, 1 - y)]


def _ici_start(srcs, lands, mode, name, group=None):
    n = len(srcs)

    def body(*refs):
        s_refs, land_refs = refs[:n], refs[n:2 * n]
        send_sems, recv_sems = refs[2 * n], refs[2 * n + 1]
        token = refs[-1]
        x, y, cc = lax.axis_index("x"), lax.axis_index("y"), lax.axis_index("c")
        mine = 2 * x + y if mode == "by_chip" else 4 * x + 2 * y + cc
        peers = [(px, py, cc) for px, py in _other_chips(x, y)]
        if mode == "by_device":
            peers = [(x, y, 1 - cc)] + peers + [(px, py, 1 - cc) for px, py in _other_chips(x, y)]
        first = 0
        for size in ([n] if group is None else group):
            first += size
            for px, py, pc in peers:
                for i in range(first - size, first):
                    src = s_refs[i]
                    if mode == "by_chip":
                        src = src.at[2 * px + py]
                    elif mode == "by_device":
                        src = src.at[4 * px + 2 * py + pc]
                    pltpu.make_async_remote_copy(
                        src_ref=src, dst_ref=land_refs[i].at[mine], send_sem=send_sems.at[i], recv_sem=recv_sems.at[i],
                        device_id=(px, py, pc), device_id_type=MESH).start()
        token[...] = jnp.zeros_like(token)

    out = pl.pallas_call(
        body, name=name,
        out_shape=(pltpu.SemaphoreType.DMA((n,)), pltpu.SemaphoreType.DMA((n,)), *[_hbm(s) for s in srcs], *[_hbm(l) for l in lands],
                   _sds((8, 128), F32)),
        in_specs=[HBM] * (2 * n), out_specs=(SEMS, SEMS, *[HBM] * (2 * n), pl.BlockSpec(memory_space=pltpu.VMEM)),
        input_output_aliases={i: 2 + i for i in range(2 * n)}, compiler_params=pltpu.CompilerParams(has_side_effects=EFFECT),
    )(*[pltpu.with_memory_space_constraint(s, pltpu.HBM) for s in srcs],
      *[pltpu.with_memory_space_constraint(l, pltpu.HBM) for l in lands])
    return out[0], out[1], out[2:2 + n], out[2 + n:2 + 2 * n], out[-1], 7 if mode == "by_device" else 3


def _ici_wait(started, after, name, only=None):
    send_sems, recv_sems, srcs, lands, _, copies = started
    only = list(range(len(srcs))) if only is None else only
    srcs, lands = [srcs[i] for i in only], [lands[i] for i in only]
    n = len(srcs)

    def body(*refs):
        land_refs = refs[n:2 * n]
        send_sems, recv_sems = refs[2 * n], refs[2 * n + 1]
        x, y, cc = lax.axis_index("x"), lax.axis_index("y"), lax.axis_index("c")
        for i in range(n):
            three = land_refs[i].at[pl.ds(0, copies)]
            cp = pltpu.make_async_remote_copy(src_ref=three, dst_ref=three, send_sem=send_sems.at[only[i]],
                                              recv_sem=recv_sems.at[only[i]],
                                              device_id=(x, y, cc), device_id_type=MESH)
            cp.wait_send()
            cp.wait_recv()

    return pl.pallas_call(
        body, name=name, out_shape=tuple(_hbm(l) for l in lands), in_specs=[HBM] * (2 * n) + [SEMS, SEMS, ANY],
        out_specs=tuple([HBM] * n), input_output_aliases={n + i: i for i in range(n)},
        compiler_params=pltpu.CompilerParams(has_side_effects=EFFECT))(*srcs, *lands, send_sems, recv_sems, after)


def _gather_d2d(blocks, lands, name):
    n = len(blocks)

    def body(*refs):
        x_refs, land_refs = refs[:n], refs[2 * n:3 * n]
        send_sems, recv_sems, in_sems, out_sems = refs[3 * n:3 * n + 4]
        stage = refs[3 * n + 4:]
        x, y, cc = lax.axis_index("x"), lax.axis_index("y"), lax.axis_index("c")
        sibling = (x, y, 1 - cc)
        staged = [pltpu.make_async_copy(x_refs[i], stage[i], in_sems.at[i]) for i in range(n)]
        for cp in staged:
            cp.start()
        copies = []
        for i in range(n):
            slot = land_refs[i].at[4 * x + 2 * y + cc]
            copies.append(pltpu.make_async_remote_copy(src_ref=x_refs[i], dst_ref=slot, send_sem=send_sems.at[4 * i],
                                                       recv_sem=recv_sems.at[4 * i], device_id=sibling, device_id_type=MESH))
            for j, (px, py) in enumerate(_other_chips(x, y)):
                slot = land_refs[i].at[4 * px + 2 * py + cc]
                copies.append(pltpu.make_async_remote_copy(src_ref=slot, dst_ref=slot, send_sem=send_sems.at[4 * i + 1 + j],
                                                           recv_sem=recv_sems.at[4 * i + 1 + j], device_id=sibling, device_id_type=MESH))
        for cp in copies:
            cp.start()
        mine = []
        for i in range(n):
            staged[i].wait()
            mine.append(pltpu.make_async_copy(stage[i], land_refs[i].at[4 * x + 2 * y + cc], out_sems.at[i]))
            mine[i].start()
        for i in range(n):
            slot = land_refs[i].at[4 * x + 2 * y + (1 - cc)]
            pltpu.make_async_remote_copy(src_ref=slot, dst_ref=slot, send_sem=send_sems.at[4 * i], recv_sem=recv_sems.at[4 * i],
                                         device_id=sibling, device_id_type=MESH).wait_recv()
            for j, (px, py) in enumerate(_other_chips(x, y)):
                slot = land_refs[i].at[4 * px + 2 * py + (1 - cc)]
                pltpu.make_async_remote_copy(src_ref=slot, dst_ref=slot, send_sem=send_sems.at[4 * i + 1 + j],
                                             recv_sem=recv_sems.at[4 * i + 1 + j], device_id=sibling, device_id_type=MESH).wait_recv()
        for cp in copies:
            cp.wait_send()
        for cp in mine:
            cp.wait()

    return pl.pallas_call(
        body, out_shape=tuple(_sds(l.shape, l.dtype) for l in lands), in_specs=[ANY] * (2 * n), out_specs=(ANY,) * n,
        input_output_aliases={n + i: i for i in range(n)},
        scratch_shapes=[pltpu.SemaphoreType.DMA((4 * n,)), pltpu.SemaphoreType.DMA((4 * n,)), pltpu.SemaphoreType.DMA((n,)),
                        pltpu.SemaphoreType.DMA((n,))] + [pltpu.VMEM(b.shape, b.dtype) for b in blocks],
        name=name, compiler_params=_cp())(*blocks, *lands)


def _sum_own(parts, recvs, mine, name):
    n = len(parts)

    def body(c_ref, *refs):
        s = pl.program_id(0)
        for i in range(n):
            val = jnp.where(c_ref[0] == s, refs[i][...], refs[n + i][...]).astype(F32)
            _acc(refs[2 * n + i], val, s == 0)

    kept = [pl.BlockSpec((None,) + p.shape[1:], lambda s, cref: (cref[0], 0, 0)) for p in parts]
    ins = [pl.BlockSpec((None,) + p.shape[1:], lambda s, cref: (s, 0, 0)) for p in parts]
    return pl.pallas_call(
        body, grid_spec=pltpu.PrefetchScalarGridSpec(
            num_scalar_prefetch=1, grid=(parts[0].shape[0],), in_specs=kept + ins,
            out_specs=tuple(pl.BlockSpec(p.shape[1:], lambda s, cref: (0, 0)) for p in parts)),
        out_shape=tuple(_sds(p.shape[1:], F32) for p in parts), name=name, compiler_params=_cp())(mine, *parts, *recvs)


BIG = (("w_in", True), ("w_gate", True), ("w_mem_kv", False), ("w_branch", True), ("w_out", False), ("w_ffn_in", True),
       ("w_ffn_out", False))

SMALL = ("norm_mix_g", "norm_mem_g", "ret_decay_fwd", "ret_decay_bwd", "ret_norm_g", "pool_w", "pool_scale", "na_q_norm_g",
         "na_k_norm_g", "na_rpb", "mem_q_norm_g", "mem_k_norm_g", "norm_ffn_g")
WEIGHTS = ("norm_mix_g", "norm_mem_g", "w_in", "w_gate", "ret_decay_fwd", "ret_decay_bwd", "ret_norm_g", "pool_w", "pool_scale",
           "na_q_norm_g", "na_k_norm_g", "na_rpb", "mem_q_norm_g", "mem_k_norm_g", "w_mem_kv", "w_branch", "w_out", "norm_ffn_g",
           "w_ffn_in", "w_ffn_out")


def _to_exchange(name, transposed, shard):
    if name == "w_branch":
        return jnp.swapaxes(shard, 1, 2).reshape(NH * (D // N_DEV), BW)
    return shard.T if transposed else shard


def _from_exchange(name, transposed, block):
    if name == "w_branch":
        return jnp.swapaxes(block.reshape(NH, D // N_DEV, BW), 1, 2)
    return block.T if transposed else block


def _whole_from_gathered(name, g):
    if name == "w_branch":
        return jnp.swapaxes(g.reshape(N_DEV, NH, D // N_DEV, BW), 0, 1).reshape(NH, D, BW)
    return g.reshape(N_DEV * g.shape[1], g.shape[2])


def _by_destination(name, g):
    if name == "w_branch":
        g = jnp.swapaxes(g.reshape(NH, N_DEV, D // N_DEV, BW), 0, 1).reshape(N_DEV * NH * (D // N_DEV), BW)
    return g.reshape(4, 2, g.shape[0] // N_DEV, g.shape[1])


SMALL_PAD = 1024


def _pack_small(vals, loss=None):
    parts = [vals[n] for n in SMALL] + [jnp.zeros((1,), F32) if loss is None else loss.reshape(1)]
    rows = []
    for p in parts:
        flat = p.reshape(-1)
        rows.append(jnp.pad(flat, (0, -flat.shape[0] % SMALL_PAD)).reshape(-1, 128))
    return jnp.concatenate(rows, axis=0)


def _unpack_small(packed, like):
    out, off = {}, 0
    for n in SMALL:
        sz = int(np.prod(like[n].shape))
        nrow = -(-sz // SMALL_PAD) * (SMALL_PAD // 128)
        out[n] = packed[off:off + nrow].reshape(-1)[:sz].reshape(like[n].shape)
        off += nrow
    return out, packed[off, 0]


def _na_constants():
    c = np.arange(GRID_W)
    win = np.clip(c - NA_COLS_WIN // 2, 0, GRID_W - NA_COLS_WIN)
    kc = np.arange(GRID_W)
    inside = (kc[None, :] >= win[:, None]) & (kc[None, :] < win[:, None] + NA_COLS_WIN)
    off = kc[None, :] - c[:, None] + NA_COLS_WIN - 1
    onehot = np.zeros((128, GRID_W, GRID_W), np.float32)
    for b in range(2 * NA_COLS_WIN - 1):
        onehot[b] = (off == b) & inside
    maskadd = np.where(inside, 0.0, NEG).astype(np.float32)
    return onehot.reshape(128, GRID_W * GRID_W), maskadd


def _na_bias_table(tab, maskadd):
    n_off = 2 * NA_ROWS_WIN - 1
    t4 = tab[:NH * n_off].reshape(NH, n_off, GRID_W, GRID_W) + maskadd[None, None]
    by_off = t4.transpose(1, 0, 2, 3).reshape(n_off, NH * GRID_W, GRID_W)
    return jnp.concatenate([by_off[:-1], by_off[1:]], axis=-1)


def _rotary_tables(t):
    half = HD // 2
    inv = ROPE_THETA ** (-jnp.arange(half, dtype=F32) / half)
    ang = jnp.arange(t, dtype=F32)[:, None] * inv[None, :]
    cos, sin = jnp.cos(ang), jnp.sin(ang)
    return jnp.tile(jnp.concatenate([cos, cos], axis=-1), (1, NH)), jnp.tile(jnp.concatenate([-sin, sin], axis=-1), (1, NH))


def _block_diag(pw):
    out = jnp.zeros((BW, BW), pw.dtype)
    for g in range(NH):
        out = lax.dynamic_update_slice(out, pw[g], (g * HD, g * HD))
    return out


def _tile4(g):
    return jnp.tile(g.reshape(1, HD), (1, NH))


def _layer_fwd(x, mem, sw, lw, consts, fetch, h=None, next_norm_g=None):
    cos2, sin2, onehot, maskadd = consts
    if h is None:
        h = _rmsnorm_fwd(x, sw["norm_mix_g"].reshape(1, D), "norm_mix_fwd")
    proj = _mm(h, lw["w_in"], tb=True, name="mm_in")
    gp = _mm(h, lw["w_gate"], tb=True, out_dtype=BF16, name="mm_gate")
    g_naq, g_nak, g_mq = _tile4(sw["na_q_norm_g"]), _tile4(sw["na_k_norm_g"]), _tile4(sw["mem_q_norm_g"])
    rq, rk, rv, nq, nk, nv, mq = _prep_fwd(proj, cos2, sin2, g_naq, g_nak, g_mq)

    lgf, lgb = jax.nn.log_sigmoid(sw["ret_decay_fwd"]), jax.nn.log_sigmoid(sw["ret_decay_bwd"])
    g_ret = sw["ret_norm_g"].reshape(1, BW)
    o_ret, ret = _ret_fwd(rq, rk, rv, proj, lgf, lgb, g_ret)

    wbd = _block_diag(sw["pool_w"]).astype(BF16)
    p_scale = sw["pool_scale"].reshape(1, BW)
    pool = _pool_fwd(proj, wbd, p_scale)

    rpb_pad = jnp.pad(sw["na_rpb"].reshape(NH * 15, 31), ((0, 4), (0, 97)))
    ball = _na_bias_table(_rpb_expand(rpb_pad, onehot), maskadd)
    na = _na_fwd(nq, nk, nv, ball)

    lw.update(fetch(1, na))
    memn = _rmsnorm_fwd(mem, sw["norm_mem_g"].reshape(1, D), "norm_mem_fwd")
    kv = _mm(memn, lw["w_mem_kv"], name="mm_memkv")
    g_mk = _tile4(sw["mem_k_norm_g"])
    mk, mv = _memkv_prep(kv, g_mk)
    mo = _mem_fwd(mq, mk, mv)

    br = (ret, pool, na, mo)
    merged = _merge_fwd(br, lw["w_branch"], gp)
    x1, h2 = _mm(merged, lw["w_out"], add=x, norm_g=sw["norm_ffn_g"].reshape(1, D), name="mm_out")
    lw.update(fetch(2, x1))
    ffa, ffg, yff = _ffn_in_fwd(h2, lw["w_ffn_in"])
    if next_norm_g is None:
        x2, h_next = _mm(yff, lw["w_ffn_out"], add=x1, name="mm_ffn_out"), None
    else:
        x2, h_next = _mm(yff, lw["w_ffn_out"], add=x1, norm_g=next_norm_g.reshape(1, D), name="mm_ffn_out")
    saved = dict(x=x, h=h, proj=proj, gp=gp, rq=rq, rk=rk, rv=rv, nq=nq, nk=nk, nv=nv, mq=mq, o_ret=o_ret, ball=ball, memn=memn,
                 kv=kv, mk=mk, mv=mv, br=br, merged=merged, x1=x1, h2=h2, ffa=ffa, ffg=ffg, yff=yff, lgf=lgf, lgb=lgb, wbd=wbd)
    return x2, h_next, saved


def _layer_bwd(dx2, dx2b, mem, sw, lw, sv, consts, dep=None):
    cos2, sin2, onehot, maskadd = consts
    gb, gs = {}, {}
    d_a, d_g = _ffn_out_bwd(dx2b, lw["w_ffn_out"], sv["ffa"], sv["ffg"], dep)
    gb["w_ffn_out"] = _mm(sv["yff"], dx2b, ta=True, out_dtype=BF16, name="mm_ffn_out_dw")
    dh2 = _mm(d_a, lw["w_ffn_in"], b_half=0, name="mm_ffn_in_dx_a")
    dx1, dx1b, dg = _mm_norm_bwd(d_g, lw["w_ffn_in"], dh2, sv["x1"], sw["norm_ffn_g"].reshape(1, D), dx2, b_half=1,
                                 name="mm_ffn_in_dx_g")
    gs["norm_ffn_g"] = dg.reshape(D)
    dw_a = _mm(d_a, sv["h2"], ta=True, out_dtype=BF16, out_half=(0, None), name="mm_ffn_in_dw_a")
    gb["w_ffn_in"] = _mm(d_g, sv["h2"], ta=True, out_dtype=BF16, out_half=(1, dw_a), name="mm_ffn_in_dw_g")

    dmerged = _mm(dx1b, lw["w_out"], tb=True, name="mm_out_dx")
    gb["w_out"] = _mm(sv["merged"], dx1b, ta=True, out_dtype=BF16, name="mm_out_dw")
    dgp, dbr, gb["w_branch"] = _merge_bwd(dmerged, sv["br"], lw["w_branch"], sv["gp"])

    g_ret = sw["ret_norm_g"].reshape(1, BW)
    do_ret, d_rg, dg_ret = _ret_post_bwd(dbr, sv["o_ret"], sv["proj"], g_ret)
    d_rq, d_rk, d_rv, dlg = _ret_bwd(do_ret, sv["rq"], sv["rk"], sv["rv"], sv["lgf"], sv["lgb"])
    gs["ret_norm_g"] = dg_ret.reshape(BW)
    _, vjp_f = jax.vjp(jax.nn.log_sigmoid, sw["ret_decay_fwd"])
    _, vjp_b = jax.vjp(jax.nn.log_sigmoid, sw["ret_decay_bwd"])
    gs["ret_decay_fwd"] = vjp_f(dlg[0:NH, 0])[0]
    gs["ret_decay_bwd"] = vjp_b(dlg[NH:2 * NH, 0])[0]

    p_scale = sw["pool_scale"].reshape(1, BW)
    d_pv, dwbd, dscale = _pool_bwd(dbr, sv["proj"], sv["wbd"], p_scale)
    gs["pool_w"] = jnp.stack([dwbd[g * HD:(g + 1) * HD, g * HD:(g + 1) * HD] for g in range(NH)])
    gs["pool_scale"] = dscale.reshape(BW)

    d_nq, d_nk, d_nv, dball = _na_bwd(dbr, sv["nq"], sv["nk"], sv["nv"], sv["ball"])
    _, vjp_tab = jax.vjp(lambda tab: _na_bias_table(tab, maskadd), jnp.zeros((64, GRID_W * GRID_W), F32))
    drpb = _rpb_reduce(vjp_tab(dball)[0], onehot)
    gs["na_rpb"] = drpb[:NH * 15, :31].reshape(NH, 15, 31)

    d_mq, d_mk, d_mv = _mem_bwd(dbr, sv["mq"], sv["mk"], sv["mv"])
    g_mk = _tile4(sw["mem_k_norm_g"])
    dkv, dg_mk = _memkv_bwd(sv["kv"], d_mk, d_mv, g_mk)
    gs["mem_k_norm_g"] = dg_mk.reshape(NH, HD).sum(0)
    gb["w_mem_kv"] = _mm(sv["memn"], dkv, ta=True, out_dtype=BF16, name="mm_memkv_dw")
    dmemn = _mm(dkv, lw["w_mem_kv"], tb=True, name="mm_memkv_dx")
    _, _, dg_mem = _rmsnorm_bwd(dmemn, mem, sw["norm_mem_g"].reshape(1, D), jnp.zeros_like(mem), "norm_mem_bwd")
    gs["norm_mem_g"] = dg_mem.reshape(D)

    g_naq, g_nak, g_mq = _tile4(sw["na_q_norm_g"]), _tile4(sw["na_k_norm_g"]), _tile4(sw["mem_q_norm_g"])
    dproj, dg_naq, dg_nak, dg_mq = _prep_bwd(sv["proj"], cos2, sin2, g_naq, g_nak, g_mq, d_rq, d_rk, d_rv, d_rg, d_pv, d_nq, d_nk,
                                             d_nv, d_mq)
    gs["na_q_norm_g"] = dg_naq.reshape(NH, HD).sum(0)
    gs["na_k_norm_g"] = dg_nak.reshape(NH, HD).sum(0)
    gs["mem_q_norm_g"] = dg_mq.reshape(NH, HD).sum(0)

    gb["w_in"] = _mm(dproj, sv["h"], ta=True, out_dtype=BF16, name="mm_in_dw")
    gb["w_gate"] = _mm(dgp, sv["h"], ta=True, out_dtype=BF16, name="mm_gate_dw")
    dh = _mm(dproj, lw["w_in"], name="mm_in_dx")
    dx, dxb, dg = _mm_norm_bwd(dgp, lw["w_gate"], dh, sv["x"], sw["norm_mix_g"].reshape(1, D), dx1, name="mm_gate_dx")
    gs["norm_mix_g"] = dg.reshape(D)
    return dx, dxb, gb, gs


def _local_step(x, mem, target, small, get_layer, on_grads):
    t = x.shape[0]
    cos2, sin2 = _rotary_tables(t)
    onehot, maskadd = _na_constants()
    consts = (cos2, sin2, jnp.asarray(onehot), jnp.asarray(maskadd))
    saved, weights, cur, h = [], [], x, None
    for l in range(DEPTH):
        sw = {n: small[n][l] for n in SMALL}
        lw, fetch = get_layer(l, cur)
        weights.append(lw)
        cur, h, sv = _layer_fwd(cur, mem, sw, lw, consts, fetch, h, small["norm_mix_g"][l + 1] if l + 1 < DEPTH else None)
        saved.append(sv)
    dy, dyb, loss_tile = _loss_head(cur, target)
    small_g = {n: [None] * DEPTH for n in SMALL}
    dep = None
    for l in reversed(range(DEPTH)):
        sw = {n: small[n][l] for n in SMALL}
        dy, dyb, gb, gs = _layer_bwd(dy, dyb, mem, sw, weights[l], saved[l], consts, dep)
        dep = on_grads(l, gb, dy)
        for n in SMALL:
            small_g[n][l] = gs[n]
    return loss_tile[0, 0], dy, {n: jnp.stack(v) for n, v in small_g.items()}


def _flat2d(a):
    return a.reshape(-1, a.shape[-1])


def kernel(x, mem, norm_mix_g, norm_mem_g, w_in, w_gate, ret_decay_fwd, ret_decay_bwd, ret_norm_g, pool_w, pool_scale, na_q_norm_g, na_k_norm_g, na_rpb, mem_q_norm_g, mem_k_norm_g, w_mem_kv, w_branch, w_out, norm_ffn_g, w_ffn_in, w_ffn_out, loss_target, m_norm_mix_g, m_norm_mem_g, m_w_in, m_w_gate, m_ret_decay_fwd, m_ret_decay_bwd, m_ret_norm_g, m_pool_w, m_pool_scale, m_na_q_norm_g, m_na_k_norm_g, m_na_rpb, m_mem_q_norm_g, m_mem_k_norm_g, m_w_mem_kv, m_w_branch, m_w_out, m_norm_ffn_g, m_w_ffn_in, m_w_ffn_out, v_norm_mix_g, v_norm_mem_g, v_w_in, v_w_gate, v_ret_decay_fwd, v_ret_decay_bwd, v_ret_norm_g, v_pool_w, v_pool_scale, v_na_q_norm_g, v_na_k_norm_g, v_na_rpb, v_mem_q_norm_g, v_mem_k_norm_g, v_w_mem_kv, v_w_branch, v_w_out, v_norm_ffn_g, v_w_ffn_in, v_w_ffn_out):
    w = dict(norm_mix_g=norm_mix_g, norm_mem_g=norm_mem_g, w_in=w_in, w_gate=w_gate, ret_decay_fwd=ret_decay_fwd,
             ret_decay_bwd=ret_decay_bwd, ret_norm_g=ret_norm_g, pool_w=pool_w, pool_scale=pool_scale, na_q_norm_g=na_q_norm_g,
             na_k_norm_g=na_k_norm_g, na_rpb=na_rpb, mem_q_norm_g=mem_q_norm_g, mem_k_norm_g=mem_k_norm_g, w_mem_kv=w_mem_kv,
             w_branch=w_branch, w_out=w_out, norm_ffn_g=norm_ffn_g, w_ffn_in=w_ffn_in, w_ffn_out=w_ffn_out)
    m = dict(norm_mix_g=m_norm_mix_g, norm_mem_g=m_norm_mem_g, w_in=m_w_in, w_gate=m_w_gate, ret_decay_fwd=m_ret_decay_fwd,
             ret_decay_bwd=m_ret_decay_bwd, ret_norm_g=m_ret_norm_g, pool_w=m_pool_w, pool_scale=m_pool_scale, na_q_norm_g=m_na_q_norm_g,
             na_k_norm_g=m_na_k_norm_g, na_rpb=m_na_rpb, mem_q_norm_g=m_mem_q_norm_g, mem_k_norm_g=m_mem_k_norm_g, w_mem_kv=m_w_mem_kv,
             w_branch=m_w_branch, w_out=m_w_out, norm_ffn_g=m_norm_ffn_g, w_ffn_in=m_w_ffn_in, w_ffn_out=m_w_ffn_out)
    v = dict(norm_mix_g=v_norm_mix_g, norm_mem_g=v_norm_mem_g, w_in=v_w_in, w_gate=v_w_gate, ret_decay_fwd=v_ret_decay_fwd,
             ret_decay_bwd=v_ret_decay_bwd, ret_norm_g=v_ret_norm_g, pool_w=v_pool_w, pool_scale=v_pool_scale, na_q_norm_g=v_na_q_norm_g,
             na_k_norm_g=v_na_k_norm_g, na_rpb=v_na_rpb, mem_q_norm_g=v_mem_q_norm_g, mem_k_norm_g=v_mem_k_norm_g, w_mem_kv=v_w_mem_kv,
             w_branch=v_w_branch, w_out=v_w_out, norm_ffn_g=v_norm_ffn_g, w_ffn_in=v_w_ffn_in, w_ffn_out=v_w_ffn_out)
    assert x.shape == (1, 2048, D) and mem.shape == (1, N_MEM, D) and w_in.shape == (DEPTH, D, 9 * BW // N_DEV)

    first_groups = [[0, 1], [2, 3, 4], [5, 6]]
    blocks = [_to_exchange(name, tr, w[name][l]).astype(BF16) for l in range(DEPTH) for name, tr in BIG]
    started = _ici_start(blocks, [lax.empty((N_DEV,) + b.shape, BF16) for b in blocks], "gather", "gather_ici_start",
                         [len(g) for g in first_groups] + [len(BIG)] * (DEPTH - 1))

    def get_group(l, only, after, tag):
        at = [l * len(BIG) + i for i in only]
        lands = _ici_wait(started, after, "gather_ici_wait_%d%s" % (l, tag), at)
        whole = _gather_d2d([started[2][i] for i in at], lands, "gather_d2d")
        return {BIG[i][0]: _whole_from_gathered(BIG[i][0], g) for i, g in zip(only, whole)}

    def get_layer(l, after):
        if l > 0:
            return get_group(l, list(range(len(BIG))), after, ""), lambda stage, after2: {}
        return (get_group(l, first_groups[0], started[4], "a"),
                lambda stage, after2: get_group(l, first_groups[stage], after2, "abc"[stage]))

    cidx = lax.axis_index("c").astype(jnp.int32).reshape(1)
    chip = (2 * lax.axis_index("x") + lax.axis_index("y")).astype(jnp.int32).reshape(1)
    in_flight = []

    def flip_of(name, tr):
        return (lambda a: jnp.swapaxes(a, 1, 2)) if name in ("w_in", "w_ffn_in") else (lambda a: a)

    def rows3(a):
        return a.reshape(DEPTH, -1, a.shape[-1])

    opt_in = {name: tuple(rows3(flip_of(name, tr)(t[name])) for t in (w, m, v)) for name, tr in BIG}
    opt_out = {name: tuple(lax.empty(opt_in[name][0].shape, F32) for _ in range(4)) for name, _ in BIG}

    device = (2 * chip + cidx).astype(jnp.int32)

    def finish(l, st, after):
        recv = _ici_wait(st, after, "rs_ici_wait_%d" % l)
        sums = _sum_own(st[2], recv, chip if st[5] == 3 else device, "rs_sum")
        for (name, tr), s in zip(BIG, sums):
            g = s if name in ("w_in", "w_ffn_in") else _from_exchange(name, tr, s)
            wx, mx, vx = opt_in[name]
            opt_out[name] = _adamw_layer(l, wx, g.reshape(-1, g.shape[-1]), mx, vx, opt_out[name], "adamw_" + name)

    def on_grads(l, gb, after):
        send = [_by_destination(name, gb[name]) for name, _ in BIG]
        if l > 0:
            send = [s.reshape((N_DEV,) + s.shape[2:]) for s in send]
            st = _ici_start(send, [lax.empty(s.shape, BF16) for s in send], "by_device", "rs_ici_start_%d" % l)
        else:
            from_core = _rs_core_swap(send, "rs_core_swap")
            chip_part = _pair_sum(send, from_core, cidx)
            st = _ici_start(chip_part, [lax.empty(p.shape, BF16) for p in chip_part], "by_chip", "rs_ici_start_%d" % l)
        in_flight.append((l, st))
        return st[4]

    loss_local, dx, small_g = _local_step(x[0], mem[0], loss_target[0], {n: w[n] for n in SMALL}, get_layer, on_grads)

    last_started = in_flight[-1][1][4]
    for l, st in in_flight[:-1]:
        finish(l, st, last_started)

    small_all, = _all_gather([_pack_small(small_g, loss_local) + last_started[0:1]], "gather_small")
    packed_g = _sum_slots(small_all, "small_sum")
    small_sum, loss = _unpack_small(packed_g, {n: w[n] for n in SMALL})
    d_, m_, v_ = _adamw(_pack_small({n: w[n] for n in SMALL}), packed_g, _pack_small({n: m[n] for n in SMALL}),
                        _pack_small({n: v[n] for n in SMALL}), "adamw_small")
    updated = d_[0:8]
    for name, _ in BIG:
        updated = updated + opt_out[name][0][1, 0:8, 0:128]
    finish(*in_flight[-1], updated)

    grads, delta, new_m, new_v = {}, {}, {}, {}
    for name, tr in BIG:
        shape = flip_of(name, tr)(w[name]).shape
        delta[name], new_m[name], new_v[name], grads[name] = (flip_of(name, tr)(a.reshape(shape)) for a in opt_out[name])
    like = {n: w[n] for n in SMALL}
    ds, _ = _unpack_small(d_, like)
    ms, _ = _unpack_small(m_, like)
    vs, _ = _unpack_small(v_, like)
    for n in SMALL:
        grads[n], delta[n], new_m[n], new_v[n] = small_sum[n], ds[n], ms[n], vs[n]

    return (loss, dx[None], *[grads[n] for n in WEIGHTS], *[delta[n] for n in WEIGHTS], *[new_m[n] for n in WEIGHTS],
            *[new_v[n] for n in WEIGHTS])
```
